```python
import math
import jax, jax.numpy as jnp
from jax import lax
import numpy as np

D_MODEL = 1024
BATCH = 8
SEQ = 4096
DEPTH = 2

CHUNK = 64
PLE_DIM = 256
D_POOL = D_MODEL // 2
D_CONV = D_MODEL - D_POOL
POOL_WINDOWS = (2, 4, 8, 16)
POOL_GROUP = D_POOL // len(POOL_WINDOWS)
CONV_KERNEL = 31
HEAD_DIM = 64
N_HEADS = D_MODEL // HEAD_DIM
LEFT_CHUNKS = 8
BAND = (LEFT_CHUNKS + 1) * CHUNK
MAX_REL_DIST = 256
D_FF = ((8 * D_MODEL // 3) + 127) // 128 * 128
FFN_CONV_KERNEL = 3
N_EVEN = (DEPTH + 1) // 2
N_ODD = DEPTH // 2
DEEPNORM_ALPHA = (2 * DEPTH) ** 0.25
DEEPNORM_BETA = (8 * DEPTH) ** -0.25
LN_EPS = 1e-5
NEG_INF = -1e30

kernel_name = "hybrid_pool_conv_chunkattn_encoder"


def layer_norm(x, g, b):
    x32 = x.astype(jnp.float32)
    mu = jnp.mean(x32, axis=-1, keepdims=True)
    var = jnp.mean(jnp.square(x32 - mu), axis=-1, keepdims=True)
    y = (x32 - mu) * lax.rsqrt(var + LN_EPS)
    return (y * g.astype(jnp.float32) + b.astype(jnp.float32)).astype(x.dtype)


def causal_dwconv(x, w, b):
    k = w.shape[0]
    c = x.shape[-1]
    y = lax.conv_general_dilated(
        x, w[:, None, :].astype(x.dtype), window_strides=(1,), padding=[(k - 1, 0)],
        dimension_numbers=('NWC', 'WIO', 'NWC'), feature_group_count=c)
    return y + b


def pool_conv_mixer(x, w_in, pool_w, pool_scale, dw_w, dw_b, cn_g, cn_b, w_out):
    bsz, s, _ = x.shape
    u = x @ w_in
    a = u[..., :D_POOL]
    b_val = u[..., D_POOL:D_POOL + D_CONV]
    b_gate = u[..., D_POOL + D_CONV:]

    a32 = a.astype(jnp.float32)
    cs = jnp.pad(jnp.cumsum(a32, axis=1), ((0, 0), (1, 0), (0, 0)))
    pos = jnp.arange(1, s + 1, dtype=jnp.float32)
    groups = []
    for g, w in enumerate(POOL_WINDOWS):
        sl = slice(g * POOL_GROUP, (g + 1) * POOL_GROUP)
        csg = cs[..., sl]
        lower = jnp.pad(csg[:, :s + 1 - w], ((0, 0), (w - 1, 0), (0, 0)))
        mean = (csg[:, 1:] - lower) / jnp.minimum(pos, float(w))[None, :, None]
        groups.append(mean - a32[..., sl])
    d = jnp.stack(groups, axis=2).astype(x.dtype)
    y_a = jnp.einsum('bsgc,gcd->bsgd', d, pool_w).reshape(bsz, s, D_POOL) * pool_scale

    glu = b_val * jax.nn.sigmoid(b_gate)
    h = causal_dwconv(glu, dw_w, dw_b)
    y_b = jax.nn.silu(layer_norm(h, cn_g, cn_b))

    return jnp.concatenate([y_a, y_b], axis=-1) @ w_out


def chunked_rel_attention(x, w_qkv, rel_bias, w_o):
    bsz, s, _ = x.shape
    nc = s // CHUNK
    pad = LEFT_CHUNKS * CHUNK
    q, k, v = jnp.split(x @ w_qkv, 3, axis=-1)
    q = q.reshape(bsz, nc, CHUNK, N_HEADS, HEAD_DIM).transpose(1, 0, 2, 3, 4)
    k = jnp.pad(k.reshape(bsz, s, N_HEADS, HEAD_DIM), ((0, 0), (pad, 0), (0, 0), (0, 0)))
    v = jnp.pad(v.reshape(bsz, s, N_HEADS, HEAD_DIM), ((0, 0), (pad, 0), (0, 0), (0, 0)))

    qi = jnp.arange(CHUNK)[:, None]
    kj = jnp.arange(BAND)[None, :]
    rel = jnp.clip(pad + qi - kj, -MAX_REL_DIST, MAX_REL_DIST) + MAX_REL_DIST
    bias = rel_bias[:, rel].astype(jnp.float32)
    scale = HEAD_DIM ** -0.5

    def one_chunk(args):
        qc, c = args
        kb = lax.dynamic_slice_in_dim(k, c * CHUNK, BAND, axis=1)
        vb = lax.dynamic_slice_in_dim(v, c * CHUNK, BAND, axis=1)
        sc = jnp.einsum('bqhd,bkhd->bhqk', qc, kb).astype(jnp.float32) * scale + bias
        key_pos = c * CHUNK - pad + jnp.arange(BAND)
        sc = jnp.where((key_pos >= 0)[None, None, None, :], sc, NEG_INF)
        pr = jax.nn.softmax(sc, axis=-1).astype(vb.dtype)
        return jnp.einsum('bhqk,bkhd->bqhd', pr, vb)

    out = lax.map(one_chunk, (q, jnp.arange(nc)))
    out = out.transpose(1, 0, 2, 3, 4).reshape(bsz, s, D_MODEL)
    return out @ w_o


def conv_ffn(x, w_up, dw_w, dw_b, w_down):
    gate, val = jnp.split(x @ w_up, 2, axis=-1)
    gate = causal_dwconv(gate, dw_w, dw_b)
    return (jax.nn.gelu(gate) * val) @ w_down


def _fwd_setup_inputs(seed: int = 0) -> dict:
    key = jax.random.key(seed)
    ks = jax.random.split(key, 32)
    f32 = jnp.float32

    def nrm(k, shape, scale):
        return jax.random.normal(k, shape, f32) * scale

    d_in_even = D_POOL + 2 * D_CONV
    return {
        "x": nrm(ks[0], (BATCH, SEQ, D_MODEL), 1.0),
        "p": nrm(ks[1], (DEPTH, BATCH, SEQ, PLE_DIM), 1.0),
        "mix_w_in": nrm(ks[2], (N_EVEN, D_MODEL, d_in_even), D_MODEL ** -0.5),
        "pool_w": nrm(ks[3], (N_EVEN, len(POOL_WINDOWS), POOL_GROUP, POOL_GROUP), POOL_GROUP ** -0.5),
        "pool_scale": 1.0 + nrm(ks[4], (N_EVEN, D_POOL), 0.1),
        "conv_dw_w": nrm(ks[5], (N_EVEN, CONV_KERNEL, D_CONV), CONV_KERNEL ** -0.5),
        "conv_dw_b": nrm(ks[6], (N_EVEN, D_CONV), 0.02),
        "conv_ln_g": 1.0 + nrm(ks[7], (N_EVEN, D_CONV), 0.02),
        "conv_ln_b": nrm(ks[8], (N_EVEN, D_CONV), 0.02),
        "mix_w_out": nrm(ks[9], (N_EVEN, D_MODEL, D_MODEL), D_MODEL ** -0.5 * DEEPNORM_BETA),
        "attn_w_qkv": nrm(ks[10], (N_ODD, D_MODEL, 3 * D_MODEL), D_MODEL ** -0.5),
        "attn_rel_bias": nrm(ks[11], (N_ODD, N_HEADS, 2 * MAX_REL_DIST + 1), 0.5),
        "attn_w_o": nrm(ks[12], (N_ODD, D_MODEL, D_MODEL), D_MODEL ** -0.5 * DEEPNORM_BETA),
        "ln_mix_g": 1.0 + nrm(ks[13], (DEPTH, D_MODEL), 0.02),
        "ln_mix_b": nrm(ks[14], (DEPTH, D_MODEL), 0.02),
        "ffn_w_up": nrm(ks[15], (DEPTH, D_MODEL, 2 * D_FF), D_MODEL ** -0.5),
        "ffn_dw_w": nrm(ks[16], (DEPTH, FFN_CONV_KERNEL, D_FF), FFN_CONV_KERNEL ** -0.5),
        "ffn_dw_b": nrm(ks[17], (DEPTH, D_FF), 0.02),
        "ffn_w_down": nrm(ks[18], (DEPTH, D_FF, D_MODEL), D_FF ** -0.5 * DEEPNORM_BETA),
        "ple_w_proj": nrm(ks[19], (DEPTH, PLE_DIM, D_MODEL), PLE_DIM ** -0.5),
        "ple_w_gate": nrm(ks[20], (DEPTH, D_MODEL, D_MODEL), D_MODEL ** -0.5),
        "ple_b_gate": nrm(ks[21], (DEPTH, D_MODEL), 0.02),
        "ln_ffn_g": 1.0 + nrm(ks[22], (DEPTH, D_MODEL), 0.02),
        "ln_ffn_b": nrm(ks[23], (DEPTH, D_MODEL), 0.02),
    }


def _fwd_reference(x, p, mix_w_in, pool_w, pool_scale, conv_dw_w, conv_dw_b, conv_ln_g,
              conv_ln_b, mix_w_out, attn_w_qkv, attn_rel_bias, attn_w_o, ln_mix_g,
              ln_mix_b, ffn_w_up, ffn_dw_w, ffn_dw_b, ffn_w_down, ple_w_proj,
              ple_w_gate, ple_b_gate, ln_ffn_g, ln_ffn_b):
    for i in range(DEPTH):
        j = i // 2
        if i % 2 == 0:
            mix = pool_conv_mixer(x, mix_w_in[j], pool_w[j], pool_scale[j], conv_dw_w[j],
                                  conv_dw_b[j], conv_ln_g[j], conv_ln_b[j], mix_w_out[j])
        else:
            mix = chunked_rel_attention(x, attn_w_qkv[j], attn_rel_bias[j], attn_w_o[j])
        x = layer_norm(DEEPNORM_ALPHA * x + mix, ln_mix_g[i], ln_mix_b[i])
        ffn = conv_ffn(x, ffn_w_up[i], ffn_dw_w[i], ffn_dw_b[i], ffn_w_down[i])
        gate = jax.nn.sigmoid(x @ ple_w_gate[i] + ple_b_gate[i])
        ple = gate * (p[i] @ ple_w_proj[i])
        x = layer_norm(DEEPNORM_ALPHA * x + ffn + ple, ln_ffn_g[i], ln_ffn_b[i])
    return x


import jax as _jax
import jax.numpy as _jnp

TWIN_FORMAT = 'train_step'
FWD_PARAMS = ['x', 'p', 'mix_w_in', 'pool_w', 'pool_scale', 'conv_dw_w', 'conv_dw_b', 'conv_ln_g', 'conv_ln_b', 'mix_w_out', 'attn_w_qkv', 'attn_rel_bias', 'attn_w_o', 'ln_mix_g', 'ln_mix_b', 'ffn_w_up', 'ffn_dw_w', 'ffn_dw_b', 'ffn_w_down', 'ple_w_proj', 'ple_w_gate', 'ple_b_gate', 'ln_ffn_g', 'ln_ffn_b']
TWIN_WEIGHTS = ['mix_w_in', 'pool_w', 'pool_scale', 'conv_dw_w', 'conv_dw_b', 'conv_ln_g', 'conv_ln_b', 'mix_w_out', 'attn_w_qkv', 'attn_rel_bias', 'attn_w_o', 'ln_mix_g', 'ln_mix_b', 'ffn_w_up', 'ffn_dw_w', 'ffn_dw_b', 'ffn_w_down', 'ple_w_proj', 'ple_w_gate', 'ple_b_gate', 'ln_ffn_g', 'ln_ffn_b']
TWIN_DIFF_INPUT = 'x'
TWIN_INPUTS = ['x', 'p', 'mix_w_in', 'pool_w', 'pool_scale', 'conv_dw_w', 'conv_dw_b', 'conv_ln_g', 'conv_ln_b', 'mix_w_out', 'attn_w_qkv', 'attn_rel_bias', 'attn_w_o', 'ln_mix_g', 'ln_mix_b', 'ffn_w_up', 'ffn_dw_w', 'ffn_dw_b', 'ffn_w_down', 'ple_w_proj', 'ple_w_gate', 'ple_b_gate', 'ln_ffn_g', 'ln_ffn_b', 'loss_target', 'm_mix_w_in', 'm_pool_w', 'm_pool_scale', 'm_conv_dw_w', 'm_conv_dw_b', 'm_conv_ln_g', 'm_conv_ln_b', 'm_mix_w_out', 'm_attn_w_qkv', 'm_attn_rel_bias', 'm_attn_w_o', 'm_ln_mix_g', 'm_ln_mix_b', 'm_ffn_w_up', 'm_ffn_dw_w', 'm_ffn_dw_b', 'm_ffn_w_down', 'm_ple_w_proj', 'm_ple_w_gate', 'm_ple_b_gate', 'm_ln_ffn_g', 'm_ln_ffn_b', 'v_mix_w_in', 'v_pool_w', 'v_pool_scale', 'v_conv_dw_w', 'v_conv_dw_b', 'v_conv_ln_g', 'v_conv_ln_b', 'v_mix_w_out', 'v_attn_w_qkv', 'v_attn_rel_bias', 'v_attn_w_o', 'v_ln_mix_g', 'v_ln_mix_b', 'v_ffn_w_up', 'v_ffn_dw_w', 'v_ffn_dw_b', 'v_ffn_w_down', 'v_ple_w_proj', 'v_ple_w_gate', 'v_ple_b_gate', 'v_ln_ffn_g', 'v_ln_ffn_b']
TWIN_OUTPUTS = ['loss', 'grad_x', 'grad_mix_w_in', 'grad_pool_w', 'grad_pool_scale', 'grad_conv_dw_w', 'grad_conv_dw_b', 'grad_conv_ln_g', 'grad_conv_ln_b', 'grad_mix_w_out', 'grad_attn_w_qkv', 'grad_attn_rel_bias', 'grad_attn_w_o', 'grad_ln_mix_g', 'grad_ln_mix_b', 'grad_ffn_w_up', 'grad_ffn_dw_w', 'grad_ffn_dw_b', 'grad_ffn_w_down', 'grad_ple_w_proj', 'grad_ple_w_gate', 'grad_ple_b_gate', 'grad_ln_ffn_g', 'grad_ln_ffn_b', 'delta_mix_w_in', 'delta_pool_w', 'delta_pool_scale', 'delta_conv_dw_w', 'delta_conv_dw_b', 'delta_conv_ln_g', 'delta_conv_ln_b', 'delta_mix_w_out', 'delta_attn_w_qkv', 'delta_attn_rel_bias', 'delta_attn_w_o', 'delta_ln_mix_g', 'delta_ln_mix_b', 'delta_ffn_w_up', 'delta_ffn_dw_w', 'delta_ffn_dw_b', 'delta_ffn_w_down', 'delta_ple_w_proj', 'delta_ple_w_gate', 'delta_ple_b_gate', 'delta_ln_ffn_g', 'delta_ln_ffn_b', 'new_m_mix_w_in', 'new_m_pool_w', 'new_m_pool_scale', 'new_m_conv_dw_w', 'new_m_conv_dw_b', 'new_m_conv_ln_g', 'new_m_conv_ln_b', 'new_m_mix_w_out', 'new_m_attn_w_qkv', 'new_m_attn_rel_bias', 'new_m_attn_w_o', 'new_m_ln_mix_g', 'new_m_ln_mix_b', 'new_m_ffn_w_up', 'new_m_ffn_dw_w', 'new_m_ffn_dw_b', 'new_m_ffn_w_down', 'new_m_ple_w_proj', 'new_m_ple_w_gate', 'new_m_ple_b_gate', 'new_m_ln_ffn_g', 'new_m_ln_ffn_b', 'new_v_mix_w_in', 'new_v_pool_w', 'new_v_pool_scale', 'new_v_conv_dw_w', 'new_v_conv_dw_b', 'new_v_conv_ln_g', 'new_v_conv_ln_b', 'new_v_mix_w_out', 'new_v_attn_w_qkv', 'new_v_attn_rel_bias', 'new_v_attn_w_o', 'new_v_ln_mix_g', 'new_v_ln_mix_b', 'new_v_ffn_w_up', 'new_v_ffn_dw_w', 'new_v_ffn_dw_b', 'new_v_ffn_w_down', 'new_v_ple_w_proj', 'new_v_ple_w_gate', 'new_v_ple_b_gate', 'new_v_ln_ffn_g', 'new_v_ln_ffn_b']
TWIN_LEAF_KINDS = {'loss': 'loss', 'grad_x': 'grad_x', 'grad_mix_w_in': 'grad_w', 'grad_pool_w': 'grad_w', 'grad_pool_scale': 'grad_w', 'grad_conv_dw_w': 'grad_w', 'grad_conv_dw_b': 'grad_w', 'grad_conv_ln_g': 'grad_w', 'grad_conv_ln_b': 'grad_w', 'grad_mix_w_out': 'grad_w', 'grad_attn_w_qkv': 'grad_w', 'grad_attn_rel_bias': 'grad_w', 'grad_attn_w_o': 'grad_w', 'grad_ln_mix_g': 'grad_w', 'grad_ln_mix_b': 'grad_w', 'grad_ffn_w_up': 'grad_w', 'grad_ffn_dw_w': 'grad_w', 'grad_ffn_dw_b': 'grad_w', 'grad_ffn_w_down': 'grad_w', 'grad_ple_w_proj': 'grad_w', 'grad_ple_w_gate': 'grad_w', 'grad_ple_b_gate': 'grad_w', 'grad_ln_ffn_g': 'grad_w', 'grad_ln_ffn_b': 'grad_w', 'delta_mix_w_in': 'delta_w', 'delta_pool_w': 'delta_w', 'delta_pool_scale': 'delta_w', 'delta_conv_dw_w': 'delta_w', 'delta_conv_dw_b': 'delta_w', 'delta_conv_ln_g': 'delta_w', 'delta_conv_ln_b': 'delta_w', 'delta_mix_w_out': 'delta_w', 'delta_attn_w_qkv': 'delta_w', 'delta_attn_rel_bias': 'delta_w', 'delta_attn_w_o': 'delta_w', 'delta_ln_mix_g': 'delta_w', 'delta_ln_mix_b': 'delta_w', 'delta_ffn_w_up': 'delta_w', 'delta_ffn_dw_w': 'delta_w', 'delta_ffn_dw_b': 'delta_w', 'delta_ffn_w_down': 'delta_w', 'delta_ple_w_proj': 'delta_w', 'delta_ple_w_gate': 'delta_w', 'delta_ple_b_gate': 'delta_w', 'delta_ln_ffn_g': 'delta_w', 'delta_ln_ffn_b': 'delta_w', 'new_m_mix_w_in': 'new_m', 'new_m_pool_w': 'new_m', 'new_m_pool_scale': 'new_m', 'new_m_conv_dw_w': 'new_m', 'new_m_conv_dw_b': 'new_m', 'new_m_conv_ln_g': 'new_m', 'new_m_conv_ln_b': 'new_m', 'new_m_mix_w_out': 'new_m', 'new_m_attn_w_qkv': 'new_m', 'new_m_attn_rel_bias': 'new_m', 'new_m_attn_w_o': 'new_m', 'new_m_ln_mix_g': 'new_m', 'new_m_ln_mix_b': 'new_m', 'new_m_ffn_w_up': 'new_m', 'new_m_ffn_dw_w': 'new_m', 'new_m_ffn_dw_b': 'new_m', 'new_m_ffn_w_down': 'new_m', 'new_m_ple_w_proj': 'new_m', 'new_m_ple_w_gate': 'new_m', 'new_m_ple_b_gate': 'new_m', 'new_m_ln_ffn_g': 'new_m', 'new_m_ln_ffn_b': 'new_m', 'new_v_mix_w_in': 'new_v', 'new_v_pool_w': 'new_v', 'new_v_pool_scale': 'new_v', 'new_v_conv_dw_w': 'new_v', 'new_v_conv_dw_b': 'new_v', 'new_v_conv_ln_g': 'new_v', 'new_v_conv_ln_b': 'new_v', 'new_v_mix_w_out': 'new_v', 'new_v_attn_w_qkv': 'new_v', 'new_v_attn_rel_bias': 'new_v', 'new_v_attn_w_o': 'new_v', 'new_v_ln_mix_g': 'new_v', 'new_v_ln_mix_b': 'new_v', 'new_v_ffn_w_up': 'new_v', 'new_v_ffn_dw_w': 'new_v', 'new_v_ffn_dw_b': 'new_v', 'new_v_ffn_w_down': 'new_v', 'new_v_ple_w_proj': 'new_v', 'new_v_ple_w_gate': 'new_v', 'new_v_ple_b_gate': 'new_v', 'new_v_ln_ffn_g': 'new_v', 'new_v_ln_ffn_b': 'new_v'}


def _forward(args):
    return _fwd_reference(*[args[k] for k in FWD_PARAMS])


def _output_shape():
    out = _jax.eval_shape(lambda: _forward(_fwd_setup_inputs(0)))
    return out.shape, out.dtype

N_MICROBATCH = 1
ADAM_LR = 0.001
ADAM_B1 = 0.9
ADAM_B2 = 0.999
ADAM_EPS = 1e-08
ADAM_WD = 0.01
ADAM_STEP = 10
PER_EXAMPLE_BATCH_AXIS = {'x': 0, 'p': 1, 'loss_target': 0}
SHARED_INPUTS = []
_WEIGHT_DTYPES = {'mix_w_in': _jnp.float32, 'pool_w': _jnp.float32, 'pool_scale': _jnp.float32, 'conv_dw_w': _jnp.float32, 'conv_dw_b': _jnp.float32, 'conv_ln_g': _jnp.float32, 'conv_ln_b': _jnp.float32, 'mix_w_out': _jnp.float32, 'attn_w_qkv': _jnp.float32, 'attn_rel_bias': _jnp.float32, 'attn_w_o': _jnp.float32, 'ln_mix_g': _jnp.float32, 'ln_mix_b': _jnp.float32, 'ffn_w_up': _jnp.float32, 'ffn_dw_w': _jnp.float32, 'ffn_dw_b': _jnp.float32, 'ffn_w_down': _jnp.float32, 'ple_w_proj': _jnp.float32, 'ple_w_gate': _jnp.float32, 'ple_b_gate': _jnp.float32, 'ln_ffn_g': _jnp.float32, 'ln_ffn_b': _jnp.float32}
MOMENT_SCALE = {'mix_w_in': 3.631429e-02, 'pool_w': 5.042978e-02, 'pool_scale': 4.915771e-02, 'conv_dw_w': 3.555532e-02, 'conv_dw_b': 9.735285e-02, 'conv_ln_g': 4.734795e-02, 'conv_ln_b': 6.105396e-02, 'mix_w_out': 9.164255e-02, 'attn_w_qkv': 9.712117e-03, 'attn_rel_bias': 3.222536e-03, 'attn_w_o': 2.146230e-02, 'ln_mix_g': 7.810784e-01, 'ln_mix_b': 4.079551e-01, 'ffn_w_up': 2.284789e-02, 'ffn_dw_w': 2.302838e-02, 'ffn_dw_b': 2.242033e-02, 'ffn_w_down': 7.462319e-02, 'ple_w_proj': 6.112926e-02, 'ple_w_gate': 2.385125e-02, 'ple_b_gate': 3.710994e-02, 'ln_ffn_g': 2.260994e+01, 'ln_ffn_b': 1.194843e+00}


def _to_microbatches(a, axis):
    t = _jnp.moveaxis(a, axis, 0)
    t = t.reshape((N_MICROBATCH, t.shape[0] // N_MICROBATCH) + t.shape[1:])
    return _jnp.moveaxis(t, 1, axis + 1)


def setup_inputs(seed: int = 0) -> dict:
    inp = _fwd_setup_inputs(seed)
    key = _jax.random.fold_in(_jax.random.key(seed), 7919)
    shape, _ = _output_shape()
    out = dict(inp)
    out["loss_target"] = _jax.random.normal(_jax.random.fold_in(key, 0), shape, _jnp.float32)
    for i, name in enumerate(TWIN_WEIGHTS):
        w = inp[name].astype(_jnp.float32)
        if MOMENT_SCALE is None:
            s = _jnp.sqrt(_jnp.mean(_jnp.square(w)) + 1e-30)
        else:
            s = MOMENT_SCALE[name]
        km, kv = _jax.random.split(_jax.random.fold_in(key, i + 1))
        out[name] = w
        out["m_" + name] = s * _jax.random.normal(km, w.shape, _jnp.float32)
        out["v_" + name] = (s * s) * _jax.random.uniform(kv, w.shape, _jnp.float32, 0.5, 1.5)
    if N_MICROBATCH > 1:
        for name, axis in PER_EXAMPLE_BATCH_AXIS.items():
            out[name] = _to_microbatches(out[name], axis)
    return {'x': out['x'], 'p': out['p'], 'mix_w_in': out['mix_w_in'], 'pool_w': out['pool_w'], 'pool_scale': out['pool_scale'], 'conv_dw_w': out['conv_dw_w'], 'conv_dw_b': out['conv_dw_b'], 'conv_ln_g': out['conv_ln_g'], 'conv_ln_b': out['conv_ln_b'], 'mix_w_out': out['mix_w_out'], 'attn_w_qkv': out['attn_w_qkv'], 'attn_rel_bias': out['attn_rel_bias'], 'attn_w_o': out['attn_w_o'], 'ln_mix_g': out['ln_mix_g'], 'ln_mix_b': out['ln_mix_b'], 'ffn_w_up': out['ffn_w_up'], 'ffn_dw_w': out['ffn_dw_w'], 'ffn_dw_b': out['ffn_dw_b'], 'ffn_w_down': out['ffn_w_down'], 'ple_w_proj': out['ple_w_proj'], 'ple_w_gate': out['ple_w_gate'], 'ple_b_gate': out['ple_b_gate'], 'ln_ffn_g': out['ln_ffn_g'], 'ln_ffn_b': out['ln_ffn_b'], 'loss_target': out['loss_target'], 'm_mix_w_in': out['m_mix_w_in'], 'm_pool_w': out['m_pool_w'], 'm_pool_scale': out['m_pool_scale'], 'm_conv_dw_w': out['m_conv_dw_w'], 'm_conv_dw_b': out['m_conv_dw_b'], 'm_conv_ln_g': out['m_conv_ln_g'], 'm_conv_ln_b': out['m_conv_ln_b'], 'm_mix_w_out': out['m_mix_w_out'], 'm_attn_w_qkv': out['m_attn_w_qkv'], 'm_attn_rel_bias': out['m_attn_rel_bias'], 'm_attn_w_o': out['m_attn_w_o'], 'm_ln_mix_g': out['m_ln_mix_g'], 'm_ln_mix_b': out['m_ln_mix_b'], 'm_ffn_w_up': out['m_ffn_w_up'], 'm_ffn_dw_w': out['m_ffn_dw_w'], 'm_ffn_dw_b': out['m_ffn_dw_b'], 'm_ffn_w_down': out['m_ffn_w_down'], 'm_ple_w_proj': out['m_ple_w_proj'], 'm_ple_w_gate': out['m_ple_w_gate'], 'm_ple_b_gate': out['m_ple_b_gate'], 'm_ln_ffn_g': out['m_ln_ffn_g'], 'm_ln_ffn_b': out['m_ln_ffn_b'], 'v_mix_w_in': out['v_mix_w_in'], 'v_pool_w': out['v_pool_w'], 'v_pool_scale': out['v_pool_scale'], 'v_conv_dw_w': out['v_conv_dw_w'], 'v_conv_dw_b': out['v_conv_dw_b'], 'v_conv_ln_g': out['v_conv_ln_g'], 'v_conv_ln_b': out['v_conv_ln_b'], 'v_mix_w_out': out['v_mix_w_out'], 'v_attn_w_qkv': out['v_attn_w_qkv'], 'v_attn_rel_bias': out['v_attn_rel_bias'], 'v_attn_w_o': out['v_attn_w_o'], 'v_ln_mix_g': out['v_ln_mix_g'], 'v_ln_mix_b': out['v_ln_mix_b'], 'v_ffn_w_up': out['v_ffn_w_up'], 'v_ffn_dw_w': out['v_ffn_dw_w'], 'v_ffn_dw_b': out['v_ffn_dw_b'], 'v_ffn_w_down': out['v_ffn_w_down'], 'v_ple_w_proj': out['v_ple_w_proj'], 'v_ple_w_gate': out['v_ple_w_gate'], 'v_ple_b_gate': out['v_ple_b_gate'], 'v_ln_ffn_g': out['v_ln_ffn_g'], 'v_ln_ffn_b': out['v_ln_ffn_b']}


def _loss(weights, diff, rest, loss_target):
    with _jax.named_scope("forward"):
        args = {**rest, TWIN_DIFF_INPUT: diff, **{k: w.astype(_WEIGHT_DTYPES[k]) for k, w in weights.items()}}
        y = _forward(args)
    with _jax.named_scope("loss_head"):
        err = _jnp.square(y.astype(_jnp.float32) - loss_target)
        return 0.5 * _jnp.sum(_jnp.mean(err, axis=-1)) if err.ndim else 0.5 * err


def _adamw(w, g, m, v):
    m = ADAM_B1 * m + (1.0 - ADAM_B1) * g
    v = ADAM_B2 * v + (1.0 - ADAM_B2) * _jnp.square(g)
    m_hat = m / (1.0 - ADAM_B1 ** ADAM_STEP)
    v_hat = v / (1.0 - ADAM_B2 ** ADAM_STEP)
    delta = -ADAM_LR * (m_hat / (_jnp.sqrt(v_hat) + ADAM_EPS) + ADAM_WD * w)
    return delta, m, v


def reference(x, p, mix_w_in, pool_w, pool_scale, conv_dw_w, conv_dw_b, conv_ln_g, conv_ln_b, mix_w_out, attn_w_qkv, attn_rel_bias, attn_w_o, ln_mix_g, ln_mix_b, ffn_w_up, ffn_dw_w, ffn_dw_b, ffn_w_down, ple_w_proj, ple_w_gate, ple_b_gate, ln_ffn_g, ln_ffn_b, loss_target, m_mix_w_in, m_pool_w, m_pool_scale, m_conv_dw_w, m_conv_dw_b, m_conv_ln_g, m_conv_ln_b, m_mix_w_out, m_attn_w_qkv, m_attn_rel_bias, m_attn_w_o, m_ln_mix_g, m_ln_mix_b, m_ffn_w_up, m_ffn_dw_w, m_ffn_dw_b, m_ffn_w_down, m_ple_w_proj, m_ple_w_gate, m_ple_b_gate, m_ln_ffn_g, m_ln_ffn_b, v_mix_w_in, v_pool_w, v_pool_scale, v_conv_dw_w, v_conv_dw_b, v_conv_ln_g, v_conv_ln_b, v_mix_w_out, v_attn_w_qkv, v_attn_rel_bias, v_attn_w_o, v_ln_mix_g, v_ln_mix_b, v_ffn_w_up, v_ffn_dw_w, v_ffn_dw_b, v_ffn_w_down, v_ple_w_proj, v_ple_w_gate, v_ple_b_gate, v_ln_ffn_g, v_ln_ffn_b):
    given = dict(x=x, p=p, mix_w_in=mix_w_in, pool_w=pool_w, pool_scale=pool_scale, conv_dw_w=conv_dw_w, conv_dw_b=conv_dw_b, conv_ln_g=conv_ln_g, conv_ln_b=conv_ln_b, mix_w_out=mix_w_out, attn_w_qkv=attn_w_qkv, attn_rel_bias=attn_rel_bias, attn_w_o=attn_w_o, ln_mix_g=ln_mix_g, ln_mix_b=ln_mix_b, ffn_w_up=ffn_w_up, ffn_dw_w=ffn_dw_w, ffn_dw_b=ffn_dw_b, ffn_w_down=ffn_w_down, ple_w_proj=ple_w_proj, ple_w_gate=ple_w_gate, ple_b_gate=ple_b_gate, ln_ffn_g=ln_ffn_g, ln_ffn_b=ln_ffn_b, loss_target=loss_target, m_mix_w_in=m_mix_w_in, m_pool_w=m_pool_w, m_pool_scale=m_pool_scale, m_conv_dw_w=m_conv_dw_w, m_conv_dw_b=m_conv_dw_b, m_conv_ln_g=m_conv_ln_g, m_conv_ln_b=m_conv_ln_b, m_mix_w_out=m_mix_w_out, m_attn_w_qkv=m_attn_w_qkv, m_attn_rel_bias=m_attn_rel_bias, m_attn_w_o=m_attn_w_o, m_ln_mix_g=m_ln_mix_g, m_ln_mix_b=m_ln_mix_b, m_ffn_w_up=m_ffn_w_up, m_ffn_dw_w=m_ffn_dw_w, m_ffn_dw_b=m_ffn_dw_b, m_ffn_w_down=m_ffn_w_down, m_ple_w_proj=m_ple_w_proj, m_ple_w_gate=m_ple_w_gate, m_ple_b_gate=m_ple_b_gate, m_ln_ffn_g=m_ln_ffn_g, m_ln_ffn_b=m_ln_ffn_b, v_mix_w_in=v_mix_w_in, v_pool_w=v_pool_w, v_pool_scale=v_pool_scale, v_conv_dw_w=v_conv_dw_w, v_conv_dw_b=v_conv_dw_b, v_conv_ln_g=v_conv_ln_g, v_conv_ln_b=v_conv_ln_b, v_mix_w_out=v_mix_w_out, v_attn_w_qkv=v_attn_w_qkv, v_attn_rel_bias=v_attn_rel_bias, v_attn_w_o=v_attn_w_o, v_ln_mix_g=v_ln_mix_g, v_ln_mix_b=v_ln_mix_b, v_ffn_w_up=v_ffn_w_up, v_ffn_dw_w=v_ffn_dw_w, v_ffn_dw_b=v_ffn_dw_b, v_ffn_w_down=v_ffn_w_down, v_ple_w_proj=v_ple_w_proj, v_ple_w_gate=v_ple_w_gate, v_ple_b_gate=v_ple_b_gate, v_ln_ffn_g=v_ln_ffn_g, v_ln_ffn_b=v_ln_ffn_b)
    weights = {n: given[n] for n in TWIN_WEIGHTS}
    shared = {n: given[n] for n in SHARED_INPUTS}
    per_example = {n: given[n] for n in ['x', 'p']}
    grad_fn = _jax.value_and_grad(_loss, argnums=(0, 1))

    def one_microbatch(ex, loss_target):
        ex = dict(ex)
        diff = ex.pop(TWIN_DIFF_INPUT)
        return grad_fn(weights, diff, {**shared, **ex}, loss_target)

    if N_MICROBATCH == 1:
        loss, (grad_w, grad_x) = one_microbatch(per_example, given["loss_target"])
    else:
        def body(carry, xs):
            loss_sum, grad_sum = carry
            l_k, (gw_k, gx_k) = one_microbatch(xs[0], xs[1])
            with _jax.named_scope("update"):
                return (loss_sum + l_k, _jax.tree.map(_jnp.add, grad_sum, gw_k)), gx_k

        init = (_jnp.zeros((), _jnp.float32), _jax.tree.map(_jnp.zeros_like, weights))
        (loss, grad_w), grad_x = _jax.lax.scan(body, init, (per_example, given["loss_target"]))
    with _jax.named_scope("update"):
        delta_w, new_m, new_v = {}, {}, {}
        for n in TWIN_WEIGHTS:
            delta_w[n], new_m[n], new_v[n] = _adamw(weights[n], grad_w[n], given["m_" + n], given["v_" + n])
    return (loss, grad_x, *[grad_w[n] for n in TWIN_WEIGHTS], *[delta_w[n] for n in TWIN_WEIGHTS],
            *[new_m[n] for n in TWIN_WEIGHTS], *[new_v[n] for n in TWIN_WEIGHTS])
```

```python
import functools
import math

import jax
import jax.numpy as jnp
from jax import lax
from jax.experimental import pallas as pl
from jax.experimental.pallas import tpu as pltpu

F32 = jnp.float32
BF16 = jnp.bfloat16
MESH = pl.DeviceIdType.MESH

N_LAYERS = 2
ALPHA = (2 * N_LAYERS) ** 0.25
LN_EPS = 1e-5
NEG_INF = -1e30
CHUNK = 64
LEFT_CHUNKS = 8
PAD_ROWS = LEFT_CHUNKS * CHUNK
HEAD_DIM = 64
N_HEADS = 16
MAX_REL = 256
POOL_WINDOWS = (2, 4, 8, 16)
POOL_GROUP = 128
CONV_K = 31
FFN_K = 3
CONV_HALO = 32
FFN_HALO = 8
Q_TILE = 256
K_WIN = Q_TILE + PAD_ROWS
N_SHARD = 4
LANES = 128

ADAM_LR = 0.001
ADAM_B1 = 0.9
ADAM_B2 = 0.999
ADAM_EPS = 1e-08
ADAM_WD = 0.01
ADAM_STEP = 10
ADAM_BC1 = 1.0 - ADAM_B1 ** ADAM_STEP
ADAM_BC2 = 1.0 - ADAM_B2 ** ADAM_STEP

DIMS = {
    "nn": (((1,), (0,)), ((), ())),
    "nt": (((1,), (1,)), ((), ())),
    "tn": (((0,), (0,)), ((), ())),
}


def _cp(vmem_mb=48, **kw):
    return pltpu.CompilerParams(vmem_limit_bytes=vmem_mb * 1024 * 1024, **kw)


def _dot(a, b, mode):
    return lax.dot_general(a.astype(BF16), b.astype(BF16), DIMS[mode], preferred_element_type=F32)


def _sig(x):
    return 1.0 / (1.0 + jnp.exp(-x))


def _row_tile(s):
    return min(512, s // 4)


def _mm(name, mode, a, b, in_specs, out_shape, out_spec, acc_shape, grid, nk, zero_first=False, vmem_mb=48):
    out_f32 = out_shape.dtype == F32

    def body(a_ref, b_ref, o_ref, *scr):
        k = pl.program_id(2)

        def compute():
            part = _dot(a_ref[...], b_ref[...], mode)
            if nk == 1:
                o_ref[...] = part.astype(o_ref.dtype)
                return
            acc = o_ref if out_f32 else scr[0]

            @pl.when(k == 0)
            def _():
                acc[...] = part

            @pl.when(k > 0)
            def _():
                acc[...] += part

            if not out_f32:
                @pl.when(k == nk - 1)
                def _():
                    o_ref[...] = acc[...].astype(o_ref.dtype)

        if zero_first:
            @pl.when(pl.program_id(1) == 0)
            def _():
                o_ref[...] = jnp.zeros(o_ref.shape, o_ref.dtype)

            pl.when(pl.program_id(1) > 0)(compute)
        else:
            compute()

    scratch = [] if (nk == 1 or out_f32) else [pltpu.VMEM(acc_shape, F32)]
    return pl.pallas_call(
        body, name=name, grid=grid, in_specs=in_specs, out_specs=out_spec, out_shape=out_shape,
        scratch_shapes=scratch, compiler_params=_cp(vmem_mb),
    )(a, b)


def mm_cols_fwd(name, a, wc, out_dtype, pad_blocks=0):
    s, k = a.shape
    n4 = wc.shape[2]
    tm = _row_tile(s)
    nt = s // tm
    return _mm(
        name, "nn", a, wc,
        [pl.BlockSpec((tm, k), lambda j, i, r: (jnp.maximum(i - pad_blocks, 0), 0)),
         pl.BlockSpec((None, k, n4), lambda j, i, r: (j, 0, 0))],
        jax.ShapeDtypeStruct((s + pad_blocks * tm, N_SHARD * n4), out_dtype),
        pl.BlockSpec((tm, n4), lambda j, i, r: (i, j)),
        None, (N_SHARD, nt + pad_blocks, 1), 1, zero_first=pad_blocks > 0)


def mm_cols_dx(name, dy, wc):
    s = dy.shape[0]
    _, k, n4 = wc.shape
    tm = _row_tile(s)
    return _mm(
        name, "nt", dy, wc,
        [pl.BlockSpec((tm, n4), lambda g, i, r: (i, r)),
         pl.BlockSpec((None, k, n4), lambda g, i, r: (r, 0, 0))],
        jax.ShapeDtypeStruct((s, k), F32),
        pl.BlockSpec((tm, k), lambda g, i, r: (i, 0)),
        (tm, k), (1, s // tm, N_SHARD), N_SHARD)


def mm_cols_dw(name, a, dy):
    s, k = a.shape
    n4 = dy.shape[1] // N_SHARD
    tm = _row_tile(s)
    nt = s // tm
    return _mm(
        name, "tn", a, dy,
        [pl.BlockSpec((tm, k), lambda j, g, r: (r, 0)),
         pl.BlockSpec((tm, n4), lambda j, g, r: (r, j))],
        jax.ShapeDtypeStruct((N_SHARD, k, n4), F32),
        pl.BlockSpec((None, k, n4), lambda j, g, r: (j, 0, 0)),
        (k, n4), (N_SHARD, 1, nt), nt)


def _k_tile(k):
    return k if k <= 1024 else k // 2


def mm_rows_fwd(name, a, wr, out_dtype=F32):
    s, k = a.shape
    n = wr.shape[1]
    tm = _row_tile(s)
    tk = _k_tile(k)
    nk = k // tk
    return _mm(
        name, "nn", a, wr,
        [pl.BlockSpec((tm, tk), lambda g, i, r: (i, r)),
         pl.BlockSpec((tk, n), lambda g, i, r: (r, 0))],
        jax.ShapeDtypeStruct((s, n), out_dtype),
        pl.BlockSpec((tm, n), lambda g, i, r: (i, 0)),
        (tm, n), (1, s // tm, nk), nk)


def mm_rows_dx(name, dy, wr, out_dtype=F32):
    s, n = dy.shape
    k = wr.shape[0]
    tm = _row_tile(s)
    tk = _k_tile(k)
    return _mm(
        name, "nt", dy, wr,
        [pl.BlockSpec((tm, n), lambda j, i, r: (i, 0)),
         pl.BlockSpec((tk, n), lambda j, i, r: (j, 0))],
        jax.ShapeDtypeStruct((s, k), out_dtype),
        pl.BlockSpec((tm, tk), lambda j, i, r: (i, j)),
        None, (k // tk, s // tm, 1), 1)


def mm_rows_dw(name, a, dy):
    s, k = a.shape
    n = dy.shape[1]
    tm = _row_tile(s)
    tk = _k_tile(k)
    nt = s // tm
    return _mm(
        name, "tn", a, dy,
        [pl.BlockSpec((tm, tk), lambda j, g, r: (r, j)),
         pl.BlockSpec((tm, n), lambda j, g, r: (r, 0))],
        jax.ShapeDtypeStruct((k, n), F32),
        pl.BlockSpec((tk, n), lambda j, g, r: (j, 0)),
        (tk, n), (k // tk, 1, nt), nt)


def _row(tm, c, col=0):
    return pl.BlockSpec((tm, c), lambda i: (i, col))


def _full(shape):
    nd = len(shape)
    return pl.BlockSpec(shape, lambda i: (0,) * nd)


def _prev(tm, h, c, col=0):
    return pl.BlockSpec((h, c), lambda i: (jnp.maximum(i * (tm // h) - 1, 0), col))


def _next(tm, h, c, s, col=0):
    return pl.BlockSpec((h, c), lambda i: (jnp.minimum((i + 1) * (tm // h), s // h - 1), col))


def _acc_add(ref, first, val):
    @pl.when(first)
    def _():
        ref[...] = val

    @pl.when(jnp.logical_not(first))
    def _():
        ref[...] += val


def _colsum(v):
    return jnp.sum(v, axis=0, keepdims=True)


def _ln_stats(z):
    mu = jnp.mean(z, axis=-1, keepdims=True)
    zc = z - mu
    var = jnp.mean(zc * zc, axis=-1, keepdims=True)
    rstd = lax.rsqrt(var + LN_EPS)
    return zc * rstd, rstd


def _ln_bwd(dxhat, xhat, rstd):
    m1 = jnp.mean(dxhat, axis=-1, keepdims=True)
    m2 = jnp.mean(dxhat * xhat, axis=-1, keepdims=True)
    return rstd * (dxhat - m1 - xhat * m2)


def ln_fwd(name, x, f, g, b, ple=None):
    s, d = x.shape
    tm = _row_tile(s)
    n_in = 2 + (3 if ple is not None else 0)

    def body(*refs):
        x_ref, f_ref = refs[0], refs[1]
        g_ref, b_ref = refs[n_in], refs[n_in + 1]
        y_ref, xh_ref, rs_ref = refs[n_in + 2:]
        z = ALPHA * x_ref[...] + f_ref[...]
        if ple is not None:
            pgl_ref, pp_ref, bg_ref = refs[2:5]
            z = z + _sig(pgl_ref[...] + bg_ref[...]) * pp_ref[...]
        xhat, rstd = _ln_stats(z)
        y_ref[...] = xhat * g_ref[...] + b_ref[...]
        xh_ref[...] = xhat
        rs_ref[...] = jnp.broadcast_to(rstd, rs_ref.shape)

    ins = [x, f]
    specs = [_row(tm, d), _row(tm, d)]
    if ple is not None:
        pgl, pp, bg = ple
        ins += [pgl, pp, bg]
        specs += [_row(tm, d), _row(tm, d), _full((1, d))]
    ins += [g, b]
    specs += [_full((1, d)), _full((1, d))]
    return pl.pallas_call(
        body, name=name, grid=(s // tm,), in_specs=specs,
        out_specs=[_row(tm, d), _row(tm, d), _row(tm, LANES)],
        out_shape=[jax.ShapeDtypeStruct((s, d), F32), jax.ShapeDtypeStruct((s, d), F32),
                   jax.ShapeDtypeStruct((s, LANES), F32)],
        compiler_params=_cp(),
    )(*ins)


def ln_bwd(name, parts, xhat, rstd, g, ple=None):
    s, d = xhat.shape
    tm = _row_tile(s)
    coefs = [c for c, _ in parts]
    n_p = len(parts)
    n_in = n_p + 3 + (3 if ple is not None else 0)

    def body(*refs):
        first = pl.program_id(0) == 0
        dy = coefs[0] * refs[0][...].astype(F32)
        for j in range(1, n_p):
            dy = dy + coefs[j] * refs[j][...].astype(F32)
        xh = refs[n_p][...]
        rs = refs[n_p + 1][:, 0:1]
        g_v = refs[n_p + 2][...]
        outs = refs[n_in:]
        dz = _ln_bwd(dy * g_v, xh, rs)
        outs[0][...] = dz
        _acc_add(outs[1], first, _colsum(dy * xh))
        _acc_add(outs[2], first, _colsum(dy))
        if ple is not None:
            pgl_ref, pp_ref, bg_ref = refs[n_p + 3:n_p + 6]
            pg = _sig(pgl_ref[...] + bg_ref[...])
            dpgl = dz * pp_ref[...] * pg * (1.0 - pg)
            outs[3][...] = (dz * pg).astype(BF16)
            outs[4][...] = dpgl.astype(BF16)
            _acc_add(outs[5], first, _colsum(dpgl))

    ins = [p for _, p in parts] + [xhat, rstd, g]
    specs = [_row(tm, d)] * n_p + [_row(tm, d), _row(tm, LANES), _full((1, d))]
    out_specs = [_row(tm, d), _full((1, d)), _full((1, d))]
    out_shape = [jax.ShapeDtypeStruct((s, d), F32), jax.ShapeDtypeStruct((1, d), F32),
                 jax.ShapeDtypeStruct((1, d), F32)]
    if ple is not None:
        pgl, pp, bg = ple
        ins += [pgl, pp, bg]
        specs += [_row(tm, d), _row(tm, d), _full((1, d))]
        out_specs += [_row(tm, d), _row(tm, d), _full((1, d))]
        out_shape += [jax.ShapeDtypeStruct((s, d), BF16), jax.ShapeDtypeStruct((s, d), BF16),
                      jax.ShapeDtypeStruct((1, d), F32)]
    return pl.pallas_call(
        body, name=name, grid=(s // tm,), in_specs=specs, out_specs=out_specs, out_shape=out_shape,
        compiler_params=_cp(),
    )(*ins)


def loss_fwd_bwd(name, y, target):
    s, d = y.shape
    tm = _row_tile(s)

    def body(y_ref, t_ref, dy_ref, l_ref):
        first = pl.program_id(0) == 0
        err = y_ref[...] - t_ref[...]
        dy_ref[...] = err * (1.0 / d)
        part = 0.5 * jnp.sum(jnp.mean(err * err, axis=-1, keepdims=True), axis=0, keepdims=True)
        _acc_add(l_ref, first, jnp.broadcast_to(part, l_ref.shape))

    return pl.pallas_call(
        body, name=name, grid=(s // tm,), in_specs=[_row(tm, d), _row(tm, d)],
        out_specs=[_row(tm, d), _full((8, LANES))],
        out_shape=[jax.ShapeDtypeStruct((s, d), F32), jax.ShapeDtypeStruct((8, LANES), F32)],
        compiler_params=_cp(),
    )(y, target)


def scaled_sum(name, parts):
    s, d = parts[0][1].shape
    tm = _row_tile(s)
    coefs = [c for c, _ in parts]

    def body(*refs):
        acc = coefs[0] * refs[0][...].astype(F32)
        for j in range(1, len(coefs)):
            acc = acc + coefs[j] * refs[j][...].astype(F32)
        refs[-1][...] = acc

    return pl.pallas_call(
        body, name=name, grid=(s // tm,), in_specs=[_row(tm, d)] * len(parts), out_specs=_row(tm, d),
        out_shape=jax.ShapeDtypeStruct((s, d), F32), compiler_params=_cp(),
    )(*[p for _, p in parts])


def _tile_pos(i, tm, rows):
    return (i * tm + lax.broadcasted_iota(jnp.int32, (rows, 1), 0) + 1).astype(F32)


def mixer_fwd(name, u, pool_w, pool_scale, conv_w, conv_b, cn_g, cn_b):
    s = u.shape[0]
    dp = 512
    tm = min(256, s // 4)
    h = CONV_HALO

    def body(a_c, a_p, bv_c, bv_p, bg_c, bg_p, pw_ref, ps_ref, cw_ref, cb_ref, cg_ref, cbt_ref,
             cat_ref, d_ref, e_ref, glu_ref, hh_ref, rs_ref, ext_a, ext_g):
        i = pl.program_id(0)
        first = i == 0
        ext_a[0:h, :] = jnp.where(first, 0.0, a_p[...])
        ext_a[h:, :] = a_c[...]
        ext_g[0:h, :] = jnp.where(first, 0.0, bv_p[...] * _sig(bg_p[...]))
        glu = bv_c[...] * _sig(bg_c[...])
        ext_g[h:, :] = glu
        glu_ref[...] = glu
        pos = _tile_pos(i, tm, tm)
        for gi, w in enumerate(POOL_WINDOWS):
            cs = slice(gi * POOL_GROUP, (gi + 1) * POOL_GROUP)
            a_g = ext_a[pl.ds(h, tm), cs]
            acc = a_g
            for sh in range(1, w):
                acc = acc + ext_a[pl.ds(h - sh, tm), cs]
            d_g = acc / jnp.minimum(pos, float(w)) - a_g
            d_ref[:, cs] = d_g.astype(BF16)
            e_g = _dot(d_g, pw_ref[gi], "nn")
            e_ref[:, cs] = e_g
            cat_ref[:, cs] = (e_g * ps_ref[:, cs]).astype(BF16)
        hcv = jnp.broadcast_to(cb_ref[...], (tm, dp))
        for sh in range(CONV_K):
            hcv = hcv + ext_g[pl.ds(h - sh, tm), :] * cw_ref[pl.ds(CONV_K - 1 - sh, 1), :]
        hhat, rstd = _ln_stats(hcv)
        hl = hhat * cg_ref[...] + cbt_ref[...]
        cat_ref[:, dp:] = (hl * _sig(hl)).astype(BF16)
        hh_ref[...] = hhat
        rs_ref[...] = jnp.broadcast_to(rstd, rs_ref.shape)

    specs = [_row(tm, dp, 0), _prev(tm, h, dp, 0), _row(tm, dp, 1), _prev(tm, h, dp, 1),
             _row(tm, dp, 2), _prev(tm, h, dp, 2),
             _full((4, POOL_GROUP, POOL_GROUP)), _full((1, dp)), _full((CONV_K, dp)),
             _full((1, dp)), _full((1, dp)), _full((1, dp))]
    out_specs = [_row(tm, 2 * dp), _row(tm, dp), _row(tm, dp), _row(tm, dp), _row(tm, dp), _row(tm, LANES)]
    out_shape = [jax.ShapeDtypeStruct((s, 2 * dp), BF16), jax.ShapeDtypeStruct((s, dp), BF16),
                 jax.ShapeDtypeStruct((s, dp), F32), jax.ShapeDtypeStruct((s, dp), F32),
                 jax.ShapeDtypeStruct((s, dp), F32), jax.ShapeDtypeStruct((s, LANES), F32)]
    return pl.pallas_call(
        body, name=name, grid=(s // tm,), in_specs=specs, out_specs=out_specs, out_shape=out_shape,
        scratch_shapes=[pltpu.VMEM((h + tm, dp), F32), pltpu.VMEM((h + tm, dp), F32)],
        compiler_params=_cp(),
    )(u, u, u, u, u, u, pool_w, pool_scale, conv_w, conv_b, cn_g, cn_b)


def mixer_bwd(name, dcat, u, d_sv, e_sv, glu_sv, hh_sv, rs_sv, pool_w, pool_scale, conv_w, cn_g, cn_b):
    s = u.shape[0]
    dp = 512
    tm = min(256, s // 4)
    h = CONV_HALO
    nt = s // tm

    def body(dc_c, dc_n, bv_c, bg_c, d_c, e_c, gl_c, gl_p, hh_c, hh_n, rs_c, rs_n,
             pw_ref, ps_ref, cw_ref, cg_ref, cbt_ref,
             du_ref, dpw_ref, dps_ref, dcw_ref, dcb_ref, dcg_ref, dcbt_ref,
             ext_dh, ext_g, ext_r):
        i = pl.program_id(0)
        first = i == 0
        last = i == nt - 1
        cg = cg_ref[...]

        def conv_grads(dyb, hhat, rstd):
            hl = hhat * cg + cbt_ref[...]
            sg = _sig(hl)
            dhl = dyb * (sg * (1.0 + hl * (1.0 - sg)))
            return _ln_bwd(dhl * cg, hhat, rstd), dhl

        hh_cur = hh_c[...]
        dh_c, dhl_c = conv_grads(dc_c[:, dp:], hh_cur, rs_c[:, 0:1])
        dh_n, _ = conv_grads(dc_n[:, dp:], hh_n[...], rs_n[:, 0:1])
        ext_dh[0:tm, :] = dh_c
        ext_dh[tm:, :] = jnp.where(last, 0.0, dh_n)
        ext_g[0:h, :] = jnp.where(first, 0.0, gl_p[...])
        ext_g[h:, :] = gl_c[...]
        dglu = jnp.zeros((tm, dp), F32)
        for sh in range(CONV_K):
            dglu = dglu + ext_dh[pl.ds(sh, tm), :] * cw_ref[pl.ds(CONV_K - 1 - sh, 1), :]

        @pl.when(first)
        def _():
            dcw_ref[...] = jnp.zeros(dcw_ref.shape, F32)

        for sh in range(CONV_K):
            dcw_ref[pl.ds(CONV_K - 1 - sh, 1), :] += _colsum(dh_c * ext_g[pl.ds(h - sh, tm), :])
        _acc_add(dcb_ref, first, _colsum(dh_c))
        _acc_add(dcg_ref, first, _colsum(dhl_c * hh_cur))
        _acc_add(dcbt_ref, first, _colsum(dhl_c))
        sgate = _sig(bg_c[...])
        bv = bv_c[...]
        du_ref[:, dp:2 * dp] = dglu * sgate
        du_ref[:, 2 * dp:] = dglu * bv * sgate * (1.0 - sgate)

        pos_c = _tile_pos(i, tm, tm)
        pos_n = _tile_pos(i + 1, tm, h)
        _acc_add(dps_ref, first, _colsum(dc_c[:, :dp] * e_c[...]))
        for gi, w in enumerate(POOL_WINDOWS):
            cs = slice(gi * POOL_GROUP, (gi + 1) * POOL_GROUP)
            pw = pw_ref[gi]
            de_c = dc_c[:, cs] * ps_ref[:, cs]
            de_n = dc_n[:, cs] * ps_ref[:, cs]
            dd_c = _dot(de_c, pw, "nt")
            dd_n = _dot(de_n, pw, "nt")
            ext_r[0:tm, :] = dd_c / jnp.minimum(pos_c, float(w))
            ext_r[tm:, :] = jnp.where(last, 0.0, dd_n / jnp.minimum(pos_n, float(w)))
            acc = -dd_c
            for sh in range(w):
                acc = acc + ext_r[pl.ds(sh, tm), :]
            du_ref[:, cs] = acc
            dpw_g = _dot(d_c[:, cs], de_c, "tn")

            @pl.when(first)
            def _():
                dpw_ref[gi] = dpw_g

            @pl.when(jnp.logical_not(first))
            def _():
                dpw_ref[gi] += dpw_g

    specs = [_row(tm, 2 * dp), _next(tm, h, 2 * dp, s), _row(tm, dp, 1), _row(tm, dp, 2),
             _row(tm, dp), _row(tm, dp), _row(tm, dp), _prev(tm, h, dp),
             _row(tm, dp), _next(tm, h, dp, s), _row(tm, LANES), _next(tm, h, LANES, s),
             _full((4, POOL_GROUP, POOL_GROUP)), _full((1, dp)), _full((CONV_K, dp)),
             _full((1, dp)), _full((1, dp))]
    out_specs = [_row(tm, 3 * dp), _full((4, POOL_GROUP, POOL_GROUP)), _full((1, dp)), _full((CONV_K, dp)),
                 _full((1, dp)), _full((1, dp)), _full((1, dp))]
    out_shape = [jax.ShapeDtypeStruct((s, 3 * dp), F32),
                 jax.ShapeDtypeStruct((4, POOL_GROUP, POOL_GROUP), F32), jax.ShapeDtypeStruct((1, dp), F32),
                 jax.ShapeDtypeStruct((CONV_K, dp), F32), jax.ShapeDtypeStruct((1, dp), F32),
                 jax.ShapeDtypeStruct((1, dp), F32), jax.ShapeDtypeStruct((1, dp), F32)]
    return pl.pallas_call(
        body, name=name, grid=(nt,), in_specs=specs, out_specs=out_specs, out_shape=out_shape,
        scratch_shapes=[pltpu.VMEM((tm + h, dp), F32), pltpu.VMEM((h + tm, dp), F32),
                        pltpu.VMEM((tm + h, POOL_GROUP), F32)],
        compiler_params=_cp(),
    )(dcat, dcat, u, u, d_sv, e_sv, glu_sv, glu_sv, hh_sv, hh_sv, rs_sv, rs_sv,
      pool_w, pool_scale, conv_w, cn_g, cn_b)


GELU_C = math.sqrt(2.0 / math.pi)


def _gelu_parts(x):
    inner = GELU_C * (x + 0.044715 * x * x * x)
    t = jnp.tanh(inner)
    gelu = 0.5 * x * (1.0 + t)
    dgelu = 0.5 * (1.0 + t) + 0.5 * x * (1.0 - t * t) * GELU_C * (1.0 + 3.0 * 0.044715 * x * x)
    return gelu, dgelu


def ffn_act_fwd(name, gv, dw_w, dw_b):
    s = gv.shape[0]
    dff = gv.shape[1] // 2
    tm = min(128, s // 4)
    h = FFN_HALO

    def body(g_c, g_p, v_c, w_ref, b_ref, hid_ref, ext):
        first = pl.program_id(0) == 0
        ext[0:h, :] = jnp.where(first, 0.0, g_p[...])
        ext[h:, :] = g_c[...]
        gc = jnp.broadcast_to(b_ref[...], (tm, dff))
        for sh in range(FFN_K):
            gc = gc + ext[pl.ds(h - sh, tm), :] * w_ref[pl.ds(FFN_K - 1 - sh, 1), :]
        gelu, _ = _gelu_parts(gc)
        hid_ref[...] = (gelu * v_c[...]).astype(BF16)

    return pl.pallas_call(
        body, name=name, grid=(s // tm,),
        in_specs=[_row(tm, dff, 0), _prev(tm, h, dff, 0), _row(tm, dff, 1), _full((FFN_K, dff)), _full((1, dff))],
        out_specs=_row(tm, dff), out_shape=jax.ShapeDtypeStruct((s, dff), BF16),
        scratch_shapes=[pltpu.VMEM((h + tm, dff), F32)], compiler_params=_cp(),
    )(gv, gv, gv, dw_w, dw_b)


def ffn_act_bwd(name, dhid, gv, dw_w, dw_b):
    s = gv.shape[0]
    dff = gv.shape[1] // 2
    tm = min(128, s // 4)
    h = FFN_HALO
    nt = s // tm

    def body(dh_c, dh_n, g_p, g_c, g_n, v_c, v_n, w_ref, b_ref, dgv_ref, dw_ref, db_ref, ext_g, ext_d):
        i = pl.program_id(0)
        first = i == 0
        last = i == nt - 1
        ext_g[0:h, :] = jnp.where(first, 0.0, g_p[...])
        ext_g[pl.ds(h, tm), :] = g_c[...]
        ext_g[pl.ds(h + tm, h), :] = g_n[...]
        gc = jnp.broadcast_to(b_ref[...], (tm + h, dff))
        for sh in range(FFN_K):
            gc = gc + ext_g[pl.ds(h - sh, tm + h), :] * w_ref[pl.ds(FFN_K - 1 - sh, 1), :]
        gelu, dgelu = _gelu_parts(gc)
        dgc_c = dh_c[...] * v_c[...] * dgelu[0:tm]
        ext_d[0:tm, :] = dgc_c
        ext_d[tm:, :] = jnp.where(last, 0.0, dh_n[...] * v_n[...] * dgelu[tm:])
        dgate = jnp.zeros((tm, dff), F32)
        for sh in range(FFN_K):
            dgate = dgate + ext_d[pl.ds(sh, tm), :] * w_ref[pl.ds(FFN_K - 1 - sh, 1), :]
        dgv_ref[:, :dff] = dgate.astype(BF16)
        dgv_ref[:, dff:] = (dh_c[...] * gelu[0:tm]).astype(BF16)

        @pl.when(first)
        def _():
            dw_ref[...] = jnp.zeros(dw_ref.shape, F32)

        for sh in range(FFN_K):
            dw_ref[pl.ds(FFN_K - 1 - sh, 1), :] += _colsum(dgc_c * ext_g[pl.ds(h - sh, tm), :])
        _acc_add(db_ref, first, _colsum(dgc_c))

    return pl.pallas_call(
        body, name=name, grid=(nt,),
        in_specs=[_row(tm, dff), _next(tm, h, dff, s),
                  _prev(tm, h, dff, 0), _row(tm, dff, 0), _next(tm, h, dff, s, 0),
                  _row(tm, dff, 1), _next(tm, h, dff, s, 1),
                  _full((FFN_K, dff)), _full((1, dff))],
        out_specs=[_row(tm, 2 * dff), _full((FFN_K, dff)), _full((1, dff))],
        out_shape=[jax.ShapeDtypeStruct((s, 2 * dff), BF16), jax.ShapeDtypeStruct((FFN_K, dff), F32),
                   jax.ShapeDtypeStruct((1, dff), F32)],
        scratch_shapes=[pltpu.VMEM((h + tm + h, dff), F32), pltpu.VMEM((tm + h, dff), F32)],
        compiler_params=_cp(),
    )(dhid, dhid, gv, gv, gv, gv, gv, dw_w, dw_b)


def _toeplitz_bias(rel_bias):
    nh = rel_bias.shape[0]
    width = Q_TILE + K_WIN - 1
    sat = width - (2 * MAX_REL - 1)
    trev = jnp.concatenate(
        [jnp.broadcast_to(rel_bias[:, 2 * MAX_REL:], (nh, sat)), jnp.flip(rel_bias[:, 1:2 * MAX_REL], axis=1)], axis=1)
    z = jnp.broadcast_to(trev[:, None, :], (nh, Q_TILE, width)).reshape(nh, Q_TILE * width)
    z = jnp.pad(z, ((0, 0), (0, Q_TILE))).reshape(nh, Q_TILE, width + 1)[:, :, :K_WIN]
    return jnp.flip(z, axis=1)


def _shear_for_bias_grad(ds_sum):
    nh = ds_sum.shape[0]
    width = Q_TILE + K_WIN - 1
    z = jnp.flip(ds_sum, axis=1)
    z = jnp.pad(z, ((0, 0), (0, 0), (0, width + 1 - K_WIN))).reshape(nh, Q_TILE * (width + 1))
    z = z[:, :Q_TILE * width].reshape(nh, Q_TILE, width)
    return jnp.pad(z, ((0, 0), (0, 0), (0, 1)))


def _attn_mask(t):
    row = lax.broadcasted_iota(jnp.int32, (Q_TILE, K_WIN), 0)
    col = lax.broadcasted_iota(jnp.int32, (Q_TILE, K_WIN), 1)
    qc = row // CHUNK
    kc = col // CHUNK
    return (kc >= qc) & (kc <= qc + LEFT_CHUNKS) & (t * Q_TILE + col >= PAD_ROWS)


def _attn_probs(q2, k3, bias, mask, head):
    lane = lax.broadcasted_iota(jnp.int32, q2.shape, 1)
    q_h = jnp.where(lane // HEAD_DIM == head, q2, jnp.zeros_like(q2))
    sc = _dot(q_h, k3, "nt") * (HEAD_DIM ** -0.5) + bias
    sc = jnp.where(mask, sc, NEG_INF)
    m = jnp.max(sc, axis=-1, keepdims=True)
    p = jnp.exp(sc - m)
    return q_h, p / jnp.sum(p, axis=-1, keepdims=True)


def _attn_specs(d_model):
    nq = PAD_ROWS // Q_TILE
    hp_k = d_model // LANES
    specs = [pl.BlockSpec((Q_TILE, LANES), lambda hp, t: (t + nq, hp))]
    for which in (1, 2):
        for j in range(K_WIN // Q_TILE):
            specs.append(pl.BlockSpec((Q_TILE, LANES), lambda hp, t, j=j, which=which: (t + j, which * hp_k + hp)))
    specs.append(pl.BlockSpec((2, Q_TILE, K_WIN), lambda hp, t: (hp, 0, 0)))
    return specs


def attn_fwd(name, qkvp, bias):
    s = qkvp.shape[0] - PAD_ROWS
    d_model = qkvp.shape[1] // 3
    nw = K_WIN // Q_TILE

    def body(q_ref, *refs):
        k_refs, v_refs, b_ref, o_ref = refs[:nw], refs[nw:2 * nw], refs[2 * nw], refs[2 * nw + 1]
        t = pl.program_id(1)
        q2 = q_ref[...]
        k3 = jnp.concatenate([r[...] for r in k_refs], axis=0)
        v3 = jnp.concatenate([r[...] for r in v_refs], axis=0)
        mask = _attn_mask(t)
        outs = []
        for head in range(2):
            _, p = _attn_probs(q2, k3, b_ref[head], mask, head)
            outs.append(_dot(p, v3, "nn"))
        lane = lax.broadcasted_iota(jnp.int32, (Q_TILE, LANES), 1)
        o_ref[...] = jnp.where(lane < HEAD_DIM, outs[0], outs[1]).astype(BF16)

    return pl.pallas_call(
        body, name=name, grid=(d_model // LANES, s // Q_TILE),
        in_specs=_attn_specs(d_model), out_specs=pl.BlockSpec((Q_TILE, LANES), lambda hp, t: (t, hp)),
        out_shape=jax.ShapeDtypeStruct((s, d_model), BF16), compiler_params=_cp(),
    )(qkvp, *([qkvp] * (2 * nw)), bias)


def attn_bwd(name, qkvp, bias, do):
    s = qkvp.shape[0] - PAD_ROWS
    d_model = qkvp.shape[1] // 3
    nw = K_WIN // Q_TILE
    nt = s // Q_TILE
    scale = HEAD_DIM ** -0.5

    def body(q_ref, *refs):
        k_refs, v_refs = refs[:nw], refs[nw:2 * nw]
        b_ref, do_ref, dq_ref, dk_ref, dv_ref, ds_ref, dk_acc, dv_acc = refs[2 * nw:]
        t = pl.program_id(1)
        first = t == 0

        @pl.when(first)
        def _():
            dk_acc[...] = jnp.zeros(dk_acc.shape, F32)
            dv_acc[...] = jnp.zeros(dv_acc.shape, F32)

        q2 = q_ref[...]
        do2 = do_ref[...]
        k3 = jnp.concatenate([r[...] for r in k_refs], axis=0)
        v3 = jnp.concatenate([r[...] for r in v_refs], axis=0)
        mask = _attn_mask(t)
        lane = lax.broadcasted_iota(jnp.int32, (Q_TILE, LANES), 1)
        dqs = []
        dk_win = jnp.zeros((K_WIN, LANES), F32)
        dv_win = jnp.zeros((K_WIN, LANES), F32)
        for head in range(2):
            q_h, p = _attn_probs(q2, k3, b_ref[head], mask, head)
            do_h = jnp.where(lane // HEAD_DIM == head, do2, jnp.zeros_like(do2))
            dp = _dot(do_h, v3, "nt")
            ds = p * (dp - jnp.sum(p * dp, axis=-1, keepdims=True))
            _acc_add(ds_ref.at[head], first, ds)
            dsb = (ds * scale).astype(BF16)
            dqs.append(_dot(dsb, k3, "nn"))
            dk_win = dk_win + _dot(dsb, q_h, "tn")
            dv_win = dv_win + _dot(p, do_h, "tn")
        dq_ref[...] = jnp.where(lane < HEAD_DIM, dqs[0], dqs[1]).astype(BF16)
        start = pl.multiple_of(t * Q_TILE, Q_TILE)
        dk_acc[pl.ds(start, K_WIN), :] += dk_win
        dv_acc[pl.ds(start, K_WIN), :] += dv_win

        @pl.when(t == nt - 1)
        def _():
            dk_ref[...] = dk_acc[pl.ds(PAD_ROWS, s), :].astype(BF16)
            dv_ref[...] = dv_acc[pl.ds(PAD_ROWS, s), :].astype(BF16)

    specs = _attn_specs(d_model) + [pl.BlockSpec((Q_TILE, LANES), lambda hp, t: (t, hp))]
    col_spec = pl.BlockSpec((s, LANES), lambda hp, t: (0, hp))
    return pl.pallas_call(
        body, name=name, grid=(d_model // LANES, nt), in_specs=specs,
        out_specs=[pl.BlockSpec((Q_TILE, LANES), lambda hp, t: (t, hp)), col_spec, col_spec,
                   pl.BlockSpec((2, Q_TILE, K_WIN), lambda hp, t: (hp, 0, 0))],
        out_shape=[jax.ShapeDtypeStruct((s, d_model), BF16)] * 3
        + [jax.ShapeDtypeStruct((N_HEADS, Q_TILE, K_WIN), F32)],
        scratch_shapes=[pltpu.VMEM((PAD_ROWS + s, LANES), F32), pltpu.VMEM((PAD_ROWS + s, LANES), F32)],
        compiler_params=_cp(),
    )(qkvp, *([qkvp] * (2 * nw)), bias, do)


def bias_grad_reduce(name, sheared):
    nh, _, width = sheared.shape
    sat = width - 2 * MAX_REL

    def body(x_ref, col_ref, sat_ref):
        cols = _colsum(x_ref[...])
        col_ref[...] = cols
        e = lax.broadcasted_iota(jnp.int32, cols.shape, 1)
        tot = jnp.sum(jnp.where(e < sat, cols, 0.0), axis=-1, keepdims=True)
        sat_ref[...] = jnp.broadcast_to(tot, sat_ref.shape)

    return pl.pallas_call(
        body, name=name, grid=(nh,),
        in_specs=[pl.BlockSpec((None, Q_TILE, width), lambda hh: (hh, 0, 0))],
        out_specs=[pl.BlockSpec((None, 1, width), lambda hh: (hh, 0, 0)),
                   pl.BlockSpec((None, 1, LANES), lambda hh: (hh, 0, 0))],
        out_shape=[jax.ShapeDtypeStruct((nh, 1, width), F32), jax.ShapeDtypeStruct((nh, 1, LANES), F32)],
        compiler_params=_cp(),
    )(sheared)


def _ew_rows(r):
    for cand in (512, 256, 128, 64, 32, 16, 8):
        if r % cand == 0:
            return cand
    return r


def cast_bf16(name, w, layer=None):
    r, c = w.shape[-2:]
    tr = _ew_rows(r)

    def body(w_ref, o_ref):
        o_ref[...] = w_ref[...].astype(BF16)

    if layer is None:
        spec = pl.BlockSpec((tr, c), lambda i: (i, 0))
    else:
        spec = pl.BlockSpec((None, tr, c), lambda i: (layer, i, 0))
    return pl.pallas_call(
        body, name=name, grid=(r // tr,), in_specs=[spec], out_specs=pl.BlockSpec((tr, c), lambda i: (i, 0)),
        out_shape=jax.ShapeDtypeStruct((r, c), BF16), compiler_params=_cp(),
    )(w)


def adamw(name, w, grads, m, v):
    nl, r, c = w.shape
    tr = _ew_rows(r)

    def body(*refs):
        w_ref, m_ref, v_ref = refs[0], refs[1], refs[2]
        g_refs = refs[3:3 + nl]
        d_ref, nm_ref, nv_ref = refs[3 + nl:]
        layer = pl.program_id(0)
        g = g_refs[0][...]
        for j in range(1, nl):
            g = jnp.where(layer == j, g_refs[j][...], g)
        nm = ADAM_B1 * m_ref[...] + (1.0 - ADAM_B1) * g
        nv = ADAM_B2 * v_ref[...] + (1.0 - ADAM_B2) * (g * g)
        m_hat = nm / ADAM_BC1
        v_hat = nv / ADAM_BC2
        d_ref[...] = -ADAM_LR * (m_hat / (jnp.sqrt(v_hat) + ADAM_EPS) + ADAM_WD * w_ref[...])
        nm_ref[...] = nm
        nv_ref[...] = nv

    p_spec = pl.BlockSpec((None, tr, c), lambda l, i: (l, i, 0))
    g_spec = pl.BlockSpec((tr, c), lambda l, i: (i, 0))
    return pl.pallas_call(
        body, name=name, grid=(nl, r // tr), in_specs=[p_spec] * 3 + [g_spec] * nl, out_specs=[p_spec] * 3,
        out_shape=[jax.ShapeDtypeStruct((nl, r, c), F32)] * 3, compiler_params=_cp(),
    )(w, m, v, *grads)


def sum_blocks(name, gathered, n_blocks):
    r = gathered.shape[0] // n_blocks
    c = gathered.shape[1]
    tr = _ew_rows(r)
    nt = r // tr

    def body(*refs):
        acc = refs[0][...]
        for j in range(1, n_blocks):
            acc = acc + refs[j][...]
        refs[-1][...] = acc

    specs = [pl.BlockSpec((tr, c), lambda i, j=j: (j * nt + i, 0)) for j in range(n_blocks)]
    return pl.pallas_call(
        body, name=name, grid=(nt,), in_specs=specs, out_specs=pl.BlockSpec((tr, c), lambda i: (i, 0)),
        out_shape=jax.ShapeDtypeStruct((r, c), F32), compiler_params=_cp(),
    )(*([gathered] * n_blocks))


HBM = pl.BlockSpec(memory_space=pl.ANY)


def _place():
    return lax.axis_index("x"), lax.axis_index("y"), lax.axis_index("c")


def _other_chips(x, y):
    return [(1 - x, y), (x, 1 - y), (1 - x, 1 - y)]


def gather_weights(shards):
    n = len(shards)

    def body(*refs):
        sh = refs[:n]
        out = refs[n:2 * n]
        send, recv, fsend, frecv, lsem = refs[2 * n:]
        x, y, c = _place()
        me = 2 * x + y
        chips = _other_chips(x, y)
        sibling = (x, y, 1 - c)
        local = []
        firsts = []
        for a in range(n):
            hr = sh[a].shape[0] // 2
            cp = pltpu.make_async_copy(sh[a], out[a].at[me], lsem.at[a])
            cp.start()
            local.append(cp)
            for j, (cx, cy) in enumerate(chips):
                rc = pltpu.make_async_remote_copy(
                    src_ref=sh[a].at[pl.ds(c * hr, hr)], dst_ref=out[a].at[me, pl.ds(c * hr, hr)],
                    send_sem=send.at[3 * a + j], recv_sem=recv.at[3 * a + j],
                    device_id=(cx, cy, c), device_id_type=MESH)
                rc.start()
                firsts.append(rc)
        passed = []
        for a in range(n):
            hr = sh[a].shape[0] // 2
            for j, (cx, cy) in enumerate(chips):
                landed = out[a].at[2 * cx + cy, pl.ds(c * hr, hr)]
                pltpu.make_async_remote_copy(
                    src_ref=landed, dst_ref=landed, send_sem=send.at[3 * a + j], recv_sem=recv.at[3 * a + j],
                    device_id=(cx, cy, c), device_id_type=MESH).wait_recv()
                fw = pltpu.make_async_remote_copy(
                    src_ref=landed, dst_ref=landed, send_sem=fsend.at[3 * a + j], recv_sem=frecv.at[3 * a + j],
                    device_id=sibling, device_id_type=MESH)
                fw.start()
                passed.append(fw)
        for a in range(n):
            hr = sh[a].shape[0] // 2
            for j, (cx, cy) in enumerate(chips):
                theirs = out[a].at[2 * cx + cy, pl.ds((1 - c) * hr, hr)]
                pltpu.make_async_remote_copy(
                    src_ref=theirs, dst_ref=theirs, send_sem=fsend.at[3 * a + j], recv_sem=frecv.at[3 * a + j],
                    device_id=sibling, device_id_type=MESH).wait_recv()
        for cp in firsts + passed:
            cp.wait_send()
        for cp in local:
            cp.wait()

    return pl.pallas_call(
        body, name="gather_weights", in_specs=[HBM] * n, out_specs=[HBM] * n,
        out_shape=[jax.ShapeDtypeStruct((N_SHARD,) + w.shape, w.dtype) for w in shards],
        scratch_shapes=[pltpu.SemaphoreType.DMA((3 * n,))] * 4 + [pltpu.SemaphoreType.DMA((n,))],
        compiler_params=_cp(),
    )(*shards)


def swap_halves(grads):
    n = len(grads)

    def body(*refs):
        g = refs[:n]
        land = refs[n:2 * n]
        send, recv = refs[2 * n:]
        x, y, c = _place()
        copies = []
        for a in range(n):
            hr = g[a].shape[1] // 2
            rc = pltpu.make_async_remote_copy(
                src_ref=g[a].at[:, pl.ds((1 - c) * hr, hr)], dst_ref=land[a],
                send_sem=send.at[a], recv_sem=recv.at[a], device_id=(x, y, 1 - c), device_id_type=MESH)
            rc.start()
            copies.append(rc)
        for rc in copies:
            rc.wait()

    return pl.pallas_call(
        body, name="swap_halves", in_specs=[HBM] * n, out_specs=[HBM] * n,
        out_shape=[jax.ShapeDtypeStruct((N_SHARD, g.shape[1] // 2, g.shape[2]), F32) for g in grads],
        scratch_shapes=[pltpu.SemaphoreType.DMA((n,))] * 2,
        compiler_params=_cp(),
    )(*grads)


def add_halves(name, grad, landed, c_idx):
    _, r, c = grad.shape
    hr = r // 2
    tr = _ew_rows(hr)
    nt = hr // tr

    def body(c_ref, g_ref, l_ref, o_ref, ob_ref):
        tot = g_ref[...] + l_ref[...]
        o_ref[...] = tot
        ob_ref[...] = tot.astype(BF16)

    blk = pl.BlockSpec((None, tr, c), lambda sh, i, c_ref: (sh, i, 0))
    grid_spec = pltpu.PrefetchScalarGridSpec(
        num_scalar_prefetch=1, grid=(N_SHARD, nt),
        in_specs=[pl.BlockSpec((None, tr, c), lambda sh, i, c_ref: (sh, c_ref[0] * nt + i, 0)), blk],
        out_specs=[blk, blk])
    return pl.pallas_call(
        body, name=name, grid_spec=grid_spec,
        out_shape=[jax.ShapeDtypeStruct((N_SHARD, hr, c), F32), jax.ShapeDtypeStruct((N_SHARD, hr, c), BF16)],
        compiler_params=_cp(),
    )(c_idx, grad, landed)


def send_to_owners(halves):
    n = len(halves)

    def body(*refs):
        src = refs[:n]
        land = refs[n:2 * n]
        send, recv = refs[2 * n:]
        x, y, c = _place()
        chips = _other_chips(x, y)
        copies = []
        for a in range(n):
            for j, (cx, cy) in enumerate(chips):
                rc = pltpu.make_async_remote_copy(
                    src_ref=src[a].at[2 * cx + cy], dst_ref=land[a].at[j],
                    send_sem=send.at[3 * a + j], recv_sem=recv.at[3 * a + j],
                    device_id=(cx, cy, c), device_id_type=MESH)
                rc.start()
                copies.append(rc)
        for rc in copies:
            rc.wait()

    return pl.pallas_call(
        body, name="send_to_owners", in_specs=[HBM] * n, out_specs=[HBM] * n,
        out_shape=[jax.ShapeDtypeStruct((3,) + h.shape[1:], BF16) for h in halves],
        scratch_shapes=[pltpu.SemaphoreType.DMA((3 * n,))] * 2,
        compiler_params=_cp(),
    )(*halves)


def add_owned(name, own, landed, s_idx):
    _, hr, c = own.shape
    tr = _ew_rows(hr)

    def body(s_ref, o_ref, l0, l1, l2, out_ref):
        out_ref[...] = ((o_ref[...] + l0[...].astype(F32)) + l1[...].astype(F32)) + l2[...].astype(F32)

    grid_spec = pltpu.PrefetchScalarGridSpec(
        num_scalar_prefetch=1, grid=(hr // tr,),
        in_specs=[pl.BlockSpec((None, tr, c), lambda i, s_ref: (s_ref[0], i, 0))]
        + [pl.BlockSpec((None, tr, c), lambda i, s_ref, j=j: (j, i, 0)) for j in range(3)],
        out_specs=pl.BlockSpec((tr, c), lambda i, s_ref: (i, 0)))
    return pl.pallas_call(
        body, name=name, grid_spec=grid_spec, out_shape=jax.ShapeDtypeStruct((hr, c), F32),
        compiler_params=_cp(),
    )(s_idx, own, landed, landed, landed)


def join_halves(halves):
    n = len(halves)

    def body(*refs):
        src = refs[:n]
        out = refs[n:2 * n]
        send, recv, lsem = refs[2 * n:]
        x, y, c = _place()
        copies = []
        local = []
        for a in range(n):
            hr = src[a].shape[0]
            cp = pltpu.make_async_copy(src[a], out[a].at[pl.ds(c * hr, hr)], lsem.at[a])
            cp.start()
            local.append(cp)
            rc = pltpu.make_async_remote_copy(
                src_ref=src[a], dst_ref=out[a].at[pl.ds(c * hr, hr)],
                send_sem=send.at[a], recv_sem=recv.at[a], device_id=(x, y, 1 - c), device_id_type=MESH)
            rc.start()
            copies.append(rc)
        for a in range(n):
            hr = src[a].shape[0]
            theirs = out[a].at[pl.ds((1 - c) * hr, hr)]
            pltpu.make_async_remote_copy(
                src_ref=src[a], dst_ref=theirs, send_sem=send.at[a], recv_sem=recv.at[a],
                device_id=(x, y, 1 - c), device_id_type=MESH).wait_recv()
        for rc in copies:
            rc.wait_send()
        for cp in local:
            cp.wait()

    return pl.pallas_call(
        body, name="join_halves", in_specs=[HBM] * n, out_specs=[HBM] * n,
        out_shape=[jax.ShapeDtypeStruct((2 * h.shape[0], h.shape[1]), F32) for h in halves],
        scratch_shapes=[pltpu.SemaphoreType.DMA((n,))] * 3,
        compiler_params=_cp(),
    )(*halves)


def gather_small(name, block):
    m_per, n = block.shape

    def body(x_ref, out_ref, send_sems, recv_sems, local_sem):
        x, y, c = _place()
        me, sibling = (x, y, c), (x, y, 1 - c)
        chips = _other_chips(x, y)

        def rows(px, py, pc):
            return out_ref.at[pl.ds((4 * px + 2 * py + pc) * m_per, m_per), :]

        def copy(k, blk, to, src=None):
            return pltpu.make_async_remote_copy(
                src_ref=rows(*blk) if src is None else src, dst_ref=rows(*blk),
                send_sem=send_sems.at[k], recv_sem=recv_sems.at[k], device_id=to, device_id_type=MESH)

        mine = pltpu.make_async_copy(x_ref, rows(*me), local_sem)
        mine.start()
        first = [copy(0, me, sibling, src=x_ref)]
        first += [copy(1 + j, me, (*chip, c), src=x_ref) for j, chip in enumerate(chips)]
        for cp in first:
            cp.start()
        passed = [copy(4 + j, (*chip, c), sibling) for j, chip in enumerate(chips)]
        for j, chip in enumerate(chips):
            copy(1 + j, (*chip, c), me).wait_recv()
            passed[j].start()
        copy(0, sibling, me).wait_recv()
        for j, chip in enumerate(chips):
            copy(4 + j, (*chip, 1 - c), me).wait_recv()
        for cp in first + passed:
            cp.wait_send()
        mine.wait()

    return pl.pallas_call(
        body, name=name, out_shape=jax.ShapeDtypeStruct((8 * m_per, n), block.dtype),
        in_specs=[pl.BlockSpec(memory_space=pltpu.VMEM)], out_specs=pl.BlockSpec(memory_space=pltpu.VMEM),
        scratch_shapes=[pltpu.SemaphoreType.DMA((7,)), pltpu.SemaphoreType.DMA((7,)), pltpu.SemaphoreType.DMA],
        compiler_params=_cp(),
    )(block)


PACK_QUANTUM = 8 * LANES


def _pack(arrays):
    pieces = []
    for a in arrays:
        flat = a.reshape(-1)
        padded = -(-flat.shape[0] // PACK_QUANTUM) * PACK_QUANTUM
        pieces.append(jnp.pad(flat, (0, padded - flat.shape[0])).reshape(-1, LANES))
    return jnp.concatenate(pieces, axis=0)


def _unpack(packed, shapes):
    out = []
    row = 0
    for shp in shapes:
        size = math.prod(shp)
        rows = -(-size // PACK_QUANTUM) * 8
        out.append(packed[row:row + rows].reshape(-1)[:size].reshape(shp))
        row += rows
    return out


def kernel(x, p, mix_w_in, pool_w, pool_scale, conv_dw_w, conv_dw_b, conv_ln_g, conv_ln_b, mix_w_out, attn_w_qkv, attn_rel_bias, attn_w_o, ln_mix_g, ln_mix_b, ffn_w_up, ffn_dw_w, ffn_dw_b, ffn_w_down, ple_w_proj, ple_w_gate, ple_b_gate, ln_ffn_g, ln_ffn_b, loss_target, m_mix_w_in, m_pool_w, m_pool_scale, m_conv_dw_w, m_conv_dw_b, m_conv_ln_g, m_conv_ln_b, m_mix_w_out, m_attn_w_qkv, m_attn_rel_bias, m_attn_w_o, m_ln_mix_g, m_ln_mix_b, m_ffn_w_up, m_ffn_dw_w, m_ffn_dw_b, m_ffn_w_down, m_ple_w_proj, m_ple_w_gate, m_ple_b_gate, m_ln_ffn_g, m_ln_ffn_b, v_mix_w_in, v_pool_w, v_pool_scale, v_conv_dw_w, v_conv_dw_b, v_conv_ln_g, v_conv_ln_b, v_mix_w_out, v_attn_w_qkv, v_attn_rel_bias, v_attn_w_o, v_ln_mix_g, v_ln_mix_b, v_ffn_w_up, v_ffn_dw_w, v_ffn_dw_b, v_ffn_w_down, v_ple_w_proj, v_ple_w_gate, v_ple_b_gate, v_ln_ffn_g, v_ln_ffn_b):
    xi, yi, ci = _place()
    shard_idx = (2 * xi + yi).astype(jnp.int32)
    s_arr = shard_idx.reshape(1)
    c_arr = ci.astype(jnp.int32).reshape(1)

    x0 = x[0]
    target = loss_target[0]
    seq = x0.shape[0]

    big = [
        ("mix_w_in", mix_w_in, m_mix_w_in, v_mix_w_in, True),
        ("mix_w_out", mix_w_out, m_mix_w_out, v_mix_w_out, False),
        ("attn_w_qkv", attn_w_qkv, m_attn_w_qkv, v_attn_w_qkv, True),
        ("attn_w_o", attn_w_o, m_attn_w_o, v_attn_w_o, False),
        ("ffn_w_up", ffn_w_up, m_ffn_w_up, v_ffn_w_up, True),
        ("ffn_w_down", ffn_w_down, m_ffn_w_down, v_ffn_w_down, False),
        ("ple_w_proj", ple_w_proj, m_ple_w_proj, v_ple_w_proj, True),
        ("ple_w_gate", ple_w_gate, m_ple_w_gate, v_ple_w_gate, False),
    ]
    keys = []
    shards = []
    for nm, w, _, _, _ in big:
        for layer in range(w.shape[0]):
            keys.append((nm, layer))
            shards.append(cast_bf16(f"cast_{nm}_{layer}", w, layer))
    gathered = dict(zip(keys, gather_weights(shards)))
    col_sharded = {nm: cs for nm, _, _, _, cs in big}

    def weight(nm, layer):
        g = gathered[(nm, layer)]
        if col_sharded[nm]:
            return g
        return g.reshape(g.shape[0] * g.shape[1], g.shape[2])

    dw_shapes = [conv_dw_w.shape, ffn_dw_w.shape]
    dw_packed = _pack([conv_dw_w, ffn_dw_w])
    dw_rows = dw_packed.shape[0]
    dw_all = gather_small("gather_dw", dw_packed)
    dw_parts = [_unpack(dw_all[2 * k * dw_rows:(2 * k + 1) * dw_rows], dw_shapes) for k in range(N_SHARD)]
    conv_w_full = jnp.concatenate([pc[0] for pc in dw_parts], axis=2)[0]
    ffn_dw_full = jnp.concatenate([pc[1] for pc in dw_parts], axis=2)

    big_grads = {}
    small_grads = {}

    saved = []
    h_in = x0
    for layer in range(N_LAYERS):
        sv = {"x_in": h_in}
        if layer % 2 == 0:
            u = mm_cols_fwd("mix_in", h_in, weight("mix_w_in", 0), F32)
            cat, d_sv, e_sv, glu_sv, hh_sv, rs_sv = mixer_fwd(
                "mixer_fwd", u, pool_w[0], pool_scale, conv_w_full, conv_dw_b, conv_ln_g, conv_ln_b)
            mix = mm_rows_fwd("mix_out", cat, weight("mix_w_out", 0))
            sv.update(u=u, cat=cat, d=d_sv, e=e_sv, glu=glu_sv, hh=hh_sv, rs=rs_sv)
        else:
            qkvp = mm_cols_fwd("attn_qkv", h_in, weight("attn_w_qkv", 0), BF16,
                               pad_blocks=PAD_ROWS // _row_tile(seq))
            bias = _toeplitz_bias(attn_rel_bias[0])
            att = attn_fwd("attn_fwd", qkvp, bias)
            mix = mm_rows_fwd("attn_out", att, weight("attn_w_o", 0))
            sv.update(qkvp=qkvp, bias=bias, att=att)
        x1, xh1, rs1 = ln_fwd(f"ln_mix_{layer}", h_in, mix, ln_mix_g[layer:layer + 1], ln_mix_b[layer:layer + 1])
        gv = mm_cols_fwd(f"ffn_up_{layer}", x1, weight("ffn_w_up", layer), F32)
        hid = ffn_act_fwd(f"ffn_act_{layer}", gv, ffn_dw_full[layer], ffn_dw_b[layer:layer + 1])
        ffn = mm_rows_fwd(f"ffn_down_{layer}", hid, weight("ffn_w_down", layer))
        pgl = mm_rows_fwd(f"ple_gate_{layer}", x1, weight("ple_w_gate", layer))
        pp = mm_cols_fwd(f"ple_proj_{layer}", p[layer, 0], weight("ple_w_proj", layer), F32)
        bg = ple_b_gate[layer:layer + 1]
        x2, xh2, rs2 = ln_fwd(f"ln_ffn_{layer}", x1, ffn, ln_ffn_g[layer:layer + 1], ln_ffn_b[layer:layer + 1],
                              ple=(pgl, pp, bg))
        sv.update(x1=x1, xh1=xh1, rs1=rs1, gv=gv, hid=hid, pgl=pgl, pp=pp, xh2=xh2, rs2=rs2)
        saved.append(sv)
        h_in = x2

    dy, loss_part = loss_fwd_bwd("loss", h_in, target)

    parts = [(1.0, dy)]
    for layer in reversed(range(N_LAYERS)):
        sv = saved[layer]
        bg = ple_b_gate[layer:layer + 1]
        dz2, dg2, db2, dpp, dpgl, dbg = ln_bwd(
            f"ln_ffn_bwd_{layer}", parts, sv["xh2"], sv["rs2"], ln_ffn_g[layer:layer + 1],
            ple=(sv["pgl"], sv["pp"], bg))
        small_grads[("ln_ffn_g", layer)] = dg2
        small_grads[("ln_ffn_b", layer)] = db2
        small_grads[("ple_b_gate", layer)] = dbg
        w_down = weight("ffn_w_down", layer)
        dhid = mm_rows_dx(f"ffn_down_dx_{layer}", dz2, w_down)
        big_grads[("ffn_w_down", layer)] = mm_rows_dw(f"ffn_down_dw_{layer}", sv["hid"], dz2)
        dgv, ddw, ddb = ffn_act_bwd(f"ffn_act_bwd_{layer}", dhid, sv["gv"], ffn_dw_full[layer],
                                    ffn_dw_b[layer:layer + 1])
        small_grads[("ffn_dw_w", layer)] = ddw
        small_grads[("ffn_dw_b", layer)] = ddb
        big_grads[("ffn_w_up", layer)] = mm_cols_dw(f"ffn_up_dw_{layer}", sv["x1"], dgv)
        t_up = mm_cols_dx(f"ffn_up_dx_{layer}", dgv, weight("ffn_w_up", layer))
        big_grads[("ple_w_gate", layer)] = mm_rows_dw(f"ple_gate_dw_{layer}", sv["x1"], dpgl)
        t_gate = mm_rows_dx(f"ple_gate_dx_{layer}", dpgl, weight("ple_w_gate", layer))
        big_grads[("ple_w_proj", layer)] = mm_cols_dw(f"ple_proj_dw_{layer}", p[layer, 0], dpp)
        dz1, dg1, db1 = ln_bwd(
            f"ln_mix_bwd_{layer}", [(ALPHA, dz2), (1.0, t_up), (1.0, t_gate)], sv["xh1"], sv["rs1"],
            ln_mix_g[layer:layer + 1])
        small_grads[("ln_mix_g", layer)] = dg1
        small_grads[("ln_mix_b", layer)] = db1
        if layer % 2 == 0:
            dcat = mm_rows_dx("mix_out_dx", dz1, weight("mix_w_out", 0))
            big_grads[("mix_w_out", 0)] = mm_rows_dw("mix_out_dw", sv["cat"], dz1)
            du, dpw, dps, dcw, dcb, dcg, dcbt = mixer_bwd(
                "mixer_bwd", dcat, sv["u"], sv["d"], sv["e"], sv["glu"], sv["hh"], sv["rs"],
                pool_w[0], pool_scale, conv_w_full, conv_ln_g, conv_ln_b)
            small_grads[("pool_w", 0)] = dpw
            small_grads[("pool_scale", 0)] = dps
            small_grads[("conv_dw_w", 0)] = dcw
            small_grads[("conv_dw_b", 0)] = dcb
            small_grads[("conv_ln_g", 0)] = dcg
            small_grads[("conv_ln_b", 0)] = dcbt
            big_grads[("mix_w_in", 0)] = mm_cols_dw("mix_in_dw", sv["x_in"], du)
            t_mix = mm_cols_dx("mix_in_dx", du, weight("mix_w_in", 0))
        else:
            do = mm_rows_dx("attn_out_dx", dz1, weight("attn_w_o", 0), out_dtype=BF16)
            big_grads[("attn_w_o", 0)] = mm_rows_dw("attn_out_dw", sv["att"], dz1)
            dq, dk, dv, ds_sum = attn_bwd("attn_bwd", sv["qkvp"], sv["bias"], do)
            cols, sat = bias_grad_reduce("bias_grad", _shear_for_bias_grad(ds_sum))
            width = Q_TILE + K_WIN
            d_rel = jnp.concatenate(
                [jnp.zeros((N_HEADS, 1), F32),
                 jnp.flip(cols[:, 0, width - 2 * MAX_REL:width - 1], axis=1),
                 sat[:, 0, 0:1]], axis=1)
            small_grads[("attn_rel_bias", 0)] = d_rel
            dqkv = jnp.concatenate([dq, dk, dv], axis=1)
            big_grads[("attn_w_qkv", 0)] = mm_cols_dw("attn_qkv_dw", sv["x_in"], dqkv)
            t_mix = mm_cols_dx("attn_qkv_dx", dqkv, weight("attn_w_qkv", 0))
        parts = [(ALPHA, dz1), (1.0, t_mix)]
    grad_x = scaled_sum("grad_x", parts)

    g_list = []
    for key in keys:
        g = big_grads[key]
        if g.ndim == 2:
            g = g.reshape(N_SHARD, g.shape[0] // N_SHARD, g.shape[1])
        g_list.append(g)
    landed = swap_halves(g_list)
    own_f32 = []
    wire = []
    for key, g, ld in zip(keys, g_list, landed):
        o, ob = add_halves(f"add_halves_{key[0]}_{key[1]}", g, ld, c_arr)
        own_f32.append(o)
        wire.append(ob)
    arrived = send_to_owners(wire)
    finals = [add_owned(f"add_owned_{key[0]}_{key[1]}", o, ar, s_arr) for key, o, ar in zip(keys, own_f32, arrived)]
    shard_grads = dict(zip(keys, join_halves(finals)))

    big_out = {}
    for nm, w, m, v, _ in big:
        gl = [shard_grads[(nm, layer)] for layer in range(w.shape[0])]
        delta, new_m, new_v = adamw(f"adamw_{nm}", w, gl, m, v)
        big_out[nm] = (jnp.stack(gl, axis=0), delta, new_m, new_v)

    small = [
        ("pool_w", pool_w, m_pool_w, v_pool_w, None),
        ("pool_scale", pool_scale, m_pool_scale, v_pool_scale, None),
        ("conv_dw_w", conv_dw_w, m_conv_dw_w, v_conv_dw_w, 2),
        ("conv_dw_b", conv_dw_b, m_conv_dw_b, v_conv_dw_b, None),
        ("conv_ln_g", conv_ln_g, m_conv_ln_g, v_conv_ln_g, None),
        ("conv_ln_b", conv_ln_b, m_conv_ln_b, v_conv_ln_b, None),
        ("attn_rel_bias", attn_rel_bias, m_attn_rel_bias, v_attn_rel_bias, None),
        ("ln_mix_g", ln_mix_g, m_ln_mix_g, v_ln_mix_g, None),
        ("ln_mix_b", ln_mix_b, m_ln_mix_b, v_ln_mix_b, None),
        ("ffn_dw_w", ffn_dw_w, m_ffn_dw_w, v_ffn_dw_w, 2),
        ("ffn_dw_b", ffn_dw_b, m_ffn_dw_b, v_ffn_dw_b, None),
        ("ple_b_gate", ple_b_gate, m_ple_b_gate, v_ple_b_gate, None),
        ("ln_ffn_g", ln_ffn_g, m_ln_ffn_g, v_ln_ffn_g, None),
        ("ln_ffn_b", ln_ffn_b, m_ln_ffn_b, v_ln_ffn_b, None),
    ]
    full_grads = []
    for nm, w, _, _, shard_axis in small:
        full = list(w.shape)
        if shard_axis is not None:
            full[shard_axis] *= N_SHARD
        per_layer = [small_grads[(nm, layer)].reshape((1,) + tuple(full[1:])) for layer in range(w.shape[0])]
        full_grads.append(jnp.concatenate(per_layer, axis=0))
    packed = _pack(full_grads + [loss_part])
    total = sum_blocks("sum_small", gather_small("gather_small_grads", packed), 8)
    unpacked = _unpack(total, [g.shape for g in full_grads] + [loss_part.shape])
    loss = unpacked[-1][0, 0]
    local_grads = []
    for (nm, w, _, _, shard_axis), g in zip(small, unpacked[:-1]):
        if shard_axis is not None:
            width = w.shape[shard_axis]
            g = lax.dynamic_slice_in_dim(g, shard_idx * width, width, axis=shard_axis)
        local_grads.append(g.reshape(w.shape))
    shapes = [w.shape for _, w, _, _, _ in small]
    pg = _pack(local_grads)
    pw = _pack([w for _, w, _, _, _ in small])
    pm = _pack([m for _, _, m, _, _ in small])
    pv = _pack([v for _, _, _, v, _ in small])
    delta_s, new_m_s, new_v_s = adamw("adamw_small", pw[None], [pg], pm[None], pv[None])
    small_out = {}
    for (nm, _, _, _, _), g, d_, m_, v_ in zip(
            small, local_grads, _unpack(delta_s[0], shapes), _unpack(new_m_s[0], shapes), _unpack(new_v_s[0], shapes)):
        small_out[nm] = (g, d_, m_, v_)

    order = ["mix_w_in", "pool_w", "pool_scale", "conv_dw_w", "conv_dw_b", "conv_ln_g", "conv_ln_b", "mix_w_out",
             "attn_w_qkv", "attn_rel_bias", "attn_w_o", "ln_mix_g", "ln_mix_b", "ffn_w_up", "ffn_dw_w", "ffn_dw_b",
             "ffn_w_down", "ple_w_proj", "ple_w_gate", "ple_b_gate", "ln_ffn_g", "ln_ffn_b"]
    res = {**big_out, **small_out}
    outs = [loss, grad_x[None]]
    for slot in range(4):
        outs += [res[nm][slot] for nm in order]
    return tuple(outs)
```

```python
import functools
import math

import jax
import jax.numpy as jnp
from jax import lax
from jax.experimental import pallas as pl
from jax.experimental.pallas import tpu as pltpu

F32 = jnp.float32
BF16 = jnp.bfloat16
MESH = pl.DeviceIdType.MESH

N_LAYERS = 2
ALPHA = (2 * N_LAYERS) ** 0.25
LN_EPS = 1e-5
NEG_INF = -1e30
CHUNK = 64
LEFT_CHUNKS = 8
PAD_ROWS = LEFT_CHUNKS * CHUNK
HEAD_DIM = 64
N_HEADS = 16
MAX_REL = 256
POOL_WINDOWS = (2, 4, 8, 16)
POOL_GROUP = 128
CONV_K = 31
FFN_K = 3
CONV_HALO = 32
FFN_HALO = 8
Q_TILE = 256
K_WIN = Q_TILE + PAD_ROWS
SHEAR_W = Q_TILE + K_WIN
SHEAR_SAT = SHEAR_W - 2 * MAX_REL
N_SHARD = 4
LANES = 128

ADAM_LR = 0.001
ADAM_B1 = 0.9
ADAM_B2 = 0.999
ADAM_EPS = 1e-08
ADAM_WD = 0.01
ADAM_STEP = 10
ADAM_BC1 = 1.0 - ADAM_B1 ** ADAM_STEP
ADAM_BC2 = 1.0 - ADAM_B2 ** ADAM_STEP

DIMS = {
    "nn": (((1,), (0,)), ((), ())),
    "nt": (((1,), (1,)), ((), ())),
    "tn": (((0,), (0,)), ((), ())),
}


def _cp(vmem_mb=48, **kw):
    return pltpu.CompilerParams(vmem_limit_bytes=vmem_mb * 1024 * 1024, **kw)


def _dot(a, b, mode):
    return lax.dot_general(a.astype(BF16), b.astype(BF16), DIMS[mode], preferred_element_type=F32)


def _sig(x):
    return 1.0 / (1.0 + jnp.exp(-x))


def _row_tile(s):
    return min(512, s // 4)


def _mm(name, mode, a, b, in_specs, out_shape, out_spec, acc_shape, grid, nk, zero_first=False, vmem_mb=48):
    out_f32 = out_shape.dtype == F32

    def body(a_ref, b_ref, o_ref, *scr):
        k = pl.program_id(2)

        def compute():
            part = _dot(a_ref[...], b_ref[...], mode)
            if nk == 1:
                o_ref[...] = part.astype(o_ref.dtype)
                return
            acc = o_ref if out_f32 else scr[0]

            @pl.when(k == 0)
            def _():
                acc[...] = part

            @pl.when(k > 0)
            def _():
                acc[...] += part

            if not out_f32:
                @pl.when(k == nk - 1)
                def _():
                    o_ref[...] = acc[...].astype(o_ref.dtype)

        if zero_first:
            @pl.when(pl.program_id(1) == 0)
            def _():
                o_ref[...] = jnp.zeros(o_ref.shape, o_ref.dtype)

            pl.when(pl.program_id(1) > 0)(compute)
        else:
            compute()

    scratch = [] if (nk == 1 or out_f32) else [pltpu.VMEM(acc_shape, F32)]
    return pl.pallas_call(
        body, name=name, grid=grid, in_specs=in_specs, out_specs=out_spec, out_shape=out_shape,
        scratch_shapes=scratch, compiler_params=_cp(vmem_mb),
    )(a, b)


def mm_cols_fwd(name, a, wc, out_dtype, pad_blocks=0):
    s, k = a.shape
    n4 = wc.shape[2]
    tm = _row_tile(s)
    nt = s // tm
    return _mm(
        name, "nn", a, wc,
        [pl.BlockSpec((tm, k), lambda j, i, r: (jnp.maximum(i - pad_blocks, 0), 0)),
         pl.BlockSpec((None, k, n4), lambda j, i, r: (j, 0, 0))],
        jax.ShapeDtypeStruct((s + pad_blocks * tm, N_SHARD * n4), out_dtype),
        pl.BlockSpec((tm, n4), lambda j, i, r: (i, j)),
        None, (N_SHARD, nt + pad_blocks, 1), 1, zero_first=pad_blocks > 0)


def mm_cols_dx(name, dy, wc):
    s = dy.shape[0]
    _, k, n4 = wc.shape
    tm = _row_tile(s)
    return _mm(
        name, "nt", dy, wc,
        [pl.BlockSpec((tm, n4), lambda g, i, r: (i, r)),
         pl.BlockSpec((None, k, n4), lambda g, i, r: (r, 0, 0))],
        jax.ShapeDtypeStruct((s, k), F32),
        pl.BlockSpec((tm, k), lambda g, i, r: (i, 0)),
        (tm, k), (1, s // tm, N_SHARD), N_SHARD)


def mm_cols_dw(name, a, dy):
    s, k = a.shape
    n4 = dy.shape[1] // N_SHARD
    tm = _row_tile(s)
    nt = s // tm
    return _mm(
        name, "tn", a, dy,
        [pl.BlockSpec((tm, k), lambda j, g, r: (r, 0)),
         pl.BlockSpec((tm, n4), lambda j, g, r: (r, j))],
        jax.ShapeDtypeStruct((N_SHARD, k, n4), F32),
        pl.BlockSpec((None, k, n4), lambda j, g, r: (j, 0, 0)),
        (k, n4), (N_SHARD, 1, nt), nt)


def _k_tile(k):
    return k if k <= 1024 else k // 2


def mm_rows_fwd(name, a, wr, out_dtype=F32):
    s, k = a.shape
    n = wr.shape[1]
    tm = _row_tile(s)
    tk = _k_tile(k)
    nk = k // tk
    return _mm(
        name, "nn", a, wr,
        [pl.BlockSpec((tm, tk), lambda g, i, r: (i, r)),
         pl.BlockSpec((tk, n), lambda g, i, r: (r, 0))],
        jax.ShapeDtypeStruct((s, n), out_dtype),
        pl.BlockSpec((tm, n), lambda g, i, r: (i, 0)),
        (tm, n), (1, s // tm, nk), nk)


def mm_rows_dx(name, dy, wr, out_dtype=F32):
    s, n = dy.shape
    k = wr.shape[0]
    tm = _row_tile(s)
    tk = _k_tile(k)
    return _mm(
        name, "nt", dy, wr,
        [pl.BlockSpec((tm, n), lambda j, i, r: (i, 0)),
         pl.BlockSpec((tk, n), lambda j, i, r: (j, 0))],
        jax.ShapeDtypeStruct((s, k), out_dtype),
        pl.BlockSpec((tm, tk), lambda j, i, r: (i, j)),
        None, (k // tk, s // tm, 1), 1)


def mm_rows_dw(name, a, dy):
    s, k = a.shape
    n = dy.shape[1]
    tm = _row_tile(s)
    tk = _k_tile(k)
    nt = s // tm
    return _mm(
        name, "tn", a, dy,
        [pl.BlockSpec((tm, tk), lambda j, g, r: (r, j)),
         pl.BlockSpec((tm, n), lambda j, g, r: (r, 0))],
        jax.ShapeDtypeStruct((k, n), F32),
        pl.BlockSpec((tk, n), lambda j, g, r: (j, 0)),
        (tk, n), (k // tk, 1, nt), nt)


def _row(tm, c, col=0):
    return pl.BlockSpec((tm, c), lambda i: (i, col))


def _full(shape):
    nd = len(shape)
    return pl.BlockSpec(shape, lambda i: (0,) * nd)


def _prev(tm, h, c, col=0):
    return pl.BlockSpec((h, c), lambda i: (jnp.maximum(i * (tm // h) - 1, 0), col))


def _next(tm, h, c, s, col=0):
    return pl.BlockSpec((h, c), lambda i: (jnp.minimum((i + 1) * (tm // h), s // h - 1), col))


def _acc_add(ref, first, val):
    @pl.when(first)
    def _():
        ref[...] = val

    @pl.when(jnp.logical_not(first))
    def _():
        ref[...] += val


def _colsum(v):
    return jnp.sum(v, axis=0, keepdims=True)


def _ln_stats(z):
    mu = jnp.mean(z, axis=-1, keepdims=True)
    zc = z - mu
    var = jnp.mean(zc * zc, axis=-1, keepdims=True)
    rstd = lax.rsqrt(var + LN_EPS)
    return zc * rstd, rstd


def _ln_bwd(dxhat, xhat, rstd):
    m1 = jnp.mean(dxhat, axis=-1, keepdims=True)
    m2 = jnp.mean(dxhat * xhat, axis=-1, keepdims=True)
    return rstd * (dxhat - m1 - xhat * m2)


def ln_fwd(name, x, f, g, b, ple=None):
    s, d = x.shape
    tm = _row_tile(s)
    n_in = 2 + (3 if ple is not None else 0)

    def body(*refs):
        x_ref, f_ref = refs[0], refs[1]
        g_ref, b_ref = refs[n_in], refs[n_in + 1]
        y_ref, xh_ref, rs_ref = refs[n_in + 2:]
        z = ALPHA * x_ref[...] + f_ref[...]
        if ple is not None:
            pgl_ref, pp_ref, bg_ref = refs[2:5]
            z = z + _sig(pgl_ref[...] + bg_ref[...]) * pp_ref[...]
        xhat, rstd = _ln_stats(z)
        y_ref[...] = xhat * g_ref[...] + b_ref[...]
        xh_ref[...] = xhat
        rs_ref[...] = jnp.broadcast_to(rstd, rs_ref.shape)

    ins = [x, f]
    specs = [_row(tm, d), _row(tm, d)]
    if ple is not None:
        pgl, pp, bg = ple
        ins += [pgl, pp, bg]
        specs += [_row(tm, d), _row(tm, d), _full((1, d))]
    ins += [g, b]
    specs += [_full((1, d)), _full((1, d))]
    return pl.pallas_call(
        body, name=name, grid=(s // tm,), in_specs=specs,
        out_specs=[_row(tm, d), _row(tm, d), _row(tm, LANES)],
        out_shape=[jax.ShapeDtypeStruct((s, d), F32), jax.ShapeDtypeStruct((s, d), F32),
                   jax.ShapeDtypeStruct((s, LANES), F32)],
        compiler_params=_cp(),
    )(*ins)


def ln_bwd(name, parts, xhat, rstd, g, ple=None):
    s, d = xhat.shape
    tm = _row_tile(s)
    coefs = [c for c, _ in parts]
    n_p = len(parts)
    n_in = n_p + 3 + (3 if ple is not None else 0)

    def body(*refs):
        first = pl.program_id(0) == 0
        dy = coefs[0] * refs[0][...].astype(F32)
        for j in range(1, n_p):
            dy = dy + coefs[j] * refs[j][...].astype(F32)
        xh = refs[n_p][...]
        rs = refs[n_p + 1][:, 0:1]
        g_v = refs[n_p + 2][...]
        outs = refs[n_in:]
        dz = _ln_bwd(dy * g_v, xh, rs)
        outs[0][...] = dz
        _acc_add(outs[1], first, _colsum(dy * xh))
        _acc_add(outs[2], first, _colsum(dy))
        if ple is not None:
            pgl_ref, pp_ref, bg_ref = refs[n_p + 3:n_p + 6]
            pg = _sig(pgl_ref[...] + bg_ref[...])
            dpgl = dz * pp_ref[...] * pg * (1.0 - pg)
            outs[3][...] = (dz * pg).astype(BF16)
            outs[4][...] = dpgl.astype(BF16)
            _acc_add(outs[5], first, _colsum(dpgl))

    ins = [p for _, p in parts] + [xhat, rstd, g]
    specs = [_row(tm, d)] * n_p + [_row(tm, d), _row(tm, LANES), _full((1, d))]
    out_specs = [_row(tm, d), _full((1, d)), _full((1, d))]
    out_shape = [jax.ShapeDtypeStruct((s, d), F32), jax.ShapeDtypeStruct((1, d), F32),
                 jax.ShapeDtypeStruct((1, d), F32)]
    if ple is not None:
        pgl, pp, bg = ple
        ins += [pgl, pp, bg]
        specs += [_row(tm, d), _row(tm, d), _full((1, d))]
        out_specs += [_row(tm, d), _row(tm, d), _full((1, d))]
        out_shape += [jax.ShapeDtypeStruct((s, d), BF16), jax.ShapeDtypeStruct((s, d), BF16),
                      jax.ShapeDtypeStruct((1, d), F32)]
    return pl.pallas_call(
        body, name=name, grid=(s // tm,), in_specs=specs, out_specs=out_specs, out_shape=out_shape,
        compiler_params=_cp(),
    )(*ins)


def loss_fwd_bwd(name, y, target):
    s, d = y.shape
    tm = _row_tile(s)

    def body(y_ref, t_ref, dy_ref, l_ref):
        first = pl.program_id(0) == 0
        err = y_ref[...] - t_ref[...]
        dy_ref[...] = err * (1.0 / d)
        part = 0.5 * jnp.sum(jnp.mean(err * err, axis=-1, keepdims=True), axis=0, keepdims=True)
        _acc_add(l_ref, first, jnp.broadcast_to(part, l_ref.shape))

    return pl.pallas_call(
        body, name=name, grid=(s // tm,), in_specs=[_row(tm, d), _row(tm, d)],
        out_specs=[_row(tm, d), _full((8, LANES))],
        out_shape=[jax.ShapeDtypeStruct((s, d), F32), jax.ShapeDtypeStruct((8, LANES), F32)],
        compiler_params=_cp(),
    )(y, target)


def scaled_sum(name, parts):
    s, d = parts[0][1].shape
    tm = _row_tile(s)
    coefs = [c for c, _ in parts]

    def body(*refs):
        acc = coefs[0] * refs[0][...].astype(F32)
        for j in range(1, len(coefs)):
            acc = acc + coefs[j] * refs[j][...].astype(F32)
        refs[-1][...] = acc

    return pl.pallas_call(
        body, name=name, grid=(s // tm,), in_specs=[_row(tm, d)] * len(parts), out_specs=_row(tm, d),
        out_shape=jax.ShapeDtypeStruct((s, d), F32), compiler_params=_cp(),
    )(*[p for _, p in parts])


def _tile_pos(i, tm, rows):
    return (i * tm + lax.broadcasted_iota(jnp.int32, (rows, 1), 0) + 1).astype(F32)


def mixer_fwd(name, u, pool_w, pool_scale, conv_w, conv_b, cn_g, cn_b):
    s = u.shape[0]
    dp = 512
    tm = min(256, s // 4)
    h = CONV_HALO

    def body(a_c, a_p, bv_c, bv_p, bg_c, bg_p, pw_ref, ps_ref, cw_ref, cb_ref, cg_ref, cbt_ref,
             cat_ref, d_ref, e_ref, glu_ref, hh_ref, rs_ref, ext_a, ext_g):
        i = pl.program_id(0)
        first = i == 0
        ext_a[0:h, :] = jnp.where(first, 0.0, a_p[...])
        ext_a[h:, :] = a_c[...]
        ext_g[0:h, :] = jnp.where(first, 0.0, bv_p[...] * _sig(bg_p[...]))
        glu = bv_c[...] * _sig(bg_c[...])
        ext_g[h:, :] = glu
        glu_ref[...] = glu
        pos = _tile_pos(i, tm, tm)
        for gi, w in enumerate(POOL_WINDOWS):
            cs = slice(gi * POOL_GROUP, (gi + 1) * POOL_GROUP)
            a_g = ext_a[pl.ds(h, tm), cs]
            acc = a_g
            for sh in range(1, w):
                acc = acc + ext_a[pl.ds(h - sh, tm), cs]
            d_g = acc / jnp.minimum(pos, float(w)) - a_g
            d_ref[:, cs] = d_g.astype(BF16)
            e_g = _dot(d_g, pw_ref[gi], "nn")
            e_ref[:, cs] = e_g
            cat_ref[:, cs] = (e_g * ps_ref[:, cs]).astype(BF16)
        hcv = jnp.broadcast_to(cb_ref[...], (tm, dp))
        for sh in range(CONV_K):
            hcv = hcv + ext_g[pl.ds(h - sh, tm), :] * cw_ref[pl.ds(CONV_K - 1 - sh, 1), :]
        hhat, rstd = _ln_stats(hcv)
        hl = hhat * cg_ref[...] + cbt_ref[...]
        cat_ref[:, dp:] = (hl * _sig(hl)).astype(BF16)
        hh_ref[...] = hhat
        rs_ref[...] = jnp.broadcast_to(rstd, rs_ref.shape)

    specs = [_row(tm, dp, 0), _prev(tm, h, dp, 0), _row(tm, dp, 1), _prev(tm, h, dp, 1),
             _row(tm, dp, 2), _prev(tm, h, dp, 2),
             _full((4, POOL_GROUP, POOL_GROUP)), _full((1, dp)), _full((CONV_K, dp)),
             _full((1, dp)), _full((1, dp)), _full((1, dp))]
    out_specs = [_row(tm, 2 * dp), _row(tm, dp), _row(tm, dp), _row(tm, dp), _row(tm, dp), _row(tm, LANES)]
    out_shape = [jax.ShapeDtypeStruct((s, 2 * dp), BF16), jax.ShapeDtypeStruct((s, dp), BF16),
                 jax.ShapeDtypeStruct((s, dp), F32), jax.ShapeDtypeStruct((s, dp), F32),
                 jax.ShapeDtypeStruct((s, dp), F32), jax.ShapeDtypeStruct((s, LANES), F32)]
    return pl.pallas_call(
        body, name=name, grid=(s // tm,), in_specs=specs, out_specs=out_specs, out_shape=out_shape,
        scratch_shapes=[pltpu.VMEM((h + tm, dp), F32), pltpu.VMEM((h + tm, dp), F32)],
        compiler_params=_cp(),
    )(u, u, u, u, u, u, pool_w, pool_scale, conv_w, conv_b, cn_g, cn_b)


def mixer_bwd(name, dcat, u, d_sv, e_sv, glu_sv, hh_sv, rs_sv, pool_w, pool_scale, conv_w, cn_g, cn_b):
    s = u.shape[0]
    dp = 512
    tm = min(256, s // 4)
    h = CONV_HALO
    nt = s // tm

    def body(dc_c, dc_n, bv_c, bg_c, d_c, e_c, gl_c, gl_p, hh_c, hh_n, rs_c, rs_n,
             pw_ref, ps_ref, cw_ref, cg_ref, cbt_ref,
             du_ref, dpw_ref, dps_ref, dcw_ref, dcb_ref, dcg_ref, dcbt_ref,
             ext_dh, ext_g, ext_r):
        i = pl.program_id(0)
        first = i == 0
        last = i == nt - 1
        cg = cg_ref[...]

        def conv_grads(dyb, hhat, rstd):
            hl = hhat * cg + cbt_ref[...]
            sg = _sig(hl)
            dhl = dyb * (sg * (1.0 + hl * (1.0 - sg)))
            return _ln_bwd(dhl * cg, hhat, rstd), dhl

        hh_cur = hh_c[...]
        dh_c, dhl_c = conv_grads(dc_c[:, dp:], hh_cur, rs_c[:, 0:1])
        dh_n, _ = conv_grads(dc_n[:, dp:], hh_n[...], rs_n[:, 0:1])
        ext_dh[0:tm, :] = dh_c
        ext_dh[tm:, :] = jnp.where(last, 0.0, dh_n)
        ext_g[0:h, :] = jnp.where(first, 0.0, gl_p[...])
        ext_g[h:, :] = gl_c[...]
        dglu = jnp.zeros((tm, dp), F32)
        for sh in range(CONV_K):
            dglu = dglu + ext_dh[pl.ds(sh, tm), :] * cw_ref[pl.ds(CONV_K - 1 - sh, 1), :]

        @pl.when(first)
        def _():
            dcw_ref[...] = jnp.zeros(dcw_ref.shape, F32)

        for sh in range(CONV_K):
            dcw_ref[pl.ds(CONV_K - 1 - sh, 1), :] += _colsum(dh_c * ext_g[pl.ds(h - sh, tm), :])
        _acc_add(dcb_ref, first, _colsum(dh_c))
        _acc_add(dcg_ref, first, _colsum(dhl_c * hh_cur))
        _acc_add(dcbt_ref, first, _colsum(dhl_c))
        sgate = _sig(bg_c[...])
        bv = bv_c[...]
        du_ref[:, dp:2 * dp] = dglu * sgate
        du_ref[:, 2 * dp:] = dglu * bv * sgate * (1.0 - sgate)

        pos_c = _tile_pos(i, tm, tm)
        pos_n = _tile_pos(i + 1, tm, h)
        _acc_add(dps_ref, first, _colsum(dc_c[:, :dp] * e_c[...]))
        for gi, w in enumerate(POOL_WINDOWS):
            cs = slice(gi * POOL_GROUP, (gi + 1) * POOL_GROUP)
            pw = pw_ref[gi]
            de_c = dc_c[:, cs] * ps_ref[:, cs]
            de_n = dc_n[:, cs] * ps_ref[:, cs]
            dd_c = _dot(de_c, pw, "nt")
            dd_n = _dot(de_n, pw, "nt")
            ext_r[0:tm, :] = dd_c / jnp.minimum(pos_c, float(w))
            ext_r[tm:, :] = jnp.where(last, 0.0, dd_n / jnp.minimum(pos_n, float(w)))
            acc = -dd_c
            for sh in range(w):
                acc = acc + ext_r[pl.ds(sh, tm), :]
            du_ref[:, cs] = acc
            dpw_g = _dot(d_c[:, cs], de_c, "tn")

            @pl.when(first)
            def _():
                dpw_ref[gi] = dpw_g

            @pl.when(jnp.logical_not(first))
            def _():
                dpw_ref[gi] += dpw_g

    specs = [_row(tm, 2 * dp), _next(tm, h, 2 * dp, s), _row(tm, dp, 1), _row(tm, dp, 2),
             _row(tm, dp), _row(tm, dp), _row(tm, dp), _prev(tm, h, dp),
             _row(tm, dp), _next(tm, h, dp, s), _row(tm, LANES), _next(tm, h, LANES, s),
             _full((4, POOL_GROUP, POOL_GROUP)), _full((1, dp)), _full((CONV_K, dp)),
             _full((1, dp)), _full((1, dp))]
    out_specs = [_row(tm, 3 * dp), _full((4, POOL_GROUP, POOL_GROUP)), _full((1, dp)), _full((CONV_K, dp)),
                 _full((1, dp)), _full((1, dp)), _full((1, dp))]
    out_shape = [jax.ShapeDtypeStruct((s, 3 * dp), F32),
                 jax.ShapeDtypeStruct((4, POOL_GROUP, POOL_GROUP), F32), jax.ShapeDtypeStruct((1, dp), F32),
                 jax.ShapeDtypeStruct((CONV_K, dp), F32), jax.ShapeDtypeStruct((1, dp), F32),
                 jax.ShapeDtypeStruct((1, dp), F32), jax.ShapeDtypeStruct((1, dp), F32)]
    return pl.pallas_call(
        body, name=name, grid=(nt,), in_specs=specs, out_specs=out_specs, out_shape=out_shape,
        scratch_shapes=[pltpu.VMEM((tm + h, dp), F32), pltpu.VMEM((h + tm, dp), F32),
                        pltpu.VMEM((tm + h, POOL_GROUP), F32)],
        compiler_params=_cp(),
    )(dcat, dcat, u, u, d_sv, e_sv, glu_sv, glu_sv, hh_sv, hh_sv, rs_sv, rs_sv,
      pool_w, pool_scale, conv_w, cn_g, cn_b)


GELU_C = math.sqrt(2.0 / math.pi)


def _gelu_parts(x):
    inner = GELU_C * (x + 0.044715 * x * x * x)
    t = jnp.tanh(inner)
    gelu = 0.5 * x * (1.0 + t)
    dgelu = 0.5 * (1.0 + t) + 0.5 * x * (1.0 - t * t) * GELU_C * (1.0 + 3.0 * 0.044715 * x * x)
    return gelu, dgelu


def ffn_act_fwd(name, gv, dw_w, dw_b):
    s = gv.shape[0]
    dff = gv.shape[1] // 2
    tm = min(128, s // 4)
    h = FFN_HALO

    def body(g_c, g_p, v_c, w_ref, b_ref, hid_ref, ext):
        first = pl.program_id(0) == 0
        ext[0:h, :] = jnp.where(first, 0.0, g_p[...])
        ext[h:, :] = g_c[...]
        gc = jnp.broadcast_to(b_ref[...], (tm, dff))
        for sh in range(FFN_K):
            gc = gc + ext[pl.ds(h - sh, tm), :] * w_ref[pl.ds(FFN_K - 1 - sh, 1), :]
        gelu, _ = _gelu_parts(gc)
        hid_ref[...] = (gelu * v_c[...]).astype(BF16)

    return pl.pallas_call(
        body, name=name, grid=(s // tm,),
        in_specs=[_row(tm, dff, 0), _prev(tm, h, dff, 0), _row(tm, dff, 1), _full((FFN_K, dff)), _full((1, dff))],
        out_specs=_row(tm, dff), out_shape=jax.ShapeDtypeStruct((s, dff), BF16),
        scratch_shapes=[pltpu.VMEM((h + tm, dff), F32)], compiler_params=_cp(),
    )(gv, gv, gv, dw_w, dw_b)


def ffn_act_bwd(name, dhid, gv, dw_w, dw_b):
    s = gv.shape[0]
    dff = gv.shape[1] // 2
    tm = min(128, s // 4)
    h = FFN_HALO
    nt = s // tm

    def body(dh_c, dh_n, g_p, g_c, g_n, v_c, v_n, w_ref, b_ref, dgv_ref, dw_ref, db_ref, ext_g, ext_d):
        i = pl.program_id(0)
        first = i == 0
        last = i == nt - 1
        ext_g[0:h, :] = jnp.where(first, 0.0, g_p[...])
        ext_g[pl.ds(h, tm), :] = g_c[...]
        ext_g[pl.ds(h + tm, h), :] = g_n[...]
        gc = jnp.broadcast_to(b_ref[...], (tm + h, dff))
        for sh in range(FFN_K):
            gc = gc + ext_g[pl.ds(h - sh, tm + h), :] * w_ref[pl.ds(FFN_K - 1 - sh, 1), :]
        gelu, dgelu = _gelu_parts(gc)
        dgc_c = dh_c[...] * v_c[...] * dgelu[0:tm]
        ext_d[0:tm, :] = dgc_c
        ext_d[tm:, :] = jnp.where(last, 0.0, dh_n[...] * v_n[...] * dgelu[tm:])
        dgate = jnp.zeros((tm, dff), F32)
        for sh in range(FFN_K):
            dgate = dgate + ext_d[pl.ds(sh, tm), :] * w_ref[pl.ds(FFN_K - 1 - sh, 1), :]
        dgv_ref[:, :dff] = dgate.astype(BF16)
        dgv_ref[:, dff:] = (dh_c[...] * gelu[0:tm]).astype(BF16)

        @pl.when(first)
        def _():
            dw_ref[...] = jnp.zeros(dw_ref.shape, F32)

        for sh in range(FFN_K):
            dw_ref[pl.ds(FFN_K - 1 - sh, 1), :] += _colsum(dgc_c * ext_g[pl.ds(h - sh, tm), :])
        _acc_add(db_ref, first, _colsum(dgc_c))

    return pl.pallas_call(
        body, name=name, grid=(nt,),
        in_specs=[_row(tm, dff), _next(tm, h, dff, s),
                  _prev(tm, h, dff, 0), _row(tm, dff, 0), _next(tm, h, dff, s, 0),
                  _row(tm, dff, 1), _next(tm, h, dff, s, 1),
                  _full((FFN_K, dff)), _full((1, dff))],
        out_specs=[_row(tm, 2 * dff), _full((FFN_K, dff)), _full((1, dff))],
        out_shape=[jax.ShapeDtypeStruct((s, 2 * dff), BF16), jax.ShapeDtypeStruct((FFN_K, dff), F32),
                   jax.ShapeDtypeStruct((1, dff), F32)],
        scratch_shapes=[pltpu.VMEM((h + tm + h, dff), F32), pltpu.VMEM((tm + h, dff), F32)],
        compiler_params=_cp(),
    )(dhid, dhid, gv, gv, gv, gv, gv, dw_w, dw_b)


def _toeplitz_bias(rel_bias):
    nh = rel_bias.shape[0]
    zero = jnp.zeros((nh, 1), rel_bias.dtype)
    line = jnp.concatenate(
        [zero, jnp.broadcast_to(rel_bias[:, 2 * MAX_REL:], (nh, SHEAR_SAT)),
         jnp.flip(rel_bias[:, 1:2 * MAX_REL], axis=1), zero], axis=1)
    z = jnp.broadcast_to(line[:, None, :], (nh, Q_TILE, SHEAR_W + 1)).reshape(nh, Q_TILE * (SHEAR_W + 1))
    return z[:, :Q_TILE * SHEAR_W].reshape(nh, Q_TILE, SHEAR_W)[:, :, Q_TILE:]


def _shear_for_bias_grad(ds_sum):
    nh = ds_sum.shape[0]
    z = jnp.pad(ds_sum, ((0, 0), (0, 0), (Q_TILE, 0))).reshape(nh, Q_TILE * SHEAR_W)
    return jnp.pad(z, ((0, 0), (0, Q_TILE))).reshape(nh, Q_TILE, SHEAR_W + 1)


def _attn_mask(t):
    row = lax.broadcasted_iota(jnp.int32, (Q_TILE, K_WIN), 0)
    col = lax.broadcasted_iota(jnp.int32, (Q_TILE, K_WIN), 1)
    qc = row // CHUNK
    kc = col // CHUNK
    return (kc >= qc) & (kc <= qc + LEFT_CHUNKS) & (t * Q_TILE + col >= PAD_ROWS)


def _attn_probs(q2, k3, bias, mask, head):
    lane = lax.broadcasted_iota(jnp.int32, q2.shape, 1)
    q_h = jnp.where(lane // HEAD_DIM == head, q2, jnp.zeros_like(q2))
    sc = _dot(q_h, k3, "nt") * (HEAD_DIM ** -0.5) + bias
    sc = jnp.where(mask, sc, NEG_INF)
    m = jnp.max(sc, axis=-1, keepdims=True)
    p = jnp.exp(sc - m)
    return q_h, p / jnp.sum(p, axis=-1, keepdims=True)


def _attn_specs(d_model):
    nq = PAD_ROWS // Q_TILE
    hp_k = d_model // LANES
    specs = [pl.BlockSpec((Q_TILE, LANES), lambda hp, t: (t + nq, hp))]
    for which in (1, 2):
        for j in range(K_WIN // Q_TILE):
            specs.append(pl.BlockSpec((Q_TILE, LANES), lambda hp, t, j=j, which=which: (t + j, which * hp_k + hp)))
    specs.append(pl.BlockSpec((2, Q_TILE, K_WIN), lambda hp, t: (hp, 0, 0)))
    return specs


def attn_fwd(name, qkvp, bias):
    s = qkvp.shape[0] - PAD_ROWS
    d_model = qkvp.shape[1] // 3
    nw = K_WIN // Q_TILE

    def body(q_ref, *refs):
        k_refs, v_refs, b_ref, o_ref = refs[:nw], refs[nw:2 * nw], refs[2 * nw], refs[2 * nw + 1]
        t = pl.program_id(1)
        q2 = q_ref[...]
        k3 = jnp.concatenate([r[...] for r in k_refs], axis=0)
        v3 = jnp.concatenate([r[...] for r in v_refs], axis=0)
        mask = _attn_mask(t)
        outs = []
        for head in range(2):
            _, p = _attn_probs(q2, k3, b_ref[head], mask, head)
            outs.append(_dot(p, v3, "nn"))
        lane = lax.broadcasted_iota(jnp.int32, (Q_TILE, LANES), 1)
        o_ref[...] = jnp.where(lane < HEAD_DIM, outs[0], outs[1]).astype(BF16)

    return pl.pallas_call(
        body, name=name, grid=(d_model // LANES, s // Q_TILE),
        in_specs=_attn_specs(d_model), out_specs=pl.BlockSpec((Q_TILE, LANES), lambda hp, t: (t, hp)),
        out_shape=jax.ShapeDtypeStruct((s, d_model), BF16), compiler_params=_cp(),
    )(qkvp, *([qkvp] * (2 * nw)), bias)


def attn_bwd(name, qkvp, bias, do):
    s = qkvp.shape[0] - PAD_ROWS
    d_model = qkvp.shape[1] // 3
    nw = K_WIN // Q_TILE
    nt = s // Q_TILE
    scale = HEAD_DIM ** -0.5

    def body(q_ref, *refs):
        k_refs, v_refs = refs[:nw], refs[nw:2 * nw]
        b_ref, do_ref, dq_ref, dk_ref, dv_ref, ds_ref, dk_acc, dv_acc = refs[2 * nw:]
        t = pl.program_id(1)
        first = t == 0

        @pl.when(first)
        def _():
            dk_acc[...] = jnp.zeros(dk_acc.shape, F32)
            dv_acc[...] = jnp.zeros(dv_acc.shape, F32)

        q2 = q_ref[...]
        do2 = do_ref[...]
        k3 = jnp.concatenate([r[...] for r in k_refs], axis=0)
        v3 = jnp.concatenate([r[...] for r in v_refs], axis=0)
        mask = _attn_mask(t)
        lane = lax.broadcasted_iota(jnp.int32, (Q_TILE, LANES), 1)
        dqs = []
        dk_win = jnp.zeros((K_WIN, LANES), F32)
        dv_win = jnp.zeros((K_WIN, LANES), F32)
        for head in range(2):
            q_h, p = _attn_probs(q2, k3, b_ref[head], mask, head)
            do_h = jnp.where(lane // HEAD_DIM == head, do2, jnp.zeros_like(do2))
            dp = _dot(do_h, v3, "nt")
            ds = p * (dp - jnp.sum(p * dp, axis=-1, keepdims=True))
            _acc_add(ds_ref.at[head], first, ds)
            dsb = (ds * scale).astype(BF16)
            dqs.append(_dot(dsb, k3, "nn"))
            dk_win = dk_win + _dot(dsb, q_h, "tn")
            dv_win = dv_win + _dot(p, do_h, "tn")
        dq_ref[...] = jnp.where(lane < HEAD_DIM, dqs[0], dqs[1]).astype(BF16)
        start = pl.multiple_of(t * Q_TILE, Q_TILE)
        dk_acc[pl.ds(start, K_WIN), :] += dk_win
        dv_acc[pl.ds(start, K_WIN), :] += dv_win

        @pl.when(t == nt - 1)
        def _():
            dk_ref[...] = dk_acc[pl.ds(PAD_ROWS, s), :].astype(BF16)
            dv_ref[...] = dv_acc[pl.ds(PAD_ROWS, s), :].astype(BF16)

    specs = _attn_specs(d_model) + [pl.BlockSpec((Q_TILE, LANES), lambda hp, t: (t, hp))]
    col_spec = pl.BlockSpec((s, LANES), lambda hp, t: (0, hp))
    return pl.pallas_call(
        body, name=name, grid=(d_model // LANES, nt), in_specs=specs,
        out_specs=[pl.BlockSpec((Q_TILE, LANES), lambda hp, t: (t, hp)), col_spec, col_spec,
                   pl.BlockSpec((2, Q_TILE, K_WIN), lambda hp, t: (hp, 0, 0))],
        out_shape=[jax.ShapeDtypeStruct((s, d_model), BF16)] * 3
        + [jax.ShapeDtypeStruct((N_HEADS, Q_TILE, K_WIN), F32)],
        scratch_shapes=[pltpu.VMEM((PAD_ROWS + s, LANES), F32), pltpu.VMEM((PAD_ROWS + s, LANES), F32)],
        compiler_params=_cp(),
    )(qkvp, *([qkvp] * (2 * nw)), bias, do)


def bias_grad_reduce(name, sheared):
    nh, _, width = sheared.shape

    def body(x_ref, col_ref, sat_ref):
        cols = _colsum(x_ref[...])
        col_ref[...] = cols
        k = lax.broadcasted_iota(jnp.int32, cols.shape, 1)
        tot = jnp.sum(jnp.where((k >= 1) & (k <= SHEAR_SAT), cols, 0.0), axis=-1, keepdims=True)
        sat_ref[...] = jnp.broadcast_to(tot, sat_ref.shape)

    return pl.pallas_call(
        body, name=name, grid=(nh,),
        in_specs=[pl.BlockSpec((None, Q_TILE, width), lambda hh: (hh, 0, 0))],
        out_specs=[pl.BlockSpec((None, 1, width), lambda hh: (hh, 0, 0)),
                   pl.BlockSpec((None, 1, LANES), lambda hh: (hh, 0, 0))],
        out_shape=[jax.ShapeDtypeStruct((nh, 1, width), F32), jax.ShapeDtypeStruct((nh, 1, LANES), F32)],
        compiler_params=_cp(),
    )(sheared)


def _ew_rows(r):
    for cand in (512, 256, 128, 64, 32, 16, 8):
        if r % cand == 0:
            return cand
    return r


def cast_into_gathered(name, w, layer, s_idx):
    r, c = w.shape[-2:]
    tr = _ew_rows(r)

    def body(s_ref, w_ref, o_ref):
        o_ref[...] = w_ref[...].astype(BF16)

    grid_spec = pltpu.PrefetchScalarGridSpec(
        num_scalar_prefetch=1, grid=(r // tr,),
        in_specs=[pl.BlockSpec((None, tr, c), lambda i, s_ref: (layer, i, 0))],
        out_specs=pl.BlockSpec((None, tr, c), lambda i, s_ref: (s_ref[0], i, 0)))
    return pl.pallas_call(
        body, name=name, grid_spec=grid_spec, out_shape=jax.ShapeDtypeStruct((N_SHARD, r, c), BF16),
        compiler_params=_cp(),
    )(s_idx, w)


def adamw(name, w, grads, m, v):
    nl, r, c = w.shape
    tr = _ew_rows(r)

    def body(*refs):
        w_ref, m_ref, v_ref = refs[0], refs[1], refs[2]
        g_refs = refs[3:3 + nl]
        d_ref, nm_ref, nv_ref = refs[3 + nl:]
        layer = pl.program_id(0)
        g = g_refs[0][...]
        for j in range(1, nl):
            g = jnp.where(layer == j, g_refs[j][...], g)
        nm = ADAM_B1 * m_ref[...] + (1.0 - ADAM_B1) * g
        nv = ADAM_B2 * v_ref[...] + (1.0 - ADAM_B2) * (g * g)
        m_hat = nm / ADAM_BC1
        v_hat = nv / ADAM_BC2
        d_ref[...] = -ADAM_LR * (m_hat / (jnp.sqrt(v_hat) + ADAM_EPS) + ADAM_WD * w_ref[...])
        nm_ref[...] = nm
        nv_ref[...] = nv

    p_spec = pl.BlockSpec((None, tr, c), lambda l, i: (l, i, 0))
    g_spec = pl.BlockSpec((tr, c), lambda l, i: (i, 0))
    return pl.pallas_call(
        body, name=name, grid=(nl, r // tr), in_specs=[p_spec] * 3 + [g_spec] * nl, out_specs=[p_spec] * 3,
        out_shape=[jax.ShapeDtypeStruct((nl, r, c), F32)] * 3, compiler_params=_cp(),
    )(w, m, v, *grads)


def sum_blocks(name, gathered, n_blocks):
    r = gathered.shape[0] // n_blocks
    c = gathered.shape[1]
    tr = _ew_rows(r)
    nt = r // tr

    def body(*refs):
        acc = refs[0][...]
        for j in range(1, n_blocks):
            acc = acc + refs[j][...]
        refs[-1][...] = acc

    specs = [pl.BlockSpec((tr, c), lambda i, j=j: (j * nt + i, 0)) for j in range(n_blocks)]
    return pl.pallas_call(
        body, name=name, grid=(nt,), in_specs=specs, out_specs=pl.BlockSpec((tr, c), lambda i: (i, 0)),
        out_shape=jax.ShapeDtypeStruct((r, c), F32), compiler_params=_cp(),
    )(*([gathered] * n_blocks))


HBM = pl.BlockSpec(memory_space=pl.ANY)


def _place():
    return lax.axis_index("x"), lax.axis_index("y"), lax.axis_index("c")


def _other_chips(x, y):
    return [(1 - x, y), (x, 1 - y), (1 - x, 1 - y)]


def gather_weights(bufs):
    n = len(bufs)

    def body(*refs):
        out = refs[n:2 * n]
        send, recv, fsend, frecv = refs[2 * n:]
        x, y, c = _place()
        me = 2 * x + y
        chips = _other_chips(x, y)
        sibling = (x, y, 1 - c)
        firsts = []
        for a in range(n):
            hr = out[a].shape[1] // 2
            mine = out[a].at[me, pl.ds(c * hr, hr)]
            for j, (cx, cy) in enumerate(chips):
                rc = pltpu.make_async_remote_copy(
                    src_ref=mine, dst_ref=mine, send_sem=send.at[3 * a + j], recv_sem=recv.at[3 * a + j],
                    device_id=(cx, cy, c), device_id_type=MESH)
                rc.start()
                firsts.append(rc)
        passed = []
        for a in range(n):
            hr = out[a].shape[1] // 2
            for j, (cx, cy) in enumerate(chips):
                landed = out[a].at[2 * cx + cy, pl.ds(c * hr, hr)]
                pltpu.make_async_remote_copy(
                    src_ref=landed, dst_ref=landed, send_sem=send.at[3 * a + j], recv_sem=recv.at[3 * a + j],
                    device_id=(cx, cy, c), device_id_type=MESH).wait_recv()
                fw = pltpu.make_async_remote_copy(
                    src_ref=landed, dst_ref=landed, send_sem=fsend.at[3 * a + j], recv_sem=frecv.at[3 * a + j],
                    device_id=sibling, device_id_type=MESH)
                fw.start()
                passed.append(fw)
        for a in range(n):
            hr = out[a].shape[1] // 2
            for j, (cx, cy) in enumerate(chips):
                theirs = out[a].at[2 * cx + cy, pl.ds((1 - c) * hr, hr)]
                pltpu.make_async_remote_copy(
                    src_ref=theirs, dst_ref=theirs, send_sem=fsend.at[3 * a + j], recv_sem=frecv.at[3 * a + j],
                    device_id=sibling, device_id_type=MESH).wait_recv()
        for cp in firsts + passed:
            cp.wait_send()

    return pl.pallas_call(
        body, name="gather_weights", in_specs=[HBM] * n, out_specs=[HBM] * n,
        out_shape=[jax.ShapeDtypeStruct(b.shape, b.dtype) for b in bufs],
        input_output_aliases={a: a for a in range(n)},
        scratch_shapes=[pltpu.SemaphoreType.DMA((3 * n,))] * 4,
        compiler_params=_cp(),
    )(*bufs)


def swap_halves(grads):
    n = len(grads)

    def body(*refs):
        g = refs[:n]
        land = refs[n:2 * n]
        send, recv = refs[2 * n:]
        x, y, c = _place()
        copies = []
        for a in range(n):
            hr = g[a].shape[1] // 2
            rc = pltpu.make_async_remote_copy(
                src_ref=g[a].at[:, pl.ds((1 - c) * hr, hr)], dst_ref=land[a],
                send_sem=send.at[a], recv_sem=recv.at[a], device_id=(x, y, 1 - c), device_id_type=MESH)
            rc.start()
            copies.append(rc)
        for rc in copies:
            rc.wait()

    return pl.pallas_call(
        body, name="swap_halves", in_specs=[HBM] * n, out_specs=[HBM] * n,
        out_shape=[jax.ShapeDtypeStruct((N_SHARD, g.shape[1] // 2, g.shape[2]), F32) for g in grads],
        scratch_shapes=[pltpu.SemaphoreType.DMA((n,))] * 2,
        compiler_params=_cp(),
    )(*grads)


def add_halves(name, grad, landed, c_idx):
    _, r, c = grad.shape
    hr = r // 2
    tr = _ew_rows(hr)
    nt = hr // tr

    def body(c_ref, g_ref, l_ref, o_ref, ob_ref):
        tot = g_ref[...] + l_ref[...]
        o_ref[...] = tot
        ob_ref[...] = tot.astype(BF16)

    blk = pl.BlockSpec((None, tr, c), lambda sh, i, c_ref: (sh, i, 0))
    grid_spec = pltpu.PrefetchScalarGridSpec(
        num_scalar_prefetch=1, grid=(N_SHARD, nt),
        in_specs=[pl.BlockSpec((None, tr, c), lambda sh, i, c_ref: (sh, c_ref[0] * nt + i, 0)), blk],
        out_specs=[blk, blk])
    return pl.pallas_call(
        body, name=name, grid_spec=grid_spec,
        out_shape=[jax.ShapeDtypeStruct((N_SHARD, hr, c), F32), jax.ShapeDtypeStruct((N_SHARD, hr, c), BF16)],
        compiler_params=_cp(),
    )(c_idx, grad, landed)


def send_to_owners(halves):
    n = len(halves)

    def body(*refs):
        src = refs[:n]
        land = refs[n:2 * n]
        send, recv = refs[2 * n:]
        x, y, c = _place()
        chips = _other_chips(x, y)
        copies = []
        for a in range(n):
            for j, (cx, cy) in enumerate(chips):
                rc = pltpu.make_async_remote_copy(
                    src_ref=src[a].at[2 * cx + cy], dst_ref=land[a].at[j],
                    send_sem=send.at[3 * a + j], recv_sem=recv.at[3 * a + j],
                    device_id=(cx, cy, c), device_id_type=MESH)
                rc.start()
                copies.append(rc)
        for rc in copies:
            rc.wait()

    return pl.pallas_call(
        body, name="send_to_owners", in_specs=[HBM] * n, out_specs=[HBM] * n,
        out_shape=[jax.ShapeDtypeStruct((3,) + h.shape[1:], BF16) for h in halves],
        scratch_shapes=[pltpu.SemaphoreType.DMA((3 * n,))] * 2,
        compiler_params=_cp(),
    )(*halves)


def add_owned(name, own, landed, sc_idx):
    _, hr, c = own.shape
    tr = _ew_rows(hr)
    nt = hr // tr

    def body(sc_ref, o_ref, l0, l1, l2, out_ref):
        out_ref[...] = ((o_ref[...] + l0[...].astype(F32)) + l1[...].astype(F32)) + l2[...].astype(F32)

    grid_spec = pltpu.PrefetchScalarGridSpec(
        num_scalar_prefetch=1, grid=(nt,),
        in_specs=[pl.BlockSpec((None, tr, c), lambda i, sc_ref: (sc_ref[0], i, 0))]
        + [pl.BlockSpec((None, tr, c), lambda i, sc_ref, j=j: (j, i, 0)) for j in range(3)],
        out_specs=pl.BlockSpec((tr, c), lambda i, sc_ref: (sc_ref[1] * nt + i, 0)))
    return pl.pallas_call(
        body, name=name, grid_spec=grid_spec, out_shape=jax.ShapeDtypeStruct((2 * hr, c), F32),
        compiler_params=_cp(),
    )(sc_idx, own, landed, landed, landed)


def join_halves(bufs):
    n = len(bufs)

    def body(*refs):
        out = refs[n:2 * n]
        send, recv = refs[2 * n:]
        x, y, c = _place()
        copies = []
        for a in range(n):
            hr = out[a].shape[0] // 2
            mine = out[a].at[pl.ds(c * hr, hr)]
            rc = pltpu.make_async_remote_copy(
                src_ref=mine, dst_ref=mine, send_sem=send.at[a], recv_sem=recv.at[a],
                device_id=(x, y, 1 - c), device_id_type=MESH)
            rc.start()
            copies.append(rc)
        for a in range(n):
            hr = out[a].shape[0] // 2
            theirs = out[a].at[pl.ds((1 - c) * hr, hr)]
            pltpu.make_async_remote_copy(
                src_ref=theirs, dst_ref=theirs, send_sem=send.at[a], recv_sem=recv.at[a],
                device_id=(x, y, 1 - c), device_id_type=MESH).wait_recv()
        for rc in copies:
            rc.wait_send()

    return pl.pallas_call(
        body, name="join_halves", in_specs=[HBM] * n, out_specs=[HBM] * n,
        out_shape=[jax.ShapeDtypeStruct(b.shape, b.dtype) for b in bufs],
        input_output_aliases={a: a for a in range(n)},
        scratch_shapes=[pltpu.SemaphoreType.DMA((n,))] * 2,
        compiler_params=_cp(),
    )(*bufs)


def gather_small(name, block):
    m_per, n = block.shape

    def body(x_ref, out_ref, send_sems, recv_sems, local_sem):
        x, y, c = _place()
        me, sibling = (x, y, c), (x, y, 1 - c)
        chips = _other_chips(x, y)

        def rows(px, py, pc):
            return out_ref.at[pl.ds((4 * px + 2 * py + pc) * m_per, m_per), :]

        def copy(k, blk, to, src=None):
            return pltpu.make_async_remote_copy(
                src_ref=rows(*blk) if src is None else src, dst_ref=rows(*blk),
                send_sem=send_sems.at[k], recv_sem=recv_sems.at[k], device_id=to, device_id_type=MESH)

        mine = pltpu.make_async_copy(x_ref, rows(*me), local_sem)
        mine.start()
        first = [copy(0, me, sibling, src=x_ref)]
        first += [copy(1 + j, me, (*chip, c), src=x_ref) for j, chip in enumerate(chips)]
        for cp in first:
            cp.start()
        passed = [copy(4 + j, (*chip, c), sibling) for j, chip in enumerate(chips)]
        for j, chip in enumerate(chips):
            copy(1 + j, (*chip, c), me).wait_recv()
            passed[j].start()
        copy(0, sibling, me).wait_recv()
        for j, chip in enumerate(chips):
            copy(4 + j, (*chip, 1 - c), me).wait_recv()
        for cp in first + passed:
            cp.wait_send()
        mine.wait()

    return pl.pallas_call(
        body, name=name, out_shape=jax.ShapeDtypeStruct((8 * m_per, n), block.dtype),
        in_specs=[pl.BlockSpec(memory_space=pltpu.VMEM)], out_specs=pl.BlockSpec(memory_space=pltpu.VMEM),
        scratch_shapes=[pltpu.SemaphoreType.DMA((7,)), pltpu.SemaphoreType.DMA((7,)), pltpu.SemaphoreType.DMA],
        compiler_params=_cp(),
    )(block)


PACK_QUANTUM = 8 * LANES


def _pack(arrays):
    pieces = []
    for a in arrays:
        flat = a.reshape(-1)
        padded = -(-flat.shape[0] // PACK_QUANTUM) * PACK_QUANTUM
        pieces.append(jnp.pad(flat, (0, padded - flat.shape[0])).reshape(-1, LANES))
    return jnp.concatenate(pieces, axis=0)


def _unpack(packed, shapes):
    out = []
    row = 0
    for shp in shapes:
        size = math.prod(shp)
        rows = -(-size // PACK_QUANTUM) * 8
        out.append(packed[row:row + rows].reshape(-1)[:size].reshape(shp))
        row += rows
    return out


def kernel(x, p, mix_w_in, pool_w, pool_scale, conv_dw_w, conv_dw_b, conv_ln_g, conv_ln_b, mix_w_out, attn_w_qkv, attn_rel_bias, attn_w_o, ln_mix_g, ln_mix_b, ffn_w_up, ffn_dw_w, ffn_dw_b, ffn_w_down, ple_w_proj, ple_w_gate, ple_b_gate, ln_ffn_g, ln_ffn_b, loss_target, m_mix_w_in, m_pool_w, m_pool_scale, m_conv_dw_w, m_conv_dw_b, m_conv_ln_g, m_conv_ln_b, m_mix_w_out, m_attn_w_qkv, m_attn_rel_bias, m_attn_w_o, m_ln_mix_g, m_ln_mix_b, m_ffn_w_up, m_ffn_dw_w, m_ffn_dw_b, m_ffn_w_down, m_ple_w_proj, m_ple_w_gate, m_ple_b_gate, m_ln_ffn_g, m_ln_ffn_b, v_mix_w_in, v_pool_w, v_pool_scale, v_conv_dw_w, v_conv_dw_b, v_conv_ln_g, v_conv_ln_b, v_mix_w_out, v_attn_w_qkv, v_attn_rel_bias, v_attn_w_o, v_ln_mix_g, v_ln_mix_b, v_ffn_w_up, v_ffn_dw_w, v_ffn_dw_b, v_ffn_w_down, v_ple_w_proj, v_ple_w_gate, v_ple_b_gate, v_ln_ffn_g, v_ln_ffn_b):
    xi, yi, ci = _place()
    shard_idx = (2 * xi + yi).astype(jnp.int32)
    s_arr = shard_idx.reshape(1)
    c_arr = ci.astype(jnp.int32).reshape(1)
    sc_arr = jnp.concatenate([s_arr, c_arr])

    x0 = x[0]
    target = loss_target[0]
    seq = x0.shape[0]

    big = [
        ("mix_w_in", mix_w_in, m_mix_w_in, v_mix_w_in, True),
        ("mix_w_out", mix_w_out, m_mix_w_out, v_mix_w_out, False),
        ("attn_w_qkv", attn_w_qkv, m_attn_w_qkv, v_attn_w_qkv, True),
        ("attn_w_o", attn_w_o, m_attn_w_o, v_attn_w_o, False),
        ("ffn_w_up", ffn_w_up, m_ffn_w_up, v_ffn_w_up, True),
        ("ffn_w_down", ffn_w_down, m_ffn_w_down, v_ffn_w_down, False),
        ("ple_w_proj", ple_w_proj, m_ple_w_proj, v_ple_w_proj, True),
        ("ple_w_gate", ple_w_gate, m_ple_w_gate, v_ple_w_gate, False),
    ]
    keys = []
    shards = []
    for nm, w, _, _, _ in big:
        for layer in range(w.shape[0]):
            keys.append((nm, layer))
            shards.append(cast_into_gathered(f"cast_{nm}_{layer}", w, layer, s_arr))
    gathered = dict(zip(keys, gather_weights(shards)))
    col_sharded = {nm: cs for nm, _, _, _, cs in big}

    def weight(nm, layer):
        g = gathered[(nm, layer)]
        if col_sharded[nm]:
            return g
        return g.reshape(g.shape[0] * g.shape[1], g.shape[2])

    dw_shapes = [conv_dw_w.shape, ffn_dw_w.shape]
    dw_packed = _pack([conv_dw_w, ffn_dw_w])
    dw_rows = dw_packed.shape[0]
    dw_all = gather_small("gather_dw", dw_packed)
    dw_parts = [_unpack(dw_all[2 * k * dw_rows:(2 * k + 1) * dw_rows], dw_shapes) for k in range(N_SHARD)]
    conv_w_full = jnp.concatenate([pc[0] for pc in dw_parts], axis=2)[0]
    ffn_dw_full = jnp.concatenate([pc[1] for pc in dw_parts], axis=2)

    big_grads = {}
    small_grads = {}

    saved = []
    h_in = x0
    for layer in range(N_LAYERS):
        sv = {"x_in": h_in}
        if layer % 2 == 0:
            u = mm_cols_fwd("mix_in", h_in, weight("mix_w_in", 0), F32)
            cat, d_sv, e_sv, glu_sv, hh_sv, rs_sv = mixer_fwd(
                "mixer_fwd", u, pool_w[0], pool_scale, conv_w_full, conv_dw_b, conv_ln_g, conv_ln_b)
            mix = mm_rows_fwd("mix_out", cat, weight("mix_w_out", 0))
            sv.update(u=u, cat=cat, d=d_sv, e=e_sv, glu=glu_sv, hh=hh_sv, rs=rs_sv)
        else:
            qkvp = mm_cols_fwd("attn_qkv", h_in, weight("attn_w_qkv", 0), BF16,
                               pad_blocks=PAD_ROWS // _row_tile(seq))
            bias = _toeplitz_bias(attn_rel_bias[0])
            att = attn_fwd("attn_fwd", qkvp, bias)
            mix = mm_rows_fwd("attn_out", att, weight("attn_w_o", 0))
            sv.update(qkvp=qkvp, bias=bias, att=att)
        x1, xh1, rs1 = ln_fwd(f"ln_mix_{layer}", h_in, mix, ln_mix_g[layer:layer + 1], ln_mix_b[layer:layer + 1])
        gv = mm_cols_fwd(f"ffn_up_{layer}", x1, weight("ffn_w_up", layer), F32)
        hid = ffn_act_fwd(f"ffn_act_{layer}", gv, ffn_dw_full[layer], ffn_dw_b[layer:layer + 1])
        ffn = mm_rows_fwd(f"ffn_down_{layer}", hid, weight("ffn_w_down", layer))
        pgl = mm_rows_fwd(f"ple_gate_{layer}", x1, weight("ple_w_gate", layer))
        pp = mm_cols_fwd(f"ple_proj_{layer}", p[layer, 0], weight("ple_w_proj", layer), F32)
        bg = ple_b_gate[layer:layer + 1]
        x2, xh2, rs2 = ln_fwd(f"ln_ffn_{layer}", x1, ffn, ln_ffn_g[layer:layer + 1], ln_ffn_b[layer:layer + 1],
                              ple=(pgl, pp, bg))
        sv.update(x1=x1, xh1=xh1, rs1=rs1, gv=gv, hid=hid, pgl=pgl, pp=pp, xh2=xh2, rs2=rs2)
        saved.append(sv)
        h_in = x2

    dy, loss_part = loss_fwd_bwd("loss", h_in, target)

    parts = [(1.0, dy)]
    for layer in reversed(range(N_LAYERS)):
        sv = saved[layer]
        bg = ple_b_gate[layer:layer + 1]
        dz2, dg2, db2, dpp, dpgl, dbg = ln_bwd(
            f"ln_ffn_bwd_{layer}", parts, sv["xh2"], sv["rs2"], ln_ffn_g[layer:layer + 1],
            ple=(sv["pgl"], sv["pp"], bg))
        small_grads[("ln_ffn_g", layer)] = dg2
        small_grads[("ln_ffn_b", layer)] = db2
        small_grads[("ple_b_gate", layer)] = dbg
        w_down = weight("ffn_w_down", layer)
        dhid = mm_rows_dx(f"ffn_down_dx_{layer}", dz2, w_down)
        big_grads[("ffn_w_down", layer)] = mm_rows_dw(f"ffn_down_dw_{layer}", sv["hid"], dz2)
        dgv, ddw, ddb = ffn_act_bwd(f"ffn_act_bwd_{layer}", dhid, sv["gv"], ffn_dw_full[layer],
                                    ffn_dw_b[layer:layer + 1])
        small_grads[("ffn_dw_w", layer)] = ddw
        small_grads[("ffn_dw_b", layer)] = ddb
        big_grads[("ffn_w_up", layer)] = mm_cols_dw(f"ffn_up_dw_{layer}", sv["x1"], dgv)
        t_up = mm_cols_dx(f"ffn_up_dx_{layer}", dgv, weight("ffn_w_up", layer))
        big_grads[("ple_w_gate", layer)] = mm_rows_dw(f"ple_gate_dw_{layer}", sv["x1"], dpgl)
        t_gate = mm_rows_dx(f"ple_gate_dx_{layer}", dpgl, weight("ple_w_gate", layer))
        big_grads[("ple_w_proj", layer)] = mm_cols_dw(f"ple_proj_dw_{layer}", p[layer, 0], dpp)
        dz1, dg1, db1 = ln_bwd(
            f"ln_mix_bwd_{layer}", [(ALPHA, dz2), (1.0, t_up), (1.0, t_gate)], sv["xh1"], sv["rs1"],
            ln_mix_g[layer:layer + 1])
        small_grads[("ln_mix_g", layer)] = dg1
        small_grads[("ln_mix_b", layer)] = db1
        if layer % 2 == 0:
            dcat = mm_rows_dx("mix_out_dx", dz1, weight("mix_w_out", 0))
            big_grads[("mix_w_out", 0)] = mm_rows_dw("mix_out_dw", sv["cat"], dz1)
            du, dpw, dps, dcw, dcb, dcg, dcbt = mixer_bwd(
                "mixer_bwd", dcat, sv["u"], sv["d"], sv["e"], sv["glu"], sv["hh"], sv["rs"],
                pool_w[0], pool_scale, conv_w_full, conv_ln_g, conv_ln_b)
            small_grads[("pool_w", 0)] = dpw
            small_grads[("pool_scale", 0)] = dps
            small_grads[("conv_dw_w", 0)] = dcw
            small_grads[("conv_dw_b", 0)] = dcb
            small_grads[("conv_ln_g", 0)] = dcg
            small_grads[("conv_ln_b", 0)] = dcbt
            big_grads[("mix_w_in", 0)] = mm_cols_dw("mix_in_dw", sv["x_in"], du)
            t_mix = mm_cols_dx("mix_in_dx", du, weight("mix_w_in", 0))
        else:
            do = mm_rows_dx("attn_out_dx", dz1, weight("attn_w_o", 0), out_dtype=BF16)
            big_grads[("attn_w_o", 0)] = mm_rows_dw("attn_out_dw", sv["att"], dz1)
            dq, dk, dv, ds_sum = attn_bwd("attn_bwd", sv["qkvp"], sv["bias"], do)
            cols, sat = bias_grad_reduce("bias_grad", _shear_for_bias_grad(ds_sum))
            d_rel = jnp.concatenate(
                [jnp.zeros((N_HEADS, 1), F32),
                 jnp.flip(cols[:, 0, SHEAR_SAT + 1:SHEAR_W], axis=1),
                 sat[:, 0, 0:1]], axis=1)
            small_grads[("attn_rel_bias", 0)] = d_rel
            dqkv = jnp.concatenate([dq, dk, dv], axis=1)
            big_grads[("attn_w_qkv", 0)] = mm_cols_dw("attn_qkv_dw", sv["x_in"], dqkv)
            t_mix = mm_cols_dx("attn_qkv_dx", dqkv, weight("attn_w_qkv", 0))
        parts = [(ALPHA, dz1), (1.0, t_mix)]
    grad_x = scaled_sum("grad_x", parts)

    g_list = []
    for key in keys:
        g = big_grads[key]
        if g.ndim == 2:
            g = g.reshape(N_SHARD, g.shape[0] // N_SHARD, g.shape[1])
        g_list.append(g)
    landed = swap_halves(g_list)
    own_f32 = []
    wire = []
    for key, g, ld in zip(keys, g_list, landed):
        o, ob = add_halves(f"add_halves_{key[0]}_{key[1]}", g, ld, c_arr)
        own_f32.append(o)
        wire.append(ob)
    arrived = send_to_owners(wire)
    finals = [add_owned(f"add_owned_{key[0]}_{key[1]}", o, ar, sc_arr) for key, o, ar in zip(keys, own_f32, arrived)]
    shard_grads = dict(zip(keys, join_halves(finals)))

    big_out = {}
    for nm, w, m, v, _ in big:
        gl = [shard_grads[(nm, layer)] for layer in range(w.shape[0])]
        delta, new_m, new_v = adamw(f"adamw_{nm}", w, gl, m, v)
        big_out[nm] = (jnp.stack(gl, axis=0), delta, new_m, new_v)

    small = [
        ("pool_w", pool_w, m_pool_w, v_pool_w, None),
        ("pool_scale", pool_scale, m_pool_scale, v_pool_scale, None),
        ("conv_dw_w", conv_dw_w, m_conv_dw_w, v_conv_dw_w, 2),
        ("conv_dw_b", conv_dw_b, m_conv_dw_b, v_conv_dw_b, None),
        ("conv_ln_g", conv_ln_g, m_conv_ln_g, v_conv_ln_g, None),
        ("conv_ln_b", conv_ln_b, m_conv_ln_b, v_conv_ln_b, None),
        ("attn_rel_bias", attn_rel_bias, m_attn_rel_bias, v_attn_rel_bias, None),
        ("ln_mix_g", ln_mix_g, m_ln_mix_g, v_ln_mix_g, None),
        ("ln_mix_b", ln_mix_b, m_ln_mix_b, v_ln_mix_b, None),
        ("ffn_dw_w", ffn_dw_w, m_ffn_dw_w, v_ffn_dw_w, 2),
        ("ffn_dw_b", ffn_dw_b, m_ffn_dw_b, v_ffn_dw_b, None),
        ("ple_b_gate", ple_b_gate, m_ple_b_gate, v_ple_b_gate, None),
        ("ln_ffn_g", ln_ffn_g, m_ln_ffn_g, v_ln_ffn_g, None),
        ("ln_ffn_b", ln_ffn_b, m_ln_ffn_b, v_ln_ffn_b, None),
    ]
    full_grads = []
    for nm, w, _, _, shard_axis in small:
        full = list(w.shape)
        if shard_axis is not None:
            full[shard_axis] *= N_SHARD
        per_layer = [small_grads[(nm, layer)].reshape((1,) + tuple(full[1:])) for layer in range(w.shape[0])]
        full_grads.append(jnp.concatenate(per_layer, axis=0))
    packed = _pack(full_grads + [loss_part])
    total = sum_blocks("sum_small", gather_small("gather_small_grads", packed), 8)
    unpacked = _unpack(total, [g.shape for g in full_grads] + [loss_part.shape])
    loss = unpacked[-1][0, 0]
    local_grads = []
    for (nm, w, _, _, shard_axis), g in zip(small, unpacked[:-1]):
        if shard_axis is not None:
            width = w.shape[shard_axis]
            g = lax.dynamic_slice_in_dim(g, shard_idx * width, width, axis=shard_axis)
        local_grads.append(g.reshape(w.shape))
    shapes = [w.shape for _, w, _, _, _ in small]
    pg = _pack(local_grads)
    pw = _pack([w for _, w, _, _, _ in small])
    pm = _pack([m for _, _, m, _, _ in small])
    pv = _pack([v for _, _, _, v, _ in small])
    delta_s, new_m_s, new_v_s = adamw("adamw_small", pw[None], [pg], pm[None], pv[None])
    small_out = {}
    for (nm, _, _, _, _), g, d_, m_, v_ in zip(
            small, local_grads, _unpack(delta_s[0], shapes), _unpack(new_m_s[0], shapes), _unpack(new_v_s[0], shapes)):
        small_out[nm] = (g, d_, m_, v_)

    order = ["mix_w_in", "pool_w", "pool_scale", "conv_dw_w", "conv_dw_b", "conv_ln_g", "conv_ln_b", "mix_w_out",
             "attn_w_qkv", "attn_rel_bias", "attn_w_o", "ln_mix_g", "ln_mix_b", "ffn_w_up", "ffn_dw_w", "ffn_dw_b",
             "ffn_w_down", "ple_w_proj", "ple_w_gate", "ple_b_gate", "ln_ffn_g", "ln_ffn_b"]
    res = {**big_out, **small_out}
    outs = [loss, grad_x[None]]
    for slot in range(4):
        outs += [res[nm][slot] for nm in order]
    return tuple(outs)
```

```python
import functools
import math

import jax
import jax.numpy as jnp
from jax import lax
from jax.experimental import pallas as pl
from jax.experimental.pallas import tpu as pltpu

F32 = jnp.float32
BF16 = jnp.bfloat16
MESH = pl.DeviceIdType.MESH

N_LAYERS = 2
ALPHA = (2 * N_LAYERS) ** 0.25
LN_EPS = 1e-5
NEG_INF = -1e30
CHUNK = 64
LEFT_CHUNKS = 8
PAD_ROWS = LEFT_CHUNKS * CHUNK
HEAD_DIM = 64
N_HEADS = 16
MAX_REL = 256
POOL_WINDOWS = (2, 4, 8, 16)
POOL_GROUP = 128
CONV_K = 31
FFN_K = 3
CONV_HALO = 32
FFN_HALO = 8
Q_TILE = 256
K_WIN = Q_TILE + PAD_ROWS
SHEAR_W = Q_TILE + K_WIN
SHEAR_SAT = SHEAR_W - 2 * MAX_REL
N_SHARD = 4
LANES = 128

ADAM_LR = 0.001
ADAM_B1 = 0.9
ADAM_B2 = 0.999
ADAM_EPS = 1e-08
ADAM_WD = 0.01
ADAM_STEP = 10
ADAM_BC1 = 1.0 - ADAM_B1 ** ADAM_STEP
ADAM_BC2 = 1.0 - ADAM_B2 ** ADAM_STEP

DIMS = {
    "nn": (((1,), (0,)), ((), ())),
    "nt": (((1,), (1,)), ((), ())),
    "tn": (((0,), (0,)), ((), ())),
}


def _cp(vmem_mb=48, **kw):
    return pltpu.CompilerParams(vmem_limit_bytes=vmem_mb * 1024 * 1024, **kw)


def _dot(a, b, mode):
    return lax.dot_general(a.astype(BF16), b.astype(BF16), DIMS[mode], preferred_element_type=F32)


def _sig(x):
    return 1.0 / (1.0 + jnp.exp(-x))


def _row_tile(s):
    return min(512, s // 4)


def _mm(name, mode, a, b, in_specs, out_shape, out_spec, acc_shape, grid, nk, zero_first=False, vmem_mb=48):
    out_f32 = out_shape.dtype == F32

    def body(a_ref, b_ref, o_ref, *scr):
        k = pl.program_id(2)

        def compute():
            part = _dot(a_ref[...], b_ref[...], mode)
            if nk == 1:
                o_ref[...] = part.astype(o_ref.dtype)
                return
            acc = o_ref if out_f32 else scr[0]

            @pl.when(k == 0)
            def _():
                acc[...] = part

            @pl.when(k > 0)
            def _():
                acc[...] += part

            if not out_f32:
                @pl.when(k == nk - 1)
                def _():
                    o_ref[...] = acc[...].astype(o_ref.dtype)

        if zero_first:
            @pl.when(pl.program_id(1) == 0)
            def _():
                o_ref[...] = jnp.zeros(o_ref.shape, o_ref.dtype)

            pl.when(pl.program_id(1) > 0)(compute)
        else:
            compute()

    scratch = [] if (nk == 1 or out_f32) else [pltpu.VMEM(acc_shape, F32)]
    return pl.pallas_call(
        body, name=name, grid=grid, in_specs=in_specs, out_specs=out_spec, out_shape=out_shape,
        scratch_shapes=scratch, compiler_params=_cp(vmem_mb),
    )(a, b)


def mm_cols_fwd(name, a, wc, out_dtype, pad_blocks=0):
    s, k = a.shape
    n4 = wc.shape[2]
    tm = _row_tile(s)
    nt = s // tm
    return _mm(
        name, "nn", a, wc,
        [pl.BlockSpec((tm, k), lambda j, i, r: (jnp.maximum(i - pad_blocks, 0), 0)),
         pl.BlockSpec((None, k, n4), lambda j, i, r: (j, 0, 0))],
        jax.ShapeDtypeStruct((s + pad_blocks * tm, N_SHARD * n4), out_dtype),
        pl.BlockSpec((tm, n4), lambda j, i, r: (i, j)),
        None, (N_SHARD, nt + pad_blocks, 1), 1, zero_first=pad_blocks > 0)


def mm_cols_dx(name, dy, wc):
    s = dy.shape[0]
    _, k, n4 = wc.shape
    tm = _row_tile(s)
    return _mm(
        name, "nt", dy, wc,
        [pl.BlockSpec((tm, n4), lambda g, i, r: (i, r)),
         pl.BlockSpec((None, k, n4), lambda g, i, r: (r, 0, 0))],
        jax.ShapeDtypeStruct((s, k), F32),
        pl.BlockSpec((tm, k), lambda g, i, r: (i, 0)),
        (tm, k), (1, s // tm, N_SHARD), N_SHARD)


def mm_cols_dw(name, a, dy):
    s, k = a.shape
    n4 = dy.shape[1] // N_SHARD
    tm = _row_tile(s)
    nt = s // tm
    return _mm(
        name, "tn", a, dy,
        [pl.BlockSpec((tm, k), lambda j, g, r: (r, 0)),
         pl.BlockSpec((tm, n4), lambda j, g, r: (r, j))],
        jax.ShapeDtypeStruct((N_SHARD, k, n4), F32),
        pl.BlockSpec((None, k, n4), lambda j, g, r: (j, 0, 0)),
        (k, n4), (N_SHARD, 1, nt), nt)


def _k_tile(k):
    return k if k <= 1024 else k // 2


def mm_rows_fwd(name, a, wr, out_dtype=F32):
    s, k = a.shape
    n = wr.shape[1]
    tm = _row_tile(s)
    tk = _k_tile(k)
    nk = k // tk
    return _mm(
        name, "nn", a, wr,
        [pl.BlockSpec((tm, tk), lambda g, i, r: (i, r)),
         pl.BlockSpec((tk, n), lambda g, i, r: (r, 0))],
        jax.ShapeDtypeStruct((s, n), out_dtype),
        pl.BlockSpec((tm, n), lambda g, i, r: (i, 0)),
        (tm, n), (1, s // tm, nk), nk)


def mm_rows_dx(name, dy, wr, out_dtype=F32):
    s, n = dy.shape
    k = wr.shape[0]
    tm = _row_tile(s)
    tk = _k_tile(k)
    return _mm(
        name, "nt", dy, wr,
        [pl.BlockSpec((tm, n), lambda j, i, r: (i, 0)),
         pl.BlockSpec((tk, n), lambda j, i, r: (j, 0))],
        jax.ShapeDtypeStruct((s, k), out_dtype),
        pl.BlockSpec((tm, tk), lambda j, i, r: (i, j)),
        None, (k // tk, s // tm, 1), 1)


def mm_rows_dw(name, a, dy):
    s, k = a.shape
    n = dy.shape[1]
    tm = _row_tile(s)
    tk = _k_tile(k)
    nt = s // tm
    return _mm(
        name, "tn", a, dy,
        [pl.BlockSpec((tm, tk), lambda j, g, r: (r, j)),
         pl.BlockSpec((tm, n), lambda j, g, r: (r, 0))],
        jax.ShapeDtypeStruct((k, n), F32),
        pl.BlockSpec((tk, n), lambda j, g, r: (j, 0)),
        (tk, n), (k // tk, 1, nt), nt)


def _row(tm, c, col=0):
    return pl.BlockSpec((tm, c), lambda i: (i, col))


def _full(shape):
    nd = len(shape)
    return pl.BlockSpec(shape, lambda i: (0,) * nd)


def _prev(tm, h, c, col=0):
    return pl.BlockSpec((h, c), lambda i: (jnp.maximum(i * (tm // h) - 1, 0), col))


def _next(tm, h, c, s, col=0):
    return pl.BlockSpec((h, c), lambda i: (jnp.minimum((i + 1) * (tm // h), s // h - 1), col))


def _acc_add(ref, first, val):
    @pl.when(first)
    def _():
        ref[...] = val

    @pl.when(jnp.logical_not(first))
    def _():
        ref[...] += val


def _colsum(v):
    return jnp.sum(v, axis=0, keepdims=True)


def _ln_stats(z):
    mu = jnp.mean(z, axis=-1, keepdims=True)
    zc = z - mu
    var = jnp.mean(zc * zc, axis=-1, keepdims=True)
    rstd = lax.rsqrt(var + LN_EPS)
    return zc * rstd, rstd


def _ln_bwd(dxhat, xhat, rstd):
    m1 = jnp.mean(dxhat, axis=-1, keepdims=True)
    m2 = jnp.mean(dxhat * xhat, axis=-1, keepdims=True)
    return rstd * (dxhat - m1 - xhat * m2)


def ln_fwd(name, x, f, g, b, ple=None):
    s, d = x.shape
    tm = _row_tile(s)
    n_in = 2 + (3 if ple is not None else 0)

    def body(*refs):
        x_ref, f_ref = refs[0], refs[1]
        g_ref, b_ref = refs[n_in], refs[n_in + 1]
        y_ref, xh_ref, rs_ref = refs[n_in + 2:]
        z = ALPHA * x_ref[...] + f_ref[...]
        if ple is not None:
            pgl_ref, pp_ref, bg_ref = refs[2:5]
            z = z + _sig(pgl_ref[...] + bg_ref[...]) * pp_ref[...]
        xhat, rstd = _ln_stats(z)
        y_ref[...] = xhat * g_ref[...] + b_ref[...]
        xh_ref[...] = xhat
        rs_ref[...] = jnp.broadcast_to(rstd, rs_ref.shape)

    ins = [x, f]
    specs = [_row(tm, d), _row(tm, d)]
    if ple is not None:
        pgl, pp, bg = ple
        ins += [pgl, pp, bg]
        specs += [_row(tm, d), _row(tm, d), _full((1, d))]
    ins += [g, b]
    specs += [_full((1, d)), _full((1, d))]
    return pl.pallas_call(
        body, name=name, grid=(s // tm,), in_specs=specs,
        out_specs=[_row(tm, d), _row(tm, d), _row(tm, LANES)],
        out_shape=[jax.ShapeDtypeStruct((s, d), F32), jax.ShapeDtypeStruct((s, d), F32),
                   jax.ShapeDtypeStruct((s, LANES), F32)],
        compiler_params=_cp(),
    )(*ins)


def ln_bwd(name, parts, xhat, rstd, g, ple=None):
    s, d = xhat.shape
    tm = _row_tile(s)
    coefs = [c for c, _ in parts]
    n_p = len(parts)
    n_in = n_p + 3 + (3 if ple is not None else 0)

    def body(*refs):
        first = pl.program_id(0) == 0
        dy = coefs[0] * refs[0][...].astype(F32)
        for j in range(1, n_p):
            dy = dy + coefs[j] * refs[j][...].astype(F32)
        xh = refs[n_p][...]
        rs = refs[n_p + 1][:, 0:1]
        g_v = refs[n_p + 2][...]
        outs = refs[n_in:]
        dz = _ln_bwd(dy * g_v, xh, rs)
        outs[0][...] = dz
        _acc_add(outs[1], first, _colsum(dy * xh))
        _acc_add(outs[2], first, _colsum(dy))
        if ple is not None:
            pgl_ref, pp_ref, bg_ref = refs[n_p + 3:n_p + 6]
            pg = _sig(pgl_ref[...] + bg_ref[...])
            dpgl = dz * pp_ref[...] * pg * (1.0 - pg)
            outs[3][...] = (dz * pg).astype(BF16)
            outs[4][...] = dpgl.astype(BF16)
            _acc_add(outs[5], first, _colsum(dpgl))

    ins = [p for _, p in parts] + [xhat, rstd, g]
    specs = [_row(tm, d)] * n_p + [_row(tm, d), _row(tm, LANES), _full((1, d))]
    out_specs = [_row(tm, d), _full((1, d)), _full((1, d))]
    out_shape = [jax.ShapeDtypeStruct((s, d), F32), jax.ShapeDtypeStruct((1, d), F32),
                 jax.ShapeDtypeStruct((1, d), F32)]
    if ple is not None:
        pgl, pp, bg = ple
        ins += [pgl, pp, bg]
        specs += [_row(tm, d), _row(tm, d), _full((1, d))]
        out_specs += [_row(tm, d), _row(tm, d), _full((1, d))]
        out_shape += [jax.ShapeDtypeStruct((s, d), BF16), jax.ShapeDtypeStruct((s, d), BF16),
                      jax.ShapeDtypeStruct((1, d), F32)]
    return pl.pallas_call(
        body, name=name, grid=(s // tm,), in_specs=specs, out_specs=out_specs, out_shape=out_shape,
        compiler_params=_cp(),
    )(*ins)


def loss_fwd_bwd(name, y, target):
    s, d = y.shape
    tm = _row_tile(s)

    def body(y_ref, t_ref, dy_ref, l_ref):
        first = pl.program_id(0) == 0
        err = y_ref[...] - t_ref[...]
        dy_ref[...] = err * (1.0 / d)
        part = 0.5 * jnp.sum(jnp.mean(err * err, axis=-1, keepdims=True), axis=0, keepdims=True)
        _acc_add(l_ref, first, jnp.broadcast_to(part, l_ref.shape))

    return pl.pallas_call(
        body, name=name, grid=(s // tm,), in_specs=[_row(tm, d), _row(tm, d)],
        out_specs=[_row(tm, d), _full((8, LANES))],
        out_shape=[jax.ShapeDtypeStruct((s, d), F32), jax.ShapeDtypeStruct((8, LANES), F32)],
        compiler_params=_cp(),
    )(y, target)


def scaled_sum(name, parts):
    s, d = parts[0][1].shape
    tm = _row_tile(s)
    coefs = [c for c, _ in parts]

    def body(*refs):
        acc = coefs[0] * refs[0][...].astype(F32)
        for j in range(1, len(coefs)):
            acc = acc + coefs[j] * refs[j][...].astype(F32)
        refs[-1][...] = acc

    return pl.pallas_call(
        body, name=name, grid=(s // tm,), in_specs=[_row(tm, d)] * len(parts), out_specs=_row(tm, d),
        out_shape=jax.ShapeDtypeStruct((s, d), F32), compiler_params=_cp(),
    )(*[p for _, p in parts])


def _tile_pos(i, tm, rows):
    return (i * tm + lax.broadcasted_iota(jnp.int32, (rows, 1), 0) + 1).astype(F32)


def mixer_fwd(name, u, pool_w, pool_scale, conv_w, conv_b, cn_g, cn_b):
    s = u.shape[0]
    dp = 512
    tm = min(256, s // 4)
    h = CONV_HALO

    def body(a_c, a_p, bv_c, bv_p, bg_c, bg_p, pw_ref, ps_ref, cw_ref, cb_ref, cg_ref, cbt_ref,
             cat_ref, d_ref, e_ref, glu_ref, hh_ref, rs_ref, ext_a, ext_g):
        i = pl.program_id(0)
        first = i == 0
        ext_a[0:h, :] = jnp.where(first, 0.0, a_p[...])
        ext_a[h:, :] = a_c[...]
        ext_g[0:h, :] = jnp.where(first, 0.0, bv_p[...] * _sig(bg_p[...]))
        glu = bv_c[...] * _sig(bg_c[...])
        ext_g[h:, :] = glu
        glu_ref[...] = glu
        pos = _tile_pos(i, tm, tm)
        for gi, w in enumerate(POOL_WINDOWS):
            cs = slice(gi * POOL_GROUP, (gi + 1) * POOL_GROUP)
            a_g = ext_a[pl.ds(h, tm), cs]
            acc = a_g
            for sh in range(1, w):
                acc = acc + ext_a[pl.ds(h - sh, tm), cs]
            d_g = acc / jnp.minimum(pos, float(w)) - a_g
            d_ref[:, cs] = d_g.astype(BF16)
            e_g = _dot(d_g, pw_ref[gi], "nn")
            e_ref[:, cs] = e_g
            cat_ref[:, cs] = (e_g * ps_ref[:, cs]).astype(BF16)
        hcv = jnp.broadcast_to(cb_ref[...], (tm, dp))
        for sh in range(CONV_K):
            hcv = hcv + ext_g[pl.ds(h - sh, tm), :] * cw_ref[pl.ds(CONV_K - 1 - sh, 1), :]
        hhat, rstd = _ln_stats(hcv)
        hl = hhat * cg_ref[...] + cbt_ref[...]
        cat_ref[:, dp:] = (hl * _sig(hl)).astype(BF16)
        hh_ref[...] = hhat
        rs_ref[...] = jnp.broadcast_to(rstd, rs_ref.shape)

    specs = [_row(tm, dp, 0), _prev(tm, h, dp, 0), _row(tm, dp, 1), _prev(tm, h, dp, 1),
             _row(tm, dp, 2), _prev(tm, h, dp, 2),
             _full((4, POOL_GROUP, POOL_GROUP)), _full((1, dp)), _full((CONV_K, dp)),
             _full((1, dp)), _full((1, dp)), _full((1, dp))]
    out_specs = [_row(tm, 2 * dp), _row(tm, dp), _row(tm, dp), _row(tm, dp), _row(tm, dp), _row(tm, LANES)]
    out_shape = [jax.ShapeDtypeStruct((s, 2 * dp), BF16), jax.ShapeDtypeStruct((s, dp), BF16),
                 jax.ShapeDtypeStruct((s, dp), F32), jax.ShapeDtypeStruct((s, dp), F32),
                 jax.ShapeDtypeStruct((s, dp), F32), jax.ShapeDtypeStruct((s, LANES), F32)]
    return pl.pallas_call(
        body, name=name, grid=(s // tm,), in_specs=specs, out_specs=out_specs, out_shape=out_shape,
        scratch_shapes=[pltpu.VMEM((h + tm, dp), F32), pltpu.VMEM((h + tm, dp), F32)],
        compiler_params=_cp(),
    )(u, u, u, u, u, u, pool_w, pool_scale, conv_w, conv_b, cn_g, cn_b)


def mixer_bwd(name, dcat, u, d_sv, e_sv, glu_sv, hh_sv, rs_sv, pool_w, pool_scale, conv_w, cn_g, cn_b):
    s = u.shape[0]
    dp = 512
    tm = min(256, s // 4)
    h = CONV_HALO
    nt = s // tm

    def body(dc_c, dc_n, bv_c, bg_c, d_c, e_c, gl_c, gl_p, hh_c, hh_n, rs_c, rs_n,
             pw_ref, ps_ref, cw_ref, cg_ref, cbt_ref,
             du_ref, dpw_ref, dps_ref, dcw_ref, dcb_ref, dcg_ref, dcbt_ref,
             ext_dh, ext_g, ext_r):
        i = pl.program_id(0)
        first = i == 0
        last = i == nt - 1
        cg = cg_ref[...]

        def conv_grads(dyb, hhat, rstd):
            hl = hhat * cg + cbt_ref[...]
            sg = _sig(hl)
            dhl = dyb * (sg * (1.0 + hl * (1.0 - sg)))
            return _ln_bwd(dhl * cg, hhat, rstd), dhl

        hh_cur = hh_c[...]
        dh_c, dhl_c = conv_grads(dc_c[:, dp:], hh_cur, rs_c[:, 0:1])
        dh_n, _ = conv_grads(dc_n[:, dp:], hh_n[...], rs_n[:, 0:1])
        ext_dh[0:tm, :] = dh_c
        ext_dh[tm:, :] = jnp.where(last, 0.0, dh_n)
        ext_g[0:h, :] = jnp.where(first, 0.0, gl_p[...])
        ext_g[h:, :] = gl_c[...]
        dglu = jnp.zeros((tm, dp), F32)
        for sh in range(CONV_K):
            dglu = dglu + ext_dh[pl.ds(sh, tm), :] * cw_ref[pl.ds(CONV_K - 1 - sh, 1), :]

        @pl.when(first)
        def _():
            dcw_ref[...] = jnp.zeros(dcw_ref.shape, F32)

        for sh in range(CONV_K):
            dcw_ref[pl.ds(CONV_K - 1 - sh, 1), :] += _colsum(dh_c * ext_g[pl.ds(h - sh, tm), :])
        _acc_add(dcb_ref, first, _colsum(dh_c))
        _acc_add(dcg_ref, first, _colsum(dhl_c * hh_cur))
        _acc_add(dcbt_ref, first, _colsum(dhl_c))
        sgate = _sig(bg_c[...])
        bv = bv_c[...]
        du_ref[:, dp:2 * dp] = dglu * sgate
        du_ref[:, 2 * dp:] = dglu * bv * sgate * (1.0 - sgate)

        pos_c = _tile_pos(i, tm, tm)
        pos_n = _tile_pos(i + 1, tm, h)
        _acc_add(dps_ref, first, _colsum(dc_c[:, :dp] * e_c[...]))
        for gi, w in enumerate(POOL_WINDOWS):
            cs = slice(gi * POOL_GROUP, (gi + 1) * POOL_GROUP)
            pw = pw_ref[gi]
            de_c = dc_c[:, cs] * ps_ref[:, cs]
            de_n = dc_n[:, cs] * ps_ref[:, cs]
            dd_c = _dot(de_c, pw, "nt")
            dd_n = _dot(de_n, pw, "nt")
            ext_r[0:tm, :] = dd_c / jnp.minimum(pos_c, float(w))
            ext_r[tm:, :] = jnp.where(last, 0.0, dd_n / jnp.minimum(pos_n, float(w)))
            acc = -dd_c
            for sh in range(w):
                acc = acc + ext_r[pl.ds(sh, tm), :]
            du_ref[:, cs] = acc
            dpw_g = _dot(d_c[:, cs], de_c, "tn")

            @pl.when(first)
            def _():
                dpw_ref[gi] = dpw_g

            @pl.when(jnp.logical_not(first))
            def _():
                dpw_ref[gi] += dpw_g

    specs = [_row(tm, 2 * dp), _next(tm, h, 2 * dp, s), _row(tm, dp, 1), _row(tm, dp, 2),
             _row(tm, dp), _row(tm, dp), _row(tm, dp), _prev(tm, h, dp),
             _row(tm, dp), _next(tm, h, dp, s), _row(tm, LANES), _next(tm, h, LANES, s),
             _full((4, POOL_GROUP, POOL_GROUP)), _full((1, dp)), _full((CONV_K, dp)),
             _full((1, dp)), _full((1, dp))]
    out_specs = [_row(tm, 3 * dp), _full((4, POOL_GROUP, POOL_GROUP)), _full((1, dp)), _full((CONV_K, dp)),
                 _full((1, dp)), _full((1, dp)), _full((1, dp))]
    out_shape = [jax.ShapeDtypeStruct((s, 3 * dp), F32),
                 jax.ShapeDtypeStruct((4, POOL_GROUP, POOL_GROUP), F32), jax.ShapeDtypeStruct((1, dp), F32),
                 jax.ShapeDtypeStruct((CONV_K, dp), F32), jax.ShapeDtypeStruct((1, dp), F32),
                 jax.ShapeDtypeStruct((1, dp), F32), jax.ShapeDtypeStruct((1, dp), F32)]
    return pl.pallas_call(
        body, name=name, grid=(nt,), in_specs=specs, out_specs=out_specs, out_shape=out_shape,
        scratch_shapes=[pltpu.VMEM((tm + h, dp), F32), pltpu.VMEM((h + tm, dp), F32),
                        pltpu.VMEM((tm + h, POOL_GROUP), F32)],
        compiler_params=_cp(),
    )(dcat, dcat, u, u, d_sv, e_sv, glu_sv, glu_sv, hh_sv, hh_sv, rs_sv, rs_sv,
      pool_w, pool_scale, conv_w, cn_g, cn_b)


GELU_C = math.sqrt(2.0 / math.pi)


def _gelu_parts(x):
    inner = GELU_C * (x + 0.044715 * x * x * x)
    t = jnp.tanh(inner)
    gelu = 0.5 * x * (1.0 + t)
    dgelu = 0.5 * (1.0 + t) + 0.5 * x * (1.0 - t * t) * GELU_C * (1.0 + 3.0 * 0.044715 * x * x)
    return gelu, dgelu


def ffn_act_fwd(name, gv, dw_w, dw_b):
    s = gv.shape[0]
    dff = gv.shape[1] // 2
    tm = min(128, s // 4)
    h = FFN_HALO

    def body(g_c, g_p, v_c, w_ref, b_ref, hid_ref, ext):
        first = pl.program_id(0) == 0
        ext[0:h, :] = jnp.where(first, 0.0, g_p[...])
        ext[h:, :] = g_c[...]
        gc = jnp.broadcast_to(b_ref[...], (tm, dff))
        for sh in range(FFN_K):
            gc = gc + ext[pl.ds(h - sh, tm), :] * w_ref[pl.ds(FFN_K - 1 - sh, 1), :]
        gelu, _ = _gelu_parts(gc)
        hid_ref[...] = (gelu * v_c[...]).astype(BF16)

    return pl.pallas_call(
        body, name=name, grid=(s // tm,),
        in_specs=[_row(tm, dff, 0), _prev(tm, h, dff, 0), _row(tm, dff, 1), _full((FFN_K, dff)), _full((1, dff))],
        out_specs=_row(tm, dff), out_shape=jax.ShapeDtypeStruct((s, dff), BF16),
        scratch_shapes=[pltpu.VMEM((h + tm, dff), F32)], compiler_params=_cp(),
    )(gv, gv, gv, dw_w, dw_b)


def ffn_act_bwd(name, dhid, gv, dw_w, dw_b):
    s = gv.shape[0]
    dff = gv.shape[1] // 2
    tm = min(128, s // 4)
    h = FFN_HALO
    nt = s // tm

    def body(dh_c, dh_n, g_p, g_c, g_n, v_c, v_n, w_ref, b_ref, dgv_ref, dw_ref, db_ref, ext_g, ext_d):
        i = pl.program_id(0)
        first = i == 0
        last = i == nt - 1
        ext_g[0:h, :] = jnp.where(first, 0.0, g_p[...])
        ext_g[pl.ds(h, tm), :] = g_c[...]
        ext_g[pl.ds(h + tm, h), :] = g_n[...]
        gc = jnp.broadcast_to(b_ref[...], (tm + h, dff))
        for sh in range(FFN_K):
            gc = gc + ext_g[pl.ds(h - sh, tm + h), :] * w_ref[pl.ds(FFN_K - 1 - sh, 1), :]
        gelu, dgelu = _gelu_parts(gc)
        dgc_c = dh_c[...] * v_c[...] * dgelu[0:tm]
        ext_d[0:tm, :] = dgc_c
        ext_d[tm:, :] = jnp.where(last, 0.0, dh_n[...] * v_n[...] * dgelu[tm:])
        dgate = jnp.zeros((tm, dff), F32)
        for sh in range(FFN_K):
            dgate = dgate + ext_d[pl.ds(sh, tm), :] * w_ref[pl.ds(FFN_K - 1 - sh, 1), :]
        dgv_ref[:, :dff] = dgate.astype(BF16)
        dgv_ref[:, dff:] = (dh_c[...] * gelu[0:tm]).astype(BF16)

        @pl.when(first)
        def _():
            dw_ref[...] = jnp.zeros(dw_ref.shape, F32)

        for sh in range(FFN_K):
            dw_ref[pl.ds(FFN_K - 1 - sh, 1), :] += _colsum(dgc_c * ext_g[pl.ds(h - sh, tm), :])
        _acc_add(db_ref, first, _colsum(dgc_c))

    return pl.pallas_call(
        body, name=name, grid=(nt,),
        in_specs=[_row(tm, dff), _next(tm, h, dff, s),
                  _prev(tm, h, dff, 0), _row(tm, dff, 0), _next(tm, h, dff, s, 0),
                  _row(tm, dff, 1), _next(tm, h, dff, s, 1),
                  _full((FFN_K, dff)), _full((1, dff))],
        out_specs=[_row(tm, 2 * dff), _full((FFN_K, dff)), _full((1, dff))],
        out_shape=[jax.ShapeDtypeStruct((s, 2 * dff), BF16), jax.ShapeDtypeStruct((FFN_K, dff), F32),
                   jax.ShapeDtypeStruct((1, dff), F32)],
        scratch_shapes=[pltpu.VMEM((h + tm + h, dff), F32), pltpu.VMEM((tm + h, dff), F32)],
        compiler_params=_cp(),
    )(dhid, dhid, gv, gv, gv, gv, gv, dw_w, dw_b)


def _toeplitz_bias(rel_bias):
    nh = rel_bias.shape[0]
    zero = jnp.zeros((nh, 1), rel_bias.dtype)
    line = jnp.concatenate(
        [zero, jnp.broadcast_to(rel_bias[:, 2 * MAX_REL:], (nh, SHEAR_SAT)),
         jnp.flip(rel_bias[:, 1:2 * MAX_REL], axis=1), zero], axis=1)
    z = jnp.broadcast_to(line[:, None, :], (nh, Q_TILE, SHEAR_W + 1)).reshape(nh, Q_TILE * (SHEAR_W + 1))
    return z[:, :Q_TILE * SHEAR_W].reshape(nh, Q_TILE, SHEAR_W)[:, :, Q_TILE:]


def _shear_for_bias_grad(ds_sum):
    nh = ds_sum.shape[0]
    z = jnp.pad(ds_sum, ((0, 0), (0, 0), (Q_TILE, 0))).reshape(nh, Q_TILE * SHEAR_W)
    return jnp.pad(z, ((0, 0), (0, Q_TILE))).reshape(nh, Q_TILE, SHEAR_W + 1)


def _attn_mask(t):
    row = lax.broadcasted_iota(jnp.int32, (Q_TILE, K_WIN), 0)
    col = lax.broadcasted_iota(jnp.int32, (Q_TILE, K_WIN), 1)
    qc = row // CHUNK
    kc = col // CHUNK
    return (kc >= qc) & (kc <= qc + LEFT_CHUNKS) & (t * Q_TILE + col >= PAD_ROWS)


def _attn_probs(q2, k3, bias, mask, head):
    lane = lax.broadcasted_iota(jnp.int32, q2.shape, 1)
    q_h = jnp.where(lane // HEAD_DIM == head, q2, jnp.zeros_like(q2))
    sc = _dot(q_h, k3, "nt") * (HEAD_DIM ** -0.5) + bias
    sc = jnp.where(mask, sc, NEG_INF)
    m = jnp.max(sc, axis=-1, keepdims=True)
    p = jnp.exp(sc - m)
    return q_h, p / jnp.sum(p, axis=-1, keepdims=True)


def _attn_specs(d_model):
    nq = PAD_ROWS // Q_TILE
    hp_k = d_model // LANES
    specs = [pl.BlockSpec((Q_TILE, LANES), lambda hp, t: (t + nq, hp))]
    for which in (1, 2):
        for j in range(K_WIN // Q_TILE):
            specs.append(pl.BlockSpec((Q_TILE, LANES), lambda hp, t, j=j, which=which: (t + j, which * hp_k + hp)))
    specs.append(pl.BlockSpec((2, Q_TILE, K_WIN), lambda hp, t: (hp, 0, 0)))
    return specs


def attn_fwd(name, qkvp, bias):
    s = qkvp.shape[0] - PAD_ROWS
    d_model = qkvp.shape[1] // 3
    nw = K_WIN // Q_TILE

    def body(q_ref, *refs):
        k_refs, v_refs, b_ref, o_ref = refs[:nw], refs[nw:2 * nw], refs[2 * nw], refs[2 * nw + 1]
        t = pl.program_id(1)
        q2 = q_ref[...]
        k3 = jnp.concatenate([r[...] for r in k_refs], axis=0)
        v3 = jnp.concatenate([r[...] for r in v_refs], axis=0)
        mask = _attn_mask(t)
        outs = []
        for head in range(2):
            _, p = _attn_probs(q2, k3, b_ref[head], mask, head)
            outs.append(_dot(p, v3, "nn"))
        lane = lax.broadcasted_iota(jnp.int32, (Q_TILE, LANES), 1)
        o_ref[...] = jnp.where(lane < HEAD_DIM, outs[0], outs[1]).astype(BF16)

    return pl.pallas_call(
        body, name=name, grid=(d_model // LANES, s // Q_TILE),
        in_specs=_attn_specs(d_model), out_specs=pl.BlockSpec((Q_TILE, LANES), lambda hp, t: (t, hp)),
        out_shape=jax.ShapeDtypeStruct((s, d_model), BF16), compiler_params=_cp(),
    )(qkvp, *([qkvp] * (2 * nw)), bias)


def attn_bwd(name, qkvp, bias, do):
    s = qkvp.shape[0] - PAD_ROWS
    d_model = qkvp.shape[1] // 3
    nw = K_WIN // Q_TILE
    nt = s // Q_TILE
    scale = HEAD_DIM ** -0.5

    def body(q_ref, *refs):
        k_refs, v_refs = refs[:nw], refs[nw:2 * nw]
        b_ref, do_ref, dq_ref, dk_ref, dv_ref, ds_ref, dk_acc, dv_acc = refs[2 * nw:]
        t = pl.program_id(1)
        first = t == 0

        @pl.when(first)
        def _():
            dk_acc[...] = jnp.zeros(dk_acc.shape, F32)
            dv_acc[...] = jnp.zeros(dv_acc.shape, F32)

        q2 = q_ref[...]
        do2 = do_ref[...]
        k3 = jnp.concatenate([r[...] for r in k_refs], axis=0)
        v3 = jnp.concatenate([r[...] for r in v_refs], axis=0)
        mask = _attn_mask(t)
        lane = lax.broadcasted_iota(jnp.int32, (Q_TILE, LANES), 1)
        dqs = []
        dk_win = jnp.zeros((K_WIN, LANES), F32)
        dv_win = jnp.zeros((K_WIN, LANES), F32)
        for head in range(2):
            q_h, p = _attn_probs(q2, k3, b_ref[head], mask, head)
            do_h = jnp.where(lane // HEAD_DIM == head, do2, jnp.zeros_like(do2))
            dp = _dot(do_h, v3, "nt")
            ds = p * (dp - jnp.sum(p * dp, axis=-1, keepdims=True))
            _acc_add(ds_ref.at[head], first, ds)
            dsb = (ds * scale).astype(BF16)
            dqs.append(_dot(dsb, k3, "nn"))
            dk_win = dk_win + _dot(dsb, q_h, "tn")
            dv_win = dv_win + _dot(p, do_h, "tn")
        dq_ref[...] = jnp.where(lane < HEAD_DIM, dqs[0], dqs[1]).astype(BF16)
        start = pl.multiple_of(t * Q_TILE, Q_TILE)
        dk_acc[pl.ds(start, K_WIN), :] += dk_win
        dv_acc[pl.ds(start, K_WIN), :] += dv_win

        @pl.when(t == nt - 1)
        def _():
            dk_ref[...] = dk_acc[pl.ds(PAD_ROWS, s), :].astype(BF16)
            dv_ref[...] = dv_acc[pl.ds(PAD_ROWS, s), :].astype(BF16)

    specs = _attn_specs(d_model) + [pl.BlockSpec((Q_TILE, LANES), lambda hp, t: (t, hp))]
    col_spec = pl.BlockSpec((s, LANES), lambda hp, t: (0, hp))
    return pl.pallas_call(
        body, name=name, grid=(d_model // LANES, nt), in_specs=specs,
        out_specs=[pl.BlockSpec((Q_TILE, LANES), lambda hp, t: (t, hp)), col_spec, col_spec,
                   pl.BlockSpec((2, Q_TILE, K_WIN), lambda hp, t: (hp, 0, 0))],
        out_shape=[jax.ShapeDtypeStruct((s, d_model), BF16)] * 3
        + [jax.ShapeDtypeStruct((N_HEADS, Q_TILE, K_WIN), F32)],
        scratch_shapes=[pltpu.VMEM((PAD_ROWS + s, LANES), F32), pltpu.VMEM((PAD_ROWS + s, LANES), F32)],
        compiler_params=_cp(),
    )(qkvp, *([qkvp] * (2 * nw)), bias, do)


def bias_grad_reduce(name, sheared):
    nh, _, width = sheared.shape

    def body(x_ref, col_ref, sat_ref):
        cols = _colsum(x_ref[...])
        col_ref[...] = cols
        k = lax.broadcasted_iota(jnp.int32, cols.shape, 1)
        tot = jnp.sum(jnp.where((k >= 1) & (k <= SHEAR_SAT), cols, 0.0), axis=-1, keepdims=True)
        sat_ref[...] = jnp.broadcast_to(tot, sat_ref.shape)

    return pl.pallas_call(
        body, name=name, grid=(nh,),
        in_specs=[pl.BlockSpec((None, Q_TILE, width), lambda hh: (hh, 0, 0))],
        out_specs=[pl.BlockSpec((None, 1, width), lambda hh: (hh, 0, 0)),
                   pl.BlockSpec((None, 1, LANES), lambda hh: (hh, 0, 0))],
        out_shape=[jax.ShapeDtypeStruct((nh, 1, width), F32), jax.ShapeDtypeStruct((nh, 1, LANES), F32)],
        compiler_params=_cp(),
    )(sheared)


def _ew_rows(r):
    for cand in (512, 256, 128, 64, 32, 16, 8):
        if r % cand == 0:
            return cand
    return r


def cast_into_gathered(name, w, layer, s_idx):
    r, c = w.shape[-2:]
    tr = _ew_rows(r)

    def body(s_ref, w_ref, o_ref):
        o_ref[...] = w_ref[...].astype(BF16)

    grid_spec = pltpu.PrefetchScalarGridSpec(
        num_scalar_prefetch=1, grid=(r // tr,),
        in_specs=[pl.BlockSpec((None, tr, c), lambda i, s_ref: (layer, i, 0))],
        out_specs=pl.BlockSpec((None, tr, c), lambda i, s_ref: (s_ref[0], i, 0)))
    return pl.pallas_call(
        body, name=name, grid_spec=grid_spec, out_shape=jax.ShapeDtypeStruct((N_SHARD, r, c), BF16),
        compiler_params=_cp(),
    )(s_idx, w)


def adamw(name, w, grads, m, v):
    nl, r, c = w.shape
    tr = _ew_rows(r)

    def body(*refs):
        w_ref, m_ref, v_ref = refs[0], refs[1], refs[2]
        g_refs = refs[3:3 + nl]
        d_ref, nm_ref, nv_ref = refs[3 + nl:]
        layer = pl.program_id(0)
        g = g_refs[0][...]
        for j in range(1, nl):
            g = jnp.where(layer == j, g_refs[j][...], g)
        nm = ADAM_B1 * m_ref[...] + (1.0 - ADAM_B1) * g
        nv = ADAM_B2 * v_ref[...] + (1.0 - ADAM_B2) * (g * g)
        m_hat = nm / ADAM_BC1
        v_hat = nv / ADAM_BC2
        d_ref[...] = -ADAM_LR * (m_hat / (jnp.sqrt(v_hat) + ADAM_EPS) + ADAM_WD * w_ref[...])
        nm_ref[...] = nm
        nv_ref[...] = nv

    p_spec = pl.BlockSpec((None, tr, c), lambda l, i: (l, i, 0))
    g_spec = pl.BlockSpec((tr, c), lambda l, i: (i, 0))
    return pl.pallas_call(
        body, name=name, grid=(nl, r // tr), in_specs=[p_spec] * 3 + [g_spec] * nl, out_specs=[p_spec] * 3,
        out_shape=[jax.ShapeDtypeStruct((nl, r, c), F32)] * 3, compiler_params=_cp(),
    )(w, m, v, *grads)


def sum_blocks(name, gathered, n_blocks):
    r = gathered.shape[0] // n_blocks
    c = gathered.shape[1]
    tr = _ew_rows(r)
    nt = r // tr

    def body(*refs):
        acc = refs[0][...]
        for j in range(1, n_blocks):
            acc = acc + refs[j][...]
        refs[-1][...] = acc

    specs = [pl.BlockSpec((tr, c), lambda i, j=j: (j * nt + i, 0)) for j in range(n_blocks)]
    return pl.pallas_call(
        body, name=name, grid=(nt,), in_specs=specs, out_specs=pl.BlockSpec((tr, c), lambda i: (i, 0)),
        out_shape=jax.ShapeDtypeStruct((r, c), F32), compiler_params=_cp(),
    )(*([gathered] * n_blocks))


def _place():
    return lax.axis_index("x"), lax.axis_index("y"), lax.axis_index("c")


def _other_chips(x, y):
    return [(1 - x, y), (x, 1 - y), (1 - x, 1 - y)]


HBM_SPEC = pl.BlockSpec(memory_space=pltpu.HBM)
SEM_SPEC = pl.BlockSpec(memory_space=pltpu.SEMAPHORE)
ANY_SPEC = pl.BlockSpec(memory_space=pl.ANY)
EFFECT = pltpu.SideEffectType.DATAFLOW_SIDE_EFFECTING


def _in_hbm(a):
    return pltpu.with_memory_space_constraint(a, pltpu.HBM)


def copies_start(name, bufs, plan, n_copies):
    n = len(bufs)

    def body(*refs):
        send, recv = refs[n], refs[n + 1]
        token = refs[2 * n + 2]
        for k, (src, dst, peer, _) in enumerate(plan(refs[:n])):
            pltpu.make_async_remote_copy(
                src_ref=src, dst_ref=dst, send_sem=send.at[k], recv_sem=recv.at[k],
                device_id=peer, device_id_type=MESH).start()
        token[...] = jnp.zeros(token.shape, F32)

    outs = pl.pallas_call(
        body, name=name,
        out_shape=(pltpu.SemaphoreType.DMA((n_copies,)), pltpu.SemaphoreType.DMA((n_copies,)),
                   *[pltpu.HBM(b.shape, b.dtype) for b in bufs], jax.ShapeDtypeStruct((8, LANES), F32)),
        in_specs=[HBM_SPEC] * n,
        out_specs=(SEM_SPEC, SEM_SPEC, *([HBM_SPEC] * n), pl.BlockSpec(memory_space=pltpu.VMEM)),
        input_output_aliases={a: a + 2 for a in range(n)},
        compiler_params=pltpu.CompilerParams(has_side_effects=EFFECT),
    )(*[_in_hbm(b) for b in bufs])
    return outs[0], outs[1], list(outs[2:2 + n]), outs[2 + n]


def copies_wait(name, bufs, send, recv, plan, sem_base, after):
    n = len(bufs)

    def body(*refs):
        send_ref, recv_ref = refs[n], refs[n + 1]
        for k, (src, _, peer, land) in enumerate(plan(refs[:n])):
            cp = pltpu.make_async_remote_copy(
                src_ref=src, dst_ref=land, send_sem=send_ref.at[sem_base + k], recv_sem=recv_ref.at[sem_base + k],
                device_id=peer, device_id_type=MESH)
            cp.wait_send()
            cp.wait_recv()

    outs = pl.pallas_call(
        body, name=name,
        out_shape=tuple(pltpu.HBM(b.shape, b.dtype) for b in bufs),
        in_specs=[HBM_SPEC] * n + [SEM_SPEC, SEM_SPEC, ANY_SPEC], out_specs=tuple([HBM_SPEC] * n),
        input_output_aliases={a: a for a in range(n)},
        compiler_params=pltpu.CompilerParams(has_side_effects=EFFECT),
    )(*bufs, send, recv, after)
    return list(outs)


def gather_plan(refs):
    x, y, c = _place()
    me = 2 * x + y
    return [(buf.at[me], buf.at[me], (cx, cy, c), buf.at[2 * cx + cy])
            for buf in refs for cx, cy in _other_chips(x, y)]


def swap_plan(refs):
    x, y, c = _place()
    n = len(refs) // 2
    out = []
    for g, land in zip(refs[:n], refs[n:]):
        hr = g.shape[1] // 2
        out.append((g.at[:, pl.ds((1 - c) * hr, hr)], land, (x, y, 1 - c), land))
    return out


def owners_plan(refs):
    x, y, c = _place()
    n = len(refs) // 2
    return [(src.at[2 * cx + cy], land.at[j], (cx, cy, c), land.at[j])
            for src, land in zip(refs[:n], refs[n:]) for j, (cx, cy) in enumerate(_other_chips(x, y))]


def join_plan(refs):
    x, y, c = _place()
    out = []
    for buf in refs:
        hr = buf.shape[0] // 2
        mine = buf.at[pl.ds(c * hr, hr)]
        out.append((mine, mine, (x, y, 1 - c), buf.at[pl.ds((1 - c) * hr, hr)]))
    return out


def add_halves(name, grad, landed, c_idx):
    _, r, c = grad.shape
    hr = r // 2
    tr = _ew_rows(hr)
    nt = hr // tr

    def body(c_ref, g_ref, l_ref, o_ref, ob_ref):
        tot = g_ref[...] + l_ref[...]
        o_ref[...] = tot
        ob_ref[...] = tot.astype(BF16)

    blk = pl.BlockSpec((None, tr, c), lambda sh, i, c_ref: (sh, i, 0))
    grid_spec = pltpu.PrefetchScalarGridSpec(
        num_scalar_prefetch=1, grid=(N_SHARD, nt),
        in_specs=[pl.BlockSpec((None, tr, c), lambda sh, i, c_ref: (sh, c_ref[0] * nt + i, 0)), blk],
        out_specs=[blk, blk])
    return pl.pallas_call(
        body, name=name, grid_spec=grid_spec,
        out_shape=[jax.ShapeDtypeStruct((N_SHARD, hr, c), F32), jax.ShapeDtypeStruct((N_SHARD, hr, c), BF16)],
        compiler_params=_cp(),
    )(c_idx, grad, landed)


def add_owned(name, own, landed, sc_idx):
    _, hr, c = own.shape
    tr = _ew_rows(hr)
    nt = hr // tr

    def body(sc_ref, o_ref, l0, l1, l2, out_ref):
        out_ref[...] = ((o_ref[...] + l0[...].astype(F32)) + l1[...].astype(F32)) + l2[...].astype(F32)

    grid_spec = pltpu.PrefetchScalarGridSpec(
        num_scalar_prefetch=1, grid=(nt,),
        in_specs=[pl.BlockSpec((None, tr, c), lambda i, sc_ref: (sc_ref[0], i, 0))]
        + [pl.BlockSpec((None, tr, c), lambda i, sc_ref, j=j: (j, i, 0)) for j in range(3)],
        out_specs=pl.BlockSpec((tr, c), lambda i, sc_ref: (sc_ref[1] * nt + i, 0)))
    return pl.pallas_call(
        body, name=name, grid_spec=grid_spec, out_shape=jax.ShapeDtypeStruct((2 * hr, c), F32),
        compiler_params=_cp(),
    )(sc_idx, own, landed, landed, landed)


def gather_small(name, block):
    m_per, n = block.shape

    def body(x_ref, out_ref, send_sems, recv_sems, local_sem):
        x, y, c = _place()
        me, sibling = (x, y, c), (x, y, 1 - c)
        chips = _other_chips(x, y)

        def rows(px, py, pc):
            return out_ref.at[pl.ds((4 * px + 2 * py + pc) * m_per, m_per), :]

        def copy(k, blk, to, src=None):
            return pltpu.make_async_remote_copy(
                src_ref=rows(*blk) if src is None else src, dst_ref=rows(*blk),
                send_sem=send_sems.at[k], recv_sem=recv_sems.at[k], device_id=to, device_id_type=MESH)

        mine = pltpu.make_async_copy(x_ref, rows(*me), local_sem)
        mine.start()
        first = [copy(0, me, sibling, src=x_ref)]
        first += [copy(1 + j, me, (*chip, c), src=x_ref) for j, chip in enumerate(chips)]
        for cp in first:
            cp.start()
        passed = [copy(4 + j, (*chip, c), sibling) for j, chip in enumerate(chips)]
        for j, chip in enumerate(chips):
            copy(1 + j, (*chip, c), me).wait_recv()
            passed[j].start()
        copy(0, sibling, me).wait_recv()
        for j, chip in enumerate(chips):
            copy(4 + j, (*chip, 1 - c), me).wait_recv()
        for cp in first + passed:
            cp.wait_send()
        mine.wait()

    return pl.pallas_call(
        body, name=name, out_shape=jax.ShapeDtypeStruct((8 * m_per, n), block.dtype),
        in_specs=[pl.BlockSpec(memory_space=pltpu.VMEM)], out_specs=pl.BlockSpec(memory_space=pltpu.VMEM),
        scratch_shapes=[pltpu.SemaphoreType.DMA((7,)), pltpu.SemaphoreType.DMA((7,)), pltpu.SemaphoreType.DMA],
        compiler_params=_cp(),
    )(block)


PACK_QUANTUM = 8 * LANES


def _pack(arrays):
    pieces = []
    for a in arrays:
        flat = a.reshape(-1)
        padded = -(-flat.shape[0] // PACK_QUANTUM) * PACK_QUANTUM
        pieces.append(jnp.pad(flat, (0, padded - flat.shape[0])).reshape(-1, LANES))
    return jnp.concatenate(pieces, axis=0)


def _unpack(packed, shapes):
    out = []
    row = 0
    for shp in shapes:
        size = math.prod(shp)
        rows = -(-size // PACK_QUANTUM) * 8
        out.append(packed[row:row + rows].reshape(-1)[:size].reshape(shp))
        row += rows
    return out


def kernel(x, p, mix_w_in, pool_w, pool_scale, conv_dw_w, conv_dw_b, conv_ln_g, conv_ln_b, mix_w_out, attn_w_qkv, attn_rel_bias, attn_w_o, ln_mix_g, ln_mix_b, ffn_w_up, ffn_dw_w, ffn_dw_b, ffn_w_down, ple_w_proj, ple_w_gate, ple_b_gate, ln_ffn_g, ln_ffn_b, loss_target, m_mix_w_in, m_pool_w, m_pool_scale, m_conv_dw_w, m_conv_dw_b, m_conv_ln_g, m_conv_ln_b, m_mix_w_out, m_attn_w_qkv, m_attn_rel_bias, m_attn_w_o, m_ln_mix_g, m_ln_mix_b, m_ffn_w_up, m_ffn_dw_w, m_ffn_dw_b, m_ffn_w_down, m_ple_w_proj, m_ple_w_gate, m_ple_b_gate, m_ln_ffn_g, m_ln_ffn_b, v_mix_w_in, v_pool_w, v_pool_scale, v_conv_dw_w, v_conv_dw_b, v_conv_ln_g, v_conv_ln_b, v_mix_w_out, v_attn_w_qkv, v_attn_rel_bias, v_attn_w_o, v_ln_mix_g, v_ln_mix_b, v_ffn_w_up, v_ffn_dw_w, v_ffn_dw_b, v_ffn_w_down, v_ple_w_proj, v_ple_w_gate, v_ple_b_gate, v_ln_ffn_g, v_ln_ffn_b):
    xi, yi, ci = _place()
    shard_idx = (2 * xi + yi).astype(jnp.int32)
    s_arr = shard_idx.reshape(1)
    c_arr = ci.astype(jnp.int32).reshape(1)
    sc_arr = jnp.concatenate([s_arr, c_arr])

    x0 = x[0]
    target = loss_target[0]
    seq = x0.shape[0]

    big = [
        ("mix_w_in", mix_w_in, m_mix_w_in, v_mix_w_in, True),
        ("mix_w_out", mix_w_out, m_mix_w_out, v_mix_w_out, False),
        ("attn_w_qkv", attn_w_qkv, m_attn_w_qkv, v_attn_w_qkv, True),
        ("attn_w_o", attn_w_o, m_attn_w_o, v_attn_w_o, False),
        ("ffn_w_up", ffn_w_up, m_ffn_w_up, v_ffn_w_up, True),
        ("ffn_w_down", ffn_w_down, m_ffn_w_down, v_ffn_w_down, False),
        ("ple_w_proj", ple_w_proj, m_ple_w_proj, v_ple_w_proj, True),
        ("ple_w_gate", ple_w_gate, m_ple_w_gate, v_ple_w_gate, False),
    ]
    params = {nm: w for nm, w, _, _, _ in big}
    col_sharded = {nm: cs for nm, _, _, _, cs in big}
    keys = [("mix_w_in", 0), ("mix_w_out", 0), ("ffn_w_up", 0), ("ffn_w_down", 0), ("ple_w_gate", 0),
            ("ple_w_proj", 0), ("attn_w_qkv", 0), ("attn_w_o", 0), ("ffn_w_up", 1), ("ffn_w_down", 1),
            ("ple_w_gate", 1), ("ple_w_proj", 1)]
    shards = [cast_into_gathered(f"cast_{nm}_{layer}", params[nm], layer, s_arr) for nm, layer in keys]
    g_send, g_recv, g_bufs, _ = copies_start("gather_start", shards, gather_plan, 3 * len(keys))
    arrived_w = {}

    def weight(nm, layer, after=None):
        key = (nm, layer)
        if key not in arrived_w:
            a = keys.index(key)
            arrived_w[key] = copies_wait(f"gather_wait_{nm}_{layer}", [g_bufs[a]], g_send, g_recv, gather_plan,
                                         3 * a, after)[0]
        g = arrived_w[key]
        if col_sharded[nm]:
            return g
        return g.reshape(g.shape[0] * g.shape[1], g.shape[2])

    def tie(a, token):
        return a + token[0:1, 0:1].astype(a.dtype)

    class Reducer:
        def __init__(self, tag, group):
            self.tag, self.group, self.stage = tag, group, 0
            self.n = len(group)
            self.result = None

        def advance(self, after):
            tag, n = self.tag, self.n
            if self.stage == 0:
                grads = []
                for key in self.group:
                    g = big_grads[key]
                    grads.append(g if g.ndim == 3 else g.reshape(N_SHARD, g.shape[0] // N_SHARD, g.shape[1]))
                lands = [lax.empty((N_SHARD, g.shape[1] // 2, g.shape[2]), F32) for g in grads]
                self.sems = copies_start(f"swap_start_{tag}", grads + lands, swap_plan, n)
            elif self.stage == 1:
                send, recv, bufs, _ = self.sems
                outs = copies_wait(f"swap_wait_{tag}", bufs, send, recv, swap_plan, 0, after)
                self.own, wire = [], []
                for key, g, ld in zip(self.group, outs[:n], outs[n:]):
                    o, ob = add_halves(f"add_halves_{key[0]}_{key[1]}", g, ld, c_arr)
                    self.own.append(o)
                    wire.append(ob)
                lands = [lax.empty((3,) + w.shape[1:], BF16) for w in wire]
                self.sems = copies_start(f"owners_start_{tag}", wire + lands, owners_plan, 3 * n)
            elif self.stage == 2:
                send, recv, bufs, _ = self.sems
                outs = copies_wait(f"owners_wait_{tag}", bufs, send, recv, owners_plan, 0, after)
                finals = [add_owned(f"add_owned_{key[0]}_{key[1]}", o, ar, sc_arr)
                          for key, o, ar in zip(self.group, self.own, outs[n:])]
                self.sems = copies_start(f"join_start_{tag}", finals, join_plan, n)
            elif self.stage == 3:
                send, recv, bufs, _ = self.sems
                outs = copies_wait(f"join_wait_{tag}", bufs, send, recv, join_plan, 0, after)
                self.result = dict(zip(self.group, outs))
                self.sems = None
            self.stage += 1
            return None if self.sems is None else self.sems[3]

    dw_shapes = [conv_dw_w.shape, ffn_dw_w.shape]
    dw_packed = _pack([conv_dw_w, ffn_dw_w])
    dw_rows = dw_packed.shape[0]
    dw_all = gather_small("gather_dw", dw_packed)
    dw_parts = [_unpack(dw_all[2 * k * dw_rows:(2 * k + 1) * dw_rows], dw_shapes) for k in range(N_SHARD)]
    conv_w_full = jnp.concatenate([pc[0] for pc in dw_parts], axis=2)[0]
    ffn_dw_full = jnp.concatenate([pc[1] for pc in dw_parts], axis=2)

    big_grads = {}
    small_grads = {}

    saved = []
    h_in = x0
    for layer in range(N_LAYERS):
        sv = {"x_in": h_in}
        if layer % 2 == 0:
            u = mm_cols_fwd("mix_in", h_in, weight("mix_w_in", 0, h_in), F32)
            cat, d_sv, e_sv, glu_sv, hh_sv, rs_sv = mixer_fwd(
                "mixer_fwd", u, pool_w[0], pool_scale, conv_w_full, conv_dw_b, conv_ln_g, conv_ln_b)
            mix = mm_rows_fwd("mix_out", cat, weight("mix_w_out", 0, cat))
            sv.update(u=u, cat=cat, d=d_sv, e=e_sv, glu=glu_sv, hh=hh_sv, rs=rs_sv)
        else:
            qkvp = mm_cols_fwd("attn_qkv", h_in, weight("attn_w_qkv", 0, h_in), BF16,
                               pad_blocks=PAD_ROWS // _row_tile(seq))
            bias = _toeplitz_bias(attn_rel_bias[0])
            att = attn_fwd("attn_fwd", qkvp, bias)
            mix = mm_rows_fwd("attn_out", att, weight("attn_w_o", 0, att))
            sv.update(qkvp=qkvp, bias=bias, att=att)
        x1, xh1, rs1 = ln_fwd(f"ln_mix_{layer}", h_in, mix, ln_mix_g[layer:layer + 1], ln_mix_b[layer:layer + 1])
        gv = mm_cols_fwd(f"ffn_up_{layer}", x1, weight("ffn_w_up", layer, x1), F32)
        hid = ffn_act_fwd(f"ffn_act_{layer}", gv, ffn_dw_full[layer], ffn_dw_b[layer:layer + 1])
        ffn = mm_rows_fwd(f"ffn_down_{layer}", hid, weight("ffn_w_down", layer, hid))
        pgl = mm_rows_fwd(f"ple_gate_{layer}", x1, weight("ple_w_gate", layer, ffn))
        pp = mm_cols_fwd(f"ple_proj_{layer}", p[layer, 0], weight("ple_w_proj", layer, pgl), F32)
        bg = ple_b_gate[layer:layer + 1]
        x2, xh2, rs2 = ln_fwd(f"ln_ffn_{layer}", x1, ffn, ln_ffn_g[layer:layer + 1], ln_ffn_b[layer:layer + 1],
                              ple=(pgl, pp, bg))
        sv.update(x1=x1, xh1=xh1, rs1=rs1, gv=gv, hid=hid, pgl=pgl, pp=pp, xh2=xh2, rs2=rs2)
        saved.append(sv)
        h_in = x2

    dy, loss_part = loss_fwd_bwd("loss", h_in, target)

    reducers = []

    def open_group(tag, group):
        reducers.append(Reducer(tag, group))
        return reducers[-1].advance(None)

    def hook(after):
        token = None
        for red in reducers:
            if red.stage < 4:
                tk = red.advance(after)
                if tk is not None:
                    token = tk if token is None else token + tk
        return token

    def tied(a, token):
        return a if token is None else tie(a, token)

    parts = [(1.0, dy)]
    token = None
    for layer in reversed(range(N_LAYERS)):
        sv = saved[layer]
        bg = ple_b_gate[layer:layer + 1]
        if layer == 0:
            token = open_group("layer1", [("attn_w_qkv", 0), ("attn_w_o", 0), ("ffn_w_up", 1), ("ffn_w_down", 1),
                                          ("ple_w_gate", 1), ("ple_w_proj", 1)])
        dz2, dg2, db2, dpp, dpgl, dbg = ln_bwd(
            f"ln_ffn_bwd_{layer}", parts, sv["xh2"], sv["rs2"], tied(ln_ffn_g[layer:layer + 1], token),
            ple=(sv["pgl"], sv["pp"], bg))
        small_grads[("ln_ffn_g", layer)] = dg2
        small_grads[("ln_ffn_b", layer)] = db2
        small_grads[("ple_b_gate", layer)] = dbg
        w_down = weight("ffn_w_down", layer)
        dhid = mm_rows_dx(f"ffn_down_dx_{layer}", dz2, w_down)
        big_grads[("ffn_w_down", layer)] = mm_rows_dw(f"ffn_down_dw_{layer}", sv["hid"], dz2)
        token = hook(big_grads[("ffn_w_down", layer)])
        dgv, ddw, ddb = ffn_act_bwd(f"ffn_act_bwd_{layer}", dhid, sv["gv"], ffn_dw_full[layer],
                                    tied(ffn_dw_b[layer:layer + 1], token))
        small_grads[("ffn_dw_w", layer)] = ddw
        small_grads[("ffn_dw_b", layer)] = ddb
        big_grads[("ffn_w_up", layer)] = mm_cols_dw(f"ffn_up_dw_{layer}", sv["x1"], dgv)
        t_up = mm_cols_dx(f"ffn_up_dx_{layer}", dgv, weight("ffn_w_up", layer))
        token = hook(t_up)
        big_grads[("ple_w_gate", layer)] = mm_rows_dw(f"ple_gate_dw_{layer}", sv["x1"], dpgl)
        t_gate = mm_rows_dx(f"ple_gate_dx_{layer}", dpgl, weight("ple_w_gate", layer))
        big_grads[("ple_w_proj", layer)] = mm_cols_dw(f"ple_proj_dw_{layer}", p[layer, 0], dpp)
        token2 = hook(big_grads[("ple_w_proj", layer)])
        if token2 is not None:
            token = token2 if token is None else token + token2
        if layer == 0:
            token3 = open_group("layer0_ffn", [("ffn_w_up", 0), ("ffn_w_down", 0), ("ple_w_gate", 0), ("ple_w_proj", 0)])
            token = token3 if token is None else token + token3
        dz1, dg1, db1 = ln_bwd(
            f"ln_mix_bwd_{layer}", [(ALPHA, dz2), (1.0, t_up), (1.0, t_gate)], sv["xh1"], sv["rs1"],
            tied(ln_mix_g[layer:layer + 1], token))
        small_grads[("ln_mix_g", layer)] = dg1
        small_grads[("ln_mix_b", layer)] = db1
        if layer % 2 == 0:
            dcat = mm_rows_dx("mix_out_dx", dz1, weight("mix_w_out", 0))
            big_grads[("mix_w_out", 0)] = mm_rows_dw("mix_out_dw", sv["cat"], dz1)
            token = hook(big_grads[("mix_w_out", 0)])
            du, dpw, dps, dcw, dcb, dcg, dcbt = mixer_bwd(
                "mixer_bwd", dcat, sv["u"], sv["d"], sv["e"], sv["glu"], sv["hh"], sv["rs"],
                pool_w[0], pool_scale, conv_w_full, tied(conv_ln_g, token), conv_ln_b)
            small_grads[("pool_w", 0)] = dpw
            small_grads[("pool_scale", 0)] = dps
            small_grads[("conv_dw_w", 0)] = dcw
            small_grads[("conv_dw_b", 0)] = dcb
            small_grads[("conv_ln_g", 0)] = dcg
            small_grads[("conv_ln_b", 0)] = dcbt
            big_grads[("mix_w_in", 0)] = mm_cols_dw("mix_in_dw", sv["x_in"], du)
            hook(big_grads[("mix_w_in", 0)])
            open_group("layer0_mix", [("mix_w_in", 0), ("mix_w_out", 0)])
            t_mix = mm_cols_dx("mix_in_dx", du, weight("mix_w_in", 0))
            hook(t_mix)
        else:
            do = mm_rows_dx("attn_out_dx", dz1, weight("attn_w_o", 0), out_dtype=BF16)
            big_grads[("attn_w_o", 0)] = mm_rows_dw("attn_out_dw", sv["att"], dz1)
            dq, dk, dv, ds_sum = attn_bwd("attn_bwd", sv["qkvp"], sv["bias"], do)
            cols, sat = bias_grad_reduce("bias_grad", _shear_for_bias_grad(ds_sum))
            d_rel = jnp.concatenate(
                [jnp.zeros((N_HEADS, 1), F32),
                 jnp.flip(cols[:, 0, SHEAR_SAT + 1:SHEAR_W], axis=1),
                 sat[:, 0, 0:1]], axis=1)
            small_grads[("attn_rel_bias", 0)] = d_rel
            dqkv = jnp.concatenate([dq, dk, dv], axis=1)
            big_grads[("attn_w_qkv", 0)] = mm_cols_dw("attn_qkv_dw", sv["x_in"], dqkv)
            t_mix = mm_cols_dx("attn_qkv_dx", dqkv, weight("attn_w_qkv", 0))
        parts = [(ALPHA, dz1), (1.0, t_mix)]
    grad_x = scaled_sum("grad_x", parts)
    hook(grad_x)
    hook(grad_x)
    shard_grads = {}
    for red in reducers:
        shard_grads.update(red.result)

    big_out = {}
    for nm, w, m, v, _ in big:
        gl = [shard_grads[(nm, layer)] for layer in range(w.shape[0])]
        delta, new_m, new_v = adamw(f"adamw_{nm}", w, gl, m, v)
        big_out[nm] = (jnp.stack(gl, axis=0), delta, new_m, new_v)

    small = [
        ("pool_w", pool_w, m_pool_w, v_pool_w, None),
        ("pool_scale", pool_scale, m_pool_scale, v_pool_scale, None),
        ("conv_dw_w", conv_dw_w, m_conv_dw_w, v_conv_dw_w, 2),
        ("conv_dw_b", conv_dw_b, m_conv_dw_b, v_conv_dw_b, None),
        ("conv_ln_g", conv_ln_g, m_conv_ln_g, v_conv_ln_g, None),
        ("conv_ln_b", conv_ln_b, m_conv_ln_b, v_conv_ln_b, None),
        ("attn_rel_bias", attn_rel_bias, m_attn_rel_bias, v_attn_rel_bias, None),
        ("ln_mix_g", ln_mix_g, m_ln_mix_g, v_ln_mix_g, None),
        ("ln_mix_b", ln_mix_b, m_ln_mix_b, v_ln_mix_b, None),
        ("ffn_dw_w", ffn_dw_w, m_ffn_dw_w, v_ffn_dw_w, 2),
        ("ffn_dw_b", ffn_dw_b, m_ffn_dw_b, v_ffn_dw_b, None),
        ("ple_b_gate", ple_b_gate, m_ple_b_gate, v_ple_b_gate, None),
        ("ln_ffn_g", ln_ffn_g, m_ln_ffn_g, v_ln_ffn_g, None),
        ("ln_ffn_b", ln_ffn_b, m_ln_ffn_b, v_ln_ffn_b, None),
    ]
    full_grads = []
    for nm, w, _, _, shard_axis in small:
        full = list(w.shape)
        if shard_axis is not None:
            full[shard_axis] *= N_SHARD
        per_layer = [small_grads[(nm, layer)].reshape((1,) + tuple(full[1:])) for layer in range(w.shape[0])]
        full_grads.append(jnp.concatenate(per_layer, axis=0))
    packed = _pack(full_grads + [loss_part])
    total = sum_blocks("sum_small", gather_small("gather_small_grads", packed), 8)
    unpacked = _unpack(total, [g.shape for g in full_grads] + [loss_part.shape])
    loss = unpacked[-1][0, 0]
    local_grads = []
    for (nm, w, _, _, shard_axis), g in zip(small, unpacked[:-1]):
        if shard_axis is not None:
            width = w.shape[shard_axis]
            g = lax.dynamic_slice_in_dim(g, shard_idx * width, width, axis=shard_axis)
        local_grads.append(g.reshape(w.shape))
    shapes = [w.shape for _, w, _, _, _ in small]
    pg = _pack(local_grads)
    pw = _pack([w for _, w, _, _, _ in small])
    pm = _pack([m for _, _, m, _, _ in small])
    pv = _pack([v for _, _, _, v, _ in small])
    delta_s, new_m_s, new_v_s = adamw("adamw_small", pw[None], [pg], pm[None], pv[None])
    small_out = {}
    for (nm, _, _, _, _), g, d_, m_, v_ in zip(
            small, local_grads, _unpack(delta_s[0], shapes), _unpack(new_m_s[0], shapes), _unpack(new_v_s[0], shapes)):
        small_out[nm] = (g, d_, m_, v_)

    order = ["mix_w_in", "pool_w", "pool_scale", "conv_dw_w", "conv_dw_b", "conv_ln_g", "conv_ln_b", "mix_w_out",
             "attn_w_qkv", "attn_rel_bias", "attn_w_o", "ln_mix_g", "ln_mix_b", "ffn_w_up", "ffn_dw_w", "ffn_dw_b",
             "ffn_w_down", "ple_w_proj", "ple_w_gate", "ple_b_gate", "ln_ffn_g", "ln_ffn_b"]
    res = {**big_out, **small_out}
    outs = [loss, grad_x[None]]
    for slot in range(4):
        outs += [res[nm][slot] for nm in order]
    return tuple(outs)
```

```python
import functools
import math

import jax
import jax.numpy as jnp
from jax import lax
from jax.experimental import pallas as pl
from jax.experimental.pallas import tpu as pltpu

F32 = jnp.float32
BF16 = jnp.bfloat16
MESH = pl.DeviceIdType.MESH

N_LAYERS = 2
ALPHA = (2 * N_LAYERS) ** 0.25
LN_EPS = 1e-5
NEG_INF = -1e30
CHUNK = 64
LEFT_CHUNKS = 8
PAD_ROWS = LEFT_CHUNKS * CHUNK
HEAD_DIM = 64
N_HEADS = 16
MAX_REL = 256
POOL_WINDOWS = (2, 4, 8, 16)
POOL_GROUP = 128
CONV_K = 31
FFN_K = 3
CONV_HALO = 32
FFN_HALO = 8
FFN_TILE = 256
FFN_CHUNK_ROWS = 32
FFN_CHUNK_LANES = 256
Q_TILE = 256
K_WIN = Q_TILE + PAD_ROWS
SHEAR_W = Q_TILE + K_WIN
SHEAR_SAT = SHEAR_W - 2 * MAX_REL
N_SHARD = 4
LANES = 128

ADAM_LR = 0.001
ADAM_B1 = 0.9
ADAM_B2 = 0.999
ADAM_EPS = 1e-08
ADAM_WD = 0.01
ADAM_STEP = 10
ADAM_BC1 = 1.0 - ADAM_B1 ** ADAM_STEP
ADAM_BC2 = 1.0 - ADAM_B2 ** ADAM_STEP

DIMS = {
    "nn": (((1,), (0,)), ((), ())),
    "nt": (((1,), (1,)), ((), ())),
    "tn": (((0,), (0,)), ((), ())),
}


def _cp(vmem_mb=48, **kw):
    return pltpu.CompilerParams(vmem_limit_bytes=vmem_mb * 1024 * 1024, **kw)


def _dot(a, b, mode):
    return lax.dot_general(a.astype(BF16), b.astype(BF16), DIMS[mode], preferred_element_type=F32)


def _sig(x):
    return 1.0 / (1.0 + jnp.exp(-x))


def _row_tile(s):
    return min(512, s // 4)


def _mm_tile(s):
    return min(1024, s // 4)


def _mm(name, mode, a, b, in_specs, out_shape, out_spec, acc_shape, grid, nk, zero_first=False, vmem_mb=48):
    out_f32 = out_shape.dtype == F32

    def body(a_ref, b_ref, o_ref, *scr):
        k = pl.program_id(2)

        def compute():
            part = _dot(a_ref[...], b_ref[...], mode)
            if nk == 1:
                o_ref[...] = part.astype(o_ref.dtype)
                return
            acc = o_ref if out_f32 else scr[0]

            @pl.when(k == 0)
            def _():
                acc[...] = part

            @pl.when(k > 0)
            def _():
                acc[...] += part

            if not out_f32:
                @pl.when(k == nk - 1)
                def _():
                    o_ref[...] = acc[...].astype(o_ref.dtype)

        if zero_first:
            @pl.when(pl.program_id(1) == 0)
            def _():
                o_ref[...] = jnp.zeros(o_ref.shape, o_ref.dtype)

            pl.when(pl.program_id(1) > 0)(compute)
        else:
            compute()

    scratch = [] if (nk == 1 or out_f32) else [pltpu.VMEM(acc_shape, F32)]
    return pl.pallas_call(
        body, name=name, grid=grid, in_specs=in_specs, out_specs=out_spec, out_shape=out_shape,
        scratch_shapes=scratch, compiler_params=_cp(vmem_mb),
    )(a, b)


def mm_cols_fwd(name, a, wc, out_dtype, pad_blocks=0):
    s, k = a.shape
    n4 = wc.shape[2]
    tm = _row_tile(s) if pad_blocks else _mm_tile(s)
    nt = s // tm
    return _mm(
        name, "nn", a, wc,
        [pl.BlockSpec((tm, k), lambda j, i, r: (jnp.maximum(i - pad_blocks, 0), 0)),
         pl.BlockSpec((None, k, n4), lambda j, i, r: (j, 0, 0))],
        jax.ShapeDtypeStruct((s + pad_blocks * tm, N_SHARD * n4), out_dtype),
        pl.BlockSpec((tm, n4), lambda j, i, r: (i, j)),
        None, (N_SHARD, nt + pad_blocks, 1), 1, zero_first=pad_blocks > 0)


def mm_cols_dx(name, dy, wc):
    s = dy.shape[0]
    _, k, n4 = wc.shape
    tm = _mm_tile(s)
    return _mm(
        name, "nt", dy, wc,
        [pl.BlockSpec((tm, n4), lambda g, i, r: (i, r)),
         pl.BlockSpec((None, k, n4), lambda g, i, r: (r, 0, 0))],
        jax.ShapeDtypeStruct((s, k), F32),
        pl.BlockSpec((tm, k), lambda g, i, r: (i, 0)),
        (tm, k), (1, s // tm, N_SHARD), N_SHARD)


def mm_cols_dw(name, a, dy):
    s, k = a.shape
    n4 = dy.shape[1] // N_SHARD
    tm = _mm_tile(s)
    nt = s // tm
    return _mm(
        name, "tn", a, dy,
        [pl.BlockSpec((tm, k), lambda j, g, r: (r, 0)),
         pl.BlockSpec((tm, n4), lambda j, g, r: (r, j))],
        jax.ShapeDtypeStruct((N_SHARD, k, n4), F32),
        pl.BlockSpec((None, k, n4), lambda j, g, r: (j, 0, 0)),
        (k, n4), (N_SHARD, 1, nt), nt)


def _k_tile(k):
    return k if k <= 1024 else k // 2


def mm_rows_fwd(name, a, wr, out_dtype=F32):
    s, k = a.shape
    n = wr.shape[1]
    tm = _mm_tile(s)
    tk = _k_tile(k)
    nk = k // tk
    return _mm(
        name, "nn", a, wr,
        [pl.BlockSpec((tm, tk), lambda g, i, r: (i, r)),
         pl.BlockSpec((tk, n), lambda g, i, r: (r, 0))],
        jax.ShapeDtypeStruct((s, n), out_dtype),
        pl.BlockSpec((tm, n), lambda g, i, r: (i, 0)),
        (tm, n), (1, s // tm, nk), nk)


def mm_rows_dx(name, dy, wr, out_dtype=F32):
    s, n = dy.shape
    k = wr.shape[0]
    tm = _mm_tile(s)
    tk = _k_tile(k)
    return _mm(
        name, "nt", dy, wr,
        [pl.BlockSpec((tm, n), lambda j, i, r: (i, 0)),
         pl.BlockSpec((tk, n), lambda j, i, r: (j, 0))],
        jax.ShapeDtypeStruct((s, k), out_dtype),
        pl.BlockSpec((tm, tk), lambda j, i, r: (i, j)),
        None, (k // tk, s // tm, 1), 1)


def mm_rows_dw(name, a, dy):
    s, k = a.shape
    n = dy.shape[1]
    tm = _mm_tile(s)
    tk = _k_tile(k)
    nt = s // tm
    return _mm(
        name, "tn", a, dy,
        [pl.BlockSpec((tm, tk), lambda j, g, r: (r, j)),
         pl.BlockSpec((tm, n), lambda j, g, r: (r, 0))],
        jax.ShapeDtypeStruct((k, n), F32),
        pl.BlockSpec((tk, n), lambda j, g, r: (j, 0)),
        (tk, n), (k // tk, 1, nt), nt)


def _row(tm, c, col=0):
    return pl.BlockSpec((tm, c), lambda i: (i, col))


def _full(shape):
    nd = len(shape)
    return pl.BlockSpec(shape, lambda i: (0,) * nd)


def _prev(tm, h, c, col=0):
    return pl.BlockSpec((h, c), lambda i: (jnp.maximum(i * (tm // h) - 1, 0), col))


def _next(tm, h, c, s, col=0):
    return pl.BlockSpec((h, c), lambda i: (jnp.minimum((i + 1) * (tm // h), s // h - 1), col))


def _acc_add(ref, first, val):
    @pl.when(first)
    def _():
        ref[...] = val

    @pl.when(jnp.logical_not(first))
    def _():
        ref[...] += val


def _colsum(v):
    return jnp.sum(v, axis=0, keepdims=True)


def _ln_stats(z):
    mu = jnp.mean(z, axis=-1, keepdims=True)
    zc = z - mu
    var = jnp.mean(zc * zc, axis=-1, keepdims=True)
    rstd = lax.rsqrt(var + LN_EPS)
    return zc * rstd, rstd


def _ln_bwd(dxhat, xhat, rstd):
    m1 = jnp.mean(dxhat, axis=-1, keepdims=True)
    m2 = jnp.mean(dxhat * xhat, axis=-1, keepdims=True)
    return rstd * (dxhat - m1 - xhat * m2)


def ln_fwd(name, x, f, g, b, ple=None):
    s, d = x.shape
    tm = _row_tile(s)
    n_in = 2 + (3 if ple is not None else 0)

    def body(*refs):
        x_ref, f_ref = refs[0], refs[1]
        g_ref, b_ref = refs[n_in], refs[n_in + 1]
        y_ref, xh_ref, rs_ref = refs[n_in + 2:]
        z = ALPHA * x_ref[...] + f_ref[...]
        if ple is not None:
            pgl_ref, pp_ref, bg_ref = refs[2:5]
            z = z + _sig(pgl_ref[...] + bg_ref[...]) * pp_ref[...]
        xhat, rstd = _ln_stats(z)
        y_ref[...] = xhat * g_ref[...] + b_ref[...]
        xh_ref[...] = xhat
        rs_ref[...] = jnp.broadcast_to(rstd, rs_ref.shape)

    ins = [x, f]
    specs = [_row(tm, d), _row(tm, d)]
    if ple is not None:
        pgl, pp, bg = ple
        ins += [pgl, pp, bg]
        specs += [_row(tm, d), _row(tm, d), _full((1, d))]
    ins += [g, b]
    specs += [_full((1, d)), _full((1, d))]
    return pl.pallas_call(
        body, name=name, grid=(s // tm,), in_specs=specs,
        out_specs=[_row(tm, d), _row(tm, d), _row(tm, LANES)],
        out_shape=[jax.ShapeDtypeStruct((s, d), F32), jax.ShapeDtypeStruct((s, d), F32),
                   jax.ShapeDtypeStruct((s, LANES), F32)],
        compiler_params=_cp(),
    )(*ins)


def ln_bwd(name, parts, xhat, rstd, g, ple=None):
    s, d = xhat.shape
    tm = _row_tile(s)
    coefs = [c for c, _ in parts]
    n_p = len(parts)
    n_in = n_p + 3 + (3 if ple is not None else 0)

    def body(*refs):
        first = pl.program_id(0) == 0
        dy = coefs[0] * refs[0][...].astype(F32)
        for j in range(1, n_p):
            dy = dy + coefs[j] * refs[j][...].astype(F32)
        xh = refs[n_p][...]
        rs = refs[n_p + 1][:, 0:1]
        g_v = refs[n_p + 2][...]
        outs = refs[n_in:]
        dz = _ln_bwd(dy * g_v, xh, rs)
        outs[0][...] = dz
        _acc_add(outs[1], first, _colsum(dy * xh))
        _acc_add(outs[2], first, _colsum(dy))
        if ple is not None:
            pgl_ref, pp_ref, bg_ref = refs[n_p + 3:n_p + 6]
            pg = _sig(pgl_ref[...] + bg_ref[...])
            dpgl = dz * pp_ref[...] * pg * (1.0 - pg)
            outs[3][...] = (dz * pg).astype(BF16)
            outs[4][...] = dpgl.astype(BF16)
            _acc_add(outs[5], first, _colsum(dpgl))

    ins = [p for _, p in parts] + [xhat, rstd, g]
    specs = [_row(tm, d)] * n_p + [_row(tm, d), _row(tm, LANES), _full((1, d))]
    out_specs = [_row(tm, d), _full((1, d)), _full((1, d))]
    out_shape = [jax.ShapeDtypeStruct((s, d), F32), jax.ShapeDtypeStruct((1, d), F32),
                 jax.ShapeDtypeStruct((1, d), F32)]
    if ple is not None:
        pgl, pp, bg = ple
        ins += [pgl, pp, bg]
        specs += [_row(tm, d), _row(tm, d), _full((1, d))]
        out_specs += [_row(tm, d), _row(tm, d), _full((1, d))]
        out_shape += [jax.ShapeDtypeStruct((s, d), BF16), jax.ShapeDtypeStruct((s, d), BF16),
                      jax.ShapeDtypeStruct((1, d), F32)]
    return pl.pallas_call(
        body, name=name, grid=(s // tm,), in_specs=specs, out_specs=out_specs, out_shape=out_shape,
        compiler_params=_cp(),
    )(*ins)


def loss_fwd_bwd(name, y, target):
    s, d = y.shape
    tm = _row_tile(s)

    def body(y_ref, t_ref, dy_ref, l_ref):
        first = pl.program_id(0) == 0
        err = y_ref[...] - t_ref[...]
        dy_ref[...] = err * (1.0 / d)
        part = 0.5 * jnp.sum(jnp.mean(err * err, axis=-1, keepdims=True), axis=0, keepdims=True)
        _acc_add(l_ref, first, jnp.broadcast_to(part, l_ref.shape))

    return pl.pallas_call(
        body, name=name, grid=(s // tm,), in_specs=[_row(tm, d), _row(tm, d)],
        out_specs=[_row(tm, d), _full((8, LANES))],
        out_shape=[jax.ShapeDtypeStruct((s, d), F32), jax.ShapeDtypeStruct((8, LANES), F32)],
        compiler_params=_cp(),
    )(y, target)


def scaled_sum(name, parts):
    s, d = parts[0][1].shape
    tm = _row_tile(s)
    coefs = [c for c, _ in parts]

    def body(*refs):
        acc = coefs[0] * refs[0][...].astype(F32)
        for j in range(1, len(coefs)):
            acc = acc + coefs[j] * refs[j][...].astype(F32)
        refs[-1][...] = acc

    return pl.pallas_call(
        body, name=name, grid=(s // tm,), in_specs=[_row(tm, d)] * len(parts), out_specs=_row(tm, d),
        out_shape=jax.ShapeDtypeStruct((s, d), F32), compiler_params=_cp(),
    )(*[p for _, p in parts])


def _tile_pos(i, tm, rows):
    return (i * tm + lax.broadcasted_iota(jnp.int32, (rows, 1), 0) + 1).astype(F32)


def mixer_fwd(name, u, pool_w, pool_scale, conv_w, conv_b, cn_g, cn_b):
    s = u.shape[0]
    dp = 512
    tm = min(256, s // 4)
    h = CONV_HALO

    def body(a_c, a_p, bv_c, bv_p, bg_c, bg_p, pw_ref, ps_ref, cw_ref, cb_ref, cg_ref, cbt_ref,
             cat_ref, d_ref, e_ref, glu_ref, hh_ref, rs_ref, ext_a, ext_g):
        i = pl.program_id(0)
        first = i == 0
        ext_a[0:h, :] = jnp.where(first, 0.0, a_p[...])
        ext_a[h:, :] = a_c[...]
        ext_g[0:h, :] = jnp.where(first, 0.0, bv_p[...] * _sig(bg_p[...]))
        glu = bv_c[...] * _sig(bg_c[...])
        ext_g[h:, :] = glu
        glu_ref[...] = glu
        pos = _tile_pos(i, tm, tm)
        for gi, w in enumerate(POOL_WINDOWS):
            cs = slice(gi * POOL_GROUP, (gi + 1) * POOL_GROUP)
            a_g = ext_a[pl.ds(h, tm), cs]
            acc = a_g
            for sh in range(1, w):
                acc = acc + ext_a[pl.ds(h - sh, tm), cs]
            d_g = acc / jnp.minimum(pos, float(w)) - a_g
            d_ref[:, cs] = d_g.astype(BF16)
            e_g = _dot(d_g, pw_ref[gi], "nn")
            e_ref[:, cs] = e_g
            cat_ref[:, cs] = (e_g * ps_ref[:, cs]).astype(BF16)
        hcv = jnp.broadcast_to(cb_ref[...], (tm, dp))
        for sh in range(CONV_K):
            hcv = hcv + ext_g[pl.ds(h - sh, tm), :] * cw_ref[pl.ds(CONV_K - 1 - sh, 1), :]
        hhat, rstd = _ln_stats(hcv)
        hl = hhat * cg_ref[...] + cbt_ref[...]
        cat_ref[:, dp:] = (hl * _sig(hl)).astype(BF16)
        hh_ref[...] = hhat
        rs_ref[...] = jnp.broadcast_to(rstd, rs_ref.shape)

    specs = [_row(tm, dp, 0), _prev(tm, h, dp, 0), _row(tm, dp, 1), _prev(tm, h, dp, 1),
             _row(tm, dp, 2), _prev(tm, h, dp, 2),
             _full((4, POOL_GROUP, POOL_GROUP)), _full((1, dp)), _full((CONV_K, dp)),
             _full((1, dp)), _full((1, dp)), _full((1, dp))]
    out_specs = [_row(tm, 2 * dp), _row(tm, dp), _row(tm, dp), _row(tm, dp), _row(tm, dp), _row(tm, LANES)]
    out_shape = [jax.ShapeDtypeStruct((s, 2 * dp), BF16), jax.ShapeDtypeStruct((s, dp), BF16),
                 jax.ShapeDtypeStruct((s, dp), F32), jax.ShapeDtypeStruct((s, dp), F32),
                 jax.ShapeDtypeStruct((s, dp), F32), jax.ShapeDtypeStruct((s, LANES), F32)]
    return pl.pallas_call(
        body, name=name, grid=(s // tm,), in_specs=specs, out_specs=out_specs, out_shape=out_shape,
        scratch_shapes=[pltpu.VMEM((h + tm, dp), F32), pltpu.VMEM((h + tm, dp), F32)],
        compiler_params=_cp(),
    )(u, u, u, u, u, u, pool_w, pool_scale, conv_w, conv_b, cn_g, cn_b)


def mixer_bwd(name, dcat, u, d_sv, e_sv, glu_sv, hh_sv, rs_sv, pool_w, pool_scale, conv_w, cn_g, cn_b):
    s = u.shape[0]
    dp = 512
    tm = min(256, s // 4)
    h = CONV_HALO
    nt = s // tm

    def body(dc_c, dc_n, bv_c, bg_c, d_c, e_c, gl_c, gl_p, hh_c, hh_n, rs_c, rs_n,
             pw_ref, ps_ref, cw_ref, cg_ref, cbt_ref,
             du_ref, dpw_ref, dps_ref, dcw_ref, dcb_ref, dcg_ref, dcbt_ref,
             ext_dh, ext_g, ext_r):
        i = pl.program_id(0)
        first = i == 0
        last = i == nt - 1
        cg = cg_ref[...]

        def conv_grads(dyb, hhat, rstd):
            hl = hhat * cg + cbt_ref[...]
            sg = _sig(hl)
            dhl = dyb * (sg * (1.0 + hl * (1.0 - sg)))
            return _ln_bwd(dhl * cg, hhat, rstd), dhl

        hh_cur = hh_c[...]
        dh_c, dhl_c = conv_grads(dc_c[:, dp:], hh_cur, rs_c[:, 0:1])
        dh_n, _ = conv_grads(dc_n[:, dp:], hh_n[...], rs_n[:, 0:1])
        ext_dh[0:tm, :] = dh_c
        ext_dh[tm:, :] = jnp.where(last, 0.0, dh_n)
        ext_g[0:h, :] = jnp.where(first, 0.0, gl_p[...])
        ext_g[h:, :] = gl_c[...]
        dglu = jnp.zeros((tm, dp), F32)
        for sh in range(CONV_K):
            dglu = dglu + ext_dh[pl.ds(sh, tm), :] * cw_ref[pl.ds(CONV_K - 1 - sh, 1), :]

        @pl.when(first)
        def _():
            dcw_ref[...] = jnp.zeros(dcw_ref.shape, F32)

        for sh in range(CONV_K):
            dcw_ref[pl.ds(CONV_K - 1 - sh, 1), :] += _colsum(dh_c * ext_g[pl.ds(h - sh, tm), :])
        _acc_add(dcb_ref, first, _colsum(dh_c))
        _acc_add(dcg_ref, first, _colsum(dhl_c * hh_cur))
        _acc_add(dcbt_ref, first, _colsum(dhl_c))
        sgate = _sig(bg_c[...])
        bv = bv_c[...]
        du_ref[:, dp:2 * dp] = dglu * sgate
        du_ref[:, 2 * dp:] = dglu * bv * sgate * (1.0 - sgate)

        pos_c = _tile_pos(i, tm, tm)
        pos_n = _tile_pos(i + 1, tm, h)
        _acc_add(dps_ref, first, _colsum(dc_c[:, :dp] * e_c[...]))
        for gi, w in enumerate(POOL_WINDOWS):
            cs = slice(gi * POOL_GROUP, (gi + 1) * POOL_GROUP)
            pw = pw_ref[gi]
            de_c = dc_c[:, cs] * ps_ref[:, cs]
            de_n = dc_n[:, cs] * ps_ref[:, cs]
            dd_c = _dot(de_c, pw, "nt")
            dd_n = _dot(de_n, pw, "nt")
            ext_r[0:tm, :] = dd_c / jnp.minimum(pos_c, float(w))
            ext_r[tm:, :] = jnp.where(last, 0.0, dd_n / jnp.minimum(pos_n, float(w)))
            acc = -dd_c
            for sh in range(w):
                acc = acc + ext_r[pl.ds(sh, tm), :]
            du_ref[:, cs] = acc
            dpw_g = _dot(d_c[:, cs], de_c, "tn")

            @pl.when(first)
            def _():
                dpw_ref[gi] = dpw_g

            @pl.when(jnp.logical_not(first))
            def _():
                dpw_ref[gi] += dpw_g

    specs = [_row(tm, 2 * dp), _next(tm, h, 2 * dp, s), _row(tm, dp, 1), _row(tm, dp, 2),
             _row(tm, dp), _row(tm, dp), _row(tm, dp), _prev(tm, h, dp),
             _row(tm, dp), _next(tm, h, dp, s), _row(tm, LANES), _next(tm, h, LANES, s),
             _full((4, POOL_GROUP, POOL_GROUP)), _full((1, dp)), _full((CONV_K, dp)),
             _full((1, dp)), _full((1, dp))]
    out_specs = [_row(tm, 3 * dp), _full((4, POOL_GROUP, POOL_GROUP)), _full((1, dp)), _full((CONV_K, dp)),
                 _full((1, dp)), _full((1, dp)), _full((1, dp))]
    out_shape = [jax.ShapeDtypeStruct((s, 3 * dp), F32),
                 jax.ShapeDtypeStruct((4, POOL_GROUP, POOL_GROUP), F32), jax.ShapeDtypeStruct((1, dp), F32),
                 jax.ShapeDtypeStruct((CONV_K, dp), F32), jax.ShapeDtypeStruct((1, dp), F32),
                 jax.ShapeDtypeStruct((1, dp), F32), jax.ShapeDtypeStruct((1, dp), F32)]
    return pl.pallas_call(
        body, name=name, grid=(nt,), in_specs=specs, out_specs=out_specs, out_shape=out_shape,
        scratch_shapes=[pltpu.VMEM((tm + h, dp), F32), pltpu.VMEM((h + tm, dp), F32),
                        pltpu.VMEM((tm + h, POOL_GROUP), F32)],
        compiler_params=_cp(),
    )(dcat, dcat, u, u, d_sv, e_sv, glu_sv, glu_sv, hh_sv, hh_sv, rs_sv, rs_sv,
      pool_w, pool_scale, conv_w, cn_g, cn_b)


GELU_C = math.sqrt(2.0 / math.pi)


def _gelu_parts(x):
    x2 = x * x
    t = jnp.tanh(x * (GELU_C + (GELU_C * 0.044715) * x2))
    half_1pt = 0.5 + 0.5 * t
    gelu = x * half_1pt
    dgelu = half_1pt + (0.5 * x) * (1.0 - t * t) * (GELU_C + (3.0 * GELU_C * 0.044715) * x2)
    return gelu, dgelu


def ffn_act_fwd(name, gv, dw_w, dw_b):
    s = gv.shape[0]
    dff = gv.shape[1] // 2
    tm = min(FFN_TILE, s // 4)
    h = FFN_HALO
    rc = FFN_CHUNK_ROWS
    lw = FFN_CHUNK_LANES

    def body(g_c, g_p, v_c, w_ref, b_ref, hid_ref):
        first = pl.program_id(0) == 0

        def chunk(ci, carry):
            r0 = pl.multiple_of(ci * rc, rc)
            above = pl.multiple_of(jnp.maximum(r0 - h, 0), h)
            for lg in range(dff // lw):
                cs = slice(lg * lw, (lg + 1) * lw)
                top = jnp.where(ci == 0, jnp.where(first, 0.0, g_p[:, cs]), g_c[pl.ds(above, h), cs])
                win = jnp.concatenate([top, g_c[pl.ds(r0, rc), cs]], axis=0)
                gc = jnp.broadcast_to(b_ref[:, cs], (rc, lw))
                for sh in range(FFN_K):
                    gc = gc + win[h - sh:h - sh + rc] * w_ref[pl.ds(FFN_K - 1 - sh, 1), cs]
                gelu, _ = _gelu_parts(gc)
                hid_ref[pl.ds(r0, rc), cs] = (gelu * v_c[pl.ds(r0, rc), cs]).astype(BF16)
            return carry

        lax.fori_loop(0, tm // rc, chunk, 0)

    return pl.pallas_call(
        body, name=name, grid=(s // tm,),
        in_specs=[_row(tm, dff, 0), _prev(tm, h, dff, 0), _row(tm, dff, 1), _full((FFN_K, dff)), _full((1, dff))],
        out_specs=_row(tm, dff), out_shape=jax.ShapeDtypeStruct((s, dff), BF16),
        compiler_params=_cp(),
    )(gv, gv, gv, dw_w, dw_b)


def ffn_act_bwd(name, dhid, gv, dw_w, dw_b):
    s = gv.shape[0]
    dff = gv.shape[1] // 2
    tm = min(FFN_TILE, s // 4)
    h = FFN_HALO
    nt = s // tm
    rc = FFN_CHUNK_ROWS
    lw = FFN_CHUNK_LANES
    n_chunks = tm // rc

    def body(dh_c, dh_n, g_p, g_c, g_n, v_c, v_n, w_ref, b_ref, dgv_ref, dw_ref, db_ref):
        i = pl.program_id(0)
        first = i == 0
        last = i == nt - 1

        @pl.when(first)
        def _():
            dw_ref[...] = jnp.zeros(dw_ref.shape, F32)
            db_ref[...] = jnp.zeros(db_ref.shape, F32)

        def chunk(ci, carry):
            r0 = pl.multiple_of(ci * rc, rc)
            above = pl.multiple_of(jnp.maximum(r0 - h, 0), h)
            below = pl.multiple_of(jnp.minimum(r0 + rc, tm - h), h)
            at_end = ci == n_chunks - 1
            for lg in range(dff // lw):
                cs = slice(lg * lw, (lg + 1) * lw)
                top = jnp.where(ci == 0, jnp.where(first, 0.0, g_p[:, cs]), g_c[pl.ds(above, h), cs])
                bot = jnp.where(at_end, g_n[:, cs], g_c[pl.ds(below, h), cs])
                win = jnp.concatenate([top, g_c[pl.ds(r0, rc), cs], bot], axis=0)
                shifted = [win[h - sh:h - sh + rc + h] for sh in range(FFN_K)]
                gc = jnp.broadcast_to(b_ref[:, cs], (rc + h, lw))
                for sh in range(FFN_K):
                    gc = gc + shifted[sh] * w_ref[pl.ds(FFN_K - 1 - sh, 1), cs]
                gelu, dgelu = _gelu_parts(gc)
                dh_mid = dh_c[pl.ds(r0, rc), cs]
                hv_bot = jnp.where(at_end, jnp.where(last, 0.0, dh_n[:, cs] * v_n[:, cs]),
                                   dh_c[pl.ds(below, h), cs] * v_c[pl.ds(below, h), cs])
                dgc = jnp.concatenate([dh_mid * v_c[pl.ds(r0, rc), cs], hv_bot], axis=0) * dgelu
                dgate = jnp.zeros((rc, lw), F32)
                for sh in range(FFN_K):
                    dgate = dgate + dgc[sh:sh + rc] * w_ref[pl.ds(FFN_K - 1 - sh, 1), cs]
                dgv_ref[pl.ds(r0, rc), cs] = dgate.astype(BF16)
                dgv_ref[pl.ds(r0, rc), slice(dff + lg * lw, dff + (lg + 1) * lw)] = (dh_mid * gelu[0:rc]).astype(BF16)
                dgc_mid = dgc[0:rc]
                for sh in range(FFN_K):
                    dw_ref[pl.ds(FFN_K - 1 - sh, 1), cs] += _colsum(dgc_mid * shifted[sh][0:rc])
                db_ref[:, cs] += _colsum(dgc_mid)
            return carry

        lax.fori_loop(0, n_chunks, chunk, 0)

    return pl.pallas_call(
        body, name=name, grid=(nt,),
        in_specs=[_row(tm, dff), _next(tm, h, dff, s),
                  _prev(tm, h, dff, 0), _row(tm, dff, 0), _next(tm, h, dff, s, 0),
                  _row(tm, dff, 1), _next(tm, h, dff, s, 1),
                  _full((FFN_K, dff)), _full((1, dff))],
        out_specs=[_row(tm, 2 * dff), _full((FFN_K, dff)), _full((1, dff))],
        out_shape=[jax.ShapeDtypeStruct((s, 2 * dff), BF16), jax.ShapeDtypeStruct((FFN_K, dff), F32),
                   jax.ShapeDtypeStruct((1, dff), F32)],
        compiler_params=_cp(),
    )(dhid, dhid, gv, gv, gv, gv, gv, dw_w, dw_b)


def _toeplitz_bias(rel_bias):
    nh = rel_bias.shape[0]
    zero = jnp.zeros((nh, 1), rel_bias.dtype)
    line = jnp.concatenate(
        [zero, jnp.broadcast_to(rel_bias[:, 2 * MAX_REL:], (nh, SHEAR_SAT)),
         jnp.flip(rel_bias[:, 1:2 * MAX_REL], axis=1), zero], axis=1)
    z = jnp.broadcast_to(line[:, None, :], (nh, Q_TILE, SHEAR_W + 1)).reshape(nh, Q_TILE * (SHEAR_W + 1))
    return z[:, :Q_TILE * SHEAR_W].reshape(nh, Q_TILE, SHEAR_W)[:, :, Q_TILE:]


def _shear_for_bias_grad(ds_sum):
    nh = ds_sum.shape[0]
    z = jnp.pad(ds_sum, ((0, 0), (0, 0), (Q_TILE, 0))).reshape(nh, Q_TILE * SHEAR_W)
    return jnp.pad(z, ((0, 0), (0, Q_TILE))).reshape(nh, Q_TILE, SHEAR_W + 1)


def _attn_mask(t):
    row = lax.broadcasted_iota(jnp.int32, (Q_TILE, K_WIN), 0)
    col = lax.broadcasted_iota(jnp.int32, (Q_TILE, K_WIN), 1)
    qc = row // CHUNK
    kc = col // CHUNK
    return (kc >= qc) & (kc <= qc + LEFT_CHUNKS) & (t * Q_TILE + col >= PAD_ROWS)


def _attn_probs(q2, k3, bias, mask, head):
    lane = lax.broadcasted_iota(jnp.int32, q2.shape, 1)
    q_h = jnp.where(lane // HEAD_DIM == head, q2, jnp.zeros_like(q2))
    sc = _dot(q_h, k3, "nt") * (HEAD_DIM ** -0.5) + bias
    sc = jnp.where(mask, sc, NEG_INF)
    m = jnp.max(sc, axis=-1, keepdims=True)
    p = jnp.exp(sc - m)
    return q_h, p / jnp.sum(p, axis=-1, keepdims=True)


def _attn_specs(d_model):
    nq = PAD_ROWS // Q_TILE
    hp_k = d_model // LANES
    specs = [pl.BlockSpec((Q_TILE, LANES), lambda hp, t: (t + nq, hp))]
    for which in (1, 2):
        for j in range(K_WIN // Q_TILE):
            specs.append(pl.BlockSpec((Q_TILE, LANES), lambda hp, t, j=j, which=which: (t + j, which * hp_k + hp)))
    specs.append(pl.BlockSpec((2, Q_TILE, K_WIN), lambda hp, t: (hp, 0, 0)))
    return specs


def attn_fwd(name, qkvp, bias):
    s = qkvp.shape[0] - PAD_ROWS
    d_model = qkvp.shape[1] // 3
    nw = K_WIN // Q_TILE

    def body(q_ref, *refs):
        k_refs, v_refs, b_ref, o_ref = refs[:nw], refs[nw:2 * nw], refs[2 * nw], refs[2 * nw + 1]
        t = pl.program_id(1)
        q2 = q_ref[...]
        k3 = jnp.concatenate([r[...] for r in k_refs], axis=0)
        v3 = jnp.concatenate([r[...] for r in v_refs], axis=0)
        mask = _attn_mask(t)
        outs = []
        for head in range(2):
            _, p = _attn_probs(q2, k3, b_ref[head], mask, head)
            outs.append(_dot(p, v3, "nn"))
        lane = lax.broadcasted_iota(jnp.int32, (Q_TILE, LANES), 1)
        o_ref[...] = jnp.where(lane < HEAD_DIM, outs[0], outs[1]).astype(BF16)

    return pl.pallas_call(
        body, name=name, grid=(d_model // LANES, s // Q_TILE),
        in_specs=_attn_specs(d_model), out_specs=pl.BlockSpec((Q_TILE, LANES), lambda hp, t: (t, hp)),
        out_shape=jax.ShapeDtypeStruct((s, d_model), BF16), compiler_params=_cp(),
    )(qkvp, *([qkvp] * (2 * nw)), bias)


def attn_bwd(name, qkvp, bias, do):
    s = qkvp.shape[0] - PAD_ROWS
    d_model = qkvp.shape[1] // 3
    nw = K_WIN // Q_TILE
    nt = s // Q_TILE
    scale = HEAD_DIM ** -0.5

    def body(q_ref, *refs):
        k_refs, v_refs = refs[:nw], refs[nw:2 * nw]
        b_ref, do_ref, dq_ref, dk_ref, dv_ref, ds_ref, dk_acc, dv_acc = refs[2 * nw:]
        t = pl.program_id(1)
        first = t == 0

        @pl.when(first)
        def _():
            dk_acc[...] = jnp.zeros(dk_acc.shape, F32)
            dv_acc[...] = jnp.zeros(dv_acc.shape, F32)

        q2 = q_ref[...]
        do2 = do_ref[...]
        k3 = jnp.concatenate([r[...] for r in k_refs], axis=0)
        v3 = jnp.concatenate([r[...] for r in v_refs], axis=0)
        mask = _attn_mask(t)
        lane = lax.broadcasted_iota(jnp.int32, (Q_TILE, LANES), 1)
        dqs = []
        dk_win = jnp.zeros((K_WIN, LANES), F32)
        dv_win = jnp.zeros((K_WIN, LANES), F32)
        for head in range(2):
            q_h, p = _attn_probs(q2, k3, b_ref[head], mask, head)
            do_h = jnp.where(lane // HEAD_DIM == head, do2, jnp.zeros_like(do2))
            dp = _dot(do_h, v3, "nt")
            ds = p * (dp - jnp.sum(p * dp, axis=-1, keepdims=True))
            _acc_add(ds_ref.at[head], first, ds)
            dsb = (ds * scale).astype(BF16)
            dqs.append(_dot(dsb, k3, "nn"))
            dk_win = dk_win + _dot(dsb, q_h, "tn")
            dv_win = dv_win + _dot(p, do_h, "tn")
        dq_ref[...] = jnp.where(lane < HEAD_DIM, dqs[0], dqs[1]).astype(BF16)
        start = pl.multiple_of(t * Q_TILE, Q_TILE)
        dk_acc[pl.ds(start, K_WIN), :] += dk_win
        dv_acc[pl.ds(start, K_WIN), :] += dv_win

        @pl.when(t == nt - 1)
        def _():
            dk_ref[...] = dk_acc[pl.ds(PAD_ROWS, s), :].astype(BF16)
            dv_ref[...] = dv_acc[pl.ds(PAD_ROWS, s), :].astype(BF16)

    specs = _attn_specs(d_model) + [pl.BlockSpec((Q_TILE, LANES), lambda hp, t: (t, hp))]
    col_spec = pl.BlockSpec((s, LANES), lambda hp, t: (0, hp))
    return pl.pallas_call(
        body, name=name, grid=(d_model // LANES, nt), in_specs=specs,
        out_specs=[pl.BlockSpec((Q_TILE, LANES), lambda hp, t: (t, hp)), col_spec, col_spec,
                   pl.BlockSpec((2, Q_TILE, K_WIN), lambda hp, t: (hp, 0, 0))],
        out_shape=[jax.ShapeDtypeStruct((s, d_model), BF16)] * 3
        + [jax.ShapeDtypeStruct((N_HEADS, Q_TILE, K_WIN), F32)],
        scratch_shapes=[pltpu.VMEM((PAD_ROWS + s, LANES), F32), pltpu.VMEM((PAD_ROWS + s, LANES), F32)],
        compiler_params=_cp(),
    )(qkvp, *([qkvp] * (2 * nw)), bias, do)


def bias_grad_reduce(name, sheared):
    nh, _, width = sheared.shape

    def body(x_ref, col_ref, sat_ref):
        cols = _colsum(x_ref[...])
        col_ref[...] = cols
        k = lax.broadcasted_iota(jnp.int32, cols.shape, 1)
        tot = jnp.sum(jnp.where((k >= 1) & (k <= SHEAR_SAT), cols, 0.0), axis=-1, keepdims=True)
        sat_ref[...] = jnp.broadcast_to(tot, sat_ref.shape)

    return pl.pallas_call(
        body, name=name, grid=(nh,),
        in_specs=[pl.BlockSpec((None, Q_TILE, width), lambda hh: (hh, 0, 0))],
        out_specs=[pl.BlockSpec((None, 1, width), lambda hh: (hh, 0, 0)),
                   pl.BlockSpec((None, 1, LANES), lambda hh: (hh, 0, 0))],
        out_shape=[jax.ShapeDtypeStruct((nh, 1, width), F32), jax.ShapeDtypeStruct((nh, 1, LANES), F32)],
        compiler_params=_cp(),
    )(sheared)


def _ew_rows(r):
    for cand in (512, 256, 128, 64, 32, 16, 8):
        if r % cand == 0:
            return cand
    return r


def cast_into_gathered(name, w, layer, s_idx):
    r, c = w.shape[-2:]
    tr = _ew_rows(r)

    def body(s_ref, w_ref, o_ref):
        o_ref[...] = w_ref[...].astype(BF16)

    grid_spec = pltpu.PrefetchScalarGridSpec(
        num_scalar_prefetch=1, grid=(r // tr,),
        in_specs=[pl.BlockSpec((None, tr, c), lambda i, s_ref: (layer, i, 0))],
        out_specs=pl.BlockSpec((None, tr, c), lambda i, s_ref: (s_ref[0], i, 0)))
    return pl.pallas_call(
        body, name=name, grid_spec=grid_spec, out_shape=jax.ShapeDtypeStruct((N_SHARD, r, c), BF16),
        compiler_params=_cp(),
    )(s_idx, w)


def adamw(name, w, grads, m, v):
    nl, r, c = w.shape
    tr = _ew_rows(r)

    def body(*refs):
        w_ref, m_ref, v_ref = refs[0], refs[1], refs[2]
        g_refs = refs[3:3 + nl]
        d_ref, nm_ref, nv_ref = refs[3 + nl:]
        layer = pl.program_id(0)
        g = g_refs[0][...]
        for j in range(1, nl):
            g = jnp.where(layer == j, g_refs[j][...], g)
        nm = ADAM_B1 * m_ref[...] + (1.0 - ADAM_B1) * g
        nv = ADAM_B2 * v_ref[...] + (1.0 - ADAM_B2) * (g * g)
        m_hat = nm / ADAM_BC1
        v_hat = nv / ADAM_BC2
        d_ref[...] = -ADAM_LR * (m_hat / (jnp.sqrt(v_hat) + ADAM_EPS) + ADAM_WD * w_ref[...])
        nm_ref[...] = nm
        nv_ref[...] = nv

    p_spec = pl.BlockSpec((None, tr, c), lambda l, i: (l, i, 0))
    g_spec = pl.BlockSpec((tr, c), lambda l, i: (i, 0))
    return pl.pallas_call(
        body, name=name, grid=(nl, r // tr), in_specs=[p_spec] * 3 + [g_spec] * nl, out_specs=[p_spec] * 3,
        out_shape=[jax.ShapeDtypeStruct((nl, r, c), F32)] * 3, compiler_params=_cp(),
    )(w, m, v, *grads)


def sum_blocks(name, gathered, n_blocks):
    r = gathered.shape[0] // n_blocks
    c = gathered.shape[1]
    tr = _ew_rows(r)
    nt = r // tr

    def body(*refs):
        acc = refs[0][...]
        for j in range(1, n_blocks):
            acc = acc + refs[j][...]
        refs[-1][...] = acc

    specs = [pl.BlockSpec((tr, c), lambda i, j=j: (j * nt + i, 0)) for j in range(n_blocks)]
    return pl.pallas_call(
        body, name=name, grid=(nt,), in_specs=specs, out_specs=pl.BlockSpec((tr, c), lambda i: (i, 0)),
        out_shape=jax.ShapeDtypeStruct((r, c), F32), compiler_params=_cp(),
    )(*([gathered] * n_blocks))


def _place():
    return lax.axis_index("x"), lax.axis_index("y"), lax.axis_index("c")


def _other_chips(x, y):
    return [(1 - x, y), (x, 1 - y), (1 - x, 1 - y)]


HBM_SPEC = pl.BlockSpec(memory_space=pltpu.HBM)
SEM_SPEC = pl.BlockSpec(memory_space=pltpu.SEMAPHORE)
ANY_SPEC = pl.BlockSpec(memory_space=pl.ANY)
EFFECT = pltpu.SideEffectType.DATAFLOW_SIDE_EFFECTING


def _in_hbm(a):
    return pltpu.with_memory_space_constraint(a, pltpu.HBM)


def copies_start(name, bufs, plan, n_copies):
    n = len(bufs)

    def body(*refs):
        send, recv = refs[n], refs[n + 1]
        token = refs[2 * n + 2]
        for k, (src, dst, peer, _) in enumerate(plan(refs[:n])):
            pltpu.make_async_remote_copy(
                src_ref=src, dst_ref=dst, send_sem=send.at[k], recv_sem=recv.at[k],
                device_id=peer, device_id_type=MESH).start()
        token[...] = jnp.zeros(token.shape, F32)

    outs = pl.pallas_call(
        body, name=name,
        out_shape=(pltpu.SemaphoreType.DMA((n_copies,)), pltpu.SemaphoreType.DMA((n_copies,)),
                   *[pltpu.HBM(b.shape, b.dtype) for b in bufs], jax.ShapeDtypeStruct((8, LANES), F32)),
        in_specs=[HBM_SPEC] * n,
        out_specs=(SEM_SPEC, SEM_SPEC, *([HBM_SPEC] * n), pl.BlockSpec(memory_space=pltpu.VMEM)),
        input_output_aliases={a: a + 2 for a in range(n)},
        compiler_params=pltpu.CompilerParams(has_side_effects=EFFECT),
    )(*[_in_hbm(b) for b in bufs])
    return outs[0], outs[1], list(outs[2:2 + n]), outs[2 + n]


def copies_wait(name, bufs, send, recv, plan, sem_base, after):
    n = len(bufs)

    def body(*refs):
        send_ref, recv_ref = refs[n], refs[n + 1]
        for k, (src, _, peer, land) in enumerate(plan(refs[:n])):
            cp = pltpu.make_async_remote_copy(
                src_ref=src, dst_ref=land, send_sem=send_ref.at[sem_base + k], recv_sem=recv_ref.at[sem_base + k],
                device_id=peer, device_id_type=MESH)
            cp.wait_send()
            cp.wait_recv()

    outs = pl.pallas_call(
        body, name=name,
        out_shape=tuple(pltpu.HBM(b.shape, b.dtype) for b in bufs),
        in_specs=[HBM_SPEC] * n + [SEM_SPEC, SEM_SPEC, ANY_SPEC], out_specs=tuple([HBM_SPEC] * n),
        input_output_aliases={a: a for a in range(n)},
        compiler_params=pltpu.CompilerParams(has_side_effects=EFFECT),
    )(*bufs, send, recv, after)
    return list(outs)


def gather_plan(refs):
    x, y, c = _place()
    me = 2 * x + y
    return [(buf.at[me], buf.at[me], (cx, cy, c), buf.at[2 * cx + cy])
            for buf in refs for cx, cy in _other_chips(x, y)]


def swap_plan(refs):
    x, y, c = _place()
    n = len(refs) // 2
    out = []
    for g, land in zip(refs[:n], refs[n:]):
        hr = g.shape[1] // 2
        out.append((g.at[:, pl.ds((1 - c) * hr, hr)], land, (x, y, 1 - c), land))
    return out


def owners_plan(refs):
    x, y, c = _place()
    n = len(refs) // 2
    return [(src.at[2 * cx + cy], land.at[j], (cx, cy, c), land.at[j])
            for src, land in zip(refs[:n], refs[n:]) for j, (cx, cy) in enumerate(_other_chips(x, y))]


def join_plan(refs):
    x, y, c = _place()
    out = []
    for buf in refs:
        hr = buf.shape[0] // 2
        mine = buf.at[pl.ds(c * hr, hr)]
        out.append((mine, mine, (x, y, 1 - c), buf.at[pl.ds((1 - c) * hr, hr)]))
    return out


def add_halves(name, grad, landed, c_idx):
    _, r, c = grad.shape
    hr = r // 2
    tr = _ew_rows(hr)
    nt = hr // tr

    def body(c_ref, g_ref, l_ref, o_ref, ob_ref):
        tot = g_ref[...] + l_ref[...]
        o_ref[...] = tot
        ob_ref[...] = tot.astype(BF16)

    blk = pl.BlockSpec((None, tr, c), lambda sh, i, c_ref: (sh, i, 0))
    grid_spec = pltpu.PrefetchScalarGridSpec(
        num_scalar_prefetch=1, grid=(N_SHARD, nt),
        in_specs=[pl.BlockSpec((None, tr, c), lambda sh, i, c_ref: (sh, c_ref[0] * nt + i, 0)), blk],
        out_specs=[blk, blk])
    return pl.pallas_call(
        body, name=name, grid_spec=grid_spec,
        out_shape=[jax.ShapeDtypeStruct((N_SHARD, hr, c), F32), jax.ShapeDtypeStruct((N_SHARD, hr, c), BF16)],
        compiler_params=_cp(),
    )(c_idx, grad, landed)


def add_owned(name, own, landed, sc_idx):
    _, hr, c = own.shape
    tr = _ew_rows(hr)
    nt = hr // tr

    def body(sc_ref, o_ref, l0, l1, l2, out_ref):
        out_ref[...] = ((o_ref[...] + l0[...].astype(F32)) + l1[...].astype(F32)) + l2[...].astype(F32)

    grid_spec = pltpu.PrefetchScalarGridSpec(
        num_scalar_prefetch=1, grid=(nt,),
        in_specs=[pl.BlockSpec((None, tr, c), lambda i, sc_ref: (sc_ref[0], i, 0))]
        + [pl.BlockSpec((None, tr, c), lambda i, sc_ref, j=j: (j, i, 0)) for j in range(3)],
        out_specs=pl.BlockSpec((tr, c), lambda i, sc_ref: (sc_ref[1] * nt + i, 0)))
    return pl.pallas_call(
        body, name=name, grid_spec=grid_spec, out_shape=jax.ShapeDtypeStruct((2 * hr, c), F32),
        compiler_params=_cp(),
    )(sc_idx, own, landed, landed, landed)


def gather_small(name, block):
    m_per, n = block.shape

    def body(x_ref, out_ref, send_sems, recv_sems, local_sem):
        x, y, c = _place()
        me, sibling = (x, y, c), (x, y, 1 - c)
        chips = _other_chips(x, y)

        def rows(px, py, pc):
            return out_ref.at[pl.ds((4 * px + 2 * py + pc) * m_per, m_per), :]

        def copy(k, blk, to, src=None):
            return pltpu.make_async_remote_copy(
                src_ref=rows(*blk) if src is None else src, dst_ref=rows(*blk),
                send_sem=send_sems.at[k], recv_sem=recv_sems.at[k], device_id=to, device_id_type=MESH)

        mine = pltpu.make_async_copy(x_ref, rows(*me), local_sem)
        mine.start()
        first = [copy(0, me, sibling, src=x_ref)]
        first += [copy(1 + j, me, (*chip, c), src=x_ref) for j, chip in enumerate(chips)]
        for cp in first:
            cp.start()
        passed = [copy(4 + j, (*chip, c), sibling) for j, chip in enumerate(chips)]
        for j, chip in enumerate(chips):
            copy(1 + j, (*chip, c), me).wait_recv()
            passed[j].start()
        copy(0, sibling, me).wait_recv()
        for j, chip in enumerate(chips):
            copy(4 + j, (*chip, 1 - c), me).wait_recv()
        for cp in first + passed:
            cp.wait_send()
        mine.wait()

    return pl.pallas_call(
        body, name=name, out_shape=jax.ShapeDtypeStruct((8 * m_per, n), block.dtype),
        in_specs=[pl.BlockSpec(memory_space=pltpu.VMEM)], out_specs=pl.BlockSpec(memory_space=pltpu.VMEM),
        scratch_shapes=[pltpu.SemaphoreType.DMA((7,)), pltpu.SemaphoreType.DMA((7,)), pltpu.SemaphoreType.DMA],
        compiler_params=_cp(),
    )(block)


PACK_QUANTUM = 8 * LANES


def _pack(arrays):
    pieces = []
    for a in arrays:
        flat = a.reshape(-1)
        padded = -(-flat.shape[0] // PACK_QUANTUM) * PACK_QUANTUM
        pieces.append(jnp.pad(flat, (0, padded - flat.shape[0])).reshape(-1, LANES))
    return jnp.concatenate(pieces, axis=0)


def _unpack(packed, shapes):
    out = []
    row = 0
    for shp in shapes:
        size = math.prod(shp)
        rows = -(-size // PACK_QUANTUM) * 8
        out.append(packed[row:row + rows].reshape(-1)[:size].reshape(shp))
        row += rows
    return out


def kernel(x, p, mix_w_in, pool_w, pool_scale, conv_dw_w, conv_dw_b, conv_ln_g, conv_ln_b, mix_w_out, attn_w_qkv, attn_rel_bias, attn_w_o, ln_mix_g, ln_mix_b, ffn_w_up, ffn_dw_w, ffn_dw_b, ffn_w_down, ple_w_proj, ple_w_gate, ple_b_gate, ln_ffn_g, ln_ffn_b, loss_target, m_mix_w_in, m_pool_w, m_pool_scale, m_conv_dw_w, m_conv_dw_b, m_conv_ln_g, m_conv_ln_b, m_mix_w_out, m_attn_w_qkv, m_attn_rel_bias, m_attn_w_o, m_ln_mix_g, m_ln_mix_b, m_ffn_w_up, m_ffn_dw_w, m_ffn_dw_b, m_ffn_w_down, m_ple_w_proj, m_ple_w_gate, m_ple_b_gate, m_ln_ffn_g, m_ln_ffn_b, v_mix_w_in, v_pool_w, v_pool_scale, v_conv_dw_w, v_conv_dw_b, v_conv_ln_g, v_conv_ln_b, v_mix_w_out, v_attn_w_qkv, v_attn_rel_bias, v_attn_w_o, v_ln_mix_g, v_ln_mix_b, v_ffn_w_up, v_ffn_dw_w, v_ffn_dw_b, v_ffn_w_down, v_ple_w_proj, v_ple_w_gate, v_ple_b_gate, v_ln_ffn_g, v_ln_ffn_b):
    xi, yi, ci = _place()
    shard_idx = (2 * xi + yi).astype(jnp.int32)
    s_arr = shard_idx.reshape(1)
    c_arr = ci.astype(jnp.int32).reshape(1)
    sc_arr = jnp.concatenate([s_arr, c_arr])

    x0 = x[0]
    target = loss_target[0]
    seq = x0.shape[0]

    big = [
        ("mix_w_in", mix_w_in, m_mix_w_in, v_mix_w_in, True),
        ("mix_w_out", mix_w_out, m_mix_w_out, v_mix_w_out, False),
        ("attn_w_qkv", attn_w_qkv, m_attn_w_qkv, v_attn_w_qkv, True),
        ("attn_w_o", attn_w_o, m_attn_w_o, v_attn_w_o, False),
        ("ffn_w_up", ffn_w_up, m_ffn_w_up, v_ffn_w_up, True),
        ("ffn_w_down", ffn_w_down, m_ffn_w_down, v_ffn_w_down, False),
        ("ple_w_proj", ple_w_proj, m_ple_w_proj, v_ple_w_proj, True),
        ("ple_w_gate", ple_w_gate, m_ple_w_gate, v_ple_w_gate, False),
    ]
    params = {nm: w for nm, w, _, _, _ in big}
    col_sharded = {nm: cs for nm, _, _, _, cs in big}
    keys = [("mix_w_in", 0), ("mix_w_out", 0), ("ffn_w_up", 0), ("ffn_w_down", 0), ("ple_w_gate", 0),
            ("ple_w_proj", 0), ("attn_w_qkv", 0), ("attn_w_o", 0), ("ffn_w_up", 1), ("ffn_w_down", 1),
            ("ple_w_gate", 1), ("ple_w_proj", 1)]
    shards = [cast_into_gathered(f"cast_{nm}_{layer}", params[nm], layer, s_arr) for nm, layer in keys]
    g_send, g_recv, g_bufs, _ = copies_start("gather_start", shards, gather_plan, 3 * len(keys))
    arrived_w = {}

    def weight(nm, layer, after=None):
        key = (nm, layer)
        if key not in arrived_w:
            a = keys.index(key)
            arrived_w[key] = copies_wait(f"gather_wait_{nm}_{layer}", [g_bufs[a]], g_send, g_recv, gather_plan,
                                         3 * a, after)[0]
        g = arrived_w[key]
        if col_sharded[nm]:
            return g
        return g.reshape(g.shape[0] * g.shape[1], g.shape[2])

    def tie(a, token):
        return a + token[0:1, 0:1].astype(a.dtype)

    class Reducer:
        def __init__(self, tag, group):
            self.tag, self.group, self.stage = tag, group, 0
            self.n = len(group)
            self.result = None

        def advance(self, after):
            tag, n = self.tag, self.n
            if self.stage == 0:
                grads = []
                for key in self.group:
                    g = big_grads[key]
                    grads.append(g if g.ndim == 3 else g.reshape(N_SHARD, g.shape[0] // N_SHARD, g.shape[1]))
                lands = [lax.empty((N_SHARD, g.shape[1] // 2, g.shape[2]), F32) for g in grads]
                self.sems = copies_start(f"swap_start_{tag}", grads + lands, swap_plan, n)
            elif self.stage == 1:
                send, recv, bufs, _ = self.sems
                outs = copies_wait(f"swap_wait_{tag}", bufs, send, recv, swap_plan, 0, after)
                self.own, wire = [], []
                for key, g, ld in zip(self.group, outs[:n], outs[n:]):
                    o, ob = add_halves(f"add_halves_{key[0]}_{key[1]}", g, ld, c_arr)
                    self.own.append(o)
                    wire.append(ob)
                lands = [lax.empty((3,) + w.shape[1:], BF16) for w in wire]
                self.sems = copies_start(f"owners_start_{tag}", wire + lands, owners_plan, 3 * n)
            elif self.stage == 2:
                send, recv, bufs, _ = self.sems
                outs = copies_wait(f"owners_wait_{tag}", bufs, send, recv, owners_plan, 0, after)
                finals = [add_owned(f"add_owned_{key[0]}_{key[1]}", o, ar, sc_arr)
                          for key, o, ar in zip(self.group, self.own, outs[n:])]
                self.sems = copies_start(f"join_start_{tag}", finals, join_plan, n)
            elif self.stage == 3:
                send, recv, bufs, _ = self.sems
                outs = copies_wait(f"join_wait_{tag}", bufs, send, recv, join_plan, 0, after)
                self.result = dict(zip(self.group, outs))
                self.sems = None
            self.stage += 1
            return None if self.sems is None else self.sems[3]

    dw_shapes = [conv_dw_w.shape, ffn_dw_w.shape]
    dw_packed = _pack([conv_dw_w, ffn_dw_w])
    dw_rows = dw_packed.shape[0]
    dw_all = gather_small("gather_dw", dw_packed)
    dw_parts = [_unpack(dw_all[2 * k * dw_rows:(2 * k + 1) * dw_rows], dw_shapes) for k in range(N_SHARD)]
    conv_w_full = jnp.concatenate([pc[0] for pc in dw_parts], axis=2)[0]
    ffn_dw_full = jnp.concatenate([pc[1] for pc in dw_parts], axis=2)

    big_grads = {}
    small_grads = {}

    saved = []
    h_in = x0
    for layer in range(N_LAYERS):
        sv = {"x_in": h_in}
        if layer % 2 == 0:
            u = mm_cols_fwd("mix_in", h_in, weight("mix_w_in", 0, h_in), F32)
            cat, d_sv, e_sv, glu_sv, hh_sv, rs_sv = mixer_fwd(
                "mixer_fwd", u, pool_w[0], pool_scale, conv_w_full, conv_dw_b, conv_ln_g, conv_ln_b)
            mix = mm_rows_fwd("mix_out", cat, weight("mix_w_out", 0, cat))
            sv.update(u=u, cat=cat, d=d_sv, e=e_sv, glu=glu_sv, hh=hh_sv, rs=rs_sv)
        else:
            qkvp = mm_cols_fwd("attn_qkv", h_in, weight("attn_w_qkv", 0, h_in), BF16,
                               pad_blocks=PAD_ROWS // _row_tile(seq))
            bias = _toeplitz_bias(attn_rel_bias[0])
            att = attn_fwd("attn_fwd", qkvp, bias)
            mix = mm_rows_fwd("attn_out", att, weight("attn_w_o", 0, att))
            sv.update(qkvp=qkvp, bias=bias, att=att)
        x1, xh1, rs1 = ln_fwd(f"ln_mix_{layer}", h_in, mix, ln_mix_g[layer:layer + 1], ln_mix_b[layer:layer + 1])
        gv = mm_cols_fwd(f"ffn_up_{layer}", x1, weight("ffn_w_up", layer, x1), F32)
        hid = ffn_act_fwd(f"ffn_act_{layer}", gv, ffn_dw_full[layer], ffn_dw_b[layer:layer + 1])
        ffn = mm_rows_fwd(f"ffn_down_{layer}", hid, weight("ffn_w_down", layer, hid))
        pgl = mm_rows_fwd(f"ple_gate_{layer}", x1, weight("ple_w_gate", layer, ffn))
        pp = mm_cols_fwd(f"ple_proj_{layer}", p[layer, 0], weight("ple_w_proj", layer, pgl), F32)
        bg = ple_b_gate[layer:layer + 1]
        x2, xh2, rs2 = ln_fwd(f"ln_ffn_{layer}", x1, ffn, ln_ffn_g[layer:layer + 1], ln_ffn_b[layer:layer + 1],
                              ple=(pgl, pp, bg))
        sv.update(x1=x1, xh1=xh1, rs1=rs1, gv=gv, hid=hid, pgl=pgl, pp=pp, xh2=xh2, rs2=rs2)
        saved.append(sv)
        h_in = x2

    dy, loss_part = loss_fwd_bwd("loss", h_in, target)

    reducers = []

    def open_group(tag, group):
        reducers.append(Reducer(tag, group))
        return reducers[-1].advance(None)

    def hook(after):
        token = None
        for red in reducers:
            if red.stage < 4:
                tk = red.advance(after)
                if tk is not None:
                    token = tk if token is None else token + tk
        return token

    def tied(a, token):
        return a if token is None else tie(a, token)

    parts = [(1.0, dy)]
    token = None
    for layer in reversed(range(N_LAYERS)):
        sv = saved[layer]
        bg = ple_b_gate[layer:layer + 1]
        if layer == 0:
            token = open_group("layer1", [("attn_w_qkv", 0), ("attn_w_o", 0), ("ffn_w_up", 1), ("ffn_w_down", 1),
                                          ("ple_w_gate", 1), ("ple_w_proj", 1)])
        dz2, dg2, db2, dpp, dpgl, dbg = ln_bwd(
            f"ln_ffn_bwd_{layer}", parts, sv["xh2"], sv["rs2"], tied(ln_ffn_g[layer:layer + 1], token),
            ple=(sv["pgl"], sv["pp"], bg))
        small_grads[("ln_ffn_g", layer)] = dg2
        small_grads[("ln_ffn_b", layer)] = db2
        small_grads[("ple_b_gate", layer)] = dbg
        w_down = weight("ffn_w_down", layer)
        dhid = mm_rows_dx(f"ffn_down_dx_{layer}", dz2, w_down)
        big_grads[("ffn_w_down", layer)] = mm_rows_dw(f"ffn_down_dw_{layer}", sv["hid"], dz2)
        token = hook(big_grads[("ffn_w_down", layer)])
        dgv, ddw, ddb = ffn_act_bwd(f"ffn_act_bwd_{layer}", dhid, sv["gv"], ffn_dw_full[layer],
                                    tied(ffn_dw_b[layer:layer + 1], token))
        small_grads[("ffn_dw_w", layer)] = ddw
        small_grads[("ffn_dw_b", layer)] = ddb
        big_grads[("ffn_w_up", layer)] = mm_cols_dw(f"ffn_up_dw_{layer}", sv["x1"], dgv)
        t_up = mm_cols_dx(f"ffn_up_dx_{layer}", dgv, weight("ffn_w_up", layer))
        token = hook(t_up)
        big_grads[("ple_w_gate", layer)] = mm_rows_dw(f"ple_gate_dw_{layer}", sv["x1"], dpgl)
        t_gate = mm_rows_dx(f"ple_gate_dx_{layer}", dpgl, weight("ple_w_gate", layer))
        big_grads[("ple_w_proj", layer)] = mm_cols_dw(f"ple_proj_dw_{layer}", p[layer, 0], dpp)
        token2 = hook(big_grads[("ple_w_proj", layer)])
        if token2 is not None:
            token = token2 if token is None else token + token2
        if layer == 0:
            token3 = open_group("layer0_ffn", [("ffn_w_up", 0), ("ffn_w_down", 0), ("ple_w_gate", 0), ("ple_w_proj", 0)])
            token = token3 if token is None else token + token3
        dz1, dg1, db1 = ln_bwd(
            f"ln_mix_bwd_{layer}", [(ALPHA, dz2), (1.0, t_up), (1.0, t_gate)], sv["xh1"], sv["rs1"],
            tied(ln_mix_g[layer:layer + 1], token))
        small_grads[("ln_mix_g", layer)] = dg1
        small_grads[("ln_mix_b", layer)] = db1
        if layer % 2 == 0:
            dcat = mm_rows_dx("mix_out_dx", dz1, weight("mix_w_out", 0))
            big_grads[("mix_w_out", 0)] = mm_rows_dw("mix_out_dw", sv["cat"], dz1)
            token = hook(big_grads[("mix_w_out", 0)])
            du, dpw, dps, dcw, dcb, dcg, dcbt = mixer_bwd(
                "mixer_bwd", dcat, sv["u"], sv["d"], sv["e"], sv["glu"], sv["hh"], sv["rs"],
                pool_w[0], pool_scale, conv_w_full, tied(conv_ln_g, token), conv_ln_b)
            small_grads[("pool_w", 0)] = dpw
            small_grads[("pool_scale", 0)] = dps
            small_grads[("conv_dw_w", 0)] = dcw
            small_grads[("conv_dw_b", 0)] = dcb
            small_grads[("conv_ln_g", 0)] = dcg
            small_grads[("conv_ln_b", 0)] = dcbt
            big_grads[("mix_w_in", 0)] = mm_cols_dw("mix_in_dw", sv["x_in"], du)
            hook(big_grads[("mix_w_in", 0)])
            open_group("layer0_mix", [("mix_w_in", 0), ("mix_w_out", 0)])
            t_mix = mm_cols_dx("mix_in_dx", du, weight("mix_w_in", 0))
            hook(t_mix)
        else:
            do = mm_rows_dx("attn_out_dx", dz1, weight("attn_w_o", 0), out_dtype=BF16)
            big_grads[("attn_w_o", 0)] = mm_rows_dw("attn_out_dw", sv["att"], dz1)
            dq, dk, dv, ds_sum = attn_bwd("attn_bwd", sv["qkvp"], sv["bias"], do)
            cols, sat = bias_grad_reduce("bias_grad", _shear_for_bias_grad(ds_sum))
            d_rel = jnp.concatenate(
                [jnp.zeros((N_HEADS, 1), F32),
                 jnp.flip(cols[:, 0, SHEAR_SAT + 1:SHEAR_W], axis=1),
                 sat[:, 0, 0:1]], axis=1)
            small_grads[("attn_rel_bias", 0)] = d_rel
            dqkv = jnp.concatenate([dq, dk, dv], axis=1)
            big_grads[("attn_w_qkv", 0)] = mm_cols_dw("attn_qkv_dw", sv["x_in"], dqkv)
            t_mix = mm_cols_dx("attn_qkv_dx", dqkv, weight("attn_w_qkv", 0))
        parts = [(ALPHA, dz1), (1.0, t_mix)]
    grad_x = scaled_sum("grad_x", parts)
    hook(grad_x)
    hook(grad_x)
    shard_grads = {}
    for red in reducers:
        shard_grads.update(red.result)

    big_out = {}
    for nm, w, m, v, _ in big:
        gl = [shard_grads[(nm, layer)] for layer in range(w.shape[0])]
        delta, new_m, new_v = adamw(f"adamw_{nm}", w, gl, m, v)
        big_out[nm] = (jnp.stack(gl, axis=0), delta, new_m, new_v)

    small = [
        ("pool_w", pool_w, m_pool_w, v_pool_w, None),
        ("pool_scale", pool_scale, m_pool_scale, v_pool_scale, None),
        ("conv_dw_w", conv_dw_w, m_conv_dw_w, v_conv_dw_w, 2),
        ("conv_dw_b", conv_dw_b, m_conv_dw_b, v_conv_dw_b, None),
        ("conv_ln_g", conv_ln_g, m_conv_ln_g, v_conv_ln_g, None),
        ("conv_ln_b", conv_ln_b, m_conv_ln_b, v_conv_ln_b, None),
        ("attn_rel_bias", attn_rel_bias, m_attn_rel_bias, v_attn_rel_bias, None),
        ("ln_mix_g", ln_mix_g, m_ln_mix_g, v_ln_mix_g, None),
        ("ln_mix_b", ln_mix_b, m_ln_mix_b, v_ln_mix_b, None),
        ("ffn_dw_w", ffn_dw_w, m_ffn_dw_w, v_ffn_dw_w, 2),
        ("ffn_dw_b", ffn_dw_b, m_ffn_dw_b, v_ffn_dw_b, None),
        ("ple_b_gate", ple_b_gate, m_ple_b_gate, v_ple_b_gate, None),
        ("ln_ffn_g", ln_ffn_g, m_ln_ffn_g, v_ln_ffn_g, None),
        ("ln_ffn_b", ln_ffn_b, m_ln_ffn_b, v_ln_ffn_b, None),
    ]
    full_grads = []
    for nm, w, _, _, shard_axis in small:
        full = list(w.shape)
        if shard_axis is not None:
            full[shard_axis] *= N_SHARD
        per_layer = [small_grads[(nm, layer)].reshape((1,) + tuple(full[1:])) for layer in range(w.shape[0])]
        full_grads.append(jnp.concatenate(per_layer, axis=0))
    packed = _pack(full_grads + [loss_part])
    total = sum_blocks("sum_small", gather_small("gather_small_grads", packed), 8)
    unpacked = _unpack(total, [g.shape for g in full_grads] + [loss_part.shape])
    loss = unpacked[-1][0, 0]
    local_grads = []
    for (nm, w, _, _, shard_axis), g in zip(small, unpacked[:-1]):
        if shard_axis is not None:
            width = w.shape[shard_axis]
            g = lax.dynamic_slice_in_dim(g, shard_idx * width, width, axis=shard_axis)
        local_grads.append(g.reshape(w.shape))
    shapes = [w.shape for _, w, _, _, _ in small]
    pg = _pack(local_grads)
    pw = _pack([w for _, w, _, _, _ in small])
    pm = _pack([m for _, _, m, _, _ in small])
    pv = _pack([v for _, _, _, v, _ in small])
    delta_s, new_m_s, new_v_s = adamw("adamw_small", pw[None], [pg], pm[None], pv[None])
    small_out = {}
    for (nm, _, _, _, _), g, d_, m_, v_ in zip(
            small, local_grads, _unpack(delta_s[0], shapes), _unpack(new_m_s[0], shapes), _unpack(new_v_s[0], shapes)):
        small_out[nm] = (g, d_, m_, v_)

    order = ["mix_w_in", "pool_w", "pool_scale", "conv_dw_w", "conv_dw_b", "conv_ln_g", "conv_ln_b", "mix_w_out",
             "attn_w_qkv", "attn_rel_bias", "attn_w_o", "ln_mix_g", "ln_mix_b", "ffn_w_up", "ffn_dw_w", "ffn_dw_b",
             "ffn_w_down", "ple_w_proj", "ple_w_gate", "ple_b_gate", "ln_ffn_g", "ln_ffn_b"]
    res = {**big_out, **small_out}
    outs = [loss, grad_x[None]]
    for slot in range(4):
        outs += [res[nm][slot] for nm in order]
    return tuple(outs)
```

```python
import functools
import math

import jax
import jax.numpy as jnp
from jax import lax
from jax.experimental import pallas as pl
from jax.experimental.pallas import tpu as pltpu

F32 = jnp.float32
BF16 = jnp.bfloat16
MESH = pl.DeviceIdType.MESH

N_LAYERS = 2
ALPHA = (2 * N_LAYERS) ** 0.25
LN_EPS = 1e-5
NEG_INF = -1e30
CHUNK = 64
LEFT_CHUNKS = 8
PAD_ROWS = LEFT_CHUNKS * CHUNK
HEAD_DIM = 64
N_HEADS = 16
MAX_REL = 256
POOL_WINDOWS = (2, 4, 8, 16)
POOL_GROUP = 128
CONV_K = 31
FFN_K = 3
CONV_HALO = 32
FFN_HALO = 8
FFN_TILE = 256
FFN_CHUNK_ROWS = 32
FFN_CHUNK_LANES = 256
Q_TILE = 256
K_WIN = Q_TILE + PAD_ROWS
SHEAR_W = Q_TILE + K_WIN
SHEAR_SAT = SHEAR_W - 2 * MAX_REL
N_SHARD = 4
LANES = 128

ADAM_LR = 0.001
ADAM_B1 = 0.9
ADAM_B2 = 0.999
ADAM_EPS = 1e-08
ADAM_WD = 0.01
ADAM_STEP = 10
ADAM_BC1 = 1.0 - ADAM_B1 ** ADAM_STEP
ADAM_BC2 = 1.0 - ADAM_B2 ** ADAM_STEP

DIMS = {
    "nn": (((1,), (0,)), ((), ())),
    "nt": (((1,), (1,)), ((), ())),
    "tn": (((0,), (0,)), ((), ())),
}


def _cp(vmem_mb=48, **kw):
    return pltpu.CompilerParams(vmem_limit_bytes=vmem_mb * 1024 * 1024, **kw)


def _in_hbm(a):
    return pltpu.with_memory_space_constraint(a, pltpu.HBM)


def _hbm_result(s):
    return pltpu.HBM(s.shape, s.dtype) if isinstance(s, jax.ShapeDtypeStruct) else s


def _call(body, *, out_shape, **kw):
    if isinstance(out_shape, (list, tuple)):
        out_shape = type(out_shape)(_hbm_result(s) for s in out_shape)
    else:
        out_shape = _hbm_result(out_shape)
    call = pl.pallas_call(body, out_shape=out_shape, **kw)

    def run(*args):
        return call(*[a if jnp.issubdtype(a.dtype, jnp.integer) else _in_hbm(a) for a in args])

    return run


def _dot(a, b, mode):
    return lax.dot_general(a.astype(BF16), b.astype(BF16), DIMS[mode], preferred_element_type=F32)


def _sig(x):
    return 1.0 / (1.0 + jnp.exp(-x))


def _row_tile(s):
    return min(512, s // 4)


def _mm_tile(s):
    return min(1024, s // 4)


def _mm(name, mode, a, b, in_specs, out_shape, out_spec, acc_shape, grid, nk, zero_first=False, vmem_mb=48):
    out_f32 = out_shape.dtype == F32

    def body(a_ref, b_ref, o_ref, *scr):
        k = pl.program_id(2)

        def compute():
            part = _dot(a_ref[...], b_ref[...], mode)
            if nk == 1:
                o_ref[...] = part.astype(o_ref.dtype)
                return
            acc = o_ref if out_f32 else scr[0]

            @pl.when(k == 0)
            def _():
                acc[...] = part

            @pl.when(k > 0)
            def _():
                acc[...] += part

            if not out_f32:
                @pl.when(k == nk - 1)
                def _():
                    o_ref[...] = acc[...].astype(o_ref.dtype)

        if zero_first:
            @pl.when(pl.program_id(1) == 0)
            def _():
                o_ref[...] = jnp.zeros(o_ref.shape, o_ref.dtype)

            pl.when(pl.program_id(1) > 0)(compute)
        else:
            compute()

    scratch = [] if (nk == 1 or out_f32) else [pltpu.VMEM(acc_shape, F32)]
    return _call(
        body, name=name, grid=grid, in_specs=in_specs, out_specs=out_spec, out_shape=out_shape,
        scratch_shapes=scratch, compiler_params=_cp(vmem_mb),
    )(a, b)


def mm_cols_fwd(name, a, wc, out_dtype, pad_blocks=0, part=(0, 1)):
    s, k = a.shape
    s //= part[1]
    n4 = wc.shape[2]
    tm = _row_tile(s) if pad_blocks else _mm_tile(s)
    nt = s // tm
    first_block = part[0] * nt
    return _mm(
        name, "nn", a, wc,
        [pl.BlockSpec((tm, k), lambda j, i, r: (first_block + jnp.maximum(i - pad_blocks, 0), 0)),
         pl.BlockSpec((None, k, n4), lambda j, i, r: (j, 0, 0))],
        jax.ShapeDtypeStruct((s + pad_blocks * tm, N_SHARD * n4), out_dtype),
        pl.BlockSpec((tm, n4), lambda j, i, r: (i, j)),
        None, (N_SHARD, nt + pad_blocks, 1), 1, zero_first=pad_blocks > 0)


def mm_cols_dx(name, dy, wc):
    s = dy.shape[0]
    _, k, n4 = wc.shape
    tm = _mm_tile(s)
    return _mm(
        name, "nt", dy, wc,
        [pl.BlockSpec((tm, n4), lambda g, i, r: (i, r)),
         pl.BlockSpec((None, k, n4), lambda g, i, r: (r, 0, 0))],
        jax.ShapeDtypeStruct((s, k), F32),
        pl.BlockSpec((tm, k), lambda g, i, r: (i, 0)),
        (tm, k), (1, s // tm, N_SHARD), N_SHARD)


def mm_cols_dw(name, a, dy, part=(0, 1)):
    s, k = a.shape
    s //= part[1]
    n4 = dy.shape[1] // N_SHARD
    tm = _mm_tile(s)
    nt = s // tm
    first_block = part[0] * nt
    return _mm(
        name, "tn", a, dy,
        [pl.BlockSpec((tm, k), lambda j, g, r: (first_block + r, 0)),
         pl.BlockSpec((tm, n4), lambda j, g, r: (r, j))],
        jax.ShapeDtypeStruct((N_SHARD, k, n4), F32),
        pl.BlockSpec((None, k, n4), lambda j, g, r: (j, 0, 0)),
        (k, n4), (N_SHARD, 1, nt), nt)


def _k_tile(k):
    return k if k <= 1024 else k // 2


def mm_rows_fwd(name, a, wr, out_dtype=F32):
    s, k = a.shape
    n = wr.shape[1]
    tm = _mm_tile(s)
    tk = _k_tile(k)
    nk = k // tk
    return _mm(
        name, "nn", a, wr,
        [pl.BlockSpec((tm, tk), lambda g, i, r: (i, r)),
         pl.BlockSpec((tk, n), lambda g, i, r: (r, 0))],
        jax.ShapeDtypeStruct((s, n), out_dtype),
        pl.BlockSpec((tm, n), lambda g, i, r: (i, 0)),
        (tm, n), (1, s // tm, nk), nk)


def mm_rows_dx(name, dy, wr, out_dtype=F32):
    s, n = dy.shape
    k = wr.shape[0]
    tm = _mm_tile(s)
    tk = _k_tile(k)
    return _mm(
        name, "nt", dy, wr,
        [pl.BlockSpec((tm, n), lambda j, i, r: (i, 0)),
         pl.BlockSpec((tk, n), lambda j, i, r: (j, 0))],
        jax.ShapeDtypeStruct((s, k), out_dtype),
        pl.BlockSpec((tm, tk), lambda j, i, r: (i, j)),
        None, (k // tk, s // tm, 1), 1)


def mm_rows_dw(name, a, dy):
    s, k = a.shape
    n = dy.shape[1]
    tm = _mm_tile(s)
    tk = _k_tile(k)
    nt = s // tm
    return _mm(
        name, "tn", a, dy,
        [pl.BlockSpec((tm, tk), lambda j, g, r: (r, j)),
         pl.BlockSpec((tm, n), lambda j, g, r: (r, 0))],
        jax.ShapeDtypeStruct((k, n), F32),
        pl.BlockSpec((tk, n), lambda j, g, r: (j, 0)),
        (tk, n), (k // tk, 1, nt), nt)


def _row(tm, c, col=0):
    return pl.BlockSpec((tm, c), lambda i: (i, col))


def _full(shape):
    nd = len(shape)
    return pl.BlockSpec(shape, lambda i: (0,) * nd)


def _prev(tm, h, c, col=0):
    return pl.BlockSpec((h, c), lambda i: (jnp.maximum(i * (tm // h) - 1, 0), col))


def _next(tm, h, c, s, col=0):
    return pl.BlockSpec((h, c), lambda i: (jnp.minimum((i + 1) * (tm // h), s // h - 1), col))


def _acc_add(ref, first, val):
    @pl.when(first)
    def _():
        ref[...] = val

    @pl.when(jnp.logical_not(first))
    def _():
        ref[...] += val


def _colsum(v):
    return jnp.sum(v, axis=0, keepdims=True)


def _ln_stats(z):
    mu = jnp.mean(z, axis=-1, keepdims=True)
    zc = z - mu
    var = jnp.mean(zc * zc, axis=-1, keepdims=True)
    rstd = lax.rsqrt(var + LN_EPS)
    return zc * rstd, rstd


def _ln_bwd(dxhat, xhat, rstd):
    m1 = jnp.mean(dxhat, axis=-1, keepdims=True)
    m2 = jnp.mean(dxhat * xhat, axis=-1, keepdims=True)
    return rstd * (dxhat - m1 - xhat * m2)


def ln_fwd(name, x, f, g, b, ple=None):
    s, d = x.shape
    tm = _row_tile(s)
    n_in = 2 + (3 if ple is not None else 0)

    def body(*refs):
        x_ref, f_ref = refs[0], refs[1]
        g_ref, b_ref = refs[n_in], refs[n_in + 1]
        y_ref, xh_ref, rs_ref = refs[n_in + 2:]
        z = ALPHA * x_ref[...] + f_ref[...]
        if ple is not None:
            pgl_ref, pp_ref, bg_ref = refs[2:5]
            z = z + _sig(pgl_ref[...] + bg_ref[...]) * pp_ref[...]
        xhat, rstd = _ln_stats(z)
        y_ref[...] = xhat * g_ref[...] + b_ref[...]
        xh_ref[...] = xhat
        rs_ref[...] = jnp.broadcast_to(rstd, rs_ref.shape)

    ins = [x, f]
    specs = [_row(tm, d), _row(tm, d)]
    if ple is not None:
        pgl, pp, bg = ple
        ins += [pgl, pp, bg]
        specs += [_row(tm, d), _row(tm, d), _full((1, d))]
    ins += [g, b]
    specs += [_full((1, d)), _full((1, d))]
    return _call(
        body, name=name, grid=(s // tm,), in_specs=specs,
        out_specs=[_row(tm, d), _row(tm, d), _row(tm, LANES)],
        out_shape=[jax.ShapeDtypeStruct((s, d), F32), jax.ShapeDtypeStruct((s, d), F32),
                   jax.ShapeDtypeStruct((s, LANES), F32)],
        compiler_params=_cp(),
    )(*ins)


def ln_bwd(name, parts, xhat, rstd, g, ple=None):
    s, d = xhat.shape
    tm = _row_tile(s)
    coefs = [c for c, _ in parts]
    n_p = len(parts)
    n_in = n_p + 3 + (3 if ple is not None else 0)

    def body(*refs):
        first = pl.program_id(0) == 0
        dy = coefs[0] * refs[0][...].astype(F32)
        for j in range(1, n_p):
            dy = dy + coefs[j] * refs[j][...].astype(F32)
        xh = refs[n_p][...]
        rs = refs[n_p + 1][:, 0:1]
        g_v = refs[n_p + 2][...]
        outs = refs[n_in:]
        dz = _ln_bwd(dy * g_v, xh, rs)
        outs[0][...] = dz
        _acc_add(outs[1], first, _colsum(dy * xh))
        _acc_add(outs[2], first, _colsum(dy))
        if ple is not None:
            pgl_ref, pp_ref, bg_ref = refs[n_p + 3:n_p + 6]
            pg = _sig(pgl_ref[...] + bg_ref[...])
            dpgl = dz * pp_ref[...] * pg * (1.0 - pg)
            outs[3][...] = (dz * pg).astype(BF16)
            outs[4][...] = dpgl.astype(BF16)
            _acc_add(outs[5], first, _colsum(dpgl))

    ins = [p for _, p in parts] + [xhat, rstd, g]
    specs = [_row(tm, d)] * n_p + [_row(tm, d), _row(tm, LANES), _full((1, d))]
    out_specs = [_row(tm, d), _full((1, d)), _full((1, d))]
    out_shape = [jax.ShapeDtypeStruct((s, d), F32), jax.ShapeDtypeStruct((1, d), F32),
                 jax.ShapeDtypeStruct((1, d), F32)]
    if ple is not None:
        pgl, pp, bg = ple
        ins += [pgl, pp, bg]
        specs += [_row(tm, d), _row(tm, d), _full((1, d))]
        out_specs += [_row(tm, d), _row(tm, d), _full((1, d))]
        out_shape += [jax.ShapeDtypeStruct((s, d), BF16), jax.ShapeDtypeStruct((s, d), BF16),
                      jax.ShapeDtypeStruct((1, d), F32)]
    return _call(
        body, name=name, grid=(s // tm,), in_specs=specs, out_specs=out_specs, out_shape=out_shape,
        compiler_params=_cp(),
    )(*ins)


def loss_fwd_bwd(name, y, target):
    s, d = y.shape
    tm = _row_tile(s)

    def body(y_ref, t_ref, dy_ref, l_ref):
        first = pl.program_id(0) == 0
        err = y_ref[...] - t_ref[...]
        dy_ref[...] = err * (1.0 / d)
        part = 0.5 * jnp.sum(jnp.mean(err * err, axis=-1, keepdims=True), axis=0, keepdims=True)
        _acc_add(l_ref, first, jnp.broadcast_to(part, l_ref.shape))

    return _call(
        body, name=name, grid=(s // tm,), in_specs=[_row(tm, d), _row(tm, d)],
        out_specs=[_row(tm, d), _full((8, LANES))],
        out_shape=[jax.ShapeDtypeStruct((s, d), F32), jax.ShapeDtypeStruct((8, LANES), F32)],
        compiler_params=_cp(),
    )(y, target)


def scaled_sum(name, parts):
    s, d = parts[0][1].shape
    tm = _row_tile(s)
    coefs = [c for c, _ in parts]

    def body(*refs):
        acc = coefs[0] * refs[0][...].astype(F32)
        for j in range(1, len(coefs)):
            acc = acc + coefs[j] * refs[j][...].astype(F32)
        refs[-1][...] = acc

    return _call(
        body, name=name, grid=(s // tm,), in_specs=[_row(tm, d)] * len(parts), out_specs=_row(tm, d),
        out_shape=jax.ShapeDtypeStruct((s, d), F32), compiler_params=_cp(),
    )(*[p for _, p in parts])


def _tile_pos(i, tm, rows):
    return (i * tm + lax.broadcasted_iota(jnp.int32, (rows, 1), 0) + 1).astype(F32)


def mixer_fwd(name, u, pool_w, pool_scale, conv_w, conv_b, cn_g, cn_b):
    s = u.shape[0]
    dp = 512
    tm = min(256, s // 4)
    h = CONV_HALO

    def body(a_c, a_p, bv_c, bv_p, bg_c, bg_p, pw_ref, ps_ref, cw_ref, cb_ref, cg_ref, cbt_ref,
             cat_ref, d_ref, e_ref, glu_ref, hh_ref, rs_ref, ext_a, ext_g):
        i = pl.program_id(0)
        first = i == 0
        ext_a[0:h, :] = jnp.where(first, 0.0, a_p[...])
        ext_a[h:, :] = a_c[...]
        ext_g[0:h, :] = jnp.where(first, 0.0, bv_p[...] * _sig(bg_p[...]))
        glu = bv_c[...] * _sig(bg_c[...])
        ext_g[h:, :] = glu
        glu_ref[...] = glu
        pos = _tile_pos(i, tm, tm)
        for gi, w in enumerate(POOL_WINDOWS):
            cs = slice(gi * POOL_GROUP, (gi + 1) * POOL_GROUP)
            a_g = ext_a[pl.ds(h, tm), cs]
            acc = a_g
            for sh in range(1, w):
                acc = acc + ext_a[pl.ds(h - sh, tm), cs]
            d_g = acc / jnp.minimum(pos, float(w)) - a_g
            d_ref[:, cs] = d_g.astype(BF16)
            e_g = _dot(d_g, pw_ref[gi], "nn")
            e_ref[:, cs] = e_g
            cat_ref[:, cs] = (e_g * ps_ref[:, cs]).astype(BF16)
        hcv = jnp.broadcast_to(cb_ref[...], (tm, dp))
        for sh in range(CONV_K):
            hcv = hcv + ext_g[pl.ds(h - sh, tm), :] * cw_ref[pl.ds(CONV_K - 1 - sh, 1), :]
        hhat, rstd = _ln_stats(hcv)
        hl = hhat * cg_ref[...] + cbt_ref[...]
        cat_ref[:, dp:] = (hl * _sig(hl)).astype(BF16)
        hh_ref[...] = hhat
        rs_ref[...] = jnp.broadcast_to(rstd, rs_ref.shape)

    specs = [_row(tm, dp, 0), _prev(tm, h, dp, 0), _row(tm, dp, 1), _prev(tm, h, dp, 1),
             _row(tm, dp, 2), _prev(tm, h, dp, 2),
             _full((4, POOL_GROUP, POOL_GROUP)), _full((1, dp)), _full((CONV_K, dp)),
             _full((1, dp)), _full((1, dp)), _full((1, dp))]
    out_specs = [_row(tm, 2 * dp), _row(tm, dp), _row(tm, dp), _row(tm, dp), _row(tm, dp), _row(tm, LANES)]
    out_shape = [jax.ShapeDtypeStruct((s, 2 * dp), BF16), jax.ShapeDtypeStruct((s, dp), BF16),
                 jax.ShapeDtypeStruct((s, dp), F32), jax.ShapeDtypeStruct((s, dp), F32),
                 jax.ShapeDtypeStruct((s, dp), F32), jax.ShapeDtypeStruct((s, LANES), F32)]
    return _call(
        body, name=name, grid=(s // tm,), in_specs=specs, out_specs=out_specs, out_shape=out_shape,
        scratch_shapes=[pltpu.VMEM((h + tm, dp), F32), pltpu.VMEM((h + tm, dp), F32)],
        compiler_params=_cp(),
    )(u, u, u, u, u, u, pool_w, pool_scale, conv_w, conv_b, cn_g, cn_b)


def mixer_bwd(name, dcat, u, d_sv, e_sv, glu_sv, hh_sv, rs_sv, pool_w, pool_scale, conv_w, cn_g, cn_b):
    s = u.shape[0]
    dp = 512
    tm = min(256, s // 4)
    h = CONV_HALO
    nt = s // tm

    def body(dc_c, dc_n, bv_c, bg_c, d_c, e_c, gl_c, gl_p, hh_c, hh_n, rs_c, rs_n,
             pw_ref, ps_ref, cw_ref, cg_ref, cbt_ref,
             du_ref, dpw_ref, dps_ref, dcw_ref, dcb_ref, dcg_ref, dcbt_ref,
             ext_dh, ext_g, ext_r):
        i = pl.program_id(0)
        first = i == 0
        last = i == nt - 1
        cg = cg_ref[...]

        def conv_grads(dyb, hhat, rstd):
            hl = hhat * cg + cbt_ref[...]
            sg = _sig(hl)
            dhl = dyb * (sg * (1.0 + hl * (1.0 - sg)))
            return _ln_bwd(dhl * cg, hhat, rstd), dhl

        hh_cur = hh_c[...]
        dh_c, dhl_c = conv_grads(dc_c[:, dp:], hh_cur, rs_c[:, 0:1])
        dh_n, _ = conv_grads(dc_n[:, dp:], hh_n[...], rs_n[:, 0:1])
        ext_dh[0:tm, :] = dh_c
        ext_dh[tm:, :] = jnp.where(last, 0.0, dh_n)
        ext_g[0:h, :] = jnp.where(first, 0.0, gl_p[...])
        ext_g[h:, :] = gl_c[...]
        dglu = jnp.zeros((tm, dp), F32)
        for sh in range(CONV_K):
            dglu = dglu + ext_dh[pl.ds(sh, tm), :] * cw_ref[pl.ds(CONV_K - 1 - sh, 1), :]

        @pl.when(first)
        def _():
            dcw_ref[...] = jnp.zeros(dcw_ref.shape, F32)

        for sh in range(CONV_K):
            dcw_ref[pl.ds(CONV_K - 1 - sh, 1), :] += _colsum(dh_c * ext_g[pl.ds(h - sh, tm), :])
        _acc_add(dcb_ref, first, _colsum(dh_c))
        _acc_add(dcg_ref, first, _colsum(dhl_c * hh_cur))
        _acc_add(dcbt_ref, first, _colsum(dhl_c))
        sgate = _sig(bg_c[...])
        bv = bv_c[...]
        du_ref[:, dp:2 * dp] = dglu * sgate
        du_ref[:, 2 * dp:] = dglu * bv * sgate * (1.0 - sgate)

        pos_c = _tile_pos(i, tm, tm)
        pos_n = _tile_pos(i + 1, tm, h)
        _acc_add(dps_ref, first, _colsum(dc_c[:, :dp] * e_c[...]))
        for gi, w in enumerate(POOL_WINDOWS):
            cs = slice(gi * POOL_GROUP, (gi + 1) * POOL_GROUP)
            pw = pw_ref[gi]
            de_c = dc_c[:, cs] * ps_ref[:, cs]
            de_n = dc_n[:, cs] * ps_ref[:, cs]
            dd_c = _dot(de_c, pw, "nt")
            dd_n = _dot(de_n, pw, "nt")
            ext_r[0:tm, :] = dd_c / jnp.minimum(pos_c, float(w))
            ext_r[tm:, :] = jnp.where(last, 0.0, dd_n / jnp.minimum(pos_n, float(w)))
            acc = -dd_c
            for sh in range(w):
                acc = acc + ext_r[pl.ds(sh, tm), :]
            du_ref[:, cs] = acc
            dpw_g = _dot(d_c[:, cs], de_c, "tn")

            @pl.when(first)
            def _():
                dpw_ref[gi] = dpw_g

            @pl.when(jnp.logical_not(first))
            def _():
                dpw_ref[gi] += dpw_g

    specs = [_row(tm, 2 * dp), _next(tm, h, 2 * dp, s), _row(tm, dp, 1), _row(tm, dp, 2),
             _row(tm, dp), _row(tm, dp), _row(tm, dp), _prev(tm, h, dp),
             _row(tm, dp), _next(tm, h, dp, s), _row(tm, LANES), _next(tm, h, LANES, s),
             _full((4, POOL_GROUP, POOL_GROUP)), _full((1, dp)), _full((CONV_K, dp)),
             _full((1, dp)), _full((1, dp))]
    out_specs = [_row(tm, 3 * dp), _full((4, POOL_GROUP, POOL_GROUP)), _full((1, dp)), _full((CONV_K, dp)),
                 _full((1, dp)), _full((1, dp)), _full((1, dp))]
    out_shape = [jax.ShapeDtypeStruct((s, 3 * dp), F32),
                 jax.ShapeDtypeStruct((4, POOL_GROUP, POOL_GROUP), F32), jax.ShapeDtypeStruct((1, dp), F32),
                 jax.ShapeDtypeStruct((CONV_K, dp), F32), jax.ShapeDtypeStruct((1, dp), F32),
                 jax.ShapeDtypeStruct((1, dp), F32), jax.ShapeDtypeStruct((1, dp), F32)]
    return _call(
        body, name=name, grid=(nt,), in_specs=specs, out_specs=out_specs, out_shape=out_shape,
        scratch_shapes=[pltpu.VMEM((tm + h, dp), F32), pltpu.VMEM((h + tm, dp), F32),
                        pltpu.VMEM((tm + h, POOL_GROUP), F32)],
        compiler_params=_cp(),
    )(dcat, dcat, u, u, d_sv, e_sv, glu_sv, glu_sv, hh_sv, hh_sv, rs_sv, rs_sv,
      pool_w, pool_scale, conv_w, cn_g, cn_b)


GELU_C = math.sqrt(2.0 / math.pi)


def _gelu_parts(x):
    x2 = x * x
    t = jnp.tanh(x * (GELU_C + (GELU_C * 0.044715) * x2))
    half_1pt = 0.5 + 0.5 * t
    gelu = x * half_1pt
    dgelu = half_1pt + (0.5 * x) * (1.0 - t * t) * (GELU_C + (3.0 * GELU_C * 0.044715) * x2)
    return gelu, dgelu


def ffn_act_fwd(name, gv, dw_w, dw_b):
    s = gv.shape[0]
    dff = gv.shape[1] // 2
    tm = min(FFN_TILE, s // 4)
    h = FFN_HALO
    rc = FFN_CHUNK_ROWS
    lw = FFN_CHUNK_LANES

    def body(g_c, g_p, v_c, w_ref, b_ref, hid_ref):
        first = pl.program_id(0) == 0

        def chunk(ci, carry):
            r0 = pl.multiple_of(ci * rc, rc)
            above = pl.multiple_of(jnp.maximum(r0 - h, 0), h)
            for lg in range(dff // lw):
                cs = slice(lg * lw, (lg + 1) * lw)
                top = jnp.where(ci == 0, jnp.where(first, 0.0, g_p[:, cs]), g_c[pl.ds(above, h), cs])
                win = jnp.concatenate([top, g_c[pl.ds(r0, rc), cs]], axis=0)
                gc = jnp.broadcast_to(b_ref[:, cs], (rc, lw))
                for sh in range(FFN_K):
                    gc = gc + win[h - sh:h - sh + rc] * w_ref[pl.ds(FFN_K - 1 - sh, 1), cs]
                gelu, _ = _gelu_parts(gc)
                hid_ref[pl.ds(r0, rc), cs] = (gelu * v_c[pl.ds(r0, rc), cs]).astype(BF16)
            return carry

        lax.fori_loop(0, tm // rc, chunk, 0)

    return _call(
        body, name=name, grid=(s // tm,),
        in_specs=[_row(tm, dff, 0), _prev(tm, h, dff, 0), _row(tm, dff, 1), _full((FFN_K, dff)), _full((1, dff))],
        out_specs=_row(tm, dff), out_shape=jax.ShapeDtypeStruct((s, dff), BF16),
        compiler_params=_cp(),
    )(gv, gv, gv, dw_w, dw_b)


def ffn_act_bwd(name, dhid, gv, dw_w, dw_b):
    s = gv.shape[0]
    dff = gv.shape[1] // 2
    tm = min(FFN_TILE, s // 4)
    h = FFN_HALO
    nt = s // tm
    rc = FFN_CHUNK_ROWS
    lw = FFN_CHUNK_LANES
    n_chunks = tm // rc

    def body(dh_c, dh_n, g_p, g_c, g_n, v_c, v_n, w_ref, b_ref, dgv_ref, dw_ref, db_ref):
        i = pl.program_id(0)
        first = i == 0
        last = i == nt - 1

        @pl.when(first)
        def _():
            dw_ref[...] = jnp.zeros(dw_ref.shape, F32)
            db_ref[...] = jnp.zeros(db_ref.shape, F32)

        def chunk(ci, carry):
            r0 = pl.multiple_of(ci * rc, rc)
            above = pl.multiple_of(jnp.maximum(r0 - h, 0), h)
            below = pl.multiple_of(jnp.minimum(r0 + rc, tm - h), h)
            at_end = ci == n_chunks - 1
            for lg in range(dff // lw):
                cs = slice(lg * lw, (lg + 1) * lw)
                top = jnp.where(ci == 0, jnp.where(first, 0.0, g_p[:, cs]), g_c[pl.ds(above, h), cs])
                bot = jnp.where(at_end, g_n[:, cs], g_c[pl.ds(below, h), cs])
                win = jnp.concatenate([top, g_c[pl.ds(r0, rc), cs], bot], axis=0)
                shifted = [win[h - sh:h - sh + rc + h] for sh in range(FFN_K)]
                gc = jnp.broadcast_to(b_ref[:, cs], (rc + h, lw))
                for sh in range(FFN_K):
                    gc = gc + shifted[sh] * w_ref[pl.ds(FFN_K - 1 - sh, 1), cs]
                gelu, dgelu = _gelu_parts(gc)
                dh_mid = dh_c[pl.ds(r0, rc), cs]
                hv_bot = jnp.where(at_end, jnp.where(last, 0.0, dh_n[:, cs] * v_n[:, cs]),
                                   dh_c[pl.ds(below, h), cs] * v_c[pl.ds(below, h), cs])
                dgc = jnp.concatenate([dh_mid * v_c[pl.ds(r0, rc), cs], hv_bot], axis=0) * dgelu
                dgate = jnp.zeros((rc, lw), F32)
                for sh in range(FFN_K):
                    dgate = dgate + dgc[sh:sh + rc] * w_ref[pl.ds(FFN_K - 1 - sh, 1), cs]
                dgv_ref[pl.ds(r0, rc), cs] = dgate.astype(BF16)
                dgv_ref[pl.ds(r0, rc), slice(dff + lg * lw, dff + (lg + 1) * lw)] = (dh_mid * gelu[0:rc]).astype(BF16)
                dgc_mid = dgc[0:rc]
                for sh in range(FFN_K):
                    dw_ref[pl.ds(FFN_K - 1 - sh, 1), cs] += _colsum(dgc_mid * shifted[sh][0:rc])
                db_ref[:, cs] += _colsum(dgc_mid)
            return carry

        lax.fori_loop(0, n_chunks, chunk, 0)

    return _call(
        body, name=name, grid=(nt,),
        in_specs=[_row(tm, dff), _next(tm, h, dff, s),
                  _prev(tm, h, dff, 0), _row(tm, dff, 0), _next(tm, h, dff, s, 0),
                  _row(tm, dff, 1), _next(tm, h, dff, s, 1),
                  _full((FFN_K, dff)), _full((1, dff))],
        out_specs=[_row(tm, 2 * dff), _full((FFN_K, dff)), _full((1, dff))],
        out_shape=[jax.ShapeDtypeStruct((s, 2 * dff), BF16), jax.ShapeDtypeStruct((FFN_K, dff), F32),
                   jax.ShapeDtypeStruct((1, dff), F32)],
        compiler_params=_cp(),
    )(dhid, dhid, gv, gv, gv, gv, gv, dw_w, dw_b)


def _toeplitz_bias(rel_bias):
    nh = rel_bias.shape[0]
    zero = jnp.zeros((nh, 1), rel_bias.dtype)
    line = jnp.concatenate(
        [zero, jnp.broadcast_to(rel_bias[:, 2 * MAX_REL:], (nh, SHEAR_SAT)),
         jnp.flip(rel_bias[:, 1:2 * MAX_REL], axis=1), zero], axis=1)
    z = jnp.broadcast_to(line[:, None, :], (nh, Q_TILE, SHEAR_W + 1)).reshape(nh, Q_TILE * (SHEAR_W + 1))
    return z[:, :Q_TILE * SHEAR_W].reshape(nh, Q_TILE, SHEAR_W)[:, :, Q_TILE:]


def _shear_for_bias_grad(ds_sum):
    nh = ds_sum.shape[0]
    z = jnp.pad(ds_sum, ((0, 0), (0, 0), (Q_TILE, 0))).reshape(nh, Q_TILE * SHEAR_W)
    return jnp.pad(z, ((0, 0), (0, Q_TILE))).reshape(nh, Q_TILE, SHEAR_W + 1)


def _attn_mask(t):
    row = lax.broadcasted_iota(jnp.int32, (Q_TILE, K_WIN), 0)
    col = lax.broadcasted_iota(jnp.int32, (Q_TILE, K_WIN), 1)
    qc = row // CHUNK
    kc = col // CHUNK
    return (kc >= qc) & (kc <= qc + LEFT_CHUNKS) & (t * Q_TILE + col >= PAD_ROWS)


def _attn_probs(q2, k3, bias, mask, head):
    lane = lax.broadcasted_iota(jnp.int32, q2.shape, 1)
    q_h = jnp.where(lane // HEAD_DIM == head, q2, jnp.zeros_like(q2))
    sc = _dot(q_h, k3, "nt") * (HEAD_DIM ** -0.5) + bias
    sc = jnp.where(mask, sc, NEG_INF)
    m = jnp.max(sc, axis=-1, keepdims=True)
    p = jnp.exp(sc - m)
    return q_h, p / jnp.sum(p, axis=-1, keepdims=True)


def _attn_specs(d_model):
    nq = PAD_ROWS // Q_TILE
    hp_k = d_model // LANES
    specs = [pl.BlockSpec((Q_TILE, LANES), lambda hp, t: (t + nq, hp))]
    for which in (1, 2):
        for j in range(K_WIN // Q_TILE):
            specs.append(pl.BlockSpec((Q_TILE, LANES), lambda hp, t, j=j, which=which: (t + j, which * hp_k + hp)))
    specs.append(pl.BlockSpec((2, Q_TILE, K_WIN), lambda hp, t: (hp, 0, 0)))
    return specs


def attn_fwd(name, qkvp, bias):
    s = qkvp.shape[0] - PAD_ROWS
    d_model = qkvp.shape[1] // 3
    nw = K_WIN // Q_TILE

    def body(q_ref, *refs):
        k_refs, v_refs, b_ref, o_ref = refs[:nw], refs[nw:2 * nw], refs[2 * nw], refs[2 * nw + 1]
        t = pl.program_id(1)
        q2 = q_ref[...]
        k3 = jnp.concatenate([r[...] for r in k_refs], axis=0)
        v3 = jnp.concatenate([r[...] for r in v_refs], axis=0)
        mask = _attn_mask(t)
        outs = []
        for head in range(2):
            _, p = _attn_probs(q2, k3, b_ref[head], mask, head)
            outs.append(_dot(p, v3, "nn"))
        lane = lax.broadcasted_iota(jnp.int32, (Q_TILE, LANES), 1)
        o_ref[...] = jnp.where(lane < HEAD_DIM, outs[0], outs[1]).astype(BF16)

    return _call(
        body, name=name, grid=(d_model // LANES, s // Q_TILE),
        in_specs=_attn_specs(d_model), out_specs=pl.BlockSpec((Q_TILE, LANES), lambda hp, t: (t, hp)),
        out_shape=jax.ShapeDtypeStruct((s, d_model), BF16), compiler_params=_cp(),
    )(qkvp, *([qkvp] * (2 * nw)), bias)


def attn_bwd(name, qkvp, bias, do):
    s = qkvp.shape[0] - PAD_ROWS
    d_model = qkvp.shape[1] // 3
    nw = K_WIN // Q_TILE
    nt = s // Q_TILE
    scale = HEAD_DIM ** -0.5

    def body(q_ref, *refs):
        k_refs, v_refs = refs[:nw], refs[nw:2 * nw]
        b_ref, do_ref, dq_ref, dk_ref, dv_ref, ds_ref, dk_acc, dv_acc = refs[2 * nw:]
        t = pl.program_id(1)
        first = t == 0

        @pl.when(first)
        def _():
            dk_acc[...] = jnp.zeros(dk_acc.shape, F32)
            dv_acc[...] = jnp.zeros(dv_acc.shape, F32)

        q2 = q_ref[...]
        do2 = do_ref[...]
        k3 = jnp.concatenate([r[...] for r in k_refs], axis=0)
        v3 = jnp.concatenate([r[...] for r in v_refs], axis=0)
        mask = _attn_mask(t)
        lane = lax.broadcasted_iota(jnp.int32, (Q_TILE, LANES), 1)
        dqs = []
        dk_win = jnp.zeros((K_WIN, LANES), F32)
        dv_win = jnp.zeros((K_WIN, LANES), F32)
        for head in range(2):
            q_h, p = _attn_probs(q2, k3, b_ref[head], mask, head)
            do_h = jnp.where(lane // HEAD_DIM == head, do2, jnp.zeros_like(do2))
            dp = _dot(do_h, v3, "nt")
            ds = p * (dp - jnp.sum(p * dp, axis=-1, keepdims=True))
            _acc_add(ds_ref.at[head], first, ds)
            dsb = (ds * scale).astype(BF16)
            dqs.append(_dot(dsb, k3, "nn"))
            dk_win = dk_win + _dot(dsb, q_h, "tn")
            dv_win = dv_win + _dot(p, do_h, "tn")
        dq_ref[...] = jnp.where(lane < HEAD_DIM, dqs[0], dqs[1]).astype(BF16)
        start = pl.multiple_of(t * Q_TILE, Q_TILE)
        dk_acc[pl.ds(start, K_WIN), :] += dk_win
        dv_acc[pl.ds(start, K_WIN), :] += dv_win

        @pl.when(t == nt - 1)
        def _():
            dk_ref[...] = dk_acc[pl.ds(PAD_ROWS, s), :].astype(BF16)
            dv_ref[...] = dv_acc[pl.ds(PAD_ROWS, s), :].astype(BF16)

    specs = _attn_specs(d_model) + [pl.BlockSpec((Q_TILE, LANES), lambda hp, t: (t, hp))]
    col_spec = pl.BlockSpec((s, LANES), lambda hp, t: (0, hp))
    return _call(
        body, name=name, grid=(d_model // LANES, nt), in_specs=specs,
        out_specs=[pl.BlockSpec((Q_TILE, LANES), lambda hp, t: (t, hp)), col_spec, col_spec,
                   pl.BlockSpec((2, Q_TILE, K_WIN), lambda hp, t: (hp, 0, 0))],
        out_shape=[jax.ShapeDtypeStruct((s, d_model), BF16)] * 3
        + [jax.ShapeDtypeStruct((N_HEADS, Q_TILE, K_WIN), F32)],
        scratch_shapes=[pltpu.VMEM((PAD_ROWS + s, LANES), F32), pltpu.VMEM((PAD_ROWS + s, LANES), F32)],
        compiler_params=_cp(),
    )(qkvp, *([qkvp] * (2 * nw)), bias, do)


def bias_grad_reduce(name, sheared):
    nh, _, width = sheared.shape

    def body(x_ref, col_ref, sat_ref):
        cols = _colsum(x_ref[...])
        col_ref[...] = cols
        k = lax.broadcasted_iota(jnp.int32, cols.shape, 1)
        tot = jnp.sum(jnp.where((k >= 1) & (k <= SHEAR_SAT), cols, 0.0), axis=-1, keepdims=True)
        sat_ref[...] = jnp.broadcast_to(tot, sat_ref.shape)

    return _call(
        body, name=name, grid=(nh,),
        in_specs=[pl.BlockSpec((None, Q_TILE, width), lambda hh: (hh, 0, 0))],
        out_specs=[pl.BlockSpec((None, 1, width), lambda hh: (hh, 0, 0)),
                   pl.BlockSpec((None, 1, LANES), lambda hh: (hh, 0, 0))],
        out_shape=[jax.ShapeDtypeStruct((nh, 1, width), F32), jax.ShapeDtypeStruct((nh, 1, LANES), F32)],
        compiler_params=_cp(),
    )(sheared)


def _ew_rows(r, most=512):
    for cand in (512, 256, 128, 64, 32, 16, 8):
        if cand <= most and r % cand == 0:
            return cand
    return r


def cast_into_gathered(name, w, layer, s_idx):
    r, c = w.shape[-2:]
    tr = _ew_rows(r)

    def body(s_ref, w_ref, o_ref):
        o_ref[...] = w_ref[...].astype(BF16)

    grid_spec = pltpu.PrefetchScalarGridSpec(
        num_scalar_prefetch=1, grid=(r // tr,),
        in_specs=[pl.BlockSpec((None, tr, c), lambda i, s_ref: (layer, i, 0))],
        out_specs=pl.BlockSpec((None, tr, c), lambda i, s_ref: (s_ref[0], i, 0)))
    return _call(
        body, name=name, grid_spec=grid_spec, out_shape=jax.ShapeDtypeStruct((N_SHARD, r, c), BF16),
        compiler_params=_cp(),
    )(s_idx, w)


def adamw(name, w, grads, m, v):
    nl, r, c = w.shape
    tr = _ew_rows(r, 256)

    def body(*refs):
        w_ref, m_ref, v_ref = refs[0], refs[1], refs[2]
        g_refs = refs[3:3 + nl]
        d_ref, nm_ref, nv_ref = refs[3 + nl:]
        layer = pl.program_id(0)
        g = g_refs[0][...]
        for j in range(1, nl):
            g = jnp.where(layer == j, g_refs[j][...], g)
        nm = ADAM_B1 * m_ref[...] + (1.0 - ADAM_B1) * g
        nv = ADAM_B2 * v_ref[...] + (1.0 - ADAM_B2) * (g * g)
        m_hat = nm / ADAM_BC1
        v_hat = nv / ADAM_BC2
        d_ref[...] = -ADAM_LR * (m_hat / (jnp.sqrt(v_hat) + ADAM_EPS) + ADAM_WD * w_ref[...])
        nm_ref[...] = nm
        nv_ref[...] = nv

    p_spec = pl.BlockSpec((None, tr, c), lambda l, i: (l, i, 0))
    g_spec = pl.BlockSpec((tr, c), lambda l, i: (i, 0))
    return _call(
        body, name=name, grid=(nl, r // tr), in_specs=[p_spec] * 3 + [g_spec] * nl, out_specs=[p_spec] * 3,
        out_shape=[jax.ShapeDtypeStruct((nl, r, c), F32)] * 3, compiler_params=_cp(),
    )(w, m, v, *grads)


def sum_blocks(name, gathered, n_blocks):
    r = gathered.shape[0] // n_blocks
    c = gathered.shape[1]
    tr = _ew_rows(r)
    nt = r // tr

    def body(*refs):
        acc = refs[0][...]
        for j in range(1, n_blocks):
            acc = acc + refs[j][...]
        refs[-1][...] = acc

    specs = [pl.BlockSpec((tr, c), lambda i, j=j: (j * nt + i, 0)) for j in range(n_blocks)]
    return _call(
        body, name=name, grid=(nt,), in_specs=specs, out_specs=pl.BlockSpec((tr, c), lambda i: (i, 0)),
        out_shape=jax.ShapeDtypeStruct((r, c), F32), compiler_params=_cp(),
    )(*([gathered] * n_blocks))


def _place():
    return lax.axis_index("x"), lax.axis_index("y"), lax.axis_index("c")


def _other_chips(x, y):
    return [(1 - x, y), (x, 1 - y), (1 - x, 1 - y)]


HBM_SPEC = pl.BlockSpec(memory_space=pltpu.HBM)
SEM_SPEC = pl.BlockSpec(memory_space=pltpu.SEMAPHORE)
ANY_SPEC = pl.BlockSpec(memory_space=pl.ANY)
EFFECT = pltpu.SideEffectType.DATAFLOW_SIDE_EFFECTING


def copies_start(name, bufs, plan, n_copies):
    n = len(bufs)

    def body(*refs):
        send, recv = refs[n], refs[n + 1]
        token = refs[2 * n + 2]
        for k, (src, dst, peer, _) in enumerate(plan(refs[:n])):
            pltpu.make_async_remote_copy(
                src_ref=src, dst_ref=dst, send_sem=send.at[k], recv_sem=recv.at[k],
                device_id=peer, device_id_type=MESH).start()
        token[...] = jnp.zeros(token.shape, F32)

    outs = pl.pallas_call(
        body, name=name,
        out_shape=(pltpu.SemaphoreType.DMA((n_copies,)), pltpu.SemaphoreType.DMA((n_copies,)),
                   *[pltpu.HBM(b.shape, b.dtype) for b in bufs], jax.ShapeDtypeStruct((8, LANES), F32)),
        in_specs=[HBM_SPEC] * n,
        out_specs=(SEM_SPEC, SEM_SPEC, *([HBM_SPEC] * n), pl.BlockSpec(memory_space=pltpu.VMEM)),
        input_output_aliases={a: a + 2 for a in range(n)},
        compiler_params=pltpu.CompilerParams(has_side_effects=EFFECT),
    )(*[_in_hbm(b) for b in bufs])
    return outs[0], outs[1], list(outs[2:2 + n]), outs[2 + n]


def copies_wait(name, bufs, send, recv, plan, sem_base, after):
    n = len(bufs)

    def body(*refs):
        send_ref, recv_ref = refs[n], refs[n + 1]
        for k, (src, _, peer, land) in enumerate(plan(refs[:n])):
            cp = pltpu.make_async_remote_copy(
                src_ref=src, dst_ref=land, send_sem=send_ref.at[sem_base + k], recv_sem=recv_ref.at[sem_base + k],
                device_id=peer, device_id_type=MESH)
            cp.wait_send()
            cp.wait_recv()

    outs = pl.pallas_call(
        body, name=name,
        out_shape=tuple(pltpu.HBM(b.shape, b.dtype) for b in bufs),
        in_specs=[HBM_SPEC] * n + [SEM_SPEC, SEM_SPEC, ANY_SPEC], out_specs=tuple([HBM_SPEC] * n),
        input_output_aliases={a: a for a in range(n)},
        compiler_params=pltpu.CompilerParams(has_side_effects=EFFECT),
    )(*bufs, send, recv, after)
    return list(outs)


def gather_plan(refs):
    x, y, c = _place()
    me = 2 * x + y
    return [(buf.at[me], buf.at[me], (cx, cy, c), buf.at[2 * cx + cy])
            for buf in refs for cx, cy in _other_chips(x, y)]


def swap_plan(refs):
    x, y, c = _place()
    n = len(refs) // 2
    out = []
    for g, land in zip(refs[:n], refs[n:]):
        hr = g.shape[1] // 2
        out.append((g.at[:, pl.ds((1 - c) * hr, hr)], land, (x, y, 1 - c), land))
    return out


def owners_plan(refs):
    x, y, c = _place()
    n = len(refs) // 2
    return [(src.at[2 * cx + cy], land.at[j], (cx, cy, c), land.at[j])
            for src, land in zip(refs[:n], refs[n:]) for j, (cx, cy) in enumerate(_other_chips(x, y))]


def join_plan(refs):
    x, y, c = _place()
    out = []
    for buf in refs:
        hr = buf.shape[0] // 2
        mine = buf.at[pl.ds(c * hr, hr)]
        out.append((mine, mine, (x, y, 1 - c), buf.at[pl.ds((1 - c) * hr, hr)]))
    return out


def add_halves(name, grad, landed, c_idx):
    _, r, c = grad.shape
    hr = r // 2
    tr = _ew_rows(hr)
    nt = hr // tr

    def body(c_ref, g_ref, l_ref, o_ref, ob_ref):
        tot = g_ref[...] + l_ref[...]
        o_ref[...] = tot
        ob_ref[...] = tot.astype(BF16)

    blk = pl.BlockSpec((None, tr, c), lambda sh, i, c_ref: (sh, i, 0))
    grid_spec = pltpu.PrefetchScalarGridSpec(
        num_scalar_prefetch=1, grid=(N_SHARD, nt),
        in_specs=[pl.BlockSpec((None, tr, c), lambda sh, i, c_ref: (sh, c_ref[0] * nt + i, 0)), blk],
        out_specs=[blk, blk])
    return _call(
        body, name=name, grid_spec=grid_spec,
        out_shape=[jax.ShapeDtypeStruct((N_SHARD, hr, c), F32), jax.ShapeDtypeStruct((N_SHARD, hr, c), BF16)],
        compiler_params=_cp(),
    )(c_idx, grad, landed)


def add_owned(name, own, landed, sc_idx):
    _, hr, c = own.shape
    tr = _ew_rows(hr)
    nt = hr // tr

    def body(sc_ref, o_ref, l0, l1, l2, out_ref):
        out_ref[...] = ((o_ref[...] + l0[...].astype(F32)) + l1[...].astype(F32)) + l2[...].astype(F32)

    grid_spec = pltpu.PrefetchScalarGridSpec(
        num_scalar_prefetch=1, grid=(nt,),
        in_specs=[pl.BlockSpec((None, tr, c), lambda i, sc_ref: (sc_ref[0], i, 0))]
        + [pl.BlockSpec((None, tr, c), lambda i, sc_ref, j=j: (j, i, 0)) for j in range(3)],
        out_specs=pl.BlockSpec((tr, c), lambda i, sc_ref: (sc_ref[1] * nt + i, 0)))
    return _call(
        body, name=name, grid_spec=grid_spec, out_shape=jax.ShapeDtypeStruct((2 * hr, c), F32),
        compiler_params=_cp(),
    )(sc_idx, own, landed, landed, landed)


def gather_small(name, block):
    m_per, n = block.shape

    def body(x_ref, out_ref, send_sems, recv_sems, local_sem):
        x, y, c = _place()
        me, sibling = (x, y, c), (x, y, 1 - c)
        chips = _other_chips(x, y)

        def rows(px, py, pc):
            return out_ref.at[pl.ds((4 * px + 2 * py + pc) * m_per, m_per), :]

        def copy(k, blk, to, src=None):
            return pltpu.make_async_remote_copy(
                src_ref=rows(*blk) if src is None else src, dst_ref=rows(*blk),
                send_sem=send_sems.at[k], recv_sem=recv_sems.at[k], device_id=to, device_id_type=MESH)

        mine = pltpu.make_async_copy(x_ref, rows(*me), local_sem)
        mine.start()
        first = [copy(0, me, sibling, src=x_ref)]
        first += [copy(1 + j, me, (*chip, c), src=x_ref) for j, chip in enumerate(chips)]
        for cp in first:
            cp.start()
        passed = [copy(4 + j, (*chip, c), sibling) for j, chip in enumerate(chips)]
        for j, chip in enumerate(chips):
            copy(1 + j, (*chip, c), me).wait_recv()
            passed[j].start()
        copy(0, sibling, me).wait_recv()
        for j, chip in enumerate(chips):
            copy(4 + j, (*chip, 1 - c), me).wait_recv()
        for cp in first + passed:
            cp.wait_send()
        mine.wait()

    return _call(
        body, name=name, out_shape=jax.ShapeDtypeStruct((8 * m_per, n), block.dtype),
        in_specs=[ANY_SPEC], out_specs=ANY_SPEC,
        scratch_shapes=[pltpu.SemaphoreType.DMA((7,)), pltpu.SemaphoreType.DMA((7,)), pltpu.SemaphoreType.DMA],
        compiler_params=_cp(),
    )(block)


PACK_QUANTUM = 8 * LANES


def _pack(arrays):
    pieces = []
    for a in arrays:
        flat = a.reshape(-1)
        padded = -(-flat.shape[0] // PACK_QUANTUM) * PACK_QUANTUM
        pieces.append(jnp.pad(flat, (0, padded - flat.shape[0])).reshape(-1, LANES))
    return jnp.concatenate(pieces, axis=0)


def _unpack(packed, shapes):
    out = []
    row = 0
    for shp in shapes:
        size = math.prod(shp)
        rows = -(-size // PACK_QUANTUM) * 8
        out.append(packed[row:row + rows].reshape(-1)[:size].reshape(shp))
        row += rows
    return out


def kernel(x, p, mix_w_in, pool_w, pool_scale, conv_dw_w, conv_dw_b, conv_ln_g, conv_ln_b, mix_w_out, attn_w_qkv, attn_rel_bias, attn_w_o, ln_mix_g, ln_mix_b, ffn_w_up, ffn_dw_w, ffn_dw_b, ffn_w_down, ple_w_proj, ple_w_gate, ple_b_gate, ln_ffn_g, ln_ffn_b, loss_target, m_mix_w_in, m_pool_w, m_pool_scale, m_conv_dw_w, m_conv_dw_b, m_conv_ln_g, m_conv_ln_b, m_mix_w_out, m_attn_w_qkv, m_attn_rel_bias, m_attn_w_o, m_ln_mix_g, m_ln_mix_b, m_ffn_w_up, m_ffn_dw_w, m_ffn_dw_b, m_ffn_w_down, m_ple_w_proj, m_ple_w_gate, m_ple_b_gate, m_ln_ffn_g, m_ln_ffn_b, v_mix_w_in, v_pool_w, v_pool_scale, v_conv_dw_w, v_conv_dw_b, v_conv_ln_g, v_conv_ln_b, v_mix_w_out, v_attn_w_qkv, v_attn_rel_bias, v_attn_w_o, v_ln_mix_g, v_ln_mix_b, v_ffn_w_up, v_ffn_dw_w, v_ffn_dw_b, v_ffn_w_down, v_ple_w_proj, v_ple_w_gate, v_ple_b_gate, v_ln_ffn_g, v_ln_ffn_b):
    xi, yi, ci = _place()
    shard_idx = (2 * xi + yi).astype(jnp.int32)
    s_arr = shard_idx.reshape(1)
    c_arr = ci.astype(jnp.int32).reshape(1)
    sc_arr = jnp.concatenate([s_arr, c_arr])

    x0 = x[0]
    target = loss_target[0]
    p_rows = p.reshape(p.shape[0] * p.shape[2], p.shape[3])
    seq = x0.shape[0]

    big = [
        ("mix_w_in", mix_w_in, m_mix_w_in, v_mix_w_in, True),
        ("mix_w_out", mix_w_out, m_mix_w_out, v_mix_w_out, False),
        ("attn_w_qkv", attn_w_qkv, m_attn_w_qkv, v_attn_w_qkv, True),
        ("attn_w_o", attn_w_o, m_attn_w_o, v_attn_w_o, False),
        ("ffn_w_up", ffn_w_up, m_ffn_w_up, v_ffn_w_up, True),
        ("ffn_w_down", ffn_w_down, m_ffn_w_down, v_ffn_w_down, False),
        ("ple_w_proj", ple_w_proj, m_ple_w_proj, v_ple_w_proj, True),
        ("ple_w_gate", ple_w_gate, m_ple_w_gate, v_ple_w_gate, False),
    ]
    params = {nm: w for nm, w, _, _, _ in big}
    col_sharded = {nm: cs for nm, _, _, _, cs in big}
    keys = [("mix_w_in", 0), ("mix_w_out", 0), ("ffn_w_up", 0), ("ffn_w_down", 0), ("ple_w_gate", 0),
            ("ple_w_proj", 0), ("attn_w_qkv", 0), ("attn_w_o", 0), ("ffn_w_up", 1), ("ffn_w_down", 1),
            ("ple_w_gate", 1), ("ple_w_proj", 1)]
    shards = [cast_into_gathered(f"cast_{nm}_{layer}", params[nm], layer, s_arr) for nm, layer in keys]
    g_send, g_recv, g_bufs, _ = copies_start("gather_start", shards, gather_plan, 3 * len(keys))
    arrived_w = {}

    def weight(nm, layer, after=None):
        key = (nm, layer)
        if key not in arrived_w:
            a = keys.index(key)
            arrived_w[key] = copies_wait(f"gather_wait_{nm}_{layer}", [g_bufs[a]], g_send, g_recv, gather_plan,
                                         3 * a, after)[0]
        g = arrived_w[key]
        if col_sharded[nm]:
            return g
        return g.reshape(g.shape[0] * g.shape[1], g.shape[2])

    def tie(a, token):
        return a + token[0:1, 0:1].astype(a.dtype)

    class Reducer:
        def __init__(self, tag, group):
            self.tag, self.group, self.stage = tag, group, 0
            self.n = len(group)
            self.result = None

        def advance(self, after):
            tag, n = self.tag, self.n
            if self.stage == 0:
                grads = []
                for key in self.group:
                    g = big_grads[key]
                    grads.append(g if g.ndim == 3 else g.reshape(N_SHARD, g.shape[0] // N_SHARD, g.shape[1]))
                lands = [lax.empty((N_SHARD, g.shape[1] // 2, g.shape[2]), F32) for g in grads]
                self.sems = copies_start(f"swap_start_{tag}", grads + lands, swap_plan, n)
            elif self.stage == 1:
                send, recv, bufs, _ = self.sems
                outs = copies_wait(f"swap_wait_{tag}", bufs, send, recv, swap_plan, 0, after)
                self.own, wire = [], []
                for key, g, ld in zip(self.group, outs[:n], outs[n:]):
                    o, ob = add_halves(f"add_halves_{key[0]}_{key[1]}", g, ld, c_arr)
                    self.own.append(o)
                    wire.append(ob)
                lands = [lax.empty((3,) + w.shape[1:], BF16) for w in wire]
                self.sems = copies_start(f"owners_start_{tag}", wire + lands, owners_plan, 3 * n)
            elif self.stage == 2:
                send, recv, bufs, _ = self.sems
                outs = copies_wait(f"owners_wait_{tag}", bufs, send, recv, owners_plan, 0, after)
                finals = [add_owned(f"add_owned_{key[0]}_{key[1]}", o, ar, sc_arr)
                          for key, o, ar in zip(self.group, self.own, outs[n:])]
                self.sems = copies_start(f"join_start_{tag}", finals, join_plan, n)
            elif self.stage == 3:
                send, recv, bufs, _ = self.sems
                outs = copies_wait(f"join_wait_{tag}", bufs, send, recv, join_plan, 0, after)
                self.result = dict(zip(self.group, outs))
                self.sems = None
            self.stage += 1
            return None if self.sems is None else self.sems[3]

    dw_shapes = [conv_dw_w.shape, ffn_dw_w.shape]
    dw_packed = _pack([conv_dw_w, ffn_dw_w])
    dw_rows = dw_packed.shape[0]
    dw_all = gather_small("gather_dw", dw_packed)
    dw_parts = [_unpack(dw_all[2 * k * dw_rows:(2 * k + 1) * dw_rows], dw_shapes) for k in range(N_SHARD)]
    conv_w_full = jnp.concatenate([pc[0] for pc in dw_parts], axis=2)[0]
    ffn_dw_full = jnp.concatenate([pc[1] for pc in dw_parts], axis=2)

    big_grads = {}
    small_grads = {}

    saved = []
    h_in = x0
    for layer in range(N_LAYERS):
        sv = {"x_in": h_in}
        if layer % 2 == 0:
            u = mm_cols_fwd("mix_in", h_in, weight("mix_w_in", 0, h_in), F32)
            cat, d_sv, e_sv, glu_sv, hh_sv, rs_sv = mixer_fwd(
                "mixer_fwd", u, pool_w[0], pool_scale, conv_w_full, conv_dw_b, conv_ln_g, conv_ln_b)
            mix = mm_rows_fwd("mix_out", cat, weight("mix_w_out", 0, cat))
            sv.update(u=u, cat=cat, d=d_sv, e=e_sv, glu=glu_sv, hh=hh_sv, rs=rs_sv)
        else:
            qkvp = mm_cols_fwd("attn_qkv", h_in, weight("attn_w_qkv", 0, h_in), BF16,
                               pad_blocks=PAD_ROWS // _row_tile(seq))
            bias = _toeplitz_bias(attn_rel_bias[0])
            att = attn_fwd("attn_fwd", qkvp, bias)
            mix = mm_rows_fwd("attn_out", att, weight("attn_w_o", 0, att))
            sv.update(qkvp=qkvp, bias=bias, att=att)
        x1, xh1, rs1 = ln_fwd(f"ln_mix_{layer}", h_in, mix, ln_mix_g[layer:layer + 1], ln_mix_b[layer:layer + 1])
        gv = mm_cols_fwd(f"ffn_up_{layer}", x1, weight("ffn_w_up", layer, x1), F32)
        hid = ffn_act_fwd(f"ffn_act_{layer}", gv, ffn_dw_full[layer], ffn_dw_b[layer:layer + 1])
        ffn = mm_rows_fwd(f"ffn_down_{layer}", hid, weight("ffn_w_down", layer, hid))
        pgl = mm_rows_fwd(f"ple_gate_{layer}", x1, weight("ple_w_gate", layer, ffn))
        pp = mm_cols_fwd(f"ple_proj_{layer}", p_rows, weight("ple_w_proj", layer, pgl), F32, part=(layer, N_LAYERS))
        bg = ple_b_gate[layer:layer + 1]
        x2, xh2, rs2 = ln_fwd(f"ln_ffn_{layer}", x1, ffn, ln_ffn_g[layer:layer + 1], ln_ffn_b[layer:layer + 1],
                              ple=(pgl, pp, bg))
        sv.update(x1=x1, xh1=xh1, rs1=rs1, gv=gv, hid=hid, pgl=pgl, pp=pp, xh2=xh2, rs2=rs2)
        saved.append(sv)
        h_in = x2

    dy, loss_part = loss_fwd_bwd("loss", h_in, target)

    reducers = []

    def open_group(tag, group):
        reducers.append(Reducer(tag, group))
        return reducers[-1].advance(None)

    def hook(after):
        token = None
        for red in reducers:
            if red.stage < 4:
                tk = red.advance(after)
                if tk is not None:
                    token = tk if token is None else token + tk
        return token

    def tied(a, token):
        return a if token is None else tie(a, token)

    parts = [(1.0, dy)]
    token = None
    for layer in reversed(range(N_LAYERS)):
        sv = saved[layer]
        bg = ple_b_gate[layer:layer + 1]
        if layer == 0:
            token = open_group("layer1", [("attn_w_qkv", 0), ("attn_w_o", 0), ("ffn_w_up", 1), ("ffn_w_down", 1),
                                          ("ple_w_gate", 1), ("ple_w_proj", 1)])
        dz2, dg2, db2, dpp, dpgl, dbg = ln_bwd(
            f"ln_ffn_bwd_{layer}", parts, sv["xh2"], sv["rs2"], tied(ln_ffn_g[layer:layer + 1], token),
            ple=(sv["pgl"], sv["pp"], bg))
        small_grads[("ln_ffn_g", layer)] = dg2
        small_grads[("ln_ffn_b", layer)] = db2
        small_grads[("ple_b_gate", layer)] = dbg
        w_down = weight("ffn_w_down", layer)
        dhid = mm_rows_dx(f"ffn_down_dx_{layer}", dz2, w_down)
        big_grads[("ffn_w_down", layer)] = mm_rows_dw(f"ffn_down_dw_{layer}", sv["hid"], dz2)
        token = hook(big_grads[("ffn_w_down", layer)])
        dgv, ddw, ddb = ffn_act_bwd(f"ffn_act_bwd_{layer}", dhid, sv["gv"], ffn_dw_full[layer],
                                    tied(ffn_dw_b[layer:layer + 1], token))
        small_grads[("ffn_dw_w", layer)] = ddw
        small_grads[("ffn_dw_b", layer)] = ddb
        big_grads[("ffn_w_up", layer)] = mm_cols_dw(f"ffn_up_dw_{layer}", sv["x1"], dgv)
        t_up = mm_cols_dx(f"ffn_up_dx_{layer}", dgv, weight("ffn_w_up", layer))
        token = hook(t_up)
        big_grads[("ple_w_gate", layer)] = mm_rows_dw(f"ple_gate_dw_{layer}", sv["x1"], dpgl)
        t_gate = mm_rows_dx(f"ple_gate_dx_{layer}", dpgl, weight("ple_w_gate", layer))
        big_grads[("ple_w_proj", layer)] = mm_cols_dw(f"ple_proj_dw_{layer}", p_rows, dpp, part=(layer, N_LAYERS))
        token2 = hook(big_grads[("ple_w_proj", layer)])
        if token2 is not None:
            token = token2 if token is None else token + token2
        if layer == 0:
            token3 = open_group("layer0_ffn", [("ffn_w_up", 0), ("ffn_w_down", 0), ("ple_w_gate", 0), ("ple_w_proj", 0)])
            token = token3 if token is None else token + token3
        dz1, dg1, db1 = ln_bwd(
            f"ln_mix_bwd_{layer}", [(ALPHA, dz2), (1.0, t_up), (1.0, t_gate)], sv["xh1"], sv["rs1"],
            tied(ln_mix_g[layer:layer + 1], token))
        small_grads[("ln_mix_g", layer)] = dg1
        small_grads[("ln_mix_b", layer)] = db1
        if layer % 2 == 0:
            dcat = mm_rows_dx("mix_out_dx", dz1, weight("mix_w_out", 0))
            big_grads[("mix_w_out", 0)] = mm_rows_dw("mix_out_dw", sv["cat"], dz1)
            token = hook(big_grads[("mix_w_out", 0)])
            du, dpw, dps, dcw, dcb, dcg, dcbt = mixer_bwd(
                "mixer_bwd", dcat, sv["u"], sv["d"], sv["e"], sv["glu"], sv["hh"], sv["rs"],
                pool_w[0], pool_scale, conv_w_full, tied(conv_ln_g, token), conv_ln_b)
            small_grads[("pool_w", 0)] = dpw
            small_grads[("pool_scale", 0)] = dps
            small_grads[("conv_dw_w", 0)] = dcw
            small_grads[("conv_dw_b", 0)] = dcb
            small_grads[("conv_ln_g", 0)] = dcg
            small_grads[("conv_ln_b", 0)] = dcbt
            big_grads[("mix_w_in", 0)] = mm_cols_dw("mix_in_dw", sv["x_in"], du)
            hook(big_grads[("mix_w_in", 0)])
            open_group("layer0_mix", [("mix_w_in", 0), ("mix_w_out", 0)])
            t_mix = mm_cols_dx("mix_in_dx", du, weight("mix_w_in", 0))
            hook(t_mix)
        else:
            do = mm_rows_dx("attn_out_dx", dz1, weight("attn_w_o", 0), out_dtype=BF16)
            big_grads[("attn_w_o", 0)] = mm_rows_dw("attn_out_dw", sv["att"], dz1)
            dq, dk, dv, ds_sum = attn_bwd("attn_bwd", sv["qkvp"], sv["bias"], do)
            cols, sat = bias_grad_reduce("bias_grad", _shear_for_bias_grad(ds_sum))
            d_rel = jnp.concatenate(
                [jnp.zeros((N_HEADS, 1), F32),
                 jnp.flip(cols[:, 0, SHEAR_SAT + 1:SHEAR_W], axis=1),
                 sat[:, 0, 0:1]], axis=1)
            small_grads[("attn_rel_bias", 0)] = d_rel
            dqkv = jnp.concatenate([dq, dk, dv], axis=1)
            big_grads[("attn_w_qkv", 0)] = mm_cols_dw("attn_qkv_dw", sv["x_in"], dqkv)
            t_mix = mm_cols_dx("attn_qkv_dx", dqkv, weight("attn_w_qkv", 0))
        parts = [(ALPHA, dz1), (1.0, t_mix)]
    grad_x = scaled_sum("grad_x", parts)
    hook(grad_x)
    hook(grad_x)
    shard_grads = {}
    for red in reducers:
        shard_grads.update(red.result)

    big_out = {}
    for nm, w, m, v, _ in big:
        gl = [shard_grads[(nm, layer)] for layer in range(w.shape[0])]
        delta, new_m, new_v = adamw(f"adamw_{nm}", w, gl, m, v)
        big_out[nm] = (jnp.stack(gl, axis=0), delta, new_m, new_v)

    small = [
        ("pool_w", pool_w, m_pool_w, v_pool_w, None),
        ("pool_scale", pool_scale, m_pool_scale, v_pool_scale, None),
        ("conv_dw_w", conv_dw_w, m_conv_dw_w, v_conv_dw_w, 2),
        ("conv_dw_b", conv_dw_b, m_conv_dw_b, v_conv_dw_b, None),
        ("conv_ln_g", conv_ln_g, m_conv_ln_g, v_conv_ln_g, None),
        ("conv_ln_b", conv_ln_b, m_conv_ln_b, v_conv_ln_b, None),
        ("attn_rel_bias", attn_rel_bias, m_attn_rel_bias, v_attn_rel_bias, None),
        ("ln_mix_g", ln_mix_g, m_ln_mix_g, v_ln_mix_g, None),
        ("ln_mix_b", ln_mix_b, m_ln_mix_b, v_ln_mix_b, None),
        ("ffn_dw_w", ffn_dw_w, m_ffn_dw_w, v_ffn_dw_w, 2),
        ("ffn_dw_b", ffn_dw_b, m_ffn_dw_b, v_ffn_dw_b, None),
        ("ple_b_gate", ple_b_gate, m_ple_b_gate, v_ple_b_gate, None),
        ("ln_ffn_g", ln_ffn_g, m_ln_ffn_g, v_ln_ffn_g, None),
        ("ln_ffn_b", ln_ffn_b, m_ln_ffn_b, v_ln_ffn_b, None),
    ]
    full_grads = []
    for nm, w, _, _, shard_axis in small:
        full = list(w.shape)
        if shard_axis is not None:
            full[shard_axis] *= N_SHARD
        per_layer = [small_grads[(nm, layer)].reshape((1,) + tuple(full[1:])) for layer in range(w.shape[0])]
        full_grads.append(jnp.concatenate(per_layer, axis=0))
    packed = _pack(full_grads + [loss_part])
    total = sum_blocks("sum_small", gather_small("gather_small_grads", packed), 8)
    unpacked = _unpack(total, [g.shape for g in full_grads] + [loss_part.shape])
    loss = unpacked[-1][0, 0]
    local_grads = []
    for (nm, w, _, _, shard_axis), g in zip(small, unpacked[:-1]):
        if shard_axis is not None:
            width = w.shape[shard_axis]
            g = lax.dynamic_slice_in_dim(g, shard_idx * width, width, axis=shard_axis)
        local_grads.append(g.reshape(w.shape))
    shapes = [w.shape for _, w, _, _, _ in small]
    pg = _pack(local_grads)
    pw = _pack([w for _, w, _, _, _ in small])
    pm = _pack([m for _, _, m, _, _ in small])
    pv = _pack([v for _, _, _, v, _ in small])
    delta_s, new_m_s, new_v_s = adamw("adamw_small", pw[None], [pg], pm[None], pv[None])
    small_out = {}
    for (nm, _, _, _, _), g, d_, m_, v_ in zip(
            small, local_grads, _unpack(delta_s[0], shapes), _unpack(new_m_s[0], shapes), _unpack(new_v_s[0], shapes)):
        small_out[nm] = (g, d_, m_, v_)

    order = ["mix_w_in", "pool_w", "pool_scale", "conv_dw_w", "conv_dw_b", "conv_ln_g", "conv_ln_b", "mix_w_out",
             "attn_w_qkv", "attn_rel_bias", "attn_w_o", "ln_mix_g", "ln_mix_b", "ffn_w_up", "ffn_dw_w", "ffn_dw_b",
             "ffn_w_down", "ple_w_proj", "ple_w_gate", "ple_b_gate", "ln_ffn_g", "ln_ffn_b"]
    res = {**big_out, **small_out}
    outs = [loss, grad_x[None]]
    for slot in range(4):
        outs += [res[nm][slot] for nm in order]
    return tuple(outs)
```

```python
import functools
import math

import jax
import jax.numpy as jnp
from jax import lax
from jax.experimental import pallas as pl
from jax.experimental.pallas import tpu as pltpu

F32 = jnp.float32
BF16 = jnp.bfloat16
MESH = pl.DeviceIdType.MESH

N_LAYERS = 2
ALPHA = (2 * N_LAYERS) ** 0.25
LN_EPS = 1e-5
NEG_INF = -1e30
CHUNK = 64
LEFT_CHUNKS = 8
PAD_ROWS = LEFT_CHUNKS * CHUNK
HEAD_DIM = 64
N_HEADS = 16
MAX_REL = 256
POOL_WINDOWS = (2, 4, 8, 16)
POOL_GROUP = 128
CONV_K = 31
FFN_K = 3
CONV_HALO = 32
FFN_HALO = 8
FFN_TILE = 256
FFN_CHUNK_ROWS = 32
FFN_CHUNK_LANES = 256
Q_TILE = 256
K_WIN = Q_TILE + PAD_ROWS
SHEAR_W = Q_TILE + K_WIN
SHEAR_SAT = SHEAR_W - 2 * MAX_REL
N_SHARD = 4
LANES = 128
SUBLANES = 8

ADAM_LR = 0.001
ADAM_B1 = 0.9
ADAM_B2 = 0.999
ADAM_EPS = 1e-08
ADAM_WD = 0.01
ADAM_STEP = 10
ADAM_BC1 = 1.0 - ADAM_B1 ** ADAM_STEP
ADAM_BC2 = 1.0 - ADAM_B2 ** ADAM_STEP

DIMS = {
    "nn": (((1,), (0,)), ((), ())),
    "nt": (((1,), (1,)), ((), ())),
    "tn": (((0,), (0,)), ((), ())),
}


def _cp(vmem_mb=48, **kw):
    return pltpu.CompilerParams(vmem_limit_bytes=vmem_mb * 1024 * 1024, **kw)


def _in_hbm(a):
    return pltpu.with_memory_space_constraint(a, pltpu.HBM)


def _call(body, **kw):
    return pl.pallas_call(body, **kw)


def _dot(a, b, mode):
    return lax.dot_general(a.astype(BF16), b.astype(BF16), DIMS[mode], preferred_element_type=F32)


def _sig(x):
    return 1.0 / (1.0 + jnp.exp(-x))


def _row_tile(s):
    return min(512, s // 4)


def _mm_tile(s):
    return min(1024, s // 4)


def _mm(name, mode, a, b, in_specs, out_shape, out_spec, acc_shape, grid, nk, zero_first=False, vmem_mb=48,
        addend=None):
    out_f32 = out_shape.dtype == F32

    def body(a_ref, b_ref, *rest):
        k = pl.program_id(2)
        if addend is None:
            o_ref, scr = rest[0], rest[1:]
        else:
            add_ref, o_ref, scr = rest[0], rest[1], rest[2:]

        def compute():
            part = _dot(a_ref[...], b_ref[...], mode)
            if nk == 1:
                if addend is not None:
                    part = part + addend[0] * add_ref[...]
                o_ref[...] = part.astype(o_ref.dtype)
                return
            acc = o_ref if out_f32 else scr[0]

            @pl.when(k == 0)
            def _():
                acc[...] = part if addend is None else part + addend[0] * add_ref[...]

            @pl.when(k > 0)
            def _():
                acc[...] += part

            if not out_f32:
                @pl.when(k == nk - 1)
                def _():
                    o_ref[...] = acc[...].astype(o_ref.dtype)

        if zero_first:
            @pl.when(pl.program_id(1) == 0)
            def _():
                o_ref[...] = jnp.zeros(o_ref.shape, o_ref.dtype)

            pl.when(pl.program_id(1) > 0)(compute)
        else:
            compute()

    scratch = [] if (nk == 1 or out_f32) else [pltpu.VMEM(acc_shape, F32)]
    operands = [a, b] if addend is None else [a, b, addend[1]]
    specs = list(in_specs) if addend is None else list(in_specs) + [out_spec]
    return _call(
        body, name=name, grid=grid, in_specs=specs, out_specs=out_spec, out_shape=out_shape,
        scratch_shapes=scratch, compiler_params=_cp(vmem_mb),
    )(*operands)


def mm_cols_fwd(name, a, wc, out_dtype, pad_blocks=0, part=(0, 1)):
    s, k = a.shape
    s //= part[1]
    n4 = wc.shape[2]
    tm = _row_tile(s) if pad_blocks else _mm_tile(s)
    nt = s // tm
    first_block = part[0] * nt
    return _mm(
        name, "nn", a, wc,
        [pl.BlockSpec((tm, k), lambda j, i, r: (first_block + jnp.maximum(i - pad_blocks, 0), 0)),
         pl.BlockSpec((None, k, n4), lambda j, i, r: (j, 0, 0))],
        jax.ShapeDtypeStruct((s + pad_blocks * tm, N_SHARD * n4), out_dtype),
        pl.BlockSpec((tm, n4), lambda j, i, r: (i, j)),
        None, (N_SHARD, nt + pad_blocks, 1), 1, zero_first=pad_blocks > 0)


def mm_cols_dx(name, dy, wc, addend=None):
    s = dy.shape[0]
    _, k, n4 = wc.shape
    tm = _mm_tile(s)
    return _mm(
        name, "nt", dy, wc,
        [pl.BlockSpec((tm, n4), lambda g, i, r: (i, r)),
         pl.BlockSpec((None, k, n4), lambda g, i, r: (r, 0, 0))],
        jax.ShapeDtypeStruct((s, k), F32),
        pl.BlockSpec((tm, k), lambda g, i, r: (i, 0)),
        (tm, k), (1, s // tm, N_SHARD), N_SHARD, addend=addend)


def mm_cols_dw(name, a, dy, part=(0, 1)):
    s, k = a.shape
    s //= part[1]
    n4 = dy.shape[1] // N_SHARD
    tm = _mm_tile(s)
    nt = s // tm
    first_block = part[0] * nt
    return _mm(
        name, "tn", a, dy,
        [pl.BlockSpec((tm, k), lambda j, g, r: (first_block + r, 0)),
         pl.BlockSpec((tm, n4), lambda j, g, r: (r, j))],
        jax.ShapeDtypeStruct((N_SHARD, k, n4), F32),
        pl.BlockSpec((None, k, n4), lambda j, g, r: (j, 0, 0)),
        (k, n4), (N_SHARD, 1, nt), nt)


def _k_tile(k):
    return k if k <= 1024 else k // 2


def mm_rows_fwd(name, a, wr, out_dtype=F32):
    s, k = a.shape
    n = wr.shape[1]
    tm = _mm_tile(s)
    tk = _k_tile(k)
    nk = k // tk
    return _mm(
        name, "nn", a, wr,
        [pl.BlockSpec((tm, tk), lambda g, i, r: (i, r)),
         pl.BlockSpec((tk, n), lambda g, i, r: (r, 0))],
        jax.ShapeDtypeStruct((s, n), out_dtype),
        pl.BlockSpec((tm, n), lambda g, i, r: (i, 0)),
        (tm, n), (1, s // tm, nk), nk)


def mm_rows_dx(name, dy, wr, out_dtype=F32):
    s, n = dy.shape
    k = wr.shape[0]
    tm = _mm_tile(s)
    tk = _k_tile(k)
    return _mm(
        name, "nt", dy, wr,
        [pl.BlockSpec((tm, n), lambda j, i, r: (i, 0)),
         pl.BlockSpec((tk, n), lambda j, i, r: (j, 0))],
        jax.ShapeDtypeStruct((s, k), out_dtype),
        pl.BlockSpec((tm, tk), lambda j, i, r: (i, j)),
        None, (k // tk, s // tm, 1), 1)


def mm_rows_dw(name, a, dy):
    s, k = a.shape
    n = dy.shape[1]
    tm = _mm_tile(s)
    tk = _k_tile(k)
    nt = s // tm
    return _mm(
        name, "tn", a, dy,
        [pl.BlockSpec((tm, tk), lambda j, g, r: (r, j)),
         pl.BlockSpec((tm, n), lambda j, g, r: (r, 0))],
        jax.ShapeDtypeStruct((k, n), F32),
        pl.BlockSpec((tk, n), lambda j, g, r: (j, 0)),
        (tk, n), (k // tk, 1, nt), nt)


def _row(tm, c, col=0):
    return pl.BlockSpec((tm, c), lambda i: (i, col))


def _full(shape):
    nd = len(shape)
    return pl.BlockSpec(shape, lambda i: (0,) * nd)


def _prev(tm, h, c, col=0):
    return pl.BlockSpec((h, c), lambda i: (jnp.maximum(i * (tm // h) - 1, 0), col))


def _next(tm, h, c, s, col=0):
    return pl.BlockSpec((h, c), lambda i: (jnp.minimum((i + 1) * (tm // h), s // h - 1), col))


def _acc_add(ref, first, val):
    @pl.when(first)
    def _():
        ref[...] = val

    @pl.when(jnp.logical_not(first))
    def _():
        ref[...] += val


def _colsum(v):
    return jnp.sum(v, axis=0, keepdims=True)


def _ln_stats(z):
    mu = jnp.mean(z, axis=-1, keepdims=True)
    zc = z - mu
    var = jnp.mean(zc * zc, axis=-1, keepdims=True)
    rstd = lax.rsqrt(var + LN_EPS)
    return zc * rstd, rstd


def _ln_bwd(dxhat, xhat, rstd):
    m1 = jnp.mean(dxhat, axis=-1, keepdims=True)
    m2 = jnp.mean(dxhat * xhat, axis=-1, keepdims=True)
    return rstd * (dxhat - m1 - xhat * m2)


def ln_fwd(name, x, f, g, b, ple=None, emit_y=True):
    s, d = x.shape
    tm = _row_tile(s)
    n_in = 2 + (3 if ple is not None else 0)

    def body(*refs):
        x_ref, f_ref = refs[0], refs[1]
        g_ref, b_ref = refs[n_in], refs[n_in + 1]
        xh_ref, rs_ref = refs[-2:]
        z = ALPHA * x_ref[...] + f_ref[...]
        if ple is not None:
            pgl_ref, pp_ref, bg_ref = refs[2:5]
            z = z + _sig(pgl_ref[...] + bg_ref[...]) * pp_ref[...]
        xhat, rstd = _ln_stats(z)
        if emit_y:
            refs[n_in + 2][...] = xhat * g_ref[...] + b_ref[...]
        xh_ref[...] = xhat
        rs_ref[...] = jnp.broadcast_to(rstd, rs_ref.shape)

    ins = [x, f]
    specs = [_row(tm, d), _row(tm, d)]
    if ple is not None:
        pgl, pp, bg = ple
        ins += [pgl, pp, bg]
        specs += [_row(tm, d), _row(tm, d), _full((1, d))]
    ins += [g, b]
    specs += [_full((1, d)), _full((1, d))]
    n_y = 1 if emit_y else 0
    outs = _call(
        body, name=name, grid=(s // tm,), in_specs=specs,
        out_specs=[_row(tm, d)] * (n_y + 1) + [_row(tm, LANES)],
        out_shape=[jax.ShapeDtypeStruct((s, d), F32)] * (n_y + 1) + [jax.ShapeDtypeStruct((s, LANES), F32)],
        compiler_params=_cp(),
    )(*ins)
    return (outs[0], outs[1], outs[2]) if emit_y else (None, outs[0], outs[1])


def ln_bwd(name, parts, xhat, rstd, g, ple=None, loss=None):
    s, d = xhat.shape
    tm = _row_tile(s)
    coefs = [c for c, _ in parts]
    n_p = len(parts)
    n_ple = 3 if ple is not None else 0
    n_in = n_p + 3 + n_ple + (2 if loss is not None else 0)

    def body(*refs):
        first = pl.program_id(0) == 0
        xh = refs[n_p][...]
        rs = refs[n_p + 1][:, 0:1]
        g_v = refs[n_p + 2][...]
        outs = refs[n_in:]
        if loss is not None:
            t_ref, b_ref = refs[n_p + 3 + n_ple:n_p + 5 + n_ple]
            err = (xh * g_v + b_ref[...]) - t_ref[...]
            dy = err * (1.0 / d)
            part = 0.5 * jnp.sum(jnp.mean(err * err, axis=-1, keepdims=True), axis=0, keepdims=True)
            _acc_add(outs[-1], first, jnp.broadcast_to(part, outs[-1].shape))
        else:
            dy = coefs[0] * refs[0][...].astype(F32)
            for j in range(1, n_p):
                dy = dy + coefs[j] * refs[j][...].astype(F32)
        dz = _ln_bwd(dy * g_v, xh, rs)
        outs[0][...] = dz
        _acc_add(outs[1], first, _colsum(dy * xh))
        _acc_add(outs[2], first, _colsum(dy))
        if ple is not None:
            pgl_ref, pp_ref, bg_ref = refs[n_p + 3:n_p + 6]
            pg = _sig(pgl_ref[...] + bg_ref[...])
            dpgl = dz * pp_ref[...] * pg * (1.0 - pg)
            outs[3][...] = (dz * pg).astype(BF16)
            outs[4][...] = dpgl.astype(BF16)
            _acc_add(outs[5], first, _colsum(dpgl))

    ins = [p for _, p in parts] + [xhat, rstd, g]
    specs = [_row(tm, d)] * n_p + [_row(tm, d), _row(tm, LANES), _full((1, d))]
    out_specs = [_row(tm, d), _full((1, d)), _full((1, d))]
    out_shape = [jax.ShapeDtypeStruct((s, d), F32), jax.ShapeDtypeStruct((1, d), F32),
                 jax.ShapeDtypeStruct((1, d), F32)]
    if ple is not None:
        pgl, pp, bg = ple
        ins += [pgl, pp, bg]
        specs += [_row(tm, d), _row(tm, d), _full((1, d))]
        out_specs += [_row(tm, d), _row(tm, d), _full((1, d))]
        out_shape += [jax.ShapeDtypeStruct((s, d), BF16), jax.ShapeDtypeStruct((s, d), BF16),
                      jax.ShapeDtypeStruct((1, d), F32)]
    if loss is not None:
        target, b = loss
        ins += [target, b]
        specs += [_row(tm, d), _full((1, d))]
        out_specs += [_full((8, LANES))]
        out_shape += [jax.ShapeDtypeStruct((8, LANES), F32)]
    return _call(
        body, name=name, grid=(s // tm,), in_specs=specs, out_specs=out_specs, out_shape=out_shape,
        compiler_params=_cp(),
    )(*ins)


def _fill_rotations(rot_ref, x, direction):
    n = x.shape[0]
    rot_ref[0] = x
    for b in range(1, SUBLANES):
        if direction < 0:
            rot_ref[b, SUBLANES:n, :] = x[SUBLANES - b:n - b]
        else:
            rot_ref[b, 0:n - SUBLANES, :] = x[b:n - SUBLANES + b]


def _rotated(rot_ref, start, rows, cs, direction=-1):
    b = (-start) % SUBLANES if direction < 0 else start % SUBLANES
    aligned = start + b if direction < 0 else start - b
    return rot_ref[b, pl.ds(aligned, rows), cs]


def _tile_pos(i, tm, rows):
    return (i * tm + lax.broadcasted_iota(jnp.int32, (rows, 1), 0) + 1).astype(F32)


def mixer_fwd(name, u, pool_w, pool_scale, conv_w, conv_b, cn_g, cn_b):
    s = u.shape[0]
    dp = 512
    tm = min(256, s // 4)
    h = CONV_HALO

    def body(a_c, a_p, bv_c, bv_p, bg_c, bg_p, pw_ref, ps_ref, cw_ref, cb_ref, cg_ref, cbt_ref,
             cat_ref, d_ref, e_ref, glu_ref, hh_ref, rs_ref, ext_a, rot_g, conv_out):
        i = pl.program_id(0)
        first = i == 0
        ext_a[0:h, :] = jnp.where(first, 0.0, a_p[...])
        ext_a[h:, :] = a_c[...]
        glu = bv_c[...] * _sig(bg_c[...])
        glu_ref[...] = glu
        _fill_rotations(rot_g, jnp.concatenate([jnp.where(first, 0.0, bv_p[...] * _sig(bg_p[...])), glu], axis=0), -1)
        pos = _tile_pos(i, tm, tm)
        for gi, w in enumerate(POOL_WINDOWS):
            cs = slice(gi * POOL_GROUP, (gi + 1) * POOL_GROUP)
            a_g = ext_a[pl.ds(h, tm), cs]
            acc = a_g
            for sh in range(1, w):
                acc = acc + ext_a[pl.ds(h - sh, tm), cs]
            d_g = acc / jnp.minimum(pos, float(w)) - a_g
            d_ref[:, cs] = d_g.astype(BF16)
            e_g = _dot(d_g, pw_ref[gi], "nn")
            e_ref[:, cs] = e_g
            cat_ref[:, cs] = (e_g * ps_ref[:, cs]).astype(BF16)
        for lg in range(dp // LANES):
            cs = slice(lg * LANES, (lg + 1) * LANES)
            acc = jnp.broadcast_to(cb_ref[:, cs], (tm, LANES))
            for sh in range(CONV_K):
                acc = acc + _rotated(rot_g, h - sh, tm, cs) * cw_ref[pl.ds(CONV_K - 1 - sh, 1), cs]
            conv_out[:, cs] = acc
        hhat, rstd = _ln_stats(conv_out[...])
        hl = hhat * cg_ref[...] + cbt_ref[...]
        cat_ref[:, dp:] = (hl * _sig(hl)).astype(BF16)
        hh_ref[...] = hhat
        rs_ref[...] = jnp.broadcast_to(rstd, rs_ref.shape)

    specs = [_row(tm, dp, 0), _prev(tm, h, dp, 0), _row(tm, dp, 1), _prev(tm, h, dp, 1),
             _row(tm, dp, 2), _prev(tm, h, dp, 2),
             _full((4, POOL_GROUP, POOL_GROUP)), _full((1, dp)), _full((CONV_K, dp)),
             _full((1, dp)), _full((1, dp)), _full((1, dp))]
    out_specs = [_row(tm, 2 * dp), _row(tm, dp), _row(tm, dp), _row(tm, dp), _row(tm, dp), _row(tm, LANES)]
    out_shape = [jax.ShapeDtypeStruct((s, 2 * dp), BF16), jax.ShapeDtypeStruct((s, dp), BF16),
                 jax.ShapeDtypeStruct((s, dp), F32), jax.ShapeDtypeStruct((s, dp), F32),
                 jax.ShapeDtypeStruct((s, dp), F32), jax.ShapeDtypeStruct((s, LANES), F32)]
    return _call(
        body, name=name, grid=(s // tm,), in_specs=specs, out_specs=out_specs, out_shape=out_shape,
        scratch_shapes=[pltpu.VMEM((h + tm, dp), F32), pltpu.VMEM((SUBLANES, h + tm, dp), F32),
                        pltpu.VMEM((tm, dp), F32)],
        compiler_params=_cp(),
    )(u, u, u, u, u, u, pool_w, pool_scale, conv_w, conv_b, cn_g, cn_b)


def mixer_bwd(name, dcat, u, d_sv, e_sv, glu_sv, hh_sv, rs_sv, pool_w, pool_scale, conv_w, cn_g, cn_b):
    s = u.shape[0]
    dp = 512
    tm = min(256, s // 4)
    h = CONV_HALO
    nt = s // tm

    def body(dc_c, dc_n, bv_c, bg_c, d_c, e_c, gl_c, gl_p, hh_c, hh_n, rs_c, rs_n,
             pw_ref, ps_ref, cw_ref, cg_ref, cbt_ref,
             du_ref, dpw_ref, dps_ref, dcw_ref, dcb_ref, dcg_ref, dcbt_ref,
             ext_dh, ext_g, ext_r):
        i = pl.program_id(0)
        first = i == 0
        last = i == nt - 1
        cg = cg_ref[...]

        def conv_grads(dyb, hhat, rstd):
            hl = hhat * cg + cbt_ref[...]
            sg = _sig(hl)
            dhl = dyb * (sg * (1.0 + hl * (1.0 - sg)))
            return _ln_bwd(dhl * cg, hhat, rstd), dhl

        hh_cur = hh_c[...]
        dh_c, dhl_c = conv_grads(dc_c[:, dp:], hh_cur, rs_c[:, 0:1])
        dh_n, _ = conv_grads(dc_n[:, dp:], hh_n[...], rs_n[:, 0:1])
        _fill_rotations(ext_dh, jnp.concatenate([dh_c, jnp.where(last, 0.0, dh_n)], axis=0), 1)
        _fill_rotations(ext_g, jnp.concatenate([jnp.where(first, 0.0, gl_p[...]), gl_c[...]], axis=0), -1)

        @pl.when(first)
        def _():
            dcw_ref[...] = jnp.zeros(dcw_ref.shape, F32)

        for lg in range(dp // LANES):
            cs = slice(lg * LANES, (lg + 1) * LANES)
            dglu = jnp.zeros((tm, LANES), F32)
            for sh in range(CONV_K):
                dglu = dglu + _rotated(ext_dh, sh, tm, cs, 1) * cw_ref[pl.ds(CONV_K - 1 - sh, 1), cs]
            dh_g = ext_dh[0, pl.ds(0, tm), cs]
            for sh in range(CONV_K):
                dcw_ref[pl.ds(CONV_K - 1 - sh, 1), cs] += _colsum(dh_g * _rotated(ext_g, h - sh, tm, cs))
            sgate = _sig(bg_c[:, cs])
            du_ref[:, dp + lg * LANES:dp + (lg + 1) * LANES] = dglu * sgate
            du_ref[:, 2 * dp + lg * LANES:2 * dp + (lg + 1) * LANES] = dglu * bv_c[:, cs] * sgate * (1.0 - sgate)
        _acc_add(dcb_ref, first, _colsum(dh_c))
        _acc_add(dcg_ref, first, _colsum(dhl_c * hh_cur))
        _acc_add(dcbt_ref, first, _colsum(dhl_c))

        pos_c = _tile_pos(i, tm, tm)
        pos_n = _tile_pos(i + 1, tm, h)
        _acc_add(dps_ref, first, _colsum(dc_c[:, :dp] * e_c[...]))
        for gi, w in enumerate(POOL_WINDOWS):
            cs = slice(gi * POOL_GROUP, (gi + 1) * POOL_GROUP)
            pw = pw_ref[gi]
            de_c = dc_c[:, cs] * ps_ref[:, cs]
            de_n = dc_n[:, cs] * ps_ref[:, cs]
            dd_c = _dot(de_c, pw, "nt")
            dd_n = _dot(de_n, pw, "nt")
            ext_r[0:tm, :] = dd_c / jnp.minimum(pos_c, float(w))
            ext_r[tm:, :] = jnp.where(last, 0.0, dd_n / jnp.minimum(pos_n, float(w)))
            acc = -dd_c
            for sh in range(w):
                acc = acc + ext_r[pl.ds(sh, tm), :]
            du_ref[:, cs] = acc
            dpw_g = _dot(d_c[:, cs], de_c, "tn")

            @pl.when(first)
            def _():
                dpw_ref[gi] = dpw_g

            @pl.when(jnp.logical_not(first))
            def _():
                dpw_ref[gi] += dpw_g

    specs = [_row(tm, 2 * dp), _next(tm, h, 2 * dp, s), _row(tm, dp, 1), _row(tm, dp, 2),
             _row(tm, dp), _row(tm, dp), _row(tm, dp), _prev(tm, h, dp),
             _row(tm, dp), _next(tm, h, dp, s), _row(tm, LANES), _next(tm, h, LANES, s),
             _full((4, POOL_GROUP, POOL_GROUP)), _full((1, dp)), _full((CONV_K, dp)),
             _full((1, dp)), _full((1, dp))]
    out_specs = [_row(tm, 3 * dp), _full((4, POOL_GROUP, POOL_GROUP)), _full((1, dp)), _full((CONV_K, dp)),
                 _full((1, dp)), _full((1, dp)), _full((1, dp))]
    out_shape = [jax.ShapeDtypeStruct((s, 3 * dp), F32),
                 jax.ShapeDtypeStruct((4, POOL_GROUP, POOL_GROUP), F32), jax.ShapeDtypeStruct((1, dp), F32),
                 jax.ShapeDtypeStruct((CONV_K, dp), F32), jax.ShapeDtypeStruct((1, dp), F32),
                 jax.ShapeDtypeStruct((1, dp), F32), jax.ShapeDtypeStruct((1, dp), F32)]
    return _call(
        body, name=name, grid=(nt,), in_specs=specs, out_specs=out_specs, out_shape=out_shape,
        scratch_shapes=[pltpu.VMEM((SUBLANES, tm + h, dp), F32), pltpu.VMEM((SUBLANES, h + tm, dp), F32),
                        pltpu.VMEM((tm + h, POOL_GROUP), F32)],
        compiler_params=_cp(),
    )(dcat, dcat, u, u, d_sv, e_sv, glu_sv, glu_sv, hh_sv, hh_sv, rs_sv, rs_sv,
      pool_w, pool_scale, conv_w, cn_g, cn_b)


GELU_C = math.sqrt(2.0 / math.pi)


def _gelu_parts(x):
    x2 = x * x
    t = jnp.tanh(x * (GELU_C + (GELU_C * 0.044715) * x2))
    half_1pt = 0.5 + 0.5 * t
    gelu = x * half_1pt
    dgelu = half_1pt + (0.5 * x) * (1.0 - t * t) * (GELU_C + (3.0 * GELU_C * 0.044715) * x2)
    return gelu, dgelu


def ffn_act_fwd(name, gv, dw_w, dw_b):
    s = gv.shape[0]
    dff = gv.shape[1] // 2
    tm = min(FFN_TILE, s // 4)
    h = FFN_HALO
    rc = FFN_CHUNK_ROWS
    lw = FFN_CHUNK_LANES

    def body(g_c, g_p, v_c, w_ref, b_ref, hid_ref):
        first = pl.program_id(0) == 0

        def chunk(ci, carry):
            r0 = pl.multiple_of(ci * rc, rc)
            above = pl.multiple_of(jnp.maximum(r0 - h, 0), h)
            for lg in range(dff // lw):
                cs = slice(lg * lw, (lg + 1) * lw)
                top = jnp.where(ci == 0, jnp.where(first, 0.0, g_p[:, cs]), g_c[pl.ds(above, h), cs])
                win = jnp.concatenate([top, g_c[pl.ds(r0, rc), cs]], axis=0)
                gc = jnp.broadcast_to(b_ref[:, cs], (rc, lw))
                for sh in range(FFN_K):
                    gc = gc + win[h - sh:h - sh + rc] * w_ref[pl.ds(FFN_K - 1 - sh, 1), cs]
                gelu, _ = _gelu_parts(gc)
                hid_ref[pl.ds(r0, rc), cs] = (gelu * v_c[pl.ds(r0, rc), cs]).astype(BF16)
            return carry

        lax.fori_loop(0, tm // rc, chunk, 0)

    return _call(
        body, name=name, grid=(s // tm,),
        in_specs=[_row(tm, dff, 0), _prev(tm, h, dff, 0), _row(tm, dff, 1), _full((FFN_K, dff)), _full((1, dff))],
        out_specs=_row(tm, dff), out_shape=jax.ShapeDtypeStruct((s, dff), BF16),
        compiler_params=_cp(),
    )(gv, gv, gv, dw_w, dw_b)


def ffn_act_bwd(name, dhid, gv, dw_w, dw_b):
    s = gv.shape[0]
    dff = gv.shape[1] // 2
    tm = min(FFN_TILE, s // 4)
    h = FFN_HALO
    nt = s // tm
    rc = FFN_CHUNK_ROWS
    lw = FFN_CHUNK_LANES
    n_chunks = tm // rc

    def body(dh_c, dh_n, g_p, g_c, g_n, v_c, v_n, w_ref, b_ref, dgv_ref, dw_ref, db_ref):
        i = pl.program_id(0)
        first = i == 0
        last = i == nt - 1

        @pl.when(first)
        def _():
            dw_ref[...] = jnp.zeros(dw_ref.shape, F32)
            db_ref[...] = jnp.zeros(db_ref.shape, F32)

        def chunk(ci, carry):
            r0 = pl.multiple_of(ci * rc, rc)
            above = pl.multiple_of(jnp.maximum(r0 - h, 0), h)
            below = pl.multiple_of(jnp.minimum(r0 + rc, tm - h), h)
            at_end = ci == n_chunks - 1
            for lg in range(dff // lw):
                cs = slice(lg * lw, (lg + 1) * lw)
                top = jnp.where(ci == 0, jnp.where(first, 0.0, g_p[:, cs]), g_c[pl.ds(above, h), cs])
                bot = jnp.where(at_end, g_n[:, cs], g_c[pl.ds(below, h), cs])
                win = jnp.concatenate([top, g_c[pl.ds(r0, rc), cs], bot], axis=0)
                shifted = [win[h - sh:h - sh + rc + h] for sh in range(FFN_K)]
                gc = jnp.broadcast_to(b_ref[:, cs], (rc + h, lw))
                for sh in range(FFN_K):
                    gc = gc + shifted[sh] * w_ref[pl.ds(FFN_K - 1 - sh, 1), cs]
                gelu, dgelu = _gelu_parts(gc)
                dh_mid = dh_c[pl.ds(r0, rc), cs]
                hv_bot = jnp.where(at_end, jnp.where(last, 0.0, dh_n[:, cs] * v_n[:, cs]),
                                   dh_c[pl.ds(below, h), cs] * v_c[pl.ds(below, h), cs])
                dgc = jnp.concatenate([dh_mid * v_c[pl.ds(r0, rc), cs], hv_bot], axis=0) * dgelu
                dgate = jnp.zeros((rc, lw), F32)
                for sh in range(FFN_K):
                    dgate = dgate + dgc[sh:sh + rc] * w_ref[pl.ds(FFN_K - 1 - sh, 1), cs]
                dgv_ref[pl.ds(r0, rc), cs] = dgate.astype(BF16)
                dgv_ref[pl.ds(r0, rc), slice(dff + lg * lw, dff + (lg + 1) * lw)] = (dh_mid * gelu[0:rc]).astype(BF16)
                dgc_mid = dgc[0:rc]
                for sh in range(FFN_K):
                    dw_ref[pl.ds(FFN_K - 1 - sh, 1), cs] += _colsum(dgc_mid * shifted[sh][0:rc])
                db_ref[:, cs] += _colsum(dgc_mid)
            return carry

        lax.fori_loop(0, n_chunks, chunk, 0)

    return _call(
        body, name=name, grid=(nt,),
        in_specs=[_row(tm, dff), _next(tm, h, dff, s),
                  _prev(tm, h, dff, 0), _row(tm, dff, 0), _next(tm, h, dff, s, 0),
                  _row(tm, dff, 1), _next(tm, h, dff, s, 1),
                  _full((FFN_K, dff)), _full((1, dff))],
        out_specs=[_row(tm, 2 * dff), _full((FFN_K, dff)), _full((1, dff))],
        out_shape=[jax.ShapeDtypeStruct((s, 2 * dff), BF16), jax.ShapeDtypeStruct((FFN_K, dff), F32),
                   jax.ShapeDtypeStruct((1, dff), F32)],
        compiler_params=_cp(),
    )(dhid, dhid, gv, gv, gv, gv, gv, dw_w, dw_b)


def _toeplitz_bias(rel_bias):
    nh = rel_bias.shape[0]
    zero = jnp.zeros((nh, 1), rel_bias.dtype)
    line = jnp.concatenate(
        [zero, jnp.broadcast_to(rel_bias[:, 2 * MAX_REL:], (nh, SHEAR_SAT)),
         jnp.flip(rel_bias[:, 1:2 * MAX_REL], axis=1), zero], axis=1)
    z = jnp.broadcast_to(line[:, None, :], (nh, Q_TILE, SHEAR_W + 1)).reshape(nh, Q_TILE * (SHEAR_W + 1))
    return z[:, :Q_TILE * SHEAR_W].reshape(nh, Q_TILE, SHEAR_W)[:, :, Q_TILE:]


def _shear_for_bias_grad(ds_sum):
    nh = ds_sum.shape[0]
    z = jnp.pad(ds_sum, ((0, 0), (0, 0), (Q_TILE, 0))).reshape(nh, Q_TILE * SHEAR_W)
    return jnp.pad(z, ((0, 0), (0, Q_TILE))).reshape(nh, Q_TILE, SHEAR_W + 1)


def _band_masked(bias):
    qc = lax.broadcasted_iota(jnp.int32, (Q_TILE, K_WIN), 0) // CHUNK
    kc = lax.broadcasted_iota(jnp.int32, (Q_TILE, K_WIN), 1) // CHUNK
    return jnp.where(((kc >= qc) & (kc <= qc + LEFT_CHUNKS))[None], bias, NEG_INF)


def _stack_heads(x2):
    lane = lax.broadcasted_iota(jnp.int32, x2.shape, 1)
    zero = jnp.zeros_like(x2)
    return jnp.concatenate([jnp.where(lane < HEAD_DIM, x2, zero), jnp.where(lane < HEAD_DIM, zero, x2)], axis=0)


def _unstack_heads(x_st):
    lane = lax.broadcasted_iota(jnp.int32, (Q_TILE, LANES), 1)
    return jnp.where(lane < HEAD_DIM, x_st[:Q_TILE], x_st[Q_TILE:])


def _attn_probs(q_st, k3, bias_st, t):
    sc = _dot(q_st, k3, "nt") * (HEAD_DIM ** -0.5) + bias_st
    col = lax.broadcasted_iota(jnp.int32, sc.shape, 1)
    sc = jnp.where(col >= PAD_ROWS - t * Q_TILE, sc, NEG_INF)
    m = jnp.max(sc, axis=-1, keepdims=True)
    p = jnp.exp(sc - m)
    return p * (1.0 / jnp.sum(p, axis=-1, keepdims=True))


def _attn_specs(d_model):
    nq = PAD_ROWS // Q_TILE
    hp_k = d_model // LANES
    specs = [pl.BlockSpec((Q_TILE, LANES), lambda hp, t: (t + nq, hp))]
    for which in (1, 2):
        for j in range(K_WIN // Q_TILE):
            specs.append(pl.BlockSpec((Q_TILE, LANES), lambda hp, t, j=j, which=which: (t + j, which * hp_k + hp)))
    specs.append(pl.BlockSpec((2, Q_TILE, K_WIN), lambda hp, t: (hp, 0, 0)))
    return specs


def attn_fwd(name, qkvp, bias):
    s = qkvp.shape[0] - PAD_ROWS
    d_model = qkvp.shape[1] // 3
    nw = K_WIN // Q_TILE

    def body(q_ref, *refs):
        k_refs, v_refs, b_ref, o_ref = refs[:nw], refs[nw:2 * nw], refs[2 * nw], refs[2 * nw + 1]
        t = pl.program_id(1)
        k3 = jnp.concatenate([r[...] for r in k_refs], axis=0)
        v3 = jnp.concatenate([r[...] for r in v_refs], axis=0)
        p = _attn_probs(_stack_heads(q_ref[...]), k3, b_ref[...].reshape(2 * Q_TILE, K_WIN), t)
        o_ref[...] = _unstack_heads(_dot(p, v3, "nn")).astype(BF16)

    return _call(
        body, name=name, grid=(d_model // LANES, s // Q_TILE),
        in_specs=_attn_specs(d_model), out_specs=pl.BlockSpec((Q_TILE, LANES), lambda hp, t: (t, hp)),
        out_shape=jax.ShapeDtypeStruct((s, d_model), BF16), compiler_params=_cp(),
    )(qkvp, *([qkvp] * (2 * nw)), bias)


def attn_bwd(name, qkvp, bias, do):
    s = qkvp.shape[0] - PAD_ROWS
    d_model = qkvp.shape[1] // 3
    nw = K_WIN // Q_TILE
    nt = s // Q_TILE
    scale = HEAD_DIM ** -0.5

    def body(q_ref, *refs):
        k_refs, v_refs = refs[:nw], refs[nw:2 * nw]
        b_ref, do_ref, dq_ref, dk_ref, dv_ref, ds_ref, dk_acc, dv_acc = refs[2 * nw:]
        t = pl.program_id(1)
        first = t == 0

        @pl.when(first)
        def _():
            dk_acc[...] = jnp.zeros(dk_acc.shape, F32)
            dv_acc[...] = jnp.zeros(dv_acc.shape, F32)

        q_st = _stack_heads(q_ref[...])
        do_st = _stack_heads(do_ref[...])
        k3 = jnp.concatenate([r[...] for r in k_refs], axis=0)
        v3 = jnp.concatenate([r[...] for r in v_refs], axis=0)
        p = _attn_probs(q_st, k3, b_ref[...].reshape(2 * Q_TILE, K_WIN), t)
        dp = _dot(do_st, v3, "nt")
        ds = p * (dp - jnp.sum(p * dp, axis=-1, keepdims=True))
        _acc_add(ds_ref, first, ds.reshape(2, Q_TILE, K_WIN))
        dsb = (ds * scale).astype(BF16)
        dq_ref[...] = _unstack_heads(_dot(dsb, k3, "nn")).astype(BF16)
        start = pl.multiple_of(t * Q_TILE, Q_TILE)
        dk_acc[pl.ds(start, K_WIN), :] += _dot(dsb, q_st, "tn")
        dv_acc[pl.ds(start, K_WIN), :] += _dot(p, do_st, "tn")

        @pl.when(t == nt - 1)
        def _():
            dk_ref[...] = dk_acc[pl.ds(PAD_ROWS, s), :].astype(BF16)
            dv_ref[...] = dv_acc[pl.ds(PAD_ROWS, s), :].astype(BF16)

    specs = _attn_specs(d_model) + [pl.BlockSpec((Q_TILE, LANES), lambda hp, t: (t, hp))]
    col_spec = pl.BlockSpec((s, LANES), lambda hp, t: (0, hp))
    return _call(
        body, name=name, grid=(d_model // LANES, nt), in_specs=specs,
        out_specs=[pl.BlockSpec((Q_TILE, LANES), lambda hp, t: (t, hp)), col_spec, col_spec,
                   pl.BlockSpec((2, Q_TILE, K_WIN), lambda hp, t: (hp, 0, 0))],
        out_shape=[jax.ShapeDtypeStruct((s, d_model), BF16)] * 3
        + [jax.ShapeDtypeStruct((N_HEADS, Q_TILE, K_WIN), F32)],
        scratch_shapes=[pltpu.VMEM((PAD_ROWS + s, LANES), F32), pltpu.VMEM((PAD_ROWS + s, LANES), F32)],
        compiler_params=_cp(),
    )(qkvp, *([qkvp] * (2 * nw)), bias, do)


def bias_grad_reduce(name, sheared):
    nh, _, width = sheared.shape

    def body(x_ref, col_ref, sat_ref):
        cols = _colsum(x_ref[...])
        col_ref[...] = cols
        k = lax.broadcasted_iota(jnp.int32, cols.shape, 1)
        tot = jnp.sum(jnp.where((k >= 1) & (k <= SHEAR_SAT), cols, 0.0), axis=-1, keepdims=True)
        sat_ref[...] = jnp.broadcast_to(tot, sat_ref.shape)

    return _call(
        body, name=name, grid=(nh,),
        in_specs=[pl.BlockSpec((None, Q_TILE, width), lambda hh: (hh, 0, 0))],
        out_specs=[pl.BlockSpec((None, 1, width), lambda hh: (hh, 0, 0)),
                   pl.BlockSpec((None, 1, LANES), lambda hh: (hh, 0, 0))],
        out_shape=[jax.ShapeDtypeStruct((nh, 1, width), F32), jax.ShapeDtypeStruct((nh, 1, LANES), F32)],
        compiler_params=_cp(),
    )(sheared)


def _ew_rows(r, most=512):
    for cand in (512, 256, 128, 64, 32, 16, 8):
        if cand <= most and r % cand == 0:
            return cand
    return r


def cast_into_gathered(name, w, layer, s_idx):
    r, c = w.shape[-2:]
    tr = _ew_rows(r)

    def body(s_ref, w_ref, o_ref):
        o_ref[...] = w_ref[...].astype(BF16)

    grid_spec = pltpu.PrefetchScalarGridSpec(
        num_scalar_prefetch=1, grid=(r // tr,),
        in_specs=[pl.BlockSpec((None, tr, c), lambda i, s_ref: (layer, i, 0))],
        out_specs=pl.BlockSpec((None, tr, c), lambda i, s_ref: (s_ref[0], i, 0)))
    return _call(
        body, name=name, grid_spec=grid_spec, out_shape=jax.ShapeDtypeStruct((N_SHARD, r, c), BF16),
        compiler_params=_cp(),
    )(s_idx, w)


def adamw(name, w, grads, m, v):
    nl, r, c = w.shape
    tr = _ew_rows(r, 256)

    def body(*refs):
        w_ref, m_ref, v_ref = refs[0], refs[1], refs[2]
        g_refs = refs[3:3 + nl]
        d_ref, nm_ref, nv_ref = refs[3 + nl:]
        layer = pl.program_id(0)
        g = g_refs[0][...]
        for j in range(1, nl):
            g = jnp.where(layer == j, g_refs[j][...], g)
        nm = ADAM_B1 * m_ref[...] + (1.0 - ADAM_B1) * g
        nv = ADAM_B2 * v_ref[...] + (1.0 - ADAM_B2) * (g * g)
        m_hat = nm / ADAM_BC1
        v_hat = nv / ADAM_BC2
        d_ref[...] = -ADAM_LR * (m_hat / (jnp.sqrt(v_hat) + ADAM_EPS) + ADAM_WD * w_ref[...])
        nm_ref[...] = nm
        nv_ref[...] = nv

    p_spec = pl.BlockSpec((None, tr, c), lambda l, i: (l, i, 0))
    g_spec = pl.BlockSpec((tr, c), lambda l, i: (i, 0))
    return _call(
        body, name=name, grid=(nl, r // tr), in_specs=[p_spec] * 3 + [g_spec] * nl, out_specs=[p_spec] * 3,
        out_shape=[jax.ShapeDtypeStruct((nl, r, c), F32)] * 3, compiler_params=_cp(),
    )(w, m, v, *grads)


def sum_blocks(name, gathered, n_blocks):
    r = gathered.shape[0] // n_blocks
    c = gathered.shape[1]
    tr = _ew_rows(r)
    nt = r // tr

    def body(*refs):
        acc = refs[0][...]
        for j in range(1, n_blocks):
            acc = acc + refs[j][...]
        refs[-1][...] = acc

    specs = [pl.BlockSpec((tr, c), lambda i, j=j: (j * nt + i, 0)) for j in range(n_blocks)]
    return _call(
        body, name=name, grid=(nt,), in_specs=specs, out_specs=pl.BlockSpec((tr, c), lambda i: (i, 0)),
        out_shape=jax.ShapeDtypeStruct((r, c), F32), compiler_params=_cp(),
    )(*([gathered] * n_blocks))


def _place():
    return lax.axis_index("x"), lax.axis_index("y"), lax.axis_index("c")


def _other_chips(x, y):
    return [(1 - x, y), (x, 1 - y), (1 - x, 1 - y)]


HBM_SPEC = pl.BlockSpec(memory_space=pltpu.HBM)
SEM_SPEC = pl.BlockSpec(memory_space=pltpu.SEMAPHORE)
ANY_SPEC = pl.BlockSpec(memory_space=pl.ANY)
EFFECT = pltpu.SideEffectType.DATAFLOW_SIDE_EFFECTING


def copies_start(name, bufs, plan, n_copies):
    n = len(bufs)

    def body(*refs):
        send, recv = refs[n], refs[n + 1]
        token = refs[2 * n + 2]
        for k, (src, dst, peer, _) in enumerate(plan(refs[:n])):
            pltpu.make_async_remote_copy(
                src_ref=src, dst_ref=dst, send_sem=send.at[k], recv_sem=recv.at[k],
                device_id=peer, device_id_type=MESH).start()
        token[...] = jnp.zeros(token.shape, F32)

    outs = pl.pallas_call(
        body, name=name,
        out_shape=(pltpu.SemaphoreType.DMA((n_copies,)), pltpu.SemaphoreType.DMA((n_copies,)),
                   *[pltpu.HBM(b.shape, b.dtype) for b in bufs], jax.ShapeDtypeStruct((8, LANES), F32)),
        in_specs=[HBM_SPEC] * n,
        out_specs=(SEM_SPEC, SEM_SPEC, *([HBM_SPEC] * n), pl.BlockSpec(memory_space=pltpu.VMEM)),
        input_output_aliases={a: a + 2 for a in range(n)},
        compiler_params=pltpu.CompilerParams(has_side_effects=EFFECT),
    )(*[_in_hbm(b) for b in bufs])
    return outs[0], outs[1], list(outs[2:2 + n]), outs[2 + n]


def copies_wait(name, bufs, send, recv, plan, sem_base, after):
    n = len(bufs)

    def body(*refs):
        send_ref, recv_ref = refs[n], refs[n + 1]
        for k, (src, _, peer, land) in enumerate(plan(refs[:n])):
            cp = pltpu.make_async_remote_copy(
                src_ref=src, dst_ref=land, send_sem=send_ref.at[sem_base + k], recv_sem=recv_ref.at[sem_base + k],
                device_id=peer, device_id_type=MESH)
            cp.wait_send()
            cp.wait_recv()

    outs = pl.pallas_call(
        body, name=name,
        out_shape=tuple(pltpu.HBM(b.shape, b.dtype) for b in bufs),
        in_specs=[HBM_SPEC] * n + [SEM_SPEC, SEM_SPEC, ANY_SPEC], out_specs=tuple([HBM_SPEC] * n),
        input_output_aliases={a: a for a in range(n)},
        compiler_params=pltpu.CompilerParams(has_side_effects=EFFECT),
    )(*bufs, send, recv, after)
    return list(outs)


def gather_plan(refs):
    x, y, c = _place()
    me = 2 * x + y
    return [(buf.at[me], buf.at[me], (cx, cy, c), buf.at[2 * cx + cy])
            for buf in refs for cx, cy in _other_chips(x, y)]


def swap_plan(refs):
    x, y, c = _place()
    n = len(refs) // 2
    out = []
    for g, land in zip(refs[:n], refs[n:]):
        hr = g.shape[1] // 2
        out.append((g.at[:, pl.ds((1 - c) * hr, hr)], land, (x, y, 1 - c), land))
    return out


def owners_plan(refs):
    x, y, c = _place()
    n = len(refs) // 2
    return [(src.at[2 * cx + cy], land.at[j], (cx, cy, c), land.at[j])
            for src, land in zip(refs[:n], refs[n:]) for j, (cx, cy) in enumerate(_other_chips(x, y))]


def join_plan(refs):
    x, y, c = _place()
    out = []
    for buf in refs:
        hr = buf.shape[0] // 2
        mine = buf.at[pl.ds(c * hr, hr)]
        out.append((mine, mine, (x, y, 1 - c), buf.at[pl.ds((1 - c) * hr, hr)]))
    return out


def add_halves(name, grad, landed, c_idx):
    _, r, c = grad.shape
    hr = r // 2
    tr = _ew_rows(hr)
    nt = hr // tr

    def body(c_ref, g_ref, l_ref, o_ref, ob_ref):
        tot = g_ref[...] + l_ref[...]
        o_ref[...] = tot
        ob_ref[...] = tot.astype(BF16)

    blk = pl.BlockSpec((None, tr, c), lambda sh, i, c_ref: (sh, i, 0))
    grid_spec = pltpu.PrefetchScalarGridSpec(
        num_scalar_prefetch=1, grid=(N_SHARD, nt),
        in_specs=[pl.BlockSpec((None, tr, c), lambda sh, i, c_ref: (sh, c_ref[0] * nt + i, 0)), blk],
        out_specs=[blk, blk])
    return _call(
        body, name=name, grid_spec=grid_spec,
        out_shape=[jax.ShapeDtypeStruct((N_SHARD, hr, c), F32), jax.ShapeDtypeStruct((N_SHARD, hr, c), BF16)],
        compiler_params=_cp(),
    )(c_idx, grad, landed)


def add_owned(name, own, landed, sc_idx):
    _, hr, c = own.shape
    tr = _ew_rows(hr)
    nt = hr // tr

    def body(sc_ref, o_ref, l0, l1, l2, out_ref):
        out_ref[...] = ((o_ref[...] + l0[...].astype(F32)) + l1[...].astype(F32)) + l2[...].astype(F32)

    grid_spec = pltpu.PrefetchScalarGridSpec(
        num_scalar_prefetch=1, grid=(nt,),
        in_specs=[pl.BlockSpec((None, tr, c), lambda i, sc_ref: (sc_ref[0], i, 0))]
        + [pl.BlockSpec((None, tr, c), lambda i, sc_ref, j=j: (j, i, 0)) for j in range(3)],
        out_specs=pl.BlockSpec((tr, c), lambda i, sc_ref: (sc_ref[1] * nt + i, 0)))
    return _call(
        body, name=name, grid_spec=grid_spec, out_shape=jax.ShapeDtypeStruct((2 * hr, c), F32),
        compiler_params=_cp(),
    )(sc_idx, own, landed, landed, landed)


def gather_small(name, block):
    m_per, n = block.shape

    def body(x_ref, out_ref, send_sems, recv_sems, local_sem):
        x, y, c = _place()
        me, sibling = (x, y, c), (x, y, 1 - c)
        chips = _other_chips(x, y)

        def rows(px, py, pc):
            return out_ref.at[pl.ds((4 * px + 2 * py + pc) * m_per, m_per), :]

        def copy(k, blk, to, src=None):
            return pltpu.make_async_remote_copy(
                src_ref=rows(*blk) if src is None else src, dst_ref=rows(*blk),
                send_sem=send_sems.at[k], recv_sem=recv_sems.at[k], device_id=to, device_id_type=MESH)

        mine = pltpu.make_async_copy(x_ref, rows(*me), local_sem)
        mine.start()
        first = [copy(0, me, sibling, src=x_ref)]
        first += [copy(1 + j, me, (*chip, c), src=x_ref) for j, chip in enumerate(chips)]
        for cp in first:
            cp.start()
        passed = [copy(4 + j, (*chip, c), sibling) for j, chip in enumerate(chips)]
        for j, chip in enumerate(chips):
            copy(1 + j, (*chip, c), me).wait_recv()
            passed[j].start()
        copy(0, sibling, me).wait_recv()
        for j, chip in enumerate(chips):
            copy(4 + j, (*chip, 1 - c), me).wait_recv()
        for cp in first + passed:
            cp.wait_send()
        mine.wait()

    return _call(
        body, name=name, out_shape=jax.ShapeDtypeStruct((8 * m_per, n), block.dtype),
        in_specs=[pl.BlockSpec(memory_space=pltpu.VMEM)], out_specs=pl.BlockSpec(memory_space=pltpu.VMEM),
        scratch_shapes=[pltpu.SemaphoreType.DMA((7,)), pltpu.SemaphoreType.DMA((7,)), pltpu.SemaphoreType.DMA],
        compiler_params=_cp(),
    )(block)


PACK_QUANTUM = 8 * LANES


def _pack(arrays):
    pieces = []
    for a in arrays:
        flat = a.reshape(-1)
        padded = -(-flat.shape[0] // PACK_QUANTUM) * PACK_QUANTUM
        pieces.append(jnp.pad(flat, (0, padded - flat.shape[0])).reshape(-1, LANES))
    return jnp.concatenate(pieces, axis=0)


def _unpack(packed, shapes):
    out = []
    row = 0
    for shp in shapes:
        size = math.prod(shp)
        rows = -(-size // PACK_QUANTUM) * 8
        out.append(packed[row:row + rows].reshape(-1)[:size].reshape(shp))
        row += rows
    return out


def kernel(x, p, mix_w_in, pool_w, pool_scale, conv_dw_w, conv_dw_b, conv_ln_g, conv_ln_b, mix_w_out, attn_w_qkv, attn_rel_bias, attn_w_o, ln_mix_g, ln_mix_b, ffn_w_up, ffn_dw_w, ffn_dw_b, ffn_w_down, ple_w_proj, ple_w_gate, ple_b_gate, ln_ffn_g, ln_ffn_b, loss_target, m_mix_w_in, m_pool_w, m_pool_scale, m_conv_dw_w, m_conv_dw_b, m_conv_ln_g, m_conv_ln_b, m_mix_w_out, m_attn_w_qkv, m_attn_rel_bias, m_attn_w_o, m_ln_mix_g, m_ln_mix_b, m_ffn_w_up, m_ffn_dw_w, m_ffn_dw_b, m_ffn_w_down, m_ple_w_proj, m_ple_w_gate, m_ple_b_gate, m_ln_ffn_g, m_ln_ffn_b, v_mix_w_in, v_pool_w, v_pool_scale, v_conv_dw_w, v_conv_dw_b, v_conv_ln_g, v_conv_ln_b, v_mix_w_out, v_attn_w_qkv, v_attn_rel_bias, v_attn_w_o, v_ln_mix_g, v_ln_mix_b, v_ffn_w_up, v_ffn_dw_w, v_ffn_dw_b, v_ffn_w_down, v_ple_w_proj, v_ple_w_gate, v_ple_b_gate, v_ln_ffn_g, v_ln_ffn_b):
    xi, yi, ci = _place()
    shard_idx = (2 * xi + yi).astype(jnp.int32)
    s_arr = shard_idx.reshape(1)
    c_arr = ci.astype(jnp.int32).reshape(1)
    sc_arr = jnp.concatenate([s_arr, c_arr])

    x0 = x[0]
    target = loss_target[0]
    p_rows = p.reshape(p.shape[0] * p.shape[2], p.shape[3])
    seq = x0.shape[0]

    big = [
        ("mix_w_in", mix_w_in, m_mix_w_in, v_mix_w_in, True),
        ("mix_w_out", mix_w_out, m_mix_w_out, v_mix_w_out, False),
        ("attn_w_qkv", attn_w_qkv, m_attn_w_qkv, v_attn_w_qkv, True),
        ("attn_w_o", attn_w_o, m_attn_w_o, v_attn_w_o, False),
        ("ffn_w_up", ffn_w_up, m_ffn_w_up, v_ffn_w_up, True),
        ("ffn_w_down", ffn_w_down, m_ffn_w_down, v_ffn_w_down, False),
        ("ple_w_proj", ple_w_proj, m_ple_w_proj, v_ple_w_proj, True),
        ("ple_w_gate", ple_w_gate, m_ple_w_gate, v_ple_w_gate, False),
    ]
    params = {nm: w for nm, w, _, _, _ in big}
    col_sharded = {nm: cs for nm, _, _, _, cs in big}
    keys = [("mix_w_in", 0), ("mix_w_out", 0), ("ffn_w_up", 0), ("ffn_w_down", 0), ("ple_w_gate", 0),
            ("ple_w_proj", 0), ("attn_w_qkv", 0), ("attn_w_o", 0), ("ffn_w_up", 1), ("ffn_w_down", 1),
            ("ple_w_gate", 1), ("ple_w_proj", 1)]
    shards = [cast_into_gathered(f"cast_{nm}_{layer}", params[nm], layer, s_arr) for nm, layer in keys]
    g_send, g_recv, g_bufs, _ = copies_start("gather_start", shards, gather_plan, 3 * len(keys))
    arrived_w = {}

    def weight(nm, layer, after=None):
        key = (nm, layer)
        if key not in arrived_w:
            a = keys.index(key)
            arrived_w[key] = copies_wait(f"gather_wait_{nm}_{layer}", [g_bufs[a]], g_send, g_recv, gather_plan,
                                         3 * a, after)[0]
        g = arrived_w[key]
        if col_sharded[nm]:
            return g
        return g.reshape(g.shape[0] * g.shape[1], g.shape[2])

    def tie(a, token):
        return a + token[0:1, 0:1].astype(a.dtype)

    class Reducer:
        def __init__(self, tag, group):
            self.tag, self.group, self.stage = tag, group, 0
            self.n = len(group)
            self.result = None

        def advance(self, after):
            tag, n = self.tag, self.n
            if self.stage == 0:
                grads = []
                for key in self.group:
                    g = big_grads[key]
                    grads.append(g if g.ndim == 3 else g.reshape(N_SHARD, g.shape[0] // N_SHARD, g.shape[1]))
                lands = [lax.empty((N_SHARD, g.shape[1] // 2, g.shape[2]), F32) for g in grads]
                self.sems = copies_start(f"swap_start_{tag}", grads + lands, swap_plan, n)
            elif self.stage == 1:
                send, recv, bufs, _ = self.sems
                outs = copies_wait(f"swap_wait_{tag}", bufs, send, recv, swap_plan, 0, after)
                self.own, wire = [], []
                for key, g, ld in zip(self.group, outs[:n], outs[n:]):
                    o, ob = add_halves(f"add_halves_{key[0]}_{key[1]}", g, ld, c_arr)
                    self.own.append(o)
                    wire.append(ob)
                lands = [lax.empty((3,) + w.shape[1:], BF16) for w in wire]
                self.sems = copies_start(f"owners_start_{tag}", wire + lands, owners_plan, 3 * n)
            elif self.stage == 2:
                send, recv, bufs, _ = self.sems
                outs = copies_wait(f"owners_wait_{tag}", bufs, send, recv, owners_plan, 0, after)
                finals = [add_owned(f"add_owned_{key[0]}_{key[1]}", o, ar, sc_arr)
                          for key, o, ar in zip(self.group, self.own, outs[n:])]
                self.sems = copies_start(f"join_start_{tag}", finals, join_plan, n)
            elif self.stage == 3:
                send, recv, bufs, _ = self.sems
                outs = copies_wait(f"join_wait_{tag}", bufs, send, recv, join_plan, 0, after)
                self.result = dict(zip(self.group, outs))
                self.sems = None
            self.stage += 1
            return None if self.sems is None else self.sems[3]

    dw_shapes = [conv_dw_w.shape, ffn_dw_w.shape]
    dw_packed = _pack([conv_dw_w, ffn_dw_w])
    dw_rows = dw_packed.shape[0]
    dw_all = gather_small("gather_dw", dw_packed)
    dw_parts = [_unpack(dw_all[2 * k * dw_rows:(2 * k + 1) * dw_rows], dw_shapes) for k in range(N_SHARD)]
    conv_w_full = jnp.concatenate([pc[0] for pc in dw_parts], axis=2)[0]
    ffn_dw_full = jnp.concatenate([pc[1] for pc in dw_parts], axis=2)

    big_grads = {}
    small_grads = {}

    saved = []
    h_in = x0
    for layer in range(N_LAYERS):
        sv = {"x_in": h_in}
        if layer % 2 == 0:
            u = mm_cols_fwd("mix_in", h_in, weight("mix_w_in", 0, h_in), F32)
            cat, d_sv, e_sv, glu_sv, hh_sv, rs_sv = mixer_fwd(
                "mixer_fwd", u, pool_w[0], pool_scale, conv_w_full, conv_dw_b, conv_ln_g, conv_ln_b)
            mix = mm_rows_fwd("mix_out", cat, weight("mix_w_out", 0, cat))
            sv.update(u=u, cat=cat, d=d_sv, e=e_sv, glu=glu_sv, hh=hh_sv, rs=rs_sv)
        else:
            qkvp = mm_cols_fwd("attn_qkv", h_in, weight("attn_w_qkv", 0, h_in), BF16,
                               pad_blocks=PAD_ROWS // _row_tile(seq))
            bias = _band_masked(_toeplitz_bias(attn_rel_bias[0]))
            att = attn_fwd("attn_fwd", qkvp, bias)
            mix = mm_rows_fwd("attn_out", att, weight("attn_w_o", 0, att))
            sv.update(qkvp=qkvp, bias=bias, att=att)
        x1, xh1, rs1 = ln_fwd(f"ln_mix_{layer}", h_in, mix, ln_mix_g[layer:layer + 1], ln_mix_b[layer:layer + 1])
        gv = mm_cols_fwd(f"ffn_up_{layer}", x1, weight("ffn_w_up", layer, x1), F32)
        hid = ffn_act_fwd(f"ffn_act_{layer}", gv, ffn_dw_full[layer], ffn_dw_b[layer:layer + 1])
        ffn = mm_rows_fwd(f"ffn_down_{layer}", hid, weight("ffn_w_down", layer, hid))
        pgl = mm_rows_fwd(f"ple_gate_{layer}", x1, weight("ple_w_gate", layer, ffn))
        pp = mm_cols_fwd(f"ple_proj_{layer}", p_rows, weight("ple_w_proj", layer, pgl), F32, part=(layer, N_LAYERS))
        bg = ple_b_gate[layer:layer + 1]
        x2, xh2, rs2 = ln_fwd(f"ln_ffn_{layer}", x1, ffn, ln_ffn_g[layer:layer + 1], ln_ffn_b[layer:layer + 1],
                              ple=(pgl, pp, bg), emit_y=layer < N_LAYERS - 1)
        sv.update(x1=x1, xh1=xh1, rs1=rs1, gv=gv, hid=hid, pgl=pgl, pp=pp, xh2=xh2, rs2=rs2)
        saved.append(sv)
        h_in = x2

    reducers = []

    def open_group(tag, group):
        reducers.append(Reducer(tag, group))
        return reducers[-1].advance(None)

    def hook(after):
        token = None
        for red in reducers:
            if red.stage < 4:
                tk = red.advance(after)
                if tk is not None:
                    token = tk if token is None else token + tk
        return token

    def tied(a, token):
        return a if token is None else tie(a, token)

    parts = []
    token = None
    for layer in reversed(range(N_LAYERS)):
        sv = saved[layer]
        bg = ple_b_gate[layer:layer + 1]
        if layer == 0:
            token = open_group("layer1", [("attn_w_qkv", 0), ("attn_w_o", 0), ("ffn_w_up", 1), ("ffn_w_down", 1),
                                          ("ple_w_gate", 1), ("ple_w_proj", 1)])
        last = layer == N_LAYERS - 1
        res = ln_bwd(
            f"ln_ffn_bwd_{layer}", parts, sv["xh2"], sv["rs2"], tied(ln_ffn_g[layer:layer + 1], token),
            ple=(sv["pgl"], sv["pp"], bg), loss=(target, ln_ffn_b[layer:layer + 1]) if last else None)
        dz2, dg2, db2, dpp, dpgl, dbg = res[:6]
        if last:
            loss_part = res[6]
        small_grads[("ln_ffn_g", layer)] = dg2
        small_grads[("ln_ffn_b", layer)] = db2
        small_grads[("ple_b_gate", layer)] = dbg
        w_down = weight("ffn_w_down", layer)
        dhid = mm_rows_dx(f"ffn_down_dx_{layer}", dz2, w_down)
        big_grads[("ffn_w_down", layer)] = mm_rows_dw(f"ffn_down_dw_{layer}", sv["hid"], dz2)
        token = hook(big_grads[("ffn_w_down", layer)])
        dgv, ddw, ddb = ffn_act_bwd(f"ffn_act_bwd_{layer}", dhid, sv["gv"], ffn_dw_full[layer],
                                    tied(ffn_dw_b[layer:layer + 1], token))
        small_grads[("ffn_dw_w", layer)] = ddw
        small_grads[("ffn_dw_b", layer)] = ddb
        big_grads[("ffn_w_up", layer)] = mm_cols_dw(f"ffn_up_dw_{layer}", sv["x1"], dgv)
        t_up = mm_cols_dx(f"ffn_up_dx_{layer}", dgv, weight("ffn_w_up", layer))
        token = hook(t_up)
        big_grads[("ple_w_gate", layer)] = mm_rows_dw(f"ple_gate_dw_{layer}", sv["x1"], dpgl)
        t_gate = mm_rows_dx(f"ple_gate_dx_{layer}", dpgl, weight("ple_w_gate", layer))
        big_grads[("ple_w_proj", layer)] = mm_cols_dw(f"ple_proj_dw_{layer}", p_rows, dpp, part=(layer, N_LAYERS))
        token2 = hook(big_grads[("ple_w_proj", layer)])
        if token2 is not None:
            token = token2 if token is None else token + token2
        if layer == 0:
            token3 = open_group("layer0_ffn", [("ffn_w_up", 0), ("ffn_w_down", 0), ("ple_w_gate", 0), ("ple_w_proj", 0)])
            token = token3 if token is None else token + token3
        dz1, dg1, db1 = ln_bwd(
            f"ln_mix_bwd_{layer}", [(ALPHA, dz2), (1.0, t_up), (1.0, t_gate)], sv["xh1"], sv["rs1"],
            tied(ln_mix_g[layer:layer + 1], token))
        small_grads[("ln_mix_g", layer)] = dg1
        small_grads[("ln_mix_b", layer)] = db1
        if layer % 2 == 0:
            dcat = mm_rows_dx("mix_out_dx", dz1, weight("mix_w_out", 0))
            big_grads[("mix_w_out", 0)] = mm_rows_dw("mix_out_dw", sv["cat"], dz1)
            token = hook(big_grads[("mix_w_out", 0)])
            du, dpw, dps, dcw, dcb, dcg, dcbt = mixer_bwd(
                "mixer_bwd", dcat, sv["u"], sv["d"], sv["e"], sv["glu"], sv["hh"], sv["rs"],
                pool_w[0], pool_scale, conv_w_full, tied(conv_ln_g, token), conv_ln_b)
            small_grads[("pool_w", 0)] = dpw
            small_grads[("pool_scale", 0)] = dps
            small_grads[("conv_dw_w", 0)] = dcw
            small_grads[("conv_dw_b", 0)] = dcb
            small_grads[("conv_ln_g", 0)] = dcg
            small_grads[("conv_ln_b", 0)] = dcbt
            big_grads[("mix_w_in", 0)] = mm_cols_dw("mix_in_dw", sv["x_in"], du)
            hook(big_grads[("mix_w_in", 0)])
            open_group("layer0_mix", [("mix_w_in", 0), ("mix_w_out", 0)])
            dx_in = mm_cols_dx("mix_in_dx", du, weight("mix_w_in", 0), addend=(ALPHA, dz1))
            hook(dx_in)
        else:
            do = mm_rows_dx("attn_out_dx", dz1, weight("attn_w_o", 0), out_dtype=BF16)
            big_grads[("attn_w_o", 0)] = mm_rows_dw("attn_out_dw", sv["att"], dz1)
            dq, dk, dv, ds_sum = attn_bwd("attn_bwd", sv["qkvp"], sv["bias"], do)
            cols, sat = bias_grad_reduce("bias_grad", _shear_for_bias_grad(ds_sum))
            d_rel = jnp.concatenate(
                [jnp.zeros((N_HEADS, 1), F32),
                 jnp.flip(cols[:, 0, SHEAR_SAT + 1:SHEAR_W], axis=1),
                 sat[:, 0, 0:1]], axis=1)
            small_grads[("attn_rel_bias", 0)] = d_rel
            dqkv = jnp.concatenate([dq, dk, dv], axis=1)
            big_grads[("attn_w_qkv", 0)] = mm_cols_dw("attn_qkv_dw", sv["x_in"], dqkv)
            dx_in = mm_cols_dx("attn_qkv_dx", dqkv, weight("attn_w_qkv", 0), addend=(ALPHA, dz1))
        parts = [(1.0, dx_in)]
    grad_x = dx_in
    hook(grad_x)
    hook(grad_x)
    shard_grads = {}
    for red in reducers:
        shard_grads.update(red.result)

    big_out = {}
    for nm, w, m, v, _ in big:
        gl = [shard_grads[(nm, layer)] for layer in range(w.shape[0])]
        delta, new_m, new_v = adamw(f"adamw_{nm}", w, gl, m, v)
        big_out[nm] = (jnp.stack(gl, axis=0), delta, new_m, new_v)

    small = [
        ("pool_w", pool_w, m_pool_w, v_pool_w, None),
        ("pool_scale", pool_scale, m_pool_scale, v_pool_scale, None),
        ("conv_dw_w", conv_dw_w, m_conv_dw_w, v_conv_dw_w, 2),
        ("conv_dw_b", conv_dw_b, m_conv_dw_b, v_conv_dw_b, None),
        ("conv_ln_g", conv_ln_g, m_conv_ln_g, v_conv_ln_g, None),
        ("conv_ln_b", conv_ln_b, m_conv_ln_b, v_conv_ln_b, None),
        ("attn_rel_bias", attn_rel_bias, m_attn_rel_bias, v_attn_rel_bias, None),
        ("ln_mix_g", ln_mix_g, m_ln_mix_g, v_ln_mix_g, None),
        ("ln_mix_b", ln_mix_b, m_ln_mix_b, v_ln_mix_b, None),
        ("ffn_dw_w", ffn_dw_w, m_ffn_dw_w, v_ffn_dw_w, 2),
        ("ffn_dw_b", ffn_dw_b, m_ffn_dw_b, v_ffn_dw_b, None),
        ("ple_b_gate", ple_b_gate, m_ple_b_gate, v_ple_b_gate, None),
        ("ln_ffn_g", ln_ffn_g, m_ln_ffn_g, v_ln_ffn_g, None),
        ("ln_ffn_b", ln_ffn_b, m_ln_ffn_b, v_ln_ffn_b, None),
    ]
    full_grads = []
    for nm, w, _, _, shard_axis in small:
        full = list(w.shape)
        if shard_axis is not None:
            full[shard_axis] *= N_SHARD
        per_layer = [small_grads[(nm, layer)].reshape((1,) + tuple(full[1:])) for layer in range(w.shape[0])]
        full_grads.append(jnp.concatenate(per_layer, axis=0))
    packed = _pack(full_grads + [loss_part])
    total = sum_blocks("sum_small", gather_small("gather_small_grads", packed), 8)
    unpacked = _unpack(total, [g.shape for g in full_grads] + [loss_part.shape])
    loss = unpacked[-1][0, 0]
    local_grads = []
    for (nm, w, _, _, shard_axis), g in zip(small, unpacked[:-1]):
        if shard_axis is not None:
            width = w.shape[shard_axis]
            g = lax.dynamic_slice_in_dim(g, shard_idx * width, width, axis=shard_axis)
        local_grads.append(g.reshape(w.shape))
    shapes = [w.shape for _, w, _, _, _ in small]
    pg = _pack(local_grads)
    pw = _pack([w for _, w, _, _, _ in small])
    pm = _pack([m for _, _, m, _, _ in small])
    pv = _pack([v for _, _, _, v, _ in small])
    delta_s, new_m_s, new_v_s = adamw("adamw_small", pw[None], [pg], pm[None], pv[None])
    small_out = {}
    for (nm, _, _, _, _), g, d_, m_, v_ in zip(
            small, local_grads, _unpack(delta_s[0], shapes), _unpack(new_m_s[0], shapes), _unpack(new_v_s[0], shapes)):
        small_out[nm] = (g, d_, m_, v_)

    order = ["mix_w_in", "pool_w", "pool_scale", "conv_dw_w", "conv_dw_b", "conv_ln_g", "conv_ln_b", "mix_w_out",
             "attn_w_qkv", "attn_rel_bias", "attn_w_o", "ln_mix_g", "ln_mix_b", "ffn_w_up", "ffn_dw_w", "ffn_dw_b",
             "ffn_w_down", "ple_w_proj", "ple_w_gate", "ple_b_gate", "ln_ffn_g", "ln_ffn_b"]
    res = {**big_out, **small_out}
    outs = [loss, grad_x[None]]
    for slot in range(4):
        outs += [res[nm][slot] for nm in order]
    return tuple(outs)
```

```python
import functools
import math

import jax
import jax.numpy as jnp
from jax import lax
from jax.experimental import pallas as pl
from jax.experimental.pallas import tpu as pltpu

F32 = jnp.float32
BF16 = jnp.bfloat16
MESH = pl.DeviceIdType.MESH

N_LAYERS = 2
ALPHA = (2 * N_LAYERS) ** 0.25
LN_EPS = 1e-5
NEG_INF = -1e30
CHUNK = 64
LEFT_CHUNKS = 8
PAD_ROWS = LEFT_CHUNKS * CHUNK
HEAD_DIM = 64
N_HEADS = 16
MAX_REL = 256
POOL_WINDOWS = (2, 4, 8, 16)
POOL_GROUP = 128
CONV_K = 31
FFN_K = 3
CONV_HALO = 32
FFN_HALO = 8
FFN_TILE = 256
FFN_CHUNK_ROWS = 32
FFN_CHUNK_LANES = 256
Q_TILE = 256
K_WIN = Q_TILE + PAD_ROWS
SHEAR_W = Q_TILE + K_WIN
SHEAR_SAT = SHEAR_W - 2 * MAX_REL
N_SHARD = 4
LANES = 128
SUBLANES = 8

ADAM_LR = 0.001
ADAM_B1 = 0.9
ADAM_B2 = 0.999
ADAM_EPS = 1e-08
ADAM_WD = 0.01
ADAM_STEP = 10
ADAM_BC1 = 1.0 - ADAM_B1 ** ADAM_STEP
ADAM_BC2 = 1.0 - ADAM_B2 ** ADAM_STEP

DIMS = {
    "nn": (((1,), (0,)), ((), ())),
    "nt": (((1,), (1,)), ((), ())),
    "tn": (((0,), (0,)), ((), ())),
}


def _cp(vmem_mb=48, **kw):
    return pltpu.CompilerParams(vmem_limit_bytes=vmem_mb * 1024 * 1024, **kw)


def _in_hbm(a):
    return pltpu.with_memory_space_constraint(a, pltpu.HBM)


def _call(body, **kw):
    return pl.pallas_call(body, **kw)


def _dot(a, b, mode):
    return lax.dot_general(a.astype(BF16), b.astype(BF16), DIMS[mode], preferred_element_type=F32)


def _sig(x):
    return 1.0 / (1.0 + jnp.exp(-x))


def _row_tile(s):
    return min(512, s // 4)


def _mm_tile(s):
    return min(1024, s // 4)


def _mm(name, mode, a, b, in_specs, out_shape, out_spec, acc_shape, grid, nk, zero_first=False, vmem_mb=48,
        addend=None):
    out_f32 = out_shape.dtype == F32

    def body(a_ref, b_ref, *rest):
        k = pl.program_id(2)
        if addend is None:
            o_ref, scr = rest[0], rest[1:]
        else:
            add_ref, o_ref, scr = rest[0], rest[1], rest[2:]

        def compute():
            part = _dot(a_ref[...], b_ref[...], mode)
            if nk == 1:
                if addend is not None:
                    part = part + addend[0] * add_ref[...]
                o_ref[...] = part.astype(o_ref.dtype)
                return
            acc = o_ref if out_f32 else scr[0]

            @pl.when(k == 0)
            def _():
                acc[...] = part if addend is None else part + addend[0] * add_ref[...]

            @pl.when(k > 0)
            def _():
                acc[...] += part

            if not out_f32:
                @pl.when(k == nk - 1)
                def _():
                    o_ref[...] = acc[...].astype(o_ref.dtype)

        if zero_first:
            @pl.when(pl.program_id(1) == 0)
            def _():
                o_ref[...] = jnp.zeros(o_ref.shape, o_ref.dtype)

            pl.when(pl.program_id(1) > 0)(compute)
        else:
            compute()

    scratch = [] if (nk == 1 or out_f32) else [pltpu.VMEM(acc_shape, F32)]
    operands = [a, b] if addend is None else [a, b, addend[1]]
    specs = list(in_specs) if addend is None else list(in_specs) + [out_spec]
    return _call(
        body, name=name, grid=grid, in_specs=specs, out_specs=out_spec, out_shape=out_shape,
        scratch_shapes=scratch, compiler_params=_cp(vmem_mb),
    )(*operands)


def mm_cols_fwd(name, a, wc, out_dtype, pad_blocks=0, part=(0, 1)):
    s, k = a.shape
    s //= part[1]
    n4 = wc.shape[2]
    tm = _row_tile(s) if pad_blocks else _mm_tile(s)
    nt = s // tm
    first_block = part[0] * nt
    return _mm(
        name, "nn", a, wc,
        [pl.BlockSpec((tm, k), lambda j, i, r: (first_block + jnp.maximum(i - pad_blocks, 0), 0)),
         pl.BlockSpec((None, k, n4), lambda j, i, r: (j, 0, 0))],
        jax.ShapeDtypeStruct((s + pad_blocks * tm, N_SHARD * n4), out_dtype),
        pl.BlockSpec((tm, n4), lambda j, i, r: (i, j)),
        None, (N_SHARD, nt + pad_blocks, 1), 1, zero_first=pad_blocks > 0)


def mm_cols_dx(name, dy, wc, addend=None):
    s = dy.shape[0]
    _, k, n4 = wc.shape
    tm = _mm_tile(s)
    return _mm(
        name, "nt", dy, wc,
        [pl.BlockSpec((tm, n4), lambda g, i, r: (i, r)),
         pl.BlockSpec((None, k, n4), lambda g, i, r: (r, 0, 0))],
        jax.ShapeDtypeStruct((s, k), F32),
        pl.BlockSpec((tm, k), lambda g, i, r: (i, 0)),
        (tm, k), (1, s // tm, N_SHARD), N_SHARD, addend=addend)


def mm_cols_dw(name, a, dy, part=(0, 1)):
    s, k = a.shape
    s //= part[1]
    n4 = dy.shape[1] // N_SHARD
    tm = _mm_tile(s)
    nt = s // tm
    first_block = part[0] * nt
    return _mm(
        name, "tn", a, dy,
        [pl.BlockSpec((tm, k), lambda j, g, r: (first_block + r, 0)),
         pl.BlockSpec((tm, n4), lambda j, g, r: (r, j))],
        jax.ShapeDtypeStruct((N_SHARD, k, n4), F32),
        pl.BlockSpec((None, k, n4), lambda j, g, r: (j, 0, 0)),
        (k, n4), (N_SHARD, 1, nt), nt)


def _k_tile(k):
    return k if k <= 1024 else k // 2


def mm_rows_fwd(name, a, wr, out_dtype=F32):
    s, k = a.shape
    n = wr.shape[1]
    tm = _mm_tile(s)
    tk = _k_tile(k)
    nk = k // tk
    return _mm(
        name, "nn", a, wr,
        [pl.BlockSpec((tm, tk), lambda g, i, r: (i, r)),
         pl.BlockSpec((tk, n), lambda g, i, r: (r, 0))],
        jax.ShapeDtypeStruct((s, n), out_dtype),
        pl.BlockSpec((tm, n), lambda g, i, r: (i, 0)),
        (tm, n), (1, s // tm, nk), nk)


def mm_rows_dx(name, dy, wr, out_dtype=F32):
    s, n = dy.shape
    k = wr.shape[0]
    tm = _mm_tile(s)
    tk = _k_tile(k)
    return _mm(
        name, "nt", dy, wr,
        [pl.BlockSpec((tm, n), lambda j, i, r: (i, 0)),
         pl.BlockSpec((tk, n), lambda j, i, r: (j, 0))],
        jax.ShapeDtypeStruct((s, k), out_dtype),
        pl.BlockSpec((tm, tk), lambda j, i, r: (i, j)),
        None, (k // tk, s // tm, 1), 1)


def mm_rows_dw(name, a, dy):
    s, k = a.shape
    n = dy.shape[1]
    tm = _mm_tile(s)
    tk = _k_tile(k)
    nt = s // tm
    return _mm(
        name, "tn", a, dy,
        [pl.BlockSpec((tm, tk), lambda j, g, r: (r, j)),
         pl.BlockSpec((tm, n), lambda j, g, r: (r, 0))],
        jax.ShapeDtypeStruct((k, n), F32),
        pl.BlockSpec((tk, n), lambda j, g, r: (j, 0)),
        (tk, n), (k // tk, 1, nt), nt)


def _row(tm, c, col=0):
    return pl.BlockSpec((tm, c), lambda i: (i, col))


def _full(shape):
    nd = len(shape)
    return pl.BlockSpec(shape, lambda i: (0,) * nd)


def _prev(tm, h, c, col=0):
    return pl.BlockSpec((h, c), lambda i: (jnp.maximum(i * (tm // h) - 1, 0), col))


def _next(tm, h, c, s, col=0):
    return pl.BlockSpec((h, c), lambda i: (jnp.minimum((i + 1) * (tm // h), s // h - 1), col))


def _acc_add(ref, first, val):
    @pl.when(first)
    def _():
        ref[...] = val

    @pl.when(jnp.logical_not(first))
    def _():
        ref[...] += val


def _colsum(v):
    return jnp.sum(v, axis=0, keepdims=True)


def _ln_stats(z):
    mu = jnp.mean(z, axis=-1, keepdims=True)
    zc = z - mu
    var = jnp.mean(zc * zc, axis=-1, keepdims=True)
    rstd = lax.rsqrt(var + LN_EPS)
    return zc * rstd, rstd


def _ln_bwd(dxhat, xhat, rstd):
    m1 = jnp.mean(dxhat, axis=-1, keepdims=True)
    m2 = jnp.mean(dxhat * xhat, axis=-1, keepdims=True)
    return rstd * (dxhat - m1 - xhat * m2)


def ln_fwd(name, x, f, g, b, ple=None, emit_y=True):
    s, d = x.shape
    tm = _row_tile(s)
    n_in = 2 + (3 if ple is not None else 0)

    def body(*refs):
        x_ref, f_ref = refs[0], refs[1]
        g_ref, b_ref = refs[n_in], refs[n_in + 1]
        xh_ref, rs_ref = refs[-2:]
        z = ALPHA * x_ref[...] + f_ref[...]
        if ple is not None:
            pgl_ref, pp_ref, bg_ref = refs[2:5]
            z = z + _sig(pgl_ref[...] + bg_ref[...]) * pp_ref[...]
        xhat, rstd = _ln_stats(z)
        if emit_y:
            refs[n_in + 2][...] = xhat * g_ref[...] + b_ref[...]
        xh_ref[...] = xhat
        rs_ref[...] = jnp.broadcast_to(rstd, rs_ref.shape)

    ins = [x, f]
    specs = [_row(tm, d), _row(tm, d)]
    if ple is not None:
        pgl, pp, bg = ple
        ins += [pgl, pp, bg]
        specs += [_row(tm, d), _row(tm, d), _full((1, d))]
    ins += [g, b]
    specs += [_full((1, d)), _full((1, d))]
    n_y = 1 if emit_y else 0
    outs = _call(
        body, name=name, grid=(s // tm,), in_specs=specs,
        out_specs=[_row(tm, d)] * (n_y + 1) + [_row(tm, LANES)],
        out_shape=[jax.ShapeDtypeStruct((s, d), F32)] * (n_y + 1) + [jax.ShapeDtypeStruct((s, LANES), F32)],
        compiler_params=_cp(),
    )(*ins)
    return (outs[0], outs[1], outs[2]) if emit_y else (None, outs[0], outs[1])


def ln_bwd(name, parts, xhat, rstd, g, ple=None, loss=None):
    s, d = xhat.shape
    tm = _row_tile(s)
    coefs = [c for c, _ in parts]
    n_p = len(parts)
    n_ple = 3 if ple is not None else 0
    n_in = n_p + 3 + n_ple + (2 if loss is not None else 0)

    def body(*refs):
        first = pl.program_id(0) == 0
        xh = refs[n_p][...]
        rs = refs[n_p + 1][:, 0:1]
        g_v = refs[n_p + 2][...]
        outs = refs[n_in:]
        if loss is not None:
            t_ref, b_ref = refs[n_p + 3 + n_ple:n_p + 5 + n_ple]
            err = (xh * g_v + b_ref[...]) - t_ref[...]
            dy = err * (1.0 / d)
            part = 0.5 * jnp.sum(jnp.mean(err * err, axis=-1, keepdims=True), axis=0, keepdims=True)
            _acc_add(outs[-1], first, jnp.broadcast_to(part, outs[-1].shape))
        else:
            dy = coefs[0] * refs[0][...].astype(F32)
            for j in range(1, n_p):
                dy = dy + coefs[j] * refs[j][...].astype(F32)
        dz = _ln_bwd(dy * g_v, xh, rs)
        outs[0][...] = dz
        _acc_add(outs[1], first, _colsum(dy * xh))
        _acc_add(outs[2], first, _colsum(dy))
        if ple is not None:
            pgl_ref, pp_ref, bg_ref = refs[n_p + 3:n_p + 6]
            pg = _sig(pgl_ref[...] + bg_ref[...])
            dpgl = dz * pp_ref[...] * pg * (1.0 - pg)
            outs[3][...] = (dz * pg).astype(BF16)
            outs[4][...] = dpgl.astype(BF16)
            _acc_add(outs[5], first, _colsum(dpgl))

    ins = [p for _, p in parts] + [xhat, rstd, g]
    specs = [_row(tm, d)] * n_p + [_row(tm, d), _row(tm, LANES), _full((1, d))]
    out_specs = [_row(tm, d), _full((1, d)), _full((1, d))]
    out_shape = [jax.ShapeDtypeStruct((s, d), F32), jax.ShapeDtypeStruct((1, d), F32),
                 jax.ShapeDtypeStruct((1, d), F32)]
    if ple is not None:
        pgl, pp, bg = ple
        ins += [pgl, pp, bg]
        specs += [_row(tm, d), _row(tm, d), _full((1, d))]
        out_specs += [_row(tm, d), _row(tm, d), _full((1, d))]
        out_shape += [jax.ShapeDtypeStruct((s, d), BF16), jax.ShapeDtypeStruct((s, d), BF16),
                      jax.ShapeDtypeStruct((1, d), F32)]
    if loss is not None:
        target, b = loss
        ins += [target, b]
        specs += [_row(tm, d), _full((1, d))]
        out_specs += [_full((8, LANES))]
        out_shape += [jax.ShapeDtypeStruct((8, LANES), F32)]
    return _call(
        body, name=name, grid=(s // tm,), in_specs=specs, out_specs=out_specs, out_shape=out_shape,
        compiler_params=_cp(),
    )(*ins)


def _fill_rotations(rot_ref, x, direction):
    n = x.shape[0]
    rot_ref[0] = x
    for b in range(1, SUBLANES):
        if direction < 0:
            rot_ref[b, SUBLANES:n, :] = x[SUBLANES - b:n - b]
        else:
            rot_ref[b, 0:n - SUBLANES, :] = x[b:n - SUBLANES + b]


def _rotated(rot_ref, start, rows, cs, direction=-1):
    b = (-start) % SUBLANES if direction < 0 else start % SUBLANES
    aligned = start + b if direction < 0 else start - b
    return rot_ref[b, pl.ds(aligned, rows), cs]


def _tile_pos(i, tm, rows):
    return (i * tm + lax.broadcasted_iota(jnp.int32, (rows, 1), 0) + 1).astype(F32)


def mixer_fwd(name, u, pool_w, pool_scale, conv_w, conv_b, cn_g, cn_b):
    s = u.shape[0]
    dp = 512
    tm = min(256, s // 4)
    h = CONV_HALO

    def body(a_c, a_p, bv_c, bv_p, bg_c, bg_p, pw_ref, ps_ref, cw_ref, cb_ref, cg_ref, cbt_ref,
             cat_ref, d_ref, e_ref, glu_ref, hh_ref, rs_ref, ext_a, rot_g, conv_out):
        i = pl.program_id(0)
        first = i == 0
        ext_a[0:h, :] = jnp.where(first, 0.0, a_p[...])
        ext_a[h:, :] = a_c[...]
        glu = bv_c[...] * _sig(bg_c[...])
        glu_ref[...] = glu
        _fill_rotations(rot_g, jnp.concatenate([jnp.where(first, 0.0, bv_p[...] * _sig(bg_p[...])), glu], axis=0), -1)
        pos = _tile_pos(i, tm, tm)
        for gi, w in enumerate(POOL_WINDOWS):
            cs = slice(gi * POOL_GROUP, (gi + 1) * POOL_GROUP)
            a_g = ext_a[pl.ds(h, tm), cs]
            acc = a_g
            for sh in range(1, w):
                acc = acc + ext_a[pl.ds(h - sh, tm), cs]
            d_g = acc / jnp.minimum(pos, float(w)) - a_g
            d_ref[:, cs] = d_g.astype(BF16)
            e_g = _dot(d_g, pw_ref[gi], "nn")
            e_ref[:, cs] = e_g
            cat_ref[:, cs] = (e_g * ps_ref[:, cs]).astype(BF16)
        for lg in range(dp // LANES):
            cs = slice(lg * LANES, (lg + 1) * LANES)
            acc = jnp.broadcast_to(cb_ref[:, cs], (tm, LANES))
            for sh in range(CONV_K):
                acc = acc + _rotated(rot_g, h - sh, tm, cs) * cw_ref[pl.ds(CONV_K - 1 - sh, 1), cs]
            conv_out[:, cs] = acc
        hhat, rstd = _ln_stats(conv_out[...])
        hl = hhat * cg_ref[...] + cbt_ref[...]
        cat_ref[:, dp:] = (hl * _sig(hl)).astype(BF16)
        hh_ref[...] = hhat
        rs_ref[...] = jnp.broadcast_to(rstd, rs_ref.shape)

    specs = [_row(tm, dp, 0), _prev(tm, h, dp, 0), _row(tm, dp, 1), _prev(tm, h, dp, 1),
             _row(tm, dp, 2), _prev(tm, h, dp, 2),
             _full((4, POOL_GROUP, POOL_GROUP)), _full((1, dp)), _full((CONV_K, dp)),
             _full((1, dp)), _full((1, dp)), _full((1, dp))]
    out_specs = [_row(tm, 2 * dp), _row(tm, dp), _row(tm, dp), _row(tm, dp), _row(tm, dp), _row(tm, LANES)]
    out_shape = [jax.ShapeDtypeStruct((s, 2 * dp), BF16), jax.ShapeDtypeStruct((s, dp), BF16),
                 jax.ShapeDtypeStruct((s, dp), F32), jax.ShapeDtypeStruct((s, dp), F32),
                 jax.ShapeDtypeStruct((s, dp), F32), jax.ShapeDtypeStruct((s, LANES), F32)]
    return _call(
        body, name=name, grid=(s // tm,), in_specs=specs, out_specs=out_specs, out_shape=out_shape,
        scratch_shapes=[pltpu.VMEM((h + tm, dp), F32), pltpu.VMEM((SUBLANES, h + tm, dp), F32),
                        pltpu.VMEM((tm, dp), F32)],
        compiler_params=_cp(),
    )(u, u, u, u, u, u, pool_w, pool_scale, conv_w, conv_b, cn_g, cn_b)


def mixer_bwd(name, dcat, u, d_sv, e_sv, glu_sv, hh_sv, rs_sv, pool_w, pool_scale, conv_w, cn_g, cn_b):
    s = u.shape[0]
    dp = 512
    tm = min(256, s // 4)
    h = CONV_HALO
    nt = s // tm

    def body(dc_c, dc_n, bv_c, bg_c, d_c, e_c, gl_c, gl_p, hh_c, hh_n, rs_c, rs_n,
             pw_ref, ps_ref, cw_ref, cg_ref, cbt_ref,
             du_ref, dpw_ref, dps_ref, dcw_ref, dcb_ref, dcg_ref, dcbt_ref,
             ext_dh, ext_g, ext_r):
        i = pl.program_id(0)
        first = i == 0
        last = i == nt - 1
        cg = cg_ref[...]

        def conv_grads(dyb, hhat, rstd):
            hl = hhat * cg + cbt_ref[...]
            sg = _sig(hl)
            dhl = dyb * (sg * (1.0 + hl * (1.0 - sg)))
            return _ln_bwd(dhl * cg, hhat, rstd), dhl

        hh_cur = hh_c[...]
        dh_c, dhl_c = conv_grads(dc_c[:, dp:], hh_cur, rs_c[:, 0:1])
        dh_n, _ = conv_grads(dc_n[:, dp:], hh_n[...], rs_n[:, 0:1])
        _fill_rotations(ext_dh, jnp.concatenate([dh_c, jnp.where(last, 0.0, dh_n)], axis=0), 1)
        _fill_rotations(ext_g, jnp.concatenate([jnp.where(first, 0.0, gl_p[...]), gl_c[...]], axis=0), -1)

        @pl.when(first)
        def _():
            dcw_ref[...] = jnp.zeros(dcw_ref.shape, F32)

        for lg in range(dp // LANES):
            cs = slice(lg * LANES, (lg + 1) * LANES)
            dglu = jnp.zeros((tm, LANES), F32)
            for sh in range(CONV_K):
                dglu = dglu + _rotated(ext_dh, sh, tm, cs, 1) * cw_ref[pl.ds(CONV_K - 1 - sh, 1), cs]
            dh_g = ext_dh[0, pl.ds(0, tm), cs]
            for sh in range(CONV_K):
                dcw_ref[pl.ds(CONV_K - 1 - sh, 1), cs] += _colsum(dh_g * _rotated(ext_g, h - sh, tm, cs))
            sgate = _sig(bg_c[:, cs])
            du_ref[:, dp + lg * LANES:dp + (lg + 1) * LANES] = dglu * sgate
            du_ref[:, 2 * dp + lg * LANES:2 * dp + (lg + 1) * LANES] = dglu * bv_c[:, cs] * sgate * (1.0 - sgate)
        _acc_add(dcb_ref, first, _colsum(dh_c))
        _acc_add(dcg_ref, first, _colsum(dhl_c * hh_cur))
        _acc_add(dcbt_ref, first, _colsum(dhl_c))

        pos_c = _tile_pos(i, tm, tm)
        pos_n = _tile_pos(i + 1, tm, h)
        _acc_add(dps_ref, first, _colsum(dc_c[:, :dp] * e_c[...]))
        for gi, w in enumerate(POOL_WINDOWS):
            cs = slice(gi * POOL_GROUP, (gi + 1) * POOL_GROUP)
            pw = pw_ref[gi]
            de_c = dc_c[:, cs] * ps_ref[:, cs]
            de_n = dc_n[:, cs] * ps_ref[:, cs]
            dd_c = _dot(de_c, pw, "nt")
            dd_n = _dot(de_n, pw, "nt")
            ext_r[0:tm, :] = dd_c / jnp.minimum(pos_c, float(w))
            ext_r[tm:, :] = jnp.where(last, 0.0, dd_n / jnp.minimum(pos_n, float(w)))
            acc = -dd_c
            for sh in range(w):
                acc = acc + ext_r[pl.ds(sh, tm), :]
            du_ref[:, cs] = acc
            dpw_g = _dot(d_c[:, cs], de_c, "tn")

            @pl.when(first)
            def _():
                dpw_ref[gi] = dpw_g

            @pl.when(jnp.logical_not(first))
            def _():
                dpw_ref[gi] += dpw_g

    specs = [_row(tm, 2 * dp), _next(tm, h, 2 * dp, s), _row(tm, dp, 1), _row(tm, dp, 2),
             _row(tm, dp), _row(tm, dp), _row(tm, dp), _prev(tm, h, dp),
             _row(tm, dp), _next(tm, h, dp, s), _row(tm, LANES), _next(tm, h, LANES, s),
             _full((4, POOL_GROUP, POOL_GROUP)), _full((1, dp)), _full((CONV_K, dp)),
             _full((1, dp)), _full((1, dp))]
    out_specs = [_row(tm, 3 * dp), _full((4, POOL_GROUP, POOL_GROUP)), _full((1, dp)), _full((CONV_K, dp)),
                 _full((1, dp)), _full((1, dp)), _full((1, dp))]
    out_shape = [jax.ShapeDtypeStruct((s, 3 * dp), F32),
                 jax.ShapeDtypeStruct((4, POOL_GROUP, POOL_GROUP), F32), jax.ShapeDtypeStruct((1, dp), F32),
                 jax.ShapeDtypeStruct((CONV_K, dp), F32), jax.ShapeDtypeStruct((1, dp), F32),
                 jax.ShapeDtypeStruct((1, dp), F32), jax.ShapeDtypeStruct((1, dp), F32)]
    return _call(
        body, name=name, grid=(nt,), in_specs=specs, out_specs=out_specs, out_shape=out_shape,
        scratch_shapes=[pltpu.VMEM((SUBLANES, tm + h, dp), F32), pltpu.VMEM((SUBLANES, h + tm, dp), F32),
                        pltpu.VMEM((tm + h, POOL_GROUP), F32)],
        compiler_params=_cp(),
    )(dcat, dcat, u, u, d_sv, e_sv, glu_sv, glu_sv, hh_sv, hh_sv, rs_sv, rs_sv,
      pool_w, pool_scale, conv_w, cn_g, cn_b)


GELU_C = math.sqrt(2.0 / math.pi)


def _gelu_parts(x):
    x2 = x * x
    t = jnp.tanh(x * (GELU_C + (GELU_C * 0.044715) * x2))
    half_1pt = 0.5 + 0.5 * t
    gelu = x * half_1pt
    dgelu = half_1pt + (0.5 * x) * (1.0 - t * t) * (GELU_C + (3.0 * GELU_C * 0.044715) * x2)
    return gelu, dgelu


def ffn_act_fwd(name, gv, dw_w, dw_b):
    s = gv.shape[0]
    dff = gv.shape[1] // 2
    tm = min(FFN_TILE, s // 4)
    h = FFN_HALO
    rc = FFN_CHUNK_ROWS
    lw = FFN_CHUNK_LANES

    def body(g_c, g_p, v_c, w_ref, b_ref, hid_ref):
        first = pl.program_id(0) == 0

        def chunk(ci, carry):
            r0 = pl.multiple_of(ci * rc, rc)
            above = pl.multiple_of(jnp.maximum(r0 - h, 0), h)
            for lg in range(dff // lw):
                cs = slice(lg * lw, (lg + 1) * lw)
                top = jnp.where(ci == 0, jnp.where(first, 0.0, g_p[:, cs]), g_c[pl.ds(above, h), cs])
                win = jnp.concatenate([top, g_c[pl.ds(r0, rc), cs]], axis=0)
                gc = jnp.broadcast_to(b_ref[:, cs], (rc, lw))
                for sh in range(FFN_K):
                    gc = gc + win[h - sh:h - sh + rc] * w_ref[pl.ds(FFN_K - 1 - sh, 1), cs]
                gelu, _ = _gelu_parts(gc)
                hid_ref[pl.ds(r0, rc), cs] = (gelu * v_c[pl.ds(r0, rc), cs]).astype(BF16)
            return carry

        lax.fori_loop(0, tm // rc, chunk, 0)

    return _call(
        body, name=name, grid=(s // tm,),
        in_specs=[_row(tm, dff, 0), _prev(tm, h, dff, 0), _row(tm, dff, 1), _full((FFN_K, dff)), _full((1, dff))],
        out_specs=_row(tm, dff), out_shape=jax.ShapeDtypeStruct((s, dff), BF16),
        compiler_params=_cp(),
    )(gv, gv, gv, dw_w, dw_b)


def ffn_act_bwd(name, dhid, gv, dw_w, dw_b):
    s = gv.shape[0]
    dff = gv.shape[1] // 2
    tm = min(FFN_TILE, s // 4)
    h = FFN_HALO
    nt = s // tm
    rc = FFN_CHUNK_ROWS
    lw = FFN_CHUNK_LANES
    n_chunks = tm // rc

    def body(dh_c, dh_n, g_p, g_c, g_n, v_c, v_n, w_ref, b_ref, dgv_ref, dw_ref, db_ref):
        i = pl.program_id(0)
        first = i == 0
        last = i == nt - 1

        @pl.when(first)
        def _():
            dw_ref[...] = jnp.zeros(dw_ref.shape, F32)
            db_ref[...] = jnp.zeros(db_ref.shape, F32)

        def chunk(ci, carry):
            r0 = pl.multiple_of(ci * rc, rc)
            above = pl.multiple_of(jnp.maximum(r0 - h, 0), h)
            below = pl.multiple_of(jnp.minimum(r0 + rc, tm - h), h)
            at_end = ci == n_chunks - 1
            for lg in range(dff // lw):
                cs = slice(lg * lw, (lg + 1) * lw)
                top = jnp.where(ci == 0, jnp.where(first, 0.0, g_p[:, cs]), g_c[pl.ds(above, h), cs])
                bot = jnp.where(at_end, g_n[:, cs], g_c[pl.ds(below, h), cs])
                win = jnp.concatenate([top, g_c[pl.ds(r0, rc), cs], bot], axis=0)
                shifted = [win[h - sh:h - sh + rc + h] for sh in range(FFN_K)]
                gc = jnp.broadcast_to(b_ref[:, cs], (rc + h, lw))
                for sh in range(FFN_K):
                    gc = gc + shifted[sh] * w_ref[pl.ds(FFN_K - 1 - sh, 1), cs]
                gelu, dgelu = _gelu_parts(gc)
                dh_mid = dh_c[pl.ds(r0, rc), cs]
                hv_bot = jnp.where(at_end, jnp.where(last, 0.0, dh_n[:, cs] * v_n[:, cs]),
                                   dh_c[pl.ds(below, h), cs] * v_c[pl.ds(below, h), cs])
                dgc = jnp.concatenate([dh_mid * v_c[pl.ds(r0, rc), cs], hv_bot], axis=0) * dgelu
                dgate = jnp.zeros((rc, lw), F32)
                for sh in range(FFN_K):
                    dgate = dgate + dgc[sh:sh + rc] * w_ref[pl.ds(FFN_K - 1 - sh, 1), cs]
                dgv_ref[pl.ds(r0, rc), cs] = dgate.astype(BF16)
                dgv_ref[pl.ds(r0, rc), slice(dff + lg * lw, dff + (lg + 1) * lw)] = (dh_mid * gelu[0:rc]).astype(BF16)
                dgc_mid = dgc[0:rc]
                for sh in range(FFN_K):
                    dw_ref[pl.ds(FFN_K - 1 - sh, 1), cs] += _colsum(dgc_mid * shifted[sh][0:rc])
                db_ref[:, cs] += _colsum(dgc_mid)
            return carry

        lax.fori_loop(0, n_chunks, chunk, 0)

    return _call(
        body, name=name, grid=(nt,),
        in_specs=[_row(tm, dff), _next(tm, h, dff, s),
                  _prev(tm, h, dff, 0), _row(tm, dff, 0), _next(tm, h, dff, s, 0),
                  _row(tm, dff, 1), _next(tm, h, dff, s, 1),
                  _full((FFN_K, dff)), _full((1, dff))],
        out_specs=[_row(tm, 2 * dff), _full((FFN_K, dff)), _full((1, dff))],
        out_shape=[jax.ShapeDtypeStruct((s, 2 * dff), BF16), jax.ShapeDtypeStruct((FFN_K, dff), F32),
                   jax.ShapeDtypeStruct((1, dff), F32)],
        compiler_params=_cp(),
    )(dhid, dhid, gv, gv, gv, gv, gv, dw_w, dw_b)


def _toeplitz_bias(rel_bias):
    nh = rel_bias.shape[0]
    zero = jnp.zeros((nh, 1), rel_bias.dtype)
    line = jnp.concatenate(
        [zero, jnp.broadcast_to(rel_bias[:, 2 * MAX_REL:], (nh, SHEAR_SAT)),
         jnp.flip(rel_bias[:, 1:2 * MAX_REL], axis=1), zero], axis=1)
    z = jnp.broadcast_to(line[:, None, :], (nh, Q_TILE, SHEAR_W + 1)).reshape(nh, Q_TILE * (SHEAR_W + 1))
    return z[:, :Q_TILE * SHEAR_W].reshape(nh, Q_TILE, SHEAR_W)[:, :, Q_TILE:]


def _shear_for_bias_grad(ds_sum):
    nh = ds_sum.shape[0]
    z = jnp.pad(ds_sum, ((0, 0), (0, 0), (Q_TILE, 0))).reshape(nh, Q_TILE * SHEAR_W)
    return jnp.pad(z, ((0, 0), (0, Q_TILE))).reshape(nh, Q_TILE, SHEAR_W + 1)


def _band_masked(bias):
    qc = lax.broadcasted_iota(jnp.int32, (Q_TILE, K_WIN), 0) // CHUNK
    kc = lax.broadcasted_iota(jnp.int32, (Q_TILE, K_WIN), 1) // CHUNK
    return jnp.where(((kc >= qc) & (kc <= qc + LEFT_CHUNKS))[None], bias, NEG_INF)


def _stack_heads(x2):
    lane = lax.broadcasted_iota(jnp.int32, x2.shape, 1)
    zero = jnp.zeros_like(x2)
    return jnp.concatenate([jnp.where(lane < HEAD_DIM, x2, zero), jnp.where(lane < HEAD_DIM, zero, x2)], axis=0)


def _unstack_heads(x_st):
    lane = lax.broadcasted_iota(jnp.int32, (Q_TILE, LANES), 1)
    return jnp.where(lane < HEAD_DIM, x_st[:Q_TILE], x_st[Q_TILE:])


def _attn_probs(q_st, k3, bias_st, t):
    sc = _dot(q_st, k3, "nt") * (HEAD_DIM ** -0.5) + bias_st
    col = lax.broadcasted_iota(jnp.int32, sc.shape, 1)
    sc = jnp.where(col >= PAD_ROWS - t * Q_TILE, sc, NEG_INF)
    m = jnp.max(sc, axis=-1, keepdims=True)
    p = jnp.exp(sc - m)
    return p * (1.0 / jnp.sum(p, axis=-1, keepdims=True))


def _attn_specs(d_model):
    nq = PAD_ROWS // Q_TILE
    hp_k = d_model // LANES
    specs = [pl.BlockSpec((Q_TILE, LANES), lambda hp, t: (t + nq, hp))]
    for which in (1, 2):
        for j in range(K_WIN // Q_TILE):
            specs.append(pl.BlockSpec((Q_TILE, LANES), lambda hp, t, j=j, which=which: (t + j, which * hp_k + hp)))
    specs.append(pl.BlockSpec((2, Q_TILE, K_WIN), lambda hp, t: (hp, 0, 0)))
    return specs


def attn_fwd(name, qkvp, bias):
    s = qkvp.shape[0] - PAD_ROWS
    d_model = qkvp.shape[1] // 3
    nw = K_WIN // Q_TILE

    def body(q_ref, *refs):
        k_refs, v_refs, b_ref, o_ref = refs[:nw], refs[nw:2 * nw], refs[2 * nw], refs[2 * nw + 1]
        t = pl.program_id(1)
        k3 = jnp.concatenate([r[...] for r in k_refs], axis=0)
        v3 = jnp.concatenate([r[...] for r in v_refs], axis=0)
        p = _attn_probs(_stack_heads(q_ref[...]), k3, b_ref[...].reshape(2 * Q_TILE, K_WIN), t)
        o_ref[...] = _unstack_heads(_dot(p, v3, "nn")).astype(BF16)

    return _call(
        body, name=name, grid=(d_model // LANES, s // Q_TILE),
        in_specs=_attn_specs(d_model), out_specs=pl.BlockSpec((Q_TILE, LANES), lambda hp, t: (t, hp)),
        out_shape=jax.ShapeDtypeStruct((s, d_model), BF16), compiler_params=_cp(),
    )(qkvp, *([qkvp] * (2 * nw)), bias)


def attn_bwd(name, qkvp, bias, do):
    s = qkvp.shape[0] - PAD_ROWS
    d_model = qkvp.shape[1] // 3
    nw = K_WIN // Q_TILE
    nt = s // Q_TILE
    scale = HEAD_DIM ** -0.5

    def body(q_ref, *refs):
        k_refs, v_refs = refs[:nw], refs[nw:2 * nw]
        b_ref, do_ref, dq_ref, dk_ref, dv_ref, ds_ref, dk_acc, dv_acc = refs[2 * nw:]
        t = pl.program_id(1)
        first = t == 0

        @pl.when(first)
        def _():
            dk_acc[...] = jnp.zeros(dk_acc.shape, F32)
            dv_acc[...] = jnp.zeros(dv_acc.shape, F32)

        q_st = _stack_heads(q_ref[...])
        do_st = _stack_heads(do_ref[...])
        k3 = jnp.concatenate([r[...] for r in k_refs], axis=0)
        v3 = jnp.concatenate([r[...] for r in v_refs], axis=0)
        p = _attn_probs(q_st, k3, b_ref[...].reshape(2 * Q_TILE, K_WIN), t)
        dp = _dot(do_st, v3, "nt")
        ds = p * (dp - jnp.sum(p * dp, axis=-1, keepdims=True))
        _acc_add(ds_ref, first, ds.reshape(2, Q_TILE, K_WIN))
        dsb = (ds * scale).astype(BF16)
        dq_ref[...] = _unstack_heads(_dot(dsb, k3, "nn")).astype(BF16)
        start = pl.multiple_of(t * Q_TILE, Q_TILE)
        dk_acc[pl.ds(start, K_WIN), :] += _dot(dsb, q_st, "tn")
        dv_acc[pl.ds(start, K_WIN), :] += _dot(p, do_st, "tn")

        @pl.when(t == nt - 1)
        def _():
            dk_ref[...] = dk_acc[pl.ds(PAD_ROWS, s), :].astype(BF16)
            dv_ref[...] = dv_acc[pl.ds(PAD_ROWS, s), :].astype(BF16)

    specs = _attn_specs(d_model) + [pl.BlockSpec((Q_TILE, LANES), lambda hp, t: (t, hp))]
    col_spec = pl.BlockSpec((s, LANES), lambda hp, t: (0, hp))
    return _call(
        body, name=name, grid=(d_model // LANES, nt), in_specs=specs,
        out_specs=[pl.BlockSpec((Q_TILE, LANES), lambda hp, t: (t, hp)), col_spec, col_spec,
                   pl.BlockSpec((2, Q_TILE, K_WIN), lambda hp, t: (hp, 0, 0))],
        out_shape=[jax.ShapeDtypeStruct((s, d_model), BF16)] * 3
        + [jax.ShapeDtypeStruct((N_HEADS, Q_TILE, K_WIN), F32)],
        scratch_shapes=[pltpu.VMEM((PAD_ROWS + s, LANES), F32), pltpu.VMEM((PAD_ROWS + s, LANES), F32)],
        compiler_params=_cp(),
    )(qkvp, *([qkvp] * (2 * nw)), bias, do)


def bias_grad_reduce(name, sheared):
    nh, _, width = sheared.shape

    def body(x_ref, col_ref, sat_ref):
        cols = _colsum(x_ref[...])
        col_ref[...] = cols
        k = lax.broadcasted_iota(jnp.int32, cols.shape, 1)
        tot = jnp.sum(jnp.where((k >= 1) & (k <= SHEAR_SAT), cols, 0.0), axis=-1, keepdims=True)
        sat_ref[...] = jnp.broadcast_to(tot, sat_ref.shape)

    return _call(
        body, name=name, grid=(nh,),
        in_specs=[pl.BlockSpec((None, Q_TILE, width), lambda hh: (hh, 0, 0))],
        out_specs=[pl.BlockSpec((None, 1, width), lambda hh: (hh, 0, 0)),
                   pl.BlockSpec((None, 1, LANES), lambda hh: (hh, 0, 0))],
        out_shape=[jax.ShapeDtypeStruct((nh, 1, width), F32), jax.ShapeDtypeStruct((nh, 1, LANES), F32)],
        compiler_params=_cp(),
    )(sheared)


def _ew_rows(r, most=512):
    for cand in (512, 256, 128, 64, 32, 16, 8):
        if cand <= most and r % cand == 0:
            return cand
    return r


def cast_into_gathered(name, w, layer, s_idx, n_blocks=N_SHARD, dtype=BF16):
    r, c = w.shape[-2:]
    tr = _ew_rows(r)

    def body(s_ref, w_ref, o_ref):
        o_ref[...] = w_ref[...].astype(dtype)

    grid_spec = pltpu.PrefetchScalarGridSpec(
        num_scalar_prefetch=1, grid=(r // tr,),
        in_specs=[pl.BlockSpec((None, tr, c), lambda i, s_ref: (layer, i, 0))],
        out_specs=pl.BlockSpec((None, tr, c), lambda i, s_ref: (s_ref[0], i, 0)))
    return _call(
        body, name=name, grid_spec=grid_spec, out_shape=jax.ShapeDtypeStruct((n_blocks, r, c), dtype),
        compiler_params=_cp(),
    )(s_idx, w)


def adamw(name, w, grads, m, v, token=None):
    nl, r, c = w.shape
    tr = _ew_rows(r, 256)

    def body(*refs):
        w_ref, m_ref, v_ref = refs[0], refs[1], refs[2]
        g_refs = refs[3:3 + nl]
        d_ref, nm_ref, nv_ref = refs[-3:]
        layer = pl.program_id(0)
        g = g_refs[0][...]
        for j in range(1, nl):
            g = jnp.where(layer == j, g_refs[j][...], g)
        nm = ADAM_B1 * m_ref[...] + (1.0 - ADAM_B1) * g
        nv = ADAM_B2 * v_ref[...] + (1.0 - ADAM_B2) * (g * g)
        m_hat = nm / ADAM_BC1
        v_hat = nv / ADAM_BC2
        d_ref[...] = -ADAM_LR * (m_hat / (jnp.sqrt(v_hat) + ADAM_EPS) + ADAM_WD * w_ref[...])
        nm_ref[...] = nm
        nv_ref[...] = nv

    p_spec = pl.BlockSpec((None, tr, c), lambda l, i: (l, i, 0))
    g_spec = pl.BlockSpec((tr, c), lambda l, i: (i, 0))
    extra = [] if token is None else [token]
    extra_specs = [] if token is None else [ANY_SPEC]
    return _call(
        body, name=name, grid=(nl, r // tr), in_specs=[p_spec] * 3 + [g_spec] * nl + extra_specs,
        out_specs=[p_spec] * 3, out_shape=[jax.ShapeDtypeStruct((nl, r, c), F32)] * 3, compiler_params=_cp(),
    )(w, m, v, *grads, *extra)


def sum_blocks(name, gathered, n_blocks):
    r = gathered.shape[0] // n_blocks
    c = gathered.shape[1]
    tr = _ew_rows(r)
    nt = r // tr

    def body(*refs):
        acc = refs[0][...]
        for j in range(1, n_blocks):
            acc = acc + refs[j][...]
        refs[-1][...] = acc

    specs = [pl.BlockSpec((tr, c), lambda i, j=j: (j * nt + i, 0)) for j in range(n_blocks)]
    return _call(
        body, name=name, grid=(nt,), in_specs=specs, out_specs=pl.BlockSpec((tr, c), lambda i: (i, 0)),
        out_shape=jax.ShapeDtypeStruct((r, c), F32), compiler_params=_cp(),
    )(*([gathered] * n_blocks))


def _place():
    return lax.axis_index("x"), lax.axis_index("y"), lax.axis_index("c")


def _other_chips(x, y):
    return [(1 - x, y), (x, 1 - y), (1 - x, 1 - y)]


HBM_SPEC = pl.BlockSpec(memory_space=pltpu.HBM)
SEM_SPEC = pl.BlockSpec(memory_space=pltpu.SEMAPHORE)
ANY_SPEC = pl.BlockSpec(memory_space=pl.ANY)
EFFECT = pltpu.SideEffectType.DATAFLOW_SIDE_EFFECTING


def copies_start(name, bufs, plan, n_copies):
    n = len(bufs)

    def body(*refs):
        send, recv = refs[n], refs[n + 1]
        token = refs[2 * n + 2]
        for k, (src, dst, peer, _) in enumerate(plan(refs[:n])):
            pltpu.make_async_remote_copy(
                src_ref=src, dst_ref=dst, send_sem=send.at[k], recv_sem=recv.at[k],
                device_id=peer, device_id_type=MESH).start()
        token[...] = jnp.zeros(token.shape, F32)

    outs = pl.pallas_call(
        body, name=name,
        out_shape=(pltpu.SemaphoreType.DMA((n_copies,)), pltpu.SemaphoreType.DMA((n_copies,)),
                   *[pltpu.HBM(b.shape, b.dtype) for b in bufs], jax.ShapeDtypeStruct((8, LANES), F32)),
        in_specs=[HBM_SPEC] * n,
        out_specs=(SEM_SPEC, SEM_SPEC, *([HBM_SPEC] * n), pl.BlockSpec(memory_space=pltpu.VMEM)),
        input_output_aliases={a: a + 2 for a in range(n)},
        compiler_params=pltpu.CompilerParams(has_side_effects=EFFECT),
    )(*[_in_hbm(b) for b in bufs])
    return outs[0], outs[1], list(outs[2:2 + n]), outs[2 + n]


def copies_wait(name, bufs, send, recv, plan, sem_base, after):
    n = len(bufs)

    def body(*refs):
        send_ref, recv_ref = refs[n], refs[n + 1]
        for k, (src, _, peer, land) in enumerate(plan(refs[:n])):
            cp = pltpu.make_async_remote_copy(
                src_ref=src, dst_ref=land, send_sem=send_ref.at[sem_base + k], recv_sem=recv_ref.at[sem_base + k],
                device_id=peer, device_id_type=MESH)
            cp.wait_send()
            cp.wait_recv()

    outs = pl.pallas_call(
        body, name=name,
        out_shape=tuple(pltpu.HBM(b.shape, b.dtype) for b in bufs),
        in_specs=[HBM_SPEC] * n + [SEM_SPEC, SEM_SPEC, ANY_SPEC], out_specs=tuple([HBM_SPEC] * n),
        input_output_aliases={a: a for a in range(n)},
        compiler_params=pltpu.CompilerParams(has_side_effects=EFFECT),
    )(*bufs, send, recv, after)
    return list(outs)


def gather_plan(refs):
    x, y, c = _place()
    me = 2 * x + y
    return [(buf.at[me], buf.at[me], (cx, cy, c), buf.at[2 * cx + cy])
            for buf in refs for cx, cy in _other_chips(x, y)]


def all_plan(refs):
    x, y, c = _place()
    me = 4 * x + 2 * y + c
    out = []
    for buf in refs:
        for flip in range(1, 8):
            px = 1 - x if flip & 4 else x
            py = 1 - y if flip & 2 else y
            pc = 1 - c if flip & 1 else c
            out.append((buf.at[me], buf.at[me], (px, py, pc), buf.at[4 * px + 2 * py + pc]))
    return out


def swap_plan(refs):
    x, y, c = _place()
    n = len(refs) // 2
    out = []
    for g, land in zip(refs[:n], refs[n:]):
        hr = g.shape[1] // 2
        out.append((g.at[:, pl.ds((1 - c) * hr, hr)], land, (x, y, 1 - c), land))
    return out


def owners_plan(refs):
    x, y, c = _place()
    n = len(refs) // 2
    return [(src.at[2 * cx + cy], land.at[j], (cx, cy, c), land.at[j])
            for src, land in zip(refs[:n], refs[n:]) for j, (cx, cy) in enumerate(_other_chips(x, y))]


def join_plan(refs):
    x, y, c = _place()
    out = []
    for buf in refs:
        hr = buf.shape[0] // 2
        mine = buf.at[pl.ds(c * hr, hr)]
        out.append((mine, mine, (x, y, 1 - c), buf.at[pl.ds((1 - c) * hr, hr)]))
    return out


def add_halves(name, grad, landed, c_idx):
    _, r, c = grad.shape
    hr = r // 2
    tr = _ew_rows(hr)
    nt = hr // tr

    def body(c_ref, g_ref, l_ref, o_ref, ob_ref):
        tot = g_ref[...] + l_ref[...]
        o_ref[...] = tot
        ob_ref[...] = tot.astype(BF16)

    blk = pl.BlockSpec((None, tr, c), lambda sh, i, c_ref: (sh, i, 0))
    grid_spec = pltpu.PrefetchScalarGridSpec(
        num_scalar_prefetch=1, grid=(N_SHARD, nt),
        in_specs=[pl.BlockSpec((None, tr, c), lambda sh, i, c_ref: (sh, c_ref[0] * nt + i, 0)), blk],
        out_specs=[blk, blk])
    return _call(
        body, name=name, grid_spec=grid_spec,
        out_shape=[jax.ShapeDtypeStruct((N_SHARD, hr, c), F32), jax.ShapeDtypeStruct((N_SHARD, hr, c), BF16)],
        compiler_params=_cp(),
    )(c_idx, grad, landed)


def add_owned(name, own, landed, sc_idx):
    _, hr, c = own.shape
    tr = _ew_rows(hr)
    nt = hr // tr

    def body(sc_ref, o_ref, l0, l1, l2, out_ref):
        out_ref[...] = ((o_ref[...] + l0[...].astype(F32)) + l1[...].astype(F32)) + l2[...].astype(F32)

    grid_spec = pltpu.PrefetchScalarGridSpec(
        num_scalar_prefetch=1, grid=(nt,),
        in_specs=[pl.BlockSpec((None, tr, c), lambda i, sc_ref: (sc_ref[0], i, 0))]
        + [pl.BlockSpec((None, tr, c), lambda i, sc_ref, j=j: (j, i, 0)) for j in range(3)],
        out_specs=pl.BlockSpec((tr, c), lambda i, sc_ref: (sc_ref[1] * nt + i, 0)))
    return _call(
        body, name=name, grid_spec=grid_spec, out_shape=jax.ShapeDtypeStruct((2 * hr, c), F32),
        compiler_params=_cp(),
    )(sc_idx, own, landed, landed, landed)


PACK_QUANTUM = 8 * LANES


def _pack(arrays):
    pieces = []
    for a in arrays:
        flat = a.reshape(-1)
        padded = -(-flat.shape[0] // PACK_QUANTUM) * PACK_QUANTUM
        pieces.append(jnp.pad(flat, (0, padded - flat.shape[0])).reshape(-1, LANES))
    return jnp.concatenate(pieces, axis=0)


def _unpack(packed, shapes):
    out = []
    row = 0
    for shp in shapes:
        size = math.prod(shp)
        rows = -(-size // PACK_QUANTUM) * 8
        out.append(packed[row:row + rows].reshape(-1)[:size].reshape(shp))
        row += rows
    return out


def kernel(x, p, mix_w_in, pool_w, pool_scale, conv_dw_w, conv_dw_b, conv_ln_g, conv_ln_b, mix_w_out, attn_w_qkv, attn_rel_bias, attn_w_o, ln_mix_g, ln_mix_b, ffn_w_up, ffn_dw_w, ffn_dw_b, ffn_w_down, ple_w_proj, ple_w_gate, ple_b_gate, ln_ffn_g, ln_ffn_b, loss_target, m_mix_w_in, m_pool_w, m_pool_scale, m_conv_dw_w, m_conv_dw_b, m_conv_ln_g, m_conv_ln_b, m_mix_w_out, m_attn_w_qkv, m_attn_rel_bias, m_attn_w_o, m_ln_mix_g, m_ln_mix_b, m_ffn_w_up, m_ffn_dw_w, m_ffn_dw_b, m_ffn_w_down, m_ple_w_proj, m_ple_w_gate, m_ple_b_gate, m_ln_ffn_g, m_ln_ffn_b, v_mix_w_in, v_pool_w, v_pool_scale, v_conv_dw_w, v_conv_dw_b, v_conv_ln_g, v_conv_ln_b, v_mix_w_out, v_attn_w_qkv, v_attn_rel_bias, v_attn_w_o, v_ln_mix_g, v_ln_mix_b, v_ffn_w_up, v_ffn_dw_w, v_ffn_dw_b, v_ffn_w_down, v_ple_w_proj, v_ple_w_gate, v_ple_b_gate, v_ln_ffn_g, v_ln_ffn_b):
    xi, yi, ci = _place()
    shard_idx = (2 * xi + yi).astype(jnp.int32)
    s_arr = shard_idx.reshape(1)
    c_arr = ci.astype(jnp.int32).reshape(1)
    sc_arr = jnp.concatenate([s_arr, c_arr])

    x0 = x[0]
    target = loss_target[0]
    p_rows = p.reshape(p.shape[0] * p.shape[2], p.shape[3])
    seq = x0.shape[0]

    big = [
        ("mix_w_in", mix_w_in, m_mix_w_in, v_mix_w_in, True),
        ("mix_w_out", mix_w_out, m_mix_w_out, v_mix_w_out, False),
        ("attn_w_qkv", attn_w_qkv, m_attn_w_qkv, v_attn_w_qkv, True),
        ("attn_w_o", attn_w_o, m_attn_w_o, v_attn_w_o, False),
        ("ffn_w_up", ffn_w_up, m_ffn_w_up, v_ffn_w_up, True),
        ("ffn_w_down", ffn_w_down, m_ffn_w_down, v_ffn_w_down, False),
        ("ple_w_proj", ple_w_proj, m_ple_w_proj, v_ple_w_proj, True),
        ("ple_w_gate", ple_w_gate, m_ple_w_gate, v_ple_w_gate, False),
    ]
    params = {nm: w for nm, w, _, _, _ in big}
    col_sharded = {nm: cs for nm, _, _, _, cs in big}
    keys = [("mix_w_in", 0), ("mix_w_out", 0), ("ffn_w_up", 0), ("ffn_w_down", 0), ("ple_w_gate", 0),
            ("ple_w_proj", 0), ("attn_w_qkv", 0), ("attn_w_o", 0), ("ffn_w_up", 1), ("ffn_w_down", 1),
            ("ple_w_gate", 1), ("ple_w_proj", 1)]
    dw_shapes = [conv_dw_w.shape, ffn_dw_w.shape]
    dw_block = cast_into_gathered("place_dw", _pack([conv_dw_w, ffn_dw_w])[None], 0, s_arr, dtype=F32)
    n_first = 2
    started = {}
    for tag, group in (("first", keys[:n_first]), ("rest", keys[n_first:])):
        shards = [cast_into_gathered(f"cast_{nm}_{layer}", params[nm], layer, s_arr) for nm, layer in group]
        if tag == "first":
            shards.append(dw_block)
        send, recv, bufs, _ = copies_start(f"gather_start_{tag}", shards, gather_plan, 3 * len(shards))
        for a, key in enumerate(group):
            started[key] = (send, recv, bufs[a], 3 * a)
        if tag == "first":
            dw_started = (send, recv, bufs[-1], 3 * len(group))
    arrived_w = {}

    def weight(nm, layer, after=None):
        key = (nm, layer)
        if key not in arrived_w:
            send, recv, buf, base = started[key]
            arrived_w[key] = copies_wait(f"gather_wait_{nm}_{layer}", [buf], send, recv, gather_plan, base, after)[0]
        g = arrived_w[key]
        if col_sharded[nm]:
            return g
        return g.reshape(g.shape[0] * g.shape[1], g.shape[2])

    def tie(a, token):
        return a + token[0:1, 0:1].astype(a.dtype)

    class Reducer:
        def __init__(self, tag, group):
            self.tag, self.group, self.stage = tag, group, 0
            self.n = len(group)
            self.result = None

        def advance(self, after):
            tag, n = self.tag, self.n
            if self.stage == 0:
                grads = []
                for key in self.group:
                    g = big_grads[key]
                    grads.append(g if g.ndim == 3 else g.reshape(N_SHARD, g.shape[0] // N_SHARD, g.shape[1]))
                lands = [lax.empty((N_SHARD, g.shape[1] // 2, g.shape[2]), F32) for g in grads]
                self.sems = copies_start(f"swap_start_{tag}", grads + lands, swap_plan, n)
            elif self.stage == 1:
                send, recv, bufs, _ = self.sems
                outs = copies_wait(f"swap_wait_{tag}", bufs, send, recv, swap_plan, 0, after)
                self.own, wire = [], []
                for key, g, ld in zip(self.group, outs[:n], outs[n:]):
                    o, ob = add_halves(f"add_halves_{key[0]}_{key[1]}", g, ld, c_arr)
                    self.own.append(o)
                    wire.append(ob)
                lands = [lax.empty((3,) + w.shape[1:], BF16) for w in wire]
                self.sems = copies_start(f"owners_start_{tag}", wire + lands, owners_plan, 3 * n)
            elif self.stage == 2:
                send, recv, bufs, _ = self.sems
                outs = copies_wait(f"owners_wait_{tag}", bufs, send, recv, owners_plan, 0, after)
                finals = [add_owned(f"add_owned_{key[0]}_{key[1]}", o, ar, sc_arr)
                          for key, o, ar in zip(self.group, self.own, outs[n:])]
                self.sems = copies_start(f"join_start_{tag}", finals, join_plan, n)
            elif self.stage == 3:
                send, recv, bufs, _ = self.sems
                outs = copies_wait(f"join_wait_{tag}", bufs, send, recv, join_plan, 0, after)
                self.result = dict(zip(self.group, outs))
                self.sems = None
            self.stage += 1
            return None if self.sems is None else self.sems[3]

    dw_cache = []

    def conv_weights(after):
        if not dw_cache:
            send, recv, buf, base = dw_started
            dw_all = copies_wait("gather_wait_dw", [buf], send, recv, gather_plan, base, after)[0]
            dw_parts = [_unpack(dw_all[k], dw_shapes) for k in range(N_SHARD)]
            dw_cache.append(jnp.concatenate([pc[0] for pc in dw_parts], axis=2)[0])
            dw_cache.append(jnp.concatenate([pc[1] for pc in dw_parts], axis=2))
        return dw_cache

    big_grads = {}
    small_grads = {}

    saved = []
    h_in = x0
    for layer in range(N_LAYERS):
        sv = {"x_in": h_in}
        if layer % 2 == 0:
            u = mm_cols_fwd("mix_in", h_in, weight("mix_w_in", 0, h_in), F32)
            conv_w_full, ffn_dw_full = conv_weights(u)
            cat, d_sv, e_sv, glu_sv, hh_sv, rs_sv = mixer_fwd(
                "mixer_fwd", u, pool_w[0], pool_scale, conv_w_full, conv_dw_b, conv_ln_g, conv_ln_b)
            mix = mm_rows_fwd("mix_out", cat, weight("mix_w_out", 0, cat))
            sv.update(u=u, cat=cat, d=d_sv, e=e_sv, glu=glu_sv, hh=hh_sv, rs=rs_sv)
        else:
            qkvp = mm_cols_fwd("attn_qkv", h_in, weight("attn_w_qkv", 0, h_in), BF16,
                               pad_blocks=PAD_ROWS // _row_tile(seq))
            bias = _band_masked(_toeplitz_bias(attn_rel_bias[0]))
            att = attn_fwd("attn_fwd", qkvp, bias)
            mix = mm_rows_fwd("attn_out", att, weight("attn_w_o", 0, att))
            sv.update(qkvp=qkvp, bias=bias, att=att)
        x1, xh1, rs1 = ln_fwd(f"ln_mix_{layer}", h_in, mix, ln_mix_g[layer:layer + 1], ln_mix_b[layer:layer + 1])
        gv = mm_cols_fwd(f"ffn_up_{layer}", x1, weight("ffn_w_up", layer, x1), F32)
        hid = ffn_act_fwd(f"ffn_act_{layer}", gv, ffn_dw_full[layer], ffn_dw_b[layer:layer + 1])
        ffn = mm_rows_fwd(f"ffn_down_{layer}", hid, weight("ffn_w_down", layer, hid))
        pgl = mm_rows_fwd(f"ple_gate_{layer}", x1, weight("ple_w_gate", layer, ffn))
        pp = mm_cols_fwd(f"ple_proj_{layer}", p_rows, weight("ple_w_proj", layer, pgl), F32, part=(layer, N_LAYERS))
        bg = ple_b_gate[layer:layer + 1]
        x2, xh2, rs2 = ln_fwd(f"ln_ffn_{layer}", x1, ffn, ln_ffn_g[layer:layer + 1], ln_ffn_b[layer:layer + 1],
                              ple=(pgl, pp, bg), emit_y=layer < N_LAYERS - 1)
        sv.update(x1=x1, xh1=xh1, rs1=rs1, gv=gv, hid=hid, pgl=pgl, pp=pp, xh2=xh2, rs2=rs2)
        saved.append(sv)
        h_in = x2

    reducers = []

    def open_group(tag, group):
        reducers.append(Reducer(tag, group))
        return reducers[-1].advance(None)

    def hook(after):
        token = None
        for red in reducers:
            if red.stage < 4:
                tk = red.advance(after)
                if tk is not None:
                    token = tk if token is None else token + tk
        return token

    def tied(a, token):
        return a if token is None else tie(a, token)

    parts = []
    token = None
    for layer in reversed(range(N_LAYERS)):
        sv = saved[layer]
        bg = ple_b_gate[layer:layer + 1]
        if layer == 0:
            token = open_group("layer1", [("attn_w_qkv", 0), ("attn_w_o", 0), ("ffn_w_up", 1), ("ffn_w_down", 1),
                                          ("ple_w_gate", 1), ("ple_w_proj", 1)])
        last = layer == N_LAYERS - 1
        res = ln_bwd(
            f"ln_ffn_bwd_{layer}", parts, sv["xh2"], sv["rs2"], tied(ln_ffn_g[layer:layer + 1], token),
            ple=(sv["pgl"], sv["pp"], bg), loss=(target, ln_ffn_b[layer:layer + 1]) if last else None)
        dz2, dg2, db2, dpp, dpgl, dbg = res[:6]
        if last:
            loss_part = res[6]
        small_grads[("ln_ffn_g", layer)] = dg2
        small_grads[("ln_ffn_b", layer)] = db2
        small_grads[("ple_b_gate", layer)] = dbg
        w_down = weight("ffn_w_down", layer)
        dhid = mm_rows_dx(f"ffn_down_dx_{layer}", dz2, w_down)
        big_grads[("ffn_w_down", layer)] = mm_rows_dw(f"ffn_down_dw_{layer}", sv["hid"], dz2)
        token = hook(big_grads[("ffn_w_down", layer)])
        dgv, ddw, ddb = ffn_act_bwd(f"ffn_act_bwd_{layer}", dhid, sv["gv"], ffn_dw_full[layer],
                                    tied(ffn_dw_b[layer:layer + 1], token))
        small_grads[("ffn_dw_w", layer)] = ddw
        small_grads[("ffn_dw_b", layer)] = ddb
        big_grads[("ffn_w_up", layer)] = mm_cols_dw(f"ffn_up_dw_{layer}", sv["x1"], dgv)
        t_up = mm_cols_dx(f"ffn_up_dx_{layer}", dgv, weight("ffn_w_up", layer))
        token = hook(t_up)
        big_grads[("ple_w_gate", layer)] = mm_rows_dw(f"ple_gate_dw_{layer}", sv["x1"], dpgl)
        t_gate = mm_rows_dx(f"ple_gate_dx_{layer}", dpgl, weight("ple_w_gate", layer))
        big_grads[("ple_w_proj", layer)] = mm_cols_dw(f"ple_proj_dw_{layer}", p_rows, dpp, part=(layer, N_LAYERS))
        token2 = hook(big_grads[("ple_w_proj", layer)])
        if token2 is not None:
            token = token2 if token is None else token + token2
        if layer == 0:
            token3 = open_group("layer0_ffn", [("ffn_w_up", 0), ("ffn_w_down", 0), ("ple_w_gate", 0), ("ple_w_proj", 0)])
            token = token3 if token is None else token + token3
        dz1, dg1, db1 = ln_bwd(
            f"ln_mix_bwd_{layer}", [(ALPHA, dz2), (1.0, t_up), (1.0, t_gate)], sv["xh1"], sv["rs1"],
            tied(ln_mix_g[layer:layer + 1], token))
        small_grads[("ln_mix_g", layer)] = dg1
        small_grads[("ln_mix_b", layer)] = db1
        if layer % 2 == 0:
            dcat = mm_rows_dx("mix_out_dx", dz1, weight("mix_w_out", 0))
            big_grads[("mix_w_out", 0)] = mm_rows_dw("mix_out_dw", sv["cat"], dz1)
            token = hook(big_grads[("mix_w_out", 0)])
            du, dpw, dps, dcw, dcb, dcg, dcbt = mixer_bwd(
                "mixer_bwd", dcat, sv["u"], sv["d"], sv["e"], sv["glu"], sv["hh"], sv["rs"],
                pool_w[0], pool_scale, conv_w_full, tied(conv_ln_g, token), conv_ln_b)
            small_grads[("pool_w", 0)] = dpw
            small_grads[("pool_scale", 0)] = dps
            small_grads[("conv_dw_w", 0)] = dcw
            small_grads[("conv_dw_b", 0)] = dcb
            small_grads[("conv_ln_g", 0)] = dcg
            small_grads[("conv_ln_b", 0)] = dcbt
            big_grads[("mix_w_in", 0)] = mm_cols_dw("mix_in_dw", sv["x_in"], du)
            hook(big_grads[("mix_w_in", 0)])
            open_group("layer0_mix", [("mix_w_in", 0), ("mix_w_out", 0)])
            dx_in = mm_cols_dx("mix_in_dx", du, weight("mix_w_in", 0), addend=(ALPHA, dz1))
            token = hook(dx_in)
        else:
            do = mm_rows_dx("attn_out_dx", dz1, weight("attn_w_o", 0), out_dtype=BF16)
            big_grads[("attn_w_o", 0)] = mm_rows_dw("attn_out_dw", sv["att"], dz1)
            dq, dk, dv, ds_sum = attn_bwd("attn_bwd", sv["qkvp"], sv["bias"], do)
            cols, sat = bias_grad_reduce("bias_grad", _shear_for_bias_grad(ds_sum))
            d_rel = jnp.concatenate(
                [jnp.zeros((N_HEADS, 1), F32),
                 jnp.flip(cols[:, 0, SHEAR_SAT + 1:SHEAR_W], axis=1),
                 sat[:, 0, 0:1]], axis=1)
            small_grads[("attn_rel_bias", 0)] = d_rel
            dqkv = jnp.concatenate([dq, dk, dv], axis=1)
            big_grads[("attn_w_qkv", 0)] = mm_cols_dw("attn_qkv_dw", sv["x_in"], dqkv)
            dx_in = mm_cols_dx("attn_qkv_dx", dqkv, weight("attn_w_qkv", 0), addend=(ALPHA, dz1))
        parts = [(1.0, dx_in)]
    grad_x = dx_in

    small = [
        ("pool_w", pool_w, m_pool_w, v_pool_w, None),
        ("pool_scale", pool_scale, m_pool_scale, v_pool_scale, None),
        ("conv_dw_w", conv_dw_w, m_conv_dw_w, v_conv_dw_w, 2),
        ("conv_dw_b", conv_dw_b, m_conv_dw_b, v_conv_dw_b, None),
        ("conv_ln_g", conv_ln_g, m_conv_ln_g, v_conv_ln_g, None),
        ("conv_ln_b", conv_ln_b, m_conv_ln_b, v_conv_ln_b, None),
        ("attn_rel_bias", attn_rel_bias, m_attn_rel_bias, v_attn_rel_bias, None),
        ("ln_mix_g", ln_mix_g, m_ln_mix_g, v_ln_mix_g, None),
        ("ln_mix_b", ln_mix_b, m_ln_mix_b, v_ln_mix_b, None),
        ("ffn_dw_w", ffn_dw_w, m_ffn_dw_w, v_ffn_dw_w, 2),
        ("ffn_dw_b", ffn_dw_b, m_ffn_dw_b, v_ffn_dw_b, None),
        ("ple_b_gate", ple_b_gate, m_ple_b_gate, v_ple_b_gate, None),
        ("ln_ffn_g", ln_ffn_g, m_ln_ffn_g, v_ln_ffn_g, None),
        ("ln_ffn_b", ln_ffn_b, m_ln_ffn_b, v_ln_ffn_b, None),
    ]
    full_grads = []
    for nm, w, _, _, shard_axis in small:
        full = list(w.shape)
        if shard_axis is not None:
            full[shard_axis] *= N_SHARD
        per_layer = [small_grads[(nm, layer)].reshape((1,) + tuple(full[1:])) for layer in range(w.shape[0])]
        full_grads.append(jnp.concatenate(per_layer, axis=0))
    packed = _pack(full_grads + [loss_part])
    dev_arr = (4 * xi + 2 * yi + ci).astype(jnp.int32).reshape(1)
    sg_block = cast_into_gathered("place_small_grads", packed[None], 0, dev_arr, n_blocks=8, dtype=F32)
    sg_send, sg_recv, sg_bufs, _ = copies_start("small_grads_start", [sg_block], all_plan, 7)

    shard_grads = {}
    for red in reducers:
        if red.stage == 4:
            shard_grads.update(red.result)
    big_out = {}

    def update_big(names, tok):
        for nm, w, m, v, _ in big:
            if nm in names:
                gl = [shard_grads[(nm, layer)] for layer in range(w.shape[0])]
                delta, new_m, new_v = adamw(f"adamw_{nm}", w, gl, m, v, token=tok)
                big_out[nm] = (jnp.stack(gl, axis=0), delta, new_m, new_v)

    last_group = ("mix_w_in", "mix_w_out")
    update_big([nm for nm, _, _, _, _ in big if nm not in last_group], token)
    token = hook(big_out["ffn_w_up"][1])

    gathered_sg = copies_wait("small_grads_wait", sg_bufs, sg_send, sg_recv, all_plan, 0, big_out["ffn_w_down"][1])[0]
    total = sum_blocks("sum_small", gathered_sg.reshape(8 * packed.shape[0], LANES), 8)
    unpacked = _unpack(total, [g.shape for g in full_grads] + [loss_part.shape])
    loss = unpacked[-1][0, 0]
    local_grads = []
    for (nm, w, _, _, shard_axis), g in zip(small, unpacked[:-1]):
        if shard_axis is not None:
            width = w.shape[shard_axis]
            g = lax.dynamic_slice_in_dim(g, shard_idx * width, width, axis=shard_axis)
        local_grads.append(g.reshape(w.shape))
    shapes = [w.shape for _, w, _, _, _ in small]
    pg = _pack(local_grads)
    pw = _pack([w for _, w, _, _, _ in small])
    pm = _pack([m for _, _, m, _, _ in small])
    pv = _pack([v for _, _, _, v, _ in small])
    delta_s, new_m_s, new_v_s = adamw("adamw_small", pw[None], [pg], pm[None], pv[None], token=token)
    hook(delta_s)
    for red in reducers:
        shard_grads.update(red.result)
    update_big(last_group, None)
    small_out = {}
    for (nm, _, _, _, _), g, d_, m_, v_ in zip(
            small, local_grads, _unpack(delta_s[0], shapes), _unpack(new_m_s[0], shapes), _unpack(new_v_s[0], shapes)):
        small_out[nm] = (g, d_, m_, v_)

    order = ["mix_w_in", "pool_w", "pool_scale", "conv_dw_w", "conv_dw_b", "conv_ln_g", "conv_ln_b", "mix_w_out",
             "attn_w_qkv", "attn_rel_bias", "attn_w_o", "ln_mix_g", "ln_mix_b", "ffn_w_up", "ffn_dw_w", "ffn_dw_b",
             "ffn_w_down", "ple_w_proj", "ple_w_gate", "ple_b_gate", "ln_ffn_g", "ln_ffn_b"]
    res = {**big_out, **small_out}
    outs = [loss, grad_x[None]]
    for slot in range(4):
        outs += [res[nm][slot] for nm in order]
    return tuple(outs)
```

```python
import functools
import math

import jax
import jax.numpy as jnp
from jax import lax
from jax.experimental import pallas as pl
from jax.experimental.pallas import tpu as pltpu

F32 = jnp.float32
BF16 = jnp.bfloat16
MESH = pl.DeviceIdType.MESH

N_LAYERS = 2
ALPHA = (2 * N_LAYERS) ** 0.25
LN_EPS = 1e-5
NEG_INF = -1e30
CHUNK = 64
LEFT_CHUNKS = 8
PAD_ROWS = LEFT_CHUNKS * CHUNK
HEAD_DIM = 64
N_HEADS = 16
MAX_REL = 256
POOL_WINDOWS = (2, 4, 8, 16)
POOL_GROUP = 128
CONV_K = 31
FFN_K = 3
CONV_HALO = 32
FFN_HALO = 8
FFN_TILE = 256
FFN_CHUNK_ROWS = 32
FFN_CHUNK_LANES = 256
Q_TILE = 256
K_WIN = Q_TILE + PAD_ROWS
SHEAR_W = Q_TILE + K_WIN
SHEAR_SAT = SHEAR_W - 2 * MAX_REL
N_SHARD = 4
LANES = 128
SUBLANES = 8

ADAM_LR = 0.001
ADAM_B1 = 0.9
ADAM_B2 = 0.999
ADAM_EPS = 1e-08
ADAM_WD = 0.01
ADAM_STEP = 10
ADAM_BC1 = 1.0 - ADAM_B1 ** ADAM_STEP
ADAM_BC2 = 1.0 - ADAM_B2 ** ADAM_STEP

DIMS = {
    "nn": (((1,), (0,)), ((), ())),
    "nt": (((1,), (1,)), ((), ())),
    "tn": (((0,), (0,)), ((), ())),
}


def _cp(vmem_mb=48, **kw):
    return pltpu.CompilerParams(vmem_limit_bytes=vmem_mb * 1024 * 1024, **kw)


def _in_hbm(a):
    return pltpu.with_memory_space_constraint(a, pltpu.HBM)


def _call(body, **kw):
    return pl.pallas_call(body, **kw)


def _dot(a, b, mode):
    return lax.dot_general(a.astype(BF16), b.astype(BF16), DIMS[mode], preferred_element_type=F32)


def _sig(x):
    return 1.0 / (1.0 + jnp.exp(-x))


def _row_tile(s):
    return min(512, s // 4)


def _mm_tile(s):
    return min(1024, s // 4)


def _mm(name, mode, a, b, in_specs, out_shape, out_spec, acc_shape, grid, nk, zero_first=False, vmem_mb=48,
        addend=None):
    out_f32 = out_shape.dtype == F32

    def body(a_ref, b_ref, *rest):
        k = pl.program_id(2)
        if addend is None:
            o_ref, scr = rest[0], rest[1:]
        else:
            add_ref, o_ref, scr = rest[0], rest[1], rest[2:]

        def compute():
            part = _dot(a_ref[...], b_ref[...], mode)
            if nk == 1:
                if addend is not None:
                    part = part + addend[0] * add_ref[...]
                o_ref[...] = part.astype(o_ref.dtype)
                return
            acc = o_ref if out_f32 else scr[0]

            @pl.when(k == 0)
            def _():
                acc[...] = part if addend is None else part + addend[0] * add_ref[...]

            @pl.when(k > 0)
            def _():
                acc[...] += part

            if not out_f32:
                @pl.when(k == nk - 1)
                def _():
                    o_ref[...] = acc[...].astype(o_ref.dtype)

        if zero_first:
            @pl.when(pl.program_id(1) == 0)
            def _():
                o_ref[...] = jnp.zeros(o_ref.shape, o_ref.dtype)

            pl.when(pl.program_id(1) > 0)(compute)
        else:
            compute()

    scratch = [] if (nk == 1 or out_f32) else [pltpu.VMEM(acc_shape, F32)]
    operands = [a, b] if addend is None else [a, b, addend[1]]
    specs = list(in_specs) if addend is None else list(in_specs) + [out_spec]
    return _call(
        body, name=name, grid=grid, in_specs=specs, out_specs=out_spec, out_shape=out_shape,
        scratch_shapes=scratch, compiler_params=_cp(vmem_mb),
    )(*operands)


def mm_cols_fwd(name, a, wc, out_dtype, pad_blocks=0, part=(0, 1)):
    s, k = a.shape
    s //= part[1]
    n4 = wc.shape[2]
    tm = _row_tile(s) if pad_blocks else _mm_tile(s)
    nt = s // tm
    first_block = part[0] * nt
    return _mm(
        name, "nn", a, wc,
        [pl.BlockSpec((tm, k), lambda j, i, r: (first_block + jnp.maximum(i - pad_blocks, 0), 0)),
         pl.BlockSpec((None, k, n4), lambda j, i, r: (j, 0, 0))],
        jax.ShapeDtypeStruct((s + pad_blocks * tm, N_SHARD * n4), out_dtype),
        pl.BlockSpec((tm, n4), lambda j, i, r: (i, j)),
        None, (N_SHARD, nt + pad_blocks, 1), 1, zero_first=pad_blocks > 0)


def mm_cols_dx(name, dy, wc, addend=None):
    s = dy.shape[0]
    _, k, n4 = wc.shape
    tm = _mm_tile(s)
    return _mm(
        name, "nt", dy, wc,
        [pl.BlockSpec((tm, n4), lambda g, i, r: (i, r)),
         pl.BlockSpec((None, k, n4), lambda g, i, r: (r, 0, 0))],
        jax.ShapeDtypeStruct((s, k), F32),
        pl.BlockSpec((tm, k), lambda g, i, r: (i, 0)),
        (tm, k), (1, s // tm, N_SHARD), N_SHARD, addend=addend)


def mm_cols_dw(name, a, dy, part=(0, 1)):
    s, k = a.shape
    s //= part[1]
    n4 = dy.shape[1] // N_SHARD
    tm = _mm_tile(s)
    nt = s // tm
    first_block = part[0] * nt
    return _mm(
        name, "tn", a, dy,
        [pl.BlockSpec((tm, k), lambda j, g, r: (first_block + r, 0)),
         pl.BlockSpec((tm, n4), lambda j, g, r: (r, j))],
        jax.ShapeDtypeStruct((N_SHARD, k, n4), F32),
        pl.BlockSpec((None, k, n4), lambda j, g, r: (j, 0, 0)),
        (k, n4), (N_SHARD, 1, nt), nt)


def _k_tile(k):
    return k if k <= 1024 else k // 2


def mm_rows_fwd(name, a, wr, out_dtype=F32):
    s, k = a.shape
    n = wr.shape[1]
    tm = _mm_tile(s)
    tk = _k_tile(k)
    nk = k // tk
    return _mm(
        name, "nn", a, wr,
        [pl.BlockSpec((tm, tk), lambda g, i, r: (i, r)),
         pl.BlockSpec((tk, n), lambda g, i, r: (r, 0))],
        jax.ShapeDtypeStruct((s, n), out_dtype),
        pl.BlockSpec((tm, n), lambda g, i, r: (i, 0)),
        (tm, n), (1, s // tm, nk), nk)


def mm_rows_dx(name, dy, wr, out_dtype=F32):
    s, n = dy.shape
    k = wr.shape[0]
    tm = _mm_tile(s)
    tk = _k_tile(k)
    return _mm(
        name, "nt", dy, wr,
        [pl.BlockSpec((tm, n), lambda j, i, r: (i, 0)),
         pl.BlockSpec((tk, n), lambda j, i, r: (j, 0))],
        jax.ShapeDtypeStruct((s, k), out_dtype),
        pl.BlockSpec((tm, tk), lambda j, i, r: (i, j)),
        None, (k // tk, s // tm, 1), 1)


def mm_rows_dw(name, a, dy):
    s, k = a.shape
    n = dy.shape[1]
    tm = _mm_tile(s)
    tk = _k_tile(k)
    nt = s // tm
    return _mm(
        name, "tn", a, dy,
        [pl.BlockSpec((tm, tk), lambda j, g, r: (r, j)),
         pl.BlockSpec((tm, n), lambda j, g, r: (r, 0))],
        jax.ShapeDtypeStruct((k, n), F32),
        pl.BlockSpec((tk, n), lambda j, g, r: (j, 0)),
        (tk, n), (k // tk, 1, nt), nt)


def _row(tm, c, col=0):
    return pl.BlockSpec((tm, c), lambda i: (i, col))


def _full(shape):
    nd = len(shape)
    return pl.BlockSpec(shape, lambda i: (0,) * nd)


def _prev(tm, h, c, col=0):
    return pl.BlockSpec((h, c), lambda i: (jnp.maximum(i * (tm // h) - 1, 0), col))


def _next(tm, h, c, s, col=0):
    return pl.BlockSpec((h, c), lambda i: (jnp.minimum((i + 1) * (tm // h), s // h - 1), col))


def _acc_add(ref, first, val):
    @pl.when(first)
    def _():
        ref[...] = val

    @pl.when(jnp.logical_not(first))
    def _():
        ref[...] += val


def _colsum(v):
    return jnp.sum(v, axis=0, keepdims=True)


def _ln_stats(z):
    mu = jnp.mean(z, axis=-1, keepdims=True)
    zc = z - mu
    var = jnp.mean(zc * zc, axis=-1, keepdims=True)
    rstd = lax.rsqrt(var + LN_EPS)
    return zc * rstd, rstd


def _ln_bwd(dxhat, xhat, rstd):
    m1 = jnp.mean(dxhat, axis=-1, keepdims=True)
    m2 = jnp.mean(dxhat * xhat, axis=-1, keepdims=True)
    return rstd * (dxhat - m1 - xhat * m2)


def ln_fwd(name, x, f, g, b, ple=None, emit_y=True):
    s, d = x.shape
    tm = _row_tile(s)
    n_in = 2 + (3 if ple is not None else 0)

    def body(*refs):
        x_ref, f_ref = refs[0], refs[1]
        g_ref, b_ref = refs[n_in], refs[n_in + 1]
        xh_ref, rs_ref = refs[-2:]
        z = ALPHA * x_ref[...] + f_ref[...]
        if ple is not None:
            pgl_ref, pp_ref, bg_ref = refs[2:5]
            z = z + _sig(pgl_ref[...] + bg_ref[...]) * pp_ref[...]
        xhat, rstd = _ln_stats(z)
        if emit_y:
            refs[n_in + 2][...] = xhat * g_ref[...] + b_ref[...]
        xh_ref[...] = xhat
        rs_ref[...] = jnp.broadcast_to(rstd, rs_ref.shape)

    ins = [x, f]
    specs = [_row(tm, d), _row(tm, d)]
    if ple is not None:
        pgl, pp, bg = ple
        ins += [pgl, pp, bg]
        specs += [_row(tm, d), _row(tm, d), _full((1, d))]
    ins += [g, b]
    specs += [_full((1, d)), _full((1, d))]
    n_y = 1 if emit_y else 0
    outs = _call(
        body, name=name, grid=(s // tm,), in_specs=specs,
        out_specs=[_row(tm, d)] * (n_y + 1) + [_row(tm, LANES)],
        out_shape=[jax.ShapeDtypeStruct((s, d), F32)] * (n_y + 1) + [jax.ShapeDtypeStruct((s, LANES), F32)],
        compiler_params=_cp(),
    )(*ins)
    return (outs[0], outs[1], outs[2]) if emit_y else (None, outs[0], outs[1])


def ln_bwd(name, parts, xhat, rstd, g, ple=None, loss=None):
    s, d = xhat.shape
    tm = _row_tile(s)
    coefs = [c for c, _ in parts]
    n_p = len(parts)
    n_ple = 3 if ple is not None else 0
    n_in = n_p + 3 + n_ple + (2 if loss is not None else 0)

    def body(*refs):
        first = pl.program_id(0) == 0
        xh = refs[n_p][...]
        rs = refs[n_p + 1][:, 0:1]
        g_v = refs[n_p + 2][...]
        outs = refs[n_in:]
        if loss is not None:
            t_ref, b_ref = refs[n_p + 3 + n_ple:n_p + 5 + n_ple]
            err = (xh * g_v + b_ref[...]) - t_ref[...]
            dy = err * (1.0 / d)
            part = 0.5 * jnp.sum(jnp.mean(err * err, axis=-1, keepdims=True), axis=0, keepdims=True)
            _acc_add(outs[-1], first, jnp.broadcast_to(part, outs[-1].shape))
        else:
            dy = coefs[0] * refs[0][...].astype(F32)
            for j in range(1, n_p):
                dy = dy + coefs[j] * refs[j][...].astype(F32)
        dz = _ln_bwd(dy * g_v, xh, rs)
        outs[0][...] = dz
        _acc_add(outs[1], first, _colsum(dy * xh))
        _acc_add(outs[2], first, _colsum(dy))
        if ple is not None:
            pgl_ref, pp_ref, bg_ref = refs[n_p + 3:n_p + 6]
            pg = _sig(pgl_ref[...] + bg_ref[...])
            dpgl = dz * pp_ref[...] * pg * (1.0 - pg)
            outs[3][...] = (dz * pg).astype(BF16)
            outs[4][...] = dpgl.astype(BF16)
            _acc_add(outs[5], first, _colsum(dpgl))

    ins = [p for _, p in parts] + [xhat, rstd, g]
    specs = [_row(tm, d)] * n_p + [_row(tm, d), _row(tm, LANES), _full((1, d))]
    out_specs = [_row(tm, d), _full((1, d)), _full((1, d))]
    out_shape = [jax.ShapeDtypeStruct((s, d), F32), jax.ShapeDtypeStruct((1, d), F32),
                 jax.ShapeDtypeStruct((1, d), F32)]
    if ple is not None:
        pgl, pp, bg = ple
        ins += [pgl, pp, bg]
        specs += [_row(tm, d), _row(tm, d), _full((1, d))]
        out_specs += [_row(tm, d), _row(tm, d), _full((1, d))]
        out_shape += [jax.ShapeDtypeStruct((s, d), BF16), jax.ShapeDtypeStruct((s, d), BF16),
                      jax.ShapeDtypeStruct((1, d), F32)]
    if loss is not None:
        target, b = loss
        ins += [target, b]
        specs += [_row(tm, d), _full((1, d))]
        out_specs += [_full((8, LANES))]
        out_shape += [jax.ShapeDtypeStruct((8, LANES), F32)]
    return _call(
        body, name=name, grid=(s // tm,), in_specs=specs, out_specs=out_specs, out_shape=out_shape,
        compiler_params=_cp(),
    )(*ins)


def _fill_rotations(rot_ref, x, direction):
    n = x.shape[0]
    rot_ref[0] = x
    for b in range(1, SUBLANES):
        if direction < 0:
            rot_ref[b, SUBLANES:n, :] = x[SUBLANES - b:n - b]
        else:
            rot_ref[b, 0:n - SUBLANES, :] = x[b:n - SUBLANES + b]


def _rotated(rot_ref, start, rows, cs, direction=-1):
    b = (-start) % SUBLANES if direction < 0 else start % SUBLANES
    aligned = start + b if direction < 0 else start - b
    return rot_ref[b, pl.ds(aligned, rows), cs]


def _tile_pos(i, tm, rows):
    return (i * tm + lax.broadcasted_iota(jnp.int32, (rows, 1), 0) + 1).astype(F32)


def mixer_fwd(name, u, pool_w, pool_scale, conv_w, conv_b, cn_g, cn_b):
    s = u.shape[0]
    dp = 512
    tm = min(256, s // 4)
    h = CONV_HALO

    def body(a_c, a_p, bv_c, bv_p, bg_c, bg_p, pw_ref, ps_ref, cw_ref, cb_ref, cg_ref, cbt_ref,
             cat_ref, d_ref, e_ref, glu_ref, hh_ref, rs_ref, ext_a, rot_g, conv_out):
        i = pl.program_id(0)
        first = i == 0
        ext_a[0:h, :] = jnp.where(first, 0.0, a_p[...])
        ext_a[h:, :] = a_c[...]
        glu = bv_c[...] * _sig(bg_c[...])
        glu_ref[...] = glu
        _fill_rotations(rot_g, jnp.concatenate([jnp.where(first, 0.0, bv_p[...] * _sig(bg_p[...])), glu], axis=0), -1)
        pos = _tile_pos(i, tm, tm)
        for gi, w in enumerate(POOL_WINDOWS):
            cs = slice(gi * POOL_GROUP, (gi + 1) * POOL_GROUP)
            a_g = ext_a[pl.ds(h, tm), cs]
            acc = a_g
            for sh in range(1, w):
                acc = acc + ext_a[pl.ds(h - sh, tm), cs]
            d_g = acc / jnp.minimum(pos, float(w)) - a_g
            d_ref[:, cs] = d_g.astype(BF16)
            e_g = _dot(d_g, pw_ref[gi], "nn")
            e_ref[:, cs] = e_g
            cat_ref[:, cs] = (e_g * ps_ref[:, cs]).astype(BF16)
        for lg in range(dp // LANES):
            cs = slice(lg * LANES, (lg + 1) * LANES)
            acc = jnp.broadcast_to(cb_ref[:, cs], (tm, LANES))
            for sh in range(CONV_K):
                acc = acc + _rotated(rot_g, h - sh, tm, cs) * cw_ref[pl.ds(CONV_K - 1 - sh, 1), cs]
            conv_out[:, cs] = acc
        hhat, rstd = _ln_stats(conv_out[...])
        hl = hhat * cg_ref[...] + cbt_ref[...]
        cat_ref[:, dp:] = (hl * _sig(hl)).astype(BF16)
        hh_ref[...] = hhat
        rs_ref[...] = jnp.broadcast_to(rstd, rs_ref.shape)

    specs = [_row(tm, dp, 0), _prev(tm, h, dp, 0), _row(tm, dp, 1), _prev(tm, h, dp, 1),
             _row(tm, dp, 2), _prev(tm, h, dp, 2),
             _full((4, POOL_GROUP, POOL_GROUP)), _full((1, dp)), _full((CONV_K, dp)),
             _full((1, dp)), _full((1, dp)), _full((1, dp))]
    out_specs = [_row(tm, 2 * dp), _row(tm, dp), _row(tm, dp), _row(tm, dp), _row(tm, dp), _row(tm, LANES)]
    out_shape = [jax.ShapeDtypeStruct((s, 2 * dp), BF16), jax.ShapeDtypeStruct((s, dp), BF16),
                 jax.ShapeDtypeStruct((s, dp), F32), jax.ShapeDtypeStruct((s, dp), F32),
                 jax.ShapeDtypeStruct((s, dp), F32), jax.ShapeDtypeStruct((s, LANES), F32)]
    return _call(
        body, name=name, grid=(s // tm,), in_specs=specs, out_specs=out_specs, out_shape=out_shape,
        scratch_shapes=[pltpu.VMEM((h + tm, dp), F32), pltpu.VMEM((SUBLANES, h + tm, dp), F32),
                        pltpu.VMEM((tm, dp), F32)],
        compiler_params=_cp(),
    )(u, u, u, u, u, u, pool_w, pool_scale, conv_w, conv_b, cn_g, cn_b)


def mixer_bwd(name, dcat, u, d_sv, e_sv, glu_sv, hh_sv, rs_sv, pool_w, pool_scale, conv_w, cn_g, cn_b):
    s = u.shape[0]
    dp = 512
    tm = min(256, s // 4)
    h = CONV_HALO
    nt = s // tm

    def body(dc_c, dc_n, bv_c, bg_c, d_c, e_c, gl_c, gl_p, hh_c, hh_n, rs_c, rs_n,
             pw_ref, ps_ref, cw_ref, cg_ref, cbt_ref,
             du_ref, dpw_ref, dps_ref, dcw_ref, dcb_ref, dcg_ref, dcbt_ref,
             ext_dh, ext_g, ext_r):
        i = pl.program_id(0)
        first = i == 0
        last = i == nt - 1
        cg = cg_ref[...]

        def conv_grads(dyb, hhat, rstd):
            hl = hhat * cg + cbt_ref[...]
            sg = _sig(hl)
            dhl = dyb * (sg * (1.0 + hl * (1.0 - sg)))
            return _ln_bwd(dhl * cg, hhat, rstd), dhl

        hh_cur = hh_c[...]
        dh_c, dhl_c = conv_grads(dc_c[:, dp:], hh_cur, rs_c[:, 0:1])
        dh_n, _ = conv_grads(dc_n[:, dp:], hh_n[...], rs_n[:, 0:1])
        _fill_rotations(ext_dh, jnp.concatenate([dh_c, jnp.where(last, 0.0, dh_n)], axis=0), 1)
        _fill_rotations(ext_g, jnp.concatenate([jnp.where(first, 0.0, gl_p[...]), gl_c[...]], axis=0), -1)

        @pl.when(first)
        def _():
            dcw_ref[...] = jnp.zeros(dcw_ref.shape, F32)

        for lg in range(dp // LANES):
            cs = slice(lg * LANES, (lg + 1) * LANES)
            dglu = jnp.zeros((tm, LANES), F32)
            for sh in range(CONV_K):
                dglu = dglu + _rotated(ext_dh, sh, tm, cs, 1) * cw_ref[pl.ds(CONV_K - 1 - sh, 1), cs]
            dh_g = ext_dh[0, pl.ds(0, tm), cs]
            for sh in range(CONV_K):
                dcw_ref[pl.ds(CONV_K - 1 - sh, 1), cs] += _colsum(dh_g * _rotated(ext_g, h - sh, tm, cs))
            sgate = _sig(bg_c[:, cs])
            du_ref[:, dp + lg * LANES:dp + (lg + 1) * LANES] = dglu * sgate
            du_ref[:, 2 * dp + lg * LANES:2 * dp + (lg + 1) * LANES] = dglu * bv_c[:, cs] * sgate * (1.0 - sgate)
        _acc_add(dcb_ref, first, _colsum(dh_c))
        _acc_add(dcg_ref, first, _colsum(dhl_c * hh_cur))
        _acc_add(dcbt_ref, first, _colsum(dhl_c))

        pos_c = _tile_pos(i, tm, tm)
        pos_n = _tile_pos(i + 1, tm, h)
        _acc_add(dps_ref, first, _colsum(dc_c[:, :dp] * e_c[...]))
        for gi, w in enumerate(POOL_WINDOWS):
            cs = slice(gi * POOL_GROUP, (gi + 1) * POOL_GROUP)
            pw = pw_ref[gi]
            de_c = dc_c[:, cs] * ps_ref[:, cs]
            de_n = dc_n[:, cs] * ps_ref[:, cs]
            dd_c = _dot(de_c, pw, "nt")
            dd_n = _dot(de_n, pw, "nt")
            ext_r[0:tm, :] = dd_c / jnp.minimum(pos_c, float(w))
            ext_r[tm:, :] = jnp.where(last, 0.0, dd_n / jnp.minimum(pos_n, float(w)))
            acc = -dd_c
            for sh in range(w):
                acc = acc + ext_r[pl.ds(sh, tm), :]
            du_ref[:, cs] = acc
            dpw_g = _dot(d_c[:, cs], de_c, "tn")

            @pl.when(first)
            def _():
                dpw_ref[gi] = dpw_g

            @pl.when(jnp.logical_not(first))
            def _():
                dpw_ref[gi] += dpw_g

    specs = [_row(tm, 2 * dp), _next(tm, h, 2 * dp, s), _row(tm, dp, 1), _row(tm, dp, 2),
             _row(tm, dp), _row(tm, dp), _row(tm, dp), _prev(tm, h, dp),
             _row(tm, dp), _next(tm, h, dp, s), _row(tm, LANES), _next(tm, h, LANES, s),
             _full((4, POOL_GROUP, POOL_GROUP)), _full((1, dp)), _full((CONV_K, dp)),
             _full((1, dp)), _full((1, dp))]
    out_specs = [_row(tm, 3 * dp), _full((4, POOL_GROUP, POOL_GROUP)), _full((1, dp)), _full((CONV_K, dp)),
                 _full((1, dp)), _full((1, dp)), _full((1, dp))]
    out_shape = [jax.ShapeDtypeStruct((s, 3 * dp), F32),
                 jax.ShapeDtypeStruct((4, POOL_GROUP, POOL_GROUP), F32), jax.ShapeDtypeStruct((1, dp), F32),
                 jax.ShapeDtypeStruct((CONV_K, dp), F32), jax.ShapeDtypeStruct((1, dp), F32),
                 jax.ShapeDtypeStruct((1, dp), F32), jax.ShapeDtypeStruct((1, dp), F32)]
    return _call(
        body, name=name, grid=(nt,), in_specs=specs, out_specs=out_specs, out_shape=out_shape,
        scratch_shapes=[pltpu.VMEM((SUBLANES, tm + h, dp), F32), pltpu.VMEM((SUBLANES, h + tm, dp), F32),
                        pltpu.VMEM((tm + h, POOL_GROUP), F32)],
        compiler_params=_cp(),
    )(dcat, dcat, u, u, d_sv, e_sv, glu_sv, glu_sv, hh_sv, hh_sv, rs_sv, rs_sv,
      pool_w, pool_scale, conv_w, cn_g, cn_b)


GELU_C = math.sqrt(2.0 / math.pi)


def _gelu_parts(x):
    x2 = x * x
    t = jnp.tanh(x * (GELU_C + (GELU_C * 0.044715) * x2))
    half_1pt = 0.5 + 0.5 * t
    gelu = x * half_1pt
    dgelu = half_1pt + (0.5 * x) * (1.0 - t * t) * (GELU_C + (3.0 * GELU_C * 0.044715) * x2)
    return gelu, dgelu


def ffn_act_fwd(name, gv, dw_w, dw_b):
    s = gv.shape[0]
    dff = gv.shape[1] // 2
    tm = min(FFN_TILE, s // 4)
    h = FFN_HALO
    rc = FFN_CHUNK_ROWS
    lw = FFN_CHUNK_LANES

    def body(g_c, g_p, v_c, w_ref, b_ref, hid_ref):
        first = pl.program_id(0) == 0

        def chunk(ci, carry):
            r0 = pl.multiple_of(ci * rc, rc)
            above = pl.multiple_of(jnp.maximum(r0 - h, 0), h)
            for lg in range(dff // lw):
                cs = slice(lg * lw, (lg + 1) * lw)
                top = jnp.where(ci == 0, jnp.where(first, 0.0, g_p[:, cs]), g_c[pl.ds(above, h), cs])
                win = jnp.concatenate([top, g_c[pl.ds(r0, rc), cs]], axis=0)
                gc = jnp.broadcast_to(b_ref[:, cs], (rc, lw))
                for sh in range(FFN_K):
                    gc = gc + win[h - sh:h - sh + rc] * w_ref[pl.ds(FFN_K - 1 - sh, 1), cs]
                gelu, _ = _gelu_parts(gc)
                hid_ref[pl.ds(r0, rc), cs] = (gelu * v_c[pl.ds(r0, rc), cs]).astype(BF16)
            return carry

        lax.fori_loop(0, tm // rc, chunk, 0)

    return _call(
        body, name=name, grid=(s // tm,),
        in_specs=[_row(tm, dff, 0), _prev(tm, h, dff, 0), _row(tm, dff, 1), _full((FFN_K, dff)), _full((1, dff))],
        out_specs=_row(tm, dff), out_shape=jax.ShapeDtypeStruct((s, dff), BF16),
        compiler_params=_cp(),
    )(gv, gv, gv, dw_w, dw_b)


def ffn_act_bwd(name, dhid, gv, dw_w, dw_b):
    s = gv.shape[0]
    dff = gv.shape[1] // 2
    tm = min(FFN_TILE, s // 4)
    h = FFN_HALO
    nt = s // tm
    rc = FFN_CHUNK_ROWS
    lw = FFN_CHUNK_LANES
    n_chunks = tm // rc

    def body(dh_c, dh_n, g_p, g_c, g_n, v_c, v_n, w_ref, b_ref, dgv_ref, dw_ref, db_ref):
        i = pl.program_id(0)
        first = i == 0
        last = i == nt - 1

        @pl.when(first)
        def _():
            dw_ref[...] = jnp.zeros(dw_ref.shape, F32)
            db_ref[...] = jnp.zeros(db_ref.shape, F32)

        def chunk(ci, carry):
            r0 = pl.multiple_of(ci * rc, rc)
            above = pl.multiple_of(jnp.maximum(r0 - h, 0), h)
            below = pl.multiple_of(jnp.minimum(r0 + rc, tm - h), h)
            at_end = ci == n_chunks - 1
            for lg in range(dff // lw):
                cs = slice(lg * lw, (lg + 1) * lw)
                top = jnp.where(ci == 0, jnp.where(first, 0.0, g_p[:, cs]), g_c[pl.ds(above, h), cs])
                bot = jnp.where(at_end, g_n[:, cs], g_c[pl.ds(below, h), cs])
                win = jnp.concatenate([top, g_c[pl.ds(r0, rc), cs], bot], axis=0)
                shifted = [win[h - sh:h - sh + rc + h] for sh in range(FFN_K)]
                gc = jnp.broadcast_to(b_ref[:, cs], (rc + h, lw))
                for sh in range(FFN_K):
                    gc = gc + shifted[sh] * w_ref[pl.ds(FFN_K - 1 - sh, 1), cs]
                gelu, dgelu = _gelu_parts(gc)
                dh_mid = dh_c[pl.ds(r0, rc), cs]
                hv_bot = jnp.where(at_end, jnp.where(last, 0.0, dh_n[:, cs] * v_n[:, cs]),
                                   dh_c[pl.ds(below, h), cs] * v_c[pl.ds(below, h), cs])
                dgc = jnp.concatenate([dh_mid * v_c[pl.ds(r0, rc), cs], hv_bot], axis=0) * dgelu
                dgate = jnp.zeros((rc, lw), F32)
                for sh in range(FFN_K):
                    dgate = dgate + dgc[sh:sh + rc] * w_ref[pl.ds(FFN_K - 1 - sh, 1), cs]
                dgv_ref[pl.ds(r0, rc), cs] = dgate.astype(BF16)
                dgv_ref[pl.ds(r0, rc), slice(dff + lg * lw, dff + (lg + 1) * lw)] = (dh_mid * gelu[0:rc]).astype(BF16)
                dgc_mid = dgc[0:rc]
                for sh in range(FFN_K):
                    dw_ref[pl.ds(FFN_K - 1 - sh, 1), cs] += _colsum(dgc_mid * shifted[sh][0:rc])
                db_ref[:, cs] += _colsum(dgc_mid)
            return carry

        lax.fori_loop(0, n_chunks, chunk, 0)

    return _call(
        body, name=name, grid=(nt,),
        in_specs=[_row(tm, dff), _next(tm, h, dff, s),
                  _prev(tm, h, dff, 0), _row(tm, dff, 0), _next(tm, h, dff, s, 0),
                  _row(tm, dff, 1), _next(tm, h, dff, s, 1),
                  _full((FFN_K, dff)), _full((1, dff))],
        out_specs=[_row(tm, 2 * dff), _full((FFN_K, dff)), _full((1, dff))],
        out_shape=[jax.ShapeDtypeStruct((s, 2 * dff), BF16), jax.ShapeDtypeStruct((FFN_K, dff), F32),
                   jax.ShapeDtypeStruct((1, dff), F32)],
        compiler_params=_cp(),
    )(dhid, dhid, gv, gv, gv, gv, gv, dw_w, dw_b)


def _toeplitz_bias(rel_bias):
    nh = rel_bias.shape[0]
    zero = jnp.zeros((nh, 1), rel_bias.dtype)
    line = jnp.concatenate(
        [zero, jnp.broadcast_to(rel_bias[:, 2 * MAX_REL:], (nh, SHEAR_SAT)),
         jnp.flip(rel_bias[:, 1:2 * MAX_REL], axis=1), zero], axis=1)
    z = jnp.broadcast_to(line[:, None, :], (nh, Q_TILE, SHEAR_W + 1)).reshape(nh, Q_TILE * (SHEAR_W + 1))
    return z[:, :Q_TILE * SHEAR_W].reshape(nh, Q_TILE, SHEAR_W)[:, :, Q_TILE:]


def _shear_for_bias_grad(ds_sum):
    nh = ds_sum.shape[0]
    z = jnp.pad(ds_sum, ((0, 0), (0, 0), (Q_TILE, 0))).reshape(nh, Q_TILE * SHEAR_W)
    return jnp.pad(z, ((0, 0), (0, Q_TILE))).reshape(nh, Q_TILE, SHEAR_W + 1)


def _band_masked(bias):
    qc = lax.broadcasted_iota(jnp.int32, (Q_TILE, K_WIN), 0) // CHUNK
    kc = lax.broadcasted_iota(jnp.int32, (Q_TILE, K_WIN), 1) // CHUNK
    return jnp.where(((kc >= qc) & (kc <= qc + LEFT_CHUNKS))[None], bias, NEG_INF)


def _stack_heads(x2):
    lane = lax.broadcasted_iota(jnp.int32, x2.shape, 1)
    zero = jnp.zeros_like(x2)
    return jnp.concatenate([jnp.where(lane < HEAD_DIM, x2, zero), jnp.where(lane < HEAD_DIM, zero, x2)], axis=0)


def _unstack_heads(x_st):
    lane = lax.broadcasted_iota(jnp.int32, (Q_TILE, LANES), 1)
    return jnp.where(lane < HEAD_DIM, x_st[:Q_TILE], x_st[Q_TILE:])


def _attn_probs(q_st, k3, bias_st, t):
    sc = _dot(q_st, k3, "nt") * (HEAD_DIM ** -0.5) + bias_st
    col = lax.broadcasted_iota(jnp.int32, sc.shape, 1)
    sc = jnp.where(col >= PAD_ROWS - t * Q_TILE, sc, NEG_INF)
    m = jnp.max(sc, axis=-1, keepdims=True)
    p = jnp.exp(sc - m)
    return p * (1.0 / jnp.sum(p, axis=-1, keepdims=True))


def _attn_specs(d_model):
    nq = PAD_ROWS // Q_TILE
    hp_k = d_model // LANES
    specs = [pl.BlockSpec((Q_TILE, LANES), lambda hp, t: (t + nq, hp))]
    for which in (1, 2):
        for j in range(K_WIN // Q_TILE):
            specs.append(pl.BlockSpec((Q_TILE, LANES), lambda hp, t, j=j, which=which: (t + j, which * hp_k + hp)))
    specs.append(pl.BlockSpec((2, Q_TILE, K_WIN), lambda hp, t: (hp, 0, 0)))
    return specs


def attn_fwd(name, qkvp, bias):
    s = qkvp.shape[0] - PAD_ROWS
    d_model = qkvp.shape[1] // 3
    nw = K_WIN // Q_TILE

    def body(q_ref, *refs):
        k_refs, v_refs, b_ref, o_ref = refs[:nw], refs[nw:2 * nw], refs[2 * nw], refs[2 * nw + 1]
        t = pl.program_id(1)
        k3 = jnp.concatenate([r[...] for r in k_refs], axis=0)
        v3 = jnp.concatenate([r[...] for r in v_refs], axis=0)
        p = _attn_probs(_stack_heads(q_ref[...]), k3, b_ref[...].reshape(2 * Q_TILE, K_WIN), t)
        o_ref[...] = _unstack_heads(_dot(p, v3, "nn")).astype(BF16)

    return _call(
        body, name=name, grid=(d_model // LANES, s // Q_TILE),
        in_specs=_attn_specs(d_model), out_specs=pl.BlockSpec((Q_TILE, LANES), lambda hp, t: (t, hp)),
        out_shape=jax.ShapeDtypeStruct((s, d_model), BF16), compiler_params=_cp(),
    )(qkvp, *([qkvp] * (2 * nw)), bias)


def attn_bwd(name, qkvp, bias, do):
    s = qkvp.shape[0] - PAD_ROWS
    d_model = qkvp.shape[1] // 3
    nw = K_WIN // Q_TILE
    nt = s // Q_TILE
    scale = HEAD_DIM ** -0.5

    def body(q_ref, *refs):
        k_refs, v_refs = refs[:nw], refs[nw:2 * nw]
        b_ref, do_ref, dq_ref, dk_ref, dv_ref, ds_ref, dk_acc, dv_acc = refs[2 * nw:]
        t = pl.program_id(1)
        first = t == 0

        @pl.when(first)
        def _():
            dk_acc[...] = jnp.zeros(dk_acc.shape, F32)
            dv_acc[...] = jnp.zeros(dv_acc.shape, F32)

        q_st = _stack_heads(q_ref[...])
        do_st = _stack_heads(do_ref[...])
        k3 = jnp.concatenate([r[...] for r in k_refs], axis=0)
        v3 = jnp.concatenate([r[...] for r in v_refs], axis=0)
        p = _attn_probs(q_st, k3, b_ref[...].reshape(2 * Q_TILE, K_WIN), t)
        dp = _dot(do_st, v3, "nt")
        ds = p * (dp - jnp.sum(p * dp, axis=-1, keepdims=True))
        _acc_add(ds_ref, first, ds.reshape(2, Q_TILE, K_WIN))
        dsb = (ds * scale).astype(BF16)
        dq_ref[...] = _unstack_heads(_dot(dsb, k3, "nn")).astype(BF16)
        start = pl.multiple_of(t * Q_TILE, Q_TILE)
        dk_acc[pl.ds(start, K_WIN), :] += _dot(dsb, q_st, "tn")
        dv_acc[pl.ds(start, K_WIN), :] += _dot(p, do_st, "tn")

        @pl.when(t == nt - 1)
        def _():
            dk_ref[...] = dk_acc[pl.ds(PAD_ROWS, s), :].astype(BF16)
            dv_ref[...] = dv_acc[pl.ds(PAD_ROWS, s), :].astype(BF16)

    specs = _attn_specs(d_model) + [pl.BlockSpec((Q_TILE, LANES), lambda hp, t: (t, hp))]
    col_spec = pl.BlockSpec((s, LANES), lambda hp, t: (0, hp))
    return _call(
        body, name=name, grid=(d_model // LANES, nt), in_specs=specs,
        out_specs=[pl.BlockSpec((Q_TILE, LANES), lambda hp, t: (t, hp)), col_spec, col_spec,
                   pl.BlockSpec((2, Q_TILE, K_WIN), lambda hp, t: (hp, 0, 0))],
        out_shape=[jax.ShapeDtypeStruct((s, d_model), BF16)] * 3
        + [jax.ShapeDtypeStruct((N_HEADS, Q_TILE, K_WIN), F32)],
        scratch_shapes=[pltpu.VMEM((PAD_ROWS + s, LANES), F32), pltpu.VMEM((PAD_ROWS + s, LANES), F32)],
        compiler_params=_cp(),
    )(qkvp, *([qkvp] * (2 * nw)), bias, do)


def bias_grad_reduce(name, sheared):
    nh, _, width = sheared.shape

    def body(x_ref, col_ref, sat_ref):
        cols = _colsum(x_ref[...])
        col_ref[...] = cols
        k = lax.broadcasted_iota(jnp.int32, cols.shape, 1)
        tot = jnp.sum(jnp.where((k >= 1) & (k <= SHEAR_SAT), cols, 0.0), axis=-1, keepdims=True)
        sat_ref[...] = jnp.broadcast_to(tot, sat_ref.shape)

    return _call(
        body, name=name, grid=(nh,),
        in_specs=[pl.BlockSpec((None, Q_TILE, width), lambda hh: (hh, 0, 0))],
        out_specs=[pl.BlockSpec((None, 1, width), lambda hh: (hh, 0, 0)),
                   pl.BlockSpec((None, 1, LANES), lambda hh: (hh, 0, 0))],
        out_shape=[jax.ShapeDtypeStruct((nh, 1, width), F32), jax.ShapeDtypeStruct((nh, 1, LANES), F32)],
        compiler_params=_cp(),
    )(sheared)


def _ew_rows(r, most=512):
    for cand in (512, 256, 128, 64, 32, 16, 8):
        if cand <= most and r % cand == 0:
            return cand
    return r


def cast_into_gathered(name, w, layer, s_idx, n_blocks=N_SHARD, dtype=BF16, token=None):
    r, c = w.shape[-2:]
    tr = _ew_rows(r)

    def body(s_ref, w_ref, *rest):
        rest[-1][...] = w_ref[...].astype(dtype)

    extra = [] if token is None else [token]
    grid_spec = pltpu.PrefetchScalarGridSpec(
        num_scalar_prefetch=1, grid=(r // tr,),
        in_specs=[pl.BlockSpec((None, tr, c), lambda i, s_ref: (layer, i, 0))] + [ANY_SPEC] * len(extra),
        out_specs=pl.BlockSpec((None, tr, c), lambda i, s_ref: (s_ref[0], i, 0)))
    return _call(
        body, name=name, grid_spec=grid_spec, out_shape=jax.ShapeDtypeStruct((n_blocks, r, c), dtype),
        compiler_params=_cp(),
    )(s_idx, w, *extra)


def adamw(name, w, grads, m, v, token=None):
    nl, r, c = w.shape
    tr = _ew_rows(r, 256)

    def body(*refs):
        w_ref, m_ref, v_ref = refs[0], refs[1], refs[2]
        g_refs = refs[3:3 + nl]
        d_ref, nm_ref, nv_ref = refs[-3:]
        layer = pl.program_id(0)
        g = g_refs[0][...]
        for j in range(1, nl):
            g = jnp.where(layer == j, g_refs[j][...], g)
        nm = ADAM_B1 * m_ref[...] + (1.0 - ADAM_B1) * g
        nv = ADAM_B2 * v_ref[...] + (1.0 - ADAM_B2) * (g * g)
        m_hat = nm / ADAM_BC1
        v_hat = nv / ADAM_BC2
        d_ref[...] = -ADAM_LR * (m_hat / (jnp.sqrt(v_hat) + ADAM_EPS) + ADAM_WD * w_ref[...])
        nm_ref[...] = nm
        nv_ref[...] = nv

    p_spec = pl.BlockSpec((None, tr, c), lambda l, i: (l, i, 0))
    g_spec = pl.BlockSpec((tr, c), lambda l, i: (i, 0))
    extra = [] if token is None else [token]
    extra_specs = [] if token is None else [ANY_SPEC]
    return _call(
        body, name=name, grid=(nl, r // tr), in_specs=[p_spec] * 3 + [g_spec] * nl + extra_specs,
        out_specs=[p_spec] * 3, out_shape=[jax.ShapeDtypeStruct((nl, r, c), F32)] * 3, compiler_params=_cp(),
    )(w, m, v, *grads, *extra)


def sum_blocks(name, gathered, n_blocks):
    r = gathered.shape[0] // n_blocks
    c = gathered.shape[1]
    tr = _ew_rows(r)
    nt = r // tr

    def body(*refs):
        acc = refs[0][...]
        for j in range(1, n_blocks):
            acc = acc + refs[j][...]
        refs[-1][...] = acc

    specs = [pl.BlockSpec((tr, c), lambda i, j=j: (j * nt + i, 0)) for j in range(n_blocks)]
    return _call(
        body, name=name, grid=(nt,), in_specs=specs, out_specs=pl.BlockSpec((tr, c), lambda i: (i, 0)),
        out_shape=jax.ShapeDtypeStruct((r, c), F32), compiler_params=_cp(),
    )(*([gathered] * n_blocks))


def _place():
    return lax.axis_index("x"), lax.axis_index("y"), lax.axis_index("c")


def _other_chips(x, y):
    return [(1 - x, y), (x, 1 - y), (1 - x, 1 - y)]


HBM_SPEC = pl.BlockSpec(memory_space=pltpu.HBM)
SEM_SPEC = pl.BlockSpec(memory_space=pltpu.SEMAPHORE)
ANY_SPEC = pl.BlockSpec(memory_space=pl.ANY)
EFFECT = pltpu.SideEffectType.DATAFLOW_SIDE_EFFECTING


def copies_start(name, bufs, plan, n_copies):
    n = len(bufs)

    def body(*refs):
        send, recv = refs[n], refs[n + 1]
        token = refs[2 * n + 2]
        for k, (src, dst, peer, _) in enumerate(plan(refs[:n])):
            pltpu.make_async_remote_copy(
                src_ref=src, dst_ref=dst, send_sem=send.at[k], recv_sem=recv.at[k],
                device_id=peer, device_id_type=MESH).start()
        token[...] = jnp.zeros(token.shape, F32)

    outs = pl.pallas_call(
        body, name=name,
        out_shape=(pltpu.SemaphoreType.DMA((n_copies,)), pltpu.SemaphoreType.DMA((n_copies,)),
                   *[pltpu.HBM(b.shape, b.dtype) for b in bufs], jax.ShapeDtypeStruct((8, LANES), F32)),
        in_specs=[HBM_SPEC] * n,
        out_specs=(SEM_SPEC, SEM_SPEC, *([HBM_SPEC] * n), pl.BlockSpec(memory_space=pltpu.VMEM)),
        input_output_aliases={a: a + 2 for a in range(n)},
        compiler_params=pltpu.CompilerParams(has_side_effects=EFFECT),
    )(*[_in_hbm(b) for b in bufs])
    return outs[0], outs[1], list(outs[2:2 + n]), outs[2 + n]


def copies_wait(name, bufs, send, recv, plan, sem_base, after):
    n = len(bufs)

    def body(*refs):
        send_ref, recv_ref = refs[n], refs[n + 1]
        for k, (src, _, peer, land) in enumerate(plan(refs[:n])):
            cp = pltpu.make_async_remote_copy(
                src_ref=src, dst_ref=land, send_sem=send_ref.at[sem_base + k], recv_sem=recv_ref.at[sem_base + k],
                device_id=peer, device_id_type=MESH)
            cp.wait_send()
            cp.wait_recv()

    outs = pl.pallas_call(
        body, name=name,
        out_shape=tuple(pltpu.HBM(b.shape, b.dtype) for b in bufs),
        in_specs=[HBM_SPEC] * n + [SEM_SPEC, SEM_SPEC, ANY_SPEC], out_specs=tuple([HBM_SPEC] * n),
        input_output_aliases={a: a for a in range(n)},
        compiler_params=pltpu.CompilerParams(has_side_effects=EFFECT),
    )(*bufs, send, recv, after)
    return list(outs)


def gather_plan(refs):
    x, y, c = _place()
    me = 2 * x + y
    return [(buf.at[me], buf.at[me], (cx, cy, c), buf.at[2 * cx + cy])
            for buf in refs for cx, cy in _other_chips(x, y)]


def all_plan(refs):
    x, y, c = _place()
    me = 4 * x + 2 * y + c
    out = []
    for buf in refs:
        for flip in range(1, 8):
            px = 1 - x if flip & 4 else x
            py = 1 - y if flip & 2 else y
            pc = 1 - c if flip & 1 else c
            out.append((buf.at[me], buf.at[me], (px, py, pc), buf.at[4 * px + 2 * py + pc]))
    return out


def swap_plan(refs):
    x, y, c = _place()
    n = len(refs) // 2
    out = []
    for g, land in zip(refs[:n], refs[n:]):
        hr = g.shape[1] // 2
        out.append((g.at[:, pl.ds((1 - c) * hr, hr)], land, (x, y, 1 - c), land))
    return out


def owners_plan(refs):
    x, y, c = _place()
    n = len(refs) // 2
    return [(src.at[2 * cx + cy], land.at[j], (cx, cy, c), land.at[j])
            for src, land in zip(refs[:n], refs[n:]) for j, (cx, cy) in enumerate(_other_chips(x, y))]


def join_plan(refs):
    x, y, c = _place()
    out = []
    for buf in refs:
        hr = buf.shape[0] // 2
        mine = buf.at[pl.ds(c * hr, hr)]
        out.append((mine, mine, (x, y, 1 - c), buf.at[pl.ds((1 - c) * hr, hr)]))
    return out


def add_halves(name, grad, landed, c_idx):
    _, r, c = grad.shape
    hr = r // 2
    tr = _ew_rows(hr)
    nt = hr // tr

    def body(c_ref, g_ref, l_ref, o_ref, ob_ref):
        tot = g_ref[...] + l_ref[...]
        o_ref[...] = tot
        ob_ref[...] = tot.astype(BF16)

    blk = pl.BlockSpec((None, tr, c), lambda sh, i, c_ref: (sh, i, 0))
    grid_spec = pltpu.PrefetchScalarGridSpec(
        num_scalar_prefetch=1, grid=(N_SHARD, nt),
        in_specs=[pl.BlockSpec((None, tr, c), lambda sh, i, c_ref: (sh, c_ref[0] * nt + i, 0)), blk],
        out_specs=[blk, blk])
    return _call(
        body, name=name, grid_spec=grid_spec,
        out_shape=[jax.ShapeDtypeStruct((N_SHARD, hr, c), F32), jax.ShapeDtypeStruct((N_SHARD, hr, c), BF16)],
        compiler_params=_cp(),
    )(c_idx, grad, landed)


def add_owned(name, own, landed, sc_idx):
    _, hr, c = own.shape
    tr = _ew_rows(hr)
    nt = hr // tr

    def body(sc_ref, o_ref, l0, l1, l2, out_ref):
        out_ref[...] = ((o_ref[...] + l0[...].astype(F32)) + l1[...].astype(F32)) + l2[...].astype(F32)

    grid_spec = pltpu.PrefetchScalarGridSpec(
        num_scalar_prefetch=1, grid=(nt,),
        in_specs=[pl.BlockSpec((None, tr, c), lambda i, sc_ref: (sc_ref[0], i, 0))]
        + [pl.BlockSpec((None, tr, c), lambda i, sc_ref, j=j: (j, i, 0)) for j in range(3)],
        out_specs=pl.BlockSpec((tr, c), lambda i, sc_ref: (sc_ref[1] * nt + i, 0)))
    return _call(
        body, name=name, grid_spec=grid_spec, out_shape=jax.ShapeDtypeStruct((2 * hr, c), F32),
        compiler_params=_cp(),
    )(sc_idx, own, landed, landed, landed)


PACK_QUANTUM = 8 * LANES


def _pack(arrays):
    pieces = []
    for a in arrays:
        flat = a.reshape(-1)
        padded = -(-flat.shape[0] // PACK_QUANTUM) * PACK_QUANTUM
        pieces.append(jnp.pad(flat, (0, padded - flat.shape[0])).reshape(-1, LANES))
    return jnp.concatenate(pieces, axis=0)


def _unpack(packed, shapes):
    out = []
    row = 0
    for shp in shapes:
        size = math.prod(shp)
        rows = -(-size // PACK_QUANTUM) * 8
        out.append(packed[row:row + rows].reshape(-1)[:size].reshape(shp))
        row += rows
    return out


def kernel(x, p, mix_w_in, pool_w, pool_scale, conv_dw_w, conv_dw_b, conv_ln_g, conv_ln_b, mix_w_out, attn_w_qkv, attn_rel_bias, attn_w_o, ln_mix_g, ln_mix_b, ffn_w_up, ffn_dw_w, ffn_dw_b, ffn_w_down, ple_w_proj, ple_w_gate, ple_b_gate, ln_ffn_g, ln_ffn_b, loss_target, m_mix_w_in, m_pool_w, m_pool_scale, m_conv_dw_w, m_conv_dw_b, m_conv_ln_g, m_conv_ln_b, m_mix_w_out, m_attn_w_qkv, m_attn_rel_bias, m_attn_w_o, m_ln_mix_g, m_ln_mix_b, m_ffn_w_up, m_ffn_dw_w, m_ffn_dw_b, m_ffn_w_down, m_ple_w_proj, m_ple_w_gate, m_ple_b_gate, m_ln_ffn_g, m_ln_ffn_b, v_mix_w_in, v_pool_w, v_pool_scale, v_conv_dw_w, v_conv_dw_b, v_conv_ln_g, v_conv_ln_b, v_mix_w_out, v_attn_w_qkv, v_attn_rel_bias, v_attn_w_o, v_ln_mix_g, v_ln_mix_b, v_ffn_w_up, v_ffn_dw_w, v_ffn_dw_b, v_ffn_w_down, v_ple_w_proj, v_ple_w_gate, v_ple_b_gate, v_ln_ffn_g, v_ln_ffn_b):
    xi, yi, ci = _place()
    shard_idx = (2 * xi + yi).astype(jnp.int32)
    s_arr = shard_idx.reshape(1)
    c_arr = ci.astype(jnp.int32).reshape(1)
    sc_arr = jnp.concatenate([s_arr, c_arr])

    x0 = x[0]
    target = loss_target[0]
    p_rows = p.reshape(p.shape[0] * p.shape[2], p.shape[3])
    seq = x0.shape[0]

    big = [
        ("mix_w_in", mix_w_in, m_mix_w_in, v_mix_w_in, True),
        ("mix_w_out", mix_w_out, m_mix_w_out, v_mix_w_out, False),
        ("attn_w_qkv", attn_w_qkv, m_attn_w_qkv, v_attn_w_qkv, True),
        ("attn_w_o", attn_w_o, m_attn_w_o, v_attn_w_o, False),
        ("ffn_w_up", ffn_w_up, m_ffn_w_up, v_ffn_w_up, True),
        ("ffn_w_down", ffn_w_down, m_ffn_w_down, v_ffn_w_down, False),
        ("ple_w_proj", ple_w_proj, m_ple_w_proj, v_ple_w_proj, True),
        ("ple_w_gate", ple_w_gate, m_ple_w_gate, v_ple_w_gate, False),
    ]
    params = {nm: w for nm, w, _, _, _ in big}
    col_sharded = {nm: cs for nm, _, _, _, cs in big}
    keys = [("mix_w_in", 0), ("mix_w_out", 0), ("ffn_w_up", 0), ("ffn_w_down", 0), ("ple_w_gate", 0),
            ("ple_w_proj", 0), ("attn_w_qkv", 0), ("attn_w_o", 0), ("ffn_w_up", 1), ("ffn_w_down", 1),
            ("ple_w_gate", 1), ("ple_w_proj", 1)]
    dw_shapes = [conv_dw_w.shape, ffn_dw_w.shape]
    dw_block = cast_into_gathered("place_dw", _pack([conv_dw_w, ffn_dw_w])[None], 0, s_arr, dtype=F32)
    n_first = 2
    started = {}
    gather_token = None
    for tag, group in (("first", keys[:n_first]), ("rest", keys[n_first:])):
        shards = [cast_into_gathered(f"cast_{nm}_{layer}", params[nm], layer, s_arr, token=gather_token)
                  for nm, layer in group]
        if tag == "first":
            shards.append(dw_block)
        send, recv, bufs, gather_token = copies_start(f"gather_start_{tag}", shards, gather_plan, 3 * len(shards))
        for a, key in enumerate(group):
            started[key] = (send, recv, bufs[a], 3 * a)
        if tag == "first":
            dw_started = (send, recv, bufs[-1], 3 * len(group))
    arrived_w = {}

    def weight(nm, layer, after=None):
        key = (nm, layer)
        if key not in arrived_w:
            send, recv, buf, base = started[key]
            arrived_w[key] = copies_wait(f"gather_wait_{nm}_{layer}", [buf], send, recv, gather_plan, base, after)[0]
        g = arrived_w[key]
        if col_sharded[nm]:
            return g
        return g.reshape(g.shape[0] * g.shape[1], g.shape[2])

    def tie(a, token):
        return a + token[0:1, 0:1].astype(a.dtype)

    class Reducer:
        def __init__(self, tag, group):
            self.tag, self.group, self.stage = tag, group, 0
            self.n = len(group)
            self.result = None

        def advance(self, after):
            tag, n = self.tag, self.n
            if self.stage == 0:
                grads = []
                for key in self.group:
                    g = big_grads[key]
                    grads.append(g if g.ndim == 3 else g.reshape(N_SHARD, g.shape[0] // N_SHARD, g.shape[1]))
                lands = [lax.empty((N_SHARD, g.shape[1] // 2, g.shape[2]), F32) for g in grads]
                self.sems = copies_start(f"swap_start_{tag}", grads + lands, swap_plan, n)
            elif self.stage == 1:
                send, recv, bufs, _ = self.sems
                outs = copies_wait(f"swap_wait_{tag}", bufs, send, recv, swap_plan, 0, after)
                self.own, wire = [], []
                for key, g, ld in zip(self.group, outs[:n], outs[n:]):
                    o, ob = add_halves(f"add_halves_{key[0]}_{key[1]}", g, ld, c_arr)
                    self.own.append(o)
                    wire.append(ob)
                lands = [lax.empty((3,) + w.shape[1:], BF16) for w in wire]
                self.sems = copies_start(f"owners_start_{tag}", wire + lands, owners_plan, 3 * n)
            elif self.stage == 2:
                send, recv, bufs, _ = self.sems
                outs = copies_wait(f"owners_wait_{tag}", bufs, send, recv, owners_plan, 0, after)
                finals = [add_owned(f"add_owned_{key[0]}_{key[1]}", o, ar, sc_arr)
                          for key, o, ar in zip(self.group, self.own, outs[n:])]
                self.sems = copies_start(f"join_start_{tag}", finals, join_plan, n)
            elif self.stage == 3:
                send, recv, bufs, _ = self.sems
                outs = copies_wait(f"join_wait_{tag}", bufs, send, recv, join_plan, 0, after)
                self.result = dict(zip(self.group, outs))
                self.sems = None
            self.stage += 1
            return None if self.sems is None else self.sems[3]

    dw_cache = []

    def conv_weights(after):
        if not dw_cache:
            send, recv, buf, base = dw_started
            dw_all = copies_wait("gather_wait_dw", [buf], send, recv, gather_plan, base, after)[0]
            dw_parts = [_unpack(dw_all[k], dw_shapes) for k in range(N_SHARD)]
            dw_cache.append(jnp.concatenate([pc[0] for pc in dw_parts], axis=2)[0])
            dw_cache.append(jnp.concatenate([pc[1] for pc in dw_parts], axis=2))
        return dw_cache

    big_grads = {}
    small_grads = {}

    saved = []
    h_in = x0
    for layer in range(N_LAYERS):
        sv = {"x_in": h_in}
        if layer % 2 == 0:
            u = mm_cols_fwd("mix_in", h_in, weight("mix_w_in", 0, gather_token), F32)
            conv_w_full, ffn_dw_full = conv_weights(u)
            cat, d_sv, e_sv, glu_sv, hh_sv, rs_sv = mixer_fwd(
                "mixer_fwd", u, pool_w[0], pool_scale, conv_w_full, conv_dw_b, conv_ln_g, conv_ln_b)
            mix = mm_rows_fwd("mix_out", cat, weight("mix_w_out", 0, cat))
            sv.update(u=u, cat=cat, d=d_sv, e=e_sv, glu=glu_sv, hh=hh_sv, rs=rs_sv)
        else:
            qkvp = mm_cols_fwd("attn_qkv", h_in, weight("attn_w_qkv", 0, h_in), BF16,
                               pad_blocks=PAD_ROWS // _row_tile(seq))
            bias = _band_masked(_toeplitz_bias(tie(attn_rel_bias[0], gather_token)))
            att = attn_fwd("attn_fwd", qkvp, bias)
            mix = mm_rows_fwd("attn_out", att, weight("attn_w_o", 0, att))
            sv.update(qkvp=qkvp, bias=bias, att=att)
        x1, xh1, rs1 = ln_fwd(f"ln_mix_{layer}", h_in, mix, ln_mix_g[layer:layer + 1], ln_mix_b[layer:layer + 1])
        gv = mm_cols_fwd(f"ffn_up_{layer}", x1, weight("ffn_w_up", layer, x1), F32)
        hid = ffn_act_fwd(f"ffn_act_{layer}", gv, ffn_dw_full[layer], ffn_dw_b[layer:layer + 1])
        ffn = mm_rows_fwd(f"ffn_down_{layer}", hid, weight("ffn_w_down", layer, hid))
        pgl = mm_rows_fwd(f"ple_gate_{layer}", x1, weight("ple_w_gate", layer, ffn))
        pp = mm_cols_fwd(f"ple_proj_{layer}", p_rows, weight("ple_w_proj", layer, pgl), F32, part=(layer, N_LAYERS))
        bg = ple_b_gate[layer:layer + 1]
        x2, xh2, rs2 = ln_fwd(f"ln_ffn_{layer}", x1, ffn, ln_ffn_g[layer:layer + 1], ln_ffn_b[layer:layer + 1],
                              ple=(pgl, pp, bg), emit_y=layer < N_LAYERS - 1)
        sv.update(x1=x1, xh1=xh1, rs1=rs1, gv=gv, hid=hid, pgl=pgl, pp=pp, xh2=xh2, rs2=rs2)
        saved.append(sv)
        h_in = x2

    reducers = []

    def open_group(tag, group):
        reducers.append(Reducer(tag, group))
        return reducers[-1].advance(None)

    def hook(after):
        token = None
        for red in reducers:
            if red.stage < 4:
                tk = red.advance(after)
                if tk is not None:
                    token = tk if token is None else token + tk
        return token

    def tied(a, token):
        return a if token is None else tie(a, token)

    parts = []
    token = None
    for layer in reversed(range(N_LAYERS)):
        sv = saved[layer]
        bg = ple_b_gate[layer:layer + 1]
        if layer == 0:
            token = open_group("layer1", [("attn_w_qkv", 0), ("attn_w_o", 0), ("ffn_w_up", 1), ("ffn_w_down", 1),
                                          ("ple_w_gate", 1), ("ple_w_proj", 1)])
        last = layer == N_LAYERS - 1
        res = ln_bwd(
            f"ln_ffn_bwd_{layer}", parts, sv["xh2"], sv["rs2"], tied(ln_ffn_g[layer:layer + 1], token),
            ple=(sv["pgl"], sv["pp"], bg), loss=(target, ln_ffn_b[layer:layer + 1]) if last else None)
        dz2, dg2, db2, dpp, dpgl, dbg = res[:6]
        if last:
            loss_part = res[6]
        small_grads[("ln_ffn_g", layer)] = dg2
        small_grads[("ln_ffn_b", layer)] = db2
        small_grads[("ple_b_gate", layer)] = dbg
        w_down = weight("ffn_w_down", layer)
        dhid = mm_rows_dx(f"ffn_down_dx_{layer}", dz2, w_down)
        big_grads[("ffn_w_down", layer)] = mm_rows_dw(f"ffn_down_dw_{layer}", sv["hid"], dz2)
        token = hook(big_grads[("ffn_w_down", layer)])
        dgv, ddw, ddb = ffn_act_bwd(f"ffn_act_bwd_{layer}", dhid, sv["gv"], ffn_dw_full[layer],
                                    tied(ffn_dw_b[layer:layer + 1], token))
        small_grads[("ffn_dw_w", layer)] = ddw
        small_grads[("ffn_dw_b", layer)] = ddb
        big_grads[("ffn_w_up", layer)] = mm_cols_dw(f"ffn_up_dw_{layer}", sv["x1"], dgv)
        t_up = mm_cols_dx(f"ffn_up_dx_{layer}", dgv, weight("ffn_w_up", layer))
        token = hook(t_up)
        big_grads[("ple_w_gate", layer)] = mm_rows_dw(f"ple_gate_dw_{layer}", sv["x1"], dpgl)
        t_gate = mm_rows_dx(f"ple_gate_dx_{layer}", dpgl, weight("ple_w_gate", layer))
        big_grads[("ple_w_proj", layer)] = mm_cols_dw(f"ple_proj_dw_{layer}", p_rows, dpp, part=(layer, N_LAYERS))
        token2 = hook(big_grads[("ple_w_proj", layer)])
        if token2 is not None:
            token = token2 if token is None else token + token2
        if layer == 0:
            token3 = open_group("layer0_ffn", [("ffn_w_up", 0), ("ffn_w_down", 0), ("ple_w_gate", 0), ("ple_w_proj", 0)])
            token = token3 if token is None else token + token3
        dz1, dg1, db1 = ln_bwd(
            f"ln_mix_bwd_{layer}", [(ALPHA, dz2), (1.0, t_up), (1.0, t_gate)], sv["xh1"], sv["rs1"],
            tied(ln_mix_g[layer:layer + 1], token))
        small_grads[("ln_mix_g", layer)] = dg1
        small_grads[("ln_mix_b", layer)] = db1
        if layer % 2 == 0:
            dcat = mm_rows_dx("mix_out_dx", dz1, weight("mix_w_out", 0))
            big_grads[("mix_w_out", 0)] = mm_rows_dw("mix_out_dw", sv["cat"], dz1)
            token = hook(big_grads[("mix_w_out", 0)])
            du, dpw, dps, dcw, dcb, dcg, dcbt = mixer_bwd(
                "mixer_bwd", dcat, sv["u"], sv["d"], sv["e"], sv["glu"], sv["hh"], sv["rs"],
                pool_w[0], pool_scale, conv_w_full, tied(conv_ln_g, token), conv_ln_b)
            small_grads[("pool_w", 0)] = dpw
            small_grads[("pool_scale", 0)] = dps
            small_grads[("conv_dw_w", 0)] = dcw
            small_grads[("conv_dw_b", 0)] = dcb
            small_grads[("conv_ln_g", 0)] = dcg
            small_grads[("conv_ln_b", 0)] = dcbt
            big_grads[("mix_w_in", 0)] = mm_cols_dw("mix_in_dw", sv["x_in"], du)
            hook(big_grads[("mix_w_in", 0)])
            open_group("layer0_mix", [("mix_w_in", 0), ("mix_w_out", 0)])
            dx_in = mm_cols_dx("mix_in_dx", du, weight("mix_w_in", 0), addend=(ALPHA, dz1))
            token = hook(dx_in)
        else:
            do = mm_rows_dx("attn_out_dx", dz1, weight("attn_w_o", 0), out_dtype=BF16)
            big_grads[("attn_w_o", 0)] = mm_rows_dw("attn_out_dw", sv["att"], dz1)
            dq, dk, dv, ds_sum = attn_bwd("attn_bwd", sv["qkvp"], sv["bias"], do)
            cols, sat = bias_grad_reduce("bias_grad", _shear_for_bias_grad(ds_sum))
            d_rel = jnp.concatenate(
                [jnp.zeros((N_HEADS, 1), F32),
                 jnp.flip(cols[:, 0, SHEAR_SAT + 1:SHEAR_W], axis=1),
                 sat[:, 0, 0:1]], axis=1)
            small_grads[("attn_rel_bias", 0)] = d_rel
            dqkv = jnp.concatenate([dq, dk, dv], axis=1)
            big_grads[("attn_w_qkv", 0)] = mm_cols_dw("attn_qkv_dw", sv["x_in"], dqkv)
            dx_in = mm_cols_dx("attn_qkv_dx", dqkv, weight("attn_w_qkv", 0), addend=(ALPHA, dz1))
        parts = [(1.0, dx_in)]
    grad_x = dx_in

    small = [
        ("pool_w", pool_w, m_pool_w, v_pool_w, None),
        ("pool_scale", pool_scale, m_pool_scale, v_pool_scale, None),
        ("conv_dw_w", conv_dw_w, m_conv_dw_w, v_conv_dw_w, 2),
        ("conv_dw_b", conv_dw_b, m_conv_dw_b, v_conv_dw_b, None),
        ("conv_ln_g", conv_ln_g, m_conv_ln_g, v_conv_ln_g, None),
        ("conv_ln_b", conv_ln_b, m_conv_ln_b, v_conv_ln_b, None),
        ("attn_rel_bias", attn_rel_bias, m_attn_rel_bias, v_attn_rel_bias, None),
        ("ln_mix_g", ln_mix_g, m_ln_mix_g, v_ln_mix_g, None),
        ("ln_mix_b", ln_mix_b, m_ln_mix_b, v_ln_mix_b, None),
        ("ffn_dw_w", ffn_dw_w, m_ffn_dw_w, v_ffn_dw_w, 2),
        ("ffn_dw_b", ffn_dw_b, m_ffn_dw_b, v_ffn_dw_b, None),
        ("ple_b_gate", ple_b_gate, m_ple_b_gate, v_ple_b_gate, None),
        ("ln_ffn_g", ln_ffn_g, m_ln_ffn_g, v_ln_ffn_g, None),
        ("ln_ffn_b", ln_ffn_b, m_ln_ffn_b, v_ln_ffn_b, None),
    ]
    full_grads = []
    for nm, w, _, _, shard_axis in small:
        full = list(w.shape)
        if shard_axis is not None:
            full[shard_axis] *= N_SHARD
        per_layer = [small_grads[(nm, layer)].reshape((1,) + tuple(full[1:])) for layer in range(w.shape[0])]
        full_grads.append(jnp.concatenate(per_layer, axis=0))
    packed = _pack(full_grads + [loss_part])
    dev_arr = (4 * xi + 2 * yi + ci).astype(jnp.int32).reshape(1)
    sg_block = cast_into_gathered("place_small_grads", packed[None], 0, dev_arr, n_blocks=8, dtype=F32)
    sg_send, sg_recv, sg_bufs, sg_token = copies_start("small_grads_start", [sg_block], all_plan, 7)
    token = sg_token if token is None else token + sg_token

    shard_grads = {}
    for red in reducers:
        if red.stage == 4:
            shard_grads.update(red.result)
    big_out = {}

    def update_big(names, tok):
        for nm, w, m, v, _ in big:
            if nm in names:
                gl = [shard_grads[(nm, layer)] for layer in range(w.shape[0])]
                delta, new_m, new_v = adamw(f"adamw_{nm}", w, gl, m, v, token=tok)
                big_out[nm] = (jnp.stack(gl, axis=0), delta, new_m, new_v)

    last_group = ("mix_w_in", "mix_w_out")
    update_big([nm for nm, _, _, _, _ in big if nm not in last_group], token)
    token = hook(big_out["ffn_w_up"][1])

    gathered_sg = copies_wait("small_grads_wait", sg_bufs, sg_send, sg_recv, all_plan, 0, big_out["ffn_w_down"][1])[0]
    total = sum_blocks("sum_small", gathered_sg.reshape(8 * packed.shape[0], LANES), 8)
    unpacked = _unpack(total, [g.shape for g in full_grads] + [loss_part.shape])
    loss = unpacked[-1][0, 0]
    local_grads = []
    for (nm, w, _, _, shard_axis), g in zip(small, unpacked[:-1]):
        if shard_axis is not None:
            width = w.shape[shard_axis]
            g = lax.dynamic_slice_in_dim(g, shard_idx * width, width, axis=shard_axis)
        local_grads.append(g.reshape(w.shape))
    shapes = [w.shape for _, w, _, _, _ in small]
    pg = _pack(local_grads)
    pw = _pack([w for _, w, _, _, _ in small])
    pm = _pack([m for _, _, m, _, _ in small])
    pv = _pack([v for _, _, _, v, _ in small])
    delta_s, new_m_s, new_v_s = adamw("adamw_small", pw[None], [pg], pm[None], pv[None], token=token)
    hook(delta_s)
    for red in reducers:
        shard_grads.update(red.result)
    update_big(last_group, None)
    small_out = {}
    for (nm, _, _, _, _), g, d_, m_, v_ in zip(
            small, local_grads, _unpack(delta_s[0], shapes), _unpack(new_m_s[0], shapes), _unpack(new_v_s[0], shapes)):
        small_out[nm] = (g, d_, m_, v_)

    order = ["mix_w_in", "pool_w", "pool_scale", "conv_dw_w", "conv_dw_b", "conv_ln_g", "conv_ln_b", "mix_w_out",
             "attn_w_qkv", "attn_rel_bias", "attn_w_o", "ln_mix_g", "ln_mix_b", "ffn_w_up", "ffn_dw_w", "ffn_dw_b",
             "ffn_w_down", "ple_w_proj", "ple_w_gate", "ple_b_gate", "ln_ffn_g", "ln_ffn_b"]
    res = {**big_out, **small_out}
    outs = [loss, grad_x[None]]
    for slot in range(4):
        outs += [res[nm][slot] for nm in order]
    return tuple(outs)
```

```python
import functools
import math

import jax
import jax.numpy as jnp
from jax import lax
from jax.experimental import pallas as pl
from jax.experimental.pallas import tpu as pltpu

F32 = jnp.float32
BF16 = jnp.bfloat16
MESH = pl.DeviceIdType.MESH

N_LAYERS = 2
ALPHA = (2 * N_LAYERS) ** 0.25
LN_EPS = 1e-5
NEG_INF = -1e30
CHUNK = 64
LEFT_CHUNKS = 8
PAD_ROWS = LEFT_CHUNKS * CHUNK
HEAD_DIM = 64
N_HEADS = 16
MAX_REL = 256
POOL_WINDOWS = (2, 4, 8, 16)
POOL_GROUP = 128
CONV_K = 31
FFN_K = 3
CONV_HALO = 32
FFN_HALO = 8
FFN_TILE = 256
FFN_CHUNK_ROWS = 32
FFN_CHUNK_LANES = 256
Q_TILE = 256
K_WIN = Q_TILE + PAD_ROWS
LANES = 128
SUBLANES = 8
ATTN_PAIRS = 2
ATTN_LANES = ATTN_PAIRS * LANES
SHEAR_W = Q_TILE + K_WIN
SHEAR_SAT = SHEAR_W - 2 * MAX_REL
N_SHARD = 4

ADAM_LR = 0.001
ADAM_B1 = 0.9
ADAM_B2 = 0.999
ADAM_EPS = 1e-08
ADAM_WD = 0.01
ADAM_STEP = 10
ADAM_BC1 = 1.0 - ADAM_B1 ** ADAM_STEP
ADAM_BC2 = 1.0 - ADAM_B2 ** ADAM_STEP

DIMS = {
    "nn": (((1,), (0,)), ((), ())),
    "nt": (((1,), (1,)), ((), ())),
    "tn": (((0,), (0,)), ((), ())),
}


def _cp(vmem_mb=48, **kw):
    return pltpu.CompilerParams(vmem_limit_bytes=vmem_mb * 1024 * 1024, **kw)


def _in_hbm(a):
    return pltpu.with_memory_space_constraint(a, pltpu.HBM)


def _call(body, **kw):
    return pl.pallas_call(body, **kw)


def _dot(a, b, mode):
    return lax.dot_general(a.astype(BF16), b.astype(BF16), DIMS[mode], preferred_element_type=F32)


def _sig(x):
    return 1.0 / (1.0 + jnp.exp(-x))


def _row_tile(s):
    return min(512, s // 4)


def _mm_tile(s):
    return min(1024, s // 4)


def _mm(name, mode, a, b, in_specs, out_shape, out_spec, acc_shape, grid, nk, zero_first=False, vmem_mb=48,
        addend=None):
    out_f32 = out_shape.dtype == F32

    def body(a_ref, b_ref, *rest):
        k = pl.program_id(2)
        if addend is None:
            o_ref, scr = rest[0], rest[1:]
        else:
            add_ref, o_ref, scr = rest[0], rest[1], rest[2:]

        def compute():
            part = _dot(a_ref[...], b_ref[...], mode)
            if nk == 1:
                if addend is not None:
                    part = part + addend[0] * add_ref[...]
                o_ref[...] = part.astype(o_ref.dtype)
                return
            acc = o_ref if out_f32 else scr[0]

            @pl.when(k == 0)
            def _():
                acc[...] = part if addend is None else part + addend[0] * add_ref[...]

            @pl.when(k > 0)
            def _():
                acc[...] += part

            if not out_f32:
                @pl.when(k == nk - 1)
                def _():
                    o_ref[...] = acc[...].astype(o_ref.dtype)

        if zero_first:
            @pl.when(pl.program_id(1) == 0)
            def _():
                o_ref[...] = jnp.zeros(o_ref.shape, o_ref.dtype)

            pl.when(pl.program_id(1) > 0)(compute)
        else:
            compute()

    scratch = [] if (nk == 1 or out_f32) else [pltpu.VMEM(acc_shape, F32)]
    operands = [a, b] if addend is None else [a, b, addend[1]]
    specs = list(in_specs) if addend is None else list(in_specs) + [out_spec]
    return _call(
        body, name=name, grid=grid, in_specs=specs, out_specs=out_spec, out_shape=out_shape,
        scratch_shapes=scratch, compiler_params=_cp(vmem_mb),
    )(*operands)


def mm_cols_fwd(name, a, wc, out_dtype, pad_blocks=0, part=(0, 1)):
    s, k = a.shape
    s //= part[1]
    n4 = wc.shape[2]
    tm = _row_tile(s) if pad_blocks else _mm_tile(s)
    nt = s // tm
    first_block = part[0] * nt
    return _mm(
        name, "nn", a, wc,
        [pl.BlockSpec((tm, k), lambda j, i, r: (first_block + jnp.maximum(i - pad_blocks, 0), 0)),
         pl.BlockSpec((None, k, n4), lambda j, i, r: (j, 0, 0))],
        jax.ShapeDtypeStruct((s + pad_blocks * tm, N_SHARD * n4), out_dtype),
        pl.BlockSpec((tm, n4), lambda j, i, r: (i, j)),
        None, (N_SHARD, nt + pad_blocks, 1), 1, zero_first=pad_blocks > 0)


def mm_cols_dx(name, dy, wc, addend=None):
    s = dy.shape[0]
    _, k, n4 = wc.shape
    tm = _mm_tile(s)
    return _mm(
        name, "nt", dy, wc,
        [pl.BlockSpec((tm, n4), lambda g, i, r: (i, r)),
         pl.BlockSpec((None, k, n4), lambda g, i, r: (r, 0, 0))],
        jax.ShapeDtypeStruct((s, k), F32),
        pl.BlockSpec((tm, k), lambda g, i, r: (i, 0)),
        (tm, k), (1, s // tm, N_SHARD), N_SHARD, addend=addend)


def mm_cols_dw(name, a, dy, part=(0, 1)):
    s, k = a.shape
    s //= part[1]
    n4 = dy.shape[1] // N_SHARD
    tm = _mm_tile(s)
    nt = s // tm
    first_block = part[0] * nt
    return _mm(
        name, "tn", a, dy,
        [pl.BlockSpec((tm, k), lambda j, g, r: (first_block + r, 0)),
         pl.BlockSpec((tm, n4), lambda j, g, r: (r, j))],
        jax.ShapeDtypeStruct((N_SHARD, k, n4), F32),
        pl.BlockSpec((None, k, n4), lambda j, g, r: (j, 0, 0)),
        (k, n4), (N_SHARD, 1, nt), nt)


def _k_tile(k):
    return k if k <= 1024 else k // 2


def mm_rows_fwd(name, a, wr, out_dtype=F32):
    s, k = a.shape
    n = wr.shape[1]
    tm = _mm_tile(s)
    tk = _k_tile(k)
    nk = k // tk
    return _mm(
        name, "nn", a, wr,
        [pl.BlockSpec((tm, tk), lambda g, i, r: (i, r)),
         pl.BlockSpec((tk, n), lambda g, i, r: (r, 0))],
        jax.ShapeDtypeStruct((s, n), out_dtype),
        pl.BlockSpec((tm, n), lambda g, i, r: (i, 0)),
        (tm, n), (1, s // tm, nk), nk)


def mm_rows_dx(name, dy, wr, out_dtype=F32):
    s, n = dy.shape
    k = wr.shape[0]
    tm = _mm_tile(s)
    tk = _k_tile(k)
    return _mm(
        name, "nt", dy, wr,
        [pl.BlockSpec((tm, n), lambda j, i, r: (i, 0)),
         pl.BlockSpec((tk, n), lambda j, i, r: (j, 0))],
        jax.ShapeDtypeStruct((s, k), out_dtype),
        pl.BlockSpec((tm, tk), lambda j, i, r: (i, j)),
        None, (k // tk, s // tm, 1), 1)


def mm_rows_dw(name, a, dy):
    s, k = a.shape
    n = dy.shape[1]
    tm = _mm_tile(s)
    tk = _k_tile(k)
    nt = s // tm
    return _mm(
        name, "tn", a, dy,
        [pl.BlockSpec((tm, tk), lambda j, g, r: (r, j)),
         pl.BlockSpec((tm, n), lambda j, g, r: (r, 0))],
        jax.ShapeDtypeStruct((k, n), F32),
        pl.BlockSpec((tk, n), lambda j, g, r: (j, 0)),
        (tk, n), (k // tk, 1, nt), nt)


def _row(tm, c, col=0):
    return pl.BlockSpec((tm, c), lambda i: (i, col))


def _full(shape):
    nd = len(shape)
    return pl.BlockSpec(shape, lambda i: (0,) * nd)


def _prev(tm, h, c, col=0):
    return pl.BlockSpec((h, c), lambda i: (jnp.maximum(i * (tm // h) - 1, 0), col))


def _next(tm, h, c, s, col=0):
    return pl.BlockSpec((h, c), lambda i: (jnp.minimum((i + 1) * (tm // h), s // h - 1), col))


def _acc_add(ref, first, val):
    @pl.when(first)
    def _():
        ref[...] = val

    @pl.when(jnp.logical_not(first))
    def _():
        ref[...] += val


def _colsum(v):
    return jnp.sum(v, axis=0, keepdims=True)


def _ln_stats(z):
    mu = jnp.mean(z, axis=-1, keepdims=True)
    zc = z - mu
    var = jnp.mean(zc * zc, axis=-1, keepdims=True)
    rstd = lax.rsqrt(var + LN_EPS)
    return zc * rstd, rstd


def _ln_bwd(dxhat, xhat, rstd):
    m1 = jnp.mean(dxhat, axis=-1, keepdims=True)
    m2 = jnp.mean(dxhat * xhat, axis=-1, keepdims=True)
    return rstd * (dxhat - m1 - xhat * m2)


def ln_fwd(name, x, f, g, b, ple=None, emit_y=True):
    s, d = x.shape
    tm = _row_tile(s)
    n_in = 2 + (3 if ple is not None else 0)

    def body(*refs):
        x_ref, f_ref = refs[0], refs[1]
        g_ref, b_ref = refs[n_in], refs[n_in + 1]
        xh_ref, rs_ref = refs[-2:]
        z = ALPHA * x_ref[...] + f_ref[...]
        if ple is not None:
            pgl_ref, pp_ref, bg_ref = refs[2:5]
            z = z + _sig(pgl_ref[...] + bg_ref[...]) * pp_ref[...]
        xhat, rstd = _ln_stats(z)
        if emit_y:
            refs[n_in + 2][...] = xhat * g_ref[...] + b_ref[...]
        xh_ref[...] = xhat
        rs_ref[...] = jnp.broadcast_to(rstd, rs_ref.shape)

    ins = [x, f]
    specs = [_row(tm, d), _row(tm, d)]
    if ple is not None:
        pgl, pp, bg = ple
        ins += [pgl, pp, bg]
        specs += [_row(tm, d), _row(tm, d), _full((1, d))]
    ins += [g, b]
    specs += [_full((1, d)), _full((1, d))]
    n_y = 1 if emit_y else 0
    outs = _call(
        body, name=name, grid=(s // tm,), in_specs=specs,
        out_specs=[_row(tm, d)] * (n_y + 1) + [_row(tm, LANES)],
        out_shape=[jax.ShapeDtypeStruct((s, d), F32)] * (n_y + 1) + [jax.ShapeDtypeStruct((s, LANES), F32)],
        compiler_params=_cp(),
    )(*ins)
    return (outs[0], outs[1], outs[2]) if emit_y else (None, outs[0], outs[1])


def ln_bwd(name, parts, xhat, rstd, g, ple=None, loss=None):
    s, d = xhat.shape
    tm = _row_tile(s)
    coefs = [c for c, _ in parts]
    n_p = len(parts)
    n_ple = 3 if ple is not None else 0
    n_in = n_p + 3 + n_ple + (2 if loss is not None else 0)

    def body(*refs):
        first = pl.program_id(0) == 0
        xh = refs[n_p][...]
        rs = refs[n_p + 1][:, 0:1]
        g_v = refs[n_p + 2][...]
        outs = refs[n_in:]
        if loss is not None:
            t_ref, b_ref = refs[n_p + 3 + n_ple:n_p + 5 + n_ple]
            err = (xh * g_v + b_ref[...]) - t_ref[...]
            dy = err * (1.0 / d)
            part = 0.5 * jnp.sum(jnp.mean(err * err, axis=-1, keepdims=True), axis=0, keepdims=True)
            _acc_add(outs[-1], first, jnp.broadcast_to(part, outs[-1].shape))
        else:
            dy = coefs[0] * refs[0][...].astype(F32)
            for j in range(1, n_p):
                dy = dy + coefs[j] * refs[j][...].astype(F32)
        dz = _ln_bwd(dy * g_v, xh, rs)
        outs[0][...] = dz
        _acc_add(outs[1], first, _colsum(dy * xh))
        _acc_add(outs[2], first, _colsum(dy))
        if ple is not None:
            pgl_ref, pp_ref, bg_ref = refs[n_p + 3:n_p + 6]
            pg = _sig(pgl_ref[...] + bg_ref[...])
            dpgl = dz * pp_ref[...] * pg * (1.0 - pg)
            outs[3][...] = (dz * pg).astype(BF16)
            outs[4][...] = dpgl.astype(BF16)
            _acc_add(outs[5], first, _colsum(dpgl))

    ins = [p for _, p in parts] + [xhat, rstd, g]
    specs = [_row(tm, d)] * n_p + [_row(tm, d), _row(tm, LANES), _full((1, d))]
    out_specs = [_row(tm, d), _full((1, d)), _full((1, d))]
    out_shape = [jax.ShapeDtypeStruct((s, d), F32), jax.ShapeDtypeStruct((1, d), F32),
                 jax.ShapeDtypeStruct((1, d), F32)]
    if ple is not None:
        pgl, pp, bg = ple
        ins += [pgl, pp, bg]
        specs += [_row(tm, d), _row(tm, d), _full((1, d))]
        out_specs += [_row(tm, d), _row(tm, d), _full((1, d))]
        out_shape += [jax.ShapeDtypeStruct((s, d), BF16), jax.ShapeDtypeStruct((s, d), BF16),
                      jax.ShapeDtypeStruct((1, d), F32)]
    if loss is not None:
        target, b = loss
        ins += [target, b]
        specs += [_row(tm, d), _full((1, d))]
        out_specs += [_full((8, LANES))]
        out_shape += [jax.ShapeDtypeStruct((8, LANES), F32)]
    return _call(
        body, name=name, grid=(s // tm,), in_specs=specs, out_specs=out_specs, out_shape=out_shape,
        compiler_params=_cp(),
    )(*ins)


def _fill_rotations(rot_ref, x, direction):
    n = x.shape[0]
    rot_ref[0] = x
    for b in range(1, SUBLANES):
        if direction < 0:
            rot_ref[b, SUBLANES:n, :] = x[SUBLANES - b:n - b]
        else:
            rot_ref[b, 0:n - SUBLANES, :] = x[b:n - SUBLANES + b]


def _rotated(rot_ref, start, rows, cs, direction=-1):
    b = (-start) % SUBLANES if direction < 0 else start % SUBLANES
    aligned = start + b if direction < 0 else start - b
    return rot_ref[b, pl.ds(aligned, rows), cs]


def _tile_pos(i, tm, rows):
    return (i * tm + lax.broadcasted_iota(jnp.int32, (rows, 1), 0) + 1).astype(F32)


def mixer_fwd(name, u, pool_w, pool_scale, conv_w, conv_b, cn_g, cn_b):
    s = u.shape[0]
    dp = 512
    tm = min(256, s // 4)
    h = CONV_HALO

    def body(a_c, a_p, bv_c, bv_p, bg_c, bg_p, pw_ref, ps_ref, cw_ref, cb_ref, cg_ref, cbt_ref,
             cat_ref, d_ref, e_ref, glu_ref, hh_ref, rs_ref, ext_a, rot_g, conv_out):
        i = pl.program_id(0)
        first = i == 0
        ext_a[0:h, :] = jnp.where(first, 0.0, a_p[...])
        ext_a[h:, :] = a_c[...]
        glu = bv_c[...] * _sig(bg_c[...])
        glu_ref[...] = glu
        _fill_rotations(rot_g, jnp.concatenate([jnp.where(first, 0.0, bv_p[...] * _sig(bg_p[...])), glu], axis=0), -1)
        pos = _tile_pos(i, tm, tm)
        for gi, w in enumerate(POOL_WINDOWS):
            cs = slice(gi * POOL_GROUP, (gi + 1) * POOL_GROUP)
            a_g = ext_a[pl.ds(h, tm), cs]
            acc = a_g
            for sh in range(1, w):
                acc = acc + ext_a[pl.ds(h - sh, tm), cs]
            d_g = acc / jnp.minimum(pos, float(w)) - a_g
            d_ref[:, cs] = d_g.astype(BF16)
            e_g = _dot(d_g, pw_ref[gi], "nn")
            e_ref[:, cs] = e_g
            cat_ref[:, cs] = (e_g * ps_ref[:, cs]).astype(BF16)
        for lg in range(dp // LANES):
            cs = slice(lg * LANES, (lg + 1) * LANES)
            acc = jnp.broadcast_to(cb_ref[:, cs], (tm, LANES))
            for sh in range(CONV_K):
                acc = acc + _rotated(rot_g, h - sh, tm, cs) * cw_ref[pl.ds(CONV_K - 1 - sh, 1), cs]
            conv_out[:, cs] = acc
        hhat, rstd = _ln_stats(conv_out[...])
        hl = hhat * cg_ref[...] + cbt_ref[...]
        cat_ref[:, dp:] = (hl * _sig(hl)).astype(BF16)
        hh_ref[...] = hhat
        rs_ref[...] = jnp.broadcast_to(rstd, rs_ref.shape)

    specs = [_row(tm, dp, 0), _prev(tm, h, dp, 0), _row(tm, dp, 1), _prev(tm, h, dp, 1),
             _row(tm, dp, 2), _prev(tm, h, dp, 2),
             _full((4, POOL_GROUP, POOL_GROUP)), _full((1, dp)), _full((CONV_K, dp)),
             _full((1, dp)), _full((1, dp)), _full((1, dp))]
    out_specs = [_row(tm, 2 * dp), _row(tm, dp), _row(tm, dp), _row(tm, dp), _row(tm, dp), _row(tm, LANES)]
    out_shape = [jax.ShapeDtypeStruct((s, 2 * dp), BF16), jax.ShapeDtypeStruct((s, dp), BF16),
                 jax.ShapeDtypeStruct((s, dp), F32), jax.ShapeDtypeStruct((s, dp), F32),
                 jax.ShapeDtypeStruct((s, dp), F32), jax.ShapeDtypeStruct((s, LANES), F32)]
    return _call(
        body, name=name, grid=(s // tm,), in_specs=specs, out_specs=out_specs, out_shape=out_shape,
        scratch_shapes=[pltpu.VMEM((h + tm, dp), F32), pltpu.VMEM((SUBLANES, h + tm, dp), F32),
                        pltpu.VMEM((tm, dp), F32)],
        compiler_params=_cp(),
    )(u, u, u, u, u, u, pool_w, pool_scale, conv_w, conv_b, cn_g, cn_b)


def mixer_bwd(name, dcat, u, d_sv, e_sv, glu_sv, hh_sv, rs_sv, pool_w, pool_scale, conv_w, cn_g, cn_b):
    s = u.shape[0]
    dp = 512
    tm = min(256, s // 4)
    h = CONV_HALO
    nt = s // tm

    def body(dc_c, dc_n, bv_c, bg_c, d_c, e_c, gl_c, gl_p, hh_c, hh_n, rs_c, rs_n,
             pw_ref, ps_ref, cw_ref, cg_ref, cbt_ref,
             du_ref, dpw_ref, dps_ref, dcw_ref, dcb_ref, dcg_ref, dcbt_ref,
             ext_dh, ext_g, ext_r):
        i = pl.program_id(0)
        first = i == 0
        last = i == nt - 1
        cg = cg_ref[...]

        def conv_grads(dyb, hhat, rstd):
            hl = hhat * cg + cbt_ref[...]
            sg = _sig(hl)
            dhl = dyb * (sg * (1.0 + hl * (1.0 - sg)))
            return _ln_bwd(dhl * cg, hhat, rstd), dhl

        hh_cur = hh_c[...]
        dh_c, dhl_c = conv_grads(dc_c[:, dp:], hh_cur, rs_c[:, 0:1])
        dh_n, _ = conv_grads(dc_n[:, dp:], hh_n[...], rs_n[:, 0:1])
        _fill_rotations(ext_dh, jnp.concatenate([dh_c, jnp.where(last, 0.0, dh_n)], axis=0), 1)
        _fill_rotations(ext_g, jnp.concatenate([jnp.where(first, 0.0, gl_p[...]), gl_c[...]], axis=0), -1)

        @pl.when(first)
        def _():
            dcw_ref[...] = jnp.zeros(dcw_ref.shape, F32)

        for lg in range(dp // LANES):
            cs = slice(lg * LANES, (lg + 1) * LANES)
            dglu = jnp.zeros((tm, LANES), F32)
            for sh in range(CONV_K):
                dglu = dglu + _rotated(ext_dh, sh, tm, cs, 1) * cw_ref[pl.ds(CONV_K - 1 - sh, 1), cs]
            dh_g = ext_dh[0, pl.ds(0, tm), cs]
            for sh in range(CONV_K):
                dcw_ref[pl.ds(CONV_K - 1 - sh, 1), cs] += _colsum(dh_g * _rotated(ext_g, h - sh, tm, cs))
            sgate = _sig(bg_c[:, cs])
            du_ref[:, dp + lg * LANES:dp + (lg + 1) * LANES] = dglu * sgate
            du_ref[:, 2 * dp + lg * LANES:2 * dp + (lg + 1) * LANES] = dglu * bv_c[:, cs] * sgate * (1.0 - sgate)
        _acc_add(dcb_ref, first, _colsum(dh_c))
        _acc_add(dcg_ref, first, _colsum(dhl_c * hh_cur))
        _acc_add(dcbt_ref, first, _colsum(dhl_c))

        pos_c = _tile_pos(i, tm, tm)
        pos_n = _tile_pos(i + 1, tm, h)
        _acc_add(dps_ref, first, _colsum(dc_c[:, :dp] * e_c[...]))
        for gi, w in enumerate(POOL_WINDOWS):
            cs = slice(gi * POOL_GROUP, (gi + 1) * POOL_GROUP)
            pw = pw_ref[gi]
            de_c = dc_c[:, cs] * ps_ref[:, cs]
            de_n = dc_n[:, cs] * ps_ref[:, cs]
            dd_c = _dot(de_c, pw, "nt")
            dd_n = _dot(de_n, pw, "nt")
            ext_r[0:tm, :] = dd_c / jnp.minimum(pos_c, float(w))
            ext_r[tm:, :] = jnp.where(last, 0.0, dd_n / jnp.minimum(pos_n, float(w)))
            acc = -dd_c
            for sh in range(w):
                acc = acc + ext_r[pl.ds(sh, tm), :]
            du_ref[:, cs] = acc
            dpw_g = _dot(d_c[:, cs], de_c, "tn")

            @pl.when(first)
            def _():
                dpw_ref[gi] = dpw_g

            @pl.when(jnp.logical_not(first))
            def _():
                dpw_ref[gi] += dpw_g

    specs = [_row(tm, 2 * dp), _next(tm, h, 2 * dp, s), _row(tm, dp, 1), _row(tm, dp, 2),
             _row(tm, dp), _row(tm, dp), _row(tm, dp), _prev(tm, h, dp),
             _row(tm, dp), _next(tm, h, dp, s), _row(tm, LANES), _next(tm, h, LANES, s),
             _full((4, POOL_GROUP, POOL_GROUP)), _full((1, dp)), _full((CONV_K, dp)),
             _full((1, dp)), _full((1, dp))]
    out_specs = [_row(tm, 3 * dp), _full((4, POOL_GROUP, POOL_GROUP)), _full((1, dp)), _full((CONV_K, dp)),
                 _full((1, dp)), _full((1, dp)), _full((1, dp))]
    out_shape = [jax.ShapeDtypeStruct((s, 3 * dp), F32),
                 jax.ShapeDtypeStruct((4, POOL_GROUP, POOL_GROUP), F32), jax.ShapeDtypeStruct((1, dp), F32),
                 jax.ShapeDtypeStruct((CONV_K, dp), F32), jax.ShapeDtypeStruct((1, dp), F32),
                 jax.ShapeDtypeStruct((1, dp), F32), jax.ShapeDtypeStruct((1, dp), F32)]
    return _call(
        body, name=name, grid=(nt,), in_specs=specs, out_specs=out_specs, out_shape=out_shape,
        scratch_shapes=[pltpu.VMEM((SUBLANES, tm + h, dp), F32), pltpu.VMEM((SUBLANES, h + tm, dp), F32),
                        pltpu.VMEM((tm + h, POOL_GROUP), F32)],
        compiler_params=_cp(),
    )(dcat, dcat, u, u, d_sv, e_sv, glu_sv, glu_sv, hh_sv, hh_sv, rs_sv, rs_sv,
      pool_w, pool_scale, conv_w, cn_g, cn_b)


GELU_C = math.sqrt(2.0 / math.pi)


def _gelu_parts(x):
    x2 = x * x
    t = jnp.tanh(x * (GELU_C + (GELU_C * 0.044715) * x2))
    half_1pt = 0.5 + 0.5 * t
    gelu = x * half_1pt
    dgelu = half_1pt + (0.5 * x) * (1.0 - t * t) * (GELU_C + (3.0 * GELU_C * 0.044715) * x2)
    return gelu, dgelu


def ffn_act_fwd(name, gv, dw_w, dw_b):
    s = gv.shape[0]
    dff = gv.shape[1] // 2
    tm = min(FFN_TILE, s // 4)
    h = FFN_HALO
    rc = FFN_CHUNK_ROWS
    lw = FFN_CHUNK_LANES

    def body(g_c, g_p, v_c, w_ref, b_ref, hid_ref):
        first = pl.program_id(0) == 0

        def chunk(ci, carry):
            r0 = pl.multiple_of(ci * rc, rc)
            above = pl.multiple_of(jnp.maximum(r0 - h, 0), h)
            for lg in range(dff // lw):
                cs = slice(lg * lw, (lg + 1) * lw)
                top = jnp.where(ci == 0, jnp.where(first, 0.0, g_p[:, cs]), g_c[pl.ds(above, h), cs])
                win = jnp.concatenate([top, g_c[pl.ds(r0, rc), cs]], axis=0)
                gc = jnp.broadcast_to(b_ref[:, cs], (rc, lw))
                for sh in range(FFN_K):
                    gc = gc + win[h - sh:h - sh + rc] * w_ref[pl.ds(FFN_K - 1 - sh, 1), cs]
                gelu, _ = _gelu_parts(gc)
                hid_ref[pl.ds(r0, rc), cs] = (gelu * v_c[pl.ds(r0, rc), cs]).astype(BF16)
            return carry

        lax.fori_loop(0, tm // rc, chunk, 0)

    return _call(
        body, name=name, grid=(s // tm,),
        in_specs=[_row(tm, dff, 0), _prev(tm, h, dff, 0), _row(tm, dff, 1), _full((FFN_K, dff)), _full((1, dff))],
        out_specs=_row(tm, dff), out_shape=jax.ShapeDtypeStruct((s, dff), BF16),
        compiler_params=_cp(),
    )(gv, gv, gv, dw_w, dw_b)


def ffn_act_bwd(name, dhid, gv, dw_w, dw_b):
    s = gv.shape[0]
    dff = gv.shape[1] // 2
    tm = min(FFN_TILE, s // 4)
    h = FFN_HALO
    nt = s // tm
    rc = FFN_CHUNK_ROWS
    lw = FFN_CHUNK_LANES
    n_chunks = tm // rc

    def body(dh_c, dh_n, g_p, g_c, g_n, v_c, v_n, w_ref, b_ref, dgv_ref, dw_ref, db_ref):
        i = pl.program_id(0)
        first = i == 0
        last = i == nt - 1

        @pl.when(first)
        def _():
            dw_ref[...] = jnp.zeros(dw_ref.shape, F32)
            db_ref[...] = jnp.zeros(db_ref.shape, F32)

        def chunk(ci, carry):
            r0 = pl.multiple_of(ci * rc, rc)
            above = pl.multiple_of(jnp.maximum(r0 - h, 0), h)
            below = pl.multiple_of(jnp.minimum(r0 + rc, tm - h), h)
            at_end = ci == n_chunks - 1
            for lg in range(dff // lw):
                cs = slice(lg * lw, (lg + 1) * lw)
                top = jnp.where(ci == 0, jnp.where(first, 0.0, g_p[:, cs]), g_c[pl.ds(above, h), cs])
                bot = jnp.where(at_end, g_n[:, cs], g_c[pl.ds(below, h), cs])
                win = jnp.concatenate([top, g_c[pl.ds(r0, rc), cs], bot], axis=0)
                shifted = [win[h - sh:h - sh + rc + h] for sh in range(FFN_K)]
                gc = jnp.broadcast_to(b_ref[:, cs], (rc + h, lw))
                for sh in range(FFN_K):
                    gc = gc + shifted[sh] * w_ref[pl.ds(FFN_K - 1 - sh, 1), cs]
                gelu, dgelu = _gelu_parts(gc)
                dh_mid = dh_c[pl.ds(r0, rc), cs]
                hv_bot = jnp.where(at_end, jnp.where(last, 0.0, dh_n[:, cs] * v_n[:, cs]),
                                   dh_c[pl.ds(below, h), cs] * v_c[pl.ds(below, h), cs])
                dgc = jnp.concatenate([dh_mid * v_c[pl.ds(r0, rc), cs], hv_bot], axis=0) * dgelu
                dgate = jnp.zeros((rc, lw), F32)
                for sh in range(FFN_K):
                    dgate = dgate + dgc[sh:sh + rc] * w_ref[pl.ds(FFN_K - 1 - sh, 1), cs]
                dgv_ref[pl.ds(r0, rc), cs] = dgate.astype(BF16)
                dgv_ref[pl.ds(r0, rc), slice(dff + lg * lw, dff + (lg + 1) * lw)] = (dh_mid * gelu[0:rc]).astype(BF16)
                dgc_mid = dgc[0:rc]
                for sh in range(FFN_K):
                    dw_ref[pl.ds(FFN_K - 1 - sh, 1), cs] += _colsum(dgc_mid * shifted[sh][0:rc])
                db_ref[:, cs] += _colsum(dgc_mid)
            return carry

        lax.fori_loop(0, n_chunks, chunk, 0)

    return _call(
        body, name=name, grid=(nt,),
        in_specs=[_row(tm, dff), _next(tm, h, dff, s),
                  _prev(tm, h, dff, 0), _row(tm, dff, 0), _next(tm, h, dff, s, 0),
                  _row(tm, dff, 1), _next(tm, h, dff, s, 1),
                  _full((FFN_K, dff)), _full((1, dff))],
        out_specs=[_row(tm, 2 * dff), _full((FFN_K, dff)), _full((1, dff))],
        out_shape=[jax.ShapeDtypeStruct((s, 2 * dff), BF16), jax.ShapeDtypeStruct((FFN_K, dff), F32),
                   jax.ShapeDtypeStruct((1, dff), F32)],
        compiler_params=_cp(),
    )(dhid, dhid, gv, gv, gv, gv, gv, dw_w, dw_b)


def _bias_line(rel_bias):
    nh = rel_bias.shape[0]
    line = jnp.concatenate(
        [jnp.zeros((nh, 1), rel_bias.dtype), jnp.broadcast_to(rel_bias[:, 2 * MAX_REL:], (nh, SHEAR_SAT)),
         jnp.flip(rel_bias[:, 1:2 * MAX_REL], axis=1)], axis=1)
    return line[:, None, :]


def bias_tile(name, line):
    nh = line.shape[0]

    def body(l_ref, o_ref):
        x = jnp.broadcast_to(l_ref[...], (Q_TILE, SHEAR_W))
        z = pltpu.roll(x, SHEAR_W - Q_TILE, 1, stride=1, stride_axis=0)
        qc = lax.broadcasted_iota(jnp.int32, (Q_TILE, K_WIN), 0) // CHUNK
        kc = lax.broadcasted_iota(jnp.int32, (Q_TILE, K_WIN), 1) // CHUNK
        o_ref[...] = jnp.where((kc >= qc) & (kc <= qc + LEFT_CHUNKS), z[:, :K_WIN], NEG_INF)

    return _call(
        body, name=name, grid=(nh,), in_specs=[pl.BlockSpec((None, 1, SHEAR_W), lambda hh: (hh, 0, 0))],
        out_specs=pl.BlockSpec((None, Q_TILE, K_WIN), lambda hh: (hh, 0, 0)),
        out_shape=jax.ShapeDtypeStruct((nh, Q_TILE, K_WIN), F32), compiler_params=_cp(),
    )(line)


def _stack_heads(x2):
    lane = lax.broadcasted_iota(jnp.int32, x2.shape, 1)
    zero = jnp.zeros_like(x2)
    return jnp.concatenate([jnp.where(lane < HEAD_DIM, x2, zero), jnp.where(lane < HEAD_DIM, zero, x2)], axis=0)


def _unstack_heads(x_st):
    lane = lax.broadcasted_iota(jnp.int32, (Q_TILE, LANES), 1)
    return jnp.where(lane < HEAD_DIM, x_st[:Q_TILE], x_st[Q_TILE:])


def _attn_probs(q_st, k3, bias_st, t):
    sc = _dot(q_st, k3, "nt") * (HEAD_DIM ** -0.5) + bias_st
    col = lax.broadcasted_iota(jnp.int32, sc.shape, 1)
    sc = jnp.where(col >= PAD_ROWS - t * Q_TILE, sc, NEG_INF)
    m = jnp.max(sc, axis=-1, keepdims=True)
    p = jnp.exp(sc - m)
    return p * (1.0 / jnp.sum(p, axis=-1, keepdims=True))


def _attn_specs(d_model):
    nq = PAD_ROWS // Q_TILE
    groups = d_model // ATTN_LANES
    specs = [pl.BlockSpec((Q_TILE, ATTN_LANES), lambda g, t: (t + nq, g))]
    for which in (1, 2):
        for j in range(K_WIN // Q_TILE):
            specs.append(pl.BlockSpec((Q_TILE, ATTN_LANES), lambda g, t, j=j, which=which: (t + j, which * groups + g)))
    specs.append(pl.BlockSpec((2 * ATTN_PAIRS, Q_TILE, K_WIN), lambda g, t: (g, 0, 0)))
    return specs


def attn_fwd(name, qkvp, bias):
    s = qkvp.shape[0] - PAD_ROWS
    d_model = qkvp.shape[1] // 3
    nw = K_WIN // Q_TILE

    def body(q_ref, *refs):
        k_refs, v_refs, b_ref, o_ref = refs[:nw], refs[nw:2 * nw], refs[2 * nw], refs[2 * nw + 1]
        t = pl.program_id(1)
        for j in range(ATTN_PAIRS):
            ls = slice(j * LANES, (j + 1) * LANES)
            k3 = jnp.concatenate([r[:, ls] for r in k_refs], axis=0)
            v3 = jnp.concatenate([r[:, ls] for r in v_refs], axis=0)
            bias_st = b_ref[2 * j:2 * j + 2].reshape(2 * Q_TILE, K_WIN)
            p = _attn_probs(_stack_heads(q_ref[:, ls]), k3, bias_st, t)
            o_ref[:, ls] = _unstack_heads(_dot(p, v3, "nn")).astype(BF16)

    return _call(
        body, name=name, grid=(d_model // ATTN_LANES, s // Q_TILE),
        in_specs=_attn_specs(d_model), out_specs=pl.BlockSpec((Q_TILE, ATTN_LANES), lambda g, t: (t, g)),
        out_shape=jax.ShapeDtypeStruct((s, d_model), BF16), compiler_params=_cp(),
    )(qkvp, *([qkvp] * (2 * nw)), bias)


def attn_bwd(name, qkvp, bias, do):
    s = qkvp.shape[0] - PAD_ROWS
    d_model = qkvp.shape[1] // 3
    nw = K_WIN // Q_TILE
    nt = s // Q_TILE
    scale = HEAD_DIM ** -0.5

    def body(q_ref, *refs):
        k_refs, v_refs = refs[:nw], refs[nw:2 * nw]
        b_ref, do_ref, dq_ref, dk_ref, dv_ref, ds_ref, dk_acc, dv_acc = refs[2 * nw:]
        t = pl.program_id(1)
        first = t == 0

        @pl.when(first)
        def _():
            dk_acc[...] = jnp.zeros(dk_acc.shape, F32)
            dv_acc[...] = jnp.zeros(dv_acc.shape, F32)

        start = pl.multiple_of(t * Q_TILE, Q_TILE)
        for j in range(ATTN_PAIRS):
            ls = slice(j * LANES, (j + 1) * LANES)
            q_st = _stack_heads(q_ref[:, ls])
            do_st = _stack_heads(do_ref[:, ls])
            k3 = jnp.concatenate([r[:, ls] for r in k_refs], axis=0)
            v3 = jnp.concatenate([r[:, ls] for r in v_refs], axis=0)
            p = _attn_probs(q_st, k3, b_ref[2 * j:2 * j + 2].reshape(2 * Q_TILE, K_WIN), t)
            dp = _dot(do_st, v3, "nt")
            ds = p * (dp - jnp.sum(p * dp, axis=-1, keepdims=True))
            _acc_add(ds_ref.at[2 * j:2 * j + 2], first, ds.reshape(2, Q_TILE, K_WIN))
            dsb = (ds * scale).astype(BF16)
            dq_ref[:, ls] = _unstack_heads(_dot(dsb, k3, "nn")).astype(BF16)
            dk_acc[pl.ds(start, K_WIN), ls] += _dot(dsb, q_st, "tn")
            dv_acc[pl.ds(start, K_WIN), ls] += _dot(p, do_st, "tn")

        @pl.when(t == nt - 1)
        def _():
            dk_ref[...] = dk_acc[pl.ds(PAD_ROWS, s), :].astype(BF16)
            dv_ref[...] = dv_acc[pl.ds(PAD_ROWS, s), :].astype(BF16)

    specs = _attn_specs(d_model) + [pl.BlockSpec((Q_TILE, ATTN_LANES), lambda g, t: (t, g))]
    col_spec = pl.BlockSpec((s, ATTN_LANES), lambda g, t: (0, g))
    return _call(
        body, name=name, grid=(d_model // ATTN_LANES, nt), in_specs=specs,
        out_specs=[pl.BlockSpec((Q_TILE, ATTN_LANES), lambda g, t: (t, g)), col_spec, col_spec,
                   pl.BlockSpec((2 * ATTN_PAIRS, Q_TILE, K_WIN), lambda g, t: (g, 0, 0))],
        out_shape=[jax.ShapeDtypeStruct((s, d_model), BF16)] * 3
        + [jax.ShapeDtypeStruct((N_HEADS, Q_TILE, K_WIN), F32)],
        scratch_shapes=[pltpu.VMEM((PAD_ROWS + s, ATTN_LANES), F32), pltpu.VMEM((PAD_ROWS + s, ATTN_LANES), F32)],
        compiler_params=_cp(),
    )(qkvp, *([qkvp] * (2 * nw)), bias, do)


def _shear_for_bias_grad(ds_sum):
    nh = ds_sum.shape[0]
    z = jnp.pad(ds_sum, ((0, 0), (0, 0), (Q_TILE, 0))).reshape(nh, Q_TILE * SHEAR_W)
    return jnp.pad(z, ((0, 0), (0, Q_TILE))).reshape(nh, Q_TILE, SHEAR_W + 1)


def bias_grad_reduce(name, sheared):
    nh, _, width = sheared.shape

    def body(x_ref, col_ref, sat_ref):
        cols = _colsum(x_ref[...])
        col_ref[...] = cols
        k = lax.broadcasted_iota(jnp.int32, cols.shape, 1)
        tot = jnp.sum(jnp.where((k >= 1) & (k <= SHEAR_SAT), cols, 0.0), axis=-1, keepdims=True)
        sat_ref[...] = jnp.broadcast_to(tot, sat_ref.shape)

    return _call(
        body, name=name, grid=(nh,),
        in_specs=[pl.BlockSpec((None, Q_TILE, width), lambda hh: (hh, 0, 0))],
        out_specs=[pl.BlockSpec((None, 1, width), lambda hh: (hh, 0, 0)),
                   pl.BlockSpec((None, 1, LANES), lambda hh: (hh, 0, 0))],
        out_shape=[jax.ShapeDtypeStruct((nh, 1, width), F32), jax.ShapeDtypeStruct((nh, 1, LANES), F32)],
        compiler_params=_cp(),
    )(sheared)


def _ew_rows(r, most=512):
    for cand in (512, 256, 128, 64, 32, 16, 8):
        if cand <= most and r % cand == 0:
            return cand
    return r


def cast_into_gathered(name, w, layer, s_idx, n_blocks=N_SHARD, dtype=BF16, token=None):
    r, c = w.shape[-2:]
    tr = _ew_rows(r)

    def body(s_ref, w_ref, *rest):
        rest[-1][...] = w_ref[...].astype(dtype)

    extra = [] if token is None else [token]
    grid_spec = pltpu.PrefetchScalarGridSpec(
        num_scalar_prefetch=1, grid=(r // tr,),
        in_specs=[pl.BlockSpec((None, tr, c), lambda i, s_ref: (layer, i, 0))] + [ANY_SPEC] * len(extra),
        out_specs=pl.BlockSpec((None, tr, c), lambda i, s_ref: (s_ref[0], i, 0)))
    return _call(
        body, name=name, grid_spec=grid_spec, out_shape=jax.ShapeDtypeStruct((n_blocks, r, c), dtype),
        compiler_params=_cp(),
    )(s_idx, w, *extra)


def adamw(name, w, grads, m, v, token=None):
    nl, r, c = w.shape
    tr = _ew_rows(r, 256)

    def body(*refs):
        w_ref, m_ref, v_ref = refs[0], refs[1], refs[2]
        g_refs = refs[3:3 + nl]
        d_ref, nm_ref, nv_ref = refs[-3:]
        layer = pl.program_id(0)
        g = g_refs[0][...]
        for j in range(1, nl):
            g = jnp.where(layer == j, g_refs[j][...], g)
        nm = ADAM_B1 * m_ref[...] + (1.0 - ADAM_B1) * g
        nv = ADAM_B2 * v_ref[...] + (1.0 - ADAM_B2) * (g * g)
        m_hat = nm / ADAM_BC1
        v_hat = nv / ADAM_BC2
        d_ref[...] = -ADAM_LR * (m_hat / (jnp.sqrt(v_hat) + ADAM_EPS) + ADAM_WD * w_ref[...])
        nm_ref[...] = nm
        nv_ref[...] = nv

    p_spec = pl.BlockSpec((None, tr, c), lambda l, i: (l, i, 0))
    g_spec = pl.BlockSpec((tr, c), lambda l, i: (i, 0))
    extra = [] if token is None else [token]
    extra_specs = [] if token is None else [ANY_SPEC]
    return _call(
        body, name=name, grid=(nl, r // tr), in_specs=[p_spec] * 3 + [g_spec] * nl + extra_specs,
        out_specs=[p_spec] * 3, out_shape=[jax.ShapeDtypeStruct((nl, r, c), F32)] * 3, compiler_params=_cp(),
    )(w, m, v, *grads, *extra)


def sum_blocks(name, gathered, n_blocks):
    r = gathered.shape[0] // n_blocks
    c = gathered.shape[1]
    tr = _ew_rows(r)
    nt = r // tr

    def body(*refs):
        acc = refs[0][...]
        for j in range(1, n_blocks):
            acc = acc + refs[j][...]
        refs[-1][...] = acc

    specs = [pl.BlockSpec((tr, c), lambda i, j=j: (j * nt + i, 0)) for j in range(n_blocks)]
    return _call(
        body, name=name, grid=(nt,), in_specs=specs, out_specs=pl.BlockSpec((tr, c), lambda i: (i, 0)),
        out_shape=jax.ShapeDtypeStruct((r, c), F32), compiler_params=_cp(),
    )(*([gathered] * n_blocks))


def _place():
    return lax.axis_index("x"), lax.axis_index("y"), lax.axis_index("c")


def _other_chips(x, y):
    return [(1 - x, y), (x, 1 - y), (1 - x, 1 - y)]


HBM_SPEC = pl.BlockSpec(memory_space=pltpu.HBM)
SEM_SPEC = pl.BlockSpec(memory_space=pltpu.SEMAPHORE)
ANY_SPEC = pl.BlockSpec(memory_space=pl.ANY)
EFFECT = pltpu.SideEffectType.DATAFLOW_SIDE_EFFECTING


def copies_start(name, bufs, plan, n_copies):
    n = len(bufs)

    def body(*refs):
        send, recv = refs[n], refs[n + 1]
        token = refs[2 * n + 2]
        for k, (src, dst, peer, _) in enumerate(plan(refs[:n])):
            pltpu.make_async_remote_copy(
                src_ref=src, dst_ref=dst, send_sem=send.at[k], recv_sem=recv.at[k],
                device_id=peer, device_id_type=MESH).start()
        token[...] = jnp.zeros(token.shape, F32)

    outs = pl.pallas_call(
        body, name=name,
        out_shape=(pltpu.SemaphoreType.DMA((n_copies,)), pltpu.SemaphoreType.DMA((n_copies,)),
                   *[pltpu.HBM(b.shape, b.dtype) for b in bufs], jax.ShapeDtypeStruct((8, LANES), F32)),
        in_specs=[HBM_SPEC] * n,
        out_specs=(SEM_SPEC, SEM_SPEC, *([HBM_SPEC] * n), pl.BlockSpec(memory_space=pltpu.VMEM)),
        input_output_aliases={a: a + 2 for a in range(n)},
        compiler_params=pltpu.CompilerParams(has_side_effects=EFFECT),
    )(*[_in_hbm(b) for b in bufs])
    return outs[0], outs[1], list(outs[2:2 + n]), outs[2 + n]


def copies_wait(name, bufs, send, recv, plan, sem_base, after):
    n = len(bufs)

    def body(*refs):
        send_ref, recv_ref = refs[n], refs[n + 1]
        for k, (src, _, peer, land) in enumerate(plan(refs[:n])):
            cp = pltpu.make_async_remote_copy(
                src_ref=src, dst_ref=land, send_sem=send_ref.at[sem_base + k], recv_sem=recv_ref.at[sem_base + k],
                device_id=peer, device_id_type=MESH)
            cp.wait_send()
            cp.wait_recv()

    outs = pl.pallas_call(
        body, name=name,
        out_shape=tuple(pltpu.HBM(b.shape, b.dtype) for b in bufs),
        in_specs=[HBM_SPEC] * n + [SEM_SPEC, SEM_SPEC, ANY_SPEC], out_specs=tuple([HBM_SPEC] * n),
        input_output_aliases={a: a for a in range(n)},
        compiler_params=pltpu.CompilerParams(has_side_effects=EFFECT),
    )(*bufs, send, recv, after)
    return list(outs)


def gather_plan(refs):
    x, y, c = _place()
    me = 2 * x + y
    return [(buf.at[me], buf.at[me], (cx, cy, c), buf.at[2 * cx + cy])
            for buf in refs for cx, cy in _other_chips(x, y)]


def all_plan(refs):
    x, y, c = _place()
    me = 4 * x + 2 * y + c
    out = []
    for buf in refs:
        for flip in range(1, 8):
            px = 1 - x if flip & 4 else x
            py = 1 - y if flip & 2 else y
            pc = 1 - c if flip & 1 else c
            out.append((buf.at[me], buf.at[me], (px, py, pc), buf.at[4 * px + 2 * py + pc]))
    return out


def swap_plan(refs):
    x, y, c = _place()
    n = len(refs) // 2
    out = []
    for g, land in zip(refs[:n], refs[n:]):
        hr = g.shape[1] // 2
        out.append((g.at[:, pl.ds((1 - c) * hr, hr)], land, (x, y, 1 - c), land))
    return out


def owners_plan(refs):
    x, y, c = _place()
    n = len(refs) // 2
    return [(src.at[2 * cx + cy], land.at[j], (cx, cy, c), land.at[j])
            for src, land in zip(refs[:n], refs[n:]) for j, (cx, cy) in enumerate(_other_chips(x, y))]


def join_plan(refs):
    x, y, c = _place()
    out = []
    for buf in refs:
        hr = buf.shape[0] // 2
        mine = buf.at[pl.ds(c * hr, hr)]
        out.append((mine, mine, (x, y, 1 - c), buf.at[pl.ds((1 - c) * hr, hr)]))
    return out


def add_halves(name, grad, landed, c_idx):
    _, r, c = grad.shape
    hr = r // 2
    tr = _ew_rows(hr)
    nt = hr // tr

    def body(c_ref, g_ref, l_ref, o_ref, ob_ref):
        tot = g_ref[...] + l_ref[...]
        o_ref[...] = tot
        ob_ref[...] = tot.astype(BF16)

    blk = pl.BlockSpec((None, tr, c), lambda sh, i, c_ref: (sh, i, 0))
    grid_spec = pltpu.PrefetchScalarGridSpec(
        num_scalar_prefetch=1, grid=(N_SHARD, nt),
        in_specs=[pl.BlockSpec((None, tr, c), lambda sh, i, c_ref: (sh, c_ref[0] * nt + i, 0)), blk],
        out_specs=[blk, blk])
    return _call(
        body, name=name, grid_spec=grid_spec,
        out_shape=[jax.ShapeDtypeStruct((N_SHARD, hr, c), F32), jax.ShapeDtypeStruct((N_SHARD, hr, c), BF16)],
        compiler_params=_cp(),
    )(c_idx, grad, landed)


def add_owned(name, own, landed, sc_idx):
    _, hr, c = own.shape
    tr = _ew_rows(hr)
    nt = hr // tr

    def body(sc_ref, o_ref, l0, l1, l2, out_ref):
        out_ref[...] = ((o_ref[...] + l0[...].astype(F32)) + l1[...].astype(F32)) + l2[...].astype(F32)

    grid_spec = pltpu.PrefetchScalarGridSpec(
        num_scalar_prefetch=1, grid=(nt,),
        in_specs=[pl.BlockSpec((None, tr, c), lambda i, sc_ref: (sc_ref[0], i, 0))]
        + [pl.BlockSpec((None, tr, c), lambda i, sc_ref, j=j: (j, i, 0)) for j in range(3)],
        out_specs=pl.BlockSpec((tr, c), lambda i, sc_ref: (sc_ref[1] * nt + i, 0)))
    return _call(
        body, name=name, grid_spec=grid_spec, out_shape=jax.ShapeDtypeStruct((2 * hr, c), F32),
        compiler_params=_cp(),
    )(sc_idx, own, landed, landed, landed)


PACK_QUANTUM = 8 * LANES


def _pack(arrays):
    pieces = []
    for a in arrays:
        flat = a.reshape(-1)
        padded = -(-flat.shape[0] // PACK_QUANTUM) * PACK_QUANTUM
        pieces.append(jnp.pad(flat, (0, padded - flat.shape[0])).reshape(-1, LANES))
    return jnp.concatenate(pieces, axis=0)


def _unpack(packed, shapes):
    out = []
    row = 0
    for shp in shapes:
        size = math.prod(shp)
        rows = -(-size // PACK_QUANTUM) * 8
        out.append(packed[row:row + rows].reshape(-1)[:size].reshape(shp))
        row += rows
    return out


def kernel(x, p, mix_w_in, pool_w, pool_scale, conv_dw_w, conv_dw_b, conv_ln_g, conv_ln_b, mix_w_out, attn_w_qkv, attn_rel_bias, attn_w_o, ln_mix_g, ln_mix_b, ffn_w_up, ffn_dw_w, ffn_dw_b, ffn_w_down, ple_w_proj, ple_w_gate, ple_b_gate, ln_ffn_g, ln_ffn_b, loss_target, m_mix_w_in, m_pool_w, m_pool_scale, m_conv_dw_w, m_conv_dw_b, m_conv_ln_g, m_conv_ln_b, m_mix_w_out, m_attn_w_qkv, m_attn_rel_bias, m_attn_w_o, m_ln_mix_g, m_ln_mix_b, m_ffn_w_up, m_ffn_dw_w, m_ffn_dw_b, m_ffn_w_down, m_ple_w_proj, m_ple_w_gate, m_ple_b_gate, m_ln_ffn_g, m_ln_ffn_b, v_mix_w_in, v_pool_w, v_pool_scale, v_conv_dw_w, v_conv_dw_b, v_conv_ln_g, v_conv_ln_b, v_mix_w_out, v_attn_w_qkv, v_attn_rel_bias, v_attn_w_o, v_ln_mix_g, v_ln_mix_b, v_ffn_w_up, v_ffn_dw_w, v_ffn_dw_b, v_ffn_w_down, v_ple_w_proj, v_ple_w_gate, v_ple_b_gate, v_ln_ffn_g, v_ln_ffn_b):
    xi, yi, ci = _place()
    shard_idx = (2 * xi + yi).astype(jnp.int32)
    s_arr = shard_idx.reshape(1)
    c_arr = ci.astype(jnp.int32).reshape(1)
    sc_arr = jnp.concatenate([s_arr, c_arr])

    x0 = x[0]
    target = loss_target[0]
    p_rows = p.reshape(p.shape[0] * p.shape[2], p.shape[3])
    seq = x0.shape[0]

    big = [
        ("mix_w_in", mix_w_in, m_mix_w_in, v_mix_w_in, True),
        ("mix_w_out", mix_w_out, m_mix_w_out, v_mix_w_out, False),
        ("attn_w_qkv", attn_w_qkv, m_attn_w_qkv, v_attn_w_qkv, True),
        ("attn_w_o", attn_w_o, m_attn_w_o, v_attn_w_o, False),
        ("ffn_w_up", ffn_w_up, m_ffn_w_up, v_ffn_w_up, True),
        ("ffn_w_down", ffn_w_down, m_ffn_w_down, v_ffn_w_down, False),
        ("ple_w_proj", ple_w_proj, m_ple_w_proj, v_ple_w_proj, True),
        ("ple_w_gate", ple_w_gate, m_ple_w_gate, v_ple_w_gate, False),
    ]
    params = {nm: w for nm, w, _, _, _ in big}
    col_sharded = {nm: cs for nm, _, _, _, cs in big}
    keys = [("mix_w_in", 0), ("mix_w_out", 0), ("ffn_w_up", 0), ("ffn_w_down", 0), ("ple_w_gate", 0),
            ("ple_w_proj", 0), ("attn_w_qkv", 0), ("attn_w_o", 0), ("ffn_w_up", 1), ("ffn_w_down", 1),
            ("ple_w_gate", 1), ("ple_w_proj", 1)]
    dw_shapes = [conv_dw_w.shape, ffn_dw_w.shape]
    dw_block = cast_into_gathered("place_dw", _pack([conv_dw_w, ffn_dw_w])[None], 0, s_arr, dtype=F32)
    n_first = 2
    started = {}
    gather_token = None
    for tag, group in (("first", keys[:n_first]), ("rest", keys[n_first:])):
        shards = [cast_into_gathered(f"cast_{nm}_{layer}", params[nm], layer, s_arr, token=gather_token)
                  for nm, layer in group]
        if tag == "first":
            shards.append(dw_block)
        send, recv, bufs, gather_token = copies_start(f"gather_start_{tag}", shards, gather_plan, 3 * len(shards))
        for a, key in enumerate(group):
            started[key] = (send, recv, bufs[a], 3 * a)
        if tag == "first":
            dw_started = (send, recv, bufs[-1], 3 * len(group))
    arrived_w = {}

    def weight(nm, layer, after=None):
        key = (nm, layer)
        if key not in arrived_w:
            send, recv, buf, base = started[key]
            arrived_w[key] = copies_wait(f"gather_wait_{nm}_{layer}", [buf], send, recv, gather_plan, base, after)[0]
        g = arrived_w[key]
        if col_sharded[nm]:
            return g
        return g.reshape(g.shape[0] * g.shape[1], g.shape[2])

    def tie(a, token):
        return a + token[0:1, 0:1].astype(a.dtype)

    class Reducer:
        def __init__(self, tag, group):
            self.tag, self.group, self.stage = tag, group, 0
            self.n = len(group)
            self.result = None

        def advance(self, after):
            tag, n = self.tag, self.n
            if self.stage == 0:
                grads = []
                for key in self.group:
                    g = big_grads[key]
                    grads.append(g if g.ndim == 3 else g.reshape(N_SHARD, g.shape[0] // N_SHARD, g.shape[1]))
                lands = [lax.empty((N_SHARD, g.shape[1] // 2, g.shape[2]), F32) for g in grads]
                self.sems = copies_start(f"swap_start_{tag}", grads + lands, swap_plan, n)
            elif self.stage == 1:
                send, recv, bufs, _ = self.sems
                outs = copies_wait(f"swap_wait_{tag}", bufs, send, recv, swap_plan, 0, after)
                self.own, wire = [], []
                for key, g, ld in zip(self.group, outs[:n], outs[n:]):
                    o, ob = add_halves(f"add_halves_{key[0]}_{key[1]}", g, ld, c_arr)
                    self.own.append(o)
                    wire.append(ob)
                lands = [lax.empty((3,) + w.shape[1:], BF16) for w in wire]
                self.sems = copies_start(f"owners_start_{tag}", wire + lands, owners_plan, 3 * n)
            elif self.stage == 2:
                send, recv, bufs, _ = self.sems
                outs = copies_wait(f"owners_wait_{tag}", bufs, send, recv, owners_plan, 0, after)
                finals = [add_owned(f"add_owned_{key[0]}_{key[1]}", o, ar, sc_arr)
                          for key, o, ar in zip(self.group, self.own, outs[n:])]
                self.sems = copies_start(f"join_start_{tag}", finals, join_plan, n)
            elif self.stage == 3:
                send, recv, bufs, _ = self.sems
                outs = copies_wait(f"join_wait_{tag}", bufs, send, recv, join_plan, 0, after)
                self.result = dict(zip(self.group, outs))
                self.sems = None
            self.stage += 1
            return None if self.sems is None else self.sems[3]

    dw_cache = []

    def conv_weights(after):
        if not dw_cache:
            send, recv, buf, base = dw_started
            dw_all = copies_wait("gather_wait_dw", [buf], send, recv, gather_plan, base, after)[0]
            dw_parts = [_unpack(dw_all[k], dw_shapes) for k in range(N_SHARD)]
            dw_cache.append(jnp.concatenate([pc[0] for pc in dw_parts], axis=2)[0])
            dw_cache.append(jnp.concatenate([pc[1] for pc in dw_parts], axis=2))
        return dw_cache

    big_grads = {}
    small_grads = {}

    saved = []
    h_in = x0
    for layer in range(N_LAYERS):
        sv = {"x_in": h_in}
        if layer % 2 == 0:
            u = mm_cols_fwd("mix_in", h_in, weight("mix_w_in", 0, gather_token), F32)
            conv_w_full, ffn_dw_full = conv_weights(u)
            cat, d_sv, e_sv, glu_sv, hh_sv, rs_sv = mixer_fwd(
                "mixer_fwd", u, pool_w[0], pool_scale, conv_w_full, conv_dw_b, conv_ln_g, conv_ln_b)
            mix = mm_rows_fwd("mix_out", cat, weight("mix_w_out", 0, cat))
            sv.update(u=u, cat=cat, d=d_sv, e=e_sv, glu=glu_sv, hh=hh_sv, rs=rs_sv)
        else:
            qkvp = mm_cols_fwd("attn_qkv", h_in, weight("attn_w_qkv", 0, h_in), BF16,
                               pad_blocks=PAD_ROWS // _row_tile(seq))
            bias = bias_tile("bias_tile", _bias_line(attn_rel_bias[0]))
            att = attn_fwd("attn_fwd", qkvp, bias)
            mix = mm_rows_fwd("attn_out", att, weight("attn_w_o", 0, att))
            sv.update(qkvp=qkvp, bias=bias, att=att)
        x1, xh1, rs1 = ln_fwd(f"ln_mix_{layer}", h_in, mix, ln_mix_g[layer:layer + 1], ln_mix_b[layer:layer + 1])
        gv = mm_cols_fwd(f"ffn_up_{layer}", x1, weight("ffn_w_up", layer, x1), F32)
        hid = ffn_act_fwd(f"ffn_act_{layer}", gv, ffn_dw_full[layer], ffn_dw_b[layer:layer + 1])
        ffn = mm_rows_fwd(f"ffn_down_{layer}", hid, weight("ffn_w_down", layer, hid))
        pgl = mm_rows_fwd(f"ple_gate_{layer}", x1, weight("ple_w_gate", layer, ffn))
        pp = mm_cols_fwd(f"ple_proj_{layer}", p_rows, weight("ple_w_proj", layer, pgl), F32, part=(layer, N_LAYERS))
        bg = ple_b_gate[layer:layer + 1]
        x2, xh2, rs2 = ln_fwd(f"ln_ffn_{layer}", x1, ffn, ln_ffn_g[layer:layer + 1], ln_ffn_b[layer:layer + 1],
                              ple=(pgl, pp, bg), emit_y=layer < N_LAYERS - 1)
        sv.update(x1=x1, xh1=xh1, rs1=rs1, gv=gv, hid=hid, pgl=pgl, pp=pp, xh2=xh2, rs2=rs2)
        saved.append(sv)
        h_in = x2

    reducers = []

    def open_group(tag, group):
        reducers.append(Reducer(tag, group))
        return reducers[-1].advance(None)

    def hook(after):
        token = None
        for red in reducers:
            if red.stage < 4:
                tk = red.advance(after)
                if tk is not None:
                    token = tk if token is None else token + tk
        return token

    def tied(a, token):
        return a if token is None else tie(a, token)

    parts = []
    token = None
    for layer in reversed(range(N_LAYERS)):
        sv = saved[layer]
        bg = ple_b_gate[layer:layer + 1]
        if layer == 0:
            token = open_group("layer1", [("attn_w_qkv", 0), ("attn_w_o", 0), ("ffn_w_up", 1), ("ffn_w_down", 1),
                                          ("ple_w_gate", 1), ("ple_w_proj", 1)])
        last = layer == N_LAYERS - 1
        res = ln_bwd(
            f"ln_ffn_bwd_{layer}", parts, sv["xh2"], sv["rs2"], tied(ln_ffn_g[layer:layer + 1], token),
            ple=(sv["pgl"], sv["pp"], bg), loss=(target, ln_ffn_b[layer:layer + 1]) if last else None)
        dz2, dg2, db2, dpp, dpgl, dbg = res[:6]
        if last:
            loss_part = res[6]
        small_grads[("ln_ffn_g", layer)] = dg2
        small_grads[("ln_ffn_b", layer)] = db2
        small_grads[("ple_b_gate", layer)] = dbg
        w_down = weight("ffn_w_down", layer)
        dhid = mm_rows_dx(f"ffn_down_dx_{layer}", dz2, w_down)
        big_grads[("ffn_w_down", layer)] = mm_rows_dw(f"ffn_down_dw_{layer}", sv["hid"], dz2)
        token = hook(big_grads[("ffn_w_down", layer)])
        dgv, ddw, ddb = ffn_act_bwd(f"ffn_act_bwd_{layer}", dhid, sv["gv"], ffn_dw_full[layer],
                                    tied(ffn_dw_b[layer:layer + 1], token))
        small_grads[("ffn_dw_w", layer)] = ddw
        small_grads[("ffn_dw_b", layer)] = ddb
        big_grads[("ffn_w_up", layer)] = mm_cols_dw(f"ffn_up_dw_{layer}", sv["x1"], dgv)
        t_up = mm_cols_dx(f"ffn_up_dx_{layer}", dgv, weight("ffn_w_up", layer))
        token = hook(t_up)
        big_grads[("ple_w_gate", layer)] = mm_rows_dw(f"ple_gate_dw_{layer}", sv["x1"], dpgl)
        t_gate = mm_rows_dx(f"ple_gate_dx_{layer}", dpgl, weight("ple_w_gate", layer))
        big_grads[("ple_w_proj", layer)] = mm_cols_dw(f"ple_proj_dw_{layer}", p_rows, dpp, part=(layer, N_LAYERS))
        token2 = hook(big_grads[("ple_w_proj", layer)])
        if token2 is not None:
            token = token2 if token is None else token + token2
        if layer == 0:
            token3 = open_group("layer0_ffn", [("ffn_w_up", 0), ("ffn_w_down", 0), ("ple_w_gate", 0), ("ple_w_proj", 0)])
            token = token3 if token is None else token + token3
        dz1, dg1, db1 = ln_bwd(
            f"ln_mix_bwd_{layer}", [(ALPHA, dz2), (1.0, t_up), (1.0, t_gate)], sv["xh1"], sv["rs1"],
            tied(ln_mix_g[layer:layer + 1], token))
        small_grads[("ln_mix_g", layer)] = dg1
        small_grads[("ln_mix_b", layer)] = db1
        if layer % 2 == 0:
            dcat = mm_rows_dx("mix_out_dx", dz1, weight("mix_w_out", 0))
            big_grads[("mix_w_out", 0)] = mm_rows_dw("mix_out_dw", sv["cat"], dz1)
            token = hook(big_grads[("mix_w_out", 0)])
            du, dpw, dps, dcw, dcb, dcg, dcbt = mixer_bwd(
                "mixer_bwd", dcat, sv["u"], sv["d"], sv["e"], sv["glu"], sv["hh"], sv["rs"],
                pool_w[0], pool_scale, conv_w_full, tied(conv_ln_g, token), conv_ln_b)
            small_grads[("pool_w", 0)] = dpw
            small_grads[("pool_scale", 0)] = dps
            small_grads[("conv_dw_w", 0)] = dcw
            small_grads[("conv_dw_b", 0)] = dcb
            small_grads[("conv_ln_g", 0)] = dcg
            small_grads[("conv_ln_b", 0)] = dcbt
            big_grads[("mix_w_in", 0)] = mm_cols_dw("mix_in_dw", sv["x_in"], du)
            hook(big_grads[("mix_w_in", 0)])
            open_group("layer0_mix", [("mix_w_in", 0), ("mix_w_out", 0)])
            dx_in = mm_cols_dx("mix_in_dx", du, weight("mix_w_in", 0), addend=(ALPHA, dz1))
            token = hook(dx_in)
        else:
            do = mm_rows_dx("attn_out_dx", dz1, weight("attn_w_o", 0), out_dtype=BF16)
            big_grads[("attn_w_o", 0)] = mm_rows_dw("attn_out_dw", sv["att"], dz1)
            dq, dk, dv, ds_sum = attn_bwd("attn_bwd", sv["qkvp"], sv["bias"], do)
            cols, sat = bias_grad_reduce("bias_grad", _shear_for_bias_grad(ds_sum))
            d_rel = jnp.concatenate(
                [jnp.zeros((N_HEADS, 1), F32),
                 jnp.flip(cols[:, 0, SHEAR_SAT + 1:SHEAR_W], axis=1),
                 sat[:, 0, 0:1]], axis=1)
            small_grads[("attn_rel_bias", 0)] = d_rel
            dqkv = jnp.concatenate([dq, dk, dv], axis=1)
            big_grads[("attn_w_qkv", 0)] = mm_cols_dw("attn_qkv_dw", sv["x_in"], dqkv)
            dx_in = mm_cols_dx("attn_qkv_dx", dqkv, weight("attn_w_qkv", 0), addend=(ALPHA, dz1))
        parts = [(1.0, dx_in)]
    grad_x = dx_in

    small = [
        ("pool_w", pool_w, m_pool_w, v_pool_w, None),
        ("pool_scale", pool_scale, m_pool_scale, v_pool_scale, None),
        ("conv_dw_w", conv_dw_w, m_conv_dw_w, v_conv_dw_w, 2),
        ("conv_dw_b", conv_dw_b, m_conv_dw_b, v_conv_dw_b, None),
        ("conv_ln_g", conv_ln_g, m_conv_ln_g, v_conv_ln_g, None),
        ("conv_ln_b", conv_ln_b, m_conv_ln_b, v_conv_ln_b, None),
        ("attn_rel_bias", attn_rel_bias, m_attn_rel_bias, v_attn_rel_bias, None),
        ("ln_mix_g", ln_mix_g, m_ln_mix_g, v_ln_mix_g, None),
        ("ln_mix_b", ln_mix_b, m_ln_mix_b, v_ln_mix_b, None),
        ("ffn_dw_w", ffn_dw_w, m_ffn_dw_w, v_ffn_dw_w, 2),
        ("ffn_dw_b", ffn_dw_b, m_ffn_dw_b, v_ffn_dw_b, None),
        ("ple_b_gate", ple_b_gate, m_ple_b_gate, v_ple_b_gate, None),
        ("ln_ffn_g", ln_ffn_g, m_ln_ffn_g, v_ln_ffn_g, None),
        ("ln_ffn_b", ln_ffn_b, m_ln_ffn_b, v_ln_ffn_b, None),
    ]
    full_grads = []
    for nm, w, _, _, shard_axis in small:
        full = list(w.shape)
        if shard_axis is not None:
            full[shard_axis] *= N_SHARD
        per_layer = [small_grads[(nm, layer)].reshape((1,) + tuple(full[1:])) for layer in range(w.shape[0])]
        full_grads.append(jnp.concatenate(per_layer, axis=0))
    packed = _pack(full_grads + [loss_part])
    dev_arr = (4 * xi + 2 * yi + ci).astype(jnp.int32).reshape(1)
    sg_block = cast_into_gathered("place_small_grads", packed[None], 0, dev_arr, n_blocks=8, dtype=F32)
    sg_send, sg_recv, sg_bufs, sg_token = copies_start("small_grads_start", [sg_block], all_plan, 7)
    token = sg_token if token is None else token + sg_token

    shard_grads = {}
    for red in reducers:
        if red.stage == 4:
            shard_grads.update(red.result)
    big_out = {}

    def update_big(names, tok):
        for nm, w, m, v, _ in big:
            if nm in names:
                gl = [shard_grads[(nm, layer)] for layer in range(w.shape[0])]
                delta, new_m, new_v = adamw(f"adamw_{nm}", w, gl, m, v, token=tok)
                big_out[nm] = (jnp.stack(gl, axis=0), delta, new_m, new_v)

    last_group = ("mix_w_in", "mix_w_out")
    update_big([nm for nm, _, _, _, _ in big if nm not in last_group], token)
    token = hook(big_out["ffn_w_up"][1])

    gathered_sg = copies_wait("small_grads_wait", sg_bufs, sg_send, sg_recv, all_plan, 0, big_out["ffn_w_down"][1])[0]
    total = sum_blocks("sum_small", gathered_sg.reshape(8 * packed.shape[0], LANES), 8)
    unpacked = _unpack(total, [g.shape for g in full_grads] + [loss_part.shape])
    loss = unpacked[-1][0, 0]
    local_grads = []
    for (nm, w, _, _, shard_axis), g in zip(small, unpacked[:-1]):
        if shard_axis is not None:
            width = w.shape[shard_axis]
            g = lax.dynamic_slice_in_dim(g, shard_idx * width, width, axis=shard_axis)
        local_grads.append(g.reshape(w.shape))
    shapes = [w.shape for _, w, _, _, _ in small]
    pg = _pack(local_grads)
    pw = _pack([w for _, w, _, _, _ in small])
    pm = _pack([m for _, _, m, _, _ in small])
    pv = _pack([v for _, _, _, v, _ in small])
    delta_s, new_m_s, new_v_s = adamw("adamw_small", pw[None], [pg], pm[None], pv[None], token=token)
    hook(delta_s)
    for red in reducers:
        shard_grads.update(red.result)
    update_big(last_group, None)
    small_out = {}
    for (nm, _, _, _, _), g, d_, m_, v_ in zip(
            small, local_grads, _unpack(delta_s[0], shapes), _unpack(new_m_s[0], shapes), _unpack(new_v_s[0], shapes)):
        small_out[nm] = (g, d_, m_, v_)

    order = ["mix_w_in", "pool_w", "pool_scale", "conv_dw_w", "conv_dw_b", "conv_ln_g", "conv_ln_b", "mix_w_out",
             "attn_w_qkv", "attn_rel_bias", "attn_w_o", "ln_mix_g", "ln_mix_b", "ffn_w_up", "ffn_dw_w", "ffn_dw_b",
             "ffn_w_down", "ple_w_proj", "ple_w_gate", "ple_b_gate", "ln_ffn_g", "ln_ffn_b"]
    res = {**big_out, **small_out}
    outs = [loss, grad_x[None]]
    for slot in range(4):
        outs += [res[nm][slot] for nm in order]
    return tuple(outs)
```

```python
import functools
import math

import jax
import jax.numpy as jnp
from jax import lax
from jax.experimental import pallas as pl
from jax.experimental.pallas import tpu as pltpu

F32 = jnp.float32
BF16 = jnp.bfloat16
MESH = pl.DeviceIdType.MESH

N_LAYERS = 2
ALPHA = (2 * N_LAYERS) ** 0.25
LN_EPS = 1e-5
NEG_INF = -1e30
CHUNK = 64
LEFT_CHUNKS = 8
PAD_ROWS = LEFT_CHUNKS * CHUNK
HEAD_DIM = 64
N_HEADS = 16
MAX_REL = 256
POOL_WINDOWS = (2, 4, 8, 16)
POOL_GROUP = 128
CONV_K = 31
FFN_K = 3
CONV_HALO = 32
FFN_HALO = 8
FFN_TILE = 256
FFN_CHUNK_ROWS = 32
FFN_CHUNK_LANES = 256
Q_TILE = 256
K_WIN = Q_TILE + PAD_ROWS
LANES = 128
SUBLANES = 8
ATTN_PAIRS = 2
ATTN_LANES = ATTN_PAIRS * LANES
SHEAR_W = Q_TILE + K_WIN
SHEAR_SAT = SHEAR_W - 2 * MAX_REL
N_SHARD = 4

ADAM_LR = 0.001
ADAM_B1 = 0.9
ADAM_B2 = 0.999
ADAM_EPS = 1e-08
ADAM_WD = 0.01
ADAM_STEP = 10
ADAM_BC1 = 1.0 - ADAM_B1 ** ADAM_STEP
ADAM_BC2 = 1.0 - ADAM_B2 ** ADAM_STEP

DIMS = {
    "nn": (((1,), (0,)), ((), ())),
    "nt": (((1,), (1,)), ((), ())),
    "tn": (((0,), (0,)), ((), ())),
}


def _cp(vmem_mb=48, **kw):
    return pltpu.CompilerParams(vmem_limit_bytes=vmem_mb * 1024 * 1024, **kw)


def _in_hbm(a):
    return pltpu.with_memory_space_constraint(a, pltpu.HBM)


STAGING_LIMIT_BYTES = 1 << 20


def _call(body, **kw):
    call = pl.pallas_call(body, **kw)

    def run(*args):
        pinned = []
        for a in args:
            big = a.size * a.dtype.itemsize >= STAGING_LIMIT_BYTES
            pinned.append(_in_hbm(a) if big and not jnp.issubdtype(a.dtype, jnp.integer) else a)
        return call(*pinned)

    return run


def _dot(a, b, mode):
    return lax.dot_general(a.astype(BF16), b.astype(BF16), DIMS[mode], preferred_element_type=F32)


def _sig(x):
    return 1.0 / (1.0 + jnp.exp(-x))


def _row_tile(s):
    return min(512, s // 4)


def _mm_tile(s):
    return min(1024, s // 4)


def _mm(name, mode, a, b, in_specs, out_shape, out_spec, acc_shape, grid, nk, zero_first=False, vmem_mb=48,
        addend=None):
    out_f32 = out_shape.dtype == F32

    def body(a_ref, b_ref, *rest):
        k = pl.program_id(2)
        if addend is None:
            o_ref, scr = rest[0], rest[1:]
        else:
            add_ref, o_ref, scr = rest[0], rest[1], rest[2:]

        def compute():
            part = _dot(a_ref[...], b_ref[...], mode)
            if nk == 1:
                if addend is not None:
                    part = part + addend[0] * add_ref[...]
                o_ref[...] = part.astype(o_ref.dtype)
                return
            acc = o_ref if out_f32 else scr[0]

            @pl.when(k == 0)
            def _():
                acc[...] = part if addend is None else part + addend[0] * add_ref[...]

            @pl.when(k > 0)
            def _():
                acc[...] += part

            if not out_f32:
                @pl.when(k == nk - 1)
                def _():
                    o_ref[...] = acc[...].astype(o_ref.dtype)

        if zero_first:
            @pl.when(pl.program_id(1) == 0)
            def _():
                o_ref[...] = jnp.zeros(o_ref.shape, o_ref.dtype)

            pl.when(pl.program_id(1) > 0)(compute)
        else:
            compute()

    scratch = [] if (nk == 1 or out_f32) else [pltpu.VMEM(acc_shape, F32)]
    operands = [a, b] if addend is None else [a, b, addend[1]]
    specs = list(in_specs) if addend is None else list(in_specs) + [out_spec]
    return _call(
        body, name=name, grid=grid, in_specs=specs, out_specs=out_spec, out_shape=out_shape,
        scratch_shapes=scratch, compiler_params=_cp(vmem_mb),
    )(*operands)


def mm_cols_fwd(name, a, wc, out_dtype, pad_blocks=0, part=(0, 1)):
    s, k = a.shape
    s //= part[1]
    n4 = wc.shape[2]
    tm = _row_tile(s) if pad_blocks else _mm_tile(s)
    nt = s // tm
    first_block = part[0] * nt
    return _mm(
        name, "nn", a, wc,
        [pl.BlockSpec((tm, k), lambda j, i, r: (first_block + jnp.maximum(i - pad_blocks, 0), 0)),
         pl.BlockSpec((None, k, n4), lambda j, i, r: (j, 0, 0))],
        jax.ShapeDtypeStruct((s + pad_blocks * tm, N_SHARD * n4), out_dtype),
        pl.BlockSpec((tm, n4), lambda j, i, r: (i, j)),
        None, (N_SHARD, nt + pad_blocks, 1), 1, zero_first=pad_blocks > 0)


def mm_cols_dx(name, dy, wc, addend=None):
    s = dy.shape[0]
    _, k, n4 = wc.shape
    tm = _mm_tile(s)
    return _mm(
        name, "nt", dy, wc,
        [pl.BlockSpec((tm, n4), lambda g, i, r: (i, r)),
         pl.BlockSpec((None, k, n4), lambda g, i, r: (r, 0, 0))],
        jax.ShapeDtypeStruct((s, k), F32),
        pl.BlockSpec((tm, k), lambda g, i, r: (i, 0)),
        (tm, k), (1, s // tm, N_SHARD), N_SHARD, addend=addend)


def mm_cols_dw(name, a, dy, part=(0, 1)):
    s, k = a.shape
    s //= part[1]
    n4 = dy.shape[1] // N_SHARD
    tm = _mm_tile(s)
    nt = s // tm
    first_block = part[0] * nt
    return _mm(
        name, "tn", a, dy,
        [pl.BlockSpec((tm, k), lambda j, g, r: (first_block + r, 0)),
         pl.BlockSpec((tm, n4), lambda j, g, r: (r, j))],
        jax.ShapeDtypeStruct((N_SHARD, k, n4), F32),
        pl.BlockSpec((None, k, n4), lambda j, g, r: (j, 0, 0)),
        (k, n4), (N_SHARD, 1, nt), nt)


def _k_tile(k):
    return k if k <= 1024 else k // 2


def mm_rows_fwd(name, a, wr, out_dtype=F32):
    s, k = a.shape
    n = wr.shape[1]
    tm = _mm_tile(s)
    tk = _k_tile(k)
    nk = k // tk
    return _mm(
        name, "nn", a, wr,
        [pl.BlockSpec((tm, tk), lambda g, i, r: (i, r)),
         pl.BlockSpec((tk, n), lambda g, i, r: (r, 0))],
        jax.ShapeDtypeStruct((s, n), out_dtype),
        pl.BlockSpec((tm, n), lambda g, i, r: (i, 0)),
        (tm, n), (1, s // tm, nk), nk)


def mm_rows_dx(name, dy, wr, out_dtype=F32):
    s, n = dy.shape
    k = wr.shape[0]
    tm = _mm_tile(s)
    tk = _k_tile(k)
    return _mm(
        name, "nt", dy, wr,
        [pl.BlockSpec((tm, n), lambda j, i, r: (i, 0)),
         pl.BlockSpec((tk, n), lambda j, i, r: (j, 0))],
        jax.ShapeDtypeStruct((s, k), out_dtype),
        pl.BlockSpec((tm, tk), lambda j, i, r: (i, j)),
        None, (k // tk, s // tm, 1), 1)


def mm_rows_dw(name, a, dy):
    s, k = a.shape
    n = dy.shape[1]
    tm = _mm_tile(s)
    tk = _k_tile(k)
    nt = s // tm
    return _mm(
        name, "tn", a, dy,
        [pl.BlockSpec((tm, tk), lambda j, g, r: (r, j)),
         pl.BlockSpec((tm, n), lambda j, g, r: (r, 0))],
        jax.ShapeDtypeStruct((k, n), F32),
        pl.BlockSpec((tk, n), lambda j, g, r: (j, 0)),
        (tk, n), (k // tk, 1, nt), nt)


def _row(tm, c, col=0):
    return pl.BlockSpec((tm, c), lambda i: (i, col))


def _full(shape):
    nd = len(shape)
    return pl.BlockSpec(shape, lambda i: (0,) * nd)


def _prev(tm, h, c, col=0):
    return pl.BlockSpec((h, c), lambda i: (jnp.maximum(i * (tm // h) - 1, 0), col))


def _next(tm, h, c, s, col=0):
    return pl.BlockSpec((h, c), lambda i: (jnp.minimum((i + 1) * (tm // h), s // h - 1), col))


def _acc_add(ref, first, val):
    @pl.when(first)
    def _():
        ref[...] = val

    @pl.when(jnp.logical_not(first))
    def _():
        ref[...] += val


def _colsum(v):
    return jnp.sum(v, axis=0, keepdims=True)


def _ln_stats(z):
    mu = jnp.mean(z, axis=-1, keepdims=True)
    zc = z - mu
    var = jnp.mean(zc * zc, axis=-1, keepdims=True)
    rstd = lax.rsqrt(var + LN_EPS)
    return zc * rstd, rstd


def _ln_bwd(dxhat, xhat, rstd):
    m1 = jnp.mean(dxhat, axis=-1, keepdims=True)
    m2 = jnp.mean(dxhat * xhat, axis=-1, keepdims=True)
    return rstd * (dxhat - m1 - xhat * m2)


def ln_fwd(name, x, f, g, b, ple=None, emit_y=True):
    s, d = x.shape
    tm = _row_tile(s)
    n_in = 2 + (3 if ple is not None else 0)

    def body(*refs):
        x_ref, f_ref = refs[0], refs[1]
        g_ref, b_ref = refs[n_in], refs[n_in + 1]
        xh_ref, rs_ref = refs[-2:]
        z = ALPHA * x_ref[...] + f_ref[...]
        if ple is not None:
            pgl_ref, pp_ref, bg_ref = refs[2:5]
            z = z + _sig(pgl_ref[...] + bg_ref[...]) * pp_ref[...]
        xhat, rstd = _ln_stats(z)
        if emit_y:
            refs[n_in + 2][...] = xhat * g_ref[...] + b_ref[...]
        xh_ref[...] = xhat
        rs_ref[...] = jnp.broadcast_to(rstd, rs_ref.shape)

    ins = [x, f]
    specs = [_row(tm, d), _row(tm, d)]
    if ple is not None:
        pgl, pp, bg = ple
        ins += [pgl, pp, bg]
        specs += [_row(tm, d), _row(tm, d), _full((1, d))]
    ins += [g, b]
    specs += [_full((1, d)), _full((1, d))]
    n_y = 1 if emit_y else 0
    outs = _call(
        body, name=name, grid=(s // tm,), in_specs=specs,
        out_specs=[_row(tm, d)] * (n_y + 1) + [_row(tm, LANES)],
        out_shape=[jax.ShapeDtypeStruct((s, d), F32)] * (n_y + 1) + [jax.ShapeDtypeStruct((s, LANES), F32)],
        compiler_params=_cp(),
    )(*ins)
    return (outs[0], outs[1], outs[2]) if emit_y else (None, outs[0], outs[1])


def ln_bwd(name, parts, xhat, rstd, g, ple=None, loss=None):
    s, d = xhat.shape
    tm = _row_tile(s)
    coefs = [c for c, _ in parts]
    n_p = len(parts)
    n_ple = 3 if ple is not None else 0
    n_in = n_p + 3 + n_ple + (2 if loss is not None else 0)

    def body(*refs):
        first = pl.program_id(0) == 0
        xh = refs[n_p][...]
        rs = refs[n_p + 1][:, 0:1]
        g_v = refs[n_p + 2][...]
        outs = refs[n_in:]
        if loss is not None:
            t_ref, b_ref = refs[n_p + 3 + n_ple:n_p + 5 + n_ple]
            err = (xh * g_v + b_ref[...]) - t_ref[...]
            dy = err * (1.0 / d)
            part = 0.5 * jnp.sum(jnp.mean(err * err, axis=-1, keepdims=True), axis=0, keepdims=True)
            _acc_add(outs[-1], first, jnp.broadcast_to(part, outs[-1].shape))
        else:
            dy = coefs[0] * refs[0][...].astype(F32)
            for j in range(1, n_p):
                dy = dy + coefs[j] * refs[j][...].astype(F32)
        dz = _ln_bwd(dy * g_v, xh, rs)
        outs[0][...] = dz
        _acc_add(outs[1], first, _colsum(dy * xh))
        _acc_add(outs[2], first, _colsum(dy))
        if ple is not None:
            pgl_ref, pp_ref, bg_ref = refs[n_p + 3:n_p + 6]
            pg = _sig(pgl_ref[...] + bg_ref[...])
            dpgl = dz * pp_ref[...] * pg * (1.0 - pg)
            outs[3][...] = (dz * pg).astype(BF16)
            outs[4][...] = dpgl.astype(BF16)
            _acc_add(outs[5], first, _colsum(dpgl))

    ins = [p for _, p in parts] + [xhat, rstd, g]
    specs = [_row(tm, d)] * n_p + [_row(tm, d), _row(tm, LANES), _full((1, d))]
    out_specs = [_row(tm, d), _full((1, d)), _full((1, d))]
    out_shape = [jax.ShapeDtypeStruct((s, d), F32), jax.ShapeDtypeStruct((1, d), F32),
                 jax.ShapeDtypeStruct((1, d), F32)]
    if ple is not None:
        pgl, pp, bg = ple
        ins += [pgl, pp, bg]
        specs += [_row(tm, d), _row(tm, d), _full((1, d))]
        out_specs += [_row(tm, d), _row(tm, d), _full((1, d))]
        out_shape += [jax.ShapeDtypeStruct((s, d), BF16), jax.ShapeDtypeStruct((s, d), BF16),
                      jax.ShapeDtypeStruct((1, d), F32)]
    if loss is not None:
        target, b = loss
        ins += [target, b]
        specs += [_row(tm, d), _full((1, d))]
        out_specs += [_full((8, LANES))]
        out_shape += [jax.ShapeDtypeStruct((8, LANES), F32)]
    return _call(
        body, name=name, grid=(s // tm,), in_specs=specs, out_specs=out_specs, out_shape=out_shape,
        compiler_params=_cp(),
    )(*ins)


def _fill_rotations(rot_ref, x, direction):
    n = x.shape[0]
    rot_ref[0] = x
    for b in range(1, SUBLANES):
        if direction < 0:
            rot_ref[b, SUBLANES:n, :] = x[SUBLANES - b:n - b]
        else:
            rot_ref[b, 0:n - SUBLANES, :] = x[b:n - SUBLANES + b]


def _rotated(rot_ref, start, rows, cs, direction=-1):
    b = (-start) % SUBLANES if direction < 0 else start % SUBLANES
    aligned = start + b if direction < 0 else start - b
    return rot_ref[b, pl.ds(aligned, rows), cs]


def _tile_pos(i, tm, rows):
    return (i * tm + lax.broadcasted_iota(jnp.int32, (rows, 1), 0) + 1).astype(F32)


def mixer_fwd(name, u, pool_w, pool_scale, conv_w, conv_b, cn_g, cn_b):
    s = u.shape[0]
    dp = 512
    tm = min(256, s // 4)
    h = CONV_HALO

    def body(a_c, a_p, bv_c, bv_p, bg_c, bg_p, pw_ref, ps_ref, cw_ref, cb_ref, cg_ref, cbt_ref,
             cat_ref, d_ref, e_ref, glu_ref, hh_ref, rs_ref, ext_a, rot_g, conv_out):
        i = pl.program_id(0)
        first = i == 0
        ext_a[0:h, :] = jnp.where(first, 0.0, a_p[...])
        ext_a[h:, :] = a_c[...]
        glu = bv_c[...] * _sig(bg_c[...])
        glu_ref[...] = glu
        _fill_rotations(rot_g, jnp.concatenate([jnp.where(first, 0.0, bv_p[...] * _sig(bg_p[...])), glu], axis=0), -1)
        pos = _tile_pos(i, tm, tm)
        for gi, w in enumerate(POOL_WINDOWS):
            cs = slice(gi * POOL_GROUP, (gi + 1) * POOL_GROUP)
            a_g = ext_a[pl.ds(h, tm), cs]
            acc = a_g
            for sh in range(1, w):
                acc = acc + ext_a[pl.ds(h - sh, tm), cs]
            d_g = acc / jnp.minimum(pos, float(w)) - a_g
            d_ref[:, cs] = d_g.astype(BF16)
            e_g = _dot(d_g, pw_ref[gi], "nn")
            e_ref[:, cs] = e_g
            cat_ref[:, cs] = (e_g * ps_ref[:, cs]).astype(BF16)
        for lg in range(dp // LANES):
            cs = slice(lg * LANES, (lg + 1) * LANES)
            acc = jnp.broadcast_to(cb_ref[:, cs], (tm, LANES))
            for sh in range(CONV_K):
                acc = acc + _rotated(rot_g, h - sh, tm, cs) * cw_ref[pl.ds(CONV_K - 1 - sh, 1), cs]
            conv_out[:, cs] = acc
        hhat, rstd = _ln_stats(conv_out[...])
        hl = hhat * cg_ref[...] + cbt_ref[...]
        cat_ref[:, dp:] = (hl * _sig(hl)).astype(BF16)
        hh_ref[...] = hhat
        rs_ref[...] = jnp.broadcast_to(rstd, rs_ref.shape)

    specs = [_row(tm, dp, 0), _prev(tm, h, dp, 0), _row(tm, dp, 1), _prev(tm, h, dp, 1),
             _row(tm, dp, 2), _prev(tm, h, dp, 2),
             _full((4, POOL_GROUP, POOL_GROUP)), _full((1, dp)), _full((CONV_K, dp)),
             _full((1, dp)), _full((1, dp)), _full((1, dp))]
    out_specs = [_row(tm, 2 * dp), _row(tm, dp), _row(tm, dp), _row(tm, dp), _row(tm, dp), _row(tm, LANES)]
    out_shape = [jax.ShapeDtypeStruct((s, 2 * dp), BF16), jax.ShapeDtypeStruct((s, dp), BF16),
                 jax.ShapeDtypeStruct((s, dp), F32), jax.ShapeDtypeStruct((s, dp), F32),
                 jax.ShapeDtypeStruct((s, dp), F32), jax.ShapeDtypeStruct((s, LANES), F32)]
    return _call(
        body, name=name, grid=(s // tm,), in_specs=specs, out_specs=out_specs, out_shape=out_shape,
        scratch_shapes=[pltpu.VMEM((h + tm, dp), F32), pltpu.VMEM((SUBLANES, h + tm, dp), F32),
                        pltpu.VMEM((tm, dp), F32)],
        compiler_params=_cp(),
    )(u, u, u, u, u, u, pool_w, pool_scale, conv_w, conv_b, cn_g, cn_b)


def mixer_bwd(name, dcat, u, d_sv, e_sv, glu_sv, hh_sv, rs_sv, pool_w, pool_scale, conv_w, cn_g, cn_b):
    s = u.shape[0]
    dp = 512
    tm = min(256, s // 4)
    h = CONV_HALO
    nt = s // tm

    def body(dc_c, dc_n, bv_c, bg_c, d_c, e_c, gl_c, gl_p, hh_c, hh_n, rs_c, rs_n,
             pw_ref, ps_ref, cw_ref, cg_ref, cbt_ref,
             du_ref, dpw_ref, dps_ref, dcw_ref, dcb_ref, dcg_ref, dcbt_ref,
             ext_dh, ext_g, ext_r):
        i = pl.program_id(0)
        first = i == 0
        last = i == nt - 1
        cg = cg_ref[...]

        def conv_grads(dyb, hhat, rstd):
            hl = hhat * cg + cbt_ref[...]
            sg = _sig(hl)
            dhl = dyb * (sg * (1.0 + hl * (1.0 - sg)))
            return _ln_bwd(dhl * cg, hhat, rstd), dhl

        hh_cur = hh_c[...]
        dh_c, dhl_c = conv_grads(dc_c[:, dp:], hh_cur, rs_c[:, 0:1])
        dh_n, _ = conv_grads(dc_n[:, dp:], hh_n[...], rs_n[:, 0:1])
        _fill_rotations(ext_dh, jnp.concatenate([dh_c, jnp.where(last, 0.0, dh_n)], axis=0), 1)
        _fill_rotations(ext_g, jnp.concatenate([jnp.where(first, 0.0, gl_p[...]), gl_c[...]], axis=0), -1)

        @pl.when(first)
        def _():
            dcw_ref[...] = jnp.zeros(dcw_ref.shape, F32)

        for lg in range(dp // LANES):
            cs = slice(lg * LANES, (lg + 1) * LANES)
            dglu = jnp.zeros((tm, LANES), F32)
            for sh in range(CONV_K):
                dglu = dglu + _rotated(ext_dh, sh, tm, cs, 1) * cw_ref[pl.ds(CONV_K - 1 - sh, 1), cs]
            dh_g = ext_dh[0, pl.ds(0, tm), cs]
            for sh in range(CONV_K):
                dcw_ref[pl.ds(CONV_K - 1 - sh, 1), cs] += _colsum(dh_g * _rotated(ext_g, h - sh, tm, cs))
            sgate = _sig(bg_c[:, cs])
            du_ref[:, dp + lg * LANES:dp + (lg + 1) * LANES] = dglu * sgate
            du_ref[:, 2 * dp + lg * LANES:2 * dp + (lg + 1) * LANES] = dglu * bv_c[:, cs] * sgate * (1.0 - sgate)
        _acc_add(dcb_ref, first, _colsum(dh_c))
        _acc_add(dcg_ref, first, _colsum(dhl_c * hh_cur))
        _acc_add(dcbt_ref, first, _colsum(dhl_c))

        pos_c = _tile_pos(i, tm, tm)
        pos_n = _tile_pos(i + 1, tm, h)
        _acc_add(dps_ref, first, _colsum(dc_c[:, :dp] * e_c[...]))
        for gi, w in enumerate(POOL_WINDOWS):
            cs = slice(gi * POOL_GROUP, (gi + 1) * POOL_GROUP)
            pw = pw_ref[gi]
            de_c = dc_c[:, cs] * ps_ref[:, cs]
            de_n = dc_n[:, cs] * ps_ref[:, cs]
            dd_c = _dot(de_c, pw, "nt")
            dd_n = _dot(de_n, pw, "nt")
            ext_r[0:tm, :] = dd_c / jnp.minimum(pos_c, float(w))
            ext_r[tm:, :] = jnp.where(last, 0.0, dd_n / jnp.minimum(pos_n, float(w)))
            acc = -dd_c
            for sh in range(w):
                acc = acc + ext_r[pl.ds(sh, tm), :]
            du_ref[:, cs] = acc
            dpw_g = _dot(d_c[:, cs], de_c, "tn")

            @pl.when(first)
            def _():
                dpw_ref[gi] = dpw_g

            @pl.when(jnp.logical_not(first))
            def _():
                dpw_ref[gi] += dpw_g

    specs = [_row(tm, 2 * dp), _next(tm, h, 2 * dp, s), _row(tm, dp, 1), _row(tm, dp, 2),
             _row(tm, dp), _row(tm, dp), _row(tm, dp), _prev(tm, h, dp),
             _row(tm, dp), _next(tm, h, dp, s), _row(tm, LANES), _next(tm, h, LANES, s),
             _full((4, POOL_GROUP, POOL_GROUP)), _full((1, dp)), _full((CONV_K, dp)),
             _full((1, dp)), _full((1, dp))]
    out_specs = [_row(tm, 3 * dp), _full((4, POOL_GROUP, POOL_GROUP)), _full((1, dp)), _full((CONV_K, dp)),
                 _full((1, dp)), _full((1, dp)), _full((1, dp))]
    out_shape = [jax.ShapeDtypeStruct((s, 3 * dp), F32),
                 jax.ShapeDtypeStruct((4, POOL_GROUP, POOL_GROUP), F32), jax.ShapeDtypeStruct((1, dp), F32),
                 jax.ShapeDtypeStruct((CONV_K, dp), F32), jax.ShapeDtypeStruct((1, dp), F32),
                 jax.ShapeDtypeStruct((1, dp), F32), jax.ShapeDtypeStruct((1, dp), F32)]
    return _call(
        body, name=name, grid=(nt,), in_specs=specs, out_specs=out_specs, out_shape=out_shape,
        scratch_shapes=[pltpu.VMEM((SUBLANES, tm + h, dp), F32), pltpu.VMEM((SUBLANES, h + tm, dp), F32),
                        pltpu.VMEM((tm + h, POOL_GROUP), F32)],
        compiler_params=_cp(),
    )(dcat, dcat, u, u, d_sv, e_sv, glu_sv, glu_sv, hh_sv, hh_sv, rs_sv, rs_sv,
      pool_w, pool_scale, conv_w, cn_g, cn_b)


GELU_C = math.sqrt(2.0 / math.pi)


def _gelu_parts(x):
    x2 = x * x
    t = jnp.tanh(x * (GELU_C + (GELU_C * 0.044715) * x2))
    half_1pt = 0.5 + 0.5 * t
    gelu = x * half_1pt
    dgelu = half_1pt + (0.5 * x) * (1.0 - t * t) * (GELU_C + (3.0 * GELU_C * 0.044715) * x2)
    return gelu, dgelu


def ffn_act_fwd(name, gv, dw_w, dw_b):
    s = gv.shape[0]
    dff = gv.shape[1] // 2
    tm = min(FFN_TILE, s // 4)
    h = FFN_HALO
    rc = FFN_CHUNK_ROWS
    lw = FFN_CHUNK_LANES

    def body(g_c, g_p, v_c, w_ref, b_ref, hid_ref):
        first = pl.program_id(0) == 0

        def chunk(ci, carry):
            r0 = pl.multiple_of(ci * rc, rc)
            above = pl.multiple_of(jnp.maximum(r0 - h, 0), h)
            for lg in range(dff // lw):
                cs = slice(lg * lw, (lg + 1) * lw)
                top = jnp.where(ci == 0, jnp.where(first, 0.0, g_p[:, cs]), g_c[pl.ds(above, h), cs])
                win = jnp.concatenate([top, g_c[pl.ds(r0, rc), cs]], axis=0)
                gc = jnp.broadcast_to(b_ref[:, cs], (rc, lw))
                for sh in range(FFN_K):
                    gc = gc + win[h - sh:h - sh + rc] * w_ref[pl.ds(FFN_K - 1 - sh, 1), cs]
                gelu, _ = _gelu_parts(gc)
                hid_ref[pl.ds(r0, rc), cs] = (gelu * v_c[pl.ds(r0, rc), cs]).astype(BF16)
            return carry

        lax.fori_loop(0, tm // rc, chunk, 0)

    return _call(
        body, name=name, grid=(s // tm,),
        in_specs=[_row(tm, dff, 0), _prev(tm, h, dff, 0), _row(tm, dff, 1), _full((FFN_K, dff)), _full((1, dff))],
        out_specs=_row(tm, dff), out_shape=jax.ShapeDtypeStruct((s, dff), BF16),
        compiler_params=_cp(),
    )(gv, gv, gv, dw_w, dw_b)


def ffn_act_bwd(name, dhid, gv, dw_w, dw_b):
    s = gv.shape[0]
    dff = gv.shape[1] // 2
    tm = min(FFN_TILE, s // 4)
    h = FFN_HALO
    nt = s // tm
    rc = FFN_CHUNK_ROWS
    lw = FFN_CHUNK_LANES
    n_chunks = tm // rc

    def body(dh_c, dh_n, g_p, g_c, g_n, v_c, v_n, w_ref, b_ref, dgv_ref, dw_ref, db_ref):
        i = pl.program_id(0)
        first = i == 0
        last = i == nt - 1

        @pl.when(first)
        def _():
            dw_ref[...] = jnp.zeros(dw_ref.shape, F32)
            db_ref[...] = jnp.zeros(db_ref.shape, F32)

        def chunk(ci, carry):
            r0 = pl.multiple_of(ci * rc, rc)
            above = pl.multiple_of(jnp.maximum(r0 - h, 0), h)
            below = pl.multiple_of(jnp.minimum(r0 + rc, tm - h), h)
            at_end = ci == n_chunks - 1
            for lg in range(dff // lw):
                cs = slice(lg * lw, (lg + 1) * lw)
                top = jnp.where(ci == 0, jnp.where(first, 0.0, g_p[:, cs]), g_c[pl.ds(above, h), cs])
                bot = jnp.where(at_end, g_n[:, cs], g_c[pl.ds(below, h), cs])
                win = jnp.concatenate([top, g_c[pl.ds(r0, rc), cs], bot], axis=0)
                shifted = [win[h - sh:h - sh + rc + h] for sh in range(FFN_K)]
                gc = jnp.broadcast_to(b_ref[:, cs], (rc + h, lw))
                for sh in range(FFN_K):
                    gc = gc + shifted[sh] * w_ref[pl.ds(FFN_K - 1 - sh, 1), cs]
                gelu, dgelu = _gelu_parts(gc)
                dh_mid = dh_c[pl.ds(r0, rc), cs]
                hv_bot = jnp.where(at_end, jnp.where(last, 0.0, dh_n[:, cs] * v_n[:, cs]),
                                   dh_c[pl.ds(below, h), cs] * v_c[pl.ds(below, h), cs])
                dgc = jnp.concatenate([dh_mid * v_c[pl.ds(r0, rc), cs], hv_bot], axis=0) * dgelu
                dgate = jnp.zeros((rc, lw), F32)
                for sh in range(FFN_K):
                    dgate = dgate + dgc[sh:sh + rc] * w_ref[pl.ds(FFN_K - 1 - sh, 1), cs]
                dgv_ref[pl.ds(r0, rc), cs] = dgate.astype(BF16)
                dgv_ref[pl.ds(r0, rc), slice(dff + lg * lw, dff + (lg + 1) * lw)] = (dh_mid * gelu[0:rc]).astype(BF16)
                dgc_mid = dgc[0:rc]
                for sh in range(FFN_K):
                    dw_ref[pl.ds(FFN_K - 1 - sh, 1), cs] += _colsum(dgc_mid * shifted[sh][0:rc])
                db_ref[:, cs] += _colsum(dgc_mid)
            return carry

        lax.fori_loop(0, n_chunks, chunk, 0)

    return _call(
        body, name=name, grid=(nt,),
        in_specs=[_row(tm, dff), _next(tm, h, dff, s),
                  _prev(tm, h, dff, 0), _row(tm, dff, 0), _next(tm, h, dff, s, 0),
                  _row(tm, dff, 1), _next(tm, h, dff, s, 1),
                  _full((FFN_K, dff)), _full((1, dff))],
        out_specs=[_row(tm, 2 * dff), _full((FFN_K, dff)), _full((1, dff))],
        out_shape=[jax.ShapeDtypeStruct((s, 2 * dff), BF16), jax.ShapeDtypeStruct((FFN_K, dff), F32),
                   jax.ShapeDtypeStruct((1, dff), F32)],
        compiler_params=_cp(),
    )(dhid, dhid, gv, gv, gv, gv, gv, dw_w, dw_b)


def _bias_line(rel_bias):
    nh = rel_bias.shape[0]
    line = jnp.concatenate(
        [jnp.zeros((nh, 1), rel_bias.dtype), jnp.broadcast_to(rel_bias[:, 2 * MAX_REL:], (nh, SHEAR_SAT)),
         jnp.flip(rel_bias[:, 1:2 * MAX_REL], axis=1)], axis=1)
    return line[:, None, :]


def bias_tile(name, line):
    nh = line.shape[0]

    def body(l_ref, o_ref):
        x = jnp.broadcast_to(l_ref[...], (Q_TILE, SHEAR_W))
        z = pltpu.roll(x, SHEAR_W - Q_TILE, 1, stride=1, stride_axis=0)
        qc = lax.broadcasted_iota(jnp.int32, (Q_TILE, K_WIN), 0) // CHUNK
        kc = lax.broadcasted_iota(jnp.int32, (Q_TILE, K_WIN), 1) // CHUNK
        o_ref[...] = jnp.where((kc >= qc) & (kc <= qc + LEFT_CHUNKS), z[:, :K_WIN], NEG_INF)

    return _call(
        body, name=name, grid=(nh,), in_specs=[pl.BlockSpec((None, 1, SHEAR_W), lambda hh: (hh, 0, 0))],
        out_specs=pl.BlockSpec((None, Q_TILE, K_WIN), lambda hh: (hh, 0, 0)),
        out_shape=jax.ShapeDtypeStruct((nh, Q_TILE, K_WIN), F32), compiler_params=_cp(),
    )(line)


def _stack_heads(x2):
    lane = lax.broadcasted_iota(jnp.int32, x2.shape, 1)
    zero = jnp.zeros_like(x2)
    return jnp.concatenate([jnp.where(lane < HEAD_DIM, x2, zero), jnp.where(lane < HEAD_DIM, zero, x2)], axis=0)


def _unstack_heads(x_st):
    lane = lax.broadcasted_iota(jnp.int32, (Q_TILE, LANES), 1)
    return jnp.where(lane < HEAD_DIM, x_st[:Q_TILE], x_st[Q_TILE:])


def _attn_probs(q_st, k3, bias_st, t):
    sc = _dot(q_st, k3, "nt") * (HEAD_DIM ** -0.5) + bias_st
    col = lax.broadcasted_iota(jnp.int32, sc.shape, 1)
    sc = jnp.where(col >= PAD_ROWS - t * Q_TILE, sc, NEG_INF)
    m = jnp.max(sc, axis=-1, keepdims=True)
    p = jnp.exp(sc - m)
    return p * (1.0 / jnp.sum(p, axis=-1, keepdims=True))


def _attn_specs(d_model):
    nq = PAD_ROWS // Q_TILE
    groups = d_model // ATTN_LANES
    specs = [pl.BlockSpec((Q_TILE, ATTN_LANES), lambda g, t: (t + nq, g))]
    for which in (1, 2):
        for j in range(K_WIN // Q_TILE):
            specs.append(pl.BlockSpec((Q_TILE, ATTN_LANES), lambda g, t, j=j, which=which: (t + j, which * groups + g)))
    specs.append(pl.BlockSpec((2 * ATTN_PAIRS, Q_TILE, K_WIN), lambda g, t: (g, 0, 0)))
    return specs


def attn_fwd(name, qkvp, bias):
    s = qkvp.shape[0] - PAD_ROWS
    d_model = qkvp.shape[1] // 3
    nw = K_WIN // Q_TILE

    def body(q_ref, *refs):
        k_refs, v_refs, b_ref, o_ref = refs[:nw], refs[nw:2 * nw], refs[2 * nw], refs[2 * nw + 1]
        t = pl.program_id(1)
        for j in range(ATTN_PAIRS):
            ls = slice(j * LANES, (j + 1) * LANES)
            k3 = jnp.concatenate([r[:, ls] for r in k_refs], axis=0)
            v3 = jnp.concatenate([r[:, ls] for r in v_refs], axis=0)
            bias_st = b_ref[2 * j:2 * j + 2].reshape(2 * Q_TILE, K_WIN)
            p = _attn_probs(_stack_heads(q_ref[:, ls]), k3, bias_st, t)
            o_ref[:, ls] = _unstack_heads(_dot(p, v3, "nn")).astype(BF16)

    return _call(
        body, name=name, grid=(d_model // ATTN_LANES, s // Q_TILE),
        in_specs=_attn_specs(d_model), out_specs=pl.BlockSpec((Q_TILE, ATTN_LANES), lambda g, t: (t, g)),
        out_shape=jax.ShapeDtypeStruct((s, d_model), BF16), compiler_params=_cp(),
    )(qkvp, *([qkvp] * (2 * nw)), bias)


def attn_bwd(name, qkvp, bias, do):
    s = qkvp.shape[0] - PAD_ROWS
    d_model = qkvp.shape[1] // 3
    nw = K_WIN // Q_TILE
    nt = s // Q_TILE
    scale = HEAD_DIM ** -0.5

    def body(q_ref, *refs):
        k_refs, v_refs = refs[:nw], refs[nw:2 * nw]
        b_ref, do_ref, dq_ref, dk_ref, dv_ref, ds_ref, dk_acc, dv_acc = refs[2 * nw:]
        t = pl.program_id(1)
        first = t == 0

        @pl.when(first)
        def _():
            dk_acc[...] = jnp.zeros(dk_acc.shape, F32)
            dv_acc[...] = jnp.zeros(dv_acc.shape, F32)

        start = pl.multiple_of(t * Q_TILE, Q_TILE)
        for j in range(ATTN_PAIRS):
            ls = slice(j * LANES, (j + 1) * LANES)
            q_st = _stack_heads(q_ref[:, ls])
            do_st = _stack_heads(do_ref[:, ls])
            k3 = jnp.concatenate([r[:, ls] for r in k_refs], axis=0)
            v3 = jnp.concatenate([r[:, ls] for r in v_refs], axis=0)
            p = _attn_probs(q_st, k3, b_ref[2 * j:2 * j + 2].reshape(2 * Q_TILE, K_WIN), t)
            dp = _dot(do_st, v3, "nt")
            ds = p * (dp - jnp.sum(p * dp, axis=-1, keepdims=True))
            _acc_add(ds_ref.at[2 * j:2 * j + 2], first, ds.reshape(2, Q_TILE, K_WIN))
            dsb = (ds * scale).astype(BF16)
            dq_ref[:, ls] = _unstack_heads(_dot(dsb, k3, "nn")).astype(BF16)
            dk_acc[pl.ds(start, K_WIN), ls] += _dot(dsb, q_st, "tn")
            dv_acc[pl.ds(start, K_WIN), ls] += _dot(p, do_st, "tn")

        @pl.when(t == nt - 1)
        def _():
            dk_ref[...] = dk_acc[pl.ds(PAD_ROWS, s), :].astype(BF16)
            dv_ref[...] = dv_acc[pl.ds(PAD_ROWS, s), :].astype(BF16)

    specs = _attn_specs(d_model) + [pl.BlockSpec((Q_TILE, ATTN_LANES), lambda g, t: (t, g))]
    col_spec = pl.BlockSpec((s, ATTN_LANES), lambda g, t: (0, g))
    return _call(
        body, name=name, grid=(d_model // ATTN_LANES, nt), in_specs=specs,
        out_specs=[pl.BlockSpec((Q_TILE, ATTN_LANES), lambda g, t: (t, g)), col_spec, col_spec,
                   pl.BlockSpec((2 * ATTN_PAIRS, Q_TILE, K_WIN), lambda g, t: (g, 0, 0))],
        out_shape=[jax.ShapeDtypeStruct((s, d_model), BF16)] * 3
        + [jax.ShapeDtypeStruct((N_HEADS, Q_TILE, K_WIN), F32)],
        scratch_shapes=[pltpu.VMEM((PAD_ROWS + s, ATTN_LANES), F32), pltpu.VMEM((PAD_ROWS + s, ATTN_LANES), F32)],
        compiler_params=_cp(),
    )(qkvp, *([qkvp] * (2 * nw)), bias, do)


def _shear_for_bias_grad(ds_sum):
    nh = ds_sum.shape[0]
    z = jnp.pad(ds_sum, ((0, 0), (0, 0), (Q_TILE, 0))).reshape(nh, Q_TILE * SHEAR_W)
    return jnp.pad(z, ((0, 0), (0, Q_TILE))).reshape(nh, Q_TILE, SHEAR_W + 1)


def bias_grad_reduce(name, sheared):
    nh, _, width = sheared.shape

    def body(x_ref, col_ref, sat_ref):
        cols = _colsum(x_ref[...])
        col_ref[...] = cols
        k = lax.broadcasted_iota(jnp.int32, cols.shape, 1)
        tot = jnp.sum(jnp.where((k >= 1) & (k <= SHEAR_SAT), cols, 0.0), axis=-1, keepdims=True)
        sat_ref[...] = jnp.broadcast_to(tot, sat_ref.shape)

    return _call(
        body, name=name, grid=(nh,),
        in_specs=[pl.BlockSpec((None, Q_TILE, width), lambda hh: (hh, 0, 0))],
        out_specs=[pl.BlockSpec((None, 1, width), lambda hh: (hh, 0, 0)),
                   pl.BlockSpec((None, 1, LANES), lambda hh: (hh, 0, 0))],
        out_shape=[jax.ShapeDtypeStruct((nh, 1, width), F32), jax.ShapeDtypeStruct((nh, 1, LANES), F32)],
        compiler_params=_cp(),
    )(sheared)


def _ew_rows(r, most=512):
    for cand in (512, 256, 128, 64, 32, 16, 8):
        if cand <= most and r % cand == 0:
            return cand
    return r


def cast_into_gathered(name, w, layer, s_idx, n_blocks=N_SHARD, dtype=BF16, token=None):
    r, c = w.shape[-2:]
    tr = _ew_rows(r)

    def body(s_ref, w_ref, *rest):
        rest[-1][...] = w_ref[...].astype(dtype)

    extra = [] if token is None else [token]
    grid_spec = pltpu.PrefetchScalarGridSpec(
        num_scalar_prefetch=1, grid=(r // tr,),
        in_specs=[pl.BlockSpec((None, tr, c), lambda i, s_ref: (layer, i, 0))] + [ANY_SPEC] * len(extra),
        out_specs=pl.BlockSpec((None, tr, c), lambda i, s_ref: (s_ref[0], i, 0)))
    return _call(
        body, name=name, grid_spec=grid_spec, out_shape=jax.ShapeDtypeStruct((n_blocks, r, c), dtype),
        compiler_params=_cp(),
    )(s_idx, w, *extra)


def adamw(name, w, grads, m, v, token=None):
    nl, r, c = w.shape
    tr = _ew_rows(r, 256)

    def body(*refs):
        w_ref, m_ref, v_ref = refs[0], refs[1], refs[2]
        g_refs = refs[3:3 + nl]
        d_ref, nm_ref, nv_ref = refs[-3:]
        layer = pl.program_id(0)
        g = g_refs[0][...]
        for j in range(1, nl):
            g = jnp.where(layer == j, g_refs[j][...], g)
        nm = ADAM_B1 * m_ref[...] + (1.0 - ADAM_B1) * g
        nv = ADAM_B2 * v_ref[...] + (1.0 - ADAM_B2) * (g * g)
        m_hat = nm / ADAM_BC1
        v_hat = nv / ADAM_BC2
        d_ref[...] = -ADAM_LR * (m_hat / (jnp.sqrt(v_hat) + ADAM_EPS) + ADAM_WD * w_ref[...])
        nm_ref[...] = nm
        nv_ref[...] = nv

    p_spec = pl.BlockSpec((None, tr, c), lambda l, i: (l, i, 0))
    g_spec = pl.BlockSpec((tr, c), lambda l, i: (i, 0))
    extra = [] if token is None else [token]
    extra_specs = [] if token is None else [ANY_SPEC]
    return _call(
        body, name=name, grid=(nl, r // tr), in_specs=[p_spec] * 3 + [g_spec] * nl + extra_specs,
        out_specs=[p_spec] * 3, out_shape=[jax.ShapeDtypeStruct((nl, r, c), F32)] * 3, compiler_params=_cp(),
    )(w, m, v, *grads, *extra)


def sum_blocks(name, gathered, n_blocks):
    r = gathered.shape[0] // n_blocks
    c = gathered.shape[1]
    tr = _ew_rows(r)
    nt = r // tr

    def body(*refs):
        acc = refs[0][...]
        for j in range(1, n_blocks):
            acc = acc + refs[j][...]
        refs[-1][...] = acc

    specs = [pl.BlockSpec((tr, c), lambda i, j=j: (j * nt + i, 0)) for j in range(n_blocks)]
    return _call(
        body, name=name, grid=(nt,), in_specs=specs, out_specs=pl.BlockSpec((tr, c), lambda i: (i, 0)),
        out_shape=jax.ShapeDtypeStruct((r, c), F32), compiler_params=_cp(),
    )(*([gathered] * n_blocks))


def _place():
    return lax.axis_index("x"), lax.axis_index("y"), lax.axis_index("c")


def _other_chips(x, y):
    return [(1 - x, y), (x, 1 - y), (1 - x, 1 - y)]


HBM_SPEC = pl.BlockSpec(memory_space=pltpu.HBM)
SEM_SPEC = pl.BlockSpec(memory_space=pltpu.SEMAPHORE)
ANY_SPEC = pl.BlockSpec(memory_space=pl.ANY)
EFFECT = pltpu.SideEffectType.DATAFLOW_SIDE_EFFECTING


def copies_start(name, bufs, plan, n_copies):
    n = len(bufs)

    def body(*refs):
        send, recv = refs[n], refs[n + 1]
        token = refs[2 * n + 2]
        for k, (src, dst, peer, _) in enumerate(plan(refs[:n])):
            pltpu.make_async_remote_copy(
                src_ref=src, dst_ref=dst, send_sem=send.at[k], recv_sem=recv.at[k],
                device_id=peer, device_id_type=MESH).start()
        token[...] = jnp.zeros(token.shape, F32)

    outs = pl.pallas_call(
        body, name=name,
        out_shape=(pltpu.SemaphoreType.DMA((n_copies,)), pltpu.SemaphoreType.DMA((n_copies,)),
                   *[pltpu.HBM(b.shape, b.dtype) for b in bufs], jax.ShapeDtypeStruct((8, LANES), F32)),
        in_specs=[HBM_SPEC] * n,
        out_specs=(SEM_SPEC, SEM_SPEC, *([HBM_SPEC] * n), pl.BlockSpec(memory_space=pltpu.VMEM)),
        input_output_aliases={a: a + 2 for a in range(n)},
        compiler_params=pltpu.CompilerParams(has_side_effects=EFFECT),
    )(*[_in_hbm(b) for b in bufs])
    return outs[0], outs[1], list(outs[2:2 + n]), outs[2 + n]


def copies_wait(name, bufs, send, recv, plan, sem_base, after):
    n = len(bufs)

    def body(*refs):
        send_ref, recv_ref = refs[n], refs[n + 1]
        for k, (src, _, peer, land) in enumerate(plan(refs[:n])):
            cp = pltpu.make_async_remote_copy(
                src_ref=src, dst_ref=land, send_sem=send_ref.at[sem_base + k], recv_sem=recv_ref.at[sem_base + k],
                device_id=peer, device_id_type=MESH)
            cp.wait_send()
            cp.wait_recv()

    outs = pl.pallas_call(
        body, name=name,
        out_shape=tuple(pltpu.HBM(b.shape, b.dtype) for b in bufs),
        in_specs=[HBM_SPEC] * n + [SEM_SPEC, SEM_SPEC, ANY_SPEC], out_specs=tuple([HBM_SPEC] * n),
        input_output_aliases={a: a for a in range(n)},
        compiler_params=pltpu.CompilerParams(has_side_effects=EFFECT),
    )(*bufs, send, recv, after)
    return list(outs)


def gather_plan(refs):
    x, y, c = _place()
    me = 2 * x + y
    return [(buf.at[me], buf.at[me], (cx, cy, c), buf.at[2 * cx + cy])
            for buf in refs for cx, cy in _other_chips(x, y)]


def all_plan(refs):
    x, y, c = _place()
    me = 4 * x + 2 * y + c
    out = []
    for buf in refs:
        for flip in range(1, 8):
            px = 1 - x if flip & 4 else x
            py = 1 - y if flip & 2 else y
            pc = 1 - c if flip & 1 else c
            out.append((buf.at[me], buf.at[me], (px, py, pc), buf.at[4 * px + 2 * py + pc]))
    return out


def swap_plan(refs):
    x, y, c = _place()
    n = len(refs) // 2
    out = []
    for g, land in zip(refs[:n], refs[n:]):
        hr = g.shape[1] // 2
        out.append((g.at[:, pl.ds((1 - c) * hr, hr)], land, (x, y, 1 - c), land))
    return out


def owners_plan(refs):
    x, y, c = _place()
    n = len(refs) // 2
    return [(src.at[2 * cx + cy], land.at[j], (cx, cy, c), land.at[j])
            for src, land in zip(refs[:n], refs[n:]) for j, (cx, cy) in enumerate(_other_chips(x, y))]


def join_plan(refs):
    x, y, c = _place()
    out = []
    for buf in refs:
        hr = buf.shape[0] // 2
        mine = buf.at[pl.ds(c * hr, hr)]
        out.append((mine, mine, (x, y, 1 - c), buf.at[pl.ds((1 - c) * hr, hr)]))
    return out


def add_halves(name, grad, landed, sc_idx):
    _, r, c = grad.shape
    hr = r // 2
    tr = _ew_rows(hr)
    nt = hr // tr

    def body(sc_ref, g_ref, l_ref, own_ref, wire_ref):
        tot = g_ref[...] + l_ref[...]
        wire_ref[...] = tot.astype(BF16)

        @pl.when(pl.program_id(1) == sc_ref[0])
        def _():
            own_ref[...] = tot

    grid_spec = pltpu.PrefetchScalarGridSpec(
        num_scalar_prefetch=1, grid=(nt, N_SHARD),
        in_specs=[pl.BlockSpec((None, tr, c), lambda i, sh, sc_ref: (sh, sc_ref[1] * nt + i, 0)),
                  pl.BlockSpec((None, tr, c), lambda i, sh, sc_ref: (sh, i, 0))],
        out_specs=[pl.BlockSpec((tr, c), lambda i, sh, sc_ref: (i, 0)),
                   pl.BlockSpec((None, tr, c), lambda i, sh, sc_ref: (sh, i, 0))])
    return _call(
        body, name=name, grid_spec=grid_spec,
        out_shape=[jax.ShapeDtypeStruct((hr, c), F32), jax.ShapeDtypeStruct((N_SHARD, hr, c), BF16)],
        compiler_params=_cp(),
    )(sc_idx, grad, landed)


def add_owned(name, own, landed, sc_idx):
    hr, c = own.shape
    tr = _ew_rows(hr)
    nt = hr // tr

    def body(sc_ref, o_ref, l0, l1, l2, out_ref):
        out_ref[...] = ((o_ref[...] + l0[...].astype(F32)) + l1[...].astype(F32)) + l2[...].astype(F32)

    grid_spec = pltpu.PrefetchScalarGridSpec(
        num_scalar_prefetch=1, grid=(nt,),
        in_specs=[pl.BlockSpec((tr, c), lambda i, sc_ref: (i, 0))]
        + [pl.BlockSpec((None, tr, c), lambda i, sc_ref, j=j: (j, i, 0)) for j in range(3)],
        out_specs=pl.BlockSpec((tr, c), lambda i, sc_ref: (sc_ref[1] * nt + i, 0)))
    return _call(
        body, name=name, grid_spec=grid_spec, out_shape=jax.ShapeDtypeStruct((2 * hr, c), F32),
        compiler_params=_cp(),
    )(sc_idx, own, landed, landed, landed)


PACK_QUANTUM = 8 * LANES


def _pack(arrays):
    pieces = []
    for a in arrays:
        flat = a.reshape(-1)
        padded = -(-flat.shape[0] // PACK_QUANTUM) * PACK_QUANTUM
        pieces.append(jnp.pad(flat, (0, padded - flat.shape[0])).reshape(-1, LANES))
    return jnp.concatenate(pieces, axis=0)


def _unpack(packed, shapes):
    out = []
    row = 0
    for shp in shapes:
        size = math.prod(shp)
        rows = -(-size // PACK_QUANTUM) * 8
        out.append(packed[row:row + rows].reshape(-1)[:size].reshape(shp))
        row += rows
    return out


def kernel(x, p, mix_w_in, pool_w, pool_scale, conv_dw_w, conv_dw_b, conv_ln_g, conv_ln_b, mix_w_out, attn_w_qkv, attn_rel_bias, attn_w_o, ln_mix_g, ln_mix_b, ffn_w_up, ffn_dw_w, ffn_dw_b, ffn_w_down, ple_w_proj, ple_w_gate, ple_b_gate, ln_ffn_g, ln_ffn_b, loss_target, m_mix_w_in, m_pool_w, m_pool_scale, m_conv_dw_w, m_conv_dw_b, m_conv_ln_g, m_conv_ln_b, m_mix_w_out, m_attn_w_qkv, m_attn_rel_bias, m_attn_w_o, m_ln_mix_g, m_ln_mix_b, m_ffn_w_up, m_ffn_dw_w, m_ffn_dw_b, m_ffn_w_down, m_ple_w_proj, m_ple_w_gate, m_ple_b_gate, m_ln_ffn_g, m_ln_ffn_b, v_mix_w_in, v_pool_w, v_pool_scale, v_conv_dw_w, v_conv_dw_b, v_conv_ln_g, v_conv_ln_b, v_mix_w_out, v_attn_w_qkv, v_attn_rel_bias, v_attn_w_o, v_ln_mix_g, v_ln_mix_b, v_ffn_w_up, v_ffn_dw_w, v_ffn_dw_b, v_ffn_w_down, v_ple_w_proj, v_ple_w_gate, v_ple_b_gate, v_ln_ffn_g, v_ln_ffn_b):
    xi, yi, ci = _place()
    shard_idx = (2 * xi + yi).astype(jnp.int32)
    s_arr = shard_idx.reshape(1)
    c_arr = ci.astype(jnp.int32).reshape(1)
    sc_arr = jnp.concatenate([s_arr, c_arr])

    x0 = x[0]
    target = loss_target[0]
    p_rows = p.reshape(p.shape[0] * p.shape[2], p.shape[3])
    seq = x0.shape[0]

    big = [
        ("mix_w_in", mix_w_in, m_mix_w_in, v_mix_w_in, True),
        ("mix_w_out", mix_w_out, m_mix_w_out, v_mix_w_out, False),
        ("attn_w_qkv", attn_w_qkv, m_attn_w_qkv, v_attn_w_qkv, True),
        ("attn_w_o", attn_w_o, m_attn_w_o, v_attn_w_o, False),
        ("ffn_w_up", ffn_w_up, m_ffn_w_up, v_ffn_w_up, True),
        ("ffn_w_down", ffn_w_down, m_ffn_w_down, v_ffn_w_down, False),
        ("ple_w_proj", ple_w_proj, m_ple_w_proj, v_ple_w_proj, True),
        ("ple_w_gate", ple_w_gate, m_ple_w_gate, v_ple_w_gate, False),
    ]
    params = {nm: w for nm, w, _, _, _ in big}
    col_sharded = {nm: cs for nm, _, _, _, cs in big}
    keys = [("mix_w_in", 0), ("mix_w_out", 0), ("ffn_w_up", 0), ("ffn_w_down", 0), ("ple_w_gate", 0),
            ("ple_w_proj", 0), ("attn_w_qkv", 0), ("attn_w_o", 0), ("ffn_w_up", 1), ("ffn_w_down", 1),
            ("ple_w_gate", 1), ("ple_w_proj", 1)]
    dw_shapes = [conv_dw_w.shape, ffn_dw_w.shape]
    dw_block = cast_into_gathered("place_dw", _pack([conv_dw_w, ffn_dw_w])[None], 0, s_arr, dtype=F32)
    n_first = 2
    started = {}
    gather_token = None
    for tag, group in (("first", keys[:n_first]), ("rest", keys[n_first:])):
        shards = [cast_into_gathered(f"cast_{nm}_{layer}", params[nm], layer, s_arr, token=gather_token)
                  for nm, layer in group]
        if tag == "first":
            shards.append(dw_block)
        send, recv, bufs, gather_token = copies_start(f"gather_start_{tag}", shards, gather_plan, 3 * len(shards))
        for a, key in enumerate(group):
            started[key] = (send, recv, bufs[a], 3 * a)
        if tag == "first":
            dw_started = (send, recv, bufs[-1], 3 * len(group))
    arrived_w = {}

    def weight(nm, layer, after=None):
        key = (nm, layer)
        if key not in arrived_w:
            send, recv, buf, base = started[key]
            arrived_w[key] = copies_wait(f"gather_wait_{nm}_{layer}", [buf], send, recv, gather_plan, base, after)[0]
        g = arrived_w[key]
        if col_sharded[nm]:
            return g
        return g.reshape(g.shape[0] * g.shape[1], g.shape[2])

    def tie(a, token):
        return a + token[0:1, 0:1].astype(a.dtype)

    class Reducer:
        def __init__(self, tag, group):
            self.tag, self.group, self.stage = tag, group, 0
            self.n = len(group)
            self.result = None

        def advance(self, after):
            tag, n = self.tag, self.n
            if self.stage == 0:
                grads = []
                for key in self.group:
                    g = big_grads[key]
                    grads.append(g if g.ndim == 3 else g.reshape(N_SHARD, g.shape[0] // N_SHARD, g.shape[1]))
                lands = [lax.empty((N_SHARD, g.shape[1] // 2, g.shape[2]), F32) for g in grads]
                self.sems = copies_start(f"swap_start_{tag}", grads + lands, swap_plan, n)
            elif self.stage == 1:
                send, recv, bufs, _ = self.sems
                outs = copies_wait(f"swap_wait_{tag}", bufs, send, recv, swap_plan, 0, after)
                self.own, wire = [], []
                for key, g, ld in zip(self.group, outs[:n], outs[n:]):
                    o, ob = add_halves(f"add_halves_{key[0]}_{key[1]}", g, ld, sc_arr)
                    self.own.append(o)
                    wire.append(ob)
                lands = [lax.empty((3,) + w.shape[1:], BF16) for w in wire]
                self.sems = copies_start(f"owners_start_{tag}", wire + lands, owners_plan, 3 * n)
            elif self.stage == 2:
                send, recv, bufs, _ = self.sems
                outs = copies_wait(f"owners_wait_{tag}", bufs, send, recv, owners_plan, 0, after)
                finals = [add_owned(f"add_owned_{key[0]}_{key[1]}", o, ar, sc_arr)
                          for key, o, ar in zip(self.group, self.own, outs[n:])]
                self.sems = copies_start(f"join_start_{tag}", finals, join_plan, n)
            elif self.stage == 3:
                send, recv, bufs, _ = self.sems
                outs = copies_wait(f"join_wait_{tag}", bufs, send, recv, join_plan, 0, after)
                self.result = dict(zip(self.group, outs))
                self.sems = None
            self.stage += 1
            return None if self.sems is None else self.sems[3]

    dw_cache = []

    def conv_weights(after):
        if not dw_cache:
            send, recv, buf, base = dw_started
            dw_all = copies_wait("gather_wait_dw", [buf], send, recv, gather_plan, base, after)[0]
            dw_parts = [_unpack(dw_all[k], dw_shapes) for k in range(N_SHARD)]
            dw_cache.append(jnp.concatenate([pc[0] for pc in dw_parts], axis=2)[0])
            dw_cache.append(jnp.concatenate([pc[1] for pc in dw_parts], axis=2))
        return dw_cache

    big_grads = {}
    small_grads = {}

    saved = []
    h_in = x0
    for layer in range(N_LAYERS):
        sv = {"x_in": h_in}
        if layer % 2 == 0:
            u = mm_cols_fwd("mix_in", h_in, weight("mix_w_in", 0, gather_token), F32)
            conv_w_full, ffn_dw_full = conv_weights(u)
            cat, d_sv, e_sv, glu_sv, hh_sv, rs_sv = mixer_fwd(
                "mixer_fwd", u, pool_w[0], pool_scale, conv_w_full, conv_dw_b, conv_ln_g, conv_ln_b)
            mix = mm_rows_fwd("mix_out", cat, weight("mix_w_out", 0, cat))
            sv.update(u=u, cat=cat, d=d_sv, e=e_sv, glu=glu_sv, hh=hh_sv, rs=rs_sv)
        else:
            qkvp = mm_cols_fwd("attn_qkv", h_in, weight("attn_w_qkv", 0, h_in), BF16,
                               pad_blocks=PAD_ROWS // _row_tile(seq))
            bias = bias_tile("bias_tile", _bias_line(attn_rel_bias[0]))
            att = attn_fwd("attn_fwd", qkvp, bias)
            mix = mm_rows_fwd("attn_out", att, weight("attn_w_o", 0, att))
            sv.update(qkvp=qkvp, bias=bias, att=att)
        x1, xh1, rs1 = ln_fwd(f"ln_mix_{layer}", h_in, mix, ln_mix_g[layer:layer + 1], ln_mix_b[layer:layer + 1])
        gv = mm_cols_fwd(f"ffn_up_{layer}", x1, weight("ffn_w_up", layer, x1), F32)
        hid = ffn_act_fwd(f"ffn_act_{layer}", gv, ffn_dw_full[layer], ffn_dw_b[layer:layer + 1])
        ffn = mm_rows_fwd(f"ffn_down_{layer}", hid, weight("ffn_w_down", layer, hid))
        pgl = mm_rows_fwd(f"ple_gate_{layer}", x1, weight("ple_w_gate", layer, ffn))
        pp = mm_cols_fwd(f"ple_proj_{layer}", p_rows, weight("ple_w_proj", layer, pgl), F32, part=(layer, N_LAYERS))
        bg = ple_b_gate[layer:layer + 1]
        x2, xh2, rs2 = ln_fwd(f"ln_ffn_{layer}", x1, ffn, ln_ffn_g[layer:layer + 1], ln_ffn_b[layer:layer + 1],
                              ple=(pgl, pp, bg), emit_y=layer < N_LAYERS - 1)
        sv.update(x1=x1, xh1=xh1, rs1=rs1, gv=gv, hid=hid, pgl=pgl, pp=pp, xh2=xh2, rs2=rs2)
        saved.append(sv)
        h_in = x2

    reducers = []

    def open_group(tag, group):
        reducers.append(Reducer(tag, group))
        return reducers[-1].advance(None)

    def hook(after):
        token = None
        for red in reducers:
            if red.stage < 4:
                tk = red.advance(after)
                if tk is not None:
                    token = tk if token is None else token + tk
        return token

    def tied(a, token):
        return a if token is None else tie(a, token)

    parts = []
    token = None
    for layer in reversed(range(N_LAYERS)):
        sv = saved[layer]
        bg = ple_b_gate[layer:layer + 1]
        if layer == 0:
            token = open_group("layer1", [("attn_w_qkv", 0), ("attn_w_o", 0), ("ffn_w_up", 1), ("ffn_w_down", 1),
                                          ("ple_w_gate", 1), ("ple_w_proj", 1)])
        last = layer == N_LAYERS - 1
        res = ln_bwd(
            f"ln_ffn_bwd_{layer}", parts, sv["xh2"], sv["rs2"], tied(ln_ffn_g[layer:layer + 1], token),
            ple=(sv["pgl"], sv["pp"], bg), loss=(target, ln_ffn_b[layer:layer + 1]) if last else None)
        dz2, dg2, db2, dpp, dpgl, dbg = res[:6]
        if last:
            loss_part = res[6]
        small_grads[("ln_ffn_g", layer)] = dg2
        small_grads[("ln_ffn_b", layer)] = db2
        small_grads[("ple_b_gate", layer)] = dbg
        w_down = weight("ffn_w_down", layer)
        dhid = mm_rows_dx(f"ffn_down_dx_{layer}", dz2, w_down)
        big_grads[("ffn_w_down", layer)] = mm_rows_dw(f"ffn_down_dw_{layer}", sv["hid"], dz2)
        token = hook(big_grads[("ffn_w_down", layer)])
        dgv, ddw, ddb = ffn_act_bwd(f"ffn_act_bwd_{layer}", dhid, sv["gv"], ffn_dw_full[layer],
                                    tied(ffn_dw_b[layer:layer + 1], token))
        small_grads[("ffn_dw_w", layer)] = ddw
        small_grads[("ffn_dw_b", layer)] = ddb
        big_grads[("ffn_w_up", layer)] = mm_cols_dw(f"ffn_up_dw_{layer}", sv["x1"], dgv)
        t_up = mm_cols_dx(f"ffn_up_dx_{layer}", dgv, weight("ffn_w_up", layer))
        token = hook(t_up)
        big_grads[("ple_w_gate", layer)] = mm_rows_dw(f"ple_gate_dw_{layer}", sv["x1"], dpgl)
        t_gate = mm_rows_dx(f"ple_gate_dx_{layer}", dpgl, weight("ple_w_gate", layer))
        big_grads[("ple_w_proj", layer)] = mm_cols_dw(f"ple_proj_dw_{layer}", p_rows, dpp, part=(layer, N_LAYERS))
        token2 = hook(big_grads[("ple_w_proj", layer)])
        if token2 is not None:
            token = token2 if token is None else token + token2
        if layer == 0:
            token3 = open_group("layer0_ffn", [("ffn_w_up", 0), ("ffn_w_down", 0), ("ple_w_gate", 0), ("ple_w_proj", 0)])
            token = token3 if token is None else token + token3
        dz1, dg1, db1 = ln_bwd(
            f"ln_mix_bwd_{layer}", [(ALPHA, dz2), (1.0, t_up), (1.0, t_gate)], sv["xh1"], sv["rs1"],
            tied(ln_mix_g[layer:layer + 1], token))
        small_grads[("ln_mix_g", layer)] = dg1
        small_grads[("ln_mix_b", layer)] = db1
        if layer % 2 == 0:
            dcat = mm_rows_dx("mix_out_dx", dz1, weight("mix_w_out", 0))
            big_grads[("mix_w_out", 0)] = mm_rows_dw("mix_out_dw", sv["cat"], dz1)
            token = hook(big_grads[("mix_w_out", 0)])
            du, dpw, dps, dcw, dcb, dcg, dcbt = mixer_bwd(
                "mixer_bwd", dcat, sv["u"], sv["d"], sv["e"], sv["glu"], sv["hh"], sv["rs"],
                pool_w[0], pool_scale, conv_w_full, tied(conv_ln_g, token), conv_ln_b)
            small_grads[("pool_w", 0)] = dpw
            small_grads[("pool_scale", 0)] = dps
            small_grads[("conv_dw_w", 0)] = dcw
            small_grads[("conv_dw_b", 0)] = dcb
            small_grads[("conv_ln_g", 0)] = dcg
            small_grads[("conv_ln_b", 0)] = dcbt
            big_grads[("mix_w_in", 0)] = mm_cols_dw("mix_in_dw", sv["x_in"], du)
            hook(big_grads[("mix_w_in", 0)])
            open_group("layer0_mix", [("mix_w_in", 0), ("mix_w_out", 0)])
            dx_in = mm_cols_dx("mix_in_dx", du, weight("mix_w_in", 0), addend=(ALPHA, dz1))
            token = hook(dx_in)
        else:
            do = mm_rows_dx("attn_out_dx", dz1, weight("attn_w_o", 0), out_dtype=BF16)
            big_grads[("attn_w_o", 0)] = mm_rows_dw("attn_out_dw", sv["att"], dz1)
            dq, dk, dv, ds_sum = attn_bwd("attn_bwd", sv["qkvp"], sv["bias"], do)
            cols, sat = bias_grad_reduce("bias_grad", _shear_for_bias_grad(ds_sum))
            d_rel = jnp.concatenate(
                [jnp.zeros((N_HEADS, 1), F32),
                 jnp.flip(cols[:, 0, SHEAR_SAT + 1:SHEAR_W], axis=1),
                 sat[:, 0, 0:1]], axis=1)
            small_grads[("attn_rel_bias", 0)] = d_rel
            dqkv = jnp.concatenate([dq, dk, dv], axis=1)
            big_grads[("attn_w_qkv", 0)] = mm_cols_dw("attn_qkv_dw", sv["x_in"], dqkv)
            dx_in = mm_cols_dx("attn_qkv_dx", dqkv, weight("attn_w_qkv", 0), addend=(ALPHA, dz1))
        parts = [(1.0, dx_in)]
    grad_x = dx_in

    small = [
        ("pool_w", pool_w, m_pool_w, v_pool_w, None),
        ("pool_scale", pool_scale, m_pool_scale, v_pool_scale, None),
        ("conv_dw_w", conv_dw_w, m_conv_dw_w, v_conv_dw_w, 2),
        ("conv_dw_b", conv_dw_b, m_conv_dw_b, v_conv_dw_b, None),
        ("conv_ln_g", conv_ln_g, m_conv_ln_g, v_conv_ln_g, None),
        ("conv_ln_b", conv_ln_b, m_conv_ln_b, v_conv_ln_b, None),
        ("attn_rel_bias", attn_rel_bias, m_attn_rel_bias, v_attn_rel_bias, None),
        ("ln_mix_g", ln_mix_g, m_ln_mix_g, v_ln_mix_g, None),
        ("ln_mix_b", ln_mix_b, m_ln_mix_b, v_ln_mix_b, None),
        ("ffn_dw_w", ffn_dw_w, m_ffn_dw_w, v_ffn_dw_w, 2),
        ("ffn_dw_b", ffn_dw_b, m_ffn_dw_b, v_ffn_dw_b, None),
        ("ple_b_gate", ple_b_gate, m_ple_b_gate, v_ple_b_gate, None),
        ("ln_ffn_g", ln_ffn_g, m_ln_ffn_g, v_ln_ffn_g, None),
        ("ln_ffn_b", ln_ffn_b, m_ln_ffn_b, v_ln_ffn_b, None),
    ]
    full_grads = []
    for nm, w, _, _, shard_axis in small:
        full = list(w.shape)
        if shard_axis is not None:
            full[shard_axis] *= N_SHARD
        per_layer = [small_grads[(nm, layer)].reshape((1,) + tuple(full[1:])) for layer in range(w.shape[0])]
        full_grads.append(jnp.concatenate(per_layer, axis=0))
    packed = _pack(full_grads + [loss_part])
    dev_arr = (4 * xi + 2 * yi + ci).astype(jnp.int32).reshape(1)
    sg_block = cast_into_gathered("place_small_grads", packed[None], 0, dev_arr, n_blocks=8, dtype=F32)
    sg_send, sg_recv, sg_bufs, sg_token = copies_start("small_grads_start", [sg_block], all_plan, 7)
    token = sg_token if token is None else token + sg_token

    shard_grads = {}
    for red in reducers:
        if red.stage == 4:
            shard_grads.update(red.result)
    big_out = {}

    def update_big(names, tok):
        for nm, w, m, v, _ in big:
            if nm in names:
                gl = [shard_grads[(nm, layer)] for layer in range(w.shape[0])]
                delta, new_m, new_v = adamw(f"adamw_{nm}", w, gl, m, v, token=tok)
                big_out[nm] = (jnp.stack(gl, axis=0), delta, new_m, new_v)

    last_group = ("mix_w_in", "mix_w_out")
    update_big([nm for nm, _, _, _, _ in big if nm not in last_group], token)
    token = hook(big_out["ffn_w_up"][1])

    gathered_sg = copies_wait("small_grads_wait", sg_bufs, sg_send, sg_recv, all_plan, 0, big_out["ffn_w_down"][1])[0]
    total = sum_blocks("sum_small", gathered_sg.reshape(8 * packed.shape[0], LANES), 8)
    unpacked = _unpack(total, [g.shape for g in full_grads] + [loss_part.shape])
    loss = unpacked[-1][0, 0]
    local_grads = []
    for (nm, w, _, _, shard_axis), g in zip(small, unpacked[:-1]):
        if shard_axis is not None:
            width = w.shape[shard_axis]
            g = lax.dynamic_slice_in_dim(g, shard_idx * width, width, axis=shard_axis)
        local_grads.append(g.reshape(w.shape))
    shapes = [w.shape for _, w, _, _, _ in small]
    pg = _pack(local_grads)
    pw = _pack([w for _, w, _, _, _ in small])
    pm = _pack([m for _, _, m, _, _ in small])
    pv = _pack([v for _, _, _, v, _ in small])
    delta_s, new_m_s, new_v_s = adamw("adamw_small", pw[None], [pg], pm[None], pv[None], token=token)
    hook(delta_s)
    for red in reducers:
        shard_grads.update(red.result)
    update_big(last_group, None)
    small_out = {}
    for (nm, _, _, _, _), g, d_, m_, v_ in zip(
            small, local_grads, _unpack(delta_s[0], shapes), _unpack(new_m_s[0], shapes), _unpack(new_v_s[0], shapes)):
        small_out[nm] = (g, d_, m_, v_)

    order = ["mix_w_in", "pool_w", "pool_scale", "conv_dw_w", "conv_dw_b", "conv_ln_g", "conv_ln_b", "mix_w_out",
             "attn_w_qkv", "attn_rel_bias", "attn_w_o", "ln_mix_g", "ln_mix_b", "ffn_w_up", "ffn_dw_w", "ffn_dw_b",
             "ffn_w_down", "ple_w_proj", "ple_w_gate", "ple_b_gate", "ln_ffn_g", "ln_ffn_b"]
    res = {**big_out, **small_out}
    outs = [loss, grad_x[None]]
    for slot in range(4):
        outs += [res[nm][slot] for nm in order]
    return tuple(outs)
```

```python
import functools
import math

import jax
import jax.numpy as jnp
from jax import lax
from jax.experimental import pallas as pl
from jax.experimental.pallas import tpu as pltpu

F32 = jnp.float32
BF16 = jnp.bfloat16
MESH = pl.DeviceIdType.MESH

N_LAYERS = 2
ALPHA = (2 * N_LAYERS) ** 0.25
LN_EPS = 1e-5
NEG_INF = -1e30
CHUNK = 64
LEFT_CHUNKS = 8
PAD_ROWS = LEFT_CHUNKS * CHUNK
HEAD_DIM = 64
N_HEADS = 16
MAX_REL = 256
POOL_WINDOWS = (2, 4, 8, 16)
POOL_GROUP = 128
CONV_K = 31
FFN_K = 3
CONV_HALO = 32
FFN_HALO = 8
FFN_TILE = 256
FFN_CHUNK_ROWS = 32
FFN_CHUNK_LANES = 256
Q_TILE = 256
K_WIN = Q_TILE + PAD_ROWS
LANES = 128
SUBLANES = 8
ATTN_PAIRS = 2
ATTN_LANES = ATTN_PAIRS * LANES
SHEAR_W = Q_TILE + K_WIN
SHEAR_SAT = SHEAR_W - 2 * MAX_REL
N_SHARD = 4

ADAM_LR = 0.001
ADAM_B1 = 0.9
ADAM_B2 = 0.999
ADAM_EPS = 1e-08
ADAM_WD = 0.01
ADAM_STEP = 10
ADAM_BC1 = 1.0 - ADAM_B1 ** ADAM_STEP
ADAM_BC2 = 1.0 - ADAM_B2 ** ADAM_STEP

DIMS = {
    "nn": (((1,), (0,)), ((), ())),
    "nt": (((1,), (1,)), ((), ())),
    "tn": (((0,), (0,)), ((), ())),
}


def _cp(vmem_mb=48, **kw):
    return pltpu.CompilerParams(vmem_limit_bytes=vmem_mb * 1024 * 1024, **kw)


def _in_hbm(a):
    return pltpu.with_memory_space_constraint(a, pltpu.HBM)


STAGING_LIMIT_BYTES = 1 << 20
SUM_BLOCK_ROWS = 2048
SMALL_BLOCK_BYTES = 1 << 19


def _call(body, **kw):
    call = pl.pallas_call(body, **kw)

    def run(*args):
        pinned = []
        for a in args:
            big = a.size * a.dtype.itemsize >= STAGING_LIMIT_BYTES
            pinned.append(_in_hbm(a) if big and not jnp.issubdtype(a.dtype, jnp.integer) else a)
        return call(*pinned)

    return run


def _dot(a, b, mode):
    return lax.dot_general(a.astype(BF16), b.astype(BF16), DIMS[mode], preferred_element_type=F32)


def _sig(x):
    return 1.0 / (1.0 + jnp.exp(-x))


def _row_tile(s):
    return min(512, s // 4)


def _mm_tile(s):
    return min(1024, s // 4)


def _mm(name, mode, a, b, in_specs, out_shape, out_spec, acc_shape, grid, nk, zero_first=False, vmem_mb=48,
        addend=None):
    out_f32 = out_shape.dtype == F32

    def body(a_ref, b_ref, *rest):
        k = pl.program_id(2)
        if addend is None:
            o_ref, scr = rest[0], rest[1:]
        else:
            add_ref, o_ref, scr = rest[0], rest[1], rest[2:]

        def compute():
            part = _dot(a_ref[...], b_ref[...], mode)
            if nk == 1:
                if addend is not None:
                    part = part + addend[0] * add_ref[...]
                o_ref[...] = part.astype(o_ref.dtype)
                return
            acc = o_ref if out_f32 else scr[0]

            @pl.when(k == 0)
            def _():
                acc[...] = part if addend is None else part + addend[0] * add_ref[...]

            @pl.when(k > 0)
            def _():
                acc[...] += part

            if not out_f32:
                @pl.when(k == nk - 1)
                def _():
                    o_ref[...] = acc[...].astype(o_ref.dtype)

        if zero_first:
            @pl.when(pl.program_id(1) == 0)
            def _():
                o_ref[...] = jnp.zeros(o_ref.shape, o_ref.dtype)

            pl.when(pl.program_id(1) > 0)(compute)
        else:
            compute()

    scratch = [] if (nk == 1 or out_f32) else [pltpu.VMEM(acc_shape, F32)]
    operands = [a, b] if addend is None else [a, b, addend[1]]
    specs = list(in_specs) if addend is None else list(in_specs) + [out_spec]
    return _call(
        body, name=name, grid=grid, in_specs=specs, out_specs=out_spec, out_shape=out_shape,
        scratch_shapes=scratch, compiler_params=_cp(vmem_mb),
    )(*operands)


def mm_cols_fwd(name, a, wc, out_dtype, pad_blocks=0, part=(0, 1)):
    s, k = a.shape
    s //= part[1]
    n4 = wc.shape[2]
    tm = _row_tile(s) if pad_blocks else _mm_tile(s)
    nt = s // tm
    first_block = part[0] * nt
    return _mm(
        name, "nn", a, wc,
        [pl.BlockSpec((tm, k), lambda j, i, r: (first_block + jnp.maximum(i - pad_blocks, 0), 0)),
         pl.BlockSpec((None, k, n4), lambda j, i, r: (j, 0, 0))],
        jax.ShapeDtypeStruct((s + pad_blocks * tm, N_SHARD * n4), out_dtype),
        pl.BlockSpec((tm, n4), lambda j, i, r: (i, j)),
        None, (N_SHARD, nt + pad_blocks, 1), 1, zero_first=pad_blocks > 0)


def mm_cols_dx(name, dy, wc, addend=None):
    s = dy.shape[0]
    _, k, n4 = wc.shape
    tm = _mm_tile(s)
    return _mm(
        name, "nt", dy, wc,
        [pl.BlockSpec((tm, n4), lambda g, i, r: (i, r)),
         pl.BlockSpec((None, k, n4), lambda g, i, r: (r, 0, 0))],
        jax.ShapeDtypeStruct((s, k), F32),
        pl.BlockSpec((tm, k), lambda g, i, r: (i, 0)),
        (tm, k), (1, s // tm, N_SHARD), N_SHARD, addend=addend)


def mm_cols_dw(name, a, dy, part=(0, 1)):
    s, k = a.shape
    s //= part[1]
    n4 = dy.shape[1] // N_SHARD
    tm = _mm_tile(s)
    nt = s // tm
    first_block = part[0] * nt
    return _mm(
        name, "tn", a, dy,
        [pl.BlockSpec((tm, k), lambda j, g, r: (first_block + r, 0)),
         pl.BlockSpec((tm, n4), lambda j, g, r: (r, j))],
        jax.ShapeDtypeStruct((N_SHARD, k, n4), F32),
        pl.BlockSpec((None, k, n4), lambda j, g, r: (j, 0, 0)),
        (k, n4), (N_SHARD, 1, nt), nt)


def _k_tile(k):
    return k if k <= 1024 else k // 2


def mm_rows_fwd(name, a, wr, out_dtype=F32):
    s, k = a.shape
    n = wr.shape[1]
    tm = _mm_tile(s)
    tk = _k_tile(k)
    nk = k // tk
    return _mm(
        name, "nn", a, wr,
        [pl.BlockSpec((tm, tk), lambda g, i, r: (i, r)),
         pl.BlockSpec((tk, n), lambda g, i, r: (r, 0))],
        jax.ShapeDtypeStruct((s, n), out_dtype),
        pl.BlockSpec((tm, n), lambda g, i, r: (i, 0)),
        (tm, n), (1, s // tm, nk), nk)


def mm_rows_dx(name, dy, wr, out_dtype=F32):
    s, n = dy.shape
    k = wr.shape[0]
    tm = _mm_tile(s)
    tk = _k_tile(k)
    return _mm(
        name, "nt", dy, wr,
        [pl.BlockSpec((tm, n), lambda j, i, r: (i, 0)),
         pl.BlockSpec((tk, n), lambda j, i, r: (j, 0))],
        jax.ShapeDtypeStruct((s, k), out_dtype),
        pl.BlockSpec((tm, tk), lambda j, i, r: (i, j)),
        None, (k // tk, s // tm, 1), 1)


def mm_rows_dw(name, a, dy):
    s, k = a.shape
    n = dy.shape[1]
    tm = _mm_tile(s)
    tk = _k_tile(k)
    nt = s // tm
    return _mm(
        name, "tn", a, dy,
        [pl.BlockSpec((tm, tk), lambda j, g, r: (r, j)),
         pl.BlockSpec((tm, n), lambda j, g, r: (r, 0))],
        jax.ShapeDtypeStruct((k, n), F32),
        pl.BlockSpec((tk, n), lambda j, g, r: (j, 0)),
        (tk, n), (k // tk, 1, nt), nt)


def _row(tm, c, col=0):
    return pl.BlockSpec((tm, c), lambda i: (i, col))


def _full(shape):
    nd = len(shape)
    return pl.BlockSpec(shape, lambda i: (0,) * nd)


def _prev(tm, h, c, col=0):
    return pl.BlockSpec((h, c), lambda i: (jnp.maximum(i * (tm // h) - 1, 0), col))


def _next(tm, h, c, s, col=0):
    return pl.BlockSpec((h, c), lambda i: (jnp.minimum((i + 1) * (tm // h), s // h - 1), col))


def _acc_add(ref, first, val):
    @pl.when(first)
    def _():
        ref[...] = val

    @pl.when(jnp.logical_not(first))
    def _():
        ref[...] += val


def _colsum(v):
    return jnp.sum(v, axis=0, keepdims=True)


def _ln_stats(z):
    mu = jnp.mean(z, axis=-1, keepdims=True)
    zc = z - mu
    var = jnp.mean(zc * zc, axis=-1, keepdims=True)
    rstd = lax.rsqrt(var + LN_EPS)
    return zc * rstd, rstd


def _ln_bwd(dxhat, xhat, rstd):
    m1 = jnp.mean(dxhat, axis=-1, keepdims=True)
    m2 = jnp.mean(dxhat * xhat, axis=-1, keepdims=True)
    return rstd * (dxhat - m1 - xhat * m2)


def ln_fwd(name, x, f, g, b, ple=None, emit_y=True):
    s, d = x.shape
    tm = _row_tile(s)
    n_in = 2 + (3 if ple is not None else 0)

    def body(*refs):
        x_ref, f_ref = refs[0], refs[1]
        g_ref, b_ref = refs[n_in], refs[n_in + 1]
        xh_ref, rs_ref = refs[-2:]
        z = ALPHA * x_ref[...] + f_ref[...]
        if ple is not None:
            pgl_ref, pp_ref, bg_ref = refs[2:5]
            z = z + _sig(pgl_ref[...] + bg_ref[...]) * pp_ref[...]
        xhat, rstd = _ln_stats(z)
        if emit_y:
            refs[n_in + 2][...] = xhat * g_ref[...] + b_ref[...]
        xh_ref[...] = xhat
        rs_ref[...] = jnp.broadcast_to(rstd, rs_ref.shape)

    ins = [x, f]
    specs = [_row(tm, d), _row(tm, d)]
    if ple is not None:
        pgl, pp, bg = ple
        ins += [pgl, pp, bg]
        specs += [_row(tm, d), _row(tm, d), _full((1, d))]
    ins += [g, b]
    specs += [_full((1, d)), _full((1, d))]
    n_y = 1 if emit_y else 0
    outs = _call(
        body, name=name, grid=(s // tm,), in_specs=specs,
        out_specs=[_row(tm, d)] * (n_y + 1) + [_row(tm, LANES)],
        out_shape=[jax.ShapeDtypeStruct((s, d), F32)] * (n_y + 1) + [jax.ShapeDtypeStruct((s, LANES), F32)],
        compiler_params=_cp(),
    )(*ins)
    return (outs[0], outs[1], outs[2]) if emit_y else (None, outs[0], outs[1])


def ln_bwd(name, parts, xhat, rstd, g, ple=None, loss=None):
    s, d = xhat.shape
    tm = _row_tile(s)
    coefs = [c for c, _ in parts]
    n_p = len(parts)
    n_ple = 3 if ple is not None else 0
    n_in = n_p + 3 + n_ple + (2 if loss is not None else 0)

    def body(*refs):
        first = pl.program_id(0) == 0
        xh = refs[n_p][...]
        rs = refs[n_p + 1][:, 0:1]
        g_v = refs[n_p + 2][...]
        outs = refs[n_in:]
        if loss is not None:
            t_ref, b_ref = refs[n_p + 3 + n_ple:n_p + 5 + n_ple]
            err = (xh * g_v + b_ref[...]) - t_ref[...]
            dy = err * (1.0 / d)
            part = 0.5 * jnp.sum(jnp.mean(err * err, axis=-1, keepdims=True), axis=0, keepdims=True)
            _acc_add(outs[-1], first, jnp.broadcast_to(part, outs[-1].shape))
        else:
            dy = coefs[0] * refs[0][...].astype(F32)
            for j in range(1, n_p):
                dy = dy + coefs[j] * refs[j][...].astype(F32)
        dz = _ln_bwd(dy * g_v, xh, rs)
        outs[0][...] = dz
        _acc_add(outs[1], first, _colsum(dy * xh))
        _acc_add(outs[2], first, _colsum(dy))
        if ple is not None:
            pgl_ref, pp_ref, bg_ref = refs[n_p + 3:n_p + 6]
            pg = _sig(pgl_ref[...] + bg_ref[...])
            dpgl = dz * pp_ref[...] * pg * (1.0 - pg)
            outs[3][...] = (dz * pg).astype(BF16)
            outs[4][...] = dpgl.astype(BF16)
            _acc_add(outs[5], first, _colsum(dpgl))

    ins = [p for _, p in parts] + [xhat, rstd, g]
    specs = [_row(tm, d)] * n_p + [_row(tm, d), _row(tm, LANES), _full((1, d))]
    out_specs = [_row(tm, d), _full((1, d)), _full((1, d))]
    out_shape = [jax.ShapeDtypeStruct((s, d), F32), jax.ShapeDtypeStruct((1, d), F32),
                 jax.ShapeDtypeStruct((1, d), F32)]
    if ple is not None:
        pgl, pp, bg = ple
        ins += [pgl, pp, bg]
        specs += [_row(tm, d), _row(tm, d), _full((1, d))]
        out_specs += [_row(tm, d), _row(tm, d), _full((1, d))]
        out_shape += [jax.ShapeDtypeStruct((s, d), BF16), jax.ShapeDtypeStruct((s, d), BF16),
                      jax.ShapeDtypeStruct((1, d), F32)]
    if loss is not None:
        target, b = loss
        ins += [target, b]
        specs += [_row(tm, d), _full((1, d))]
        out_specs += [_full((8, LANES))]
        out_shape += [jax.ShapeDtypeStruct((8, LANES), F32)]
    return _call(
        body, name=name, grid=(s // tm,), in_specs=specs, out_specs=out_specs, out_shape=out_shape,
        compiler_params=_cp(),
    )(*ins)


def _fill_rotations(rot_ref, x, direction):
    n = x.shape[0]
    rot_ref[0] = x
    for b in range(1, SUBLANES):
        if direction < 0:
            rot_ref[b, SUBLANES:n, :] = x[SUBLANES - b:n - b]
        else:
            rot_ref[b, 0:n - SUBLANES, :] = x[b:n - SUBLANES + b]


def _rotated(rot_ref, start, rows, cs, direction=-1):
    b = (-start) % SUBLANES if direction < 0 else start % SUBLANES
    aligned = start + b if direction < 0 else start - b
    return rot_ref[b, pl.ds(aligned, rows), cs]


def _tile_pos(i, tm, rows):
    return (i * tm + lax.broadcasted_iota(jnp.int32, (rows, 1), 0) + 1).astype(F32)


def mixer_fwd(name, u, pool_w, pool_scale, conv_w, conv_b, cn_g, cn_b):
    s = u.shape[0]
    dp = 512
    tm = min(256, s // 4)
    h = CONV_HALO

    def body(a_c, a_p, bv_c, bv_p, bg_c, bg_p, pw_ref, ps_ref, cw_ref, cb_ref, cg_ref, cbt_ref,
             cat_ref, d_ref, e_ref, glu_ref, hh_ref, rs_ref, ext_a, rot_g, conv_out):
        i = pl.program_id(0)
        first = i == 0
        ext_a[0:h, :] = jnp.where(first, 0.0, a_p[...])
        ext_a[h:, :] = a_c[...]
        glu = bv_c[...] * _sig(bg_c[...])
        glu_ref[...] = glu
        _fill_rotations(rot_g, jnp.concatenate([jnp.where(first, 0.0, bv_p[...] * _sig(bg_p[...])), glu], axis=0), -1)
        pos = _tile_pos(i, tm, tm)
        for gi, w in enumerate(POOL_WINDOWS):
            cs = slice(gi * POOL_GROUP, (gi + 1) * POOL_GROUP)
            a_g = ext_a[pl.ds(h, tm), cs]
            acc = a_g
            for sh in range(1, w):
                acc = acc + ext_a[pl.ds(h - sh, tm), cs]
            d_g = acc / jnp.minimum(pos, float(w)) - a_g
            d_ref[:, cs] = d_g.astype(BF16)
            e_g = _dot(d_g, pw_ref[gi], "nn")
            e_ref[:, cs] = e_g
            cat_ref[:, cs] = (e_g * ps_ref[:, cs]).astype(BF16)
        for lg in range(dp // LANES):
            cs = slice(lg * LANES, (lg + 1) * LANES)
            acc = jnp.broadcast_to(cb_ref[:, cs], (tm, LANES))
            for sh in range(CONV_K):
                acc = acc + _rotated(rot_g, h - sh, tm, cs) * cw_ref[pl.ds(CONV_K - 1 - sh, 1), cs]
            conv_out[:, cs] = acc
        hhat, rstd = _ln_stats(conv_out[...])
        hl = hhat * cg_ref[...] + cbt_ref[...]
        cat_ref[:, dp:] = (hl * _sig(hl)).astype(BF16)
        hh_ref[...] = hhat
        rs_ref[...] = jnp.broadcast_to(rstd, rs_ref.shape)

    specs = [_row(tm, dp, 0), _prev(tm, h, dp, 0), _row(tm, dp, 1), _prev(tm, h, dp, 1),
             _row(tm, dp, 2), _prev(tm, h, dp, 2),
             _full((4, POOL_GROUP, POOL_GROUP)), _full((1, dp)), _full((CONV_K, dp)),
             _full((1, dp)), _full((1, dp)), _full((1, dp))]
    out_specs = [_row(tm, 2 * dp), _row(tm, dp), _row(tm, dp), _row(tm, dp), _row(tm, dp), _row(tm, LANES)]
    out_shape = [jax.ShapeDtypeStruct((s, 2 * dp), BF16), jax.ShapeDtypeStruct((s, dp), BF16),
                 jax.ShapeDtypeStruct((s, dp), F32), jax.ShapeDtypeStruct((s, dp), F32),
                 jax.ShapeDtypeStruct((s, dp), F32), jax.ShapeDtypeStruct((s, LANES), F32)]
    return _call(
        body, name=name, grid=(s // tm,), in_specs=specs, out_specs=out_specs, out_shape=out_shape,
        scratch_shapes=[pltpu.VMEM((h + tm, dp), F32), pltpu.VMEM((SUBLANES, h + tm, dp), F32),
                        pltpu.VMEM((tm, dp), F32)],
        compiler_params=_cp(),
    )(u, u, u, u, u, u, pool_w, pool_scale, conv_w, conv_b, cn_g, cn_b)


def mixer_bwd(name, dcat, u, d_sv, e_sv, glu_sv, hh_sv, rs_sv, pool_w, pool_scale, conv_w, cn_g, cn_b):
    s = u.shape[0]
    dp = 512
    tm = min(256, s // 4)
    h = CONV_HALO
    nt = s // tm

    def body(dc_c, dc_n, bv_c, bg_c, d_c, e_c, gl_c, gl_p, hh_c, hh_n, rs_c, rs_n,
             pw_ref, ps_ref, cw_ref, cg_ref, cbt_ref,
             du_ref, dpw_ref, dps_ref, dcw_ref, dcb_ref, dcg_ref, dcbt_ref,
             ext_dh, ext_g, ext_r):
        i = pl.program_id(0)
        first = i == 0
        last = i == nt - 1
        cg = cg_ref[...]

        def conv_grads(dyb, hhat, rstd):
            hl = hhat * cg + cbt_ref[...]
            sg = _sig(hl)
            dhl = dyb * (sg * (1.0 + hl * (1.0 - sg)))
            return _ln_bwd(dhl * cg, hhat, rstd), dhl

        hh_cur = hh_c[...]
        dh_c, dhl_c = conv_grads(dc_c[:, dp:], hh_cur, rs_c[:, 0:1])
        dh_n, _ = conv_grads(dc_n[:, dp:], hh_n[...], rs_n[:, 0:1])
        _fill_rotations(ext_dh, jnp.concatenate([dh_c, jnp.where(last, 0.0, dh_n)], axis=0), 1)
        _fill_rotations(ext_g, jnp.concatenate([jnp.where(first, 0.0, gl_p[...]), gl_c[...]], axis=0), -1)

        @pl.when(first)
        def _():
            dcw_ref[...] = jnp.zeros(dcw_ref.shape, F32)

        for lg in range(dp // LANES):
            cs = slice(lg * LANES, (lg + 1) * LANES)
            dglu = jnp.zeros((tm, LANES), F32)
            for sh in range(CONV_K):
                dglu = dglu + _rotated(ext_dh, sh, tm, cs, 1) * cw_ref[pl.ds(CONV_K - 1 - sh, 1), cs]
            dh_g = ext_dh[0, pl.ds(0, tm), cs]
            for sh in range(CONV_K):
                dcw_ref[pl.ds(CONV_K - 1 - sh, 1), cs] += _colsum(dh_g * _rotated(ext_g, h - sh, tm, cs))
            sgate = _sig(bg_c[:, cs])
            du_ref[:, dp + lg * LANES:dp + (lg + 1) * LANES] = dglu * sgate
            du_ref[:, 2 * dp + lg * LANES:2 * dp + (lg + 1) * LANES] = dglu * bv_c[:, cs] * sgate * (1.0 - sgate)
        _acc_add(dcb_ref, first, _colsum(dh_c))
        _acc_add(dcg_ref, first, _colsum(dhl_c * hh_cur))
        _acc_add(dcbt_ref, first, _colsum(dhl_c))

        pos_c = _tile_pos(i, tm, tm)
        pos_n = _tile_pos(i + 1, tm, h)
        _acc_add(dps_ref, first, _colsum(dc_c[:, :dp] * e_c[...]))
        for gi, w in enumerate(POOL_WINDOWS):
            cs = slice(gi * POOL_GROUP, (gi + 1) * POOL_GROUP)
            pw = pw_ref[gi]
            de_c = dc_c[:, cs] * ps_ref[:, cs]
            de_n = dc_n[:, cs] * ps_ref[:, cs]
            dd_c = _dot(de_c, pw, "nt")
            dd_n = _dot(de_n, pw, "nt")
            ext_r[0:tm, :] = dd_c / jnp.minimum(pos_c, float(w))
            ext_r[tm:, :] = jnp.where(last, 0.0, dd_n / jnp.minimum(pos_n, float(w)))
            acc = -dd_c
            for sh in range(w):
                acc = acc + ext_r[pl.ds(sh, tm), :]
            du_ref[:, cs] = acc
            dpw_g = _dot(d_c[:, cs], de_c, "tn")

            @pl.when(first)
            def _():
                dpw_ref[gi] = dpw_g

            @pl.when(jnp.logical_not(first))
            def _():
                dpw_ref[gi] += dpw_g

    specs = [_row(tm, 2 * dp), _next(tm, h, 2 * dp, s), _row(tm, dp, 1), _row(tm, dp, 2),
             _row(tm, dp), _row(tm, dp), _row(tm, dp), _prev(tm, h, dp),
             _row(tm, dp), _next(tm, h, dp, s), _row(tm, LANES), _next(tm, h, LANES, s),
             _full((4, POOL_GROUP, POOL_GROUP)), _full((1, dp)), _full((CONV_K, dp)),
             _full((1, dp)), _full((1, dp))]
    out_specs = [_row(tm, 3 * dp), _full((4, POOL_GROUP, POOL_GROUP)), _full((1, dp)), _full((CONV_K, dp)),
                 _full((1, dp)), _full((1, dp)), _full((1, dp))]
    out_shape = [jax.ShapeDtypeStruct((s, 3 * dp), F32),
                 jax.ShapeDtypeStruct((4, POOL_GROUP, POOL_GROUP), F32), jax.ShapeDtypeStruct((1, dp), F32),
                 jax.ShapeDtypeStruct((CONV_K, dp), F32), jax.ShapeDtypeStruct((1, dp), F32),
                 jax.ShapeDtypeStruct((1, dp), F32), jax.ShapeDtypeStruct((1, dp), F32)]
    return _call(
        body, name=name, grid=(nt,), in_specs=specs, out_specs=out_specs, out_shape=out_shape,
        scratch_shapes=[pltpu.VMEM((SUBLANES, tm + h, dp), F32), pltpu.VMEM((SUBLANES, h + tm, dp), F32),
                        pltpu.VMEM((tm + h, POOL_GROUP), F32)],
        compiler_params=_cp(),
    )(dcat, dcat, u, u, d_sv, e_sv, glu_sv, glu_sv, hh_sv, hh_sv, rs_sv, rs_sv,
      pool_w, pool_scale, conv_w, cn_g, cn_b)


GELU_C = math.sqrt(2.0 / math.pi)


def _gelu_parts(x):
    x2 = x * x
    t = jnp.tanh(x * (GELU_C + (GELU_C * 0.044715) * x2))
    half_1pt = 0.5 + 0.5 * t
    gelu = x * half_1pt
    dgelu = half_1pt + (0.5 * x) * (1.0 - t * t) * (GELU_C + (3.0 * GELU_C * 0.044715) * x2)
    return gelu, dgelu


def ffn_act_fwd(name, gv, dw_w, dw_b):
    s = gv.shape[0]
    dff = gv.shape[1] // 2
    tm = min(FFN_TILE, s // 4)
    h = FFN_HALO
    rc = FFN_CHUNK_ROWS
    lw = FFN_CHUNK_LANES

    def body(g_c, g_p, v_c, w_ref, b_ref, hid_ref):
        first = pl.program_id(0) == 0

        def chunk(ci, carry):
            r0 = pl.multiple_of(ci * rc, rc)
            above = pl.multiple_of(jnp.maximum(r0 - h, 0), h)
            for lg in range(dff // lw):
                cs = slice(lg * lw, (lg + 1) * lw)
                top = jnp.where(ci == 0, jnp.where(first, 0.0, g_p[:, cs]), g_c[pl.ds(above, h), cs])
                win = jnp.concatenate([top, g_c[pl.ds(r0, rc), cs]], axis=0)
                gc = jnp.broadcast_to(b_ref[:, cs], (rc, lw))
                for sh in range(FFN_K):
                    gc = gc + win[h - sh:h - sh + rc] * w_ref[pl.ds(FFN_K - 1 - sh, 1), cs]
                gelu, _ = _gelu_parts(gc)
                hid_ref[pl.ds(r0, rc), cs] = (gelu * v_c[pl.ds(r0, rc), cs]).astype(BF16)
            return carry

        lax.fori_loop(0, tm // rc, chunk, 0)

    return _call(
        body, name=name, grid=(s // tm,),
        in_specs=[_row(tm, dff, 0), _prev(tm, h, dff, 0), _row(tm, dff, 1), _full((FFN_K, dff)), _full((1, dff))],
        out_specs=_row(tm, dff), out_shape=jax.ShapeDtypeStruct((s, dff), BF16),
        compiler_params=_cp(),
    )(gv, gv, gv, dw_w, dw_b)


def ffn_act_bwd(name, dhid, gv, dw_w, dw_b):
    s = gv.shape[0]
    dff = gv.shape[1] // 2
    tm = min(FFN_TILE, s // 4)
    h = FFN_HALO
    nt = s // tm
    rc = FFN_CHUNK_ROWS
    lw = FFN_CHUNK_LANES
    n_chunks = tm // rc

    def body(dh_c, dh_n, g_p, g_c, g_n, v_c, v_n, w_ref, b_ref, dgv_ref, dw_ref, db_ref):
        i = pl.program_id(0)
        first = i == 0
        last = i == nt - 1

        @pl.when(first)
        def _():
            dw_ref[...] = jnp.zeros(dw_ref.shape, F32)
            db_ref[...] = jnp.zeros(db_ref.shape, F32)

        def chunk(ci, carry):
            r0 = pl.multiple_of(ci * rc, rc)
            above = pl.multiple_of(jnp.maximum(r0 - h, 0), h)
            below = pl.multiple_of(jnp.minimum(r0 + rc, tm - h), h)
            at_end = ci == n_chunks - 1
            for lg in range(dff // lw):
                cs = slice(lg * lw, (lg + 1) * lw)
                top = jnp.where(ci == 0, jnp.where(first, 0.0, g_p[:, cs]), g_c[pl.ds(above, h), cs])
                bot = jnp.where(at_end, g_n[:, cs], g_c[pl.ds(below, h), cs])
                win = jnp.concatenate([top, g_c[pl.ds(r0, rc), cs], bot], axis=0)
                shifted = [win[h - sh:h - sh + rc + h] for sh in range(FFN_K)]
                gc = jnp.broadcast_to(b_ref[:, cs], (rc + h, lw))
                for sh in range(FFN_K):
                    gc = gc + shifted[sh] * w_ref[pl.ds(FFN_K - 1 - sh, 1), cs]
                gelu, dgelu = _gelu_parts(gc)
                dh_mid = dh_c[pl.ds(r0, rc), cs]
                hv_bot = jnp.where(at_end, jnp.where(last, 0.0, dh_n[:, cs] * v_n[:, cs]),
                                   dh_c[pl.ds(below, h), cs] * v_c[pl.ds(below, h), cs])
                dgc = jnp.concatenate([dh_mid * v_c[pl.ds(r0, rc), cs], hv_bot], axis=0) * dgelu
                dgate = jnp.zeros((rc, lw), F32)
                for sh in range(FFN_K):
                    dgate = dgate + dgc[sh:sh + rc] * w_ref[pl.ds(FFN_K - 1 - sh, 1), cs]
                dgv_ref[pl.ds(r0, rc), cs] = dgate.astype(BF16)
                dgv_ref[pl.ds(r0, rc), slice(dff + lg * lw, dff + (lg + 1) * lw)] = (dh_mid * gelu[0:rc]).astype(BF16)
                dgc_mid = dgc[0:rc]
                for sh in range(FFN_K):
                    dw_ref[pl.ds(FFN_K - 1 - sh, 1), cs] += _colsum(dgc_mid * shifted[sh][0:rc])
                db_ref[:, cs] += _colsum(dgc_mid)
            return carry

        lax.fori_loop(0, n_chunks, chunk, 0)

    return _call(
        body, name=name, grid=(nt,),
        in_specs=[_row(tm, dff), _next(tm, h, dff, s),
                  _prev(tm, h, dff, 0), _row(tm, dff, 0), _next(tm, h, dff, s, 0),
                  _row(tm, dff, 1), _next(tm, h, dff, s, 1),
                  _full((FFN_K, dff)), _full((1, dff))],
        out_specs=[_row(tm, 2 * dff), _full((FFN_K, dff)), _full((1, dff))],
        out_shape=[jax.ShapeDtypeStruct((s, 2 * dff), BF16), jax.ShapeDtypeStruct((FFN_K, dff), F32),
                   jax.ShapeDtypeStruct((1, dff), F32)],
        compiler_params=_cp(),
    )(dhid, dhid, gv, gv, gv, gv, gv, dw_w, dw_b)


def _bias_line(rel_bias):
    nh = rel_bias.shape[0]
    line = jnp.concatenate(
        [jnp.zeros((nh, 1), rel_bias.dtype), jnp.broadcast_to(rel_bias[:, 2 * MAX_REL:], (nh, SHEAR_SAT)),
         jnp.flip(rel_bias[:, 1:2 * MAX_REL], axis=1)], axis=1)
    return line[:, None, :]


def bias_tile(name, line):
    nh = line.shape[0]

    def body(l_ref, o_ref):
        x = jnp.broadcast_to(l_ref[...], (Q_TILE, SHEAR_W))
        z = pltpu.roll(x, SHEAR_W - Q_TILE, 1, stride=1, stride_axis=0)
        qc = lax.broadcasted_iota(jnp.int32, (Q_TILE, K_WIN), 0) // CHUNK
        kc = lax.broadcasted_iota(jnp.int32, (Q_TILE, K_WIN), 1) // CHUNK
        o_ref[...] = jnp.where((kc >= qc) & (kc <= qc + LEFT_CHUNKS), z[:, :K_WIN], NEG_INF)

    return _call(
        body, name=name, grid=(nh,), in_specs=[pl.BlockSpec((None, 1, SHEAR_W), lambda hh: (hh, 0, 0))],
        out_specs=pl.BlockSpec((None, Q_TILE, K_WIN), lambda hh: (hh, 0, 0)),
        out_shape=jax.ShapeDtypeStruct((nh, Q_TILE, K_WIN), F32), compiler_params=_cp(),
    )(line)


def _stack_heads(x2):
    lane = lax.broadcasted_iota(jnp.int32, x2.shape, 1)
    zero = jnp.zeros_like(x2)
    return jnp.concatenate([jnp.where(lane < HEAD_DIM, x2, zero), jnp.where(lane < HEAD_DIM, zero, x2)], axis=0)


def _unstack_heads(x_st):
    lane = lax.broadcasted_iota(jnp.int32, (Q_TILE, LANES), 1)
    return jnp.where(lane < HEAD_DIM, x_st[:Q_TILE], x_st[Q_TILE:])


def _attn_probs(q_st, k3, bias_st, t):
    sc = _dot(q_st, k3, "nt") * (HEAD_DIM ** -0.5) + bias_st
    col = lax.broadcasted_iota(jnp.int32, sc.shape, 1)
    sc = jnp.where(col >= PAD_ROWS - t * Q_TILE, sc, NEG_INF)
    m = jnp.max(sc, axis=-1, keepdims=True)
    p = jnp.exp(sc - m)
    return p * (1.0 / jnp.sum(p, axis=-1, keepdims=True))


def _attn_specs(d_model):
    nq = PAD_ROWS // Q_TILE
    groups = d_model // ATTN_LANES
    specs = [pl.BlockSpec((Q_TILE, ATTN_LANES), lambda g, t: (t + nq, g))]
    for which in (1, 2):
        for j in range(K_WIN // Q_TILE):
            specs.append(pl.BlockSpec((Q_TILE, ATTN_LANES), lambda g, t, j=j, which=which: (t + j, which * groups + g)))
    specs.append(pl.BlockSpec((2 * ATTN_PAIRS, Q_TILE, K_WIN), lambda g, t: (g, 0, 0)))
    return specs


def attn_fwd(name, qkvp, bias):
    s = qkvp.shape[0] - PAD_ROWS
    d_model = qkvp.shape[1] // 3
    nw = K_WIN // Q_TILE

    def body(q_ref, *refs):
        k_refs, v_refs, b_ref, o_ref = refs[:nw], refs[nw:2 * nw], refs[2 * nw], refs[2 * nw + 1]
        t = pl.program_id(1)
        for j in range(ATTN_PAIRS):
            ls = slice(j * LANES, (j + 1) * LANES)
            k3 = jnp.concatenate([r[:, ls] for r in k_refs], axis=0)
            v3 = jnp.concatenate([r[:, ls] for r in v_refs], axis=0)
            bias_st = b_ref[2 * j:2 * j + 2].reshape(2 * Q_TILE, K_WIN)
            p = _attn_probs(_stack_heads(q_ref[:, ls]), k3, bias_st, t)
            o_ref[:, ls] = _unstack_heads(_dot(p, v3, "nn")).astype(BF16)

    return _call(
        body, name=name, grid=(d_model // ATTN_LANES, s // Q_TILE),
        in_specs=_attn_specs(d_model), out_specs=pl.BlockSpec((Q_TILE, ATTN_LANES), lambda g, t: (t, g)),
        out_shape=jax.ShapeDtypeStruct((s, d_model), BF16), compiler_params=_cp(),
    )(qkvp, *([qkvp] * (2 * nw)), bias)


def attn_bwd(name, qkvp, bias, do):
    s = qkvp.shape[0] - PAD_ROWS
    d_model = qkvp.shape[1] // 3
    nw = K_WIN // Q_TILE
    nt = s // Q_TILE
    scale = HEAD_DIM ** -0.5

    def body(q_ref, *refs):
        k_refs, v_refs = refs[:nw], refs[nw:2 * nw]
        b_ref, do_ref, dq_ref, dk_ref, dv_ref, ds_ref, dk_acc, dv_acc = refs[2 * nw:]
        t = pl.program_id(1)
        first = t == 0

        @pl.when(first)
        def _():
            dk_acc[...] = jnp.zeros(dk_acc.shape, F32)
            dv_acc[...] = jnp.zeros(dv_acc.shape, F32)

        start = pl.multiple_of(t * Q_TILE, Q_TILE)
        for j in range(ATTN_PAIRS):
            ls = slice(j * LANES, (j + 1) * LANES)
            q_st = _stack_heads(q_ref[:, ls])
            do_st = _stack_heads(do_ref[:, ls])
            k3 = jnp.concatenate([r[:, ls] for r in k_refs], axis=0)
            v3 = jnp.concatenate([r[:, ls] for r in v_refs], axis=0)
            p = _attn_probs(q_st, k3, b_ref[2 * j:2 * j + 2].reshape(2 * Q_TILE, K_WIN), t)
            dp = _dot(do_st, v3, "nt")
            ds = p * (dp - jnp.sum(p * dp, axis=-1, keepdims=True))
            _acc_add(ds_ref.at[2 * j:2 * j + 2], first, ds.reshape(2, Q_TILE, K_WIN))
            dsb = (ds * scale).astype(BF16)
            dq_ref[:, ls] = _unstack_heads(_dot(dsb, k3, "nn")).astype(BF16)
            dk_acc[pl.ds(start, K_WIN), ls] += _dot(dsb, q_st, "tn")
            dv_acc[pl.ds(start, K_WIN), ls] += _dot(p, do_st, "tn")

        @pl.when(t == nt - 1)
        def _():
            dk_ref[...] = dk_acc[pl.ds(PAD_ROWS, s), :].astype(BF16)
            dv_ref[...] = dv_acc[pl.ds(PAD_ROWS, s), :].astype(BF16)

    specs = _attn_specs(d_model) + [pl.BlockSpec((Q_TILE, ATTN_LANES), lambda g, t: (t, g))]
    col_spec = pl.BlockSpec((s, ATTN_LANES), lambda g, t: (0, g))
    return _call(
        body, name=name, grid=(d_model // ATTN_LANES, nt), in_specs=specs,
        out_specs=[pl.BlockSpec((Q_TILE, ATTN_LANES), lambda g, t: (t, g)), col_spec, col_spec,
                   pl.BlockSpec((2 * ATTN_PAIRS, Q_TILE, K_WIN), lambda g, t: (g, 0, 0))],
        out_shape=[jax.ShapeDtypeStruct((s, d_model), BF16)] * 3
        + [jax.ShapeDtypeStruct((N_HEADS, Q_TILE, K_WIN), F32)],
        scratch_shapes=[pltpu.VMEM((PAD_ROWS + s, ATTN_LANES), F32), pltpu.VMEM((PAD_ROWS + s, ATTN_LANES), F32)],
        compiler_params=_cp(),
    )(qkvp, *([qkvp] * (2 * nw)), bias, do)


def bias_grad_reduce(name, ds_sum):
    nh = ds_sum.shape[0]
    width = SHEAR_W + Q_TILE
    first_k = Q_TILE - 1

    def body(x_ref, col_ref, sat_ref):
        x = x_ref[...]
        hi = x.astype(BF16)
        lo = (x - hi.astype(F32)).astype(BF16)
        r = lax.broadcasted_iota(jnp.int32, (Q_TILE, Q_TILE), 0)
        c = lax.broadcasted_iota(jnp.int32, (Q_TILE, Q_TILE), 1)
        exchange = jnp.where(r + c == Q_TILE - 1, 1.0, 0.0).astype(BF16)
        x_rev = _dot(exchange, hi, "nn") + _dot(exchange, lo, "nn")
        zeros = jnp.zeros((Q_TILE, Q_TILE), F32)
        y = pltpu.roll(jnp.concatenate([zeros, x_rev, zeros], axis=1), 0, 1, stride=1, stride_axis=0)
        cols = _colsum(y)
        col_ref[...] = cols
        k = lax.broadcasted_iota(jnp.int32, cols.shape, 1) - first_k
        tot = jnp.sum(jnp.where((k >= 1) & (k <= SHEAR_SAT), cols, 0.0), axis=-1, keepdims=True)
        sat_ref[...] = jnp.broadcast_to(tot, sat_ref.shape)

    return _call(
        body, name=name, grid=(nh,),
        in_specs=[pl.BlockSpec((None, Q_TILE, K_WIN), lambda hh: (hh, 0, 0))],
        out_specs=[pl.BlockSpec((None, 1, width), lambda hh: (hh, 0, 0)),
                   pl.BlockSpec((None, 1, LANES), lambda hh: (hh, 0, 0))],
        out_shape=[jax.ShapeDtypeStruct((nh, 1, width), F32), jax.ShapeDtypeStruct((nh, 1, LANES), F32)],
        compiler_params=_cp(),
    )(ds_sum)


def _ew_rows(r, most=512, cols=None):
    if cols is not None and r * cols * 4 <= SMALL_BLOCK_BYTES:
        return r
    for cand in (512, 256, 128, 64, 32, 16, 8):
        if cand <= most and r % cand == 0:
            return cand
    return r


def cast_into_gathered(name, w, layer, s_idx, n_blocks=N_SHARD, dtype=BF16, token=None):
    r, c = w.shape[-2:]
    tr = _ew_rows(r, cols=c)

    def body(s_ref, w_ref, *rest):
        rest[-1][...] = w_ref[...].astype(dtype)

    extra = [] if token is None else [token]
    grid_spec = pltpu.PrefetchScalarGridSpec(
        num_scalar_prefetch=1, grid=(r // tr,),
        in_specs=[pl.BlockSpec((None, tr, c), lambda i, s_ref: (layer, i, 0))] + [ANY_SPEC] * len(extra),
        out_specs=pl.BlockSpec((None, tr, c), lambda i, s_ref: (s_ref[0], i, 0)))
    return _call(
        body, name=name, grid_spec=grid_spec, out_shape=jax.ShapeDtypeStruct((n_blocks, r, c), dtype),
        compiler_params=_cp(),
    )(s_idx, w, *extra)


def adamw(name, w, grads, m, v, token=None):
    nl, r, c = w.shape
    tr = _ew_rows(r, 256, cols=c)

    def body(*refs):
        w_ref, m_ref, v_ref = refs[0], refs[1], refs[2]
        g_refs = refs[3:3 + nl]
        d_ref, nm_ref, nv_ref = refs[-3:]
        layer = pl.program_id(0)
        g = g_refs[0][...]
        for j in range(1, nl):
            g = jnp.where(layer == j, g_refs[j][...], g)
        nm = ADAM_B1 * m_ref[...] + (1.0 - ADAM_B1) * g
        nv = ADAM_B2 * v_ref[...] + (1.0 - ADAM_B2) * (g * g)
        m_hat = nm / ADAM_BC1
        v_hat = nv / ADAM_BC2
        d_ref[...] = -ADAM_LR * (m_hat / (jnp.sqrt(v_hat) + ADAM_EPS) + ADAM_WD * w_ref[...])
        nm_ref[...] = nm
        nv_ref[...] = nv

    p_spec = pl.BlockSpec((None, tr, c), lambda l, i: (l, i, 0))
    g_spec = pl.BlockSpec((tr, c), lambda l, i: (i, 0))
    extra = [] if token is None else [token]
    extra_specs = [] if token is None else [ANY_SPEC]
    return _call(
        body, name=name, grid=(nl, r // tr), in_specs=[p_spec] * 3 + [g_spec] * nl + extra_specs,
        out_specs=[p_spec] * 3, out_shape=[jax.ShapeDtypeStruct((nl, r, c), F32)] * 3, compiler_params=_cp(),
    )(w, m, v, *grads, *extra)


def sum_blocks(name, gathered, n_blocks):
    r = gathered.shape[0] // n_blocks
    c = gathered.shape[1]
    tr = r if r <= SUM_BLOCK_ROWS else _ew_rows(r)
    nt = r // tr

    def body(*refs):
        acc = refs[0][...]
        for j in range(1, n_blocks):
            acc = acc + refs[j][...]
        refs[-1][...] = acc

    specs = [pl.BlockSpec((tr, c), lambda i, j=j: (j * nt + i, 0)) for j in range(n_blocks)]
    return _call(
        body, name=name, grid=(nt,), in_specs=specs, out_specs=pl.BlockSpec((tr, c), lambda i: (i, 0)),
        out_shape=jax.ShapeDtypeStruct((r, c), F32), compiler_params=_cp(),
    )(*([gathered] * n_blocks))


def _place():
    return lax.axis_index("x"), lax.axis_index("y"), lax.axis_index("c")


def _other_chips(x, y):
    return [(1 - x, y), (x, 1 - y), (1 - x, 1 - y)]


HBM_SPEC = pl.BlockSpec(memory_space=pltpu.HBM)
SEM_SPEC = pl.BlockSpec(memory_space=pltpu.SEMAPHORE)
ANY_SPEC = pl.BlockSpec(memory_space=pl.ANY)
EFFECT = pltpu.SideEffectType.DATAFLOW_SIDE_EFFECTING


def copies_start(name, bufs, plan, n_copies):
    n = len(bufs)

    def body(*refs):
        send, recv = refs[n], refs[n + 1]
        token = refs[2 * n + 2]
        for k, (src, dst, peer, _) in enumerate(plan(refs[:n])):
            pltpu.make_async_remote_copy(
                src_ref=src, dst_ref=dst, send_sem=send.at[k], recv_sem=recv.at[k],
                device_id=peer, device_id_type=MESH).start()
        token[...] = jnp.zeros(token.shape, F32)

    outs = pl.pallas_call(
        body, name=name,
        out_shape=(pltpu.SemaphoreType.DMA((n_copies,)), pltpu.SemaphoreType.DMA((n_copies,)),
                   *[pltpu.HBM(b.shape, b.dtype) for b in bufs], jax.ShapeDtypeStruct((8, LANES), F32)),
        in_specs=[HBM_SPEC] * n,
        out_specs=(SEM_SPEC, SEM_SPEC, *([HBM_SPEC] * n), pl.BlockSpec(memory_space=pltpu.VMEM)),
        input_output_aliases={a: a + 2 for a in range(n)},
        compiler_params=pltpu.CompilerParams(has_side_effects=EFFECT),
    )(*[_in_hbm(b) for b in bufs])
    return outs[0], outs[1], list(outs[2:2 + n]), outs[2 + n]


def copies_wait(name, bufs, send, recv, plan, sem_base, after):
    n = len(bufs)

    def body(*refs):
        send_ref, recv_ref = refs[n], refs[n + 1]
        for k, (src, _, peer, land) in enumerate(plan(refs[:n])):
            cp = pltpu.make_async_remote_copy(
                src_ref=src, dst_ref=land, send_sem=send_ref.at[sem_base + k], recv_sem=recv_ref.at[sem_base + k],
                device_id=peer, device_id_type=MESH)
            cp.wait_send()
            cp.wait_recv()

    outs = pl.pallas_call(
        body, name=name,
        out_shape=tuple(pltpu.HBM(b.shape, b.dtype) for b in bufs),
        in_specs=[HBM_SPEC] * n + [SEM_SPEC, SEM_SPEC, ANY_SPEC], out_specs=tuple([HBM_SPEC] * n),
        input_output_aliases={a: a for a in range(n)},
        compiler_params=pltpu.CompilerParams(has_side_effects=EFFECT),
    )(*bufs, send, recv, after)
    return list(outs)


def gather_plan(refs):
    x, y, c = _place()
    me = 2 * x + y
    return [(buf.at[me], buf.at[me], (cx, cy, c), buf.at[2 * cx + cy])
            for buf in refs for cx, cy in _other_chips(x, y)]


def all_plan(refs):
    x, y, c = _place()
    me = 4 * x + 2 * y + c
    out = []
    for buf in refs:
        for flip in range(1, 8):
            px = 1 - x if flip & 4 else x
            py = 1 - y if flip & 2 else y
            pc = 1 - c if flip & 1 else c
            out.append((buf.at[me], buf.at[me], (px, py, pc), buf.at[4 * px + 2 * py + pc]))
    return out


def swap_plan(refs):
    x, y, c = _place()
    n = len(refs) // 2
    out = []
    for g, land in zip(refs[:n], refs[n:]):
        hr = g.shape[1] // 2
        out.append((g.at[:, pl.ds((1 - c) * hr, hr)], land, (x, y, 1 - c), land))
    return out


def owners_plan(refs):
    x, y, c = _place()
    n = len(refs) // 2
    return [(src.at[2 * cx + cy], land.at[j], (cx, cy, c), land.at[j])
            for src, land in zip(refs[:n], refs[n:]) for j, (cx, cy) in enumerate(_other_chips(x, y))]


def join_plan(refs):
    x, y, c = _place()
    out = []
    for buf in refs:
        hr = buf.shape[0] // 2
        mine = buf.at[pl.ds(c * hr, hr)]
        out.append((mine, mine, (x, y, 1 - c), buf.at[pl.ds((1 - c) * hr, hr)]))
    return out


def add_halves(name, grad, landed, sc_idx):
    _, r, c = grad.shape
    hr = r // 2
    tr = _ew_rows(hr)
    nt = hr // tr

    def body(sc_ref, g_ref, l_ref, own_ref, wire_ref):
        tot = g_ref[...] + l_ref[...]
        wire_ref[...] = tot.astype(BF16)

        @pl.when(pl.program_id(1) == sc_ref[0])
        def _():
            own_ref[...] = tot

    grid_spec = pltpu.PrefetchScalarGridSpec(
        num_scalar_prefetch=1, grid=(nt, N_SHARD),
        in_specs=[pl.BlockSpec((None, tr, c), lambda i, sh, sc_ref: (sh, sc_ref[1] * nt + i, 0)),
                  pl.BlockSpec((None, tr, c), lambda i, sh, sc_ref: (sh, i, 0))],
        out_specs=[pl.BlockSpec((tr, c), lambda i, sh, sc_ref: (i, 0)),
                   pl.BlockSpec((None, tr, c), lambda i, sh, sc_ref: (sh, i, 0))])
    return _call(
        body, name=name, grid_spec=grid_spec,
        out_shape=[jax.ShapeDtypeStruct((hr, c), F32), jax.ShapeDtypeStruct((N_SHARD, hr, c), BF16)],
        compiler_params=_cp(),
    )(sc_idx, grad, landed)


def add_owned(name, own, landed, sc_idx):
    hr, c = own.shape
    tr = _ew_rows(hr)
    nt = hr // tr

    def body(sc_ref, o_ref, l0, l1, l2, out_ref):
        out_ref[...] = ((o_ref[...] + l0[...].astype(F32)) + l1[...].astype(F32)) + l2[...].astype(F32)

    grid_spec = pltpu.PrefetchScalarGridSpec(
        num_scalar_prefetch=1, grid=(nt,),
        in_specs=[pl.BlockSpec((tr, c), lambda i, sc_ref: (i, 0))]
        + [pl.BlockSpec((None, tr, c), lambda i, sc_ref, j=j: (j, i, 0)) for j in range(3)],
        out_specs=pl.BlockSpec((tr, c), lambda i, sc_ref: (sc_ref[1] * nt + i, 0)))
    return _call(
        body, name=name, grid_spec=grid_spec, out_shape=jax.ShapeDtypeStruct((2 * hr, c), F32),
        compiler_params=_cp(),
    )(sc_idx, own, landed, landed, landed)


PACK_QUANTUM = 8 * LANES


def _pack(arrays):
    pieces = []
    for a in arrays:
        flat = a.reshape(-1)
        padded = -(-flat.shape[0] // PACK_QUANTUM) * PACK_QUANTUM
        pieces.append(jnp.pad(flat, (0, padded - flat.shape[0])).reshape(-1, LANES))
    return jnp.concatenate(pieces, axis=0)


def _unpack(packed, shapes):
    out = []
    row = 0
    for shp in shapes:
        size = math.prod(shp)
        rows = -(-size // PACK_QUANTUM) * 8
        out.append(packed[row:row + rows].reshape(-1)[:size].reshape(shp))
        row += rows
    return out


def kernel(x, p, mix_w_in, pool_w, pool_scale, conv_dw_w, conv_dw_b, conv_ln_g, conv_ln_b, mix_w_out, attn_w_qkv, attn_rel_bias, attn_w_o, ln_mix_g, ln_mix_b, ffn_w_up, ffn_dw_w, ffn_dw_b, ffn_w_down, ple_w_proj, ple_w_gate, ple_b_gate, ln_ffn_g, ln_ffn_b, loss_target, m_mix_w_in, m_pool_w, m_pool_scale, m_conv_dw_w, m_conv_dw_b, m_conv_ln_g, m_conv_ln_b, m_mix_w_out, m_attn_w_qkv, m_attn_rel_bias, m_attn_w_o, m_ln_mix_g, m_ln_mix_b, m_ffn_w_up, m_ffn_dw_w, m_ffn_dw_b, m_ffn_w_down, m_ple_w_proj, m_ple_w_gate, m_ple_b_gate, m_ln_ffn_g, m_ln_ffn_b, v_mix_w_in, v_pool_w, v_pool_scale, v_conv_dw_w, v_conv_dw_b, v_conv_ln_g, v_conv_ln_b, v_mix_w_out, v_attn_w_qkv, v_attn_rel_bias, v_attn_w_o, v_ln_mix_g, v_ln_mix_b, v_ffn_w_up, v_ffn_dw_w, v_ffn_dw_b, v_ffn_w_down, v_ple_w_proj, v_ple_w_gate, v_ple_b_gate, v_ln_ffn_g, v_ln_ffn_b):
    xi, yi, ci = _place()
    shard_idx = (2 * xi + yi).astype(jnp.int32)
    s_arr = shard_idx.reshape(1)
    c_arr = ci.astype(jnp.int32).reshape(1)
    sc_arr = jnp.concatenate([s_arr, c_arr])

    x0 = x[0]
    target = loss_target[0]
    p_rows = p.reshape(p.shape[0] * p.shape[2], p.shape[3])
    seq = x0.shape[0]

    big = [
        ("mix_w_in", mix_w_in, m_mix_w_in, v_mix_w_in, True),
        ("mix_w_out", mix_w_out, m_mix_w_out, v_mix_w_out, False),
        ("attn_w_qkv", attn_w_qkv, m_attn_w_qkv, v_attn_w_qkv, True),
        ("attn_w_o", attn_w_o, m_attn_w_o, v_attn_w_o, False),
        ("ffn_w_up", ffn_w_up, m_ffn_w_up, v_ffn_w_up, True),
        ("ffn_w_down", ffn_w_down, m_ffn_w_down, v_ffn_w_down, False),
        ("ple_w_proj", ple_w_proj, m_ple_w_proj, v_ple_w_proj, True),
        ("ple_w_gate", ple_w_gate, m_ple_w_gate, v_ple_w_gate, False),
    ]
    params = {nm: w for nm, w, _, _, _ in big}
    col_sharded = {nm: cs for nm, _, _, _, cs in big}
    keys = [("mix_w_in", 0), ("mix_w_out", 0), ("ffn_w_up", 0), ("ffn_w_down", 0), ("ple_w_gate", 0),
            ("ple_w_proj", 0), ("attn_w_qkv", 0), ("attn_w_o", 0), ("ffn_w_up", 1), ("ffn_w_down", 1),
            ("ple_w_gate", 1), ("ple_w_proj", 1)]
    dw_shapes = [conv_dw_w.shape, ffn_dw_w.shape]
    dw_block = cast_into_gathered("place_dw", _pack([conv_dw_w, ffn_dw_w])[None], 0, s_arr, dtype=F32)
    n_first = 2
    started = {}
    gather_token = None
    for tag, group in (("first", keys[:n_first]), ("rest", keys[n_first:])):
        shards = [cast_into_gathered(f"cast_{nm}_{layer}", params[nm], layer, s_arr, token=gather_token)
                  for nm, layer in group]
        if tag == "first":
            shards.append(dw_block)
        send, recv, bufs, gather_token = copies_start(f"gather_start_{tag}", shards, gather_plan, 3 * len(shards))
        for a, key in enumerate(group):
            started[key] = (send, recv, bufs[a], 3 * a)
        if tag == "first":
            dw_started = (send, recv, bufs[-1], 3 * len(group))
    arrived_w = {}

    def weight(nm, layer, after=None):
        key = (nm, layer)
        if key not in arrived_w:
            send, recv, buf, base = started[key]
            arrived_w[key] = copies_wait(f"gather_wait_{nm}_{layer}", [buf], send, recv, gather_plan, base, after)[0]
        g = arrived_w[key]
        if col_sharded[nm]:
            return g
        return g.reshape(g.shape[0] * g.shape[1], g.shape[2])

    def tie(a, token):
        return a + token[0:1, 0:1].astype(a.dtype)

    class Reducer:
        def __init__(self, tag, group):
            self.tag, self.group, self.stage = tag, group, 0
            self.n = len(group)
            self.result = None

        def advance(self, after):
            tag, n = self.tag, self.n
            if self.stage == 0:
                grads = []
                for key in self.group:
                    g = big_grads[key]
                    grads.append(g if g.ndim == 3 else g.reshape(N_SHARD, g.shape[0] // N_SHARD, g.shape[1]))
                lands = [lax.empty((N_SHARD, g.shape[1] // 2, g.shape[2]), F32) for g in grads]
                self.sems = copies_start(f"swap_start_{tag}", grads + lands, swap_plan, n)
            elif self.stage == 1:
                send, recv, bufs, _ = self.sems
                outs = copies_wait(f"swap_wait_{tag}", bufs, send, recv, swap_plan, 0, after)
                self.own, wire = [], []
                for key, g, ld in zip(self.group, outs[:n], outs[n:]):
                    o, ob = add_halves(f"add_halves_{key[0]}_{key[1]}", g, ld, sc_arr)
                    self.own.append(o)
                    wire.append(ob)
                lands = [lax.empty((3,) + w.shape[1:], BF16) for w in wire]
                self.sems = copies_start(f"owners_start_{tag}", wire + lands, owners_plan, 3 * n)
            elif self.stage == 2:
                send, recv, bufs, _ = self.sems
                outs = copies_wait(f"owners_wait_{tag}", bufs, send, recv, owners_plan, 0, after)
                finals = [add_owned(f"add_owned_{key[0]}_{key[1]}", o, ar, sc_arr)
                          for key, o, ar in zip(self.group, self.own, outs[n:])]
                self.sems = copies_start(f"join_start_{tag}", finals, join_plan, n)
            elif self.stage == 3:
                send, recv, bufs, _ = self.sems
                outs = copies_wait(f"join_wait_{tag}", bufs, send, recv, join_plan, 0, after)
                self.result = dict(zip(self.group, outs))
                self.sems = None
            self.stage += 1
            return None if self.sems is None else self.sems[3]

    dw_cache = []

    def conv_weights(after):
        if not dw_cache:
            send, recv, buf, base = dw_started
            dw_all = copies_wait("gather_wait_dw", [buf], send, recv, gather_plan, base, after)[0]
            dw_parts = [_unpack(dw_all[k], dw_shapes) for k in range(N_SHARD)]
            dw_cache.append(jnp.concatenate([pc[0] for pc in dw_parts], axis=2)[0])
            dw_cache.append(jnp.concatenate([pc[1] for pc in dw_parts], axis=2))
        return dw_cache

    big_grads = {}
    small_grads = {}

    saved = []
    h_in = x0
    for layer in range(N_LAYERS):
        sv = {"x_in": h_in}
        if layer % 2 == 0:
            u = mm_cols_fwd("mix_in", h_in, weight("mix_w_in", 0, gather_token), F32)
            conv_w_full, ffn_dw_full = conv_weights(u)
            cat, d_sv, e_sv, glu_sv, hh_sv, rs_sv = mixer_fwd(
                "mixer_fwd", u, pool_w[0], pool_scale, conv_w_full, conv_dw_b, conv_ln_g, conv_ln_b)
            mix = mm_rows_fwd("mix_out", cat, weight("mix_w_out", 0, cat))
            sv.update(u=u, cat=cat, d=d_sv, e=e_sv, glu=glu_sv, hh=hh_sv, rs=rs_sv)
        else:
            qkvp = mm_cols_fwd("attn_qkv", h_in, weight("attn_w_qkv", 0, h_in), BF16,
                               pad_blocks=PAD_ROWS // _row_tile(seq))
            bias = bias_tile("bias_tile", _bias_line(attn_rel_bias[0]))
            att = attn_fwd("attn_fwd", qkvp, bias)
            mix = mm_rows_fwd("attn_out", att, weight("attn_w_o", 0, att))
            sv.update(qkvp=qkvp, bias=bias, att=att)
        x1, xh1, rs1 = ln_fwd(f"ln_mix_{layer}", h_in, mix, ln_mix_g[layer:layer + 1], ln_mix_b[layer:layer + 1])
        gv = mm_cols_fwd(f"ffn_up_{layer}", x1, weight("ffn_w_up", layer, x1), F32)
        hid = ffn_act_fwd(f"ffn_act_{layer}", gv, ffn_dw_full[layer], ffn_dw_b[layer:layer + 1])
        ffn = mm_rows_fwd(f"ffn_down_{layer}", hid, weight("ffn_w_down", layer, hid))
        pgl = mm_rows_fwd(f"ple_gate_{layer}", x1, weight("ple_w_gate", layer, ffn))
        pp = mm_cols_fwd(f"ple_proj_{layer}", p_rows, weight("ple_w_proj", layer, pgl), F32, part=(layer, N_LAYERS))
        bg = ple_b_gate[layer:layer + 1]
        x2, xh2, rs2 = ln_fwd(f"ln_ffn_{layer}", x1, ffn, ln_ffn_g[layer:layer + 1], ln_ffn_b[layer:layer + 1],
                              ple=(pgl, pp, bg), emit_y=layer < N_LAYERS - 1)
        sv.update(x1=x1, xh1=xh1, rs1=rs1, gv=gv, hid=hid, pgl=pgl, pp=pp, xh2=xh2, rs2=rs2)
        saved.append(sv)
        h_in = x2

    reducers = []

    def open_group(tag, group):
        reducers.append(Reducer(tag, group))
        return reducers[-1].advance(None)

    def hook(after):
        token = None
        for red in reducers:
            if red.stage < 4:
                tk = red.advance(after)
                if tk is not None:
                    token = tk if token is None else token + tk
        return token

    def tied(a, token):
        return a if token is None else tie(a, token)

    parts = []
    token = None
    for layer in reversed(range(N_LAYERS)):
        sv = saved[layer]
        bg = ple_b_gate[layer:layer + 1]
        if layer == 0:
            token = open_group("layer1", [("attn_w_qkv", 0), ("attn_w_o", 0), ("ffn_w_up", 1), ("ffn_w_down", 1),
                                          ("ple_w_gate", 1), ("ple_w_proj", 1)])
        last = layer == N_LAYERS - 1
        res = ln_bwd(
            f"ln_ffn_bwd_{layer}", parts, sv["xh2"], sv["rs2"], tied(ln_ffn_g[layer:layer + 1], token),
            ple=(sv["pgl"], sv["pp"], bg), loss=(target, ln_ffn_b[layer:layer + 1]) if last else None)
        dz2, dg2, db2, dpp, dpgl, dbg = res[:6]
        if last:
            loss_part = res[6]
        small_grads[("ln_ffn_g", layer)] = dg2
        small_grads[("ln_ffn_b", layer)] = db2
        small_grads[("ple_b_gate", layer)] = dbg
        w_down = weight("ffn_w_down", layer)
        dhid = mm_rows_dx(f"ffn_down_dx_{layer}", dz2, w_down)
        big_grads[("ffn_w_down", layer)] = mm_rows_dw(f"ffn_down_dw_{layer}", sv["hid"], dz2)
        token = hook(big_grads[("ffn_w_down", layer)])
        dgv, ddw, ddb = ffn_act_bwd(f"ffn_act_bwd_{layer}", dhid, sv["gv"], ffn_dw_full[layer],
                                    tied(ffn_dw_b[layer:layer + 1], token))
        small_grads[("ffn_dw_w", layer)] = ddw
        small_grads[("ffn_dw_b", layer)] = ddb
        big_grads[("ffn_w_up", layer)] = mm_cols_dw(f"ffn_up_dw_{layer}", sv["x1"], dgv)
        t_up = mm_cols_dx(f"ffn_up_dx_{layer}", dgv, weight("ffn_w_up", layer))
        token = hook(t_up)
        big_grads[("ple_w_gate", layer)] = mm_rows_dw(f"ple_gate_dw_{layer}", sv["x1"], dpgl)
        t_gate = mm_rows_dx(f"ple_gate_dx_{layer}", dpgl, weight("ple_w_gate", layer))
        big_grads[("ple_w_proj", layer)] = mm_cols_dw(f"ple_proj_dw_{layer}", p_rows, dpp, part=(layer, N_LAYERS))
        token2 = hook(big_grads[("ple_w_proj", layer)])
        if token2 is not None:
            token = token2 if token is None else token + token2
        if layer == 0:
            token3 = open_group("layer0_ffn", [("ffn_w_up", 0), ("ffn_w_down", 0), ("ple_w_gate", 0), ("ple_w_proj", 0)])
            token = token3 if token is None else token + token3
        dz1, dg1, db1 = ln_bwd(
            f"ln_mix_bwd_{layer}", [(ALPHA, dz2), (1.0, t_up), (1.0, t_gate)], sv["xh1"], sv["rs1"],
            tied(ln_mix_g[layer:layer + 1], token))
        small_grads[("ln_mix_g", layer)] = dg1
        small_grads[("ln_mix_b", layer)] = db1
        if layer % 2 == 0:
            dcat = mm_rows_dx("mix_out_dx", dz1, weight("mix_w_out", 0))
            big_grads[("mix_w_out", 0)] = mm_rows_dw("mix_out_dw", sv["cat"], dz1)
            token = hook(big_grads[("mix_w_out", 0)])
            du, dpw, dps, dcw, dcb, dcg, dcbt = mixer_bwd(
                "mixer_bwd", dcat, sv["u"], sv["d"], sv["e"], sv["glu"], sv["hh"], sv["rs"],
                pool_w[0], pool_scale, conv_w_full, tied(conv_ln_g, token), conv_ln_b)
            small_grads[("pool_w", 0)] = dpw
            small_grads[("pool_scale", 0)] = dps
            small_grads[("conv_dw_w", 0)] = dcw
            small_grads[("conv_dw_b", 0)] = dcb
            small_grads[("conv_ln_g", 0)] = dcg
            small_grads[("conv_ln_b", 0)] = dcbt
            big_grads[("mix_w_in", 0)] = mm_cols_dw("mix_in_dw", sv["x_in"], du)
            hook(big_grads[("mix_w_in", 0)])
            open_group("layer0_mix", [("mix_w_in", 0), ("mix_w_out", 0)])
            dx_in = mm_cols_dx("mix_in_dx", du, weight("mix_w_in", 0), addend=(ALPHA, dz1))
            token = hook(dx_in)
        else:
            do = mm_rows_dx("attn_out_dx", dz1, weight("attn_w_o", 0), out_dtype=BF16)
            big_grads[("attn_w_o", 0)] = mm_rows_dw("attn_out_dw", sv["att"], dz1)
            dq, dk, dv, ds_sum = attn_bwd("attn_bwd", sv["qkvp"], sv["bias"], do)
            cols, sat = bias_grad_reduce("bias_grad", ds_sum)
            d_rel = jnp.concatenate(
                [jnp.zeros((N_HEADS, 1), F32),
                 jnp.flip(cols[:, 0, Q_TILE + SHEAR_SAT:Q_TILE - 1 + SHEAR_W], axis=1),
                 sat[:, 0, 0:1]], axis=1)
            small_grads[("attn_rel_bias", 0)] = d_rel
            dqkv = jnp.concatenate([dq, dk, dv], axis=1)
            big_grads[("attn_w_qkv", 0)] = mm_cols_dw("attn_qkv_dw", sv["x_in"], dqkv)
            dx_in = mm_cols_dx("attn_qkv_dx", dqkv, weight("attn_w_qkv", 0), addend=(ALPHA, dz1))
        parts = [(1.0, dx_in)]
    grad_x = dx_in

    small = [
        ("pool_w", pool_w, m_pool_w, v_pool_w, None),
        ("pool_scale", pool_scale, m_pool_scale, v_pool_scale, None),
        ("conv_dw_w", conv_dw_w, m_conv_dw_w, v_conv_dw_w, 2),
        ("conv_dw_b", conv_dw_b, m_conv_dw_b, v_conv_dw_b, None),
        ("conv_ln_g", conv_ln_g, m_conv_ln_g, v_conv_ln_g, None),
        ("conv_ln_b", conv_ln_b, m_conv_ln_b, v_conv_ln_b, None),
        ("attn_rel_bias", attn_rel_bias, m_attn_rel_bias, v_attn_rel_bias, None),
        ("ln_mix_g", ln_mix_g, m_ln_mix_g, v_ln_mix_g, None),
        ("ln_mix_b", ln_mix_b, m_ln_mix_b, v_ln_mix_b, None),
        ("ffn_dw_w", ffn_dw_w, m_ffn_dw_w, v_ffn_dw_w, 2),
        ("ffn_dw_b", ffn_dw_b, m_ffn_dw_b, v_ffn_dw_b, None),
        ("ple_b_gate", ple_b_gate, m_ple_b_gate, v_ple_b_gate, None),
        ("ln_ffn_g", ln_ffn_g, m_ln_ffn_g, v_ln_ffn_g, None),
        ("ln_ffn_b", ln_ffn_b, m_ln_ffn_b, v_ln_ffn_b, None),
    ]
    full_grads = []
    for nm, w, _, _, shard_axis in small:
        full = list(w.shape)
        if shard_axis is not None:
            full[shard_axis] *= N_SHARD
        per_layer = [small_grads[(nm, layer)].reshape((1,) + tuple(full[1:])) for layer in range(w.shape[0])]
        full_grads.append(jnp.concatenate(per_layer, axis=0))
    packed = _pack(full_grads + [loss_part])
    dev_arr = (4 * xi + 2 * yi + ci).astype(jnp.int32).reshape(1)
    sg_block = cast_into_gathered("place_small_grads", packed[None], 0, dev_arr, n_blocks=8, dtype=F32)
    sg_send, sg_recv, sg_bufs, sg_token = copies_start("small_grads_start", [sg_block], all_plan, 7)
    token = sg_token if token is None else token + sg_token

    shard_grads = {}
    for red in reducers:
        if red.stage == 4:
            shard_grads.update(red.result)
    big_out = {}

    def update_big(names, tok):
        for nm, w, m, v, _ in big:
            if nm in names:
                gl = [shard_grads[(nm, layer)] for layer in range(w.shape[0])]
                delta, new_m, new_v = adamw(f"adamw_{nm}", w, gl, m, v, token=tok)
                big_out[nm] = (jnp.stack(gl, axis=0), delta, new_m, new_v)

    last_group = ("mix_w_in", "mix_w_out")
    update_big([nm for nm, _, _, _, _ in big if nm not in last_group], token)
    token = hook(big_out["ffn_w_up"][1])

    gathered_sg = copies_wait("small_grads_wait", sg_bufs, sg_send, sg_recv, all_plan, 0, big_out["ffn_w_down"][1])[0]
    total = sum_blocks("sum_small", gathered_sg.reshape(8 * packed.shape[0], LANES), 8)
    unpacked = _unpack(total, [g.shape for g in full_grads] + [loss_part.shape])
    loss = unpacked[-1][0, 0]
    local_grads = []
    for (nm, w, _, _, shard_axis), g in zip(small, unpacked[:-1]):
        if shard_axis is not None:
            width = w.shape[shard_axis]
            g = lax.dynamic_slice_in_dim(g, shard_idx * width, width, axis=shard_axis)
        local_grads.append(g.reshape(w.shape))
    shapes = [w.shape for _, w, _, _, _ in small]
    pg = _pack(local_grads)
    pw = _pack([w for _, w, _, _, _ in small])
    pm = _pack([m for _, _, m, _, _ in small])
    pv = _pack([v for _, _, _, v, _ in small])
    delta_s, new_m_s, new_v_s = adamw("adamw_small", pw[None], [pg], pm[None], pv[None], token=token)
    hook(delta_s)
    for red in reducers:
        shard_grads.update(red.result)
    update_big(last_group, None)
    small_out = {}
    for (nm, _, _, _, _), g, d_, m_, v_ in zip(
            small, local_grads, _unpack(delta_s[0], shapes), _unpack(new_m_s[0], shapes), _unpack(new_v_s[0], shapes)):
        small_out[nm] = (g, d_, m_, v_)

    order = ["mix_w_in", "pool_w", "pool_scale", "conv_dw_w", "conv_dw_b", "conv_ln_g", "conv_ln_b", "mix_w_out",
             "attn_w_qkv", "attn_rel_bias", "attn_w_o", "ln_mix_g", "ln_mix_b", "ffn_w_up", "ffn_dw_w", "ffn_dw_b",
             "ffn_w_down", "ple_w_proj", "ple_w_gate", "ple_b_gate", "ln_ffn_g", "ln_ffn_b"]
    res = {**big_out, **small_out}
    outs = [loss, grad_x[None]]
    for slot in range(4):
        outs += [res[nm][slot] for nm in order]
    return tuple(outs)
```

```python
import functools
import math

import jax
import jax.numpy as jnp
from jax import lax
from jax.experimental import pallas as pl
from jax.experimental.pallas import tpu as pltpu

F32 = jnp.float32
BF16 = jnp.bfloat16
MESH = pl.DeviceIdType.MESH

N_LAYERS = 2
ALPHA = (2 * N_LAYERS) ** 0.25
LN_EPS = 1e-5
NEG_INF = -1e30
CHUNK = 64
LEFT_CHUNKS = 8
PAD_ROWS = LEFT_CHUNKS * CHUNK
HEAD_DIM = 64
N_HEADS = 16
MAX_REL = 256
POOL_WINDOWS = (2, 4, 8, 16)
POOL_GROUP = 128
CONV_K = 31
FFN_K = 3
CONV_HALO = 32
FFN_HALO = 8
FFN_TILE = 256
FFN_CHUNK_ROWS = 32
FFN_CHUNK_LANES = 256
Q_TILE = 256
K_WIN = Q_TILE + PAD_ROWS
LANES = 128
SUBLANES = 8
ATTN_PAIRS = 2
ATTN_LANES = ATTN_PAIRS * LANES
SHEAR_W = Q_TILE + K_WIN
SHEAR_SAT = SHEAR_W - 2 * MAX_REL
N_SHARD = 4

ADAM_LR = 0.001
ADAM_B1 = 0.9
ADAM_B2 = 0.999
ADAM_EPS = 1e-08
ADAM_WD = 0.01
ADAM_STEP = 10
ADAM_BC1 = 1.0 - ADAM_B1 ** ADAM_STEP
ADAM_BC2 = 1.0 - ADAM_B2 ** ADAM_STEP

DIMS = {
    "nn": (((1,), (0,)), ((), ())),
    "nt": (((1,), (1,)), ((), ())),
    "tn": (((0,), (0,)), ((), ())),
}


def _cp(vmem_mb=48, **kw):
    return pltpu.CompilerParams(vmem_limit_bytes=vmem_mb * 1024 * 1024, **kw)


def _in_hbm(a):
    return pltpu.with_memory_space_constraint(a, pltpu.HBM)


STAGING_LIMIT_BYTES = 1 << 20
SUM_BLOCK_ROWS = 2048
SMALL_BLOCK_BYTES = 1 << 19


def _call(body, **kw):
    call = pl.pallas_call(body, **kw)

    def run(*args):
        pinned = []
        for a in args:
            big = a.size * a.dtype.itemsize >= STAGING_LIMIT_BYTES
            pinned.append(_in_hbm(a) if big and not jnp.issubdtype(a.dtype, jnp.integer) else a)
        return call(*pinned)

    return run


def _dot(a, b, mode):
    return lax.dot_general(a.astype(BF16), b.astype(BF16), DIMS[mode], preferred_element_type=F32)


def _sig(x):
    return 1.0 / (1.0 + jnp.exp(-x))


def _row_tile(s):
    return min(512, s // 4)


def _mm_tile(s):
    return min(1024, s // 4)


def _mm(name, mode, a, b, in_specs, out_shape, out_spec, acc_shape, grid, nk, zero_first=False, vmem_mb=48,
        addend=None):
    out_f32 = out_shape.dtype == F32

    def body(a_ref, b_ref, *rest):
        k = pl.program_id(2)
        if addend is None:
            o_ref, scr = rest[0], rest[1:]
        else:
            add_ref, o_ref, scr = rest[0], rest[1], rest[2:]

        def compute():
            part = _dot(a_ref[...], b_ref[...], mode)
            if nk == 1:
                if addend is not None:
                    part = part + addend[0] * add_ref[...]
                o_ref[...] = part.astype(o_ref.dtype)
                return
            acc = o_ref if out_f32 else scr[0]

            @pl.when(k == 0)
            def _():
                acc[...] = part if addend is None else part + addend[0] * add_ref[...]

            @pl.when(k > 0)
            def _():
                acc[...] += part

            if not out_f32:
                @pl.when(k == nk - 1)
                def _():
                    o_ref[...] = acc[...].astype(o_ref.dtype)

        if zero_first:
            @pl.when(pl.program_id(1) == 0)
            def _():
                o_ref[...] = jnp.zeros(o_ref.shape, o_ref.dtype)

            pl.when(pl.program_id(1) > 0)(compute)
        else:
            compute()

    scratch = [] if (nk == 1 or out_f32) else [pltpu.VMEM(acc_shape, F32)]
    operands = [a, b] if addend is None else [a, b, addend[1]]
    specs = list(in_specs) if addend is None else list(in_specs) + [out_spec]
    return _call(
        body, name=name, grid=grid, in_specs=specs, out_specs=out_spec, out_shape=out_shape,
        scratch_shapes=scratch, compiler_params=_cp(vmem_mb),
    )(*operands)


def mm_cols_fwd(name, a, wc, out_dtype, pad_blocks=0, part=(0, 1)):
    s, k = a.shape
    s //= part[1]
    n4 = wc.shape[2]
    tm = _row_tile(s) if pad_blocks else _mm_tile(s)
    nt = s // tm
    first_block = part[0] * nt
    return _mm(
        name, "nn", a, wc,
        [pl.BlockSpec((tm, k), lambda j, i, r: (first_block + jnp.maximum(i - pad_blocks, 0), 0)),
         pl.BlockSpec((None, k, n4), lambda j, i, r: (j, 0, 0))],
        jax.ShapeDtypeStruct((s + pad_blocks * tm, N_SHARD * n4), out_dtype),
        pl.BlockSpec((tm, n4), lambda j, i, r: (i, j)),
        None, (N_SHARD, nt + pad_blocks, 1), 1, zero_first=pad_blocks > 0)


def mm_cols_dx(name, dy, wc, addend=None):
    s = dy.shape[0]
    _, k, n4 = wc.shape
    tm = _mm_tile(s)
    return _mm(
        name, "nt", dy, wc,
        [pl.BlockSpec((tm, n4), lambda g, i, r: (i, r)),
         pl.BlockSpec((None, k, n4), lambda g, i, r: (r, 0, 0))],
        jax.ShapeDtypeStruct((s, k), F32),
        pl.BlockSpec((tm, k), lambda g, i, r: (i, 0)),
        (tm, k), (1, s // tm, N_SHARD), N_SHARD, addend=addend)


def mm_cols_dw(name, a, dy, part=(0, 1)):
    s, k = a.shape
    s //= part[1]
    n4 = dy.shape[1] // N_SHARD
    tm = _mm_tile(s)
    nt = s // tm
    first_block = part[0] * nt
    return _mm(
        name, "tn", a, dy,
        [pl.BlockSpec((tm, k), lambda j, g, r: (first_block + r, 0)),
         pl.BlockSpec((tm, n4), lambda j, g, r: (r, j))],
        jax.ShapeDtypeStruct((N_SHARD, k, n4), F32),
        pl.BlockSpec((None, k, n4), lambda j, g, r: (j, 0, 0)),
        (k, n4), (N_SHARD, 1, nt), nt)


def _k_tile(k):
    return k if k <= 1024 else k // 2


def mm_rows_fwd(name, a, wr, out_dtype=F32):
    s, k = a.shape
    n = wr.shape[1]
    tm = _mm_tile(s)
    tk = _k_tile(k)
    nk = k // tk
    return _mm(
        name, "nn", a, wr,
        [pl.BlockSpec((tm, tk), lambda g, i, r: (i, r)),
         pl.BlockSpec((tk, n), lambda g, i, r: (r, 0))],
        jax.ShapeDtypeStruct((s, n), out_dtype),
        pl.BlockSpec((tm, n), lambda g, i, r: (i, 0)),
        (tm, n), (1, s // tm, nk), nk)


def mm_rows_dx(name, dy, wr, out_dtype=F32):
    s, n = dy.shape
    k = wr.shape[0]
    tm = _mm_tile(s)
    tk = _k_tile(k)
    return _mm(
        name, "nt", dy, wr,
        [pl.BlockSpec((tm, n), lambda j, i, r: (i, 0)),
         pl.BlockSpec((tk, n), lambda j, i, r: (j, 0))],
        jax.ShapeDtypeStruct((s, k), out_dtype),
        pl.BlockSpec((tm, tk), lambda j, i, r: (i, j)),
        None, (k // tk, s // tm, 1), 1)


def mm_rows_dw(name, a, dy):
    s, k = a.shape
    n = dy.shape[1]
    tm = _mm_tile(s)
    tk = _k_tile(k)
    nt = s // tm
    return _mm(
        name, "tn", a, dy,
        [pl.BlockSpec((tm, tk), lambda j, g, r: (r, j)),
         pl.BlockSpec((tm, n), lambda j, g, r: (r, 0))],
        jax.ShapeDtypeStruct((k, n), F32),
        pl.BlockSpec((tk, n), lambda j, g, r: (j, 0)),
        (tk, n), (k // tk, 1, nt), nt)


def _row(tm, c, col=0):
    return pl.BlockSpec((tm, c), lambda i: (i, col))


def _full(shape):
    nd = len(shape)
    return pl.BlockSpec(shape, lambda i: (0,) * nd)


def _prev(tm, h, c, col=0):
    return pl.BlockSpec((h, c), lambda i: (jnp.maximum(i * (tm // h) - 1, 0), col))


def _next(tm, h, c, s, col=0):
    return pl.BlockSpec((h, c), lambda i: (jnp.minimum((i + 1) * (tm // h), s // h - 1), col))


def _acc_add(ref, first, val):
    @pl.when(first)
    def _():
        ref[...] = val

    @pl.when(jnp.logical_not(first))
    def _():
        ref[...] += val


def _colsum(v):
    return jnp.sum(v, axis=0, keepdims=True)


def _ln_stats(z):
    mu = jnp.mean(z, axis=-1, keepdims=True)
    zc = z - mu
    var = jnp.mean(zc * zc, axis=-1, keepdims=True)
    rstd = lax.rsqrt(var + LN_EPS)
    return zc * rstd, rstd


def _ln_bwd(dxhat, xhat, rstd):
    m1 = jnp.mean(dxhat, axis=-1, keepdims=True)
    m2 = jnp.mean(dxhat * xhat, axis=-1, keepdims=True)
    return rstd * (dxhat - m1 - xhat * m2)


def ln_fwd(name, x, f, g, b, ple=None, emit_y=True):
    s, d = x.shape
    tm = _row_tile(s)
    n_in = 2 + (3 if ple is not None else 0)

    def body(*refs):
        x_ref, f_ref = refs[0], refs[1]
        g_ref, b_ref = refs[n_in], refs[n_in + 1]
        xh_ref, rs_ref = refs[-2:]
        z = ALPHA * x_ref[...] + f_ref[...]
        if ple is not None:
            pgl_ref, pp_ref, bg_ref = refs[2:5]
            z = z + _sig(pgl_ref[...] + bg_ref[...]) * pp_ref[...]
        xhat, rstd = _ln_stats(z)
        if emit_y:
            y = xhat * g_ref[...] + b_ref[...]
            refs[n_in + 2][...] = y
            refs[n_in + 3][...] = y.astype(BF16)
        xh_ref[...] = xhat
        rs_ref[...] = jnp.broadcast_to(rstd, rs_ref.shape)

    ins = [x, f]
    specs = [_row(tm, d), _row(tm, d)]
    if ple is not None:
        pgl, pp, bg = ple
        ins += [pgl, pp, bg]
        specs += [_row(tm, d), _row(tm, d), _full((1, d))]
    ins += [g, b]
    specs += [_full((1, d)), _full((1, d))]
    y_shapes = [jax.ShapeDtypeStruct((s, d), F32), jax.ShapeDtypeStruct((s, d), BF16)] if emit_y else []
    outs = _call(
        body, name=name, grid=(s // tm,), in_specs=specs,
        out_specs=[_row(tm, d)] * (len(y_shapes) + 1) + [_row(tm, LANES)],
        out_shape=y_shapes + [jax.ShapeDtypeStruct((s, d), F32), jax.ShapeDtypeStruct((s, LANES), F32)],
        compiler_params=_cp(),
    )(*ins)
    return tuple(outs) if emit_y else (None, None, outs[0], outs[1])


def ln_bwd(name, parts, xhat, rstd, g, ple=None, loss=None):
    s, d = xhat.shape
    tm = _row_tile(s)
    coefs = [c for c, _ in parts]
    n_p = len(parts)
    n_ple = 3 if ple is not None else 0
    n_in = n_p + 3 + n_ple + (2 if loss is not None else 0)

    def body(*refs):
        first = pl.program_id(0) == 0
        xh = refs[n_p][...]
        rs = refs[n_p + 1][:, 0:1]
        g_v = refs[n_p + 2][...]
        outs = refs[n_in:]
        if loss is not None:
            t_ref, b_ref = refs[n_p + 3 + n_ple:n_p + 5 + n_ple]
            err = (xh * g_v + b_ref[...]) - t_ref[...]
            dy = err * (1.0 / d)
            part = 0.5 * jnp.sum(jnp.mean(err * err, axis=-1, keepdims=True), axis=0, keepdims=True)
            _acc_add(outs[-1], first, jnp.broadcast_to(part, outs[-1].shape))
        else:
            dy = coefs[0] * refs[0][...].astype(F32)
            for j in range(1, n_p):
                dy = dy + coefs[j] * refs[j][...].astype(F32)
        dz = _ln_bwd(dy * g_v, xh, rs)
        outs[0][...] = dz
        _acc_add(outs[1], first, _colsum(dy * xh))
        _acc_add(outs[2], first, _colsum(dy))
        if ple is not None:
            pgl_ref, pp_ref, bg_ref = refs[n_p + 3:n_p + 6]
            pg = _sig(pgl_ref[...] + bg_ref[...])
            dpgl = dz * pp_ref[...] * pg * (1.0 - pg)
            outs[3][...] = (dz * pg).astype(BF16)
            outs[4][...] = dpgl.astype(BF16)
            _acc_add(outs[5], first, _colsum(dpgl))

    ins = [p for _, p in parts] + [xhat, rstd, g]
    specs = [_row(tm, d)] * n_p + [_row(tm, d), _row(tm, LANES), _full((1, d))]
    out_specs = [_row(tm, d), _full((1, d)), _full((1, d))]
    out_shape = [jax.ShapeDtypeStruct((s, d), F32), jax.ShapeDtypeStruct((1, d), F32),
                 jax.ShapeDtypeStruct((1, d), F32)]
    if ple is not None:
        pgl, pp, bg = ple
        ins += [pgl, pp, bg]
        specs += [_row(tm, d), _row(tm, d), _full((1, d))]
        out_specs += [_row(tm, d), _row(tm, d), _full((1, d))]
        out_shape += [jax.ShapeDtypeStruct((s, d), BF16), jax.ShapeDtypeStruct((s, d), BF16),
                      jax.ShapeDtypeStruct((1, d), F32)]
    if loss is not None:
        target, b = loss
        ins += [target, b]
        specs += [_row(tm, d), _full((1, d))]
        out_specs += [_full((8, LANES))]
        out_shape += [jax.ShapeDtypeStruct((8, LANES), F32)]
    return _call(
        body, name=name, grid=(s // tm,), in_specs=specs, out_specs=out_specs, out_shape=out_shape,
        compiler_params=_cp(),
    )(*ins)


def _fill_rotations(rot_ref, x, direction):
    n = x.shape[0]
    rot_ref[0] = x
    for b in range(1, SUBLANES):
        if direction < 0:
            rot_ref[b, SUBLANES:n, :] = x[SUBLANES - b:n - b]
        else:
            rot_ref[b, 0:n - SUBLANES, :] = x[b:n - SUBLANES + b]


def _rotated(rot_ref, start, rows, cs, direction=-1):
    b = (-start) % SUBLANES if direction < 0 else start % SUBLANES
    aligned = start + b if direction < 0 else start - b
    return rot_ref[b, pl.ds(aligned, rows), cs]


def _tile_pos(i, tm, rows):
    return (i * tm + lax.broadcasted_iota(jnp.int32, (rows, 1), 0) + 1).astype(F32)


def mixer_fwd(name, u, pool_w, pool_scale, conv_w, conv_b, cn_g, cn_b):
    s = u.shape[0]
    dp = 512
    tm = min(256, s // 4)
    h = CONV_HALO

    def body(a_c, a_p, bv_c, bv_p, bg_c, bg_p, pw_ref, ps_ref, cw_ref, cb_ref, cg_ref, cbt_ref,
             cat_ref, d_ref, e_ref, glu_ref, hh_ref, rs_ref, ext_a, rot_g, conv_out):
        i = pl.program_id(0)
        first = i == 0
        ext_a[0:h, :] = jnp.where(first, 0.0, a_p[...])
        ext_a[h:, :] = a_c[...]
        glu = bv_c[...] * _sig(bg_c[...])
        glu_ref[...] = glu
        _fill_rotations(rot_g, jnp.concatenate([jnp.where(first, 0.0, bv_p[...] * _sig(bg_p[...])), glu], axis=0), -1)
        pos = _tile_pos(i, tm, tm)
        for gi, w in enumerate(POOL_WINDOWS):
            cs = slice(gi * POOL_GROUP, (gi + 1) * POOL_GROUP)
            a_g = ext_a[pl.ds(h, tm), cs]
            acc = a_g
            for sh in range(1, w):
                acc = acc + ext_a[pl.ds(h - sh, tm), cs]
            d_g = acc / jnp.minimum(pos, float(w)) - a_g
            d_ref[:, cs] = d_g.astype(BF16)
            e_g = _dot(d_g, pw_ref[gi], "nn")
            e_ref[:, cs] = e_g
            cat_ref[:, cs] = (e_g * ps_ref[:, cs]).astype(BF16)
        for lg in range(dp // LANES):
            cs = slice(lg * LANES, (lg + 1) * LANES)
            acc = jnp.broadcast_to(cb_ref[:, cs], (tm, LANES))
            for sh in range(CONV_K):
                acc = acc + _rotated(rot_g, h - sh, tm, cs) * cw_ref[pl.ds(CONV_K - 1 - sh, 1), cs]
            conv_out[:, cs] = acc
        hhat, rstd = _ln_stats(conv_out[...])
        hl = hhat * cg_ref[...] + cbt_ref[...]
        cat_ref[:, dp:] = (hl * _sig(hl)).astype(BF16)
        hh_ref[...] = hhat
        rs_ref[...] = jnp.broadcast_to(rstd, rs_ref.shape)

    specs = [_row(tm, dp, 0), _prev(tm, h, dp, 0), _row(tm, dp, 1), _prev(tm, h, dp, 1),
             _row(tm, dp, 2), _prev(tm, h, dp, 2),
             _full((4, POOL_GROUP, POOL_GROUP)), _full((1, dp)), _full((CONV_K, dp)),
             _full((1, dp)), _full((1, dp)), _full((1, dp))]
    out_specs = [_row(tm, 2 * dp), _row(tm, dp), _row(tm, dp), _row(tm, dp), _row(tm, dp), _row(tm, LANES)]
    out_shape = [jax.ShapeDtypeStruct((s, 2 * dp), BF16), jax.ShapeDtypeStruct((s, dp), BF16),
                 jax.ShapeDtypeStruct((s, dp), F32), jax.ShapeDtypeStruct((s, dp), F32),
                 jax.ShapeDtypeStruct((s, dp), F32), jax.ShapeDtypeStruct((s, LANES), F32)]
    return _call(
        body, name=name, grid=(s // tm,), in_specs=specs, out_specs=out_specs, out_shape=out_shape,
        scratch_shapes=[pltpu.VMEM((h + tm, dp), F32), pltpu.VMEM((SUBLANES, h + tm, dp), F32),
                        pltpu.VMEM((tm, dp), F32)],
        compiler_params=_cp(),
    )(u, u, u, u, u, u, pool_w, pool_scale, conv_w, conv_b, cn_g, cn_b)


def mixer_bwd(name, dcat, u, d_sv, e_sv, glu_sv, hh_sv, rs_sv, pool_w, pool_scale, conv_w, cn_g, cn_b):
    s = u.shape[0]
    dp = 512
    tm = min(256, s // 4)
    h = CONV_HALO
    nt = s // tm

    def body(dc_c, dc_n, bv_c, bg_c, d_c, e_c, gl_c, gl_p, hh_c, hh_n, rs_c, rs_n,
             pw_ref, ps_ref, cw_ref, cg_ref, cbt_ref,
             du_ref, dpw_ref, dps_ref, dcw_ref, dcb_ref, dcg_ref, dcbt_ref,
             ext_dh, ext_g, ext_r):
        i = pl.program_id(0)
        first = i == 0
        last = i == nt - 1
        cg = cg_ref[...]

        def conv_grads(dyb, hhat, rstd):
            hl = hhat * cg + cbt_ref[...]
            sg = _sig(hl)
            dhl = dyb * (sg * (1.0 + hl * (1.0 - sg)))
            return _ln_bwd(dhl * cg, hhat, rstd), dhl

        hh_cur = hh_c[...]
        dh_c, dhl_c = conv_grads(dc_c[:, dp:], hh_cur, rs_c[:, 0:1])
        dh_n, _ = conv_grads(dc_n[:, dp:], hh_n[...], rs_n[:, 0:1])
        _fill_rotations(ext_dh, jnp.concatenate([dh_c, jnp.where(last, 0.0, dh_n)], axis=0), 1)
        _fill_rotations(ext_g, jnp.concatenate([jnp.where(first, 0.0, gl_p[...]), gl_c[...]], axis=0), -1)

        @pl.when(first)
        def _():
            dcw_ref[...] = jnp.zeros(dcw_ref.shape, F32)

        for lg in range(dp // LANES):
            cs = slice(lg * LANES, (lg + 1) * LANES)
            dglu = jnp.zeros((tm, LANES), F32)
            for sh in range(CONV_K):
                dglu = dglu + _rotated(ext_dh, sh, tm, cs, 1) * cw_ref[pl.ds(CONV_K - 1 - sh, 1), cs]
            dh_g = ext_dh[0, pl.ds(0, tm), cs]
            for sh in range(CONV_K):
                dcw_ref[pl.ds(CONV_K - 1 - sh, 1), cs] += _colsum(dh_g * _rotated(ext_g, h - sh, tm, cs))
            sgate = _sig(bg_c[:, cs])
            du_ref[:, dp + lg * LANES:dp + (lg + 1) * LANES] = dglu * sgate
            du_ref[:, 2 * dp + lg * LANES:2 * dp + (lg + 1) * LANES] = dglu * bv_c[:, cs] * sgate * (1.0 - sgate)
        _acc_add(dcb_ref, first, _colsum(dh_c))
        _acc_add(dcg_ref, first, _colsum(dhl_c * hh_cur))
        _acc_add(dcbt_ref, first, _colsum(dhl_c))

        pos_c = _tile_pos(i, tm, tm)
        pos_n = _tile_pos(i + 1, tm, h)
        _acc_add(dps_ref, first, _colsum(dc_c[:, :dp] * e_c[...]))
        for gi, w in enumerate(POOL_WINDOWS):
            cs = slice(gi * POOL_GROUP, (gi + 1) * POOL_GROUP)
            pw = pw_ref[gi]
            de_c = dc_c[:, cs] * ps_ref[:, cs]
            de_n = dc_n[:, cs] * ps_ref[:, cs]
            dd_c = _dot(de_c, pw, "nt")
            dd_n = _dot(de_n, pw, "nt")
            ext_r[0:tm, :] = dd_c / jnp.minimum(pos_c, float(w))
            ext_r[tm:, :] = jnp.where(last, 0.0, dd_n / jnp.minimum(pos_n, float(w)))
            acc = -dd_c
            for sh in range(w):
                acc = acc + ext_r[pl.ds(sh, tm), :]
            du_ref[:, cs] = acc
            dpw_g = _dot(d_c[:, cs], de_c, "tn")

            @pl.when(first)
            def _():
                dpw_ref[gi] = dpw_g

            @pl.when(jnp.logical_not(first))
            def _():
                dpw_ref[gi] += dpw_g

    specs = [_row(tm, 2 * dp), _next(tm, h, 2 * dp, s), _row(tm, dp, 1), _row(tm, dp, 2),
             _row(tm, dp), _row(tm, dp), _row(tm, dp), _prev(tm, h, dp),
             _row(tm, dp), _next(tm, h, dp, s), _row(tm, LANES), _next(tm, h, LANES, s),
             _full((4, POOL_GROUP, POOL_GROUP)), _full((1, dp)), _full((CONV_K, dp)),
             _full((1, dp)), _full((1, dp))]
    out_specs = [_row(tm, 3 * dp), _full((4, POOL_GROUP, POOL_GROUP)), _full((1, dp)), _full((CONV_K, dp)),
                 _full((1, dp)), _full((1, dp)), _full((1, dp))]
    out_shape = [jax.ShapeDtypeStruct((s, 3 * dp), F32),
                 jax.ShapeDtypeStruct((4, POOL_GROUP, POOL_GROUP), F32), jax.ShapeDtypeStruct((1, dp), F32),
                 jax.ShapeDtypeStruct((CONV_K, dp), F32), jax.ShapeDtypeStruct((1, dp), F32),
                 jax.ShapeDtypeStruct((1, dp), F32), jax.ShapeDtypeStruct((1, dp), F32)]
    return _call(
        body, name=name, grid=(nt,), in_specs=specs, out_specs=out_specs, out_shape=out_shape,
        scratch_shapes=[pltpu.VMEM((SUBLANES, tm + h, dp), F32), pltpu.VMEM((SUBLANES, h + tm, dp), F32),
                        pltpu.VMEM((tm + h, POOL_GROUP), F32)],
        compiler_params=_cp(),
    )(dcat, dcat, u, u, d_sv, e_sv, glu_sv, glu_sv, hh_sv, hh_sv, rs_sv, rs_sv,
      pool_w, pool_scale, conv_w, cn_g, cn_b)


GELU_C = math.sqrt(2.0 / math.pi)


def _gelu_parts(x):
    x2 = x * x
    t = jnp.tanh(x * (GELU_C + (GELU_C * 0.044715) * x2))
    half_1pt = 0.5 + 0.5 * t
    gelu = x * half_1pt
    dgelu = half_1pt + (0.5 * x) * (1.0 - t * t) * (GELU_C + (3.0 * GELU_C * 0.044715) * x2)
    return gelu, dgelu


def ffn_act_fwd(name, gv, dw_w, dw_b):
    s = gv.shape[0]
    dff = gv.shape[1] // 2
    tm = min(FFN_TILE, s // 4)
    h = FFN_HALO
    rc = FFN_CHUNK_ROWS
    lw = FFN_CHUNK_LANES

    def body(g_c, g_p, v_c, w_ref, b_ref, hid_ref):
        first = pl.program_id(0) == 0

        def chunk(ci, carry):
            r0 = pl.multiple_of(ci * rc, rc)
            above = pl.multiple_of(jnp.maximum(r0 - h, 0), h)
            for lg in range(dff // lw):
                cs = slice(lg * lw, (lg + 1) * lw)
                top = jnp.where(ci == 0, jnp.where(first, 0.0, g_p[:, cs]), g_c[pl.ds(above, h), cs])
                win = jnp.concatenate([top, g_c[pl.ds(r0, rc), cs]], axis=0)
                gc = jnp.broadcast_to(b_ref[:, cs], (rc, lw))
                for sh in range(FFN_K):
                    gc = gc + win[h - sh:h - sh + rc] * w_ref[pl.ds(FFN_K - 1 - sh, 1), cs]
                gelu, _ = _gelu_parts(gc)
                hid_ref[pl.ds(r0, rc), cs] = (gelu * v_c[pl.ds(r0, rc), cs]).astype(BF16)
            return carry

        lax.fori_loop(0, tm // rc, chunk, 0)

    return _call(
        body, name=name, grid=(s // tm,),
        in_specs=[_row(tm, dff, 0), _prev(tm, h, dff, 0), _row(tm, dff, 1), _full((FFN_K, dff)), _full((1, dff))],
        out_specs=_row(tm, dff), out_shape=jax.ShapeDtypeStruct((s, dff), BF16),
        compiler_params=_cp(),
    )(gv, gv, gv, dw_w, dw_b)


def ffn_act_bwd(name, dhid, gv, dw_w, dw_b):
    s = gv.shape[0]
    dff = gv.shape[1] // 2
    tm = min(FFN_TILE, s // 4)
    h = FFN_HALO
    nt = s // tm
    rc = FFN_CHUNK_ROWS
    lw = FFN_CHUNK_LANES
    n_chunks = tm // rc

    def body(dh_c, dh_n, g_p, g_c, g_n, v_c, v_n, w_ref, b_ref, dgv_ref, dw_ref, db_ref):
        i = pl.program_id(0)
        first = i == 0
        last = i == nt - 1

        @pl.when(first)
        def _():
            dw_ref[...] = jnp.zeros(dw_ref.shape, F32)
            db_ref[...] = jnp.zeros(db_ref.shape, F32)

        def chunk(ci, carry):
            r0 = pl.multiple_of(ci * rc, rc)
            above = pl.multiple_of(jnp.maximum(r0 - h, 0), h)
            below = pl.multiple_of(jnp.minimum(r0 + rc, tm - h), h)
            at_end = ci == n_chunks - 1
            for lg in range(dff // lw):
                cs = slice(lg * lw, (lg + 1) * lw)
                top = jnp.where(ci == 0, jnp.where(first, 0.0, g_p[:, cs]), g_c[pl.ds(above, h), cs])
                bot = jnp.where(at_end, g_n[:, cs], g_c[pl.ds(below, h), cs])
                win = jnp.concatenate([top, g_c[pl.ds(r0, rc), cs], bot], axis=0)
                shifted = [win[h - sh:h - sh + rc + h] for sh in range(FFN_K)]
                gc = jnp.broadcast_to(b_ref[:, cs], (rc + h, lw))
                for sh in range(FFN_K):
                    gc = gc + shifted[sh] * w_ref[pl.ds(FFN_K - 1 - sh, 1), cs]
                gelu, dgelu = _gelu_parts(gc)
                dh_mid = dh_c[pl.ds(r0, rc), cs]
                hv_bot = jnp.where(at_end, jnp.where(last, 0.0, dh_n[:, cs] * v_n[:, cs]),
                                   dh_c[pl.ds(below, h), cs] * v_c[pl.ds(below, h), cs])
                dgc = jnp.concatenate([dh_mid * v_c[pl.ds(r0, rc), cs], hv_bot], axis=0) * dgelu
                dgate = jnp.zeros((rc, lw), F32)
                for sh in range(FFN_K):
                    dgate = dgate + dgc[sh:sh + rc] * w_ref[pl.ds(FFN_K - 1 - sh, 1), cs]
                dgv_ref[pl.ds(r0, rc), cs] = dgate.astype(BF16)
                dgv_ref[pl.ds(r0, rc), slice(dff + lg * lw, dff + (lg + 1) * lw)] = (dh_mid * gelu[0:rc]).astype(BF16)
                dgc_mid = dgc[0:rc]
                for sh in range(FFN_K):
                    dw_ref[pl.ds(FFN_K - 1 - sh, 1), cs] += _colsum(dgc_mid * shifted[sh][0:rc])
                db_ref[:, cs] += _colsum(dgc_mid)
            return carry

        lax.fori_loop(0, n_chunks, chunk, 0)

    return _call(
        body, name=name, grid=(nt,),
        in_specs=[_row(tm, dff), _next(tm, h, dff, s),
                  _prev(tm, h, dff, 0), _row(tm, dff, 0), _next(tm, h, dff, s, 0),
                  _row(tm, dff, 1), _next(tm, h, dff, s, 1),
                  _full((FFN_K, dff)), _full((1, dff))],
        out_specs=[_row(tm, 2 * dff), _full((FFN_K, dff)), _full((1, dff))],
        out_shape=[jax.ShapeDtypeStruct((s, 2 * dff), BF16), jax.ShapeDtypeStruct((FFN_K, dff), F32),
                   jax.ShapeDtypeStruct((1, dff), F32)],
        compiler_params=_cp(),
    )(dhid, dhid, gv, gv, gv, gv, gv, dw_w, dw_b)


def _bias_line(rel_bias):
    nh = rel_bias.shape[0]
    line = jnp.concatenate(
        [jnp.zeros((nh, 1), rel_bias.dtype), jnp.broadcast_to(rel_bias[:, 2 * MAX_REL:], (nh, SHEAR_SAT)),
         jnp.flip(rel_bias[:, 1:2 * MAX_REL], axis=1)], axis=1)
    return line[:, None, :]


def bias_tile(name, line):
    nh = line.shape[0]

    def body(l_ref, o_ref):
        x = jnp.broadcast_to(l_ref[...], (Q_TILE, SHEAR_W))
        z = pltpu.roll(x, SHEAR_W - Q_TILE, 1, stride=1, stride_axis=0)
        qc = lax.broadcasted_iota(jnp.int32, (Q_TILE, K_WIN), 0) // CHUNK
        kc = lax.broadcasted_iota(jnp.int32, (Q_TILE, K_WIN), 1) // CHUNK
        o_ref[...] = jnp.where((kc >= qc) & (kc <= qc + LEFT_CHUNKS), z[:, :K_WIN], NEG_INF)

    return _call(
        body, name=name, grid=(nh,), in_specs=[pl.BlockSpec((None, 1, SHEAR_W), lambda hh: (hh, 0, 0))],
        out_specs=pl.BlockSpec((None, Q_TILE, K_WIN), lambda hh: (hh, 0, 0)),
        out_shape=jax.ShapeDtypeStruct((nh, Q_TILE, K_WIN), F32), compiler_params=_cp(),
    )(line)


def _stack_heads(x2):
    lane = lax.broadcasted_iota(jnp.int32, x2.shape, 1)
    zero = jnp.zeros_like(x2)
    return jnp.concatenate([jnp.where(lane < HEAD_DIM, x2, zero), jnp.where(lane < HEAD_DIM, zero, x2)], axis=0)


def _unstack_heads(x_st):
    lane = lax.broadcasted_iota(jnp.int32, (Q_TILE, LANES), 1)
    return jnp.where(lane < HEAD_DIM, x_st[:Q_TILE], x_st[Q_TILE:])


def _attn_probs(q_st, k3, bias_st, t):
    sc = _dot(q_st, k3, "nt") * (HEAD_DIM ** -0.5) + bias_st
    col = lax.broadcasted_iota(jnp.int32, sc.shape, 1)
    sc = jnp.where(col >= PAD_ROWS - t * Q_TILE, sc, NEG_INF)
    m = jnp.max(sc, axis=-1, keepdims=True)
    p = jnp.exp(sc - m)
    return p * (1.0 / jnp.sum(p, axis=-1, keepdims=True))


def _attn_specs(d_model):
    nq = PAD_ROWS // Q_TILE
    groups = d_model // ATTN_LANES
    specs = [pl.BlockSpec((Q_TILE, ATTN_LANES), lambda g, t: (t + nq, g))]
    for which in (1, 2):
        for j in range(K_WIN // Q_TILE):
            specs.append(pl.BlockSpec((Q_TILE, ATTN_LANES), lambda g, t, j=j, which=which: (t + j, which * groups + g)))
    specs.append(pl.BlockSpec((2 * ATTN_PAIRS, Q_TILE, K_WIN), lambda g, t: (g, 0, 0)))
    return specs


def attn_fwd(name, qkvp, bias):
    s = qkvp.shape[0] - PAD_ROWS
    d_model = qkvp.shape[1] // 3
    nw = K_WIN // Q_TILE

    def body(q_ref, *refs):
        k_refs, v_refs, b_ref, o_ref = refs[:nw], refs[nw:2 * nw], refs[2 * nw], refs[2 * nw + 1]
        t = pl.program_id(1)
        for j in range(ATTN_PAIRS):
            ls = slice(j * LANES, (j + 1) * LANES)
            k3 = jnp.concatenate([r[:, ls] for r in k_refs], axis=0)
            v3 = jnp.concatenate([r[:, ls] for r in v_refs], axis=0)
            bias_st = b_ref[2 * j:2 * j + 2].reshape(2 * Q_TILE, K_WIN)
            p = _attn_probs(_stack_heads(q_ref[:, ls]), k3, bias_st, t)
            o_ref[:, ls] = _unstack_heads(_dot(p, v3, "nn")).astype(BF16)

    return _call(
        body, name=name, grid=(d_model // ATTN_LANES, s // Q_TILE),
        in_specs=_attn_specs(d_model), out_specs=pl.BlockSpec((Q_TILE, ATTN_LANES), lambda g, t: (t, g)),
        out_shape=jax.ShapeDtypeStruct((s, d_model), BF16), compiler_params=_cp(),
    )(qkvp, *([qkvp] * (2 * nw)), bias)


def attn_bwd(name, qkvp, bias, do):
    s = qkvp.shape[0] - PAD_ROWS
    d_model = qkvp.shape[1] // 3
    nw = K_WIN // Q_TILE
    nt = s // Q_TILE
    scale = HEAD_DIM ** -0.5

    def body(q_ref, *refs):
        k_refs, v_refs = refs[:nw], refs[nw:2 * nw]
        b_ref, do_ref, dq_ref, dk_ref, dv_ref, ds_ref, dk_acc, dv_acc = refs[2 * nw:]
        t = pl.program_id(1)
        first = t == 0

        @pl.when(first)
        def _():
            dk_acc[...] = jnp.zeros(dk_acc.shape, F32)
            dv_acc[...] = jnp.zeros(dv_acc.shape, F32)

        start = pl.multiple_of(t * Q_TILE, Q_TILE)
        for j in range(ATTN_PAIRS):
            ls = slice(j * LANES, (j + 1) * LANES)
            q_st = _stack_heads(q_ref[:, ls])
            do_st = _stack_heads(do_ref[:, ls])
            k3 = jnp.concatenate([r[:, ls] for r in k_refs], axis=0)
            v3 = jnp.concatenate([r[:, ls] for r in v_refs], axis=0)
            p = _attn_probs(q_st, k3, b_ref[2 * j:2 * j + 2].reshape(2 * Q_TILE, K_WIN), t)
            dp = _dot(do_st, v3, "nt")
            ds = p * (dp - jnp.sum(p * dp, axis=-1, keepdims=True))
            _acc_add(ds_ref.at[2 * j:2 * j + 2], first, ds.reshape(2, Q_TILE, K_WIN))
            dsb = (ds * scale).astype(BF16)
            dq_ref[:, ls] = _unstack_heads(_dot(dsb, k3, "nn")).astype(BF16)
            dk_acc[pl.ds(start, K_WIN), ls] += _dot(dsb, q_st, "tn")
            dv_acc[pl.ds(start, K_WIN), ls] += _dot(p, do_st, "tn")

        @pl.when(t == nt - 1)
        def _():
            dk_ref[...] = dk_acc[pl.ds(PAD_ROWS, s), :].astype(BF16)
            dv_ref[...] = dv_acc[pl.ds(PAD_ROWS, s), :].astype(BF16)

    specs = _attn_specs(d_model) + [pl.BlockSpec((Q_TILE, ATTN_LANES), lambda g, t: (t, g))]
    col_spec = pl.BlockSpec((s, ATTN_LANES), lambda g, t: (0, g))
    return _call(
        body, name=name, grid=(d_model // ATTN_LANES, nt), in_specs=specs,
        out_specs=[pl.BlockSpec((Q_TILE, ATTN_LANES), lambda g, t: (t, g)), col_spec, col_spec,
                   pl.BlockSpec((2 * ATTN_PAIRS, Q_TILE, K_WIN), lambda g, t: (g, 0, 0))],
        out_shape=[jax.ShapeDtypeStruct((s, d_model), BF16)] * 3
        + [jax.ShapeDtypeStruct((N_HEADS, Q_TILE, K_WIN), F32)],
        scratch_shapes=[pltpu.VMEM((PAD_ROWS + s, ATTN_LANES), F32), pltpu.VMEM((PAD_ROWS + s, ATTN_LANES), F32)],
        compiler_params=_cp(),
    )(qkvp, *([qkvp] * (2 * nw)), bias, do)


def bias_grad_reduce(name, ds_sum):
    nh = ds_sum.shape[0]
    width = SHEAR_W + Q_TILE
    first_k = Q_TILE - 1

    def body(x_ref, col_ref, sat_ref):
        x = x_ref[...]
        hi = x.astype(BF16)
        lo = (x - hi.astype(F32)).astype(BF16)
        r = lax.broadcasted_iota(jnp.int32, (Q_TILE, Q_TILE), 0)
        c = lax.broadcasted_iota(jnp.int32, (Q_TILE, Q_TILE), 1)
        exchange = jnp.where(r + c == Q_TILE - 1, 1.0, 0.0).astype(BF16)
        x_rev = _dot(exchange, hi, "nn") + _dot(exchange, lo, "nn")
        zeros = jnp.zeros((Q_TILE, Q_TILE), F32)
        y = pltpu.roll(jnp.concatenate([zeros, x_rev, zeros], axis=1), 0, 1, stride=1, stride_axis=0)
        cols = _colsum(y)
        col_ref[...] = cols
        k = lax.broadcasted_iota(jnp.int32, cols.shape, 1) - first_k
        tot = jnp.sum(jnp.where((k >= 1) & (k <= SHEAR_SAT), cols, 0.0), axis=-1, keepdims=True)
        sat_ref[...] = jnp.broadcast_to(tot, sat_ref.shape)

    return _call(
        body, name=name, grid=(nh,),
        in_specs=[pl.BlockSpec((None, Q_TILE, K_WIN), lambda hh: (hh, 0, 0))],
        out_specs=[pl.BlockSpec((None, 1, width), lambda hh: (hh, 0, 0)),
                   pl.BlockSpec((None, 1, LANES), lambda hh: (hh, 0, 0))],
        out_shape=[jax.ShapeDtypeStruct((nh, 1, width), F32), jax.ShapeDtypeStruct((nh, 1, LANES), F32)],
        compiler_params=_cp(),
    )(ds_sum)


def _ew_rows(r, most=512, cols=None):
    if cols is not None and r * cols * 4 <= SMALL_BLOCK_BYTES:
        return r
    for cand in (512, 256, 128, 64, 32, 16, 8):
        if cand <= most and r % cand == 0:
            return cand
    return r


def to_bf16(name, a):
    s, d = a.shape
    tm = _row_tile(s)

    def body(a_ref, o_ref):
        o_ref[...] = a_ref[...].astype(BF16)

    return _call(
        body, name=name, grid=(s // tm,), in_specs=[_row(tm, d)], out_specs=_row(tm, d),
        out_shape=jax.ShapeDtypeStruct((s, d), BF16), compiler_params=_cp(),
    )(a)


def cast_into_gathered(name, w, layer, s_idx, n_blocks=N_SHARD, dtype=BF16, token=None):
    r, c = w.shape[-2:]
    tr = _ew_rows(r, cols=c)

    def body(s_ref, w_ref, *rest):
        rest[-1][...] = w_ref[...].astype(dtype)

    extra = [] if token is None else [token]
    grid_spec = pltpu.PrefetchScalarGridSpec(
        num_scalar_prefetch=1, grid=(r // tr,),
        in_specs=[pl.BlockSpec((None, tr, c), lambda i, s_ref: (layer, i, 0))] + [ANY_SPEC] * len(extra),
        out_specs=pl.BlockSpec((None, tr, c), lambda i, s_ref: (s_ref[0], i, 0)))
    return _call(
        body, name=name, grid_spec=grid_spec, out_shape=jax.ShapeDtypeStruct((n_blocks, r, c), dtype),
        compiler_params=_cp(),
    )(s_idx, w, *extra)


def adamw(name, w, grads, m, v, token=None):
    nl, r, c = w.shape
    tr = _ew_rows(r, 256, cols=c)

    def body(*refs):
        w_ref, m_ref, v_ref = refs[0], refs[1], refs[2]
        g_refs = refs[3:3 + nl]
        d_ref, nm_ref, nv_ref = refs[-3:]
        layer = pl.program_id(0)
        g = g_refs[0][...]
        for j in range(1, nl):
            g = jnp.where(layer == j, g_refs[j][...], g)
        nm = ADAM_B1 * m_ref[...] + (1.0 - ADAM_B1) * g
        nv = ADAM_B2 * v_ref[...] + (1.0 - ADAM_B2) * (g * g)
        m_hat = nm / ADAM_BC1
        v_hat = nv / ADAM_BC2
        d_ref[...] = -ADAM_LR * (m_hat / (jnp.sqrt(v_hat) + ADAM_EPS) + ADAM_WD * w_ref[...])
        nm_ref[...] = nm
        nv_ref[...] = nv

    p_spec = pl.BlockSpec((None, tr, c), lambda l, i: (l, i, 0))
    g_spec = pl.BlockSpec((tr, c), lambda l, i: (i, 0))
    extra = [] if token is None else [token]
    extra_specs = [] if token is None else [ANY_SPEC]
    return _call(
        body, name=name, grid=(nl, r // tr), in_specs=[p_spec] * 3 + [g_spec] * nl + extra_specs,
        out_specs=[p_spec] * 3, out_shape=[jax.ShapeDtypeStruct((nl, r, c), F32)] * 3, compiler_params=_cp(),
    )(w, m, v, *grads, *extra)


def sum_blocks(name, gathered, n_blocks):
    r = gathered.shape[0] // n_blocks
    c = gathered.shape[1]
    tr = r if r <= SUM_BLOCK_ROWS else _ew_rows(r)
    nt = r // tr

    def body(*refs):
        acc = refs[0][...]
        for j in range(1, n_blocks):
            acc = acc + refs[j][...]
        refs[-1][...] = acc

    specs = [pl.BlockSpec((tr, c), lambda i, j=j: (j * nt + i, 0)) for j in range(n_blocks)]
    return _call(
        body, name=name, grid=(nt,), in_specs=specs, out_specs=pl.BlockSpec((tr, c), lambda i: (i, 0)),
        out_shape=jax.ShapeDtypeStruct((r, c), F32), compiler_params=_cp(),
    )(*([gathered] * n_blocks))


def _place():
    return lax.axis_index("x"), lax.axis_index("y"), lax.axis_index("c")


def _other_chips(x, y):
    return [(1 - x, y), (x, 1 - y), (1 - x, 1 - y)]


HBM_SPEC = pl.BlockSpec(memory_space=pltpu.HBM)
SEM_SPEC = pl.BlockSpec(memory_space=pltpu.SEMAPHORE)
ANY_SPEC = pl.BlockSpec(memory_space=pl.ANY)
EFFECT = pltpu.SideEffectType.DATAFLOW_SIDE_EFFECTING


def copies_start(name, bufs, plan, n_copies):
    n = len(bufs)

    def body(*refs):
        send, recv = refs[n], refs[n + 1]
        token = refs[2 * n + 2]
        for k, (src, dst, peer, _) in enumerate(plan(refs[:n])):
            pltpu.make_async_remote_copy(
                src_ref=src, dst_ref=dst, send_sem=send.at[k], recv_sem=recv.at[k],
                device_id=peer, device_id_type=MESH).start()
        token[...] = jnp.zeros(token.shape, F32)

    outs = pl.pallas_call(
        body, name=name,
        out_shape=(pltpu.SemaphoreType.DMA((n_copies,)), pltpu.SemaphoreType.DMA((n_copies,)),
                   *[pltpu.HBM(b.shape, b.dtype) for b in bufs], jax.ShapeDtypeStruct((8, LANES), F32)),
        in_specs=[HBM_SPEC] * n,
        out_specs=(SEM_SPEC, SEM_SPEC, *([HBM_SPEC] * n), pl.BlockSpec(memory_space=pltpu.VMEM)),
        input_output_aliases={a: a + 2 for a in range(n)},
        compiler_params=pltpu.CompilerParams(has_side_effects=EFFECT),
    )(*[_in_hbm(b) for b in bufs])
    return outs[0], outs[1], list(outs[2:2 + n]), outs[2 + n]


def copies_wait(name, bufs, send, recv, plan, sem_base, after):
    n = len(bufs)

    def body(*refs):
        send_ref, recv_ref = refs[n], refs[n + 1]
        for k, (src, _, peer, land) in enumerate(plan(refs[:n])):
            cp = pltpu.make_async_remote_copy(
                src_ref=src, dst_ref=land, send_sem=send_ref.at[sem_base + k], recv_sem=recv_ref.at[sem_base + k],
                device_id=peer, device_id_type=MESH)
            cp.wait_send()
            cp.wait_recv()

    outs = pl.pallas_call(
        body, name=name,
        out_shape=tuple(pltpu.HBM(b.shape, b.dtype) for b in bufs),
        in_specs=[HBM_SPEC] * n + [SEM_SPEC, SEM_SPEC, ANY_SPEC], out_specs=tuple([HBM_SPEC] * n),
        input_output_aliases={a: a for a in range(n)},
        compiler_params=pltpu.CompilerParams(has_side_effects=EFFECT),
    )(*bufs, send, recv, after)
    return list(outs)


def gather_plan(refs):
    x, y, c = _place()
    me = 2 * x + y
    return [(buf.at[me], buf.at[me], (cx, cy, c), buf.at[2 * cx + cy])
            for buf in refs for cx, cy in _other_chips(x, y)]


def all_plan(refs):
    x, y, c = _place()
    me = 4 * x + 2 * y + c
    out = []
    for buf in refs:
        for flip in range(1, 8):
            px = 1 - x if flip & 4 else x
            py = 1 - y if flip & 2 else y
            pc = 1 - c if flip & 1 else c
            out.append((buf.at[me], buf.at[me], (px, py, pc), buf.at[4 * px + 2 * py + pc]))
    return out


def swap_plan(refs):
    x, y, c = _place()
    n = len(refs) // 2
    out = []
    for g, land in zip(refs[:n], refs[n:]):
        hr = g.shape[1] // 2
        out.append((g.at[:, pl.ds((1 - c) * hr, hr)], land, (x, y, 1 - c), land))
    return out


def owners_plan(refs):
    x, y, c = _place()
    n = len(refs) // 2
    return [(src.at[2 * cx + cy], land.at[j], (cx, cy, c), land.at[j])
            for src, land in zip(refs[:n], refs[n:]) for j, (cx, cy) in enumerate(_other_chips(x, y))]


def join_plan(refs):
    x, y, c = _place()
    out = []
    for buf in refs:
        hr = buf.shape[0] // 2
        mine = buf.at[pl.ds(c * hr, hr)]
        out.append((mine, mine, (x, y, 1 - c), buf.at[pl.ds((1 - c) * hr, hr)]))
    return out


def add_halves(name, grad, landed, sc_idx):
    _, r, c = grad.shape
    hr = r // 2
    tr = _ew_rows(hr)
    nt = hr // tr

    def body(sc_ref, g_ref, l_ref, own_ref, wire_ref):
        tot = g_ref[...] + l_ref[...]
        wire_ref[...] = tot.astype(BF16)

        @pl.when(pl.program_id(1) == sc_ref[0])
        def _():
            own_ref[...] = tot

    grid_spec = pltpu.PrefetchScalarGridSpec(
        num_scalar_prefetch=1, grid=(nt, N_SHARD),
        in_specs=[pl.BlockSpec((None, tr, c), lambda i, sh, sc_ref: (sh, sc_ref[1] * nt + i, 0)),
                  pl.BlockSpec((None, tr, c), lambda i, sh, sc_ref: (sh, i, 0))],
        out_specs=[pl.BlockSpec((tr, c), lambda i, sh, sc_ref: (i, 0)),
                   pl.BlockSpec((None, tr, c), lambda i, sh, sc_ref: (sh, i, 0))])
    return _call(
        body, name=name, grid_spec=grid_spec,
        out_shape=[jax.ShapeDtypeStruct((hr, c), F32), jax.ShapeDtypeStruct((N_SHARD, hr, c), BF16)],
        compiler_params=_cp(),
    )(sc_idx, grad, landed)


def add_owned(name, own, landed, sc_idx):
    hr, c = own.shape
    tr = _ew_rows(hr)
    nt = hr // tr

    def body(sc_ref, o_ref, l0, l1, l2, out_ref):
        out_ref[...] = ((o_ref[...] + l0[...].astype(F32)) + l1[...].astype(F32)) + l2[...].astype(F32)

    grid_spec = pltpu.PrefetchScalarGridSpec(
        num_scalar_prefetch=1, grid=(nt,),
        in_specs=[pl.BlockSpec((tr, c), lambda i, sc_ref: (i, 0))]
        + [pl.BlockSpec((None, tr, c), lambda i, sc_ref, j=j: (j, i, 0)) for j in range(3)],
        out_specs=pl.BlockSpec((tr, c), lambda i, sc_ref: (sc_ref[1] * nt + i, 0)))
    return _call(
        body, name=name, grid_spec=grid_spec, out_shape=jax.ShapeDtypeStruct((2 * hr, c), F32),
        compiler_params=_cp(),
    )(sc_idx, own, landed, landed, landed)


PACK_QUANTUM = 8 * LANES


def _pack(arrays):
    pieces = []
    for a in arrays:
        flat = a.reshape(-1)
        padded = -(-flat.shape[0] // PACK_QUANTUM) * PACK_QUANTUM
        pieces.append(jnp.pad(flat, (0, padded - flat.shape[0])).reshape(-1, LANES))
    return jnp.concatenate(pieces, axis=0)


def _unpack(packed, shapes):
    out = []
    row = 0
    for shp in shapes:
        size = math.prod(shp)
        rows = -(-size // PACK_QUANTUM) * 8
        out.append(packed[row:row + rows].reshape(-1)[:size].reshape(shp))
        row += rows
    return out


def kernel(x, p, mix_w_in, pool_w, pool_scale, conv_dw_w, conv_dw_b, conv_ln_g, conv_ln_b, mix_w_out, attn_w_qkv, attn_rel_bias, attn_w_o, ln_mix_g, ln_mix_b, ffn_w_up, ffn_dw_w, ffn_dw_b, ffn_w_down, ple_w_proj, ple_w_gate, ple_b_gate, ln_ffn_g, ln_ffn_b, loss_target, m_mix_w_in, m_pool_w, m_pool_scale, m_conv_dw_w, m_conv_dw_b, m_conv_ln_g, m_conv_ln_b, m_mix_w_out, m_attn_w_qkv, m_attn_rel_bias, m_attn_w_o, m_ln_mix_g, m_ln_mix_b, m_ffn_w_up, m_ffn_dw_w, m_ffn_dw_b, m_ffn_w_down, m_ple_w_proj, m_ple_w_gate, m_ple_b_gate, m_ln_ffn_g, m_ln_ffn_b, v_mix_w_in, v_pool_w, v_pool_scale, v_conv_dw_w, v_conv_dw_b, v_conv_ln_g, v_conv_ln_b, v_mix_w_out, v_attn_w_qkv, v_attn_rel_bias, v_attn_w_o, v_ln_mix_g, v_ln_mix_b, v_ffn_w_up, v_ffn_dw_w, v_ffn_dw_b, v_ffn_w_down, v_ple_w_proj, v_ple_w_gate, v_ple_b_gate, v_ln_ffn_g, v_ln_ffn_b):
    xi, yi, ci = _place()
    shard_idx = (2 * xi + yi).astype(jnp.int32)
    s_arr = shard_idx.reshape(1)
    c_arr = ci.astype(jnp.int32).reshape(1)
    sc_arr = jnp.concatenate([s_arr, c_arr])

    x0 = x[0]
    target = loss_target[0]
    p_rows = p.reshape(p.shape[0] * p.shape[2], p.shape[3])
    seq = x0.shape[0]

    big = [
        ("mix_w_in", mix_w_in, m_mix_w_in, v_mix_w_in, True),
        ("mix_w_out", mix_w_out, m_mix_w_out, v_mix_w_out, False),
        ("attn_w_qkv", attn_w_qkv, m_attn_w_qkv, v_attn_w_qkv, True),
        ("attn_w_o", attn_w_o, m_attn_w_o, v_attn_w_o, False),
        ("ffn_w_up", ffn_w_up, m_ffn_w_up, v_ffn_w_up, True),
        ("ffn_w_down", ffn_w_down, m_ffn_w_down, v_ffn_w_down, False),
        ("ple_w_proj", ple_w_proj, m_ple_w_proj, v_ple_w_proj, True),
        ("ple_w_gate", ple_w_gate, m_ple_w_gate, v_ple_w_gate, False),
    ]
    params = {nm: w for nm, w, _, _, _ in big}
    col_sharded = {nm: cs for nm, _, _, _, cs in big}
    keys = [("mix_w_in", 0), ("mix_w_out", 0), ("ffn_w_up", 0), ("ffn_w_down", 0), ("ple_w_gate", 0),
            ("ple_w_proj", 0), ("attn_w_qkv", 0), ("attn_w_o", 0), ("ffn_w_up", 1), ("ffn_w_down", 1),
            ("ple_w_gate", 1), ("ple_w_proj", 1)]
    dw_shapes = [conv_dw_w.shape, ffn_dw_w.shape]
    dw_block = cast_into_gathered("place_dw", _pack([conv_dw_w, ffn_dw_w])[None], 0, s_arr, dtype=F32)
    n_first = 2
    started = {}
    gather_token = None
    for tag, group in (("first", keys[:n_first]), ("rest", keys[n_first:])):
        shards = [cast_into_gathered(f"cast_{nm}_{layer}", params[nm], layer, s_arr, token=gather_token)
                  for nm, layer in group]
        if tag == "first":
            shards.append(dw_block)
        send, recv, bufs, gather_token = copies_start(f"gather_start_{tag}", shards, gather_plan, 3 * len(shards))
        for a, key in enumerate(group):
            started[key] = (send, recv, bufs[a], 3 * a)
        if tag == "first":
            dw_started = (send, recv, bufs[-1], 3 * len(group))
    arrived_w = {}

    def weight(nm, layer, after=None):
        key = (nm, layer)
        if key not in arrived_w:
            send, recv, buf, base = started[key]
            arrived_w[key] = copies_wait(f"gather_wait_{nm}_{layer}", [buf], send, recv, gather_plan, base, after)[0]
        g = arrived_w[key]
        if col_sharded[nm]:
            return g
        return g.reshape(g.shape[0] * g.shape[1], g.shape[2])

    def tie(a, token):
        return a + token[0:1, 0:1].astype(a.dtype)

    class Reducer:
        def __init__(self, tag, group):
            self.tag, self.group, self.stage = tag, group, 0
            self.n = len(group)
            self.result = None

        def advance(self, after):
            tag, n = self.tag, self.n
            if self.stage == 0:
                grads = []
                for key in self.group:
                    g = big_grads[key]
                    grads.append(g if g.ndim == 3 else g.reshape(N_SHARD, g.shape[0] // N_SHARD, g.shape[1]))
                lands = [lax.empty((N_SHARD, g.shape[1] // 2, g.shape[2]), F32) for g in grads]
                self.sems = copies_start(f"swap_start_{tag}", grads + lands, swap_plan, n)
            elif self.stage == 1:
                send, recv, bufs, _ = self.sems
                outs = copies_wait(f"swap_wait_{tag}", bufs, send, recv, swap_plan, 0, after)
                self.own, wire = [], []
                for key, g, ld in zip(self.group, outs[:n], outs[n:]):
                    o, ob = add_halves(f"add_halves_{key[0]}_{key[1]}", g, ld, sc_arr)
                    self.own.append(o)
                    wire.append(ob)
                lands = [lax.empty((3,) + w.shape[1:], BF16) for w in wire]
                self.sems = copies_start(f"owners_start_{tag}", wire + lands, owners_plan, 3 * n)
            elif self.stage == 2:
                send, recv, bufs, _ = self.sems
                outs = copies_wait(f"owners_wait_{tag}", bufs, send, recv, owners_plan, 0, after)
                finals = [add_owned(f"add_owned_{key[0]}_{key[1]}", o, ar, sc_arr)
                          for key, o, ar in zip(self.group, self.own, outs[n:])]
                self.sems = copies_start(f"join_start_{tag}", finals, join_plan, n)
            elif self.stage == 3:
                send, recv, bufs, _ = self.sems
                outs = copies_wait(f"join_wait_{tag}", bufs, send, recv, join_plan, 0, after)
                self.result = dict(zip(self.group, outs))
                self.sems = None
            self.stage += 1
            return None if self.sems is None else self.sems[3]

    dw_cache = []

    def conv_weights(after):
        if not dw_cache:
            send, recv, buf, base = dw_started
            dw_all = copies_wait("gather_wait_dw", [buf], send, recv, gather_plan, base, after)[0]
            dw_parts = [_unpack(dw_all[k], dw_shapes) for k in range(N_SHARD)]
            dw_cache.append(jnp.concatenate([pc[0] for pc in dw_parts], axis=2)[0])
            dw_cache.append(jnp.concatenate([pc[1] for pc in dw_parts], axis=2))
        return dw_cache

    big_grads = {}
    small_grads = {}

    saved = []
    h_in = x0
    h_in_b = to_bf16("x_bf16", x0)
    for layer in range(N_LAYERS):
        sv = {"x_in": h_in_b}
        if layer % 2 == 0:
            u = mm_cols_fwd("mix_in", h_in_b, weight("mix_w_in", 0, gather_token), F32)
            conv_w_full, ffn_dw_full = conv_weights(u)
            cat, d_sv, e_sv, glu_sv, hh_sv, rs_sv = mixer_fwd(
                "mixer_fwd", u, pool_w[0], pool_scale, conv_w_full, conv_dw_b, conv_ln_g, conv_ln_b)
            mix = mm_rows_fwd("mix_out", cat, weight("mix_w_out", 0, cat))
            sv.update(u=u, cat=cat, d=d_sv, e=e_sv, glu=glu_sv, hh=hh_sv, rs=rs_sv)
        else:
            qkvp = mm_cols_fwd("attn_qkv", h_in_b, weight("attn_w_qkv", 0, h_in_b), BF16,
                               pad_blocks=PAD_ROWS // _row_tile(seq))
            bias = bias_tile("bias_tile", _bias_line(attn_rel_bias[0]))
            att = attn_fwd("attn_fwd", qkvp, bias)
            mix = mm_rows_fwd("attn_out", att, weight("attn_w_o", 0, att))
            sv.update(qkvp=qkvp, bias=bias, att=att)
        x1, x1_b, xh1, rs1 = ln_fwd(f"ln_mix_{layer}", h_in, mix, ln_mix_g[layer:layer + 1],
                                    ln_mix_b[layer:layer + 1])
        gv = mm_cols_fwd(f"ffn_up_{layer}", x1_b, weight("ffn_w_up", layer, x1_b), F32)
        hid = ffn_act_fwd(f"ffn_act_{layer}", gv, ffn_dw_full[layer], ffn_dw_b[layer:layer + 1])
        ffn = mm_rows_fwd(f"ffn_down_{layer}", hid, weight("ffn_w_down", layer, hid))
        pgl = mm_rows_fwd(f"ple_gate_{layer}", x1_b, weight("ple_w_gate", layer, ffn))
        pp = mm_cols_fwd(f"ple_proj_{layer}", p_rows, weight("ple_w_proj", layer, pgl), F32, part=(layer, N_LAYERS))
        bg = ple_b_gate[layer:layer + 1]
        x2, x2_b, xh2, rs2 = ln_fwd(f"ln_ffn_{layer}", x1, ffn, ln_ffn_g[layer:layer + 1], ln_ffn_b[layer:layer + 1],
                                    ple=(pgl, pp, bg), emit_y=layer < N_LAYERS - 1)
        sv.update(x1=x1_b, xh1=xh1, rs1=rs1, gv=gv, hid=hid, pgl=pgl, pp=pp, xh2=xh2, rs2=rs2)
        saved.append(sv)
        h_in, h_in_b = x2, x2_b

    reducers = []

    def open_group(tag, group):
        reducers.append(Reducer(tag, group))
        return reducers[-1].advance(None)

    def hook(after):
        token = None
        for red in reducers:
            if red.stage < 4:
                tk = red.advance(after)
                if tk is not None:
                    token = tk if token is None else token + tk
        return token

    def tied(a, token):
        return a if token is None else tie(a, token)

    parts = []
    token = None
    for layer in reversed(range(N_LAYERS)):
        sv = saved[layer]
        bg = ple_b_gate[layer:layer + 1]
        if layer == 0:
            token = open_group("layer1", [("attn_w_qkv", 0), ("attn_w_o", 0), ("ffn_w_up", 1), ("ffn_w_down", 1),
                                          ("ple_w_gate", 1), ("ple_w_proj", 1)])
        last = layer == N_LAYERS - 1
        res = ln_bwd(
            f"ln_ffn_bwd_{layer}", parts, sv["xh2"], sv["rs2"], tied(ln_ffn_g[layer:layer + 1], token),
            ple=(sv["pgl"], sv["pp"], bg), loss=(target, ln_ffn_b[layer:layer + 1]) if last else None)
        dz2, dg2, db2, dpp, dpgl, dbg = res[:6]
        if last:
            loss_part = res[6]
        small_grads[("ln_ffn_g", layer)] = dg2
        small_grads[("ln_ffn_b", layer)] = db2
        small_grads[("ple_b_gate", layer)] = dbg
        w_down = weight("ffn_w_down", layer)
        dhid = mm_rows_dx(f"ffn_down_dx_{layer}", dz2, w_down)
        big_grads[("ffn_w_down", layer)] = mm_rows_dw(f"ffn_down_dw_{layer}", sv["hid"], dz2)
        token = hook(big_grads[("ffn_w_down", layer)])
        dgv, ddw, ddb = ffn_act_bwd(f"ffn_act_bwd_{layer}", dhid, sv["gv"], ffn_dw_full[layer],
                                    tied(ffn_dw_b[layer:layer + 1], token))
        small_grads[("ffn_dw_w", layer)] = ddw
        small_grads[("ffn_dw_b", layer)] = ddb
        big_grads[("ffn_w_up", layer)] = mm_cols_dw(f"ffn_up_dw_{layer}", sv["x1"], dgv)
        t_up = mm_cols_dx(f"ffn_up_dx_{layer}", dgv, weight("ffn_w_up", layer))
        token = hook(t_up)
        big_grads[("ple_w_gate", layer)] = mm_rows_dw(f"ple_gate_dw_{layer}", sv["x1"], dpgl)
        t_gate = mm_rows_dx(f"ple_gate_dx_{layer}", dpgl, weight("ple_w_gate", layer))
        big_grads[("ple_w_proj", layer)] = mm_cols_dw(f"ple_proj_dw_{layer}", p_rows, dpp, part=(layer, N_LAYERS))
        token2 = hook(big_grads[("ple_w_proj", layer)])
        if token2 is not None:
            token = token2 if token is None else token + token2
        if layer == 0:
            token3 = open_group("layer0_ffn", [("ffn_w_up", 0), ("ffn_w_down", 0), ("ple_w_gate", 0), ("ple_w_proj", 0)])
            token = token3 if token is None else token + token3
        dz1, dg1, db1 = ln_bwd(
            f"ln_mix_bwd_{layer}", [(ALPHA, dz2), (1.0, t_up), (1.0, t_gate)], sv["xh1"], sv["rs1"],
            tied(ln_mix_g[layer:layer + 1], token))
        small_grads[("ln_mix_g", layer)] = dg1
        small_grads[("ln_mix_b", layer)] = db1
        if layer % 2 == 0:
            dcat = mm_rows_dx("mix_out_dx", dz1, weight("mix_w_out", 0))
            big_grads[("mix_w_out", 0)] = mm_rows_dw("mix_out_dw", sv["cat"], dz1)
            token = hook(big_grads[("mix_w_out", 0)])
            du, dpw, dps, dcw, dcb, dcg, dcbt = mixer_bwd(
                "mixer_bwd", dcat, sv["u"], sv["d"], sv["e"], sv["glu"], sv["hh"], sv["rs"],
                pool_w[0], pool_scale, conv_w_full, tied(conv_ln_g, token), conv_ln_b)
            small_grads[("pool_w", 0)] = dpw
            small_grads[("pool_scale", 0)] = dps
            small_grads[("conv_dw_w", 0)] = dcw
            small_grads[("conv_dw_b", 0)] = dcb
            small_grads[("conv_ln_g", 0)] = dcg
            small_grads[("conv_ln_b", 0)] = dcbt
            big_grads[("mix_w_in", 0)] = mm_cols_dw("mix_in_dw", sv["x_in"], du)
            hook(big_grads[("mix_w_in", 0)])
            open_group("layer0_mix", [("mix_w_in", 0), ("mix_w_out", 0)])
            dx_in = mm_cols_dx("mix_in_dx", du, weight("mix_w_in", 0), addend=(ALPHA, dz1))
            token = hook(dx_in)
        else:
            do = mm_rows_dx("attn_out_dx", dz1, weight("attn_w_o", 0), out_dtype=BF16)
            big_grads[("attn_w_o", 0)] = mm_rows_dw("attn_out_dw", sv["att"], dz1)
            dq, dk, dv, ds_sum = attn_bwd("attn_bwd", sv["qkvp"], sv["bias"], do)
            cols, sat = bias_grad_reduce("bias_grad", ds_sum)
            d_rel = jnp.concatenate(
                [jnp.zeros((N_HEADS, 1), F32),
                 jnp.flip(cols[:, 0, Q_TILE + SHEAR_SAT:Q_TILE - 1 + SHEAR_W], axis=1),
                 sat[:, 0, 0:1]], axis=1)
            small_grads[("attn_rel_bias", 0)] = d_rel
            dqkv = jnp.concatenate([dq, dk, dv], axis=1)
            big_grads[("attn_w_qkv", 0)] = mm_cols_dw("attn_qkv_dw", sv["x_in"], dqkv)
            dx_in = mm_cols_dx("attn_qkv_dx", dqkv, weight("attn_w_qkv", 0), addend=(ALPHA, dz1))
        parts = [(1.0, dx_in)]
    grad_x = dx_in

    small = [
        ("pool_w", pool_w, m_pool_w, v_pool_w, None),
        ("pool_scale", pool_scale, m_pool_scale, v_pool_scale, None),
        ("conv_dw_w", conv_dw_w, m_conv_dw_w, v_conv_dw_w, 2),
        ("conv_dw_b", conv_dw_b, m_conv_dw_b, v_conv_dw_b, None),
        ("conv_ln_g", conv_ln_g, m_conv_ln_g, v_conv_ln_g, None),
        ("conv_ln_b", conv_ln_b, m_conv_ln_b, v_conv_ln_b, None),
        ("attn_rel_bias", attn_rel_bias, m_attn_rel_bias, v_attn_rel_bias, None),
        ("ln_mix_g", ln_mix_g, m_ln_mix_g, v_ln_mix_g, None),
        ("ln_mix_b", ln_mix_b, m_ln_mix_b, v_ln_mix_b, None),
        ("ffn_dw_w", ffn_dw_w, m_ffn_dw_w, v_ffn_dw_w, 2),
        ("ffn_dw_b", ffn_dw_b, m_ffn_dw_b, v_ffn_dw_b, None),
        ("ple_b_gate", ple_b_gate, m_ple_b_gate, v_ple_b_gate, None),
        ("ln_ffn_g", ln_ffn_g, m_ln_ffn_g, v_ln_ffn_g, None),
        ("ln_ffn_b", ln_ffn_b, m_ln_ffn_b, v_ln_ffn_b, None),
    ]
    full_grads = []
    for nm, w, _, _, shard_axis in small:
        full = list(w.shape)
        if shard_axis is not None:
            full[shard_axis] *= N_SHARD
        per_layer = [small_grads[(nm, layer)].reshape((1,) + tuple(full[1:])) for layer in range(w.shape[0])]
        full_grads.append(jnp.concatenate(per_layer, axis=0))
    packed = _pack(full_grads + [loss_part])
    dev_arr = (4 * xi + 2 * yi + ci).astype(jnp.int32).reshape(1)
    sg_block = cast_into_gathered("place_small_grads", packed[None], 0, dev_arr, n_blocks=8, dtype=F32)
    sg_send, sg_recv, sg_bufs, sg_token = copies_start("small_grads_start", [sg_block], all_plan, 7)
    token = sg_token if token is None else token + sg_token

    shard_grads = {}
    for red in reducers:
        if red.stage == 4:
            shard_grads.update(red.result)
    big_out = {}

    def update_big(names, tok):
        for nm, w, m, v, _ in big:
            if nm in names:
                gl = [shard_grads[(nm, layer)] for layer in range(w.shape[0])]
                delta, new_m, new_v = adamw(f"adamw_{nm}", w, gl, m, v, token=tok)
                big_out[nm] = (jnp.stack(gl, axis=0), delta, new_m, new_v)

    last_group = ("mix_w_in", "mix_w_out")
    update_big([nm for nm, _, _, _, _ in big if nm not in last_group], token)
    token = hook(big_out["ffn_w_up"][1])

    gathered_sg = copies_wait("small_grads_wait", sg_bufs, sg_send, sg_recv, all_plan, 0, big_out["ffn_w_down"][1])[0]
    total = sum_blocks("sum_small", gathered_sg.reshape(8 * packed.shape[0], LANES), 8)
    unpacked = _unpack(total, [g.shape for g in full_grads] + [loss_part.shape])
    loss = unpacked[-1][0, 0]
    local_grads = []
    for (nm, w, _, _, shard_axis), g in zip(small, unpacked[:-1]):
        if shard_axis is not None:
            width = w.shape[shard_axis]
            g = lax.dynamic_slice_in_dim(g, shard_idx * width, width, axis=shard_axis)
        local_grads.append(g.reshape(w.shape))
    shapes = [w.shape for _, w, _, _, _ in small]
    pg = _pack(local_grads)
    pw = _pack([w for _, w, _, _, _ in small])
    pm = _pack([m for _, _, m, _, _ in small])
    pv = _pack([v for _, _, _, v, _ in small])
    delta_s, new_m_s, new_v_s = adamw("adamw_small", pw[None], [pg], pm[None], pv[None], token=token)
    hook(delta_s)
    for red in reducers:
        shard_grads.update(red.result)
    update_big(last_group, None)
    small_out = {}
    for (nm, _, _, _, _), g, d_, m_, v_ in zip(
            small, local_grads, _unpack(delta_s[0], shapes), _unpack(new_m_s[0], shapes), _unpack(new_v_s[0], shapes)):
        small_out[nm] = (g, d_, m_, v_)

    order = ["mix_w_in", "pool_w", "pool_scale", "conv_dw_w", "conv_dw_b", "conv_ln_g", "conv_ln_b", "mix_w_out",
             "attn_w_qkv", "attn_rel_bias", "attn_w_o", "ln_mix_g", "ln_mix_b", "ffn_w_up", "ffn_dw_w", "ffn_dw_b",
             "ffn_w_down", "ple_w_proj", "ple_w_gate", "ple_b_gate", "ln_ffn_g", "ln_ffn_b"]
    res = {**big_out, **small_out}
    outs = [loss, grad_x[None]]
    for slot in range(4):
        outs += [res[nm][slot] for nm in order]
    return tuple(outs)
```

```python
import functools
import math

import jax
import jax.numpy as jnp
from jax import lax
from jax.experimental import pallas as pl
from jax.experimental.pallas import tpu as pltpu

F32 = jnp.float32
BF16 = jnp.bfloat16
MESH = pl.DeviceIdType.MESH

N_LAYERS = 2
ALPHA = (2 * N_LAYERS) ** 0.25
LN_EPS = 1e-5
NEG_INF = -1e30
CHUNK = 64
LEFT_CHUNKS = 8
PAD_ROWS = LEFT_CHUNKS * CHUNK
HEAD_DIM = 64
N_HEADS = 16
MAX_REL = 256
POOL_WINDOWS = (2, 4, 8, 16)
POOL_GROUP = 128
CONV_K = 31
FFN_K = 3
CONV_HALO = 32
FFN_HALO = 8
FFN_TILE = 256
FFN_CHUNK_ROWS = 32
FFN_CHUNK_LANES = 256
Q_TILE = 256
K_WIN = Q_TILE + PAD_ROWS
LANES = 128
SUBLANES = 8
ATTN_PAIRS = 2
ATTN_LANES = ATTN_PAIRS * LANES
SHEAR_W = Q_TILE + K_WIN
SHEAR_SAT = SHEAR_W - 2 * MAX_REL
N_SHARD = 4

ADAM_LR = 0.001
ADAM_B1 = 0.9
ADAM_B2 = 0.999
ADAM_EPS = 1e-08
ADAM_WD = 0.01
ADAM_STEP = 10
ADAM_BC1 = 1.0 - ADAM_B1 ** ADAM_STEP
ADAM_BC2 = 1.0 - ADAM_B2 ** ADAM_STEP

DIMS = {
    "nn": (((1,), (0,)), ((), ())),
    "nt": (((1,), (1,)), ((), ())),
    "tn": (((0,), (0,)), ((), ())),
}


def _cp(vmem_mb=48, **kw):
    return pltpu.CompilerParams(vmem_limit_bytes=vmem_mb * 1024 * 1024, **kw)


def _in_hbm(a):
    return pltpu.with_memory_space_constraint(a, pltpu.HBM)


STAGING_LIMIT_BYTES = 1 << 20
SUM_BLOCK_ROWS = 2048
SMALL_BLOCK_BYTES = 1 << 19


def _call(body, **kw):
    call = pl.pallas_call(body, **kw)

    def run(*args):
        pinned = []
        for a in args:
            big = a.size * a.dtype.itemsize >= STAGING_LIMIT_BYTES
            pinned.append(_in_hbm(a) if big and not jnp.issubdtype(a.dtype, jnp.integer) else a)
        return call(*pinned)

    return run


def _dot(a, b, mode):
    return lax.dot_general(a.astype(BF16), b.astype(BF16), DIMS[mode], preferred_element_type=F32)


def _sig(x):
    return 1.0 / (1.0 + jnp.exp(-x))


def _row_tile(s):
    return min(512, s // 4)


def _mm_tile(s):
    return min(1024, s // 4)


def _mm(name, mode, a, b, in_specs, out_shape, out_spec, acc_shape, grid, nk, zero_first=False, vmem_mb=48,
        addend=None):
    out_f32 = out_shape.dtype == F32

    def body(a_ref, b_ref, *rest):
        k = pl.program_id(2)
        if addend is None:
            o_ref, scr = rest[0], rest[1:]
        else:
            add_ref, o_ref, scr = rest[0], rest[1], rest[2:]

        def compute():
            part = _dot(a_ref[...], b_ref[...], mode)
            if nk == 1:
                if addend is not None:
                    part = part + addend[0] * add_ref[...]
                o_ref[...] = part.astype(o_ref.dtype)
                return
            acc = o_ref if out_f32 else scr[0]

            @pl.when(k == 0)
            def _():
                acc[...] = part if addend is None else part + addend[0] * add_ref[...]

            @pl.when(k > 0)
            def _():
                acc[...] += part

            if not out_f32:
                @pl.when(k == nk - 1)
                def _():
                    o_ref[...] = acc[...].astype(o_ref.dtype)

        if zero_first:
            @pl.when(pl.program_id(1) == 0)
            def _():
                o_ref[...] = jnp.zeros(o_ref.shape, o_ref.dtype)

            pl.when(pl.program_id(1) > 0)(compute)
        else:
            compute()

    scratch = [] if (nk == 1 or out_f32) else [pltpu.VMEM(acc_shape, F32)]
    operands = [a, b] if addend is None else [a, b, addend[1]]
    specs = list(in_specs) if addend is None else list(in_specs) + [out_spec]
    return _call(
        body, name=name, grid=grid, in_specs=specs, out_specs=out_spec, out_shape=out_shape,
        scratch_shapes=scratch, compiler_params=_cp(vmem_mb),
    )(*operands)


def mm_cols_fwd(name, a, wc, out_dtype, pad_blocks=0, part=(0, 1)):
    s, k = a.shape
    s //= part[1]
    n4 = wc.shape[2]
    tm = _row_tile(s) if pad_blocks else _mm_tile(s)
    nt = s // tm
    first_block = part[0] * nt
    return _mm(
        name, "nn", a, wc,
        [pl.BlockSpec((tm, k), lambda j, i, r: (first_block + jnp.maximum(i - pad_blocks, 0), 0)),
         pl.BlockSpec((None, k, n4), lambda j, i, r: (j, 0, 0))],
        jax.ShapeDtypeStruct((s + pad_blocks * tm, N_SHARD * n4), out_dtype),
        pl.BlockSpec((tm, n4), lambda j, i, r: (i, j)),
        None, (N_SHARD, nt + pad_blocks, 1), 1, zero_first=pad_blocks > 0)


def mm_cols_dx(name, dy, wc, addend=None):
    s = dy.shape[0]
    _, k, n4 = wc.shape
    tm = _mm_tile(s)
    return _mm(
        name, "nt", dy, wc,
        [pl.BlockSpec((tm, n4), lambda g, i, r: (i, r)),
         pl.BlockSpec((None, k, n4), lambda g, i, r: (r, 0, 0))],
        jax.ShapeDtypeStruct((s, k), F32),
        pl.BlockSpec((tm, k), lambda g, i, r: (i, 0)),
        (tm, k), (1, s // tm, N_SHARD), N_SHARD, addend=addend)


def mm_cols_dw(name, a, dy, part=(0, 1)):
    s, k = a.shape
    s //= part[1]
    n4 = dy.shape[1] // N_SHARD
    tm = _mm_tile(s)
    nt = s // tm
    first_block = part[0] * nt
    return _mm(
        name, "tn", a, dy,
        [pl.BlockSpec((tm, k), lambda j, g, r: (first_block + r, 0)),
         pl.BlockSpec((tm, n4), lambda j, g, r: (r, j))],
        jax.ShapeDtypeStruct((N_SHARD, k, n4), F32),
        pl.BlockSpec((None, k, n4), lambda j, g, r: (j, 0, 0)),
        (k, n4), (N_SHARD, 1, nt), nt)


def _k_tile(k):
    return k if k <= 1024 else k // 2


def mm_rows_fwd(name, a, wr, out_dtype=F32):
    s, k = a.shape
    n = wr.shape[1]
    tm = _mm_tile(s)
    tk = _k_tile(k)
    nk = k // tk
    return _mm(
        name, "nn", a, wr,
        [pl.BlockSpec((tm, tk), lambda g, i, r: (i, r)),
         pl.BlockSpec((tk, n), lambda g, i, r: (r, 0))],
        jax.ShapeDtypeStruct((s, n), out_dtype),
        pl.BlockSpec((tm, n), lambda g, i, r: (i, 0)),
        (tm, n), (1, s // tm, nk), nk)


def mm_rows_dx(name, dy, wr, out_dtype=F32):
    s, n = dy.shape
    k = wr.shape[0]
    tm = _mm_tile(s)
    tk = _k_tile(k)
    return _mm(
        name, "nt", dy, wr,
        [pl.BlockSpec((tm, n), lambda j, i, r: (i, 0)),
         pl.BlockSpec((tk, n), lambda j, i, r: (j, 0))],
        jax.ShapeDtypeStruct((s, k), out_dtype),
        pl.BlockSpec((tm, tk), lambda j, i, r: (i, j)),
        None, (k // tk, s // tm, 1), 1)


def mm_rows_dw(name, a, dy):
    s, k = a.shape
    n = dy.shape[1]
    tm = _mm_tile(s)
    tk = _k_tile(k)
    nt = s // tm
    return _mm(
        name, "tn", a, dy,
        [pl.BlockSpec((tm, tk), lambda j, g, r: (r, j)),
         pl.BlockSpec((tm, n), lambda j, g, r: (r, 0))],
        jax.ShapeDtypeStruct((k, n), F32),
        pl.BlockSpec((tk, n), lambda j, g, r: (j, 0)),
        (tk, n), (k // tk, 1, nt), nt)


def _row(tm, c, col=0):
    return pl.BlockSpec((tm, c), lambda i: (i, col))


def _full(shape):
    nd = len(shape)
    return pl.BlockSpec(shape, lambda i: (0,) * nd)


def _prev(tm, h, c, col=0):
    return pl.BlockSpec((h, c), lambda i: (jnp.maximum(i * (tm // h) - 1, 0), col))


def _next(tm, h, c, s, col=0):
    return pl.BlockSpec((h, c), lambda i: (jnp.minimum((i + 1) * (tm // h), s // h - 1), col))


def _acc_add(ref, first, val):
    @pl.when(first)
    def _():
        ref[...] = val

    @pl.when(jnp.logical_not(first))
    def _():
        ref[...] += val


def _colsum(v):
    return jnp.sum(v, axis=0, keepdims=True)


def _ln_stats(z):
    mu = jnp.mean(z, axis=-1, keepdims=True)
    zc = z - mu
    var = jnp.mean(zc * zc, axis=-1, keepdims=True)
    rstd = lax.rsqrt(var + LN_EPS)
    return zc * rstd, rstd


def _ln_bwd(dxhat, xhat, rstd):
    m1 = jnp.mean(dxhat, axis=-1, keepdims=True)
    m2 = jnp.mean(dxhat * xhat, axis=-1, keepdims=True)
    return rstd * (dxhat - m1 - xhat * m2)


def ln_fwd(name, x, f, g, b, ple=None, emit_y=True):
    s, d = x.shape
    tm = _row_tile(s)
    n_in = 2 + (3 if ple is not None else 0)

    def body(*refs):
        x_ref, f_ref = refs[0], refs[1]
        g_ref, b_ref = refs[n_in], refs[n_in + 1]
        xh_ref, rs_ref = refs[-2:]
        z = ALPHA * x_ref[...] + f_ref[...]
        if ple is not None:
            pgl_ref, pp_ref, bg_ref = refs[2:5]
            z = z + _sig(pgl_ref[...] + bg_ref[...]) * pp_ref[...]
        xhat, rstd = _ln_stats(z)
        if emit_y:
            y = xhat * g_ref[...] + b_ref[...]
            refs[n_in + 2][...] = y
            refs[n_in + 3][...] = y.astype(BF16)
        xh_ref[...] = xhat
        rs_ref[...] = jnp.broadcast_to(rstd, rs_ref.shape)

    ins = [x, f]
    specs = [_row(tm, d), _row(tm, d)]
    if ple is not None:
        pgl, pp, bg = ple
        ins += [pgl, pp, bg]
        specs += [_row(tm, d), _row(tm, d), _full((1, d))]
    ins += [g, b]
    specs += [_full((1, d)), _full((1, d))]
    y_shapes = [jax.ShapeDtypeStruct((s, d), F32), jax.ShapeDtypeStruct((s, d), BF16)] if emit_y else []
    outs = _call(
        body, name=name, grid=(s // tm,), in_specs=specs,
        out_specs=[_row(tm, d)] * (len(y_shapes) + 1) + [_row(tm, LANES)],
        out_shape=y_shapes + [jax.ShapeDtypeStruct((s, d), F32), jax.ShapeDtypeStruct((s, LANES), F32)],
        compiler_params=_cp(),
    )(*ins)
    return tuple(outs) if emit_y else (None, None, outs[0], outs[1])


def ln_bwd(name, parts, xhat, rstd, g, ple=None, loss=None):
    s, d = xhat.shape
    tm = _row_tile(s)
    coefs = [c for c, _ in parts]
    n_p = len(parts)
    n_ple = 3 if ple is not None else 0
    n_in = n_p + 3 + n_ple + (2 if loss is not None else 0)

    def body(*refs):
        first = pl.program_id(0) == 0
        xh = refs[n_p][...]
        rs = refs[n_p + 1][:, 0:1]
        g_v = refs[n_p + 2][...]
        outs = refs[n_in:]
        if loss is not None:
            t_ref, b_ref = refs[n_p + 3 + n_ple:n_p + 5 + n_ple]
            err = (xh * g_v + b_ref[...]) - t_ref[...]
            dy = err * (1.0 / d)
            part = 0.5 * jnp.sum(jnp.mean(err * err, axis=-1, keepdims=True), axis=0, keepdims=True)
            _acc_add(outs[-1], first, jnp.broadcast_to(part, outs[-1].shape))
        else:
            dy = coefs[0] * refs[0][...].astype(F32)
            for j in range(1, n_p):
                dy = dy + coefs[j] * refs[j][...].astype(F32)
        dz = _ln_bwd(dy * g_v, xh, rs)
        outs[0][...] = dz
        _acc_add(outs[1], first, _colsum(dy * xh))
        _acc_add(outs[2], first, _colsum(dy))
        if ple is not None:
            pgl_ref, pp_ref, bg_ref = refs[n_p + 3:n_p + 6]
            pg = _sig(pgl_ref[...] + bg_ref[...])
            dpgl = dz * pp_ref[...] * pg * (1.0 - pg)
            outs[3][...] = (dz * pg).astype(BF16)
            outs[4][...] = dpgl.astype(BF16)
            _acc_add(outs[5], first, _colsum(dpgl))

    ins = [p for _, p in parts] + [xhat, rstd, g]
    specs = [_row(tm, d)] * n_p + [_row(tm, d), _row(tm, LANES), _full((1, d))]
    out_specs = [_row(tm, d), _full((1, d)), _full((1, d))]
    out_shape = [jax.ShapeDtypeStruct((s, d), F32), jax.ShapeDtypeStruct((1, d), F32),
                 jax.ShapeDtypeStruct((1, d), F32)]
    if ple is not None:
        pgl, pp, bg = ple
        ins += [pgl, pp, bg]
        specs += [_row(tm, d), _row(tm, d), _full((1, d))]
        out_specs += [_row(tm, d), _row(tm, d), _full((1, d))]
        out_shape += [jax.ShapeDtypeStruct((s, d), BF16), jax.ShapeDtypeStruct((s, d), BF16),
                      jax.ShapeDtypeStruct((1, d), F32)]
    if loss is not None:
        target, b = loss
        ins += [target, b]
        specs += [_row(tm, d), _full((1, d))]
        out_specs += [_full((8, LANES))]
        out_shape += [jax.ShapeDtypeStruct((8, LANES), F32)]
    return _call(
        body, name=name, grid=(s // tm,), in_specs=specs, out_specs=out_specs, out_shape=out_shape,
        compiler_params=_cp(),
    )(*ins)


def _fill_rotations(rot_ref, x, direction):
    n = x.shape[0]
    rot_ref[0] = x
    for b in range(1, SUBLANES):
        if direction < 0:
            rot_ref[b, SUBLANES:n, :] = x[SUBLANES - b:n - b]
        else:
            rot_ref[b, 0:n - SUBLANES, :] = x[b:n - SUBLANES + b]


def _rotated(rot_ref, start, rows, cs, direction=-1):
    b = (-start) % SUBLANES if direction < 0 else start % SUBLANES
    aligned = start + b if direction < 0 else start - b
    return rot_ref[b, pl.ds(aligned, rows), cs]


def _tile_pos(i, tm, rows):
    return (i * tm + lax.broadcasted_iota(jnp.int32, (rows, 1), 0) + 1).astype(F32)


def mixer_fwd(name, u, pool_w, pool_scale, conv_w, conv_b, cn_g, cn_b):
    s = u.shape[0]
    dp = 512
    tm = min(256, s // 4)
    h = CONV_HALO

    def body(a_c, a_p, bv_c, bv_p, bg_c, bg_p, pw_ref, ps_ref, cw_ref, cb_ref, cg_ref, cbt_ref,
             cat_ref, d_ref, e_ref, glu_ref, hh_ref, rs_ref, ext_a, rot_g, conv_out):
        i = pl.program_id(0)
        first = i == 0
        ext_a[0:h, :] = jnp.where(first, 0.0, a_p[...])
        ext_a[h:, :] = a_c[...]
        glu = bv_c[...] * _sig(bg_c[...])
        glu_ref[...] = glu
        _fill_rotations(rot_g, jnp.concatenate([jnp.where(first, 0.0, bv_p[...] * _sig(bg_p[...])), glu], axis=0), -1)
        pos = _tile_pos(i, tm, tm)
        for gi, w in enumerate(POOL_WINDOWS):
            cs = slice(gi * POOL_GROUP, (gi + 1) * POOL_GROUP)
            a_g = ext_a[pl.ds(h, tm), cs]
            acc = a_g
            for sh in range(1, w):
                acc = acc + ext_a[pl.ds(h - sh, tm), cs]
            d_g = acc / jnp.minimum(pos, float(w)) - a_g
            d_ref[:, cs] = d_g.astype(BF16)
            e_g = _dot(d_g, pw_ref[gi], "nn")
            e_ref[:, cs] = e_g
            cat_ref[:, cs] = (e_g * ps_ref[:, cs]).astype(BF16)
        for lg in range(dp // LANES):
            cs = slice(lg * LANES, (lg + 1) * LANES)
            acc = jnp.broadcast_to(cb_ref[:, cs], (tm, LANES))
            for sh in range(CONV_K):
                acc = acc + _rotated(rot_g, h - sh, tm, cs) * cw_ref[pl.ds(CONV_K - 1 - sh, 1), cs]
            conv_out[:, cs] = acc
        hhat, rstd = _ln_stats(conv_out[...])
        hl = hhat * cg_ref[...] + cbt_ref[...]
        cat_ref[:, dp:] = (hl * _sig(hl)).astype(BF16)
        hh_ref[...] = hhat
        rs_ref[...] = jnp.broadcast_to(rstd, rs_ref.shape)

    specs = [_row(tm, dp, 0), _prev(tm, h, dp, 0), _row(tm, dp, 1), _prev(tm, h, dp, 1),
             _row(tm, dp, 2), _prev(tm, h, dp, 2),
             _full((4, POOL_GROUP, POOL_GROUP)), _full((1, dp)), _full((CONV_K, dp)),
             _full((1, dp)), _full((1, dp)), _full((1, dp))]
    out_specs = [_row(tm, 2 * dp), _row(tm, dp), _row(tm, dp), _row(tm, dp), _row(tm, dp), _row(tm, LANES)]
    out_shape = [jax.ShapeDtypeStruct((s, 2 * dp), BF16), jax.ShapeDtypeStruct((s, dp), BF16),
                 jax.ShapeDtypeStruct((s, dp), F32), jax.ShapeDtypeStruct((s, dp), F32),
                 jax.ShapeDtypeStruct((s, dp), F32), jax.ShapeDtypeStruct((s, LANES), F32)]
    return _call(
        body, name=name, grid=(s // tm,), in_specs=specs, out_specs=out_specs, out_shape=out_shape,
        scratch_shapes=[pltpu.VMEM((h + tm, dp), F32), pltpu.VMEM((SUBLANES, h + tm, dp), F32),
                        pltpu.VMEM((tm, dp), F32)],
        compiler_params=_cp(),
    )(u, u, u, u, u, u, pool_w, pool_scale, conv_w, conv_b, cn_g, cn_b)


def mixer_bwd(name, dcat, u, d_sv, e_sv, glu_sv, hh_sv, rs_sv, pool_w, pool_scale, conv_w, cn_g, cn_b):
    s = u.shape[0]
    dp = 512
    tm = min(256, s // 4)
    h = CONV_HALO
    nt = s // tm

    def body(dc_c, dc_n, bv_c, bg_c, d_c, e_c, gl_c, gl_p, hh_c, hh_n, rs_c, rs_n,
             pw_ref, ps_ref, cw_ref, cg_ref, cbt_ref,
             du_ref, dpw_ref, dps_ref, dcw_ref, dcb_ref, dcg_ref, dcbt_ref,
             ext_dh, ext_g, ext_r):
        i = pl.program_id(0)
        first = i == 0
        last = i == nt - 1
        cg = cg_ref[...]

        def conv_grads(dyb, hhat, rstd):
            hl = hhat * cg + cbt_ref[...]
            sg = _sig(hl)
            dhl = dyb * (sg * (1.0 + hl * (1.0 - sg)))
            return _ln_bwd(dhl * cg, hhat, rstd), dhl

        hh_cur = hh_c[...]
        dh_c, dhl_c = conv_grads(dc_c[:, dp:], hh_cur, rs_c[:, 0:1])
        dh_n, _ = conv_grads(dc_n[:, dp:], hh_n[...], rs_n[:, 0:1])
        _fill_rotations(ext_dh, jnp.concatenate([dh_c, jnp.where(last, 0.0, dh_n)], axis=0), 1)
        _fill_rotations(ext_g, jnp.concatenate([jnp.where(first, 0.0, gl_p[...]), gl_c[...]], axis=0), -1)

        @pl.when(first)
        def _():
            dcw_ref[...] = jnp.zeros(dcw_ref.shape, F32)

        for lg in range(dp // LANES):
            cs = slice(lg * LANES, (lg + 1) * LANES)
            dglu = jnp.zeros((tm, LANES), F32)
            for sh in range(CONV_K):
                dglu = dglu + _rotated(ext_dh, sh, tm, cs, 1) * cw_ref[pl.ds(CONV_K - 1 - sh, 1), cs]
            dh_g = ext_dh[0, pl.ds(0, tm), cs]
            for sh in range(CONV_K):
                dcw_ref[pl.ds(CONV_K - 1 - sh, 1), cs] += _colsum(dh_g * _rotated(ext_g, h - sh, tm, cs))
            sgate = _sig(bg_c[:, cs])
            du_ref[:, dp + lg * LANES:dp + (lg + 1) * LANES] = dglu * sgate
            du_ref[:, 2 * dp + lg * LANES:2 * dp + (lg + 1) * LANES] = dglu * bv_c[:, cs] * sgate * (1.0 - sgate)
        _acc_add(dcb_ref, first, _colsum(dh_c))
        _acc_add(dcg_ref, first, _colsum(dhl_c * hh_cur))
        _acc_add(dcbt_ref, first, _colsum(dhl_c))

        pos_c = _tile_pos(i, tm, tm)
        pos_n = _tile_pos(i + 1, tm, h)
        _acc_add(dps_ref, first, _colsum(dc_c[:, :dp] * e_c[...]))
        for gi, w in enumerate(POOL_WINDOWS):
            cs = slice(gi * POOL_GROUP, (gi + 1) * POOL_GROUP)
            pw = pw_ref[gi]
            de_c = dc_c[:, cs] * ps_ref[:, cs]
            de_n = dc_n[:, cs] * ps_ref[:, cs]
            dd_c = _dot(de_c, pw, "nt")
            dd_n = _dot(de_n, pw, "nt")
            ext_r[0:tm, :] = dd_c / jnp.minimum(pos_c, float(w))
            ext_r[tm:, :] = jnp.where(last, 0.0, dd_n / jnp.minimum(pos_n, float(w)))
            acc = -dd_c
            for sh in range(w):
                acc = acc + ext_r[pl.ds(sh, tm), :]
            du_ref[:, cs] = acc
            dpw_g = _dot(d_c[:, cs], de_c, "tn")

            @pl.when(first)
            def _():
                dpw_ref[gi] = dpw_g

            @pl.when(jnp.logical_not(first))
            def _():
                dpw_ref[gi] += dpw_g

    specs = [_row(tm, 2 * dp), _next(tm, h, 2 * dp, s), _row(tm, dp, 1), _row(tm, dp, 2),
             _row(tm, dp), _row(tm, dp), _row(tm, dp), _prev(tm, h, dp),
             _row(tm, dp), _next(tm, h, dp, s), _row(tm, LANES), _next(tm, h, LANES, s),
             _full((4, POOL_GROUP, POOL_GROUP)), _full((1, dp)), _full((CONV_K, dp)),
             _full((1, dp)), _full((1, dp))]
    out_specs = [_row(tm, 3 * dp), _full((4, POOL_GROUP, POOL_GROUP)), _full((1, dp)), _full((CONV_K, dp)),
                 _full((1, dp)), _full((1, dp)), _full((1, dp))]
    out_shape = [jax.ShapeDtypeStruct((s, 3 * dp), F32),
                 jax.ShapeDtypeStruct((4, POOL_GROUP, POOL_GROUP), F32), jax.ShapeDtypeStruct((1, dp), F32),
                 jax.ShapeDtypeStruct((CONV_K, dp), F32), jax.ShapeDtypeStruct((1, dp), F32),
                 jax.ShapeDtypeStruct((1, dp), F32), jax.ShapeDtypeStruct((1, dp), F32)]
    return _call(
        body, name=name, grid=(nt,), in_specs=specs, out_specs=out_specs, out_shape=out_shape,
        scratch_shapes=[pltpu.VMEM((SUBLANES, tm + h, dp), F32), pltpu.VMEM((SUBLANES, h + tm, dp), F32),
                        pltpu.VMEM((tm + h, POOL_GROUP), F32)],
        compiler_params=_cp(),
    )(dcat, dcat, u, u, d_sv, e_sv, glu_sv, glu_sv, hh_sv, hh_sv, rs_sv, rs_sv,
      pool_w, pool_scale, conv_w, cn_g, cn_b)


GELU_C = math.sqrt(2.0 / math.pi)


def _gelu_parts(x):
    x2 = x * x
    t = jnp.tanh(x * (GELU_C + (GELU_C * 0.044715) * x2))
    half_1pt = 0.5 + 0.5 * t
    gelu = x * half_1pt
    dgelu = half_1pt + (0.5 * x) * (1.0 - t * t) * (GELU_C + (3.0 * GELU_C * 0.044715) * x2)
    return gelu, dgelu


def ffn_act_fwd(name, gv, dw_w, dw_b):
    s = gv.shape[0]
    dff = gv.shape[1] // 2
    tm = min(FFN_TILE, s // 4)
    h = FFN_HALO
    rc = FFN_CHUNK_ROWS
    lw = FFN_CHUNK_LANES

    def body(g_c, g_p, v_c, w_ref, b_ref, hid_ref):
        first = pl.program_id(0) == 0

        def chunk(ci, carry):
            r0 = pl.multiple_of(ci * rc, rc)
            above = pl.multiple_of(jnp.maximum(r0 - h, 0), h)
            for lg in range(dff // lw):
                cs = slice(lg * lw, (lg + 1) * lw)
                top = jnp.where(ci == 0, jnp.where(first, 0.0, g_p[:, cs]), g_c[pl.ds(above, h), cs])
                win = jnp.concatenate([top, g_c[pl.ds(r0, rc), cs]], axis=0)
                gc = jnp.broadcast_to(b_ref[:, cs], (rc, lw))
                for sh in range(FFN_K):
                    gc = gc + win[h - sh:h - sh + rc] * w_ref[pl.ds(FFN_K - 1 - sh, 1), cs]
                gelu, _ = _gelu_parts(gc)
                hid_ref[pl.ds(r0, rc), cs] = (gelu * v_c[pl.ds(r0, rc), cs]).astype(BF16)
            return carry

        lax.fori_loop(0, tm // rc, chunk, 0)

    return _call(
        body, name=name, grid=(s // tm,),
        in_specs=[_row(tm, dff, 0), _prev(tm, h, dff, 0), _row(tm, dff, 1), _full((FFN_K, dff)), _full((1, dff))],
        out_specs=_row(tm, dff), out_shape=jax.ShapeDtypeStruct((s, dff), BF16),
        compiler_params=_cp(),
    )(gv, gv, gv, dw_w, dw_b)


def ffn_act_bwd(name, dhid, gv, dw_w, dw_b):
    s = gv.shape[0]
    dff = gv.shape[1] // 2
    tm = min(FFN_TILE, s // 4)
    h = FFN_HALO
    nt = s // tm
    rc = FFN_CHUNK_ROWS
    lw = FFN_CHUNK_LANES
    n_chunks = tm // rc

    def body(dh_c, dh_n, g_p, g_c, g_n, v_c, v_n, w_ref, b_ref, dgv_ref, dw_ref, db_ref):
        i = pl.program_id(0)
        first = i == 0
        last = i == nt - 1

        @pl.when(first)
        def _():
            dw_ref[...] = jnp.zeros(dw_ref.shape, F32)
            db_ref[...] = jnp.zeros(db_ref.shape, F32)

        def chunk(ci, carry):
            r0 = pl.multiple_of(ci * rc, rc)
            above = pl.multiple_of(jnp.maximum(r0 - h, 0), h)
            below = pl.multiple_of(jnp.minimum(r0 + rc, tm - h), h)
            at_end = ci == n_chunks - 1
            for lg in range(dff // lw):
                cs = slice(lg * lw, (lg + 1) * lw)
                top = jnp.where(ci == 0, jnp.where(first, 0.0, g_p[:, cs]), g_c[pl.ds(above, h), cs])
                bot = jnp.where(at_end, g_n[:, cs], g_c[pl.ds(below, h), cs])
                win = jnp.concatenate([top, g_c[pl.ds(r0, rc), cs], bot], axis=0)
                shifted = [win[h - sh:h - sh + rc + h] for sh in range(FFN_K)]
                gc = jnp.broadcast_to(b_ref[:, cs], (rc + h, lw))
                for sh in range(FFN_K):
                    gc = gc + shifted[sh] * w_ref[pl.ds(FFN_K - 1 - sh, 1), cs]
                gelu, dgelu = _gelu_parts(gc)
                dh_mid = dh_c[pl.ds(r0, rc), cs]
                hv_bot = jnp.where(at_end, jnp.where(last, 0.0, dh_n[:, cs] * v_n[:, cs]),
                                   dh_c[pl.ds(below, h), cs] * v_c[pl.ds(below, h), cs])
                dgc = jnp.concatenate([dh_mid * v_c[pl.ds(r0, rc), cs], hv_bot], axis=0) * dgelu
                dgate = jnp.zeros((rc, lw), F32)
                for sh in range(FFN_K):
                    dgate = dgate + dgc[sh:sh + rc] * w_ref[pl.ds(FFN_K - 1 - sh, 1), cs]
                dgv_ref[pl.ds(r0, rc), cs] = dgate.astype(BF16)
                dgv_ref[pl.ds(r0, rc), slice(dff + lg * lw, dff + (lg + 1) * lw)] = (dh_mid * gelu[0:rc]).astype(BF16)
                dgc_mid = dgc[0:rc]
                for sh in range(FFN_K):
                    dw_ref[pl.ds(FFN_K - 1 - sh, 1), cs] += _colsum(dgc_mid * shifted[sh][0:rc])
                db_ref[:, cs] += _colsum(dgc_mid)
            return carry

        lax.fori_loop(0, n_chunks, chunk, 0)

    return _call(
        body, name=name, grid=(nt,),
        in_specs=[_row(tm, dff), _next(tm, h, dff, s),
                  _prev(tm, h, dff, 0), _row(tm, dff, 0), _next(tm, h, dff, s, 0),
                  _row(tm, dff, 1), _next(tm, h, dff, s, 1),
                  _full((FFN_K, dff)), _full((1, dff))],
        out_specs=[_row(tm, 2 * dff), _full((FFN_K, dff)), _full((1, dff))],
        out_shape=[jax.ShapeDtypeStruct((s, 2 * dff), BF16), jax.ShapeDtypeStruct((FFN_K, dff), F32),
                   jax.ShapeDtypeStruct((1, dff), F32)],
        compiler_params=_cp(),
    )(dhid, dhid, gv, gv, gv, gv, gv, dw_w, dw_b)


def _bias_line(rel_bias):
    nh = rel_bias.shape[0]
    line = jnp.concatenate(
        [jnp.zeros((nh, 1), rel_bias.dtype), jnp.broadcast_to(rel_bias[:, 2 * MAX_REL:], (nh, SHEAR_SAT)),
         jnp.flip(rel_bias[:, 1:2 * MAX_REL], axis=1)], axis=1)
    return line[:, None, :]


def bias_tile(name, line):
    nh = line.shape[0]

    def body(l_ref, o_ref):
        x = jnp.broadcast_to(l_ref[...], (Q_TILE, SHEAR_W))
        z = pltpu.roll(x, SHEAR_W - Q_TILE, 1, stride=1, stride_axis=0)
        qc = lax.broadcasted_iota(jnp.int32, (Q_TILE, K_WIN), 0) // CHUNK
        kc = lax.broadcasted_iota(jnp.int32, (Q_TILE, K_WIN), 1) // CHUNK
        o_ref[...] = jnp.where((kc >= qc) & (kc <= qc + LEFT_CHUNKS), z[:, :K_WIN], NEG_INF)

    return _call(
        body, name=name, grid=(nh,), in_specs=[pl.BlockSpec((None, 1, SHEAR_W), lambda hh: (hh, 0, 0))],
        out_specs=pl.BlockSpec((None, Q_TILE, K_WIN), lambda hh: (hh, 0, 0)),
        out_shape=jax.ShapeDtypeStruct((nh, Q_TILE, K_WIN), F32), compiler_params=_cp(),
    )(line)


def _stack_heads(x2):
    lane = lax.broadcasted_iota(jnp.int32, x2.shape, 1)
    zero = jnp.zeros_like(x2)
    return jnp.concatenate([jnp.where(lane < HEAD_DIM, x2, zero), jnp.where(lane < HEAD_DIM, zero, x2)], axis=0)


def _unstack_heads(x_st):
    lane = lax.broadcasted_iota(jnp.int32, (Q_TILE, LANES), 1)
    return jnp.where(lane < HEAD_DIM, x_st[:Q_TILE], x_st[Q_TILE:])


def _attn_probs(q_st, k3, bias_st, t):
    sc = _dot(q_st, k3, "nt") * (HEAD_DIM ** -0.5) + bias_st
    col = lax.broadcasted_iota(jnp.int32, sc.shape, 1)
    sc = jnp.where(col >= PAD_ROWS - t * Q_TILE, sc, NEG_INF)
    m = jnp.max(sc, axis=-1, keepdims=True)
    p = jnp.exp(sc - m)
    return p * (1.0 / jnp.sum(p, axis=-1, keepdims=True))


def _attn_specs(d_model):
    nq = PAD_ROWS // Q_TILE
    groups = d_model // ATTN_LANES
    specs = [pl.BlockSpec((Q_TILE, ATTN_LANES), lambda g, t: (t + nq, g))]
    for which in (1, 2):
        for j in range(K_WIN // Q_TILE):
            specs.append(pl.BlockSpec((Q_TILE, ATTN_LANES), lambda g, t, j=j, which=which: (t + j, which * groups + g)))
    specs.append(pl.BlockSpec((2 * ATTN_PAIRS, Q_TILE, K_WIN), lambda g, t: (g, 0, 0)))
    return specs


def attn_fwd(name, qkvp, bias):
    s = qkvp.shape[0] - PAD_ROWS
    d_model = qkvp.shape[1] // 3
    nw = K_WIN // Q_TILE

    def body(q_ref, *refs):
        k_refs, v_refs, b_ref, o_ref = refs[:nw], refs[nw:2 * nw], refs[2 * nw], refs[2 * nw + 1]
        t = pl.program_id(1)
        for j in range(ATTN_PAIRS):
            ls = slice(j * LANES, (j + 1) * LANES)
            k3 = jnp.concatenate([r[:, ls] for r in k_refs], axis=0)
            v3 = jnp.concatenate([r[:, ls] for r in v_refs], axis=0)
            bias_st = b_ref[2 * j:2 * j + 2].reshape(2 * Q_TILE, K_WIN)
            p = _attn_probs(_stack_heads(q_ref[:, ls]), k3, bias_st, t)
            o_ref[:, ls] = _unstack_heads(_dot(p, v3, "nn")).astype(BF16)

    return _call(
        body, name=name, grid=(d_model // ATTN_LANES, s // Q_TILE),
        in_specs=_attn_specs(d_model), out_specs=pl.BlockSpec((Q_TILE, ATTN_LANES), lambda g, t: (t, g)),
        out_shape=jax.ShapeDtypeStruct((s, d_model), BF16), compiler_params=_cp(),
    )(qkvp, *([qkvp] * (2 * nw)), bias)


def attn_bwd(name, qkvp, bias, do):
    s = qkvp.shape[0] - PAD_ROWS
    d_model = qkvp.shape[1] // 3
    nw = K_WIN // Q_TILE
    nt = s // Q_TILE
    scale = HEAD_DIM ** -0.5

    def body(q_ref, *refs):
        k_refs, v_refs = refs[:nw], refs[nw:2 * nw]
        b_ref, do_ref, dq_ref, dk_ref, dv_ref, ds_ref, dk_acc, dv_acc = refs[2 * nw:]
        t = pl.program_id(1)
        first = t == 0

        @pl.when(first)
        def _():
            dk_acc[...] = jnp.zeros(dk_acc.shape, F32)
            dv_acc[...] = jnp.zeros(dv_acc.shape, F32)

        start = pl.multiple_of(t * Q_TILE, Q_TILE)
        for j in range(ATTN_PAIRS):
            ls = slice(j * LANES, (j + 1) * LANES)
            q_st = _stack_heads(q_ref[:, ls])
            do_st = _stack_heads(do_ref[:, ls])
            k3 = jnp.concatenate([r[:, ls] for r in k_refs], axis=0)
            v3 = jnp.concatenate([r[:, ls] for r in v_refs], axis=0)
            p = _attn_probs(q_st, k3, b_ref[2 * j:2 * j + 2].reshape(2 * Q_TILE, K_WIN), t)
            dp = _dot(do_st, v3, "nt")
            ds = p * (dp - jnp.sum(p * dp, axis=-1, keepdims=True))
            _acc_add(ds_ref.at[2 * j:2 * j + 2], first, ds.reshape(2, Q_TILE, K_WIN))
            dsb = (ds * scale).astype(BF16)
            dq_ref[:, ls] = _unstack_heads(_dot(dsb, k3, "nn")).astype(BF16)
            dk_acc[pl.ds(start, K_WIN), ls] += _dot(dsb, q_st, "tn")
            dv_acc[pl.ds(start, K_WIN), ls] += _dot(p, do_st, "tn")

        @pl.when(t == nt - 1)
        def _():
            dk_ref[...] = dk_acc[pl.ds(PAD_ROWS, s), :].astype(BF16)
            dv_ref[...] = dv_acc[pl.ds(PAD_ROWS, s), :].astype(BF16)

    specs = _attn_specs(d_model) + [pl.BlockSpec((Q_TILE, ATTN_LANES), lambda g, t: (t, g))]
    col_spec = pl.BlockSpec((s, ATTN_LANES), lambda g, t: (0, g))
    return _call(
        body, name=name, grid=(d_model // ATTN_LANES, nt), in_specs=specs,
        out_specs=[pl.BlockSpec((Q_TILE, ATTN_LANES), lambda g, t: (t, g)), col_spec, col_spec,
                   pl.BlockSpec((2 * ATTN_PAIRS, Q_TILE, K_WIN), lambda g, t: (g, 0, 0))],
        out_shape=[jax.ShapeDtypeStruct((s, d_model), BF16)] * 3
        + [jax.ShapeDtypeStruct((N_HEADS, Q_TILE, K_WIN), F32)],
        scratch_shapes=[pltpu.VMEM((PAD_ROWS + s, ATTN_LANES), F32), pltpu.VMEM((PAD_ROWS + s, ATTN_LANES), F32)],
        compiler_params=_cp(),
    )(qkvp, *([qkvp] * (2 * nw)), bias, do)


def bias_grad_reduce(name, ds_sum):
    nh = ds_sum.shape[0]
    width = SHEAR_W + Q_TILE
    first_k = Q_TILE - 1

    def body(x_ref, col_ref, sat_ref):
        x = x_ref[...]
        hi = x.astype(BF16)
        lo = (x - hi.astype(F32)).astype(BF16)
        r = lax.broadcasted_iota(jnp.int32, (Q_TILE, Q_TILE), 0)
        c = lax.broadcasted_iota(jnp.int32, (Q_TILE, Q_TILE), 1)
        exchange = jnp.where(r + c == Q_TILE - 1, 1.0, 0.0).astype(BF16)
        x_rev = _dot(exchange, hi, "nn") + _dot(exchange, lo, "nn")
        zeros = jnp.zeros((Q_TILE, Q_TILE), F32)
        y = pltpu.roll(jnp.concatenate([zeros, x_rev, zeros], axis=1), 0, 1, stride=1, stride_axis=0)
        cols = _colsum(y)
        col_ref[...] = cols
        k = lax.broadcasted_iota(jnp.int32, cols.shape, 1) - first_k
        tot = jnp.sum(jnp.where((k >= 1) & (k <= SHEAR_SAT), cols, 0.0), axis=-1, keepdims=True)
        sat_ref[...] = jnp.broadcast_to(tot, sat_ref.shape)

    return _call(
        body, name=name, grid=(nh,),
        in_specs=[pl.BlockSpec((None, Q_TILE, K_WIN), lambda hh: (hh, 0, 0))],
        out_specs=[pl.BlockSpec((None, 1, width), lambda hh: (hh, 0, 0)),
                   pl.BlockSpec((None, 1, LANES), lambda hh: (hh, 0, 0))],
        out_shape=[jax.ShapeDtypeStruct((nh, 1, width), F32), jax.ShapeDtypeStruct((nh, 1, LANES), F32)],
        compiler_params=_cp(),
    )(ds_sum)


def _ew_rows(r, most=512, cols=None):
    if cols is not None and r * cols * 4 <= SMALL_BLOCK_BYTES:
        return r
    for cand in range(min(most, r) // 16 * 16, 0, -16):
        if r % cand == 0:
            return cand
    return r


def to_bf16(name, a):
    s, d = a.shape
    tm = _row_tile(s)

    def body(a_ref, o_ref):
        o_ref[...] = a_ref[...].astype(BF16)

    return _call(
        body, name=name, grid=(s // tm,), in_specs=[_row(tm, d)], out_specs=_row(tm, d),
        out_shape=jax.ShapeDtypeStruct((s, d), BF16), compiler_params=_cp(),
    )(a)


def cast_into_gathered(name, w, layer, s_idx, n_blocks=N_SHARD, dtype=BF16, token=None):
    r, c = w.shape[-2:]
    tr = _ew_rows(r, cols=c)

    def body(s_ref, w_ref, *rest):
        rest[-1][...] = w_ref[...].astype(dtype)

    extra = [] if token is None else [token]
    grid_spec = pltpu.PrefetchScalarGridSpec(
        num_scalar_prefetch=1, grid=(r // tr,),
        in_specs=[pl.BlockSpec((None, tr, c), lambda i, s_ref: (layer, i, 0))] + [ANY_SPEC] * len(extra),
        out_specs=pl.BlockSpec((None, tr, c), lambda i, s_ref: (s_ref[0], i, 0)))
    return _call(
        body, name=name, grid_spec=grid_spec, out_shape=jax.ShapeDtypeStruct((n_blocks, r, c), dtype),
        compiler_params=_cp(),
    )(s_idx, w, *extra)


def adamw(name, w, grads, m, v, token=None):
    nl, r, c = w.shape
    tr = _ew_rows(r, 256, cols=c)

    def body(*refs):
        w_ref, m_ref, v_ref = refs[0], refs[1], refs[2]
        g_refs = refs[3:3 + nl]
        d_ref, nm_ref, nv_ref = refs[-3:]
        layer = pl.program_id(0)
        g = g_refs[0][...]
        for j in range(1, nl):
            g = jnp.where(layer == j, g_refs[j][...], g)
        nm = ADAM_B1 * m_ref[...] + (1.0 - ADAM_B1) * g
        nv = ADAM_B2 * v_ref[...] + (1.0 - ADAM_B2) * (g * g)
        m_hat = nm / ADAM_BC1
        v_hat = nv / ADAM_BC2
        d_ref[...] = -ADAM_LR * (m_hat / (jnp.sqrt(v_hat) + ADAM_EPS) + ADAM_WD * w_ref[...])
        nm_ref[...] = nm
        nv_ref[...] = nv

    p_spec = pl.BlockSpec((None, tr, c), lambda l, i: (l, i, 0))
    g_spec = pl.BlockSpec((tr, c), lambda l, i: (i, 0))
    extra = [] if token is None else [token]
    extra_specs = [] if token is None else [ANY_SPEC]
    return _call(
        body, name=name, grid=(nl, r // tr), in_specs=[p_spec] * 3 + [g_spec] * nl + extra_specs,
        out_specs=[p_spec] * 3, out_shape=[jax.ShapeDtypeStruct((nl, r, c), F32)] * 3, compiler_params=_cp(),
    )(w, m, v, *grads, *extra)


def sum_blocks(name, gathered, n_blocks):
    r = gathered.shape[0] // n_blocks
    c = gathered.shape[1]
    tr = r if r <= SUM_BLOCK_ROWS else _ew_rows(r)
    nt = r // tr

    def body(*refs):
        acc = refs[0][...]
        for j in range(1, n_blocks):
            acc = acc + refs[j][...]
        refs[-1][...] = acc

    specs = [pl.BlockSpec((tr, c), lambda i, j=j: (j * nt + i, 0)) for j in range(n_blocks)]
    return _call(
        body, name=name, grid=(nt,), in_specs=specs, out_specs=pl.BlockSpec((tr, c), lambda i: (i, 0)),
        out_shape=jax.ShapeDtypeStruct((r, c), F32), compiler_params=_cp(),
    )(*([gathered] * n_blocks))


def _place():
    return lax.axis_index("x"), lax.axis_index("y"), lax.axis_index("c")


def _other_chips(x, y):
    return [(1 - x, y), (x, 1 - y), (1 - x, 1 - y)]


HBM_SPEC = pl.BlockSpec(memory_space=pltpu.HBM)
SEM_SPEC = pl.BlockSpec(memory_space=pltpu.SEMAPHORE)
ANY_SPEC = pl.BlockSpec(memory_space=pl.ANY)
EFFECT = pltpu.SideEffectType.DATAFLOW_SIDE_EFFECTING


def copies_start(name, bufs, plan, n_copies):
    n = len(bufs)

    def body(*refs):
        send, recv = refs[n], refs[n + 1]
        token = refs[2 * n + 2]
        for k, (src, dst, peer, _) in enumerate(plan(refs[:n])):
            pltpu.make_async_remote_copy(
                src_ref=src, dst_ref=dst, send_sem=send.at[k], recv_sem=recv.at[k],
                device_id=peer, device_id_type=MESH).start()
        token[...] = jnp.zeros(token.shape, F32)

    outs = pl.pallas_call(
        body, name=name,
        out_shape=(pltpu.SemaphoreType.DMA((n_copies,)), pltpu.SemaphoreType.DMA((n_copies,)),
                   *[pltpu.HBM(b.shape, b.dtype) for b in bufs], jax.ShapeDtypeStruct((8, LANES), F32)),
        in_specs=[HBM_SPEC] * n,
        out_specs=(SEM_SPEC, SEM_SPEC, *([HBM_SPEC] * n), pl.BlockSpec(memory_space=pltpu.VMEM)),
        input_output_aliases={a: a + 2 for a in range(n)},
        compiler_params=pltpu.CompilerParams(has_side_effects=EFFECT),
    )(*[_in_hbm(b) for b in bufs])
    return outs[0], outs[1], list(outs[2:2 + n]), outs[2 + n]


def copies_wait(name, bufs, send, recv, plan, sem_base, after):
    n = len(bufs)

    def body(*refs):
        send_ref, recv_ref = refs[n], refs[n + 1]
        for k, (src, _, peer, land) in enumerate(plan(refs[:n])):
            cp = pltpu.make_async_remote_copy(
                src_ref=src, dst_ref=land, send_sem=send_ref.at[sem_base + k], recv_sem=recv_ref.at[sem_base + k],
                device_id=peer, device_id_type=MESH)
            cp.wait_send()
            cp.wait_recv()

    outs = pl.pallas_call(
        body, name=name,
        out_shape=tuple(pltpu.HBM(b.shape, b.dtype) for b in bufs),
        in_specs=[HBM_SPEC] * n + [SEM_SPEC, SEM_SPEC, ANY_SPEC], out_specs=tuple([HBM_SPEC] * n),
        input_output_aliases={a: a for a in range(n)},
        compiler_params=pltpu.CompilerParams(has_side_effects=EFFECT),
    )(*bufs, send, recv, after)
    return list(outs)


def gather_plan(refs):
    x, y, c = _place()
    me = 2 * x + y
    return [(buf.at[me], buf.at[me], (cx, cy, c), buf.at[2 * cx + cy])
            for buf in refs for cx, cy in _other_chips(x, y)]


def all_plan(refs):
    x, y, c = _place()
    me = 4 * x + 2 * y + c
    out = []
    for buf in refs:
        for flip in range(1, 8):
            px = 1 - x if flip & 4 else x
            py = 1 - y if flip & 2 else y
            pc = 1 - c if flip & 1 else c
            out.append((buf.at[me], buf.at[me], (px, py, pc), buf.at[4 * px + 2 * py + pc]))
    return out


def swap_plan(refs):
    x, y, c = _place()
    n = len(refs) // 2
    out = []
    for g, land in zip(refs[:n], refs[n:]):
        hr = g.shape[1] // 2
        out.append((g.at[:, pl.ds((1 - c) * hr, hr)], land, (x, y, 1 - c), land))
    return out


def owners_plan(refs):
    x, y, c = _place()
    n = len(refs) // 2
    return [(src.at[2 * cx + cy], land.at[j], (cx, cy, c), land.at[j])
            for src, land in zip(refs[:n], refs[n:]) for j, (cx, cy) in enumerate(_other_chips(x, y))]


def join_plan(refs):
    x, y, c = _place()
    out = []
    for buf in refs:
        hr = buf.shape[0] // 2
        mine = buf.at[pl.ds(c * hr, hr)]
        out.append((mine, mine, (x, y, 1 - c), buf.at[pl.ds((1 - c) * hr, hr)]))
    return out


def add_halves(name, grad, landed, sc_idx):
    _, r, c = grad.shape
    hr = r // 2
    tr = _ew_rows(hr)
    nt = hr // tr

    def body(sc_ref, g_ref, l_ref, own_ref, wire_ref):
        tot = g_ref[...] + l_ref[...]
        wire_ref[...] = tot.astype(BF16)

        @pl.when(pl.program_id(1) == sc_ref[0])
        def _():
            own_ref[...] = tot

    grid_spec = pltpu.PrefetchScalarGridSpec(
        num_scalar_prefetch=1, grid=(nt, N_SHARD),
        in_specs=[pl.BlockSpec((None, tr, c), lambda i, sh, sc_ref: (sh, sc_ref[1] * nt + i, 0)),
                  pl.BlockSpec((None, tr, c), lambda i, sh, sc_ref: (sh, i, 0))],
        out_specs=[pl.BlockSpec((tr, c), lambda i, sh, sc_ref: (i, 0)),
                   pl.BlockSpec((None, tr, c), lambda i, sh, sc_ref: (sh, i, 0))])
    return _call(
        body, name=name, grid_spec=grid_spec,
        out_shape=[jax.ShapeDtypeStruct((hr, c), F32), jax.ShapeDtypeStruct((N_SHARD, hr, c), BF16)],
        compiler_params=_cp(),
    )(sc_idx, grad, landed)


def add_owned(name, own, landed, sc_idx):
    hr, c = own.shape
    tr = _ew_rows(hr)
    nt = hr // tr

    def body(sc_ref, o_ref, l0, l1, l2, out_ref):
        out_ref[...] = ((o_ref[...] + l0[...].astype(F32)) + l1[...].astype(F32)) + l2[...].astype(F32)

    grid_spec = pltpu.PrefetchScalarGridSpec(
        num_scalar_prefetch=1, grid=(nt,),
        in_specs=[pl.BlockSpec((tr, c), lambda i, sc_ref: (i, 0))]
        + [pl.BlockSpec((None, tr, c), lambda i, sc_ref, j=j: (j, i, 0)) for j in range(3)],
        out_specs=pl.BlockSpec((tr, c), lambda i, sc_ref: (sc_ref[1] * nt + i, 0)))
    return _call(
        body, name=name, grid_spec=grid_spec, out_shape=jax.ShapeDtypeStruct((2 * hr, c), F32),
        compiler_params=_cp(),
    )(sc_idx, own, landed, landed, landed)


PACK_QUANTUM = 8 * LANES


def _pack(arrays):
    pieces = []
    for a in arrays:
        flat = a.reshape(-1)
        padded = -(-flat.shape[0] // PACK_QUANTUM) * PACK_QUANTUM
        pieces.append(jnp.pad(flat, (0, padded - flat.shape[0])).reshape(-1, LANES))
    return jnp.concatenate(pieces, axis=0)


def _unpack(packed, shapes):
    out = []
    row = 0
    for shp in shapes:
        size = math.prod(shp)
        rows = -(-size // PACK_QUANTUM) * 8
        out.append(packed[row:row + rows].reshape(-1)[:size].reshape(shp))
        row += rows
    return out


def kernel(x, p, mix_w_in, pool_w, pool_scale, conv_dw_w, conv_dw_b, conv_ln_g, conv_ln_b, mix_w_out, attn_w_qkv, attn_rel_bias, attn_w_o, ln_mix_g, ln_mix_b, ffn_w_up, ffn_dw_w, ffn_dw_b, ffn_w_down, ple_w_proj, ple_w_gate, ple_b_gate, ln_ffn_g, ln_ffn_b, loss_target, m_mix_w_in, m_pool_w, m_pool_scale, m_conv_dw_w, m_conv_dw_b, m_conv_ln_g, m_conv_ln_b, m_mix_w_out, m_attn_w_qkv, m_attn_rel_bias, m_attn_w_o, m_ln_mix_g, m_ln_mix_b, m_ffn_w_up, m_ffn_dw_w, m_ffn_dw_b, m_ffn_w_down, m_ple_w_proj, m_ple_w_gate, m_ple_b_gate, m_ln_ffn_g, m_ln_ffn_b, v_mix_w_in, v_pool_w, v_pool_scale, v_conv_dw_w, v_conv_dw_b, v_conv_ln_g, v_conv_ln_b, v_mix_w_out, v_attn_w_qkv, v_attn_rel_bias, v_attn_w_o, v_ln_mix_g, v_ln_mix_b, v_ffn_w_up, v_ffn_dw_w, v_ffn_dw_b, v_ffn_w_down, v_ple_w_proj, v_ple_w_gate, v_ple_b_gate, v_ln_ffn_g, v_ln_ffn_b):
    xi, yi, ci = _place()
    shard_idx = (2 * xi + yi).astype(jnp.int32)
    s_arr = shard_idx.reshape(1)
    c_arr = ci.astype(jnp.int32).reshape(1)
    sc_arr = jnp.concatenate([s_arr, c_arr])

    x0 = x[0]
    target = loss_target[0]
    p_rows = p.reshape(p.shape[0] * p.shape[2], p.shape[3])
    seq = x0.shape[0]

    big = [
        ("mix_w_in", mix_w_in, m_mix_w_in, v_mix_w_in, True),
        ("mix_w_out", mix_w_out, m_mix_w_out, v_mix_w_out, False),
        ("attn_w_qkv", attn_w_qkv, m_attn_w_qkv, v_attn_w_qkv, True),
        ("attn_w_o", attn_w_o, m_attn_w_o, v_attn_w_o, False),
        ("ffn_w_up", ffn_w_up, m_ffn_w_up, v_ffn_w_up, True),
        ("ffn_w_down", ffn_w_down, m_ffn_w_down, v_ffn_w_down, False),
        ("ple_w_proj", ple_w_proj, m_ple_w_proj, v_ple_w_proj, True),
        ("ple_w_gate", ple_w_gate, m_ple_w_gate, v_ple_w_gate, False),
    ]
    params = {nm: w for nm, w, _, _, _ in big}
    col_sharded = {nm: cs for nm, _, _, _, cs in big}
    keys = [("mix_w_in", 0), ("mix_w_out", 0), ("ffn_w_up", 0), ("ffn_w_down", 0), ("ple_w_gate", 0),
            ("ple_w_proj", 0), ("attn_w_qkv", 0), ("attn_w_o", 0), ("ffn_w_up", 1), ("ffn_w_down", 1),
            ("ple_w_gate", 1), ("ple_w_proj", 1)]
    dw_shapes = [conv_dw_w.shape, ffn_dw_w.shape]
    dw_block = cast_into_gathered("place_dw", _pack([conv_dw_w, ffn_dw_w])[None], 0, s_arr, dtype=F32)
    n_first = 2
    started = {}
    gather_token = None
    for tag, group in (("first", keys[:n_first]), ("rest", keys[n_first:])):
        shards = [cast_into_gathered(f"cast_{nm}_{layer}", params[nm], layer, s_arr, token=gather_token)
                  for nm, layer in group]
        if tag == "first":
            shards.append(dw_block)
        send, recv, bufs, gather_token = copies_start(f"gather_start_{tag}", shards, gather_plan, 3 * len(shards))
        for a, key in enumerate(group):
            started[key] = (send, recv, bufs[a], 3 * a)
        if tag == "first":
            dw_started = (send, recv, bufs[-1], 3 * len(group))
    arrived_w = {}

    def weight(nm, layer, after=None):
        key = (nm, layer)
        if key not in arrived_w:
            send, recv, buf, base = started[key]
            arrived_w[key] = copies_wait(f"gather_wait_{nm}_{layer}", [buf], send, recv, gather_plan, base, after)[0]
        g = arrived_w[key]
        if col_sharded[nm]:
            return g
        return g.reshape(g.shape[0] * g.shape[1], g.shape[2])

    def tie(a, token):
        return a + token[0:1, 0:1].astype(a.dtype)

    class Reducer:
        def __init__(self, tag, group):
            self.tag, self.group, self.stage = tag, group, 0
            self.n = len(group)
            self.result = None

        def advance(self, after):
            tag, n = self.tag, self.n
            if self.stage == 0:
                grads = []
                for key in self.group:
                    g = big_grads[key]
                    grads.append(g if g.ndim == 3 else g.reshape(N_SHARD, g.shape[0] // N_SHARD, g.shape[1]))
                lands = [lax.empty((N_SHARD, g.shape[1] // 2, g.shape[2]), F32) for g in grads]
                self.sems = copies_start(f"swap_start_{tag}", grads + lands, swap_plan, n)
            elif self.stage == 1:
                send, recv, bufs, _ = self.sems
                outs = copies_wait(f"swap_wait_{tag}", bufs, send, recv, swap_plan, 0, after)
                self.own, wire = [], []
                for key, g, ld in zip(self.group, outs[:n], outs[n:]):
                    o, ob = add_halves(f"add_halves_{key[0]}_{key[1]}", g, ld, sc_arr)
                    self.own.append(o)
                    wire.append(ob)
                lands = [lax.empty((3,) + w.shape[1:], BF16) for w in wire]
                self.sems = copies_start(f"owners_start_{tag}", wire + lands, owners_plan, 3 * n)
            elif self.stage == 2:
                send, recv, bufs, _ = self.sems
                outs = copies_wait(f"owners_wait_{tag}", bufs, send, recv, owners_plan, 0, after)
                finals = [add_owned(f"add_owned_{key[0]}_{key[1]}", o, ar, sc_arr)
                          for key, o, ar in zip(self.group, self.own, outs[n:])]
                self.sems = copies_start(f"join_start_{tag}", finals, join_plan, n)
            elif self.stage == 3:
                send, recv, bufs, _ = self.sems
                outs = copies_wait(f"join_wait_{tag}", bufs, send, recv, join_plan, 0, after)
                self.result = dict(zip(self.group, outs))
                self.sems = None
            self.stage += 1
            return None if self.sems is None else self.sems[3]

    dw_cache = []

    def conv_weights(after):
        if not dw_cache:
            send, recv, buf, base = dw_started
            dw_all = copies_wait("gather_wait_dw", [buf], send, recv, gather_plan, base, after)[0]
            dw_parts = [_unpack(dw_all[k], dw_shapes) for k in range(N_SHARD)]
            dw_cache.append(jnp.concatenate([pc[0] for pc in dw_parts], axis=2)[0])
            dw_cache.append(jnp.concatenate([pc[1] for pc in dw_parts], axis=2))
        return dw_cache

    big_grads = {}
    small_grads = {}

    saved = []
    h_in = x0
    h_in_b = to_bf16("x_bf16", x0)
    for layer in range(N_LAYERS):
        sv = {"x_in": h_in_b}
        if layer % 2 == 0:
            u = mm_cols_fwd("mix_in", h_in_b, weight("mix_w_in", 0, gather_token), F32)
            conv_w_full, ffn_dw_full = conv_weights(u)
            cat, d_sv, e_sv, glu_sv, hh_sv, rs_sv = mixer_fwd(
                "mixer_fwd", u, pool_w[0], pool_scale, conv_w_full, conv_dw_b, conv_ln_g, conv_ln_b)
            mix = mm_rows_fwd("mix_out", cat, weight("mix_w_out", 0, cat))
            sv.update(u=u, cat=cat, d=d_sv, e=e_sv, glu=glu_sv, hh=hh_sv, rs=rs_sv)
        else:
            qkvp = mm_cols_fwd("attn_qkv", h_in_b, weight("attn_w_qkv", 0, h_in_b), BF16,
                               pad_blocks=PAD_ROWS // _row_tile(seq))
            bias = bias_tile("bias_tile", _bias_line(attn_rel_bias[0]))
            att = attn_fwd("attn_fwd", qkvp, bias)
            mix = mm_rows_fwd("attn_out", att, weight("attn_w_o", 0, att))
            sv.update(qkvp=qkvp, bias=bias, att=att)
        x1, x1_b, xh1, rs1 = ln_fwd(f"ln_mix_{layer}", h_in, mix, ln_mix_g[layer:layer + 1],
                                    ln_mix_b[layer:layer + 1])
        gv = mm_cols_fwd(f"ffn_up_{layer}", x1_b, weight("ffn_w_up", layer, x1_b), F32)
        hid = ffn_act_fwd(f"ffn_act_{layer}", gv, ffn_dw_full[layer], ffn_dw_b[layer:layer + 1])
        ffn = mm_rows_fwd(f"ffn_down_{layer}", hid, weight("ffn_w_down", layer, hid))
        pgl = mm_rows_fwd(f"ple_gate_{layer}", x1_b, weight("ple_w_gate", layer, ffn))
        pp = mm_cols_fwd(f"ple_proj_{layer}", p_rows, weight("ple_w_proj", layer, pgl), F32, part=(layer, N_LAYERS))
        bg = ple_b_gate[layer:layer + 1]
        x2, x2_b, xh2, rs2 = ln_fwd(f"ln_ffn_{layer}", x1, ffn, ln_ffn_g[layer:layer + 1], ln_ffn_b[layer:layer + 1],
                                    ple=(pgl, pp, bg), emit_y=layer < N_LAYERS - 1)
        sv.update(x1=x1_b, xh1=xh1, rs1=rs1, gv=gv, hid=hid, pgl=pgl, pp=pp, xh2=xh2, rs2=rs2)
        saved.append(sv)
        h_in, h_in_b = x2, x2_b

    reducers = []

    def open_group(tag, group):
        reducers.append(Reducer(tag, group))
        return reducers[-1].advance(None)

    def hook(after):
        token = None
        for red in reducers:
            if red.stage < 4:
                tk = red.advance(after)
                if tk is not None:
                    token = tk if token is None else token + tk
        return token

    def tied(a, token):
        return a if token is None else tie(a, token)

    parts = []
    token = None
    for layer in reversed(range(N_LAYERS)):
        sv = saved[layer]
        bg = ple_b_gate[layer:layer + 1]
        if layer == 0:
            token = open_group("layer1", [("attn_w_qkv", 0), ("attn_w_o", 0), ("ffn_w_up", 1), ("ffn_w_down", 1),
                                          ("ple_w_gate", 1), ("ple_w_proj", 1)])
        last = layer == N_LAYERS - 1
        res = ln_bwd(
            f"ln_ffn_bwd_{layer}", parts, sv["xh2"], sv["rs2"], tied(ln_ffn_g[layer:layer + 1], token),
            ple=(sv["pgl"], sv["pp"], bg), loss=(target, ln_ffn_b[layer:layer + 1]) if last else None)
        dz2, dg2, db2, dpp, dpgl, dbg = res[:6]
        if last:
            loss_part = res[6]
        small_grads[("ln_ffn_g", layer)] = dg2
        small_grads[("ln_ffn_b", layer)] = db2
        small_grads[("ple_b_gate", layer)] = dbg
        w_down = weight("ffn_w_down", layer)
        dhid = mm_rows_dx(f"ffn_down_dx_{layer}", dz2, w_down)
        big_grads[("ffn_w_down", layer)] = mm_rows_dw(f"ffn_down_dw_{layer}", sv["hid"], dz2)
        token = hook(big_grads[("ffn_w_down", layer)])
        dgv, ddw, ddb = ffn_act_bwd(f"ffn_act_bwd_{layer}", dhid, sv["gv"], ffn_dw_full[layer],
                                    tied(ffn_dw_b[layer:layer + 1], token))
        small_grads[("ffn_dw_w", layer)] = ddw
        small_grads[("ffn_dw_b", layer)] = ddb
        big_grads[("ffn_w_up", layer)] = mm_cols_dw(f"ffn_up_dw_{layer}", sv["x1"], dgv)
        t_up = mm_cols_dx(f"ffn_up_dx_{layer}", dgv, weight("ffn_w_up", layer))
        token = hook(t_up)
        big_grads[("ple_w_gate", layer)] = mm_rows_dw(f"ple_gate_dw_{layer}", sv["x1"], dpgl)
        t_gate = mm_rows_dx(f"ple_gate_dx_{layer}", dpgl, weight("ple_w_gate", layer))
        big_grads[("ple_w_proj", layer)] = mm_cols_dw(f"ple_proj_dw_{layer}", p_rows, dpp, part=(layer, N_LAYERS))
        token2 = hook(big_grads[("ple_w_proj", layer)])
        if token2 is not None:
            token = token2 if token is None else token + token2
        if layer == 0:
            token3 = open_group("layer0_ffn", [("ffn_w_up", 0), ("ffn_w_down", 0), ("ple_w_gate", 0), ("ple_w_proj", 0)])
            token = token3 if token is None else token + token3
        dz1, dg1, db1 = ln_bwd(
            f"ln_mix_bwd_{layer}", [(ALPHA, dz2), (1.0, t_up), (1.0, t_gate)], sv["xh1"], sv["rs1"],
            tied(ln_mix_g[layer:layer + 1], token))
        small_grads[("ln_mix_g", layer)] = dg1
        small_grads[("ln_mix_b", layer)] = db1
        if layer % 2 == 0:
            dcat = mm_rows_dx("mix_out_dx", dz1, weight("mix_w_out", 0))
            big_grads[("mix_w_out", 0)] = mm_rows_dw("mix_out_dw", sv["cat"], dz1)
            token = hook(big_grads[("mix_w_out", 0)])
            du, dpw, dps, dcw, dcb, dcg, dcbt = mixer_bwd(
                "mixer_bwd", dcat, sv["u"], sv["d"], sv["e"], sv["glu"], sv["hh"], sv["rs"],
                pool_w[0], pool_scale, conv_w_full, tied(conv_ln_g, token), conv_ln_b)
            small_grads[("pool_w", 0)] = dpw
            small_grads[("pool_scale", 0)] = dps
            small_grads[("conv_dw_w", 0)] = dcw
            small_grads[("conv_dw_b", 0)] = dcb
            small_grads[("conv_ln_g", 0)] = dcg
            small_grads[("conv_ln_b", 0)] = dcbt
            big_grads[("mix_w_in", 0)] = mm_cols_dw("mix_in_dw", sv["x_in"], du)
            hook(big_grads[("mix_w_in", 0)])
            open_group("layer0_mix", [("mix_w_in", 0), ("mix_w_out", 0)])
            dx_in = mm_cols_dx("mix_in_dx", du, weight("mix_w_in", 0), addend=(ALPHA, dz1))
            token = hook(dx_in)
        else:
            do = mm_rows_dx("attn_out_dx", dz1, weight("attn_w_o", 0), out_dtype=BF16)
            big_grads[("attn_w_o", 0)] = mm_rows_dw("attn_out_dw", sv["att"], dz1)
            dq, dk, dv, ds_sum = attn_bwd("attn_bwd", sv["qkvp"], sv["bias"], do)
            cols, sat = bias_grad_reduce("bias_grad", ds_sum)
            d_rel = jnp.concatenate(
                [jnp.zeros((N_HEADS, 1), F32),
                 jnp.flip(cols[:, 0, Q_TILE + SHEAR_SAT:Q_TILE - 1 + SHEAR_W], axis=1),
                 sat[:, 0, 0:1]], axis=1)
            small_grads[("attn_rel_bias", 0)] = d_rel
            dqkv = jnp.concatenate([dq, dk, dv], axis=1)
            big_grads[("attn_w_qkv", 0)] = mm_cols_dw("attn_qkv_dw", sv["x_in"], dqkv)
            dx_in = mm_cols_dx("attn_qkv_dx", dqkv, weight("attn_w_qkv", 0), addend=(ALPHA, dz1))
        parts = [(1.0, dx_in)]
    grad_x = dx_in

    small = [
        ("pool_w", pool_w, m_pool_w, v_pool_w, None),
        ("pool_scale", pool_scale, m_pool_scale, v_pool_scale, None),
        ("conv_dw_w", conv_dw_w, m_conv_dw_w, v_conv_dw_w, 2),
        ("conv_dw_b", conv_dw_b, m_conv_dw_b, v_conv_dw_b, None),
        ("conv_ln_g", conv_ln_g, m_conv_ln_g, v_conv_ln_g, None),
        ("conv_ln_b", conv_ln_b, m_conv_ln_b, v_conv_ln_b, None),
        ("attn_rel_bias", attn_rel_bias, m_attn_rel_bias, v_attn_rel_bias, None),
        ("ln_mix_g", ln_mix_g, m_ln_mix_g, v_ln_mix_g, None),
        ("ln_mix_b", ln_mix_b, m_ln_mix_b, v_ln_mix_b, None),
        ("ffn_dw_w", ffn_dw_w, m_ffn_dw_w, v_ffn_dw_w, 2),
        ("ffn_dw_b", ffn_dw_b, m_ffn_dw_b, v_ffn_dw_b, None),
        ("ple_b_gate", ple_b_gate, m_ple_b_gate, v_ple_b_gate, None),
        ("ln_ffn_g", ln_ffn_g, m_ln_ffn_g, v_ln_ffn_g, None),
        ("ln_ffn_b", ln_ffn_b, m_ln_ffn_b, v_ln_ffn_b, None),
    ]
    full_grads = []
    for nm, w, _, _, shard_axis in small:
        full = list(w.shape)
        if shard_axis is not None:
            full[shard_axis] *= N_SHARD
        per_layer = [small_grads[(nm, layer)].reshape((1,) + tuple(full[1:])) for layer in range(w.shape[0])]
        full_grads.append(jnp.concatenate(per_layer, axis=0))
    packed = _pack(full_grads + [loss_part])
    dev_arr = (4 * xi + 2 * yi + ci).astype(jnp.int32).reshape(1)
    sg_block = cast_into_gathered("place_small_grads", packed[None], 0, dev_arr, n_blocks=8, dtype=F32)
    sg_send, sg_recv, sg_bufs, sg_token = copies_start("small_grads_start", [sg_block], all_plan, 7)
    token = sg_token if token is None else token + sg_token

    shard_grads = {}
    for red in reducers:
        if red.stage == 4:
            shard_grads.update(red.result)
    big_out = {}

    def update_big(names, tok):
        for nm, w, m, v, _ in big:
            if nm in names:
                gl = [shard_grads[(nm, layer)] for layer in range(w.shape[0])]
                delta, new_m, new_v = adamw(f"adamw_{nm}", w, gl, m, v, token=tok)
                big_out[nm] = (jnp.stack(gl, axis=0), delta, new_m, new_v)

    last_group = ("mix_w_in", "mix_w_out")
    update_big([nm for nm, _, _, _, _ in big if nm not in last_group], token)
    token = hook(big_out["ffn_w_up"][1])

    gathered_sg = copies_wait("small_grads_wait", sg_bufs, sg_send, sg_recv, all_plan, 0, big_out["ffn_w_down"][1])[0]
    total = sum_blocks("sum_small", gathered_sg.reshape(8 * packed.shape[0], LANES), 8)
    unpacked = _unpack(total, [g.shape for g in full_grads] + [loss_part.shape])
    loss = unpacked[-1][0, 0]
    local_grads = []
    for (nm, w, _, _, shard_axis), g in zip(small, unpacked[:-1]):
        if shard_axis is not None:
            width = w.shape[shard_axis]
            g = lax.dynamic_slice_in_dim(g, shard_idx * width, width, axis=shard_axis)
        local_grads.append(g.reshape(w.shape))
    shapes = [w.shape for _, w, _, _, _ in small]
    pg = _pack(local_grads)
    pw = _pack([w for _, w, _, _, _ in small])
    pm = _pack([m for _, _, m, _, _ in small])
    pv = _pack([v for _, _, _, v, _ in small])
    delta_s, new_m_s, new_v_s = adamw("adamw_small", pw[None], [pg], pm[None], pv[None], token=token)
    hook(delta_s)
    for red in reducers:
        shard_grads.update(red.result)
    update_big(last_group, None)
    small_out = {}
    for (nm, _, _, _, _), g, d_, m_, v_ in zip(
            small, local_grads, _unpack(delta_s[0], shapes), _unpack(new_m_s[0], shapes), _unpack(new_v_s[0], shapes)):
        small_out[nm] = (g, d_, m_, v_)

    order = ["mix_w_in", "pool_w", "pool_scale", "conv_dw_w", "conv_dw_b", "conv_ln_g", "conv_ln_b", "mix_w_out",
             "attn_w_qkv", "attn_rel_bias", "attn_w_o", "ln_mix_g", "ln_mix_b", "ffn_w_up", "ffn_dw_w", "ffn_dw_b",
             "ffn_w_down", "ple_w_proj", "ple_w_gate", "ple_b_gate", "ln_ffn_g", "ln_ffn_b"]
    res = {**big_out, **small_out}
    outs = [loss, grad_x[None]]
    for slot in range(4):
        outs += [res[nm][slot] for nm in order]
    return tuple(outs)
```

```python
import functools
import math

import jax
import jax.numpy as jnp
from jax import lax
from jax.experimental import pallas as pl
from jax.experimental.pallas import tpu as pltpu

F32 = jnp.float32
BF16 = jnp.bfloat16
MESH = pl.DeviceIdType.MESH

N_LAYERS = 2
ALPHA = (2 * N_LAYERS) ** 0.25
LN_EPS = 1e-5
NEG_INF = -1e30
CHUNK = 64
LEFT_CHUNKS = 8
PAD_ROWS = LEFT_CHUNKS * CHUNK
HEAD_DIM = 64
ATTN_SCALE = HEAD_DIM ** -0.5
N_HEADS = 16
MAX_REL = 256
POOL_WINDOWS = (2, 4, 8, 16)
POOL_GROUP = 128
CONV_K = 31
FFN_K = 3
CONV_HALO = 32
FFN_HALO = 8
FFN_TILE = 256
FFN_CHUNK_ROWS = 64
FFN_CHUNK_LANES = 128
Q_TILE = 256
K_WIN = Q_TILE + PAD_ROWS
LANES = 128
SUBLANES = 8
ATTN_PAIRS = 2
ATTN_LANES = ATTN_PAIRS * LANES
SHEAR_W = Q_TILE + K_WIN
SHEAR_SAT = SHEAR_W - 2 * MAX_REL
N_SHARD = 4

ADAM_LR = 0.001
ADAM_B1 = 0.9
ADAM_B2 = 0.999
ADAM_EPS = 1e-08
ADAM_WD = 0.01
ADAM_STEP = 10
ADAM_BC1 = 1.0 - ADAM_B1 ** ADAM_STEP
ADAM_BC2 = 1.0 - ADAM_B2 ** ADAM_STEP

DIMS = {
    "nn": (((1,), (0,)), ((), ())),
    "nt": (((1,), (1,)), ((), ())),
    "tn": (((0,), (0,)), ((), ())),
}


def _cp(vmem_mb=48, **kw):
    return pltpu.CompilerParams(vmem_limit_bytes=vmem_mb * 1024 * 1024, **kw)


def _in_hbm(a):
    return pltpu.with_memory_space_constraint(a, pltpu.HBM)


STAGING_LIMIT_BYTES = 1 << 20
SUM_BLOCK_ROWS = 2048
SMALL_BLOCK_BYTES = 1 << 19


def _call(body, **kw):
    call = pl.pallas_call(body, **kw)

    def run(*args):
        pinned = []
        for a in args:
            big = a.size * a.dtype.itemsize >= STAGING_LIMIT_BYTES
            pinned.append(_in_hbm(a) if big and not jnp.issubdtype(a.dtype, jnp.integer) else a)
        return call(*pinned)

    return run


def _dot(a, b, mode):
    return lax.dot_general(a.astype(BF16), b.astype(BF16), DIMS[mode], preferred_element_type=F32)


def _sig(x):
    return 1.0 / (1.0 + jnp.exp(-x))


def _row_tile(s):
    return min(512, s // 4)


def _mm_tile(s):
    return min(1024, s // 4)


def _mm(name, mode, a, b, in_specs, out_shape, out_spec, acc_shape, grid, nk, zero_first=False, vmem_mb=48,
        addend=None):
    out_f32 = out_shape.dtype == F32

    def body(a_ref, b_ref, *rest):
        k = pl.program_id(2)
        if addend is None:
            o_ref, scr = rest[0], rest[1:]
        else:
            add_ref, o_ref, scr = rest[0], rest[1], rest[2:]

        def compute():
            part = _dot(a_ref[...], b_ref[...], mode)
            if nk == 1:
                if addend is not None:
                    part = part + addend[0] * add_ref[...]
                o_ref[...] = part.astype(o_ref.dtype)
                return
            acc = o_ref if out_f32 else scr[0]

            @pl.when(k == 0)
            def _():
                acc[...] = part if addend is None else part + addend[0] * add_ref[...]

            @pl.when(k > 0)
            def _():
                acc[...] += part

            if not out_f32:
                @pl.when(k == nk - 1)
                def _():
                    o_ref[...] = acc[...].astype(o_ref.dtype)

        if zero_first:
            @pl.when(pl.program_id(1) == 0)
            def _():
                o_ref[...] = jnp.zeros(o_ref.shape, o_ref.dtype)

            pl.when(pl.program_id(1) > 0)(compute)
        else:
            compute()

    scratch = [] if (nk == 1 or out_f32) else [pltpu.VMEM(acc_shape, F32)]
    operands = [a, b] if addend is None else [a, b, addend[1]]
    specs = list(in_specs) if addend is None else list(in_specs) + [out_spec]
    return _call(
        body, name=name, grid=grid, in_specs=specs, out_specs=out_spec, out_shape=out_shape,
        scratch_shapes=scratch, compiler_params=_cp(vmem_mb),
    )(*operands)


def mm_cols_fwd(name, a, wc, out_dtype, pad_blocks=0, part=(0, 1)):
    s, k = a.shape
    s //= part[1]
    n4 = wc.shape[2]
    tm = _row_tile(s) if pad_blocks else _mm_tile(s)
    nt = s // tm
    first_block = part[0] * nt
    return _mm(
        name, "nn", a, wc,
        [pl.BlockSpec((tm, k), lambda j, i, r: (first_block + jnp.maximum(i - pad_blocks, 0), 0)),
         pl.BlockSpec((None, k, n4), lambda j, i, r: (j, 0, 0))],
        jax.ShapeDtypeStruct((s + pad_blocks * tm, N_SHARD * n4), out_dtype),
        pl.BlockSpec((tm, n4), lambda j, i, r: (i, j)),
        None, (N_SHARD, nt + pad_blocks, 1), 1, zero_first=pad_blocks > 0)


def mm_cols_dx(name, dy, wc, addend=None):
    s = dy.shape[0]
    _, k, n4 = wc.shape
    tm = _mm_tile(s)
    return _mm(
        name, "nt", dy, wc,
        [pl.BlockSpec((tm, n4), lambda g, i, r: (i, r)),
         pl.BlockSpec((None, k, n4), lambda g, i, r: (r, 0, 0))],
        jax.ShapeDtypeStruct((s, k), F32),
        pl.BlockSpec((tm, k), lambda g, i, r: (i, 0)),
        (tm, k), (1, s // tm, N_SHARD), N_SHARD, addend=addend)


def mm_cols_dw(name, a, dy, part=(0, 1)):
    s, k = a.shape
    s //= part[1]
    n4 = dy.shape[1] // N_SHARD
    tm = _mm_tile(s)
    nt = s // tm
    first_block = part[0] * nt
    return _mm(
        name, "tn", a, dy,
        [pl.BlockSpec((tm, k), lambda j, g, r: (first_block + r, 0)),
         pl.BlockSpec((tm, n4), lambda j, g, r: (r, j))],
        jax.ShapeDtypeStruct((N_SHARD, k, n4), F32),
        pl.BlockSpec((None, k, n4), lambda j, g, r: (j, 0, 0)),
        (k, n4), (N_SHARD, 1, nt), nt)


def _k_tile(k):
    return k if k <= 1024 else k // 2


def mm_rows_fwd(name, a, wr, out_dtype=F32):
    s, k = a.shape
    n = wr.shape[1]
    tm = _mm_tile(s)
    tk = _k_tile(k)
    nk = k // tk
    return _mm(
        name, "nn", a, wr,
        [pl.BlockSpec((tm, tk), lambda g, i, r: (i, r)),
         pl.BlockSpec((tk, n), lambda g, i, r: (r, 0))],
        jax.ShapeDtypeStruct((s, n), out_dtype),
        pl.BlockSpec((tm, n), lambda g, i, r: (i, 0)),
        (tm, n), (1, s // tm, nk), nk)


def mm_rows_dx(name, dy, wr, out_dtype=F32):
    s, n = dy.shape
    k = wr.shape[0]
    tm = _mm_tile(s)
    tk = _k_tile(k)
    return _mm(
        name, "nt", dy, wr,
        [pl.BlockSpec((tm, n), lambda j, i, r: (i, 0)),
         pl.BlockSpec((tk, n), lambda j, i, r: (j, 0))],
        jax.ShapeDtypeStruct((s, k), out_dtype),
        pl.BlockSpec((tm, tk), lambda j, i, r: (i, j)),
        None, (k // tk, s // tm, 1), 1)


def mm_rows_dw(name, a, dy):
    s, k = a.shape
    n = dy.shape[1]
    tm = _mm_tile(s)
    tk = _k_tile(k)
    nt = s // tm
    return _mm(
        name, "tn", a, dy,
        [pl.BlockSpec((tm, tk), lambda j, g, r: (r, j)),
         pl.BlockSpec((tm, n), lambda j, g, r: (r, 0))],
        jax.ShapeDtypeStruct((k, n), F32),
        pl.BlockSpec((tk, n), lambda j, g, r: (j, 0)),
        (tk, n), (k // tk, 1, nt), nt)


def _row(tm, c, col=0):
    return pl.BlockSpec((tm, c), lambda i: (i, col))


def _full(shape):
    nd = len(shape)
    return pl.BlockSpec(shape, lambda i: (0,) * nd)


def _prev(tm, h, c, col=0):
    return pl.BlockSpec((h, c), lambda i: (jnp.maximum(i * (tm // h) - 1, 0), col))


def _next(tm, h, c, s, col=0):
    return pl.BlockSpec((h, c), lambda i: (jnp.minimum((i + 1) * (tm // h), s // h - 1), col))


def _acc_add(ref, first, val):
    @pl.when(first)
    def _():
        ref[...] = val

    @pl.when(jnp.logical_not(first))
    def _():
        ref[...] += val


def _colsum(v):
    return jnp.sum(v, axis=0, keepdims=True)


def _ln_stats(z):
    mu = jnp.mean(z, axis=-1, keepdims=True)
    zc = z - mu
    var = jnp.mean(zc * zc, axis=-1, keepdims=True)
    rstd = lax.rsqrt(var + LN_EPS)
    return zc * rstd, rstd


def _ln_bwd(dxhat, xhat, rstd):
    m1 = jnp.mean(dxhat, axis=-1, keepdims=True)
    m2 = jnp.mean(dxhat * xhat, axis=-1, keepdims=True)
    return rstd * (dxhat - m1 - xhat * m2)


def ln_fwd(name, x, f, g, b, ple=None, emit_y=True):
    s, d = x.shape
    tm = _row_tile(s)
    n_in = 2 + (3 if ple is not None else 0)

    def body(*refs):
        x_ref, f_ref = refs[0], refs[1]
        g_ref, b_ref = refs[n_in], refs[n_in + 1]
        xh_ref, rs_ref = refs[-2:]
        z = ALPHA * x_ref[...] + f_ref[...]
        if ple is not None:
            pgl_ref, pp_ref, bg_ref = refs[2:5]
            z = z + _sig(pgl_ref[...] + bg_ref[...]) * pp_ref[...]
        xhat, rstd = _ln_stats(z)
        if emit_y:
            y = xhat * g_ref[...] + b_ref[...]
            refs[n_in + 2][...] = y
            refs[n_in + 3][...] = y.astype(BF16)
        xh_ref[...] = xhat
        rs_ref[...] = jnp.broadcast_to(rstd, rs_ref.shape)

    ins = [x, f]
    specs = [_row(tm, d), _row(tm, d)]
    if ple is not None:
        pgl, pp, bg = ple
        ins += [pgl, pp, bg]
        specs += [_row(tm, d), _row(tm, d), _full((1, d))]
    ins += [g, b]
    specs += [_full((1, d)), _full((1, d))]
    y_shapes = [jax.ShapeDtypeStruct((s, d), F32), jax.ShapeDtypeStruct((s, d), BF16)] if emit_y else []
    outs = _call(
        body, name=name, grid=(s // tm,), in_specs=specs,
        out_specs=[_row(tm, d)] * (len(y_shapes) + 1) + [_row(tm, LANES)],
        out_shape=y_shapes + [jax.ShapeDtypeStruct((s, d), F32), jax.ShapeDtypeStruct((s, LANES), F32)],
        compiler_params=_cp(),
    )(*ins)
    return tuple(outs) if emit_y else (None, None, outs[0], outs[1])


def ln_bwd(name, parts, xhat, rstd, g, ple=None, loss=None):
    s, d = xhat.shape
    tm = _row_tile(s)
    coefs = [c for c, _ in parts]
    n_p = len(parts)
    n_ple = 3 if ple is not None else 0
    n_in = n_p + 3 + n_ple + (2 if loss is not None else 0)

    def body(*refs):
        first = pl.program_id(0) == 0
        xh = refs[n_p][...]
        rs = refs[n_p + 1][:, 0:1]
        g_v = refs[n_p + 2][...]
        outs = refs[n_in:]
        if loss is not None:
            t_ref, b_ref = refs[n_p + 3 + n_ple:n_p + 5 + n_ple]
            err = (xh * g_v + b_ref[...]) - t_ref[...]
            dy = err * (1.0 / d)
            part = 0.5 * jnp.sum(jnp.mean(err * err, axis=-1, keepdims=True), axis=0, keepdims=True)
            _acc_add(outs[-1], first, jnp.broadcast_to(part, outs[-1].shape))
        else:
            dy = coefs[0] * refs[0][...].astype(F32)
            for j in range(1, n_p):
                dy = dy + coefs[j] * refs[j][...].astype(F32)
        dz = _ln_bwd(dy * g_v, xh, rs)
        outs[0][...] = dz
        _acc_add(outs[1], first, _colsum(dy * xh))
        _acc_add(outs[2], first, _colsum(dy))
        if ple is not None:
            pgl_ref, pp_ref, bg_ref = refs[n_p + 3:n_p + 6]
            pg = _sig(pgl_ref[...] + bg_ref[...])
            dpgl = dz * pp_ref[...] * pg * (1.0 - pg)
            outs[3][...] = (dz * pg).astype(BF16)
            outs[4][...] = dpgl.astype(BF16)
            _acc_add(outs[5], first, _colsum(dpgl))

    ins = [p for _, p in parts] + [xhat, rstd, g]
    specs = [_row(tm, d)] * n_p + [_row(tm, d), _row(tm, LANES), _full((1, d))]
    out_specs = [_row(tm, d), _full((1, d)), _full((1, d))]
    out_shape = [jax.ShapeDtypeStruct((s, d), F32), jax.ShapeDtypeStruct((1, d), F32),
                 jax.ShapeDtypeStruct((1, d), F32)]
    if ple is not None:
        pgl, pp, bg = ple
        ins += [pgl, pp, bg]
        specs += [_row(tm, d), _row(tm, d), _full((1, d))]
        out_specs += [_row(tm, d), _row(tm, d), _full((1, d))]
        out_shape += [jax.ShapeDtypeStruct((s, d), BF16), jax.ShapeDtypeStruct((s, d), BF16),
                      jax.ShapeDtypeStruct((1, d), F32)]
    if loss is not None:
        target, b = loss
        ins += [target, b]
        specs += [_row(tm, d), _full((1, d))]
        out_specs += [_full((8, LANES))]
        out_shape += [jax.ShapeDtypeStruct((8, LANES), F32)]
    return _call(
        body, name=name, grid=(s // tm,), in_specs=specs, out_specs=out_specs, out_shape=out_shape,
        compiler_params=_cp(),
    )(*ins)


def _fill_rotations(rot_ref, x, direction):
    n = x.shape[0]
    rot_ref[0] = x
    for b in range(1, SUBLANES):
        if direction < 0:
            rot_ref[b, SUBLANES:n, :] = x[SUBLANES - b:n - b]
        else:
            rot_ref[b, 0:n - SUBLANES, :] = x[b:n - SUBLANES + b]


def _rotated(rot_ref, start, rows, cs, direction=-1):
    b = (-start) % SUBLANES if direction < 0 else start % SUBLANES
    aligned = start + b if direction < 0 else start - b
    return rot_ref[b, pl.ds(aligned, rows), cs]


def _tile_pos(i, tm, rows):
    return (i * tm + lax.broadcasted_iota(jnp.int32, (rows, 1), 0) + 1).astype(F32)


def mixer_fwd(name, u, pool_w, pool_scale, conv_w, conv_b, cn_g, cn_b):
    s = u.shape[0]
    dp = 512
    tm = min(256, s // 4)
    h = CONV_HALO

    def body(a_c, a_p, bv_c, bv_p, bg_c, bg_p, pw_ref, ps_ref, cw_ref, cb_ref, cg_ref, cbt_ref,
             cat_ref, d_ref, e_ref, glu_ref, hh_ref, rs_ref, ext_a, rot_g, conv_out):
        i = pl.program_id(0)
        first = i == 0
        ext_a[0:h, :] = jnp.where(first, 0.0, a_p[...])
        ext_a[h:, :] = a_c[...]
        glu = bv_c[...] * _sig(bg_c[...])
        glu_ref[...] = glu
        _fill_rotations(rot_g, jnp.concatenate([jnp.where(first, 0.0, bv_p[...] * _sig(bg_p[...])), glu], axis=0), -1)
        pos = _tile_pos(i, tm, tm)
        for gi, w in enumerate(POOL_WINDOWS):
            cs = slice(gi * POOL_GROUP, (gi + 1) * POOL_GROUP)
            a_g = ext_a[pl.ds(h, tm), cs]
            acc = a_g
            for sh in range(1, w):
                acc = acc + ext_a[pl.ds(h - sh, tm), cs]
            d_g = acc / jnp.minimum(pos, float(w)) - a_g
            d_ref[:, cs] = d_g.astype(BF16)
            e_g = _dot(d_g, pw_ref[gi], "nn")
            e_ref[:, cs] = e_g
            cat_ref[:, cs] = (e_g * ps_ref[:, cs]).astype(BF16)
        for lg in range(dp // LANES):
            cs = slice(lg * LANES, (lg + 1) * LANES)
            acc = jnp.broadcast_to(cb_ref[:, cs], (tm, LANES))
            for sh in range(CONV_K):
                acc = acc + _rotated(rot_g, h - sh, tm, cs) * cw_ref[pl.ds(CONV_K - 1 - sh, 1), cs]
            conv_out[:, cs] = acc
        hhat, rstd = _ln_stats(conv_out[...])
        hl = hhat * cg_ref[...] + cbt_ref[...]
        cat_ref[:, dp:] = (hl * _sig(hl)).astype(BF16)
        hh_ref[...] = hhat
        rs_ref[...] = jnp.broadcast_to(rstd, rs_ref.shape)

    specs = [_row(tm, dp, 0), _prev(tm, h, dp, 0), _row(tm, dp, 1), _prev(tm, h, dp, 1),
             _row(tm, dp, 2), _prev(tm, h, dp, 2),
             _full((4, POOL_GROUP, POOL_GROUP)), _full((1, dp)), _full((CONV_K, dp)),
             _full((1, dp)), _full((1, dp)), _full((1, dp))]
    out_specs = [_row(tm, 2 * dp), _row(tm, dp), _row(tm, dp), _row(tm, dp), _row(tm, dp), _row(tm, LANES)]
    out_shape = [jax.ShapeDtypeStruct((s, 2 * dp), BF16), jax.ShapeDtypeStruct((s, dp), BF16),
                 jax.ShapeDtypeStruct((s, dp), F32), jax.ShapeDtypeStruct((s, dp), F32),
                 jax.ShapeDtypeStruct((s, dp), F32), jax.ShapeDtypeStruct((s, LANES), F32)]
    return _call(
        body, name=name, grid=(s // tm,), in_specs=specs, out_specs=out_specs, out_shape=out_shape,
        scratch_shapes=[pltpu.VMEM((h + tm, dp), F32), pltpu.VMEM((SUBLANES, h + tm, dp), F32),
                        pltpu.VMEM((tm, dp), F32)],
        compiler_params=_cp(),
    )(u, u, u, u, u, u, pool_w, pool_scale, conv_w, conv_b, cn_g, cn_b)


def mixer_bwd(name, dcat, u, d_sv, e_sv, glu_sv, hh_sv, rs_sv, pool_w, pool_scale, conv_w, cn_g, cn_b):
    s = u.shape[0]
    dp = 512
    tm = min(256, s // 4)
    h = CONV_HALO
    nt = s // tm

    def body(dc_c, dc_n, bv_c, bg_c, d_c, e_c, gl_c, gl_p, hh_c, hh_n, rs_c, rs_n,
             pw_ref, ps_ref, cw_ref, cg_ref, cbt_ref,
             du_ref, dpw_ref, dps_ref, dcw_ref, dcb_ref, dcg_ref, dcbt_ref,
             ext_dh, ext_g, ext_r):
        i = pl.program_id(0)
        first = i == 0
        last = i == nt - 1
        cg = cg_ref[...]

        def conv_grads(dyb, hhat, rstd):
            hl = hhat * cg + cbt_ref[...]
            sg = _sig(hl)
            dhl = dyb * (sg * (1.0 + hl * (1.0 - sg)))
            return _ln_bwd(dhl * cg, hhat, rstd), dhl

        hh_cur = hh_c[...]
        dh_c, dhl_c = conv_grads(dc_c[:, dp:], hh_cur, rs_c[:, 0:1])
        dh_n, _ = conv_grads(dc_n[:, dp:], hh_n[...], rs_n[:, 0:1])
        _fill_rotations(ext_dh, jnp.concatenate([dh_c, jnp.where(last, 0.0, dh_n)], axis=0), 1)
        _fill_rotations(ext_g, jnp.concatenate([jnp.where(first, 0.0, gl_p[...]), gl_c[...]], axis=0), -1)

        @pl.when(first)
        def _():
            dcw_ref[...] = jnp.zeros(dcw_ref.shape, F32)

        for lg in range(dp // LANES):
            cs = slice(lg * LANES, (lg + 1) * LANES)
            dglu = jnp.zeros((tm, LANES), F32)
            for sh in range(CONV_K):
                dglu = dglu + _rotated(ext_dh, sh, tm, cs, 1) * cw_ref[pl.ds(CONV_K - 1 - sh, 1), cs]
            dh_g = ext_dh[0, pl.ds(0, tm), cs]
            for sh in range(CONV_K):
                dcw_ref[pl.ds(CONV_K - 1 - sh, 1), cs] += _colsum(dh_g * _rotated(ext_g, h - sh, tm, cs))
            sgate = _sig(bg_c[:, cs])
            du_ref[:, dp + lg * LANES:dp + (lg + 1) * LANES] = dglu * sgate
            du_ref[:, 2 * dp + lg * LANES:2 * dp + (lg + 1) * LANES] = dglu * bv_c[:, cs] * sgate * (1.0 - sgate)
        _acc_add(dcb_ref, first, _colsum(dh_c))
        _acc_add(dcg_ref, first, _colsum(dhl_c * hh_cur))
        _acc_add(dcbt_ref, first, _colsum(dhl_c))

        pos_c = _tile_pos(i, tm, tm)
        pos_n = _tile_pos(i + 1, tm, h)
        _acc_add(dps_ref, first, _colsum(dc_c[:, :dp] * e_c[...]))
        for gi, w in enumerate(POOL_WINDOWS):
            cs = slice(gi * POOL_GROUP, (gi + 1) * POOL_GROUP)
            pw = pw_ref[gi]
            de_c = dc_c[:, cs] * ps_ref[:, cs]
            de_n = dc_n[:, cs] * ps_ref[:, cs]
            dd_c = _dot(de_c, pw, "nt")
            dd_n = _dot(de_n, pw, "nt")
            ext_r[0:tm, :] = dd_c / jnp.minimum(pos_c, float(w))
            ext_r[tm:, :] = jnp.where(last, 0.0, dd_n / jnp.minimum(pos_n, float(w)))
            acc = -dd_c
            for sh in range(w):
                acc = acc + ext_r[pl.ds(sh, tm), :]
            du_ref[:, cs] = acc
            dpw_g = _dot(d_c[:, cs], de_c, "tn")

            @pl.when(first)
            def _():
                dpw_ref[gi] = dpw_g

            @pl.when(jnp.logical_not(first))
            def _():
                dpw_ref[gi] += dpw_g

    specs = [_row(tm, 2 * dp), _next(tm, h, 2 * dp, s), _row(tm, dp, 1), _row(tm, dp, 2),
             _row(tm, dp), _row(tm, dp), _row(tm, dp), _prev(tm, h, dp),
             _row(tm, dp), _next(tm, h, dp, s), _row(tm, LANES), _next(tm, h, LANES, s),
             _full((4, POOL_GROUP, POOL_GROUP)), _full((1, dp)), _full((CONV_K, dp)),
             _full((1, dp)), _full((1, dp))]
    out_specs = [_row(tm, 3 * dp), _full((4, POOL_GROUP, POOL_GROUP)), _full((1, dp)), _full((CONV_K, dp)),
                 _full((1, dp)), _full((1, dp)), _full((1, dp))]
    out_shape = [jax.ShapeDtypeStruct((s, 3 * dp), F32),
                 jax.ShapeDtypeStruct((4, POOL_GROUP, POOL_GROUP), F32), jax.ShapeDtypeStruct((1, dp), F32),
                 jax.ShapeDtypeStruct((CONV_K, dp), F32), jax.ShapeDtypeStruct((1, dp), F32),
                 jax.ShapeDtypeStruct((1, dp), F32), jax.ShapeDtypeStruct((1, dp), F32)]
    return _call(
        body, name=name, grid=(nt,), in_specs=specs, out_specs=out_specs, out_shape=out_shape,
        scratch_shapes=[pltpu.VMEM((SUBLANES, tm + h, dp), F32), pltpu.VMEM((SUBLANES, h + tm, dp), F32),
                        pltpu.VMEM((tm + h, POOL_GROUP), F32)],
        compiler_params=_cp(),
    )(dcat, dcat, u, u, d_sv, e_sv, glu_sv, glu_sv, hh_sv, hh_sv, rs_sv, rs_sv,
      pool_w, pool_scale, conv_w, cn_g, cn_b)


GELU_C = math.sqrt(2.0 / math.pi)


def _gelu_parts(x):
    x2 = x * x
    t = jnp.tanh(x * (GELU_C + (GELU_C * 0.044715) * x2))
    half_1pt = 0.5 + 0.5 * t
    gelu = x * half_1pt
    dgelu = half_1pt + (0.5 * x) * (1.0 - t * t) * (GELU_C + (3.0 * GELU_C * 0.044715) * x2)
    return gelu, dgelu


def ffn_act_fwd(name, gv, dw_w, dw_b):
    s = gv.shape[0]
    dff = gv.shape[1] // 2
    tm = min(FFN_TILE, s // 4)
    h = FFN_HALO
    rc = FFN_CHUNK_ROWS
    lw = FFN_CHUNK_LANES

    def body(g_c, g_p, v_c, w_ref, b_ref, hid_ref):
        first = pl.program_id(0) == 0

        def chunk(ci, carry):
            r0 = pl.multiple_of(ci * rc, rc)
            above = pl.multiple_of(jnp.maximum(r0 - h, 0), h)
            for lg in range(dff // lw):
                cs = slice(lg * lw, (lg + 1) * lw)
                top = jnp.where(ci == 0, jnp.where(first, 0.0, g_p[:, cs]), g_c[pl.ds(above, h), cs])
                win = jnp.concatenate([top, g_c[pl.ds(r0, rc), cs]], axis=0)
                gc = jnp.broadcast_to(b_ref[:, cs], (rc, lw))
                for sh in range(FFN_K):
                    gc = gc + win[h - sh:h - sh + rc] * w_ref[pl.ds(FFN_K - 1 - sh, 1), cs]
                gelu, _ = _gelu_parts(gc)
                hid_ref[pl.ds(r0, rc), cs] = (gelu * v_c[pl.ds(r0, rc), cs]).astype(BF16)
            return carry

        lax.fori_loop(0, tm // rc, chunk, 0)

    return _call(
        body, name=name, grid=(s // tm,),
        in_specs=[_row(tm, dff, 0), _prev(tm, h, dff, 0), _row(tm, dff, 1), _full((FFN_K, dff)), _full((1, dff))],
        out_specs=_row(tm, dff), out_shape=jax.ShapeDtypeStruct((s, dff), BF16),
        compiler_params=_cp(),
    )(gv, gv, gv, dw_w, dw_b)


def ffn_act_bwd(name, dhid, gv, dw_w, dw_b):
    s = gv.shape[0]
    dff = gv.shape[1] // 2
    tm = min(FFN_TILE, s // 4)
    h = FFN_HALO
    nt = s // tm
    rc = FFN_CHUNK_ROWS
    lw = FFN_CHUNK_LANES
    n_chunks = tm // rc

    def body(dh_c, dh_n, g_p, g_c, g_n, v_c, v_n, w_ref, b_ref, dgv_ref, dw_ref, db_ref):
        i = pl.program_id(0)
        first = i == 0
        last = i == nt - 1

        @pl.when(first)
        def _():
            dw_ref[...] = jnp.zeros(dw_ref.shape, F32)
            db_ref[...] = jnp.zeros(db_ref.shape, F32)

        def chunk(ci, carry):
            r0 = pl.multiple_of(ci * rc, rc)
            above = pl.multiple_of(jnp.maximum(r0 - h, 0), h)
            below = pl.multiple_of(jnp.minimum(r0 + rc, tm - h), h)
            at_end = ci == n_chunks - 1
            for lg in range(dff // lw):
                cs = slice(lg * lw, (lg + 1) * lw)
                top = jnp.where(ci == 0, jnp.where(first, 0.0, g_p[:, cs]), g_c[pl.ds(above, h), cs])
                bot = jnp.where(at_end, g_n[:, cs], g_c[pl.ds(below, h), cs])
                win = jnp.concatenate([top, g_c[pl.ds(r0, rc), cs], bot], axis=0)
                shifted = [win[h - sh:h - sh + rc + h] for sh in range(FFN_K)]
                gc = jnp.broadcast_to(b_ref[:, cs], (rc + h, lw))
                for sh in range(FFN_K):
                    gc = gc + shifted[sh] * w_ref[pl.ds(FFN_K - 1 - sh, 1), cs]
                gelu, dgelu = _gelu_parts(gc)
                dh_mid = dh_c[pl.ds(r0, rc), cs]
                hv_bot = jnp.where(at_end, jnp.where(last, 0.0, dh_n[:, cs] * v_n[:, cs]),
                                   dh_c[pl.ds(below, h), cs] * v_c[pl.ds(below, h), cs])
                dgc = jnp.concatenate([dh_mid * v_c[pl.ds(r0, rc), cs], hv_bot], axis=0) * dgelu
                dgate = jnp.zeros((rc, lw), F32)
                for sh in range(FFN_K):
                    dgate = dgate + dgc[sh:sh + rc] * w_ref[pl.ds(FFN_K - 1 - sh, 1), cs]
                dgv_ref[pl.ds(r0, rc), cs] = dgate.astype(BF16)
                dgv_ref[pl.ds(r0, rc), slice(dff + lg * lw, dff + (lg + 1) * lw)] = (dh_mid * gelu[0:rc]).astype(BF16)
                dgc_mid = dgc[0:rc]
                for sh in range(FFN_K):
                    dw_ref[pl.ds(FFN_K - 1 - sh, 1), cs] += _colsum(dgc_mid * shifted[sh][0:rc])
                db_ref[:, cs] += _colsum(dgc_mid)
            return carry

        lax.fori_loop(0, n_chunks, chunk, 0)

    return _call(
        body, name=name, grid=(nt,),
        in_specs=[_row(tm, dff), _next(tm, h, dff, s),
                  _prev(tm, h, dff, 0), _row(tm, dff, 0), _next(tm, h, dff, s, 0),
                  _row(tm, dff, 1), _next(tm, h, dff, s, 1),
                  _full((FFN_K, dff)), _full((1, dff))],
        out_specs=[_row(tm, 2 * dff), _full((FFN_K, dff)), _full((1, dff))],
        out_shape=[jax.ShapeDtypeStruct((s, 2 * dff), BF16), jax.ShapeDtypeStruct((FFN_K, dff), F32),
                   jax.ShapeDtypeStruct((1, dff), F32)],
        compiler_params=_cp(),
    )(dhid, dhid, gv, gv, gv, gv, gv, dw_w, dw_b)


def _bias_line(rel_bias):
    nh = rel_bias.shape[0]
    line = jnp.concatenate(
        [jnp.zeros((nh, 1), rel_bias.dtype), jnp.broadcast_to(rel_bias[:, 2 * MAX_REL:], (nh, SHEAR_SAT)),
         jnp.flip(rel_bias[:, 1:2 * MAX_REL], axis=1)], axis=1)
    return line[:, None, :]


def bias_tile(name, line):
    nh = line.shape[0]

    def body(l_ref, o_ref):
        x = jnp.broadcast_to(l_ref[...], (Q_TILE, SHEAR_W))
        z = pltpu.roll(x, SHEAR_W - Q_TILE, 1, stride=1, stride_axis=0)
        qc = lax.broadcasted_iota(jnp.int32, (Q_TILE, K_WIN), 0) // CHUNK
        kc = lax.broadcasted_iota(jnp.int32, (Q_TILE, K_WIN), 1) // CHUNK
        o_ref[...] = jnp.where((kc >= qc) & (kc <= qc + LEFT_CHUNKS), z[:, :K_WIN], NEG_INF)

    return _call(
        body, name=name, grid=(nh,), in_specs=[pl.BlockSpec((None, 1, SHEAR_W), lambda hh: (hh, 0, 0))],
        out_specs=pl.BlockSpec((None, Q_TILE, K_WIN), lambda hh: (hh, 0, 0)),
        out_shape=jax.ShapeDtypeStruct((nh, Q_TILE, K_WIN), F32), compiler_params=_cp(),
    )(line)


def _stack_heads(x2, scale=None):
    if scale is not None:
        x2 = x2 * jnp.asarray(scale, x2.dtype)
    lane = lax.broadcasted_iota(jnp.int32, x2.shape, 1)
    zero = jnp.zeros_like(x2)
    return jnp.concatenate([jnp.where(lane < HEAD_DIM, x2, zero), jnp.where(lane < HEAD_DIM, zero, x2)], axis=0)


def _unstack_heads(x_st):
    lane = lax.broadcasted_iota(jnp.int32, (Q_TILE, LANES), 1)
    return jnp.where(lane < HEAD_DIM, x_st[:Q_TILE], x_st[Q_TILE:])


def _attn_probs(q_st, k3, bias_st, t):
    sc = _dot(q_st, k3, "nt") + bias_st
    col = lax.broadcasted_iota(jnp.int32, sc.shape, 1)
    sc = jnp.where(col >= PAD_ROWS - t * Q_TILE, sc, NEG_INF)
    m = jnp.max(sc, axis=-1, keepdims=True)
    p = jnp.exp(sc - m)
    return p * (1.0 / jnp.sum(p, axis=-1, keepdims=True))


def _attn_specs(d_model):
    nq = PAD_ROWS // Q_TILE
    groups = d_model // ATTN_LANES
    specs = [pl.BlockSpec((Q_TILE, ATTN_LANES), lambda g, t: (t + nq, g))]
    for which in (1, 2):
        for j in range(K_WIN // Q_TILE):
            specs.append(pl.BlockSpec((Q_TILE, ATTN_LANES), lambda g, t, j=j, which=which: (t + j, which * groups + g)))
    specs.append(pl.BlockSpec((2 * ATTN_PAIRS, Q_TILE, K_WIN), lambda g, t: (g, 0, 0)))
    return specs


def attn_fwd(name, qkvp, bias):
    s = qkvp.shape[0] - PAD_ROWS
    d_model = qkvp.shape[1] // 3
    nw = K_WIN // Q_TILE

    def body(q_ref, *refs):
        k_refs, v_refs, b_ref, o_ref = refs[:nw], refs[nw:2 * nw], refs[2 * nw], refs[2 * nw + 1]
        t = pl.program_id(1)
        for j in range(ATTN_PAIRS):
            ls = slice(j * LANES, (j + 1) * LANES)
            k3 = jnp.concatenate([r[:, ls] for r in k_refs], axis=0)
            v3 = jnp.concatenate([r[:, ls] for r in v_refs], axis=0)
            bias_st = b_ref[2 * j:2 * j + 2].reshape(2 * Q_TILE, K_WIN)
            p = _attn_probs(_stack_heads(q_ref[:, ls], ATTN_SCALE), k3, bias_st, t)
            o_ref[:, ls] = _unstack_heads(_dot(p, v3, "nn")).astype(BF16)

    return _call(
        body, name=name, grid=(d_model // ATTN_LANES, s // Q_TILE),
        in_specs=_attn_specs(d_model), out_specs=pl.BlockSpec((Q_TILE, ATTN_LANES), lambda g, t: (t, g)),
        out_shape=jax.ShapeDtypeStruct((s, d_model), BF16), compiler_params=_cp(),
    )(qkvp, *([qkvp] * (2 * nw)), bias)


def attn_bwd(name, qkvp, bias, do):
    s = qkvp.shape[0] - PAD_ROWS
    d_model = qkvp.shape[1] // 3
    nw = K_WIN // Q_TILE
    nt = s // Q_TILE

    def body(q_ref, *refs):
        k_refs, v_refs = refs[:nw], refs[nw:2 * nw]
        b_ref, do_ref, dq_ref, dk_ref, dv_ref, ds_ref, dk_acc, dv_acc = refs[2 * nw:]
        t = pl.program_id(1)
        first = t == 0

        @pl.when(first)
        def _():
            dk_acc[...] = jnp.zeros(dk_acc.shape, F32)
            dv_acc[...] = jnp.zeros(dv_acc.shape, F32)

        start = pl.multiple_of(t * Q_TILE, Q_TILE)
        for j in range(ATTN_PAIRS):
            ls = slice(j * LANES, (j + 1) * LANES)
            q_st = _stack_heads(q_ref[:, ls], ATTN_SCALE)
            do_st = _stack_heads(do_ref[:, ls])
            k3 = jnp.concatenate([r[:, ls] for r in k_refs], axis=0)
            v3 = jnp.concatenate([r[:, ls] for r in v_refs], axis=0)
            p = _attn_probs(q_st, k3, b_ref[2 * j:2 * j + 2].reshape(2 * Q_TILE, K_WIN), t)
            dp = _dot(do_st, v3, "nt")
            ds = p * (dp - jnp.sum(p * dp, axis=-1, keepdims=True))
            _acc_add(ds_ref.at[2 * j:2 * j + 2], first, ds.reshape(2, Q_TILE, K_WIN))
            dsb = ds.astype(BF16)
            dq_ref[:, ls] = (_unstack_heads(_dot(dsb, k3, "nn")) * ATTN_SCALE).astype(BF16)
            dk_acc[pl.ds(start, K_WIN), ls] += _dot(dsb, q_st, "tn")
            dv_acc[pl.ds(start, K_WIN), ls] += _dot(p, do_st, "tn")

        @pl.when(t == nt - 1)
        def _():
            dk_ref[...] = dk_acc[pl.ds(PAD_ROWS, s), :].astype(BF16)
            dv_ref[...] = dv_acc[pl.ds(PAD_ROWS, s), :].astype(BF16)

    specs = _attn_specs(d_model) + [pl.BlockSpec((Q_TILE, ATTN_LANES), lambda g, t: (t, g))]
    col_spec = pl.BlockSpec((s, ATTN_LANES), lambda g, t: (0, g))
    return _call(
        body, name=name, grid=(d_model // ATTN_LANES, nt), in_specs=specs,
        out_specs=[pl.BlockSpec((Q_TILE, ATTN_LANES), lambda g, t: (t, g)), col_spec, col_spec,
                   pl.BlockSpec((2 * ATTN_PAIRS, Q_TILE, K_WIN), lambda g, t: (g, 0, 0))],
        out_shape=[jax.ShapeDtypeStruct((s, d_model), BF16)] * 3
        + [jax.ShapeDtypeStruct((N_HEADS, Q_TILE, K_WIN), F32)],
        scratch_shapes=[pltpu.VMEM((PAD_ROWS + s, ATTN_LANES), F32), pltpu.VMEM((PAD_ROWS + s, ATTN_LANES), F32)],
        compiler_params=_cp(),
    )(qkvp, *([qkvp] * (2 * nw)), bias, do)


def bias_grad_reduce(name, ds_sum):
    nh = ds_sum.shape[0]
    width = SHEAR_W + Q_TILE
    first_k = Q_TILE - 1

    def body(x_ref, col_ref, sat_ref):
        x = x_ref[...]
        hi = x.astype(BF16)
        lo = (x - hi.astype(F32)).astype(BF16)
        r = lax.broadcasted_iota(jnp.int32, (Q_TILE, Q_TILE), 0)
        c = lax.broadcasted_iota(jnp.int32, (Q_TILE, Q_TILE), 1)
        exchange = jnp.where(r + c == Q_TILE - 1, 1.0, 0.0).astype(BF16)
        x_rev = _dot(exchange, hi, "nn") + _dot(exchange, lo, "nn")
        zeros = jnp.zeros((Q_TILE, Q_TILE), F32)
        y = pltpu.roll(jnp.concatenate([zeros, x_rev, zeros], axis=1), 0, 1, stride=1, stride_axis=0)
        cols = _colsum(y)
        col_ref[...] = cols
        k = lax.broadcasted_iota(jnp.int32, cols.shape, 1) - first_k
        tot = jnp.sum(jnp.where((k >= 1) & (k <= SHEAR_SAT), cols, 0.0), axis=-1, keepdims=True)
        sat_ref[...] = jnp.broadcast_to(tot, sat_ref.shape)

    return _call(
        body, name=name, grid=(nh,),
        in_specs=[pl.BlockSpec((None, Q_TILE, K_WIN), lambda hh: (hh, 0, 0))],
        out_specs=[pl.BlockSpec((None, 1, width), lambda hh: (hh, 0, 0)),
                   pl.BlockSpec((None, 1, LANES), lambda hh: (hh, 0, 0))],
        out_shape=[jax.ShapeDtypeStruct((nh, 1, width), F32), jax.ShapeDtypeStruct((nh, 1, LANES), F32)],
        compiler_params=_cp(),
    )(ds_sum)


def _ew_rows(r, most=512, cols=None):
    if cols is not None and r * cols * 4 <= SMALL_BLOCK_BYTES:
        return r
    for cand in range(min(most, r) // 16 * 16, 0, -16):
        if r % cand == 0:
            return cand
    return r


def to_bf16(name, a):
    s, d = a.shape
    tm = _row_tile(s)

    def body(a_ref, o_ref):
        o_ref[...] = a_ref[...].astype(BF16)

    return _call(
        body, name=name, grid=(s // tm,), in_specs=[_row(tm, d)], out_specs=_row(tm, d),
        out_shape=jax.ShapeDtypeStruct((s, d), BF16), compiler_params=_cp(),
    )(a)


def cast_into_gathered(name, w, layer, s_idx, n_blocks=N_SHARD, dtype=BF16, token=None):
    r, c = w.shape[-2:]
    tr = _ew_rows(r, cols=c)

    def body(s_ref, w_ref, *rest):
        rest[-1][...] = w_ref[...].astype(dtype)

    extra = [] if token is None else [token]
    grid_spec = pltpu.PrefetchScalarGridSpec(
        num_scalar_prefetch=1, grid=(r // tr,),
        in_specs=[pl.BlockSpec((None, tr, c), lambda i, s_ref: (layer, i, 0))] + [ANY_SPEC] * len(extra),
        out_specs=pl.BlockSpec((None, tr, c), lambda i, s_ref: (s_ref[0], i, 0)))
    return _call(
        body, name=name, grid_spec=grid_spec, out_shape=jax.ShapeDtypeStruct((n_blocks, r, c), dtype),
        compiler_params=_cp(),
    )(s_idx, w, *extra)


def adamw(name, w, grads, m, v, token=None):
    nl, r, c = w.shape
    tr = _ew_rows(r, 256, cols=c)

    def body(*refs):
        w_ref, m_ref, v_ref = refs[0], refs[1], refs[2]
        g_refs = refs[3:3 + nl]
        d_ref, nm_ref, nv_ref = refs[-3:]
        layer = pl.program_id(0)
        g = g_refs[0][...]
        for j in range(1, nl):
            g = jnp.where(layer == j, g_refs[j][...], g)
        nm = ADAM_B1 * m_ref[...] + (1.0 - ADAM_B1) * g
        nv = ADAM_B2 * v_ref[...] + (1.0 - ADAM_B2) * (g * g)
        m_hat = nm / ADAM_BC1
        v_hat = nv / ADAM_BC2
        d_ref[...] = -ADAM_LR * (m_hat / (jnp.sqrt(v_hat) + ADAM_EPS) + ADAM_WD * w_ref[...])
        nm_ref[...] = nm
        nv_ref[...] = nv

    p_spec = pl.BlockSpec((None, tr, c), lambda l, i: (l, i, 0))
    g_spec = pl.BlockSpec((tr, c), lambda l, i: (i, 0))
    extra = [] if token is None else [token]
    extra_specs = [] if token is None else [ANY_SPEC]
    return _call(
        body, name=name, grid=(nl, r // tr), in_specs=[p_spec] * 3 + [g_spec] * nl + extra_specs,
        out_specs=[p_spec] * 3, out_shape=[jax.ShapeDtypeStruct((nl, r, c), F32)] * 3, compiler_params=_cp(),
    )(w, m, v, *grads, *extra)


def sum_blocks(name, gathered, n_blocks):
    r = gathered.shape[0] // n_blocks
    c = gathered.shape[1]
    tr = r if r <= SUM_BLOCK_ROWS else _ew_rows(r)
    nt = r // tr

    def body(*refs):
        acc = refs[0][...]
        for j in range(1, n_blocks):
            acc = acc + refs[j][...]
        refs[-1][...] = acc

    specs = [pl.BlockSpec((tr, c), lambda i, j=j: (j * nt + i, 0)) for j in range(n_blocks)]
    return _call(
        body, name=name, grid=(nt,), in_specs=specs, out_specs=pl.BlockSpec((tr, c), lambda i: (i, 0)),
        out_shape=jax.ShapeDtypeStruct((r, c), F32), compiler_params=_cp(),
    )(*([gathered] * n_blocks))


def _place():
    return lax.axis_index("x"), lax.axis_index("y"), lax.axis_index("c")


def _other_chips(x, y):
    return [(1 - x, y), (x, 1 - y), (1 - x, 1 - y)]


HBM_SPEC = pl.BlockSpec(memory_space=pltpu.HBM)
SEM_SPEC = pl.BlockSpec(memory_space=pltpu.SEMAPHORE)
ANY_SPEC = pl.BlockSpec(memory_space=pl.ANY)
EFFECT = pltpu.SideEffectType.DATAFLOW_SIDE_EFFECTING


def copies_start(name, bufs, plan, n_copies):
    n = len(bufs)

    def body(*refs):
        send, recv = refs[n], refs[n + 1]
        token = refs[2 * n + 2]
        for k, (src, dst, peer, _) in enumerate(plan(refs[:n])):
            pltpu.make_async_remote_copy(
                src_ref=src, dst_ref=dst, send_sem=send.at[k], recv_sem=recv.at[k],
                device_id=peer, device_id_type=MESH).start()
        token[...] = jnp.zeros(token.shape, F32)

    outs = pl.pallas_call(
        body, name=name,
        out_shape=(pltpu.SemaphoreType.DMA((n_copies,)), pltpu.SemaphoreType.DMA((n_copies,)),
                   *[pltpu.HBM(b.shape, b.dtype) for b in bufs], jax.ShapeDtypeStruct((8, LANES), F32)),
        in_specs=[HBM_SPEC] * n,
        out_specs=(SEM_SPEC, SEM_SPEC, *([HBM_SPEC] * n), pl.BlockSpec(memory_space=pltpu.VMEM)),
        input_output_aliases={a: a + 2 for a in range(n)},
        compiler_params=pltpu.CompilerParams(has_side_effects=EFFECT),
    )(*[_in_hbm(b) for b in bufs])
    return outs[0], outs[1], list(outs[2:2 + n]), outs[2 + n]


def copies_wait(name, bufs, send, recv, plan, sem_base, after):
    n = len(bufs)

    def body(*refs):
        send_ref, recv_ref = refs[n], refs[n + 1]
        for k, (src, _, peer, land) in enumerate(plan(refs[:n])):
            cp = pltpu.make_async_remote_copy(
                src_ref=src, dst_ref=land, send_sem=send_ref.at[sem_base + k], recv_sem=recv_ref.at[sem_base + k],
                device_id=peer, device_id_type=MESH)
            cp.wait_send()
            cp.wait_recv()

    outs = pl.pallas_call(
        body, name=name,
        out_shape=tuple(pltpu.HBM(b.shape, b.dtype) for b in bufs),
        in_specs=[HBM_SPEC] * n + [SEM_SPEC, SEM_SPEC, ANY_SPEC], out_specs=tuple([HBM_SPEC] * n),
        input_output_aliases={a: a for a in range(n)},
        compiler_params=pltpu.CompilerParams(has_side_effects=EFFECT),
    )(*bufs, send, recv, after)
    return list(outs)


def gather_plan(refs):
    x, y, c = _place()
    me = 2 * x + y
    return [(buf.at[me], buf.at[me], (cx, cy, c), buf.at[2 * cx + cy])
            for buf in refs for cx, cy in _other_chips(x, y)]


def all_plan(refs):
    x, y, c = _place()
    me = 4 * x + 2 * y + c
    out = []
    for buf in refs:
        for flip in range(1, 8):
            px = 1 - x if flip & 4 else x
            py = 1 - y if flip & 2 else y
            pc = 1 - c if flip & 1 else c
            out.append((buf.at[me], buf.at[me], (px, py, pc), buf.at[4 * px + 2 * py + pc]))
    return out


def swap_plan(refs):
    x, y, c = _place()
    n = len(refs) // 2
    out = []
    for g, land in zip(refs[:n], refs[n:]):
        hr = g.shape[1] // 2
        out.append((g.at[:, pl.ds((1 - c) * hr, hr)], land, (x, y, 1 - c), land))
    return out


def owners_plan(refs):
    x, y, c = _place()
    n = len(refs) // 2
    return [(src.at[2 * cx + cy], land.at[j], (cx, cy, c), land.at[j])
            for src, land in zip(refs[:n], refs[n:]) for j, (cx, cy) in enumerate(_other_chips(x, y))]


def join_plan(refs):
    x, y, c = _place()
    out = []
    for buf in refs:
        hr = buf.shape[0] // 2
        mine = buf.at[pl.ds(c * hr, hr)]
        out.append((mine, mine, (x, y, 1 - c), buf.at[pl.ds((1 - c) * hr, hr)]))
    return out


def add_halves(name, grad, landed, sc_idx):
    _, r, c = grad.shape
    hr = r // 2
    tr = _ew_rows(hr)
    nt = hr // tr

    def body(sc_ref, g_ref, l_ref, own_ref, wire_ref):
        tot = g_ref[...] + l_ref[...]
        wire_ref[...] = tot.astype(BF16)

        @pl.when(pl.program_id(1) == sc_ref[0])
        def _():
            own_ref[...] = tot

    grid_spec = pltpu.PrefetchScalarGridSpec(
        num_scalar_prefetch=1, grid=(nt, N_SHARD),
        in_specs=[pl.BlockSpec((None, tr, c), lambda i, sh, sc_ref: (sh, sc_ref[1] * nt + i, 0)),
                  pl.BlockSpec((None, tr, c), lambda i, sh, sc_ref: (sh, i, 0))],
        out_specs=[pl.BlockSpec((tr, c), lambda i, sh, sc_ref: (i, 0)),
                   pl.BlockSpec((None, tr, c), lambda i, sh, sc_ref: (sh, i, 0))])
    return _call(
        body, name=name, grid_spec=grid_spec,
        out_shape=[jax.ShapeDtypeStruct((hr, c), F32), jax.ShapeDtypeStruct((N_SHARD, hr, c), BF16)],
        compiler_params=_cp(),
    )(sc_idx, grad, landed)


def add_owned(name, own, landed, sc_idx):
    hr, c = own.shape
    tr = _ew_rows(hr)
    nt = hr // tr

    def body(sc_ref, o_ref, l0, l1, l2, out_ref):
        out_ref[...] = ((o_ref[...] + l0[...].astype(F32)) + l1[...].astype(F32)) + l2[...].astype(F32)

    grid_spec = pltpu.PrefetchScalarGridSpec(
        num_scalar_prefetch=1, grid=(nt,),
        in_specs=[pl.BlockSpec((tr, c), lambda i, sc_ref: (i, 0))]
        + [pl.BlockSpec((None, tr, c), lambda i, sc_ref, j=j: (j, i, 0)) for j in range(3)],
        out_specs=pl.BlockSpec((tr, c), lambda i, sc_ref: (sc_ref[1] * nt + i, 0)))
    return _call(
        body, name=name, grid_spec=grid_spec, out_shape=jax.ShapeDtypeStruct((2 * hr, c), F32),
        compiler_params=_cp(),
    )(sc_idx, own, landed, landed, landed)


PACK_QUANTUM = 8 * LANES


def _pack(arrays):
    pieces = []
    for a in arrays:
        flat = a.reshape(-1)
        padded = -(-flat.shape[0] // PACK_QUANTUM) * PACK_QUANTUM
        pieces.append(jnp.pad(flat, (0, padded - flat.shape[0])).reshape(-1, LANES))
    return jnp.concatenate(pieces, axis=0)


def _unpack(packed, shapes):
    out = []
    row = 0
    for shp in shapes:
        size = math.prod(shp)
        rows = -(-size // PACK_QUANTUM) * 8
        out.append(packed[row:row + rows].reshape(-1)[:size].reshape(shp))
        row += rows
    return out


def kernel(x, p, mix_w_in, pool_w, pool_scale, conv_dw_w, conv_dw_b, conv_ln_g, conv_ln_b, mix_w_out, attn_w_qkv, attn_rel_bias, attn_w_o, ln_mix_g, ln_mix_b, ffn_w_up, ffn_dw_w, ffn_dw_b, ffn_w_down, ple_w_proj, ple_w_gate, ple_b_gate, ln_ffn_g, ln_ffn_b, loss_target, m_mix_w_in, m_pool_w, m_pool_scale, m_conv_dw_w, m_conv_dw_b, m_conv_ln_g, m_conv_ln_b, m_mix_w_out, m_attn_w_qkv, m_attn_rel_bias, m_attn_w_o, m_ln_mix_g, m_ln_mix_b, m_ffn_w_up, m_ffn_dw_w, m_ffn_dw_b, m_ffn_w_down, m_ple_w_proj, m_ple_w_gate, m_ple_b_gate, m_ln_ffn_g, m_ln_ffn_b, v_mix_w_in, v_pool_w, v_pool_scale, v_conv_dw_w, v_conv_dw_b, v_conv_ln_g, v_conv_ln_b, v_mix_w_out, v_attn_w_qkv, v_attn_rel_bias, v_attn_w_o, v_ln_mix_g, v_ln_mix_b, v_ffn_w_up, v_ffn_dw_w, v_ffn_dw_b, v_ffn_w_down, v_ple_w_proj, v_ple_w_gate, v_ple_b_gate, v_ln_ffn_g, v_ln_ffn_b):
    xi, yi, ci = _place()
    shard_idx = (2 * xi + yi).astype(jnp.int32)
    s_arr = shard_idx.reshape(1)
    c_arr = ci.astype(jnp.int32).reshape(1)
    sc_arr = jnp.concatenate([s_arr, c_arr])

    x0 = x[0]
    target = loss_target[0]
    p_rows = p.reshape(p.shape[0] * p.shape[2], p.shape[3])
    seq = x0.shape[0]

    big = [
        ("mix_w_in", mix_w_in, m_mix_w_in, v_mix_w_in, True),
        ("mix_w_out", mix_w_out, m_mix_w_out, v_mix_w_out, False),
        ("attn_w_qkv", attn_w_qkv, m_attn_w_qkv, v_attn_w_qkv, True),
        ("attn_w_o", attn_w_o, m_attn_w_o, v_attn_w_o, False),
        ("ffn_w_up", ffn_w_up, m_ffn_w_up, v_ffn_w_up, True),
        ("ffn_w_down", ffn_w_down, m_ffn_w_down, v_ffn_w_down, False),
        ("ple_w_proj", ple_w_proj, m_ple_w_proj, v_ple_w_proj, True),
        ("ple_w_gate", ple_w_gate, m_ple_w_gate, v_ple_w_gate, False),
    ]
    params = {nm: w for nm, w, _, _, _ in big}
    col_sharded = {nm: cs for nm, _, _, _, cs in big}
    keys = [("mix_w_in", 0), ("mix_w_out", 0), ("ffn_w_up", 0), ("ffn_w_down", 0), ("ple_w_gate", 0),
            ("ple_w_proj", 0), ("attn_w_qkv", 0), ("attn_w_o", 0), ("ffn_w_up", 1), ("ffn_w_down", 1),
            ("ple_w_gate", 1), ("ple_w_proj", 1)]
    dw_shapes = [conv_dw_w.shape, ffn_dw_w.shape]
    dw_block = cast_into_gathered("place_dw", _pack([conv_dw_w, ffn_dw_w])[None], 0, s_arr, dtype=F32)
    n_first = 2
    started = {}
    gather_token = None
    for tag, group in (("first", keys[:n_first]), ("rest", keys[n_first:])):
        shards = [cast_into_gathered(f"cast_{nm}_{layer}", params[nm], layer, s_arr, token=gather_token)
                  for nm, layer in group]
        if tag == "first":
            shards.append(dw_block)
        send, recv, bufs, gather_token = copies_start(f"gather_start_{tag}", shards, gather_plan, 3 * len(shards))
        for a, key in enumerate(group):
            started[key] = (send, recv, bufs[a], 3 * a)
        if tag == "first":
            dw_started = (send, recv, bufs[-1], 3 * len(group))
    arrived_w = {}

    def weight(nm, layer, after=None):
        key = (nm, layer)
        if key not in arrived_w:
            send, recv, buf, base = started[key]
            arrived_w[key] = copies_wait(f"gather_wait_{nm}_{layer}", [buf], send, recv, gather_plan, base, after)[0]
        g = arrived_w[key]
        if col_sharded[nm]:
            return g
        return g.reshape(g.shape[0] * g.shape[1], g.shape[2])

    def tie(a, token):
        return a + token[0:1, 0:1].astype(a.dtype)

    class Reducer:
        def __init__(self, tag, group):
            self.tag, self.group, self.stage = tag, group, 0
            self.n = len(group)
            self.result = None

        def advance(self, after):
            tag, n = self.tag, self.n
            if self.stage == 0:
                grads = []
                for key in self.group:
                    g = big_grads[key]
                    grads.append(g if g.ndim == 3 else g.reshape(N_SHARD, g.shape[0] // N_SHARD, g.shape[1]))
                lands = [lax.empty((N_SHARD, g.shape[1] // 2, g.shape[2]), F32) for g in grads]
                self.sems = copies_start(f"swap_start_{tag}", grads + lands, swap_plan, n)
            elif self.stage == 1:
                send, recv, bufs, _ = self.sems
                outs = copies_wait(f"swap_wait_{tag}", bufs, send, recv, swap_plan, 0, after)
                self.own, wire = [], []
                for key, g, ld in zip(self.group, outs[:n], outs[n:]):
                    o, ob = add_halves(f"add_halves_{key[0]}_{key[1]}", g, ld, sc_arr)
                    self.own.append(o)
                    wire.append(ob)
                lands = [lax.empty((3,) + w.shape[1:], BF16) for w in wire]
                self.sems = copies_start(f"owners_start_{tag}", wire + lands, owners_plan, 3 * n)
            elif self.stage == 2:
                send, recv, bufs, _ = self.sems
                outs = copies_wait(f"owners_wait_{tag}", bufs, send, recv, owners_plan, 0, after)
                finals = [add_owned(f"add_owned_{key[0]}_{key[1]}", o, ar, sc_arr)
                          for key, o, ar in zip(self.group, self.own, outs[n:])]
                self.sems = copies_start(f"join_start_{tag}", finals, join_plan, n)
            elif self.stage == 3:
                send, recv, bufs, _ = self.sems
                outs = copies_wait(f"join_wait_{tag}", bufs, send, recv, join_plan, 0, after)
                self.result = dict(zip(self.group, outs))
                self.sems = None
            self.stage += 1
            return None if self.sems is None else self.sems[3]

    dw_cache = []

    def conv_weights(after):
        if not dw_cache:
            send, recv, buf, base = dw_started
            dw_all = copies_wait("gather_wait_dw", [buf], send, recv, gather_plan, base, after)[0]
            dw_parts = [_unpack(dw_all[k], dw_shapes) for k in range(N_SHARD)]
            dw_cache.append(jnp.concatenate([pc[0] for pc in dw_parts], axis=2)[0])
            dw_cache.append(jnp.concatenate([pc[1] for pc in dw_parts], axis=2))
        return dw_cache

    big_grads = {}
    small_grads = {}

    saved = []
    h_in = x0
    h_in_b = to_bf16("x_bf16", x0)
    for layer in range(N_LAYERS):
        sv = {"x_in": h_in_b}
        if layer % 2 == 0:
            u = mm_cols_fwd("mix_in", h_in_b, weight("mix_w_in", 0, gather_token), F32)
            conv_w_full, ffn_dw_full = conv_weights(u)
            cat, d_sv, e_sv, glu_sv, hh_sv, rs_sv = mixer_fwd(
                "mixer_fwd", u, pool_w[0], pool_scale, conv_w_full, conv_dw_b, conv_ln_g, conv_ln_b)
            mix = mm_rows_fwd("mix_out", cat, weight("mix_w_out", 0, cat))
            sv.update(u=u, cat=cat, d=d_sv, e=e_sv, glu=glu_sv, hh=hh_sv, rs=rs_sv)
        else:
            qkvp = mm_cols_fwd("attn_qkv", h_in_b, weight("attn_w_qkv", 0, h_in_b), BF16,
                               pad_blocks=PAD_ROWS // _row_tile(seq))
            bias = bias_tile("bias_tile", _bias_line(attn_rel_bias[0]))
            att = attn_fwd("attn_fwd", qkvp, bias)
            mix = mm_rows_fwd("attn_out", att, weight("attn_w_o", 0, att))
            sv.update(qkvp=qkvp, bias=bias, att=att)
        x1, x1_b, xh1, rs1 = ln_fwd(f"ln_mix_{layer}", h_in, mix, ln_mix_g[layer:layer + 1],
                                    ln_mix_b[layer:layer + 1])
        gv = mm_cols_fwd(f"ffn_up_{layer}", x1_b, weight("ffn_w_up", layer, x1_b), F32)
        hid = ffn_act_fwd(f"ffn_act_{layer}", gv, ffn_dw_full[layer], ffn_dw_b[layer:layer + 1])
        ffn = mm_rows_fwd(f"ffn_down_{layer}", hid, weight("ffn_w_down", layer, hid))
        pgl = mm_rows_fwd(f"ple_gate_{layer}", x1_b, weight("ple_w_gate", layer, ffn))
        pp = mm_cols_fwd(f"ple_proj_{layer}", p_rows, weight("ple_w_proj", layer, pgl), F32, part=(layer, N_LAYERS))
        bg = ple_b_gate[layer:layer + 1]
        x2, x2_b, xh2, rs2 = ln_fwd(f"ln_ffn_{layer}", x1, ffn, ln_ffn_g[layer:layer + 1], ln_ffn_b[layer:layer + 1],
                                    ple=(pgl, pp, bg), emit_y=layer < N_LAYERS - 1)
        sv.update(x1=x1_b, xh1=xh1, rs1=rs1, gv=gv, hid=hid, pgl=pgl, pp=pp, xh2=xh2, rs2=rs2)
        saved.append(sv)
        h_in, h_in_b = x2, x2_b

    reducers = []

    def open_group(tag, group):
        reducers.append(Reducer(tag, group))
        return reducers[-1].advance(None)

    def hook(after):
        token = None
        for red in reducers:
            if red.stage < 4:
                tk = red.advance(after)
                if tk is not None:
                    token = tk if token is None else token + tk
        return token

    def tied(a, token):
        return a if token is None else tie(a, token)

    parts = []
    token = None
    for layer in reversed(range(N_LAYERS)):
        sv = saved[layer]
        bg = ple_b_gate[layer:layer + 1]
        if layer == 0:
            token = open_group("layer1", [("attn_w_qkv", 0), ("attn_w_o", 0), ("ffn_w_up", 1), ("ffn_w_down", 1),
                                          ("ple_w_gate", 1), ("ple_w_proj", 1)])
        last = layer == N_LAYERS - 1
        res = ln_bwd(
            f"ln_ffn_bwd_{layer}", parts, sv["xh2"], sv["rs2"], tied(ln_ffn_g[layer:layer + 1], token),
            ple=(sv["pgl"], sv["pp"], bg), loss=(target, ln_ffn_b[layer:layer + 1]) if last else None)
        dz2, dg2, db2, dpp, dpgl, dbg = res[:6]
        if last:
            loss_part = res[6]
        small_grads[("ln_ffn_g", layer)] = dg2
        small_grads[("ln_ffn_b", layer)] = db2
        small_grads[("ple_b_gate", layer)] = dbg
        w_down = weight("ffn_w_down", layer)
        dhid = mm_rows_dx(f"ffn_down_dx_{layer}", dz2, w_down)
        big_grads[("ffn_w_down", layer)] = mm_rows_dw(f"ffn_down_dw_{layer}", sv["hid"], dz2)
        token = hook(big_grads[("ffn_w_down", layer)])
        dgv, ddw, ddb = ffn_act_bwd(f"ffn_act_bwd_{layer}", dhid, sv["gv"], ffn_dw_full[layer],
                                    tied(ffn_dw_b[layer:layer + 1], token))
        small_grads[("ffn_dw_w", layer)] = ddw
        small_grads[("ffn_dw_b", layer)] = ddb
        big_grads[("ffn_w_up", layer)] = mm_cols_dw(f"ffn_up_dw_{layer}", sv["x1"], dgv)
        t_up = mm_cols_dx(f"ffn_up_dx_{layer}", dgv, weight("ffn_w_up", layer))
        token = hook(t_up)
        big_grads[("ple_w_gate", layer)] = mm_rows_dw(f"ple_gate_dw_{layer}", sv["x1"], dpgl)
        t_gate = mm_rows_dx(f"ple_gate_dx_{layer}", dpgl, weight("ple_w_gate", layer))
        big_grads[("ple_w_proj", layer)] = mm_cols_dw(f"ple_proj_dw_{layer}", p_rows, dpp, part=(layer, N_LAYERS))
        token2 = hook(big_grads[("ple_w_proj", layer)])
        if token2 is not None:
            token = token2 if token is None else token + token2
        if layer == 0:
            token3 = open_group("layer0_ffn", [("ffn_w_up", 0), ("ffn_w_down", 0), ("ple_w_gate", 0), ("ple_w_proj", 0)])
            token = token3 if token is None else token + token3
        dz1, dg1, db1 = ln_bwd(
            f"ln_mix_bwd_{layer}", [(ALPHA, dz2), (1.0, t_up), (1.0, t_gate)], sv["xh1"], sv["rs1"],
            tied(ln_mix_g[layer:layer + 1], token))
        small_grads[("ln_mix_g", layer)] = dg1
        small_grads[("ln_mix_b", layer)] = db1
        if layer % 2 == 0:
            dcat = mm_rows_dx("mix_out_dx", dz1, weight("mix_w_out", 0))
            big_grads[("mix_w_out", 0)] = mm_rows_dw("mix_out_dw", sv["cat"], dz1)
            token = hook(big_grads[("mix_w_out", 0)])
            du, dpw, dps, dcw, dcb, dcg, dcbt = mixer_bwd(
                "mixer_bwd", dcat, sv["u"], sv["d"], sv["e"], sv["glu"], sv["hh"], sv["rs"],
                pool_w[0], pool_scale, conv_w_full, tied(conv_ln_g, token), conv_ln_b)
            small_grads[("pool_w", 0)] = dpw
            small_grads[("pool_scale", 0)] = dps
            small_grads[("conv_dw_w", 0)] = dcw
            small_grads[("conv_dw_b", 0)] = dcb
            small_grads[("conv_ln_g", 0)] = dcg
            small_grads[("conv_ln_b", 0)] = dcbt
            big_grads[("mix_w_in", 0)] = mm_cols_dw("mix_in_dw", sv["x_in"], du)
            hook(big_grads[("mix_w_in", 0)])
            open_group("layer0_mix", [("mix_w_in", 0), ("mix_w_out", 0)])
            dx_in = mm_cols_dx("mix_in_dx", du, weight("mix_w_in", 0), addend=(ALPHA, dz1))
            token = hook(dx_in)
        else:
            do = mm_rows_dx("attn_out_dx", dz1, weight("attn_w_o", 0), out_dtype=BF16)
            big_grads[("attn_w_o", 0)] = mm_rows_dw("attn_out_dw", sv["att"], dz1)
            dq, dk, dv, ds_sum = attn_bwd("attn_bwd", sv["qkvp"], sv["bias"], do)
            cols, sat = bias_grad_reduce("bias_grad", ds_sum)
            d_rel = jnp.concatenate(
                [jnp.zeros((N_HEADS, 1), F32),
                 jnp.flip(cols[:, 0, Q_TILE + SHEAR_SAT:Q_TILE - 1 + SHEAR_W], axis=1),
                 sat[:, 0, 0:1]], axis=1)
            small_grads[("attn_rel_bias", 0)] = d_rel
            dqkv = jnp.concatenate([dq, dk, dv], axis=1)
            big_grads[("attn_w_qkv", 0)] = mm_cols_dw("attn_qkv_dw", sv["x_in"], dqkv)
            dx_in = mm_cols_dx("attn_qkv_dx", dqkv, weight("attn_w_qkv", 0), addend=(ALPHA, dz1))
        parts = [(1.0, dx_in)]
    grad_x = dx_in

    small = [
        ("pool_w", pool_w, m_pool_w, v_pool_w, None),
        ("pool_scale", pool_scale, m_pool_scale, v_pool_scale, None),
        ("conv_dw_w", conv_dw_w, m_conv_dw_w, v_conv_dw_w, 2),
        ("conv_dw_b", conv_dw_b, m_conv_dw_b, v_conv_dw_b, None),
        ("conv_ln_g", conv_ln_g, m_conv_ln_g, v_conv_ln_g, None),
        ("conv_ln_b", conv_ln_b, m_conv_ln_b, v_conv_ln_b, None),
        ("attn_rel_bias", attn_rel_bias, m_attn_rel_bias, v_attn_rel_bias, None),
        ("ln_mix_g", ln_mix_g, m_ln_mix_g, v_ln_mix_g, None),
        ("ln_mix_b", ln_mix_b, m_ln_mix_b, v_ln_mix_b, None),
        ("ffn_dw_w", ffn_dw_w, m_ffn_dw_w, v_ffn_dw_w, 2),
        ("ffn_dw_b", ffn_dw_b, m_ffn_dw_b, v_ffn_dw_b, None),
        ("ple_b_gate", ple_b_gate, m_ple_b_gate, v_ple_b_gate, None),
        ("ln_ffn_g", ln_ffn_g, m_ln_ffn_g, v_ln_ffn_g, None),
        ("ln_ffn_b", ln_ffn_b, m_ln_ffn_b, v_ln_ffn_b, None),
    ]
    full_grads = []
    for nm, w, _, _, shard_axis in small:
        full = list(w.shape)
        if shard_axis is not None:
            full[shard_axis] *= N_SHARD
        per_layer = [small_grads[(nm, layer)].reshape((1,) + tuple(full[1:])) for layer in range(w.shape[0])]
        full_grads.append(jnp.concatenate(per_layer, axis=0))
    packed = _pack(full_grads + [loss_part])
    dev_arr = (4 * xi + 2 * yi + ci).astype(jnp.int32).reshape(1)
    sg_block = cast_into_gathered("place_small_grads", packed[None], 0, dev_arr, n_blocks=8, dtype=F32)
    sg_send, sg_recv, sg_bufs, sg_token = copies_start("small_grads_start", [sg_block], all_plan, 7)
    token = sg_token if token is None else token + sg_token

    shard_grads = {}
    for red in reducers:
        if red.stage == 4:
            shard_grads.update(red.result)
    big_out = {}

    def update_big(names, tok):
        for nm, w, m, v, _ in big:
            if nm in names:
                gl = [shard_grads[(nm, layer)] for layer in range(w.shape[0])]
                delta, new_m, new_v = adamw(f"adamw_{nm}", w, gl, m, v, token=tok)
                big_out[nm] = (jnp.stack(gl, axis=0), delta, new_m, new_v)

    last_group = ("mix_w_in", "mix_w_out")
    update_big([nm for nm, _, _, _, _ in big if nm not in last_group], token)
    token = hook(big_out["ffn_w_up"][1])

    gathered_sg = copies_wait("small_grads_wait", sg_bufs, sg_send, sg_recv, all_plan, 0, big_out["ffn_w_down"][1])[0]
    total = sum_blocks("sum_small", gathered_sg.reshape(8 * packed.shape[0], LANES), 8)
    unpacked = _unpack(total, [g.shape for g in full_grads] + [loss_part.shape])
    loss = unpacked[-1][0, 0]
    local_grads = []
    for (nm, w, _, _, shard_axis), g in zip(small, unpacked[:-1]):
        if shard_axis is not None:
            width = w.shape[shard_axis]
            g = lax.dynamic_slice_in_dim(g, shard_idx * width, width, axis=shard_axis)
        local_grads.append(g.reshape(w.shape))
    shapes = [w.shape for _, w, _, _, _ in small]
    pg = _pack(local_grads)
    pw = _pack([w for _, w, _, _, _ in small])
    pm = _pack([m for _, _, m, _, _ in small])
    pv = _pack([v for _, _, _, v, _ in small])
    delta_s, new_m_s, new_v_s = adamw("adamw_small", pw[None], [pg], pm[None], pv[None], token=token)
    hook(delta_s)
    for red in reducers:
        shard_grads.update(red.result)
    update_big(last_group, None)
    small_out = {}
    for (nm, _, _, _, _), g, d_, m_, v_ in zip(
            small, local_grads, _unpack(delta_s[0], shapes), _unpack(new_m_s[0], shapes), _unpack(new_v_s[0], shapes)):
        small_out[nm] = (g, d_, m_, v_)

    order = ["mix_w_in", "pool_w", "pool_scale", "conv_dw_w", "conv_dw_b", "conv_ln_g", "conv_ln_b", "mix_w_out",
             "attn_w_qkv", "attn_rel_bias", "attn_w_o", "ln_mix_g", "ln_mix_b", "ffn_w_up", "ffn_dw_w", "ffn_dw_b",
             "ffn_w_down", "ple_w_proj", "ple_w_gate", "ple_b_gate", "ln_ffn_g", "ln_ffn_b"]
    res = {**big_out, **small_out}
    outs = [loss, grad_x[None]]
    for slot in range(4):
        outs += [res[nm][slot] for nm in order]
    return tuple(outs)
```

```python
import functools
import math

import jax
import jax.numpy as jnp
from jax import lax
from jax.experimental import pallas as pl
from jax.experimental.pallas import tpu as pltpu

F32 = jnp.float32
BF16 = jnp.bfloat16
MESH = pl.DeviceIdType.MESH

N_LAYERS = 2
ALPHA = (2 * N_LAYERS) ** 0.25
LN_EPS = 1e-5
NEG_INF = -1e30
CHUNK = 64
LEFT_CHUNKS = 8
PAD_ROWS = LEFT_CHUNKS * CHUNK
HEAD_DIM = 64
ATTN_SCALE = HEAD_DIM ** -0.5
N_HEADS = 16
MAX_REL = 256
POOL_WINDOWS = (2, 4, 8, 16)
POOL_GROUP = 128
CONV_K = 31
FFN_K = 3
CONV_HALO = 32
FFN_HALO = 8
FFN_TILE = 256
FFN_CHUNK_ROWS = 64
FFN_CHUNK_LANES = 128
Q_TILE = 256
K_WIN = Q_TILE + PAD_ROWS
LANES = 128
SUBLANES = 8
ATTN_PAIRS = 2
ATTN_LANES = ATTN_PAIRS * LANES
SHEAR_W = Q_TILE + K_WIN
SHEAR_SAT = SHEAR_W - 2 * MAX_REL
N_SHARD = 4

ADAM_LR = 0.001
ADAM_B1 = 0.9
ADAM_B2 = 0.999
ADAM_EPS = 1e-08
ADAM_WD = 0.01
ADAM_STEP = 10
ADAM_BC1 = 1.0 - ADAM_B1 ** ADAM_STEP
ADAM_BC2 = 1.0 - ADAM_B2 ** ADAM_STEP

DIMS = {
    "nn": (((1,), (0,)), ((), ())),
    "nt": (((1,), (1,)), ((), ())),
    "tn": (((0,), (0,)), ((), ())),
}


def _cp(vmem_mb=48, **kw):
    return pltpu.CompilerParams(vmem_limit_bytes=vmem_mb * 1024 * 1024, **kw)


def _in_hbm(a):
    return pltpu.with_memory_space_constraint(a, pltpu.HBM)


STAGING_LIMIT_BYTES = 1 << 20
SUM_BLOCK_ROWS = 2048
SMALL_BLOCK_BYTES = 1 << 19


def _call(body, **kw):
    call = pl.pallas_call(body, **kw)

    def run(*args):
        pinned = []
        for a in args:
            big = a.size * a.dtype.itemsize >= STAGING_LIMIT_BYTES
            pinned.append(_in_hbm(a) if big and not jnp.issubdtype(a.dtype, jnp.integer) else a)
        return call(*pinned)

    return run


def _dot(a, b, mode):
    return lax.dot_general(a.astype(BF16), b.astype(BF16), DIMS[mode], preferred_element_type=F32)


def _sig(x):
    return 1.0 / (1.0 + jnp.exp(-x))


def _row_tile(s):
    return min(512, s // 4)


def _mm_tile(s):
    return min(1024, s // 4)


def _mm(name, mode, a, b, in_specs, out_shape, out_spec, acc_shape, grid, nk, zero_first=False, vmem_mb=48,
        addend=None):
    out_f32 = out_shape.dtype == F32

    def body(a_ref, b_ref, *rest):
        k = pl.program_id(2)
        if addend is None:
            o_ref, scr = rest[0], rest[1:]
        else:
            add_ref, o_ref, scr = rest[0], rest[1], rest[2:]

        def compute():
            part = _dot(a_ref[...], b_ref[...], mode)
            if nk == 1:
                if addend is not None:
                    part = part + addend[0] * add_ref[...]
                o_ref[...] = part.astype(o_ref.dtype)
                return
            acc = o_ref if out_f32 else scr[0]

            @pl.when(k == 0)
            def _():
                acc[...] = part if addend is None else part + addend[0] * add_ref[...]

            @pl.when(k > 0)
            def _():
                acc[...] += part

            if not out_f32:
                @pl.when(k == nk - 1)
                def _():
                    o_ref[...] = acc[...].astype(o_ref.dtype)

        if zero_first:
            @pl.when(pl.program_id(1) == 0)
            def _():
                o_ref[...] = jnp.zeros(o_ref.shape, o_ref.dtype)

            pl.when(pl.program_id(1) > 0)(compute)
        else:
            compute()

    scratch = [] if (nk == 1 or out_f32) else [pltpu.VMEM(acc_shape, F32)]
    operands = [a, b] if addend is None else [a, b, addend[1]]
    specs = list(in_specs) if addend is None else list(in_specs) + [out_spec]
    return _call(
        body, name=name, grid=grid, in_specs=specs, out_specs=out_spec, out_shape=out_shape,
        scratch_shapes=scratch, compiler_params=_cp(vmem_mb),
    )(*operands)


def mm_cols_fwd(name, a, wc, out_dtype, pad_blocks=0, part=(0, 1)):
    s, k = a.shape
    s //= part[1]
    n4 = wc.shape[2]
    tm = _row_tile(s) if pad_blocks else _mm_tile(s)
    nt = s // tm
    first_block = part[0] * nt
    return _mm(
        name, "nn", a, wc,
        [pl.BlockSpec((tm, k), lambda j, i, r: (first_block + jnp.maximum(i - pad_blocks, 0), 0)),
         pl.BlockSpec((None, k, n4), lambda j, i, r: (j, 0, 0))],
        jax.ShapeDtypeStruct((s + pad_blocks * tm, N_SHARD * n4), out_dtype),
        pl.BlockSpec((tm, n4), lambda j, i, r: (i, j)),
        None, (N_SHARD, nt + pad_blocks, 1), 1, zero_first=pad_blocks > 0)


def mm_cols_dx(name, dy, wc, addend=None):
    s = dy.shape[0]
    _, k, n4 = wc.shape
    tm = _mm_tile(s)
    return _mm(
        name, "nt", dy, wc,
        [pl.BlockSpec((tm, n4), lambda g, i, r: (i, r)),
         pl.BlockSpec((None, k, n4), lambda g, i, r: (r, 0, 0))],
        jax.ShapeDtypeStruct((s, k), F32),
        pl.BlockSpec((tm, k), lambda g, i, r: (i, 0)),
        (tm, k), (1, s // tm, N_SHARD), N_SHARD, addend=addend)


def mm_cols_dw(name, a, dy, part=(0, 1)):
    s, k = a.shape
    s //= part[1]
    n4 = dy.shape[1] // N_SHARD
    tm = _mm_tile(s)
    nt = s // tm
    first_block = part[0] * nt
    return _mm(
        name, "tn", a, dy,
        [pl.BlockSpec((tm, k), lambda j, g, r: (first_block + r, 0)),
         pl.BlockSpec((tm, n4), lambda j, g, r: (r, j))],
        jax.ShapeDtypeStruct((N_SHARD, k, n4), F32),
        pl.BlockSpec((None, k, n4), lambda j, g, r: (j, 0, 0)),
        (k, n4), (N_SHARD, 1, nt), nt)


def _k_tile(k):
    return k if k <= 1024 else k // 2


def mm_rows_fwd(name, a, wr, out_dtype=F32):
    s, k = a.shape
    n = wr.shape[1]
    tm = _mm_tile(s)
    tk = _k_tile(k)
    nk = k // tk
    return _mm(
        name, "nn", a, wr,
        [pl.BlockSpec((tm, tk), lambda g, i, r: (i, r)),
         pl.BlockSpec((tk, n), lambda g, i, r: (r, 0))],
        jax.ShapeDtypeStruct((s, n), out_dtype),
        pl.BlockSpec((tm, n), lambda g, i, r: (i, 0)),
        (tm, n), (1, s // tm, nk), nk)


def mm_rows_dx(name, dy, wr, out_dtype=F32):
    s, n = dy.shape
    k = wr.shape[0]
    tm = _mm_tile(s)
    tk = _k_tile(k)
    return _mm(
        name, "nt", dy, wr,
        [pl.BlockSpec((tm, n), lambda j, i, r: (i, 0)),
         pl.BlockSpec((tk, n), lambda j, i, r: (j, 0))],
        jax.ShapeDtypeStruct((s, k), out_dtype),
        pl.BlockSpec((tm, tk), lambda j, i, r: (i, j)),
        None, (k // tk, s // tm, 1), 1)


def mm_rows_dw(name, a, dy):
    s, k = a.shape
    n = dy.shape[1]
    tm = _mm_tile(s)
    tk = _k_tile(k)
    nt = s // tm
    return _mm(
        name, "tn", a, dy,
        [pl.BlockSpec((tm, tk), lambda j, g, r: (r, j)),
         pl.BlockSpec((tm, n), lambda j, g, r: (r, 0))],
        jax.ShapeDtypeStruct((k, n), F32),
        pl.BlockSpec((tk, n), lambda j, g, r: (j, 0)),
        (tk, n), (k // tk, 1, nt), nt)


def _row(tm, c, col=0):
    return pl.BlockSpec((tm, c), lambda i: (i, col))


def _full(shape):
    nd = len(shape)
    return pl.BlockSpec(shape, lambda i: (0,) * nd)


def _prev(tm, h, c, col=0):
    return pl.BlockSpec((h, c), lambda i: (jnp.maximum(i * (tm // h) - 1, 0), col))


def _next(tm, h, c, s, col=0):
    return pl.BlockSpec((h, c), lambda i: (jnp.minimum((i + 1) * (tm // h), s // h - 1), col))


def _acc_add(ref, first, val):
    @pl.when(first)
    def _():
        ref[...] = val

    @pl.when(jnp.logical_not(first))
    def _():
        ref[...] += val


def _colsum(v):
    return jnp.sum(v, axis=0, keepdims=True)


def _ln_stats(z):
    mu = jnp.mean(z, axis=-1, keepdims=True)
    zc = z - mu
    var = jnp.mean(zc * zc, axis=-1, keepdims=True)
    rstd = lax.rsqrt(var + LN_EPS)
    return zc * rstd, rstd


def _ln_bwd(dxhat, xhat, rstd):
    m1 = jnp.mean(dxhat, axis=-1, keepdims=True)
    m2 = jnp.mean(dxhat * xhat, axis=-1, keepdims=True)
    return rstd * (dxhat - m1 - xhat * m2)


def ln_fwd(name, x, f, g, b, ple=None, emit_y=True):
    s, d = x.shape
    tm = _row_tile(s)
    n_in = 2 + (3 if ple is not None else 0)

    def body(*refs):
        x_ref, f_ref = refs[0], refs[1]
        g_ref, b_ref = refs[n_in], refs[n_in + 1]
        xh_ref, rs_ref = refs[-2:]
        z = ALPHA * x_ref[...] + f_ref[...]
        if ple is not None:
            pgl_ref, pp_ref, bg_ref = refs[2:5]
            z = z + _sig(pgl_ref[...] + bg_ref[...]) * pp_ref[...]
        xhat, rstd = _ln_stats(z)
        if emit_y:
            y = xhat * g_ref[...] + b_ref[...]
            refs[n_in + 2][...] = y
            refs[n_in + 3][...] = y.astype(BF16)
        xh_ref[...] = xhat
        rs_ref[...] = jnp.broadcast_to(rstd, rs_ref.shape)

    ins = [x, f]
    specs = [_row(tm, d), _row(tm, d)]
    if ple is not None:
        pgl, pp, bg = ple
        ins += [pgl, pp, bg]
        specs += [_row(tm, d), _row(tm, d), _full((1, d))]
    ins += [g, b]
    specs += [_full((1, d)), _full((1, d))]
    y_shapes = [jax.ShapeDtypeStruct((s, d), F32), jax.ShapeDtypeStruct((s, d), BF16)] if emit_y else []
    outs = _call(
        body, name=name, grid=(s // tm,), in_specs=specs,
        out_specs=[_row(tm, d)] * (len(y_shapes) + 1) + [_row(tm, LANES)],
        out_shape=y_shapes + [jax.ShapeDtypeStruct((s, d), F32), jax.ShapeDtypeStruct((s, LANES), F32)],
        compiler_params=_cp(),
    )(*ins)
    return tuple(outs) if emit_y else (None, None, outs[0], outs[1])


def ln_bwd(name, parts, xhat, rstd, g, ple=None, loss=None):
    s, d = xhat.shape
    tm = _row_tile(s)
    coefs = [c for c, _ in parts]
    n_p = len(parts)
    n_ple = 3 if ple is not None else 0
    n_in = n_p + 3 + n_ple + (2 if loss is not None else 0)

    def body(*refs):
        first = pl.program_id(0) == 0
        xh = refs[n_p][...]
        rs = refs[n_p + 1][:, 0:1]
        g_v = refs[n_p + 2][...]
        outs = refs[n_in:]
        if loss is not None:
            t_ref, b_ref = refs[n_p + 3 + n_ple:n_p + 5 + n_ple]
            err = (xh * g_v + b_ref[...]) - t_ref[...]
            dy = err * (1.0 / d)
            part = 0.5 * jnp.sum(jnp.mean(err * err, axis=-1, keepdims=True), axis=0, keepdims=True)
            _acc_add(outs[-1], first, jnp.broadcast_to(part, outs[-1].shape))
        else:
            dy = coefs[0] * refs[0][...].astype(F32)
            for j in range(1, n_p):
                dy = dy + coefs[j] * refs[j][...].astype(F32)
        dz = _ln_bwd(dy * g_v, xh, rs)
        outs[0][...] = dz
        _acc_add(outs[1], first, _colsum(dy * xh))
        _acc_add(outs[2], first, _colsum(dy))
        if ple is not None:
            pgl_ref, pp_ref, bg_ref = refs[n_p + 3:n_p + 6]
            pg = _sig(pgl_ref[...] + bg_ref[...])
            dpgl = dz * pp_ref[...] * pg * (1.0 - pg)
            outs[3][...] = (dz * pg).astype(BF16)
            outs[4][...] = dpgl.astype(BF16)
            _acc_add(outs[5], first, _colsum(dpgl))

    ins = [p for _, p in parts] + [xhat, rstd, g]
    specs = [_row(tm, d)] * n_p + [_row(tm, d), _row(tm, LANES), _full((1, d))]
    out_specs = [_row(tm, d), _full((1, d)), _full((1, d))]
    out_shape = [jax.ShapeDtypeStruct((s, d), F32), jax.ShapeDtypeStruct((1, d), F32),
                 jax.ShapeDtypeStruct((1, d), F32)]
    if ple is not None:
        pgl, pp, bg = ple
        ins += [pgl, pp, bg]
        specs += [_row(tm, d), _row(tm, d), _full((1, d))]
        out_specs += [_row(tm, d), _row(tm, d), _full((1, d))]
        out_shape += [jax.ShapeDtypeStruct((s, d), BF16), jax.ShapeDtypeStruct((s, d), BF16),
                      jax.ShapeDtypeStruct((1, d), F32)]
    if loss is not None:
        target, b = loss
        ins += [target, b]
        specs += [_row(tm, d), _full((1, d))]
        out_specs += [_full((8, LANES))]
        out_shape += [jax.ShapeDtypeStruct((8, LANES), F32)]
    return _call(
        body, name=name, grid=(s // tm,), in_specs=specs, out_specs=out_specs, out_shape=out_shape,
        compiler_params=_cp(),
    )(*ins)


def _fill_rotations(rot_ref, x, direction):
    n = x.shape[0]
    rot_ref[0] = x
    for b in range(1, SUBLANES):
        if direction < 0:
            rot_ref[b, SUBLANES:n, :] = x[SUBLANES - b:n - b]
        else:
            rot_ref[b, 0:n - SUBLANES, :] = x[b:n - SUBLANES + b]


def _rotated(rot_ref, start, rows, cs, direction=-1):
    b = (-start) % SUBLANES if direction < 0 else start % SUBLANES
    aligned = start + b if direction < 0 else start - b
    return rot_ref[b, pl.ds(aligned, rows), cs]


def _tile_pos(i, tm, rows):
    return (i * tm + lax.broadcasted_iota(jnp.int32, (rows, 1), 0) + 1).astype(F32)


def mixer_fwd(name, u, pool_w, pool_scale, conv_w, conv_b, cn_g, cn_b):
    s = u.shape[0]
    dp = 512
    tm = min(256, s // 4)
    h = CONV_HALO

    def body(a_c, a_p, bv_c, bv_p, bg_c, bg_p, pw_ref, ps_ref, cw_ref, cb_ref, cg_ref, cbt_ref,
             cat_ref, d_ref, e_ref, glu_ref, hh_ref, rs_ref, ext_a, rot_g, conv_out):
        i = pl.program_id(0)
        first = i == 0
        ext_a[0:h, :] = jnp.where(first, 0.0, a_p[...])
        ext_a[h:, :] = a_c[...]
        glu = bv_c[...] * _sig(bg_c[...])
        glu_ref[...] = glu
        _fill_rotations(rot_g, jnp.concatenate([jnp.where(first, 0.0, bv_p[...] * _sig(bg_p[...])), glu], axis=0), -1)
        pos = _tile_pos(i, tm, tm)
        for gi, w in enumerate(POOL_WINDOWS):
            cs = slice(gi * POOL_GROUP, (gi + 1) * POOL_GROUP)
            a_g = ext_a[pl.ds(h, tm), cs]
            acc = a_g
            for sh in range(1, w):
                acc = acc + ext_a[pl.ds(h - sh, tm), cs]
            d_g = acc / jnp.minimum(pos, float(w)) - a_g
            d_ref[:, cs] = d_g.astype(BF16)
            e_g = _dot(d_g, pw_ref[gi], "nn")
            e_ref[:, cs] = e_g
            cat_ref[:, cs] = (e_g * ps_ref[:, cs]).astype(BF16)
        for lg in range(dp // LANES):
            cs = slice(lg * LANES, (lg + 1) * LANES)
            acc = jnp.broadcast_to(cb_ref[:, cs], (tm, LANES))
            for sh in range(CONV_K):
                acc = acc + _rotated(rot_g, h - sh, tm, cs) * cw_ref[pl.ds(CONV_K - 1 - sh, 1), cs]
            conv_out[:, cs] = acc
        hhat, rstd = _ln_stats(conv_out[...])
        hl = hhat * cg_ref[...] + cbt_ref[...]
        cat_ref[:, dp:] = (hl * _sig(hl)).astype(BF16)
        hh_ref[...] = hhat
        rs_ref[...] = jnp.broadcast_to(rstd, rs_ref.shape)

    specs = [_row(tm, dp, 0), _prev(tm, h, dp, 0), _row(tm, dp, 1), _prev(tm, h, dp, 1),
             _row(tm, dp, 2), _prev(tm, h, dp, 2),
             _full((4, POOL_GROUP, POOL_GROUP)), _full((1, dp)), _full((CONV_K, dp)),
             _full((1, dp)), _full((1, dp)), _full((1, dp))]
    out_specs = [_row(tm, 2 * dp), _row(tm, dp), _row(tm, dp), _row(tm, dp), _row(tm, dp), _row(tm, LANES)]
    out_shape = [jax.ShapeDtypeStruct((s, 2 * dp), BF16), jax.ShapeDtypeStruct((s, dp), BF16),
                 jax.ShapeDtypeStruct((s, dp), F32), jax.ShapeDtypeStruct((s, dp), F32),
                 jax.ShapeDtypeStruct((s, dp), F32), jax.ShapeDtypeStruct((s, LANES), F32)]
    return _call(
        body, name=name, grid=(s // tm,), in_specs=specs, out_specs=out_specs, out_shape=out_shape,
        scratch_shapes=[pltpu.VMEM((h + tm, dp), F32), pltpu.VMEM((SUBLANES, h + tm, dp), F32),
                        pltpu.VMEM((tm, dp), F32)],
        compiler_params=_cp(),
    )(u, u, u, u, u, u, pool_w, pool_scale, conv_w, conv_b, cn_g, cn_b)


def mixer_bwd(name, dcat, u, d_sv, e_sv, glu_sv, hh_sv, rs_sv, pool_w, pool_scale, conv_w, cn_g, cn_b):
    s = u.shape[0]
    dp = 512
    tm = min(256, s // 4)
    h = CONV_HALO
    nt = s // tm

    def body(dc_c, dc_n, bv_c, bg_c, d_c, e_c, gl_c, gl_p, hh_c, hh_n, rs_c, rs_n,
             pw_ref, ps_ref, cw_ref, cg_ref, cbt_ref,
             du_ref, dpw_ref, dps_ref, dcw_ref, dcb_ref, dcg_ref, dcbt_ref,
             ext_dh, ext_g, ext_r):
        i = pl.program_id(0)
        first = i == 0
        last = i == nt - 1
        cg = cg_ref[...]

        def conv_grads(dyb, hhat, rstd):
            hl = hhat * cg + cbt_ref[...]
            sg = _sig(hl)
            dhl = dyb * (sg * (1.0 + hl * (1.0 - sg)))
            return _ln_bwd(dhl * cg, hhat, rstd), dhl

        hh_cur = hh_c[...]
        dh_c, dhl_c = conv_grads(dc_c[:, dp:], hh_cur, rs_c[:, 0:1])
        dh_n, _ = conv_grads(dc_n[:, dp:], hh_n[...], rs_n[:, 0:1])
        _fill_rotations(ext_dh, jnp.concatenate([dh_c, jnp.where(last, 0.0, dh_n)], axis=0), 1)
        _fill_rotations(ext_g, jnp.concatenate([jnp.where(first, 0.0, gl_p[...]), gl_c[...]], axis=0), -1)

        @pl.when(first)
        def _():
            dcw_ref[...] = jnp.zeros(dcw_ref.shape, F32)

        for lg in range(dp // LANES):
            cs = slice(lg * LANES, (lg + 1) * LANES)
            dglu = jnp.zeros((tm, LANES), F32)
            for sh in range(CONV_K):
                dglu = dglu + _rotated(ext_dh, sh, tm, cs, 1) * cw_ref[pl.ds(CONV_K - 1 - sh, 1), cs]
            dh_g = ext_dh[0, pl.ds(0, tm), cs]
            for sh in range(CONV_K):
                dcw_ref[pl.ds(CONV_K - 1 - sh, 1), cs] += _colsum(dh_g * _rotated(ext_g, h - sh, tm, cs))
            sgate = _sig(bg_c[:, cs])
            du_ref[:, dp + lg * LANES:dp + (lg + 1) * LANES] = dglu * sgate
            du_ref[:, 2 * dp + lg * LANES:2 * dp + (lg + 1) * LANES] = dglu * bv_c[:, cs] * sgate * (1.0 - sgate)
        _acc_add(dcb_ref, first, _colsum(dh_c))
        _acc_add(dcg_ref, first, _colsum(dhl_c * hh_cur))
        _acc_add(dcbt_ref, first, _colsum(dhl_c))

        pos_c = _tile_pos(i, tm, tm)
        pos_n = _tile_pos(i + 1, tm, h)
        _acc_add(dps_ref, first, _colsum(dc_c[:, :dp] * e_c[...]))
        for gi, w in enumerate(POOL_WINDOWS):
            cs = slice(gi * POOL_GROUP, (gi + 1) * POOL_GROUP)
            pw = pw_ref[gi]
            de_c = dc_c[:, cs] * ps_ref[:, cs]
            de_n = dc_n[:, cs] * ps_ref[:, cs]
            dd_c = _dot(de_c, pw, "nt")
            dd_n = _dot(de_n, pw, "nt")
            ext_r[0:tm, :] = dd_c / jnp.minimum(pos_c, float(w))
            ext_r[tm:, :] = jnp.where(last, 0.0, dd_n / jnp.minimum(pos_n, float(w)))
            acc = -dd_c
            for sh in range(w):
                acc = acc + ext_r[pl.ds(sh, tm), :]
            du_ref[:, cs] = acc
            dpw_g = _dot(d_c[:, cs], de_c, "tn")

            @pl.when(first)
            def _():
                dpw_ref[gi] = dpw_g

            @pl.when(jnp.logical_not(first))
            def _():
                dpw_ref[gi] += dpw_g

    specs = [_row(tm, 2 * dp), _next(tm, h, 2 * dp, s), _row(tm, dp, 1), _row(tm, dp, 2),
             _row(tm, dp), _row(tm, dp), _row(tm, dp), _prev(tm, h, dp),
             _row(tm, dp), _next(tm, h, dp, s), _row(tm, LANES), _next(tm, h, LANES, s),
             _full((4, POOL_GROUP, POOL_GROUP)), _full((1, dp)), _full((CONV_K, dp)),
             _full((1, dp)), _full((1, dp))]
    out_specs = [_row(tm, 3 * dp), _full((4, POOL_GROUP, POOL_GROUP)), _full((1, dp)), _full((CONV_K, dp)),
                 _full((1, dp)), _full((1, dp)), _full((1, dp))]
    out_shape = [jax.ShapeDtypeStruct((s, 3 * dp), F32),
                 jax.ShapeDtypeStruct((4, POOL_GROUP, POOL_GROUP), F32), jax.ShapeDtypeStruct((1, dp), F32),
                 jax.ShapeDtypeStruct((CONV_K, dp), F32), jax.ShapeDtypeStruct((1, dp), F32),
                 jax.ShapeDtypeStruct((1, dp), F32), jax.ShapeDtypeStruct((1, dp), F32)]
    return _call(
        body, name=name, grid=(nt,), in_specs=specs, out_specs=out_specs, out_shape=out_shape,
        scratch_shapes=[pltpu.VMEM((SUBLANES, tm + h, dp), F32), pltpu.VMEM((SUBLANES, h + tm, dp), F32),
                        pltpu.VMEM((tm + h, POOL_GROUP), F32)],
        compiler_params=_cp(),
    )(dcat, dcat, u, u, d_sv, e_sv, glu_sv, glu_sv, hh_sv, hh_sv, rs_sv, rs_sv,
      pool_w, pool_scale, conv_w, cn_g, cn_b)


GELU_C = math.sqrt(2.0 / math.pi)


def _gelu_parts(x):
    x2 = x * x
    t = jnp.tanh(x * (GELU_C + (GELU_C * 0.044715) * x2))
    half_1pt = 0.5 + 0.5 * t
    gelu = x * half_1pt
    dgelu = half_1pt + (0.5 * x) * (1.0 - t * t) * (GELU_C + (3.0 * GELU_C * 0.044715) * x2)
    return gelu, dgelu


def ffn_act_fwd(name, gv, dw_w, dw_b):
    s = gv.shape[0]
    dff = gv.shape[1] // 2
    tm = min(FFN_TILE, s // 4)
    h = FFN_HALO
    rc = FFN_CHUNK_ROWS
    lw = FFN_CHUNK_LANES

    def body(g_c, g_p, v_c, w_ref, b_ref, hid_ref):
        first = pl.program_id(0) == 0

        def chunk(ci, carry):
            r0 = pl.multiple_of(ci * rc, rc)
            above = pl.multiple_of(jnp.maximum(r0 - h, 0), h)
            for lg in range(dff // lw):
                cs = slice(lg * lw, (lg + 1) * lw)
                top = jnp.where(ci == 0, jnp.where(first, 0.0, g_p[:, cs]), g_c[pl.ds(above, h), cs])
                win = jnp.concatenate([top, g_c[pl.ds(r0, rc), cs]], axis=0)
                gc = jnp.broadcast_to(b_ref[:, cs], (rc, lw))
                for sh in range(FFN_K):
                    gc = gc + win[h - sh:h - sh + rc] * w_ref[pl.ds(FFN_K - 1 - sh, 1), cs]
                gelu, _ = _gelu_parts(gc)
                hid_ref[pl.ds(r0, rc), cs] = (gelu * v_c[pl.ds(r0, rc), cs]).astype(BF16)
            return carry

        lax.fori_loop(0, tm // rc, chunk, 0)

    return _call(
        body, name=name, grid=(s // tm,),
        in_specs=[_row(tm, dff, 0), _prev(tm, h, dff, 0), _row(tm, dff, 1), _full((FFN_K, dff)), _full((1, dff))],
        out_specs=_row(tm, dff), out_shape=jax.ShapeDtypeStruct((s, dff), BF16),
        compiler_params=_cp(),
    )(gv, gv, gv, dw_w, dw_b)


def ffn_act_bwd(name, dhid, gv, dw_w, dw_b):
    s = gv.shape[0]
    dff = gv.shape[1] // 2
    tm = min(FFN_TILE, s // 4)
    h = FFN_HALO
    nt = s // tm
    rc = FFN_CHUNK_ROWS
    lw = FFN_CHUNK_LANES
    n_chunks = tm // rc

    def body(dh_c, dh_n, g_p, g_c, g_n, v_c, v_n, w_ref, b_ref, dgv_ref, dw_ref, db_ref):
        i = pl.program_id(0)
        first = i == 0
        last = i == nt - 1

        @pl.when(first)
        def _():
            dw_ref[...] = jnp.zeros(dw_ref.shape, F32)
            db_ref[...] = jnp.zeros(db_ref.shape, F32)

        def chunk(ci, carry):
            r0 = pl.multiple_of(ci * rc, rc)
            above = pl.multiple_of(jnp.maximum(r0 - h, 0), h)
            below = pl.multiple_of(jnp.minimum(r0 + rc, tm - h), h)
            at_end = ci == n_chunks - 1
            for lg in range(dff // lw):
                cs = slice(lg * lw, (lg + 1) * lw)
                top = jnp.where(ci == 0, jnp.where(first, 0.0, g_p[:, cs]), g_c[pl.ds(above, h), cs])
                bot = jnp.where(at_end, g_n[:, cs], g_c[pl.ds(below, h), cs])
                win = jnp.concatenate([top, g_c[pl.ds(r0, rc), cs], bot], axis=0)
                shifted = [win[h - sh:h - sh + rc + h] for sh in range(FFN_K)]
                gc = jnp.broadcast_to(b_ref[:, cs], (rc + h, lw))
                for sh in range(FFN_K):
                    gc = gc + shifted[sh] * w_ref[pl.ds(FFN_K - 1 - sh, 1), cs]
                gelu, dgelu = _gelu_parts(gc)
                dh_mid = dh_c[pl.ds(r0, rc), cs]
                hv_bot = jnp.where(at_end, jnp.where(last, 0.0, dh_n[:, cs] * v_n[:, cs]),
                                   dh_c[pl.ds(below, h), cs] * v_c[pl.ds(below, h), cs])
                dgc = jnp.concatenate([dh_mid * v_c[pl.ds(r0, rc), cs], hv_bot], axis=0) * dgelu
                dgate = jnp.zeros((rc, lw), F32)
                for sh in range(FFN_K):
                    dgate = dgate + dgc[sh:sh + rc] * w_ref[pl.ds(FFN_K - 1 - sh, 1), cs]
                dgv_ref[pl.ds(r0, rc), cs] = dgate.astype(BF16)
                dgv_ref[pl.ds(r0, rc), slice(dff + lg * lw, dff + (lg + 1) * lw)] = (dh_mid * gelu[0:rc]).astype(BF16)
                dgc_mid = dgc[0:rc]
                for sh in range(FFN_K):
                    dw_ref[pl.ds(FFN_K - 1 - sh, 1), cs] += _colsum(dgc_mid * shifted[sh][0:rc])
                db_ref[:, cs] += _colsum(dgc_mid)
            return carry

        lax.fori_loop(0, n_chunks, chunk, 0)

    return _call(
        body, name=name, grid=(nt,),
        in_specs=[_row(tm, dff), _next(tm, h, dff, s),
                  _prev(tm, h, dff, 0), _row(tm, dff, 0), _next(tm, h, dff, s, 0),
                  _row(tm, dff, 1), _next(tm, h, dff, s, 1),
                  _full((FFN_K, dff)), _full((1, dff))],
        out_specs=[_row(tm, 2 * dff), _full((FFN_K, dff)), _full((1, dff))],
        out_shape=[jax.ShapeDtypeStruct((s, 2 * dff), BF16), jax.ShapeDtypeStruct((FFN_K, dff), F32),
                   jax.ShapeDtypeStruct((1, dff), F32)],
        compiler_params=_cp(),
    )(dhid, dhid, gv, gv, gv, gv, gv, dw_w, dw_b)


def _bias_line(rel_bias):
    nh = rel_bias.shape[0]
    line = jnp.concatenate(
        [jnp.zeros((nh, 1), rel_bias.dtype), jnp.broadcast_to(rel_bias[:, 2 * MAX_REL:], (nh, SHEAR_SAT)),
         jnp.flip(rel_bias[:, 1:2 * MAX_REL], axis=1)], axis=1)
    return line[:, None, :]


def bias_tile(name, line):
    nh = line.shape[0]

    def body(l_ref, o_ref):
        x = jnp.broadcast_to(l_ref[...], (Q_TILE, SHEAR_W))
        z = pltpu.roll(x, SHEAR_W - Q_TILE, 1, stride=1, stride_axis=0)
        qc = lax.broadcasted_iota(jnp.int32, (Q_TILE, K_WIN), 0) // CHUNK
        kc = lax.broadcasted_iota(jnp.int32, (Q_TILE, K_WIN), 1) // CHUNK
        o_ref[...] = jnp.where((kc >= qc) & (kc <= qc + LEFT_CHUNKS), z[:, :K_WIN], NEG_INF)

    return _call(
        body, name=name, grid=(nh,), in_specs=[pl.BlockSpec((None, 1, SHEAR_W), lambda hh: (hh, 0, 0))],
        out_specs=pl.BlockSpec((None, Q_TILE, K_WIN), lambda hh: (hh, 0, 0)),
        out_shape=jax.ShapeDtypeStruct((nh, Q_TILE, K_WIN), F32), compiler_params=_cp(),
    )(line)


def _stack_heads(x2, scale=None):
    if scale is not None:
        x2 = x2 * jnp.asarray(scale, x2.dtype)
    lane = lax.broadcasted_iota(jnp.int32, x2.shape, 1)
    zero = jnp.zeros_like(x2)
    return jnp.concatenate([jnp.where(lane < HEAD_DIM, x2, zero), jnp.where(lane < HEAD_DIM, zero, x2)], axis=0)


def _unstack_heads(x_st):
    lane = lax.broadcasted_iota(jnp.int32, (Q_TILE, LANES), 1)
    return jnp.where(lane < HEAD_DIM, x_st[:Q_TILE], x_st[Q_TILE:])


def _attn_probs(q_st, k3, bias_st, t):
    sc = _dot(q_st, k3, "nt") + bias_st
    col = lax.broadcasted_iota(jnp.int32, sc.shape, 1)
    sc = jnp.where(col >= PAD_ROWS - t * Q_TILE, sc, NEG_INF)
    m = jnp.max(sc, axis=-1, keepdims=True)
    p = jnp.exp(sc - m)
    return p * (1.0 / jnp.sum(p, axis=-1, keepdims=True))


def _attn_specs(d_model):
    nq = PAD_ROWS // Q_TILE
    groups = d_model // ATTN_LANES
    specs = [pl.BlockSpec((Q_TILE, ATTN_LANES), lambda g, t: (t + nq, g))]
    for which in (1, 2):
        for j in range(K_WIN // Q_TILE):
            specs.append(pl.BlockSpec((Q_TILE, ATTN_LANES), lambda g, t, j=j, which=which: (t + j, which * groups + g)))
    specs.append(pl.BlockSpec((2 * ATTN_PAIRS, Q_TILE, K_WIN), lambda g, t: (g, 0, 0)))
    return specs


def attn_fwd(name, qkvp, bias):
    s = qkvp.shape[0] - PAD_ROWS
    d_model = qkvp.shape[1] // 3
    nw = K_WIN // Q_TILE

    def body(q_ref, *refs):
        k_refs, v_refs, b_ref, o_ref = refs[:nw], refs[nw:2 * nw], refs[2 * nw], refs[2 * nw + 1]
        t = pl.program_id(1)
        for j in range(ATTN_PAIRS):
            ls = slice(j * LANES, (j + 1) * LANES)
            k3 = jnp.concatenate([r[:, ls] for r in k_refs], axis=0)
            v3 = jnp.concatenate([r[:, ls] for r in v_refs], axis=0)
            bias_st = b_ref[2 * j:2 * j + 2].reshape(2 * Q_TILE, K_WIN)
            p = _attn_probs(_stack_heads(q_ref[:, ls], ATTN_SCALE), k3, bias_st, t)
            o_ref[:, ls] = _unstack_heads(_dot(p, v3, "nn")).astype(BF16)

    return _call(
        body, name=name, grid=(d_model // ATTN_LANES, s // Q_TILE),
        in_specs=_attn_specs(d_model), out_specs=pl.BlockSpec((Q_TILE, ATTN_LANES), lambda g, t: (t, g)),
        out_shape=jax.ShapeDtypeStruct((s, d_model), BF16), compiler_params=_cp(),
    )(qkvp, *([qkvp] * (2 * nw)), bias)


def attn_bwd(name, qkvp, bias, do):
    s = qkvp.shape[0] - PAD_ROWS
    d_model = qkvp.shape[1] // 3
    nw = K_WIN // Q_TILE
    nt = s // Q_TILE

    def body(q_ref, *refs):
        k_refs, v_refs = refs[:nw], refs[nw:2 * nw]
        b_ref, do_ref, dq_ref, dk_ref, dv_ref, ds_ref, dk_acc, dv_acc = refs[2 * nw:]
        t = pl.program_id(1)
        first = t == 0

        @pl.when(first)
        def _():
            dk_acc[...] = jnp.zeros(dk_acc.shape, F32)
            dv_acc[...] = jnp.zeros(dv_acc.shape, F32)

        start = pl.multiple_of(t * Q_TILE, Q_TILE)
        for j in range(ATTN_PAIRS):
            ls = slice(j * LANES, (j + 1) * LANES)
            q_st = _stack_heads(q_ref[:, ls], ATTN_SCALE)
            do_st = _stack_heads(do_ref[:, ls])
            k3 = jnp.concatenate([r[:, ls] for r in k_refs], axis=0)
            v3 = jnp.concatenate([r[:, ls] for r in v_refs], axis=0)
            p = _attn_probs(q_st, k3, b_ref[2 * j:2 * j + 2].reshape(2 * Q_TILE, K_WIN), t)
            dp = _dot(do_st, v3, "nt")
            ds = p * (dp - jnp.sum(p * dp, axis=-1, keepdims=True))
            _acc_add(ds_ref.at[2 * j:2 * j + 2], first, ds.reshape(2, Q_TILE, K_WIN))
            dsb = ds.astype(BF16)
            dq_ref[:, ls] = (_unstack_heads(_dot(dsb, k3, "nn")) * ATTN_SCALE).astype(BF16)
            dk_acc[pl.ds(start, K_WIN), ls] += _dot(dsb, q_st, "tn")
            dv_acc[pl.ds(start, K_WIN), ls] += _dot(p, do_st, "tn")

        @pl.when(t == nt - 1)
        def _():
            dk_ref[...] = dk_acc[pl.ds(PAD_ROWS, s), :].astype(BF16)
            dv_ref[...] = dv_acc[pl.ds(PAD_ROWS, s), :].astype(BF16)

    specs = _attn_specs(d_model) + [pl.BlockSpec((Q_TILE, ATTN_LANES), lambda g, t: (t, g))]
    col_spec = pl.BlockSpec((s, ATTN_LANES), lambda g, t: (0, g))
    return _call(
        body, name=name, grid=(d_model // ATTN_LANES, nt), in_specs=specs,
        out_specs=[pl.BlockSpec((Q_TILE, ATTN_LANES), lambda g, t: (t, g)), col_spec, col_spec,
                   pl.BlockSpec((2 * ATTN_PAIRS, Q_TILE, K_WIN), lambda g, t: (g, 0, 0))],
        out_shape=[jax.ShapeDtypeStruct((s, d_model), BF16)] * 3
        + [jax.ShapeDtypeStruct((N_HEADS, Q_TILE, K_WIN), F32)],
        scratch_shapes=[pltpu.VMEM((PAD_ROWS + s, ATTN_LANES), F32), pltpu.VMEM((PAD_ROWS + s, ATTN_LANES), F32)],
        compiler_params=_cp(),
    )(qkvp, *([qkvp] * (2 * nw)), bias, do)


def bias_grad_reduce(name, ds_sum):
    nh = ds_sum.shape[0]
    width = SHEAR_W + Q_TILE
    first_k = Q_TILE - 1

    def body(x_ref, col_ref, sat_ref):
        x = x_ref[...]
        hi = x.astype(BF16)
        lo = (x - hi.astype(F32)).astype(BF16)
        r = lax.broadcasted_iota(jnp.int32, (Q_TILE, Q_TILE), 0)
        c = lax.broadcasted_iota(jnp.int32, (Q_TILE, Q_TILE), 1)
        exchange = jnp.where(r + c == Q_TILE - 1, 1.0, 0.0).astype(BF16)
        x_rev = _dot(exchange, hi, "nn") + _dot(exchange, lo, "nn")
        zeros = jnp.zeros((Q_TILE, Q_TILE), F32)
        y = pltpu.roll(jnp.concatenate([zeros, x_rev, zeros], axis=1), 0, 1, stride=1, stride_axis=0)
        cols = _colsum(y)
        col_ref[...] = cols
        k = lax.broadcasted_iota(jnp.int32, cols.shape, 1) - first_k
        tot = jnp.sum(jnp.where((k >= 1) & (k <= SHEAR_SAT), cols, 0.0), axis=-1, keepdims=True)
        sat_ref[...] = jnp.broadcast_to(tot, sat_ref.shape)

    return _call(
        body, name=name, grid=(nh,),
        in_specs=[pl.BlockSpec((None, Q_TILE, K_WIN), lambda hh: (hh, 0, 0))],
        out_specs=[pl.BlockSpec((None, 1, width), lambda hh: (hh, 0, 0)),
                   pl.BlockSpec((None, 1, LANES), lambda hh: (hh, 0, 0))],
        out_shape=[jax.ShapeDtypeStruct((nh, 1, width), F32), jax.ShapeDtypeStruct((nh, 1, LANES), F32)],
        compiler_params=_cp(),
    )(ds_sum)


def _ew_rows(r, most=512, cols=None):
    if cols is not None and r * cols * 4 <= SMALL_BLOCK_BYTES:
        return r
    for cand in range(min(most, r) // 16 * 16, 0, -16):
        if r % cand == 0:
            return cand
    return r


def to_bf16(name, a):
    s, d = a.shape
    tm = _row_tile(s)

    def body(a_ref, o_ref):
        o_ref[...] = a_ref[...].astype(BF16)

    return _call(
        body, name=name, grid=(s // tm,), in_specs=[_row(tm, d)], out_specs=_row(tm, d),
        out_shape=jax.ShapeDtypeStruct((s, d), BF16), compiler_params=_cp(),
    )(a)


def cast_into_gathered(name, w, layer, s_idx, n_blocks=N_SHARD, dtype=BF16, token=None):
    r, c = w.shape[-2:]
    tr = _ew_rows(r, cols=c)

    def body(s_ref, w_ref, *rest):
        rest[-1][...] = w_ref[...].astype(dtype)

    extra = [] if token is None else [token]
    grid_spec = pltpu.PrefetchScalarGridSpec(
        num_scalar_prefetch=1, grid=(r // tr,),
        in_specs=[pl.BlockSpec((None, tr, c), lambda i, s_ref: (layer, i, 0))] + [ANY_SPEC] * len(extra),
        out_specs=pl.BlockSpec((None, tr, c), lambda i, s_ref: (s_ref[0], i, 0)))
    return _call(
        body, name=name, grid_spec=grid_spec, out_shape=jax.ShapeDtypeStruct((n_blocks, r, c), dtype),
        compiler_params=_cp(),
    )(s_idx, w, *extra)


def adamw(name, w, grads, m, v, token=None):
    nl, r, c = w.shape
    tr = _ew_rows(r, 256, cols=c)

    def body(*refs):
        w_ref, m_ref, v_ref = refs[0], refs[1], refs[2]
        g_refs = refs[3:3 + nl]
        d_ref, nm_ref, nv_ref = refs[-3:]
        layer = pl.program_id(0)
        g = g_refs[0][...]
        for j in range(1, nl):
            g = jnp.where(layer == j, g_refs[j][...], g)
        d_ref[...], nm_ref[...], nv_ref[...] = _adamw_update(w_ref[...], g, m_ref[...], v_ref[...])

    p_spec = pl.BlockSpec((None, tr, c), lambda l, i: (l, i, 0))
    g_spec = pl.BlockSpec((tr, c), lambda l, i: (i, 0))
    extra = [] if token is None else [token]
    extra_specs = [] if token is None else [ANY_SPEC]
    return _call(
        body, name=name, grid=(nl, r // tr), in_specs=[p_spec] * 3 + [g_spec] * nl + extra_specs,
        out_specs=[p_spec] * 3, out_shape=[jax.ShapeDtypeStruct((nl, r, c), F32)] * 3, compiler_params=_cp(),
    )(w, m, v, *grads, *extra)


def _adamw_update(w, g, m, v):
    nm = ADAM_B1 * m + (1.0 - ADAM_B1) * g
    nv = ADAM_B2 * v + (1.0 - ADAM_B2) * (g * g)
    delta = -ADAM_LR * ((nm / ADAM_BC1) / (jnp.sqrt(nv / ADAM_BC2) + ADAM_EPS) + ADAM_WD * w)
    return delta, nm, nv


def adamw_many(name, ws, gs, ms, vs, token):
    n = len(ws)

    def body(*refs):
        ins, outs = refs[:4 * n], refs[4 * n + 1:]
        for i in range(n):
            delta, nm, nv = _adamw_update(ins[i][...], ins[n + i][...], ins[2 * n + i][...], ins[3 * n + i][...])
            outs[3 * i][...] = delta
            outs[3 * i + 1][...] = nm
            outs[3 * i + 2][...] = nv

    vmem = pl.BlockSpec(memory_space=pltpu.VMEM)
    shapes = [jax.ShapeDtypeStruct(w.shape, F32) for w in ws for _ in range(3)]
    outs = _call(
        body, name=name, in_specs=[vmem] * (4 * n) + [ANY_SPEC], out_specs=[vmem] * (3 * n), out_shape=shapes,
        compiler_params=_cp(),
    )(*ws, *gs, *ms, *vs, token)
    return [tuple(outs[3 * i:3 * i + 3]) for i in range(n)]


def sum_blocks(name, gathered, n_blocks):
    r = gathered.shape[0] // n_blocks
    c = gathered.shape[1]
    tr = r if r <= SUM_BLOCK_ROWS else _ew_rows(r)
    nt = r // tr

    def body(*refs):
        acc = refs[0][...]
        for j in range(1, n_blocks):
            acc = acc + refs[j][...]
        refs[-1][...] = acc

    specs = [pl.BlockSpec((tr, c), lambda i, j=j: (j * nt + i, 0)) for j in range(n_blocks)]
    return _call(
        body, name=name, grid=(nt,), in_specs=specs, out_specs=pl.BlockSpec((tr, c), lambda i: (i, 0)),
        out_shape=jax.ShapeDtypeStruct((r, c), F32), compiler_params=_cp(),
    )(*([gathered] * n_blocks))


def _place():
    return lax.axis_index("x"), lax.axis_index("y"), lax.axis_index("c")


def _other_chips(x, y):
    return [(1 - x, y), (x, 1 - y), (1 - x, 1 - y)]


HBM_SPEC = pl.BlockSpec(memory_space=pltpu.HBM)
SEM_SPEC = pl.BlockSpec(memory_space=pltpu.SEMAPHORE)
ANY_SPEC = pl.BlockSpec(memory_space=pl.ANY)
EFFECT = pltpu.SideEffectType.DATAFLOW_SIDE_EFFECTING


def copies_start(name, bufs, plan, n_copies):
    n = len(bufs)

    def body(*refs):
        send, recv = refs[n], refs[n + 1]
        token = refs[2 * n + 2]
        for k, (src, dst, peer, _) in enumerate(plan(refs[:n])):
            pltpu.make_async_remote_copy(
                src_ref=src, dst_ref=dst, send_sem=send.at[k], recv_sem=recv.at[k],
                device_id=peer, device_id_type=MESH).start()
        token[...] = jnp.zeros(token.shape, F32)

    outs = pl.pallas_call(
        body, name=name,
        out_shape=(pltpu.SemaphoreType.DMA((n_copies,)), pltpu.SemaphoreType.DMA((n_copies,)),
                   *[pltpu.HBM(b.shape, b.dtype) for b in bufs], jax.ShapeDtypeStruct((8, LANES), F32)),
        in_specs=[HBM_SPEC] * n,
        out_specs=(SEM_SPEC, SEM_SPEC, *([HBM_SPEC] * n), pl.BlockSpec(memory_space=pltpu.VMEM)),
        input_output_aliases={a: a + 2 for a in range(n)},
        compiler_params=pltpu.CompilerParams(has_side_effects=EFFECT),
    )(*[_in_hbm(b) for b in bufs])
    return outs[0], outs[1], list(outs[2:2 + n]), outs[2 + n]


def copies_wait(name, bufs, send, recv, plan, sem_base, after):
    n = len(bufs)

    def body(*refs):
        send_ref, recv_ref = refs[n], refs[n + 1]
        for k, (src, _, peer, land) in enumerate(plan(refs[:n])):
            cp = pltpu.make_async_remote_copy(
                src_ref=src, dst_ref=land, send_sem=send_ref.at[sem_base + k], recv_sem=recv_ref.at[sem_base + k],
                device_id=peer, device_id_type=MESH)
            cp.wait_send()
            cp.wait_recv()

    outs = pl.pallas_call(
        body, name=name,
        out_shape=tuple(pltpu.HBM(b.shape, b.dtype) for b in bufs),
        in_specs=[HBM_SPEC] * n + [SEM_SPEC, SEM_SPEC, ANY_SPEC], out_specs=tuple([HBM_SPEC] * n),
        input_output_aliases={a: a for a in range(n)},
        compiler_params=pltpu.CompilerParams(has_side_effects=EFFECT),
    )(*bufs, send, recv, after)
    return list(outs)


def gather_plan(refs):
    x, y, c = _place()
    me = 2 * x + y
    return [(buf.at[me], buf.at[me], (cx, cy, c), buf.at[2 * cx + cy])
            for buf in refs for cx, cy in _other_chips(x, y)]


def all_plan(refs):
    x, y, c = _place()
    me = 4 * x + 2 * y + c
    out = []
    for buf in refs:
        for flip in range(1, 8):
            px = 1 - x if flip & 4 else x
            py = 1 - y if flip & 2 else y
            pc = 1 - c if flip & 1 else c
            out.append((buf.at[me], buf.at[me], (px, py, pc), buf.at[4 * px + 2 * py + pc]))
    return out


def swap_plan(refs):
    x, y, c = _place()
    n = len(refs) // 2
    out = []
    for g, land in zip(refs[:n], refs[n:]):
        hr = g.shape[1] // 2
        out.append((g.at[:, pl.ds((1 - c) * hr, hr)], land, (x, y, 1 - c), land))
    return out


def owners_plan(refs):
    x, y, c = _place()
    n = len(refs) // 2
    return [(src.at[2 * cx + cy], land.at[j], (cx, cy, c), land.at[j])
            for src, land in zip(refs[:n], refs[n:]) for j, (cx, cy) in enumerate(_other_chips(x, y))]


def join_plan(refs):
    x, y, c = _place()
    out = []
    for buf in refs:
        hr = buf.shape[0] // 2
        mine = buf.at[pl.ds(c * hr, hr)]
        out.append((mine, mine, (x, y, 1 - c), buf.at[pl.ds((1 - c) * hr, hr)]))
    return out


def add_halves(name, grad, landed, sc_idx):
    _, r, c = grad.shape
    hr = r // 2
    tr = _ew_rows(hr)
    nt = hr // tr

    def body(sc_ref, g_ref, l_ref, own_ref, wire_ref):
        tot = g_ref[...] + l_ref[...]
        wire_ref[...] = tot.astype(BF16)

        @pl.when(pl.program_id(1) == sc_ref[0])
        def _():
            own_ref[...] = tot

    grid_spec = pltpu.PrefetchScalarGridSpec(
        num_scalar_prefetch=1, grid=(nt, N_SHARD),
        in_specs=[pl.BlockSpec((None, tr, c), lambda i, sh, sc_ref: (sh, sc_ref[1] * nt + i, 0)),
                  pl.BlockSpec((None, tr, c), lambda i, sh, sc_ref: (sh, i, 0))],
        out_specs=[pl.BlockSpec((tr, c), lambda i, sh, sc_ref: (i, 0)),
                   pl.BlockSpec((None, tr, c), lambda i, sh, sc_ref: (sh, i, 0))])
    return _call(
        body, name=name, grid_spec=grid_spec,
        out_shape=[jax.ShapeDtypeStruct((hr, c), F32), jax.ShapeDtypeStruct((N_SHARD, hr, c), BF16)],
        compiler_params=_cp(),
    )(sc_idx, grad, landed)


def add_owned(name, own, landed, sc_idx):
    hr, c = own.shape
    tr = _ew_rows(hr)
    nt = hr // tr

    def body(sc_ref, o_ref, l0, l1, l2, out_ref):
        out_ref[...] = ((o_ref[...] + l0[...].astype(F32)) + l1[...].astype(F32)) + l2[...].astype(F32)

    grid_spec = pltpu.PrefetchScalarGridSpec(
        num_scalar_prefetch=1, grid=(nt,),
        in_specs=[pl.BlockSpec((tr, c), lambda i, sc_ref: (i, 0))]
        + [pl.BlockSpec((None, tr, c), lambda i, sc_ref, j=j: (j, i, 0)) for j in range(3)],
        out_specs=pl.BlockSpec((tr, c), lambda i, sc_ref: (sc_ref[1] * nt + i, 0)))
    return _call(
        body, name=name, grid_spec=grid_spec, out_shape=jax.ShapeDtypeStruct((2 * hr, c), F32),
        compiler_params=_cp(),
    )(sc_idx, own, landed, landed, landed)


PACK_QUANTUM = 8 * LANES


def _pack(arrays):
    pieces = []
    for a in arrays:
        flat = a.reshape(-1)
        padded = -(-flat.shape[0] // PACK_QUANTUM) * PACK_QUANTUM
        pieces.append(jnp.pad(flat, (0, padded - flat.shape[0])).reshape(-1, LANES))
    return jnp.concatenate(pieces, axis=0)


def _unpack(packed, shapes):
    out = []
    row = 0
    for shp in shapes:
        size = math.prod(shp)
        rows = -(-size // PACK_QUANTUM) * 8
        out.append(packed[row:row + rows].reshape(-1)[:size].reshape(shp))
        row += rows
    return out


def kernel(x, p, mix_w_in, pool_w, pool_scale, conv_dw_w, conv_dw_b, conv_ln_g, conv_ln_b, mix_w_out, attn_w_qkv, attn_rel_bias, attn_w_o, ln_mix_g, ln_mix_b, ffn_w_up, ffn_dw_w, ffn_dw_b, ffn_w_down, ple_w_proj, ple_w_gate, ple_b_gate, ln_ffn_g, ln_ffn_b, loss_target, m_mix_w_in, m_pool_w, m_pool_scale, m_conv_dw_w, m_conv_dw_b, m_conv_ln_g, m_conv_ln_b, m_mix_w_out, m_attn_w_qkv, m_attn_rel_bias, m_attn_w_o, m_ln_mix_g, m_ln_mix_b, m_ffn_w_up, m_ffn_dw_w, m_ffn_dw_b, m_ffn_w_down, m_ple_w_proj, m_ple_w_gate, m_ple_b_gate, m_ln_ffn_g, m_ln_ffn_b, v_mix_w_in, v_pool_w, v_pool_scale, v_conv_dw_w, v_conv_dw_b, v_conv_ln_g, v_conv_ln_b, v_mix_w_out, v_attn_w_qkv, v_attn_rel_bias, v_attn_w_o, v_ln_mix_g, v_ln_mix_b, v_ffn_w_up, v_ffn_dw_w, v_ffn_dw_b, v_ffn_w_down, v_ple_w_proj, v_ple_w_gate, v_ple_b_gate, v_ln_ffn_g, v_ln_ffn_b):
    xi, yi, ci = _place()
    shard_idx = (2 * xi + yi).astype(jnp.int32)
    s_arr = shard_idx.reshape(1)
    c_arr = ci.astype(jnp.int32).reshape(1)
    sc_arr = jnp.concatenate([s_arr, c_arr])

    x0 = x[0]
    target = loss_target[0]
    p_rows = p.reshape(p.shape[0] * p.shape[2], p.shape[3])
    seq = x0.shape[0]

    big = [
        ("mix_w_in", mix_w_in, m_mix_w_in, v_mix_w_in, True),
        ("mix_w_out", mix_w_out, m_mix_w_out, v_mix_w_out, False),
        ("attn_w_qkv", attn_w_qkv, m_attn_w_qkv, v_attn_w_qkv, True),
        ("attn_w_o", attn_w_o, m_attn_w_o, v_attn_w_o, False),
        ("ffn_w_up", ffn_w_up, m_ffn_w_up, v_ffn_w_up, True),
        ("ffn_w_down", ffn_w_down, m_ffn_w_down, v_ffn_w_down, False),
        ("ple_w_proj", ple_w_proj, m_ple_w_proj, v_ple_w_proj, True),
        ("ple_w_gate", ple_w_gate, m_ple_w_gate, v_ple_w_gate, False),
    ]
    params = {nm: w for nm, w, _, _, _ in big}
    col_sharded = {nm: cs for nm, _, _, _, cs in big}
    keys = [("mix_w_in", 0), ("mix_w_out", 0), ("ffn_w_up", 0), ("ffn_w_down", 0), ("ple_w_gate", 0),
            ("ple_w_proj", 0), ("attn_w_qkv", 0), ("attn_w_o", 0), ("ffn_w_up", 1), ("ffn_w_down", 1),
            ("ple_w_gate", 1), ("ple_w_proj", 1)]
    dw_shapes = [conv_dw_w.shape, ffn_dw_w.shape]
    dw_block = cast_into_gathered("place_dw", _pack([conv_dw_w, ffn_dw_w])[None], 0, s_arr, dtype=F32)
    n_first = 2
    started = {}
    gather_token = None
    for tag, group in (("first", keys[:n_first]), ("rest", keys[n_first:])):
        shards = [cast_into_gathered(f"cast_{nm}_{layer}", params[nm], layer, s_arr, token=gather_token)
                  for nm, layer in group]
        if tag == "first":
            shards.append(dw_block)
        send, recv, bufs, gather_token = copies_start(f"gather_start_{tag}", shards, gather_plan, 3 * len(shards))
        for a, key in enumerate(group):
            started[key] = (send, recv, bufs[a], 3 * a)
        if tag == "first":
            dw_started = (send, recv, bufs[-1], 3 * len(group))
    arrived_w = {}

    def weight(nm, layer, after=None):
        key = (nm, layer)
        if key not in arrived_w:
            send, recv, buf, base = started[key]
            arrived_w[key] = copies_wait(f"gather_wait_{nm}_{layer}", [buf], send, recv, gather_plan, base, after)[0]
        g = arrived_w[key]
        if col_sharded[nm]:
            return g
        return g.reshape(g.shape[0] * g.shape[1], g.shape[2])

    def tie(a, token):
        return a + token[0:1, 0:1].astype(a.dtype)

    class Reducer:
        def __init__(self, tag, group):
            self.tag, self.group, self.stage = tag, group, 0
            self.n = len(group)
            self.result = None

        def advance(self, after):
            tag, n = self.tag, self.n
            if self.stage == 0:
                grads = []
                for key in self.group:
                    g = big_grads[key]
                    grads.append(g if g.ndim == 3 else g.reshape(N_SHARD, g.shape[0] // N_SHARD, g.shape[1]))
                lands = [lax.empty((N_SHARD, g.shape[1] // 2, g.shape[2]), F32) for g in grads]
                self.sems = copies_start(f"swap_start_{tag}", grads + lands, swap_plan, n)
            elif self.stage == 1:
                send, recv, bufs, _ = self.sems
                outs = copies_wait(f"swap_wait_{tag}", bufs, send, recv, swap_plan, 0, after)
                self.own, wire = [], []
                for key, g, ld in zip(self.group, outs[:n], outs[n:]):
                    o, ob = add_halves(f"add_halves_{key[0]}_{key[1]}", g, ld, sc_arr)
                    self.own.append(o)
                    wire.append(ob)
                lands = [lax.empty((3,) + w.shape[1:], BF16) for w in wire]
                self.sems = copies_start(f"owners_start_{tag}", wire + lands, owners_plan, 3 * n)
            elif self.stage == 2:
                send, recv, bufs, _ = self.sems
                outs = copies_wait(f"owners_wait_{tag}", bufs, send, recv, owners_plan, 0, after)
                finals = [add_owned(f"add_owned_{key[0]}_{key[1]}", o, ar, sc_arr)
                          for key, o, ar in zip(self.group, self.own, outs[n:])]
                self.sems = copies_start(f"join_start_{tag}", finals, join_plan, n)
            elif self.stage == 3:
                send, recv, bufs, _ = self.sems
                outs = copies_wait(f"join_wait_{tag}", bufs, send, recv, join_plan, 0, after)
                self.result = dict(zip(self.group, outs))
                self.sems = None
            self.stage += 1
            return None if self.sems is None else self.sems[3]

    dw_cache = []

    def conv_weights(after):
        if not dw_cache:
            send, recv, buf, base = dw_started
            dw_all = copies_wait("gather_wait_dw", [buf], send, recv, gather_plan, base, after)[0]
            dw_parts = [_unpack(dw_all[k], dw_shapes) for k in range(N_SHARD)]
            dw_cache.append(jnp.concatenate([pc[0] for pc in dw_parts], axis=2)[0])
            dw_cache.append(jnp.concatenate([pc[1] for pc in dw_parts], axis=2))
        return dw_cache

    big_grads = {}
    small_grads = {}

    saved = []
    h_in = x0
    h_in_b = to_bf16("x_bf16", x0)
    for layer in range(N_LAYERS):
        sv = {"x_in": h_in_b}
        if layer % 2 == 0:
            u = mm_cols_fwd("mix_in", h_in_b, weight("mix_w_in", 0, gather_token), F32)
            conv_w_full, ffn_dw_full = conv_weights(u)
            cat, d_sv, e_sv, glu_sv, hh_sv, rs_sv = mixer_fwd(
                "mixer_fwd", u, pool_w[0], pool_scale, conv_w_full, conv_dw_b, conv_ln_g, conv_ln_b)
            mix = mm_rows_fwd("mix_out", cat, weight("mix_w_out", 0, cat))
            sv.update(u=u, cat=cat, d=d_sv, e=e_sv, glu=glu_sv, hh=hh_sv, rs=rs_sv)
        else:
            qkvp = mm_cols_fwd("attn_qkv", h_in_b, weight("attn_w_qkv", 0, h_in_b), BF16,
                               pad_blocks=PAD_ROWS // _row_tile(seq))
            bias = bias_tile("bias_tile", _bias_line(attn_rel_bias[0]))
            att = attn_fwd("attn_fwd", qkvp, bias)
            mix = mm_rows_fwd("attn_out", att, weight("attn_w_o", 0, att))
            sv.update(qkvp=qkvp, bias=bias, att=att)
        x1, x1_b, xh1, rs1 = ln_fwd(f"ln_mix_{layer}", h_in, mix, ln_mix_g[layer:layer + 1],
                                    ln_mix_b[layer:layer + 1])
        gv = mm_cols_fwd(f"ffn_up_{layer}", x1_b, weight("ffn_w_up", layer, x1_b), F32)
        hid = ffn_act_fwd(f"ffn_act_{layer}", gv, ffn_dw_full[layer], ffn_dw_b[layer:layer + 1])
        ffn = mm_rows_fwd(f"ffn_down_{layer}", hid, weight("ffn_w_down", layer, hid))
        pgl = mm_rows_fwd(f"ple_gate_{layer}", x1_b, weight("ple_w_gate", layer, ffn))
        pp = mm_cols_fwd(f"ple_proj_{layer}", p_rows, weight("ple_w_proj", layer, pgl), F32, part=(layer, N_LAYERS))
        bg = ple_b_gate[layer:layer + 1]
        x2, x2_b, xh2, rs2 = ln_fwd(f"ln_ffn_{layer}", x1, ffn, ln_ffn_g[layer:layer + 1], ln_ffn_b[layer:layer + 1],
                                    ple=(pgl, pp, bg), emit_y=layer < N_LAYERS - 1)
        sv.update(x1=x1_b, xh1=xh1, rs1=rs1, gv=gv, hid=hid, pgl=pgl, pp=pp, xh2=xh2, rs2=rs2)
        saved.append(sv)
        h_in, h_in_b = x2, x2_b

    reducers = []

    def open_group(tag, group):
        reducers.append(Reducer(tag, group))
        return reducers[-1].advance(None)

    def hook(after):
        token = None
        for red in reducers:
            if red.stage < 4:
                tk = red.advance(after)
                if tk is not None:
                    token = tk if token is None else token + tk
        return token

    def tied(a, token):
        return a if token is None else tie(a, token)

    parts = []
    token = None
    for layer in reversed(range(N_LAYERS)):
        sv = saved[layer]
        bg = ple_b_gate[layer:layer + 1]
        if layer == 0:
            token = open_group("layer1", [("attn_w_qkv", 0), ("attn_w_o", 0), ("ffn_w_up", 1), ("ffn_w_down", 1),
                                          ("ple_w_gate", 1), ("ple_w_proj", 1)])
        last = layer == N_LAYERS - 1
        res = ln_bwd(
            f"ln_ffn_bwd_{layer}", parts, sv["xh2"], sv["rs2"], tied(ln_ffn_g[layer:layer + 1], token),
            ple=(sv["pgl"], sv["pp"], bg), loss=(target, ln_ffn_b[layer:layer + 1]) if last else None)
        dz2, dg2, db2, dpp, dpgl, dbg = res[:6]
        if last:
            loss_part = res[6]
        small_grads[("ln_ffn_g", layer)] = dg2
        small_grads[("ln_ffn_b", layer)] = db2
        small_grads[("ple_b_gate", layer)] = dbg
        w_down = weight("ffn_w_down", layer)
        dhid = mm_rows_dx(f"ffn_down_dx_{layer}", dz2, w_down)
        big_grads[("ffn_w_down", layer)] = mm_rows_dw(f"ffn_down_dw_{layer}", sv["hid"], dz2)
        token = hook(big_grads[("ffn_w_down", layer)])
        dgv, ddw, ddb = ffn_act_bwd(f"ffn_act_bwd_{layer}", dhid, sv["gv"], ffn_dw_full[layer],
                                    tied(ffn_dw_b[layer:layer + 1], token))
        small_grads[("ffn_dw_w", layer)] = ddw
        small_grads[("ffn_dw_b", layer)] = ddb
        big_grads[("ffn_w_up", layer)] = mm_cols_dw(f"ffn_up_dw_{layer}", sv["x1"], dgv)
        t_up = mm_cols_dx(f"ffn_up_dx_{layer}", dgv, weight("ffn_w_up", layer))
        token = hook(t_up)
        big_grads[("ple_w_gate", layer)] = mm_rows_dw(f"ple_gate_dw_{layer}", sv["x1"], dpgl)
        t_gate = mm_rows_dx(f"ple_gate_dx_{layer}", dpgl, weight("ple_w_gate", layer))
        big_grads[("ple_w_proj", layer)] = mm_cols_dw(f"ple_proj_dw_{layer}", p_rows, dpp, part=(layer, N_LAYERS))
        token2 = hook(big_grads[("ple_w_proj", layer)])
        if token2 is not None:
            token = token2 if token is None else token + token2
        if layer == 0:
            token3 = open_group("layer0_ffn", [("ffn_w_up", 0), ("ffn_w_down", 0), ("ple_w_gate", 0), ("ple_w_proj", 0)])
            token = token3 if token is None else token + token3
        dz1, dg1, db1 = ln_bwd(
            f"ln_mix_bwd_{layer}", [(ALPHA, dz2), (1.0, t_up), (1.0, t_gate)], sv["xh1"], sv["rs1"],
            tied(ln_mix_g[layer:layer + 1], token))
        small_grads[("ln_mix_g", layer)] = dg1
        small_grads[("ln_mix_b", layer)] = db1
        if layer % 2 == 0:
            dcat = mm_rows_dx("mix_out_dx", dz1, weight("mix_w_out", 0))
            big_grads[("mix_w_out", 0)] = mm_rows_dw("mix_out_dw", sv["cat"], dz1)
            token = hook(big_grads[("mix_w_out", 0)])
            du, dpw, dps, dcw, dcb, dcg, dcbt = mixer_bwd(
                "mixer_bwd", dcat, sv["u"], sv["d"], sv["e"], sv["glu"], sv["hh"], sv["rs"],
                pool_w[0], pool_scale, conv_w_full, tied(conv_ln_g, token), conv_ln_b)
            small_grads[("pool_w", 0)] = dpw
            small_grads[("pool_scale", 0)] = dps
            small_grads[("conv_dw_w", 0)] = dcw
            small_grads[("conv_dw_b", 0)] = dcb
            small_grads[("conv_ln_g", 0)] = dcg
            small_grads[("conv_ln_b", 0)] = dcbt
            big_grads[("mix_w_in", 0)] = mm_cols_dw("mix_in_dw", sv["x_in"], du)
            hook(big_grads[("mix_w_in", 0)])
            open_group("layer0_mix", [("mix_w_in", 0), ("mix_w_out", 0)])
            dx_in = mm_cols_dx("mix_in_dx", du, weight("mix_w_in", 0), addend=(ALPHA, dz1))
            token = hook(dx_in)
        else:
            do = mm_rows_dx("attn_out_dx", dz1, weight("attn_w_o", 0), out_dtype=BF16)
            big_grads[("attn_w_o", 0)] = mm_rows_dw("attn_out_dw", sv["att"], dz1)
            dq, dk, dv, ds_sum = attn_bwd("attn_bwd", sv["qkvp"], sv["bias"], do)
            cols, sat = bias_grad_reduce("bias_grad", ds_sum)
            d_rel = jnp.concatenate(
                [jnp.zeros((N_HEADS, 1), F32),
                 jnp.flip(cols[:, 0, Q_TILE + SHEAR_SAT:Q_TILE - 1 + SHEAR_W], axis=1),
                 sat[:, 0, 0:1]], axis=1)
            small_grads[("attn_rel_bias", 0)] = d_rel
            dqkv = jnp.concatenate([dq, dk, dv], axis=1)
            big_grads[("attn_w_qkv", 0)] = mm_cols_dw("attn_qkv_dw", sv["x_in"], dqkv)
            dx_in = mm_cols_dx("attn_qkv_dx", dqkv, weight("attn_w_qkv", 0), addend=(ALPHA, dz1))
        parts = [(1.0, dx_in)]
    grad_x = dx_in

    small = [
        ("pool_w", pool_w, m_pool_w, v_pool_w, None),
        ("pool_scale", pool_scale, m_pool_scale, v_pool_scale, None),
        ("conv_dw_w", conv_dw_w, m_conv_dw_w, v_conv_dw_w, 2),
        ("conv_dw_b", conv_dw_b, m_conv_dw_b, v_conv_dw_b, None),
        ("conv_ln_g", conv_ln_g, m_conv_ln_g, v_conv_ln_g, None),
        ("conv_ln_b", conv_ln_b, m_conv_ln_b, v_conv_ln_b, None),
        ("attn_rel_bias", attn_rel_bias, m_attn_rel_bias, v_attn_rel_bias, None),
        ("ln_mix_g", ln_mix_g, m_ln_mix_g, v_ln_mix_g, None),
        ("ln_mix_b", ln_mix_b, m_ln_mix_b, v_ln_mix_b, None),
        ("ffn_dw_w", ffn_dw_w, m_ffn_dw_w, v_ffn_dw_w, 2),
        ("ffn_dw_b", ffn_dw_b, m_ffn_dw_b, v_ffn_dw_b, None),
        ("ple_b_gate", ple_b_gate, m_ple_b_gate, v_ple_b_gate, None),
        ("ln_ffn_g", ln_ffn_g, m_ln_ffn_g, v_ln_ffn_g, None),
        ("ln_ffn_b", ln_ffn_b, m_ln_ffn_b, v_ln_ffn_b, None),
    ]
    full_grads = []
    for nm, w, _, _, shard_axis in small:
        full = list(w.shape)
        if shard_axis is not None:
            full[shard_axis] *= N_SHARD
        per_layer = [small_grads[(nm, layer)].reshape((1,) + tuple(full[1:])) for layer in range(w.shape[0])]
        full_grads.append(jnp.concatenate(per_layer, axis=0))
    packed = _pack(full_grads + [loss_part])
    dev_arr = (4 * xi + 2 * yi + ci).astype(jnp.int32).reshape(1)
    sg_block = cast_into_gathered("place_small_grads", packed[None], 0, dev_arr, n_blocks=8, dtype=F32)
    sg_send, sg_recv, sg_bufs, sg_token = copies_start("small_grads_start", [sg_block], all_plan, 7)
    token = sg_token if token is None else token + sg_token

    shard_grads = {}
    for red in reducers:
        if red.stage == 4:
            shard_grads.update(red.result)
    big_out = {}

    def update_big(names, tok):
        for nm, w, m, v, _ in big:
            if nm in names:
                gl = [shard_grads[(nm, layer)] for layer in range(w.shape[0])]
                delta, new_m, new_v = adamw(f"adamw_{nm}", w, gl, m, v, token=tok)
                big_out[nm] = (jnp.stack(gl, axis=0), delta, new_m, new_v)

    last_group = ("mix_w_in", "mix_w_out")
    update_big([nm for nm, _, _, _, _ in big if nm not in last_group], token)
    token = hook(big_out["ffn_w_up"][1])

    gathered_sg = copies_wait("small_grads_wait", sg_bufs, sg_send, sg_recv, all_plan, 0, big_out["ffn_w_down"][1])[0]
    total = sum_blocks("sum_small", gathered_sg.reshape(8 * packed.shape[0], LANES), 8)
    unpacked = _unpack(total, [g.shape for g in full_grads] + [loss_part.shape])
    loss = unpacked[-1][0, 0]
    local_grads = []
    for (nm, w, _, _, shard_axis), g in zip(small, unpacked[:-1]):
        if shard_axis is not None:
            width = w.shape[shard_axis]
            g = lax.dynamic_slice_in_dim(g, shard_idx * width, width, axis=shard_axis)
        local_grads.append(g.reshape(w.shape))
    updated = adamw_many("adamw_small", [w for _, w, _, _, _ in small], local_grads,
                         [m for _, _, m, _, _ in small], [v for _, _, _, v, _ in small], token)
    hook(updated[0][0])
    for red in reducers:
        shard_grads.update(red.result)
    update_big(last_group, None)
    small_out = {}
    for (nm, _, _, _, _), g, (d_, m_, v_) in zip(small, local_grads, updated):
        small_out[nm] = (g, d_, m_, v_)

    order = ["mix_w_in", "pool_w", "pool_scale", "conv_dw_w", "conv_dw_b", "conv_ln_g", "conv_ln_b", "mix_w_out",
             "attn_w_qkv", "attn_rel_bias", "attn_w_o", "ln_mix_g", "ln_mix_b", "ffn_w_up", "ffn_dw_w", "ffn_dw_b",
             "ffn_w_down", "ple_w_proj", "ple_w_gate", "ple_b_gate", "ln_ffn_g", "ln_ffn_b"]
    res = {**big_out, **small_out}
    outs = [loss, grad_x[None]]
    for slot in range(4):
        outs += [res[nm][slot] for nm in order]
    return tuple(outs)
```

```python
import functools
import math

import jax
import jax.numpy as jnp
from jax import lax
from jax.experimental import pallas as pl
from jax.experimental.pallas import tpu as pltpu

F32 = jnp.float32
BF16 = jnp.bfloat16
MESH = pl.DeviceIdType.MESH

N_LAYERS = 2
ALPHA = (2 * N_LAYERS) ** 0.25
LN_EPS = 1e-5
NEG_INF = -1e30
CHUNK = 64
LEFT_CHUNKS = 8
PAD_ROWS = LEFT_CHUNKS * CHUNK
HEAD_DIM = 64
ATTN_SCALE = HEAD_DIM ** -0.5
N_HEADS = 16
MAX_REL = 256
POOL_WINDOWS = (2, 4, 8, 16)
POOL_GROUP = 128
CONV_K = 31
FFN_K = 3
CONV_HALO = 32
FFN_HALO = 8
FFN_TILE = 256
FFN_CHUNK_ROWS = 64
FFN_CHUNK_LANES = 128
Q_TILE = 256
K_WIN = Q_TILE + PAD_ROWS
LANES = 128
SUBLANES = 8
ATTN_PAIRS = 2
ATTN_PAIRS_FWD = 4
ATTN_LANES = ATTN_PAIRS * LANES
SHEAR_W = Q_TILE + K_WIN
SHEAR_SAT = SHEAR_W - 2 * MAX_REL
N_SHARD = 4

ADAM_LR = 0.001
ADAM_B1 = 0.9
ADAM_B2 = 0.999
ADAM_EPS = 1e-08
ADAM_WD = 0.01
ADAM_STEP = 10
ADAM_BC1 = 1.0 - ADAM_B1 ** ADAM_STEP
ADAM_BC2 = 1.0 - ADAM_B2 ** ADAM_STEP

DIMS = {
    "nn": (((1,), (0,)), ((), ())),
    "nt": (((1,), (1,)), ((), ())),
    "tn": (((0,), (0,)), ((), ())),
}


def _cp(vmem_mb=48, **kw):
    return pltpu.CompilerParams(vmem_limit_bytes=vmem_mb * 1024 * 1024, **kw)


def _in_hbm(a):
    return pltpu.with_memory_space_constraint(a, pltpu.HBM)


STAGING_LIMIT_BYTES = 1 << 20
SMALL_WEIGHT_BYTES = 1 << 20
SUM_BLOCK_ROWS = 2048
SMALL_BLOCK_BYTES = 1 << 19


def _call(body, **kw):
    call = pl.pallas_call(body, **kw)

    def run(*args):
        pinned = []
        for a in args:
            big = a.size * a.dtype.itemsize >= STAGING_LIMIT_BYTES
            pinned.append(_in_hbm(a) if big and not jnp.issubdtype(a.dtype, jnp.integer) else a)
        return call(*pinned)

    return run


def _dot(a, b, mode):
    return lax.dot_general(a.astype(BF16), b.astype(BF16), DIMS[mode], preferred_element_type=F32)


def _sig(x):
    return 1.0 / (1.0 + jnp.exp(-x))


def _row_tile(s):
    return min(512, s // 4)


def _mm_tile(s):
    return min(1024, s // 4)


def _mm(name, mode, a, b, in_specs, out_shape, out_spec, acc_shape, grid, nk, zero_first=False, vmem_mb=48,
        addend=None):
    out_f32 = out_shape.dtype == F32

    def body(a_ref, b_ref, *rest):
        k = pl.program_id(2)
        if addend is None:
            o_ref, scr = rest[0], rest[1:]
        else:
            add_ref, o_ref, scr = rest[0], rest[1], rest[2:]

        def compute():
            part = _dot(a_ref[...], b_ref[...], mode)
            if nk == 1:
                if addend is not None:
                    part = part + addend[0] * add_ref[...]
                o_ref[...] = part.astype(o_ref.dtype)
                return
            acc = o_ref if out_f32 else scr[0]

            @pl.when(k == 0)
            def _():
                acc[...] = part if addend is None else part + addend[0] * add_ref[...]

            @pl.when(k > 0)
            def _():
                acc[...] += part

            if not out_f32:
                @pl.when(k == nk - 1)
                def _():
                    o_ref[...] = acc[...].astype(o_ref.dtype)

        if zero_first:
            @pl.when(pl.program_id(1) == 0)
            def _():
                o_ref[...] = jnp.zeros(o_ref.shape, o_ref.dtype)

            pl.when(pl.program_id(1) > 0)(compute)
        else:
            compute()

    scratch = [] if (nk == 1 or out_f32) else [pltpu.VMEM(acc_shape, F32)]
    operands = [a, b] if addend is None else [a, b, addend[1]]
    specs = list(in_specs) if addend is None else list(in_specs) + [out_spec]
    return _call(
        body, name=name, grid=grid, in_specs=specs, out_specs=out_spec, out_shape=out_shape,
        scratch_shapes=scratch, compiler_params=_cp(vmem_mb),
    )(*operands)


def _is_small_weight(wc):
    return wc.size * 2 <= SMALL_WEIGHT_BYTES


def mm_cols_fwd(name, a, wc, out_dtype, pad_blocks=0, part=(0, 1)):
    s, k = a.shape
    s //= part[1]
    n4 = wc.shape[2]
    tm = _row_tile(s) if pad_blocks else _mm_tile(s)
    nt = s // tm
    first_block = part[0] * nt
    if _is_small_weight(wc) and not pad_blocks:
        def body(a_ref, w_ref, o_ref):
            a_blk = a_ref[...]
            for j in range(N_SHARD):
                o_ref[:, j * n4:(j + 1) * n4] = _dot(a_blk, w_ref[j], "nn").astype(o_ref.dtype)

        return _call(
            body, name=name, grid=(nt,),
            in_specs=[pl.BlockSpec((tm, k), lambda i: (first_block + i, 0)), _full(wc.shape)],
            out_specs=pl.BlockSpec((tm, N_SHARD * n4), lambda i: (i, 0)),
            out_shape=jax.ShapeDtypeStruct((s, N_SHARD * n4), out_dtype), compiler_params=_cp(),
        )(a, wc)
    return _mm(
        name, "nn", a, wc,
        [pl.BlockSpec((tm, k), lambda j, i, r: (first_block + jnp.maximum(i - pad_blocks, 0), 0)),
         pl.BlockSpec((None, k, n4), lambda j, i, r: (j, 0, 0))],
        jax.ShapeDtypeStruct((s + pad_blocks * tm, N_SHARD * n4), out_dtype),
        pl.BlockSpec((tm, n4), lambda j, i, r: (i, j)),
        None, (N_SHARD, nt + pad_blocks, 1), 1, zero_first=pad_blocks > 0)


def mm_cols_dx(name, dy, wc, addend=None):
    s = dy.shape[0]
    _, k, n4 = wc.shape
    tm = _mm_tile(s)
    return _mm(
        name, "nt", dy, wc,
        [pl.BlockSpec((tm, n4), lambda g, i, r: (i, r)),
         pl.BlockSpec((None, k, n4), lambda g, i, r: (r, 0, 0))],
        jax.ShapeDtypeStruct((s, k), F32),
        pl.BlockSpec((tm, k), lambda g, i, r: (i, 0)),
        (tm, k), (1, s // tm, N_SHARD), N_SHARD, addend=addend)


def mm_cols_dw(name, a, dy, part=(0, 1)):
    s, k = a.shape
    s //= part[1]
    n4 = dy.shape[1] // N_SHARD
    tm = _mm_tile(s)
    nt = s // tm
    first_block = part[0] * nt
    if k * n4 * N_SHARD * 2 <= SMALL_WEIGHT_BYTES:
        def body(a_ref, dy_ref, o_ref):
            a_blk = a_ref[...]
            first = pl.program_id(0) == 0
            for j in range(N_SHARD):
                _acc_add(o_ref.at[j], first, _dot(a_blk, dy_ref[:, j * n4:(j + 1) * n4], "tn"))

        return _call(
            body, name=name, grid=(nt,),
            in_specs=[pl.BlockSpec((tm, k), lambda r: (first_block + r, 0)),
                      pl.BlockSpec((tm, N_SHARD * n4), lambda r: (r, 0))],
            out_specs=_full((N_SHARD, k, n4)),
            out_shape=jax.ShapeDtypeStruct((N_SHARD, k, n4), F32), compiler_params=_cp(),
        )(a, dy)
    return _mm(
        name, "tn", a, dy,
        [pl.BlockSpec((tm, k), lambda j, g, r: (first_block + r, 0)),
         pl.BlockSpec((tm, n4), lambda j, g, r: (r, j))],
        jax.ShapeDtypeStruct((N_SHARD, k, n4), F32),
        pl.BlockSpec((None, k, n4), lambda j, g, r: (j, 0, 0)),
        (k, n4), (N_SHARD, 1, nt), nt)


def _k_tile(k):
    return k if k <= 1024 else k // 2


def mm_rows_fwd(name, a, wr, out_dtype=F32):
    s, k = a.shape
    n = wr.shape[1]
    tm = _mm_tile(s)
    tk = _k_tile(k)
    nk = k // tk
    return _mm(
        name, "nn", a, wr,
        [pl.BlockSpec((tm, tk), lambda g, i, r: (i, r)),
         pl.BlockSpec((tk, n), lambda g, i, r: (r, 0))],
        jax.ShapeDtypeStruct((s, n), out_dtype),
        pl.BlockSpec((tm, n), lambda g, i, r: (i, 0)),
        (tm, n), (1, s // tm, nk), nk)


def mm_rows_dx(name, dy, wr, out_dtype=F32):
    s, n = dy.shape
    k = wr.shape[0]
    tm = _mm_tile(s)
    tk = _k_tile(k)
    return _mm(
        name, "nt", dy, wr,
        [pl.BlockSpec((tm, n), lambda j, i, r: (i, 0)),
         pl.BlockSpec((tk, n), lambda j, i, r: (j, 0))],
        jax.ShapeDtypeStruct((s, k), out_dtype),
        pl.BlockSpec((tm, tk), lambda j, i, r: (i, j)),
        None, (k // tk, s // tm, 1), 1)


def mm_rows_dw(name, a, dy):
    s, k = a.shape
    n = dy.shape[1]
    tm = _mm_tile(s)
    tk = _k_tile(k)
    nt = s // tm
    return _mm(
        name, "tn", a, dy,
        [pl.BlockSpec((tm, tk), lambda j, g, r: (r, j)),
         pl.BlockSpec((tm, n), lambda j, g, r: (r, 0))],
        jax.ShapeDtypeStruct((k, n), F32),
        pl.BlockSpec((tk, n), lambda j, g, r: (j, 0)),
        (tk, n), (k // tk, 1, nt), nt)


def _row(tm, c, col=0):
    return pl.BlockSpec((tm, c), lambda i: (i, col))


def _full(shape):
    nd = len(shape)
    return pl.BlockSpec(shape, lambda i: (0,) * nd)


def _prev(tm, h, c, col=0):
    return pl.BlockSpec((h, c), lambda i: (jnp.maximum(i * (tm // h) - 1, 0), col))


def _next(tm, h, c, s, col=0):
    return pl.BlockSpec((h, c), lambda i: (jnp.minimum((i + 1) * (tm // h), s // h - 1), col))


def _acc_add(ref, first, val):
    @pl.when(first)
    def _():
        ref[...] = val

    @pl.when(jnp.logical_not(first))
    def _():
        ref[...] += val


def _colsum(v):
    return jnp.sum(v, axis=0, keepdims=True)


def _ln_stats(z):
    mu = jnp.mean(z, axis=-1, keepdims=True)
    zc = z - mu
    var = jnp.mean(zc * zc, axis=-1, keepdims=True)
    rstd = lax.rsqrt(var + LN_EPS)
    return zc * rstd, rstd


def _ln_bwd(dxhat, xhat, rstd):
    m1 = jnp.mean(dxhat, axis=-1, keepdims=True)
    m2 = jnp.mean(dxhat * xhat, axis=-1, keepdims=True)
    return rstd * (dxhat - m1 - xhat * m2)


def ln_fwd(name, x, f, g, b, ple=None, emit_y=True):
    s, d = x.shape
    tm = _row_tile(s)
    n_in = 2 + (3 if ple is not None else 0)

    def body(*refs):
        x_ref, f_ref = refs[0], refs[1]
        g_ref, b_ref = refs[n_in], refs[n_in + 1]
        xh_ref, rs_ref = refs[-2:]
        z = ALPHA * x_ref[...] + f_ref[...]
        if ple is not None:
            pgl_ref, pp_ref, bg_ref = refs[2:5]
            z = z + _sig(pgl_ref[...] + bg_ref[...]) * pp_ref[...]
        xhat, rstd = _ln_stats(z)
        if emit_y:
            y = xhat * g_ref[...] + b_ref[...]
            refs[n_in + 2][...] = y
            refs[n_in + 3][...] = y.astype(BF16)
        xh_ref[...] = xhat
        rs_ref[...] = jnp.broadcast_to(rstd, rs_ref.shape)

    ins = [x, f]
    specs = [_row(tm, d), _row(tm, d)]
    if ple is not None:
        pgl, pp, bg = ple
        ins += [pgl, pp, bg]
        specs += [_row(tm, d), _row(tm, d), _full((1, d))]
    ins += [g, b]
    specs += [_full((1, d)), _full((1, d))]
    y_shapes = [jax.ShapeDtypeStruct((s, d), F32), jax.ShapeDtypeStruct((s, d), BF16)] if emit_y else []
    outs = _call(
        body, name=name, grid=(s // tm,), in_specs=specs,
        out_specs=[_row(tm, d)] * (len(y_shapes) + 1) + [_row(tm, LANES)],
        out_shape=y_shapes + [jax.ShapeDtypeStruct((s, d), F32), jax.ShapeDtypeStruct((s, LANES), F32)],
        compiler_params=_cp(),
    )(*ins)
    return tuple(outs) if emit_y else (None, None, outs[0], outs[1])


def ln_bwd(name, parts, xhat, rstd, g, ple=None, loss=None):
    s, d = xhat.shape
    tm = _row_tile(s)
    coefs = [c for c, _ in parts]
    n_p = len(parts)
    n_ple = 3 if ple is not None else 0
    n_in = n_p + 3 + n_ple + (2 if loss is not None else 0)

    def body(*refs):
        first = pl.program_id(0) == 0
        xh = refs[n_p][...]
        rs = refs[n_p + 1][:, 0:1]
        g_v = refs[n_p + 2][...]
        outs = refs[n_in:]
        if loss is not None:
            t_ref, b_ref = refs[n_p + 3 + n_ple:n_p + 5 + n_ple]
            err = (xh * g_v + b_ref[...]) - t_ref[...]
            dy = err * (1.0 / d)
            part = 0.5 * jnp.sum(jnp.mean(err * err, axis=-1, keepdims=True), axis=0, keepdims=True)
            _acc_add(outs[-1], first, jnp.broadcast_to(part, outs[-1].shape))
        else:
            dy = coefs[0] * refs[0][...].astype(F32)
            for j in range(1, n_p):
                dy = dy + coefs[j] * refs[j][...].astype(F32)
        dz = _ln_bwd(dy * g_v, xh, rs)
        outs[0][...] = dz
        _acc_add(outs[1], first, _colsum(dy * xh))
        _acc_add(outs[2], first, _colsum(dy))
        if ple is not None:
            pgl_ref, pp_ref, bg_ref = refs[n_p + 3:n_p + 6]
            pg = _sig(pgl_ref[...] + bg_ref[...])
            dpgl = dz * pp_ref[...] * pg * (1.0 - pg)
            outs[3][...] = (dz * pg).astype(BF16)
            outs[4][...] = dpgl.astype(BF16)
            _acc_add(outs[5], first, _colsum(dpgl))

    ins = [p for _, p in parts] + [xhat, rstd, g]
    specs = [_row(tm, d)] * n_p + [_row(tm, d), _row(tm, LANES), _full((1, d))]
    out_specs = [_row(tm, d), _full((1, d)), _full((1, d))]
    out_shape = [jax.ShapeDtypeStruct((s, d), F32), jax.ShapeDtypeStruct((1, d), F32),
                 jax.ShapeDtypeStruct((1, d), F32)]
    if ple is not None:
        pgl, pp, bg = ple
        ins += [pgl, pp, bg]
        specs += [_row(tm, d), _row(tm, d), _full((1, d))]
        out_specs += [_row(tm, d), _row(tm, d), _full((1, d))]
        out_shape += [jax.ShapeDtypeStruct((s, d), BF16), jax.ShapeDtypeStruct((s, d), BF16),
                      jax.ShapeDtypeStruct((1, d), F32)]
    if loss is not None:
        target, b = loss
        ins += [target, b]
        specs += [_row(tm, d), _full((1, d))]
        out_specs += [_full((8, LANES))]
        out_shape += [jax.ShapeDtypeStruct((8, LANES), F32)]
    return _call(
        body, name=name, grid=(s // tm,), in_specs=specs, out_specs=out_specs, out_shape=out_shape,
        compiler_params=_cp(),
    )(*ins)


def _fill_rotations(rot_ref, x, direction):
    n = x.shape[0]
    rot_ref[0] = x
    for b in range(1, SUBLANES):
        if direction < 0:
            rot_ref[b, SUBLANES:n, :] = x[SUBLANES - b:n - b]
        else:
            rot_ref[b, 0:n - SUBLANES, :] = x[b:n - SUBLANES + b]


def _rotated(rot_ref, start, rows, cs, direction=-1):
    b = (-start) % SUBLANES if direction < 0 else start % SUBLANES
    aligned = start + b if direction < 0 else start - b
    return rot_ref[b, pl.ds(aligned, rows), cs]


def _tile_pos(i, tm, rows):
    return (i * tm + lax.broadcasted_iota(jnp.int32, (rows, 1), 0) + 1).astype(F32)


def mixer_fwd(name, u, pool_w, pool_scale, conv_w, conv_b, cn_g, cn_b):
    s = u.shape[0]
    dp = 512
    tm = min(256, s // 4)
    h = CONV_HALO

    def body(a_c, a_p, bv_c, bv_p, bg_c, bg_p, pw_ref, ps_ref, cw_ref, cb_ref, cg_ref, cbt_ref,
             cat_ref, d_ref, e_ref, glu_ref, hh_ref, rs_ref, ext_a, rot_g, conv_out):
        i = pl.program_id(0)
        first = i == 0
        ext_a[0:h, :] = jnp.where(first, 0.0, a_p[...])
        ext_a[h:, :] = a_c[...]
        glu = bv_c[...] * _sig(bg_c[...])
        glu_ref[...] = glu
        _fill_rotations(rot_g, jnp.concatenate([jnp.where(first, 0.0, bv_p[...] * _sig(bg_p[...])), glu], axis=0), -1)
        pos = _tile_pos(i, tm, tm)
        for gi, w in enumerate(POOL_WINDOWS):
            cs = slice(gi * POOL_GROUP, (gi + 1) * POOL_GROUP)
            a_g = ext_a[pl.ds(h, tm), cs]
            acc = a_g
            for sh in range(1, w):
                acc = acc + ext_a[pl.ds(h - sh, tm), cs]
            d_g = acc / jnp.minimum(pos, float(w)) - a_g
            d_ref[:, cs] = d_g.astype(BF16)
            e_g = _dot(d_g, pw_ref[gi], "nn")
            e_ref[:, cs] = e_g
            cat_ref[:, cs] = (e_g * ps_ref[:, cs]).astype(BF16)
        for lg in range(dp // LANES):
            cs = slice(lg * LANES, (lg + 1) * LANES)
            acc = jnp.broadcast_to(cb_ref[:, cs], (tm, LANES))
            for sh in range(CONV_K):
                acc = acc + _rotated(rot_g, h - sh, tm, cs) * cw_ref[pl.ds(CONV_K - 1 - sh, 1), cs]
            conv_out[:, cs] = acc
        hhat, rstd = _ln_stats(conv_out[...])
        hl = hhat * cg_ref[...] + cbt_ref[...]
        cat_ref[:, dp:] = (hl * _sig(hl)).astype(BF16)
        hh_ref[...] = hhat
        rs_ref[...] = jnp.broadcast_to(rstd, rs_ref.shape)

    specs = [_row(tm, dp, 0), _prev(tm, h, dp, 0), _row(tm, dp, 1), _prev(tm, h, dp, 1),
             _row(tm, dp, 2), _prev(tm, h, dp, 2),
             _full((4, POOL_GROUP, POOL_GROUP)), _full((1, dp)), _full((CONV_K, dp)),
             _full((1, dp)), _full((1, dp)), _full((1, dp))]
    out_specs = [_row(tm, 2 * dp), _row(tm, dp), _row(tm, dp), _row(tm, dp), _row(tm, dp), _row(tm, LANES)]
    out_shape = [jax.ShapeDtypeStruct((s, 2 * dp), BF16), jax.ShapeDtypeStruct((s, dp), BF16),
                 jax.ShapeDtypeStruct((s, dp), F32), jax.ShapeDtypeStruct((s, dp), F32),
                 jax.ShapeDtypeStruct((s, dp), F32), jax.ShapeDtypeStruct((s, LANES), F32)]
    return _call(
        body, name=name, grid=(s // tm,), in_specs=specs, out_specs=out_specs, out_shape=out_shape,
        scratch_shapes=[pltpu.VMEM((h + tm, dp), F32), pltpu.VMEM((SUBLANES, h + tm, dp), F32),
                        pltpu.VMEM((tm, dp), F32)],
        compiler_params=_cp(),
    )(u, u, u, u, u, u, pool_w, pool_scale, conv_w, conv_b, cn_g, cn_b)


def mixer_bwd(name, dcat, u, d_sv, e_sv, glu_sv, hh_sv, rs_sv, pool_w, pool_scale, conv_w, cn_g, cn_b):
    s = u.shape[0]
    dp = 512
    tm = min(256, s // 4)
    h = CONV_HALO
    nt = s // tm

    def body(dc_c, dc_n, bv_c, bg_c, d_c, e_c, gl_c, gl_p, hh_c, hh_n, rs_c, rs_n,
             pw_ref, ps_ref, cw_ref, cg_ref, cbt_ref,
             du_ref, dpw_ref, dps_ref, dcw_ref, dcb_ref, dcg_ref, dcbt_ref,
             ext_dh, ext_g, ext_r):
        i = pl.program_id(0)
        first = i == 0
        last = i == nt - 1
        cg = cg_ref[...]

        def conv_grads(dyb, hhat, rstd):
            hl = hhat * cg + cbt_ref[...]
            sg = _sig(hl)
            dhl = dyb * (sg * (1.0 + hl * (1.0 - sg)))
            return _ln_bwd(dhl * cg, hhat, rstd), dhl

        hh_cur = hh_c[...]
        dh_c, dhl_c = conv_grads(dc_c[:, dp:], hh_cur, rs_c[:, 0:1])
        dh_n, _ = conv_grads(dc_n[:, dp:], hh_n[...], rs_n[:, 0:1])
        _fill_rotations(ext_dh, jnp.concatenate([dh_c, jnp.where(last, 0.0, dh_n)], axis=0), 1)
        _fill_rotations(ext_g, jnp.concatenate([jnp.where(first, 0.0, gl_p[...]), gl_c[...]], axis=0), -1)

        @pl.when(first)
        def _():
            dcw_ref[...] = jnp.zeros(dcw_ref.shape, F32)

        for lg in range(dp // LANES):
            cs = slice(lg * LANES, (lg + 1) * LANES)
            dglu = jnp.zeros((tm, LANES), F32)
            for sh in range(CONV_K):
                dglu = dglu + _rotated(ext_dh, sh, tm, cs, 1) * cw_ref[pl.ds(CONV_K - 1 - sh, 1), cs]
            dh_g = ext_dh[0, pl.ds(0, tm), cs]
            for sh in range(CONV_K):
                dcw_ref[pl.ds(CONV_K - 1 - sh, 1), cs] += _colsum(dh_g * _rotated(ext_g, h - sh, tm, cs))
            sgate = _sig(bg_c[:, cs])
            du_ref[:, dp + lg * LANES:dp + (lg + 1) * LANES] = dglu * sgate
            du_ref[:, 2 * dp + lg * LANES:2 * dp + (lg + 1) * LANES] = dglu * bv_c[:, cs] * sgate * (1.0 - sgate)
        _acc_add(dcb_ref, first, _colsum(dh_c))
        _acc_add(dcg_ref, first, _colsum(dhl_c * hh_cur))
        _acc_add(dcbt_ref, first, _colsum(dhl_c))

        pos_c = _tile_pos(i, tm, tm)
        pos_n = _tile_pos(i + 1, tm, h)
        _acc_add(dps_ref, first, _colsum(dc_c[:, :dp] * e_c[...]))
        for gi, w in enumerate(POOL_WINDOWS):
            cs = slice(gi * POOL_GROUP, (gi + 1) * POOL_GROUP)
            pw = pw_ref[gi]
            de_c = dc_c[:, cs] * ps_ref[:, cs]
            de_n = dc_n[:, cs] * ps_ref[:, cs]
            dd_c = _dot(de_c, pw, "nt")
            dd_n = _dot(de_n, pw, "nt")
            ext_r[0:tm, :] = dd_c / jnp.minimum(pos_c, float(w))
            ext_r[tm:, :] = jnp.where(last, 0.0, dd_n / jnp.minimum(pos_n, float(w)))
            acc = -dd_c
            for sh in range(w):
                acc = acc + ext_r[pl.ds(sh, tm), :]
            du_ref[:, cs] = acc
            dpw_g = _dot(d_c[:, cs], de_c, "tn")

            @pl.when(first)
            def _():
                dpw_ref[gi] = dpw_g

            @pl.when(jnp.logical_not(first))
            def _():
                dpw_ref[gi] += dpw_g

    specs = [_row(tm, 2 * dp), _next(tm, h, 2 * dp, s), _row(tm, dp, 1), _row(tm, dp, 2),
             _row(tm, dp), _row(tm, dp), _row(tm, dp), _prev(tm, h, dp),
             _row(tm, dp), _next(tm, h, dp, s), _row(tm, LANES), _next(tm, h, LANES, s),
             _full((4, POOL_GROUP, POOL_GROUP)), _full((1, dp)), _full((CONV_K, dp)),
             _full((1, dp)), _full((1, dp))]
    out_specs = [_row(tm, 3 * dp), _full((4, POOL_GROUP, POOL_GROUP)), _full((1, dp)), _full((CONV_K, dp)),
                 _full((1, dp)), _full((1, dp)), _full((1, dp))]
    out_shape = [jax.ShapeDtypeStruct((s, 3 * dp), F32),
                 jax.ShapeDtypeStruct((4, POOL_GROUP, POOL_GROUP), F32), jax.ShapeDtypeStruct((1, dp), F32),
                 jax.ShapeDtypeStruct((CONV_K, dp), F32), jax.ShapeDtypeStruct((1, dp), F32),
                 jax.ShapeDtypeStruct((1, dp), F32), jax.ShapeDtypeStruct((1, dp), F32)]
    return _call(
        body, name=name, grid=(nt,), in_specs=specs, out_specs=out_specs, out_shape=out_shape,
        scratch_shapes=[pltpu.VMEM((SUBLANES, tm + h, dp), F32), pltpu.VMEM((SUBLANES, h + tm, dp), F32),
                        pltpu.VMEM((tm + h, POOL_GROUP), F32)],
        compiler_params=_cp(),
    )(dcat, dcat, u, u, d_sv, e_sv, glu_sv, glu_sv, hh_sv, hh_sv, rs_sv, rs_sv,
      pool_w, pool_scale, conv_w, cn_g, cn_b)


GELU_C = math.sqrt(2.0 / math.pi)


def _gelu_parts(x):
    x2 = x * x
    t = jnp.tanh(x * (GELU_C + (GELU_C * 0.044715) * x2))
    half_1pt = 0.5 + 0.5 * t
    gelu = x * half_1pt
    dgelu = half_1pt + (0.5 * x) * (1.0 - t * t) * (GELU_C + (3.0 * GELU_C * 0.044715) * x2)
    return gelu, dgelu


def ffn_act_fwd(name, gv, dw_w, dw_b):
    s = gv.shape[0]
    dff = gv.shape[1] // 2
    tm = min(FFN_TILE, s // 4)
    h = FFN_HALO
    rc = FFN_CHUNK_ROWS
    lw = FFN_CHUNK_LANES

    def body(g_c, g_p, v_c, w_ref, b_ref, hid_ref):
        first = pl.program_id(0) == 0

        def chunk(ci, carry):
            r0 = pl.multiple_of(ci * rc, rc)
            above = pl.multiple_of(jnp.maximum(r0 - h, 0), h)
            for lg in range(dff // lw):
                cs = slice(lg * lw, (lg + 1) * lw)
                top = jnp.where(ci == 0, jnp.where(first, 0.0, g_p[:, cs]), g_c[pl.ds(above, h), cs])
                win = jnp.concatenate([top, g_c[pl.ds(r0, rc), cs]], axis=0)
                gc = jnp.broadcast_to(b_ref[:, cs], (rc, lw))
                for sh in range(FFN_K):
                    gc = gc + win[h - sh:h - sh + rc] * w_ref[pl.ds(FFN_K - 1 - sh, 1), cs]
                gelu, _ = _gelu_parts(gc)
                hid_ref[pl.ds(r0, rc), cs] = (gelu * v_c[pl.ds(r0, rc), cs]).astype(BF16)
            return carry

        lax.fori_loop(0, tm // rc, chunk, 0)

    return _call(
        body, name=name, grid=(s // tm,),
        in_specs=[_row(tm, dff, 0), _prev(tm, h, dff, 0), _row(tm, dff, 1), _full((FFN_K, dff)), _full((1, dff))],
        out_specs=_row(tm, dff), out_shape=jax.ShapeDtypeStruct((s, dff), BF16),
        compiler_params=_cp(),
    )(gv, gv, gv, dw_w, dw_b)


def ffn_act_bwd(name, dhid, gv, dw_w, dw_b):
    s = gv.shape[0]
    dff = gv.shape[1] // 2
    tm = min(FFN_TILE, s // 4)
    h = FFN_HALO
    nt = s // tm
    rc = FFN_CHUNK_ROWS
    lw = FFN_CHUNK_LANES
    n_chunks = tm // rc

    def body(dh_c, dh_n, g_p, g_c, g_n, v_c, v_n, w_ref, b_ref, dgv_ref, dw_ref, db_ref):
        i = pl.program_id(0)
        first = i == 0
        last = i == nt - 1

        @pl.when(first)
        def _():
            dw_ref[...] = jnp.zeros(dw_ref.shape, F32)
            db_ref[...] = jnp.zeros(db_ref.shape, F32)

        def chunk(ci, carry):
            r0 = pl.multiple_of(ci * rc, rc)
            above = pl.multiple_of(jnp.maximum(r0 - h, 0), h)
            below = pl.multiple_of(jnp.minimum(r0 + rc, tm - h), h)
            at_end = ci == n_chunks - 1
            for lg in range(dff // lw):
                cs = slice(lg * lw, (lg + 1) * lw)
                top = jnp.where(ci == 0, jnp.where(first, 0.0, g_p[:, cs]), g_c[pl.ds(above, h), cs])
                bot = jnp.where(at_end, g_n[:, cs], g_c[pl.ds(below, h), cs])
                win = jnp.concatenate([top, g_c[pl.ds(r0, rc), cs], bot], axis=0)
                shifted = [win[h - sh:h - sh + rc + h] for sh in range(FFN_K)]
                gc = jnp.broadcast_to(b_ref[:, cs], (rc + h, lw))
                for sh in range(FFN_K):
                    gc = gc + shifted[sh] * w_ref[pl.ds(FFN_K - 1 - sh, 1), cs]
                gelu, dgelu = _gelu_parts(gc)
                dh_mid = dh_c[pl.ds(r0, rc), cs]
                hv_bot = jnp.where(at_end, jnp.where(last, 0.0, dh_n[:, cs] * v_n[:, cs]),
                                   dh_c[pl.ds(below, h), cs] * v_c[pl.ds(below, h), cs])
                dgc = jnp.concatenate([dh_mid * v_c[pl.ds(r0, rc), cs], hv_bot], axis=0) * dgelu
                dgate = jnp.zeros((rc, lw), F32)
                for sh in range(FFN_K):
                    dgate = dgate + dgc[sh:sh + rc] * w_ref[pl.ds(FFN_K - 1 - sh, 1), cs]
                dgv_ref[pl.ds(r0, rc), cs] = dgate.astype(BF16)
                dgv_ref[pl.ds(r0, rc), slice(dff + lg * lw, dff + (lg + 1) * lw)] = (dh_mid * gelu[0:rc]).astype(BF16)
                dgc_mid = dgc[0:rc]
                for sh in range(FFN_K):
                    dw_ref[pl.ds(FFN_K - 1 - sh, 1), cs] += _colsum(dgc_mid * shifted[sh][0:rc])
                db_ref[:, cs] += _colsum(dgc_mid)
            return carry

        lax.fori_loop(0, n_chunks, chunk, 0)

    return _call(
        body, name=name, grid=(nt,),
        in_specs=[_row(tm, dff), _next(tm, h, dff, s),
                  _prev(tm, h, dff, 0), _row(tm, dff, 0), _next(tm, h, dff, s, 0),
                  _row(tm, dff, 1), _next(tm, h, dff, s, 1),
                  _full((FFN_K, dff)), _full((1, dff))],
        out_specs=[_row(tm, 2 * dff), _full((FFN_K, dff)), _full((1, dff))],
        out_shape=[jax.ShapeDtypeStruct((s, 2 * dff), BF16), jax.ShapeDtypeStruct((FFN_K, dff), F32),
                   jax.ShapeDtypeStruct((1, dff), F32)],
        compiler_params=_cp(),
    )(dhid, dhid, gv, gv, gv, gv, gv, dw_w, dw_b)


def _bias_line(rel_bias):
    nh = rel_bias.shape[0]
    line = jnp.concatenate(
        [jnp.zeros((nh, 1), rel_bias.dtype), jnp.broadcast_to(rel_bias[:, 2 * MAX_REL:], (nh, SHEAR_SAT)),
         jnp.flip(rel_bias[:, 1:2 * MAX_REL], axis=1)], axis=1)
    return line[:, None, :]


def bias_tile(name, line):
    nh = line.shape[0]

    def body(l_ref, o_ref):
        x = jnp.broadcast_to(l_ref[...], (Q_TILE, SHEAR_W))
        z = pltpu.roll(x, SHEAR_W - Q_TILE, 1, stride=1, stride_axis=0)
        qc = lax.broadcasted_iota(jnp.int32, (Q_TILE, K_WIN), 0) // CHUNK
        kc = lax.broadcasted_iota(jnp.int32, (Q_TILE, K_WIN), 1) // CHUNK
        o_ref[...] = jnp.where((kc >= qc) & (kc <= qc + LEFT_CHUNKS), z[:, :K_WIN], NEG_INF)

    return _call(
        body, name=name, grid=(nh,), in_specs=[pl.BlockSpec((None, 1, SHEAR_W), lambda hh: (hh, 0, 0))],
        out_specs=pl.BlockSpec((None, Q_TILE, K_WIN), lambda hh: (hh, 0, 0)),
        out_shape=jax.ShapeDtypeStruct((nh, Q_TILE, K_WIN), F32), compiler_params=_cp(),
    )(line)


def _stack_heads(x2, scale=None):
    if scale is not None:
        x2 = x2 * jnp.asarray(scale, x2.dtype)
    lane = lax.broadcasted_iota(jnp.int32, x2.shape, 1)
    zero = jnp.zeros_like(x2)
    return jnp.concatenate([jnp.where(lane < HEAD_DIM, x2, zero), jnp.where(lane < HEAD_DIM, zero, x2)], axis=0)


def _unstack_heads(x_st):
    lane = lax.broadcasted_iota(jnp.int32, (Q_TILE, LANES), 1)
    return jnp.where(lane < HEAD_DIM, x_st[:Q_TILE], x_st[Q_TILE:])


def _attn_probs(q_st, k3, bias_st, t):
    sc = _dot(q_st, k3, "nt") + bias_st
    col = lax.broadcasted_iota(jnp.int32, sc.shape, 1)
    sc = jnp.where(col >= PAD_ROWS - t * Q_TILE, sc, NEG_INF)
    m = jnp.max(sc, axis=-1, keepdims=True)
    p = jnp.exp(sc - m)
    return p * (1.0 / jnp.sum(p, axis=-1, keepdims=True))


def _attn_specs(d_model, pairs):
    nq = PAD_ROWS // Q_TILE
    width = pairs * LANES
    groups = d_model // width
    specs = [pl.BlockSpec((Q_TILE, width), lambda g, t: (t + nq, g))]
    for which in (1, 2):
        for j in range(K_WIN // Q_TILE):
            specs.append(pl.BlockSpec((Q_TILE, width), lambda g, t, j=j, which=which: (t + j, which * groups + g)))
    specs.append(pl.BlockSpec((2 * pairs, Q_TILE, K_WIN), lambda g, t: (g, 0, 0)))
    return specs


def attn_fwd(name, qkvp, bias):
    s = qkvp.shape[0] - PAD_ROWS
    d_model = qkvp.shape[1] // 3
    nw = K_WIN // Q_TILE

    def body(q_ref, *refs):
        k_refs, v_refs, b_ref, o_ref = refs[:nw], refs[nw:2 * nw], refs[2 * nw], refs[2 * nw + 1]
        t = pl.program_id(1)
        for j in range(ATTN_PAIRS_FWD):
            ls = slice(j * LANES, (j + 1) * LANES)
            k3 = jnp.concatenate([r[:, ls] for r in k_refs], axis=0)
            v3 = jnp.concatenate([r[:, ls] for r in v_refs], axis=0)
            bias_st = b_ref[2 * j:2 * j + 2].reshape(2 * Q_TILE, K_WIN)
            p = _attn_probs(_stack_heads(q_ref[:, ls], ATTN_SCALE), k3, bias_st, t)
            o_ref[:, ls] = _unstack_heads(_dot(p, v3, "nn")).astype(BF16)

    width = ATTN_PAIRS_FWD * LANES
    return _call(
        body, name=name, grid=(d_model // width, s // Q_TILE),
        in_specs=_attn_specs(d_model, ATTN_PAIRS_FWD), out_specs=pl.BlockSpec((Q_TILE, width), lambda g, t: (t, g)),
        out_shape=jax.ShapeDtypeStruct((s, d_model), BF16), compiler_params=_cp(),
    )(qkvp, *([qkvp] * (2 * nw)), bias)


def attn_bwd(name, qkvp, bias, do):
    s = qkvp.shape[0] - PAD_ROWS
    d_model = qkvp.shape[1] // 3
    nw = K_WIN // Q_TILE
    nt = s // Q_TILE

    def body(q_ref, *refs):
        k_refs, v_refs = refs[:nw], refs[nw:2 * nw]
        b_ref, do_ref, dq_ref, dk_ref, dv_ref, ds_ref, dk_acc, dv_acc = refs[2 * nw:]
        t = pl.program_id(1)
        first = t == 0

        @pl.when(first)
        def _():
            dk_acc[...] = jnp.zeros(dk_acc.shape, F32)
            dv_acc[...] = jnp.zeros(dv_acc.shape, F32)

        start = pl.multiple_of(t * Q_TILE, Q_TILE)
        for j in range(ATTN_PAIRS):
            ls = slice(j * LANES, (j + 1) * LANES)
            q_st = _stack_heads(q_ref[:, ls], ATTN_SCALE)
            do_st = _stack_heads(do_ref[:, ls])
            k3 = jnp.concatenate([r[:, ls] for r in k_refs], axis=0)
            v3 = jnp.concatenate([r[:, ls] for r in v_refs], axis=0)
            p = _attn_probs(q_st, k3, b_ref[2 * j:2 * j + 2].reshape(2 * Q_TILE, K_WIN), t)
            dp = _dot(do_st, v3, "nt")
            ds = p * (dp - jnp.sum(p * dp, axis=-1, keepdims=True))
            _acc_add(ds_ref.at[2 * j:2 * j + 2], first, ds.reshape(2, Q_TILE, K_WIN))
            dsb = ds.astype(BF16)
            dq_ref[:, ls] = (_unstack_heads(_dot(dsb, k3, "nn")) * ATTN_SCALE).astype(BF16)
            dk_acc[pl.ds(start, K_WIN), ls] += _dot(dsb, q_st, "tn")
            dv_acc[pl.ds(start, K_WIN), ls] += _dot(p, do_st, "tn")

        @pl.when(t == nt - 1)
        def _():
            dk_ref[...] = dk_acc[pl.ds(PAD_ROWS, s), :].astype(BF16)
            dv_ref[...] = dv_acc[pl.ds(PAD_ROWS, s), :].astype(BF16)

    specs = _attn_specs(d_model, ATTN_PAIRS) + [pl.BlockSpec((Q_TILE, ATTN_LANES), lambda g, t: (t, g))]
    col_spec = pl.BlockSpec((s, ATTN_LANES), lambda g, t: (0, g))
    return _call(
        body, name=name, grid=(d_model // ATTN_LANES, nt), in_specs=specs,
        out_specs=[pl.BlockSpec((Q_TILE, ATTN_LANES), lambda g, t: (t, g)), col_spec, col_spec,
                   pl.BlockSpec((2 * ATTN_PAIRS, Q_TILE, K_WIN), lambda g, t: (g, 0, 0))],
        out_shape=[jax.ShapeDtypeStruct((s, d_model), BF16)] * 3
        + [jax.ShapeDtypeStruct((N_HEADS, Q_TILE, K_WIN), F32)],
        scratch_shapes=[pltpu.VMEM((PAD_ROWS + s, ATTN_LANES), F32), pltpu.VMEM((PAD_ROWS + s, ATTN_LANES), F32)],
        compiler_params=_cp(),
    )(qkvp, *([qkvp] * (2 * nw)), bias, do)


def bias_grad_reduce(name, ds_sum):
    nh = ds_sum.shape[0]
    width = SHEAR_W + Q_TILE
    first_k = Q_TILE - 1

    def body(x_ref, col_ref, sat_ref):
        x = x_ref[...]
        hi = x.astype(BF16)
        lo = (x - hi.astype(F32)).astype(BF16)
        r = lax.broadcasted_iota(jnp.int32, (Q_TILE, Q_TILE), 0)
        c = lax.broadcasted_iota(jnp.int32, (Q_TILE, Q_TILE), 1)
        exchange = jnp.where(r + c == Q_TILE - 1, 1.0, 0.0).astype(BF16)
        x_rev = _dot(exchange, hi, "nn") + _dot(exchange, lo, "nn")
        zeros = jnp.zeros((Q_TILE, Q_TILE), F32)
        y = pltpu.roll(jnp.concatenate([zeros, x_rev, zeros], axis=1), 0, 1, stride=1, stride_axis=0)
        cols = _colsum(y)
        col_ref[...] = cols
        k = lax.broadcasted_iota(jnp.int32, cols.shape, 1) - first_k
        tot = jnp.sum(jnp.where((k >= 1) & (k <= SHEAR_SAT), cols, 0.0), axis=-1, keepdims=True)
        sat_ref[...] = jnp.broadcast_to(tot, sat_ref.shape)

    return _call(
        body, name=name, grid=(nh,),
        in_specs=[pl.BlockSpec((None, Q_TILE, K_WIN), lambda hh: (hh, 0, 0))],
        out_specs=[pl.BlockSpec((None, 1, width), lambda hh: (hh, 0, 0)),
                   pl.BlockSpec((None, 1, LANES), lambda hh: (hh, 0, 0))],
        out_shape=[jax.ShapeDtypeStruct((nh, 1, width), F32), jax.ShapeDtypeStruct((nh, 1, LANES), F32)],
        compiler_params=_cp(),
    )(ds_sum)


def _ew_rows(r, most=512, cols=None):
    if cols is not None and r * cols * 4 <= SMALL_BLOCK_BYTES:
        return r
    for cand in range(min(most, r) // 16 * 16, 0, -16):
        if r % cand == 0:
            return cand
    return r


def to_bf16(name, a):
    s, d = a.shape
    tm = _row_tile(s)

    def body(a_ref, o_ref):
        o_ref[...] = a_ref[...].astype(BF16)

    return _call(
        body, name=name, grid=(s // tm,), in_specs=[_row(tm, d)], out_specs=_row(tm, d),
        out_shape=jax.ShapeDtypeStruct((s, d), BF16), compiler_params=_cp(),
    )(a)


def cast_into_gathered(name, w, layer, s_idx, n_blocks=N_SHARD, dtype=BF16, token=None):
    r, c = w.shape[-2:]
    tr = _ew_rows(r, cols=c)

    def body(s_ref, w_ref, *rest):
        rest[-1][...] = w_ref[...].astype(dtype)

    extra = [] if token is None else [token]
    grid_spec = pltpu.PrefetchScalarGridSpec(
        num_scalar_prefetch=1, grid=(r // tr,),
        in_specs=[pl.BlockSpec((None, tr, c), lambda i, s_ref: (layer, i, 0))] + [ANY_SPEC] * len(extra),
        out_specs=pl.BlockSpec((None, tr, c), lambda i, s_ref: (s_ref[0], i, 0)))
    return _call(
        body, name=name, grid_spec=grid_spec, out_shape=jax.ShapeDtypeStruct((n_blocks, r, c), dtype),
        compiler_params=_cp(),
    )(s_idx, w, *extra)


def adamw(name, w, grads, m, v, token=None):
    nl, r, c = w.shape
    tr = _ew_rows(r, 256, cols=c)

    def body(*refs):
        w_ref, m_ref, v_ref = refs[0], refs[1], refs[2]
        g_refs = refs[3:3 + nl]
        d_ref, nm_ref, nv_ref = refs[-3:]
        layer = pl.program_id(0)
        g = g_refs[0][...]
        for j in range(1, nl):
            g = jnp.where(layer == j, g_refs[j][...], g)
        d_ref[...], nm_ref[...], nv_ref[...] = _adamw_update(w_ref[...], g, m_ref[...], v_ref[...])

    p_spec = pl.BlockSpec((None, tr, c), lambda l, i: (l, i, 0))
    g_spec = pl.BlockSpec((tr, c), lambda l, i: (i, 0))
    extra = [] if token is None else [token]
    extra_specs = [] if token is None else [ANY_SPEC]
    return _call(
        body, name=name, grid=(nl, r // tr), in_specs=[p_spec] * 3 + [g_spec] * nl + extra_specs,
        out_specs=[p_spec] * 3, out_shape=[jax.ShapeDtypeStruct((nl, r, c), F32)] * 3, compiler_params=_cp(),
    )(w, m, v, *grads, *extra)


def _adamw_update(w, g, m, v):
    nm = ADAM_B1 * m + (1.0 - ADAM_B1) * g
    nv = ADAM_B2 * v + (1.0 - ADAM_B2) * (g * g)
    delta = -ADAM_LR * ((nm / ADAM_BC1) / (jnp.sqrt(nv / ADAM_BC2) + ADAM_EPS) + ADAM_WD * w)
    return delta, nm, nv


def adamw_many(name, ws, gs, ms, vs, token):
    n = len(ws)

    def body(*refs):
        ins, outs = refs[:4 * n], refs[4 * n + 1:]
        for i in range(n):
            delta, nm, nv = _adamw_update(ins[i][...], ins[n + i][...], ins[2 * n + i][...], ins[3 * n + i][...])
            outs[3 * i][...] = delta
            outs[3 * i + 1][...] = nm
            outs[3 * i + 2][...] = nv

    vmem = pl.BlockSpec(memory_space=pltpu.VMEM)
    shapes = [jax.ShapeDtypeStruct(w.shape, F32) for w in ws for _ in range(3)]
    outs = _call(
        body, name=name, in_specs=[vmem] * (4 * n) + [ANY_SPEC], out_specs=[vmem] * (3 * n), out_shape=shapes,
        compiler_params=_cp(),
    )(*ws, *gs, *ms, *vs, token)
    return [tuple(outs[3 * i:3 * i + 3]) for i in range(n)]


def sum_blocks(name, gathered, n_blocks):
    r = gathered.shape[0] // n_blocks
    c = gathered.shape[1]
    tr = r if r <= SUM_BLOCK_ROWS else _ew_rows(r)
    nt = r // tr

    def body(*refs):
        acc = refs[0][...]
        for j in range(1, n_blocks):
            acc = acc + refs[j][...]
        refs[-1][...] = acc

    specs = [pl.BlockSpec((tr, c), lambda i, j=j: (j * nt + i, 0)) for j in range(n_blocks)]
    return _call(
        body, name=name, grid=(nt,), in_specs=specs, out_specs=pl.BlockSpec((tr, c), lambda i: (i, 0)),
        out_shape=jax.ShapeDtypeStruct((r, c), F32), compiler_params=_cp(),
    )(*([gathered] * n_blocks))


def _place():
    return lax.axis_index("x"), lax.axis_index("y"), lax.axis_index("c")


def _other_chips(x, y):
    return [(1 - x, y), (x, 1 - y), (1 - x, 1 - y)]


HBM_SPEC = pl.BlockSpec(memory_space=pltpu.HBM)
SEM_SPEC = pl.BlockSpec(memory_space=pltpu.SEMAPHORE)
ANY_SPEC = pl.BlockSpec(memory_space=pl.ANY)
EFFECT = pltpu.SideEffectType.DATAFLOW_SIDE_EFFECTING


def copies_start(name, bufs, plan, n_copies):
    n = len(bufs)

    def body(*refs):
        send, recv = refs[n], refs[n + 1]
        token = refs[2 * n + 2]
        for k, (src, dst, peer, _) in enumerate(plan(refs[:n])):
            pltpu.make_async_remote_copy(
                src_ref=src, dst_ref=dst, send_sem=send.at[k], recv_sem=recv.at[k],
                device_id=peer, device_id_type=MESH).start()
        token[...] = jnp.zeros(token.shape, F32)

    outs = pl.pallas_call(
        body, name=name,
        out_shape=(pltpu.SemaphoreType.DMA((n_copies,)), pltpu.SemaphoreType.DMA((n_copies,)),
                   *[pltpu.HBM(b.shape, b.dtype) for b in bufs], jax.ShapeDtypeStruct((8, LANES), F32)),
        in_specs=[HBM_SPEC] * n,
        out_specs=(SEM_SPEC, SEM_SPEC, *([HBM_SPEC] * n), pl.BlockSpec(memory_space=pltpu.VMEM)),
        input_output_aliases={a: a + 2 for a in range(n)},
        compiler_params=pltpu.CompilerParams(has_side_effects=EFFECT),
    )(*[_in_hbm(b) for b in bufs])
    return outs[0], outs[1], list(outs[2:2 + n]), outs[2 + n]


def copies_wait(name, bufs, send, recv, plan, sem_base, after):
    n = len(bufs)

    def body(*refs):
        send_ref, recv_ref = refs[n], refs[n + 1]
        for k, (src, _, peer, land) in enumerate(plan(refs[:n])):
            cp = pltpu.make_async_remote_copy(
                src_ref=src, dst_ref=land, send_sem=send_ref.at[sem_base + k], recv_sem=recv_ref.at[sem_base + k],
                device_id=peer, device_id_type=MESH)
            cp.wait_send()
            cp.wait_recv()

    outs = pl.pallas_call(
        body, name=name,
        out_shape=tuple(pltpu.HBM(b.shape, b.dtype) for b in bufs),
        in_specs=[HBM_SPEC] * n + [SEM_SPEC, SEM_SPEC, ANY_SPEC], out_specs=tuple([HBM_SPEC] * n),
        input_output_aliases={a: a for a in range(n)},
        compiler_params=pltpu.CompilerParams(has_side_effects=EFFECT),
    )(*bufs, send, recv, after)
    return list(outs)


def gather_plan(refs):
    x, y, c = _place()
    me = 2 * x + y
    return [(buf.at[me], buf.at[me], (cx, cy, c), buf.at[2 * cx + cy])
            for buf in refs for cx, cy in _other_chips(x, y)]


def all_plan(refs):
    x, y, c = _place()
    me = 4 * x + 2 * y + c
    out = []
    for buf in refs:
        for flip in range(1, 8):
            px = 1 - x if flip & 4 else x
            py = 1 - y if flip & 2 else y
            pc = 1 - c if flip & 1 else c
            out.append((buf.at[me], buf.at[me], (px, py, pc), buf.at[4 * px + 2 * py + pc]))
    return out


def swap_plan(refs):
    x, y, c = _place()
    n = len(refs) // 2
    out = []
    for g, land in zip(refs[:n], refs[n:]):
        hr = g.shape[1] // 2
        out.append((g.at[:, pl.ds((1 - c) * hr, hr)], land, (x, y, 1 - c), land))
    return out


def owners_plan(refs):
    x, y, c = _place()
    n = len(refs) // 2
    return [(src.at[2 * cx + cy], land.at[j], (cx, cy, c), land.at[j])
            for src, land in zip(refs[:n], refs[n:]) for j, (cx, cy) in enumerate(_other_chips(x, y))]


def join_plan(refs):
    x, y, c = _place()
    out = []
    for buf in refs:
        hr = buf.shape[0] // 2
        mine = buf.at[pl.ds(c * hr, hr)]
        out.append((mine, mine, (x, y, 1 - c), buf.at[pl.ds((1 - c) * hr, hr)]))
    return out


def add_halves(name, grad, landed, sc_idx):
    _, r, c = grad.shape
    hr = r // 2
    tr = _ew_rows(hr)
    nt = hr // tr

    def body(sc_ref, g_ref, l_ref, own_ref, wire_ref):
        tot = g_ref[...] + l_ref[...]
        wire_ref[...] = tot.astype(BF16)

        @pl.when(pl.program_id(1) == sc_ref[0])
        def _():
            own_ref[...] = tot

    grid_spec = pltpu.PrefetchScalarGridSpec(
        num_scalar_prefetch=1, grid=(nt, N_SHARD),
        in_specs=[pl.BlockSpec((None, tr, c), lambda i, sh, sc_ref: (sh, sc_ref[1] * nt + i, 0)),
                  pl.BlockSpec((None, tr, c), lambda i, sh, sc_ref: (sh, i, 0))],
        out_specs=[pl.BlockSpec((tr, c), lambda i, sh, sc_ref: (i, 0)),
                   pl.BlockSpec((None, tr, c), lambda i, sh, sc_ref: (sh, i, 0))])
    return _call(
        body, name=name, grid_spec=grid_spec,
        out_shape=[jax.ShapeDtypeStruct((hr, c), F32), jax.ShapeDtypeStruct((N_SHARD, hr, c), BF16)],
        compiler_params=_cp(),
    )(sc_idx, grad, landed)


def add_owned(name, own, landed, sc_idx):
    hr, c = own.shape
    tr = _ew_rows(hr)
    nt = hr // tr

    def body(sc_ref, o_ref, l0, l1, l2, out_ref):
        out_ref[...] = ((o_ref[...] + l0[...].astype(F32)) + l1[...].astype(F32)) + l2[...].astype(F32)

    grid_spec = pltpu.PrefetchScalarGridSpec(
        num_scalar_prefetch=1, grid=(nt,),
        in_specs=[pl.BlockSpec((tr, c), lambda i, sc_ref: (i, 0))]
        + [pl.BlockSpec((None, tr, c), lambda i, sc_ref, j=j: (j, i, 0)) for j in range(3)],
        out_specs=pl.BlockSpec((tr, c), lambda i, sc_ref: (sc_ref[1] * nt + i, 0)))
    return _call(
        body, name=name, grid_spec=grid_spec, out_shape=jax.ShapeDtypeStruct((2 * hr, c), F32),
        compiler_params=_cp(),
    )(sc_idx, own, landed, landed, landed)


PACK_QUANTUM = 8 * LANES


def _pack(arrays):
    pieces = []
    for a in arrays:
        flat = a.reshape(-1)
        padded = -(-flat.shape[0] // PACK_QUANTUM) * PACK_QUANTUM
        pieces.append(jnp.pad(flat, (0, padded - flat.shape[0])).reshape(-1, LANES))
    return jnp.concatenate(pieces, axis=0)


def _unpack(packed, shapes):
    out = []
    row = 0
    for shp in shapes:
        size = math.prod(shp)
        rows = -(-size // PACK_QUANTUM) * 8
        out.append(packed[row:row + rows].reshape(-1)[:size].reshape(shp))
        row += rows
    return out


def kernel(x, p, mix_w_in, pool_w, pool_scale, conv_dw_w, conv_dw_b, conv_ln_g, conv_ln_b, mix_w_out, attn_w_qkv, attn_rel_bias, attn_w_o, ln_mix_g, ln_mix_b, ffn_w_up, ffn_dw_w, ffn_dw_b, ffn_w_down, ple_w_proj, ple_w_gate, ple_b_gate, ln_ffn_g, ln_ffn_b, loss_target, m_mix_w_in, m_pool_w, m_pool_scale, m_conv_dw_w, m_conv_dw_b, m_conv_ln_g, m_conv_ln_b, m_mix_w_out, m_attn_w_qkv, m_attn_rel_bias, m_attn_w_o, m_ln_mix_g, m_ln_mix_b, m_ffn_w_up, m_ffn_dw_w, m_ffn_dw_b, m_ffn_w_down, m_ple_w_proj, m_ple_w_gate, m_ple_b_gate, m_ln_ffn_g, m_ln_ffn_b, v_mix_w_in, v_pool_w, v_pool_scale, v_conv_dw_w, v_conv_dw_b, v_conv_ln_g, v_conv_ln_b, v_mix_w_out, v_attn_w_qkv, v_attn_rel_bias, v_attn_w_o, v_ln_mix_g, v_ln_mix_b, v_ffn_w_up, v_ffn_dw_w, v_ffn_dw_b, v_ffn_w_down, v_ple_w_proj, v_ple_w_gate, v_ple_b_gate, v_ln_ffn_g, v_ln_ffn_b):
    xi, yi, ci = _place()
    shard_idx = (2 * xi + yi).astype(jnp.int32)
    s_arr = shard_idx.reshape(1)
    c_arr = ci.astype(jnp.int32).reshape(1)
    sc_arr = jnp.concatenate([s_arr, c_arr])

    x0 = x[0]
    target = loss_target[0]
    p_rows = p.reshape(p.shape[0] * p.shape[2], p.shape[3])
    seq = x0.shape[0]

    big = [
        ("mix_w_in", mix_w_in, m_mix_w_in, v_mix_w_in, True),
        ("mix_w_out", mix_w_out, m_mix_w_out, v_mix_w_out, False),
        ("attn_w_qkv", attn_w_qkv, m_attn_w_qkv, v_attn_w_qkv, True),
        ("attn_w_o", attn_w_o, m_attn_w_o, v_attn_w_o, False),
        ("ffn_w_up", ffn_w_up, m_ffn_w_up, v_ffn_w_up, True),
        ("ffn_w_down", ffn_w_down, m_ffn_w_down, v_ffn_w_down, False),
        ("ple_w_proj", ple_w_proj, m_ple_w_proj, v_ple_w_proj, True),
        ("ple_w_gate", ple_w_gate, m_ple_w_gate, v_ple_w_gate, False),
    ]
    params = {nm: w for nm, w, _, _, _ in big}
    col_sharded = {nm: cs for nm, _, _, _, cs in big}
    keys = [("mix_w_in", 0), ("mix_w_out", 0), ("ffn_w_up", 0), ("ffn_w_down", 0), ("ple_w_gate", 0),
            ("ple_w_proj", 0), ("attn_w_qkv", 0), ("attn_w_o", 0), ("ffn_w_up", 1), ("ffn_w_down", 1),
            ("ple_w_gate", 1), ("ple_w_proj", 1)]
    dw_shapes = [conv_dw_w.shape, ffn_dw_w.shape]
    dw_block = cast_into_gathered("place_dw", _pack([conv_dw_w, ffn_dw_w])[None], 0, s_arr, dtype=F32)
    n_first = 2
    started = {}
    gather_token = None
    for tag, group in (("first", keys[:n_first]), ("rest", keys[n_first:])):
        shards = [cast_into_gathered(f"cast_{nm}_{layer}", params[nm], layer, s_arr, token=gather_token)
                  for nm, layer in group]
        if tag == "first":
            shards.append(dw_block)
        send, recv, bufs, gather_token = copies_start(f"gather_start_{tag}", shards, gather_plan, 3 * len(shards))
        for a, key in enumerate(group):
            started[key] = (send, recv, bufs[a], 3 * a)
        if tag == "first":
            dw_started = (send, recv, bufs[-1], 3 * len(group))
    arrived_w = {}

    def weight(nm, layer, after=None):
        key = (nm, layer)
        if key not in arrived_w:
            send, recv, buf, base = started[key]
            arrived_w[key] = copies_wait(f"gather_wait_{nm}_{layer}", [buf], send, recv, gather_plan, base, after)[0]
        g = arrived_w[key]
        if col_sharded[nm]:
            return g
        return g.reshape(g.shape[0] * g.shape[1], g.shape[2])

    def tie(a, token):
        return a + token[0:1, 0:1].astype(a.dtype)

    class Reducer:
        def __init__(self, tag, group):
            self.tag, self.group, self.stage = tag, group, 0
            self.n = len(group)
            self.result = None

        def advance(self, after):
            tag, n = self.tag, self.n
            if self.stage == 0:
                grads = []
                for key in self.group:
                    g = big_grads[key]
                    grads.append(g if g.ndim == 3 else g.reshape(N_SHARD, g.shape[0] // N_SHARD, g.shape[1]))
                lands = [lax.empty((N_SHARD, g.shape[1] // 2, g.shape[2]), F32) for g in grads]
                self.sems = copies_start(f"swap_start_{tag}", grads + lands, swap_plan, n)
            elif self.stage == 1:
                send, recv, bufs, _ = self.sems
                outs = copies_wait(f"swap_wait_{tag}", bufs, send, recv, swap_plan, 0, after)
                self.own, wire = [], []
                for key, g, ld in zip(self.group, outs[:n], outs[n:]):
                    o, ob = add_halves(f"add_halves_{key[0]}_{key[1]}", g, ld, sc_arr)
                    self.own.append(o)
                    wire.append(ob)
                lands = [lax.empty((3,) + w.shape[1:], BF16) for w in wire]
                self.sems = copies_start(f"owners_start_{tag}", wire + lands, owners_plan, 3 * n)
            elif self.stage == 2:
                send, recv, bufs, _ = self.sems
                outs = copies_wait(f"owners_wait_{tag}", bufs, send, recv, owners_plan, 0, after)
                finals = [add_owned(f"add_owned_{key[0]}_{key[1]}", o, ar, sc_arr)
                          for key, o, ar in zip(self.group, self.own, outs[n:])]
                self.sems = copies_start(f"join_start_{tag}", finals, join_plan, n)
            elif self.stage == 3:
                send, recv, bufs, _ = self.sems
                outs = copies_wait(f"join_wait_{tag}", bufs, send, recv, join_plan, 0, after)
                self.result = dict(zip(self.group, outs))
                self.sems = None
            self.stage += 1
            return None if self.sems is None else self.sems[3]

    dw_cache = []

    def conv_weights(after):
        if not dw_cache:
            send, recv, buf, base = dw_started
            dw_all = copies_wait("gather_wait_dw", [buf], send, recv, gather_plan, base, after)[0]
            dw_parts = [_unpack(dw_all[k], dw_shapes) for k in range(N_SHARD)]
            dw_cache.append(jnp.concatenate([pc[0] for pc in dw_parts], axis=2)[0])
            dw_cache.append(jnp.concatenate([pc[1] for pc in dw_parts], axis=2))
        return dw_cache

    big_grads = {}
    small_grads = {}

    saved = []
    h_in = x0
    h_in_b = to_bf16("x_bf16", x0)
    for layer in range(N_LAYERS):
        sv = {"x_in": h_in_b}
        if layer % 2 == 0:
            u = mm_cols_fwd("mix_in", h_in_b, weight("mix_w_in", 0, gather_token), F32)
            conv_w_full, ffn_dw_full = conv_weights(u)
            cat, d_sv, e_sv, glu_sv, hh_sv, rs_sv = mixer_fwd(
                "mixer_fwd", u, pool_w[0], pool_scale, conv_w_full, conv_dw_b, conv_ln_g, conv_ln_b)
            mix = mm_rows_fwd("mix_out", cat, weight("mix_w_out", 0, cat))
            sv.update(u=u, cat=cat, d=d_sv, e=e_sv, glu=glu_sv, hh=hh_sv, rs=rs_sv)
        else:
            qkvp = mm_cols_fwd("attn_qkv", h_in_b, weight("attn_w_qkv", 0, h_in_b), BF16,
                               pad_blocks=PAD_ROWS // _row_tile(seq))
            bias = bias_tile("bias_tile", _bias_line(attn_rel_bias[0]))
            att = attn_fwd("attn_fwd", qkvp, bias)
            mix = mm_rows_fwd("attn_out", att, weight("attn_w_o", 0, att))
            sv.update(qkvp=qkvp, bias=bias, att=att)
        x1, x1_b, xh1, rs1 = ln_fwd(f"ln_mix_{layer}", h_in, mix, ln_mix_g[layer:layer + 1],
                                    ln_mix_b[layer:layer + 1])
        gv = mm_cols_fwd(f"ffn_up_{layer}", x1_b, weight("ffn_w_up", layer, x1_b), F32)
        hid = ffn_act_fwd(f"ffn_act_{layer}", gv, ffn_dw_full[layer], ffn_dw_b[layer:layer + 1])
        ffn = mm_rows_fwd(f"ffn_down_{layer}", hid, weight("ffn_w_down", layer, hid))
        pgl = mm_rows_fwd(f"ple_gate_{layer}", x1_b, weight("ple_w_gate", layer, ffn))
        pp = mm_cols_fwd(f"ple_proj_{layer}", p_rows, weight("ple_w_proj", layer, pgl), F32, part=(layer, N_LAYERS))
        bg = ple_b_gate[layer:layer + 1]
        x2, x2_b, xh2, rs2 = ln_fwd(f"ln_ffn_{layer}", x1, ffn, ln_ffn_g[layer:layer + 1], ln_ffn_b[layer:layer + 1],
                                    ple=(pgl, pp, bg), emit_y=layer < N_LAYERS - 1)
        sv.update(x1=x1_b, xh1=xh1, rs1=rs1, gv=gv, hid=hid, pgl=pgl, pp=pp, xh2=xh2, rs2=rs2)
        saved.append(sv)
        h_in, h_in_b = x2, x2_b

    reducers = []

    def open_group(tag, group):
        reducers.append(Reducer(tag, group))
        return reducers[-1].advance(None)

    def hook(after):
        token = None
        for red in reducers:
            if red.stage < 4:
                tk = red.advance(after)
                if tk is not None:
                    token = tk if token is None else token + tk
        return token

    def tied(a, token):
        return a if token is None else tie(a, token)

    parts = []
    token = None
    for layer in reversed(range(N_LAYERS)):
        sv = saved[layer]
        bg = ple_b_gate[layer:layer + 1]
        if layer == 0:
            token = open_group("layer1", [("attn_w_qkv", 0), ("attn_w_o", 0), ("ffn_w_up", 1), ("ffn_w_down", 1),
                                          ("ple_w_gate", 1), ("ple_w_proj", 1)])
        last = layer == N_LAYERS - 1
        res = ln_bwd(
            f"ln_ffn_bwd_{layer}", parts, sv["xh2"], sv["rs2"], tied(ln_ffn_g[layer:layer + 1], token),
            ple=(sv["pgl"], sv["pp"], bg), loss=(target, ln_ffn_b[layer:layer + 1]) if last else None)
        dz2, dg2, db2, dpp, dpgl, dbg = res[:6]
        if last:
            loss_part = res[6]
        small_grads[("ln_ffn_g", layer)] = dg2
        small_grads[("ln_ffn_b", layer)] = db2
        small_grads[("ple_b_gate", layer)] = dbg
        w_down = weight("ffn_w_down", layer)
        dhid = mm_rows_dx(f"ffn_down_dx_{layer}", dz2, w_down)
        big_grads[("ffn_w_down", layer)] = mm_rows_dw(f"ffn_down_dw_{layer}", sv["hid"], dz2)
        token = hook(big_grads[("ffn_w_down", layer)])
        dgv, ddw, ddb = ffn_act_bwd(f"ffn_act_bwd_{layer}", dhid, sv["gv"], ffn_dw_full[layer],
                                    tied(ffn_dw_b[layer:layer + 1], token))
        small_grads[("ffn_dw_w", layer)] = ddw
        small_grads[("ffn_dw_b", layer)] = ddb
        big_grads[("ffn_w_up", layer)] = mm_cols_dw(f"ffn_up_dw_{layer}", sv["x1"], dgv)
        t_up = mm_cols_dx(f"ffn_up_dx_{layer}", dgv, weight("ffn_w_up", layer))
        token = hook(t_up)
        big_grads[("ple_w_gate", layer)] = mm_rows_dw(f"ple_gate_dw_{layer}", sv["x1"], dpgl)
        t_gate = mm_rows_dx(f"ple_gate_dx_{layer}", dpgl, weight("ple_w_gate", layer))
        big_grads[("ple_w_proj", layer)] = mm_cols_dw(f"ple_proj_dw_{layer}", p_rows, dpp, part=(layer, N_LAYERS))
        token2 = hook(big_grads[("ple_w_proj", layer)])
        if token2 is not None:
            token = token2 if token is None else token + token2
        if layer == 0:
            token3 = open_group("layer0_ffn", [("ffn_w_up", 0), ("ffn_w_down", 0), ("ple_w_gate", 0), ("ple_w_proj", 0)])
            token = token3 if token is None else token + token3
        dz1, dg1, db1 = ln_bwd(
            f"ln_mix_bwd_{layer}", [(ALPHA, dz2), (1.0, t_up), (1.0, t_gate)], sv["xh1"], sv["rs1"],
            tied(ln_mix_g[layer:layer + 1], token))
        small_grads[("ln_mix_g", layer)] = dg1
        small_grads[("ln_mix_b", layer)] = db1
        if layer % 2 == 0:
            dcat = mm_rows_dx("mix_out_dx", dz1, weight("mix_w_out", 0))
            big_grads[("mix_w_out", 0)] = mm_rows_dw("mix_out_dw", sv["cat"], dz1)
            token = hook(big_grads[("mix_w_out", 0)])
            du, dpw, dps, dcw, dcb, dcg, dcbt = mixer_bwd(
                "mixer_bwd", dcat, sv["u"], sv["d"], sv["e"], sv["glu"], sv["hh"], sv["rs"],
                pool_w[0], pool_scale, conv_w_full, tied(conv_ln_g, token), conv_ln_b)
            small_grads[("pool_w", 0)] = dpw
            small_grads[("pool_scale", 0)] = dps
            small_grads[("conv_dw_w", 0)] = dcw
            small_grads[("conv_dw_b", 0)] = dcb
            small_grads[("conv_ln_g", 0)] = dcg
            small_grads[("conv_ln_b", 0)] = dcbt
            big_grads[("mix_w_in", 0)] = mm_cols_dw("mix_in_dw", sv["x_in"], du)
            hook(big_grads[("mix_w_in", 0)])
            open_group("layer0_mix", [("mix_w_in", 0), ("mix_w_out", 0)])
            dx_in = mm_cols_dx("mix_in_dx", du, weight("mix_w_in", 0), addend=(ALPHA, dz1))
            token = hook(dx_in)
        else:
            do = mm_rows_dx("attn_out_dx", dz1, weight("attn_w_o", 0), out_dtype=BF16)
            big_grads[("attn_w_o", 0)] = mm_rows_dw("attn_out_dw", sv["att"], dz1)
            dq, dk, dv, ds_sum = attn_bwd("attn_bwd", sv["qkvp"], sv["bias"], do)
            cols, sat = bias_grad_reduce("bias_grad", ds_sum)
            d_rel = jnp.concatenate(
                [jnp.zeros((N_HEADS, 1), F32),
                 jnp.flip(cols[:, 0, Q_TILE + SHEAR_SAT:Q_TILE - 1 + SHEAR_W], axis=1),
                 sat[:, 0, 0:1]], axis=1)
            small_grads[("attn_rel_bias", 0)] = d_rel
            dqkv = jnp.concatenate([dq, dk, dv], axis=1)
            big_grads[("attn_w_qkv", 0)] = mm_cols_dw("attn_qkv_dw", sv["x_in"], dqkv)
            dx_in = mm_cols_dx("attn_qkv_dx", dqkv, weight("attn_w_qkv", 0), addend=(ALPHA, dz1))
        parts = [(1.0, dx_in)]
    grad_x = dx_in

    small = [
        ("pool_w", pool_w, m_pool_w, v_pool_w, None),
        ("pool_scale", pool_scale, m_pool_scale, v_pool_scale, None),
        ("conv_dw_w", conv_dw_w, m_conv_dw_w, v_conv_dw_w, 2),
        ("conv_dw_b", conv_dw_b, m_conv_dw_b, v_conv_dw_b, None),
        ("conv_ln_g", conv_ln_g, m_conv_ln_g, v_conv_ln_g, None),
        ("conv_ln_b", conv_ln_b, m_conv_ln_b, v_conv_ln_b, None),
        ("attn_rel_bias", attn_rel_bias, m_attn_rel_bias, v_attn_rel_bias, None),
        ("ln_mix_g", ln_mix_g, m_ln_mix_g, v_ln_mix_g, None),
        ("ln_mix_b", ln_mix_b, m_ln_mix_b, v_ln_mix_b, None),
        ("ffn_dw_w", ffn_dw_w, m_ffn_dw_w, v_ffn_dw_w, 2),
        ("ffn_dw_b", ffn_dw_b, m_ffn_dw_b, v_ffn_dw_b, None),
        ("ple_b_gate", ple_b_gate, m_ple_b_gate, v_ple_b_gate, None),
        ("ln_ffn_g", ln_ffn_g, m_ln_ffn_g, v_ln_ffn_g, None),
        ("ln_ffn_b", ln_ffn_b, m_ln_ffn_b, v_ln_ffn_b, None),
    ]
    full_grads = []
    for nm, w, _, _, shard_axis in small:
        full = list(w.shape)
        if shard_axis is not None:
            full[shard_axis] *= N_SHARD
        per_layer = [small_grads[(nm, layer)].reshape((1,) + tuple(full[1:])) for layer in range(w.shape[0])]
        full_grads.append(jnp.concatenate(per_layer, axis=0))
    packed = _pack(full_grads + [loss_part])
    dev_arr = (4 * xi + 2 * yi + ci).astype(jnp.int32).reshape(1)
    sg_block = cast_into_gathered("place_small_grads", packed[None], 0, dev_arr, n_blocks=8, dtype=F32)
    sg_send, sg_recv, sg_bufs, sg_token = copies_start("small_grads_start", [sg_block], all_plan, 7)
    token = sg_token if token is None else token + sg_token

    shard_grads = {}
    for red in reducers:
        if red.stage == 4:
            shard_grads.update(red.result)
    big_out = {}

    def update_big(names, tok):
        for nm, w, m, v, _ in big:
            if nm in names:
                gl = [shard_grads[(nm, layer)] for layer in range(w.shape[0])]
                delta, new_m, new_v = adamw(f"adamw_{nm}", w, gl, m, v, token=tok)
                big_out[nm] = (jnp.stack(gl, axis=0), delta, new_m, new_v)

    last_group = ("mix_w_in", "mix_w_out")
    update_big([nm for nm, _, _, _, _ in big if nm not in last_group], token)
    token = hook(big_out["ffn_w_up"][1])

    gathered_sg = copies_wait("small_grads_wait", sg_bufs, sg_send, sg_recv, all_plan, 0, big_out["ffn_w_down"][1])[0]
    total = sum_blocks("sum_small", gathered_sg.reshape(8 * packed.shape[0], LANES), 8)
    unpacked = _unpack(total, [g.shape for g in full_grads] + [loss_part.shape])
    loss = unpacked[-1][0, 0]
    local_grads = []
    for (nm, w, _, _, shard_axis), g in zip(small, unpacked[:-1]):
        if shard_axis is not None:
            width = w.shape[shard_axis]
            g = lax.dynamic_slice_in_dim(g, shard_idx * width, width, axis=shard_axis)
        local_grads.append(g.reshape(w.shape))
    updated = adamw_many("adamw_small", [w for _, w, _, _, _ in small], local_grads,
                         [m for _, _, m, _, _ in small], [v for _, _, _, v, _ in small], token)
    hook(updated[0][0])
    for red in reducers:
        shard_grads.update(red.result)
    update_big(last_group, None)
    small_out = {}
    for (nm, _, _, _, _), g, (d_, m_, v_) in zip(small, local_grads, updated):
        small_out[nm] = (g, d_, m_, v_)

    order = ["mix_w_in", "pool_w", "pool_scale", "conv_dw_w", "conv_dw_b", "conv_ln_g", "conv_ln_b", "mix_w_out",
             "attn_w_qkv", "attn_rel_bias", "attn_w_o", "ln_mix_g", "ln_mix_b", "ffn_w_up", "ffn_dw_w", "ffn_dw_b",
             "ffn_w_down", "ple_w_proj", "ple_w_gate", "ple_b_gate", "ln_ffn_g", "ln_ffn_b"]
    res = {**big_out, **small_out}
    outs = [loss, grad_x[None]]
    for slot in range(4):
        outs += [res[nm][slot] for nm in order]
    return tuple(outs)
```

```python
import functools
import math

import jax
import jax.numpy as jnp
from jax import lax
from jax.experimental import pallas as pl
from jax.experimental.pallas import tpu as pltpu

F32 = jnp.float32
BF16 = jnp.bfloat16
MESH = pl.DeviceIdType.MESH

N_LAYERS = 2
ALPHA = (2 * N_LAYERS) ** 0.25
LN_EPS = 1e-5
NEG_INF = -1e30
CHUNK = 64
LEFT_CHUNKS = 8
PAD_ROWS = LEFT_CHUNKS * CHUNK
HEAD_DIM = 64
ATTN_SCALE = HEAD_DIM ** -0.5
N_HEADS = 16
MAX_REL = 256
POOL_WINDOWS = (2, 4, 8, 16)
POOL_GROUP = 128
CONV_K = 31
FFN_K = 3
CONV_HALO = 32
FFN_HALO = 8
FFN_TILE = 256
FFN_CHUNK_ROWS = 64
FFN_CHUNK_LANES = 128
Q_TILE = 256
K_WIN = Q_TILE + PAD_ROWS
LANES = 128
SUBLANES = 8
ATTN_PAIRS = 2
ATTN_PAIRS_FWD = 4
ATTN_LANES = ATTN_PAIRS * LANES
SHEAR_W = Q_TILE + K_WIN
SHEAR_SAT = SHEAR_W - 2 * MAX_REL
N_SHARD = 4

ADAM_LR = 0.001
ADAM_B1 = 0.9
ADAM_B2 = 0.999
ADAM_EPS = 1e-08
ADAM_WD = 0.01
ADAM_STEP = 10
ADAM_BC1 = 1.0 - ADAM_B1 ** ADAM_STEP
ADAM_BC2 = 1.0 - ADAM_B2 ** ADAM_STEP

DIMS = {
    "nn": (((1,), (0,)), ((), ())),
    "nt": (((1,), (1,)), ((), ())),
    "tn": (((0,), (0,)), ((), ())),
}


def _cp(vmem_mb=48, **kw):
    return pltpu.CompilerParams(vmem_limit_bytes=vmem_mb * 1024 * 1024, **kw)


def _in_hbm(a):
    return pltpu.with_memory_space_constraint(a, pltpu.HBM)


STAGING_LIMIT_BYTES = 1 << 20
SMALL_WEIGHT_BYTES = 1 << 22
SUM_BLOCK_ROWS = 2048
SMALL_BLOCK_BYTES = 1 << 19


def _call(body, **kw):
    call = pl.pallas_call(body, **kw)

    def run(*args):
        pinned = []
        for a in args:
            big = a.size * a.dtype.itemsize >= STAGING_LIMIT_BYTES
            pinned.append(_in_hbm(a) if big and not jnp.issubdtype(a.dtype, jnp.integer) else a)
        return call(*pinned)

    return run


def _dot(a, b, mode):
    return lax.dot_general(a.astype(BF16), b.astype(BF16), DIMS[mode], preferred_element_type=F32)


def _sig(x):
    return 1.0 / (1.0 + jnp.exp(-x))


def _row_tile(s):
    return min(512, s // 4)


def _mm_tile(s):
    return min(1024, s // 4)


def _mm(name, mode, a, b, in_specs, out_shape, out_spec, acc_shape, grid, nk, zero_first=False, vmem_mb=48,
        addend=None):
    out_f32 = out_shape.dtype == F32

    def body(a_ref, b_ref, *rest):
        k = pl.program_id(2)
        if addend is None:
            o_ref, scr = rest[0], rest[1:]
        else:
            add_ref, o_ref, scr = rest[0], rest[1], rest[2:]

        def compute():
            part = _dot(a_ref[...], b_ref[...], mode)
            if nk == 1:
                if addend is not None:
                    part = part + addend[0] * add_ref[...]
                o_ref[...] = part.astype(o_ref.dtype)
                return
            acc = o_ref if out_f32 else scr[0]

            @pl.when(k == 0)
            def _():
                acc[...] = part if addend is None else part + addend[0] * add_ref[...]

            @pl.when(k > 0)
            def _():
                acc[...] += part

            if not out_f32:
                @pl.when(k == nk - 1)
                def _():
                    o_ref[...] = acc[...].astype(o_ref.dtype)

        if zero_first:
            @pl.when(pl.program_id(1) == 0)
            def _():
                o_ref[...] = jnp.zeros(o_ref.shape, o_ref.dtype)

            pl.when(pl.program_id(1) > 0)(compute)
        else:
            compute()

    scratch = [] if (nk == 1 or out_f32) else [pltpu.VMEM(acc_shape, F32)]
    operands = [a, b] if addend is None else [a, b, addend[1]]
    specs = list(in_specs) if addend is None else list(in_specs) + [out_spec]
    return _call(
        body, name=name, grid=grid, in_specs=specs, out_specs=out_spec, out_shape=out_shape,
        scratch_shapes=scratch, compiler_params=_cp(vmem_mb),
    )(*operands)


def _is_small_weight(wc):
    return wc.size * 2 <= SMALL_WEIGHT_BYTES


def _all_shards(w_ref):
    return jnp.concatenate([w_ref[j] for j in range(N_SHARD)], axis=1)


def mm_cols_fwd(name, a, wc, out_dtype, pad_blocks=0, part=(0, 1)):
    s, k = a.shape
    s //= part[1]
    n4 = wc.shape[2]
    tm = _row_tile(s) if pad_blocks else _mm_tile(s)
    nt = s // tm
    first_block = part[0] * nt
    if _is_small_weight(wc) and not pad_blocks:
        def body(a_ref, w_ref, o_ref):
            o_ref[...] = _dot(a_ref[...], _all_shards(w_ref), "nn").astype(o_ref.dtype)

        return _call(
            body, name=name, grid=(nt,),
            in_specs=[pl.BlockSpec((tm, k), lambda i: (first_block + i, 0)), _full(wc.shape)],
            out_specs=pl.BlockSpec((tm, N_SHARD * n4), lambda i: (i, 0)),
            out_shape=jax.ShapeDtypeStruct((s, N_SHARD * n4), out_dtype), compiler_params=_cp(),
        )(a, wc)
    return _mm(
        name, "nn", a, wc,
        [pl.BlockSpec((tm, k), lambda j, i, r: (first_block + jnp.maximum(i - pad_blocks, 0), 0)),
         pl.BlockSpec((None, k, n4), lambda j, i, r: (j, 0, 0))],
        jax.ShapeDtypeStruct((s + pad_blocks * tm, N_SHARD * n4), out_dtype),
        pl.BlockSpec((tm, n4), lambda j, i, r: (i, j)),
        None, (N_SHARD, nt + pad_blocks, 1), 1, zero_first=pad_blocks > 0)


def mm_cols_dx(name, dy, wc, addend=None):
    s = dy.shape[0]
    _, k, n4 = wc.shape
    tm = _mm_tile(s)
    if _is_small_weight(wc):
        def body(dy_ref, w_ref, *rest):
            part = _dot(dy_ref[...], _all_shards(w_ref), "nt")
            rest[-1][...] = part if addend is None else part + addend[0] * rest[0][...]

        out_spec = pl.BlockSpec((tm, k), lambda i: (i, 0))
        extra, extra_specs = ([], []) if addend is None else ([addend[1]], [out_spec])
        return _call(
            body, name=name, grid=(s // tm,),
            in_specs=[pl.BlockSpec((tm, N_SHARD * n4), lambda i: (i, 0)), _full(wc.shape)] + extra_specs,
            out_specs=out_spec, out_shape=jax.ShapeDtypeStruct((s, k), F32), compiler_params=_cp(),
        )(dy, wc, *extra)
    return _mm(
        name, "nt", dy, wc,
        [pl.BlockSpec((tm, n4), lambda g, i, r: (i, r)),
         pl.BlockSpec((None, k, n4), lambda g, i, r: (r, 0, 0))],
        jax.ShapeDtypeStruct((s, k), F32),
        pl.BlockSpec((tm, k), lambda g, i, r: (i, 0)),
        (tm, k), (1, s // tm, N_SHARD), N_SHARD, addend=addend)


def mm_cols_dw(name, a, dy, part=(0, 1)):
    s, k = a.shape
    s //= part[1]
    n4 = dy.shape[1] // N_SHARD
    tm = _mm_tile(s)
    nt = s // tm
    first_block = part[0] * nt
    if k * n4 * N_SHARD * 2 <= SMALL_WEIGHT_BYTES:
        def body(a_ref, dy_ref, o_ref):
            full = _dot(a_ref[...], dy_ref[...], "tn")
            first = pl.program_id(0) == 0
            for j in range(N_SHARD):
                _acc_add(o_ref.at[j], first, full[:, j * n4:(j + 1) * n4])

        return _call(
            body, name=name, grid=(nt,),
            in_specs=[pl.BlockSpec((tm, k), lambda r: (first_block + r, 0)),
                      pl.BlockSpec((tm, N_SHARD * n4), lambda r: (r, 0))],
            out_specs=_full((N_SHARD, k, n4)),
            out_shape=jax.ShapeDtypeStruct((N_SHARD, k, n4), F32), compiler_params=_cp(),
        )(a, dy)
    return _mm(
        name, "tn", a, dy,
        [pl.BlockSpec((tm, k), lambda j, g, r: (first_block + r, 0)),
         pl.BlockSpec((tm, n4), lambda j, g, r: (r, j))],
        jax.ShapeDtypeStruct((N_SHARD, k, n4), F32),
        pl.BlockSpec((None, k, n4), lambda j, g, r: (j, 0, 0)),
        (k, n4), (N_SHARD, 1, nt), nt)


def _k_tile(k):
    return k if k <= 1024 else k // 2


def mm_rows_fwd(name, a, wr, out_dtype=F32):
    s, k = a.shape
    n = wr.shape[1]
    tm = _mm_tile(s)
    tk = _k_tile(k)
    nk = k // tk
    return _mm(
        name, "nn", a, wr,
        [pl.BlockSpec((tm, tk), lambda g, i, r: (i, r)),
         pl.BlockSpec((tk, n), lambda g, i, r: (r, 0))],
        jax.ShapeDtypeStruct((s, n), out_dtype),
        pl.BlockSpec((tm, n), lambda g, i, r: (i, 0)),
        (tm, n), (1, s // tm, nk), nk)


def mm_rows_dx(name, dy, wr, out_dtype=F32):
    s, n = dy.shape
    k = wr.shape[0]
    tm = _mm_tile(s)
    tk = _k_tile(k)
    return _mm(
        name, "nt", dy, wr,
        [pl.BlockSpec((tm, n), lambda j, i, r: (i, 0)),
         pl.BlockSpec((tk, n), lambda j, i, r: (j, 0))],
        jax.ShapeDtypeStruct((s, k), out_dtype),
        pl.BlockSpec((tm, tk), lambda j, i, r: (i, j)),
        None, (k // tk, s // tm, 1), 1)


def mm_rows_dw(name, a, dy):
    s, k = a.shape
    n = dy.shape[1]
    tm = _mm_tile(s)
    tk = _k_tile(k)
    nt = s // tm
    return _mm(
        name, "tn", a, dy,
        [pl.BlockSpec((tm, tk), lambda j, g, r: (r, j)),
         pl.BlockSpec((tm, n), lambda j, g, r: (r, 0))],
        jax.ShapeDtypeStruct((k, n), F32),
        pl.BlockSpec((tk, n), lambda j, g, r: (j, 0)),
        (tk, n), (k // tk, 1, nt), nt)


def _row(tm, c, col=0):
    return pl.BlockSpec((tm, c), lambda i: (i, col))


def _full(shape):
    nd = len(shape)
    return pl.BlockSpec(shape, lambda i: (0,) * nd)


def _prev(tm, h, c, col=0):
    return pl.BlockSpec((h, c), lambda i: (jnp.maximum(i * (tm // h) - 1, 0), col))


def _next(tm, h, c, s, col=0):
    return pl.BlockSpec((h, c), lambda i: (jnp.minimum((i + 1) * (tm // h), s // h - 1), col))


def _acc_add(ref, first, val):
    @pl.when(first)
    def _():
        ref[...] = val

    @pl.when(jnp.logical_not(first))
    def _():
        ref[...] += val


def _colsum(v):
    return jnp.sum(v, axis=0, keepdims=True)


def _ln_stats(z):
    mu = jnp.mean(z, axis=-1, keepdims=True)
    zc = z - mu
    var = jnp.mean(zc * zc, axis=-1, keepdims=True)
    rstd = lax.rsqrt(var + LN_EPS)
    return zc * rstd, rstd


def _ln_bwd(dxhat, xhat, rstd):
    m1 = jnp.mean(dxhat, axis=-1, keepdims=True)
    m2 = jnp.mean(dxhat * xhat, axis=-1, keepdims=True)
    return rstd * (dxhat - m1 - xhat * m2)


def ln_fwd(name, x, f, g, b, ple=None, emit_y=True):
    s, d = x.shape
    tm = _row_tile(s)
    n_in = 2 + (3 if ple is not None else 0)

    def body(*refs):
        x_ref, f_ref = refs[0], refs[1]
        g_ref, b_ref = refs[n_in], refs[n_in + 1]
        xh_ref, rs_ref = refs[-2:]
        z = ALPHA * x_ref[...] + f_ref[...]
        if ple is not None:
            pgl_ref, pp_ref, bg_ref = refs[2:5]
            z = z + _sig(pgl_ref[...] + bg_ref[...]) * pp_ref[...]
        xhat, rstd = _ln_stats(z)
        if emit_y:
            y = xhat * g_ref[...] + b_ref[...]
            refs[n_in + 2][...] = y
            refs[n_in + 3][...] = y.astype(BF16)
        xh_ref[...] = xhat
        rs_ref[...] = jnp.broadcast_to(rstd, rs_ref.shape)

    ins = [x, f]
    specs = [_row(tm, d), _row(tm, d)]
    if ple is not None:
        pgl, pp, bg = ple
        ins += [pgl, pp, bg]
        specs += [_row(tm, d), _row(tm, d), _full((1, d))]
    ins += [g, b]
    specs += [_full((1, d)), _full((1, d))]
    y_shapes = [jax.ShapeDtypeStruct((s, d), F32), jax.ShapeDtypeStruct((s, d), BF16)] if emit_y else []
    outs = _call(
        body, name=name, grid=(s // tm,), in_specs=specs,
        out_specs=[_row(tm, d)] * (len(y_shapes) + 1) + [_row(tm, LANES)],
        out_shape=y_shapes + [jax.ShapeDtypeStruct((s, d), F32), jax.ShapeDtypeStruct((s, LANES), F32)],
        compiler_params=_cp(),
    )(*ins)
    return tuple(outs) if emit_y else (None, None, outs[0], outs[1])


def ln_bwd(name, parts, xhat, rstd, g, ple=None, loss=None):
    s, d = xhat.shape
    tm = _row_tile(s)
    coefs = [c for c, _ in parts]
    n_p = len(parts)
    n_ple = 3 if ple is not None else 0
    n_in = n_p + 3 + n_ple + (2 if loss is not None else 0)

    def body(*refs):
        first = pl.program_id(0) == 0
        xh = refs[n_p][...]
        rs = refs[n_p + 1][:, 0:1]
        g_v = refs[n_p + 2][...]
        outs = refs[n_in:]
        if loss is not None:
            t_ref, b_ref = refs[n_p + 3 + n_ple:n_p + 5 + n_ple]
            err = (xh * g_v + b_ref[...]) - t_ref[...]
            dy = err * (1.0 / d)
            part = 0.5 * jnp.sum(jnp.mean(err * err, axis=-1, keepdims=True), axis=0, keepdims=True)
            _acc_add(outs[-1], first, jnp.broadcast_to(part, outs[-1].shape))
        else:
            dy = coefs[0] * refs[0][...].astype(F32)
            for j in range(1, n_p):
                dy = dy + coefs[j] * refs[j][...].astype(F32)
        dz = _ln_bwd(dy * g_v, xh, rs)
        outs[0][...] = dz
        _acc_add(outs[1], first, _colsum(dy * xh))
        _acc_add(outs[2], first, _colsum(dy))
        if ple is not None:
            pgl_ref, pp_ref, bg_ref = refs[n_p + 3:n_p + 6]
            pg = _sig(pgl_ref[...] + bg_ref[...])
            dpgl = dz * pp_ref[...] * pg * (1.0 - pg)
            outs[3][...] = (dz * pg).astype(BF16)
            outs[4][...] = dpgl.astype(BF16)
            _acc_add(outs[5], first, _colsum(dpgl))

    ins = [p for _, p in parts] + [xhat, rstd, g]
    specs = [_row(tm, d)] * n_p + [_row(tm, d), _row(tm, LANES), _full((1, d))]
    out_specs = [_row(tm, d), _full((1, d)), _full((1, d))]
    out_shape = [jax.ShapeDtypeStruct((s, d), F32), jax.ShapeDtypeStruct((1, d), F32),
                 jax.ShapeDtypeStruct((1, d), F32)]
    if ple is not None:
        pgl, pp, bg = ple
        ins += [pgl, pp, bg]
        specs += [_row(tm, d), _row(tm, d), _full((1, d))]
        out_specs += [_row(tm, d), _row(tm, d), _full((1, d))]
        out_shape += [jax.ShapeDtypeStruct((s, d), BF16), jax.ShapeDtypeStruct((s, d), BF16),
                      jax.ShapeDtypeStruct((1, d), F32)]
    if loss is not None:
        target, b = loss
        ins += [target, b]
        specs += [_row(tm, d), _full((1, d))]
        out_specs += [_full((8, LANES))]
        out_shape += [jax.ShapeDtypeStruct((8, LANES), F32)]
    return _call(
        body, name=name, grid=(s // tm,), in_specs=specs, out_specs=out_specs, out_shape=out_shape,
        compiler_params=_cp(),
    )(*ins)


def _fill_rotations(rot_ref, x, direction):
    n = x.shape[0]
    rot_ref[0] = x
    for b in range(1, SUBLANES):
        if direction < 0:
            rot_ref[b, SUBLANES:n, :] = x[SUBLANES - b:n - b]
        else:
            rot_ref[b, 0:n - SUBLANES, :] = x[b:n - SUBLANES + b]


def _rotated(rot_ref, start, rows, cs, direction=-1):
    b = (-start) % SUBLANES if direction < 0 else start % SUBLANES
    aligned = start + b if direction < 0 else start - b
    return rot_ref[b, pl.ds(aligned, rows), cs]


def _tile_pos(i, tm, rows):
    return (i * tm + lax.broadcasted_iota(jnp.int32, (rows, 1), 0) + 1).astype(F32)


def mixer_fwd(name, u, pool_w, pool_scale, conv_w, conv_b, cn_g, cn_b):
    s = u.shape[0]
    dp = 512
    tm = min(256, s // 4)
    h = CONV_HALO

    def body(a_c, a_p, bv_c, bv_p, bg_c, bg_p, pw_ref, ps_ref, cw_ref, cb_ref, cg_ref, cbt_ref,
             cat_ref, d_ref, e_ref, glu_ref, hh_ref, rs_ref, ext_a, rot_g, conv_out):
        i = pl.program_id(0)
        first = i == 0
        ext_a[0:h, :] = jnp.where(first, 0.0, a_p[...])
        ext_a[h:, :] = a_c[...]
        glu = bv_c[...] * _sig(bg_c[...])
        glu_ref[...] = glu
        _fill_rotations(rot_g, jnp.concatenate([jnp.where(first, 0.0, bv_p[...] * _sig(bg_p[...])), glu], axis=0), -1)
        pos = _tile_pos(i, tm, tm)
        for gi, w in enumerate(POOL_WINDOWS):
            cs = slice(gi * POOL_GROUP, (gi + 1) * POOL_GROUP)
            a_g = ext_a[pl.ds(h, tm), cs]
            acc = a_g
            for sh in range(1, w):
                acc = acc + ext_a[pl.ds(h - sh, tm), cs]
            d_g = acc / jnp.minimum(pos, float(w)) - a_g
            d_ref[:, cs] = d_g.astype(BF16)
            e_g = _dot(d_g, pw_ref[gi], "nn")
            e_ref[:, cs] = e_g
            cat_ref[:, cs] = (e_g * ps_ref[:, cs]).astype(BF16)
        for lg in range(dp // LANES):
            cs = slice(lg * LANES, (lg + 1) * LANES)
            acc = jnp.broadcast_to(cb_ref[:, cs], (tm, LANES))
            for sh in range(CONV_K):
                acc = acc + _rotated(rot_g, h - sh, tm, cs) * cw_ref[pl.ds(CONV_K - 1 - sh, 1), cs]
            conv_out[:, cs] = acc
        hhat, rstd = _ln_stats(conv_out[...])
        hl = hhat * cg_ref[...] + cbt_ref[...]
        cat_ref[:, dp:] = (hl * _sig(hl)).astype(BF16)
        hh_ref[...] = hhat
        rs_ref[...] = jnp.broadcast_to(rstd, rs_ref.shape)

    specs = [_row(tm, dp, 0), _prev(tm, h, dp, 0), _row(tm, dp, 1), _prev(tm, h, dp, 1),
             _row(tm, dp, 2), _prev(tm, h, dp, 2),
             _full((4, POOL_GROUP, POOL_GROUP)), _full((1, dp)), _full((CONV_K, dp)),
             _full((1, dp)), _full((1, dp)), _full((1, dp))]
    out_specs = [_row(tm, 2 * dp), _row(tm, dp), _row(tm, dp), _row(tm, dp), _row(tm, dp), _row(tm, LANES)]
    out_shape = [jax.ShapeDtypeStruct((s, 2 * dp), BF16), jax.ShapeDtypeStruct((s, dp), BF16),
                 jax.ShapeDtypeStruct((s, dp), F32), jax.ShapeDtypeStruct((s, dp), F32),
                 jax.ShapeDtypeStruct((s, dp), F32), jax.ShapeDtypeStruct((s, LANES), F32)]
    return _call(
        body, name=name, grid=(s // tm,), in_specs=specs, out_specs=out_specs, out_shape=out_shape,
        scratch_shapes=[pltpu.VMEM((h + tm, dp), F32), pltpu.VMEM((SUBLANES, h + tm, dp), F32),
                        pltpu.VMEM((tm, dp), F32)],
        compiler_params=_cp(),
    )(u, u, u, u, u, u, pool_w, pool_scale, conv_w, conv_b, cn_g, cn_b)


def mixer_bwd(name, dcat, u, d_sv, e_sv, glu_sv, hh_sv, rs_sv, pool_w, pool_scale, conv_w, cn_g, cn_b):
    s = u.shape[0]
    dp = 512
    tm = min(256, s // 4)
    h = CONV_HALO
    nt = s // tm

    def body(dc_c, dc_n, bv_c, bg_c, d_c, e_c, gl_c, gl_p, hh_c, hh_n, rs_c, rs_n,
             pw_ref, ps_ref, cw_ref, cg_ref, cbt_ref,
             du_ref, dpw_ref, dps_ref, dcw_ref, dcb_ref, dcg_ref, dcbt_ref,
             ext_dh, ext_g, ext_r):
        i = pl.program_id(0)
        first = i == 0
        last = i == nt - 1
        cg = cg_ref[...]

        def conv_grads(dyb, hhat, rstd):
            hl = hhat * cg + cbt_ref[...]
            sg = _sig(hl)
            dhl = dyb * (sg * (1.0 + hl * (1.0 - sg)))
            return _ln_bwd(dhl * cg, hhat, rstd), dhl

        hh_cur = hh_c[...]
        dh_c, dhl_c = conv_grads(dc_c[:, dp:], hh_cur, rs_c[:, 0:1])
        dh_n, _ = conv_grads(dc_n[:, dp:], hh_n[...], rs_n[:, 0:1])
        _fill_rotations(ext_dh, jnp.concatenate([dh_c, jnp.where(last, 0.0, dh_n)], axis=0), 1)
        _fill_rotations(ext_g, jnp.concatenate([jnp.where(first, 0.0, gl_p[...]), gl_c[...]], axis=0), -1)

        @pl.when(first)
        def _():
            dcw_ref[...] = jnp.zeros(dcw_ref.shape, F32)

        for lg in range(dp // LANES):
            cs = slice(lg * LANES, (lg + 1) * LANES)
            dglu = jnp.zeros((tm, LANES), F32)
            for sh in range(CONV_K):
                dglu = dglu + _rotated(ext_dh, sh, tm, cs, 1) * cw_ref[pl.ds(CONV_K - 1 - sh, 1), cs]
            dh_g = ext_dh[0, pl.ds(0, tm), cs]
            for sh in range(CONV_K):
                dcw_ref[pl.ds(CONV_K - 1 - sh, 1), cs] += _colsum(dh_g * _rotated(ext_g, h - sh, tm, cs))
            sgate = _sig(bg_c[:, cs])
            du_ref[:, dp + lg * LANES:dp + (lg + 1) * LANES] = dglu * sgate
            du_ref[:, 2 * dp + lg * LANES:2 * dp + (lg + 1) * LANES] = dglu * bv_c[:, cs] * sgate * (1.0 - sgate)
        _acc_add(dcb_ref, first, _colsum(dh_c))
        _acc_add(dcg_ref, first, _colsum(dhl_c * hh_cur))
        _acc_add(dcbt_ref, first, _colsum(dhl_c))

        pos_c = _tile_pos(i, tm, tm)
        pos_n = _tile_pos(i + 1, tm, h)
        _acc_add(dps_ref, first, _colsum(dc_c[:, :dp] * e_c[...]))
        for gi, w in enumerate(POOL_WINDOWS):
            cs = slice(gi * POOL_GROUP, (gi + 1) * POOL_GROUP)
            pw = pw_ref[gi]
            de_c = dc_c[:, cs] * ps_ref[:, cs]
            de_n = dc_n[:, cs] * ps_ref[:, cs]
            dd_c = _dot(de_c, pw, "nt")
            dd_n = _dot(de_n, pw, "nt")
            ext_r[0:tm, :] = dd_c / jnp.minimum(pos_c, float(w))
            ext_r[tm:, :] = jnp.where(last, 0.0, dd_n / jnp.minimum(pos_n, float(w)))
            acc = -dd_c
            for sh in range(w):
                acc = acc + ext_r[pl.ds(sh, tm), :]
            du_ref[:, cs] = acc
            dpw_g = _dot(d_c[:, cs], de_c, "tn")

            @pl.when(first)
            def _():
                dpw_ref[gi] = dpw_g

            @pl.when(jnp.logical_not(first))
            def _():
                dpw_ref[gi] += dpw_g

    specs = [_row(tm, 2 * dp), _next(tm, h, 2 * dp, s), _row(tm, dp, 1), _row(tm, dp, 2),
             _row(tm, dp), _row(tm, dp), _row(tm, dp), _prev(tm, h, dp),
             _row(tm, dp), _next(tm, h, dp, s), _row(tm, LANES), _next(tm, h, LANES, s),
             _full((4, POOL_GROUP, POOL_GROUP)), _full((1, dp)), _full((CONV_K, dp)),
             _full((1, dp)), _full((1, dp))]
    out_specs = [_row(tm, 3 * dp), _full((4, POOL_GROUP, POOL_GROUP)), _full((1, dp)), _full((CONV_K, dp)),
                 _full((1, dp)), _full((1, dp)), _full((1, dp))]
    out_shape = [jax.ShapeDtypeStruct((s, 3 * dp), F32),
                 jax.ShapeDtypeStruct((4, POOL_GROUP, POOL_GROUP), F32), jax.ShapeDtypeStruct((1, dp), F32),
                 jax.ShapeDtypeStruct((CONV_K, dp), F32), jax.ShapeDtypeStruct((1, dp), F32),
                 jax.ShapeDtypeStruct((1, dp), F32), jax.ShapeDtypeStruct((1, dp), F32)]
    return _call(
        body, name=name, grid=(nt,), in_specs=specs, out_specs=out_specs, out_shape=out_shape,
        scratch_shapes=[pltpu.VMEM((SUBLANES, tm + h, dp), F32), pltpu.VMEM((SUBLANES, h + tm, dp), F32),
                        pltpu.VMEM((tm + h, POOL_GROUP), F32)],
        compiler_params=_cp(),
    )(dcat, dcat, u, u, d_sv, e_sv, glu_sv, glu_sv, hh_sv, hh_sv, rs_sv, rs_sv,
      pool_w, pool_scale, conv_w, cn_g, cn_b)


GELU_C = math.sqrt(2.0 / math.pi)


def _gelu_parts(x):
    x2 = x * x
    t = jnp.tanh(x * (GELU_C + (GELU_C * 0.044715) * x2))
    half_1pt = 0.5 + 0.5 * t
    gelu = x * half_1pt
    dgelu = half_1pt + (0.5 * x) * (1.0 - t * t) * (GELU_C + (3.0 * GELU_C * 0.044715) * x2)
    return gelu, dgelu


def ffn_act_fwd(name, gv, dw_w, dw_b):
    s = gv.shape[0]
    dff = gv.shape[1] // 2
    tm = min(FFN_TILE, s // 4)
    h = FFN_HALO
    rc = FFN_CHUNK_ROWS
    lw = FFN_CHUNK_LANES

    def body(g_c, g_p, v_c, w_ref, b_ref, hid_ref):
        first = pl.program_id(0) == 0

        def chunk(ci, carry):
            r0 = pl.multiple_of(ci * rc, rc)
            above = pl.multiple_of(jnp.maximum(r0 - h, 0), h)
            for lg in range(dff // lw):
                cs = slice(lg * lw, (lg + 1) * lw)
                top = jnp.where(ci == 0, jnp.where(first, 0.0, g_p[:, cs]), g_c[pl.ds(above, h), cs])
                win = jnp.concatenate([top, g_c[pl.ds(r0, rc), cs]], axis=0)
                gc = jnp.broadcast_to(b_ref[:, cs], (rc, lw))
                for sh in range(FFN_K):
                    gc = gc + win[h - sh:h - sh + rc] * w_ref[pl.ds(FFN_K - 1 - sh, 1), cs]
                gelu, _ = _gelu_parts(gc)
                hid_ref[pl.ds(r0, rc), cs] = (gelu * v_c[pl.ds(r0, rc), cs]).astype(BF16)
            return carry

        lax.fori_loop(0, tm // rc, chunk, 0)

    return _call(
        body, name=name, grid=(s // tm,),
        in_specs=[_row(tm, dff, 0), _prev(tm, h, dff, 0), _row(tm, dff, 1), _full((FFN_K, dff)), _full((1, dff))],
        out_specs=_row(tm, dff), out_shape=jax.ShapeDtypeStruct((s, dff), BF16),
        compiler_params=_cp(),
    )(gv, gv, gv, dw_w, dw_b)


def ffn_act_bwd(name, dhid, gv, dw_w, dw_b):
    s = gv.shape[0]
    dff = gv.shape[1] // 2
    tm = min(FFN_TILE, s // 4)
    h = FFN_HALO
    nt = s // tm
    rc = FFN_CHUNK_ROWS
    lw = FFN_CHUNK_LANES
    n_chunks = tm // rc

    def body(dh_c, dh_n, g_p, g_c, g_n, v_c, v_n, w_ref, b_ref, dgv_ref, dw_ref, db_ref):
        i = pl.program_id(0)
        first = i == 0
        last = i == nt - 1

        @pl.when(first)
        def _():
            dw_ref[...] = jnp.zeros(dw_ref.shape, F32)
            db_ref[...] = jnp.zeros(db_ref.shape, F32)

        def chunk(ci, carry):
            r0 = pl.multiple_of(ci * rc, rc)
            above = pl.multiple_of(jnp.maximum(r0 - h, 0), h)
            below = pl.multiple_of(jnp.minimum(r0 + rc, tm - h), h)
            at_end = ci == n_chunks - 1
            for lg in range(dff // lw):
                cs = slice(lg * lw, (lg + 1) * lw)
                top = jnp.where(ci == 0, jnp.where(first, 0.0, g_p[:, cs]), g_c[pl.ds(above, h), cs])
                bot = jnp.where(at_end, g_n[:, cs], g_c[pl.ds(below, h), cs])
                win = jnp.concatenate([top, g_c[pl.ds(r0, rc), cs], bot], axis=0)
                shifted = [win[h - sh:h - sh + rc + h] for sh in range(FFN_K)]
                gc = jnp.broadcast_to(b_ref[:, cs], (rc + h, lw))
                for sh in range(FFN_K):
                    gc = gc + shifted[sh] * w_ref[pl.ds(FFN_K - 1 - sh, 1), cs]
                gelu, dgelu = _gelu_parts(gc)
                dh_mid = dh_c[pl.ds(r0, rc), cs]
                hv_bot = jnp.where(at_end, jnp.where(last, 0.0, dh_n[:, cs] * v_n[:, cs]),
                                   dh_c[pl.ds(below, h), cs] * v_c[pl.ds(below, h), cs])
                dgc = jnp.concatenate([dh_mid * v_c[pl.ds(r0, rc), cs], hv_bot], axis=0) * dgelu
                dgate = jnp.zeros((rc, lw), F32)
                for sh in range(FFN_K):
                    dgate = dgate + dgc[sh:sh + rc] * w_ref[pl.ds(FFN_K - 1 - sh, 1), cs]
                dgv_ref[pl.ds(r0, rc), cs] = dgate.astype(BF16)
                dgv_ref[pl.ds(r0, rc), slice(dff + lg * lw, dff + (lg + 1) * lw)] = (dh_mid * gelu[0:rc]).astype(BF16)
                dgc_mid = dgc[0:rc]
                for sh in range(FFN_K):
                    dw_ref[pl.ds(FFN_K - 1 - sh, 1), cs] += _colsum(dgc_mid * shifted[sh][0:rc])
                db_ref[:, cs] += _colsum(dgc_mid)
            return carry

        lax.fori_loop(0, n_chunks, chunk, 0)

    return _call(
        body, name=name, grid=(nt,),
        in_specs=[_row(tm, dff), _next(tm, h, dff, s),
                  _prev(tm, h, dff, 0), _row(tm, dff, 0), _next(tm, h, dff, s, 0),
                  _row(tm, dff, 1), _next(tm, h, dff, s, 1),
                  _full((FFN_K, dff)), _full((1, dff))],
        out_specs=[_row(tm, 2 * dff), _full((FFN_K, dff)), _full((1, dff))],
        out_shape=[jax.ShapeDtypeStruct((s, 2 * dff), BF16), jax.ShapeDtypeStruct((FFN_K, dff), F32),
                   jax.ShapeDtypeStruct((1, dff), F32)],
        compiler_params=_cp(),
    )(dhid, dhid, gv, gv, gv, gv, gv, dw_w, dw_b)


def _bias_line(rel_bias):
    nh = rel_bias.shape[0]
    line = jnp.concatenate(
        [jnp.zeros((nh, 1), rel_bias.dtype), jnp.broadcast_to(rel_bias[:, 2 * MAX_REL:], (nh, SHEAR_SAT)),
         jnp.flip(rel_bias[:, 1:2 * MAX_REL], axis=1)], axis=1)
    return line[:, None, :]


def bias_tile(name, line):
    nh = line.shape[0]

    def body(l_ref, o_ref):
        x = jnp.broadcast_to(l_ref[...], (Q_TILE, SHEAR_W))
        z = pltpu.roll(x, SHEAR_W - Q_TILE, 1, stride=1, stride_axis=0)
        qc = lax.broadcasted_iota(jnp.int32, (Q_TILE, K_WIN), 0) // CHUNK
        kc = lax.broadcasted_iota(jnp.int32, (Q_TILE, K_WIN), 1) // CHUNK
        o_ref[...] = jnp.where((kc >= qc) & (kc <= qc + LEFT_CHUNKS), z[:, :K_WIN], NEG_INF)

    return _call(
        body, name=name, grid=(nh,), in_specs=[pl.BlockSpec((None, 1, SHEAR_W), lambda hh: (hh, 0, 0))],
        out_specs=pl.BlockSpec((None, Q_TILE, K_WIN), lambda hh: (hh, 0, 0)),
        out_shape=jax.ShapeDtypeStruct((nh, Q_TILE, K_WIN), F32), compiler_params=_cp(),
    )(line)


def _stack_heads(x2, scale=None):
    if scale is not None:
        x2 = x2 * jnp.asarray(scale, x2.dtype)
    lane = lax.broadcasted_iota(jnp.int32, x2.shape, 1)
    zero = jnp.zeros_like(x2)
    return jnp.concatenate([jnp.where(lane < HEAD_DIM, x2, zero), jnp.where(lane < HEAD_DIM, zero, x2)], axis=0)


def _unstack_heads(x_st):
    lane = lax.broadcasted_iota(jnp.int32, (Q_TILE, LANES), 1)
    return jnp.where(lane < HEAD_DIM, x_st[:Q_TILE], x_st[Q_TILE:])


def _attn_probs(q_st, k3, bias_st, t):
    sc = _dot(q_st, k3, "nt") + bias_st
    col = lax.broadcasted_iota(jnp.int32, sc.shape, 1)
    sc = jnp.where(col >= PAD_ROWS - t * Q_TILE, sc, NEG_INF)
    m = jnp.max(sc, axis=-1, keepdims=True)
    p = jnp.exp(sc - m)
    return p * (1.0 / jnp.sum(p, axis=-1, keepdims=True))


def _attn_specs(d_model, pairs):
    nq = PAD_ROWS // Q_TILE
    width = pairs * LANES
    groups = d_model // width
    specs = [pl.BlockSpec((Q_TILE, width), lambda g, t: (t + nq, g))]
    for which in (1, 2):
        for j in range(K_WIN // Q_TILE):
            specs.append(pl.BlockSpec((Q_TILE, width), lambda g, t, j=j, which=which: (t + j, which * groups + g)))
    specs.append(pl.BlockSpec((2 * pairs, Q_TILE, K_WIN), lambda g, t: (g, 0, 0)))
    return specs


def attn_fwd(name, qkvp, bias):
    s = qkvp.shape[0] - PAD_ROWS
    d_model = qkvp.shape[1] // 3
    nw = K_WIN // Q_TILE

    def body(q_ref, *refs):
        k_refs, v_refs, b_ref, o_ref = refs[:nw], refs[nw:2 * nw], refs[2 * nw], refs[2 * nw + 1]
        t = pl.program_id(1)
        for j in range(ATTN_PAIRS_FWD):
            ls = slice(j * LANES, (j + 1) * LANES)
            k3 = jnp.concatenate([r[:, ls] for r in k_refs], axis=0)
            v3 = jnp.concatenate([r[:, ls] for r in v_refs], axis=0)
            bias_st = b_ref[2 * j:2 * j + 2].reshape(2 * Q_TILE, K_WIN)
            p = _attn_probs(_stack_heads(q_ref[:, ls], ATTN_SCALE), k3, bias_st, t)
            o_ref[:, ls] = _unstack_heads(_dot(p, v3, "nn")).astype(BF16)

    width = ATTN_PAIRS_FWD * LANES
    return _call(
        body, name=name, grid=(d_model // width, s // Q_TILE),
        in_specs=_attn_specs(d_model, ATTN_PAIRS_FWD), out_specs=pl.BlockSpec((Q_TILE, width), lambda g, t: (t, g)),
        out_shape=jax.ShapeDtypeStruct((s, d_model), BF16), compiler_params=_cp(),
    )(qkvp, *([qkvp] * (2 * nw)), bias)


def attn_bwd(name, qkvp, bias, do):
    s = qkvp.shape[0] - PAD_ROWS
    d_model = qkvp.shape[1] // 3
    nw = K_WIN // Q_TILE
    nt = s // Q_TILE

    def body(q_ref, *refs):
        k_refs, v_refs = refs[:nw], refs[nw:2 * nw]
        b_ref, do_ref, dq_ref, dk_ref, dv_ref, ds_ref, dk_acc, dv_acc = refs[2 * nw:]
        t = pl.program_id(1)
        first = t == 0

        @pl.when(first)
        def _():
            dk_acc[...] = jnp.zeros(dk_acc.shape, F32)
            dv_acc[...] = jnp.zeros(dv_acc.shape, F32)

        start = pl.multiple_of(t * Q_TILE, Q_TILE)
        for j in range(ATTN_PAIRS):
            ls = slice(j * LANES, (j + 1) * LANES)
            q_st = _stack_heads(q_ref[:, ls], ATTN_SCALE)
            do_st = _stack_heads(do_ref[:, ls])
            k3 = jnp.concatenate([r[:, ls] for r in k_refs], axis=0)
            v3 = jnp.concatenate([r[:, ls] for r in v_refs], axis=0)
            p = _attn_probs(q_st, k3, b_ref[2 * j:2 * j + 2].reshape(2 * Q_TILE, K_WIN), t)
            dp = _dot(do_st, v3, "nt")
            ds = p * (dp - jnp.sum(p * dp, axis=-1, keepdims=True))
            _acc_add(ds_ref.at[2 * j:2 * j + 2], first, ds.reshape(2, Q_TILE, K_WIN))
            dsb = ds.astype(BF16)
            dq_ref[:, ls] = (_unstack_heads(_dot(dsb, k3, "nn")) * ATTN_SCALE).astype(BF16)
            dk_acc[pl.ds(start, K_WIN), ls] += _dot(dsb, q_st, "tn")
            dv_acc[pl.ds(start, K_WIN), ls] += _dot(p, do_st, "tn")

        @pl.when(t == nt - 1)
        def _():
            dk_ref[...] = dk_acc[pl.ds(PAD_ROWS, s), :].astype(BF16)
            dv_ref[...] = dv_acc[pl.ds(PAD_ROWS, s), :].astype(BF16)

    specs = _attn_specs(d_model, ATTN_PAIRS) + [pl.BlockSpec((Q_TILE, ATTN_LANES), lambda g, t: (t, g))]
    col_spec = pl.BlockSpec((s, ATTN_LANES), lambda g, t: (0, g))
    return _call(
        body, name=name, grid=(d_model // ATTN_LANES, nt), in_specs=specs,
        out_specs=[pl.BlockSpec((Q_TILE, ATTN_LANES), lambda g, t: (t, g)), col_spec, col_spec,
                   pl.BlockSpec((2 * ATTN_PAIRS, Q_TILE, K_WIN), lambda g, t: (g, 0, 0))],
        out_shape=[jax.ShapeDtypeStruct((s, d_model), BF16)] * 3
        + [jax.ShapeDtypeStruct((N_HEADS, Q_TILE, K_WIN), F32)],
        scratch_shapes=[pltpu.VMEM((PAD_ROWS + s, ATTN_LANES), F32), pltpu.VMEM((PAD_ROWS + s, ATTN_LANES), F32)],
        compiler_params=_cp(),
    )(qkvp, *([qkvp] * (2 * nw)), bias, do)


def bias_grad_reduce(name, ds_sum):
    nh = ds_sum.shape[0]
    width = SHEAR_W + Q_TILE
    first_k = Q_TILE - 1

    def body(x_ref, col_ref, sat_ref):
        x = x_ref[...]
        hi = x.astype(BF16)
        lo = (x - hi.astype(F32)).astype(BF16)
        r = lax.broadcasted_iota(jnp.int32, (Q_TILE, Q_TILE), 0)
        c = lax.broadcasted_iota(jnp.int32, (Q_TILE, Q_TILE), 1)
        exchange = jnp.where(r + c == Q_TILE - 1, 1.0, 0.0).astype(BF16)
        x_rev = _dot(exchange, hi, "nn") + _dot(exchange, lo, "nn")
        zeros = jnp.zeros((Q_TILE, Q_TILE), F32)
        y = pltpu.roll(jnp.concatenate([zeros, x_rev, zeros], axis=1), 0, 1, stride=1, stride_axis=0)
        cols = _colsum(y)
        col_ref[...] = cols
        k = lax.broadcasted_iota(jnp.int32, cols.shape, 1) - first_k
        tot = jnp.sum(jnp.where((k >= 1) & (k <= SHEAR_SAT), cols, 0.0), axis=-1, keepdims=True)
        sat_ref[...] = jnp.broadcast_to(tot, sat_ref.shape)

    return _call(
        body, name=name, grid=(nh,),
        in_specs=[pl.BlockSpec((None, Q_TILE, K_WIN), lambda hh: (hh, 0, 0))],
        out_specs=[pl.BlockSpec((None, 1, width), lambda hh: (hh, 0, 0)),
                   pl.BlockSpec((None, 1, LANES), lambda hh: (hh, 0, 0))],
        out_shape=[jax.ShapeDtypeStruct((nh, 1, width), F32), jax.ShapeDtypeStruct((nh, 1, LANES), F32)],
        compiler_params=_cp(),
    )(ds_sum)


def _ew_rows(r, most=512, cols=None):
    if cols is not None and r * cols * 4 <= SMALL_BLOCK_BYTES:
        return r
    for cand in range(min(most, r) // 16 * 16, 0, -16):
        if r % cand == 0:
            return cand
    return r


def to_bf16(name, a):
    s, d = a.shape
    tm = _row_tile(s)

    def body(a_ref, o_ref):
        o_ref[...] = a_ref[...].astype(BF16)

    return _call(
        body, name=name, grid=(s // tm,), in_specs=[_row(tm, d)], out_specs=_row(tm, d),
        out_shape=jax.ShapeDtypeStruct((s, d), BF16), compiler_params=_cp(),
    )(a)


def cast_into_gathered(name, w, layer, s_idx, n_blocks=N_SHARD, dtype=BF16, token=None):
    r, c = w.shape[-2:]
    tr = _ew_rows(r, cols=c)

    def body(s_ref, w_ref, *rest):
        rest[-1][...] = w_ref[...].astype(dtype)

    extra = [] if token is None else [token]
    grid_spec = pltpu.PrefetchScalarGridSpec(
        num_scalar_prefetch=1, grid=(r // tr,),
        in_specs=[pl.BlockSpec((None, tr, c), lambda i, s_ref: (layer, i, 0))] + [ANY_SPEC] * len(extra),
        out_specs=pl.BlockSpec((None, tr, c), lambda i, s_ref: (s_ref[0], i, 0)))
    return _call(
        body, name=name, grid_spec=grid_spec, out_shape=jax.ShapeDtypeStruct((n_blocks, r, c), dtype),
        compiler_params=_cp(),
    )(s_idx, w, *extra)


def adamw(name, w, grads, m, v, token=None):
    nl, r, c = w.shape
    tr = _ew_rows(r, 256, cols=c)

    def body(*refs):
        w_ref, m_ref, v_ref = refs[0], refs[1], refs[2]
        g_refs = refs[3:3 + nl]
        d_ref, nm_ref, nv_ref = refs[-3:]
        layer = pl.program_id(0)
        g = g_refs[0][...]
        for j in range(1, nl):
            g = jnp.where(layer == j, g_refs[j][...], g)
        d_ref[...], nm_ref[...], nv_ref[...] = _adamw_update(w_ref[...], g, m_ref[...], v_ref[...])

    p_spec = pl.BlockSpec((None, tr, c), lambda l, i: (l, i, 0))
    g_spec = pl.BlockSpec((tr, c), lambda l, i: (i, 0))
    extra = [] if token is None else [token]
    extra_specs = [] if token is None else [ANY_SPEC]
    return _call(
        body, name=name, grid=(nl, r // tr), in_specs=[p_spec] * 3 + [g_spec] * nl + extra_specs,
        out_specs=[p_spec] * 3, out_shape=[jax.ShapeDtypeStruct((nl, r, c), F32)] * 3, compiler_params=_cp(),
    )(w, m, v, *grads, *extra)


def _adamw_update(w, g, m, v):
    nm = ADAM_B1 * m + (1.0 - ADAM_B1) * g
    nv = ADAM_B2 * v + (1.0 - ADAM_B2) * (g * g)
    delta = -ADAM_LR * ((nm / ADAM_BC1) / (jnp.sqrt(nv / ADAM_BC2) + ADAM_EPS) + ADAM_WD * w)
    return delta, nm, nv


def adamw_many(name, ws, gs, ms, vs, token):
    n = len(ws)

    def body(*refs):
        ins, outs = refs[:4 * n], refs[4 * n + 1:]
        for i in range(n):
            delta, nm, nv = _adamw_update(ins[i][...], ins[n + i][...], ins[2 * n + i][...], ins[3 * n + i][...])
            outs[3 * i][...] = delta
            outs[3 * i + 1][...] = nm
            outs[3 * i + 2][...] = nv

    vmem = pl.BlockSpec(memory_space=pltpu.VMEM)
    shapes = [jax.ShapeDtypeStruct(w.shape, F32) for w in ws for _ in range(3)]
    outs = _call(
        body, name=name, in_specs=[vmem] * (4 * n) + [ANY_SPEC], out_specs=[vmem] * (3 * n), out_shape=shapes,
        compiler_params=_cp(),
    )(*ws, *gs, *ms, *vs, token)
    return [tuple(outs[3 * i:3 * i + 3]) for i in range(n)]


def sum_blocks(name, gathered, n_blocks):
    r = gathered.shape[0] // n_blocks
    c = gathered.shape[1]
    tr = r if r <= SUM_BLOCK_ROWS else _ew_rows(r)
    nt = r // tr

    def body(*refs):
        acc = refs[0][...]
        for j in range(1, n_blocks):
            acc = acc + refs[j][...]
        refs[-1][...] = acc

    specs = [pl.BlockSpec((tr, c), lambda i, j=j: (j * nt + i, 0)) for j in range(n_blocks)]
    return _call(
        body, name=name, grid=(nt,), in_specs=specs, out_specs=pl.BlockSpec((tr, c), lambda i: (i, 0)),
        out_shape=jax.ShapeDtypeStruct((r, c), F32), compiler_params=_cp(),
    )(*([gathered] * n_blocks))


def _place():
    return lax.axis_index("x"), lax.axis_index("y"), lax.axis_index("c")


def _other_chips(x, y):
    return [(1 - x, y), (x, 1 - y), (1 - x, 1 - y)]


HBM_SPEC = pl.BlockSpec(memory_space=pltpu.HBM)
SEM_SPEC = pl.BlockSpec(memory_space=pltpu.SEMAPHORE)
ANY_SPEC = pl.BlockSpec(memory_space=pl.ANY)
EFFECT = pltpu.SideEffectType.DATAFLOW_SIDE_EFFECTING


def copies_start(name, bufs, plan, n_copies):
    n = len(bufs)

    def body(*refs):
        send, recv = refs[n], refs[n + 1]
        token = refs[2 * n + 2]
        for k, (src, dst, peer, _) in enumerate(plan(refs[:n])):
            pltpu.make_async_remote_copy(
                src_ref=src, dst_ref=dst, send_sem=send.at[k], recv_sem=recv.at[k],
                device_id=peer, device_id_type=MESH).start()
        token[...] = jnp.zeros(token.shape, F32)

    outs = pl.pallas_call(
        body, name=name,
        out_shape=(pltpu.SemaphoreType.DMA((n_copies,)), pltpu.SemaphoreType.DMA((n_copies,)),
                   *[pltpu.HBM(b.shape, b.dtype) for b in bufs], jax.ShapeDtypeStruct((8, LANES), F32)),
        in_specs=[HBM_SPEC] * n,
        out_specs=(SEM_SPEC, SEM_SPEC, *([HBM_SPEC] * n), pl.BlockSpec(memory_space=pltpu.VMEM)),
        input_output_aliases={a: a + 2 for a in range(n)},
        compiler_params=pltpu.CompilerParams(has_side_effects=EFFECT),
    )(*[_in_hbm(b) for b in bufs])
    return outs[0], outs[1], list(outs[2:2 + n]), outs[2 + n]


def copies_wait(name, bufs, send, recv, plan, sem_base, after):
    n = len(bufs)

    def body(*refs):
        send_ref, recv_ref = refs[n], refs[n + 1]
        for k, (src, _, peer, land) in enumerate(plan(refs[:n])):
            cp = pltpu.make_async_remote_copy(
                src_ref=src, dst_ref=land, send_sem=send_ref.at[sem_base + k], recv_sem=recv_ref.at[sem_base + k],
                device_id=peer, device_id_type=MESH)
            cp.wait_send()
            cp.wait_recv()

    outs = pl.pallas_call(
        body, name=name,
        out_shape=tuple(pltpu.HBM(b.shape, b.dtype) for b in bufs),
        in_specs=[HBM_SPEC] * n + [SEM_SPEC, SEM_SPEC, ANY_SPEC], out_specs=tuple([HBM_SPEC] * n),
        input_output_aliases={a: a for a in range(n)},
        compiler_params=pltpu.CompilerParams(has_side_effects=EFFECT),
    )(*bufs, send, recv, after)
    return list(outs)


def gather_plan(refs):
    x, y, c = _place()
    me = 2 * x + y
    return [(buf.at[me], buf.at[me], (cx, cy, c), buf.at[2 * cx + cy])
            for buf in refs for cx, cy in _other_chips(x, y)]


def all_plan(refs):
    x, y, c = _place()
    me = 4 * x + 2 * y + c
    out = []
    for buf in refs:
        for flip in range(1, 8):
            px = 1 - x if flip & 4 else x
            py = 1 - y if flip & 2 else y
            pc = 1 - c if flip & 1 else c
            out.append((buf.at[me], buf.at[me], (px, py, pc), buf.at[4 * px + 2 * py + pc]))
    return out


def swap_plan(refs):
    x, y, c = _place()
    n = len(refs) // 2
    out = []
    for g, land in zip(refs[:n], refs[n:]):
        hr = g.shape[1] // 2
        out.append((g.at[:, pl.ds((1 - c) * hr, hr)], land, (x, y, 1 - c), land))
    return out


def owners_plan(refs):
    x, y, c = _place()
    n = len(refs) // 2
    return [(src.at[2 * cx + cy], land.at[j], (cx, cy, c), land.at[j])
            for src, land in zip(refs[:n], refs[n:]) for j, (cx, cy) in enumerate(_other_chips(x, y))]


def join_plan(refs):
    x, y, c = _place()
    out = []
    for buf in refs:
        hr = buf.shape[0] // 2
        mine = buf.at[pl.ds(c * hr, hr)]
        out.append((mine, mine, (x, y, 1 - c), buf.at[pl.ds((1 - c) * hr, hr)]))
    return out


def add_halves(name, grad, landed, sc_idx):
    _, r, c = grad.shape
    hr = r // 2
    tr = _ew_rows(hr)
    nt = hr // tr

    def body(sc_ref, g_ref, l_ref, own_ref, wire_ref):
        tot = g_ref[...] + l_ref[...]
        wire_ref[...] = tot.astype(BF16)

        @pl.when(pl.program_id(1) == sc_ref[0])
        def _():
            own_ref[...] = tot

    grid_spec = pltpu.PrefetchScalarGridSpec(
        num_scalar_prefetch=1, grid=(nt, N_SHARD),
        in_specs=[pl.BlockSpec((None, tr, c), lambda i, sh, sc_ref: (sh, sc_ref[1] * nt + i, 0)),
                  pl.BlockSpec((None, tr, c), lambda i, sh, sc_ref: (sh, i, 0))],
        out_specs=[pl.BlockSpec((tr, c), lambda i, sh, sc_ref: (i, 0)),
                   pl.BlockSpec((None, tr, c), lambda i, sh, sc_ref: (sh, i, 0))])
    return _call(
        body, name=name, grid_spec=grid_spec,
        out_shape=[jax.ShapeDtypeStruct((hr, c), F32), jax.ShapeDtypeStruct((N_SHARD, hr, c), BF16)],
        compiler_params=_cp(),
    )(sc_idx, grad, landed)


def add_owned(name, own, landed, sc_idx):
    hr, c = own.shape
    tr = _ew_rows(hr)
    nt = hr // tr

    def body(sc_ref, o_ref, l0, l1, l2, out_ref):
        out_ref[...] = ((o_ref[...] + l0[...].astype(F32)) + l1[...].astype(F32)) + l2[...].astype(F32)

    grid_spec = pltpu.PrefetchScalarGridSpec(
        num_scalar_prefetch=1, grid=(nt,),
        in_specs=[pl.BlockSpec((tr, c), lambda i, sc_ref: (i, 0))]
        + [pl.BlockSpec((None, tr, c), lambda i, sc_ref, j=j: (j, i, 0)) for j in range(3)],
        out_specs=pl.BlockSpec((tr, c), lambda i, sc_ref: (sc_ref[1] * nt + i, 0)))
    return _call(
        body, name=name, grid_spec=grid_spec, out_shape=jax.ShapeDtypeStruct((2 * hr, c), F32),
        compiler_params=_cp(),
    )(sc_idx, own, landed, landed, landed)


PACK_QUANTUM = 8 * LANES


def _pack(arrays):
    pieces = []
    for a in arrays:
        flat = a.reshape(-1)
        padded = -(-flat.shape[0] // PACK_QUANTUM) * PACK_QUANTUM
        pieces.append(jnp.pad(flat, (0, padded - flat.shape[0])).reshape(-1, LANES))
    return jnp.concatenate(pieces, axis=0)


def _unpack(packed, shapes):
    out = []
    row = 0
    for shp in shapes:
        size = math.prod(shp)
        rows = -(-size // PACK_QUANTUM) * 8
        out.append(packed[row:row + rows].reshape(-1)[:size].reshape(shp))
        row += rows
    return out


def kernel(x, p, mix_w_in, pool_w, pool_scale, conv_dw_w, conv_dw_b, conv_ln_g, conv_ln_b, mix_w_out, attn_w_qkv, attn_rel_bias, attn_w_o, ln_mix_g, ln_mix_b, ffn_w_up, ffn_dw_w, ffn_dw_b, ffn_w_down, ple_w_proj, ple_w_gate, ple_b_gate, ln_ffn_g, ln_ffn_b, loss_target, m_mix_w_in, m_pool_w, m_pool_scale, m_conv_dw_w, m_conv_dw_b, m_conv_ln_g, m_conv_ln_b, m_mix_w_out, m_attn_w_qkv, m_attn_rel_bias, m_attn_w_o, m_ln_mix_g, m_ln_mix_b, m_ffn_w_up, m_ffn_dw_w, m_ffn_dw_b, m_ffn_w_down, m_ple_w_proj, m_ple_w_gate, m_ple_b_gate, m_ln_ffn_g, m_ln_ffn_b, v_mix_w_in, v_pool_w, v_pool_scale, v_conv_dw_w, v_conv_dw_b, v_conv_ln_g, v_conv_ln_b, v_mix_w_out, v_attn_w_qkv, v_attn_rel_bias, v_attn_w_o, v_ln_mix_g, v_ln_mix_b, v_ffn_w_up, v_ffn_dw_w, v_ffn_dw_b, v_ffn_w_down, v_ple_w_proj, v_ple_w_gate, v_ple_b_gate, v_ln_ffn_g, v_ln_ffn_b):
    xi, yi, ci = _place()
    shard_idx = (2 * xi + yi).astype(jnp.int32)
    s_arr = shard_idx.reshape(1)
    c_arr = ci.astype(jnp.int32).reshape(1)
    sc_arr = jnp.concatenate([s_arr, c_arr])

    x0 = x[0]
    target = loss_target[0]
    p_rows = p.reshape(p.shape[0] * p.shape[2], p.shape[3])
    seq = x0.shape[0]

    big = [
        ("mix_w_in", mix_w_in, m_mix_w_in, v_mix_w_in, True),
        ("mix_w_out", mix_w_out, m_mix_w_out, v_mix_w_out, False),
        ("attn_w_qkv", attn_w_qkv, m_attn_w_qkv, v_attn_w_qkv, True),
        ("attn_w_o", attn_w_o, m_attn_w_o, v_attn_w_o, False),
        ("ffn_w_up", ffn_w_up, m_ffn_w_up, v_ffn_w_up, True),
        ("ffn_w_down", ffn_w_down, m_ffn_w_down, v_ffn_w_down, False),
        ("ple_w_proj", ple_w_proj, m_ple_w_proj, v_ple_w_proj, True),
        ("ple_w_gate", ple_w_gate, m_ple_w_gate, v_ple_w_gate, False),
    ]
    params = {nm: w for nm, w, _, _, _ in big}
    col_sharded = {nm: cs for nm, _, _, _, cs in big}
    keys = [("mix_w_in", 0), ("mix_w_out", 0), ("ffn_w_up", 0), ("ffn_w_down", 0), ("ple_w_gate", 0),
            ("ple_w_proj", 0), ("attn_w_qkv", 0), ("attn_w_o", 0), ("ffn_w_up", 1), ("ffn_w_down", 1),
            ("ple_w_gate", 1), ("ple_w_proj", 1)]
    dw_shapes = [conv_dw_w.shape, ffn_dw_w.shape]
    dw_block = cast_into_gathered("place_dw", _pack([conv_dw_w, ffn_dw_w])[None], 0, s_arr, dtype=F32)
    n_first = 2
    started = {}
    gather_token = None
    for tag, group in (("first", keys[:n_first]), ("rest", keys[n_first:])):
        shards = [cast_into_gathered(f"cast_{nm}_{layer}", params[nm], layer, s_arr, token=gather_token)
                  for nm, layer in group]
        if tag == "first":
            shards.append(dw_block)
        send, recv, bufs, gather_token = copies_start(f"gather_start_{tag}", shards, gather_plan, 3 * len(shards))
        for a, key in enumerate(group):
            started[key] = (send, recv, bufs[a], 3 * a)
        if tag == "first":
            dw_started = (send, recv, bufs[-1], 3 * len(group))
    arrived_w = {}

    def weight(nm, layer, after=None):
        key = (nm, layer)
        if key not in arrived_w:
            send, recv, buf, base = started[key]
            arrived_w[key] = copies_wait(f"gather_wait_{nm}_{layer}", [buf], send, recv, gather_plan, base, after)[0]
        g = arrived_w[key]
        if col_sharded[nm]:
            return g
        return g.reshape(g.shape[0] * g.shape[1], g.shape[2])

    def tie(a, token):
        return a + token[0:1, 0:1].astype(a.dtype)

    class Reducer:
        def __init__(self, tag, group):
            self.tag, self.group, self.stage = tag, group, 0
            self.n = len(group)
            self.result = None

        def advance(self, after):
            tag, n = self.tag, self.n
            if self.stage == 0:
                grads = []
                for key in self.group:
                    g = big_grads[key]
                    grads.append(g if g.ndim == 3 else g.reshape(N_SHARD, g.shape[0] // N_SHARD, g.shape[1]))
                lands = [lax.empty((N_SHARD, g.shape[1] // 2, g.shape[2]), F32) for g in grads]
                self.sems = copies_start(f"swap_start_{tag}", grads + lands, swap_plan, n)
            elif self.stage == 1:
                send, recv, bufs, _ = self.sems
                outs = copies_wait(f"swap_wait_{tag}", bufs, send, recv, swap_plan, 0, after)
                self.own, wire = [], []
                for key, g, ld in zip(self.group, outs[:n], outs[n:]):
                    o, ob = add_halves(f"add_halves_{key[0]}_{key[1]}", g, ld, sc_arr)
                    self.own.append(o)
                    wire.append(ob)
                lands = [lax.empty((3,) + w.shape[1:], BF16) for w in wire]
                self.sems = copies_start(f"owners_start_{tag}", wire + lands, owners_plan, 3 * n)
            elif self.stage == 2:
                send, recv, bufs, _ = self.sems
                outs = copies_wait(f"owners_wait_{tag}", bufs, send, recv, owners_plan, 0, after)
                finals = [add_owned(f"add_owned_{key[0]}_{key[1]}", o, ar, sc_arr)
                          for key, o, ar in zip(self.group, self.own, outs[n:])]
                self.sems = copies_start(f"join_start_{tag}", finals, join_plan, n)
            elif self.stage == 3:
                send, recv, bufs, _ = self.sems
                outs = copies_wait(f"join_wait_{tag}", bufs, send, recv, join_plan, 0, after)
                self.result = dict(zip(self.group, outs))
                self.sems = None
            self.stage += 1
            return None if self.sems is None else self.sems[3]

    dw_cache = []

    def conv_weights(after):
        if not dw_cache:
            send, recv, buf, base = dw_started
            dw_all = copies_wait("gather_wait_dw", [buf], send, recv, gather_plan, base, after)[0]
            dw_parts = [_unpack(dw_all[k], dw_shapes) for k in range(N_SHARD)]
            dw_cache.append(jnp.concatenate([pc[0] for pc in dw_parts], axis=2)[0])
            dw_cache.append(jnp.concatenate([pc[1] for pc in dw_parts], axis=2))
        return dw_cache

    big_grads = {}
    small_grads = {}

    saved = []
    h_in = x0
    h_in_b = to_bf16("x_bf16", x0)
    for layer in range(N_LAYERS):
        sv = {"x_in": h_in_b}
        if layer % 2 == 0:
            u = mm_cols_fwd("mix_in", h_in_b, weight("mix_w_in", 0, gather_token), F32)
            conv_w_full, ffn_dw_full = conv_weights(u)
            cat, d_sv, e_sv, glu_sv, hh_sv, rs_sv = mixer_fwd(
                "mixer_fwd", u, pool_w[0], pool_scale, conv_w_full, conv_dw_b, conv_ln_g, conv_ln_b)
            mix = mm_rows_fwd("mix_out", cat, weight("mix_w_out", 0, cat))
            sv.update(u=u, cat=cat, d=d_sv, e=e_sv, glu=glu_sv, hh=hh_sv, rs=rs_sv)
        else:
            qkvp = mm_cols_fwd("attn_qkv", h_in_b, weight("attn_w_qkv", 0, h_in_b), BF16,
                               pad_blocks=PAD_ROWS // _row_tile(seq))
            bias = bias_tile("bias_tile", _bias_line(attn_rel_bias[0]))
            att = attn_fwd("attn_fwd", qkvp, bias)
            mix = mm_rows_fwd("attn_out", att, weight("attn_w_o", 0, att))
            sv.update(qkvp=qkvp, bias=bias, att=att)
        x1, x1_b, xh1, rs1 = ln_fwd(f"ln_mix_{layer}", h_in, mix, ln_mix_g[layer:layer + 1],
                                    ln_mix_b[layer:layer + 1])
        gv = mm_cols_fwd(f"ffn_up_{layer}", x1_b, weight("ffn_w_up", layer, x1_b), F32)
        hid = ffn_act_fwd(f"ffn_act_{layer}", gv, ffn_dw_full[layer], ffn_dw_b[layer:layer + 1])
        ffn = mm_rows_fwd(f"ffn_down_{layer}", hid, weight("ffn_w_down", layer, hid))
        pgl = mm_rows_fwd(f"ple_gate_{layer}", x1_b, weight("ple_w_gate", layer, ffn))
        pp = mm_cols_fwd(f"ple_proj_{layer}", p_rows, weight("ple_w_proj", layer, pgl), F32, part=(layer, N_LAYERS))
        bg = ple_b_gate[layer:layer + 1]
        x2, x2_b, xh2, rs2 = ln_fwd(f"ln_ffn_{layer}", x1, ffn, ln_ffn_g[layer:layer + 1], ln_ffn_b[layer:layer + 1],
                                    ple=(pgl, pp, bg), emit_y=layer < N_LAYERS - 1)
        sv.update(x1=x1_b, xh1=xh1, rs1=rs1, gv=gv, hid=hid, pgl=pgl, pp=pp, xh2=xh2, rs2=rs2)
        saved.append(sv)
        h_in, h_in_b = x2, x2_b

    reducers = []

    def open_group(tag, group):
        reducers.append(Reducer(tag, group))
        return reducers[-1].advance(None)

    def hook(after):
        token = None
        for red in reducers:
            if red.stage < 4:
                tk = red.advance(after)
                if tk is not None:
                    token = tk if token is None else token + tk
        return token

    def tied(a, token):
        return a if token is None else tie(a, token)

    parts = []
    token = None
    for layer in reversed(range(N_LAYERS)):
        sv = saved[layer]
        bg = ple_b_gate[layer:layer + 1]
        if layer == 0:
            token = open_group("layer1", [("attn_w_qkv", 0), ("attn_w_o", 0), ("ffn_w_up", 1), ("ffn_w_down", 1),
                                          ("ple_w_gate", 1), ("ple_w_proj", 1)])
        last = layer == N_LAYERS - 1
        res = ln_bwd(
            f"ln_ffn_bwd_{layer}", parts, sv["xh2"], sv["rs2"], tied(ln_ffn_g[layer:layer + 1], token),
            ple=(sv["pgl"], sv["pp"], bg), loss=(target, ln_ffn_b[layer:layer + 1]) if last else None)
        dz2, dg2, db2, dpp, dpgl, dbg = res[:6]
        if last:
            loss_part = res[6]
        small_grads[("ln_ffn_g", layer)] = dg2
        small_grads[("ln_ffn_b", layer)] = db2
        small_grads[("ple_b_gate", layer)] = dbg
        w_down = weight("ffn_w_down", layer)
        dhid = mm_rows_dx(f"ffn_down_dx_{layer}", dz2, w_down)
        big_grads[("ffn_w_down", layer)] = mm_rows_dw(f"ffn_down_dw_{layer}", sv["hid"], dz2)
        token = hook(big_grads[("ffn_w_down", layer)])
        dgv, ddw, ddb = ffn_act_bwd(f"ffn_act_bwd_{layer}", dhid, sv["gv"], ffn_dw_full[layer],
                                    tied(ffn_dw_b[layer:layer + 1], token))
        small_grads[("ffn_dw_w", layer)] = ddw
        small_grads[("ffn_dw_b", layer)] = ddb
        big_grads[("ffn_w_up", layer)] = mm_cols_dw(f"ffn_up_dw_{layer}", sv["x1"], dgv)
        t_up = mm_cols_dx(f"ffn_up_dx_{layer}", dgv, weight("ffn_w_up", layer))
        token = hook(t_up)
        big_grads[("ple_w_gate", layer)] = mm_rows_dw(f"ple_gate_dw_{layer}", sv["x1"], dpgl)
        t_gate = mm_rows_dx(f"ple_gate_dx_{layer}", dpgl, weight("ple_w_gate", layer))
        big_grads[("ple_w_proj", layer)] = mm_cols_dw(f"ple_proj_dw_{layer}", p_rows, dpp, part=(layer, N_LAYERS))
        token2 = hook(big_grads[("ple_w_proj", layer)])
        if token2 is not None:
            token = token2 if token is None else token + token2
        if layer == 0:
            token3 = open_group("layer0_ffn", [("ffn_w_up", 0), ("ffn_w_down", 0), ("ple_w_gate", 0), ("ple_w_proj", 0)])
            token = token3 if token is None else token + token3
        dz1, dg1, db1 = ln_bwd(
            f"ln_mix_bwd_{layer}", [(ALPHA, dz2), (1.0, t_up), (1.0, t_gate)], sv["xh1"], sv["rs1"],
            tied(ln_mix_g[layer:layer + 1], token))
        small_grads[("ln_mix_g", layer)] = dg1
        small_grads[("ln_mix_b", layer)] = db1
        if layer % 2 == 0:
            dcat = mm_rows_dx("mix_out_dx", dz1, weight("mix_w_out", 0))
            big_grads[("mix_w_out", 0)] = mm_rows_dw("mix_out_dw", sv["cat"], dz1)
            token = hook(big_grads[("mix_w_out", 0)])
            du, dpw, dps, dcw, dcb, dcg, dcbt = mixer_bwd(
                "mixer_bwd", dcat, sv["u"], sv["d"], sv["e"], sv["glu"], sv["hh"], sv["rs"],
                pool_w[0], pool_scale, conv_w_full, tied(conv_ln_g, token), conv_ln_b)
            small_grads[("pool_w", 0)] = dpw
            small_grads[("pool_scale", 0)] = dps
            small_grads[("conv_dw_w", 0)] = dcw
            small_grads[("conv_dw_b", 0)] = dcb
            small_grads[("conv_ln_g", 0)] = dcg
            small_grads[("conv_ln_b", 0)] = dcbt
            big_grads[("mix_w_in", 0)] = mm_cols_dw("mix_in_dw", sv["x_in"], du)
            hook(big_grads[("mix_w_in", 0)])
            open_group("layer0_mix", [("mix_w_in", 0), ("mix_w_out", 0)])
            dx_in = mm_cols_dx("mix_in_dx", du, weight("mix_w_in", 0), addend=(ALPHA, dz1))
            token = hook(dx_in)
        else:
            do = mm_rows_dx("attn_out_dx", dz1, weight("attn_w_o", 0), out_dtype=BF16)
            big_grads[("attn_w_o", 0)] = mm_rows_dw("attn_out_dw", sv["att"], dz1)
            dq, dk, dv, ds_sum = attn_bwd("attn_bwd", sv["qkvp"], sv["bias"], do)
            cols, sat = bias_grad_reduce("bias_grad", ds_sum)
            d_rel = jnp.concatenate(
                [jnp.zeros((N_HEADS, 1), F32),
                 jnp.flip(cols[:, 0, Q_TILE + SHEAR_SAT:Q_TILE - 1 + SHEAR_W], axis=1),
                 sat[:, 0, 0:1]], axis=1)
            small_grads[("attn_rel_bias", 0)] = d_rel
            dqkv = jnp.concatenate([dq, dk, dv], axis=1)
            big_grads[("attn_w_qkv", 0)] = mm_cols_dw("attn_qkv_dw", sv["x_in"], dqkv)
            dx_in = mm_cols_dx("attn_qkv_dx", dqkv, weight("attn_w_qkv", 0), addend=(ALPHA, dz1))
        parts = [(1.0, dx_in)]
    grad_x = dx_in

    small = [
        ("pool_w", pool_w, m_pool_w, v_pool_w, None),
        ("pool_scale", pool_scale, m_pool_scale, v_pool_scale, None),
        ("conv_dw_w", conv_dw_w, m_conv_dw_w, v_conv_dw_w, 2),
        ("conv_dw_b", conv_dw_b, m_conv_dw_b, v_conv_dw_b, None),
        ("conv_ln_g", conv_ln_g, m_conv_ln_g, v_conv_ln_g, None),
        ("conv_ln_b", conv_ln_b, m_conv_ln_b, v_conv_ln_b, None),
        ("attn_rel_bias", attn_rel_bias, m_attn_rel_bias, v_attn_rel_bias, None),
        ("ln_mix_g", ln_mix_g, m_ln_mix_g, v_ln_mix_g, None),
        ("ln_mix_b", ln_mix_b, m_ln_mix_b, v_ln_mix_b, None),
        ("ffn_dw_w", ffn_dw_w, m_ffn_dw_w, v_ffn_dw_w, 2),
        ("ffn_dw_b", ffn_dw_b, m_ffn_dw_b, v_ffn_dw_b, None),
        ("ple_b_gate", ple_b_gate, m_ple_b_gate, v_ple_b_gate, None),
        ("ln_ffn_g", ln_ffn_g, m_ln_ffn_g, v_ln_ffn_g, None),
        ("ln_ffn_b", ln_ffn_b, m_ln_ffn_b, v_ln_ffn_b, None),
    ]
    full_grads = []
    for nm, w, _, _, shard_axis in small:
        full = list(w.shape)
        if shard_axis is not None:
            full[shard_axis] *= N_SHARD
        per_layer = [small_grads[(nm, layer)].reshape((1,) + tuple(full[1:])) for layer in range(w.shape[0])]
        full_grads.append(jnp.concatenate(per_layer, axis=0))
    packed = _pack(full_grads + [loss_part])
    dev_arr = (4 * xi + 2 * yi + ci).astype(jnp.int32).reshape(1)
    sg_block = cast_into_gathered("place_small_grads", packed[None], 0, dev_arr, n_blocks=8, dtype=F32)
    sg_send, sg_recv, sg_bufs, sg_token = copies_start("small_grads_start", [sg_block], all_plan, 7)
    token = sg_token if token is None else token + sg_token

    shard_grads = {}
    for red in reducers:
        if red.stage == 4:
            shard_grads.update(red.result)
    big_out = {}

    def update_big(names, tok):
        for nm, w, m, v, _ in big:
            if nm in names:
                gl = [shard_grads[(nm, layer)] for layer in range(w.shape[0])]
                delta, new_m, new_v = adamw(f"adamw_{nm}", w, gl, m, v, token=tok)
                big_out[nm] = (jnp.stack(gl, axis=0), delta, new_m, new_v)

    last_group = ("mix_w_in", "mix_w_out")
    update_big([nm for nm, _, _, _, _ in big if nm not in last_group], token)
    token = hook(big_out["ffn_w_up"][1])

    gathered_sg = copies_wait("small_grads_wait", sg_bufs, sg_send, sg_recv, all_plan, 0, big_out["ffn_w_down"][1])[0]
    total = sum_blocks("sum_small", gathered_sg.reshape(8 * packed.shape[0], LANES), 8)
    unpacked = _unpack(total, [g.shape for g in full_grads] + [loss_part.shape])
    loss = unpacked[-1][0, 0]
    local_grads = []
    for (nm, w, _, _, shard_axis), g in zip(small, unpacked[:-1]):
        if shard_axis is not None:
            width = w.shape[shard_axis]
            g = lax.dynamic_slice_in_dim(g, shard_idx * width, width, axis=shard_axis)
        local_grads.append(g.reshape(w.shape))
    updated = adamw_many("adamw_small", [w for _, w, _, _, _ in small], local_grads,
                         [m for _, _, m, _, _ in small], [v for _, _, _, v, _ in small], token)
    hook(updated[0][0])
    for red in reducers:
        shard_grads.update(red.result)
    update_big(last_group, None)
    small_out = {}
    for (nm, _, _, _, _), g, (d_, m_, v_) in zip(small, local_grads, updated):
        small_out[nm] = (g, d_, m_, v_)

    order = ["mix_w_in", "pool_w", "pool_scale", "conv_dw_w", "conv_dw_b", "conv_ln_g", "conv_ln_b", "mix_w_out",
             "attn_w_qkv", "attn_rel_bias", "attn_w_o", "ln_mix_g", "ln_mix_b", "ffn_w_up", "ffn_dw_w", "ffn_dw_b",
             "ffn_w_down", "ple_w_proj", "ple_w_gate", "ple_b_gate", "ln_ffn_g", "ln_ffn_b"]
    res = {**big_out, **small_out}
    outs = [loss, grad_x[None]]
    for slot in range(4):
        outs += [res[nm][slot] for nm in order]
    return tuple(outs)
```

```python
import math

import jax
import jax.numpy as jnp
from jax import lax
from jax.experimental import pallas as pl
from jax.experimental.pallas import tpu as pltpu

F32 = jnp.float32
BF16 = jnp.bfloat16
MESH = pl.DeviceIdType.MESH

N_LAYERS = 2
ALPHA = (2 * N_LAYERS) ** 0.25
LN_EPS = 1e-5
NEG_INF = -1e30
CHUNK = 64
LEFT_CHUNKS = 8
PAD_ROWS = LEFT_CHUNKS * CHUNK
HEAD_DIM = 64
ATTN_SCALE = HEAD_DIM ** -0.5
N_HEADS = 16
MAX_REL = 256
POOL_WINDOWS = (2, 4, 8, 16)
POOL_GROUP = 128
CONV_K = 31
FFN_K = 3
CONV_HALO = 32
FFN_HALO = 8
FFN_TILE = 256
FFN_CHUNK_ROWS = 64
FFN_CHUNK_LANES = 128
Q_TILE = 256
K_WIN = Q_TILE + PAD_ROWS
LANES = 128
SUBLANES = 8
ATTN_PAIRS = 2
ATTN_PAIRS_FWD = 4
ATTN_LANES = ATTN_PAIRS * LANES
SHEAR_W = Q_TILE + K_WIN
SHEAR_SAT = SHEAR_W - 2 * MAX_REL
N_SHARD = 4

ADAM_LR = 0.001
ADAM_B1 = 0.9
ADAM_B2 = 0.999
ADAM_EPS = 1e-08
ADAM_WD = 0.01
ADAM_STEP = 10
ADAM_BC1 = 1.0 - ADAM_B1 ** ADAM_STEP
ADAM_BC2 = 1.0 - ADAM_B2 ** ADAM_STEP

DIMS = {
    "nn": (((1,), (0,)), ((), ())),
    "nt": (((1,), (1,)), ((), ())),
    "tn": (((0,), (0,)), ((), ())),
}


def _cp(vmem_mb=48, **kw):
    return pltpu.CompilerParams(vmem_limit_bytes=vmem_mb * 1024 * 1024, **kw)


def _in_hbm(a):
    return pltpu.with_memory_space_constraint(a, pltpu.HBM)


STAGING_LIMIT_BYTES = 1 << 20
SMALL_WEIGHT_BYTES = 1 << 22
SUM_BLOCK_ROWS = 2048
SMALL_BLOCK_BYTES = 1 << 19


def _call(body, **kw):
    call = pl.pallas_call(body, **kw)

    def run(*args):
        pinned = []
        for a in args:
            big = a.size * a.dtype.itemsize >= STAGING_LIMIT_BYTES
            pinned.append(_in_hbm(a) if big and not jnp.issubdtype(a.dtype, jnp.integer) else a)
        return call(*pinned)

    return run


def _dot(a, b, mode):
    return lax.dot_general(a.astype(BF16), b.astype(BF16), DIMS[mode], preferred_element_type=F32)


def _sig(x):
    return 1.0 / (1.0 + jnp.exp(-x))


def _row_tile(s):
    return min(512, s // 4)


def _mm_tile(s):
    return min(1024, s // 4)


def _mm(name, mode, a, b, in_specs, out_shape, out_spec, acc_shape, grid, nk, zero_first=False, vmem_mb=48,
        addend=None):
    out_f32 = out_shape.dtype == F32

    def body(a_ref, b_ref, *rest):
        k = pl.program_id(2)
        if addend is None:
            o_ref, scr = rest[0], rest[1:]
        else:
            add_ref, o_ref, scr = rest[0], rest[1], rest[2:]

        def compute():
            part = _dot(a_ref[...], b_ref[...], mode)
            if nk == 1:
                if addend is not None:
                    part = part + addend[0] * add_ref[...]
                o_ref[...] = part.astype(o_ref.dtype)
                return
            acc = o_ref if out_f32 else scr[0]

            @pl.when(k == 0)
            def _():
                acc[...] = part if addend is None else part + addend[0] * add_ref[...]

            @pl.when(k > 0)
            def _():
                acc[...] += part

            if not out_f32:
                @pl.when(k == nk - 1)
                def _():
                    o_ref[...] = acc[...].astype(o_ref.dtype)

        if zero_first:
            @pl.when(pl.program_id(1) == 0)
            def _():
                o_ref[...] = jnp.zeros(o_ref.shape, o_ref.dtype)

            pl.when(pl.program_id(1) > 0)(compute)
        else:
            compute()

    scratch = [] if (nk == 1 or out_f32) else [pltpu.VMEM(acc_shape, F32)]
    operands = [a, b] if addend is None else [a, b, addend[1]]
    specs = list(in_specs) if addend is None else list(in_specs) + [out_spec]
    return _call(
        body, name=name, grid=grid, in_specs=specs, out_specs=out_spec, out_shape=out_shape,
        scratch_shapes=scratch, compiler_params=_cp(vmem_mb),
    )(*operands)


def _is_small_weight(wc):
    return wc.size * 2 <= SMALL_WEIGHT_BYTES


def _all_shards(w_ref):
    return jnp.concatenate([w_ref[j] for j in range(N_SHARD)], axis=1)


def mm_cols_fwd(name, a, wc, out_dtype, pad_blocks=0, part=(0, 1)):
    s, k = a.shape
    s //= part[1]
    n4 = wc.shape[2]
    tm = _row_tile(s) if pad_blocks else _mm_tile(s)
    nt = s // tm
    first_block = part[0] * nt
    if _is_small_weight(wc) and not pad_blocks:
        def body(a_ref, w_ref, o_ref):
            o_ref[...] = _dot(a_ref[...], _all_shards(w_ref), "nn").astype(o_ref.dtype)

        return _call(
            body, name=name, grid=(nt,),
            in_specs=[pl.BlockSpec((tm, k), lambda i: (first_block + i, 0)), _full(wc.shape)],
            out_specs=pl.BlockSpec((tm, N_SHARD * n4), lambda i: (i, 0)),
            out_shape=jax.ShapeDtypeStruct((s, N_SHARD * n4), out_dtype), compiler_params=_cp(),
        )(a, wc)
    return _mm(
        name, "nn", a, wc,
        [pl.BlockSpec((tm, k), lambda j, i, r: (first_block + jnp.maximum(i - pad_blocks, 0), 0)),
         pl.BlockSpec((None, k, n4), lambda j, i, r: (j, 0, 0))],
        jax.ShapeDtypeStruct((s + pad_blocks * tm, N_SHARD * n4), out_dtype),
        pl.BlockSpec((tm, n4), lambda j, i, r: (i, j)),
        None, (N_SHARD, nt + pad_blocks, 1), 1, zero_first=pad_blocks > 0)


def mm_cols_dx(name, dy, wc, addend=None):
    s = dy.shape[0]
    _, k, n4 = wc.shape
    tm = _mm_tile(s)
    if _is_small_weight(wc):
        def body(dy_ref, w_ref, *rest):
            part = _dot(dy_ref[...], _all_shards(w_ref), "nt")
            rest[-1][...] = part if addend is None else part + addend[0] * rest[0][...]

        out_spec = pl.BlockSpec((tm, k), lambda i: (i, 0))
        extra, extra_specs = ([], []) if addend is None else ([addend[1]], [out_spec])
        return _call(
            body, name=name, grid=(s // tm,),
            in_specs=[pl.BlockSpec((tm, N_SHARD * n4), lambda i: (i, 0)), _full(wc.shape)] + extra_specs,
            out_specs=out_spec, out_shape=jax.ShapeDtypeStruct((s, k), F32), compiler_params=_cp(),
        )(dy, wc, *extra)
    return _mm(
        name, "nt", dy, wc,
        [pl.BlockSpec((tm, n4), lambda g, i, r: (i, r)),
         pl.BlockSpec((None, k, n4), lambda g, i, r: (r, 0, 0))],
        jax.ShapeDtypeStruct((s, k), F32),
        pl.BlockSpec((tm, k), lambda g, i, r: (i, 0)),
        (tm, k), (1, s // tm, N_SHARD), N_SHARD, addend=addend)


def mm_cols_dw(name, a, dy, part=(0, 1)):
    s, k = a.shape
    s //= part[1]
    n4 = dy.shape[1] // N_SHARD
    tm = _mm_tile(s)
    nt = s // tm
    first_block = part[0] * nt
    if k * n4 * N_SHARD * 2 <= SMALL_WEIGHT_BYTES:
        def body(a_ref, dy_ref, o_ref):
            full = _dot(a_ref[...], dy_ref[...], "tn")
            first = pl.program_id(0) == 0
            for j in range(N_SHARD):
                _acc_add(o_ref.at[j], first, full[:, j * n4:(j + 1) * n4])

        return _call(
            body, name=name, grid=(nt,),
            in_specs=[pl.BlockSpec((tm, k), lambda r: (first_block + r, 0)),
                      pl.BlockSpec((tm, N_SHARD * n4), lambda r: (r, 0))],
            out_specs=_full((N_SHARD, k, n4)),
            out_shape=jax.ShapeDtypeStruct((N_SHARD, k, n4), F32), compiler_params=_cp(),
        )(a, dy)
    return _mm(
        name, "tn", a, dy,
        [pl.BlockSpec((tm, k), lambda j, g, r: (first_block + r, 0)),
         pl.BlockSpec((tm, n4), lambda j, g, r: (r, j))],
        jax.ShapeDtypeStruct((N_SHARD, k, n4), F32),
        pl.BlockSpec((None, k, n4), lambda j, g, r: (j, 0, 0)),
        (k, n4), (N_SHARD, 1, nt), nt)


def _k_tile(k):
    return k if k <= 1024 else k // 2


def mm_rows_fwd(name, a, wr, out_dtype=F32):
    s, k = a.shape
    n = wr.shape[1]
    tm = _mm_tile(s)
    tk = _k_tile(k)
    nk = k // tk
    return _mm(
        name, "nn", a, wr,
        [pl.BlockSpec((tm, tk), lambda g, i, r: (i, r)),
         pl.BlockSpec((tk, n), lambda g, i, r: (r, 0))],
        jax.ShapeDtypeStruct((s, n), out_dtype),
        pl.BlockSpec((tm, n), lambda g, i, r: (i, 0)),
        (tm, n), (1, s // tm, nk), nk)


def mm_rows_dx(name, dy, wr, out_dtype=F32):
    s, n = dy.shape
    k = wr.shape[0]
    tm = _mm_tile(s)
    tk = _k_tile(k)
    return _mm(
        name, "nt", dy, wr,
        [pl.BlockSpec((tm, n), lambda j, i, r: (i, 0)),
         pl.BlockSpec((tk, n), lambda j, i, r: (j, 0))],
        jax.ShapeDtypeStruct((s, k), out_dtype),
        pl.BlockSpec((tm, tk), lambda j, i, r: (i, j)),
        None, (k // tk, s // tm, 1), 1)


def mm_rows_dw(name, a, dy):
    s, k = a.shape
    n = dy.shape[1]
    tm = _mm_tile(s)
    tk = _k_tile(k)
    nt = s // tm
    return _mm(
        name, "tn", a, dy,
        [pl.BlockSpec((tm, tk), lambda j, g, r: (r, j)),
         pl.BlockSpec((tm, n), lambda j, g, r: (r, 0))],
        jax.ShapeDtypeStruct((k, n), F32),
        pl.BlockSpec((tk, n), lambda j, g, r: (j, 0)),
        (tk, n), (k // tk, 1, nt), nt)


def _row(tm, c, col=0):
    return pl.BlockSpec((tm, c), lambda i: (i, col))


def _full(shape):
    nd = len(shape)
    return pl.BlockSpec(shape, lambda i: (0,) * nd)


def _prev(tm, h, c, col=0):
    return pl.BlockSpec((h, c), lambda i: (jnp.maximum(i * (tm // h) - 1, 0), col))


def _next(tm, h, c, s, col=0):
    return pl.BlockSpec((h, c), lambda i: (jnp.minimum((i + 1) * (tm // h), s // h - 1), col))


def _acc_add(ref, first, val):
    @pl.when(first)
    def _():
        ref[...] = val

    @pl.when(jnp.logical_not(first))
    def _():
        ref[...] += val


def _colsum(v):
    return jnp.sum(v, axis=0, keepdims=True)


def _ln_stats(z):
    mu = jnp.mean(z, axis=-1, keepdims=True)
    zc = z - mu
    var = jnp.mean(zc * zc, axis=-1, keepdims=True)
    rstd = lax.rsqrt(var + LN_EPS)
    return zc * rstd, rstd


def _ln_bwd(dxhat, xhat, rstd):
    m1 = jnp.mean(dxhat, axis=-1, keepdims=True)
    m2 = jnp.mean(dxhat * xhat, axis=-1, keepdims=True)
    return rstd * (dxhat - m1 - xhat * m2)


def ln_fwd(name, x, f, g, b, ple=None, emit_y=True):
    s, d = x.shape
    tm = _row_tile(s)
    n_in = 2 + (3 if ple is not None else 0)

    def body(*refs):
        x_ref, f_ref = refs[0], refs[1]
        g_ref, b_ref = refs[n_in], refs[n_in + 1]
        xh_ref, rs_ref = refs[-2:]
        z = ALPHA * x_ref[...] + f_ref[...]
        if ple is not None:
            pgl_ref, pp_ref, bg_ref = refs[2:5]
            z = z + _sig(pgl_ref[...] + bg_ref[...]) * pp_ref[...]
        xhat, rstd = _ln_stats(z)
        if emit_y:
            y = xhat * g_ref[...] + b_ref[...]
            refs[n_in + 2][...] = y
            refs[n_in + 3][...] = y.astype(BF16)
        xh_ref[...] = xhat
        rs_ref[...] = jnp.broadcast_to(rstd, rs_ref.shape)

    ins = [x, f]
    specs = [_row(tm, d), _row(tm, d)]
    if ple is not None:
        pgl, pp, bg = ple
        ins += [pgl, pp, bg]
        specs += [_row(tm, d), _row(tm, d), _full((1, d))]
    ins += [g, b]
    specs += [_full((1, d)), _full((1, d))]
    y_shapes = [jax.ShapeDtypeStruct((s, d), F32), jax.ShapeDtypeStruct((s, d), BF16)] if emit_y else []
    outs = _call(
        body, name=name, grid=(s // tm,), in_specs=specs,
        out_specs=[_row(tm, d)] * (len(y_shapes) + 1) + [_row(tm, LANES)],
        out_shape=y_shapes + [jax.ShapeDtypeStruct((s, d), F32), jax.ShapeDtypeStruct((s, LANES), F32)],
        compiler_params=_cp(),
    )(*ins)
    return tuple(outs) if emit_y else (None, None, outs[0], outs[1])


def ln_bwd(name, parts, xhat, rstd, g, ple=None, loss=None):
    s, d = xhat.shape
    tm = _row_tile(s)
    coefs = [c for c, _ in parts]
    n_p = len(parts)
    n_ple = 3 if ple is not None else 0
    n_in = n_p + 3 + n_ple + (2 if loss is not None else 0)

    def body(*refs):
        first = pl.program_id(0) == 0
        xh = refs[n_p][...]
        rs = refs[n_p + 1][:, 0:1]
        g_v = refs[n_p + 2][...]
        outs = refs[n_in:]
        if loss is not None:
            t_ref, b_ref = refs[n_p + 3 + n_ple:n_p + 5 + n_ple]
            err = (xh * g_v + b_ref[...]) - t_ref[...]
            dy = err * (1.0 / d)
            part = 0.5 * jnp.sum(jnp.mean(err * err, axis=-1, keepdims=True), axis=0, keepdims=True)
            _acc_add(outs[-1], first, jnp.broadcast_to(part, outs[-1].shape))
        else:
            dy = coefs[0] * refs[0][...].astype(F32)
            for j in range(1, n_p):
                dy = dy + coefs[j] * refs[j][...].astype(F32)
        dz = _ln_bwd(dy * g_v, xh, rs)
        outs[0][...] = dz
        _acc_add(outs[1], first, _colsum(dy * xh))
        _acc_add(outs[2], first, _colsum(dy))
        if ple is not None:
            pgl_ref, pp_ref, bg_ref = refs[n_p + 3:n_p + 6]
            pg = _sig(pgl_ref[...] + bg_ref[...])
            dpgl = dz * pp_ref[...] * pg * (1.0 - pg)
            outs[3][...] = (dz * pg).astype(BF16)
            outs[4][...] = dpgl.astype(BF16)
            _acc_add(outs[5], first, _colsum(dpgl))

    ins = [p for _, p in parts] + [xhat, rstd, g]
    specs = [_row(tm, d)] * n_p + [_row(tm, d), _row(tm, LANES), _full((1, d))]
    out_specs = [_row(tm, d), _full((1, d)), _full((1, d))]
    out_shape = [jax.ShapeDtypeStruct((s, d), F32), jax.ShapeDtypeStruct((1, d), F32),
                 jax.ShapeDtypeStruct((1, d), F32)]
    if ple is not None:
        pgl, pp, bg = ple
        ins += [pgl, pp, bg]
        specs += [_row(tm, d), _row(tm, d), _full((1, d))]
        out_specs += [_row(tm, d), _row(tm, d), _full((1, d))]
        out_shape += [jax.ShapeDtypeStruct((s, d), BF16), jax.ShapeDtypeStruct((s, d), BF16),
                      jax.ShapeDtypeStruct((1, d), F32)]
    if loss is not None:
        target, b = loss
        ins += [target, b]
        specs += [_row(tm, d), _full((1, d))]
        out_specs += [_full((8, LANES))]
        out_shape += [jax.ShapeDtypeStruct((8, LANES), F32)]
    return _call(
        body, name=name, grid=(s // tm,), in_specs=specs, out_specs=out_specs, out_shape=out_shape,
        compiler_params=_cp(),
    )(*ins)


def _fill_rotations(rot_ref, x, direction):
    n = x.shape[0]
    rot_ref[0] = x
    for b in range(1, SUBLANES):
        if direction < 0:
            rot_ref[b, SUBLANES:n, :] = x[SUBLANES - b:n - b]
        else:
            rot_ref[b, 0:n - SUBLANES, :] = x[b:n - SUBLANES + b]


def _rotated(rot_ref, start, rows, cs, direction=-1):
    b = (-start) % SUBLANES if direction < 0 else start % SUBLANES
    aligned = start + b if direction < 0 else start - b
    return rot_ref[b, pl.ds(aligned, rows), cs]


def _tile_pos(i, tm, rows):
    return (i * tm + lax.broadcasted_iota(jnp.int32, (rows, 1), 0) + 1).astype(F32)


def mixer_fwd(name, u, pool_w, pool_scale, conv_w, conv_b, cn_g, cn_b):
    s = u.shape[0]
    dp = 512
    tm = min(256, s // 4)
    h = CONV_HALO

    def body(a_c, a_p, bv_c, bv_p, bg_c, bg_p, pw_ref, ps_ref, cw_ref, cb_ref, cg_ref, cbt_ref,
             cat_ref, d_ref, e_ref, glu_ref, hh_ref, rs_ref, ext_a, rot_g, conv_out):
        i = pl.program_id(0)
        first = i == 0
        ext_a[0:h, :] = jnp.where(first, 0.0, a_p[...])
        ext_a[h:, :] = a_c[...]
        glu = bv_c[...] * _sig(bg_c[...])
        glu_ref[...] = glu
        _fill_rotations(rot_g, jnp.concatenate([jnp.where(first, 0.0, bv_p[...] * _sig(bg_p[...])), glu], axis=0), -1)
        pos = _tile_pos(i, tm, tm)
        for gi, w in enumerate(POOL_WINDOWS):
            cs = slice(gi * POOL_GROUP, (gi + 1) * POOL_GROUP)
            a_g = ext_a[pl.ds(h, tm), cs]
            acc = a_g
            for sh in range(1, w):
                acc = acc + ext_a[pl.ds(h - sh, tm), cs]
            d_g = acc / jnp.minimum(pos, float(w)) - a_g
            d_ref[:, cs] = d_g.astype(BF16)
            e_g = _dot(d_g, pw_ref[gi], "nn")
            e_ref[:, cs] = e_g
            cat_ref[:, cs] = (e_g * ps_ref[:, cs]).astype(BF16)
        for lg in range(dp // LANES):
            cs = slice(lg * LANES, (lg + 1) * LANES)
            acc = jnp.broadcast_to(cb_ref[:, cs], (tm, LANES))
            for sh in range(CONV_K):
                acc = acc + _rotated(rot_g, h - sh, tm, cs) * cw_ref[pl.ds(CONV_K - 1 - sh, 1), cs]
            conv_out[:, cs] = acc
        hhat, rstd = _ln_stats(conv_out[...])
        hl = hhat * cg_ref[...] + cbt_ref[...]
        cat_ref[:, dp:] = (hl * _sig(hl)).astype(BF16)
        hh_ref[...] = hhat
        rs_ref[...] = jnp.broadcast_to(rstd, rs_ref.shape)

    specs = [_row(tm, dp, 0), _prev(tm, h, dp, 0), _row(tm, dp, 1), _prev(tm, h, dp, 1),
             _row(tm, dp, 2), _prev(tm, h, dp, 2),
             _full((4, POOL_GROUP, POOL_GROUP)), _full((1, dp)), _full((CONV_K, dp)),
             _full((1, dp)), _full((1, dp)), _full((1, dp))]
    out_specs = [_row(tm, 2 * dp), _row(tm, dp), _row(tm, dp), _row(tm, dp), _row(tm, dp), _row(tm, LANES)]
    out_shape = [jax.ShapeDtypeStruct((s, 2 * dp), BF16), jax.ShapeDtypeStruct((s, dp), BF16),
                 jax.ShapeDtypeStruct((s, dp), F32), jax.ShapeDtypeStruct((s, dp), F32),
                 jax.ShapeDtypeStruct((s, dp), F32), jax.ShapeDtypeStruct((s, LANES), F32)]
    return _call(
        body, name=name, grid=(s // tm,), in_specs=specs, out_specs=out_specs, out_shape=out_shape,
        scratch_shapes=[pltpu.VMEM((h + tm, dp), F32), pltpu.VMEM((SUBLANES, h + tm, dp), F32),
                        pltpu.VMEM((tm, dp), F32)],
        compiler_params=_cp(),
    )(u, u, u, u, u, u, pool_w, pool_scale, conv_w, conv_b, cn_g, cn_b)


def mixer_bwd(name, dcat, u, d_sv, e_sv, glu_sv, hh_sv, rs_sv, pool_w, pool_scale, conv_w, cn_g, cn_b):
    s = u.shape[0]
    dp = 512
    tm = min(256, s // 4)
    h = CONV_HALO
    nt = s // tm

    def body(dc_c, dc_n, bv_c, bg_c, d_c, e_c, gl_c, gl_p, hh_c, hh_n, rs_c, rs_n,
             pw_ref, ps_ref, cw_ref, cg_ref, cbt_ref,
             du_ref, dpw_ref, dps_ref, dcw_ref, dcb_ref, dcg_ref, dcbt_ref,
             ext_dh, ext_g, ext_r):
        i = pl.program_id(0)
        first = i == 0
        last = i == nt - 1
        cg = cg_ref[...]

        def conv_grads(dyb, hhat, rstd):
            hl = hhat * cg + cbt_ref[...]
            sg = _sig(hl)
            dhl = dyb * (sg * (1.0 + hl * (1.0 - sg)))
            return _ln_bwd(dhl * cg, hhat, rstd), dhl

        hh_cur = hh_c[...]
        dh_c, dhl_c = conv_grads(dc_c[:, dp:], hh_cur, rs_c[:, 0:1])
        dh_n, _ = conv_grads(dc_n[:, dp:], hh_n[...], rs_n[:, 0:1])
        _fill_rotations(ext_dh, jnp.concatenate([dh_c, jnp.where(last, 0.0, dh_n)], axis=0), 1)
        _fill_rotations(ext_g, jnp.concatenate([jnp.where(first, 0.0, gl_p[...]), gl_c[...]], axis=0), -1)

        @pl.when(first)
        def _():
            dcw_ref[...] = jnp.zeros(dcw_ref.shape, F32)

        for lg in range(dp // LANES):
            cs = slice(lg * LANES, (lg + 1) * LANES)
            dglu = jnp.zeros((tm, LANES), F32)
            for sh in range(CONV_K):
                dglu = dglu + _rotated(ext_dh, sh, tm, cs, 1) * cw_ref[pl.ds(CONV_K - 1 - sh, 1), cs]
            dh_g = ext_dh[0, pl.ds(0, tm), cs]
            for sh in range(CONV_K):
                dcw_ref[pl.ds(CONV_K - 1 - sh, 1), cs] += _colsum(dh_g * _rotated(ext_g, h - sh, tm, cs))
            sgate = _sig(bg_c[:, cs])
            du_ref[:, dp + lg * LANES:dp + (lg + 1) * LANES] = dglu * sgate
            du_ref[:, 2 * dp + lg * LANES:2 * dp + (lg + 1) * LANES] = dglu * bv_c[:, cs] * sgate * (1.0 - sgate)
        _acc_add(dcb_ref, first, _colsum(dh_c))
        _acc_add(dcg_ref, first, _colsum(dhl_c * hh_cur))
        _acc_add(dcbt_ref, first, _colsum(dhl_c))

        pos_c = _tile_pos(i, tm, tm)
        pos_n = _tile_pos(i + 1, tm, h)
        _acc_add(dps_ref, first, _colsum(dc_c[:, :dp] * e_c[...]))
        for gi, w in enumerate(POOL_WINDOWS):
            cs = slice(gi * POOL_GROUP, (gi + 1) * POOL_GROUP)
            pw = pw_ref[gi]
            de_c = dc_c[:, cs] * ps_ref[:, cs]
            de_n = dc_n[:, cs] * ps_ref[:, cs]
            dd_c = _dot(de_c, pw, "nt")
            dd_n = _dot(de_n, pw, "nt")
            ext_r[0:tm, :] = dd_c / jnp.minimum(pos_c, float(w))
            ext_r[tm:, :] = jnp.where(last, 0.0, dd_n / jnp.minimum(pos_n, float(w)))
            acc = -dd_c
            for sh in range(w):
                acc = acc + ext_r[pl.ds(sh, tm), :]
            du_ref[:, cs] = acc
            dpw_g = _dot(d_c[:, cs], de_c, "tn")

            @pl.when(first)
            def _():
                dpw_ref[gi] = dpw_g

            @pl.when(jnp.logical_not(first))
            def _():
                dpw_ref[gi] += dpw_g

    specs = [_row(tm, 2 * dp), _next(tm, h, 2 * dp, s), _row(tm, dp, 1), _row(tm, dp, 2),
             _row(tm, dp), _row(tm, dp), _row(tm, dp), _prev(tm, h, dp),
             _row(tm, dp), _next(tm, h, dp, s), _row(tm, LANES), _next(tm, h, LANES, s),
             _full((4, POOL_GROUP, POOL_GROUP)), _full((1, dp)), _full((CONV_K, dp)),
             _full((1, dp)), _full((1, dp))]
    out_specs = [_row(tm, 3 * dp), _full((4, POOL_GROUP, POOL_GROUP)), _full((1, dp)), _full((CONV_K, dp)),
                 _full((1, dp)), _full((1, dp)), _full((1, dp))]
    out_shape = [jax.ShapeDtypeStruct((s, 3 * dp), F32),
                 jax.ShapeDtypeStruct((4, POOL_GROUP, POOL_GROUP), F32), jax.ShapeDtypeStruct((1, dp), F32),
                 jax.ShapeDtypeStruct((CONV_K, dp), F32), jax.ShapeDtypeStruct((1, dp), F32),
                 jax.ShapeDtypeStruct((1, dp), F32), jax.ShapeDtypeStruct((1, dp), F32)]
    return _call(
        body, name=name, grid=(nt,), in_specs=specs, out_specs=out_specs, out_shape=out_shape,
        scratch_shapes=[pltpu.VMEM((SUBLANES, tm + h, dp), F32), pltpu.VMEM((SUBLANES, h + tm, dp), F32),
                        pltpu.VMEM((tm + h, POOL_GROUP), F32)],
        compiler_params=_cp(),
    )(dcat, dcat, u, u, d_sv, e_sv, glu_sv, glu_sv, hh_sv, hh_sv, rs_sv, rs_sv,
      pool_w, pool_scale, conv_w, cn_g, cn_b)


GELU_C = math.sqrt(2.0 / math.pi)


def _gelu_parts(x):
    x2 = x * x
    t = jnp.tanh(x * (GELU_C + (GELU_C * 0.044715) * x2))
    half_1pt = 0.5 + 0.5 * t
    gelu = x * half_1pt
    dgelu = half_1pt + (0.5 * x) * (1.0 - t * t) * (GELU_C + (3.0 * GELU_C * 0.044715) * x2)
    return gelu, dgelu


def ffn_act_fwd(name, gv, dw_w, dw_b):
    s = gv.shape[0]
    dff = gv.shape[1] // 2
    tm = min(FFN_TILE, s // 4)
    h = FFN_HALO
    rc = FFN_CHUNK_ROWS
    lw = FFN_CHUNK_LANES

    def body(g_c, g_p, v_c, w_ref, b_ref, hid_ref):
        first = pl.program_id(0) == 0

        def chunk(ci, carry):
            r0 = pl.multiple_of(ci * rc, rc)
            above = pl.multiple_of(jnp.maximum(r0 - h, 0), h)
            for lg in range(dff // lw):
                cs = slice(lg * lw, (lg + 1) * lw)
                top = jnp.where(ci == 0, jnp.where(first, 0.0, g_p[:, cs]), g_c[pl.ds(above, h), cs])
                win = jnp.concatenate([top, g_c[pl.ds(r0, rc), cs]], axis=0)
                gc = jnp.broadcast_to(b_ref[:, cs], (rc, lw))
                for sh in range(FFN_K):
                    gc = gc + win[h - sh:h - sh + rc] * w_ref[pl.ds(FFN_K - 1 - sh, 1), cs]
                gelu, _ = _gelu_parts(gc)
                hid_ref[pl.ds(r0, rc), cs] = (gelu * v_c[pl.ds(r0, rc), cs]).astype(BF16)
            return carry

        lax.fori_loop(0, tm // rc, chunk, 0)

    return _call(
        body, name=name, grid=(s // tm,),
        in_specs=[_row(tm, dff, 0), _prev(tm, h, dff, 0), _row(tm, dff, 1), _full((FFN_K, dff)), _full((1, dff))],
        out_specs=_row(tm, dff), out_shape=jax.ShapeDtypeStruct((s, dff), BF16),
        compiler_params=_cp(),
    )(gv, gv, gv, dw_w, dw_b)


def ffn_act_bwd(name, dhid, gv, dw_w, dw_b):
    s = gv.shape[0]
    dff = gv.shape[1] // 2
    tm = min(FFN_TILE, s // 4)
    h = FFN_HALO
    nt = s // tm
    rc = FFN_CHUNK_ROWS
    lw = FFN_CHUNK_LANES
    n_chunks = tm // rc

    def body(dh_c, dh_n, g_p, g_c, g_n, v_c, v_n, w_ref, b_ref, dgv_ref, dw_ref, db_ref):
        i = pl.program_id(0)
        first = i == 0
        last = i == nt - 1

        @pl.when(first)
        def _():
            dw_ref[...] = jnp.zeros(dw_ref.shape, F32)
            db_ref[...] = jnp.zeros(db_ref.shape, F32)

        def chunk(ci, carry):
            r0 = pl.multiple_of(ci * rc, rc)
            above = pl.multiple_of(jnp.maximum(r0 - h, 0), h)
            below = pl.multiple_of(jnp.minimum(r0 + rc, tm - h), h)
            at_end = ci == n_chunks - 1
            for lg in range(dff // lw):
                cs = slice(lg * lw, (lg + 1) * lw)
                top = jnp.where(ci == 0, jnp.where(first, 0.0, g_p[:, cs]), g_c[pl.ds(above, h), cs])
                bot = jnp.where(at_end, g_n[:, cs], g_c[pl.ds(below, h), cs])
                win = jnp.concatenate([top, g_c[pl.ds(r0, rc), cs], bot], axis=0)
                shifted = [win[h - sh:h - sh + rc + h] for sh in range(FFN_K)]
                gc = jnp.broadcast_to(b_ref[:, cs], (rc + h, lw))
                for sh in range(FFN_K):
                    gc = gc + shifted[sh] * w_ref[pl.ds(FFN_K - 1 - sh, 1), cs]
                gelu, dgelu = _gelu_parts(gc)
                dh_mid = dh_c[pl.ds(r0, rc), cs]
                hv_bot = jnp.where(at_end, jnp.where(last, 0.0, dh_n[:, cs] * v_n[:, cs]),
                                   dh_c[pl.ds(below, h), cs] * v_c[pl.ds(below, h), cs])
                dgc = jnp.concatenate([dh_mid * v_c[pl.ds(r0, rc), cs], hv_bot], axis=0) * dgelu
                dgate = jnp.zeros((rc, lw), F32)
                for sh in range(FFN_K):
                    dgate = dgate + dgc[sh:sh + rc] * w_ref[pl.ds(FFN_K - 1 - sh, 1), cs]
                dgv_ref[pl.ds(r0, rc), cs] = dgate.astype(BF16)
                dgv_ref[pl.ds(r0, rc), slice(dff + lg * lw, dff + (lg + 1) * lw)] = (dh_mid * gelu[0:rc]).astype(BF16)
                dgc_mid = dgc[0:rc]
                for sh in range(FFN_K):
                    dw_ref[pl.ds(FFN_K - 1 - sh, 1), cs] += _colsum(dgc_mid * shifted[sh][0:rc])
                db_ref[:, cs] += _colsum(dgc_mid)
            return carry

        lax.fori_loop(0, n_chunks, chunk, 0)

    return _call(
        body, name=name, grid=(nt,),
        in_specs=[_row(tm, dff), _next(tm, h, dff, s),
                  _prev(tm, h, dff, 0), _row(tm, dff, 0), _next(tm, h, dff, s, 0),
                  _row(tm, dff, 1), _next(tm, h, dff, s, 1),
                  _full((FFN_K, dff)), _full((1, dff))],
        out_specs=[_row(tm, 2 * dff), _full((FFN_K, dff)), _full((1, dff))],
        out_shape=[jax.ShapeDtypeStruct((s, 2 * dff), BF16), jax.ShapeDtypeStruct((FFN_K, dff), F32),
                   jax.ShapeDtypeStruct((1, dff), F32)],
        compiler_params=_cp(),
    )(dhid, dhid, gv, gv, gv, gv, gv, dw_w, dw_b)


def _bias_line(rel_bias):
    nh = rel_bias.shape[0]
    line = jnp.concatenate(
        [jnp.zeros((nh, 1), rel_bias.dtype), jnp.broadcast_to(rel_bias[:, 2 * MAX_REL:], (nh, SHEAR_SAT)),
         jnp.flip(rel_bias[:, 1:2 * MAX_REL], axis=1)], axis=1)
    return line[:, None, :]


def bias_tile(name, line):
    nh = line.shape[0]

    def body(l_ref, o_ref):
        x = jnp.broadcast_to(l_ref[...], (Q_TILE, SHEAR_W))
        z = pltpu.roll(x, SHEAR_W - Q_TILE, 1, stride=1, stride_axis=0)
        qc = lax.broadcasted_iota(jnp.int32, (Q_TILE, K_WIN), 0) // CHUNK
        kc = lax.broadcasted_iota(jnp.int32, (Q_TILE, K_WIN), 1) // CHUNK
        o_ref[...] = jnp.where((kc >= qc) & (kc <= qc + LEFT_CHUNKS), z[:, :K_WIN], NEG_INF)

    return _call(
        body, name=name, grid=(nh,), in_specs=[pl.BlockSpec((None, 1, SHEAR_W), lambda hh: (hh, 0, 0))],
        out_specs=pl.BlockSpec((None, Q_TILE, K_WIN), lambda hh: (hh, 0, 0)),
        out_shape=jax.ShapeDtypeStruct((nh, Q_TILE, K_WIN), F32), compiler_params=_cp(),
    )(line)


def _stack_heads(x2, scale=None):
    if scale is not None:
        x2 = x2 * jnp.asarray(scale, x2.dtype)
    lane = lax.broadcasted_iota(jnp.int32, x2.shape, 1)
    zero = jnp.zeros_like(x2)
    return jnp.concatenate([jnp.where(lane < HEAD_DIM, x2, zero), jnp.where(lane < HEAD_DIM, zero, x2)], axis=0)


def _unstack_heads(x_st):
    lane = lax.broadcasted_iota(jnp.int32, (Q_TILE, LANES), 1)
    return jnp.where(lane < HEAD_DIM, x_st[:Q_TILE], x_st[Q_TILE:])


def _attn_probs(q_st, k3, bias_st, t):
    sc = _dot(q_st, k3, "nt") + bias_st
    col = lax.broadcasted_iota(jnp.int32, sc.shape, 1)
    sc = jnp.where(col >= PAD_ROWS - t * Q_TILE, sc, NEG_INF)
    m = jnp.max(sc, axis=-1, keepdims=True)
    p = jnp.exp(sc - m)
    return p * (1.0 / jnp.sum(p, axis=-1, keepdims=True))


def _attn_specs(d_model, pairs):
    nq = PAD_ROWS // Q_TILE
    width = pairs * LANES
    groups = d_model // width
    specs = [pl.BlockSpec((Q_TILE, width), lambda g, t: (t + nq, g))]
    for which in (1, 2):
        for j in range(K_WIN // Q_TILE):
            specs.append(pl.BlockSpec((Q_TILE, width), lambda g, t, j=j, which=which: (t + j, which * groups + g)))
    specs.append(pl.BlockSpec((2 * pairs, Q_TILE, K_WIN), lambda g, t: (g, 0, 0)))
    return specs


def attn_fwd(name, qkvp, bias):
    s = qkvp.shape[0] - PAD_ROWS
    d_model = qkvp.shape[1] // 3
    nw = K_WIN // Q_TILE

    def body(q_ref, *refs):
        k_refs, v_refs, b_ref, o_ref = refs[:nw], refs[nw:2 * nw], refs[2 * nw], refs[2 * nw + 1]
        t = pl.program_id(1)
        for j in range(ATTN_PAIRS_FWD):
            ls = slice(j * LANES, (j + 1) * LANES)
            k3 = jnp.concatenate([r[:, ls] for r in k_refs], axis=0)
            v3 = jnp.concatenate([r[:, ls] for r in v_refs], axis=0)
            bias_st = b_ref[2 * j:2 * j + 2].reshape(2 * Q_TILE, K_WIN)
            p = _attn_probs(_stack_heads(q_ref[:, ls], ATTN_SCALE), k3, bias_st, t)
            o_ref[:, ls] = _unstack_heads(_dot(p, v3, "nn")).astype(BF16)

    width = ATTN_PAIRS_FWD * LANES
    return _call(
        body, name=name, grid=(d_model // width, s // Q_TILE),
        in_specs=_attn_specs(d_model, ATTN_PAIRS_FWD), out_specs=pl.BlockSpec((Q_TILE, width), lambda g, t: (t, g)),
        out_shape=jax.ShapeDtypeStruct((s, d_model), BF16), compiler_params=_cp(),
    )(qkvp, *([qkvp] * (2 * nw)), bias)


def attn_bwd(name, qkvp, bias, do):
    s = qkvp.shape[0] - PAD_ROWS
    d_model = qkvp.shape[1] // 3
    nw = K_WIN // Q_TILE
    nt = s // Q_TILE

    def body(q_ref, *refs):
        k_refs, v_refs = refs[:nw], refs[nw:2 * nw]
        b_ref, do_ref, dq_ref, dk_ref, dv_ref, ds_ref, dk_acc, dv_acc = refs[2 * nw:]
        t = pl.program_id(1)
        first = t == 0

        @pl.when(first)
        def _():
            dk_acc[...] = jnp.zeros(dk_acc.shape, F32)
            dv_acc[...] = jnp.zeros(dv_acc.shape, F32)
            ds_ref[...] = jnp.zeros(ds_ref.shape, F32)

        start = pl.multiple_of(t * Q_TILE, Q_TILE)
        for j in range(ATTN_PAIRS):
            ls = slice(j * LANES, (j + 1) * LANES)
            q_st = _stack_heads(q_ref[:, ls], ATTN_SCALE)
            do_st = _stack_heads(do_ref[:, ls])
            k3 = jnp.concatenate([r[:, ls] for r in k_refs], axis=0)
            v3 = jnp.concatenate([r[:, ls] for r in v_refs], axis=0)
            p = _attn_probs(q_st, k3, b_ref[2 * j:2 * j + 2].reshape(2 * Q_TILE, K_WIN), t)
            dp = _dot(do_st, v3, "nt")
            ds = p * (dp - jnp.sum(p * dp, axis=-1, keepdims=True))
            ds_ref[2 * j:2 * j + 2] += ds.reshape(2, Q_TILE, K_WIN)
            dsb = ds.astype(BF16)
            dq_ref[:, ls] = (_unstack_heads(_dot(dsb, k3, "nn")) * ATTN_SCALE).astype(BF16)
            dk_acc[pl.ds(start, K_WIN), ls] += _dot(dsb, q_st, "tn")
            dv_acc[pl.ds(start, K_WIN), ls] += _dot(p, do_st, "tn")

        @pl.when(t == nt - 1)
        def _():
            dk_ref[...] = dk_acc[pl.ds(PAD_ROWS, s), :].astype(BF16)
            dv_ref[...] = dv_acc[pl.ds(PAD_ROWS, s), :].astype(BF16)

    specs = _attn_specs(d_model, ATTN_PAIRS) + [pl.BlockSpec((Q_TILE, ATTN_LANES), lambda g, t: (t, g))]
    col_spec = pl.BlockSpec((s, ATTN_LANES), lambda g, t: (0, g))
    return _call(
        body, name=name, grid=(d_model // ATTN_LANES, nt), in_specs=specs,
        out_specs=[pl.BlockSpec((Q_TILE, ATTN_LANES), lambda g, t: (t, g)), col_spec, col_spec,
                   pl.BlockSpec((2 * ATTN_PAIRS, Q_TILE, K_WIN), lambda g, t: (g, 0, 0))],
        out_shape=[jax.ShapeDtypeStruct((s, d_model), BF16)] * 3
        + [jax.ShapeDtypeStruct((N_HEADS, Q_TILE, K_WIN), F32)],
        scratch_shapes=[pltpu.VMEM((PAD_ROWS + s, ATTN_LANES), F32), pltpu.VMEM((PAD_ROWS + s, ATTN_LANES), F32)],
        compiler_params=_cp(),
    )(qkvp, *([qkvp] * (2 * nw)), bias, do)


def bias_grad_reduce(name, ds_sum):
    nh = ds_sum.shape[0]
    width = SHEAR_W + Q_TILE
    first_k = Q_TILE - 1

    def body(x_ref, col_ref, sat_ref):
        x = x_ref[...]
        hi = x.astype(BF16)
        lo = (x - hi.astype(F32)).astype(BF16)
        r = lax.broadcasted_iota(jnp.int32, (Q_TILE, Q_TILE), 0)
        c = lax.broadcasted_iota(jnp.int32, (Q_TILE, Q_TILE), 1)
        exchange = jnp.where(r + c == Q_TILE - 1, 1.0, 0.0).astype(BF16)
        x_rev = _dot(exchange, hi, "nn") + _dot(exchange, lo, "nn")
        zeros = jnp.zeros((Q_TILE, Q_TILE), F32)
        y = pltpu.roll(jnp.concatenate([zeros, x_rev, zeros], axis=1), 0, 1, stride=1, stride_axis=0)
        cols = _colsum(y)
        col_ref[...] = cols
        k = lax.broadcasted_iota(jnp.int32, cols.shape, 1) - first_k
        tot = jnp.sum(jnp.where((k >= 1) & (k <= SHEAR_SAT), cols, 0.0), axis=-1, keepdims=True)
        sat_ref[...] = jnp.broadcast_to(tot, sat_ref.shape)

    return _call(
        body, name=name, grid=(nh,),
        in_specs=[pl.BlockSpec((None, Q_TILE, K_WIN), lambda hh: (hh, 0, 0))],
        out_specs=[pl.BlockSpec((None, 1, width), lambda hh: (hh, 0, 0)),
                   pl.BlockSpec((None, 1, LANES), lambda hh: (hh, 0, 0))],
        out_shape=[jax.ShapeDtypeStruct((nh, 1, width), F32), jax.ShapeDtypeStruct((nh, 1, LANES), F32)],
        compiler_params=_cp(),
    )(ds_sum)


def _ew_rows(r, most=512, cols=None):
    if cols is not None and r * cols * 4 <= SMALL_BLOCK_BYTES:
        return r
    for cand in range(min(most, r) // 16 * 16, 0, -16):
        if r % cand == 0:
            return cand
    return r


def cast_into_gathered(name, w, layer, s_idx, n_blocks=N_SHARD, dtype=BF16, token=None):
    r, c = w.shape[-2:]
    tr = _ew_rows(r, cols=c)

    def body(s_ref, w_ref, *rest):
        rest[-1][...] = w_ref[...].astype(dtype)

    extra = [] if token is None else [token]
    grid_spec = pltpu.PrefetchScalarGridSpec(
        num_scalar_prefetch=1, grid=(r // tr,),
        in_specs=[pl.BlockSpec((None, tr, c), lambda i, s_ref: (layer, i, 0))] + [ANY_SPEC] * len(extra),
        out_specs=pl.BlockSpec((None, tr, c), lambda i, s_ref: (s_ref[0], i, 0)))
    return _call(
        body, name=name, grid_spec=grid_spec, out_shape=jax.ShapeDtypeStruct((n_blocks, r, c), dtype),
        compiler_params=_cp(),
    )(s_idx, w, *extra)


def adamw(name, w, grads, m, v, token=None):
    nl, r, c = w.shape
    tr = _ew_rows(r, 256, cols=c)

    def body(*refs):
        w_ref, m_ref, v_ref = refs[0], refs[1], refs[2]
        g_refs = refs[3:3 + nl]
        d_ref, nm_ref, nv_ref = refs[-3:]
        layer = pl.program_id(0)
        g = g_refs[0][...]
        for j in range(1, nl):
            g = jnp.where(layer == j, g_refs[j][...], g)
        d_ref[...], nm_ref[...], nv_ref[...] = _adamw_update(w_ref[...], g, m_ref[...], v_ref[...])

    p_spec = pl.BlockSpec((None, tr, c), lambda l, i: (l, i, 0))
    g_spec = pl.BlockSpec((tr, c), lambda l, i: (i, 0))
    extra = [] if token is None else [token]
    extra_specs = [] if token is None else [ANY_SPEC]
    return _call(
        body, name=name, grid=(nl, r // tr), in_specs=[p_spec] * 3 + [g_spec] * nl + extra_specs,
        out_specs=[p_spec] * 3, out_shape=[jax.ShapeDtypeStruct((nl, r, c), F32)] * 3, compiler_params=_cp(),
    )(w, m, v, *grads, *extra)


def _adamw_update(w, g, m, v):
    nm = ADAM_B1 * m + (1.0 - ADAM_B1) * g
    nv = ADAM_B2 * v + (1.0 - ADAM_B2) * (g * g)
    delta = -ADAM_LR * ((nm / ADAM_BC1) / (jnp.sqrt(nv / ADAM_BC2) + ADAM_EPS) + ADAM_WD * w)
    return delta, nm, nv


def adamw_many(name, ws, gs, ms, vs, token):
    n = len(ws)

    def body(*refs):
        ins, outs = refs[:4 * n], refs[4 * n + 1:]
        for i in range(n):
            delta, nm, nv = _adamw_update(ins[i][...], ins[n + i][...], ins[2 * n + i][...], ins[3 * n + i][...])
            outs[3 * i][...] = delta
            outs[3 * i + 1][...] = nm
            outs[3 * i + 2][...] = nv

    vmem = pl.BlockSpec(memory_space=pltpu.VMEM)
    shapes = [jax.ShapeDtypeStruct(w.shape, F32) for w in ws for _ in range(3)]
    outs = _call(
        body, name=name, in_specs=[vmem] * (4 * n) + [ANY_SPEC], out_specs=[vmem] * (3 * n), out_shape=shapes,
        compiler_params=_cp(),
    )(*ws, *gs, *ms, *vs, token)
    return [tuple(outs[3 * i:3 * i + 3]) for i in range(n)]


def sum_blocks(name, gathered, n_blocks):
    r = gathered.shape[0] // n_blocks
    c = gathered.shape[1]
    tr = r if r <= SUM_BLOCK_ROWS else _ew_rows(r)
    nt = r // tr

    def body(*refs):
        acc = refs[0][...]
        for j in range(1, n_blocks):
            acc = acc + refs[j][...]
        refs[-1][...] = acc

    specs = [pl.BlockSpec((tr, c), lambda i, j=j: (j * nt + i, 0)) for j in range(n_blocks)]
    return _call(
        body, name=name, grid=(nt,), in_specs=specs, out_specs=pl.BlockSpec((tr, c), lambda i: (i, 0)),
        out_shape=jax.ShapeDtypeStruct((r, c), F32), compiler_params=_cp(),
    )(*([gathered] * n_blocks))


def _place():
    return lax.axis_index("x"), lax.axis_index("y"), lax.axis_index("c")


def _other_chips(x, y):
    return [(1 - x, y), (x, 1 - y), (1 - x, 1 - y)]


HBM_SPEC = pl.BlockSpec(memory_space=pltpu.HBM)
SEM_SPEC = pl.BlockSpec(memory_space=pltpu.SEMAPHORE)
ANY_SPEC = pl.BlockSpec(memory_space=pl.ANY)
EFFECT = pltpu.SideEffectType.DATAFLOW_SIDE_EFFECTING


def copies_start(name, bufs, plan, n_copies):
    n = len(bufs)

    def body(*refs):
        send, recv = refs[n], refs[n + 1]
        token = refs[2 * n + 2]
        for k, (src, dst, peer, _) in enumerate(plan(refs[:n])):
            pltpu.make_async_remote_copy(
                src_ref=src, dst_ref=dst, send_sem=send.at[k], recv_sem=recv.at[k],
                device_id=peer, device_id_type=MESH).start()
        token[...] = jnp.zeros(token.shape, F32)

    outs = pl.pallas_call(
        body, name=name,
        out_shape=(pltpu.SemaphoreType.DMA((n_copies,)), pltpu.SemaphoreType.DMA((n_copies,)),
                   *[pltpu.HBM(b.shape, b.dtype) for b in bufs], jax.ShapeDtypeStruct((8, LANES), F32)),
        in_specs=[HBM_SPEC] * n,
        out_specs=(SEM_SPEC, SEM_SPEC, *([HBM_SPEC] * n), pl.BlockSpec(memory_space=pltpu.VMEM)),
        input_output_aliases={a: a + 2 for a in range(n)},
        compiler_params=pltpu.CompilerParams(has_side_effects=EFFECT),
    )(*[_in_hbm(b) for b in bufs])
    return outs[0], outs[1], list(outs[2:2 + n]), outs[2 + n]


def copies_wait(name, bufs, send, recv, plan, sem_base, after):
    n = len(bufs)

    def body(*refs):
        send_ref, recv_ref = refs[n], refs[n + 1]
        for k, (src, _, peer, land) in enumerate(plan(refs[:n])):
            cp = pltpu.make_async_remote_copy(
                src_ref=src, dst_ref=land, send_sem=send_ref.at[sem_base + k], recv_sem=recv_ref.at[sem_base + k],
                device_id=peer, device_id_type=MESH)
            cp.wait_send()
            cp.wait_recv()

    outs = pl.pallas_call(
        body, name=name,
        out_shape=tuple(pltpu.HBM(b.shape, b.dtype) for b in bufs),
        in_specs=[HBM_SPEC] * n + [SEM_SPEC, SEM_SPEC, ANY_SPEC], out_specs=tuple([HBM_SPEC] * n),
        input_output_aliases={a: a for a in range(n)},
        compiler_params=pltpu.CompilerParams(has_side_effects=EFFECT),
    )(*bufs, send, recv, after)
    return list(outs)


def gather_plan(refs):
    x, y, c = _place()
    me = 2 * x + y
    return [(buf.at[me], buf.at[me], (cx, cy, c), buf.at[2 * cx + cy])
            for buf in refs for cx, cy in _other_chips(x, y)]


def all_plan(refs):
    x, y, c = _place()
    me = 4 * x + 2 * y + c
    out = []
    for buf in refs:
        for flip in range(1, 8):
            px = 1 - x if flip & 4 else x
            py = 1 - y if flip & 2 else y
            pc = 1 - c if flip & 1 else c
            out.append((buf.at[me], buf.at[me], (px, py, pc), buf.at[4 * px + 2 * py + pc]))
    return out


def swap_plan(refs):
    x, y, c = _place()
    n = len(refs) // 2
    out = []
    for g, land in zip(refs[:n], refs[n:]):
        hr = g.shape[1] // 2
        out.append((g.at[:, pl.ds((1 - c) * hr, hr)], land, (x, y, 1 - c), land))
    return out


def owners_plan(refs):
    x, y, c = _place()
    n = len(refs) // 2
    return [(src.at[2 * cx + cy], land.at[j], (cx, cy, c), land.at[j])
            for src, land in zip(refs[:n], refs[n:]) for j, (cx, cy) in enumerate(_other_chips(x, y))]


def join_plan(refs):
    x, y, c = _place()
    out = []
    for buf in refs:
        hr = buf.shape[0] // 2
        mine = buf.at[pl.ds(c * hr, hr)]
        out.append((mine, mine, (x, y, 1 - c), buf.at[pl.ds((1 - c) * hr, hr)]))
    return out


def add_halves(name, grad, landed, sc_idx):
    _, r, c = grad.shape
    hr = r // 2
    tr = _ew_rows(hr)
    nt = hr // tr

    def body(sc_ref, g_ref, l_ref, own_ref, wire_ref):
        tot = g_ref[...] + l_ref[...]
        wire_ref[...] = tot.astype(BF16)

        @pl.when(pl.program_id(1) == sc_ref[0])
        def _():
            own_ref[...] = tot

    grid_spec = pltpu.PrefetchScalarGridSpec(
        num_scalar_prefetch=1, grid=(nt, N_SHARD),
        in_specs=[pl.BlockSpec((None, tr, c), lambda i, sh, sc_ref: (sh, sc_ref[1] * nt + i, 0)),
                  pl.BlockSpec((None, tr, c), lambda i, sh, sc_ref: (sh, i, 0))],
        out_specs=[pl.BlockSpec((tr, c), lambda i, sh, sc_ref: (i, 0)),
                   pl.BlockSpec((None, tr, c), lambda i, sh, sc_ref: (sh, i, 0))])
    return _call(
        body, name=name, grid_spec=grid_spec,
        out_shape=[jax.ShapeDtypeStruct((hr, c), F32), jax.ShapeDtypeStruct((N_SHARD, hr, c), BF16)],
        compiler_params=_cp(),
    )(sc_idx, grad, landed)


def add_owned(name, own, landed, sc_idx):
    hr, c = own.shape
    tr = _ew_rows(hr)
    nt = hr // tr

    def body(sc_ref, o_ref, l0, l1, l2, out_ref):
        out_ref[...] = ((o_ref[...] + l0[...].astype(F32)) + l1[...].astype(F32)) + l2[...].astype(F32)

    grid_spec = pltpu.PrefetchScalarGridSpec(
        num_scalar_prefetch=1, grid=(nt,),
        in_specs=[pl.BlockSpec((tr, c), lambda i, sc_ref: (i, 0))]
        + [pl.BlockSpec((None, tr, c), lambda i, sc_ref, j=j: (j, i, 0)) for j in range(3)],
        out_specs=pl.BlockSpec((tr, c), lambda i, sc_ref: (sc_ref[1] * nt + i, 0)))
    return _call(
        body, name=name, grid_spec=grid_spec, out_shape=jax.ShapeDtypeStruct((2 * hr, c), F32),
        compiler_params=_cp(),
    )(sc_idx, own, landed, landed, landed)


PACK_QUANTUM = 8 * LANES


def _pack(arrays):
    pieces = []
    for a in arrays:
        flat = a.reshape(-1)
        padded = -(-flat.shape[0] // PACK_QUANTUM) * PACK_QUANTUM
        pieces.append(jnp.pad(flat, (0, padded - flat.shape[0])).reshape(-1, LANES))
    return jnp.concatenate(pieces, axis=0)


def _unpack(packed, shapes):
    out = []
    row = 0
    for shp in shapes:
        size = math.prod(shp)
        rows = -(-size // PACK_QUANTUM) * 8
        out.append(packed[row:row + rows].reshape(-1)[:size].reshape(shp))
        row += rows
    return out


def kernel(x, p, mix_w_in, pool_w, pool_scale, conv_dw_w, conv_dw_b, conv_ln_g, conv_ln_b, mix_w_out, attn_w_qkv, attn_rel_bias, attn_w_o, ln_mix_g, ln_mix_b, ffn_w_up, ffn_dw_w, ffn_dw_b, ffn_w_down, ple_w_proj, ple_w_gate, ple_b_gate, ln_ffn_g, ln_ffn_b, loss_target, m_mix_w_in, m_pool_w, m_pool_scale, m_conv_dw_w, m_conv_dw_b, m_conv_ln_g, m_conv_ln_b, m_mix_w_out, m_attn_w_qkv, m_attn_rel_bias, m_attn_w_o, m_ln_mix_g, m_ln_mix_b, m_ffn_w_up, m_ffn_dw_w, m_ffn_dw_b, m_ffn_w_down, m_ple_w_proj, m_ple_w_gate, m_ple_b_gate, m_ln_ffn_g, m_ln_ffn_b, v_mix_w_in, v_pool_w, v_pool_scale, v_conv_dw_w, v_conv_dw_b, v_conv_ln_g, v_conv_ln_b, v_mix_w_out, v_attn_w_qkv, v_attn_rel_bias, v_attn_w_o, v_ln_mix_g, v_ln_mix_b, v_ffn_w_up, v_ffn_dw_w, v_ffn_dw_b, v_ffn_w_down, v_ple_w_proj, v_ple_w_gate, v_ple_b_gate, v_ln_ffn_g, v_ln_ffn_b):
    xi, yi, ci = _place()
    shard_idx = (2 * xi + yi).astype(jnp.int32)
    s_arr = shard_idx.reshape(1)
    c_arr = ci.astype(jnp.int32).reshape(1)
    sc_arr = jnp.concatenate([s_arr, c_arr])

    x0 = x[0]
    target = loss_target[0]
    p_rows = p.reshape(p.shape[0] * p.shape[2], p.shape[3])
    seq = x0.shape[0]

    big = [
        ("mix_w_in", mix_w_in, m_mix_w_in, v_mix_w_in, True),
        ("mix_w_out", mix_w_out, m_mix_w_out, v_mix_w_out, False),
        ("attn_w_qkv", attn_w_qkv, m_attn_w_qkv, v_attn_w_qkv, True),
        ("attn_w_o", attn_w_o, m_attn_w_o, v_attn_w_o, False),
        ("ffn_w_up", ffn_w_up, m_ffn_w_up, v_ffn_w_up, True),
        ("ffn_w_down", ffn_w_down, m_ffn_w_down, v_ffn_w_down, False),
        ("ple_w_proj", ple_w_proj, m_ple_w_proj, v_ple_w_proj, True),
        ("ple_w_gate", ple_w_gate, m_ple_w_gate, v_ple_w_gate, False),
    ]
    params = {nm: w for nm, w, _, _, _ in big}
    col_sharded = {nm: cs for nm, _, _, _, cs in big}
    keys = [("mix_w_in", 0), ("mix_w_out", 0), ("ffn_w_up", 0), ("ffn_w_down", 0), ("ple_w_gate", 0),
            ("ple_w_proj", 0), ("attn_w_qkv", 0), ("attn_w_o", 0), ("ffn_w_up", 1), ("ffn_w_down", 1),
            ("ple_w_gate", 1), ("ple_w_proj", 1)]
    dw_shapes = [conv_dw_w.shape, ffn_dw_w.shape]
    dw_block = cast_into_gathered("place_dw", _pack([conv_dw_w, ffn_dw_w])[None], 0, s_arr, dtype=F32)
    n_first = 2
    started = {}
    gather_token = None
    for tag, group in (("first", keys[:n_first]), ("rest", keys[n_first:])):
        shards = [cast_into_gathered(f"cast_{nm}_{layer}", params[nm], layer, s_arr, token=gather_token)
                  for nm, layer in group]
        if tag == "first":
            shards.append(dw_block)
        send, recv, bufs, gather_token = copies_start(f"gather_start_{tag}", shards, gather_plan, 3 * len(shards))
        for a, key in enumerate(group):
            started[key] = (send, recv, bufs[a], 3 * a)
        if tag == "first":
            dw_started = (send, recv, bufs[-1], 3 * len(group))
    arrived_w = {}

    def weight(nm, layer, after=None):
        key = (nm, layer)
        if key not in arrived_w:
            send, recv, buf, base = started[key]
            arrived_w[key] = copies_wait(f"gather_wait_{nm}_{layer}", [buf], send, recv, gather_plan, base, after)[0]
        g = arrived_w[key]
        if col_sharded[nm]:
            return g
        return g.reshape(g.shape[0] * g.shape[1], g.shape[2])

    def tie(a, token):
        return a + token[0:1, 0:1].astype(a.dtype)

    class Reducer:
        def __init__(self, tag, group):
            self.tag, self.group, self.stage = tag, group, 0
            self.n = len(group)
            self.result = None

        def advance(self, after):
            tag, n = self.tag, self.n
            if self.stage == 0:
                grads = []
                for key in self.group:
                    g = big_grads[key]
                    grads.append(g if g.ndim == 3 else g.reshape(N_SHARD, g.shape[0] // N_SHARD, g.shape[1]))
                lands = [lax.empty((N_SHARD, g.shape[1] // 2, g.shape[2]), F32) for g in grads]
                self.sems = copies_start(f"swap_start_{tag}", grads + lands, swap_plan, n)
            elif self.stage == 1:
                send, recv, bufs, _ = self.sems
                outs = copies_wait(f"swap_wait_{tag}", bufs, send, recv, swap_plan, 0, after)
                self.own, wire = [], []
                for key, g, ld in zip(self.group, outs[:n], outs[n:]):
                    o, ob = add_halves(f"add_halves_{key[0]}_{key[1]}", g, ld, sc_arr)
                    self.own.append(o)
                    wire.append(ob)
                lands = [lax.empty((3,) + w.shape[1:], BF16) for w in wire]
                self.sems = copies_start(f"owners_start_{tag}", wire + lands, owners_plan, 3 * n)
            elif self.stage == 2:
                send, recv, bufs, _ = self.sems
                outs = copies_wait(f"owners_wait_{tag}", bufs, send, recv, owners_plan, 0, after)
                finals = [add_owned(f"add_owned_{key[0]}_{key[1]}", o, ar, sc_arr)
                          for key, o, ar in zip(self.group, self.own, outs[n:])]
                self.sems = copies_start(f"join_start_{tag}", finals, join_plan, n)
            elif self.stage == 3:
                send, recv, bufs, _ = self.sems
                outs = copies_wait(f"join_wait_{tag}", bufs, send, recv, join_plan, 0, after)
                self.result = dict(zip(self.group, outs))
                self.sems = None
            self.stage += 1
            return None if self.sems is None else self.sems[3]

    dw_cache = []

    def conv_weights(after):
        if not dw_cache:
            send, recv, buf, base = dw_started
            dw_all = copies_wait("gather_wait_dw", [buf], send, recv, gather_plan, base, after)[0]
            dw_parts = [_unpack(dw_all[k], dw_shapes) for k in range(N_SHARD)]
            dw_cache.append(jnp.concatenate([pc[0] for pc in dw_parts], axis=2)[0])
            dw_cache.append(jnp.concatenate([pc[1] for pc in dw_parts], axis=2))
        return dw_cache

    big_grads = {}
    small_grads = {}

    saved = []
    h_in = x0
    h_in_b = x0
    for layer in range(N_LAYERS):
        sv = {"x_in": h_in_b}
        if layer % 2 == 0:
            u = mm_cols_fwd("mix_in", h_in_b, weight("mix_w_in", 0, gather_token), F32)
            conv_w_full, ffn_dw_full = conv_weights(u)
            cat, d_sv, e_sv, glu_sv, hh_sv, rs_sv = mixer_fwd(
                "mixer_fwd", u, pool_w[0], pool_scale, conv_w_full, conv_dw_b, conv_ln_g, conv_ln_b)
            mix = mm_rows_fwd("mix_out", cat, weight("mix_w_out", 0, cat))
            sv.update(u=u, cat=cat, d=d_sv, e=e_sv, glu=glu_sv, hh=hh_sv, rs=rs_sv)
        else:
            qkvp = mm_cols_fwd("attn_qkv", h_in_b, weight("attn_w_qkv", 0, h_in_b), BF16,
                               pad_blocks=PAD_ROWS // _row_tile(seq))
            bias = bias_tile("bias_tile", _bias_line(attn_rel_bias[0]))
            att = attn_fwd("attn_fwd", qkvp, bias)
            mix = mm_rows_fwd("attn_out", att, weight("attn_w_o", 0, att))
            sv.update(qkvp=qkvp, bias=bias, att=att)
        x1, x1_b, xh1, rs1 = ln_fwd(f"ln_mix_{layer}", h_in, mix, ln_mix_g[layer:layer + 1],
                                    ln_mix_b[layer:layer + 1])
        gv = mm_cols_fwd(f"ffn_up_{layer}", x1_b, weight("ffn_w_up", layer, x1_b), F32)
        hid = ffn_act_fwd(f"ffn_act_{layer}", gv, ffn_dw_full[layer], ffn_dw_b[layer:layer + 1])
        ffn = mm_rows_fwd(f"ffn_down_{layer}", hid, weight("ffn_w_down", layer, hid))
        pgl = mm_rows_fwd(f"ple_gate_{layer}", x1_b, weight("ple_w_gate", layer, ffn))
        pp = mm_cols_fwd(f"ple_proj_{layer}", p_rows, weight("ple_w_proj", layer, pgl), F32, part=(layer, N_LAYERS))
        bg = ple_b_gate[layer:layer + 1]
        x2, x2_b, xh2, rs2 = ln_fwd(f"ln_ffn_{layer}", x1, ffn, ln_ffn_g[layer:layer + 1], ln_ffn_b[layer:layer + 1],
                                    ple=(pgl, pp, bg), emit_y=layer < N_LAYERS - 1)
        sv.update(x1=x1_b, xh1=xh1, rs1=rs1, gv=gv, hid=hid, pgl=pgl, pp=pp, xh2=xh2, rs2=rs2)
        saved.append(sv)
        h_in, h_in_b = x2, x2_b

    reducers = []

    def open_group(tag, group):
        reducers.append(Reducer(tag, group))
        return reducers[-1].advance(None)

    def hook(after):
        token = None
        for red in reducers:
            if red.stage < 4:
                tk = red.advance(after)
                if tk is not None:
                    token = tk if token is None else token + tk
        return token

    def tied(a, token):
        return a if token is None else tie(a, token)

    parts = []
    token = None
    for layer in reversed(range(N_LAYERS)):
        sv = saved[layer]
        bg = ple_b_gate[layer:layer + 1]
        if layer == 0:
            token = open_group("layer1", [("attn_w_qkv", 0), ("attn_w_o", 0), ("ffn_w_up", 1), ("ffn_w_down", 1),
                                          ("ple_w_gate", 1), ("ple_w_proj", 1)])
        last = layer == N_LAYERS - 1
        res = ln_bwd(
            f"ln_ffn_bwd_{layer}", parts, sv["xh2"], sv["rs2"], tied(ln_ffn_g[layer:layer + 1], token),
            ple=(sv["pgl"], sv["pp"], bg), loss=(target, ln_ffn_b[layer:layer + 1]) if last else None)
        dz2, dg2, db2, dpp, dpgl, dbg = res[:6]
        if last:
            loss_part = res[6]
        small_grads[("ln_ffn_g", layer)] = dg2
        small_grads[("ln_ffn_b", layer)] = db2
        small_grads[("ple_b_gate", layer)] = dbg
        w_down = weight("ffn_w_down", layer)
        dhid = mm_rows_dx(f"ffn_down_dx_{layer}", dz2, w_down)
        big_grads[("ffn_w_down", layer)] = mm_rows_dw(f"ffn_down_dw_{layer}", sv["hid"], dz2)
        token = hook(big_grads[("ffn_w_down", layer)])
        dgv, ddw, ddb = ffn_act_bwd(f"ffn_act_bwd_{layer}", dhid, sv["gv"], ffn_dw_full[layer],
                                    tied(ffn_dw_b[layer:layer + 1], token))
        small_grads[("ffn_dw_w", layer)] = ddw
        small_grads[("ffn_dw_b", layer)] = ddb
        big_grads[("ffn_w_up", layer)] = mm_cols_dw(f"ffn_up_dw_{layer}", sv["x1"], dgv)
        t_up = mm_cols_dx(f"ffn_up_dx_{layer}", dgv, weight("ffn_w_up", layer))
        token = hook(t_up)
        big_grads[("ple_w_gate", layer)] = mm_rows_dw(f"ple_gate_dw_{layer}", sv["x1"], dpgl)
        t_gate = mm_rows_dx(f"ple_gate_dx_{layer}", dpgl, weight("ple_w_gate", layer))
        big_grads[("ple_w_proj", layer)] = mm_cols_dw(f"ple_proj_dw_{layer}", p_rows, dpp, part=(layer, N_LAYERS))
        token2 = hook(big_grads[("ple_w_proj", layer)])
        if token2 is not None:
            token = token2 if token is None else token + token2
        if layer == 0:
            token3 = open_group("layer0_ffn", [("ffn_w_up", 0), ("ffn_w_down", 0), ("ple_w_gate", 0), ("ple_w_proj", 0)])
            token = token3 if token is None else token + token3
        dz1, dg1, db1 = ln_bwd(
            f"ln_mix_bwd_{layer}", [(ALPHA, dz2), (1.0, t_up), (1.0, t_gate)], sv["xh1"], sv["rs1"],
            tied(ln_mix_g[layer:layer + 1], token))
        small_grads[("ln_mix_g", layer)] = dg1
        small_grads[("ln_mix_b", layer)] = db1
        if layer % 2 == 0:
            dcat = mm_rows_dx("mix_out_dx", dz1, weight("mix_w_out", 0))
            big_grads[("mix_w_out", 0)] = mm_rows_dw("mix_out_dw", sv["cat"], dz1)
            token = hook(big_grads[("mix_w_out", 0)])
            du, dpw, dps, dcw, dcb, dcg, dcbt = mixer_bwd(
                "mixer_bwd", dcat, sv["u"], sv["d"], sv["e"], sv["glu"], sv["hh"], sv["rs"],
                pool_w[0], pool_scale, conv_w_full, tied(conv_ln_g, token), conv_ln_b)
            small_grads[("pool_w", 0)] = dpw
            small_grads[("pool_scale", 0)] = dps
            small_grads[("conv_dw_w", 0)] = dcw
            small_grads[("conv_dw_b", 0)] = dcb
            small_grads[("conv_ln_g", 0)] = dcg
            small_grads[("conv_ln_b", 0)] = dcbt
            big_grads[("mix_w_in", 0)] = mm_cols_dw("mix_in_dw", sv["x_in"], du)
            hook(big_grads[("mix_w_in", 0)])
            open_group("layer0_mix", [("mix_w_in", 0), ("mix_w_out", 0)])
            dx_in = mm_cols_dx("mix_in_dx", du, weight("mix_w_in", 0), addend=(ALPHA, dz1))
            token = hook(dx_in)
        else:
            do = mm_rows_dx("attn_out_dx", dz1, weight("attn_w_o", 0), out_dtype=BF16)
            big_grads[("attn_w_o", 0)] = mm_rows_dw("attn_out_dw", sv["att"], dz1)
            dq, dk, dv, ds_sum = attn_bwd("attn_bwd", sv["qkvp"], sv["bias"], do)
            cols, sat = bias_grad_reduce("bias_grad", ds_sum)
            d_rel = jnp.concatenate(
                [jnp.zeros((N_HEADS, 1), F32),
                 jnp.flip(cols[:, 0, Q_TILE + SHEAR_SAT:Q_TILE - 1 + SHEAR_W], axis=1),
                 sat[:, 0, 0:1]], axis=1)
            small_grads[("attn_rel_bias", 0)] = d_rel
            dqkv = jnp.concatenate([dq, dk, dv], axis=1)
            big_grads[("attn_w_qkv", 0)] = mm_cols_dw("attn_qkv_dw", sv["x_in"], dqkv)
            dx_in = mm_cols_dx("attn_qkv_dx", dqkv, weight("attn_w_qkv", 0), addend=(ALPHA, dz1))
        parts = [(1.0, dx_in)]
    grad_x = dx_in

    small = [
        ("pool_w", pool_w, m_pool_w, v_pool_w, None),
        ("pool_scale", pool_scale, m_pool_scale, v_pool_scale, None),
        ("conv_dw_w", conv_dw_w, m_conv_dw_w, v_conv_dw_w, 2),
        ("conv_dw_b", conv_dw_b, m_conv_dw_b, v_conv_dw_b, None),
        ("conv_ln_g", conv_ln_g, m_conv_ln_g, v_conv_ln_g, None),
        ("conv_ln_b", conv_ln_b, m_conv_ln_b, v_conv_ln_b, None),
        ("attn_rel_bias", attn_rel_bias, m_attn_rel_bias, v_attn_rel_bias, None),
        ("ln_mix_g", ln_mix_g, m_ln_mix_g, v_ln_mix_g, None),
        ("ln_mix_b", ln_mix_b, m_ln_mix_b, v_ln_mix_b, None),
        ("ffn_dw_w", ffn_dw_w, m_ffn_dw_w, v_ffn_dw_w, 2),
        ("ffn_dw_b", ffn_dw_b, m_ffn_dw_b, v_ffn_dw_b, None),
        ("ple_b_gate", ple_b_gate, m_ple_b_gate, v_ple_b_gate, None),
        ("ln_ffn_g", ln_ffn_g, m_ln_ffn_g, v_ln_ffn_g, None),
        ("ln_ffn_b", ln_ffn_b, m_ln_ffn_b, v_ln_ffn_b, None),
    ]
    full_grads = []
    for nm, w, _, _, shard_axis in small:
        full = list(w.shape)
        if shard_axis is not None:
            full[shard_axis] *= N_SHARD
        per_layer = [small_grads[(nm, layer)].reshape((1,) + tuple(full[1:])) for layer in range(w.shape[0])]
        full_grads.append(jnp.concatenate(per_layer, axis=0))
    packed = _pack(full_grads + [loss_part])
    dev_arr = (4 * xi + 2 * yi + ci).astype(jnp.int32).reshape(1)
    sg_block = cast_into_gathered("place_small_grads", packed[None], 0, dev_arr, n_blocks=8, dtype=F32)
    sg_send, sg_recv, sg_bufs, sg_token = copies_start("small_grads_start", [sg_block], all_plan, 7)
    token = sg_token if token is None else token + sg_token

    shard_grads = {}
    for red in reducers:
        if red.stage == 4:
            shard_grads.update(red.result)
    big_out = {}

    def update_big(names, tok):
        for nm, w, m, v, _ in big:
            if nm in names:
                gl = [shard_grads[(nm, layer)] for layer in range(w.shape[0])]
                delta, new_m, new_v = adamw(f"adamw_{nm}", w, gl, m, v, token=tok)
                big_out[nm] = (jnp.stack(gl, axis=0), delta, new_m, new_v)

    last_group = ("mix_w_in", "mix_w_out")
    update_big([nm for nm, _, _, _, _ in big if nm not in last_group], token)
    token = hook(big_out["ffn_w_up"][1])

    gathered_sg = copies_wait("small_grads_wait", sg_bufs, sg_send, sg_recv, all_plan, 0, big_out["ffn_w_down"][1])[0]
    total = sum_blocks("sum_small", gathered_sg.reshape(8 * packed.shape[0], LANES), 8)
    unpacked = _unpack(total, [g.shape for g in full_grads] + [loss_part.shape])
    loss = unpacked[-1][0, 0]
    local_grads = []
    for (nm, w, _, _, shard_axis), g in zip(small, unpacked[:-1]):
        if shard_axis is not None:
            width = w.shape[shard_axis]
            g = lax.dynamic_slice_in_dim(g, shard_idx * width, width, axis=shard_axis)
        local_grads.append(g.reshape(w.shape))
    updated = adamw_many("adamw_small", [w for _, w, _, _, _ in small], local_grads,
                         [m for _, _, m, _, _ in small], [v for _, _, _, v, _ in small], token)
    hook(updated[0][0])
    for red in reducers:
        shard_grads.update(red.result)
    update_big(last_group, None)
    small_out = {}
    for (nm, _, _, _, _), g, (d_, m_, v_) in zip(small, local_grads, updated):
        small_out[nm] = (g, d_, m_, v_)

    order = ["mix_w_in", "pool_w", "pool_scale", "conv_dw_w", "conv_dw_b", "conv_ln_g", "conv_ln_b", "mix_w_out",
             "attn_w_qkv", "attn_rel_bias", "attn_w_o", "ln_mix_g", "ln_mix_b", "ffn_w_up", "ffn_dw_w", "ffn_dw_b",
             "ffn_w_down", "ple_w_proj", "ple_w_gate", "ple_b_gate", "ln_ffn_g", "ln_ffn_b"]
    res = {**big_out, **small_out}
    outs = [loss, grad_x[None]]
    for slot in range(4):
        outs += [res[nm][slot] for nm in order]
    return tuple(outs)
```

```python
import math

import jax
import jax.numpy as jnp
from jax import lax
from jax.experimental import pallas as pl
from jax.experimental.pallas import tpu as pltpu

F32 = jnp.float32
BF16 = jnp.bfloat16
MESH = pl.DeviceIdType.MESH

N_LAYERS = 2
ALPHA = (2 * N_LAYERS) ** 0.25
LN_EPS = 1e-5
NEG_INF = -1e30
CHUNK = 64
LEFT_CHUNKS = 8
PAD_ROWS = LEFT_CHUNKS * CHUNK
HEAD_DIM = 64
ATTN_SCALE = HEAD_DIM ** -0.5
N_HEADS = 16
MAX_REL = 256
POOL_WINDOWS = (2, 4, 8, 16)
POOL_GROUP = 128
CONV_K = 31
FFN_K = 3
CONV_HALO = 32
FFN_HALO = 8
FFN_TILE = 256
FFN_CHUNK_ROWS = 64
FFN_CHUNK_LANES = 128
Q_TILE = 256
K_WIN = Q_TILE + PAD_ROWS
LANES = 128
SUBLANES = 8
ATTN_PAIRS = 2
ATTN_PAIRS_FWD = 4
ATTN_LANES = ATTN_PAIRS * LANES
SHEAR_W = Q_TILE + K_WIN
SHEAR_SAT = SHEAR_W - 2 * MAX_REL
N_SHARD = 4

ADAM_LR = 0.001
ADAM_B1 = 0.9
ADAM_B2 = 0.999
ADAM_EPS = 1e-08
ADAM_WD = 0.01
ADAM_STEP = 10
ADAM_BC1 = 1.0 - ADAM_B1 ** ADAM_STEP
ADAM_BC2 = 1.0 - ADAM_B2 ** ADAM_STEP

DIMS = {
    "nn": (((1,), (0,)), ((), ())),
    "nt": (((1,), (1,)), ((), ())),
    "tn": (((0,), (0,)), ((), ())),
}


def _cp(vmem_mb=48, **kw):
    return pltpu.CompilerParams(vmem_limit_bytes=vmem_mb * 1024 * 1024, **kw)


def _in_hbm(a):
    return pltpu.with_memory_space_constraint(a, pltpu.HBM)


STAGING_LIMIT_BYTES = 1 << 20
SMALL_WEIGHT_BYTES = 1 << 22
SUM_BLOCK_ROWS = 2048
SMALL_BLOCK_BYTES = 1 << 19


def _call(body, **kw):
    call = pl.pallas_call(body, **kw)

    def run(*args):
        pinned = []
        for a in args:
            big = a.size * a.dtype.itemsize >= STAGING_LIMIT_BYTES
            pinned.append(_in_hbm(a) if big and not jnp.issubdtype(a.dtype, jnp.integer) else a)
        return call(*pinned)

    return run


def _dot(a, b, mode):
    return lax.dot_general(a.astype(BF16), b.astype(BF16), DIMS[mode], preferred_element_type=F32)


def _sig(x):
    return 1.0 / (1.0 + jnp.exp(-x))


def _row_tile(s):
    return min(512, s // 4)


def _mm_tile(s):
    return min(1024, s // 4)


def _mm(name, mode, a, b, in_specs, out_shape, out_spec, acc_shape, grid, nk, zero_first=False, vmem_mb=48,
        addend=None):
    out_f32 = out_shape.dtype == F32

    def body(a_ref, b_ref, *rest):
        k = pl.program_id(2)
        if addend is None:
            o_ref, scr = rest[0], rest[1:]
        else:
            add_ref, o_ref, scr = rest[0], rest[1], rest[2:]

        def compute():
            part = _dot(a_ref[...], b_ref[...], mode)
            if nk == 1:
                if addend is not None:
                    part = part + addend[0] * add_ref[...]
                o_ref[...] = part.astype(o_ref.dtype)
                return
            acc = o_ref if out_f32 else scr[0]

            @pl.when(k == 0)
            def _():
                acc[...] = jnp.zeros(acc.shape, F32) if addend is None else addend[0] * add_ref[...]

            acc[...] += part

            if not out_f32:
                @pl.when(k == nk - 1)
                def _():
                    o_ref[...] = acc[...].astype(o_ref.dtype)

        if zero_first:
            @pl.when(pl.program_id(1) == 0)
            def _():
                o_ref[...] = jnp.zeros(o_ref.shape, o_ref.dtype)

            pl.when(pl.program_id(1) > 0)(compute)
        else:
            compute()

    scratch = [] if (nk == 1 or out_f32) else [pltpu.VMEM(acc_shape, F32)]
    operands = [a, b] if addend is None else [a, b, addend[1]]
    specs = list(in_specs) if addend is None else list(in_specs) + [out_spec]
    return _call(
        body, name=name, grid=grid, in_specs=specs, out_specs=out_spec, out_shape=out_shape,
        scratch_shapes=scratch, compiler_params=_cp(vmem_mb),
    )(*operands)


def _is_small_weight(wc):
    return wc.size * 2 <= SMALL_WEIGHT_BYTES


def _all_shards(w_ref):
    return jnp.concatenate([w_ref[j] for j in range(N_SHARD)], axis=1)


def mm_cols_fwd(name, a, wc, out_dtype, pad_blocks=0, part=(0, 1)):
    s, k = a.shape
    s //= part[1]
    n4 = wc.shape[2]
    tm = _row_tile(s) if pad_blocks else _mm_tile(s)
    nt = s // tm
    first_block = part[0] * nt
    if _is_small_weight(wc) and not pad_blocks:
        def body(a_ref, w_ref, o_ref):
            o_ref[...] = _dot(a_ref[...], _all_shards(w_ref), "nn").astype(o_ref.dtype)

        return _call(
            body, name=name, grid=(nt,),
            in_specs=[pl.BlockSpec((tm, k), lambda i: (first_block + i, 0)), _full(wc.shape)],
            out_specs=pl.BlockSpec((tm, N_SHARD * n4), lambda i: (i, 0)),
            out_shape=jax.ShapeDtypeStruct((s, N_SHARD * n4), out_dtype), compiler_params=_cp(),
        )(a, wc)
    return _mm(
        name, "nn", a, wc,
        [pl.BlockSpec((tm, k), lambda j, i, r: (first_block + jnp.maximum(i - pad_blocks, 0), 0)),
         pl.BlockSpec((None, k, n4), lambda j, i, r: (j, 0, 0))],
        jax.ShapeDtypeStruct((s + pad_blocks * tm, N_SHARD * n4), out_dtype),
        pl.BlockSpec((tm, n4), lambda j, i, r: (i, j)),
        None, (N_SHARD, nt + pad_blocks, 1), 1, zero_first=pad_blocks > 0)


def mm_cols_dx(name, dy, wc, addend=None):
    s = dy.shape[0]
    _, k, n4 = wc.shape
    tm = _mm_tile(s)
    if _is_small_weight(wc):
        def body(dy_ref, w_ref, *rest):
            part = _dot(dy_ref[...], _all_shards(w_ref), "nt")
            rest[-1][...] = part if addend is None else part + addend[0] * rest[0][...]

        out_spec = pl.BlockSpec((tm, k), lambda i: (i, 0))
        extra, extra_specs = ([], []) if addend is None else ([addend[1]], [out_spec])
        return _call(
            body, name=name, grid=(s // tm,),
            in_specs=[pl.BlockSpec((tm, N_SHARD * n4), lambda i: (i, 0)), _full(wc.shape)] + extra_specs,
            out_specs=out_spec, out_shape=jax.ShapeDtypeStruct((s, k), F32), compiler_params=_cp(),
        )(dy, wc, *extra)
    return _mm(
        name, "nt", dy, wc,
        [pl.BlockSpec((tm, n4), lambda g, i, r: (i, r)),
         pl.BlockSpec((None, k, n4), lambda g, i, r: (r, 0, 0))],
        jax.ShapeDtypeStruct((s, k), F32),
        pl.BlockSpec((tm, k), lambda g, i, r: (i, 0)),
        (tm, k), (1, s // tm, N_SHARD), N_SHARD, addend=addend)


def mm_cols_dw(name, a, dy, part=(0, 1)):
    s, k = a.shape
    s //= part[1]
    n4 = dy.shape[1] // N_SHARD
    tm = _mm_tile(s)
    nt = s // tm
    first_block = part[0] * nt
    if k * n4 * N_SHARD * 2 <= SMALL_WEIGHT_BYTES:
        def body(a_ref, dy_ref, o_ref):
            @pl.when(pl.program_id(0) == 0)
            def _():
                o_ref[...] = jnp.zeros(o_ref.shape, F32)

            full = _dot(a_ref[...], dy_ref[...], "tn")
            for j in range(N_SHARD):
                o_ref[j] += full[:, j * n4:(j + 1) * n4]

        return _call(
            body, name=name, grid=(nt,),
            in_specs=[pl.BlockSpec((tm, k), lambda r: (first_block + r, 0)),
                      pl.BlockSpec((tm, N_SHARD * n4), lambda r: (r, 0))],
            out_specs=_full((N_SHARD, k, n4)),
            out_shape=jax.ShapeDtypeStruct((N_SHARD, k, n4), F32), compiler_params=_cp(),
        )(a, dy)
    return _mm(
        name, "tn", a, dy,
        [pl.BlockSpec((tm, k), lambda j, g, r: (first_block + r, 0)),
         pl.BlockSpec((tm, n4), lambda j, g, r: (r, j))],
        jax.ShapeDtypeStruct((N_SHARD, k, n4), F32),
        pl.BlockSpec((None, k, n4), lambda j, g, r: (j, 0, 0)),
        (k, n4), (N_SHARD, 1, nt), nt)


def _k_tile(k):
    return k if k <= 1024 else k // 2


def mm_rows_fwd(name, a, wr, out_dtype=F32):
    s, k = a.shape
    n = wr.shape[1]
    tm = _mm_tile(s)
    tk = _k_tile(k)
    nk = k // tk
    return _mm(
        name, "nn", a, wr,
        [pl.BlockSpec((tm, tk), lambda g, i, r: (i, r)),
         pl.BlockSpec((tk, n), lambda g, i, r: (r, 0))],
        jax.ShapeDtypeStruct((s, n), out_dtype),
        pl.BlockSpec((tm, n), lambda g, i, r: (i, 0)),
        (tm, n), (1, s // tm, nk), nk)


def mm_rows_dx(name, dy, wr, out_dtype=F32):
    s, n = dy.shape
    k = wr.shape[0]
    tm = _mm_tile(s)
    tk = _k_tile(k)
    return _mm(
        name, "nt", dy, wr,
        [pl.BlockSpec((tm, n), lambda j, i, r: (i, 0)),
         pl.BlockSpec((tk, n), lambda j, i, r: (j, 0))],
        jax.ShapeDtypeStruct((s, k), out_dtype),
        pl.BlockSpec((tm, tk), lambda j, i, r: (i, j)),
        None, (k // tk, s // tm, 1), 1)


def mm_rows_dw(name, a, dy):
    s, k = a.shape
    n = dy.shape[1]
    tm = _mm_tile(s)
    tk = _k_tile(k)
    nt = s // tm
    return _mm(
        name, "tn", a, dy,
        [pl.BlockSpec((tm, tk), lambda j, g, r: (r, j)),
         pl.BlockSpec((tm, n), lambda j, g, r: (r, 0))],
        jax.ShapeDtypeStruct((k, n), F32),
        pl.BlockSpec((tk, n), lambda j, g, r: (j, 0)),
        (tk, n), (k // tk, 1, nt), nt)


def _row(tm, c, col=0):
    return pl.BlockSpec((tm, c), lambda i: (i, col))


def _full(shape):
    nd = len(shape)
    return pl.BlockSpec(shape, lambda i: (0,) * nd)


def _prev(tm, h, c, col=0):
    return pl.BlockSpec((h, c), lambda i: (jnp.maximum(i * (tm // h) - 1, 0), col))


def _next(tm, h, c, s, col=0):
    return pl.BlockSpec((h, c), lambda i: (jnp.minimum((i + 1) * (tm // h), s // h - 1), col))


def _acc_add(ref, first, val):
    @pl.when(first)
    def _():
        ref[...] = val

    @pl.when(jnp.logical_not(first))
    def _():
        ref[...] += val


def _colsum(v):
    return jnp.sum(v, axis=0, keepdims=True)


def _ln_stats(z):
    mu = jnp.mean(z, axis=-1, keepdims=True)
    zc = z - mu
    var = jnp.mean(zc * zc, axis=-1, keepdims=True)
    rstd = lax.rsqrt(var + LN_EPS)
    return zc * rstd, rstd


def _ln_bwd(dxhat, xhat, rstd):
    m1 = jnp.mean(dxhat, axis=-1, keepdims=True)
    m2 = jnp.mean(dxhat * xhat, axis=-1, keepdims=True)
    return rstd * (dxhat - m1 - xhat * m2)


def ln_fwd(name, x, f, g, b, ple=None, emit_y=True):
    s, d = x.shape
    tm = _row_tile(s)
    n_in = 2 + (3 if ple is not None else 0)

    def body(*refs):
        x_ref, f_ref = refs[0], refs[1]
        g_ref, b_ref = refs[n_in], refs[n_in + 1]
        xh_ref, rs_ref = refs[-2:]
        z = ALPHA * x_ref[...] + f_ref[...]
        if ple is not None:
            pgl_ref, pp_ref, bg_ref = refs[2:5]
            z = z + _sig(pgl_ref[...] + bg_ref[...]) * pp_ref[...]
        xhat, rstd = _ln_stats(z)
        if emit_y:
            y = xhat * g_ref[...] + b_ref[...]
            refs[n_in + 2][...] = y
            refs[n_in + 3][...] = y.astype(BF16)
        xh_ref[...] = xhat
        rs_ref[...] = jnp.broadcast_to(rstd, rs_ref.shape)

    ins = [x, f]
    specs = [_row(tm, d), _row(tm, d)]
    if ple is not None:
        pgl, pp, bg = ple
        ins += [pgl, pp, bg]
        specs += [_row(tm, d), _row(tm, d), _full((1, d))]
    ins += [g, b]
    specs += [_full((1, d)), _full((1, d))]
    y_shapes = [jax.ShapeDtypeStruct((s, d), F32), jax.ShapeDtypeStruct((s, d), BF16)] if emit_y else []
    outs = _call(
        body, name=name, grid=(s // tm,), in_specs=specs,
        out_specs=[_row(tm, d)] * (len(y_shapes) + 1) + [_row(tm, LANES)],
        out_shape=y_shapes + [jax.ShapeDtypeStruct((s, d), F32), jax.ShapeDtypeStruct((s, LANES), F32)],
        compiler_params=_cp(),
    )(*ins)
    return tuple(outs) if emit_y else (None, None, outs[0], outs[1])


def ln_bwd(name, parts, xhat, rstd, g, ple=None, loss=None):
    s, d = xhat.shape
    tm = _row_tile(s)
    coefs = [c for c, _ in parts]
    n_p = len(parts)
    n_ple = 3 if ple is not None else 0
    n_in = n_p + 3 + n_ple + (2 if loss is not None else 0)

    def body(*refs):
        first = pl.program_id(0) == 0
        xh = refs[n_p][...]
        rs = refs[n_p + 1][:, 0:1]
        g_v = refs[n_p + 2][...]
        outs = refs[n_in:]
        if loss is not None:
            t_ref, b_ref = refs[n_p + 3 + n_ple:n_p + 5 + n_ple]
            err = (xh * g_v + b_ref[...]) - t_ref[...]
            dy = err * (1.0 / d)
            part = 0.5 * jnp.sum(jnp.mean(err * err, axis=-1, keepdims=True), axis=0, keepdims=True)
            _acc_add(outs[-1], first, jnp.broadcast_to(part, outs[-1].shape))
        else:
            dy = coefs[0] * refs[0][...].astype(F32)
            for j in range(1, n_p):
                dy = dy + coefs[j] * refs[j][...].astype(F32)
        dz = _ln_bwd(dy * g_v, xh, rs)
        outs[0][...] = dz
        _acc_add(outs[1], first, _colsum(dy * xh))
        _acc_add(outs[2], first, _colsum(dy))
        if ple is not None:
            pgl_ref, pp_ref, bg_ref = refs[n_p + 3:n_p + 6]
            pg = _sig(pgl_ref[...] + bg_ref[...])
            dpgl = dz * pp_ref[...] * pg * (1.0 - pg)
            outs[3][...] = (dz * pg).astype(BF16)
            outs[4][...] = dpgl.astype(BF16)
            _acc_add(outs[5], first, _colsum(dpgl))

    ins = [p for _, p in parts] + [xhat, rstd, g]
    specs = [_row(tm, d)] * n_p + [_row(tm, d), _row(tm, LANES), _full((1, d))]
    out_specs = [_row(tm, d), _full((1, d)), _full((1, d))]
    out_shape = [jax.ShapeDtypeStruct((s, d), F32), jax.ShapeDtypeStruct((1, d), F32),
                 jax.ShapeDtypeStruct((1, d), F32)]
    if ple is not None:
        pgl, pp, bg = ple
        ins += [pgl, pp, bg]
        specs += [_row(tm, d), _row(tm, d), _full((1, d))]
        out_specs += [_row(tm, d), _row(tm, d), _full((1, d))]
        out_shape += [jax.ShapeDtypeStruct((s, d), BF16), jax.ShapeDtypeStruct((s, d), BF16),
                      jax.ShapeDtypeStruct((1, d), F32)]
    if loss is not None:
        target, b = loss
        ins += [target, b]
        specs += [_row(tm, d), _full((1, d))]
        out_specs += [_full((8, LANES))]
        out_shape += [jax.ShapeDtypeStruct((8, LANES), F32)]
    return _call(
        body, name=name, grid=(s // tm,), in_specs=specs, out_specs=out_specs, out_shape=out_shape,
        compiler_params=_cp(),
    )(*ins)


def _fill_rotations(rot_ref, x, direction):
    n = x.shape[0]
    rot_ref[0] = x
    for b in range(1, SUBLANES):
        if direction < 0:
            rot_ref[b, SUBLANES:n, :] = x[SUBLANES - b:n - b]
        else:
            rot_ref[b, 0:n - SUBLANES, :] = x[b:n - SUBLANES + b]


def _rotated(rot_ref, start, rows, cs, direction=-1):
    b = (-start) % SUBLANES if direction < 0 else start % SUBLANES
    aligned = start + b if direction < 0 else start - b
    return rot_ref[b, pl.ds(aligned, rows), cs]


def _tile_pos(i, tm, rows):
    return (i * tm + lax.broadcasted_iota(jnp.int32, (rows, 1), 0) + 1).astype(F32)


def mixer_fwd(name, u, pool_w, pool_scale, conv_w, conv_b, cn_g, cn_b):
    s = u.shape[0]
    dp = 512
    tm = min(256, s // 4)
    h = CONV_HALO

    def body(a_c, a_p, bv_c, bv_p, bg_c, bg_p, pw_ref, ps_ref, cw_ref, cb_ref, cg_ref, cbt_ref,
             cat_ref, d_ref, e_ref, glu_ref, hh_ref, rs_ref, ext_a, rot_g, conv_out):
        i = pl.program_id(0)
        first = i == 0
        ext_a[0:h, :] = jnp.where(first, 0.0, a_p[...])
        ext_a[h:, :] = a_c[...]
        glu = bv_c[...] * _sig(bg_c[...])
        glu_ref[...] = glu
        _fill_rotations(rot_g, jnp.concatenate([jnp.where(first, 0.0, bv_p[...] * _sig(bg_p[...])), glu], axis=0), -1)
        pos = _tile_pos(i, tm, tm)
        for gi, w in enumerate(POOL_WINDOWS):
            cs = slice(gi * POOL_GROUP, (gi + 1) * POOL_GROUP)
            a_g = ext_a[pl.ds(h, tm), cs]
            acc = a_g
            for sh in range(1, w):
                acc = acc + ext_a[pl.ds(h - sh, tm), cs]
            d_g = acc / jnp.minimum(pos, float(w)) - a_g
            d_ref[:, cs] = d_g.astype(BF16)
            e_g = _dot(d_g, pw_ref[gi], "nn")
            e_ref[:, cs] = e_g
            cat_ref[:, cs] = (e_g * ps_ref[:, cs]).astype(BF16)
        for lg in range(dp // LANES):
            cs = slice(lg * LANES, (lg + 1) * LANES)
            acc = jnp.broadcast_to(cb_ref[:, cs], (tm, LANES))
            for sh in range(CONV_K):
                acc = acc + _rotated(rot_g, h - sh, tm, cs) * cw_ref[pl.ds(CONV_K - 1 - sh, 1), cs]
            conv_out[:, cs] = acc
        hhat, rstd = _ln_stats(conv_out[...])
        hl = hhat * cg_ref[...] + cbt_ref[...]
        cat_ref[:, dp:] = (hl * _sig(hl)).astype(BF16)
        hh_ref[...] = hhat
        rs_ref[...] = jnp.broadcast_to(rstd, rs_ref.shape)

    specs = [_row(tm, dp, 0), _prev(tm, h, dp, 0), _row(tm, dp, 1), _prev(tm, h, dp, 1),
             _row(tm, dp, 2), _prev(tm, h, dp, 2),
             _full((4, POOL_GROUP, POOL_GROUP)), _full((1, dp)), _full((CONV_K, dp)),
             _full((1, dp)), _full((1, dp)), _full((1, dp))]
    out_specs = [_row(tm, 2 * dp), _row(tm, dp), _row(tm, dp), _row(tm, dp), _row(tm, dp), _row(tm, LANES)]
    out_shape = [jax.ShapeDtypeStruct((s, 2 * dp), BF16), jax.ShapeDtypeStruct((s, dp), BF16),
                 jax.ShapeDtypeStruct((s, dp), F32), jax.ShapeDtypeStruct((s, dp), F32),
                 jax.ShapeDtypeStruct((s, dp), F32), jax.ShapeDtypeStruct((s, LANES), F32)]
    return _call(
        body, name=name, grid=(s // tm,), in_specs=specs, out_specs=out_specs, out_shape=out_shape,
        scratch_shapes=[pltpu.VMEM((h + tm, dp), F32), pltpu.VMEM((SUBLANES, h + tm, dp), F32),
                        pltpu.VMEM((tm, dp), F32)],
        compiler_params=_cp(),
    )(u, u, u, u, u, u, pool_w, pool_scale, conv_w, conv_b, cn_g, cn_b)


def mixer_bwd(name, dcat, u, d_sv, e_sv, glu_sv, hh_sv, rs_sv, pool_w, pool_scale, conv_w, cn_g, cn_b):
    s = u.shape[0]
    dp = 512
    tm = min(256, s // 4)
    h = CONV_HALO
    nt = s // tm

    def body(dc_c, dc_n, bv_c, bg_c, d_c, e_c, gl_c, gl_p, hh_c, hh_n, rs_c, rs_n,
             pw_ref, ps_ref, cw_ref, cg_ref, cbt_ref,
             du_ref, dpw_ref, dps_ref, dcw_ref, dcb_ref, dcg_ref, dcbt_ref,
             ext_dh, ext_g, ext_r):
        i = pl.program_id(0)
        first = i == 0
        last = i == nt - 1
        cg = cg_ref[...]

        def conv_grads(dyb, hhat, rstd):
            hl = hhat * cg + cbt_ref[...]
            sg = _sig(hl)
            dhl = dyb * (sg * (1.0 + hl * (1.0 - sg)))
            return _ln_bwd(dhl * cg, hhat, rstd), dhl

        hh_cur = hh_c[...]
        dh_c, dhl_c = conv_grads(dc_c[:, dp:], hh_cur, rs_c[:, 0:1])
        dh_n, _ = conv_grads(dc_n[:, dp:], hh_n[...], rs_n[:, 0:1])
        _fill_rotations(ext_dh, jnp.concatenate([dh_c, jnp.where(last, 0.0, dh_n)], axis=0), 1)
        _fill_rotations(ext_g, jnp.concatenate([jnp.where(first, 0.0, gl_p[...]), gl_c[...]], axis=0), -1)

        @pl.when(first)
        def _():
            dcw_ref[...] = jnp.zeros(dcw_ref.shape, F32)

        for lg in range(dp // LANES):
            cs = slice(lg * LANES, (lg + 1) * LANES)
            dglu = jnp.zeros((tm, LANES), F32)
            for sh in range(CONV_K):
                dglu = dglu + _rotated(ext_dh, sh, tm, cs, 1) * cw_ref[pl.ds(CONV_K - 1 - sh, 1), cs]
            dh_g = ext_dh[0, pl.ds(0, tm), cs]
            for sh in range(CONV_K):
                dcw_ref[pl.ds(CONV_K - 1 - sh, 1), cs] += _colsum(dh_g * _rotated(ext_g, h - sh, tm, cs))
            sgate = _sig(bg_c[:, cs])
            du_ref[:, dp + lg * LANES:dp + (lg + 1) * LANES] = dglu * sgate
            du_ref[:, 2 * dp + lg * LANES:2 * dp + (lg + 1) * LANES] = dglu * bv_c[:, cs] * sgate * (1.0 - sgate)
        _acc_add(dcb_ref, first, _colsum(dh_c))
        _acc_add(dcg_ref, first, _colsum(dhl_c * hh_cur))
        _acc_add(dcbt_ref, first, _colsum(dhl_c))

        pos_c = _tile_pos(i, tm, tm)
        pos_n = _tile_pos(i + 1, tm, h)
        _acc_add(dps_ref, first, _colsum(dc_c[:, :dp] * e_c[...]))
        for gi, w in enumerate(POOL_WINDOWS):
            cs = slice(gi * POOL_GROUP, (gi + 1) * POOL_GROUP)
            pw = pw_ref[gi]
            de_c = dc_c[:, cs] * ps_ref[:, cs]
            de_n = dc_n[:, cs] * ps_ref[:, cs]
            dd_c = _dot(de_c, pw, "nt")
            dd_n = _dot(de_n, pw, "nt")
            ext_r[0:tm, :] = dd_c / jnp.minimum(pos_c, float(w))
            ext_r[tm:, :] = jnp.where(last, 0.0, dd_n / jnp.minimum(pos_n, float(w)))
            acc = -dd_c
            for sh in range(w):
                acc = acc + ext_r[pl.ds(sh, tm), :]
            du_ref[:, cs] = acc
            dpw_g = _dot(d_c[:, cs], de_c, "tn")

            @pl.when(first)
            def _():
                dpw_ref[gi] = dpw_g

            @pl.when(jnp.logical_not(first))
            def _():
                dpw_ref[gi] += dpw_g

    specs = [_row(tm, 2 * dp), _next(tm, h, 2 * dp, s), _row(tm, dp, 1), _row(tm, dp, 2),
             _row(tm, dp), _row(tm, dp), _row(tm, dp), _prev(tm, h, dp),
             _row(tm, dp), _next(tm, h, dp, s), _row(tm, LANES), _next(tm, h, LANES, s),
             _full((4, POOL_GROUP, POOL_GROUP)), _full((1, dp)), _full((CONV_K, dp)),
             _full((1, dp)), _full((1, dp))]
    out_specs = [_row(tm, 3 * dp), _full((4, POOL_GROUP, POOL_GROUP)), _full((1, dp)), _full((CONV_K, dp)),
                 _full((1, dp)), _full((1, dp)), _full((1, dp))]
    out_shape = [jax.ShapeDtypeStruct((s, 3 * dp), F32),
                 jax.ShapeDtypeStruct((4, POOL_GROUP, POOL_GROUP), F32), jax.ShapeDtypeStruct((1, dp), F32),
                 jax.ShapeDtypeStruct((CONV_K, dp), F32), jax.ShapeDtypeStruct((1, dp), F32),
                 jax.ShapeDtypeStruct((1, dp), F32), jax.ShapeDtypeStruct((1, dp), F32)]
    return _call(
        body, name=name, grid=(nt,), in_specs=specs, out_specs=out_specs, out_shape=out_shape,
        scratch_shapes=[pltpu.VMEM((SUBLANES, tm + h, dp), F32), pltpu.VMEM((SUBLANES, h + tm, dp), F32),
                        pltpu.VMEM((tm + h, POOL_GROUP), F32)],
        compiler_params=_cp(),
    )(dcat, dcat, u, u, d_sv, e_sv, glu_sv, glu_sv, hh_sv, hh_sv, rs_sv, rs_sv,
      pool_w, pool_scale, conv_w, cn_g, cn_b)


GELU_C = math.sqrt(2.0 / math.pi)


def _gelu_parts(x):
    x2 = x * x
    t = jnp.tanh(x * (GELU_C + (GELU_C * 0.044715) * x2))
    half_1pt = 0.5 + 0.5 * t
    gelu = x * half_1pt
    dgelu = half_1pt + (0.5 * x) * (1.0 - t * t) * (GELU_C + (3.0 * GELU_C * 0.044715) * x2)
    return gelu, dgelu


def ffn_act_fwd(name, gv, dw_w, dw_b):
    s = gv.shape[0]
    dff = gv.shape[1] // 2
    tm = min(FFN_TILE, s // 4)
    h = FFN_HALO
    rc = FFN_CHUNK_ROWS
    lw = FFN_CHUNK_LANES

    def body(g_c, g_p, v_c, w_ref, b_ref, hid_ref):
        first = pl.program_id(0) == 0

        def chunk(ci, carry):
            r0 = pl.multiple_of(ci * rc, rc)
            above = pl.multiple_of(jnp.maximum(r0 - h, 0), h)
            for lg in range(dff // lw):
                cs = slice(lg * lw, (lg + 1) * lw)
                top = jnp.where(ci == 0, jnp.where(first, 0.0, g_p[:, cs]), g_c[pl.ds(above, h), cs])
                win = jnp.concatenate([top, g_c[pl.ds(r0, rc), cs]], axis=0)
                gc = jnp.broadcast_to(b_ref[:, cs], (rc, lw))
                for sh in range(FFN_K):
                    gc = gc + win[h - sh:h - sh + rc] * w_ref[pl.ds(FFN_K - 1 - sh, 1), cs]
                gelu, _ = _gelu_parts(gc)
                hid_ref[pl.ds(r0, rc), cs] = (gelu * v_c[pl.ds(r0, rc), cs]).astype(BF16)
            return carry

        lax.fori_loop(0, tm // rc, chunk, 0)

    return _call(
        body, name=name, grid=(s // tm,),
        in_specs=[_row(tm, dff, 0), _prev(tm, h, dff, 0), _row(tm, dff, 1), _full((FFN_K, dff)), _full((1, dff))],
        out_specs=_row(tm, dff), out_shape=jax.ShapeDtypeStruct((s, dff), BF16),
        compiler_params=_cp(),
    )(gv, gv, gv, dw_w, dw_b)


def ffn_act_bwd(name, dhid, gv, dw_w, dw_b):
    s = gv.shape[0]
    dff = gv.shape[1] // 2
    tm = min(FFN_TILE, s // 4)
    h = FFN_HALO
    nt = s // tm
    rc = FFN_CHUNK_ROWS
    lw = FFN_CHUNK_LANES
    n_chunks = tm // rc

    def body(dh_c, dh_n, g_p, g_c, g_n, v_c, v_n, w_ref, b_ref, dgv_ref, dw_ref, db_ref):
        i = pl.program_id(0)
        first = i == 0
        last = i == nt - 1

        @pl.when(first)
        def _():
            dw_ref[...] = jnp.zeros(dw_ref.shape, F32)
            db_ref[...] = jnp.zeros(db_ref.shape, F32)

        def chunk(ci, carry):
            r0 = pl.multiple_of(ci * rc, rc)
            above = pl.multiple_of(jnp.maximum(r0 - h, 0), h)
            below = pl.multiple_of(jnp.minimum(r0 + rc, tm - h), h)
            at_end = ci == n_chunks - 1
            for lg in range(dff // lw):
                cs = slice(lg * lw, (lg + 1) * lw)
                top = jnp.where(ci == 0, jnp.where(first, 0.0, g_p[:, cs]), g_c[pl.ds(above, h), cs])
                bot = jnp.where(at_end, g_n[:, cs], g_c[pl.ds(below, h), cs])
                win = jnp.concatenate([top, g_c[pl.ds(r0, rc), cs], bot], axis=0)
                shifted = [win[h - sh:h - sh + rc + h] for sh in range(FFN_K)]
                gc = jnp.broadcast_to(b_ref[:, cs], (rc + h, lw))
                for sh in range(FFN_K):
                    gc = gc + shifted[sh] * w_ref[pl.ds(FFN_K - 1 - sh, 1), cs]
                gelu, dgelu = _gelu_parts(gc)
                dh_mid = dh_c[pl.ds(r0, rc), cs]
                hv_bot = jnp.where(at_end, jnp.where(last, 0.0, dh_n[:, cs] * v_n[:, cs]),
                                   dh_c[pl.ds(below, h), cs] * v_c[pl.ds(below, h), cs])
                dgc = jnp.concatenate([dh_mid * v_c[pl.ds(r0, rc), cs], hv_bot], axis=0) * dgelu
                dgate = jnp.zeros((rc, lw), F32)
                for sh in range(FFN_K):
                    dgate = dgate + dgc[sh:sh + rc] * w_ref[pl.ds(FFN_K - 1 - sh, 1), cs]
                dgv_ref[pl.ds(r0, rc), cs] = dgate.astype(BF16)
                dgv_ref[pl.ds(r0, rc), slice(dff + lg * lw, dff + (lg + 1) * lw)] = (dh_mid * gelu[0:rc]).astype(BF16)
                dgc_mid = dgc[0:rc]
                for sh in range(FFN_K):
                    dw_ref[pl.ds(FFN_K - 1 - sh, 1), cs] += _colsum(dgc_mid * shifted[sh][0:rc])
                db_ref[:, cs] += _colsum(dgc_mid)
            return carry

        lax.fori_loop(0, n_chunks, chunk, 0)

    return _call(
        body, name=name, grid=(nt,),
        in_specs=[_row(tm, dff), _next(tm, h, dff, s),
                  _prev(tm, h, dff, 0), _row(tm, dff, 0), _next(tm, h, dff, s, 0),
                  _row(tm, dff, 1), _next(tm, h, dff, s, 1),
                  _full((FFN_K, dff)), _full((1, dff))],
        out_specs=[_row(tm, 2 * dff), _full((FFN_K, dff)), _full((1, dff))],
        out_shape=[jax.ShapeDtypeStruct((s, 2 * dff), BF16), jax.ShapeDtypeStruct((FFN_K, dff), F32),
                   jax.ShapeDtypeStruct((1, dff), F32)],
        compiler_params=_cp(),
    )(dhid, dhid, gv, gv, gv, gv, gv, dw_w, dw_b)


def _bias_line(rel_bias):
    nh = rel_bias.shape[0]
    line = jnp.concatenate(
        [jnp.zeros((nh, 1), rel_bias.dtype), jnp.broadcast_to(rel_bias[:, 2 * MAX_REL:], (nh, SHEAR_SAT)),
         jnp.flip(rel_bias[:, 1:2 * MAX_REL], axis=1)], axis=1)
    return line[:, None, :]


def bias_tile(name, line):
    nh = line.shape[0]

    def body(l_ref, o_ref):
        x = jnp.broadcast_to(l_ref[...], (Q_TILE, SHEAR_W))
        z = pltpu.roll(x, SHEAR_W - Q_TILE, 1, stride=1, stride_axis=0)
        qc = lax.broadcasted_iota(jnp.int32, (Q_TILE, K_WIN), 0) // CHUNK
        kc = lax.broadcasted_iota(jnp.int32, (Q_TILE, K_WIN), 1) // CHUNK
        o_ref[...] = jnp.where((kc >= qc) & (kc <= qc + LEFT_CHUNKS), z[:, :K_WIN], NEG_INF)

    return _call(
        body, name=name, grid=(nh,), in_specs=[pl.BlockSpec((None, 1, SHEAR_W), lambda hh: (hh, 0, 0))],
        out_specs=pl.BlockSpec((None, Q_TILE, K_WIN), lambda hh: (hh, 0, 0)),
        out_shape=jax.ShapeDtypeStruct((nh, Q_TILE, K_WIN), F32), compiler_params=_cp(),
    )(line)


def _stack_heads(x2, scale=None):
    if scale is not None:
        x2 = x2 * jnp.asarray(scale, x2.dtype)
    lane = lax.broadcasted_iota(jnp.int32, x2.shape, 1)
    zero = jnp.zeros_like(x2)
    return jnp.concatenate([jnp.where(lane < HEAD_DIM, x2, zero), jnp.where(lane < HEAD_DIM, zero, x2)], axis=0)


def _unstack_heads(x_st):
    lane = lax.broadcasted_iota(jnp.int32, (Q_TILE, LANES), 1)
    return jnp.where(lane < HEAD_DIM, x_st[:Q_TILE], x_st[Q_TILE:])


def _attn_probs(q_st, k3, bias_st, t):
    sc = _dot(q_st, k3, "nt") + bias_st
    col = lax.broadcasted_iota(jnp.int32, sc.shape, 1)
    sc = jnp.where(col >= PAD_ROWS - t * Q_TILE, sc, NEG_INF)
    m = jnp.max(sc, axis=-1, keepdims=True)
    p = jnp.exp(sc - m)
    return p * (1.0 / jnp.sum(p, axis=-1, keepdims=True))


def _attn_specs(d_model, pairs):
    nq = PAD_ROWS // Q_TILE
    width = pairs * LANES
    groups = d_model // width
    specs = [pl.BlockSpec((Q_TILE, width), lambda g, t: (t + nq, g))]
    for which in (1, 2):
        for j in range(K_WIN // Q_TILE):
            specs.append(pl.BlockSpec((Q_TILE, width), lambda g, t, j=j, which=which: (t + j, which * groups + g)))
    specs.append(pl.BlockSpec((2 * pairs, Q_TILE, K_WIN), lambda g, t: (g, 0, 0)))
    return specs


def attn_fwd(name, qkvp, bias):
    s = qkvp.shape[0] - PAD_ROWS
    d_model = qkvp.shape[1] // 3
    nw = K_WIN // Q_TILE

    def body(q_ref, *refs):
        k_refs, v_refs, b_ref, o_ref = refs[:nw], refs[nw:2 * nw], refs[2 * nw], refs[2 * nw + 1]
        t = pl.program_id(1)
        for j in range(ATTN_PAIRS_FWD):
            ls = slice(j * LANES, (j + 1) * LANES)
            k3 = jnp.concatenate([r[:, ls] for r in k_refs], axis=0)
            v3 = jnp.concatenate([r[:, ls] for r in v_refs], axis=0)
            bias_st = b_ref[2 * j:2 * j + 2].reshape(2 * Q_TILE, K_WIN)
            p = _attn_probs(_stack_heads(q_ref[:, ls], ATTN_SCALE), k3, bias_st, t)
            o_ref[:, ls] = _unstack_heads(_dot(p, v3, "nn")).astype(BF16)

    width = ATTN_PAIRS_FWD * LANES
    return _call(
        body, name=name, grid=(d_model // width, s // Q_TILE),
        in_specs=_attn_specs(d_model, ATTN_PAIRS_FWD), out_specs=pl.BlockSpec((Q_TILE, width), lambda g, t: (t, g)),
        out_shape=jax.ShapeDtypeStruct((s, d_model), BF16), compiler_params=_cp(),
    )(qkvp, *([qkvp] * (2 * nw)), bias)


def attn_bwd(name, qkvp, bias, do):
    s = qkvp.shape[0] - PAD_ROWS
    d_model = qkvp.shape[1] // 3
    nw = K_WIN // Q_TILE
    nt = s // Q_TILE

    def body(q_ref, *refs):
        k_refs, v_refs = refs[:nw], refs[nw:2 * nw]
        b_ref, do_ref, dq_ref, dk_ref, dv_ref, ds_ref, dk_acc, dv_acc = refs[2 * nw:]
        t = pl.program_id(1)
        first = t == 0

        @pl.when(first)
        def _():
            dk_acc[...] = jnp.zeros(dk_acc.shape, F32)
            dv_acc[...] = jnp.zeros(dv_acc.shape, F32)
            ds_ref[...] = jnp.zeros(ds_ref.shape, F32)

        start = pl.multiple_of(t * Q_TILE, Q_TILE)
        for j in range(ATTN_PAIRS):
            ls = slice(j * LANES, (j + 1) * LANES)
            q_st = _stack_heads(q_ref[:, ls], ATTN_SCALE)
            do_st = _stack_heads(do_ref[:, ls])
            k3 = jnp.concatenate([r[:, ls] for r in k_refs], axis=0)
            v3 = jnp.concatenate([r[:, ls] for r in v_refs], axis=0)
            p = _attn_probs(q_st, k3, b_ref[2 * j:2 * j + 2].reshape(2 * Q_TILE, K_WIN), t)
            dp = _dot(do_st, v3, "nt")
            ds = p * (dp - jnp.sum(p * dp, axis=-1, keepdims=True))
            ds_ref[2 * j:2 * j + 2] += ds.reshape(2, Q_TILE, K_WIN)
            dsb = ds.astype(BF16)
            dq_ref[:, ls] = (_unstack_heads(_dot(dsb, k3, "nn")) * ATTN_SCALE).astype(BF16)
            dk_acc[pl.ds(start, K_WIN), ls] += _dot(dsb, q_st, "tn")
            dv_acc[pl.ds(start, K_WIN), ls] += _dot(p, do_st, "tn")

        @pl.when(t == nt - 1)
        def _():
            dk_ref[...] = dk_acc[pl.ds(PAD_ROWS, s), :].astype(BF16)
            dv_ref[...] = dv_acc[pl.ds(PAD_ROWS, s), :].astype(BF16)

    specs = _attn_specs(d_model, ATTN_PAIRS) + [pl.BlockSpec((Q_TILE, ATTN_LANES), lambda g, t: (t, g))]
    col_spec = pl.BlockSpec((s, ATTN_LANES), lambda g, t: (0, g))
    return _call(
        body, name=name, grid=(d_model // ATTN_LANES, nt), in_specs=specs,
        out_specs=[pl.BlockSpec((Q_TILE, ATTN_LANES), lambda g, t: (t, g)), col_spec, col_spec,
                   pl.BlockSpec((2 * ATTN_PAIRS, Q_TILE, K_WIN), lambda g, t: (g, 0, 0))],
        out_shape=[jax.ShapeDtypeStruct((s, d_model), BF16)] * 3
        + [jax.ShapeDtypeStruct((N_HEADS, Q_TILE, K_WIN), F32)],
        scratch_shapes=[pltpu.VMEM((PAD_ROWS + s, ATTN_LANES), F32), pltpu.VMEM((PAD_ROWS + s, ATTN_LANES), F32)],
        compiler_params=_cp(),
    )(qkvp, *([qkvp] * (2 * nw)), bias, do)


def bias_grad_reduce(name, ds_sum):
    nh = ds_sum.shape[0]
    width = SHEAR_W + Q_TILE
    first_k = Q_TILE - 1

    def body(x_ref, col_ref, sat_ref):
        x = x_ref[...]
        hi = x.astype(BF16)
        lo = (x - hi.astype(F32)).astype(BF16)
        r = lax.broadcasted_iota(jnp.int32, (Q_TILE, Q_TILE), 0)
        c = lax.broadcasted_iota(jnp.int32, (Q_TILE, Q_TILE), 1)
        exchange = jnp.where(r + c == Q_TILE - 1, 1.0, 0.0).astype(BF16)
        x_rev = _dot(exchange, hi, "nn") + _dot(exchange, lo, "nn")
        zeros = jnp.zeros((Q_TILE, Q_TILE), F32)
        y = pltpu.roll(jnp.concatenate([zeros, x_rev, zeros], axis=1), 0, 1, stride=1, stride_axis=0)
        cols = _colsum(y)
        col_ref[...] = cols
        k = lax.broadcasted_iota(jnp.int32, cols.shape, 1) - first_k
        tot = jnp.sum(jnp.where((k >= 1) & (k <= SHEAR_SAT), cols, 0.0), axis=-1, keepdims=True)
        sat_ref[...] = jnp.broadcast_to(tot, sat_ref.shape)

    return _call(
        body, name=name, grid=(nh,),
        in_specs=[pl.BlockSpec((None, Q_TILE, K_WIN), lambda hh: (hh, 0, 0))],
        out_specs=[pl.BlockSpec((None, 1, width), lambda hh: (hh, 0, 0)),
                   pl.BlockSpec((None, 1, LANES), lambda hh: (hh, 0, 0))],
        out_shape=[jax.ShapeDtypeStruct((nh, 1, width), F32), jax.ShapeDtypeStruct((nh, 1, LANES), F32)],
        compiler_params=_cp(),
    )(ds_sum)


def _ew_rows(r, most=512, cols=None):
    if cols is not None and r * cols * 4 <= SMALL_BLOCK_BYTES:
        return r
    for cand in range(min(most, r) // 16 * 16, 0, -16):
        if r % cand == 0:
            return cand
    return r


def cast_into_gathered(name, w, layer, s_idx, n_blocks=N_SHARD, dtype=BF16, token=None):
    r, c = w.shape[-2:]
    tr = _ew_rows(r, cols=c)

    def body(s_ref, w_ref, *rest):
        rest[-1][...] = w_ref[...].astype(dtype)

    extra = [] if token is None else [token]
    grid_spec = pltpu.PrefetchScalarGridSpec(
        num_scalar_prefetch=1, grid=(r // tr,),
        in_specs=[pl.BlockSpec((None, tr, c), lambda i, s_ref: (layer, i, 0))] + [ANY_SPEC] * len(extra),
        out_specs=pl.BlockSpec((None, tr, c), lambda i, s_ref: (s_ref[0], i, 0)))
    return _call(
        body, name=name, grid_spec=grid_spec, out_shape=jax.ShapeDtypeStruct((n_blocks, r, c), dtype),
        compiler_params=_cp(),
    )(s_idx, w, *extra)


def adamw(name, w, grads, m, v, token=None):
    nl, r, c = w.shape
    tr = _ew_rows(r, 256, cols=c)

    def body(*refs):
        w_ref, m_ref, v_ref = refs[0], refs[1], refs[2]
        g_refs = refs[3:3 + nl]
        d_ref, nm_ref, nv_ref = refs[-3:]
        layer = pl.program_id(0)
        g = g_refs[0][...]
        for j in range(1, nl):
            g = jnp.where(layer == j, g_refs[j][...], g)
        d_ref[...], nm_ref[...], nv_ref[...] = _adamw_update(w_ref[...], g, m_ref[...], v_ref[...])

    p_spec = pl.BlockSpec((None, tr, c), lambda l, i: (l, i, 0))
    g_spec = pl.BlockSpec((tr, c), lambda l, i: (i, 0))
    extra = [] if token is None else [token]
    extra_specs = [] if token is None else [ANY_SPEC]
    return _call(
        body, name=name, grid=(nl, r // tr), in_specs=[p_spec] * 3 + [g_spec] * nl + extra_specs,
        out_specs=[p_spec] * 3, out_shape=[jax.ShapeDtypeStruct((nl, r, c), F32)] * 3, compiler_params=_cp(),
    )(w, m, v, *grads, *extra)


def _adamw_update(w, g, m, v):
    nm = ADAM_B1 * m + (1.0 - ADAM_B1) * g
    nv = ADAM_B2 * v + (1.0 - ADAM_B2) * (g * g)
    delta = -ADAM_LR * ((nm / ADAM_BC1) / (jnp.sqrt(nv / ADAM_BC2) + ADAM_EPS) + ADAM_WD * w)
    return delta, nm, nv


def adamw_many(name, ws, gs, ms, vs, token):
    n = len(ws)

    def body(*refs):
        ins, outs = refs[:4 * n], refs[4 * n + 1:]
        for i in range(n):
            delta, nm, nv = _adamw_update(ins[i][...], ins[n + i][...], ins[2 * n + i][...], ins[3 * n + i][...])
            outs[3 * i][...] = delta
            outs[3 * i + 1][...] = nm
            outs[3 * i + 2][...] = nv

    vmem = pl.BlockSpec(memory_space=pltpu.VMEM)
    shapes = [jax.ShapeDtypeStruct(w.shape, F32) for w in ws for _ in range(3)]
    outs = _call(
        body, name=name, in_specs=[vmem] * (4 * n) + [ANY_SPEC], out_specs=[vmem] * (3 * n), out_shape=shapes,
        compiler_params=_cp(),
    )(*ws, *gs, *ms, *vs, token)
    return [tuple(outs[3 * i:3 * i + 3]) for i in range(n)]


def sum_blocks(name, gathered, n_blocks):
    r = gathered.shape[0] // n_blocks
    c = gathered.shape[1]
    tr = r if r <= SUM_BLOCK_ROWS else _ew_rows(r)
    nt = r // tr

    def body(*refs):
        acc = refs[0][...]
        for j in range(1, n_blocks):
            acc = acc + refs[j][...]
        refs[-1][...] = acc

    specs = [pl.BlockSpec((tr, c), lambda i, j=j: (j * nt + i, 0)) for j in range(n_blocks)]
    return _call(
        body, name=name, grid=(nt,), in_specs=specs, out_specs=pl.BlockSpec((tr, c), lambda i: (i, 0)),
        out_shape=jax.ShapeDtypeStruct((r, c), F32), compiler_params=_cp(),
    )(*([gathered] * n_blocks))


def _place():
    return lax.axis_index("x"), lax.axis_index("y"), lax.axis_index("c")


def _other_chips(x, y):
    return [(1 - x, y), (x, 1 - y), (1 - x, 1 - y)]


HBM_SPEC = pl.BlockSpec(memory_space=pltpu.HBM)
SEM_SPEC = pl.BlockSpec(memory_space=pltpu.SEMAPHORE)
ANY_SPEC = pl.BlockSpec(memory_space=pl.ANY)
EFFECT = pltpu.SideEffectType.DATAFLOW_SIDE_EFFECTING


def copies_start(name, bufs, plan, n_copies):
    n = len(bufs)

    def body(*refs):
        send, recv = refs[n], refs[n + 1]
        token = refs[2 * n + 2]
        for k, (src, dst, peer, _) in enumerate(plan(refs[:n])):
            pltpu.make_async_remote_copy(
                src_ref=src, dst_ref=dst, send_sem=send.at[k], recv_sem=recv.at[k],
                device_id=peer, device_id_type=MESH).start()
        token[...] = jnp.zeros(token.shape, F32)

    outs = pl.pallas_call(
        body, name=name,
        out_shape=(pltpu.SemaphoreType.DMA((n_copies,)), pltpu.SemaphoreType.DMA((n_copies,)),
                   *[pltpu.HBM(b.shape, b.dtype) for b in bufs], jax.ShapeDtypeStruct((8, LANES), F32)),
        in_specs=[HBM_SPEC] * n,
        out_specs=(SEM_SPEC, SEM_SPEC, *([HBM_SPEC] * n), pl.BlockSpec(memory_space=pltpu.VMEM)),
        input_output_aliases={a: a + 2 for a in range(n)},
        compiler_params=pltpu.CompilerParams(has_side_effects=EFFECT),
    )(*[_in_hbm(b) for b in bufs])
    return outs[0], outs[1], list(outs[2:2 + n]), outs[2 + n]


def copies_wait(name, bufs, send, recv, plan, sem_base, after):
    n = len(bufs)

    def body(*refs):
        send_ref, recv_ref = refs[n], refs[n + 1]
        for k, (src, _, peer, land) in enumerate(plan(refs[:n])):
            cp = pltpu.make_async_remote_copy(
                src_ref=src, dst_ref=land, send_sem=send_ref.at[sem_base + k], recv_sem=recv_ref.at[sem_base + k],
                device_id=peer, device_id_type=MESH)
            cp.wait_send()
            cp.wait_recv()

    outs = pl.pallas_call(
        body, name=name,
        out_shape=tuple(pltpu.HBM(b.shape, b.dtype) for b in bufs),
        in_specs=[HBM_SPEC] * n + [SEM_SPEC, SEM_SPEC, ANY_SPEC], out_specs=tuple([HBM_SPEC] * n),
        input_output_aliases={a: a for a in range(n)},
        compiler_params=pltpu.CompilerParams(has_side_effects=EFFECT),
    )(*bufs, send, recv, after)
    return list(outs)


def gather_plan(refs):
    x, y, c = _place()
    me = 2 * x + y
    return [(buf.at[me], buf.at[me], (cx, cy, c), buf.at[2 * cx + cy])
            for buf in refs for cx, cy in _other_chips(x, y)]


def all_plan(refs):
    x, y, c = _place()
    me = 4 * x + 2 * y + c
    out = []
    for buf in refs:
        for flip in range(1, 8):
            px = 1 - x if flip & 4 else x
            py = 1 - y if flip & 2 else y
            pc = 1 - c if flip & 1 else c
            out.append((buf.at[me], buf.at[me], (px, py, pc), buf.at[4 * px + 2 * py + pc]))
    return out


def swap_plan(refs):
    x, y, c = _place()
    n = len(refs) // 2
    out = []
    for g, land in zip(refs[:n], refs[n:]):
        hr = g.shape[1] // 2
        out.append((g.at[:, pl.ds((1 - c) * hr, hr)], land, (x, y, 1 - c), land))
    return out


def owners_plan(refs):
    x, y, c = _place()
    n = len(refs) // 2
    return [(src.at[2 * cx + cy], land.at[j], (cx, cy, c), land.at[j])
            for src, land in zip(refs[:n], refs[n:]) for j, (cx, cy) in enumerate(_other_chips(x, y))]


def join_plan(refs):
    x, y, c = _place()
    out = []
    for buf in refs:
        hr = buf.shape[0] // 2
        mine = buf.at[pl.ds(c * hr, hr)]
        out.append((mine, mine, (x, y, 1 - c), buf.at[pl.ds((1 - c) * hr, hr)]))
    return out


def add_halves(name, grad, landed, sc_idx):
    _, r, c = grad.shape
    hr = r // 2
    tr = _ew_rows(hr)
    nt = hr // tr

    def body(sc_ref, g_ref, l_ref, own_ref, wire_ref):
        tot = g_ref[...] + l_ref[...]
        wire_ref[...] = tot.astype(BF16)

        @pl.when(pl.program_id(1) == sc_ref[0])
        def _():
            own_ref[...] = tot

    grid_spec = pltpu.PrefetchScalarGridSpec(
        num_scalar_prefetch=1, grid=(nt, N_SHARD),
        in_specs=[pl.BlockSpec((None, tr, c), lambda i, sh, sc_ref: (sh, sc_ref[1] * nt + i, 0)),
                  pl.BlockSpec((None, tr, c), lambda i, sh, sc_ref: (sh, i, 0))],
        out_specs=[pl.BlockSpec((tr, c), lambda i, sh, sc_ref: (i, 0)),
                   pl.BlockSpec((None, tr, c), lambda i, sh, sc_ref: (sh, i, 0))])
    return _call(
        body, name=name, grid_spec=grid_spec,
        out_shape=[jax.ShapeDtypeStruct((hr, c), F32), jax.ShapeDtypeStruct((N_SHARD, hr, c), BF16)],
        compiler_params=_cp(),
    )(sc_idx, grad, landed)


def add_owned(name, own, landed, sc_idx):
    hr, c = own.shape
    tr = _ew_rows(hr)
    nt = hr // tr

    def body(sc_ref, o_ref, l0, l1, l2, out_ref):
        out_ref[...] = ((o_ref[...] + l0[...].astype(F32)) + l1[...].astype(F32)) + l2[...].astype(F32)

    grid_spec = pltpu.PrefetchScalarGridSpec(
        num_scalar_prefetch=1, grid=(nt,),
        in_specs=[pl.BlockSpec((tr, c), lambda i, sc_ref: (i, 0))]
        + [pl.BlockSpec((None, tr, c), lambda i, sc_ref, j=j: (j, i, 0)) for j in range(3)],
        out_specs=pl.BlockSpec((tr, c), lambda i, sc_ref: (sc_ref[1] * nt + i, 0)))
    return _call(
        body, name=name, grid_spec=grid_spec, out_shape=jax.ShapeDtypeStruct((2 * hr, c), F32),
        compiler_params=_cp(),
    )(sc_idx, own, landed, landed, landed)


PACK_QUANTUM = 8 * LANES


def _pack(arrays):
    pieces = []
    for a in arrays:
        flat = a.reshape(-1)
        padded = -(-flat.shape[0] // PACK_QUANTUM) * PACK_QUANTUM
        pieces.append(jnp.pad(flat, (0, padded - flat.shape[0])).reshape(-1, LANES))
    return jnp.concatenate(pieces, axis=0)


def _unpack(packed, shapes):
    out = []
    row = 0
    for shp in shapes:
        size = math.prod(shp)
        rows = -(-size // PACK_QUANTUM) * 8
        out.append(packed[row:row + rows].reshape(-1)[:size].reshape(shp))
        row += rows
    return out


def kernel(x, p, mix_w_in, pool_w, pool_scale, conv_dw_w, conv_dw_b, conv_ln_g, conv_ln_b, mix_w_out, attn_w_qkv, attn_rel_bias, attn_w_o, ln_mix_g, ln_mix_b, ffn_w_up, ffn_dw_w, ffn_dw_b, ffn_w_down, ple_w_proj, ple_w_gate, ple_b_gate, ln_ffn_g, ln_ffn_b, loss_target, m_mix_w_in, m_pool_w, m_pool_scale, m_conv_dw_w, m_conv_dw_b, m_conv_ln_g, m_conv_ln_b, m_mix_w_out, m_attn_w_qkv, m_attn_rel_bias, m_attn_w_o, m_ln_mix_g, m_ln_mix_b, m_ffn_w_up, m_ffn_dw_w, m_ffn_dw_b, m_ffn_w_down, m_ple_w_proj, m_ple_w_gate, m_ple_b_gate, m_ln_ffn_g, m_ln_ffn_b, v_mix_w_in, v_pool_w, v_pool_scale, v_conv_dw_w, v_conv_dw_b, v_conv_ln_g, v_conv_ln_b, v_mix_w_out, v_attn_w_qkv, v_attn_rel_bias, v_attn_w_o, v_ln_mix_g, v_ln_mix_b, v_ffn_w_up, v_ffn_dw_w, v_ffn_dw_b, v_ffn_w_down, v_ple_w_proj, v_ple_w_gate, v_ple_b_gate, v_ln_ffn_g, v_ln_ffn_b):
    xi, yi, ci = _place()
    shard_idx = (2 * xi + yi).astype(jnp.int32)
    s_arr = shard_idx.reshape(1)
    c_arr = ci.astype(jnp.int32).reshape(1)
    sc_arr = jnp.concatenate([s_arr, c_arr])

    x0 = x[0]
    target = loss_target[0]
    p_rows = p.reshape(p.shape[0] * p.shape[2], p.shape[3])
    seq = x0.shape[0]

    big = [
        ("mix_w_in", mix_w_in, m_mix_w_in, v_mix_w_in, True),
        ("mix_w_out", mix_w_out, m_mix_w_out, v_mix_w_out, False),
        ("attn_w_qkv", attn_w_qkv, m_attn_w_qkv, v_attn_w_qkv, True),
        ("attn_w_o", attn_w_o, m_attn_w_o, v_attn_w_o, False),
        ("ffn_w_up", ffn_w_up, m_ffn_w_up, v_ffn_w_up, True),
        ("ffn_w_down", ffn_w_down, m_ffn_w_down, v_ffn_w_down, False),
        ("ple_w_proj", ple_w_proj, m_ple_w_proj, v_ple_w_proj, True),
        ("ple_w_gate", ple_w_gate, m_ple_w_gate, v_ple_w_gate, False),
    ]
    params = {nm: w for nm, w, _, _, _ in big}
    col_sharded = {nm: cs for nm, _, _, _, cs in big}
    keys = [("mix_w_in", 0), ("mix_w_out", 0), ("ffn_w_up", 0), ("ffn_w_down", 0), ("ple_w_gate", 0),
            ("ple_w_proj", 0), ("attn_w_qkv", 0), ("attn_w_o", 0), ("ffn_w_up", 1), ("ffn_w_down", 1),
            ("ple_w_gate", 1), ("ple_w_proj", 1)]
    dw_shapes = [conv_dw_w.shape, ffn_dw_w.shape]
    dw_block = cast_into_gathered("place_dw", _pack([conv_dw_w, ffn_dw_w])[None], 0, s_arr, dtype=F32)
    n_first = 2
    started = {}
    gather_token = None
    for tag, group in (("first", keys[:n_first]), ("rest", keys[n_first:])):
        shards = [cast_into_gathered(f"cast_{nm}_{layer}", params[nm], layer, s_arr, token=gather_token)
                  for nm, layer in group]
        if tag == "first":
            shards.append(dw_block)
        send, recv, bufs, gather_token = copies_start(f"gather_start_{tag}", shards, gather_plan, 3 * len(shards))
        for a, key in enumerate(group):
            started[key] = (send, recv, bufs[a], 3 * a)
        if tag == "first":
            dw_started = (send, recv, bufs[-1], 3 * len(group))
    arrived_w = {}

    def weight(nm, layer, after=None):
        key = (nm, layer)
        if key not in arrived_w:
            send, recv, buf, base = started[key]
            arrived_w[key] = copies_wait(f"gather_wait_{nm}_{layer}", [buf], send, recv, gather_plan, base, after)[0]
        g = arrived_w[key]
        if col_sharded[nm]:
            return g
        return g.reshape(g.shape[0] * g.shape[1], g.shape[2])

    def tie(a, token):
        return a + token[0:1, 0:1].astype(a.dtype)

    class Reducer:
        def __init__(self, tag, group):
            self.tag, self.group, self.stage = tag, group, 0
            self.n = len(group)
            self.result = None

        def advance(self, after):
            tag, n = self.tag, self.n
            if self.stage == 0:
                grads = []
                for key in self.group:
                    g = big_grads[key]
                    grads.append(g if g.ndim == 3 else g.reshape(N_SHARD, g.shape[0] // N_SHARD, g.shape[1]))
                lands = [lax.empty((N_SHARD, g.shape[1] // 2, g.shape[2]), F32) for g in grads]
                self.sems = copies_start(f"swap_start_{tag}", grads + lands, swap_plan, n)
            elif self.stage == 1:
                send, recv, bufs, _ = self.sems
                outs = copies_wait(f"swap_wait_{tag}", bufs, send, recv, swap_plan, 0, after)
                self.own, wire = [], []
                for key, g, ld in zip(self.group, outs[:n], outs[n:]):
                    o, ob = add_halves(f"add_halves_{key[0]}_{key[1]}", g, ld, sc_arr)
                    self.own.append(o)
                    wire.append(ob)
                lands = [lax.empty((3,) + w.shape[1:], BF16) for w in wire]
                self.sems = copies_start(f"owners_start_{tag}", wire + lands, owners_plan, 3 * n)
            elif self.stage == 2:
                send, recv, bufs, _ = self.sems
                outs = copies_wait(f"owners_wait_{tag}", bufs, send, recv, owners_plan, 0, after)
                finals = [add_owned(f"add_owned_{key[0]}_{key[1]}", o, ar, sc_arr)
                          for key, o, ar in zip(self.group, self.own, outs[n:])]
                self.sems = copies_start(f"join_start_{tag}", finals, join_plan, n)
            elif self.stage == 3:
                send, recv, bufs, _ = self.sems
                outs = copies_wait(f"join_wait_{tag}", bufs, send, recv, join_plan, 0, after)
                self.result = dict(zip(self.group, outs))
                self.sems = None
            self.stage += 1
            return None if self.sems is None else self.sems[3]

    dw_cache = []

    def conv_weights(after):
        if not dw_cache:
            send, recv, buf, base = dw_started
            dw_all = copies_wait("gather_wait_dw", [buf], send, recv, gather_plan, base, after)[0]
            dw_parts = [_unpack(dw_all[k], dw_shapes) for k in range(N_SHARD)]
            dw_cache.append(jnp.concatenate([pc[0] for pc in dw_parts], axis=2)[0])
            dw_cache.append(jnp.concatenate([pc[1] for pc in dw_parts], axis=2))
        return dw_cache

    big_grads = {}
    small_grads = {}

    saved = []
    h_in = x0
    h_in_b = x0
    for layer in range(N_LAYERS):
        sv = {"x_in": h_in_b}
        if layer % 2 == 0:
            u = mm_cols_fwd("mix_in", h_in_b, weight("mix_w_in", 0, gather_token), F32)
            conv_w_full, ffn_dw_full = conv_weights(u)
            cat, d_sv, e_sv, glu_sv, hh_sv, rs_sv = mixer_fwd(
                "mixer_fwd", u, pool_w[0], pool_scale, conv_w_full, conv_dw_b, conv_ln_g, conv_ln_b)
            mix = mm_rows_fwd("mix_out", cat, weight("mix_w_out", 0, cat))
            sv.update(u=u, cat=cat, d=d_sv, e=e_sv, glu=glu_sv, hh=hh_sv, rs=rs_sv)
        else:
            qkvp = mm_cols_fwd("attn_qkv", h_in_b, weight("attn_w_qkv", 0, h_in_b), BF16,
                               pad_blocks=PAD_ROWS // _row_tile(seq))
            bias = bias_tile("bias_tile", _bias_line(attn_rel_bias[0]))
            att = attn_fwd("attn_fwd", qkvp, bias)
            mix = mm_rows_fwd("attn_out", att, weight("attn_w_o", 0, att))
            sv.update(qkvp=qkvp, bias=bias, att=att)
        x1, x1_b, xh1, rs1 = ln_fwd(f"ln_mix_{layer}", h_in, mix, ln_mix_g[layer:layer + 1],
                                    ln_mix_b[layer:layer + 1])
        gv = mm_cols_fwd(f"ffn_up_{layer}", x1_b, weight("ffn_w_up", layer, x1_b), F32)
        hid = ffn_act_fwd(f"ffn_act_{layer}", gv, ffn_dw_full[layer], ffn_dw_b[layer:layer + 1])
        ffn = mm_rows_fwd(f"ffn_down_{layer}", hid, weight("ffn_w_down", layer, hid))
        pgl = mm_rows_fwd(f"ple_gate_{layer}", x1_b, weight("ple_w_gate", layer, ffn))
        pp = mm_cols_fwd(f"ple_proj_{layer}", p_rows, weight("ple_w_proj", layer, pgl), F32, part=(layer, N_LAYERS))
        bg = ple_b_gate[layer:layer + 1]
        x2, x2_b, xh2, rs2 = ln_fwd(f"ln_ffn_{layer}", x1, ffn, ln_ffn_g[layer:layer + 1], ln_ffn_b[layer:layer + 1],
                                    ple=(pgl, pp, bg), emit_y=layer < N_LAYERS - 1)
        sv.update(x1=x1_b, xh1=xh1, rs1=rs1, gv=gv, hid=hid, pgl=pgl, pp=pp, xh2=xh2, rs2=rs2)
        saved.append(sv)
        h_in, h_in_b = x2, x2_b

    reducers = []

    def open_group(tag, group):
        reducers.append(Reducer(tag, group))
        return reducers[-1].advance(None)

    def hook(after):
        token = None
        for red in reducers:
            if red.stage < 4:
                tk = red.advance(after)
                if tk is not None:
                    token = tk if token is None else token + tk
        return token

    def tied(a, token):
        return a if token is None else tie(a, token)

    parts = []
    token = None
    for layer in reversed(range(N_LAYERS)):
        sv = saved[layer]
        bg = ple_b_gate[layer:layer + 1]
        if layer == 0:
            token = open_group("layer1", [("attn_w_qkv", 0), ("attn_w_o", 0), ("ffn_w_up", 1), ("ffn_w_down", 1),
                                          ("ple_w_gate", 1), ("ple_w_proj", 1)])
        last = layer == N_LAYERS - 1
        res = ln_bwd(
            f"ln_ffn_bwd_{layer}", parts, sv["xh2"], sv["rs2"], tied(ln_ffn_g[layer:layer + 1], token),
            ple=(sv["pgl"], sv["pp"], bg), loss=(target, ln_ffn_b[layer:layer + 1]) if last else None)
        dz2, dg2, db2, dpp, dpgl, dbg = res[:6]
        if last:
            loss_part = res[6]
        small_grads[("ln_ffn_g", layer)] = dg2
        small_grads[("ln_ffn_b", layer)] = db2
        small_grads[("ple_b_gate", layer)] = dbg
        w_down = weight("ffn_w_down", layer)
        dhid = mm_rows_dx(f"ffn_down_dx_{layer}", dz2, w_down)
        big_grads[("ffn_w_down", layer)] = mm_rows_dw(f"ffn_down_dw_{layer}", sv["hid"], dz2)
        token = hook(big_grads[("ffn_w_down", layer)])
        dgv, ddw, ddb = ffn_act_bwd(f"ffn_act_bwd_{layer}", dhid, sv["gv"], ffn_dw_full[layer],
                                    tied(ffn_dw_b[layer:layer + 1], token))
        small_grads[("ffn_dw_w", layer)] = ddw
        small_grads[("ffn_dw_b", layer)] = ddb
        big_grads[("ffn_w_up", layer)] = mm_cols_dw(f"ffn_up_dw_{layer}", sv["x1"], dgv)
        t_up = mm_cols_dx(f"ffn_up_dx_{layer}", dgv, weight("ffn_w_up", layer))
        token = hook(t_up)
        big_grads[("ple_w_gate", layer)] = mm_rows_dw(f"ple_gate_dw_{layer}", sv["x1"], dpgl)
        t_gate = mm_rows_dx(f"ple_gate_dx_{layer}", dpgl, weight("ple_w_gate", layer))
        big_grads[("ple_w_proj", layer)] = mm_cols_dw(f"ple_proj_dw_{layer}", p_rows, dpp, part=(layer, N_LAYERS))
        token2 = hook(big_grads[("ple_w_proj", layer)])
        if token2 is not None:
            token = token2 if token is None else token + token2
        if layer == 0:
            token3 = open_group("layer0_ffn", [("ffn_w_up", 0), ("ffn_w_down", 0), ("ple_w_gate", 0), ("ple_w_proj", 0)])
            token = token3 if token is None else token + token3
        dz1, dg1, db1 = ln_bwd(
            f"ln_mix_bwd_{layer}", [(ALPHA, dz2), (1.0, t_up), (1.0, t_gate)], sv["xh1"], sv["rs1"],
            tied(ln_mix_g[layer:layer + 1], token))
        small_grads[("ln_mix_g", layer)] = dg1
        small_grads[("ln_mix_b", layer)] = db1
        if layer % 2 == 0:
            dcat = mm_rows_dx("mix_out_dx", dz1, weight("mix_w_out", 0))
            big_grads[("mix_w_out", 0)] = mm_rows_dw("mix_out_dw", sv["cat"], dz1)
            token = hook(big_grads[("mix_w_out", 0)])
            du, dpw, dps, dcw, dcb, dcg, dcbt = mixer_bwd(
                "mixer_bwd", dcat, sv["u"], sv["d"], sv["e"], sv["glu"], sv["hh"], sv["rs"],
                pool_w[0], pool_scale, conv_w_full, tied(conv_ln_g, token), conv_ln_b)
            small_grads[("pool_w", 0)] = dpw
            small_grads[("pool_scale", 0)] = dps
            small_grads[("conv_dw_w", 0)] = dcw
            small_grads[("conv_dw_b", 0)] = dcb
            small_grads[("conv_ln_g", 0)] = dcg
            small_grads[("conv_ln_b", 0)] = dcbt
            big_grads[("mix_w_in", 0)] = mm_cols_dw("mix_in_dw", sv["x_in"], du)
            hook(big_grads[("mix_w_in", 0)])
            open_group("layer0_mix", [("mix_w_in", 0), ("mix_w_out", 0)])
            dx_in = mm_cols_dx("mix_in_dx", du, weight("mix_w_in", 0), addend=(ALPHA, dz1))
            token = hook(dx_in)
        else:
            do = mm_rows_dx("attn_out_dx", dz1, weight("attn_w_o", 0), out_dtype=BF16)
            big_grads[("attn_w_o", 0)] = mm_rows_dw("attn_out_dw", sv["att"], dz1)
            dq, dk, dv, ds_sum = attn_bwd("attn_bwd", sv["qkvp"], sv["bias"], do)
            cols, sat = bias_grad_reduce("bias_grad", ds_sum)
            d_rel = jnp.concatenate(
                [jnp.zeros((N_HEADS, 1), F32),
                 jnp.flip(cols[:, 0, Q_TILE + SHEAR_SAT:Q_TILE - 1 + SHEAR_W], axis=1),
                 sat[:, 0, 0:1]], axis=1)
            small_grads[("attn_rel_bias", 0)] = d_rel
            dqkv = jnp.concatenate([dq, dk, dv], axis=1)
            big_grads[("attn_w_qkv", 0)] = mm_cols_dw("attn_qkv_dw", sv["x_in"], dqkv)
            dx_in = mm_cols_dx("attn_qkv_dx", dqkv, weight("attn_w_qkv", 0), addend=(ALPHA, dz1))
        parts = [(1.0, dx_in)]
    grad_x = dx_in

    small = [
        ("pool_w", pool_w, m_pool_w, v_pool_w, None),
        ("pool_scale", pool_scale, m_pool_scale, v_pool_scale, None),
        ("conv_dw_w", conv_dw_w, m_conv_dw_w, v_conv_dw_w, 2),
        ("conv_dw_b", conv_dw_b, m_conv_dw_b, v_conv_dw_b, None),
        ("conv_ln_g", conv_ln_g, m_conv_ln_g, v_conv_ln_g, None),
        ("conv_ln_b", conv_ln_b, m_conv_ln_b, v_conv_ln_b, None),
        ("attn_rel_bias", attn_rel_bias, m_attn_rel_bias, v_attn_rel_bias, None),
        ("ln_mix_g", ln_mix_g, m_ln_mix_g, v_ln_mix_g, None),
        ("ln_mix_b", ln_mix_b, m_ln_mix_b, v_ln_mix_b, None),
        ("ffn_dw_w", ffn_dw_w, m_ffn_dw_w, v_ffn_dw_w, 2),
        ("ffn_dw_b", ffn_dw_b, m_ffn_dw_b, v_ffn_dw_b, None),
        ("ple_b_gate", ple_b_gate, m_ple_b_gate, v_ple_b_gate, None),
        ("ln_ffn_g", ln_ffn_g, m_ln_ffn_g, v_ln_ffn_g, None),
        ("ln_ffn_b", ln_ffn_b, m_ln_ffn_b, v_ln_ffn_b, None),
    ]
    full_grads = []
    for nm, w, _, _, shard_axis in small:
        full = list(w.shape)
        if shard_axis is not None:
            full[shard_axis] *= N_SHARD
        per_layer = [small_grads[(nm, layer)].reshape((1,) + tuple(full[1:])) for layer in range(w.shape[0])]
        full_grads.append(jnp.concatenate(per_layer, axis=0))
    packed = _pack(full_grads + [loss_part])
    dev_arr = (4 * xi + 2 * yi + ci).astype(jnp.int32).reshape(1)
    sg_block = cast_into_gathered("place_small_grads", packed[None], 0, dev_arr, n_blocks=8, dtype=F32)
    sg_send, sg_recv, sg_bufs, sg_token = copies_start("small_grads_start", [sg_block], all_plan, 7)
    token = sg_token if token is None else token + sg_token

    shard_grads = {}
    for red in reducers:
        if red.stage == 4:
            shard_grads.update(red.result)
    big_out = {}

    def update_big(names, tok):
        for nm, w, m, v, _ in big:
            if nm in names:
                gl = [shard_grads[(nm, layer)] for layer in range(w.shape[0])]
                delta, new_m, new_v = adamw(f"adamw_{nm}", w, gl, m, v, token=tok)
                big_out[nm] = (jnp.stack(gl, axis=0), delta, new_m, new_v)

    last_group = ("mix_w_in", "mix_w_out")
    update_big([nm for nm, _, _, _, _ in big if nm not in last_group], token)
    token = hook(big_out["ffn_w_up"][1])

    gathered_sg = copies_wait("small_grads_wait", sg_bufs, sg_send, sg_recv, all_plan, 0, big_out["ffn_w_down"][1])[0]
    total = sum_blocks("sum_small", gathered_sg.reshape(8 * packed.shape[0], LANES), 8)
    unpacked = _unpack(total, [g.shape for g in full_grads] + [loss_part.shape])
    loss = unpacked[-1][0, 0]
    local_grads = []
    for (nm, w, _, _, shard_axis), g in zip(small, unpacked[:-1]):
        if shard_axis is not None:
            width = w.shape[shard_axis]
            g = lax.dynamic_slice_in_dim(g, shard_idx * width, width, axis=shard_axis)
        local_grads.append(g.reshape(w.shape))
    updated = adamw_many("adamw_small", [w for _, w, _, _, _ in small], local_grads,
                         [m for _, _, m, _, _ in small], [v for _, _, _, v, _ in small], token)
    hook(updated[0][0])
    for red in reducers:
        shard_grads.update(red.result)
    update_big(last_group, None)
    small_out = {}
    for (nm, _, _, _, _), g, (d_, m_, v_) in zip(small, local_grads, updated):
        small_out[nm] = (g, d_, m_, v_)

    order = ["mix_w_in", "pool_w", "pool_scale", "conv_dw_w", "conv_dw_b", "conv_ln_g", "conv_ln_b", "mix_w_out",
             "attn_w_qkv", "attn_rel_bias", "attn_w_o", "ln_mix_g", "ln_mix_b", "ffn_w_up", "ffn_dw_w", "ffn_dw_b",
             "ffn_w_down", "ple_w_proj", "ple_w_gate", "ple_b_gate", "ln_ffn_g", "ln_ffn_b"]
    res = {**big_out, **small_out}
    outs = [loss, grad_x[None]]
    for slot in range(4):
        outs += [res[nm][slot] for nm in order]
    return tuple(outs)
```

```python
import math

import jax
import jax.numpy as jnp
from jax import lax
from jax.experimental import pallas as pl
from jax.experimental.pallas import tpu as pltpu

F32 = jnp.float32
BF16 = jnp.bfloat16
MESH = pl.DeviceIdType.MESH

N_LAYERS = 2
ALPHA = (2 * N_LAYERS) ** 0.25
LN_EPS = 1e-5
NEG_INF = -1e30
CHUNK = 64
LEFT_CHUNKS = 8
PAD_ROWS = LEFT_CHUNKS * CHUNK
HEAD_DIM = 64
ATTN_SCALE = HEAD_DIM ** -0.5
N_HEADS = 16
MAX_REL = 256
POOL_WINDOWS = (2, 4, 8, 16)
POOL_GROUP = 128
CONV_K = 31
FFN_K = 3
CONV_HALO = 32
FFN_HALO = 8
FFN_TILE = 256
FFN_CHUNK_ROWS = 64
FFN_CHUNK_LANES = 128
Q_TILE = 256
K_WIN = Q_TILE + PAD_ROWS
LANES = 128
SUBLANES = 8
ATTN_PAIRS = 2
ATTN_PAIRS_FWD = 8
ATTN_LANES = ATTN_PAIRS * LANES
SHEAR_W = Q_TILE + K_WIN
SHEAR_SAT = SHEAR_W - 2 * MAX_REL
N_SHARD = 4

ADAM_LR = 0.001
ADAM_B1 = 0.9
ADAM_B2 = 0.999
ADAM_EPS = 1e-08
ADAM_WD = 0.01
ADAM_STEP = 10
ADAM_BC1 = 1.0 - ADAM_B1 ** ADAM_STEP
ADAM_BC2 = 1.0 - ADAM_B2 ** ADAM_STEP

DIMS = {
    "nn": (((1,), (0,)), ((), ())),
    "nt": (((1,), (1,)), ((), ())),
    "tn": (((0,), (0,)), ((), ())),
}


def _cp(vmem_mb=48, **kw):
    return pltpu.CompilerParams(vmem_limit_bytes=vmem_mb * 1024 * 1024, **kw)


def _in_hbm(a):
    return pltpu.with_memory_space_constraint(a, pltpu.HBM)


STAGING_LIMIT_BYTES = 1 << 20
SMALL_WEIGHT_BYTES = 1 << 22
SUM_BLOCK_ROWS = 2048
SMALL_BLOCK_BYTES = 1 << 19


def _call(body, **kw):
    call = pl.pallas_call(body, **kw)

    def run(*args):
        pinned = []
        for a in args:
            big = a.size * a.dtype.itemsize >= STAGING_LIMIT_BYTES
            pinned.append(_in_hbm(a) if big and not jnp.issubdtype(a.dtype, jnp.integer) else a)
        return call(*pinned)

    return run


def _dot(a, b, mode):
    return lax.dot_general(a.astype(BF16), b.astype(BF16), DIMS[mode], preferred_element_type=F32)


def _sig(x):
    return 1.0 / (1.0 + jnp.exp(-x))


def _row_tile(s):
    return min(512, s // 4)


def _mm_tile(s):
    return min(1024, s // 4)


def _mm(name, mode, a, b, in_specs, out_shape, out_spec, acc_shape, grid, nk, zero_first=False, vmem_mb=48,
        addend=None):
    out_f32 = out_shape.dtype == F32

    def body(a_ref, b_ref, *rest):
        k = pl.program_id(2)
        if addend is None:
            o_ref, scr = rest[0], rest[1:]
        else:
            add_ref, o_ref, scr = rest[0], rest[1], rest[2:]

        def compute():
            part = _dot(a_ref[...], b_ref[...], mode)
            if nk == 1:
                if addend is not None:
                    part = part + addend[0] * add_ref[...]
                o_ref[...] = part.astype(o_ref.dtype)
                return
            acc = o_ref if out_f32 else scr[0]

            @pl.when(k == 0)
            def _():
                acc[...] = part if addend is None else part + addend[0] * add_ref[...]

            @pl.when(k > 0)
            def _():
                acc[...] += part

            if not out_f32:
                @pl.when(k == nk - 1)
                def _():
                    o_ref[...] = acc[...].astype(o_ref.dtype)

        if zero_first:
            @pl.when(pl.program_id(1) == 0)
            def _():
                o_ref[...] = jnp.zeros(o_ref.shape, o_ref.dtype)

            pl.when(pl.program_id(1) > 0)(compute)
        else:
            compute()

    scratch = [] if (nk == 1 or out_f32) else [pltpu.VMEM(acc_shape, F32)]
    operands = [a, b] if addend is None else [a, b, addend[1]]
    specs = list(in_specs) if addend is None else list(in_specs) + [out_spec]
    return _call(
        body, name=name, grid=grid, in_specs=specs, out_specs=out_spec, out_shape=out_shape,
        scratch_shapes=scratch, compiler_params=_cp(vmem_mb),
    )(*operands)


def _is_small_weight(wc):
    return wc.size * 2 <= SMALL_WEIGHT_BYTES


def _all_shards(w_ref):
    return jnp.concatenate([w_ref[j] for j in range(N_SHARD)], axis=1)


def mm_cols_fwd(name, a, wc, out_dtype, pad_blocks=0, part=(0, 1)):
    s, k = a.shape
    s //= part[1]
    n4 = wc.shape[2]
    tm = _row_tile(s) if pad_blocks else _mm_tile(s)
    nt = s // tm
    first_block = part[0] * nt
    if _is_small_weight(wc) and not pad_blocks:
        def body(a_ref, w_ref, o_ref):
            o_ref[...] = _dot(a_ref[...], _all_shards(w_ref), "nn").astype(o_ref.dtype)

        return _call(
            body, name=name, grid=(nt,),
            in_specs=[pl.BlockSpec((tm, k), lambda i: (first_block + i, 0)), _full(wc.shape)],
            out_specs=pl.BlockSpec((tm, N_SHARD * n4), lambda i: (i, 0)),
            out_shape=jax.ShapeDtypeStruct((s, N_SHARD * n4), out_dtype), compiler_params=_cp(),
        )(a, wc)
    return _mm(
        name, "nn", a, wc,
        [pl.BlockSpec((tm, k), lambda j, i, r: (first_block + jnp.maximum(i - pad_blocks, 0), 0)),
         pl.BlockSpec((None, k, n4), lambda j, i, r: (j, 0, 0))],
        jax.ShapeDtypeStruct((s + pad_blocks * tm, N_SHARD * n4), out_dtype),
        pl.BlockSpec((tm, n4), lambda j, i, r: (i, j)),
        None, (N_SHARD, nt + pad_blocks, 1), 1, zero_first=pad_blocks > 0)


def mm_cols_dx(name, dy, wc, addend=None):
    s = dy.shape[0]
    _, k, n4 = wc.shape
    tm = _mm_tile(s)
    if _is_small_weight(wc):
        def body(dy_ref, w_ref, *rest):
            part = _dot(dy_ref[...], _all_shards(w_ref), "nt")
            rest[-1][...] = part if addend is None else part + addend[0] * rest[0][...]

        out_spec = pl.BlockSpec((tm, k), lambda i: (i, 0))
        extra, extra_specs = ([], []) if addend is None else ([addend[1]], [out_spec])
        return _call(
            body, name=name, grid=(s // tm,),
            in_specs=[pl.BlockSpec((tm, N_SHARD * n4), lambda i: (i, 0)), _full(wc.shape)] + extra_specs,
            out_specs=out_spec, out_shape=jax.ShapeDtypeStruct((s, k), F32), compiler_params=_cp(),
        )(dy, wc, *extra)
    return _mm(
        name, "nt", dy, wc,
        [pl.BlockSpec((tm, n4), lambda g, i, r: (i, r)),
         pl.BlockSpec((None, k, n4), lambda g, i, r: (r, 0, 0))],
        jax.ShapeDtypeStruct((s, k), F32),
        pl.BlockSpec((tm, k), lambda g, i, r: (i, 0)),
        (tm, k), (1, s // tm, N_SHARD), N_SHARD, addend=addend)


def mm_cols_dw(name, a, dy, part=(0, 1)):
    s, k = a.shape
    s //= part[1]
    n4 = dy.shape[1] // N_SHARD
    tm = _mm_tile(s)
    nt = s // tm
    first_block = part[0] * nt
    if k * n4 * N_SHARD * 2 <= SMALL_WEIGHT_BYTES:
        def body(a_ref, dy_ref, o_ref):
            full = _dot(a_ref[...], dy_ref[...], "tn")
            first = pl.program_id(0) == 0
            for j in range(N_SHARD):
                _acc_add(o_ref.at[j], first, full[:, j * n4:(j + 1) * n4])

        return _call(
            body, name=name, grid=(nt,),
            in_specs=[pl.BlockSpec((tm, k), lambda r: (first_block + r, 0)),
                      pl.BlockSpec((tm, N_SHARD * n4), lambda r: (r, 0))],
            out_specs=_full((N_SHARD, k, n4)),
            out_shape=jax.ShapeDtypeStruct((N_SHARD, k, n4), F32), compiler_params=_cp(),
        )(a, dy)
    return _mm(
        name, "tn", a, dy,
        [pl.BlockSpec((tm, k), lambda j, g, r: (first_block + r, 0)),
         pl.BlockSpec((tm, n4), lambda j, g, r: (r, j))],
        jax.ShapeDtypeStruct((N_SHARD, k, n4), F32),
        pl.BlockSpec((None, k, n4), lambda j, g, r: (j, 0, 0)),
        (k, n4), (N_SHARD, 1, nt), nt)


def _k_tile(k):
    return k if k <= 1024 else k // 2


def mm_rows_fwd(name, a, wr, out_dtype=F32):
    s, k = a.shape
    n = wr.shape[1]
    tm = min(2 * _mm_tile(s), s // 2)
    tk = _k_tile(k)
    nk = k // tk
    return _mm(
        name, "nn", a, wr,
        [pl.BlockSpec((tm, tk), lambda g, i, r: (i, r)),
         pl.BlockSpec((tk, n), lambda g, i, r: (r, 0))],
        jax.ShapeDtypeStruct((s, n), out_dtype),
        pl.BlockSpec((tm, n), lambda g, i, r: (i, 0)),
        (tm, n), (1, s // tm, nk), nk)


def mm_rows_dx(name, dy, wr, out_dtype=F32):
    s, n = dy.shape
    k = wr.shape[0]
    tm = _mm_tile(s)
    tk = _k_tile(k)
    return _mm(
        name, "nt", dy, wr,
        [pl.BlockSpec((tm, n), lambda j, i, r: (i, 0)),
         pl.BlockSpec((tk, n), lambda j, i, r: (j, 0))],
        jax.ShapeDtypeStruct((s, k), out_dtype),
        pl.BlockSpec((tm, tk), lambda j, i, r: (i, j)),
        None, (k // tk, s // tm, 1), 1)


def mm_rows_dw(name, a, dy):
    s, k = a.shape
    n = dy.shape[1]
    tm = _mm_tile(s)
    tk = _k_tile(k)
    nt = s // tm
    return _mm(
        name, "tn", a, dy,
        [pl.BlockSpec((tm, tk), lambda j, g, r: (r, j)),
         pl.BlockSpec((tm, n), lambda j, g, r: (r, 0))],
        jax.ShapeDtypeStruct((k, n), F32),
        pl.BlockSpec((tk, n), lambda j, g, r: (j, 0)),
        (tk, n), (k // tk, 1, nt), nt)


def _row(tm, c, col=0):
    return pl.BlockSpec((tm, c), lambda i: (i, col))


def _full(shape):
    nd = len(shape)
    return pl.BlockSpec(shape, lambda i: (0,) * nd)


def _prev(tm, h, c, col=0):
    return pl.BlockSpec((h, c), lambda i: (jnp.maximum(i * (tm // h) - 1, 0), col))


def _next(tm, h, c, s, col=0):
    return pl.BlockSpec((h, c), lambda i: (jnp.minimum((i + 1) * (tm // h), s // h - 1), col))


def _acc_add(ref, first, val):
    @pl.when(first)
    def _():
        ref[...] = val

    @pl.when(jnp.logical_not(first))
    def _():
        ref[...] += val


def _colsum(v):
    return jnp.sum(v, axis=0, keepdims=True)


def _ln_stats(z):
    mu = jnp.mean(z, axis=-1, keepdims=True)
    zc = z - mu
    var = jnp.mean(zc * zc, axis=-1, keepdims=True)
    rstd = lax.rsqrt(var + LN_EPS)
    return zc * rstd, rstd


def _ln_bwd(dxhat, xhat, rstd):
    m1 = jnp.mean(dxhat, axis=-1, keepdims=True)
    m2 = jnp.mean(dxhat * xhat, axis=-1, keepdims=True)
    return rstd * (dxhat - m1 - xhat * m2)


def ln_fwd(name, x, f, g, b, ple=None, emit_y=True):
    s, d = x.shape
    tm = _row_tile(s)
    n_in = 2 + (3 if ple is not None else 0)

    def body(*refs):
        x_ref, f_ref = refs[0], refs[1]
        g_ref, b_ref = refs[n_in], refs[n_in + 1]
        xh_ref, rs_ref = refs[-2:]
        z = ALPHA * x_ref[...] + f_ref[...]
        if ple is not None:
            pgl_ref, pp_ref, bg_ref = refs[2:5]
            z = z + _sig(pgl_ref[...] + bg_ref[...]) * pp_ref[...]
        xhat, rstd = _ln_stats(z)
        if emit_y:
            y = xhat * g_ref[...] + b_ref[...]
            refs[n_in + 2][...] = y
            refs[n_in + 3][...] = y.astype(BF16)
        xh_ref[...] = xhat
        rs_ref[...] = jnp.broadcast_to(rstd, rs_ref.shape)

    ins = [x, f]
    specs = [_row(tm, d), _row(tm, d)]
    if ple is not None:
        pgl, pp, bg = ple
        ins += [pgl, pp, bg]
        specs += [_row(tm, d), _row(tm, d), _full((1, d))]
    ins += [g, b]
    specs += [_full((1, d)), _full((1, d))]
    y_shapes = [jax.ShapeDtypeStruct((s, d), F32), jax.ShapeDtypeStruct((s, d), BF16)] if emit_y else []
    outs = _call(
        body, name=name, grid=(s // tm,), in_specs=specs,
        out_specs=[_row(tm, d)] * (len(y_shapes) + 1) + [_row(tm, LANES)],
        out_shape=y_shapes + [jax.ShapeDtypeStruct((s, d), F32), jax.ShapeDtypeStruct((s, LANES), F32)],
        compiler_params=_cp(),
    )(*ins)
    return tuple(outs) if emit_y else (None, None, outs[0], outs[1])


def ln_bwd(name, parts, xhat, rstd, g, ple=None, loss=None):
    s, d = xhat.shape
    tm = _row_tile(s)
    coefs = [c for c, _ in parts]
    n_p = len(parts)
    n_ple = 3 if ple is not None else 0
    n_in = n_p + 3 + n_ple + (2 if loss is not None else 0)

    def body(*refs):
        first = pl.program_id(0) == 0
        xh = refs[n_p][...]
        rs = refs[n_p + 1][:, 0:1]
        g_v = refs[n_p + 2][...]
        outs = refs[n_in:]
        if loss is not None:
            t_ref, b_ref = refs[n_p + 3 + n_ple:n_p + 5 + n_ple]
            err = (xh * g_v + b_ref[...]) - t_ref[...]
            dy = err * (1.0 / d)
            part = 0.5 * jnp.sum(jnp.mean(err * err, axis=-1, keepdims=True), axis=0, keepdims=True)
            _acc_add(outs[-1], first, jnp.broadcast_to(part, outs[-1].shape))
        else:
            dy = coefs[0] * refs[0][...].astype(F32)
            for j in range(1, n_p):
                dy = dy + coefs[j] * refs[j][...].astype(F32)
        dz = _ln_bwd(dy * g_v, xh, rs)
        outs[0][...] = dz
        _acc_add(outs[1], first, _colsum(dy * xh))
        _acc_add(outs[2], first, _colsum(dy))
        if ple is not None:
            pgl_ref, pp_ref, bg_ref = refs[n_p + 3:n_p + 6]
            pg = _sig(pgl_ref[...] + bg_ref[...])
            dpgl = dz * pp_ref[...] * pg * (1.0 - pg)
            outs[3][...] = (dz * pg).astype(BF16)
            outs[4][...] = dpgl.astype(BF16)
            _acc_add(outs[5], first, _colsum(dpgl))

    ins = [p for _, p in parts] + [xhat, rstd, g]
    specs = [_row(tm, d)] * n_p + [_row(tm, d), _row(tm, LANES), _full((1, d))]
    out_specs = [_row(tm, d), _full((1, d)), _full((1, d))]
    out_shape = [jax.ShapeDtypeStruct((s, d), F32), jax.ShapeDtypeStruct((1, d), F32),
                 jax.ShapeDtypeStruct((1, d), F32)]
    if ple is not None:
        pgl, pp, bg = ple
        ins += [pgl, pp, bg]
        specs += [_row(tm, d), _row(tm, d), _full((1, d))]
        out_specs += [_row(tm, d), _row(tm, d), _full((1, d))]
        out_shape += [jax.ShapeDtypeStruct((s, d), BF16), jax.ShapeDtypeStruct((s, d), BF16),
                      jax.ShapeDtypeStruct((1, d), F32)]
    if loss is not None:
        target, b = loss
        ins += [target, b]
        specs += [_row(tm, d), _full((1, d))]
        out_specs += [_full((8, LANES))]
        out_shape += [jax.ShapeDtypeStruct((8, LANES), F32)]
    return _call(
        body, name=name, grid=(s // tm,), in_specs=specs, out_specs=out_specs, out_shape=out_shape,
        compiler_params=_cp(),
    )(*ins)


def _fill_rotations(rot_ref, x, direction):
    n = x.shape[0]
    rot_ref[0] = x
    for b in range(1, SUBLANES):
        if direction < 0:
            rot_ref[b, SUBLANES:n, :] = x[SUBLANES - b:n - b]
        else:
            rot_ref[b, 0:n - SUBLANES, :] = x[b:n - SUBLANES + b]


def _rotated(rot_ref, start, rows, cs, direction=-1):
    b = (-start) % SUBLANES if direction < 0 else start % SUBLANES
    aligned = start + b if direction < 0 else start - b
    return rot_ref[b, pl.ds(aligned, rows), cs]


def _tile_pos(i, tm, rows):
    return (i * tm + lax.broadcasted_iota(jnp.int32, (rows, 1), 0) + 1).astype(F32)


def mixer_fwd(name, u, pool_w, pool_scale, conv_w, conv_b, cn_g, cn_b):
    s = u.shape[0]
    dp = 512
    tm = min(256, s // 4)
    h = CONV_HALO

    def body(a_c, a_p, bv_c, bv_p, bg_c, bg_p, pw_ref, ps_ref, cw_ref, cb_ref, cg_ref, cbt_ref,
             cat_ref, d_ref, e_ref, glu_ref, hh_ref, rs_ref, ext_a, rot_g, conv_out):
        i = pl.program_id(0)
        first = i == 0
        ext_a[0:h, :] = jnp.where(first, 0.0, a_p[...])
        ext_a[h:, :] = a_c[...]
        glu = bv_c[...] * _sig(bg_c[...])
        glu_ref[...] = glu
        _fill_rotations(rot_g, jnp.concatenate([jnp.where(first, 0.0, bv_p[...] * _sig(bg_p[...])), glu], axis=0), -1)
        pos = _tile_pos(i, tm, tm)
        for gi, w in enumerate(POOL_WINDOWS):
            cs = slice(gi * POOL_GROUP, (gi + 1) * POOL_GROUP)
            a_g = ext_a[pl.ds(h, tm), cs]
            acc = a_g
            for sh in range(1, w):
                acc = acc + ext_a[pl.ds(h - sh, tm), cs]
            d_g = acc / jnp.minimum(pos, float(w)) - a_g
            d_ref[:, cs] = d_g.astype(BF16)
            e_g = _dot(d_g, pw_ref[gi], "nn")
            e_ref[:, cs] = e_g
            cat_ref[:, cs] = (e_g * ps_ref[:, cs]).astype(BF16)
        for lg in range(dp // LANES):
            cs = slice(lg * LANES, (lg + 1) * LANES)
            acc = jnp.broadcast_to(cb_ref[:, cs], (tm, LANES))
            for sh in range(CONV_K):
                acc = acc + _rotated(rot_g, h - sh, tm, cs) * cw_ref[pl.ds(CONV_K - 1 - sh, 1), cs]
            conv_out[:, cs] = acc
        hhat, rstd = _ln_stats(conv_out[...])
        hl = hhat * cg_ref[...] + cbt_ref[...]
        cat_ref[:, dp:] = (hl * _sig(hl)).astype(BF16)
        hh_ref[...] = hhat
        rs_ref[...] = jnp.broadcast_to(rstd, rs_ref.shape)

    specs = [_row(tm, dp, 0), _prev(tm, h, dp, 0), _row(tm, dp, 1), _prev(tm, h, dp, 1),
             _row(tm, dp, 2), _prev(tm, h, dp, 2),
             _full((4, POOL_GROUP, POOL_GROUP)), _full((1, dp)), _full((CONV_K, dp)),
             _full((1, dp)), _full((1, dp)), _full((1, dp))]
    out_specs = [_row(tm, 2 * dp), _row(tm, dp), _row(tm, dp), _row(tm, dp), _row(tm, dp), _row(tm, LANES)]
    out_shape = [jax.ShapeDtypeStruct((s, 2 * dp), BF16), jax.ShapeDtypeStruct((s, dp), BF16),
                 jax.ShapeDtypeStruct((s, dp), F32), jax.ShapeDtypeStruct((s, dp), F32),
                 jax.ShapeDtypeStruct((s, dp), F32), jax.ShapeDtypeStruct((s, LANES), F32)]
    return _call(
        body, name=name, grid=(s // tm,), in_specs=specs, out_specs=out_specs, out_shape=out_shape,
        scratch_shapes=[pltpu.VMEM((h + tm, dp), F32), pltpu.VMEM((SUBLANES, h + tm, dp), F32),
                        pltpu.VMEM((tm, dp), F32)],
        compiler_params=_cp(),
    )(u, u, u, u, u, u, pool_w, pool_scale, conv_w, conv_b, cn_g, cn_b)


def mixer_bwd(name, dcat, u, d_sv, e_sv, glu_sv, hh_sv, rs_sv, pool_w, pool_scale, conv_w, cn_g, cn_b):
    s = u.shape[0]
    dp = 512
    tm = min(256, s // 4)
    h = CONV_HALO
    nt = s // tm

    def body(dc_c, dc_n, bv_c, bg_c, d_c, e_c, gl_c, gl_p, hh_c, hh_n, rs_c, rs_n,
             pw_ref, ps_ref, cw_ref, cg_ref, cbt_ref,
             du_ref, dpw_ref, dps_ref, dcw_ref, dcb_ref, dcg_ref, dcbt_ref,
             ext_dh, ext_g, ext_r):
        i = pl.program_id(0)
        first = i == 0
        last = i == nt - 1
        cg = cg_ref[...]

        def conv_grads(dyb, hhat, rstd):
            hl = hhat * cg + cbt_ref[...]
            sg = _sig(hl)
            dhl = dyb * (sg * (1.0 + hl * (1.0 - sg)))
            return _ln_bwd(dhl * cg, hhat, rstd), dhl

        hh_cur = hh_c[...]
        dh_c, dhl_c = conv_grads(dc_c[:, dp:], hh_cur, rs_c[:, 0:1])
        dh_n, _ = conv_grads(dc_n[:, dp:], hh_n[...], rs_n[:, 0:1])
        _fill_rotations(ext_dh, jnp.concatenate([dh_c, jnp.where(last, 0.0, dh_n)], axis=0), 1)
        _fill_rotations(ext_g, jnp.concatenate([jnp.where(first, 0.0, gl_p[...]), gl_c[...]], axis=0), -1)

        @pl.when(first)
        def _():
            dcw_ref[...] = jnp.zeros(dcw_ref.shape, F32)

        for lg in range(dp // LANES):
            cs = slice(lg * LANES, (lg + 1) * LANES)
            dglu = jnp.zeros((tm, LANES), F32)
            for sh in range(CONV_K):
                dglu = dglu + _rotated(ext_dh, sh, tm, cs, 1) * cw_ref[pl.ds(CONV_K - 1 - sh, 1), cs]
            dh_g = ext_dh[0, pl.ds(0, tm), cs]
            for sh in range(CONV_K):
                dcw_ref[pl.ds(CONV_K - 1 - sh, 1), cs] += _colsum(dh_g * _rotated(ext_g, h - sh, tm, cs))
            sgate = _sig(bg_c[:, cs])
            du_ref[:, dp + lg * LANES:dp + (lg + 1) * LANES] = dglu * sgate
            du_ref[:, 2 * dp + lg * LANES:2 * dp + (lg + 1) * LANES] = dglu * bv_c[:, cs] * sgate * (1.0 - sgate)
        _acc_add(dcb_ref, first, _colsum(dh_c))
        _acc_add(dcg_ref, first, _colsum(dhl_c * hh_cur))
        _acc_add(dcbt_ref, first, _colsum(dhl_c))

        pos_c = _tile_pos(i, tm, tm)
        pos_n = _tile_pos(i + 1, tm, h)
        _acc_add(dps_ref, first, _colsum(dc_c[:, :dp] * e_c[...]))
        for gi, w in enumerate(POOL_WINDOWS):
            cs = slice(gi * POOL_GROUP, (gi + 1) * POOL_GROUP)
            pw = pw_ref[gi]
            de_c = dc_c[:, cs] * ps_ref[:, cs]
            de_n = dc_n[:, cs] * ps_ref[:, cs]
            dd_c = _dot(de_c, pw, "nt")
            dd_n = _dot(de_n, pw, "nt")
            ext_r[0:tm, :] = dd_c / jnp.minimum(pos_c, float(w))
            ext_r[tm:, :] = jnp.where(last, 0.0, dd_n / jnp.minimum(pos_n, float(w)))
            acc = -dd_c
            for sh in range(w):
                acc = acc + ext_r[pl.ds(sh, tm), :]
            du_ref[:, cs] = acc
            dpw_g = _dot(d_c[:, cs], de_c, "tn")

            @pl.when(first)
            def _():
                dpw_ref[gi] = dpw_g

            @pl.when(jnp.logical_not(first))
            def _():
                dpw_ref[gi] += dpw_g

    specs = [_row(tm, 2 * dp), _next(tm, h, 2 * dp, s), _row(tm, dp, 1), _row(tm, dp, 2),
             _row(tm, dp), _row(tm, dp), _row(tm, dp), _prev(tm, h, dp),
             _row(tm, dp), _next(tm, h, dp, s), _row(tm, LANES), _next(tm, h, LANES, s),
             _full((4, POOL_GROUP, POOL_GROUP)), _full((1, dp)), _full((CONV_K, dp)),
             _full((1, dp)), _full((1, dp))]
    out_specs = [_row(tm, 3 * dp), _full((4, POOL_GROUP, POOL_GROUP)), _full((1, dp)), _full((CONV_K, dp)),
                 _full((1, dp)), _full((1, dp)), _full((1, dp))]
    out_shape = [jax.ShapeDtypeStruct((s, 3 * dp), F32),
                 jax.ShapeDtypeStruct((4, POOL_GROUP, POOL_GROUP), F32), jax.ShapeDtypeStruct((1, dp), F32),
                 jax.ShapeDtypeStruct((CONV_K, dp), F32), jax.ShapeDtypeStruct((1, dp), F32),
                 jax.ShapeDtypeStruct((1, dp), F32), jax.ShapeDtypeStruct((1, dp), F32)]
    return _call(
        body, name=name, grid=(nt,), in_specs=specs, out_specs=out_specs, out_shape=out_shape,
        scratch_shapes=[pltpu.VMEM((SUBLANES, tm + h, dp), F32), pltpu.VMEM((SUBLANES, h + tm, dp), F32),
                        pltpu.VMEM((tm + h, POOL_GROUP), F32)],
        compiler_params=_cp(),
    )(dcat, dcat, u, u, d_sv, e_sv, glu_sv, glu_sv, hh_sv, hh_sv, rs_sv, rs_sv,
      pool_w, pool_scale, conv_w, cn_g, cn_b)


GELU_C = math.sqrt(2.0 / math.pi)


def _gelu_parts(x):
    x2 = x * x
    t = jnp.tanh(x * (GELU_C + (GELU_C * 0.044715) * x2))
    half_1pt = 0.5 + 0.5 * t
    gelu = x * half_1pt
    dgelu = half_1pt + (0.5 * x) * (1.0 - t * t) * (GELU_C + (3.0 * GELU_C * 0.044715) * x2)
    return gelu, dgelu


def ffn_act_fwd(name, gv, dw_w, dw_b):
    s = gv.shape[0]
    dff = gv.shape[1] // 2
    tm = min(FFN_TILE, s // 4)
    h = FFN_HALO
    rc = FFN_CHUNK_ROWS
    lw = FFN_CHUNK_LANES

    def body(g_c, g_p, v_c, w_ref, b_ref, hid_ref):
        first = pl.program_id(0) == 0

        def chunk(ci, carry):
            r0 = pl.multiple_of(ci * rc, rc)
            above = pl.multiple_of(jnp.maximum(r0 - h, 0), h)
            for lg in range(dff // lw):
                cs = slice(lg * lw, (lg + 1) * lw)
                top = jnp.where(ci == 0, jnp.where(first, 0.0, g_p[:, cs]), g_c[pl.ds(above, h), cs])
                win = jnp.concatenate([top, g_c[pl.ds(r0, rc), cs]], axis=0)
                gc = jnp.broadcast_to(b_ref[:, cs], (rc, lw))
                for sh in range(FFN_K):
                    gc = gc + win[h - sh:h - sh + rc] * w_ref[pl.ds(FFN_K - 1 - sh, 1), cs]
                gelu, _ = _gelu_parts(gc)
                hid_ref[pl.ds(r0, rc), cs] = (gelu * v_c[pl.ds(r0, rc), cs]).astype(BF16)
            return carry

        lax.fori_loop(0, tm // rc, chunk, 0)

    return _call(
        body, name=name, grid=(s // tm,),
        in_specs=[_row(tm, dff, 0), _prev(tm, h, dff, 0), _row(tm, dff, 1), _full((FFN_K, dff)), _full((1, dff))],
        out_specs=_row(tm, dff), out_shape=jax.ShapeDtypeStruct((s, dff), BF16),
        compiler_params=_cp(),
    )(gv, gv, gv, dw_w, dw_b)


def ffn_act_bwd(name, dhid, gv, dw_w, dw_b):
    s = gv.shape[0]
    dff = gv.shape[1] // 2
    tm = min(FFN_TILE, s // 4)
    h = FFN_HALO
    nt = s // tm
    rc = FFN_CHUNK_ROWS
    lw = FFN_CHUNK_LANES
    n_chunks = tm // rc

    def body(dh_c, dh_n, g_p, g_c, g_n, v_c, v_n, w_ref, b_ref, dgv_ref, dw_ref, db_ref):
        i = pl.program_id(0)
        first = i == 0
        last = i == nt - 1

        @pl.when(first)
        def _():
            dw_ref[...] = jnp.zeros(dw_ref.shape, F32)
            db_ref[...] = jnp.zeros(db_ref.shape, F32)

        def chunk(ci, carry):
            r0 = pl.multiple_of(ci * rc, rc)
            above = pl.multiple_of(jnp.maximum(r0 - h, 0), h)
            below = pl.multiple_of(jnp.minimum(r0 + rc, tm - h), h)
            at_end = ci == n_chunks - 1
            for lg in range(dff // lw):
                cs = slice(lg * lw, (lg + 1) * lw)
                top = jnp.where(ci == 0, jnp.where(first, 0.0, g_p[:, cs]), g_c[pl.ds(above, h), cs])
                bot = jnp.where(at_end, g_n[:, cs], g_c[pl.ds(below, h), cs])
                win = jnp.concatenate([top, g_c[pl.ds(r0, rc), cs], bot], axis=0)
                shifted = [win[h - sh:h - sh + rc + h] for sh in range(FFN_K)]
                gc = jnp.broadcast_to(b_ref[:, cs], (rc + h, lw))
                for sh in range(FFN_K):
                    gc = gc + shifted[sh] * w_ref[pl.ds(FFN_K - 1 - sh, 1), cs]
                gelu, dgelu = _gelu_parts(gc)
                dh_mid = dh_c[pl.ds(r0, rc), cs]
                hv_bot = jnp.where(at_end, jnp.where(last, 0.0, dh_n[:, cs] * v_n[:, cs]),
                                   dh_c[pl.ds(below, h), cs] * v_c[pl.ds(below, h), cs])
                dgc = jnp.concatenate([dh_mid * v_c[pl.ds(r0, rc), cs], hv_bot], axis=0) * dgelu
                dgate = jnp.zeros((rc, lw), F32)
                for sh in range(FFN_K):
                    dgate = dgate + dgc[sh:sh + rc] * w_ref[pl.ds(FFN_K - 1 - sh, 1), cs]
                dgv_ref[pl.ds(r0, rc), cs] = dgate.astype(BF16)
                dgv_ref[pl.ds(r0, rc), slice(dff + lg * lw, dff + (lg + 1) * lw)] = (dh_mid * gelu[0:rc]).astype(BF16)
                dgc_mid = dgc[0:rc]
                for sh in range(FFN_K):
                    dw_ref[pl.ds(FFN_K - 1 - sh, 1), cs] += _colsum(dgc_mid * shifted[sh][0:rc])
                db_ref[:, cs] += _colsum(dgc_mid)
            return carry

        lax.fori_loop(0, n_chunks, chunk, 0)

    return _call(
        body, name=name, grid=(nt,),
        in_specs=[_row(tm, dff), _next(tm, h, dff, s),
                  _prev(tm, h, dff, 0), _row(tm, dff, 0), _next(tm, h, dff, s, 0),
                  _row(tm, dff, 1), _next(tm, h, dff, s, 1),
                  _full((FFN_K, dff)), _full((1, dff))],
        out_specs=[_row(tm, 2 * dff), _full((FFN_K, dff)), _full((1, dff))],
        out_shape=[jax.ShapeDtypeStruct((s, 2 * dff), BF16), jax.ShapeDtypeStruct((FFN_K, dff), F32),
                   jax.ShapeDtypeStruct((1, dff), F32)],
        compiler_params=_cp(),
    )(dhid, dhid, gv, gv, gv, gv, gv, dw_w, dw_b)


def _bias_line(rel_bias):
    nh = rel_bias.shape[0]
    line = jnp.concatenate(
        [jnp.zeros((nh, 1), rel_bias.dtype), jnp.broadcast_to(rel_bias[:, 2 * MAX_REL:], (nh, SHEAR_SAT)),
         jnp.flip(rel_bias[:, 1:2 * MAX_REL], axis=1)], axis=1)
    return line[:, None, :]


def bias_tile(name, line):
    nh = line.shape[0]

    def body(l_ref, o_ref):
        x = jnp.broadcast_to(l_ref[...], (Q_TILE, SHEAR_W))
        z = pltpu.roll(x, SHEAR_W - Q_TILE, 1, stride=1, stride_axis=0)
        qc = lax.broadcasted_iota(jnp.int32, (Q_TILE, K_WIN), 0) // CHUNK
        kc = lax.broadcasted_iota(jnp.int32, (Q_TILE, K_WIN), 1) // CHUNK
        o_ref[...] = jnp.where((kc >= qc) & (kc <= qc + LEFT_CHUNKS), z[:, :K_WIN], NEG_INF)

    return _call(
        body, name=name, grid=(nh,), in_specs=[pl.BlockSpec((None, 1, SHEAR_W), lambda hh: (hh, 0, 0))],
        out_specs=pl.BlockSpec((None, Q_TILE, K_WIN), lambda hh: (hh, 0, 0)),
        out_shape=jax.ShapeDtypeStruct((nh, Q_TILE, K_WIN), F32), compiler_params=_cp(),
    )(line)


def _stack_heads(x2, scale=None):
    if scale is not None:
        x2 = x2 * jnp.asarray(scale, x2.dtype)
    lane = lax.broadcasted_iota(jnp.int32, x2.shape, 1)
    zero = jnp.zeros_like(x2)
    return jnp.concatenate([jnp.where(lane < HEAD_DIM, x2, zero), jnp.where(lane < HEAD_DIM, zero, x2)], axis=0)


def _unstack_heads(x_st):
    lane = lax.broadcasted_iota(jnp.int32, (Q_TILE, LANES), 1)
    return jnp.where(lane < HEAD_DIM, x_st[:Q_TILE], x_st[Q_TILE:])


def _attn_probs(q_st, k3, bias_st, t):
    sc = _dot(q_st, k3, "nt") + bias_st
    col = lax.broadcasted_iota(jnp.int32, sc.shape, 1)
    sc = jnp.where(col >= PAD_ROWS - t * Q_TILE, sc, NEG_INF)
    m = jnp.max(sc, axis=-1, keepdims=True)
    p = jnp.exp(sc - m)
    return p * (1.0 / jnp.sum(p, axis=-1, keepdims=True))


def _attn_specs(d_model, pairs):
    nq = PAD_ROWS // Q_TILE
    width = pairs * LANES
    groups = d_model // width
    specs = [pl.BlockSpec((Q_TILE, width), lambda g, t: (t + nq, g))]
    for which in (1, 2):
        for j in range(K_WIN // Q_TILE):
            specs.append(pl.BlockSpec((Q_TILE, width), lambda g, t, j=j, which=which: (t + j, which * groups + g)))
    specs.append(pl.BlockSpec((2 * pairs, Q_TILE, K_WIN), lambda g, t: (g, 0, 0)))
    return specs


def attn_fwd(name, qkvp, bias):
    s = qkvp.shape[0] - PAD_ROWS
    d_model = qkvp.shape[1] // 3
    nw = K_WIN // Q_TILE

    def body(q_ref, *refs):
        k_refs, v_refs, b_ref, o_ref = refs[:nw], refs[nw:2 * nw], refs[2 * nw], refs[2 * nw + 1]
        t = pl.program_id(1)
        for j in range(ATTN_PAIRS_FWD):
            ls = slice(j * LANES, (j + 1) * LANES)
            k3 = jnp.concatenate([r[:, ls] for r in k_refs], axis=0)
            v3 = jnp.concatenate([r[:, ls] for r in v_refs], axis=0)
            bias_st = b_ref[2 * j:2 * j + 2].reshape(2 * Q_TILE, K_WIN)
            p = _attn_probs(_stack_heads(q_ref[:, ls], ATTN_SCALE), k3, bias_st, t)
            o_ref[:, ls] = _unstack_heads(_dot(p, v3, "nn")).astype(BF16)

    width = ATTN_PAIRS_FWD * LANES
    return _call(
        body, name=name, grid=(d_model // width, s // Q_TILE),
        in_specs=_attn_specs(d_model, ATTN_PAIRS_FWD), out_specs=pl.BlockSpec((Q_TILE, width), lambda g, t: (t, g)),
        out_shape=jax.ShapeDtypeStruct((s, d_model), BF16), compiler_params=_cp(),
    )(qkvp, *([qkvp] * (2 * nw)), bias)


def attn_bwd(name, qkvp, bias, do):
    s = qkvp.shape[0] - PAD_ROWS
    d_model = qkvp.shape[1] // 3
    nw = K_WIN // Q_TILE
    nt = s // Q_TILE

    def body(q_ref, *refs):
        k_refs, v_refs = refs[:nw], refs[nw:2 * nw]
        b_ref, do_ref, dq_ref, dk_ref, dv_ref, ds_ref, dk_acc, dv_acc = refs[2 * nw:]
        t = pl.program_id(1)
        first = t == 0

        @pl.when(first)
        def _():
            dk_acc[...] = jnp.zeros(dk_acc.shape, F32)
            dv_acc[...] = jnp.zeros(dv_acc.shape, F32)
            ds_ref[...] = jnp.zeros(ds_ref.shape, F32)

        start = pl.multiple_of(t * Q_TILE, Q_TILE)
        for j in range(ATTN_PAIRS):
            ls = slice(j * LANES, (j + 1) * LANES)
            q_st = _stack_heads(q_ref[:, ls], ATTN_SCALE)
            do_st = _stack_heads(do_ref[:, ls])
            k3 = jnp.concatenate([r[:, ls] for r in k_refs], axis=0)
            v3 = jnp.concatenate([r[:, ls] for r in v_refs], axis=0)
            p = _attn_probs(q_st, k3, b_ref[2 * j:2 * j + 2].reshape(2 * Q_TILE, K_WIN), t)
            dp = _dot(do_st, v3, "nt")
            ds = p * (dp - jnp.sum(p * dp, axis=-1, keepdims=True))
            ds_ref[2 * j:2 * j + 2] += ds.reshape(2, Q_TILE, K_WIN)
            dsb = ds.astype(BF16)
            dq_ref[:, ls] = (_unstack_heads(_dot(dsb, k3, "nn")) * ATTN_SCALE).astype(BF16)
            dk_acc[pl.ds(start, K_WIN), ls] += _dot(dsb, q_st, "tn")
            dv_acc[pl.ds(start, K_WIN), ls] += _dot(p, do_st, "tn")

        @pl.when(t == nt - 1)
        def _():
            dk_ref[...] = dk_acc[pl.ds(PAD_ROWS, s), :].astype(BF16)
            dv_ref[...] = dv_acc[pl.ds(PAD_ROWS, s), :].astype(BF16)

    specs = _attn_specs(d_model, ATTN_PAIRS) + [pl.BlockSpec((Q_TILE, ATTN_LANES), lambda g, t: (t, g))]
    col_spec = pl.BlockSpec((s, ATTN_LANES), lambda g, t: (0, g))
    return _call(
        body, name=name, grid=(d_model // ATTN_LANES, nt), in_specs=specs,
        out_specs=[pl.BlockSpec((Q_TILE, ATTN_LANES), lambda g, t: (t, g)), col_spec, col_spec,
                   pl.BlockSpec((2 * ATTN_PAIRS, Q_TILE, K_WIN), lambda g, t: (g, 0, 0))],
        out_shape=[jax.ShapeDtypeStruct((s, d_model), BF16)] * 3
        + [jax.ShapeDtypeStruct((N_HEADS, Q_TILE, K_WIN), F32)],
        scratch_shapes=[pltpu.VMEM((PAD_ROWS + s, ATTN_LANES), F32), pltpu.VMEM((PAD_ROWS + s, ATTN_LANES), F32)],
        compiler_params=_cp(),
    )(qkvp, *([qkvp] * (2 * nw)), bias, do)


def bias_grad_reduce(name, ds_sum):
    nh = ds_sum.shape[0]
    width = SHEAR_W + Q_TILE
    first_k = Q_TILE - 1

    def body(x_ref, col_ref, sat_ref):
        x = x_ref[...]
        hi = x.astype(BF16)
        lo = (x - hi.astype(F32)).astype(BF16)
        r = lax.broadcasted_iota(jnp.int32, (Q_TILE, Q_TILE), 0)
        c = lax.broadcasted_iota(jnp.int32, (Q_TILE, Q_TILE), 1)
        exchange = jnp.where(r + c == Q_TILE - 1, 1.0, 0.0).astype(BF16)
        x_rev = _dot(exchange, hi, "nn") + _dot(exchange, lo, "nn")
        zeros = jnp.zeros((Q_TILE, Q_TILE), F32)
        y = pltpu.roll(jnp.concatenate([zeros, x_rev, zeros], axis=1), 0, 1, stride=1, stride_axis=0)
        cols = _colsum(y)
        col_ref[...] = cols
        k = lax.broadcasted_iota(jnp.int32, cols.shape, 1) - first_k
        tot = jnp.sum(jnp.where((k >= 1) & (k <= SHEAR_SAT), cols, 0.0), axis=-1, keepdims=True)
        sat_ref[...] = jnp.broadcast_to(tot, sat_ref.shape)

    return _call(
        body, name=name, grid=(nh,),
        in_specs=[pl.BlockSpec((None, Q_TILE, K_WIN), lambda hh: (hh, 0, 0))],
        out_specs=[pl.BlockSpec((None, 1, width), lambda hh: (hh, 0, 0)),
                   pl.BlockSpec((None, 1, LANES), lambda hh: (hh, 0, 0))],
        out_shape=[jax.ShapeDtypeStruct((nh, 1, width), F32), jax.ShapeDtypeStruct((nh, 1, LANES), F32)],
        compiler_params=_cp(),
    )(ds_sum)


def _ew_rows(r, most=512, cols=None):
    if cols is not None and r * cols * 4 <= SMALL_BLOCK_BYTES:
        return r
    for cand in range(min(most, r) // 16 * 16, 0, -16):
        if r % cand == 0:
            return cand
    return r


def cast_into_gathered(name, w, layer, s_idx, n_blocks=N_SHARD, dtype=BF16, token=None):
    r, c = w.shape[-2:]
    tr = _ew_rows(r, cols=c)

    def body(s_ref, w_ref, *rest):
        rest[-1][...] = w_ref[...].astype(dtype)

    extra = [] if token is None else [token]
    grid_spec = pltpu.PrefetchScalarGridSpec(
        num_scalar_prefetch=1, grid=(r // tr,),
        in_specs=[pl.BlockSpec((None, tr, c), lambda i, s_ref: (layer, i, 0))] + [ANY_SPEC] * len(extra),
        out_specs=pl.BlockSpec((None, tr, c), lambda i, s_ref: (s_ref[0], i, 0)))
    return _call(
        body, name=name, grid_spec=grid_spec, out_shape=jax.ShapeDtypeStruct((n_blocks, r, c), dtype),
        compiler_params=_cp(),
    )(s_idx, w, *extra)


def adamw(name, w, grads, m, v, token=None):
    nl, r, c = w.shape
    tr = _ew_rows(r, 256, cols=c)

    def body(*refs):
        w_ref, m_ref, v_ref = refs[0], refs[1], refs[2]
        g_refs = refs[3:3 + nl]
        d_ref, nm_ref, nv_ref = refs[-3:]
        layer = pl.program_id(0)
        g = g_refs[0][...]
        for j in range(1, nl):
            g = jnp.where(layer == j, g_refs[j][...], g)
        d_ref[...], nm_ref[...], nv_ref[...] = _adamw_update(w_ref[...], g, m_ref[...], v_ref[...])

    p_spec = pl.BlockSpec((None, tr, c), lambda l, i: (l, i, 0))
    g_spec = pl.BlockSpec((tr, c), lambda l, i: (i, 0))
    extra = [] if token is None else [token]
    extra_specs = [] if token is None else [ANY_SPEC]
    return _call(
        body, name=name, grid=(nl, r // tr), in_specs=[p_spec] * 3 + [g_spec] * nl + extra_specs,
        out_specs=[p_spec] * 3, out_shape=[jax.ShapeDtypeStruct((nl, r, c), F32)] * 3, compiler_params=_cp(),
    )(w, m, v, *grads, *extra)


def _adamw_update(w, g, m, v):
    nm = ADAM_B1 * m + (1.0 - ADAM_B1) * g
    nv = ADAM_B2 * v + (1.0 - ADAM_B2) * (g * g)
    delta = -ADAM_LR * ((nm / ADAM_BC1) / (jnp.sqrt(nv / ADAM_BC2) + ADAM_EPS) + ADAM_WD * w)
    return delta, nm, nv


def adamw_many(name, ws, gs, ms, vs, token):
    n = len(ws)

    def body(*refs):
        ins, outs = refs[:4 * n], refs[4 * n + 1:]
        for i in range(n):
            delta, nm, nv = _adamw_update(ins[i][...], ins[n + i][...], ins[2 * n + i][...], ins[3 * n + i][...])
            outs[3 * i][...] = delta
            outs[3 * i + 1][...] = nm
            outs[3 * i + 2][...] = nv

    vmem = pl.BlockSpec(memory_space=pltpu.VMEM)
    shapes = [jax.ShapeDtypeStruct(w.shape, F32) for w in ws for _ in range(3)]
    outs = _call(
        body, name=name, in_specs=[vmem] * (4 * n) + [ANY_SPEC], out_specs=[vmem] * (3 * n), out_shape=shapes,
        compiler_params=_cp(),
    )(*ws, *gs, *ms, *vs, token)
    return [tuple(outs[3 * i:3 * i + 3]) for i in range(n)]


def sum_blocks(name, gathered, n_blocks):
    r = gathered.shape[0] // n_blocks
    c = gathered.shape[1]
    tr = r if r <= SUM_BLOCK_ROWS else _ew_rows(r)
    nt = r // tr

    def body(*refs):
        acc = refs[0][...]
        for j in range(1, n_blocks):
            acc = acc + refs[j][...]
        refs[-1][...] = acc

    specs = [pl.BlockSpec((tr, c), lambda i, j=j: (j * nt + i, 0)) for j in range(n_blocks)]
    return _call(
        body, name=name, grid=(nt,), in_specs=specs, out_specs=pl.BlockSpec((tr, c), lambda i: (i, 0)),
        out_shape=jax.ShapeDtypeStruct((r, c), F32), compiler_params=_cp(),
    )(*([gathered] * n_blocks))


def _place():
    return lax.axis_index("x"), lax.axis_index("y"), lax.axis_index("c")


def _other_chips(x, y):
    return [(1 - x, y), (x, 1 - y), (1 - x, 1 - y)]


HBM_SPEC = pl.BlockSpec(memory_space=pltpu.HBM)
SEM_SPEC = pl.BlockSpec(memory_space=pltpu.SEMAPHORE)
ANY_SPEC = pl.BlockSpec(memory_space=pl.ANY)
EFFECT = pltpu.SideEffectType.DATAFLOW_SIDE_EFFECTING


def copies_start(name, bufs, plan, n_copies):
    n = len(bufs)

    def body(*refs):
        send, recv = refs[n], refs[n + 1]
        token = refs[2 * n + 2]
        for k, (src, dst, peer, _) in enumerate(plan(refs[:n])):
            pltpu.make_async_remote_copy(
                src_ref=src, dst_ref=dst, send_sem=send.at[k], recv_sem=recv.at[k],
                device_id=peer, device_id_type=MESH).start()
        token[...] = jnp.zeros(token.shape, F32)

    outs = pl.pallas_call(
        body, name=name,
        out_shape=(pltpu.SemaphoreType.DMA((n_copies,)), pltpu.SemaphoreType.DMA((n_copies,)),
                   *[pltpu.HBM(b.shape, b.dtype) for b in bufs], jax.ShapeDtypeStruct((8, LANES), F32)),
        in_specs=[HBM_SPEC] * n,
        out_specs=(SEM_SPEC, SEM_SPEC, *([HBM_SPEC] * n), pl.BlockSpec(memory_space=pltpu.VMEM)),
        input_output_aliases={a: a + 2 for a in range(n)},
        compiler_params=pltpu.CompilerParams(has_side_effects=EFFECT),
    )(*[_in_hbm(b) for b in bufs])
    return outs[0], outs[1], list(outs[2:2 + n]), outs[2 + n]


def copies_wait(name, bufs, send, recv, plan, sem_base, after):
    n = len(bufs)

    def body(*refs):
        send_ref, recv_ref = refs[n], refs[n + 1]
        for k, (src, _, peer, land) in enumerate(plan(refs[:n])):
            cp = pltpu.make_async_remote_copy(
                src_ref=src, dst_ref=land, send_sem=send_ref.at[sem_base + k], recv_sem=recv_ref.at[sem_base + k],
                device_id=peer, device_id_type=MESH)
            cp.wait_send()
            cp.wait_recv()

    outs = pl.pallas_call(
        body, name=name,
        out_shape=tuple(pltpu.HBM(b.shape, b.dtype) for b in bufs),
        in_specs=[HBM_SPEC] * n + [SEM_SPEC, SEM_SPEC, ANY_SPEC], out_specs=tuple([HBM_SPEC] * n),
        input_output_aliases={a: a for a in range(n)},
        compiler_params=pltpu.CompilerParams(has_side_effects=EFFECT),
    )(*bufs, send, recv, after)
    return list(outs)


def gather_plan(refs):
    x, y, c = _place()
    me = 2 * x + y
    return [(buf.at[me], buf.at[me], (cx, cy, c), buf.at[2 * cx + cy])
            for buf in refs for cx, cy in _other_chips(x, y)]


def all_plan(refs):
    x, y, c = _place()
    me = 4 * x + 2 * y + c
    out = []
    for buf in refs:
        for flip in range(1, 8):
            px = 1 - x if flip & 4 else x
            py = 1 - y if flip & 2 else y
            pc = 1 - c if flip & 1 else c
            out.append((buf.at[me], buf.at[me], (px, py, pc), buf.at[4 * px + 2 * py + pc]))
    return out


def swap_plan(refs):
    x, y, c = _place()
    n = len(refs) // 2
    out = []
    for g, land in zip(refs[:n], refs[n:]):
        hr = g.shape[1] // 2
        out.append((g.at[:, pl.ds((1 - c) * hr, hr)], land, (x, y, 1 - c), land))
    return out


def owners_plan(refs):
    x, y, c = _place()
    n = len(refs) // 2
    return [(src.at[2 * cx + cy], land.at[j], (cx, cy, c), land.at[j])
            for src, land in zip(refs[:n], refs[n:]) for j, (cx, cy) in enumerate(_other_chips(x, y))]


def join_plan(refs):
    x, y, c = _place()
    out = []
    for buf in refs:
        hr = buf.shape[0] // 2
        mine = buf.at[pl.ds(c * hr, hr)]
        out.append((mine, mine, (x, y, 1 - c), buf.at[pl.ds((1 - c) * hr, hr)]))
    return out


def add_halves(name, grad, landed, sc_idx):
    _, r, c = grad.shape
    hr = r // 2
    tr = _ew_rows(hr)
    nt = hr // tr

    def body(sc_ref, g_ref, l_ref, own_ref, wire_ref):
        tot = g_ref[...] + l_ref[...]
        wire_ref[...] = tot.astype(BF16)

        @pl.when(pl.program_id(1) == sc_ref[0])
        def _():
            own_ref[...] = tot

    grid_spec = pltpu.PrefetchScalarGridSpec(
        num_scalar_prefetch=1, grid=(nt, N_SHARD),
        in_specs=[pl.BlockSpec((None, tr, c), lambda i, sh, sc_ref: (sh, sc_ref[1] * nt + i, 0)),
                  pl.BlockSpec((None, tr, c), lambda i, sh, sc_ref: (sh, i, 0))],
        out_specs=[pl.BlockSpec((tr, c), lambda i, sh, sc_ref: (i, 0)),
                   pl.BlockSpec((None, tr, c), lambda i, sh, sc_ref: (sh, i, 0))])
    return _call(
        body, name=name, grid_spec=grid_spec,
        out_shape=[jax.ShapeDtypeStruct((hr, c), F32), jax.ShapeDtypeStruct((N_SHARD, hr, c), BF16)],
        compiler_params=_cp(),
    )(sc_idx, grad, landed)


def add_owned(name, own, landed, sc_idx):
    hr, c = own.shape
    tr = _ew_rows(hr)
    nt = hr // tr

    def body(sc_ref, o_ref, l0, l1, l2, out_ref):
        out_ref[...] = ((o_ref[...] + l0[...].astype(F32)) + l1[...].astype(F32)) + l2[...].astype(F32)

    grid_spec = pltpu.PrefetchScalarGridSpec(
        num_scalar_prefetch=1, grid=(nt,),
        in_specs=[pl.BlockSpec((tr, c), lambda i, sc_ref: (i, 0))]
        + [pl.BlockSpec((None, tr, c), lambda i, sc_ref, j=j: (j, i, 0)) for j in range(3)],
        out_specs=pl.BlockSpec((tr, c), lambda i, sc_ref: (sc_ref[1] * nt + i, 0)))
    return _call(
        body, name=name, grid_spec=grid_spec, out_shape=jax.ShapeDtypeStruct((2 * hr, c), F32),
        compiler_params=_cp(),
    )(sc_idx, own, landed, landed, landed)


PACK_QUANTUM = 8 * LANES


def _pack(arrays):
    pieces = []
    for a in arrays:
        flat = a.reshape(-1)
        padded = -(-flat.shape[0] // PACK_QUANTUM) * PACK_QUANTUM
        pieces.append(jnp.pad(flat, (0, padded - flat.shape[0])).reshape(-1, LANES))
    return jnp.concatenate(pieces, axis=0)


def _unpack(packed, shapes):
    out = []
    row = 0
    for shp in shapes:
        size = math.prod(shp)
        rows = -(-size // PACK_QUANTUM) * 8
        out.append(packed[row:row + rows].reshape(-1)[:size].reshape(shp))
        row += rows
    return out


def kernel(x, p, mix_w_in, pool_w, pool_scale, conv_dw_w, conv_dw_b, conv_ln_g, conv_ln_b, mix_w_out, attn_w_qkv, attn_rel_bias, attn_w_o, ln_mix_g, ln_mix_b, ffn_w_up, ffn_dw_w, ffn_dw_b, ffn_w_down, ple_w_proj, ple_w_gate, ple_b_gate, ln_ffn_g, ln_ffn_b, loss_target, m_mix_w_in, m_pool_w, m_pool_scale, m_conv_dw_w, m_conv_dw_b, m_conv_ln_g, m_conv_ln_b, m_mix_w_out, m_attn_w_qkv, m_attn_rel_bias, m_attn_w_o, m_ln_mix_g, m_ln_mix_b, m_ffn_w_up, m_ffn_dw_w, m_ffn_dw_b, m_ffn_w_down, m_ple_w_proj, m_ple_w_gate, m_ple_b_gate, m_ln_ffn_g, m_ln_ffn_b, v_mix_w_in, v_pool_w, v_pool_scale, v_conv_dw_w, v_conv_dw_b, v_conv_ln_g, v_conv_ln_b, v_mix_w_out, v_attn_w_qkv, v_attn_rel_bias, v_attn_w_o, v_ln_mix_g, v_ln_mix_b, v_ffn_w_up, v_ffn_dw_w, v_ffn_dw_b, v_ffn_w_down, v_ple_w_proj, v_ple_w_gate, v_ple_b_gate, v_ln_ffn_g, v_ln_ffn_b):
    xi, yi, ci = _place()
    shard_idx = (2 * xi + yi).astype(jnp.int32)
    s_arr = shard_idx.reshape(1)
    c_arr = ci.astype(jnp.int32).reshape(1)
    sc_arr = jnp.concatenate([s_arr, c_arr])

    x0 = x[0]
    target = loss_target[0]
    p_rows = p.reshape(p.shape[0] * p.shape[2], p.shape[3])
    seq = x0.shape[0]

    big = [
        ("mix_w_in", mix_w_in, m_mix_w_in, v_mix_w_in, True),
        ("mix_w_out", mix_w_out, m_mix_w_out, v_mix_w_out, False),
        ("attn_w_qkv", attn_w_qkv, m_attn_w_qkv, v_attn_w_qkv, True),
        ("attn_w_o", attn_w_o, m_attn_w_o, v_attn_w_o, False),
        ("ffn_w_up", ffn_w_up, m_ffn_w_up, v_ffn_w_up, True),
        ("ffn_w_down", ffn_w_down, m_ffn_w_down, v_ffn_w_down, False),
        ("ple_w_proj", ple_w_proj, m_ple_w_proj, v_ple_w_proj, True),
        ("ple_w_gate", ple_w_gate, m_ple_w_gate, v_ple_w_gate, False),
    ]
    params = {nm: w for nm, w, _, _, _ in big}
    col_sharded = {nm: cs for nm, _, _, _, cs in big}
    keys = [("mix_w_in", 0), ("mix_w_out", 0), ("ffn_w_up", 0), ("ffn_w_down", 0), ("ple_w_gate", 0),
            ("ple_w_proj", 0), ("attn_w_qkv", 0), ("attn_w_o", 0), ("ffn_w_up", 1), ("ffn_w_down", 1),
            ("ple_w_gate", 1), ("ple_w_proj", 1)]
    dw_shapes = [conv_dw_w.shape, ffn_dw_w.shape]
    dw_block = cast_into_gathered("place_dw", _pack([conv_dw_w, ffn_dw_w])[None], 0, s_arr, dtype=F32)
    n_first = 2
    started = {}
    gather_token = None
    for tag, group in (("first", keys[:n_first]), ("rest", keys[n_first:])):
        shards = [cast_into_gathered(f"cast_{nm}_{layer}", params[nm], layer, s_arr, token=gather_token)
                  for nm, layer in group]
        if tag == "first":
            shards.append(dw_block)
        send, recv, bufs, gather_token = copies_start(f"gather_start_{tag}", shards, gather_plan, 3 * len(shards))
        for a, key in enumerate(group):
            started[key] = (send, recv, bufs[a], 3 * a)
        if tag == "first":
            dw_started = (send, recv, bufs[-1], 3 * len(group))
    arrived_w = {}

    def weight(nm, layer, after=None):
        key = (nm, layer)
        if key not in arrived_w:
            send, recv, buf, base = started[key]
            arrived_w[key] = copies_wait(f"gather_wait_{nm}_{layer}", [buf], send, recv, gather_plan, base, after)[0]
        g = arrived_w[key]
        if col_sharded[nm]:
            return g
        return g.reshape(g.shape[0] * g.shape[1], g.shape[2])

    def tie(a, token):
        return a + token[0:1, 0:1].astype(a.dtype)

    class Reducer:
        def __init__(self, tag, group):
            self.tag, self.group, self.stage = tag, group, 0
            self.n = len(group)
            self.result = None

        def advance(self, after):
            tag, n = self.tag, self.n
            if self.stage == 0:
                grads = []
                for key in self.group:
                    g = big_grads[key]
                    grads.append(g if g.ndim == 3 else g.reshape(N_SHARD, g.shape[0] // N_SHARD, g.shape[1]))
                lands = [lax.empty((N_SHARD, g.shape[1] // 2, g.shape[2]), F32) for g in grads]
                self.sems = copies_start(f"swap_start_{tag}", grads + lands, swap_plan, n)
            elif self.stage == 1:
                send, recv, bufs, _ = self.sems
                outs = copies_wait(f"swap_wait_{tag}", bufs, send, recv, swap_plan, 0, after)
                self.own, wire = [], []
                for key, g, ld in zip(self.group, outs[:n], outs[n:]):
                    o, ob = add_halves(f"add_halves_{key[0]}_{key[1]}", g, ld, sc_arr)
                    self.own.append(o)
                    wire.append(ob)
                lands = [lax.empty((3,) + w.shape[1:], BF16) for w in wire]
                self.sems = copies_start(f"owners_start_{tag}", wire + lands, owners_plan, 3 * n)
            elif self.stage == 2:
                send, recv, bufs, _ = self.sems
                outs = copies_wait(f"owners_wait_{tag}", bufs, send, recv, owners_plan, 0, after)
                finals = [add_owned(f"add_owned_{key[0]}_{key[1]}", o, ar, sc_arr)
                          for key, o, ar in zip(self.group, self.own, outs[n:])]
                self.sems = copies_start(f"join_start_{tag}", finals, join_plan, n)
            elif self.stage == 3:
                send, recv, bufs, _ = self.sems
                outs = copies_wait(f"join_wait_{tag}", bufs, send, recv, join_plan, 0, after)
                self.result = dict(zip(self.group, outs))
                self.sems = None
            self.stage += 1
            return None if self.sems is None else self.sems[3]

    dw_cache = []

    def conv_weights(after):
        if not dw_cache:
            send, recv, buf, base = dw_started
            dw_all = copies_wait("gather_wait_dw", [buf], send, recv, gather_plan, base, after)[0]
            dw_parts = [_unpack(dw_all[k], dw_shapes) for k in range(N_SHARD)]
            dw_cache.append(jnp.concatenate([pc[0] for pc in dw_parts], axis=2)[0])
            dw_cache.append(jnp.concatenate([pc[1] for pc in dw_parts], axis=2))
        return dw_cache

    big_grads = {}
    small_grads = {}

    saved = []
    h_in = x0
    h_in_b = x0
    for layer in range(N_LAYERS):
        sv = {"x_in": h_in_b}
        if layer % 2 == 0:
            u = mm_cols_fwd("mix_in", h_in_b, weight("mix_w_in", 0, gather_token), F32)
            conv_w_full, ffn_dw_full = conv_weights(u)
            cat, d_sv, e_sv, glu_sv, hh_sv, rs_sv = mixer_fwd(
                "mixer_fwd", u, pool_w[0], pool_scale, conv_w_full, conv_dw_b, conv_ln_g, conv_ln_b)
            mix = mm_rows_fwd("mix_out", cat, weight("mix_w_out", 0, cat))
            sv.update(u=u, cat=cat, d=d_sv, e=e_sv, glu=glu_sv, hh=hh_sv, rs=rs_sv)
        else:
            qkvp = mm_cols_fwd("attn_qkv", h_in_b, weight("attn_w_qkv", 0, h_in_b), BF16,
                               pad_blocks=PAD_ROWS // _row_tile(seq))
            bias = bias_tile("bias_tile", _bias_line(attn_rel_bias[0]))
            att = attn_fwd("attn_fwd", qkvp, bias)
            mix = mm_rows_fwd("attn_out", att, weight("attn_w_o", 0, att))
            sv.update(qkvp=qkvp, bias=bias, att=att)
        x1, x1_b, xh1, rs1 = ln_fwd(f"ln_mix_{layer}", h_in, mix, ln_mix_g[layer:layer + 1],
                                    ln_mix_b[layer:layer + 1])
        gv = mm_cols_fwd(f"ffn_up_{layer}", x1_b, weight("ffn_w_up", layer, x1_b), F32)
        hid = ffn_act_fwd(f"ffn_act_{layer}", gv, ffn_dw_full[layer], ffn_dw_b[layer:layer + 1])
        ffn = mm_rows_fwd(f"ffn_down_{layer}", hid, weight("ffn_w_down", layer, hid))
        pgl = mm_rows_fwd(f"ple_gate_{layer}", x1_b, weight("ple_w_gate", layer, ffn))
        pp = mm_cols_fwd(f"ple_proj_{layer}", p_rows, weight("ple_w_proj", layer, pgl), F32, part=(layer, N_LAYERS))
        bg = ple_b_gate[layer:layer + 1]
        x2, x2_b, xh2, rs2 = ln_fwd(f"ln_ffn_{layer}", x1, ffn, ln_ffn_g[layer:layer + 1], ln_ffn_b[layer:layer + 1],
                                    ple=(pgl, pp, bg), emit_y=layer < N_LAYERS - 1)
        sv.update(x1=x1_b, xh1=xh1, rs1=rs1, gv=gv, hid=hid, pgl=pgl, pp=pp, xh2=xh2, rs2=rs2)
        saved.append(sv)
        h_in, h_in_b = x2, x2_b

    reducers = []

    def open_group(tag, group):
        reducers.append(Reducer(tag, group))
        return reducers[-1].advance(None)

    def hook(after):
        token = None
        for red in reducers:
            if red.stage < 4:
                tk = red.advance(after)
                if tk is not None:
                    token = tk if token is None else token + tk
        return token

    def tied(a, token):
        return a if token is None else tie(a, token)

    parts = []
    token = None
    for layer in reversed(range(N_LAYERS)):
        sv = saved[layer]
        bg = ple_b_gate[layer:layer + 1]
        if layer == 0:
            token = open_group("layer1", [("attn_w_qkv", 0), ("attn_w_o", 0), ("ffn_w_up", 1), ("ffn_w_down", 1),
                                          ("ple_w_gate", 1), ("ple_w_proj", 1)])
        last = layer == N_LAYERS - 1
        res = ln_bwd(
            f"ln_ffn_bwd_{layer}", parts, sv["xh2"], sv["rs2"], tied(ln_ffn_g[layer:layer + 1], token),
            ple=(sv["pgl"], sv["pp"], bg), loss=(target, ln_ffn_b[layer:layer + 1]) if last else None)
        dz2, dg2, db2, dpp, dpgl, dbg = res[:6]
        if last:
            loss_part = res[6]
        small_grads[("ln_ffn_g", layer)] = dg2
        small_grads[("ln_ffn_b", layer)] = db2
        small_grads[("ple_b_gate", layer)] = dbg
        w_down = weight("ffn_w_down", layer)
        dhid = mm_rows_dx(f"ffn_down_dx_{layer}", dz2, w_down)
        big_grads[("ffn_w_down", layer)] = mm_rows_dw(f"ffn_down_dw_{layer}", sv["hid"], dz2)
        token = hook(big_grads[("ffn_w_down", layer)])
        dgv, ddw, ddb = ffn_act_bwd(f"ffn_act_bwd_{layer}", dhid, sv["gv"], ffn_dw_full[layer],
                                    tied(ffn_dw_b[layer:layer + 1], token))
        small_grads[("ffn_dw_w", layer)] = ddw
        small_grads[("ffn_dw_b", layer)] = ddb
        big_grads[("ffn_w_up", layer)] = mm_cols_dw(f"ffn_up_dw_{layer}", sv["x1"], dgv)
        t_up = mm_cols_dx(f"ffn_up_dx_{layer}", dgv, weight("ffn_w_up", layer))
        token = hook(t_up)
        big_grads[("ple_w_gate", layer)] = mm_rows_dw(f"ple_gate_dw_{layer}", sv["x1"], dpgl)
        t_gate = mm_rows_dx(f"ple_gate_dx_{layer}", dpgl, weight("ple_w_gate", layer))
        big_grads[("ple_w_proj", layer)] = mm_cols_dw(f"ple_proj_dw_{layer}", p_rows, dpp, part=(layer, N_LAYERS))
        token2 = hook(big_grads[("ple_w_proj", layer)])
        if token2 is not None:
            token = token2 if token is None else token + token2
        if layer == 0:
            token3 = open_group("layer0_ffn", [("ffn_w_up", 0), ("ffn_w_down", 0), ("ple_w_gate", 0), ("ple_w_proj", 0)])
            token = token3 if token is None else token + token3
        dz1, dg1, db1 = ln_bwd(
            f"ln_mix_bwd_{layer}", [(ALPHA, dz2), (1.0, t_up), (1.0, t_gate)], sv["xh1"], sv["rs1"],
            tied(ln_mix_g[layer:layer + 1], token))
        small_grads[("ln_mix_g", layer)] = dg1
        small_grads[("ln_mix_b", layer)] = db1
        if layer % 2 == 0:
            dcat = mm_rows_dx("mix_out_dx", dz1, weight("mix_w_out", 0))
            big_grads[("mix_w_out", 0)] = mm_rows_dw("mix_out_dw", sv["cat"], dz1)
            token = hook(big_grads[("mix_w_out", 0)])
            du, dpw, dps, dcw, dcb, dcg, dcbt = mixer_bwd(
                "mixer_bwd", dcat, sv["u"], sv["d"], sv["e"], sv["glu"], sv["hh"], sv["rs"],
                pool_w[0], pool_scale, conv_w_full, tied(conv_ln_g, token), conv_ln_b)
            small_grads[("pool_w", 0)] = dpw
            small_grads[("pool_scale", 0)] = dps
            small_grads[("conv_dw_w", 0)] = dcw
            small_grads[("conv_dw_b", 0)] = dcb
            small_grads[("conv_ln_g", 0)] = dcg
            small_grads[("conv_ln_b", 0)] = dcbt
            big_grads[("mix_w_in", 0)] = mm_cols_dw("mix_in_dw", sv["x_in"], du)
            hook(big_grads[("mix_w_in", 0)])
            open_group("layer0_mix", [("mix_w_in", 0), ("mix_w_out", 0)])
            dx_in = mm_cols_dx("mix_in_dx", du, weight("mix_w_in", 0), addend=(ALPHA, dz1))
            token = hook(dx_in)
        else:
            do = mm_rows_dx("attn_out_dx", dz1, weight("attn_w_o", 0), out_dtype=BF16)
            big_grads[("attn_w_o", 0)] = mm_rows_dw("attn_out_dw", sv["att"], dz1)
            dq, dk, dv, ds_sum = attn_bwd("attn_bwd", sv["qkvp"], sv["bias"], do)
            cols, sat = bias_grad_reduce("bias_grad", ds_sum)
            d_rel = jnp.concatenate(
                [jnp.zeros((N_HEADS, 1), F32),
                 jnp.flip(cols[:, 0, Q_TILE + SHEAR_SAT:Q_TILE - 1 + SHEAR_W], axis=1),
                 sat[:, 0, 0:1]], axis=1)
            small_grads[("attn_rel_bias", 0)] = d_rel
            dqkv = jnp.concatenate([dq, dk, dv], axis=1)
            big_grads[("attn_w_qkv", 0)] = mm_cols_dw("attn_qkv_dw", sv["x_in"], dqkv)
            dx_in = mm_cols_dx("attn_qkv_dx", dqkv, weight("attn_w_qkv", 0), addend=(ALPHA, dz1))
        parts = [(1.0, dx_in)]
    grad_x = dx_in

    small = [
        ("pool_w", pool_w, m_pool_w, v_pool_w, None),
        ("pool_scale", pool_scale, m_pool_scale, v_pool_scale, None),
        ("conv_dw_w", conv_dw_w, m_conv_dw_w, v_conv_dw_w, 2),
        ("conv_dw_b", conv_dw_b, m_conv_dw_b, v_conv_dw_b, None),
        ("conv_ln_g", conv_ln_g, m_conv_ln_g, v_conv_ln_g, None),
        ("conv_ln_b", conv_ln_b, m_conv_ln_b, v_conv_ln_b, None),
        ("attn_rel_bias", attn_rel_bias, m_attn_rel_bias, v_attn_rel_bias, None),
        ("ln_mix_g", ln_mix_g, m_ln_mix_g, v_ln_mix_g, None),
        ("ln_mix_b", ln_mix_b, m_ln_mix_b, v_ln_mix_b, None),
        ("ffn_dw_w", ffn_dw_w, m_ffn_dw_w, v_ffn_dw_w, 2),
        ("ffn_dw_b", ffn_dw_b, m_ffn_dw_b, v_ffn_dw_b, None),
        ("ple_b_gate", ple_b_gate, m_ple_b_gate, v_ple_b_gate, None),
        ("ln_ffn_g", ln_ffn_g, m_ln_ffn_g, v_ln_ffn_g, None),
        ("ln_ffn_b", ln_ffn_b, m_ln_ffn_b, v_ln_ffn_b, None),
    ]
    full_grads = []
    for nm, w, _, _, shard_axis in small:
        full = list(w.shape)
        if shard_axis is not None:
            full[shard_axis] *= N_SHARD
        per_layer = [small_grads[(nm, layer)].reshape((1,) + tuple(full[1:])) for layer in range(w.shape[0])]
        full_grads.append(jnp.concatenate(per_layer, axis=0))
    packed = _pack(full_grads + [loss_part])
    dev_arr = (4 * xi + 2 * yi + ci).astype(jnp.int32).reshape(1)
    sg_block = cast_into_gathered("place_small_grads", packed[None], 0, dev_arr, n_blocks=8, dtype=F32)
    sg_send, sg_recv, sg_bufs, sg_token = copies_start("small_grads_start", [sg_block], all_plan, 7)
    token = sg_token if token is None else token + sg_token

    shard_grads = {}
    for red in reducers:
        if red.stage == 4:
            shard_grads.update(red.result)
    big_out = {}

    def update_big(names, tok):
        for nm, w, m, v, _ in big:
            if nm in names:
                gl = [shard_grads[(nm, layer)] for layer in range(w.shape[0])]
                delta, new_m, new_v = adamw(f"adamw_{nm}", w, gl, m, v, token=tok)
                big_out[nm] = (jnp.stack(gl, axis=0), delta, new_m, new_v)

    last_group = ("mix_w_in", "mix_w_out")
    update_big([nm for nm, _, _, _, _ in big if nm not in last_group], token)
    token = hook(big_out["ffn_w_up"][1])

    gathered_sg = copies_wait("small_grads_wait", sg_bufs, sg_send, sg_recv, all_plan, 0, big_out["ffn_w_down"][1])[0]
    total = sum_blocks("sum_small", gathered_sg.reshape(8 * packed.shape[0], LANES), 8)
    unpacked = _unpack(total, [g.shape for g in full_grads] + [loss_part.shape])
    loss = unpacked[-1][0, 0]
    local_grads = []
    for (nm, w, _, _, shard_axis), g in zip(small, unpacked[:-1]):
        if shard_axis is not None:
            width = w.shape[shard_axis]
            g = lax.dynamic_slice_in_dim(g, shard_idx * width, width, axis=shard_axis)
        local_grads.append(g.reshape(w.shape))
    updated = adamw_many("adamw_small", [w for _, w, _, _, _ in small], local_grads,
                         [m for _, _, m, _, _ in small], [v for _, _, _, v, _ in small], token)
    hook(updated[0][0])
    for red in reducers:
        shard_grads.update(red.result)
    update_big(last_group, None)
    small_out = {}
    for (nm, _, _, _, _), g, (d_, m_, v_) in zip(small, local_grads, updated):
        small_out[nm] = (g, d_, m_, v_)

    order = ["mix_w_in", "pool_w", "pool_scale", "conv_dw_w", "conv_dw_b", "conv_ln_g", "conv_ln_b", "mix_w_out",
             "attn_w_qkv", "attn_rel_bias", "attn_w_o", "ln_mix_g", "ln_mix_b", "ffn_w_up", "ffn_dw_w", "ffn_dw_b",
             "ffn_w_down", "ple_w_proj", "ple_w_gate", "ple_b_gate", "ln_ffn_g", "ln_ffn_b"]
    res = {**big_out, **small_out}
    outs = [loss, grad_x[None]]
    for slot in range(4):
        outs += [res[nm][slot] for nm in order]
    return tuple(outs)
```

```python
import math

import jax
import jax.numpy as jnp
from jax import lax
from jax.experimental import pallas as pl
from jax.experimental.pallas import tpu as pltpu

F32 = jnp.float32
BF16 = jnp.bfloat16
MESH = pl.DeviceIdType.MESH

N_LAYERS = 2
ALPHA = (2 * N_LAYERS) ** 0.25
LN_EPS = 1e-5
NEG_INF = -1e30
CHUNK = 64
LEFT_CHUNKS = 8
PAD_ROWS = LEFT_CHUNKS * CHUNK
HEAD_DIM = 64
ATTN_SCALE = HEAD_DIM ** -0.5
N_HEADS = 16
MAX_REL = 256
POOL_WINDOWS = (2, 4, 8, 16)
POOL_GROUP = 128
CONV_K = 31
FFN_K = 3
CONV_HALO = 32
FFN_HALO = 8
FFN_TILE = 256
FFN_CHUNK_ROWS = 64
FFN_CHUNK_LANES = 128
Q_TILE = 256
K_WIN = Q_TILE + PAD_ROWS
LANES = 128
SUBLANES = 8
ATTN_PAIRS = 2
ATTN_PAIRS_FWD = 8
ATTN_LANES = ATTN_PAIRS * LANES
SHEAR_W = Q_TILE + K_WIN
SHEAR_SAT = SHEAR_W - 2 * MAX_REL
N_SHARD = 4

ADAM_LR = 0.001
ADAM_B1 = 0.9
ADAM_B2 = 0.999
ADAM_EPS = 1e-08
ADAM_WD = 0.01
ADAM_STEP = 10
ADAM_BC1 = 1.0 - ADAM_B1 ** ADAM_STEP
ADAM_BC2 = 1.0 - ADAM_B2 ** ADAM_STEP

DIMS = {
    "nn": (((1,), (0,)), ((), ())),
    "nt": (((1,), (1,)), ((), ())),
    "tn": (((0,), (0,)), ((), ())),
}


def _cp(vmem_mb=48, **kw):
    return pltpu.CompilerParams(vmem_limit_bytes=vmem_mb * 1024 * 1024, **kw)


def _in_hbm(a):
    return pltpu.with_memory_space_constraint(a, pltpu.HBM)


STAGING_LIMIT_BYTES = 1 << 20
SMALL_WEIGHT_BYTES = 1 << 22
SUM_BLOCK_ROWS = 2048
SMALL_BLOCK_BYTES = 1 << 19


def _call(body, **kw):
    call = pl.pallas_call(body, **kw)

    def run(*args):
        pinned = []
        for a in args:
            big = a.size * a.dtype.itemsize >= STAGING_LIMIT_BYTES
            pinned.append(_in_hbm(a) if big and not jnp.issubdtype(a.dtype, jnp.integer) else a)
        return call(*pinned)

    return run


def _dot(a, b, mode):
    return lax.dot_general(a.astype(BF16), b.astype(BF16), DIMS[mode], preferred_element_type=F32)


def _sig(x):
    return 1.0 / (1.0 + jnp.exp(-x))


def _row_tile(s):
    return min(512, s // 4)


def _mm_tile(s):
    return min(1024, s // 4)


def _mm(name, mode, a, b, in_specs, out_shape, out_spec, acc_shape, grid, nk, zero_first=False, vmem_mb=48,
        addend=None):
    out_f32 = out_shape.dtype == F32

    def body(a_ref, b_ref, *rest):
        k = pl.program_id(2)
        if addend is None:
            o_ref, scr = rest[0], rest[1:]
        else:
            add_ref, o_ref, scr = rest[0], rest[1], rest[2:]

        def compute():
            part = _dot(a_ref[...], b_ref[...], mode)
            if nk == 1:
                if addend is not None:
                    part = part + addend[0] * add_ref[...]
                o_ref[...] = part.astype(o_ref.dtype)
                return
            acc = o_ref if out_f32 else scr[0]

            @pl.when(k == 0)
            def _():
                acc[...] = part if addend is None else part + addend[0] * add_ref[...]

            @pl.when(k > 0)
            def _():
                acc[...] += part

            if not out_f32:
                @pl.when(k == nk - 1)
                def _():
                    o_ref[...] = acc[...].astype(o_ref.dtype)

        if zero_first:
            @pl.when(pl.program_id(1) == 0)
            def _():
                o_ref[...] = jnp.zeros(o_ref.shape, o_ref.dtype)

            pl.when(pl.program_id(1) > 0)(compute)
        else:
            compute()

    scratch = [] if (nk == 1 or out_f32) else [pltpu.VMEM(acc_shape, F32)]
    operands = [a, b] if addend is None else [a, b, addend[1]]
    specs = list(in_specs) if addend is None else list(in_specs) + [out_spec]
    return _call(
        body, name=name, grid=grid, in_specs=specs, out_specs=out_spec, out_shape=out_shape,
        scratch_shapes=scratch, compiler_params=_cp(vmem_mb),
    )(*operands)


def _is_small_weight(wc):
    return wc.size * 2 <= SMALL_WEIGHT_BYTES


def _all_shards(w_ref):
    return jnp.concatenate([w_ref[j] for j in range(N_SHARD)], axis=1)


def mm_cols_fwd(name, a, wc, out_dtype, pad_blocks=0, part=(0, 1)):
    s, k = a.shape
    s //= part[1]
    n4 = wc.shape[2]
    tm = _row_tile(s) if pad_blocks else _mm_tile(s)
    nt = s // tm
    first_block = part[0] * nt
    if _is_small_weight(wc) and not pad_blocks:
        def body(a_ref, w_ref, o_ref):
            o_ref[...] = _dot(a_ref[...], _all_shards(w_ref), "nn").astype(o_ref.dtype)

        return _call(
            body, name=name, grid=(nt,),
            in_specs=[pl.BlockSpec((tm, k), lambda i: (first_block + i, 0)), _full(wc.shape)],
            out_specs=pl.BlockSpec((tm, N_SHARD * n4), lambda i: (i, 0)),
            out_shape=jax.ShapeDtypeStruct((s, N_SHARD * n4), out_dtype), compiler_params=_cp(),
        )(a, wc)
    return _mm(
        name, "nn", a, wc,
        [pl.BlockSpec((tm, k), lambda j, i, r: (first_block + jnp.maximum(i - pad_blocks, 0), 0)),
         pl.BlockSpec((None, k, n4), lambda j, i, r: (j, 0, 0))],
        jax.ShapeDtypeStruct((s + pad_blocks * tm, N_SHARD * n4), out_dtype),
        pl.BlockSpec((tm, n4), lambda j, i, r: (i, j)),
        None, (N_SHARD, nt + pad_blocks, 1), 1, zero_first=pad_blocks > 0)


def mm_cols_dx(name, dy, wc, addend=None):
    s = dy.shape[0]
    _, k, n4 = wc.shape
    tm = _mm_tile(s)
    if _is_small_weight(wc):
        def body(dy_ref, w_ref, *rest):
            part = _dot(dy_ref[...], _all_shards(w_ref), "nt")
            rest[-1][...] = part if addend is None else part + addend[0] * rest[0][...]

        out_spec = pl.BlockSpec((tm, k), lambda i: (i, 0))
        extra, extra_specs = ([], []) if addend is None else ([addend[1]], [out_spec])
        return _call(
            body, name=name, grid=(s // tm,),
            in_specs=[pl.BlockSpec((tm, N_SHARD * n4), lambda i: (i, 0)), _full(wc.shape)] + extra_specs,
            out_specs=out_spec, out_shape=jax.ShapeDtypeStruct((s, k), F32), compiler_params=_cp(),
        )(dy, wc, *extra)
    return _mm(
        name, "nt", dy, wc,
        [pl.BlockSpec((tm, n4), lambda g, i, r: (i, r)),
         pl.BlockSpec((None, k, n4), lambda g, i, r: (r, 0, 0))],
        jax.ShapeDtypeStruct((s, k), F32),
        pl.BlockSpec((tm, k), lambda g, i, r: (i, 0)),
        (tm, k), (1, s // tm, N_SHARD), N_SHARD, addend=addend)


def mm_cols_dw(name, a, dy, part=(0, 1)):
    s, k = a.shape
    s //= part[1]
    n4 = dy.shape[1] // N_SHARD
    tm = _mm_tile(s)
    nt = s // tm
    first_block = part[0] * nt
    if k * n4 * N_SHARD * 2 <= SMALL_WEIGHT_BYTES:
        def body(a_ref, dy_ref, o_ref):
            full = _dot(a_ref[...], dy_ref[...], "tn")
            first = pl.program_id(0) == 0
            for j in range(N_SHARD):
                _acc_add(o_ref.at[j], first, full[:, j * n4:(j + 1) * n4])

        return _call(
            body, name=name, grid=(nt,),
            in_specs=[pl.BlockSpec((tm, k), lambda r: (first_block + r, 0)),
                      pl.BlockSpec((tm, N_SHARD * n4), lambda r: (r, 0))],
            out_specs=_full((N_SHARD, k, n4)),
            out_shape=jax.ShapeDtypeStruct((N_SHARD, k, n4), F32), compiler_params=_cp(),
        )(a, dy)
    return _mm(
        name, "tn", a, dy,
        [pl.BlockSpec((tm, k), lambda j, g, r: (first_block + r, 0)),
         pl.BlockSpec((tm, n4), lambda j, g, r: (r, j))],
        jax.ShapeDtypeStruct((N_SHARD, k, n4), F32),
        pl.BlockSpec((None, k, n4), lambda j, g, r: (j, 0, 0)),
        (k, n4), (N_SHARD, 1, nt), nt)


def _k_tile(k):
    return k if k <= 1024 else k // 2


def mm_rows_fwd(name, a, wr, out_dtype=F32):
    s, k = a.shape
    n = wr.shape[1]
    tm = _mm_tile(s)
    tk = _k_tile(k)
    nk = k // tk
    return _mm(
        name, "nn", a, wr,
        [pl.BlockSpec((tm, tk), lambda g, i, r: (i, r)),
         pl.BlockSpec((tk, n), lambda g, i, r: (r, 0))],
        jax.ShapeDtypeStruct((s, n), out_dtype),
        pl.BlockSpec((tm, n), lambda g, i, r: (i, 0)),
        (tm, n), (1, s // tm, nk), nk)


def mm_rows_dx(name, dy, wr, out_dtype=F32):
    s, n = dy.shape
    k = wr.shape[0]
    tm = _mm_tile(s)
    tk = _k_tile(k)
    return _mm(
        name, "nt", dy, wr,
        [pl.BlockSpec((tm, n), lambda j, i, r: (i, 0)),
         pl.BlockSpec((tk, n), lambda j, i, r: (j, 0))],
        jax.ShapeDtypeStruct((s, k), out_dtype),
        pl.BlockSpec((tm, tk), lambda j, i, r: (i, j)),
        None, (k // tk, s // tm, 1), 1)


def mm_rows_dw(name, a, dy):
    s, k = a.shape
    n = dy.shape[1]
    tm = _mm_tile(s)
    tk = _k_tile(k)
    nt = s // tm
    return _mm(
        name, "tn", a, dy,
        [pl.BlockSpec((tm, tk), lambda j, g, r: (r, j)),
         pl.BlockSpec((tm, n), lambda j, g, r: (r, 0))],
        jax.ShapeDtypeStruct((k, n), F32),
        pl.BlockSpec((tk, n), lambda j, g, r: (j, 0)),
        (tk, n), (k // tk, 1, nt), nt)


def _row(tm, c, col=0):
    return pl.BlockSpec((tm, c), lambda i: (i, col))


def _full(shape):
    nd = len(shape)
    return pl.BlockSpec(shape, lambda i: (0,) * nd)


def _prev(tm, h, c, col=0):
    return pl.BlockSpec((h, c), lambda i: (jnp.maximum(i * (tm // h) - 1, 0), col))


def _next(tm, h, c, s, col=0):
    return pl.BlockSpec((h, c), lambda i: (jnp.minimum((i + 1) * (tm // h), s // h - 1), col))


def _acc_add(ref, first, val):
    @pl.when(first)
    def _():
        ref[...] = val

    @pl.when(jnp.logical_not(first))
    def _():
        ref[...] += val


def _colsum(v):
    return jnp.sum(v, axis=0, keepdims=True)


def _ln_stats(z):
    mu = jnp.mean(z, axis=-1, keepdims=True)
    zc = z - mu
    var = jnp.mean(zc * zc, axis=-1, keepdims=True)
    rstd = lax.rsqrt(var + LN_EPS)
    return zc * rstd, rstd


def _ln_bwd(dxhat, xhat, rstd):
    m1 = jnp.mean(dxhat, axis=-1, keepdims=True)
    m2 = jnp.mean(dxhat * xhat, axis=-1, keepdims=True)
    return rstd * (dxhat - m1 - xhat * m2)


def ln_fwd(name, x, f, g, b, ple=None, emit_y=True):
    s, d = x.shape
    tm = _row_tile(s)
    n_in = 2 + (3 if ple is not None else 0)

    def body(*refs):
        x_ref, f_ref = refs[0], refs[1]
        g_ref, b_ref = refs[n_in], refs[n_in + 1]
        xh_ref, rs_ref = refs[-2:]
        z = ALPHA * x_ref[...] + f_ref[...]
        if ple is not None:
            pgl_ref, pp_ref, bg_ref = refs[2:5]
            z = z + _sig(pgl_ref[...] + bg_ref[...]) * pp_ref[...]
        xhat, rstd = _ln_stats(z)
        if emit_y:
            y = xhat * g_ref[...] + b_ref[...]
            refs[n_in + 2][...] = y
            refs[n_in + 3][...] = y.astype(BF16)
        xh_ref[...] = xhat
        rs_ref[...] = jnp.broadcast_to(rstd, rs_ref.shape)

    ins = [x, f]
    specs = [_row(tm, d), _row(tm, d)]
    if ple is not None:
        pgl, pp, bg = ple
        ins += [pgl, pp, bg]
        specs += [_row(tm, d), _row(tm, d), _full((1, d))]
    ins += [g, b]
    specs += [_full((1, d)), _full((1, d))]
    y_shapes = [jax.ShapeDtypeStruct((s, d), F32), jax.ShapeDtypeStruct((s, d), BF16)] if emit_y else []
    outs = _call(
        body, name=name, grid=(s // tm,), in_specs=specs,
        out_specs=[_row(tm, d)] * (len(y_shapes) + 1) + [_row(tm, LANES)],
        out_shape=y_shapes + [jax.ShapeDtypeStruct((s, d), F32), jax.ShapeDtypeStruct((s, LANES), F32)],
        compiler_params=_cp(),
    )(*ins)
    return tuple(outs) if emit_y else (None, None, outs[0], outs[1])


def ln_bwd(name, parts, xhat, rstd, g, ple=None, loss=None):
    s, d = xhat.shape
    tm = _row_tile(s)
    coefs = [c for c, _ in parts]
    n_p = len(parts)
    n_ple = 3 if ple is not None else 0
    n_in = n_p + 3 + n_ple + (2 if loss is not None else 0)

    def body(*refs):
        first = pl.program_id(0) == 0
        xh = refs[n_p][...]
        rs = refs[n_p + 1][:, 0:1]
        g_v = refs[n_p + 2][...]
        outs = refs[n_in:]
        if loss is not None:
            t_ref, b_ref = refs[n_p + 3 + n_ple:n_p + 5 + n_ple]
            err = (xh * g_v + b_ref[...]) - t_ref[...]
            dy = err * (1.0 / d)
            part = 0.5 * jnp.sum(jnp.mean(err * err, axis=-1, keepdims=True), axis=0, keepdims=True)
            _acc_add(outs[-1], first, jnp.broadcast_to(part, outs[-1].shape))
        else:
            dy = coefs[0] * refs[0][...].astype(F32)
            for j in range(1, n_p):
                dy = dy + coefs[j] * refs[j][...].astype(F32)
        dz = _ln_bwd(dy * g_v, xh, rs)
        outs[0][...] = dz
        _acc_add(outs[1], first, _colsum(dy * xh))
        _acc_add(outs[2], first, _colsum(dy))
        if ple is not None:
            pgl_ref, pp_ref, bg_ref = refs[n_p + 3:n_p + 6]
            pg = _sig(pgl_ref[...] + bg_ref[...])
            dpgl = dz * pp_ref[...] * pg * (1.0 - pg)
            outs[3][...] = (dz * pg).astype(BF16)
            outs[4][...] = dpgl.astype(BF16)
            _acc_add(outs[5], first, _colsum(dpgl))

    ins = [p for _, p in parts] + [xhat, rstd, g]
    specs = [_row(tm, d)] * n_p + [_row(tm, d), _row(tm, LANES), _full((1, d))]
    out_specs = [_row(tm, d), _full((1, d)), _full((1, d))]
    out_shape = [jax.ShapeDtypeStruct((s, d), F32), jax.ShapeDtypeStruct((1, d), F32),
                 jax.ShapeDtypeStruct((1, d), F32)]
    if ple is not None:
        pgl, pp, bg = ple
        ins += [pgl, pp, bg]
        specs += [_row(tm, d), _row(tm, d), _full((1, d))]
        out_specs += [_row(tm, d), _row(tm, d), _full((1, d))]
        out_shape += [jax.ShapeDtypeStruct((s, d), BF16), jax.ShapeDtypeStruct((s, d), BF16),
                      jax.ShapeDtypeStruct((1, d), F32)]
    if loss is not None:
        target, b = loss
        ins += [target, b]
        specs += [_row(tm, d), _full((1, d))]
        out_specs += [_full((8, LANES))]
        out_shape += [jax.ShapeDtypeStruct((8, LANES), F32)]
    return _call(
        body, name=name, grid=(s // tm,), in_specs=specs, out_specs=out_specs, out_shape=out_shape,
        compiler_params=_cp(),
    )(*ins)


def _fill_rotations(rot_ref, x, direction):
    n = x.shape[0]
    rot_ref[0] = x
    for b in range(1, SUBLANES):
        if direction < 0:
            rot_ref[b, SUBLANES:n, :] = x[SUBLANES - b:n - b]
        else:
            rot_ref[b, 0:n - SUBLANES, :] = x[b:n - SUBLANES + b]


def _rotated(rot_ref, start, rows, cs, direction=-1):
    b = (-start) % SUBLANES if direction < 0 else start % SUBLANES
    aligned = start + b if direction < 0 else start - b
    return rot_ref[b, pl.ds(aligned, rows), cs]


def _tile_pos(i, tm, rows):
    return (i * tm + lax.broadcasted_iota(jnp.int32, (rows, 1), 0) + 1).astype(F32)


def mixer_fwd(name, u, pool_w, pool_scale, conv_w, conv_b, cn_g, cn_b):
    s = u.shape[0]
    dp = 512
    tm = min(256, s // 4)
    h = CONV_HALO

    def body(a_c, a_p, bv_c, bv_p, bg_c, bg_p, pw_ref, ps_ref, cw_ref, cb_ref, cg_ref, cbt_ref,
             cat_ref, d_ref, e_ref, glu_ref, hh_ref, rs_ref, ext_a, rot_g, conv_out):
        i = pl.program_id(0)
        first = i == 0
        ext_a[0:h, :] = jnp.where(first, 0.0, a_p[...])
        ext_a[h:, :] = a_c[...]
        glu = bv_c[...] * _sig(bg_c[...])
        glu_ref[...] = glu
        _fill_rotations(rot_g, jnp.concatenate([jnp.where(first, 0.0, bv_p[...] * _sig(bg_p[...])), glu], axis=0), -1)
        pos = _tile_pos(i, tm, tm)
        for gi, w in enumerate(POOL_WINDOWS):
            cs = slice(gi * POOL_GROUP, (gi + 1) * POOL_GROUP)
            a_g = ext_a[pl.ds(h, tm), cs]
            acc = a_g
            for sh in range(1, w):
                acc = acc + ext_a[pl.ds(h - sh, tm), cs]
            d_g = acc / jnp.minimum(pos, float(w)) - a_g
            d_ref[:, cs] = d_g.astype(BF16)
            e_g = _dot(d_g, pw_ref[gi], "nn")
            e_ref[:, cs] = e_g
            cat_ref[:, cs] = (e_g * ps_ref[:, cs]).astype(BF16)
        for lg in range(dp // LANES):
            cs = slice(lg * LANES, (lg + 1) * LANES)
            acc = jnp.broadcast_to(cb_ref[:, cs], (tm, LANES))
            for sh in range(CONV_K):
                acc = acc + _rotated(rot_g, h - sh, tm, cs) * cw_ref[pl.ds(CONV_K - 1 - sh, 1), cs]
            conv_out[:, cs] = acc
        hhat, rstd = _ln_stats(conv_out[...])
        hl = hhat * cg_ref[...] + cbt_ref[...]
        cat_ref[:, dp:] = (hl * _sig(hl)).astype(BF16)
        hh_ref[...] = hhat
        rs_ref[...] = jnp.broadcast_to(rstd, rs_ref.shape)

    specs = [_row(tm, dp, 0), _prev(tm, h, dp, 0), _row(tm, dp, 1), _prev(tm, h, dp, 1),
             _row(tm, dp, 2), _prev(tm, h, dp, 2),
             _full((4, POOL_GROUP, POOL_GROUP)), _full((1, dp)), _full((CONV_K, dp)),
             _full((1, dp)), _full((1, dp)), _full((1, dp))]
    out_specs = [_row(tm, 2 * dp), _row(tm, dp), _row(tm, dp), _row(tm, dp), _row(tm, dp), _row(tm, LANES)]
    out_shape = [jax.ShapeDtypeStruct((s, 2 * dp), BF16), jax.ShapeDtypeStruct((s, dp), BF16),
                 jax.ShapeDtypeStruct((s, dp), F32), jax.ShapeDtypeStruct((s, dp), F32),
                 jax.ShapeDtypeStruct((s, dp), F32), jax.ShapeDtypeStruct((s, LANES), F32)]
    return _call(
        body, name=name, grid=(s // tm,), in_specs=specs, out_specs=out_specs, out_shape=out_shape,
        scratch_shapes=[pltpu.VMEM((h + tm, dp), F32), pltpu.VMEM((SUBLANES, h + tm, dp), F32),
                        pltpu.VMEM((tm, dp), F32)],
        compiler_params=_cp(),
    )(u, u, u, u, u, u, pool_w, pool_scale, conv_w, conv_b, cn_g, cn_b)


def mixer_bwd(name, dcat, u, d_sv, e_sv, glu_sv, hh_sv, rs_sv, pool_w, pool_scale, conv_w, cn_g, cn_b):
    s = u.shape[0]
    dp = 512
    tm = min(256, s // 4)
    h = CONV_HALO
    nt = s // tm

    def body(dc_c, dc_n, bv_c, bg_c, d_c, e_c, gl_c, gl_p, hh_c, hh_n, rs_c, rs_n,
             pw_ref, ps_ref, cw_ref, cg_ref, cbt_ref,
             du_ref, dpw_ref, dps_ref, dcw_ref, dcb_ref, dcg_ref, dcbt_ref,
             ext_dh, ext_g, ext_r):
        i = pl.program_id(0)
        first = i == 0
        last = i == nt - 1
        cg = cg_ref[...]

        def conv_grads(dyb, hhat, rstd):
            hl = hhat * cg + cbt_ref[...]
            sg = _sig(hl)
            dhl = dyb * (sg * (1.0 + hl * (1.0 - sg)))
            return _ln_bwd(dhl * cg, hhat, rstd), dhl

        hh_cur = hh_c[...]
        dh_c, dhl_c = conv_grads(dc_c[:, dp:], hh_cur, rs_c[:, 0:1])
        dh_n, _ = conv_grads(dc_n[:, dp:], hh_n[...], rs_n[:, 0:1])
        _fill_rotations(ext_dh, jnp.concatenate([dh_c, jnp.where(last, 0.0, dh_n)], axis=0), 1)
        _fill_rotations(ext_g, jnp.concatenate([jnp.where(first, 0.0, gl_p[...]), gl_c[...]], axis=0), -1)

        @pl.when(first)
        def _():
            dcw_ref[...] = jnp.zeros(dcw_ref.shape, F32)

        for lg in range(dp // LANES):
            cs = slice(lg * LANES, (lg + 1) * LANES)
            dglu = jnp.zeros((tm, LANES), F32)
            for sh in range(CONV_K):
                dglu = dglu + _rotated(ext_dh, sh, tm, cs, 1) * cw_ref[pl.ds(CONV_K - 1 - sh, 1), cs]
            dh_g = ext_dh[0, pl.ds(0, tm), cs]
            for sh in range(CONV_K):
                dcw_ref[pl.ds(CONV_K - 1 - sh, 1), cs] += _colsum(dh_g * _rotated(ext_g, h - sh, tm, cs))
            sgate = _sig(bg_c[:, cs])
            du_ref[:, dp + lg * LANES:dp + (lg + 1) * LANES] = dglu * sgate
            du_ref[:, 2 * dp + lg * LANES:2 * dp + (lg + 1) * LANES] = dglu * bv_c[:, cs] * sgate * (1.0 - sgate)
        _acc_add(dcb_ref, first, _colsum(dh_c))
        _acc_add(dcg_ref, first, _colsum(dhl_c * hh_cur))
        _acc_add(dcbt_ref, first, _colsum(dhl_c))

        pos_c = _tile_pos(i, tm, tm)
        pos_n = _tile_pos(i + 1, tm, h)
        _acc_add(dps_ref, first, _colsum(dc_c[:, :dp] * e_c[...]))
        for gi, w in enumerate(POOL_WINDOWS):
            cs = slice(gi * POOL_GROUP, (gi + 1) * POOL_GROUP)
            pw = pw_ref[gi]
            de_c = dc_c[:, cs] * ps_ref[:, cs]
            de_n = dc_n[:, cs] * ps_ref[:, cs]
            dd_c = _dot(de_c, pw, "nt")
            dd_n = _dot(de_n, pw, "nt")
            ext_r[0:tm, :] = dd_c / jnp.minimum(pos_c, float(w))
            ext_r[tm:, :] = jnp.where(last, 0.0, dd_n / jnp.minimum(pos_n, float(w)))
            acc = -dd_c
            for sh in range(w):
                acc = acc + ext_r[pl.ds(sh, tm), :]
            du_ref[:, cs] = acc
            dpw_g = _dot(d_c[:, cs], de_c, "tn")

            @pl.when(first)
            def _():
                dpw_ref[gi] = dpw_g

            @pl.when(jnp.logical_not(first))
            def _():
                dpw_ref[gi] += dpw_g

    specs = [_row(tm, 2 * dp), _next(tm, h, 2 * dp, s), _row(tm, dp, 1), _row(tm, dp, 2),
             _row(tm, dp), _row(tm, dp), _row(tm, dp), _prev(tm, h, dp),
             _row(tm, dp), _next(tm, h, dp, s), _row(tm, LANES), _next(tm, h, LANES, s),
             _full((4, POOL_GROUP, POOL_GROUP)), _full((1, dp)), _full((CONV_K, dp)),
             _full((1, dp)), _full((1, dp))]
    out_specs = [_row(tm, 3 * dp), _full((4, POOL_GROUP, POOL_GROUP)), _full((1, dp)), _full((CONV_K, dp)),
                 _full((1, dp)), _full((1, dp)), _full((1, dp))]
    out_shape = [jax.ShapeDtypeStruct((s, 3 * dp), F32),
                 jax.ShapeDtypeStruct((4, POOL_GROUP, POOL_GROUP), F32), jax.ShapeDtypeStruct((1, dp), F32),
                 jax.ShapeDtypeStruct((CONV_K, dp), F32), jax.ShapeDtypeStruct((1, dp), F32),
                 jax.ShapeDtypeStruct((1, dp), F32), jax.ShapeDtypeStruct((1, dp), F32)]
    return _call(
        body, name=name, grid=(nt,), in_specs=specs, out_specs=out_specs, out_shape=out_shape,
        scratch_shapes=[pltpu.VMEM((SUBLANES, tm + h, dp), F32), pltpu.VMEM((SUBLANES, h + tm, dp), F32),
                        pltpu.VMEM((tm + h, POOL_GROUP), F32)],
        compiler_params=_cp(),
    )(dcat, dcat, u, u, d_sv, e_sv, glu_sv, glu_sv, hh_sv, hh_sv, rs_sv, rs_sv,
      pool_w, pool_scale, conv_w, cn_g, cn_b)


GELU_C = math.sqrt(2.0 / math.pi)


def _gelu_parts(x):
    x2 = x * x
    t = jnp.tanh(x * (GELU_C + (GELU_C * 0.044715) * x2))
    half_1pt = 0.5 + 0.5 * t
    gelu = x * half_1pt
    dgelu = half_1pt + (0.5 * x) * (1.0 - t * t) * (GELU_C + (3.0 * GELU_C * 0.044715) * x2)
    return gelu, dgelu


def ffn_act_fwd(name, gv, dw_w, dw_b):
    s = gv.shape[0]
    dff = gv.shape[1] // 2
    tm = min(FFN_TILE, s // 4)
    h = FFN_HALO
    rc = FFN_CHUNK_ROWS
    lw = FFN_CHUNK_LANES

    def body(g_c, g_p, v_c, w_ref, b_ref, hid_ref):
        first = pl.program_id(0) == 0

        def chunk(ci, carry):
            r0 = pl.multiple_of(ci * rc, rc)
            above = pl.multiple_of(jnp.maximum(r0 - h, 0), h)
            for lg in range(dff // lw):
                cs = slice(lg * lw, (lg + 1) * lw)
                top = jnp.where(ci == 0, jnp.where(first, 0.0, g_p[:, cs]), g_c[pl.ds(above, h), cs])
                win = jnp.concatenate([top, g_c[pl.ds(r0, rc), cs]], axis=0)
                gc = jnp.broadcast_to(b_ref[:, cs], (rc, lw))
                for sh in range(FFN_K):
                    gc = gc + win[h - sh:h - sh + rc] * w_ref[pl.ds(FFN_K - 1 - sh, 1), cs]
                gelu, _ = _gelu_parts(gc)
                hid_ref[pl.ds(r0, rc), cs] = (gelu * v_c[pl.ds(r0, rc), cs]).astype(BF16)
            return carry

        lax.fori_loop(0, tm // rc, chunk, 0)

    return _call(
        body, name=name, grid=(s // tm,),
        in_specs=[_row(tm, dff, 0), _prev(tm, h, dff, 0), _row(tm, dff, 1), _full((FFN_K, dff)), _full((1, dff))],
        out_specs=_row(tm, dff), out_shape=jax.ShapeDtypeStruct((s, dff), BF16),
        compiler_params=_cp(),
    )(gv, gv, gv, dw_w, dw_b)


def ffn_act_bwd(name, dhid, gv, dw_w, dw_b):
    s = gv.shape[0]
    dff = gv.shape[1] // 2
    tm = min(FFN_TILE, s // 4)
    h = FFN_HALO
    nt = s // tm
    rc = FFN_CHUNK_ROWS
    lw = FFN_CHUNK_LANES
    n_chunks = tm // rc

    def body(dh_c, dh_n, g_p, g_c, g_n, v_c, v_n, w_ref, b_ref, dgv_ref, dw_ref, db_ref):
        i = pl.program_id(0)
        first = i == 0
        last = i == nt - 1

        @pl.when(first)
        def _():
            dw_ref[...] = jnp.zeros(dw_ref.shape, F32)
            db_ref[...] = jnp.zeros(db_ref.shape, F32)

        def chunk(ci, carry):
            r0 = pl.multiple_of(ci * rc, rc)
            above = pl.multiple_of(jnp.maximum(r0 - h, 0), h)
            below = pl.multiple_of(jnp.minimum(r0 + rc, tm - h), h)
            at_end = ci == n_chunks - 1
            for lg in range(dff // lw):
                cs = slice(lg * lw, (lg + 1) * lw)
                top = jnp.where(ci == 0, jnp.where(first, 0.0, g_p[:, cs]), g_c[pl.ds(above, h), cs])
                bot = jnp.where(at_end, g_n[:, cs], g_c[pl.ds(below, h), cs])
                win = jnp.concatenate([top, g_c[pl.ds(r0, rc), cs], bot], axis=0)
                shifted = [win[h - sh:h - sh + rc + h] for sh in range(FFN_K)]
                gc = jnp.broadcast_to(b_ref[:, cs], (rc + h, lw))
                for sh in range(FFN_K):
                    gc = gc + shifted[sh] * w_ref[pl.ds(FFN_K - 1 - sh, 1), cs]
                gelu, dgelu = _gelu_parts(gc)
                dh_mid = dh_c[pl.ds(r0, rc), cs]
                hv_bot = jnp.where(at_end, jnp.where(last, 0.0, dh_n[:, cs] * v_n[:, cs]),
                                   dh_c[pl.ds(below, h), cs] * v_c[pl.ds(below, h), cs])
                dgc = jnp.concatenate([dh_mid * v_c[pl.ds(r0, rc), cs], hv_bot], axis=0) * dgelu
                dgate = jnp.zeros((rc, lw), F32)
                for sh in range(FFN_K):
                    dgate = dgate + dgc[sh:sh + rc] * w_ref[pl.ds(FFN_K - 1 - sh, 1), cs]
                dgv_ref[pl.ds(r0, rc), cs] = dgate.astype(BF16)
                dgv_ref[pl.ds(r0, rc), slice(dff + lg * lw, dff + (lg + 1) * lw)] = (dh_mid * gelu[0:rc]).astype(BF16)
                dgc_mid = dgc[0:rc]
                for sh in range(FFN_K):
                    dw_ref[pl.ds(FFN_K - 1 - sh, 1), cs] += _colsum(dgc_mid * shifted[sh][0:rc])
                db_ref[:, cs] += _colsum(dgc_mid)
            return carry

        lax.fori_loop(0, n_chunks, chunk, 0)

    return _call(
        body, name=name, grid=(nt,),
        in_specs=[_row(tm, dff), _next(tm, h, dff, s),
                  _prev(tm, h, dff, 0), _row(tm, dff, 0), _next(tm, h, dff, s, 0),
                  _row(tm, dff, 1), _next(tm, h, dff, s, 1),
                  _full((FFN_K, dff)), _full((1, dff))],
        out_specs=[_row(tm, 2 * dff), _full((FFN_K, dff)), _full((1, dff))],
        out_shape=[jax.ShapeDtypeStruct((s, 2 * dff), BF16), jax.ShapeDtypeStruct((FFN_K, dff), F32),
                   jax.ShapeDtypeStruct((1, dff), F32)],
        compiler_params=_cp(),
    )(dhid, dhid, gv, gv, gv, gv, gv, dw_w, dw_b)


def _bias_line(rel_bias):
    nh = rel_bias.shape[0]
    line = jnp.concatenate(
        [jnp.zeros((nh, 1), rel_bias.dtype), jnp.broadcast_to(rel_bias[:, 2 * MAX_REL:], (nh, SHEAR_SAT)),
         jnp.flip(rel_bias[:, 1:2 * MAX_REL], axis=1)], axis=1)
    return line[:, None, :]


def bias_tile(name, line):
    nh = line.shape[0]

    def body(l_ref, o_ref):
        x = jnp.broadcast_to(l_ref[...], (Q_TILE, SHEAR_W))
        z = pltpu.roll(x, SHEAR_W - Q_TILE, 1, stride=1, stride_axis=0)
        qc = lax.broadcasted_iota(jnp.int32, (Q_TILE, K_WIN), 0) // CHUNK
        kc = lax.broadcasted_iota(jnp.int32, (Q_TILE, K_WIN), 1) // CHUNK
        o_ref[...] = jnp.where((kc >= qc) & (kc <= qc + LEFT_CHUNKS), z[:, :K_WIN], NEG_INF)

    return _call(
        body, name=name, grid=(nh,), in_specs=[pl.BlockSpec((None, 1, SHEAR_W), lambda hh: (hh, 0, 0))],
        out_specs=pl.BlockSpec((None, Q_TILE, K_WIN), lambda hh: (hh, 0, 0)),
        out_shape=jax.ShapeDtypeStruct((nh, Q_TILE, K_WIN), F32), compiler_params=_cp(),
    )(line)


def _stack_heads(x2, scale=None):
    if scale is not None:
        x2 = x2 * jnp.asarray(scale, x2.dtype)
    lane = lax.broadcasted_iota(jnp.int32, x2.shape, 1)
    zero = jnp.zeros_like(x2)
    return jnp.concatenate([jnp.where(lane < HEAD_DIM, x2, zero), jnp.where(lane < HEAD_DIM, zero, x2)], axis=0)


def _unstack_heads(x_st):
    lane = lax.broadcasted_iota(jnp.int32, (Q_TILE, LANES), 1)
    return jnp.where(lane < HEAD_DIM, x_st[:Q_TILE], x_st[Q_TILE:])


def _attn_probs(q_st, k3, bias_st, t):
    sc = _dot(q_st, k3, "nt") + bias_st
    col = lax.broadcasted_iota(jnp.int32, sc.shape, 1)
    sc = jnp.where(col >= PAD_ROWS - t * Q_TILE, sc, NEG_INF)
    m = jnp.max(sc, axis=-1, keepdims=True)
    p = jnp.exp(sc - m)
    return p * (1.0 / jnp.sum(p, axis=-1, keepdims=True))


def _attn_specs(d_model, pairs):
    nq = PAD_ROWS // Q_TILE
    width = pairs * LANES
    groups = d_model // width
    specs = [pl.BlockSpec((Q_TILE, width), lambda g, t: (t + nq, g))]
    for which in (1, 2):
        for j in range(K_WIN // Q_TILE):
            specs.append(pl.BlockSpec((Q_TILE, width), lambda g, t, j=j, which=which: (t + j, which * groups + g)))
    specs.append(pl.BlockSpec((2 * pairs, Q_TILE, K_WIN), lambda g, t: (g, 0, 0)))
    return specs


def attn_fwd(name, qkvp, bias):
    s = qkvp.shape[0] - PAD_ROWS
    d_model = qkvp.shape[1] // 3
    nw = K_WIN // Q_TILE

    def body(q_ref, *refs):
        k_refs, v_refs, b_ref, o_ref = refs[:nw], refs[nw:2 * nw], refs[2 * nw], refs[2 * nw + 1]
        t = pl.program_id(1)
        for j in range(ATTN_PAIRS_FWD):
            ls = slice(j * LANES, (j + 1) * LANES)
            k3 = jnp.concatenate([r[:, ls] for r in k_refs], axis=0)
            v3 = jnp.concatenate([r[:, ls] for r in v_refs], axis=0)
            bias_st = b_ref[2 * j:2 * j + 2].reshape(2 * Q_TILE, K_WIN)
            p = _attn_probs(_stack_heads(q_ref[:, ls], ATTN_SCALE), k3, bias_st, t)
            o_ref[:, ls] = _unstack_heads(_dot(p, v3, "nn")).astype(BF16)

    width = ATTN_PAIRS_FWD * LANES
    return _call(
        body, name=name, grid=(d_model // width, s // Q_TILE),
        in_specs=_attn_specs(d_model, ATTN_PAIRS_FWD), out_specs=pl.BlockSpec((Q_TILE, width), lambda g, t: (t, g)),
        out_shape=jax.ShapeDtypeStruct((s, d_model), BF16), compiler_params=_cp(),
    )(qkvp, *([qkvp] * (2 * nw)), bias)


def attn_bwd(name, qkvp, bias, do):
    s = qkvp.shape[0] - PAD_ROWS
    d_model = qkvp.shape[1] // 3
    nw = K_WIN // Q_TILE
    nt = s // Q_TILE

    def body(q_ref, *refs):
        k_refs, v_refs = refs[:nw], refs[nw:2 * nw]
        b_ref, do_ref, dq_ref, dk_ref, dv_ref, ds_ref, dk_acc, dv_acc = refs[2 * nw:]
        t = pl.program_id(1)
        first = t == 0

        @pl.when(first)
        def _():
            dk_acc[...] = jnp.zeros(dk_acc.shape, F32)
            dv_acc[...] = jnp.zeros(dv_acc.shape, F32)
            ds_ref[...] = jnp.zeros(ds_ref.shape, F32)

        start = pl.multiple_of(t * Q_TILE, Q_TILE)
        for j in range(ATTN_PAIRS):
            ls = slice(j * LANES, (j + 1) * LANES)
            q_st = _stack_heads(q_ref[:, ls], ATTN_SCALE)
            do_st = _stack_heads(do_ref[:, ls])
            k3 = jnp.concatenate([r[:, ls] for r in k_refs], axis=0)
            v3 = jnp.concatenate([r[:, ls] for r in v_refs], axis=0)
            p = _attn_probs(q_st, k3, b_ref[2 * j:2 * j + 2].reshape(2 * Q_TILE, K_WIN), t)
            dp = _dot(do_st, v3, "nt")
            ds = p * (dp - jnp.sum(p * dp, axis=-1, keepdims=True))
            ds_ref[2 * j:2 * j + 2] += ds.reshape(2, Q_TILE, K_WIN)
            dsb = ds.astype(BF16)
            dq_ref[:, ls] = (_unstack_heads(_dot(dsb, k3, "nn")) * ATTN_SCALE).astype(BF16)
            dk_acc[pl.ds(start, K_WIN), ls] += _dot(dsb, q_st, "tn")
            dv_acc[pl.ds(start, K_WIN), ls] += _dot(p, do_st, "tn")

        @pl.when(t == nt - 1)
        def _():
            dk_ref[...] = dk_acc[pl.ds(PAD_ROWS, s), :].astype(BF16)
            dv_ref[...] = dv_acc[pl.ds(PAD_ROWS, s), :].astype(BF16)

    specs = _attn_specs(d_model, ATTN_PAIRS) + [pl.BlockSpec((Q_TILE, ATTN_LANES), lambda g, t: (t, g))]
    col_spec = pl.BlockSpec((s, ATTN_LANES), lambda g, t: (0, g))
    return _call(
        body, name=name, grid=(d_model // ATTN_LANES, nt), in_specs=specs,
        out_specs=[pl.BlockSpec((Q_TILE, ATTN_LANES), lambda g, t: (t, g)), col_spec, col_spec,
                   pl.BlockSpec((2 * ATTN_PAIRS, Q_TILE, K_WIN), lambda g, t: (g, 0, 0))],
        out_shape=[jax.ShapeDtypeStruct((s, d_model), BF16)] * 3
        + [jax.ShapeDtypeStruct((N_HEADS, Q_TILE, K_WIN), F32)],
        scratch_shapes=[pltpu.VMEM((PAD_ROWS + s, ATTN_LANES), F32), pltpu.VMEM((PAD_ROWS + s, ATTN_LANES), F32)],
        compiler_params=_cp(),
    )(qkvp, *([qkvp] * (2 * nw)), bias, do)


def bias_grad_reduce(name, ds_sum):
    nh = ds_sum.shape[0]
    width = SHEAR_W + Q_TILE
    first_k = Q_TILE - 1

    def body(x_ref, col_ref, sat_ref):
        x = x_ref[...]
        hi = x.astype(BF16)
        lo = (x - hi.astype(F32)).astype(BF16)
        r = lax.broadcasted_iota(jnp.int32, (Q_TILE, Q_TILE), 0)
        c = lax.broadcasted_iota(jnp.int32, (Q_TILE, Q_TILE), 1)
        exchange = jnp.where(r + c == Q_TILE - 1, 1.0, 0.0).astype(BF16)
        x_rev = _dot(exchange, hi, "nn") + _dot(exchange, lo, "nn")
        zeros = jnp.zeros((Q_TILE, Q_TILE), F32)
        y = pltpu.roll(jnp.concatenate([zeros, x_rev, zeros], axis=1), 0, 1, stride=1, stride_axis=0)
        cols = _colsum(y)
        col_ref[...] = cols
        k = lax.broadcasted_iota(jnp.int32, cols.shape, 1) - first_k
        tot = jnp.sum(jnp.where((k >= 1) & (k <= SHEAR_SAT), cols, 0.0), axis=-1, keepdims=True)
        sat_ref[...] = jnp.broadcast_to(tot, sat_ref.shape)

    return _call(
        body, name=name, grid=(nh,),
        in_specs=[pl.BlockSpec((None, Q_TILE, K_WIN), lambda hh: (hh, 0, 0))],
        out_specs=[pl.BlockSpec((None, 1, width), lambda hh: (hh, 0, 0)),
                   pl.BlockSpec((None, 1, LANES), lambda hh: (hh, 0, 0))],
        out_shape=[jax.ShapeDtypeStruct((nh, 1, width), F32), jax.ShapeDtypeStruct((nh, 1, LANES), F32)],
        compiler_params=_cp(),
    )(ds_sum)


def _ew_rows(r, most=512, cols=None):
    if cols is not None and r * cols * 4 <= SMALL_BLOCK_BYTES:
        return r
    for cand in range(min(most, r) // 16 * 16, 0, -16):
        if r % cand == 0:
            return cand
    return r


def cast_into_gathered(name, w, layer, s_idx, n_blocks=N_SHARD, dtype=BF16, token=None):
    r, c = w.shape[-2:]
    tr = _ew_rows(r, cols=c)

    def body(s_ref, w_ref, *rest):
        rest[-1][...] = w_ref[...].astype(dtype)

    extra = [] if token is None else [token]
    grid_spec = pltpu.PrefetchScalarGridSpec(
        num_scalar_prefetch=1, grid=(r // tr,),
        in_specs=[pl.BlockSpec((None, tr, c), lambda i, s_ref: (layer, i, 0))] + [ANY_SPEC] * len(extra),
        out_specs=pl.BlockSpec((None, tr, c), lambda i, s_ref: (s_ref[0], i, 0)))
    return _call(
        body, name=name, grid_spec=grid_spec, out_shape=jax.ShapeDtypeStruct((n_blocks, r, c), dtype),
        compiler_params=_cp(),
    )(s_idx, w, *extra)


def adamw(name, w, grads, m, v, token=None):
    nl, r, c = w.shape
    tr = _ew_rows(r, 256, cols=c)

    def body(*refs):
        w_ref, m_ref, v_ref = refs[0], refs[1], refs[2]
        g_refs = refs[3:3 + nl]
        d_ref, nm_ref, nv_ref = refs[-3:]
        layer = pl.program_id(0)
        g = g_refs[0][...]
        for j in range(1, nl):
            g = jnp.where(layer == j, g_refs[j][...], g)
        d_ref[...], nm_ref[...], nv_ref[...] = _adamw_update(w_ref[...], g, m_ref[...], v_ref[...])

    p_spec = pl.BlockSpec((None, tr, c), lambda l, i: (l, i, 0))
    g_spec = pl.BlockSpec((tr, c), lambda l, i: (i, 0))
    extra = [] if token is None else [token]
    extra_specs = [] if token is None else [ANY_SPEC]
    return _call(
        body, name=name, grid=(nl, r // tr), in_specs=[p_spec] * 3 + [g_spec] * nl + extra_specs,
        out_specs=[p_spec] * 3, out_shape=[jax.ShapeDtypeStruct((nl, r, c), F32)] * 3, compiler_params=_cp(),
    )(w, m, v, *grads, *extra)


def _adamw_update(w, g, m, v):
    nm = ADAM_B1 * m + (1.0 - ADAM_B1) * g
    nv = ADAM_B2 * v + (1.0 - ADAM_B2) * (g * g)
    delta = -ADAM_LR * ((nm / ADAM_BC1) / (jnp.sqrt(nv / ADAM_BC2) + ADAM_EPS) + ADAM_WD * w)
    return delta, nm, nv


def adamw_many(name, ws, gs, ms, vs, token):
    n = len(ws)

    def body(*refs):
        ins, outs = refs[:4 * n], refs[4 * n + 1:]
        for i in range(n):
            delta, nm, nv = _adamw_update(ins[i][...], ins[n + i][...], ins[2 * n + i][...], ins[3 * n + i][...])
            outs[3 * i][...] = delta
            outs[3 * i + 1][...] = nm
            outs[3 * i + 2][...] = nv

    vmem = pl.BlockSpec(memory_space=pltpu.VMEM)
    shapes = [jax.ShapeDtypeStruct(w.shape, F32) for w in ws for _ in range(3)]
    outs = _call(
        body, name=name, in_specs=[vmem] * (4 * n) + [ANY_SPEC], out_specs=[vmem] * (3 * n), out_shape=shapes,
        compiler_params=_cp(),
    )(*ws, *gs, *ms, *vs, token)
    return [tuple(outs[3 * i:3 * i + 3]) for i in range(n)]


def sum_blocks(name, gathered, n_blocks):
    r = gathered.shape[0] // n_blocks
    c = gathered.shape[1]
    tr = r if r <= SUM_BLOCK_ROWS else _ew_rows(r)
    nt = r // tr

    def body(*refs):
        acc = refs[0][...]
        for j in range(1, n_blocks):
            acc = acc + refs[j][...]
        refs[-1][...] = acc

    specs = [pl.BlockSpec((tr, c), lambda i, j=j: (j * nt + i, 0)) for j in range(n_blocks)]
    return _call(
        body, name=name, grid=(nt,), in_specs=specs, out_specs=pl.BlockSpec((tr, c), lambda i: (i, 0)),
        out_shape=jax.ShapeDtypeStruct((r, c), F32), compiler_params=_cp(),
    )(*([gathered] * n_blocks))


def _place():
    return lax.axis_index("x"), lax.axis_index("y"), lax.axis_index("c")


def _other_chips(x, y):
    return [(1 - x, y), (x, 1 - y), (1 - x, 1 - y)]


HBM_SPEC = pl.BlockSpec(memory_space=pltpu.HBM)
SEM_SPEC = pl.BlockSpec(memory_space=pltpu.SEMAPHORE)
ANY_SPEC = pl.BlockSpec(memory_space=pl.ANY)
EFFECT = pltpu.SideEffectType.DATAFLOW_SIDE_EFFECTING


def copies_start(name, bufs, plan, n_copies):
    n = len(bufs)

    def body(*refs):
        send, recv = refs[n], refs[n + 1]
        token = refs[2 * n + 2]
        for k, (src, dst, peer, _) in enumerate(plan(refs[:n])):
            pltpu.make_async_remote_copy(
                src_ref=src, dst_ref=dst, send_sem=send.at[k], recv_sem=recv.at[k],
                device_id=peer, device_id_type=MESH).start()
        token[...] = jnp.zeros(token.shape, F32)

    outs = pl.pallas_call(
        body, name=name,
        out_shape=(pltpu.SemaphoreType.DMA((n_copies,)), pltpu.SemaphoreType.DMA((n_copies,)),
                   *[pltpu.HBM(b.shape, b.dtype) for b in bufs], jax.ShapeDtypeStruct((8, LANES), F32)),
        in_specs=[HBM_SPEC] * n,
        out_specs=(SEM_SPEC, SEM_SPEC, *([HBM_SPEC] * n), pl.BlockSpec(memory_space=pltpu.VMEM)),
        input_output_aliases={a: a + 2 for a in range(n)},
        compiler_params=pltpu.CompilerParams(has_side_effects=EFFECT),
    )(*[_in_hbm(b) for b in bufs])
    return outs[0], outs[1], list(outs[2:2 + n]), outs[2 + n]


def copies_wait(name, bufs, send, recv, plan, sem_base, after):
    n = len(bufs)

    def body(*refs):
        send_ref, recv_ref = refs[n], refs[n + 1]
        for k, (src, _, peer, land) in enumerate(plan(refs[:n])):
            cp = pltpu.make_async_remote_copy(
                src_ref=src, dst_ref=land, send_sem=send_ref.at[sem_base + k], recv_sem=recv_ref.at[sem_base + k],
                device_id=peer, device_id_type=MESH)
            cp.wait_send()
            cp.wait_recv()

    outs = pl.pallas_call(
        body, name=name,
        out_shape=tuple(pltpu.HBM(b.shape, b.dtype) for b in bufs),
        in_specs=[HBM_SPEC] * n + [SEM_SPEC, SEM_SPEC, ANY_SPEC], out_specs=tuple([HBM_SPEC] * n),
        input_output_aliases={a: a for a in range(n)},
        compiler_params=pltpu.CompilerParams(has_side_effects=EFFECT),
    )(*bufs, send, recv, after)
    return list(outs)


def gather_plan(refs):
    x, y, c = _place()
    me = 2 * x + y
    return [(buf.at[me], buf.at[me], (cx, cy, c), buf.at[2 * cx + cy])
            for buf in refs for cx, cy in _other_chips(x, y)]


def all_plan(refs):
    x, y, c = _place()
    me = 4 * x + 2 * y + c
    out = []
    for buf in refs:
        for flip in range(1, 8):
            px = 1 - x if flip & 4 else x
            py = 1 - y if flip & 2 else y
            pc = 1 - c if flip & 1 else c
            out.append((buf.at[me], buf.at[me], (px, py, pc), buf.at[4 * px + 2 * py + pc]))
    return out


def swap_plan(refs):
    x, y, c = _place()
    n = len(refs) // 2
    out = []
    for g, land in zip(refs[:n], refs[n:]):
        hr = g.shape[1] // 2
        out.append((g.at[:, pl.ds((1 - c) * hr, hr)], land, (x, y, 1 - c), land))
    return out


def owners_plan(refs):
    x, y, c = _place()
    n = len(refs) // 2
    return [(src.at[2 * cx + cy], land.at[j], (cx, cy, c), land.at[j])
            for src, land in zip(refs[:n], refs[n:]) for j, (cx, cy) in enumerate(_other_chips(x, y))]


def join_plan(refs):
    x, y, c = _place()
    out = []
    for buf in refs:
        hr = buf.shape[0] // 2
        mine = buf.at[pl.ds(c * hr, hr)]
        out.append((mine, mine, (x, y, 1 - c), buf.at[pl.ds((1 - c) * hr, hr)]))
    return out


def add_halves(name, grad, landed, sc_idx):
    _, r, c = grad.shape
    hr = r // 2
    tr = _ew_rows(hr)
    nt = hr // tr

    def body(sc_ref, g_ref, l_ref, own_ref, wire_ref):
        tot = g_ref[...] + l_ref[...]
        wire_ref[...] = tot.astype(BF16)

        @pl.when(pl.program_id(1) == sc_ref[0])
        def _():
            own_ref[...] = tot

    grid_spec = pltpu.PrefetchScalarGridSpec(
        num_scalar_prefetch=1, grid=(nt, N_SHARD),
        in_specs=[pl.BlockSpec((None, tr, c), lambda i, sh, sc_ref: (sh, sc_ref[1] * nt + i, 0)),
                  pl.BlockSpec((None, tr, c), lambda i, sh, sc_ref: (sh, i, 0))],
        out_specs=[pl.BlockSpec((tr, c), lambda i, sh, sc_ref: (i, 0)),
                   pl.BlockSpec((None, tr, c), lambda i, sh, sc_ref: (sh, i, 0))])
    return _call(
        body, name=name, grid_spec=grid_spec,
        out_shape=[jax.ShapeDtypeStruct((hr, c), F32), jax.ShapeDtypeStruct((N_SHARD, hr, c), BF16)],
        compiler_params=_cp(),
    )(sc_idx, grad, landed)


def add_owned(name, own, landed, sc_idx):
    hr, c = own.shape
    tr = _ew_rows(hr)
    nt = hr // tr

    def body(sc_ref, o_ref, l0, l1, l2, out_ref):
        out_ref[...] = ((o_ref[...] + l0[...].astype(F32)) + l1[...].astype(F32)) + l2[...].astype(F32)

    grid_spec = pltpu.PrefetchScalarGridSpec(
        num_scalar_prefetch=1, grid=(nt,),
        in_specs=[pl.BlockSpec((tr, c), lambda i, sc_ref: (i, 0))]
        + [pl.BlockSpec((None, tr, c), lambda i, sc_ref, j=j: (j, i, 0)) for j in range(3)],
        out_specs=pl.BlockSpec((tr, c), lambda i, sc_ref: (sc_ref[1] * nt + i, 0)))
    return _call(
        body, name=name, grid_spec=grid_spec, out_shape=jax.ShapeDtypeStruct((2 * hr, c), F32),
        compiler_params=_cp(),
    )(sc_idx, own, landed, landed, landed)


PACK_QUANTUM = 8 * LANES


def _pack(arrays):
    pieces = []
    for a in arrays:
        flat = a.reshape(-1)
        padded = -(-flat.shape[0] // PACK_QUANTUM) * PACK_QUANTUM
        pieces.append(jnp.pad(flat, (0, padded - flat.shape[0])).reshape(-1, LANES))
    return jnp.concatenate(pieces, axis=0)


def _unpack(packed, shapes):
    out = []
    row = 0
    for shp in shapes:
        size = math.prod(shp)
        rows = -(-size // PACK_QUANTUM) * 8
        out.append(packed[row:row + rows].reshape(-1)[:size].reshape(shp))
        row += rows
    return out


def kernel(x, p, mix_w_in, pool_w, pool_scale, conv_dw_w, conv_dw_b, conv_ln_g, conv_ln_b, mix_w_out, attn_w_qkv, attn_rel_bias, attn_w_o, ln_mix_g, ln_mix_b, ffn_w_up, ffn_dw_w, ffn_dw_b, ffn_w_down, ple_w_proj, ple_w_gate, ple_b_gate, ln_ffn_g, ln_ffn_b, loss_target, m_mix_w_in, m_pool_w, m_pool_scale, m_conv_dw_w, m_conv_dw_b, m_conv_ln_g, m_conv_ln_b, m_mix_w_out, m_attn_w_qkv, m_attn_rel_bias, m_attn_w_o, m_ln_mix_g, m_ln_mix_b, m_ffn_w_up, m_ffn_dw_w, m_ffn_dw_b, m_ffn_w_down, m_ple_w_proj, m_ple_w_gate, m_ple_b_gate, m_ln_ffn_g, m_ln_ffn_b, v_mix_w_in, v_pool_w, v_pool_scale, v_conv_dw_w, v_conv_dw_b, v_conv_ln_g, v_conv_ln_b, v_mix_w_out, v_attn_w_qkv, v_attn_rel_bias, v_attn_w_o, v_ln_mix_g, v_ln_mix_b, v_ffn_w_up, v_ffn_dw_w, v_ffn_dw_b, v_ffn_w_down, v_ple_w_proj, v_ple_w_gate, v_ple_b_gate, v_ln_ffn_g, v_ln_ffn_b):
    xi, yi, ci = _place()
    shard_idx = (2 * xi + yi).astype(jnp.int32)
    s_arr = shard_idx.reshape(1)
    c_arr = ci.astype(jnp.int32).reshape(1)
    sc_arr = jnp.concatenate([s_arr, c_arr])

    x0 = x[0]
    target = loss_target[0]
    p_rows = p.reshape(p.shape[0] * p.shape[2], p.shape[3])
    seq = x0.shape[0]

    big = [
        ("mix_w_in", mix_w_in, m_mix_w_in, v_mix_w_in, True),
        ("mix_w_out", mix_w_out, m_mix_w_out, v_mix_w_out, False),
        ("attn_w_qkv", attn_w_qkv, m_attn_w_qkv, v_attn_w_qkv, True),
        ("attn_w_o", attn_w_o, m_attn_w_o, v_attn_w_o, False),
        ("ffn_w_up", ffn_w_up, m_ffn_w_up, v_ffn_w_up, True),
        ("ffn_w_down", ffn_w_down, m_ffn_w_down, v_ffn_w_down, False),
        ("ple_w_proj", ple_w_proj, m_ple_w_proj, v_ple_w_proj, True),
        ("ple_w_gate", ple_w_gate, m_ple_w_gate, v_ple_w_gate, False),
    ]
    params = {nm: w for nm, w, _, _, _ in big}
    col_sharded = {nm: cs for nm, _, _, _, cs in big}
    keys = [("mix_w_in", 0), ("mix_w_out", 0), ("ffn_w_up", 0), ("ffn_w_down", 0), ("ple_w_gate", 0),
            ("ple_w_proj", 0), ("attn_w_qkv", 0), ("attn_w_o", 0), ("ffn_w_up", 1), ("ffn_w_down", 1),
            ("ple_w_gate", 1), ("ple_w_proj", 1)]
    dw_shapes = [conv_dw_w.shape, ffn_dw_w.shape]
    dw_block = cast_into_gathered("place_dw", _pack([conv_dw_w, ffn_dw_w])[None], 0, s_arr, dtype=F32)
    n_first = 2
    started = {}
    gather_token = None
    for tag, group in (("first", keys[:n_first]), ("rest", keys[n_first:])):
        shards = [cast_into_gathered(f"cast_{nm}_{layer}", params[nm], layer, s_arr, token=gather_token)
                  for nm, layer in group]
        if tag == "first":
            shards.append(dw_block)
        send, recv, bufs, gather_token = copies_start(f"gather_start_{tag}", shards, gather_plan, 3 * len(shards))
        for a, key in enumerate(group):
            started[key] = (send, recv, bufs[a], 3 * a)
        if tag == "first":
            dw_started = (send, recv, bufs[-1], 3 * len(group))
    arrived_w = {}

    def weight(nm, layer, after=None):
        key = (nm, layer)
        if key not in arrived_w:
            send, recv, buf, base = started[key]
            arrived_w[key] = copies_wait(f"gather_wait_{nm}_{layer}", [buf], send, recv, gather_plan, base, after)[0]
        g = arrived_w[key]
        if col_sharded[nm]:
            return g
        return g.reshape(g.shape[0] * g.shape[1], g.shape[2])

    def tie(a, token):
        return a + token[0:1, 0:1].astype(a.dtype)

    class Reducer:
        def __init__(self, tag, group):
            self.tag, self.group, self.stage = tag, group, 0
            self.n = len(group)
            self.result = None

        def advance(self, after):
            tag, n = self.tag, self.n
            if self.stage == 0:
                grads = []
                for key in self.group:
                    g = big_grads[key]
                    grads.append(g if g.ndim == 3 else g.reshape(N_SHARD, g.shape[0] // N_SHARD, g.shape[1]))
                lands = [lax.empty((N_SHARD, g.shape[1] // 2, g.shape[2]), F32) for g in grads]
                self.sems = copies_start(f"swap_start_{tag}", grads + lands, swap_plan, n)
            elif self.stage == 1:
                send, recv, bufs, _ = self.sems
                outs = copies_wait(f"swap_wait_{tag}", bufs, send, recv, swap_plan, 0, after)
                self.own, wire = [], []
                for key, g, ld in zip(self.group, outs[:n], outs[n:]):
                    o, ob = add_halves(f"add_halves_{key[0]}_{key[1]}", g, ld, sc_arr)
                    self.own.append(o)
                    wire.append(ob)
                lands = [lax.empty((3,) + w.shape[1:], BF16) for w in wire]
                self.sems = copies_start(f"owners_start_{tag}", wire + lands, owners_plan, 3 * n)
            elif self.stage == 2:
                send, recv, bufs, _ = self.sems
                outs = copies_wait(f"owners_wait_{tag}", bufs, send, recv, owners_plan, 0, after)
                finals = [add_owned(f"add_owned_{key[0]}_{key[1]}", o, ar, sc_arr)
                          for key, o, ar in zip(self.group, self.own, outs[n:])]
                self.sems = copies_start(f"join_start_{tag}", finals, join_plan, n)
            elif self.stage == 3:
                send, recv, bufs, _ = self.sems
                outs = copies_wait(f"join_wait_{tag}", bufs, send, recv, join_plan, 0, after)
                self.result = dict(zip(self.group, outs))
                self.sems = None
            self.stage += 1
            return None if self.sems is None else self.sems[3]

    dw_cache = []

    def conv_weights(after):
        if not dw_cache:
            send, recv, buf, base = dw_started
            dw_all = copies_wait("gather_wait_dw", [buf], send, recv, gather_plan, base, after)[0]
            dw_parts = [_unpack(dw_all[k], dw_shapes) for k in range(N_SHARD)]
            dw_cache.append(jnp.concatenate([pc[0] for pc in dw_parts], axis=2)[0])
            dw_cache.append(jnp.concatenate([pc[1] for pc in dw_parts], axis=2))
        return dw_cache

    big_grads = {}
    small_grads = {}

    saved = []
    h_in = x0
    h_in_b = x0
    for layer in range(N_LAYERS):
        sv = {"x_in": h_in_b}
        if layer % 2 == 0:
            u = mm_cols_fwd("mix_in", h_in_b, weight("mix_w_in", 0, gather_token), F32)
            conv_w_full, ffn_dw_full = conv_weights(u)
            cat, d_sv, e_sv, glu_sv, hh_sv, rs_sv = mixer_fwd(
                "mixer_fwd", u, pool_w[0], pool_scale, conv_w_full, conv_dw_b, conv_ln_g, conv_ln_b)
            mix = mm_rows_fwd("mix_out", cat, weight("mix_w_out", 0, cat))
            sv.update(u=u, cat=cat, d=d_sv, e=e_sv, glu=glu_sv, hh=hh_sv, rs=rs_sv)
        else:
            qkvp = mm_cols_fwd("attn_qkv", h_in_b, weight("attn_w_qkv", 0, h_in_b), BF16,
                               pad_blocks=PAD_ROWS // _row_tile(seq))
            bias = bias_tile("bias_tile", _bias_line(attn_rel_bias[0]))
            att = attn_fwd("attn_fwd", qkvp, bias)
            mix = mm_rows_fwd("attn_out", att, weight("attn_w_o", 0, att))
            sv.update(qkvp=qkvp, bias=bias, att=att)
        x1, x1_b, xh1, rs1 = ln_fwd(f"ln_mix_{layer}", h_in, mix, ln_mix_g[layer:layer + 1],
                                    ln_mix_b[layer:layer + 1])
        gv = mm_cols_fwd(f"ffn_up_{layer}", x1_b, weight("ffn_w_up", layer, x1_b), F32)
        hid = ffn_act_fwd(f"ffn_act_{layer}", gv, ffn_dw_full[layer], ffn_dw_b[layer:layer + 1])
        ffn = mm_rows_fwd(f"ffn_down_{layer}", hid, weight("ffn_w_down", layer, hid))
        pgl = mm_rows_fwd(f"ple_gate_{layer}", x1_b, weight("ple_w_gate", layer, ffn))
        pp = mm_cols_fwd(f"ple_proj_{layer}", p_rows, weight("ple_w_proj", layer, pgl), F32, part=(layer, N_LAYERS))
        bg = ple_b_gate[layer:layer + 1]
        x2, x2_b, xh2, rs2 = ln_fwd(f"ln_ffn_{layer}", x1, ffn, ln_ffn_g[layer:layer + 1], ln_ffn_b[layer:layer + 1],
                                    ple=(pgl, pp, bg), emit_y=layer < N_LAYERS - 1)
        sv.update(x1=x1_b, xh1=xh1, rs1=rs1, gv=gv, hid=hid, pgl=pgl, pp=pp, xh2=xh2, rs2=rs2)
        saved.append(sv)
        h_in, h_in_b = x2, x2_b

    reducers = []

    def open_group(tag, group):
        reducers.append(Reducer(tag, group))
        return reducers[-1].advance(None)

    def hook(after):
        token = None
        for red in reducers:
            if red.stage < 4:
                tk = red.advance(after)
                if tk is not None:
                    token = tk if token is None else token + tk
        return token

    def tied(a, token):
        return a if token is None else tie(a, token)

    parts = []
    token = None
    for layer in reversed(range(N_LAYERS)):
        sv = saved[layer]
        bg = ple_b_gate[layer:layer + 1]
        if layer == 0:
            token = open_group("layer1", [("attn_w_qkv", 0), ("attn_w_o", 0), ("ffn_w_up", 1), ("ffn_w_down", 1),
                                          ("ple_w_gate", 1), ("ple_w_proj", 1)])
        last = layer == N_LAYERS - 1
        res = ln_bwd(
            f"ln_ffn_bwd_{layer}", parts, sv["xh2"], sv["rs2"], tied(ln_ffn_g[layer:layer + 1], token),
            ple=(sv["pgl"], sv["pp"], bg), loss=(target, ln_ffn_b[layer:layer + 1]) if last else None)
        dz2, dg2, db2, dpp, dpgl, dbg = res[:6]
        if last:
            loss_part = res[6]
        small_grads[("ln_ffn_g", layer)] = dg2
        small_grads[("ln_ffn_b", layer)] = db2
        small_grads[("ple_b_gate", layer)] = dbg
        w_down = weight("ffn_w_down", layer)
        dhid = mm_rows_dx(f"ffn_down_dx_{layer}", dz2, w_down)
        big_grads[("ffn_w_down", layer)] = mm_rows_dw(f"ffn_down_dw_{layer}", sv["hid"], dz2)
        token = hook(big_grads[("ffn_w_down", layer)])
        dgv, ddw, ddb = ffn_act_bwd(f"ffn_act_bwd_{layer}", dhid, sv["gv"], ffn_dw_full[layer],
                                    tied(ffn_dw_b[layer:layer + 1], token))
        small_grads[("ffn_dw_w", layer)] = ddw
        small_grads[("ffn_dw_b", layer)] = ddb
        big_grads[("ffn_w_up", layer)] = mm_cols_dw(f"ffn_up_dw_{layer}", sv["x1"], dgv)
        t_up = mm_cols_dx(f"ffn_up_dx_{layer}", dgv, weight("ffn_w_up", layer))
        token = hook(t_up)
        big_grads[("ple_w_gate", layer)] = mm_rows_dw(f"ple_gate_dw_{layer}", sv["x1"], dpgl)
        t_gate = mm_rows_dx(f"ple_gate_dx_{layer}", dpgl, weight("ple_w_gate", layer))
        big_grads[("ple_w_proj", layer)] = mm_cols_dw(f"ple_proj_dw_{layer}", p_rows, dpp, part=(layer, N_LAYERS))
        token2 = hook(big_grads[("ple_w_proj", layer)])
        if token2 is not None:
            token = token2 if token is None else token + token2
        if layer == 0:
            token3 = open_group("layer0_ffn", [("ffn_w_up", 0), ("ffn_w_down", 0), ("ple_w_gate", 0), ("ple_w_proj", 0)])
            token = token3 if token is None else token + token3
        dz1, dg1, db1 = ln_bwd(
            f"ln_mix_bwd_{layer}", [(ALPHA, dz2), (1.0, t_up), (1.0, t_gate)], sv["xh1"], sv["rs1"],
            tied(ln_mix_g[layer:layer + 1], token))
        small_grads[("ln_mix_g", layer)] = dg1
        small_grads[("ln_mix_b", layer)] = db1
        if layer % 2 == 0:
            dcat = mm_rows_dx("mix_out_dx", dz1, weight("mix_w_out", 0))
            big_grads[("mix_w_out", 0)] = mm_rows_dw("mix_out_dw", sv["cat"], dz1)
            token = hook(big_grads[("mix_w_out", 0)])
            du, dpw, dps, dcw, dcb, dcg, dcbt = mixer_bwd(
                "mixer_bwd", dcat, sv["u"], sv["d"], sv["e"], sv["glu"], sv["hh"], sv["rs"],
                pool_w[0], pool_scale, conv_w_full, tied(conv_ln_g, token), conv_ln_b)
            small_grads[("pool_w", 0)] = dpw
            small_grads[("pool_scale", 0)] = dps
            small_grads[("conv_dw_w", 0)] = dcw
            small_grads[("conv_dw_b", 0)] = dcb
            small_grads[("conv_ln_g", 0)] = dcg
            small_grads[("conv_ln_b", 0)] = dcbt
            big_grads[("mix_w_in", 0)] = mm_cols_dw("mix_in_dw", sv["x_in"], du)
            hook(big_grads[("mix_w_in", 0)])
            open_group("layer0_mix", [("mix_w_in", 0), ("mix_w_out", 0)])
            dx_in = mm_cols_dx("mix_in_dx", du, weight("mix_w_in", 0), addend=(ALPHA, dz1))
            token = hook(dx_in)
        else:
            do = mm_rows_dx("attn_out_dx", dz1, weight("attn_w_o", 0), out_dtype=BF16)
            big_grads[("attn_w_o", 0)] = mm_rows_dw("attn_out_dw", sv["att"], dz1)
            dq, dk, dv, ds_sum = attn_bwd("attn_bwd", sv["qkvp"], sv["bias"], do)
            cols, sat = bias_grad_reduce("bias_grad", ds_sum)
            d_rel = jnp.concatenate(
                [jnp.zeros((N_HEADS, 1), F32),
                 jnp.flip(cols[:, 0, Q_TILE + SHEAR_SAT:Q_TILE - 1 + SHEAR_W], axis=1),
                 sat[:, 0, 0:1]], axis=1)
            small_grads[("attn_rel_bias", 0)] = d_rel
            dqkv = jnp.concatenate([dq, dk, dv], axis=1)
            big_grads[("attn_w_qkv", 0)] = mm_cols_dw("attn_qkv_dw", sv["x_in"], dqkv)
            dx_in = mm_cols_dx("attn_qkv_dx", dqkv, weight("attn_w_qkv", 0), addend=(ALPHA, dz1))
        parts = [(1.0, dx_in)]
    grad_x = dx_in

    small = [
        ("pool_w", pool_w, m_pool_w, v_pool_w, None),
        ("pool_scale", pool_scale, m_pool_scale, v_pool_scale, None),
        ("conv_dw_w", conv_dw_w, m_conv_dw_w, v_conv_dw_w, 2),
        ("conv_dw_b", conv_dw_b, m_conv_dw_b, v_conv_dw_b, None),
        ("conv_ln_g", conv_ln_g, m_conv_ln_g, v_conv_ln_g, None),
        ("conv_ln_b", conv_ln_b, m_conv_ln_b, v_conv_ln_b, None),
        ("attn_rel_bias", attn_rel_bias, m_attn_rel_bias, v_attn_rel_bias, None),
        ("ln_mix_g", ln_mix_g, m_ln_mix_g, v_ln_mix_g, None),
        ("ln_mix_b", ln_mix_b, m_ln_mix_b, v_ln_mix_b, None),
        ("ffn_dw_w", ffn_dw_w, m_ffn_dw_w, v_ffn_dw_w, 2),
        ("ffn_dw_b", ffn_dw_b, m_ffn_dw_b, v_ffn_dw_b, None),
        ("ple_b_gate", ple_b_gate, m_ple_b_gate, v_ple_b_gate, None),
        ("ln_ffn_g", ln_ffn_g, m_ln_ffn_g, v_ln_ffn_g, None),
        ("ln_ffn_b", ln_ffn_b, m_ln_ffn_b, v_ln_ffn_b, None),
    ]
    full_grads = []
    for nm, w, _, _, shard_axis in small:
        full = list(w.shape)
        if shard_axis is not None:
            full[shard_axis] *= N_SHARD
        per_layer = [small_grads[(nm, layer)].reshape((1,) + tuple(full[1:])) for layer in range(w.shape[0])]
        full_grads.append(jnp.concatenate(per_layer, axis=0))
    packed = _pack(full_grads + [loss_part])
    dev_arr = (4 * xi + 2 * yi + ci).astype(jnp.int32).reshape(1)
    sg_block = cast_into_gathered("place_small_grads", packed[None], 0, dev_arr, n_blocks=8, dtype=F32)
    sg_send, sg_recv, sg_bufs, sg_token = copies_start("small_grads_start", [sg_block], all_plan, 7)
    token = sg_token if token is None else token + sg_token

    shard_grads = {}
    for red in reducers:
        if red.stage == 4:
            shard_grads.update(red.result)
    big_out = {}

    def update_big(names, tok):
        for nm, w, m, v, _ in big:
            if nm in names:
                gl = [shard_grads[(nm, layer)] for layer in range(w.shape[0])]
                delta, new_m, new_v = adamw(f"adamw_{nm}", w, gl, m, v, token=tok)
                big_out[nm] = (jnp.stack(gl, axis=0), delta, new_m, new_v)

    last_group = ("mix_w_in", "mix_w_out")
    update_big([nm for nm, _, _, _, _ in big if nm not in last_group], token)
    token = hook(big_out["ffn_w_up"][1])

    gathered_sg = copies_wait("small_grads_wait", sg_bufs, sg_send, sg_recv, all_plan, 0, big_out["ffn_w_down"][1])[0]
    total = sum_blocks("sum_small", gathered_sg.reshape(8 * packed.shape[0], LANES), 8)
    unpacked = _unpack(total, [g.shape for g in full_grads] + [loss_part.shape])
    loss = unpacked[-1][0, 0]
    local_grads = []
    for (nm, w, _, _, shard_axis), g in zip(small, unpacked[:-1]):
        if shard_axis is not None:
            width = w.shape[shard_axis]
            g = lax.dynamic_slice_in_dim(g, shard_idx * width, width, axis=shard_axis)
        local_grads.append(g.reshape(w.shape))
    updated = adamw_many("adamw_small", [w for _, w, _, _, _ in small], local_grads,
                         [m for _, _, m, _, _ in small], [v for _, _, _, v, _ in small], token)
    hook(updated[0][0])
    for red in reducers:
        shard_grads.update(red.result)
    update_big(last_group, None)
    small_out = {}
    for (nm, _, _, _, _), g, (d_, m_, v_) in zip(small, local_grads, updated):
        small_out[nm] = (g, d_, m_, v_)

    order = ["mix_w_in", "pool_w", "pool_scale", "conv_dw_w", "conv_dw_b", "conv_ln_g", "conv_ln_b", "mix_w_out",
             "attn_w_qkv", "attn_rel_bias", "attn_w_o", "ln_mix_g", "ln_mix_b", "ffn_w_up", "ffn_dw_w", "ffn_dw_b",
             "ffn_w_down", "ple_w_proj", "ple_w_gate", "ple_b_gate", "ln_ffn_g", "ln_ffn_b"]
    res = {**big_out, **small_out}
    outs = [loss, grad_x[None]]
    for slot in range(4):
        outs += [res[nm][slot] for nm in order]
    return tuple(outs)
```

```python
import math

import jax
import jax.numpy as jnp
from jax import lax
from jax.experimental import pallas as pl
from jax.experimental.pallas import tpu as pltpu

F32 = jnp.float32
BF16 = jnp.bfloat16
MESH = pl.DeviceIdType.MESH

N_LAYERS = 2
ALPHA = (2 * N_LAYERS) ** 0.25
LN_EPS = 1e-5
NEG_INF = -1e30
CHUNK = 64
LEFT_CHUNKS = 8
PAD_ROWS = LEFT_CHUNKS * CHUNK
HEAD_DIM = 64
ATTN_SCALE = HEAD_DIM ** -0.5
N_HEADS = 16
MAX_REL = 256
POOL_WINDOWS = (2, 4, 8, 16)
POOL_GROUP = 128
CONV_K = 31
FFN_K = 3
CONV_HALO = 32
FFN_HALO = 8
FFN_TILE = 256
FFN_CHUNK_ROWS = 128
FFN_CHUNK_LANES = 128
Q_TILE = 256
K_WIN = Q_TILE + PAD_ROWS
LANES = 128
SUBLANES = 8
ATTN_PAIRS = 2
ATTN_PAIRS_FWD = 8
ATTN_LANES = ATTN_PAIRS * LANES
SHEAR_W = Q_TILE + K_WIN
SHEAR_SAT = SHEAR_W - 2 * MAX_REL
N_SHARD = 4

ADAM_LR = 0.001
ADAM_B1 = 0.9
ADAM_B2 = 0.999
ADAM_EPS = 1e-08
ADAM_WD = 0.01
ADAM_STEP = 10
ADAM_BC1 = 1.0 - ADAM_B1 ** ADAM_STEP
ADAM_BC2 = 1.0 - ADAM_B2 ** ADAM_STEP

DIMS = {
    "nn": (((1,), (0,)), ((), ())),
    "nt": (((1,), (1,)), ((), ())),
    "tn": (((0,), (0,)), ((), ())),
}


def _cp(vmem_mb=48, **kw):
    return pltpu.CompilerParams(vmem_limit_bytes=vmem_mb * 1024 * 1024, **kw)


def _in_hbm(a):
    return pltpu.with_memory_space_constraint(a, pltpu.HBM)


STAGING_LIMIT_BYTES = 1 << 20
SMALL_WEIGHT_BYTES = 1 << 22
SUM_BLOCK_ROWS = 2048
SMALL_BLOCK_BYTES = 1 << 19


def _call(body, **kw):
    call = pl.pallas_call(body, **kw)

    def run(*args):
        pinned = []
        for a in args:
            big = a.size * a.dtype.itemsize >= STAGING_LIMIT_BYTES
            pinned.append(_in_hbm(a) if big and not jnp.issubdtype(a.dtype, jnp.integer) else a)
        return call(*pinned)

    return run


def _dot(a, b, mode):
    return lax.dot_general(a.astype(BF16), b.astype(BF16), DIMS[mode], preferred_element_type=F32)


def _sig(x):
    return 1.0 / (1.0 + jnp.exp(-x))


def _row_tile(s):
    return min(512, s // 4)


def _mm_tile(s):
    return min(1024, s // 4)


def _mm(name, mode, a, b, in_specs, out_shape, out_spec, acc_shape, grid, nk, zero_first=False, vmem_mb=48,
        addend=None):
    out_f32 = out_shape.dtype == F32

    def body(a_ref, b_ref, *rest):
        k = pl.program_id(2)
        if addend is None:
            o_ref, scr = rest[0], rest[1:]
        else:
            add_ref, o_ref, scr = rest[0], rest[1], rest[2:]

        def compute():
            part = _dot(a_ref[...], b_ref[...], mode)
            if nk == 1:
                if addend is not None:
                    part = part + addend[0] * add_ref[...]
                o_ref[...] = part.astype(o_ref.dtype)
                return
            acc = o_ref if out_f32 else scr[0]

            @pl.when(k == 0)
            def _():
                acc[...] = part if addend is None else part + addend[0] * add_ref[...]

            @pl.when(k > 0)
            def _():
                acc[...] += part

            if not out_f32:
                @pl.when(k == nk - 1)
                def _():
                    o_ref[...] = acc[...].astype(o_ref.dtype)

        if zero_first:
            @pl.when(pl.program_id(1) == 0)
            def _():
                o_ref[...] = jnp.zeros(o_ref.shape, o_ref.dtype)

            pl.when(pl.program_id(1) > 0)(compute)
        else:
            compute()

    scratch = [] if (nk == 1 or out_f32) else [pltpu.VMEM(acc_shape, F32)]
    operands = [a, b] if addend is None else [a, b, addend[1]]
    specs = list(in_specs) if addend is None else list(in_specs) + [out_spec]
    return _call(
        body, name=name, grid=grid, in_specs=specs, out_specs=out_spec, out_shape=out_shape,
        scratch_shapes=scratch, compiler_params=_cp(vmem_mb),
    )(*operands)


def _is_small_weight(wc):
    return wc.size * 2 <= SMALL_WEIGHT_BYTES


def _all_shards(w_ref):
    return jnp.concatenate([w_ref[j] for j in range(N_SHARD)], axis=1)


def mm_cols_fwd(name, a, wc, out_dtype, pad_blocks=0, part=(0, 1)):
    s, k = a.shape
    s //= part[1]
    n4 = wc.shape[2]
    tm = _row_tile(s) if pad_blocks else _mm_tile(s)
    nt = s // tm
    first_block = part[0] * nt
    if _is_small_weight(wc) and not pad_blocks:
        def body(a_ref, w_ref, o_ref):
            o_ref[...] = _dot(a_ref[...], _all_shards(w_ref), "nn").astype(o_ref.dtype)

        return _call(
            body, name=name, grid=(nt,),
            in_specs=[pl.BlockSpec((tm, k), lambda i: (first_block + i, 0)), _full(wc.shape)],
            out_specs=pl.BlockSpec((tm, N_SHARD * n4), lambda i: (i, 0)),
            out_shape=jax.ShapeDtypeStruct((s, N_SHARD * n4), out_dtype), compiler_params=_cp(),
        )(a, wc)
    return _mm(
        name, "nn", a, wc,
        [pl.BlockSpec((tm, k), lambda j, i, r: (first_block + jnp.maximum(i - pad_blocks, 0), 0)),
         pl.BlockSpec((None, k, n4), lambda j, i, r: (j, 0, 0))],
        jax.ShapeDtypeStruct((s + pad_blocks * tm, N_SHARD * n4), out_dtype),
        pl.BlockSpec((tm, n4), lambda j, i, r: (i, j)),
        None, (N_SHARD, nt + pad_blocks, 1), 1, zero_first=pad_blocks > 0)


def mm_cols_dx(name, dy, wc, addend=None):
    s = dy.shape[0]
    _, k, n4 = wc.shape
    tm = _mm_tile(s)
    if _is_small_weight(wc):
        def body(dy_ref, w_ref, *rest):
            part = _dot(dy_ref[...], _all_shards(w_ref), "nt")
            rest[-1][...] = part if addend is None else part + addend[0] * rest[0][...]

        out_spec = pl.BlockSpec((tm, k), lambda i: (i, 0))
        extra, extra_specs = ([], []) if addend is None else ([addend[1]], [out_spec])
        return _call(
            body, name=name, grid=(s // tm,),
            in_specs=[pl.BlockSpec((tm, N_SHARD * n4), lambda i: (i, 0)), _full(wc.shape)] + extra_specs,
            out_specs=out_spec, out_shape=jax.ShapeDtypeStruct((s, k), F32), compiler_params=_cp(),
        )(dy, wc, *extra)
    return _mm(
        name, "nt", dy, wc,
        [pl.BlockSpec((tm, n4), lambda g, i, r: (i, r)),
         pl.BlockSpec((None, k, n4), lambda g, i, r: (r, 0, 0))],
        jax.ShapeDtypeStruct((s, k), F32),
        pl.BlockSpec((tm, k), lambda g, i, r: (i, 0)),
        (tm, k), (1, s // tm, N_SHARD), N_SHARD, addend=addend)


def mm_cols_dw(name, a, dy, part=(0, 1)):
    s, k = a.shape
    s //= part[1]
    n4 = dy.shape[1] // N_SHARD
    tm = _mm_tile(s)
    nt = s // tm
    first_block = part[0] * nt
    if k * n4 * N_SHARD * 2 <= SMALL_WEIGHT_BYTES:
        def body(a_ref, dy_ref, o_ref):
            full = _dot(a_ref[...], dy_ref[...], "tn")
            first = pl.program_id(0) == 0
            for j in range(N_SHARD):
                _acc_add(o_ref.at[j], first, full[:, j * n4:(j + 1) * n4])

        return _call(
            body, name=name, grid=(nt,),
            in_specs=[pl.BlockSpec((tm, k), lambda r: (first_block + r, 0)),
                      pl.BlockSpec((tm, N_SHARD * n4), lambda r: (r, 0))],
            out_specs=_full((N_SHARD, k, n4)),
            out_shape=jax.ShapeDtypeStruct((N_SHARD, k, n4), F32), compiler_params=_cp(),
        )(a, dy)
    return _mm(
        name, "tn", a, dy,
        [pl.BlockSpec((tm, k), lambda j, g, r: (first_block + r, 0)),
         pl.BlockSpec((tm, n4), lambda j, g, r: (r, j))],
        jax.ShapeDtypeStruct((N_SHARD, k, n4), F32),
        pl.BlockSpec((None, k, n4), lambda j, g, r: (j, 0, 0)),
        (k, n4), (N_SHARD, 1, nt), nt)


def _k_tile(k):
    return k if k <= 1024 else k // 2


def mm_rows_fwd(name, a, wr, out_dtype=F32):
    s, k = a.shape
    n = wr.shape[1]
    tm = _mm_tile(s)
    tk = _k_tile(k)
    nk = k // tk
    return _mm(
        name, "nn", a, wr,
        [pl.BlockSpec((tm, tk), lambda g, i, r: (i, r)),
         pl.BlockSpec((tk, n), lambda g, i, r: (r, 0))],
        jax.ShapeDtypeStruct((s, n), out_dtype),
        pl.BlockSpec((tm, n), lambda g, i, r: (i, 0)),
        (tm, n), (1, s // tm, nk), nk)


def mm_rows_dx(name, dy, wr, out_dtype=F32):
    s, n = dy.shape
    k = wr.shape[0]
    tm = _mm_tile(s)
    tk = _k_tile(k)
    return _mm(
        name, "nt", dy, wr,
        [pl.BlockSpec((tm, n), lambda j, i, r: (i, 0)),
         pl.BlockSpec((tk, n), lambda j, i, r: (j, 0))],
        jax.ShapeDtypeStruct((s, k), out_dtype),
        pl.BlockSpec((tm, tk), lambda j, i, r: (i, j)),
        None, (k // tk, s // tm, 1), 1)


def mm_rows_dw(name, a, dy):
    s, k = a.shape
    n = dy.shape[1]
    tm = _mm_tile(s)
    tk = _k_tile(k)
    nt = s // tm
    return _mm(
        name, "tn", a, dy,
        [pl.BlockSpec((tm, tk), lambda j, g, r: (r, j)),
         pl.BlockSpec((tm, n), lambda j, g, r: (r, 0))],
        jax.ShapeDtypeStruct((k, n), F32),
        pl.BlockSpec((tk, n), lambda j, g, r: (j, 0)),
        (tk, n), (k // tk, 1, nt), nt)


def _row(tm, c, col=0):
    return pl.BlockSpec((tm, c), lambda i: (i, col))


def _full(shape):
    nd = len(shape)
    return pl.BlockSpec(shape, lambda i: (0,) * nd)


def _prev(tm, h, c, col=0):
    return pl.BlockSpec((h, c), lambda i: (jnp.maximum(i * (tm // h) - 1, 0), col))


def _next(tm, h, c, s, col=0):
    return pl.BlockSpec((h, c), lambda i: (jnp.minimum((i + 1) * (tm // h), s // h - 1), col))


def _acc_add(ref, first, val):
    @pl.when(first)
    def _():
        ref[...] = val

    @pl.when(jnp.logical_not(first))
    def _():
        ref[...] += val


def _colsum(v):
    return jnp.sum(v, axis=0, keepdims=True)


def _ln_stats(z):
    mu = jnp.mean(z, axis=-1, keepdims=True)
    zc = z - mu
    var = jnp.mean(zc * zc, axis=-1, keepdims=True)
    rstd = lax.rsqrt(var + LN_EPS)
    return zc * rstd, rstd


def _ln_bwd(dxhat, xhat, rstd):
    m1 = jnp.mean(dxhat, axis=-1, keepdims=True)
    m2 = jnp.mean(dxhat * xhat, axis=-1, keepdims=True)
    return rstd * (dxhat - m1 - xhat * m2)


def ln_fwd(name, x, f, g, b, ple=None, emit_y=True):
    s, d = x.shape
    tm = _row_tile(s)
    n_in = 2 + (3 if ple is not None else 0)

    def body(*refs):
        x_ref, f_ref = refs[0], refs[1]
        g_ref, b_ref = refs[n_in], refs[n_in + 1]
        xh_ref, rs_ref = refs[-2:]
        z = ALPHA * x_ref[...] + f_ref[...]
        if ple is not None:
            pgl_ref, pp_ref, bg_ref = refs[2:5]
            z = z + _sig(pgl_ref[...] + bg_ref[...]) * pp_ref[...]
        xhat, rstd = _ln_stats(z)
        if emit_y:
            y = xhat * g_ref[...] + b_ref[...]
            refs[n_in + 2][...] = y
            refs[n_in + 3][...] = y.astype(BF16)
        xh_ref[...] = xhat
        rs_ref[...] = jnp.broadcast_to(rstd, rs_ref.shape)

    ins = [x, f]
    specs = [_row(tm, d), _row(tm, d)]
    if ple is not None:
        pgl, pp, bg = ple
        ins += [pgl, pp, bg]
        specs += [_row(tm, d), _row(tm, d), _full((1, d))]
    ins += [g, b]
    specs += [_full((1, d)), _full((1, d))]
    y_shapes = [jax.ShapeDtypeStruct((s, d), F32), jax.ShapeDtypeStruct((s, d), BF16)] if emit_y else []
    outs = _call(
        body, name=name, grid=(s // tm,), in_specs=specs,
        out_specs=[_row(tm, d)] * (len(y_shapes) + 1) + [_row(tm, LANES)],
        out_shape=y_shapes + [jax.ShapeDtypeStruct((s, d), F32), jax.ShapeDtypeStruct((s, LANES), F32)],
        compiler_params=_cp(),
    )(*ins)
    return tuple(outs) if emit_y else (None, None, outs[0], outs[1])


def ln_bwd(name, parts, xhat, rstd, g, ple=None, loss=None):
    s, d = xhat.shape
    tm = _row_tile(s)
    coefs = [c for c, _ in parts]
    n_p = len(parts)
    n_ple = 3 if ple is not None else 0
    n_in = n_p + 3 + n_ple + (2 if loss is not None else 0)

    def body(*refs):
        first = pl.program_id(0) == 0
        xh = refs[n_p][...]
        rs = refs[n_p + 1][:, 0:1]
        g_v = refs[n_p + 2][...]
        outs = refs[n_in:]
        if loss is not None:
            t_ref, b_ref = refs[n_p + 3 + n_ple:n_p + 5 + n_ple]
            err = (xh * g_v + b_ref[...]) - t_ref[...]
            dy = err * (1.0 / d)
            part = 0.5 * jnp.sum(jnp.mean(err * err, axis=-1, keepdims=True), axis=0, keepdims=True)
            _acc_add(outs[-1], first, jnp.broadcast_to(part, outs[-1].shape))
        else:
            dy = coefs[0] * refs[0][...].astype(F32)
            for j in range(1, n_p):
                dy = dy + coefs[j] * refs[j][...].astype(F32)
        dz = _ln_bwd(dy * g_v, xh, rs)
        outs[0][...] = dz
        _acc_add(outs[1], first, _colsum(dy * xh))
        _acc_add(outs[2], first, _colsum(dy))
        if ple is not None:
            pgl_ref, pp_ref, bg_ref = refs[n_p + 3:n_p + 6]
            pg = _sig(pgl_ref[...] + bg_ref[...])
            dpgl = dz * pp_ref[...] * pg * (1.0 - pg)
            outs[3][...] = (dz * pg).astype(BF16)
            outs[4][...] = dpgl.astype(BF16)
            _acc_add(outs[5], first, _colsum(dpgl))

    ins = [p for _, p in parts] + [xhat, rstd, g]
    specs = [_row(tm, d)] * n_p + [_row(tm, d), _row(tm, LANES), _full((1, d))]
    out_specs = [_row(tm, d), _full((1, d)), _full((1, d))]
    out_shape = [jax.ShapeDtypeStruct((s, d), F32), jax.ShapeDtypeStruct((1, d), F32),
                 jax.ShapeDtypeStruct((1, d), F32)]
    if ple is not None:
        pgl, pp, bg = ple
        ins += [pgl, pp, bg]
        specs += [_row(tm, d), _row(tm, d), _full((1, d))]
        out_specs += [_row(tm, d), _row(tm, d), _full((1, d))]
        out_shape += [jax.ShapeDtypeStruct((s, d), BF16), jax.ShapeDtypeStruct((s, d), BF16),
                      jax.ShapeDtypeStruct((1, d), F32)]
    if loss is not None:
        target, b = loss
        ins += [target, b]
        specs += [_row(tm, d), _full((1, d))]
        out_specs += [_full((8, LANES))]
        out_shape += [jax.ShapeDtypeStruct((8, LANES), F32)]
    return _call(
        body, name=name, grid=(s // tm,), in_specs=specs, out_specs=out_specs, out_shape=out_shape,
        compiler_params=_cp(),
    )(*ins)


def _fill_rotations(rot_ref, x, direction):
    n = x.shape[0]
    rot_ref[0] = x
    for b in range(1, SUBLANES):
        if direction < 0:
            rot_ref[b, SUBLANES:n, :] = x[SUBLANES - b:n - b]
        else:
            rot_ref[b, 0:n - SUBLANES, :] = x[b:n - SUBLANES + b]


def _rotated(rot_ref, start, rows, cs, direction=-1):
    b = (-start) % SUBLANES if direction < 0 else start % SUBLANES
    aligned = start + b if direction < 0 else start - b
    return rot_ref[b, pl.ds(aligned, rows), cs]


def _tile_pos(i, tm, rows):
    return (i * tm + lax.broadcasted_iota(jnp.int32, (rows, 1), 0) + 1).astype(F32)


def mixer_fwd(name, u, pool_w, pool_scale, conv_w, conv_b, cn_g, cn_b):
    s = u.shape[0]
    dp = 512
    tm = min(256, s // 4)
    h = CONV_HALO

    def body(a_c, a_p, bv_c, bv_p, bg_c, bg_p, pw_ref, ps_ref, cw_ref, cb_ref, cg_ref, cbt_ref,
             cat_ref, d_ref, e_ref, glu_ref, hh_ref, rs_ref, ext_a, rot_g, conv_out):
        i = pl.program_id(0)
        first = i == 0
        ext_a[0:h, :] = jnp.where(first, 0.0, a_p[...])
        ext_a[h:, :] = a_c[...]
        glu = bv_c[...] * _sig(bg_c[...])
        glu_ref[...] = glu
        _fill_rotations(rot_g, jnp.concatenate([jnp.where(first, 0.0, bv_p[...] * _sig(bg_p[...])), glu], axis=0), -1)
        pos = _tile_pos(i, tm, tm)
        for gi, w in enumerate(POOL_WINDOWS):
            cs = slice(gi * POOL_GROUP, (gi + 1) * POOL_GROUP)
            a_g = ext_a[pl.ds(h, tm), cs]
            acc = a_g
            for sh in range(1, w):
                acc = acc + ext_a[pl.ds(h - sh, tm), cs]
            d_g = acc / jnp.minimum(pos, float(w)) - a_g
            d_ref[:, cs] = d_g.astype(BF16)
            e_g = _dot(d_g, pw_ref[gi], "nn")
            e_ref[:, cs] = e_g
            cat_ref[:, cs] = (e_g * ps_ref[:, cs]).astype(BF16)
        for lg in range(dp // LANES):
            cs = slice(lg * LANES, (lg + 1) * LANES)
            acc = jnp.broadcast_to(cb_ref[:, cs], (tm, LANES))
            for sh in range(CONV_K):
                acc = acc + _rotated(rot_g, h - sh, tm, cs) * cw_ref[pl.ds(CONV_K - 1 - sh, 1), cs]
            conv_out[:, cs] = acc
        hhat, rstd = _ln_stats(conv_out[...])
        hl = hhat * cg_ref[...] + cbt_ref[...]
        cat_ref[:, dp:] = (hl * _sig(hl)).astype(BF16)
        hh_ref[...] = hhat
        rs_ref[...] = jnp.broadcast_to(rstd, rs_ref.shape)

    specs = [_row(tm, dp, 0), _prev(tm, h, dp, 0), _row(tm, dp, 1), _prev(tm, h, dp, 1),
             _row(tm, dp, 2), _prev(tm, h, dp, 2),
             _full((4, POOL_GROUP, POOL_GROUP)), _full((1, dp)), _full((CONV_K, dp)),
             _full((1, dp)), _full((1, dp)), _full((1, dp))]
    out_specs = [_row(tm, 2 * dp), _row(tm, dp), _row(tm, dp), _row(tm, dp), _row(tm, dp), _row(tm, LANES)]
    out_shape = [jax.ShapeDtypeStruct((s, 2 * dp), BF16), jax.ShapeDtypeStruct((s, dp), BF16),
                 jax.ShapeDtypeStruct((s, dp), F32), jax.ShapeDtypeStruct((s, dp), F32),
                 jax.ShapeDtypeStruct((s, dp), F32), jax.ShapeDtypeStruct((s, LANES), F32)]
    return _call(
        body, name=name, grid=(s // tm,), in_specs=specs, out_specs=out_specs, out_shape=out_shape,
        scratch_shapes=[pltpu.VMEM((h + tm, dp), F32), pltpu.VMEM((SUBLANES, h + tm, dp), F32),
                        pltpu.VMEM((tm, dp), F32)],
        compiler_params=_cp(),
    )(u, u, u, u, u, u, pool_w, pool_scale, conv_w, conv_b, cn_g, cn_b)


def mixer_bwd(name, dcat, u, d_sv, e_sv, glu_sv, hh_sv, rs_sv, pool_w, pool_scale, conv_w, cn_g, cn_b):
    s = u.shape[0]
    dp = 512
    tm = min(256, s // 4)
    h = CONV_HALO
    nt = s // tm

    def body(dc_c, dc_n, bv_c, bg_c, d_c, e_c, gl_c, gl_p, hh_c, hh_n, rs_c, rs_n,
             pw_ref, ps_ref, cw_ref, cg_ref, cbt_ref,
             du_ref, dpw_ref, dps_ref, dcw_ref, dcb_ref, dcg_ref, dcbt_ref,
             ext_dh, ext_g, ext_r):
        i = pl.program_id(0)
        first = i == 0
        last = i == nt - 1
        cg = cg_ref[...]

        def conv_grads(dyb, hhat, rstd):
            hl = hhat * cg + cbt_ref[...]
            sg = _sig(hl)
            dhl = dyb * (sg * (1.0 + hl * (1.0 - sg)))
            return _ln_bwd(dhl * cg, hhat, rstd), dhl

        hh_cur = hh_c[...]
        dh_c, dhl_c = conv_grads(dc_c[:, dp:], hh_cur, rs_c[:, 0:1])
        dh_n, _ = conv_grads(dc_n[:, dp:], hh_n[...], rs_n[:, 0:1])
        _fill_rotations(ext_dh, jnp.concatenate([dh_c, jnp.where(last, 0.0, dh_n)], axis=0), 1)
        _fill_rotations(ext_g, jnp.concatenate([jnp.where(first, 0.0, gl_p[...]), gl_c[...]], axis=0), -1)

        @pl.when(first)
        def _():
            dcw_ref[...] = jnp.zeros(dcw_ref.shape, F32)

        for lg in range(dp // LANES):
            cs = slice(lg * LANES, (lg + 1) * LANES)
            dglu = jnp.zeros((tm, LANES), F32)
            for sh in range(CONV_K):
                dglu = dglu + _rotated(ext_dh, sh, tm, cs, 1) * cw_ref[pl.ds(CONV_K - 1 - sh, 1), cs]
            dh_g = ext_dh[0, pl.ds(0, tm), cs]
            for sh in range(CONV_K):
                dcw_ref[pl.ds(CONV_K - 1 - sh, 1), cs] += _colsum(dh_g * _rotated(ext_g, h - sh, tm, cs))
            sgate = _sig(bg_c[:, cs])
            du_ref[:, dp + lg * LANES:dp + (lg + 1) * LANES] = dglu * sgate
            du_ref[:, 2 * dp + lg * LANES:2 * dp + (lg + 1) * LANES] = dglu * bv_c[:, cs] * sgate * (1.0 - sgate)
        _acc_add(dcb_ref, first, _colsum(dh_c))
        _acc_add(dcg_ref, first, _colsum(dhl_c * hh_cur))
        _acc_add(dcbt_ref, first, _colsum(dhl_c))

        pos_c = _tile_pos(i, tm, tm)
        pos_n = _tile_pos(i + 1, tm, h)
        _acc_add(dps_ref, first, _colsum(dc_c[:, :dp] * e_c[...]))
        for gi, w in enumerate(POOL_WINDOWS):
            cs = slice(gi * POOL_GROUP, (gi + 1) * POOL_GROUP)
            pw = pw_ref[gi]
            de_c = dc_c[:, cs] * ps_ref[:, cs]
            de_n = dc_n[:, cs] * ps_ref[:, cs]
            dd_c = _dot(de_c, pw, "nt")
            dd_n = _dot(de_n, pw, "nt")
            ext_r[0:tm, :] = dd_c / jnp.minimum(pos_c, float(w))
            ext_r[tm:, :] = jnp.where(last, 0.0, dd_n / jnp.minimum(pos_n, float(w)))
            acc = -dd_c
            for sh in range(w):
                acc = acc + ext_r[pl.ds(sh, tm), :]
            du_ref[:, cs] = acc
            dpw_g = _dot(d_c[:, cs], de_c, "tn")

            @pl.when(first)
            def _():
                dpw_ref[gi] = dpw_g

            @pl.when(jnp.logical_not(first))
            def _():
                dpw_ref[gi] += dpw_g

    specs = [_row(tm, 2 * dp), _next(tm, h, 2 * dp, s), _row(tm, dp, 1), _row(tm, dp, 2),
             _row(tm, dp), _row(tm, dp), _row(tm, dp), _prev(tm, h, dp),
             _row(tm, dp), _next(tm, h, dp, s), _row(tm, LANES), _next(tm, h, LANES, s),
             _full((4, POOL_GROUP, POOL_GROUP)), _full((1, dp)), _full((CONV_K, dp)),
             _full((1, dp)), _full((1, dp))]
    out_specs = [_row(tm, 3 * dp), _full((4, POOL_GROUP, POOL_GROUP)), _full((1, dp)), _full((CONV_K, dp)),
                 _full((1, dp)), _full((1, dp)), _full((1, dp))]
    out_shape = [jax.ShapeDtypeStruct((s, 3 * dp), F32),
                 jax.ShapeDtypeStruct((4, POOL_GROUP, POOL_GROUP), F32), jax.ShapeDtypeStruct((1, dp), F32),
                 jax.ShapeDtypeStruct((CONV_K, dp), F32), jax.ShapeDtypeStruct((1, dp), F32),
                 jax.ShapeDtypeStruct((1, dp), F32), jax.ShapeDtypeStruct((1, dp), F32)]
    return _call(
        body, name=name, grid=(nt,), in_specs=specs, out_specs=out_specs, out_shape=out_shape,
        scratch_shapes=[pltpu.VMEM((SUBLANES, tm + h, dp), F32), pltpu.VMEM((SUBLANES, h + tm, dp), F32),
                        pltpu.VMEM((tm + h, POOL_GROUP), F32)],
        compiler_params=_cp(),
    )(dcat, dcat, u, u, d_sv, e_sv, glu_sv, glu_sv, hh_sv, hh_sv, rs_sv, rs_sv,
      pool_w, pool_scale, conv_w, cn_g, cn_b)


GELU_C = math.sqrt(2.0 / math.pi)


def _gelu_parts(x):
    x2 = x * x
    t = jnp.tanh(x * (GELU_C + (GELU_C * 0.044715) * x2))
    half_1pt = 0.5 + 0.5 * t
    gelu = x * half_1pt
    dgelu = half_1pt + (0.5 * x) * (1.0 - t * t) * (GELU_C + (3.0 * GELU_C * 0.044715) * x2)
    return gelu, dgelu


def ffn_act_fwd(name, gv, dw_w, dw_b):
    s = gv.shape[0]
    dff = gv.shape[1] // 2
    tm = min(FFN_TILE, s // 4)
    h = FFN_HALO
    rc = FFN_CHUNK_ROWS
    lw = FFN_CHUNK_LANES

    def body(g_c, g_p, v_c, w_ref, b_ref, hid_ref):
        first = pl.program_id(0) == 0

        def chunk(ci, carry):
            r0 = pl.multiple_of(ci * rc, rc)
            above = pl.multiple_of(jnp.maximum(r0 - h, 0), h)
            for lg in range(dff // lw):
                cs = slice(lg * lw, (lg + 1) * lw)
                top = jnp.where(ci == 0, jnp.where(first, 0.0, g_p[:, cs]), g_c[pl.ds(above, h), cs])
                win = jnp.concatenate([top, g_c[pl.ds(r0, rc), cs]], axis=0)
                gc = jnp.broadcast_to(b_ref[:, cs], (rc, lw))
                for sh in range(FFN_K):
                    gc = gc + win[h - sh:h - sh + rc] * w_ref[pl.ds(FFN_K - 1 - sh, 1), cs]
                gelu, _ = _gelu_parts(gc)
                hid_ref[pl.ds(r0, rc), cs] = (gelu * v_c[pl.ds(r0, rc), cs]).astype(BF16)
            return carry

        lax.fori_loop(0, tm // rc, chunk, 0)

    return _call(
        body, name=name, grid=(s // tm,),
        in_specs=[_row(tm, dff, 0), _prev(tm, h, dff, 0), _row(tm, dff, 1), _full((FFN_K, dff)), _full((1, dff))],
        out_specs=_row(tm, dff), out_shape=jax.ShapeDtypeStruct((s, dff), BF16),
        compiler_params=_cp(),
    )(gv, gv, gv, dw_w, dw_b)


def ffn_act_bwd(name, dhid, gv, dw_w, dw_b):
    s = gv.shape[0]
    dff = gv.shape[1] // 2
    tm = min(FFN_TILE, s // 4)
    h = FFN_HALO
    nt = s // tm
    rc = FFN_CHUNK_ROWS
    lw = FFN_CHUNK_LANES
    n_chunks = tm // rc

    def body(dh_c, dh_n, g_p, g_c, g_n, v_c, v_n, w_ref, b_ref, dgv_ref, dw_ref, db_ref):
        i = pl.program_id(0)
        first = i == 0
        last = i == nt - 1

        @pl.when(first)
        def _():
            dw_ref[...] = jnp.zeros(dw_ref.shape, F32)
            db_ref[...] = jnp.zeros(db_ref.shape, F32)

        def chunk(ci, carry):
            r0 = pl.multiple_of(ci * rc, rc)
            above = pl.multiple_of(jnp.maximum(r0 - h, 0), h)
            below = pl.multiple_of(jnp.minimum(r0 + rc, tm - h), h)
            at_end = ci == n_chunks - 1
            for lg in range(dff // lw):
                cs = slice(lg * lw, (lg + 1) * lw)
                top = jnp.where(ci == 0, jnp.where(first, 0.0, g_p[:, cs]), g_c[pl.ds(above, h), cs])
                bot = jnp.where(at_end, g_n[:, cs], g_c[pl.ds(below, h), cs])
                win = jnp.concatenate([top, g_c[pl.ds(r0, rc), cs], bot], axis=0)
                shifted = [win[h - sh:h - sh + rc + h] for sh in range(FFN_K)]
                gc = jnp.broadcast_to(b_ref[:, cs], (rc + h, lw))
                for sh in range(FFN_K):
                    gc = gc + shifted[sh] * w_ref[pl.ds(FFN_K - 1 - sh, 1), cs]
                gelu, dgelu = _gelu_parts(gc)
                dh_mid = dh_c[pl.ds(r0, rc), cs]
                hv_bot = jnp.where(at_end, jnp.where(last, 0.0, dh_n[:, cs] * v_n[:, cs]),
                                   dh_c[pl.ds(below, h), cs] * v_c[pl.ds(below, h), cs])
                dgc = jnp.concatenate([dh_mid * v_c[pl.ds(r0, rc), cs], hv_bot], axis=0) * dgelu
                dgate = jnp.zeros((rc, lw), F32)
                for sh in range(FFN_K):
                    dgate = dgate + dgc[sh:sh + rc] * w_ref[pl.ds(FFN_K - 1 - sh, 1), cs]
                dgv_ref[pl.ds(r0, rc), cs] = dgate.astype(BF16)
                dgv_ref[pl.ds(r0, rc), slice(dff + lg * lw, dff + (lg + 1) * lw)] = (dh_mid * gelu[0:rc]).astype(BF16)
                dgc_mid = dgc[0:rc]
                for sh in range(FFN_K):
                    dw_ref[pl.ds(FFN_K - 1 - sh, 1), cs] += _colsum(dgc_mid * shifted[sh][0:rc])
                db_ref[:, cs] += _colsum(dgc_mid)
            return carry

        lax.fori_loop(0, n_chunks, chunk, 0)

    return _call(
        body, name=name, grid=(nt,),
        in_specs=[_row(tm, dff), _next(tm, h, dff, s),
                  _prev(tm, h, dff, 0), _row(tm, dff, 0), _next(tm, h, dff, s, 0),
                  _row(tm, dff, 1), _next(tm, h, dff, s, 1),
                  _full((FFN_K, dff)), _full((1, dff))],
        out_specs=[_row(tm, 2 * dff), _full((FFN_K, dff)), _full((1, dff))],
        out_shape=[jax.ShapeDtypeStruct((s, 2 * dff), BF16), jax.ShapeDtypeStruct((FFN_K, dff), F32),
                   jax.ShapeDtypeStruct((1, dff), F32)],
        compiler_params=_cp(),
    )(dhid, dhid, gv, gv, gv, gv, gv, dw_w, dw_b)


def _bias_line(rel_bias):
    nh = rel_bias.shape[0]
    line = jnp.concatenate(
        [jnp.zeros((nh, 1), rel_bias.dtype), jnp.broadcast_to(rel_bias[:, 2 * MAX_REL:], (nh, SHEAR_SAT)),
         jnp.flip(rel_bias[:, 1:2 * MAX_REL], axis=1)], axis=1)
    return line[:, None, :]


def bias_tile(name, line):
    nh = line.shape[0]

    def body(l_ref, o_ref):
        x = jnp.broadcast_to(l_ref[...], (Q_TILE, SHEAR_W))
        z = pltpu.roll(x, SHEAR_W - Q_TILE, 1, stride=1, stride_axis=0)
        qc = lax.broadcasted_iota(jnp.int32, (Q_TILE, K_WIN), 0) // CHUNK
        kc = lax.broadcasted_iota(jnp.int32, (Q_TILE, K_WIN), 1) // CHUNK
        o_ref[...] = jnp.where((kc >= qc) & (kc <= qc + LEFT_CHUNKS), z[:, :K_WIN], NEG_INF)

    return _call(
        body, name=name, grid=(nh,), in_specs=[pl.BlockSpec((None, 1, SHEAR_W), lambda hh: (hh, 0, 0))],
        out_specs=pl.BlockSpec((None, Q_TILE, K_WIN), lambda hh: (hh, 0, 0)),
        out_shape=jax.ShapeDtypeStruct((nh, Q_TILE, K_WIN), F32), compiler_params=_cp(),
    )(line)


def _stack_heads(x2, scale=None):
    if scale is not None:
        x2 = x2 * jnp.asarray(scale, x2.dtype)
    lane = lax.broadcasted_iota(jnp.int32, x2.shape, 1)
    zero = jnp.zeros_like(x2)
    return jnp.concatenate([jnp.where(lane < HEAD_DIM, x2, zero), jnp.where(lane < HEAD_DIM, zero, x2)], axis=0)


def _unstack_heads(x_st):
    lane = lax.broadcasted_iota(jnp.int32, (Q_TILE, LANES), 1)
    return jnp.where(lane < HEAD_DIM, x_st[:Q_TILE], x_st[Q_TILE:])


def _attn_probs(q_st, k3, bias_st, t):
    sc = _dot(q_st, k3, "nt") + bias_st
    col = lax.broadcasted_iota(jnp.int32, sc.shape, 1)
    sc = jnp.where(col >= PAD_ROWS - t * Q_TILE, sc, NEG_INF)
    m = jnp.max(sc, axis=-1, keepdims=True)
    p = jnp.exp(sc - m)
    return p * (1.0 / jnp.sum(p, axis=-1, keepdims=True))


def _attn_specs(d_model, pairs):
    nq = PAD_ROWS // Q_TILE
    width = pairs * LANES
    groups = d_model // width
    specs = [pl.BlockSpec((Q_TILE, width), lambda g, t: (t + nq, g))]
    for which in (1, 2):
        for j in range(K_WIN // Q_TILE):
            specs.append(pl.BlockSpec((Q_TILE, width), lambda g, t, j=j, which=which: (t + j, which * groups + g)))
    specs.append(pl.BlockSpec((2 * pairs, Q_TILE, K_WIN), lambda g, t: (g, 0, 0)))
    return specs


def attn_fwd(name, qkvp, bias):
    s = qkvp.shape[0] - PAD_ROWS
    d_model = qkvp.shape[1] // 3
    nw = K_WIN // Q_TILE

    def body(q_ref, *refs):
        k_refs, v_refs, b_ref, o_ref = refs[:nw], refs[nw:2 * nw], refs[2 * nw], refs[2 * nw + 1]
        t = pl.program_id(1)
        for j in range(ATTN_PAIRS_FWD):
            ls = slice(j * LANES, (j + 1) * LANES)
            k3 = jnp.concatenate([r[:, ls] for r in k_refs], axis=0)
            v3 = jnp.concatenate([r[:, ls] for r in v_refs], axis=0)
            bias_st = b_ref[2 * j:2 * j + 2].reshape(2 * Q_TILE, K_WIN)
            p = _attn_probs(_stack_heads(q_ref[:, ls], ATTN_SCALE), k3, bias_st, t)
            o_ref[:, ls] = _unstack_heads(_dot(p, v3, "nn")).astype(BF16)

    width = ATTN_PAIRS_FWD * LANES
    return _call(
        body, name=name, grid=(d_model // width, s // Q_TILE),
        in_specs=_attn_specs(d_model, ATTN_PAIRS_FWD), out_specs=pl.BlockSpec((Q_TILE, width), lambda g, t: (t, g)),
        out_shape=jax.ShapeDtypeStruct((s, d_model), BF16), compiler_params=_cp(),
    )(qkvp, *([qkvp] * (2 * nw)), bias)


def attn_bwd(name, qkvp, bias, do):
    s = qkvp.shape[0] - PAD_ROWS
    d_model = qkvp.shape[1] // 3
    nw = K_WIN // Q_TILE
    nt = s // Q_TILE

    def body(q_ref, *refs):
        k_refs, v_refs = refs[:nw], refs[nw:2 * nw]
        b_ref, do_ref, dq_ref, dk_ref, dv_ref, ds_ref, dk_acc, dv_acc = refs[2 * nw:]
        t = pl.program_id(1)
        first = t == 0

        @pl.when(first)
        def _():
            dk_acc[...] = jnp.zeros(dk_acc.shape, F32)
            dv_acc[...] = jnp.zeros(dv_acc.shape, F32)
            ds_ref[...] = jnp.zeros(ds_ref.shape, F32)

        start = pl.multiple_of(t * Q_TILE, Q_TILE)
        for j in range(ATTN_PAIRS):
            ls = slice(j * LANES, (j + 1) * LANES)
            q_st = _stack_heads(q_ref[:, ls], ATTN_SCALE)
            do_st = _stack_heads(do_ref[:, ls])
            k3 = jnp.concatenate([r[:, ls] for r in k_refs], axis=0)
            v3 = jnp.concatenate([r[:, ls] for r in v_refs], axis=0)
            p = _attn_probs(q_st, k3, b_ref[2 * j:2 * j + 2].reshape(2 * Q_TILE, K_WIN), t)
            dp = _dot(do_st, v3, "nt")
            ds = p * (dp - jnp.sum(p * dp, axis=-1, keepdims=True))
            ds_ref[2 * j:2 * j + 2] += ds.reshape(2, Q_TILE, K_WIN)
            dsb = ds.astype(BF16)
            dq_ref[:, ls] = (_unstack_heads(_dot(dsb, k3, "nn")) * ATTN_SCALE).astype(BF16)
            dk_acc[pl.ds(start, K_WIN), ls] += _dot(dsb, q_st, "tn")
            dv_acc[pl.ds(start, K_WIN), ls] += _dot(p, do_st, "tn")

        @pl.when(t == nt - 1)
        def _():
            dk_ref[...] = dk_acc[pl.ds(PAD_ROWS, s), :].astype(BF16)
            dv_ref[...] = dv_acc[pl.ds(PAD_ROWS, s), :].astype(BF16)

    specs = _attn_specs(d_model, ATTN_PAIRS) + [pl.BlockSpec((Q_TILE, ATTN_LANES), lambda g, t: (t, g))]
    col_spec = pl.BlockSpec((s, ATTN_LANES), lambda g, t: (0, g))
    return _call(
        body, name=name, grid=(d_model // ATTN_LANES, nt), in_specs=specs,
        out_specs=[pl.BlockSpec((Q_TILE, ATTN_LANES), lambda g, t: (t, g)), col_spec, col_spec,
                   pl.BlockSpec((2 * ATTN_PAIRS, Q_TILE, K_WIN), lambda g, t: (g, 0, 0))],
        out_shape=[jax.ShapeDtypeStruct((s, d_model), BF16)] * 3
        + [jax.ShapeDtypeStruct((N_HEADS, Q_TILE, K_WIN), F32)],
        scratch_shapes=[pltpu.VMEM((PAD_ROWS + s, ATTN_LANES), F32), pltpu.VMEM((PAD_ROWS + s, ATTN_LANES), F32)],
        compiler_params=_cp(),
    )(qkvp, *([qkvp] * (2 * nw)), bias, do)


def bias_grad_reduce(name, ds_sum):
    nh = ds_sum.shape[0]
    width = SHEAR_W + Q_TILE
    first_k = Q_TILE - 1

    def body(x_ref, col_ref, sat_ref):
        x = x_ref[...]
        hi = x.astype(BF16)
        lo = (x - hi.astype(F32)).astype(BF16)
        r = lax.broadcasted_iota(jnp.int32, (Q_TILE, Q_TILE), 0)
        c = lax.broadcasted_iota(jnp.int32, (Q_TILE, Q_TILE), 1)
        exchange = jnp.where(r + c == Q_TILE - 1, 1.0, 0.0).astype(BF16)
        x_rev = _dot(exchange, hi, "nn") + _dot(exchange, lo, "nn")
        zeros = jnp.zeros((Q_TILE, Q_TILE), F32)
        y = pltpu.roll(jnp.concatenate([zeros, x_rev, zeros], axis=1), 0, 1, stride=1, stride_axis=0)
        cols = _colsum(y)
        col_ref[...] = cols
        k = lax.broadcasted_iota(jnp.int32, cols.shape, 1) - first_k
        tot = jnp.sum(jnp.where((k >= 1) & (k <= SHEAR_SAT), cols, 0.0), axis=-1, keepdims=True)
        sat_ref[...] = jnp.broadcast_to(tot, sat_ref.shape)

    return _call(
        body, name=name, grid=(nh,),
        in_specs=[pl.BlockSpec((None, Q_TILE, K_WIN), lambda hh: (hh, 0, 0))],
        out_specs=[pl.BlockSpec((None, 1, width), lambda hh: (hh, 0, 0)),
                   pl.BlockSpec((None, 1, LANES), lambda hh: (hh, 0, 0))],
        out_shape=[jax.ShapeDtypeStruct((nh, 1, width), F32), jax.ShapeDtypeStruct((nh, 1, LANES), F32)],
        compiler_params=_cp(),
    )(ds_sum)


def _ew_rows(r, most=512, cols=None):
    if cols is not None and r * cols * 4 <= SMALL_BLOCK_BYTES:
        return r
    for cand in range(min(most, r) // 16 * 16, 0, -16):
        if r % cand == 0:
            return cand
    return r


def cast_into_gathered(name, w, layer, s_idx, n_blocks=N_SHARD, dtype=BF16, token=None):
    r, c = w.shape[-2:]
    tr = _ew_rows(r, cols=c)

    def body(s_ref, w_ref, *rest):
        rest[-1][...] = w_ref[...].astype(dtype)

    extra = [] if token is None else [token]
    grid_spec = pltpu.PrefetchScalarGridSpec(
        num_scalar_prefetch=1, grid=(r // tr,),
        in_specs=[pl.BlockSpec((None, tr, c), lambda i, s_ref: (layer, i, 0))] + [ANY_SPEC] * len(extra),
        out_specs=pl.BlockSpec((None, tr, c), lambda i, s_ref: (s_ref[0], i, 0)))
    return _call(
        body, name=name, grid_spec=grid_spec, out_shape=jax.ShapeDtypeStruct((n_blocks, r, c), dtype),
        compiler_params=_cp(),
    )(s_idx, w, *extra)


def adamw(name, w, grads, m, v, token=None):
    nl, r, c = w.shape
    tr = _ew_rows(r, 256, cols=c)

    def body(*refs):
        w_ref, m_ref, v_ref = refs[0], refs[1], refs[2]
        g_refs = refs[3:3 + nl]
        d_ref, nm_ref, nv_ref = refs[-3:]
        layer = pl.program_id(0)
        g = g_refs[0][...]
        for j in range(1, nl):
            g = jnp.where(layer == j, g_refs[j][...], g)
        d_ref[...], nm_ref[...], nv_ref[...] = _adamw_update(w_ref[...], g, m_ref[...], v_ref[...])

    p_spec = pl.BlockSpec((None, tr, c), lambda l, i: (l, i, 0))
    g_spec = pl.BlockSpec((tr, c), lambda l, i: (i, 0))
    extra = [] if token is None else [token]
    extra_specs = [] if token is None else [ANY_SPEC]
    return _call(
        body, name=name, grid=(nl, r // tr), in_specs=[p_spec] * 3 + [g_spec] * nl + extra_specs,
        out_specs=[p_spec] * 3, out_shape=[jax.ShapeDtypeStruct((nl, r, c), F32)] * 3, compiler_params=_cp(),
    )(w, m, v, *grads, *extra)


def _adamw_update(w, g, m, v):
    nm = ADAM_B1 * m + (1.0 - ADAM_B1) * g
    nv = ADAM_B2 * v + (1.0 - ADAM_B2) * (g * g)
    delta = -ADAM_LR * ((nm / ADAM_BC1) / (jnp.sqrt(nv / ADAM_BC2) + ADAM_EPS) + ADAM_WD * w)
    return delta, nm, nv


def adamw_many(name, ws, gs, ms, vs, token):
    n = len(ws)

    def body(*refs):
        ins, outs = refs[:4 * n], refs[4 * n + 1:]
        for i in range(n):
            delta, nm, nv = _adamw_update(ins[i][...], ins[n + i][...], ins[2 * n + i][...], ins[3 * n + i][...])
            outs[3 * i][...] = delta
            outs[3 * i + 1][...] = nm
            outs[3 * i + 2][...] = nv

    vmem = pl.BlockSpec(memory_space=pltpu.VMEM)
    shapes = [jax.ShapeDtypeStruct(w.shape, F32) for w in ws for _ in range(3)]
    outs = _call(
        body, name=name, in_specs=[vmem] * (4 * n) + [ANY_SPEC], out_specs=[vmem] * (3 * n), out_shape=shapes,
        compiler_params=_cp(),
    )(*ws, *gs, *ms, *vs, token)
    return [tuple(outs[3 * i:3 * i + 3]) for i in range(n)]


def sum_blocks(name, gathered, n_blocks):
    r = gathered.shape[0] // n_blocks
    c = gathered.shape[1]
    tr = r if r <= SUM_BLOCK_ROWS else _ew_rows(r)
    nt = r // tr

    def body(*refs):
        acc = refs[0][...]
        for j in range(1, n_blocks):
            acc = acc + refs[j][...]
        refs[-1][...] = acc

    specs = [pl.BlockSpec((tr, c), lambda i, j=j: (j * nt + i, 0)) for j in range(n_blocks)]
    return _call(
        body, name=name, grid=(nt,), in_specs=specs, out_specs=pl.BlockSpec((tr, c), lambda i: (i, 0)),
        out_shape=jax.ShapeDtypeStruct((r, c), F32), compiler_params=_cp(),
    )(*([gathered] * n_blocks))


def _place():
    return lax.axis_index("x"), lax.axis_index("y"), lax.axis_index("c")


def _other_chips(x, y):
    return [(1 - x, y), (x, 1 - y), (1 - x, 1 - y)]


HBM_SPEC = pl.BlockSpec(memory_space=pltpu.HBM)
SEM_SPEC = pl.BlockSpec(memory_space=pltpu.SEMAPHORE)
ANY_SPEC = pl.BlockSpec(memory_space=pl.ANY)
EFFECT = pltpu.SideEffectType.DATAFLOW_SIDE_EFFECTING


def copies_start(name, bufs, plan, n_copies):
    n = len(bufs)

    def body(*refs):
        send, recv = refs[n], refs[n + 1]
        token = refs[2 * n + 2]
        for k, (src, dst, peer, _) in enumerate(plan(refs[:n])):
            pltpu.make_async_remote_copy(
                src_ref=src, dst_ref=dst, send_sem=send.at[k], recv_sem=recv.at[k],
                device_id=peer, device_id_type=MESH).start()
        token[...] = jnp.zeros(token.shape, F32)

    outs = pl.pallas_call(
        body, name=name,
        out_shape=(pltpu.SemaphoreType.DMA((n_copies,)), pltpu.SemaphoreType.DMA((n_copies,)),
                   *[pltpu.HBM(b.shape, b.dtype) for b in bufs], jax.ShapeDtypeStruct((8, LANES), F32)),
        in_specs=[HBM_SPEC] * n,
        out_specs=(SEM_SPEC, SEM_SPEC, *([HBM_SPEC] * n), pl.BlockSpec(memory_space=pltpu.VMEM)),
        input_output_aliases={a: a + 2 for a in range(n)},
        compiler_params=pltpu.CompilerParams(has_side_effects=EFFECT),
    )(*[_in_hbm(b) for b in bufs])
    return outs[0], outs[1], list(outs[2:2 + n]), outs[2 + n]


def copies_wait(name, bufs, send, recv, plan, sem_base, after):
    n = len(bufs)

    def body(*refs):
        send_ref, recv_ref = refs[n], refs[n + 1]
        for k, (src, _, peer, land) in enumerate(plan(refs[:n])):
            cp = pltpu.make_async_remote_copy(
                src_ref=src, dst_ref=land, send_sem=send_ref.at[sem_base + k], recv_sem=recv_ref.at[sem_base + k],
                device_id=peer, device_id_type=MESH)
            cp.wait_send()
            cp.wait_recv()

    outs = pl.pallas_call(
        body, name=name,
        out_shape=tuple(pltpu.HBM(b.shape, b.dtype) for b in bufs),
        in_specs=[HBM_SPEC] * n + [SEM_SPEC, SEM_SPEC, ANY_SPEC], out_specs=tuple([HBM_SPEC] * n),
        input_output_aliases={a: a for a in range(n)},
        compiler_params=pltpu.CompilerParams(has_side_effects=EFFECT),
    )(*bufs, send, recv, after)
    return list(outs)


def gather_plan(refs):
    x, y, c = _place()
    me = 2 * x + y
    return [(buf.at[me], buf.at[me], (cx, cy, c), buf.at[2 * cx + cy])
            for buf in refs for cx, cy in _other_chips(x, y)]


def all_plan(refs):
    x, y, c = _place()
    me = 4 * x + 2 * y + c
    out = []
    for buf in refs:
        for flip in range(1, 8):
            px = 1 - x if flip & 4 else x
            py = 1 - y if flip & 2 else y
            pc = 1 - c if flip & 1 else c
            out.append((buf.at[me], buf.at[me], (px, py, pc), buf.at[4 * px + 2 * py + pc]))
    return out


def swap_plan(refs):
    x, y, c = _place()
    n = len(refs) // 2
    out = []
    for g, land in zip(refs[:n], refs[n:]):
        hr = g.shape[1] // 2
        out.append((g.at[:, pl.ds((1 - c) * hr, hr)], land, (x, y, 1 - c), land))
    return out


def owners_plan(refs):
    x, y, c = _place()
    n = len(refs) // 2
    return [(src.at[2 * cx + cy], land.at[j], (cx, cy, c), land.at[j])
            for src, land in zip(refs[:n], refs[n:]) for j, (cx, cy) in enumerate(_other_chips(x, y))]


def join_plan(refs):
    x, y, c = _place()
    out = []
    for buf in refs:
        hr = buf.shape[0] // 2
        mine = buf.at[pl.ds(c * hr, hr)]
        out.append((mine, mine, (x, y, 1 - c), buf.at[pl.ds((1 - c) * hr, hr)]))
    return out


def add_halves(name, grad, landed, sc_idx):
    _, r, c = grad.shape
    hr = r // 2
    tr = _ew_rows(hr)
    nt = hr // tr

    def body(sc_ref, g_ref, l_ref, own_ref, wire_ref):
        tot = g_ref[...] + l_ref[...]
        wire_ref[...] = tot.astype(BF16)

        @pl.when(pl.program_id(1) == sc_ref[0])
        def _():
            own_ref[...] = tot

    grid_spec = pltpu.PrefetchScalarGridSpec(
        num_scalar_prefetch=1, grid=(nt, N_SHARD),
        in_specs=[pl.BlockSpec((None, tr, c), lambda i, sh, sc_ref: (sh, sc_ref[1] * nt + i, 0)),
                  pl.BlockSpec((None, tr, c), lambda i, sh, sc_ref: (sh, i, 0))],
        out_specs=[pl.BlockSpec((tr, c), lambda i, sh, sc_ref: (i, 0)),
                   pl.BlockSpec((None, tr, c), lambda i, sh, sc_ref: (sh, i, 0))])
    return _call(
        body, name=name, grid_spec=grid_spec,
        out_shape=[jax.ShapeDtypeStruct((hr, c), F32), jax.ShapeDtypeStruct((N_SHARD, hr, c), BF16)],
        compiler_params=_cp(),
    )(sc_idx, grad, landed)


def add_owned(name, own, landed, sc_idx):
    hr, c = own.shape
    tr = _ew_rows(hr)
    nt = hr // tr

    def body(sc_ref, o_ref, l0, l1, l2, out_ref):
        out_ref[...] = ((o_ref[...] + l0[...].astype(F32)) + l1[...].astype(F32)) + l2[...].astype(F32)

    grid_spec = pltpu.PrefetchScalarGridSpec(
        num_scalar_prefetch=1, grid=(nt,),
        in_specs=[pl.BlockSpec((tr, c), lambda i, sc_ref: (i, 0))]
        + [pl.BlockSpec((None, tr, c), lambda i, sc_ref, j=j: (j, i, 0)) for j in range(3)],
        out_specs=pl.BlockSpec((tr, c), lambda i, sc_ref: (sc_ref[1] * nt + i, 0)))
    return _call(
        body, name=name, grid_spec=grid_spec, out_shape=jax.ShapeDtypeStruct((2 * hr, c), F32),
        compiler_params=_cp(),
    )(sc_idx, own, landed, landed, landed)


PACK_QUANTUM = 8 * LANES


def _pack(arrays):
    pieces = []
    for a in arrays:
        flat = a.reshape(-1)
        padded = -(-flat.shape[0] // PACK_QUANTUM) * PACK_QUANTUM
        pieces.append(jnp.pad(flat, (0, padded - flat.shape[0])).reshape(-1, LANES))
    return jnp.concatenate(pieces, axis=0)


def _unpack(packed, shapes):
    out = []
    row = 0
    for shp in shapes:
        size = math.prod(shp)
        rows = -(-size // PACK_QUANTUM) * 8
        out.append(packed[row:row + rows].reshape(-1)[:size].reshape(shp))
        row += rows
    return out


def kernel(x, p, mix_w_in, pool_w, pool_scale, conv_dw_w, conv_dw_b, conv_ln_g, conv_ln_b, mix_w_out, attn_w_qkv, attn_rel_bias, attn_w_o, ln_mix_g, ln_mix_b, ffn_w_up, ffn_dw_w, ffn_dw_b, ffn_w_down, ple_w_proj, ple_w_gate, ple_b_gate, ln_ffn_g, ln_ffn_b, loss_target, m_mix_w_in, m_pool_w, m_pool_scale, m_conv_dw_w, m_conv_dw_b, m_conv_ln_g, m_conv_ln_b, m_mix_w_out, m_attn_w_qkv, m_attn_rel_bias, m_attn_w_o, m_ln_mix_g, m_ln_mix_b, m_ffn_w_up, m_ffn_dw_w, m_ffn_dw_b, m_ffn_w_down, m_ple_w_proj, m_ple_w_gate, m_ple_b_gate, m_ln_ffn_g, m_ln_ffn_b, v_mix_w_in, v_pool_w, v_pool_scale, v_conv_dw_w, v_conv_dw_b, v_conv_ln_g, v_conv_ln_b, v_mix_w_out, v_attn_w_qkv, v_attn_rel_bias, v_attn_w_o, v_ln_mix_g, v_ln_mix_b, v_ffn_w_up, v_ffn_dw_w, v_ffn_dw_b, v_ffn_w_down, v_ple_w_proj, v_ple_w_gate, v_ple_b_gate, v_ln_ffn_g, v_ln_ffn_b):
    xi, yi, ci = _place()
    shard_idx = (2 * xi + yi).astype(jnp.int32)
    s_arr = shard_idx.reshape(1)
    c_arr = ci.astype(jnp.int32).reshape(1)
    sc_arr = jnp.concatenate([s_arr, c_arr])

    x0 = x[0]
    target = loss_target[0]
    p_rows = p.reshape(p.shape[0] * p.shape[2], p.shape[3])
    seq = x0.shape[0]

    big = [
        ("mix_w_in", mix_w_in, m_mix_w_in, v_mix_w_in, True),
        ("mix_w_out", mix_w_out, m_mix_w_out, v_mix_w_out, False),
        ("attn_w_qkv", attn_w_qkv, m_attn_w_qkv, v_attn_w_qkv, True),
        ("attn_w_o", attn_w_o, m_attn_w_o, v_attn_w_o, False),
        ("ffn_w_up", ffn_w_up, m_ffn_w_up, v_ffn_w_up, True),
        ("ffn_w_down", ffn_w_down, m_ffn_w_down, v_ffn_w_down, False),
        ("ple_w_proj", ple_w_proj, m_ple_w_proj, v_ple_w_proj, True),
        ("ple_w_gate", ple_w_gate, m_ple_w_gate, v_ple_w_gate, False),
    ]
    params = {nm: w for nm, w, _, _, _ in big}
    col_sharded = {nm: cs for nm, _, _, _, cs in big}
    keys = [("mix_w_in", 0), ("mix_w_out", 0), ("ffn_w_up", 0), ("ffn_w_down", 0), ("ple_w_gate", 0),
            ("ple_w_proj", 0), ("attn_w_qkv", 0), ("attn_w_o", 0), ("ffn_w_up", 1), ("ffn_w_down", 1),
            ("ple_w_gate", 1), ("ple_w_proj", 1)]
    dw_shapes = [conv_dw_w.shape, ffn_dw_w.shape]
    dw_block = cast_into_gathered("place_dw", _pack([conv_dw_w, ffn_dw_w])[None], 0, s_arr, dtype=F32)
    n_first = 2
    started = {}
    gather_token = None
    for tag, group in (("first", keys[:n_first]), ("rest", keys[n_first:])):
        shards = [cast_into_gathered(f"cast_{nm}_{layer}", params[nm], layer, s_arr, token=gather_token)
                  for nm, layer in group]
        if tag == "first":
            shards.append(dw_block)
        send, recv, bufs, gather_token = copies_start(f"gather_start_{tag}", shards, gather_plan, 3 * len(shards))
        for a, key in enumerate(group):
            started[key] = (send, recv, bufs[a], 3 * a)
        if tag == "first":
            dw_started = (send, recv, bufs[-1], 3 * len(group))
    arrived_w = {}

    def weight(nm, layer, after=None):
        key = (nm, layer)
        if key not in arrived_w:
            send, recv, buf, base = started[key]
            arrived_w[key] = copies_wait(f"gather_wait_{nm}_{layer}", [buf], send, recv, gather_plan, base, after)[0]
        g = arrived_w[key]
        if col_sharded[nm]:
            return g
        return g.reshape(g.shape[0] * g.shape[1], g.shape[2])

    def tie(a, token):
        return a + token[0:1, 0:1].astype(a.dtype)

    class Reducer:
        def __init__(self, tag, group):
            self.tag, self.group, self.stage = tag, group, 0
            self.n = len(group)
            self.result = None

        def advance(self, after):
            tag, n = self.tag, self.n
            if self.stage == 0:
                grads = []
                for key in self.group:
                    g = big_grads[key]
                    grads.append(g if g.ndim == 3 else g.reshape(N_SHARD, g.shape[0] // N_SHARD, g.shape[1]))
                lands = [lax.empty((N_SHARD, g.shape[1] // 2, g.shape[2]), F32) for g in grads]
                self.sems = copies_start(f"swap_start_{tag}", grads + lands, swap_plan, n)
            elif self.stage == 1:
                send, recv, bufs, _ = self.sems
                outs = copies_wait(f"swap_wait_{tag}", bufs, send, recv, swap_plan, 0, after)
                self.own, wire = [], []
                for key, g, ld in zip(self.group, outs[:n], outs[n:]):
                    o, ob = add_halves(f"add_halves_{key[0]}_{key[1]}", g, ld, sc_arr)
                    self.own.append(o)
                    wire.append(ob)
                lands = [lax.empty((3,) + w.shape[1:], BF16) for w in wire]
                self.sems = copies_start(f"owners_start_{tag}", wire + lands, owners_plan, 3 * n)
            elif self.stage == 2:
                send, recv, bufs, _ = self.sems
                outs = copies_wait(f"owners_wait_{tag}", bufs, send, recv, owners_plan, 0, after)
                finals = [add_owned(f"add_owned_{key[0]}_{key[1]}", o, ar, sc_arr)
                          for key, o, ar in zip(self.group, self.own, outs[n:])]
                self.sems = copies_start(f"join_start_{tag}", finals, join_plan, n)
            elif self.stage == 3:
                send, recv, bufs, _ = self.sems
                outs = copies_wait(f"join_wait_{tag}", bufs, send, recv, join_plan, 0, after)
                self.result = dict(zip(self.group, outs))
                self.sems = None
            self.stage += 1
            return None if self.sems is None else self.sems[3]

    dw_cache = []

    def conv_weights(after):
        if not dw_cache:
            send, recv, buf, base = dw_started
            dw_all = copies_wait("gather_wait_dw", [buf], send, recv, gather_plan, base, after)[0]
            dw_parts = [_unpack(dw_all[k], dw_shapes) for k in range(N_SHARD)]
            dw_cache.append(jnp.concatenate([pc[0] for pc in dw_parts], axis=2)[0])
            dw_cache.append(jnp.concatenate([pc[1] for pc in dw_parts], axis=2))
        return dw_cache

    big_grads = {}
    small_grads = {}

    saved = []
    h_in = x0
    h_in_b = x0
    for layer in range(N_LAYERS):
        sv = {"x_in": h_in_b}
        if layer % 2 == 0:
            u = mm_cols_fwd("mix_in", h_in_b, weight("mix_w_in", 0, gather_token), F32)
            conv_w_full, ffn_dw_full = conv_weights(u)
            cat, d_sv, e_sv, glu_sv, hh_sv, rs_sv = mixer_fwd(
                "mixer_fwd", u, pool_w[0], pool_scale, conv_w_full, conv_dw_b, conv_ln_g, conv_ln_b)
            mix = mm_rows_fwd("mix_out", cat, weight("mix_w_out", 0, cat))
            sv.update(u=u, cat=cat, d=d_sv, e=e_sv, glu=glu_sv, hh=hh_sv, rs=rs_sv)
        else:
            qkvp = mm_cols_fwd("attn_qkv", h_in_b, weight("attn_w_qkv", 0, h_in_b), BF16,
                               pad_blocks=PAD_ROWS // _row_tile(seq))
            bias = bias_tile("bias_tile", _bias_line(attn_rel_bias[0]))
            att = attn_fwd("attn_fwd", qkvp, bias)
            mix = mm_rows_fwd("attn_out", att, weight("attn_w_o", 0, att))
            sv.update(qkvp=qkvp, bias=bias, att=att)
        x1, x1_b, xh1, rs1 = ln_fwd(f"ln_mix_{layer}", h_in, mix, ln_mix_g[layer:layer + 1],
                                    ln_mix_b[layer:layer + 1])
        gv = mm_cols_fwd(f"ffn_up_{layer}", x1_b, weight("ffn_w_up", layer, x1_b), F32)
        hid = ffn_act_fwd(f"ffn_act_{layer}", gv, ffn_dw_full[layer], ffn_dw_b[layer:layer + 1])
        ffn = mm_rows_fwd(f"ffn_down_{layer}", hid, weight("ffn_w_down", layer, hid))
        pgl = mm_rows_fwd(f"ple_gate_{layer}", x1_b, weight("ple_w_gate", layer, ffn))
        pp = mm_cols_fwd(f"ple_proj_{layer}", p_rows, weight("ple_w_proj", layer, pgl), F32, part=(layer, N_LAYERS))
        bg = ple_b_gate[layer:layer + 1]
        x2, x2_b, xh2, rs2 = ln_fwd(f"ln_ffn_{layer}", x1, ffn, ln_ffn_g[layer:layer + 1], ln_ffn_b[layer:layer + 1],
                                    ple=(pgl, pp, bg), emit_y=layer < N_LAYERS - 1)
        sv.update(x1=x1_b, xh1=xh1, rs1=rs1, gv=gv, hid=hid, pgl=pgl, pp=pp, xh2=xh2, rs2=rs2)
        saved.append(sv)
        h_in, h_in_b = x2, x2_b

    reducers = []

    def open_group(tag, group):
        reducers.append(Reducer(tag, group))
        return reducers[-1].advance(None)

    def hook(after):
        token = None
        for red in reducers:
            if red.stage < 4:
                tk = red.advance(after)
                if tk is not None:
                    token = tk if token is None else token + tk
        return token

    def tied(a, token):
        return a if token is None else tie(a, token)

    parts = []
    token = None
    for layer in reversed(range(N_LAYERS)):
        sv = saved[layer]
        bg = ple_b_gate[layer:layer + 1]
        if layer == 0:
            token = open_group("layer1", [("attn_w_qkv", 0), ("attn_w_o", 0), ("ffn_w_up", 1), ("ffn_w_down", 1),
                                          ("ple_w_gate", 1), ("ple_w_proj", 1)])
        last = layer == N_LAYERS - 1
        res = ln_bwd(
            f"ln_ffn_bwd_{layer}", parts, sv["xh2"], sv["rs2"], tied(ln_ffn_g[layer:layer + 1], token),
            ple=(sv["pgl"], sv["pp"], bg), loss=(target, ln_ffn_b[layer:layer + 1]) if last else None)
        dz2, dg2, db2, dpp, dpgl, dbg = res[:6]
        if last:
            loss_part = res[6]
        small_grads[("ln_ffn_g", layer)] = dg2
        small_grads[("ln_ffn_b", layer)] = db2
        small_grads[("ple_b_gate", layer)] = dbg
        w_down = weight("ffn_w_down", layer)
        dhid = mm_rows_dx(f"ffn_down_dx_{layer}", dz2, w_down)
        big_grads[("ffn_w_down", layer)] = mm_rows_dw(f"ffn_down_dw_{layer}", sv["hid"], dz2)
        token = hook(big_grads[("ffn_w_down", layer)])
        dgv, ddw, ddb = ffn_act_bwd(f"ffn_act_bwd_{layer}", dhid, sv["gv"], ffn_dw_full[layer],
                                    tied(ffn_dw_b[layer:layer + 1], token))
        small_grads[("ffn_dw_w", layer)] = ddw
        small_grads[("ffn_dw_b", layer)] = ddb
        big_grads[("ffn_w_up", layer)] = mm_cols_dw(f"ffn_up_dw_{layer}", sv["x1"], dgv)
        t_up = mm_cols_dx(f"ffn_up_dx_{layer}", dgv, weight("ffn_w_up", layer))
        token = hook(t_up)
        big_grads[("ple_w_gate", layer)] = mm_rows_dw(f"ple_gate_dw_{layer}", sv["x1"], dpgl)
        t_gate = mm_rows_dx(f"ple_gate_dx_{layer}", dpgl, weight("ple_w_gate", layer))
        big_grads[("ple_w_proj", layer)] = mm_cols_dw(f"ple_proj_dw_{layer}", p_rows, dpp, part=(layer, N_LAYERS))
        token2 = hook(big_grads[("ple_w_proj", layer)])
        if token2 is not None:
            token = token2 if token is None else token + token2
        if layer == 0:
            token3 = open_group("layer0_ffn", [("ffn_w_up", 0), ("ffn_w_down", 0), ("ple_w_gate", 0), ("ple_w_proj", 0)])
            token = token3 if token is None else token + token3
        dz1, dg1, db1 = ln_bwd(
            f"ln_mix_bwd_{layer}", [(ALPHA, dz2), (1.0, t_up), (1.0, t_gate)], sv["xh1"], sv["rs1"],
            tied(ln_mix_g[layer:layer + 1], token))
        small_grads[("ln_mix_g", layer)] = dg1
        small_grads[("ln_mix_b", layer)] = db1
        if layer % 2 == 0:
            dcat = mm_rows_dx("mix_out_dx", dz1, weight("mix_w_out", 0))
            big_grads[("mix_w_out", 0)] = mm_rows_dw("mix_out_dw", sv["cat"], dz1)
            token = hook(big_grads[("mix_w_out", 0)])
            du, dpw, dps, dcw, dcb, dcg, dcbt = mixer_bwd(
                "mixer_bwd", dcat, sv["u"], sv["d"], sv["e"], sv["glu"], sv["hh"], sv["rs"],
                pool_w[0], pool_scale, conv_w_full, tied(conv_ln_g, token), conv_ln_b)
            small_grads[("pool_w", 0)] = dpw
            small_grads[("pool_scale", 0)] = dps
            small_grads[("conv_dw_w", 0)] = dcw
            small_grads[("conv_dw_b", 0)] = dcb
            small_grads[("conv_ln_g", 0)] = dcg
            small_grads[("conv_ln_b", 0)] = dcbt
            big_grads[("mix_w_in", 0)] = mm_cols_dw("mix_in_dw", sv["x_in"], du)
            hook(big_grads[("mix_w_in", 0)])
            open_group("layer0_mix", [("mix_w_in", 0), ("mix_w_out", 0)])
            dx_in = mm_cols_dx("mix_in_dx", du, weight("mix_w_in", 0), addend=(ALPHA, dz1))
            token = hook(dx_in)
        else:
            do = mm_rows_dx("attn_out_dx", dz1, weight("attn_w_o", 0), out_dtype=BF16)
            big_grads[("attn_w_o", 0)] = mm_rows_dw("attn_out_dw", sv["att"], dz1)
            dq, dk, dv, ds_sum = attn_bwd("attn_bwd", sv["qkvp"], sv["bias"], do)
            cols, sat = bias_grad_reduce("bias_grad", ds_sum)
            d_rel = jnp.concatenate(
                [jnp.zeros((N_HEADS, 1), F32),
                 jnp.flip(cols[:, 0, Q_TILE + SHEAR_SAT:Q_TILE - 1 + SHEAR_W], axis=1),
                 sat[:, 0, 0:1]], axis=1)
            small_grads[("attn_rel_bias", 0)] = d_rel
            dqkv = jnp.concatenate([dq, dk, dv], axis=1)
            big_grads[("attn_w_qkv", 0)] = mm_cols_dw("attn_qkv_dw", sv["x_in"], dqkv)
            dx_in = mm_cols_dx("attn_qkv_dx", dqkv, weight("attn_w_qkv", 0), addend=(ALPHA, dz1))
        parts = [(1.0, dx_in)]
    grad_x = dx_in

    small = [
        ("pool_w", pool_w, m_pool_w, v_pool_w, None),
        ("pool_scale", pool_scale, m_pool_scale, v_pool_scale, None),
        ("conv_dw_w", conv_dw_w, m_conv_dw_w, v_conv_dw_w, 2),
        ("conv_dw_b", conv_dw_b, m_conv_dw_b, v_conv_dw_b, None),
        ("conv_ln_g", conv_ln_g, m_conv_ln_g, v_conv_ln_g, None),
        ("conv_ln_b", conv_ln_b, m_conv_ln_b, v_conv_ln_b, None),
        ("attn_rel_bias", attn_rel_bias, m_attn_rel_bias, v_attn_rel_bias, None),
        ("ln_mix_g", ln_mix_g, m_ln_mix_g, v_ln_mix_g, None),
        ("ln_mix_b", ln_mix_b, m_ln_mix_b, v_ln_mix_b, None),
        ("ffn_dw_w", ffn_dw_w, m_ffn_dw_w, v_ffn_dw_w, 2),
        ("ffn_dw_b", ffn_dw_b, m_ffn_dw_b, v_ffn_dw_b, None),
        ("ple_b_gate", ple_b_gate, m_ple_b_gate, v_ple_b_gate, None),
        ("ln_ffn_g", ln_ffn_g, m_ln_ffn_g, v_ln_ffn_g, None),
        ("ln_ffn_b", ln_ffn_b, m_ln_ffn_b, v_ln_ffn_b, None),
    ]
    full_grads = []
    for nm, w, _, _, shard_axis in small:
        full = list(w.shape)
        if shard_axis is not None:
            full[shard_axis] *= N_SHARD
        per_layer = [small_grads[(nm, layer)].reshape((1,) + tuple(full[1:])) for layer in range(w.shape[0])]
        full_grads.append(jnp.concatenate(per_layer, axis=0))
    packed = _pack(full_grads + [loss_part])
    dev_arr = (4 * xi + 2 * yi + ci).astype(jnp.int32).reshape(1)
    sg_block = cast_into_gathered("place_small_grads", packed[None], 0, dev_arr, n_blocks=8, dtype=F32)
    sg_send, sg_recv, sg_bufs, sg_token = copies_start("small_grads_start", [sg_block], all_plan, 7)
    token = sg_token if token is None else token + sg_token

    shard_grads = {}
    for red in reducers:
        if red.stage == 4:
            shard_grads.update(red.result)
    big_out = {}

    def update_big(names, tok):
        for nm, w, m, v, _ in big:
            if nm in names:
                gl = [shard_grads[(nm, layer)] for layer in range(w.shape[0])]
                delta, new_m, new_v = adamw(f"adamw_{nm}", w, gl, m, v, token=tok)
                big_out[nm] = (jnp.stack(gl, axis=0), delta, new_m, new_v)

    last_group = ("mix_w_in", "mix_w_out")
    update_big([nm for nm, _, _, _, _ in big if nm not in last_group], token)
    token = hook(big_out["ffn_w_up"][1])

    gathered_sg = copies_wait("small_grads_wait", sg_bufs, sg_send, sg_recv, all_plan, 0, big_out["ffn_w_down"][1])[0]
    total = sum_blocks("sum_small", gathered_sg.reshape(8 * packed.shape[0], LANES), 8)
    unpacked = _unpack(total, [g.shape for g in full_grads] + [loss_part.shape])
    loss = unpacked[-1][0, 0]
    local_grads = []
    for (nm, w, _, _, shard_axis), g in zip(small, unpacked[:-1]):
        if shard_axis is not None:
            width = w.shape[shard_axis]
            g = lax.dynamic_slice_in_dim(g, shard_idx * width, width, axis=shard_axis)
        local_grads.append(g.reshape(w.shape))
    updated = adamw_many("adamw_small", [w for _, w, _, _, _ in small], local_grads,
                         [m for _, _, m, _, _ in small], [v for _, _, _, v, _ in small], token)
    hook(updated[0][0])
    for red in reducers:
        shard_grads.update(red.result)
    update_big(last_group, None)
    small_out = {}
    for (nm, _, _, _, _), g, (d_, m_, v_) in zip(small, local_grads, updated):
        small_out[nm] = (g, d_, m_, v_)

    order = ["mix_w_in", "pool_w", "pool_scale", "conv_dw_w", "conv_dw_b", "conv_ln_g", "conv_ln_b", "mix_w_out",
             "attn_w_qkv", "attn_rel_bias", "attn_w_o", "ln_mix_g", "ln_mix_b", "ffn_w_up", "ffn_dw_w", "ffn_dw_b",
             "ffn_w_down", "ple_w_proj", "ple_w_gate", "ple_b_gate", "ln_ffn_g", "ln_ffn_b"]
    res = {**big_out, **small_out}
    outs = [loss, grad_x[None]]
    for slot in range(4):
        outs += [res[nm][slot] for nm in order]
    return tuple(outs)
```

```python
import math

import jax
import jax.numpy as jnp
from jax import lax
from jax.experimental import pallas as pl
from jax.experimental.pallas import tpu as pltpu

F32 = jnp.float32
BF16 = jnp.bfloat16
MESH = pl.DeviceIdType.MESH

N_LAYERS = 2
ALPHA = (2 * N_LAYERS) ** 0.25
LN_EPS = 1e-5
NEG_INF = -1e30
CHUNK = 64
LEFT_CHUNKS = 8
PAD_ROWS = LEFT_CHUNKS * CHUNK
HEAD_DIM = 64
ATTN_SCALE = HEAD_DIM ** -0.5
N_HEADS = 16
MAX_REL = 256
POOL_WINDOWS = (2, 4, 8, 16)
POOL_GROUP = 128
CONV_K = 31
FFN_K = 3
CONV_HALO = 32
FFN_HALO = 8
FFN_TILE = 256
FFN_CHUNK_ROWS = 256
FFN_CHUNK_LANES = 128
Q_TILE = 256
K_WIN = Q_TILE + PAD_ROWS
LANES = 128
SUBLANES = 8
ATTN_PAIRS = 2
ATTN_PAIRS_FWD = 8
ATTN_LANES = ATTN_PAIRS * LANES
SHEAR_W = Q_TILE + K_WIN
SHEAR_SAT = SHEAR_W - 2 * MAX_REL
N_SHARD = 4

ADAM_LR = 0.001
ADAM_B1 = 0.9
ADAM_B2 = 0.999
ADAM_EPS = 1e-08
ADAM_WD = 0.01
ADAM_STEP = 10
ADAM_BC1 = 1.0 - ADAM_B1 ** ADAM_STEP
ADAM_BC2 = 1.0 - ADAM_B2 ** ADAM_STEP

DIMS = {
    "nn": (((1,), (0,)), ((), ())),
    "nt": (((1,), (1,)), ((), ())),
    "tn": (((0,), (0,)), ((), ())),
}


def _cp(vmem_mb=48, **kw):
    return pltpu.CompilerParams(vmem_limit_bytes=vmem_mb * 1024 * 1024, **kw)


def _in_hbm(a):
    return pltpu.with_memory_space_constraint(a, pltpu.HBM)


STAGING_LIMIT_BYTES = 1 << 20
SMALL_WEIGHT_BYTES = 1 << 22
SUM_BLOCK_ROWS = 2048
SMALL_BLOCK_BYTES = 1 << 19


def _call(body, **kw):
    call = pl.pallas_call(body, **kw)

    def run(*args):
        pinned = []
        for a in args:
            big = a.size * a.dtype.itemsize >= STAGING_LIMIT_BYTES
            pinned.append(_in_hbm(a) if big and not jnp.issubdtype(a.dtype, jnp.integer) else a)
        return call(*pinned)

    return run


def _dot(a, b, mode):
    return lax.dot_general(a.astype(BF16), b.astype(BF16), DIMS[mode], preferred_element_type=F32)


def _sig(x):
    return 1.0 / (1.0 + jnp.exp(-x))


def _row_tile(s):
    return min(512, s // 4)


def _mm_tile(s):
    return min(1024, s // 4)


def _mm(name, mode, a, b, in_specs, out_shape, out_spec, acc_shape, grid, nk, zero_first=False, vmem_mb=48,
        addend=None):
    out_f32 = out_shape.dtype == F32

    def body(a_ref, b_ref, *rest):
        k = pl.program_id(2)
        if addend is None:
            o_ref, scr = rest[0], rest[1:]
        else:
            add_ref, o_ref, scr = rest[0], rest[1], rest[2:]

        def compute():
            part = _dot(a_ref[...], b_ref[...], mode)
            if nk == 1:
                if addend is not None:
                    part = part + addend[0] * add_ref[...]
                o_ref[...] = part.astype(o_ref.dtype)
                return
            acc = o_ref if out_f32 else scr[0]

            @pl.when(k == 0)
            def _():
                acc[...] = part if addend is None else part + addend[0] * add_ref[...]

            @pl.when(k > 0)
            def _():
                acc[...] += part

            if not out_f32:
                @pl.when(k == nk - 1)
                def _():
                    o_ref[...] = acc[...].astype(o_ref.dtype)

        if zero_first:
            @pl.when(pl.program_id(1) == 0)
            def _():
                o_ref[...] = jnp.zeros(o_ref.shape, o_ref.dtype)

            pl.when(pl.program_id(1) > 0)(compute)
        else:
            compute()

    scratch = [] if (nk == 1 or out_f32) else [pltpu.VMEM(acc_shape, F32)]
    operands = [a, b] if addend is None else [a, b, addend[1]]
    specs = list(in_specs) if addend is None else list(in_specs) + [out_spec]
    return _call(
        body, name=name, grid=grid, in_specs=specs, out_specs=out_spec, out_shape=out_shape,
        scratch_shapes=scratch, compiler_params=_cp(vmem_mb),
    )(*operands)


def _is_small_weight(wc):
    return wc.size * 2 <= SMALL_WEIGHT_BYTES


def _all_shards(w_ref):
    return jnp.concatenate([w_ref[j] for j in range(N_SHARD)], axis=1)


def mm_cols_fwd(name, a, wc, out_dtype, pad_blocks=0, part=(0, 1)):
    s, k = a.shape
    s //= part[1]
    n4 = wc.shape[2]
    tm = _row_tile(s) if pad_blocks else _mm_tile(s)
    nt = s // tm
    first_block = part[0] * nt
    if _is_small_weight(wc) and not pad_blocks:
        def body(a_ref, w_ref, o_ref):
            o_ref[...] = _dot(a_ref[...], _all_shards(w_ref), "nn").astype(o_ref.dtype)

        return _call(
            body, name=name, grid=(nt,),
            in_specs=[pl.BlockSpec((tm, k), lambda i: (first_block + i, 0)), _full(wc.shape)],
            out_specs=pl.BlockSpec((tm, N_SHARD * n4), lambda i: (i, 0)),
            out_shape=jax.ShapeDtypeStruct((s, N_SHARD * n4), out_dtype), compiler_params=_cp(),
        )(a, wc)
    return _mm(
        name, "nn", a, wc,
        [pl.BlockSpec((tm, k), lambda j, i, r: (first_block + jnp.maximum(i - pad_blocks, 0), 0)),
         pl.BlockSpec((None, k, n4), lambda j, i, r: (j, 0, 0))],
        jax.ShapeDtypeStruct((s + pad_blocks * tm, N_SHARD * n4), out_dtype),
        pl.BlockSpec((tm, n4), lambda j, i, r: (i, j)),
        None, (N_SHARD, nt + pad_blocks, 1), 1, zero_first=pad_blocks > 0)


def mm_cols_dx(name, dy, wc, addend=None):
    s = dy.shape[0]
    _, k, n4 = wc.shape
    tm = _mm_tile(s)
    if _is_small_weight(wc):
        def body(dy_ref, w_ref, *rest):
            part = _dot(dy_ref[...], _all_shards(w_ref), "nt")
            rest[-1][...] = part if addend is None else part + addend[0] * rest[0][...]

        out_spec = pl.BlockSpec((tm, k), lambda i: (i, 0))
        extra, extra_specs = ([], []) if addend is None else ([addend[1]], [out_spec])
        return _call(
            body, name=name, grid=(s // tm,),
            in_specs=[pl.BlockSpec((tm, N_SHARD * n4), lambda i: (i, 0)), _full(wc.shape)] + extra_specs,
            out_specs=out_spec, out_shape=jax.ShapeDtypeStruct((s, k), F32), compiler_params=_cp(),
        )(dy, wc, *extra)
    return _mm(
        name, "nt", dy, wc,
        [pl.BlockSpec((tm, n4), lambda g, i, r: (i, r)),
         pl.BlockSpec((None, k, n4), lambda g, i, r: (r, 0, 0))],
        jax.ShapeDtypeStruct((s, k), F32),
        pl.BlockSpec((tm, k), lambda g, i, r: (i, 0)),
        (tm, k), (1, s // tm, N_SHARD), N_SHARD, addend=addend)


def mm_cols_dw(name, a, dy, part=(0, 1)):
    s, k = a.shape
    s //= part[1]
    n4 = dy.shape[1] // N_SHARD
    tm = _mm_tile(s)
    nt = s // tm
    first_block = part[0] * nt
    if k * n4 * N_SHARD * 2 <= SMALL_WEIGHT_BYTES:
        def body(a_ref, dy_ref, o_ref):
            full = _dot(a_ref[...], dy_ref[...], "tn")
            first = pl.program_id(0) == 0
            for j in range(N_SHARD):
                _acc_add(o_ref.at[j], first, full[:, j * n4:(j + 1) * n4])

        return _call(
            body, name=name, grid=(nt,),
            in_specs=[pl.BlockSpec((tm, k), lambda r: (first_block + r, 0)),
                      pl.BlockSpec((tm, N_SHARD * n4), lambda r: (r, 0))],
            out_specs=_full((N_SHARD, k, n4)),
            out_shape=jax.ShapeDtypeStruct((N_SHARD, k, n4), F32), compiler_params=_cp(),
        )(a, dy)
    return _mm(
        name, "tn", a, dy,
        [pl.BlockSpec((tm, k), lambda j, g, r: (first_block + r, 0)),
         pl.BlockSpec((tm, n4), lambda j, g, r: (r, j))],
        jax.ShapeDtypeStruct((N_SHARD, k, n4), F32),
        pl.BlockSpec((None, k, n4), lambda j, g, r: (j, 0, 0)),
        (k, n4), (N_SHARD, 1, nt), nt)


def _k_tile(k):
    return k if k <= 1024 else k // 2


def mm_rows_fwd(name, a, wr, out_dtype=F32):
    s, k = a.shape
    n = wr.shape[1]
    tm = _mm_tile(s)
    tk = _k_tile(k)
    nk = k // tk
    return _mm(
        name, "nn", a, wr,
        [pl.BlockSpec((tm, tk), lambda g, i, r: (i, r)),
         pl.BlockSpec((tk, n), lambda g, i, r: (r, 0))],
        jax.ShapeDtypeStruct((s, n), out_dtype),
        pl.BlockSpec((tm, n), lambda g, i, r: (i, 0)),
        (tm, n), (1, s // tm, nk), nk)


def mm_rows_dx(name, dy, wr, out_dtype=F32):
    s, n = dy.shape
    k = wr.shape[0]
    tm = _mm_tile(s)
    tk = _k_tile(k)
    return _mm(
        name, "nt", dy, wr,
        [pl.BlockSpec((tm, n), lambda j, i, r: (i, 0)),
         pl.BlockSpec((tk, n), lambda j, i, r: (j, 0))],
        jax.ShapeDtypeStruct((s, k), out_dtype),
        pl.BlockSpec((tm, tk), lambda j, i, r: (i, j)),
        None, (k // tk, s // tm, 1), 1)


def mm_rows_dw(name, a, dy):
    s, k = a.shape
    n = dy.shape[1]
    tm = _mm_tile(s)
    tk = _k_tile(k)
    nt = s // tm
    return _mm(
        name, "tn", a, dy,
        [pl.BlockSpec((tm, tk), lambda j, g, r: (r, j)),
         pl.BlockSpec((tm, n), lambda j, g, r: (r, 0))],
        jax.ShapeDtypeStruct((k, n), F32),
        pl.BlockSpec((tk, n), lambda j, g, r: (j, 0)),
        (tk, n), (k // tk, 1, nt), nt)


def _row(tm, c, col=0):
    return pl.BlockSpec((tm, c), lambda i: (i, col))


def _full(shape):
    nd = len(shape)
    return pl.BlockSpec(shape, lambda i: (0,) * nd)


def _prev(tm, h, c, col=0):
    return pl.BlockSpec((h, c), lambda i: (jnp.maximum(i * (tm // h) - 1, 0), col))


def _next(tm, h, c, s, col=0):
    return pl.BlockSpec((h, c), lambda i: (jnp.minimum((i + 1) * (tm // h), s // h - 1), col))


def _acc_add(ref, first, val):
    @pl.when(first)
    def _():
        ref[...] = val

    @pl.when(jnp.logical_not(first))
    def _():
        ref[...] += val


def _colsum(v):
    return jnp.sum(v, axis=0, keepdims=True)


def _ln_stats(z):
    mu = jnp.mean(z, axis=-1, keepdims=True)
    zc = z - mu
    var = jnp.mean(zc * zc, axis=-1, keepdims=True)
    rstd = lax.rsqrt(var + LN_EPS)
    return zc * rstd, rstd


def _ln_bwd(dxhat, xhat, rstd):
    m1 = jnp.mean(dxhat, axis=-1, keepdims=True)
    m2 = jnp.mean(dxhat * xhat, axis=-1, keepdims=True)
    return rstd * (dxhat - m1 - xhat * m2)


def ln_fwd(name, x, f, g, b, ple=None, emit_y=True):
    s, d = x.shape
    tm = _row_tile(s)
    n_in = 2 + (3 if ple is not None else 0)

    def body(*refs):
        x_ref, f_ref = refs[0], refs[1]
        g_ref, b_ref = refs[n_in], refs[n_in + 1]
        xh_ref, rs_ref = refs[-2:]
        z = ALPHA * x_ref[...] + f_ref[...]
        if ple is not None:
            pgl_ref, pp_ref, bg_ref = refs[2:5]
            z = z + _sig(pgl_ref[...] + bg_ref[...]) * pp_ref[...]
        xhat, rstd = _ln_stats(z)
        if emit_y:
            y = xhat * g_ref[...] + b_ref[...]
            refs[n_in + 2][...] = y
            refs[n_in + 3][...] = y.astype(BF16)
        xh_ref[...] = xhat
        rs_ref[...] = jnp.broadcast_to(rstd, rs_ref.shape)

    ins = [x, f]
    specs = [_row(tm, d), _row(tm, d)]
    if ple is not None:
        pgl, pp, bg = ple
        ins += [pgl, pp, bg]
        specs += [_row(tm, d), _row(tm, d), _full((1, d))]
    ins += [g, b]
    specs += [_full((1, d)), _full((1, d))]
    y_shapes = [jax.ShapeDtypeStruct((s, d), F32), jax.ShapeDtypeStruct((s, d), BF16)] if emit_y else []
    outs = _call(
        body, name=name, grid=(s // tm,), in_specs=specs,
        out_specs=[_row(tm, d)] * (len(y_shapes) + 1) + [_row(tm, LANES)],
        out_shape=y_shapes + [jax.ShapeDtypeStruct((s, d), F32), jax.ShapeDtypeStruct((s, LANES), F32)],
        compiler_params=_cp(),
    )(*ins)
    return tuple(outs) if emit_y else (None, None, outs[0], outs[1])


def ln_bwd(name, parts, xhat, rstd, g, ple=None, loss=None):
    s, d = xhat.shape
    tm = _row_tile(s)
    coefs = [c for c, _ in parts]
    n_p = len(parts)
    n_ple = 3 if ple is not None else 0
    n_in = n_p + 3 + n_ple + (2 if loss is not None else 0)

    def body(*refs):
        first = pl.program_id(0) == 0
        xh = refs[n_p][...]
        rs = refs[n_p + 1][:, 0:1]
        g_v = refs[n_p + 2][...]
        outs = refs[n_in:]
        if loss is not None:
            t_ref, b_ref = refs[n_p + 3 + n_ple:n_p + 5 + n_ple]
            err = (xh * g_v + b_ref[...]) - t_ref[...]
            dy = err * (1.0 / d)
            part = 0.5 * jnp.sum(jnp.mean(err * err, axis=-1, keepdims=True), axis=0, keepdims=True)
            _acc_add(outs[-1], first, jnp.broadcast_to(part, outs[-1].shape))
        else:
            dy = coefs[0] * refs[0][...].astype(F32)
            for j in range(1, n_p):
                dy = dy + coefs[j] * refs[j][...].astype(F32)
        dz = _ln_bwd(dy * g_v, xh, rs)
        outs[0][...] = dz
        _acc_add(outs[1], first, _colsum(dy * xh))
        _acc_add(outs[2], first, _colsum(dy))
        if ple is not None:
            pgl_ref, pp_ref, bg_ref = refs[n_p + 3:n_p + 6]
            pg = _sig(pgl_ref[...] + bg_ref[...])
            dpgl = dz * pp_ref[...] * pg * (1.0 - pg)
            outs[3][...] = (dz * pg).astype(BF16)
            outs[4][...] = dpgl.astype(BF16)
            _acc_add(outs[5], first, _colsum(dpgl))

    ins = [p for _, p in parts] + [xhat, rstd, g]
    specs = [_row(tm, d)] * n_p + [_row(tm, d), _row(tm, LANES), _full((1, d))]
    out_specs = [_row(tm, d), _full((1, d)), _full((1, d))]
    out_shape = [jax.ShapeDtypeStruct((s, d), F32), jax.ShapeDtypeStruct((1, d), F32),
                 jax.ShapeDtypeStruct((1, d), F32)]
    if ple is not None:
        pgl, pp, bg = ple
        ins += [pgl, pp, bg]
        specs += [_row(tm, d), _row(tm, d), _full((1, d))]
        out_specs += [_row(tm, d), _row(tm, d), _full((1, d))]
        out_shape += [jax.ShapeDtypeStruct((s, d), BF16), jax.ShapeDtypeStruct((s, d), BF16),
                      jax.ShapeDtypeStruct((1, d), F32)]
    if loss is not None:
        target, b = loss
        ins += [target, b]
        specs += [_row(tm, d), _full((1, d))]
        out_specs += [_full((8, LANES))]
        out_shape += [jax.ShapeDtypeStruct((8, LANES), F32)]
    return _call(
        body, name=name, grid=(s // tm,), in_specs=specs, out_specs=out_specs, out_shape=out_shape,
        compiler_params=_cp(),
    )(*ins)


def _fill_rotations(rot_ref, x, direction):
    n = x.shape[0]
    rot_ref[0] = x
    for b in range(1, SUBLANES):
        if direction < 0:
            rot_ref[b, SUBLANES:n, :] = x[SUBLANES - b:n - b]
        else:
            rot_ref[b, 0:n - SUBLANES, :] = x[b:n - SUBLANES + b]


def _rotated(rot_ref, start, rows, cs, direction=-1):
    b = (-start) % SUBLANES if direction < 0 else start % SUBLANES
    aligned = start + b if direction < 0 else start - b
    return rot_ref[b, pl.ds(aligned, rows), cs]


def _tile_pos(i, tm, rows):
    return (i * tm + lax.broadcasted_iota(jnp.int32, (rows, 1), 0) + 1).astype(F32)


def mixer_fwd(name, u, pool_w, pool_scale, conv_w, conv_b, cn_g, cn_b):
    s = u.shape[0]
    dp = 512
    tm = min(256, s // 4)
    h = CONV_HALO

    def body(a_c, a_p, bv_c, bv_p, bg_c, bg_p, pw_ref, ps_ref, cw_ref, cb_ref, cg_ref, cbt_ref,
             cat_ref, d_ref, e_ref, glu_ref, hh_ref, rs_ref, ext_a, rot_g, conv_out):
        i = pl.program_id(0)
        first = i == 0
        ext_a[0:h, :] = jnp.where(first, 0.0, a_p[...])
        ext_a[h:, :] = a_c[...]
        glu = bv_c[...] * _sig(bg_c[...])
        glu_ref[...] = glu
        _fill_rotations(rot_g, jnp.concatenate([jnp.where(first, 0.0, bv_p[...] * _sig(bg_p[...])), glu], axis=0), -1)
        pos = _tile_pos(i, tm, tm)
        for gi, w in enumerate(POOL_WINDOWS):
            cs = slice(gi * POOL_GROUP, (gi + 1) * POOL_GROUP)
            a_g = ext_a[pl.ds(h, tm), cs]
            acc = a_g
            for sh in range(1, w):
                acc = acc + ext_a[pl.ds(h - sh, tm), cs]
            d_g = acc / jnp.minimum(pos, float(w)) - a_g
            d_ref[:, cs] = d_g.astype(BF16)
            e_g = _dot(d_g, pw_ref[gi], "nn")
            e_ref[:, cs] = e_g
            cat_ref[:, cs] = (e_g * ps_ref[:, cs]).astype(BF16)
        for lg in range(dp // LANES):
            cs = slice(lg * LANES, (lg + 1) * LANES)
            acc = jnp.broadcast_to(cb_ref[:, cs], (tm, LANES))
            for sh in range(CONV_K):
                acc = acc + _rotated(rot_g, h - sh, tm, cs) * cw_ref[pl.ds(CONV_K - 1 - sh, 1), cs]
            conv_out[:, cs] = acc
        hhat, rstd = _ln_stats(conv_out[...])
        hl = hhat * cg_ref[...] + cbt_ref[...]
        cat_ref[:, dp:] = (hl * _sig(hl)).astype(BF16)
        hh_ref[...] = hhat
        rs_ref[...] = jnp.broadcast_to(rstd, rs_ref.shape)

    specs = [_row(tm, dp, 0), _prev(tm, h, dp, 0), _row(tm, dp, 1), _prev(tm, h, dp, 1),
             _row(tm, dp, 2), _prev(tm, h, dp, 2),
             _full((4, POOL_GROUP, POOL_GROUP)), _full((1, dp)), _full((CONV_K, dp)),
             _full((1, dp)), _full((1, dp)), _full((1, dp))]
    out_specs = [_row(tm, 2 * dp), _row(tm, dp), _row(tm, dp), _row(tm, dp), _row(tm, dp), _row(tm, LANES)]
    out_shape = [jax.ShapeDtypeStruct((s, 2 * dp), BF16), jax.ShapeDtypeStruct((s, dp), BF16),
                 jax.ShapeDtypeStruct((s, dp), F32), jax.ShapeDtypeStruct((s, dp), F32),
                 jax.ShapeDtypeStruct((s, dp), F32), jax.ShapeDtypeStruct((s, LANES), F32)]
    return _call(
        body, name=name, grid=(s // tm,), in_specs=specs, out_specs=out_specs, out_shape=out_shape,
        scratch_shapes=[pltpu.VMEM((h + tm, dp), F32), pltpu.VMEM((SUBLANES, h + tm, dp), F32),
                        pltpu.VMEM((tm, dp), F32)],
        compiler_params=_cp(),
    )(u, u, u, u, u, u, pool_w, pool_scale, conv_w, conv_b, cn_g, cn_b)


def mixer_bwd(name, dcat, u, d_sv, e_sv, glu_sv, hh_sv, rs_sv, pool_w, pool_scale, conv_w, cn_g, cn_b):
    s = u.shape[0]
    dp = 512
    tm = min(256, s // 4)
    h = CONV_HALO
    nt = s // tm

    def body(dc_c, dc_n, bv_c, bg_c, d_c, e_c, gl_c, gl_p, hh_c, hh_n, rs_c, rs_n,
             pw_ref, ps_ref, cw_ref, cg_ref, cbt_ref,
             du_ref, dpw_ref, dps_ref, dcw_ref, dcb_ref, dcg_ref, dcbt_ref,
             ext_dh, ext_g, ext_r):
        i = pl.program_id(0)
        first = i == 0
        last = i == nt - 1
        cg = cg_ref[...]

        def conv_grads(dyb, hhat, rstd):
            hl = hhat * cg + cbt_ref[...]
            sg = _sig(hl)
            dhl = dyb * (sg * (1.0 + hl * (1.0 - sg)))
            return _ln_bwd(dhl * cg, hhat, rstd), dhl

        hh_cur = hh_c[...]
        dh_c, dhl_c = conv_grads(dc_c[:, dp:], hh_cur, rs_c[:, 0:1])
        dh_n, _ = conv_grads(dc_n[:, dp:], hh_n[...], rs_n[:, 0:1])
        _fill_rotations(ext_dh, jnp.concatenate([dh_c, jnp.where(last, 0.0, dh_n)], axis=0), 1)
        _fill_rotations(ext_g, jnp.concatenate([jnp.where(first, 0.0, gl_p[...]), gl_c[...]], axis=0), -1)

        @pl.when(first)
        def _():
            dcw_ref[...] = jnp.zeros(dcw_ref.shape, F32)

        for lg in range(dp // LANES):
            cs = slice(lg * LANES, (lg + 1) * LANES)
            dglu = jnp.zeros((tm, LANES), F32)
            for sh in range(CONV_K):
                dglu = dglu + _rotated(ext_dh, sh, tm, cs, 1) * cw_ref[pl.ds(CONV_K - 1 - sh, 1), cs]
            dh_g = ext_dh[0, pl.ds(0, tm), cs]
            for sh in range(CONV_K):
                dcw_ref[pl.ds(CONV_K - 1 - sh, 1), cs] += _colsum(dh_g * _rotated(ext_g, h - sh, tm, cs))
            sgate = _sig(bg_c[:, cs])
            du_ref[:, dp + lg * LANES:dp + (lg + 1) * LANES] = dglu * sgate
            du_ref[:, 2 * dp + lg * LANES:2 * dp + (lg + 1) * LANES] = dglu * bv_c[:, cs] * sgate * (1.0 - sgate)
        _acc_add(dcb_ref, first, _colsum(dh_c))
        _acc_add(dcg_ref, first, _colsum(dhl_c * hh_cur))
        _acc_add(dcbt_ref, first, _colsum(dhl_c))

        pos_c = _tile_pos(i, tm, tm)
        pos_n = _tile_pos(i + 1, tm, h)
        _acc_add(dps_ref, first, _colsum(dc_c[:, :dp] * e_c[...]))
        for gi, w in enumerate(POOL_WINDOWS):
            cs = slice(gi * POOL_GROUP, (gi + 1) * POOL_GROUP)
            pw = pw_ref[gi]
            de_c = dc_c[:, cs] * ps_ref[:, cs]
            de_n = dc_n[:, cs] * ps_ref[:, cs]
            dd_c = _dot(de_c, pw, "nt")
            dd_n = _dot(de_n, pw, "nt")
            ext_r[0:tm, :] = dd_c / jnp.minimum(pos_c, float(w))
            ext_r[tm:, :] = jnp.where(last, 0.0, dd_n / jnp.minimum(pos_n, float(w)))
            acc = -dd_c
            for sh in range(w):
                acc = acc + ext_r[pl.ds(sh, tm), :]
            du_ref[:, cs] = acc
            dpw_g = _dot(d_c[:, cs], de_c, "tn")

            @pl.when(first)
            def _():
                dpw_ref[gi] = dpw_g

            @pl.when(jnp.logical_not(first))
            def _():
                dpw_ref[gi] += dpw_g

    specs = [_row(tm, 2 * dp), _next(tm, h, 2 * dp, s), _row(tm, dp, 1), _row(tm, dp, 2),
             _row(tm, dp), _row(tm, dp), _row(tm, dp), _prev(tm, h, dp),
             _row(tm, dp), _next(tm, h, dp, s), _row(tm, LANES), _next(tm, h, LANES, s),
             _full((4, POOL_GROUP, POOL_GROUP)), _full((1, dp)), _full((CONV_K, dp)),
             _full((1, dp)), _full((1, dp))]
    out_specs = [_row(tm, 3 * dp), _full((4, POOL_GROUP, POOL_GROUP)), _full((1, dp)), _full((CONV_K, dp)),
                 _full((1, dp)), _full((1, dp)), _full((1, dp))]
    out_shape = [jax.ShapeDtypeStruct((s, 3 * dp), F32),
                 jax.ShapeDtypeStruct((4, POOL_GROUP, POOL_GROUP), F32), jax.ShapeDtypeStruct((1, dp), F32),
                 jax.ShapeDtypeStruct((CONV_K, dp), F32), jax.ShapeDtypeStruct((1, dp), F32),
                 jax.ShapeDtypeStruct((1, dp), F32), jax.ShapeDtypeStruct((1, dp), F32)]
    return _call(
        body, name=name, grid=(nt,), in_specs=specs, out_specs=out_specs, out_shape=out_shape,
        scratch_shapes=[pltpu.VMEM((SUBLANES, tm + h, dp), F32), pltpu.VMEM((SUBLANES, h + tm, dp), F32),
                        pltpu.VMEM((tm + h, POOL_GROUP), F32)],
        compiler_params=_cp(),
    )(dcat, dcat, u, u, d_sv, e_sv, glu_sv, glu_sv, hh_sv, hh_sv, rs_sv, rs_sv,
      pool_w, pool_scale, conv_w, cn_g, cn_b)


GELU_C = math.sqrt(2.0 / math.pi)


def _gelu_parts(x):
    x2 = x * x
    t = jnp.tanh(x * (GELU_C + (GELU_C * 0.044715) * x2))
    half_1pt = 0.5 + 0.5 * t
    gelu = x * half_1pt
    dgelu = half_1pt + (0.5 * x) * (1.0 - t * t) * (GELU_C + (3.0 * GELU_C * 0.044715) * x2)
    return gelu, dgelu


def ffn_act_fwd(name, gv, dw_w, dw_b):
    s = gv.shape[0]
    dff = gv.shape[1] // 2
    tm = min(FFN_TILE, s // 4)
    h = FFN_HALO
    rc = FFN_CHUNK_ROWS
    lw = FFN_CHUNK_LANES

    def body(g_c, g_p, v_c, w_ref, b_ref, hid_ref):
        first = pl.program_id(0) == 0

        def chunk(ci, carry):
            r0 = pl.multiple_of(ci * rc, rc)
            above = pl.multiple_of(jnp.maximum(r0 - h, 0), h)
            for lg in range(dff // lw):
                cs = slice(lg * lw, (lg + 1) * lw)
                top = jnp.where(ci == 0, jnp.where(first, 0.0, g_p[:, cs]), g_c[pl.ds(above, h), cs])
                win = jnp.concatenate([top, g_c[pl.ds(r0, rc), cs]], axis=0)
                gc = jnp.broadcast_to(b_ref[:, cs], (rc, lw))
                for sh in range(FFN_K):
                    gc = gc + win[h - sh:h - sh + rc] * w_ref[pl.ds(FFN_K - 1 - sh, 1), cs]
                gelu, _ = _gelu_parts(gc)
                hid_ref[pl.ds(r0, rc), cs] = (gelu * v_c[pl.ds(r0, rc), cs]).astype(BF16)
            return carry

        lax.fori_loop(0, tm // rc, chunk, 0)

    return _call(
        body, name=name, grid=(s // tm,),
        in_specs=[_row(tm, dff, 0), _prev(tm, h, dff, 0), _row(tm, dff, 1), _full((FFN_K, dff)), _full((1, dff))],
        out_specs=_row(tm, dff), out_shape=jax.ShapeDtypeStruct((s, dff), BF16),
        compiler_params=_cp(),
    )(gv, gv, gv, dw_w, dw_b)


def ffn_act_bwd(name, dhid, gv, dw_w, dw_b):
    s = gv.shape[0]
    dff = gv.shape[1] // 2
    tm = min(FFN_TILE, s // 4)
    h = FFN_HALO
    nt = s // tm
    rc = FFN_CHUNK_ROWS
    lw = FFN_CHUNK_LANES
    n_chunks = tm // rc

    def body(dh_c, dh_n, g_p, g_c, g_n, v_c, v_n, w_ref, b_ref, dgv_ref, dw_ref, db_ref):
        i = pl.program_id(0)
        first = i == 0
        last = i == nt - 1

        @pl.when(first)
        def _():
            dw_ref[...] = jnp.zeros(dw_ref.shape, F32)
            db_ref[...] = jnp.zeros(db_ref.shape, F32)

        def chunk(ci, carry):
            r0 = pl.multiple_of(ci * rc, rc)
            above = pl.multiple_of(jnp.maximum(r0 - h, 0), h)
            below = pl.multiple_of(jnp.minimum(r0 + rc, tm - h), h)
            at_end = ci == n_chunks - 1
            for lg in range(dff // lw):
                cs = slice(lg * lw, (lg + 1) * lw)
                top = jnp.where(ci == 0, jnp.where(first, 0.0, g_p[:, cs]), g_c[pl.ds(above, h), cs])
                bot = jnp.where(at_end, g_n[:, cs], g_c[pl.ds(below, h), cs])
                win = jnp.concatenate([top, g_c[pl.ds(r0, rc), cs], bot], axis=0)
                shifted = [win[h - sh:h - sh + rc + h] for sh in range(FFN_K)]
                gc = jnp.broadcast_to(b_ref[:, cs], (rc + h, lw))
                for sh in range(FFN_K):
                    gc = gc + shifted[sh] * w_ref[pl.ds(FFN_K - 1 - sh, 1), cs]
                gelu, dgelu = _gelu_parts(gc)
                dh_mid = dh_c[pl.ds(r0, rc), cs]
                hv_bot = jnp.where(at_end, jnp.where(last, 0.0, dh_n[:, cs] * v_n[:, cs]),
                                   dh_c[pl.ds(below, h), cs] * v_c[pl.ds(below, h), cs])
                dgc = jnp.concatenate([dh_mid * v_c[pl.ds(r0, rc), cs], hv_bot], axis=0) * dgelu
                dgate = jnp.zeros((rc, lw), F32)
                for sh in range(FFN_K):
                    dgate = dgate + dgc[sh:sh + rc] * w_ref[pl.ds(FFN_K - 1 - sh, 1), cs]
                dgv_ref[pl.ds(r0, rc), cs] = dgate.astype(BF16)
                dgv_ref[pl.ds(r0, rc), slice(dff + lg * lw, dff + (lg + 1) * lw)] = (dh_mid * gelu[0:rc]).astype(BF16)
                dgc_mid = dgc[0:rc]
                for sh in range(FFN_K):
                    dw_ref[pl.ds(FFN_K - 1 - sh, 1), cs] += _colsum(dgc_mid * shifted[sh][0:rc])
                db_ref[:, cs] += _colsum(dgc_mid)
            return carry

        lax.fori_loop(0, n_chunks, chunk, 0)

    return _call(
        body, name=name, grid=(nt,),
        in_specs=[_row(tm, dff), _next(tm, h, dff, s),
                  _prev(tm, h, dff, 0), _row(tm, dff, 0), _next(tm, h, dff, s, 0),
                  _row(tm, dff, 1), _next(tm, h, dff, s, 1),
                  _full((FFN_K, dff)), _full((1, dff))],
        out_specs=[_row(tm, 2 * dff), _full((FFN_K, dff)), _full((1, dff))],
        out_shape=[jax.ShapeDtypeStruct((s, 2 * dff), BF16), jax.ShapeDtypeStruct((FFN_K, dff), F32),
                   jax.ShapeDtypeStruct((1, dff), F32)],
        compiler_params=_cp(),
    )(dhid, dhid, gv, gv, gv, gv, gv, dw_w, dw_b)


def _bias_line(rel_bias):
    nh = rel_bias.shape[0]
    line = jnp.concatenate(
        [jnp.zeros((nh, 1), rel_bias.dtype), jnp.broadcast_to(rel_bias[:, 2 * MAX_REL:], (nh, SHEAR_SAT)),
         jnp.flip(rel_bias[:, 1:2 * MAX_REL], axis=1)], axis=1)
    return line[:, None, :]


def bias_tile(name, line):
    nh = line.shape[0]

    def body(l_ref, o_ref):
        x = jnp.broadcast_to(l_ref[...], (Q_TILE, SHEAR_W))
        z = pltpu.roll(x, SHEAR_W - Q_TILE, 1, stride=1, stride_axis=0)
        qc = lax.broadcasted_iota(jnp.int32, (Q_TILE, K_WIN), 0) // CHUNK
        kc = lax.broadcasted_iota(jnp.int32, (Q_TILE, K_WIN), 1) // CHUNK
        o_ref[...] = jnp.where((kc >= qc) & (kc <= qc + LEFT_CHUNKS), z[:, :K_WIN], NEG_INF)

    return _call(
        body, name=name, grid=(nh,), in_specs=[pl.BlockSpec((None, 1, SHEAR_W), lambda hh: (hh, 0, 0))],
        out_specs=pl.BlockSpec((None, Q_TILE, K_WIN), lambda hh: (hh, 0, 0)),
        out_shape=jax.ShapeDtypeStruct((nh, Q_TILE, K_WIN), F32), compiler_params=_cp(),
    )(line)


def _stack_heads(x2, scale=None):
    if scale is not None:
        x2 = x2 * jnp.asarray(scale, x2.dtype)
    lane = lax.broadcasted_iota(jnp.int32, x2.shape, 1)
    zero = jnp.zeros_like(x2)
    return jnp.concatenate([jnp.where(lane < HEAD_DIM, x2, zero), jnp.where(lane < HEAD_DIM, zero, x2)], axis=0)


def _unstack_heads(x_st):
    lane = lax.broadcasted_iota(jnp.int32, (Q_TILE, LANES), 1)
    return jnp.where(lane < HEAD_DIM, x_st[:Q_TILE], x_st[Q_TILE:])


def _attn_probs(q_st, k3, bias_st, t):
    sc = _dot(q_st, k3, "nt") + bias_st
    col = lax.broadcasted_iota(jnp.int32, sc.shape, 1)
    sc = jnp.where(col >= PAD_ROWS - t * Q_TILE, sc, NEG_INF)
    m = jnp.max(sc, axis=-1, keepdims=True)
    p = jnp.exp(sc - m)
    return p * (1.0 / jnp.sum(p, axis=-1, keepdims=True))


def _attn_specs(d_model, pairs):
    nq = PAD_ROWS // Q_TILE
    width = pairs * LANES
    groups = d_model // width
    specs = [pl.BlockSpec((Q_TILE, width), lambda g, t: (t + nq, g))]
    for which in (1, 2):
        for j in range(K_WIN // Q_TILE):
            specs.append(pl.BlockSpec((Q_TILE, width), lambda g, t, j=j, which=which: (t + j, which * groups + g)))
    specs.append(pl.BlockSpec((2 * pairs, Q_TILE, K_WIN), lambda g, t: (g, 0, 0)))
    return specs


def attn_fwd(name, qkvp, bias):
    s = qkvp.shape[0] - PAD_ROWS
    d_model = qkvp.shape[1] // 3
    nw = K_WIN // Q_TILE

    def body(q_ref, *refs):
        k_refs, v_refs, b_ref, o_ref = refs[:nw], refs[nw:2 * nw], refs[2 * nw], refs[2 * nw + 1]
        t = pl.program_id(1)
        for j in range(ATTN_PAIRS_FWD):
            ls = slice(j * LANES, (j + 1) * LANES)
            k3 = jnp.concatenate([r[:, ls] for r in k_refs], axis=0)
            v3 = jnp.concatenate([r[:, ls] for r in v_refs], axis=0)
            bias_st = b_ref[2 * j:2 * j + 2].reshape(2 * Q_TILE, K_WIN)
            p = _attn_probs(_stack_heads(q_ref[:, ls], ATTN_SCALE), k3, bias_st, t)
            o_ref[:, ls] = _unstack_heads(_dot(p, v3, "nn")).astype(BF16)

    width = ATTN_PAIRS_FWD * LANES
    return _call(
        body, name=name, grid=(d_model // width, s // Q_TILE),
        in_specs=_attn_specs(d_model, ATTN_PAIRS_FWD), out_specs=pl.BlockSpec((Q_TILE, width), lambda g, t: (t, g)),
        out_shape=jax.ShapeDtypeStruct((s, d_model), BF16), compiler_params=_cp(),
    )(qkvp, *([qkvp] * (2 * nw)), bias)


def attn_bwd(name, qkvp, bias, do):
    s = qkvp.shape[0] - PAD_ROWS
    d_model = qkvp.shape[1] // 3
    nw = K_WIN // Q_TILE
    nt = s // Q_TILE

    def body(q_ref, *refs):
        k_refs, v_refs = refs[:nw], refs[nw:2 * nw]
        b_ref, do_ref, dq_ref, dk_ref, dv_ref, ds_ref, dk_acc, dv_acc = refs[2 * nw:]
        t = pl.program_id(1)
        first = t == 0

        @pl.when(first)
        def _():
            dk_acc[...] = jnp.zeros(dk_acc.shape, F32)
            dv_acc[...] = jnp.zeros(dv_acc.shape, F32)
            ds_ref[...] = jnp.zeros(ds_ref.shape, F32)

        start = pl.multiple_of(t * Q_TILE, Q_TILE)
        for j in range(ATTN_PAIRS):
            ls = slice(j * LANES, (j + 1) * LANES)
            q_st = _stack_heads(q_ref[:, ls], ATTN_SCALE)
            do_st = _stack_heads(do_ref[:, ls])
            k3 = jnp.concatenate([r[:, ls] for r in k_refs], axis=0)
            v3 = jnp.concatenate([r[:, ls] for r in v_refs], axis=0)
            p = _attn_probs(q_st, k3, b_ref[2 * j:2 * j + 2].reshape(2 * Q_TILE, K_WIN), t)
            dp = _dot(do_st, v3, "nt")
            ds = p * (dp - jnp.sum(p * dp, axis=-1, keepdims=True))
            ds_ref[2 * j:2 * j + 2] += ds.reshape(2, Q_TILE, K_WIN)
            dsb = ds.astype(BF16)
            dq_ref[:, ls] = (_unstack_heads(_dot(dsb, k3, "nn")) * ATTN_SCALE).astype(BF16)
            dk_acc[pl.ds(start, K_WIN), ls] += _dot(dsb, q_st, "tn")
            dv_acc[pl.ds(start, K_WIN), ls] += _dot(p, do_st, "tn")

        @pl.when(t == nt - 1)
        def _():
            dk_ref[...] = dk_acc[pl.ds(PAD_ROWS, s), :].astype(BF16)
            dv_ref[...] = dv_acc[pl.ds(PAD_ROWS, s), :].astype(BF16)

    specs = _attn_specs(d_model, ATTN_PAIRS) + [pl.BlockSpec((Q_TILE, ATTN_LANES), lambda g, t: (t, g))]
    col_spec = pl.BlockSpec((s, ATTN_LANES), lambda g, t: (0, g))
    return _call(
        body, name=name, grid=(d_model // ATTN_LANES, nt), in_specs=specs,
        out_specs=[pl.BlockSpec((Q_TILE, ATTN_LANES), lambda g, t: (t, g)), col_spec, col_spec,
                   pl.BlockSpec((2 * ATTN_PAIRS, Q_TILE, K_WIN), lambda g, t: (g, 0, 0))],
        out_shape=[jax.ShapeDtypeStruct((s, d_model), BF16)] * 3
        + [jax.ShapeDtypeStruct((N_HEADS, Q_TILE, K_WIN), F32)],
        scratch_shapes=[pltpu.VMEM((PAD_ROWS + s, ATTN_LANES), F32), pltpu.VMEM((PAD_ROWS + s, ATTN_LANES), F32)],
        compiler_params=_cp(),
    )(qkvp, *([qkvp] * (2 * nw)), bias, do)


def bias_grad_reduce(name, ds_sum):
    nh = ds_sum.shape[0]
    width = SHEAR_W + Q_TILE
    first_k = Q_TILE - 1

    def body(x_ref, col_ref, sat_ref):
        x = x_ref[...]
        hi = x.astype(BF16)
        lo = (x - hi.astype(F32)).astype(BF16)
        r = lax.broadcasted_iota(jnp.int32, (Q_TILE, Q_TILE), 0)
        c = lax.broadcasted_iota(jnp.int32, (Q_TILE, Q_TILE), 1)
        exchange = jnp.where(r + c == Q_TILE - 1, 1.0, 0.0).astype(BF16)
        x_rev = _dot(exchange, hi, "nn") + _dot(exchange, lo, "nn")
        zeros = jnp.zeros((Q_TILE, Q_TILE), F32)
        y = pltpu.roll(jnp.concatenate([zeros, x_rev, zeros], axis=1), 0, 1, stride=1, stride_axis=0)
        cols = _colsum(y)
        col_ref[...] = cols
        k = lax.broadcasted_iota(jnp.int32, cols.shape, 1) - first_k
        tot = jnp.sum(jnp.where((k >= 1) & (k <= SHEAR_SAT), cols, 0.0), axis=-1, keepdims=True)
        sat_ref[...] = jnp.broadcast_to(tot, sat_ref.shape)

    return _call(
        body, name=name, grid=(nh,),
        in_specs=[pl.BlockSpec((None, Q_TILE, K_WIN), lambda hh: (hh, 0, 0))],
        out_specs=[pl.BlockSpec((None, 1, width), lambda hh: (hh, 0, 0)),
                   pl.BlockSpec((None, 1, LANES), lambda hh: (hh, 0, 0))],
        out_shape=[jax.ShapeDtypeStruct((nh, 1, width), F32), jax.ShapeDtypeStruct((nh, 1, LANES), F32)],
        compiler_params=_cp(),
    )(ds_sum)


def _ew_rows(r, most=512, cols=None):
    if cols is not None and r * cols * 4 <= SMALL_BLOCK_BYTES:
        return r
    for cand in range(min(most, r) // 16 * 16, 0, -16):
        if r % cand == 0:
            return cand
    return r


def cast_into_gathered(name, w, layer, s_idx, n_blocks=N_SHARD, dtype=BF16, token=None):
    r, c = w.shape[-2:]
    tr = _ew_rows(r, cols=c)

    def body(s_ref, w_ref, *rest):
        rest[-1][...] = w_ref[...].astype(dtype)

    extra = [] if token is None else [token]
    grid_spec = pltpu.PrefetchScalarGridSpec(
        num_scalar_prefetch=1, grid=(r // tr,),
        in_specs=[pl.BlockSpec((None, tr, c), lambda i, s_ref: (layer, i, 0))] + [ANY_SPEC] * len(extra),
        out_specs=pl.BlockSpec((None, tr, c), lambda i, s_ref: (s_ref[0], i, 0)))
    return _call(
        body, name=name, grid_spec=grid_spec, out_shape=jax.ShapeDtypeStruct((n_blocks, r, c), dtype),
        compiler_params=_cp(),
    )(s_idx, w, *extra)


def adamw(name, w, grads, m, v, token=None):
    nl, r, c = w.shape
    tr = _ew_rows(r, 256, cols=c)

    def body(*refs):
        w_ref, m_ref, v_ref = refs[0], refs[1], refs[2]
        g_refs = refs[3:3 + nl]
        d_ref, nm_ref, nv_ref = refs[-3:]
        layer = pl.program_id(0)
        g = g_refs[0][...]
        for j in range(1, nl):
            g = jnp.where(layer == j, g_refs[j][...], g)
        d_ref[...], nm_ref[...], nv_ref[...] = _adamw_update(w_ref[...], g, m_ref[...], v_ref[...])

    p_spec = pl.BlockSpec((None, tr, c), lambda l, i: (l, i, 0))
    g_spec = pl.BlockSpec((tr, c), lambda l, i: (i, 0))
    extra = [] if token is None else [token]
    extra_specs = [] if token is None else [ANY_SPEC]
    return _call(
        body, name=name, grid=(nl, r // tr), in_specs=[p_spec] * 3 + [g_spec] * nl + extra_specs,
        out_specs=[p_spec] * 3, out_shape=[jax.ShapeDtypeStruct((nl, r, c), F32)] * 3, compiler_params=_cp(),
    )(w, m, v, *grads, *extra)


def _adamw_update(w, g, m, v):
    nm = ADAM_B1 * m + (1.0 - ADAM_B1) * g
    nv = ADAM_B2 * v + (1.0 - ADAM_B2) * (g * g)
    delta = -ADAM_LR * ((nm / ADAM_BC1) / (jnp.sqrt(nv / ADAM_BC2) + ADAM_EPS) + ADAM_WD * w)
    return delta, nm, nv


def adamw_many(name, ws, gs, ms, vs, token):
    n = len(ws)

    def body(*refs):
        ins, outs = refs[:4 * n], refs[4 * n + 1:]
        for i in range(n):
            delta, nm, nv = _adamw_update(ins[i][...], ins[n + i][...], ins[2 * n + i][...], ins[3 * n + i][...])
            outs[3 * i][...] = delta
            outs[3 * i + 1][...] = nm
            outs[3 * i + 2][...] = nv

    vmem = pl.BlockSpec(memory_space=pltpu.VMEM)
    shapes = [jax.ShapeDtypeStruct(w.shape, F32) for w in ws for _ in range(3)]
    outs = _call(
        body, name=name, in_specs=[vmem] * (4 * n) + [ANY_SPEC], out_specs=[vmem] * (3 * n), out_shape=shapes,
        compiler_params=_cp(),
    )(*ws, *gs, *ms, *vs, token)
    return [tuple(outs[3 * i:3 * i + 3]) for i in range(n)]


def sum_blocks(name, gathered, n_blocks):
    r = gathered.shape[0] // n_blocks
    c = gathered.shape[1]
    tr = r if r <= SUM_BLOCK_ROWS else _ew_rows(r)
    nt = r // tr

    def body(*refs):
        acc = refs[0][...]
        for j in range(1, n_blocks):
            acc = acc + refs[j][...]
        refs[-1][...] = acc

    specs = [pl.BlockSpec((tr, c), lambda i, j=j: (j * nt + i, 0)) for j in range(n_blocks)]
    return _call(
        body, name=name, grid=(nt,), in_specs=specs, out_specs=pl.BlockSpec((tr, c), lambda i: (i, 0)),
        out_shape=jax.ShapeDtypeStruct((r, c), F32), compiler_params=_cp(),
    )(*([gathered] * n_blocks))


def _place():
    return lax.axis_index("x"), lax.axis_index("y"), lax.axis_index("c")


def _other_chips(x, y):
    return [(1 - x, y), (x, 1 - y), (1 - x, 1 - y)]


HBM_SPEC = pl.BlockSpec(memory_space=pltpu.HBM)
SEM_SPEC = pl.BlockSpec(memory_space=pltpu.SEMAPHORE)
ANY_SPEC = pl.BlockSpec(memory_space=pl.ANY)
EFFECT = pltpu.SideEffectType.DATAFLOW_SIDE_EFFECTING


def copies_start(name, bufs, plan, n_copies):
    n = len(bufs)

    def body(*refs):
        send, recv = refs[n], refs[n + 1]
        token = refs[2 * n + 2]
        for k, (src, dst, peer, _) in enumerate(plan(refs[:n])):
            pltpu.make_async_remote_copy(
                src_ref=src, dst_ref=dst, send_sem=send.at[k], recv_sem=recv.at[k],
                device_id=peer, device_id_type=MESH).start()
        token[...] = jnp.zeros(token.shape, F32)

    outs = pl.pallas_call(
        body, name=name,
        out_shape=(pltpu.SemaphoreType.DMA((n_copies,)), pltpu.SemaphoreType.DMA((n_copies,)),
                   *[pltpu.HBM(b.shape, b.dtype) for b in bufs], jax.ShapeDtypeStruct((8, LANES), F32)),
        in_specs=[HBM_SPEC] * n,
        out_specs=(SEM_SPEC, SEM_SPEC, *([HBM_SPEC] * n), pl.BlockSpec(memory_space=pltpu.VMEM)),
        input_output_aliases={a: a + 2 for a in range(n)},
        compiler_params=pltpu.CompilerParams(has_side_effects=EFFECT),
    )(*[_in_hbm(b) for b in bufs])
    return outs[0], outs[1], list(outs[2:2 + n]), outs[2 + n]


def copies_wait(name, bufs, send, recv, plan, sem_base, after):
    n = len(bufs)

    def body(*refs):
        send_ref, recv_ref = refs[n], refs[n + 1]
        for k, (src, _, peer, land) in enumerate(plan(refs[:n])):
            cp = pltpu.make_async_remote_copy(
                src_ref=src, dst_ref=land, send_sem=send_ref.at[sem_base + k], recv_sem=recv_ref.at[sem_base + k],
                device_id=peer, device_id_type=MESH)
            cp.wait_send()
            cp.wait_recv()

    outs = pl.pallas_call(
        body, name=name,
        out_shape=tuple(pltpu.HBM(b.shape, b.dtype) for b in bufs),
        in_specs=[HBM_SPEC] * n + [SEM_SPEC, SEM_SPEC, ANY_SPEC], out_specs=tuple([HBM_SPEC] * n),
        input_output_aliases={a: a for a in range(n)},
        compiler_params=pltpu.CompilerParams(has_side_effects=EFFECT),
    )(*bufs, send, recv, after)
    return list(outs)


def gather_plan(refs):
    x, y, c = _place()
    me = 2 * x + y
    return [(buf.at[me], buf.at[me], (cx, cy, c), buf.at[2 * cx + cy])
            for buf in refs for cx, cy in _other_chips(x, y)]


def all_plan(refs):
    x, y, c = _place()
    me = 4 * x + 2 * y + c
    out = []
    for buf in refs:
        for flip in range(1, 8):
            px = 1 - x if flip & 4 else x
            py = 1 - y if flip & 2 else y
            pc = 1 - c if flip & 1 else c
            out.append((buf.at[me], buf.at[me], (px, py, pc), buf.at[4 * px + 2 * py + pc]))
    return out


def swap_plan(refs):
    x, y, c = _place()
    n = len(refs) // 2
    out = []
    for g, land in zip(refs[:n], refs[n:]):
        hr = g.shape[1] // 2
        out.append((g.at[:, pl.ds((1 - c) * hr, hr)], land, (x, y, 1 - c), land))
    return out


def owners_plan(refs):
    x, y, c = _place()
    n = len(refs) // 2
    return [(src.at[2 * cx + cy], land.at[j], (cx, cy, c), land.at[j])
            for src, land in zip(refs[:n], refs[n:]) for j, (cx, cy) in enumerate(_other_chips(x, y))]


def join_plan(refs):
    x, y, c = _place()
    out = []
    for buf in refs:
        hr = buf.shape[0] // 2
        mine = buf.at[pl.ds(c * hr, hr)]
        out.append((mine, mine, (x, y, 1 - c), buf.at[pl.ds((1 - c) * hr, hr)]))
    return out


def add_halves(name, grad, landed, sc_idx):
    _, r, c = grad.shape
    hr = r // 2
    tr = _ew_rows(hr)
    nt = hr // tr

    def body(sc_ref, g_ref, l_ref, own_ref, wire_ref):
        tot = g_ref[...] + l_ref[...]
        wire_ref[...] = tot.astype(BF16)

        @pl.when(pl.program_id(1) == sc_ref[0])
        def _():
            own_ref[...] = tot

    grid_spec = pltpu.PrefetchScalarGridSpec(
        num_scalar_prefetch=1, grid=(nt, N_SHARD),
        in_specs=[pl.BlockSpec((None, tr, c), lambda i, sh, sc_ref: (sh, sc_ref[1] * nt + i, 0)),
                  pl.BlockSpec((None, tr, c), lambda i, sh, sc_ref: (sh, i, 0))],
        out_specs=[pl.BlockSpec((tr, c), lambda i, sh, sc_ref: (i, 0)),
                   pl.BlockSpec((None, tr, c), lambda i, sh, sc_ref: (sh, i, 0))])
    return _call(
        body, name=name, grid_spec=grid_spec,
        out_shape=[jax.ShapeDtypeStruct((hr, c), F32), jax.ShapeDtypeStruct((N_SHARD, hr, c), BF16)],
        compiler_params=_cp(),
    )(sc_idx, grad, landed)


def add_owned(name, own, landed, sc_idx):
    hr, c = own.shape
    tr = _ew_rows(hr)
    nt = hr // tr

    def body(sc_ref, o_ref, l0, l1, l2, out_ref):
        out_ref[...] = ((o_ref[...] + l0[...].astype(F32)) + l1[...].astype(F32)) + l2[...].astype(F32)

    grid_spec = pltpu.PrefetchScalarGridSpec(
        num_scalar_prefetch=1, grid=(nt,),
        in_specs=[pl.BlockSpec((tr, c), lambda i, sc_ref: (i, 0))]
        + [pl.BlockSpec((None, tr, c), lambda i, sc_ref, j=j: (j, i, 0)) for j in range(3)],
        out_specs=pl.BlockSpec((tr, c), lambda i, sc_ref: (sc_ref[1] * nt + i, 0)))
    return _call(
        body, name=name, grid_spec=grid_spec, out_shape=jax.ShapeDtypeStruct((2 * hr, c), F32),
        compiler_params=_cp(),
    )(sc_idx, own, landed, landed, landed)


PACK_QUANTUM = 8 * LANES


def _pack(arrays):
    pieces = []
    for a in arrays:
        flat = a.reshape(-1)
        padded = -(-flat.shape[0] // PACK_QUANTUM) * PACK_QUANTUM
        pieces.append(jnp.pad(flat, (0, padded - flat.shape[0])).reshape(-1, LANES))
    return jnp.concatenate(pieces, axis=0)


def _unpack(packed, shapes):
    out = []
    row = 0
    for shp in shapes:
        size = math.prod(shp)
        rows = -(-size // PACK_QUANTUM) * 8
        out.append(packed[row:row + rows].reshape(-1)[:size].reshape(shp))
        row += rows
    return out


def kernel(x, p, mix_w_in, pool_w, pool_scale, conv_dw_w, conv_dw_b, conv_ln_g, conv_ln_b, mix_w_out, attn_w_qkv, attn_rel_bias, attn_w_o, ln_mix_g, ln_mix_b, ffn_w_up, ffn_dw_w, ffn_dw_b, ffn_w_down, ple_w_proj, ple_w_gate, ple_b_gate, ln_ffn_g, ln_ffn_b, loss_target, m_mix_w_in, m_pool_w, m_pool_scale, m_conv_dw_w, m_conv_dw_b, m_conv_ln_g, m_conv_ln_b, m_mix_w_out, m_attn_w_qkv, m_attn_rel_bias, m_attn_w_o, m_ln_mix_g, m_ln_mix_b, m_ffn_w_up, m_ffn_dw_w, m_ffn_dw_b, m_ffn_w_down, m_ple_w_proj, m_ple_w_gate, m_ple_b_gate, m_ln_ffn_g, m_ln_ffn_b, v_mix_w_in, v_pool_w, v_pool_scale, v_conv_dw_w, v_conv_dw_b, v_conv_ln_g, v_conv_ln_b, v_mix_w_out, v_attn_w_qkv, v_attn_rel_bias, v_attn_w_o, v_ln_mix_g, v_ln_mix_b, v_ffn_w_up, v_ffn_dw_w, v_ffn_dw_b, v_ffn_w_down, v_ple_w_proj, v_ple_w_gate, v_ple_b_gate, v_ln_ffn_g, v_ln_ffn_b):
    xi, yi, ci = _place()
    shard_idx = (2 * xi + yi).astype(jnp.int32)
    s_arr = shard_idx.reshape(1)
    c_arr = ci.astype(jnp.int32).reshape(1)
    sc_arr = jnp.concatenate([s_arr, c_arr])

    x0 = x[0]
    target = loss_target[0]
    p_rows = p.reshape(p.shape[0] * p.shape[2], p.shape[3])
    seq = x0.shape[0]

    big = [
        ("mix_w_in", mix_w_in, m_mix_w_in, v_mix_w_in, True),
        ("mix_w_out", mix_w_out, m_mix_w_out, v_mix_w_out, False),
        ("attn_w_qkv", attn_w_qkv, m_attn_w_qkv, v_attn_w_qkv, True),
        ("attn_w_o", attn_w_o, m_attn_w_o, v_attn_w_o, False),
        ("ffn_w_up", ffn_w_up, m_ffn_w_up, v_ffn_w_up, True),
        ("ffn_w_down", ffn_w_down, m_ffn_w_down, v_ffn_w_down, False),
        ("ple_w_proj", ple_w_proj, m_ple_w_proj, v_ple_w_proj, True),
        ("ple_w_gate", ple_w_gate, m_ple_w_gate, v_ple_w_gate, False),
    ]
    params = {nm: w for nm, w, _, _, _ in big}
    col_sharded = {nm: cs for nm, _, _, _, cs in big}
    keys = [("mix_w_in", 0), ("mix_w_out", 0), ("ffn_w_up", 0), ("ffn_w_down", 0), ("ple_w_gate", 0),
            ("ple_w_proj", 0), ("attn_w_qkv", 0), ("attn_w_o", 0), ("ffn_w_up", 1), ("ffn_w_down", 1),
            ("ple_w_gate", 1), ("ple_w_proj", 1)]
    dw_shapes = [conv_dw_w.shape, ffn_dw_w.shape]
    dw_block = cast_into_gathered("place_dw", _pack([conv_dw_w, ffn_dw_w])[None], 0, s_arr, dtype=F32)
    n_first = 2
    started = {}
    gather_token = None
    for tag, group in (("first", keys[:n_first]), ("rest", keys[n_first:])):
        shards = [cast_into_gathered(f"cast_{nm}_{layer}", params[nm], layer, s_arr, token=gather_token)
                  for nm, layer in group]
        if tag == "first":
            shards.append(dw_block)
        send, recv, bufs, gather_token = copies_start(f"gather_start_{tag}", shards, gather_plan, 3 * len(shards))
        for a, key in enumerate(group):
            started[key] = (send, recv, bufs[a], 3 * a)
        if tag == "first":
            dw_started = (send, recv, bufs[-1], 3 * len(group))
    arrived_w = {}

    def weight(nm, layer, after=None):
        key = (nm, layer)
        if key not in arrived_w:
            send, recv, buf, base = started[key]
            arrived_w[key] = copies_wait(f"gather_wait_{nm}_{layer}", [buf], send, recv, gather_plan, base, after)[0]
        g = arrived_w[key]
        if col_sharded[nm]:
            return g
        return g.reshape(g.shape[0] * g.shape[1], g.shape[2])

    def tie(a, token):
        return a + token[0:1, 0:1].astype(a.dtype)

    class Reducer:
        def __init__(self, tag, group):
            self.tag, self.group, self.stage = tag, group, 0
            self.n = len(group)
            self.result = None

        def advance(self, after):
            tag, n = self.tag, self.n
            if self.stage == 0:
                grads = []
                for key in self.group:
                    g = big_grads[key]
                    grads.append(g if g.ndim == 3 else g.reshape(N_SHARD, g.shape[0] // N_SHARD, g.shape[1]))
                lands = [lax.empty((N_SHARD, g.shape[1] // 2, g.shape[2]), F32) for g in grads]
                self.sems = copies_start(f"swap_start_{tag}", grads + lands, swap_plan, n)
            elif self.stage == 1:
                send, recv, bufs, _ = self.sems
                outs = copies_wait(f"swap_wait_{tag}", bufs, send, recv, swap_plan, 0, after)
                self.own, wire = [], []
                for key, g, ld in zip(self.group, outs[:n], outs[n:]):
                    o, ob = add_halves(f"add_halves_{key[0]}_{key[1]}", g, ld, sc_arr)
                    self.own.append(o)
                    wire.append(ob)
                lands = [lax.empty((3,) + w.shape[1:], BF16) for w in wire]
                self.sems = copies_start(f"owners_start_{tag}", wire + lands, owners_plan, 3 * n)
            elif self.stage == 2:
                send, recv, bufs, _ = self.sems
                outs = copies_wait(f"owners_wait_{tag}", bufs, send, recv, owners_plan, 0, after)
                finals = [add_owned(f"add_owned_{key[0]}_{key[1]}", o, ar, sc_arr)
                          for key, o, ar in zip(self.group, self.own, outs[n:])]
                self.sems = copies_start(f"join_start_{tag}", finals, join_plan, n)
            elif self.stage == 3:
                send, recv, bufs, _ = self.sems
                outs = copies_wait(f"join_wait_{tag}", bufs, send, recv, join_plan, 0, after)
                self.result = dict(zip(self.group, outs))
                self.sems = None
            self.stage += 1
            return None if self.sems is None else self.sems[3]

    dw_cache = []

    def conv_weights(after):
        if not dw_cache:
            send, recv, buf, base = dw_started
            dw_all = copies_wait("gather_wait_dw", [buf], send, recv, gather_plan, base, after)[0]
            dw_parts = [_unpack(dw_all[k], dw_shapes) for k in range(N_SHARD)]
            dw_cache.append(jnp.concatenate([pc[0] for pc in dw_parts], axis=2)[0])
            dw_cache.append(jnp.concatenate([pc[1] for pc in dw_parts], axis=2))
        return dw_cache

    big_grads = {}
    small_grads = {}

    saved = []
    h_in = x0
    h_in_b = x0
    for layer in range(N_LAYERS):
        sv = {"x_in": h_in_b}
        if layer % 2 == 0:
            u = mm_cols_fwd("mix_in", h_in_b, weight("mix_w_in", 0, gather_token), F32)
            conv_w_full, ffn_dw_full = conv_weights(u)
            cat, d_sv, e_sv, glu_sv, hh_sv, rs_sv = mixer_fwd(
                "mixer_fwd", u, pool_w[0], pool_scale, conv_w_full, conv_dw_b, conv_ln_g, conv_ln_b)
            mix = mm_rows_fwd("mix_out", cat, weight("mix_w_out", 0, cat))
            sv.update(u=u, cat=cat, d=d_sv, e=e_sv, glu=glu_sv, hh=hh_sv, rs=rs_sv)
        else:
            qkvp = mm_cols_fwd("attn_qkv", h_in_b, weight("attn_w_qkv", 0, h_in_b), BF16,
                               pad_blocks=PAD_ROWS // _row_tile(seq))
            bias = bias_tile("bias_tile", _bias_line(attn_rel_bias[0]))
            att = attn_fwd("attn_fwd", qkvp, bias)
            mix = mm_rows_fwd("attn_out", att, weight("attn_w_o", 0, att))
            sv.update(qkvp=qkvp, bias=bias, att=att)
        x1, x1_b, xh1, rs1 = ln_fwd(f"ln_mix_{layer}", h_in, mix, ln_mix_g[layer:layer + 1],
                                    ln_mix_b[layer:layer + 1])
        gv = mm_cols_fwd(f"ffn_up_{layer}", x1_b, weight("ffn_w_up", layer, x1_b), F32)
        hid = ffn_act_fwd(f"ffn_act_{layer}", gv, ffn_dw_full[layer], ffn_dw_b[layer:layer + 1])
        ffn = mm_rows_fwd(f"ffn_down_{layer}", hid, weight("ffn_w_down", layer, hid))
        pgl = mm_rows_fwd(f"ple_gate_{layer}", x1_b, weight("ple_w_gate", layer, ffn))
        pp = mm_cols_fwd(f"ple_proj_{layer}", p_rows, weight("ple_w_proj", layer, pgl), F32, part=(layer, N_LAYERS))
        bg = ple_b_gate[layer:layer + 1]
        x2, x2_b, xh2, rs2 = ln_fwd(f"ln_ffn_{layer}", x1, ffn, ln_ffn_g[layer:layer + 1], ln_ffn_b[layer:layer + 1],
                                    ple=(pgl, pp, bg), emit_y=layer < N_LAYERS - 1)
        sv.update(x1=x1_b, xh1=xh1, rs1=rs1, gv=gv, hid=hid, pgl=pgl, pp=pp, xh2=xh2, rs2=rs2)
        saved.append(sv)
        h_in, h_in_b = x2, x2_b

    reducers = []

    def open_group(tag, group):
        reducers.append(Reducer(tag, group))
        return reducers[-1].advance(None)

    def hook(after):
        token = None
        for red in reducers:
            if red.stage < 4:
                tk = red.advance(after)
                if tk is not None:
                    token = tk if token is None else token + tk
        return token

    def tied(a, token):
        return a if token is None else tie(a, token)

    parts = []
    token = None
    for layer in reversed(range(N_LAYERS)):
        sv = saved[layer]
        bg = ple_b_gate[layer:layer + 1]
        if layer == 0:
            token = open_group("layer1", [("attn_w_qkv", 0), ("attn_w_o", 0), ("ffn_w_up", 1), ("ffn_w_down", 1),
                                          ("ple_w_gate", 1), ("ple_w_proj", 1)])
        last = layer == N_LAYERS - 1
        res = ln_bwd(
            f"ln_ffn_bwd_{layer}", parts, sv["xh2"], sv["rs2"], tied(ln_ffn_g[layer:layer + 1], token),
            ple=(sv["pgl"], sv["pp"], bg), loss=(target, ln_ffn_b[layer:layer + 1]) if last else None)
        dz2, dg2, db2, dpp, dpgl, dbg = res[:6]
        if last:
            loss_part = res[6]
        small_grads[("ln_ffn_g", layer)] = dg2
        small_grads[("ln_ffn_b", layer)] = db2
        small_grads[("ple_b_gate", layer)] = dbg
        w_down = weight("ffn_w_down", layer)
        dhid = mm_rows_dx(f"ffn_down_dx_{layer}", dz2, w_down)
        big_grads[("ffn_w_down", layer)] = mm_rows_dw(f"ffn_down_dw_{layer}", sv["hid"], dz2)
        token = hook(big_grads[("ffn_w_down", layer)])
        dgv, ddw, ddb = ffn_act_bwd(f"ffn_act_bwd_{layer}", dhid, sv["gv"], ffn_dw_full[layer],
                                    tied(ffn_dw_b[layer:layer + 1], token))
        small_grads[("ffn_dw_w", layer)] = ddw
        small_grads[("ffn_dw_b", layer)] = ddb
        big_grads[("ffn_w_up", layer)] = mm_cols_dw(f"ffn_up_dw_{layer}", sv["x1"], dgv)
        t_up = mm_cols_dx(f"ffn_up_dx_{layer}", dgv, weight("ffn_w_up", layer))
        token = hook(t_up)
        big_grads[("ple_w_gate", layer)] = mm_rows_dw(f"ple_gate_dw_{layer}", sv["x1"], dpgl)
        t_gate = mm_rows_dx(f"ple_gate_dx_{layer}", dpgl, weight("ple_w_gate", layer))
        big_grads[("ple_w_proj", layer)] = mm_cols_dw(f"ple_proj_dw_{layer}", p_rows, dpp, part=(layer, N_LAYERS))
        token2 = hook(big_grads[("ple_w_proj", layer)])
        if token2 is not None:
            token = token2 if token is None else token + token2
        if layer == 0:
            token3 = open_group("layer0_ffn", [("ffn_w_up", 0), ("ffn_w_down", 0), ("ple_w_gate", 0), ("ple_w_proj", 0)])
            token = token3 if token is None else token + token3
        dz1, dg1, db1 = ln_bwd(
            f"ln_mix_bwd_{layer}", [(ALPHA, dz2), (1.0, t_up), (1.0, t_gate)], sv["xh1"], sv["rs1"],
            tied(ln_mix_g[layer:layer + 1], token))
        small_grads[("ln_mix_g", layer)] = dg1
        small_grads[("ln_mix_b", layer)] = db1
        if layer % 2 == 0:
            dcat = mm_rows_dx("mix_out_dx", dz1, weight("mix_w_out", 0))
            big_grads[("mix_w_out", 0)] = mm_rows_dw("mix_out_dw", sv["cat"], dz1)
            token = hook(big_grads[("mix_w_out", 0)])
            du, dpw, dps, dcw, dcb, dcg, dcbt = mixer_bwd(
                "mixer_bwd", dcat, sv["u"], sv["d"], sv["e"], sv["glu"], sv["hh"], sv["rs"],
                pool_w[0], pool_scale, conv_w_full, tied(conv_ln_g, token), conv_ln_b)
            small_grads[("pool_w", 0)] = dpw
            small_grads[("pool_scale", 0)] = dps
            small_grads[("conv_dw_w", 0)] = dcw
            small_grads[("conv_dw_b", 0)] = dcb
            small_grads[("conv_ln_g", 0)] = dcg
            small_grads[("conv_ln_b", 0)] = dcbt
            big_grads[("mix_w_in", 0)] = mm_cols_dw("mix_in_dw", sv["x_in"], du)
            hook(big_grads[("mix_w_in", 0)])
            open_group("layer0_mix", [("mix_w_in", 0), ("mix_w_out", 0)])
            dx_in = mm_cols_dx("mix_in_dx", du, weight("mix_w_in", 0), addend=(ALPHA, dz1))
            token = hook(dx_in)
        else:
            do = mm_rows_dx("attn_out_dx", dz1, weight("attn_w_o", 0), out_dtype=BF16)
            big_grads[("attn_w_o", 0)] = mm_rows_dw("attn_out_dw", sv["att"], dz1)
            dq, dk, dv, ds_sum = attn_bwd("attn_bwd", sv["qkvp"], sv["bias"], do)
            cols, sat = bias_grad_reduce("bias_grad", ds_sum)
            d_rel = jnp.concatenate(
                [jnp.zeros((N_HEADS, 1), F32),
                 jnp.flip(cols[:, 0, Q_TILE + SHEAR_SAT:Q_TILE - 1 + SHEAR_W], axis=1),
                 sat[:, 0, 0:1]], axis=1)
            small_grads[("attn_rel_bias", 0)] = d_rel
            dqkv = jnp.concatenate([dq, dk, dv], axis=1)
            big_grads[("attn_w_qkv", 0)] = mm_cols_dw("attn_qkv_dw", sv["x_in"], dqkv)
            dx_in = mm_cols_dx("attn_qkv_dx", dqkv, weight("attn_w_qkv", 0), addend=(ALPHA, dz1))
        parts = [(1.0, dx_in)]
    grad_x = dx_in

    small = [
        ("pool_w", pool_w, m_pool_w, v_pool_w, None),
        ("pool_scale", pool_scale, m_pool_scale, v_pool_scale, None),
        ("conv_dw_w", conv_dw_w, m_conv_dw_w, v_conv_dw_w, 2),
        ("conv_dw_b", conv_dw_b, m_conv_dw_b, v_conv_dw_b, None),
        ("conv_ln_g", conv_ln_g, m_conv_ln_g, v_conv_ln_g, None),
        ("conv_ln_b", conv_ln_b, m_conv_ln_b, v_conv_ln_b, None),
        ("attn_rel_bias", attn_rel_bias, m_attn_rel_bias, v_attn_rel_bias, None),
        ("ln_mix_g", ln_mix_g, m_ln_mix_g, v_ln_mix_g, None),
        ("ln_mix_b", ln_mix_b, m_ln_mix_b, v_ln_mix_b, None),
        ("ffn_dw_w", ffn_dw_w, m_ffn_dw_w, v_ffn_dw_w, 2),
        ("ffn_dw_b", ffn_dw_b, m_ffn_dw_b, v_ffn_dw_b, None),
        ("ple_b_gate", ple_b_gate, m_ple_b_gate, v_ple_b_gate, None),
        ("ln_ffn_g", ln_ffn_g, m_ln_ffn_g, v_ln_ffn_g, None),
        ("ln_ffn_b", ln_ffn_b, m_ln_ffn_b, v_ln_ffn_b, None),
    ]
    full_grads = []
    for nm, w, _, _, shard_axis in small:
        full = list(w.shape)
        if shard_axis is not None:
            full[shard_axis] *= N_SHARD
        per_layer = [small_grads[(nm, layer)].reshape((1,) + tuple(full[1:])) for layer in range(w.shape[0])]
        full_grads.append(jnp.concatenate(per_layer, axis=0))
    packed = _pack(full_grads + [loss_part])
    dev_arr = (4 * xi + 2 * yi + ci).astype(jnp.int32).reshape(1)
    sg_block = cast_into_gathered("place_small_grads", packed[None], 0, dev_arr, n_blocks=8, dtype=F32)
    sg_send, sg_recv, sg_bufs, sg_token = copies_start("small_grads_start", [sg_block], all_plan, 7)
    token = sg_token if token is None else token + sg_token

    shard_grads = {}
    for red in reducers:
        if red.stage == 4:
            shard_grads.update(red.result)
    big_out = {}

    def update_big(names, tok):
        for nm, w, m, v, _ in big:
            if nm in names:
                gl = [shard_grads[(nm, layer)] for layer in range(w.shape[0])]
                delta, new_m, new_v = adamw(f"adamw_{nm}", w, gl, m, v, token=tok)
                big_out[nm] = (jnp.stack(gl, axis=0), delta, new_m, new_v)

    last_group = ("mix_w_in", "mix_w_out")
    update_big([nm for nm, _, _, _, _ in big if nm not in last_group], token)
    token = hook(big_out["ffn_w_up"][1])

    gathered_sg = copies_wait("small_grads_wait", sg_bufs, sg_send, sg_recv, all_plan, 0, big_out["ffn_w_down"][1])[0]
    total = sum_blocks("sum_small", gathered_sg.reshape(8 * packed.shape[0], LANES), 8)
    unpacked = _unpack(total, [g.shape for g in full_grads] + [loss_part.shape])
    loss = unpacked[-1][0, 0]
    local_grads = []
    for (nm, w, _, _, shard_axis), g in zip(small, unpacked[:-1]):
        if shard_axis is not None:
            width = w.shape[shard_axis]
            g = lax.dynamic_slice_in_dim(g, shard_idx * width, width, axis=shard_axis)
        local_grads.append(g.reshape(w.shape))
    updated = adamw_many("adamw_small", [w for _, w, _, _, _ in small], local_grads,
                         [m for _, _, m, _, _ in small], [v for _, _, _, v, _ in small], token)
    hook(updated[0][0])
    for red in reducers:
        shard_grads.update(red.result)
    update_big(last_group, None)
    small_out = {}
    for (nm, _, _, _, _), g, (d_, m_, v_) in zip(small, local_grads, updated):
        small_out[nm] = (g, d_, m_, v_)

    order = ["mix_w_in", "pool_w", "pool_scale", "conv_dw_w", "conv_dw_b", "conv_ln_g", "conv_ln_b", "mix_w_out",
             "attn_w_qkv", "attn_rel_bias", "attn_w_o", "ln_mix_g", "ln_mix_b", "ffn_w_up", "ffn_dw_w", "ffn_dw_b",
             "ffn_w_down", "ple_w_proj", "ple_w_gate", "ple_b_gate", "ln_ffn_g", "ln_ffn_b"]
    res = {**big_out, **small_out}
    outs = [loss, grad_x[None]]
    for slot in range(4):
        outs += [res[nm][slot] for nm in order]
    return tuple(outs)
```

```python
import math

import jax
import jax.numpy as jnp
from jax import lax
from jax.experimental import pallas as pl
from jax.experimental.pallas import tpu as pltpu

F32 = jnp.float32
BF16 = jnp.bfloat16
MESH = pl.DeviceIdType.MESH

N_LAYERS = 2
ALPHA = (2 * N_LAYERS) ** 0.25
LN_EPS = 1e-5
NEG_INF = -1e30
CHUNK = 64
LEFT_CHUNKS = 8
PAD_ROWS = LEFT_CHUNKS * CHUNK
HEAD_DIM = 64
ATTN_SCALE = HEAD_DIM ** -0.5
N_HEADS = 16
MAX_REL = 256
POOL_WINDOWS = (2, 4, 8, 16)
POOL_GROUP = 128
CONV_K = 31
FFN_K = 3
CONV_HALO = 32
FFN_HALO = 8
FFN_TILE = 256
FFN_CHUNK_ROWS = 256
FFN_CHUNK_LANES = 128
Q_TILE = 256
K_WIN = Q_TILE + PAD_ROWS
LANES = 128
SUBLANES = 8
ATTN_PAIRS = 2
ATTN_PAIRS_FWD = 4
ATTN_CHUNK_ROWS = 32
ATTN_LANES = ATTN_PAIRS * LANES
SHEAR_W = Q_TILE + K_WIN
SHEAR_SAT = SHEAR_W - 2 * MAX_REL
N_SHARD = 4

ADAM_LR = 0.001
ADAM_B1 = 0.9
ADAM_B2 = 0.999
ADAM_EPS = 1e-08
ADAM_WD = 0.01
ADAM_STEP = 10
ADAM_BC1 = 1.0 - ADAM_B1 ** ADAM_STEP
ADAM_BC2 = 1.0 - ADAM_B2 ** ADAM_STEP

DIMS = {
    "nn": (((1,), (0,)), ((), ())),
    "nt": (((1,), (1,)), ((), ())),
    "tn": (((0,), (0,)), ((), ())),
}


def _cp(vmem_mb=48, **kw):
    return pltpu.CompilerParams(vmem_limit_bytes=vmem_mb * 1024 * 1024, **kw)


def _in_hbm(a):
    return pltpu.with_memory_space_constraint(a, pltpu.HBM)


STAGING_LIMIT_BYTES = 1 << 20
SMALL_WEIGHT_BYTES = 1 << 22
SUM_BLOCK_ROWS = 2048
SMALL_BLOCK_BYTES = 1 << 19


def _call(body, **kw):
    call = pl.pallas_call(body, **kw)

    def run(*args):
        pinned = []
        for a in args:
            big = a.size * a.dtype.itemsize >= STAGING_LIMIT_BYTES
            pinned.append(_in_hbm(a) if big and not jnp.issubdtype(a.dtype, jnp.integer) else a)
        return call(*pinned)

    return run


def _dot(a, b, mode):
    return lax.dot_general(a.astype(BF16), b.astype(BF16), DIMS[mode], preferred_element_type=F32)


def _sig(x):
    return 1.0 / (1.0 + jnp.exp(-x))


def _row_tile(s):
    return min(512, s // 4)


def _mm_tile(s):
    return min(1024, s // 4)


def _mm(name, mode, a, b, in_specs, out_shape, out_spec, acc_shape, grid, nk, zero_first=False, vmem_mb=48,
        addend=None):
    out_f32 = out_shape.dtype == F32

    def body(a_ref, b_ref, *rest):
        k = pl.program_id(2)
        if addend is None:
            o_ref, scr = rest[0], rest[1:]
        else:
            add_ref, o_ref, scr = rest[0], rest[1], rest[2:]

        def compute():
            part = _dot(a_ref[...], b_ref[...], mode)
            if nk == 1:
                if addend is not None:
                    part = part + addend[0] * add_ref[...]
                o_ref[...] = part.astype(o_ref.dtype)
                return
            acc = o_ref if out_f32 else scr[0]

            @pl.when(k == 0)
            def _():
                acc[...] = part if addend is None else part + addend[0] * add_ref[...]

            @pl.when(k > 0)
            def _():
                acc[...] += part

            if not out_f32:
                @pl.when(k == nk - 1)
                def _():
                    o_ref[...] = acc[...].astype(o_ref.dtype)

        if zero_first:
            @pl.when(pl.program_id(1) == 0)
            def _():
                o_ref[...] = jnp.zeros(o_ref.shape, o_ref.dtype)

            pl.when(pl.program_id(1) > 0)(compute)
        else:
            compute()

    scratch = [] if (nk == 1 or out_f32) else [pltpu.VMEM(acc_shape, F32)]
    operands = [a, b] if addend is None else [a, b, addend[1]]
    specs = list(in_specs) if addend is None else list(in_specs) + [out_spec]
    return _call(
        body, name=name, grid=grid, in_specs=specs, out_specs=out_spec, out_shape=out_shape,
        scratch_shapes=scratch, compiler_params=_cp(vmem_mb),
    )(*operands)


def _is_small_weight(wc):
    return wc.size * 2 <= SMALL_WEIGHT_BYTES


def _all_shards(w_ref):
    return jnp.concatenate([w_ref[j] for j in range(N_SHARD)], axis=1)


def mm_cols_fwd(name, a, wc, out_dtype, pad_blocks=0, part=(0, 1)):
    s, k = a.shape
    s //= part[1]
    n4 = wc.shape[2]
    tm = _row_tile(s) if pad_blocks else _mm_tile(s)
    nt = s // tm
    first_block = part[0] * nt
    if _is_small_weight(wc) and not pad_blocks:
        def body(a_ref, w_ref, o_ref):
            o_ref[...] = _dot(a_ref[...], _all_shards(w_ref), "nn").astype(o_ref.dtype)

        return _call(
            body, name=name, grid=(nt,),
            in_specs=[pl.BlockSpec((tm, k), lambda i: (first_block + i, 0)), _full(wc.shape)],
            out_specs=pl.BlockSpec((tm, N_SHARD * n4), lambda i: (i, 0)),
            out_shape=jax.ShapeDtypeStruct((s, N_SHARD * n4), out_dtype), compiler_params=_cp(),
        )(a, wc)
    return _mm(
        name, "nn", a, wc,
        [pl.BlockSpec((tm, k), lambda j, i, r: (first_block + jnp.maximum(i - pad_blocks, 0), 0)),
         pl.BlockSpec((None, k, n4), lambda j, i, r: (j, 0, 0))],
        jax.ShapeDtypeStruct((s + pad_blocks * tm, N_SHARD * n4), out_dtype),
        pl.BlockSpec((tm, n4), lambda j, i, r: (i, j)),
        None, (N_SHARD, nt + pad_blocks, 1), 1, zero_first=pad_blocks > 0)


def mm_cols_dx(name, dy, wc, addend=None):
    s = dy.shape[0]
    _, k, n4 = wc.shape
    tm = _mm_tile(s)
    if _is_small_weight(wc):
        def body(dy_ref, w_ref, *rest):
            part = _dot(dy_ref[...], _all_shards(w_ref), "nt")
            rest[-1][...] = part if addend is None else part + addend[0] * rest[0][...]

        out_spec = pl.BlockSpec((tm, k), lambda i: (i, 0))
        extra, extra_specs = ([], []) if addend is None else ([addend[1]], [out_spec])
        return _call(
            body, name=name, grid=(s // tm,),
            in_specs=[pl.BlockSpec((tm, N_SHARD * n4), lambda i: (i, 0)), _full(wc.shape)] + extra_specs,
            out_specs=out_spec, out_shape=jax.ShapeDtypeStruct((s, k), F32), compiler_params=_cp(),
        )(dy, wc, *extra)
    return _mm(
        name, "nt", dy, wc,
        [pl.BlockSpec((tm, n4), lambda g, i, r: (i, r)),
         pl.BlockSpec((None, k, n4), lambda g, i, r: (r, 0, 0))],
        jax.ShapeDtypeStruct((s, k), F32),
        pl.BlockSpec((tm, k), lambda g, i, r: (i, 0)),
        (tm, k), (1, s // tm, N_SHARD), N_SHARD, addend=addend)


def mm_cols_dw(name, a, dy, part=(0, 1)):
    s, k = a.shape
    s //= part[1]
    n4 = dy.shape[1] // N_SHARD
    tm = _mm_tile(s)
    nt = s // tm
    first_block = part[0] * nt
    if k * n4 * N_SHARD * 2 <= SMALL_WEIGHT_BYTES:
        def body(a_ref, dy_ref, o_ref):
            full = _dot(a_ref[...], dy_ref[...], "tn")
            first = pl.program_id(0) == 0
            for j in range(N_SHARD):
                _acc_add(o_ref.at[j], first, full[:, j * n4:(j + 1) * n4])

        return _call(
            body, name=name, grid=(nt,),
            in_specs=[pl.BlockSpec((tm, k), lambda r: (first_block + r, 0)),
                      pl.BlockSpec((tm, N_SHARD * n4), lambda r: (r, 0))],
            out_specs=_full((N_SHARD, k, n4)),
            out_shape=jax.ShapeDtypeStruct((N_SHARD, k, n4), F32), compiler_params=_cp(),
        )(a, dy)
    return _mm(
        name, "tn", a, dy,
        [pl.BlockSpec((tm, k), lambda j, g, r: (first_block + r, 0)),
         pl.BlockSpec((tm, n4), lambda j, g, r: (r, j))],
        jax.ShapeDtypeStruct((N_SHARD, k, n4), F32),
        pl.BlockSpec((None, k, n4), lambda j, g, r: (j, 0, 0)),
        (k, n4), (N_SHARD, 1, nt), nt)


def _k_tile(k):
    return k if k <= 1024 else k // 2


def mm_rows_fwd(name, a, wr, out_dtype=F32):
    s, k = a.shape
    n = wr.shape[1]
    tm = _mm_tile(s)
    tk = _k_tile(k)
    nk = k // tk
    return _mm(
        name, "nn", a, wr,
        [pl.BlockSpec((tm, tk), lambda g, i, r: (i, r)),
         pl.BlockSpec((tk, n), lambda g, i, r: (r, 0))],
        jax.ShapeDtypeStruct((s, n), out_dtype),
        pl.BlockSpec((tm, n), lambda g, i, r: (i, 0)),
        (tm, n), (1, s // tm, nk), nk)


def mm_rows_dx(name, dy, wr, out_dtype=F32):
    s, n = dy.shape
    k = wr.shape[0]
    tm = _mm_tile(s)
    tk = _k_tile(k)
    return _mm(
        name, "nt", dy, wr,
        [pl.BlockSpec((tm, n), lambda j, i, r: (i, 0)),
         pl.BlockSpec((tk, n), lambda j, i, r: (j, 0))],
        jax.ShapeDtypeStruct((s, k), out_dtype),
        pl.BlockSpec((tm, tk), lambda j, i, r: (i, j)),
        None, (k // tk, s // tm, 1), 1)


def mm_rows_dw(name, a, dy):
    s, k = a.shape
    n = dy.shape[1]
    tm = _mm_tile(s)
    tk = _k_tile(k)
    nt = s // tm
    return _mm(
        name, "tn", a, dy,
        [pl.BlockSpec((tm, tk), lambda j, g, r: (r, j)),
         pl.BlockSpec((tm, n), lambda j, g, r: (r, 0))],
        jax.ShapeDtypeStruct((k, n), F32),
        pl.BlockSpec((tk, n), lambda j, g, r: (j, 0)),
        (tk, n), (k // tk, 1, nt), nt)


def _row(tm, c, col=0):
    return pl.BlockSpec((tm, c), lambda i: (i, col))


def _full(shape):
    nd = len(shape)
    return pl.BlockSpec(shape, lambda i: (0,) * nd)


def _prev(tm, h, c, col=0):
    return pl.BlockSpec((h, c), lambda i: (jnp.maximum(i * (tm // h) - 1, 0), col))


def _next(tm, h, c, s, col=0):
    return pl.BlockSpec((h, c), lambda i: (jnp.minimum((i + 1) * (tm // h), s // h - 1), col))


def _acc_add(ref, first, val):
    @pl.when(first)
    def _():
        ref[...] = val

    @pl.when(jnp.logical_not(first))
    def _():
        ref[...] += val


def _colsum(v):
    return jnp.sum(v, axis=0, keepdims=True)


def _ln_stats(z):
    mu = jnp.mean(z, axis=-1, keepdims=True)
    zc = z - mu
    var = jnp.mean(zc * zc, axis=-1, keepdims=True)
    rstd = lax.rsqrt(var + LN_EPS)
    return zc * rstd, rstd


def _ln_bwd(dxhat, xhat, rstd):
    m1 = jnp.mean(dxhat, axis=-1, keepdims=True)
    m2 = jnp.mean(dxhat * xhat, axis=-1, keepdims=True)
    return rstd * (dxhat - m1 - xhat * m2)


def ln_fwd(name, x, f, g, b, ple=None, emit_y=True):
    s, d = x.shape
    tm = _row_tile(s)
    n_in = 2 + (3 if ple is not None else 0)

    def body(*refs):
        x_ref, f_ref = refs[0], refs[1]
        g_ref, b_ref = refs[n_in], refs[n_in + 1]
        xh_ref, rs_ref = refs[-2:]
        z = ALPHA * x_ref[...] + f_ref[...]
        if ple is not None:
            pgl_ref, pp_ref, bg_ref = refs[2:5]
            z = z + _sig(pgl_ref[...] + bg_ref[...]) * pp_ref[...]
        xhat, rstd = _ln_stats(z)
        if emit_y:
            y = xhat * g_ref[...] + b_ref[...]
            refs[n_in + 2][...] = y
            refs[n_in + 3][...] = y.astype(BF16)
        xh_ref[...] = xhat
        rs_ref[...] = jnp.broadcast_to(rstd, rs_ref.shape)

    ins = [x, f]
    specs = [_row(tm, d), _row(tm, d)]
    if ple is not None:
        pgl, pp, bg = ple
        ins += [pgl, pp, bg]
        specs += [_row(tm, d), _row(tm, d), _full((1, d))]
    ins += [g, b]
    specs += [_full((1, d)), _full((1, d))]
    y_shapes = [jax.ShapeDtypeStruct((s, d), F32), jax.ShapeDtypeStruct((s, d), BF16)] if emit_y else []
    outs = _call(
        body, name=name, grid=(s // tm,), in_specs=specs,
        out_specs=[_row(tm, d)] * (len(y_shapes) + 1) + [_row(tm, LANES)],
        out_shape=y_shapes + [jax.ShapeDtypeStruct((s, d), F32), jax.ShapeDtypeStruct((s, LANES), F32)],
        compiler_params=_cp(),
    )(*ins)
    return tuple(outs) if emit_y else (None, None, outs[0], outs[1])


def ln_bwd(name, parts, xhat, rstd, g, ple=None, loss=None):
    s, d = xhat.shape
    tm = _row_tile(s)
    coefs = [c for c, _ in parts]
    n_p = len(parts)
    n_ple = 3 if ple is not None else 0
    n_in = n_p + 3 + n_ple + (2 if loss is not None else 0)

    def body(*refs):
        first = pl.program_id(0) == 0
        xh = refs[n_p][...]
        rs = refs[n_p + 1][:, 0:1]
        g_v = refs[n_p + 2][...]
        outs = refs[n_in:]
        if loss is not None:
            t_ref, b_ref = refs[n_p + 3 + n_ple:n_p + 5 + n_ple]
            err = (xh * g_v + b_ref[...]) - t_ref[...]
            dy = err * (1.0 / d)
            part = 0.5 * jnp.sum(jnp.mean(err * err, axis=-1, keepdims=True), axis=0, keepdims=True)
            _acc_add(outs[-1], first, jnp.broadcast_to(part, outs[-1].shape))
        else:
            dy = coefs[0] * refs[0][...].astype(F32)
            for j in range(1, n_p):
                dy = dy + coefs[j] * refs[j][...].astype(F32)
        dz = _ln_bwd(dy * g_v, xh, rs)
        outs[0][...] = dz
        _acc_add(outs[1], first, _colsum(dy * xh))
        _acc_add(outs[2], first, _colsum(dy))
        if ple is not None:
            pgl_ref, pp_ref, bg_ref = refs[n_p + 3:n_p + 6]
            pg = _sig(pgl_ref[...] + bg_ref[...])
            dpgl = dz * pp_ref[...] * pg * (1.0 - pg)
            outs[3][...] = (dz * pg).astype(BF16)
            outs[4][...] = dpgl.astype(BF16)
            _acc_add(outs[5], first, _colsum(dpgl))

    ins = [p for _, p in parts] + [xhat, rstd, g]
    specs = [_row(tm, d)] * n_p + [_row(tm, d), _row(tm, LANES), _full((1, d))]
    out_specs = [_row(tm, d), _full((1, d)), _full((1, d))]
    out_shape = [jax.ShapeDtypeStruct((s, d), F32), jax.ShapeDtypeStruct((1, d), F32),
                 jax.ShapeDtypeStruct((1, d), F32)]
    if ple is not None:
        pgl, pp, bg = ple
        ins += [pgl, pp, bg]
        specs += [_row(tm, d), _row(tm, d), _full((1, d))]
        out_specs += [_row(tm, d), _row(tm, d), _full((1, d))]
        out_shape += [jax.ShapeDtypeStruct((s, d), BF16), jax.ShapeDtypeStruct((s, d), BF16),
                      jax.ShapeDtypeStruct((1, d), F32)]
    if loss is not None:
        target, b = loss
        ins += [target, b]
        specs += [_row(tm, d), _full((1, d))]
        out_specs += [_full((8, LANES))]
        out_shape += [jax.ShapeDtypeStruct((8, LANES), F32)]
    return _call(
        body, name=name, grid=(s // tm,), in_specs=specs, out_specs=out_specs, out_shape=out_shape,
        compiler_params=_cp(),
    )(*ins)


def _fill_rotations(rot_ref, x, direction):
    n = x.shape[0]
    rot_ref[0] = x
    for b in range(1, SUBLANES):
        if direction < 0:
            rot_ref[b, SUBLANES:n, :] = x[SUBLANES - b:n - b]
        else:
            rot_ref[b, 0:n - SUBLANES, :] = x[b:n - SUBLANES + b]


def _rotated(rot_ref, start, rows, cs, direction=-1):
    b = (-start) % SUBLANES if direction < 0 else start % SUBLANES
    aligned = start + b if direction < 0 else start - b
    return rot_ref[b, pl.ds(aligned, rows), cs]


def _tile_pos(i, tm, rows):
    return (i * tm + lax.broadcasted_iota(jnp.int32, (rows, 1), 0) + 1).astype(F32)


def mixer_fwd(name, u, pool_w, pool_scale, conv_w, conv_b, cn_g, cn_b):
    s = u.shape[0]
    dp = 512
    tm = min(256, s // 4)
    h = CONV_HALO

    def body(a_c, a_p, bv_c, bv_p, bg_c, bg_p, pw_ref, ps_ref, cw_ref, cb_ref, cg_ref, cbt_ref,
             cat_ref, d_ref, e_ref, glu_ref, hh_ref, rs_ref, ext_a, rot_g, conv_out):
        i = pl.program_id(0)
        first = i == 0
        ext_a[0:h, :] = jnp.where(first, 0.0, a_p[...])
        ext_a[h:, :] = a_c[...]
        glu = bv_c[...] * _sig(bg_c[...])
        glu_ref[...] = glu
        _fill_rotations(rot_g, jnp.concatenate([jnp.where(first, 0.0, bv_p[...] * _sig(bg_p[...])), glu], axis=0), -1)
        pos = _tile_pos(i, tm, tm)
        for gi, w in enumerate(POOL_WINDOWS):
            cs = slice(gi * POOL_GROUP, (gi + 1) * POOL_GROUP)
            a_g = ext_a[pl.ds(h, tm), cs]
            acc = a_g
            for sh in range(1, w):
                acc = acc + ext_a[pl.ds(h - sh, tm), cs]
            d_g = acc / jnp.minimum(pos, float(w)) - a_g
            d_ref[:, cs] = d_g.astype(BF16)
            e_g = _dot(d_g, pw_ref[gi], "nn")
            e_ref[:, cs] = e_g
            cat_ref[:, cs] = (e_g * ps_ref[:, cs]).astype(BF16)
        for lg in range(dp // LANES):
            cs = slice(lg * LANES, (lg + 1) * LANES)
            acc = jnp.broadcast_to(cb_ref[:, cs], (tm, LANES))
            for sh in range(CONV_K):
                acc = acc + _rotated(rot_g, h - sh, tm, cs) * cw_ref[pl.ds(CONV_K - 1 - sh, 1), cs]
            conv_out[:, cs] = acc
        hhat, rstd = _ln_stats(conv_out[...])
        hl = hhat * cg_ref[...] + cbt_ref[...]
        cat_ref[:, dp:] = (hl * _sig(hl)).astype(BF16)
        hh_ref[...] = hhat
        rs_ref[...] = jnp.broadcast_to(rstd, rs_ref.shape)

    specs = [_row(tm, dp, 0), _prev(tm, h, dp, 0), _row(tm, dp, 1), _prev(tm, h, dp, 1),
             _row(tm, dp, 2), _prev(tm, h, dp, 2),
             _full((4, POOL_GROUP, POOL_GROUP)), _full((1, dp)), _full((CONV_K, dp)),
             _full((1, dp)), _full((1, dp)), _full((1, dp))]
    out_specs = [_row(tm, 2 * dp), _row(tm, dp), _row(tm, dp), _row(tm, dp), _row(tm, dp), _row(tm, LANES)]
    out_shape = [jax.ShapeDtypeStruct((s, 2 * dp), BF16), jax.ShapeDtypeStruct((s, dp), BF16),
                 jax.ShapeDtypeStruct((s, dp), F32), jax.ShapeDtypeStruct((s, dp), F32),
                 jax.ShapeDtypeStruct((s, dp), F32), jax.ShapeDtypeStruct((s, LANES), F32)]
    return _call(
        body, name=name, grid=(s // tm,), in_specs=specs, out_specs=out_specs, out_shape=out_shape,
        scratch_shapes=[pltpu.VMEM((h + tm, dp), F32), pltpu.VMEM((SUBLANES, h + tm, dp), F32),
                        pltpu.VMEM((tm, dp), F32)],
        compiler_params=_cp(),
    )(u, u, u, u, u, u, pool_w, pool_scale, conv_w, conv_b, cn_g, cn_b)


def mixer_bwd(name, dcat, u, d_sv, e_sv, glu_sv, hh_sv, rs_sv, pool_w, pool_scale, conv_w, cn_g, cn_b):
    s = u.shape[0]
    dp = 512
    tm = min(256, s // 4)
    h = CONV_HALO
    nt = s // tm

    def body(dc_c, dc_n, bv_c, bg_c, d_c, e_c, gl_c, gl_p, hh_c, hh_n, rs_c, rs_n,
             pw_ref, ps_ref, cw_ref, cg_ref, cbt_ref,
             du_ref, dpw_ref, dps_ref, dcw_ref, dcb_ref, dcg_ref, dcbt_ref,
             ext_dh, ext_g, ext_r):
        i = pl.program_id(0)
        first = i == 0
        last = i == nt - 1
        cg = cg_ref[...]

        def conv_grads(dyb, hhat, rstd):
            hl = hhat * cg + cbt_ref[...]
            sg = _sig(hl)
            dhl = dyb * (sg * (1.0 + hl * (1.0 - sg)))
            return _ln_bwd(dhl * cg, hhat, rstd), dhl

        hh_cur = hh_c[...]
        dh_c, dhl_c = conv_grads(dc_c[:, dp:], hh_cur, rs_c[:, 0:1])
        dh_n, _ = conv_grads(dc_n[:, dp:], hh_n[...], rs_n[:, 0:1])
        _fill_rotations(ext_dh, jnp.concatenate([dh_c, jnp.where(last, 0.0, dh_n)], axis=0), 1)
        _fill_rotations(ext_g, jnp.concatenate([jnp.where(first, 0.0, gl_p[...]), gl_c[...]], axis=0), -1)

        @pl.when(first)
        def _():
            dcw_ref[...] = jnp.zeros(dcw_ref.shape, F32)

        for lg in range(dp // LANES):
            cs = slice(lg * LANES, (lg + 1) * LANES)
            dglu = jnp.zeros((tm, LANES), F32)
            for sh in range(CONV_K):
                dglu = dglu + _rotated(ext_dh, sh, tm, cs, 1) * cw_ref[pl.ds(CONV_K - 1 - sh, 1), cs]
            dh_g = ext_dh[0, pl.ds(0, tm), cs]
            for sh in range(CONV_K):
                dcw_ref[pl.ds(CONV_K - 1 - sh, 1), cs] += _colsum(dh_g * _rotated(ext_g, h - sh, tm, cs))
            sgate = _sig(bg_c[:, cs])
            du_ref[:, dp + lg * LANES:dp + (lg + 1) * LANES] = dglu * sgate
            du_ref[:, 2 * dp + lg * LANES:2 * dp + (lg + 1) * LANES] = dglu * bv_c[:, cs] * sgate * (1.0 - sgate)
        _acc_add(dcb_ref, first, _colsum(dh_c))
        _acc_add(dcg_ref, first, _colsum(dhl_c * hh_cur))
        _acc_add(dcbt_ref, first, _colsum(dhl_c))

        pos_c = _tile_pos(i, tm, tm)
        pos_n = _tile_pos(i + 1, tm, h)
        _acc_add(dps_ref, first, _colsum(dc_c[:, :dp] * e_c[...]))
        for gi, w in enumerate(POOL_WINDOWS):
            cs = slice(gi * POOL_GROUP, (gi + 1) * POOL_GROUP)
            pw = pw_ref[gi]
            de_c = dc_c[:, cs] * ps_ref[:, cs]
            de_n = dc_n[:, cs] * ps_ref[:, cs]
            dd_c = _dot(de_c, pw, "nt")
            dd_n = _dot(de_n, pw, "nt")
            ext_r[0:tm, :] = dd_c / jnp.minimum(pos_c, float(w))
            ext_r[tm:, :] = jnp.where(last, 0.0, dd_n / jnp.minimum(pos_n, float(w)))
            acc = -dd_c
            for sh in range(w):
                acc = acc + ext_r[pl.ds(sh, tm), :]
            du_ref[:, cs] = acc
            dpw_g = _dot(d_c[:, cs], de_c, "tn")

            @pl.when(first)
            def _():
                dpw_ref[gi] = dpw_g

            @pl.when(jnp.logical_not(first))
            def _():
                dpw_ref[gi] += dpw_g

    specs = [_row(tm, 2 * dp), _next(tm, h, 2 * dp, s), _row(tm, dp, 1), _row(tm, dp, 2),
             _row(tm, dp), _row(tm, dp), _row(tm, dp), _prev(tm, h, dp),
             _row(tm, dp), _next(tm, h, dp, s), _row(tm, LANES), _next(tm, h, LANES, s),
             _full((4, POOL_GROUP, POOL_GROUP)), _full((1, dp)), _full((CONV_K, dp)),
             _full((1, dp)), _full((1, dp))]
    out_specs = [_row(tm, 3 * dp), _full((4, POOL_GROUP, POOL_GROUP)), _full((1, dp)), _full((CONV_K, dp)),
                 _full((1, dp)), _full((1, dp)), _full((1, dp))]
    out_shape = [jax.ShapeDtypeStruct((s, 3 * dp), F32),
                 jax.ShapeDtypeStruct((4, POOL_GROUP, POOL_GROUP), F32), jax.ShapeDtypeStruct((1, dp), F32),
                 jax.ShapeDtypeStruct((CONV_K, dp), F32), jax.ShapeDtypeStruct((1, dp), F32),
                 jax.ShapeDtypeStruct((1, dp), F32), jax.ShapeDtypeStruct((1, dp), F32)]
    return _call(
        body, name=name, grid=(nt,), in_specs=specs, out_specs=out_specs, out_shape=out_shape,
        scratch_shapes=[pltpu.VMEM((SUBLANES, tm + h, dp), F32), pltpu.VMEM((SUBLANES, h + tm, dp), F32),
                        pltpu.VMEM((tm + h, POOL_GROUP), F32)],
        compiler_params=_cp(),
    )(dcat, dcat, u, u, d_sv, e_sv, glu_sv, glu_sv, hh_sv, hh_sv, rs_sv, rs_sv,
      pool_w, pool_scale, conv_w, cn_g, cn_b)


GELU_C = math.sqrt(2.0 / math.pi)


def _gelu_parts(x):
    x2 = x * x
    t = jnp.tanh(x * (GELU_C + (GELU_C * 0.044715) * x2))
    half_1pt = 0.5 + 0.5 * t
    gelu = x * half_1pt
    dgelu = half_1pt + (0.5 * x) * (1.0 - t * t) * (GELU_C + (3.0 * GELU_C * 0.044715) * x2)
    return gelu, dgelu


def ffn_act_fwd(name, gv, dw_w, dw_b):
    s = gv.shape[0]
    dff = gv.shape[1] // 2
    tm = min(FFN_TILE, s // 4)
    h = FFN_HALO
    rc = FFN_CHUNK_ROWS
    lw = FFN_CHUNK_LANES

    def body(g_c, g_p, v_c, w_ref, b_ref, hid_ref):
        first = pl.program_id(0) == 0

        def chunk(ci, carry):
            r0 = pl.multiple_of(ci * rc, rc)
            above = pl.multiple_of(jnp.maximum(r0 - h, 0), h)
            for lg in range(dff // lw):
                cs = slice(lg * lw, (lg + 1) * lw)
                top = jnp.where(ci == 0, jnp.where(first, 0.0, g_p[:, cs]), g_c[pl.ds(above, h), cs])
                win = jnp.concatenate([top, g_c[pl.ds(r0, rc), cs]], axis=0)
                gc = jnp.broadcast_to(b_ref[:, cs], (rc, lw))
                for sh in range(FFN_K):
                    gc = gc + win[h - sh:h - sh + rc] * w_ref[pl.ds(FFN_K - 1 - sh, 1), cs]
                gelu, _ = _gelu_parts(gc)
                hid_ref[pl.ds(r0, rc), cs] = (gelu * v_c[pl.ds(r0, rc), cs]).astype(BF16)
            return carry

        lax.fori_loop(0, tm // rc, chunk, 0)

    return _call(
        body, name=name, grid=(s // tm,),
        in_specs=[_row(tm, dff, 0), _prev(tm, h, dff, 0), _row(tm, dff, 1), _full((FFN_K, dff)), _full((1, dff))],
        out_specs=_row(tm, dff), out_shape=jax.ShapeDtypeStruct((s, dff), BF16),
        compiler_params=_cp(),
    )(gv, gv, gv, dw_w, dw_b)


def ffn_act_bwd(name, dhid, gv, dw_w, dw_b):
    s = gv.shape[0]
    dff = gv.shape[1] // 2
    tm = min(FFN_TILE, s // 4)
    h = FFN_HALO
    nt = s // tm
    rc = FFN_CHUNK_ROWS
    lw = FFN_CHUNK_LANES
    n_chunks = tm // rc

    def body(dh_c, dh_n, g_p, g_c, g_n, v_c, v_n, w_ref, b_ref, dgv_ref, dw_ref, db_ref):
        i = pl.program_id(0)
        first = i == 0
        last = i == nt - 1

        @pl.when(first)
        def _():
            dw_ref[...] = jnp.zeros(dw_ref.shape, F32)
            db_ref[...] = jnp.zeros(db_ref.shape, F32)

        def chunk(ci, carry):
            r0 = pl.multiple_of(ci * rc, rc)
            above = pl.multiple_of(jnp.maximum(r0 - h, 0), h)
            below = pl.multiple_of(jnp.minimum(r0 + rc, tm - h), h)
            at_end = ci == n_chunks - 1
            for lg in range(dff // lw):
                cs = slice(lg * lw, (lg + 1) * lw)
                top = jnp.where(ci == 0, jnp.where(first, 0.0, g_p[:, cs]), g_c[pl.ds(above, h), cs])
                bot = jnp.where(at_end, g_n[:, cs], g_c[pl.ds(below, h), cs])
                win = jnp.concatenate([top, g_c[pl.ds(r0, rc), cs], bot], axis=0)
                shifted = [win[h - sh:h - sh + rc + h] for sh in range(FFN_K)]
                gc = jnp.broadcast_to(b_ref[:, cs], (rc + h, lw))
                for sh in range(FFN_K):
                    gc = gc + shifted[sh] * w_ref[pl.ds(FFN_K - 1 - sh, 1), cs]
                gelu, dgelu = _gelu_parts(gc)
                dh_mid = dh_c[pl.ds(r0, rc), cs]
                hv_bot = jnp.where(at_end, jnp.where(last, 0.0, dh_n[:, cs] * v_n[:, cs]),
                                   dh_c[pl.ds(below, h), cs] * v_c[pl.ds(below, h), cs])
                dgc = jnp.concatenate([dh_mid * v_c[pl.ds(r0, rc), cs], hv_bot], axis=0) * dgelu
                dgate = jnp.zeros((rc, lw), F32)
                for sh in range(FFN_K):
                    dgate = dgate + dgc[sh:sh + rc] * w_ref[pl.ds(FFN_K - 1 - sh, 1), cs]
                dgv_ref[pl.ds(r0, rc), cs] = dgate.astype(BF16)
                dgv_ref[pl.ds(r0, rc), slice(dff + lg * lw, dff + (lg + 1) * lw)] = (dh_mid * gelu[0:rc]).astype(BF16)
                dgc_mid = dgc[0:rc]
                for sh in range(FFN_K):
                    dw_ref[pl.ds(FFN_K - 1 - sh, 1), cs] += _colsum(dgc_mid * shifted[sh][0:rc])
                db_ref[:, cs] += _colsum(dgc_mid)
            return carry

        lax.fori_loop(0, n_chunks, chunk, 0)

    return _call(
        body, name=name, grid=(nt,),
        in_specs=[_row(tm, dff), _next(tm, h, dff, s),
                  _prev(tm, h, dff, 0), _row(tm, dff, 0), _next(tm, h, dff, s, 0),
                  _row(tm, dff, 1), _next(tm, h, dff, s, 1),
                  _full((FFN_K, dff)), _full((1, dff))],
        out_specs=[_row(tm, 2 * dff), _full((FFN_K, dff)), _full((1, dff))],
        out_shape=[jax.ShapeDtypeStruct((s, 2 * dff), BF16), jax.ShapeDtypeStruct((FFN_K, dff), F32),
                   jax.ShapeDtypeStruct((1, dff), F32)],
        compiler_params=_cp(),
    )(dhid, dhid, gv, gv, gv, gv, gv, dw_w, dw_b)


def _bias_line(rel_bias):
    nh = rel_bias.shape[0]
    line = jnp.concatenate(
        [jnp.zeros((nh, 1), rel_bias.dtype), jnp.broadcast_to(rel_bias[:, 2 * MAX_REL:], (nh, SHEAR_SAT)),
         jnp.flip(rel_bias[:, 1:2 * MAX_REL], axis=1)], axis=1)
    return line[:, None, :]


def bias_tile(name, line):
    nh = line.shape[0]

    def body(l_ref, o_ref):
        x = jnp.broadcast_to(l_ref[...], (Q_TILE, SHEAR_W))
        z = pltpu.roll(x, SHEAR_W - Q_TILE, 1, stride=1, stride_axis=0)
        qc = lax.broadcasted_iota(jnp.int32, (Q_TILE, K_WIN), 0) // CHUNK
        kc = lax.broadcasted_iota(jnp.int32, (Q_TILE, K_WIN), 1) // CHUNK
        o_ref[...] = jnp.where((kc >= qc) & (kc <= qc + LEFT_CHUNKS), z[:, :K_WIN], NEG_INF)

    return _call(
        body, name=name, grid=(nh,), in_specs=[pl.BlockSpec((None, 1, SHEAR_W), lambda hh: (hh, 0, 0))],
        out_specs=pl.BlockSpec((None, Q_TILE, K_WIN), lambda hh: (hh, 0, 0)),
        out_shape=jax.ShapeDtypeStruct((nh, Q_TILE, K_WIN), F32), compiler_params=_cp(),
    )(line)


def _stack_heads(x2, scale=None):
    if scale is not None:
        x2 = x2 * jnp.asarray(scale, x2.dtype)
    lane = lax.broadcasted_iota(jnp.int32, x2.shape, 1)
    zero = jnp.zeros_like(x2)
    return jnp.concatenate([jnp.where(lane < HEAD_DIM, x2, zero), jnp.where(lane < HEAD_DIM, zero, x2)], axis=0)


def _unstack_heads(x_st):
    lane = lax.broadcasted_iota(jnp.int32, (Q_TILE, LANES), 1)
    return jnp.where(lane < HEAD_DIM, x_st[:Q_TILE], x_st[Q_TILE:])


def _attn_probs(q_st, k3, bias_st, t):
    sc = _dot(q_st, k3, "nt") + bias_st
    col = lax.broadcasted_iota(jnp.int32, sc.shape, 1)
    sc = jnp.where(col >= PAD_ROWS - t * Q_TILE, sc, NEG_INF)
    m = jnp.max(sc, axis=-1, keepdims=True)
    p = jnp.exp(sc - m)
    return p * (1.0 / jnp.sum(p, axis=-1, keepdims=True))


def _attn_specs(d_model, pairs):
    nq = PAD_ROWS // Q_TILE
    width = pairs * LANES
    groups = d_model // width
    specs = [pl.BlockSpec((Q_TILE, width), lambda g, t: (t + nq, g))]
    for which in (1, 2):
        for j in range(K_WIN // Q_TILE):
            specs.append(pl.BlockSpec((Q_TILE, width), lambda g, t, j=j, which=which: (t + j, which * groups + g)))
    specs.append(pl.BlockSpec((2 * pairs, Q_TILE, K_WIN), lambda g, t: (g, 0, 0)))
    return specs


def attn_fwd(name, qkvp, bias):
    s = qkvp.shape[0] - PAD_ROWS
    d_model = qkvp.shape[1] // 3
    nw = K_WIN // Q_TILE

    rc = ATTN_CHUNK_ROWS
    per_head = Q_TILE // rc

    def body(q_ref, *refs):
        k_refs, v_refs, b_ref, o_ref, s_scr, p_scr = refs[:nw], refs[nw:2 * nw], *refs[2 * nw:]
        t = pl.program_id(1)
        for j in range(ATTN_PAIRS_FWD):
            ls = slice(j * LANES, (j + 1) * LANES)
            k3 = jnp.concatenate([r[:, ls] for r in k_refs], axis=0)
            s_scr[j] = _dot(_stack_heads(q_ref[:, ls], ATTN_SCALE), k3, "nt")

        def chunk(ci, carry):
            r0 = pl.multiple_of(ci * rc, rc)
            head = ci // per_head
            b0 = pl.multiple_of((ci % per_head) * rc, rc)
            col = lax.broadcasted_iota(jnp.int32, (rc, K_WIN), 1)
            seen = col >= PAD_ROWS - t * Q_TILE
            for j in range(ATTN_PAIRS_FWD):
                sc = s_scr[j, pl.ds(r0, rc), :] + b_ref[2 * j + head, pl.ds(b0, rc), :]
                sc = jnp.where(seen, sc, NEG_INF)
                p = jnp.exp(sc - jnp.max(sc, axis=-1, keepdims=True))
                p_scr[j, pl.ds(r0, rc), :] = (p * (1.0 / jnp.sum(p, axis=-1, keepdims=True))).astype(BF16)
            return carry

        lax.fori_loop(0, 2 * per_head, chunk, 0)
        for j in range(ATTN_PAIRS_FWD):
            ls = slice(j * LANES, (j + 1) * LANES)
            v3 = jnp.concatenate([r[:, ls] for r in v_refs], axis=0)
            o_ref[:, ls] = _unstack_heads(_dot(p_scr[j], v3, "nn")).astype(BF16)

    width = ATTN_PAIRS_FWD * LANES
    return _call(
        body, name=name, grid=(d_model // width, s // Q_TILE),
        in_specs=_attn_specs(d_model, ATTN_PAIRS_FWD), out_specs=pl.BlockSpec((Q_TILE, width), lambda g, t: (t, g)),
        out_shape=jax.ShapeDtypeStruct((s, d_model), BF16),
        scratch_shapes=[pltpu.VMEM((ATTN_PAIRS_FWD, 2 * Q_TILE, K_WIN), F32),
                        pltpu.VMEM((ATTN_PAIRS_FWD, 2 * Q_TILE, K_WIN), BF16)],
        compiler_params=_cp(),
    )(qkvp, *([qkvp] * (2 * nw)), bias)


def attn_bwd(name, qkvp, bias, do):
    s = qkvp.shape[0] - PAD_ROWS
    d_model = qkvp.shape[1] // 3
    nw = K_WIN // Q_TILE
    nt = s // Q_TILE

    def body(q_ref, *refs):
        k_refs, v_refs = refs[:nw], refs[nw:2 * nw]
        b_ref, do_ref, dq_ref, dk_ref, dv_ref, ds_ref, dk_acc, dv_acc = refs[2 * nw:]
        t = pl.program_id(1)
        first = t == 0

        @pl.when(first)
        def _():
            dk_acc[...] = jnp.zeros(dk_acc.shape, F32)
            dv_acc[...] = jnp.zeros(dv_acc.shape, F32)
            ds_ref[...] = jnp.zeros(ds_ref.shape, F32)

        start = pl.multiple_of(t * Q_TILE, Q_TILE)
        for j in range(ATTN_PAIRS):
            ls = slice(j * LANES, (j + 1) * LANES)
            q_st = _stack_heads(q_ref[:, ls], ATTN_SCALE)
            do_st = _stack_heads(do_ref[:, ls])
            k3 = jnp.concatenate([r[:, ls] for r in k_refs], axis=0)
            v3 = jnp.concatenate([r[:, ls] for r in v_refs], axis=0)
            p = _attn_probs(q_st, k3, b_ref[2 * j:2 * j + 2].reshape(2 * Q_TILE, K_WIN), t)
            dp = _dot(do_st, v3, "nt")
            ds = p * (dp - jnp.sum(p * dp, axis=-1, keepdims=True))
            ds_ref[2 * j:2 * j + 2] += ds.reshape(2, Q_TILE, K_WIN)
            dsb = ds.astype(BF16)
            dq_ref[:, ls] = (_unstack_heads(_dot(dsb, k3, "nn")) * ATTN_SCALE).astype(BF16)
            dk_acc[pl.ds(start, K_WIN), ls] += _dot(dsb, q_st, "tn")
            dv_acc[pl.ds(start, K_WIN), ls] += _dot(p, do_st, "tn")

        @pl.when(t == nt - 1)
        def _():
            dk_ref[...] = dk_acc[pl.ds(PAD_ROWS, s), :].astype(BF16)
            dv_ref[...] = dv_acc[pl.ds(PAD_ROWS, s), :].astype(BF16)

    specs = _attn_specs(d_model, ATTN_PAIRS) + [pl.BlockSpec((Q_TILE, ATTN_LANES), lambda g, t: (t, g))]
    col_spec = pl.BlockSpec((s, ATTN_LANES), lambda g, t: (0, g))
    return _call(
        body, name=name, grid=(d_model // ATTN_LANES, nt), in_specs=specs,
        out_specs=[pl.BlockSpec((Q_TILE, ATTN_LANES), lambda g, t: (t, g)), col_spec, col_spec,
                   pl.BlockSpec((2 * ATTN_PAIRS, Q_TILE, K_WIN), lambda g, t: (g, 0, 0))],
        out_shape=[jax.ShapeDtypeStruct((s, d_model), BF16)] * 3
        + [jax.ShapeDtypeStruct((N_HEADS, Q_TILE, K_WIN), F32)],
        scratch_shapes=[pltpu.VMEM((PAD_ROWS + s, ATTN_LANES), F32), pltpu.VMEM((PAD_ROWS + s, ATTN_LANES), F32)],
        compiler_params=_cp(),
    )(qkvp, *([qkvp] * (2 * nw)), bias, do)


def bias_grad_reduce(name, ds_sum):
    nh = ds_sum.shape[0]
    width = SHEAR_W + Q_TILE
    first_k = Q_TILE - 1

    def body(x_ref, col_ref, sat_ref):
        x = x_ref[...]
        hi = x.astype(BF16)
        lo = (x - hi.astype(F32)).astype(BF16)
        r = lax.broadcasted_iota(jnp.int32, (Q_TILE, Q_TILE), 0)
        c = lax.broadcasted_iota(jnp.int32, (Q_TILE, Q_TILE), 1)
        exchange = jnp.where(r + c == Q_TILE - 1, 1.0, 0.0).astype(BF16)
        x_rev = _dot(exchange, hi, "nn") + _dot(exchange, lo, "nn")
        zeros = jnp.zeros((Q_TILE, Q_TILE), F32)
        y = pltpu.roll(jnp.concatenate([zeros, x_rev, zeros], axis=1), 0, 1, stride=1, stride_axis=0)
        cols = _colsum(y)
        col_ref[...] = cols
        k = lax.broadcasted_iota(jnp.int32, cols.shape, 1) - first_k
        tot = jnp.sum(jnp.where((k >= 1) & (k <= SHEAR_SAT), cols, 0.0), axis=-1, keepdims=True)
        sat_ref[...] = jnp.broadcast_to(tot, sat_ref.shape)

    return _call(
        body, name=name, grid=(nh,),
        in_specs=[pl.BlockSpec((None, Q_TILE, K_WIN), lambda hh: (hh, 0, 0))],
        out_specs=[pl.BlockSpec((None, 1, width), lambda hh: (hh, 0, 0)),
                   pl.BlockSpec((None, 1, LANES), lambda hh: (hh, 0, 0))],
        out_shape=[jax.ShapeDtypeStruct((nh, 1, width), F32), jax.ShapeDtypeStruct((nh, 1, LANES), F32)],
        compiler_params=_cp(),
    )(ds_sum)


def _ew_rows(r, most=512, cols=None):
    if cols is not None and r * cols * 4 <= SMALL_BLOCK_BYTES:
        return r
    for cand in range(min(most, r) // 16 * 16, 0, -16):
        if r % cand == 0:
            return cand
    return r


def cast_into_gathered(name, w, layer, s_idx, n_blocks=N_SHARD, dtype=BF16, token=None):
    r, c = w.shape[-2:]
    tr = _ew_rows(r, cols=c)

    def body(s_ref, w_ref, *rest):
        rest[-1][...] = w_ref[...].astype(dtype)

    extra = [] if token is None else [token]
    grid_spec = pltpu.PrefetchScalarGridSpec(
        num_scalar_prefetch=1, grid=(r // tr,),
        in_specs=[pl.BlockSpec((None, tr, c), lambda i, s_ref: (layer, i, 0))] + [ANY_SPEC] * len(extra),
        out_specs=pl.BlockSpec((None, tr, c), lambda i, s_ref: (s_ref[0], i, 0)))
    return _call(
        body, name=name, grid_spec=grid_spec, out_shape=jax.ShapeDtypeStruct((n_blocks, r, c), dtype),
        compiler_params=_cp(),
    )(s_idx, w, *extra)


def adamw(name, w, grads, m, v, token=None):
    nl, r, c = w.shape
    tr = _ew_rows(r, 256, cols=c)

    def body(*refs):
        w_ref, m_ref, v_ref = refs[0], refs[1], refs[2]
        g_refs = refs[3:3 + nl]
        d_ref, nm_ref, nv_ref = refs[-3:]
        layer = pl.program_id(0)
        g = g_refs[0][...]
        for j in range(1, nl):
            g = jnp.where(layer == j, g_refs[j][...], g)
        d_ref[...], nm_ref[...], nv_ref[...] = _adamw_update(w_ref[...], g, m_ref[...], v_ref[...])

    p_spec = pl.BlockSpec((None, tr, c), lambda l, i: (l, i, 0))
    g_spec = pl.BlockSpec((tr, c), lambda l, i: (i, 0))
    extra = [] if token is None else [token]
    extra_specs = [] if token is None else [ANY_SPEC]
    return _call(
        body, name=name, grid=(nl, r // tr), in_specs=[p_spec] * 3 + [g_spec] * nl + extra_specs,
        out_specs=[p_spec] * 3, out_shape=[jax.ShapeDtypeStruct((nl, r, c), F32)] * 3, compiler_params=_cp(),
    )(w, m, v, *grads, *extra)


def _adamw_update(w, g, m, v):
    nm = ADAM_B1 * m + (1.0 - ADAM_B1) * g
    nv = ADAM_B2 * v + (1.0 - ADAM_B2) * (g * g)
    delta = -ADAM_LR * ((nm / ADAM_BC1) / (jnp.sqrt(nv / ADAM_BC2) + ADAM_EPS) + ADAM_WD * w)
    return delta, nm, nv


def adamw_many(name, ws, gs, ms, vs, token):
    n = len(ws)

    def body(*refs):
        ins, outs = refs[:4 * n], refs[4 * n + 1:]
        for i in range(n):
            delta, nm, nv = _adamw_update(ins[i][...], ins[n + i][...], ins[2 * n + i][...], ins[3 * n + i][...])
            outs[3 * i][...] = delta
            outs[3 * i + 1][...] = nm
            outs[3 * i + 2][...] = nv

    vmem = pl.BlockSpec(memory_space=pltpu.VMEM)
    shapes = [jax.ShapeDtypeStruct(w.shape, F32) for w in ws for _ in range(3)]
    outs = _call(
        body, name=name, in_specs=[vmem] * (4 * n) + [ANY_SPEC], out_specs=[vmem] * (3 * n), out_shape=shapes,
        compiler_params=_cp(),
    )(*ws, *gs, *ms, *vs, token)
    return [tuple(outs[3 * i:3 * i + 3]) for i in range(n)]


def sum_blocks(name, gathered, n_blocks):
    r = gathered.shape[0] // n_blocks
    c = gathered.shape[1]
    tr = r if r <= SUM_BLOCK_ROWS else _ew_rows(r)
    nt = r // tr

    def body(*refs):
        acc = refs[0][...]
        for j in range(1, n_blocks):
            acc = acc + refs[j][...]
        refs[-1][...] = acc

    specs = [pl.BlockSpec((tr, c), lambda i, j=j: (j * nt + i, 0)) for j in range(n_blocks)]
    return _call(
        body, name=name, grid=(nt,), in_specs=specs, out_specs=pl.BlockSpec((tr, c), lambda i: (i, 0)),
        out_shape=jax.ShapeDtypeStruct((r, c), F32), compiler_params=_cp(),
    )(*([gathered] * n_blocks))


def _place():
    return lax.axis_index("x"), lax.axis_index("y"), lax.axis_index("c")


def _other_chips(x, y):
    return [(1 - x, y), (x, 1 - y), (1 - x, 1 - y)]


HBM_SPEC = pl.BlockSpec(memory_space=pltpu.HBM)
SEM_SPEC = pl.BlockSpec(memory_space=pltpu.SEMAPHORE)
ANY_SPEC = pl.BlockSpec(memory_space=pl.ANY)
EFFECT = pltpu.SideEffectType.DATAFLOW_SIDE_EFFECTING


def copies_start(name, bufs, plan, n_copies):
    n = len(bufs)

    def body(*refs):
        send, recv = refs[n], refs[n + 1]
        token = refs[2 * n + 2]
        for k, (src, dst, peer, _) in enumerate(plan(refs[:n])):
            pltpu.make_async_remote_copy(
                src_ref=src, dst_ref=dst, send_sem=send.at[k], recv_sem=recv.at[k],
                device_id=peer, device_id_type=MESH).start()
        token[...] = jnp.zeros(token.shape, F32)

    outs = pl.pallas_call(
        body, name=name,
        out_shape=(pltpu.SemaphoreType.DMA((n_copies,)), pltpu.SemaphoreType.DMA((n_copies,)),
                   *[pltpu.HBM(b.shape, b.dtype) for b in bufs], jax.ShapeDtypeStruct((8, LANES), F32)),
        in_specs=[HBM_SPEC] * n,
        out_specs=(SEM_SPEC, SEM_SPEC, *([HBM_SPEC] * n), pl.BlockSpec(memory_space=pltpu.VMEM)),
        input_output_aliases={a: a + 2 for a in range(n)},
        compiler_params=pltpu.CompilerParams(has_side_effects=EFFECT),
    )(*[_in_hbm(b) for b in bufs])
    return outs[0], outs[1], list(outs[2:2 + n]), outs[2 + n]


def copies_wait(name, bufs, send, recv, plan, sem_base, after):
    n = len(bufs)

    def body(*refs):
        send_ref, recv_ref = refs[n], refs[n + 1]
        for k, (src, _, peer, land) in enumerate(plan(refs[:n])):
            cp = pltpu.make_async_remote_copy(
                src_ref=src, dst_ref=land, send_sem=send_ref.at[sem_base + k], recv_sem=recv_ref.at[sem_base + k],
                device_id=peer, device_id_type=MESH)
            cp.wait_send()
            cp.wait_recv()

    outs = pl.pallas_call(
        body, name=name,
        out_shape=tuple(pltpu.HBM(b.shape, b.dtype) for b in bufs),
        in_specs=[HBM_SPEC] * n + [SEM_SPEC, SEM_SPEC, ANY_SPEC], out_specs=tuple([HBM_SPEC] * n),
        input_output_aliases={a: a for a in range(n)},
        compiler_params=pltpu.CompilerParams(has_side_effects=EFFECT),
    )(*bufs, send, recv, after)
    return list(outs)


def gather_plan(refs):
    x, y, c = _place()
    me = 2 * x + y
    return [(buf.at[me], buf.at[me], (cx, cy, c), buf.at[2 * cx + cy])
            for buf in refs for cx, cy in _other_chips(x, y)]


def all_plan(refs):
    x, y, c = _place()
    me = 4 * x + 2 * y + c
    out = []
    for buf in refs:
        for flip in range(1, 8):
            px = 1 - x if flip & 4 else x
            py = 1 - y if flip & 2 else y
            pc = 1 - c if flip & 1 else c
            out.append((buf.at[me], buf.at[me], (px, py, pc), buf.at[4 * px + 2 * py + pc]))
    return out


def swap_plan(refs):
    x, y, c = _place()
    n = len(refs) // 2
    out = []
    for g, land in zip(refs[:n], refs[n:]):
        hr = g.shape[1] // 2
        out.append((g.at[:, pl.ds((1 - c) * hr, hr)], land, (x, y, 1 - c), land))
    return out


def owners_plan(refs):
    x, y, c = _place()
    n = len(refs) // 2
    return [(src.at[2 * cx + cy], land.at[j], (cx, cy, c), land.at[j])
            for src, land in zip(refs[:n], refs[n:]) for j, (cx, cy) in enumerate(_other_chips(x, y))]


def join_plan(refs):
    x, y, c = _place()
    out = []
    for buf in refs:
        hr = buf.shape[0] // 2
        mine = buf.at[pl.ds(c * hr, hr)]
        out.append((mine, mine, (x, y, 1 - c), buf.at[pl.ds((1 - c) * hr, hr)]))
    return out


def add_halves(name, grad, landed, sc_idx):
    _, r, c = grad.shape
    hr = r // 2
    tr = _ew_rows(hr)
    nt = hr // tr

    def body(sc_ref, g_ref, l_ref, own_ref, wire_ref):
        tot = g_ref[...] + l_ref[...]
        wire_ref[...] = tot.astype(BF16)

        @pl.when(pl.program_id(1) == sc_ref[0])
        def _():
            own_ref[...] = tot

    grid_spec = pltpu.PrefetchScalarGridSpec(
        num_scalar_prefetch=1, grid=(nt, N_SHARD),
        in_specs=[pl.BlockSpec((None, tr, c), lambda i, sh, sc_ref: (sh, sc_ref[1] * nt + i, 0)),
                  pl.BlockSpec((None, tr, c), lambda i, sh, sc_ref: (sh, i, 0))],
        out_specs=[pl.BlockSpec((tr, c), lambda i, sh, sc_ref: (i, 0)),
                   pl.BlockSpec((None, tr, c), lambda i, sh, sc_ref: (sh, i, 0))])
    return _call(
        body, name=name, grid_spec=grid_spec,
        out_shape=[jax.ShapeDtypeStruct((hr, c), F32), jax.ShapeDtypeStruct((N_SHARD, hr, c), BF16)],
        compiler_params=_cp(),
    )(sc_idx, grad, landed)


def add_owned(name, own, landed, sc_idx):
    hr, c = own.shape
    tr = _ew_rows(hr)
    nt = hr // tr

    def body(sc_ref, o_ref, l0, l1, l2, out_ref):
        out_ref[...] = ((o_ref[...] + l0[...].astype(F32)) + l1[...].astype(F32)) + l2[...].astype(F32)

    grid_spec = pltpu.PrefetchScalarGridSpec(
        num_scalar_prefetch=1, grid=(nt,),
        in_specs=[pl.BlockSpec((tr, c), lambda i, sc_ref: (i, 0))]
        + [pl.BlockSpec((None, tr, c), lambda i, sc_ref, j=j: (j, i, 0)) for j in range(3)],
        out_specs=pl.BlockSpec((tr, c), lambda i, sc_ref: (sc_ref[1] * nt + i, 0)))
    return _call(
        body, name=name, grid_spec=grid_spec, out_shape=jax.ShapeDtypeStruct((2 * hr, c), F32),
        compiler_params=_cp(),
    )(sc_idx, own, landed, landed, landed)


PACK_QUANTUM = 8 * LANES


def _pack(arrays):
    pieces = []
    for a in arrays:
        flat = a.reshape(-1)
        padded = -(-flat.shape[0] // PACK_QUANTUM) * PACK_QUANTUM
        pieces.append(jnp.pad(flat, (0, padded - flat.shape[0])).reshape(-1, LANES))
    return jnp.concatenate(pieces, axis=0)


def _unpack(packed, shapes):
    out = []
    row = 0
    for shp in shapes:
        size = math.prod(shp)
        rows = -(-size // PACK_QUANTUM) * 8
        out.append(packed[row:row + rows].reshape(-1)[:size].reshape(shp))
        row += rows
    return out


def kernel(x, p, mix_w_in, pool_w, pool_scale, conv_dw_w, conv_dw_b, conv_ln_g, conv_ln_b, mix_w_out, attn_w_qkv, attn_rel_bias, attn_w_o, ln_mix_g, ln_mix_b, ffn_w_up, ffn_dw_w, ffn_dw_b, ffn_w_down, ple_w_proj, ple_w_gate, ple_b_gate, ln_ffn_g, ln_ffn_b, loss_target, m_mix_w_in, m_pool_w, m_pool_scale, m_conv_dw_w, m_conv_dw_b, m_conv_ln_g, m_conv_ln_b, m_mix_w_out, m_attn_w_qkv, m_attn_rel_bias, m_attn_w_o, m_ln_mix_g, m_ln_mix_b, m_ffn_w_up, m_ffn_dw_w, m_ffn_dw_b, m_ffn_w_down, m_ple_w_proj, m_ple_w_gate, m_ple_b_gate, m_ln_ffn_g, m_ln_ffn_b, v_mix_w_in, v_pool_w, v_pool_scale, v_conv_dw_w, v_conv_dw_b, v_conv_ln_g, v_conv_ln_b, v_mix_w_out, v_attn_w_qkv, v_attn_rel_bias, v_attn_w_o, v_ln_mix_g, v_ln_mix_b, v_ffn_w_up, v_ffn_dw_w, v_ffn_dw_b, v_ffn_w_down, v_ple_w_proj, v_ple_w_gate, v_ple_b_gate, v_ln_ffn_g, v_ln_ffn_b):
    xi, yi, ci = _place()
    shard_idx = (2 * xi + yi).astype(jnp.int32)
    s_arr = shard_idx.reshape(1)
    c_arr = ci.astype(jnp.int32).reshape(1)
    sc_arr = jnp.concatenate([s_arr, c_arr])

    x0 = x[0]
    target = loss_target[0]
    p_rows = p.reshape(p.shape[0] * p.shape[2], p.shape[3])
    seq = x0.shape[0]

    big = [
        ("mix_w_in", mix_w_in, m_mix_w_in, v_mix_w_in, True),
        ("mix_w_out", mix_w_out, m_mix_w_out, v_mix_w_out, False),
        ("attn_w_qkv", attn_w_qkv, m_attn_w_qkv, v_attn_w_qkv, True),
        ("attn_w_o", attn_w_o, m_attn_w_o, v_attn_w_o, False),
        ("ffn_w_up", ffn_w_up, m_ffn_w_up, v_ffn_w_up, True),
        ("ffn_w_down", ffn_w_down, m_ffn_w_down, v_ffn_w_down, False),
        ("ple_w_proj", ple_w_proj, m_ple_w_proj, v_ple_w_proj, True),
        ("ple_w_gate", ple_w_gate, m_ple_w_gate, v_ple_w_gate, False),
    ]
    params = {nm: w for nm, w, _, _, _ in big}
    col_sharded = {nm: cs for nm, _, _, _, cs in big}
    keys = [("mix_w_in", 0), ("mix_w_out", 0), ("ffn_w_up", 0), ("ffn_w_down", 0), ("ple_w_gate", 0),
            ("ple_w_proj", 0), ("attn_w_qkv", 0), ("attn_w_o", 0), ("ffn_w_up", 1), ("ffn_w_down", 1),
            ("ple_w_gate", 1), ("ple_w_proj", 1)]
    dw_shapes = [conv_dw_w.shape, ffn_dw_w.shape]
    dw_block = cast_into_gathered("place_dw", _pack([conv_dw_w, ffn_dw_w])[None], 0, s_arr, dtype=F32)
    n_first = 2
    started = {}
    gather_token = None
    for tag, group in (("first", keys[:n_first]), ("rest", keys[n_first:])):
        shards = [cast_into_gathered(f"cast_{nm}_{layer}", params[nm], layer, s_arr, token=gather_token)
                  for nm, layer in group]
        if tag == "first":
            shards.append(dw_block)
        send, recv, bufs, gather_token = copies_start(f"gather_start_{tag}", shards, gather_plan, 3 * len(shards))
        for a, key in enumerate(group):
            started[key] = (send, recv, bufs[a], 3 * a)
        if tag == "first":
            dw_started = (send, recv, bufs[-1], 3 * len(group))
    arrived_w = {}

    def weight(nm, layer, after=None):
        key = (nm, layer)
        if key not in arrived_w:
            send, recv, buf, base = started[key]
            arrived_w[key] = copies_wait(f"gather_wait_{nm}_{layer}", [buf], send, recv, gather_plan, base, after)[0]
        g = arrived_w[key]
        if col_sharded[nm]:
            return g
        return g.reshape(g.shape[0] * g.shape[1], g.shape[2])

    def tie(a, token):
        return a + token[0:1, 0:1].astype(a.dtype)

    class Reducer:
        def __init__(self, tag, group):
            self.tag, self.group, self.stage = tag, group, 0
            self.n = len(group)
            self.result = None

        def advance(self, after):
            tag, n = self.tag, self.n
            if self.stage == 0:
                grads = []
                for key in self.group:
                    g = big_grads[key]
                    grads.append(g if g.ndim == 3 else g.reshape(N_SHARD, g.shape[0] // N_SHARD, g.shape[1]))
                lands = [lax.empty((N_SHARD, g.shape[1] // 2, g.shape[2]), F32) for g in grads]
                self.sems = copies_start(f"swap_start_{tag}", grads + lands, swap_plan, n)
            elif self.stage == 1:
                send, recv, bufs, _ = self.sems
                outs = copies_wait(f"swap_wait_{tag}", bufs, send, recv, swap_plan, 0, after)
                self.own, wire = [], []
                for key, g, ld in zip(self.group, outs[:n], outs[n:]):
                    o, ob = add_halves(f"add_halves_{key[0]}_{key[1]}", g, ld, sc_arr)
                    self.own.append(o)
                    wire.append(ob)
                lands = [lax.empty((3,) + w.shape[1:], BF16) for w in wire]
                self.sems = copies_start(f"owners_start_{tag}", wire + lands, owners_plan, 3 * n)
            elif self.stage == 2:
                send, recv, bufs, _ = self.sems
                outs = copies_wait(f"owners_wait_{tag}", bufs, send, recv, owners_plan, 0, after)
                finals = [add_owned(f"add_owned_{key[0]}_{key[1]}", o, ar, sc_arr)
                          for key, o, ar in zip(self.group, self.own, outs[n:])]
                self.sems = copies_start(f"join_start_{tag}", finals, join_plan, n)
            elif self.stage == 3:
                send, recv, bufs, _ = self.sems
                outs = copies_wait(f"join_wait_{tag}", bufs, send, recv, join_plan, 0, after)
                self.result = dict(zip(self.group, outs))
                self.sems = None
            self.stage += 1
            return None if self.sems is None else self.sems[3]

    dw_cache = []

    def conv_weights(after):
        if not dw_cache:
            send, recv, buf, base = dw_started
            dw_all = copies_wait("gather_wait_dw", [buf], send, recv, gather_plan, base, after)[0]
            dw_parts = [_unpack(dw_all[k], dw_shapes) for k in range(N_SHARD)]
            dw_cache.append(jnp.concatenate([pc[0] for pc in dw_parts], axis=2)[0])
            dw_cache.append(jnp.concatenate([pc[1] for pc in dw_parts], axis=2))
        return dw_cache

    big_grads = {}
    small_grads = {}

    saved = []
    h_in = x0
    h_in_b = x0
    for layer in range(N_LAYERS):
        sv = {"x_in": h_in_b}
        if layer % 2 == 0:
            u = mm_cols_fwd("mix_in", h_in_b, weight("mix_w_in", 0, gather_token), F32)
            conv_w_full, ffn_dw_full = conv_weights(u)
            cat, d_sv, e_sv, glu_sv, hh_sv, rs_sv = mixer_fwd(
                "mixer_fwd", u, pool_w[0], pool_scale, conv_w_full, conv_dw_b, conv_ln_g, conv_ln_b)
            mix = mm_rows_fwd("mix_out", cat, weight("mix_w_out", 0, cat))
            sv.update(u=u, cat=cat, d=d_sv, e=e_sv, glu=glu_sv, hh=hh_sv, rs=rs_sv)
        else:
            qkvp = mm_cols_fwd("attn_qkv", h_in_b, weight("attn_w_qkv", 0, h_in_b), BF16,
                               pad_blocks=PAD_ROWS // _row_tile(seq))
            bias = bias_tile("bias_tile", _bias_line(attn_rel_bias[0]))
            att = attn_fwd("attn_fwd", qkvp, bias)
            mix = mm_rows_fwd("attn_out", att, weight("attn_w_o", 0, att))
            sv.update(qkvp=qkvp, bias=bias, att=att)
        x1, x1_b, xh1, rs1 = ln_fwd(f"ln_mix_{layer}", h_in, mix, ln_mix_g[layer:layer + 1],
                                    ln_mix_b[layer:layer + 1])
        gv = mm_cols_fwd(f"ffn_up_{layer}", x1_b, weight("ffn_w_up", layer, x1_b), F32)
        hid = ffn_act_fwd(f"ffn_act_{layer}", gv, ffn_dw_full[layer], ffn_dw_b[layer:layer + 1])
        ffn = mm_rows_fwd(f"ffn_down_{layer}", hid, weight("ffn_w_down", layer, hid))
        pgl = mm_rows_fwd(f"ple_gate_{layer}", x1_b, weight("ple_w_gate", layer, ffn))
        pp = mm_cols_fwd(f"ple_proj_{layer}", p_rows, weight("ple_w_proj", layer, pgl), F32, part=(layer, N_LAYERS))
        bg = ple_b_gate[layer:layer + 1]
        x2, x2_b, xh2, rs2 = ln_fwd(f"ln_ffn_{layer}", x1, ffn, ln_ffn_g[layer:layer + 1], ln_ffn_b[layer:layer + 1],
                                    ple=(pgl, pp, bg), emit_y=layer < N_LAYERS - 1)
        sv.update(x1=x1_b, xh1=xh1, rs1=rs1, gv=gv, hid=hid, pgl=pgl, pp=pp, xh2=xh2, rs2=rs2)
        saved.append(sv)
        h_in, h_in_b = x2, x2_b

    reducers = []

    def open_group(tag, group):
        reducers.append(Reducer(tag, group))
        return reducers[-1].advance(None)

    def hook(after):
        token = None
        for red in reducers:
            if red.stage < 4:
                tk = red.advance(after)
                if tk is not None:
                    token = tk if token is None else token + tk
        return token

    def tied(a, token):
        return a if token is None else tie(a, token)

    parts = []
    token = None
    for layer in reversed(range(N_LAYERS)):
        sv = saved[layer]
        bg = ple_b_gate[layer:layer + 1]
        if layer == 0:
            token = open_group("layer1", [("attn_w_qkv", 0), ("attn_w_o", 0), ("ffn_w_up", 1), ("ffn_w_down", 1),
                                          ("ple_w_gate", 1), ("ple_w_proj", 1)])
        last = layer == N_LAYERS - 1
        res = ln_bwd(
            f"ln_ffn_bwd_{layer}", parts, sv["xh2"], sv["rs2"], tied(ln_ffn_g[layer:layer + 1], token),
            ple=(sv["pgl"], sv["pp"], bg), loss=(target, ln_ffn_b[layer:layer + 1]) if last else None)
        dz2, dg2, db2, dpp, dpgl, dbg = res[:6]
        if last:
            loss_part = res[6]
        small_grads[("ln_ffn_g", layer)] = dg2
        small_grads[("ln_ffn_b", layer)] = db2
        small_grads[("ple_b_gate", layer)] = dbg
        w_down = weight("ffn_w_down", layer)
        dhid = mm_rows_dx(f"ffn_down_dx_{layer}", dz2, w_down)
        big_grads[("ffn_w_down", layer)] = mm_rows_dw(f"ffn_down_dw_{layer}", sv["hid"], dz2)
        token = hook(big_grads[("ffn_w_down", layer)])
        dgv, ddw, ddb = ffn_act_bwd(f"ffn_act_bwd_{layer}", dhid, sv["gv"], ffn_dw_full[layer],
                                    tied(ffn_dw_b[layer:layer + 1], token))
        small_grads[("ffn_dw_w", layer)] = ddw
        small_grads[("ffn_dw_b", layer)] = ddb
        big_grads[("ffn_w_up", layer)] = mm_cols_dw(f"ffn_up_dw_{layer}", sv["x1"], dgv)
        t_up = mm_cols_dx(f"ffn_up_dx_{layer}", dgv, weight("ffn_w_up", layer))
        token = hook(t_up)
        big_grads[("ple_w_gate", layer)] = mm_rows_dw(f"ple_gate_dw_{layer}", sv["x1"], dpgl)
        t_gate = mm_rows_dx(f"ple_gate_dx_{layer}", dpgl, weight("ple_w_gate", layer))
        big_grads[("ple_w_proj", layer)] = mm_cols_dw(f"ple_proj_dw_{layer}", p_rows, dpp, part=(layer, N_LAYERS))
        token2 = hook(big_grads[("ple_w_proj", layer)])
        if token2 is not None:
            token = token2 if token is None else token + token2
        if layer == 0:
            token3 = open_group("layer0_ffn", [("ffn_w_up", 0), ("ffn_w_down", 0), ("ple_w_gate", 0), ("ple_w_proj", 0)])
            token = token3 if token is None else token + token3
        dz1, dg1, db1 = ln_bwd(
            f"ln_mix_bwd_{layer}", [(ALPHA, dz2), (1.0, t_up), (1.0, t_gate)], sv["xh1"], sv["rs1"],
            tied(ln_mix_g[layer:layer + 1], token))
        small_grads[("ln_mix_g", layer)] = dg1
        small_grads[("ln_mix_b", layer)] = db1
        if layer % 2 == 0:
            dcat = mm_rows_dx("mix_out_dx", dz1, weight("mix_w_out", 0))
            big_grads[("mix_w_out", 0)] = mm_rows_dw("mix_out_dw", sv["cat"], dz1)
            token = hook(big_grads[("mix_w_out", 0)])
            du, dpw, dps, dcw, dcb, dcg, dcbt = mixer_bwd(
                "mixer_bwd", dcat, sv["u"], sv["d"], sv["e"], sv["glu"], sv["hh"], sv["rs"],
                pool_w[0], pool_scale, conv_w_full, tied(conv_ln_g, token), conv_ln_b)
            small_grads[("pool_w", 0)] = dpw
            small_grads[("pool_scale", 0)] = dps
            small_grads[("conv_dw_w", 0)] = dcw
            small_grads[("conv_dw_b", 0)] = dcb
            small_grads[("conv_ln_g", 0)] = dcg
            small_grads[("conv_ln_b", 0)] = dcbt
            big_grads[("mix_w_in", 0)] = mm_cols_dw("mix_in_dw", sv["x_in"], du)
            hook(big_grads[("mix_w_in", 0)])
            open_group("layer0_mix", [("mix_w_in", 0), ("mix_w_out", 0)])
            dx_in = mm_cols_dx("mix_in_dx", du, weight("mix_w_in", 0), addend=(ALPHA, dz1))
            token = hook(dx_in)
        else:
            do = mm_rows_dx("attn_out_dx", dz1, weight("attn_w_o", 0), out_dtype=BF16)
            big_grads[("attn_w_o", 0)] = mm_rows_dw("attn_out_dw", sv["att"], dz1)
            dq, dk, dv, ds_sum = attn_bwd("attn_bwd", sv["qkvp"], sv["bias"], do)
            cols, sat = bias_grad_reduce("bias_grad", ds_sum)
            d_rel = jnp.concatenate(
                [jnp.zeros((N_HEADS, 1), F32),
                 jnp.flip(cols[:, 0, Q_TILE + SHEAR_SAT:Q_TILE - 1 + SHEAR_W], axis=1),
                 sat[:, 0, 0:1]], axis=1)
            small_grads[("attn_rel_bias", 0)] = d_rel
            dqkv = jnp.concatenate([dq, dk, dv], axis=1)
            big_grads[("attn_w_qkv", 0)] = mm_cols_dw("attn_qkv_dw", sv["x_in"], dqkv)
            dx_in = mm_cols_dx("attn_qkv_dx", dqkv, weight("attn_w_qkv", 0), addend=(ALPHA, dz1))
        parts = [(1.0, dx_in)]
    grad_x = dx_in

    small = [
        ("pool_w", pool_w, m_pool_w, v_pool_w, None),
        ("pool_scale", pool_scale, m_pool_scale, v_pool_scale, None),
        ("conv_dw_w", conv_dw_w, m_conv_dw_w, v_conv_dw_w, 2),
        ("conv_dw_b", conv_dw_b, m_conv_dw_b, v_conv_dw_b, None),
        ("conv_ln_g", conv_ln_g, m_conv_ln_g, v_conv_ln_g, None),
        ("conv_ln_b", conv_ln_b, m_conv_ln_b, v_conv_ln_b, None),
        ("attn_rel_bias", attn_rel_bias, m_attn_rel_bias, v_attn_rel_bias, None),
        ("ln_mix_g", ln_mix_g, m_ln_mix_g, v_ln_mix_g, None),
        ("ln_mix_b", ln_mix_b, m_ln_mix_b, v_ln_mix_b, None),
        ("ffn_dw_w", ffn_dw_w, m_ffn_dw_w, v_ffn_dw_w, 2),
        ("ffn_dw_b", ffn_dw_b, m_ffn_dw_b, v_ffn_dw_b, None),
        ("ple_b_gate", ple_b_gate, m_ple_b_gate, v_ple_b_gate, None),
        ("ln_ffn_g", ln_ffn_g, m_ln_ffn_g, v_ln_ffn_g, None),
        ("ln_ffn_b", ln_ffn_b, m_ln_ffn_b, v_ln_ffn_b, None),
    ]
    full_grads = []
    for nm, w, _, _, shard_axis in small:
        full = list(w.shape)
        if shard_axis is not None:
            full[shard_axis] *= N_SHARD
        per_layer = [small_grads[(nm, layer)].reshape((1,) + tuple(full[1:])) for layer in range(w.shape[0])]
        full_grads.append(jnp.concatenate(per_layer, axis=0))
    packed = _pack(full_grads + [loss_part])
    dev_arr = (4 * xi + 2 * yi + ci).astype(jnp.int32).reshape(1)
    sg_block = cast_into_gathered("place_small_grads", packed[None], 0, dev_arr, n_blocks=8, dtype=F32)
    sg_send, sg_recv, sg_bufs, sg_token = copies_start("small_grads_start", [sg_block], all_plan, 7)
    token = sg_token if token is None else token + sg_token

    shard_grads = {}
    for red in reducers:
        if red.stage == 4:
            shard_grads.update(red.result)
    big_out = {}

    def update_big(names, tok):
        for nm, w, m, v, _ in big:
            if nm in names:
                gl = [shard_grads[(nm, layer)] for layer in range(w.shape[0])]
                delta, new_m, new_v = adamw(f"adamw_{nm}", w, gl, m, v, token=tok)
                big_out[nm] = (jnp.stack(gl, axis=0), delta, new_m, new_v)

    last_group = ("mix_w_in", "mix_w_out")
    update_big([nm for nm, _, _, _, _ in big if nm not in last_group], token)
    token = hook(big_out["ffn_w_up"][1])

    gathered_sg = copies_wait("small_grads_wait", sg_bufs, sg_send, sg_recv, all_plan, 0, big_out["ffn_w_down"][1])[0]
    total = sum_blocks("sum_small", gathered_sg.reshape(8 * packed.shape[0], LANES), 8)
    unpacked = _unpack(total, [g.shape for g in full_grads] + [loss_part.shape])
    loss = unpacked[-1][0, 0]
    local_grads = []
    for (nm, w, _, _, shard_axis), g in zip(small, unpacked[:-1]):
        if shard_axis is not None:
            width = w.shape[shard_axis]
            g = lax.dynamic_slice_in_dim(g, shard_idx * width, width, axis=shard_axis)
        local_grads.append(g.reshape(w.shape))
    updated = adamw_many("adamw_small", [w for _, w, _, _, _ in small], local_grads,
                         [m for _, _, m, _, _ in small], [v for _, _, _, v, _ in small], token)
    hook(updated[0][0])
    for red in reducers:
        shard_grads.update(red.result)
    update_big(last_group, None)
    small_out = {}
    for (nm, _, _, _, _), g, (d_, m_, v_) in zip(small, local_grads, updated):
        small_out[nm] = (g, d_, m_, v_)

    order = ["mix_w_in", "pool_w", "pool_scale", "conv_dw_w", "conv_dw_b", "conv_ln_g", "conv_ln_b", "mix_w_out",
             "attn_w_qkv", "attn_rel_bias", "attn_w_o", "ln_mix_g", "ln_mix_b", "ffn_w_up", "ffn_dw_w", "ffn_dw_b",
             "ffn_w_down", "ple_w_proj", "ple_w_gate", "ple_b_gate", "ln_ffn_g", "ln_ffn_b"]
    res = {**big_out, **small_out}
    outs = [loss, grad_x[None]]
    for slot in range(4):
        outs += [res[nm][slot] for nm in order]
    return tuple(outs)
```
